```python
import math
import jax, jax.numpy as jnp
from jax import lax
import numpy as np

D_MODEL = 1024
BATCH = 8
SEQ = 2048
DEPTH = 1

ATTN_HEADS = 8
ATTN_KV_HEADS = 2
HEAD_DIM = 64
ATTN_WIDTH = ATTN_HEADS * HEAD_DIM
KV_WIDTH = ATTN_KV_HEADS * HEAD_DIM
WINDOW = 128
BLOCK = 128
SSM_CH_PER_GROUP = 16
SSM_WIDTH = D_MODEL - ATTN_WIDTH
SSM_GROUPS = SSM_WIDTH // SSM_CH_PER_GROUP
SSM_STATE = 64
DT_MIN = 1e-3
DT_MAX = 1e-1
MIX_WIDTH = ATTN_WIDTH + SSM_WIDTH
IN_WIDTH = ATTN_WIDTH + 2 * KV_WIDTH + SSM_WIDTH
D_FF = 2816
EPS = 1e-6
NEG_INF = -1e30
LAMBDA_RE_MAX = -1e-4

kernel_name = "hymba_swa_s5_macaron_block"


def _alibi_slopes(n_heads):
    return jnp.asarray(2.0 ** (-8.0 * (np.arange(n_heads) + 1) / n_heads), dtype=jnp.float32)


def _rmsnorm(x, g):
    x32 = x.astype(jnp.float32)
    y = x32 * lax.rsqrt(jnp.mean(x32 * x32, axis=-1, keepdims=True) + EPS)
    return (y * g.astype(jnp.float32)).astype(x.dtype)


def _swiglu(x, w_gate, w_up, w_down):
    return (jax.nn.silu(x @ w_gate) * (x @ w_up)) @ w_down


def _window_attention(q, k, v, sinks):
    b, l = q.shape[0], q.shape[1]
    nb = l // BLOCK
    gq = ATTN_HEADS // ATTN_KV_HEADS
    qb = q.reshape(b, nb, BLOCK, ATTN_KV_HEADS, gq, HEAD_DIM)

    def band(t):
        tp = jnp.pad(t, ((0, 0), (BLOCK, BLOCK), (0, 0), (0, 0)))
        tp = tp.reshape(b, nb + 2, BLOCK, ATTN_KV_HEADS, HEAD_DIM)
        return jnp.concatenate([tp[:, :-2], tp[:, 1:-1], tp[:, 2:]], axis=2)

    kw, vw = band(k), band(v)
    scores = jnp.einsum('bnqkgd,bnskd->bnkgqs', qb, kw).astype(jnp.float32) * (HEAD_DIM ** -0.5)
    qi = jnp.arange(BLOCK)[:, None]
    kj = jnp.arange(3 * BLOCK)[None, :]
    rel = kj - BLOCK - qi
    key_pos = jnp.arange(nb)[:, None, None] * BLOCK - BLOCK + kj[None]
    valid = (jnp.abs(rel) <= WINDOW)[None] & (key_pos >= 0) & (key_pos < l)
    slopes = _alibi_slopes(ATTN_HEADS).reshape(ATTN_KV_HEADS, gq)
    alibi = -slopes[:, :, None, None] * jnp.abs(rel).astype(jnp.float32)
    scores = jnp.where(valid[None, :, None, None], scores + alibi, NEG_INF)
    sink = jnp.broadcast_to(sinks.astype(jnp.float32).reshape(1, 1, ATTN_KV_HEADS, gq, 1, 1),
                            scores.shape[:-1] + (1,))
    probs = jax.nn.softmax(jnp.concatenate([scores, sink], axis=-1), axis=-1)[..., :-1]
    out = jnp.einsum('bnkgqs,bnskd->bnqkgd', probs.astype(v.dtype), vw)
    return out.reshape(b, l, ATTN_WIDTH)


def _s5_direction(u, lam_re, lam_im, log_dt, b_re, b_im, c_re, c_im, reverse):
    f32 = jnp.float32
    lr = jnp.minimum(lam_re.astype(f32), LAMBDA_RE_MAX)
    li = lam_im.astype(f32)
    dt = jnp.exp(log_dt.astype(f32))[:, None]
    mag = jnp.exp(lr * dt)
    a_re = mag * jnp.cos(li * dt)
    a_im = mag * jnp.sin(li * dt)
    den = lr * lr + li * li
    coef_re = ((a_re - 1.0) * lr + a_im * li) / den
    coef_im = (a_im * lr - (a_re - 1.0) * li) / den
    br, bi = b_re.astype(f32), b_im.astype(f32)
    bb_re = coef_re[..., None] * br - coef_im[..., None] * bi
    bb_im = coef_re[..., None] * bi + coef_im[..., None] * br
    bu_re = jnp.einsum('blgh,gph->blgp', u, bb_re)
    bu_im = jnp.einsum('blgh,gph->blgp', u, bb_im)
    ar = jnp.broadcast_to(a_re, bu_re.shape)
    ai = jnp.broadcast_to(a_im, bu_re.shape)

    def combine(e1, e2):
        ar1, ai1, xr1, xi1 = e1
        ar2, ai2, xr2, xi2 = e2
        return (ar2 * ar1 - ai2 * ai1,
                ar2 * ai1 + ai2 * ar1,
                ar2 * xr1 - ai2 * xi1 + xr2,
                ar2 * xi1 + ai2 * xr1 + xi2)

    _, _, xr, xi = lax.associative_scan(combine, (ar, ai, bu_re, bu_im), reverse=reverse, axis=1)
    return (jnp.einsum('blgp,ghp->blgh', xr, c_re.astype(f32))
            - jnp.einsum('blgp,ghp->blgh', xi, c_im.astype(f32)))


def _s5_mixer(u, lam_re, lam_im, log_dt, b_re, b_im, c_re, c_im, d_skip, glu_w, glu_b):
    b, l, _ = u.shape
    ug = u.astype(jnp.float32).reshape(b, l, SSM_GROUPS, SSM_CH_PER_GROUP)
    y = d_skip.astype(jnp.float32) * ug
    for direction in range(2):
        y = y + _s5_direction(ug, lam_re[direction], lam_im[direction], log_dt[direction],
                              b_re[direction], b_im[direction], c_re[direction], c_im[direction],
                              reverse=(direction == 1))
    y = jax.nn.gelu(y.reshape(b, l, SSM_WIDTH)).astype(u.dtype)
    return y * jax.nn.sigmoid(y @ glu_w + glu_b)


def _fwd_setup_inputs(seed: int = 0) -> dict:
    key = jax.random.key(seed)
    ks = iter(jax.random.split(key, 40))
    f32 = jnp.float32

    def nrm(shape, scale):
        return jax.random.normal(next(ks), shape, f32) * scale

    def gain(shape):
        return 1.0 + 0.01 * jax.random.normal(next(ks), shape, f32)

    L, G, P, Hc = DEPTH, SSM_GROUPS, SSM_STATE, SSM_CH_PER_GROUP
    lam_im_init = jnp.pi * jnp.arange(P, dtype=f32)
    return {
        "x": jax.random.normal(next(ks), (BATCH, SEQ, D_MODEL), f32),
        "norm_ffn1": gain((L, D_MODEL)),
        "ffn1_w_gate": nrm((L, D_MODEL, D_FF), D_MODEL ** -0.5),
        "ffn1_w_up": nrm((L, D_MODEL, D_FF), D_MODEL ** -0.5),
        "ffn1_w_down": nrm((L, D_FF, D_MODEL), D_FF ** -0.5),
        "norm_mix": gain((L, D_MODEL)),
        "w_in": nrm((L, D_MODEL, IN_WIDTH), D_MODEL ** -0.5),
        "attn_sinks": nrm((L, ATTN_HEADS), 0.5),
        "ssm_lambda_re": -0.5 + nrm((L, 2, G, P), 0.01),
        "ssm_lambda_im": lam_im_init + nrm((L, 2, G, P), 0.01),
        "ssm_log_dt": jax.random.uniform(next(ks), (L, 2, G), f32,
                                         minval=math.log(DT_MIN), maxval=math.log(DT_MAX)),
        "ssm_b_re": nrm((L, 2, G, P, Hc), (2.0 * Hc) ** -0.5),
        "ssm_b_im": nrm((L, 2, G, P, Hc), (2.0 * Hc) ** -0.5),
        "ssm_c_re": nrm((L, 2, G, Hc, P), (2.0 * P) ** -0.5),
        "ssm_c_im": nrm((L, 2, G, Hc, P), (2.0 * P) ** -0.5),
        "ssm_d": 1.0 + nrm((L, G, Hc), 0.1),
        "ssm_glu_w": nrm((L, SSM_WIDTH, SSM_WIDTH), SSM_WIDTH ** -0.5),
        "ssm_glu_b": nrm((L, SSM_WIDTH), 0.01),
        "attn_out_norm": gain((L, ATTN_WIDTH)),
        "ssm_out_norm": gain((L, SSM_WIDTH)),
        "w_out": nrm((L, MIX_WIDTH, D_MODEL), MIX_WIDTH ** -0.5),
        "norm_ffn2": gain((L, D_MODEL)),
        "ffn2_w_gate": nrm((L, D_MODEL, D_FF), D_MODEL ** -0.5),
        "ffn2_w_up": nrm((L, D_MODEL, D_FF), D_MODEL ** -0.5),
        "ffn2_w_down": nrm((L, D_FF, D_MODEL), D_FF ** -0.5),
        "final_norm": gain((D_MODEL,)),
    }


def _fwd_reference(x, norm_ffn1, ffn1_w_gate, ffn1_w_up, ffn1_w_down, norm_mix, w_in, attn_sinks,
              ssm_lambda_re, ssm_lambda_im, ssm_log_dt, ssm_b_re, ssm_b_im, ssm_c_re, ssm_c_im,
              ssm_d, ssm_glu_w, ssm_glu_b, attn_out_norm, ssm_out_norm, w_out,
              norm_ffn2, ffn2_w_gate, ffn2_w_up, ffn2_w_down, final_norm):
    b, l, _ = x.shape
    for layer in range(DEPTH):
        x = x + 0.5 * _swiglu(_rmsnorm(x, norm_ffn1[layer]),
                              ffn1_w_gate[layer], ffn1_w_up[layer], ffn1_w_down[layer])
        h = _rmsnorm(x, norm_mix[layer])
        proj = h @ w_in[layer]
        q, k, v, u = jnp.split(proj, [ATTN_WIDTH, ATTN_WIDTH + KV_WIDTH,
                                      ATTN_WIDTH + 2 * KV_WIDTH], axis=-1)
        attn = _window_attention(q.reshape(b, l, ATTN_HEADS, HEAD_DIM),
                                 k.reshape(b, l, ATTN_KV_HEADS, HEAD_DIM),
                                 v.reshape(b, l, ATTN_KV_HEADS, HEAD_DIM),
                                 attn_sinks[layer])
        ssm = _s5_mixer(u, ssm_lambda_re[layer], ssm_lambda_im[layer], ssm_log_dt[layer],
                        ssm_b_re[layer], ssm_b_im[layer], ssm_c_re[layer], ssm_c_im[layer],
                        ssm_d[layer], ssm_glu_w[layer], ssm_glu_b[layer])
        mixed = jnp.concatenate([_rmsnorm(attn, attn_out_norm[layer]),
                                 _rmsnorm(ssm, ssm_out_norm[layer])], axis=-1)
        x = x + mixed @ w_out[layer]
        x = x + 0.5 * _swiglu(_rmsnorm(x, norm_ffn2[layer]),
                              ffn2_w_gate[layer], ffn2_w_up[layer], ffn2_w_down[layer])
    return _rmsnorm(x, final_norm)


import jax as _jax
import jax.numpy as _jnp

TWIN_FORMAT = 'train_step'
FWD_PARAMS = ['x', 'norm_ffn1', 'ffn1_w_gate', 'ffn1_w_up', 'ffn1_w_down', 'norm_mix', 'w_in', 'attn_sinks', 'ssm_lambda_re', 'ssm_lambda_im', 'ssm_log_dt', 'ssm_b_re', 'ssm_b_im', 'ssm_c_re', 'ssm_c_im', 'ssm_d', 'ssm_glu_w', 'ssm_glu_b', 'attn_out_norm', 'ssm_out_norm', 'w_out', 'norm_ffn2', 'ffn2_w_gate', 'ffn2_w_up', 'ffn2_w_down', 'final_norm']
TWIN_WEIGHTS = ['norm_ffn1', 'ffn1_w_gate', 'ffn1_w_up', 'ffn1_w_down', 'norm_mix', 'w_in', 'attn_sinks', 'ssm_lambda_re', 'ssm_lambda_im', 'ssm_log_dt', 'ssm_b_re', 'ssm_b_im', 'ssm_c_re', 'ssm_c_im', 'ssm_d', 'ssm_glu_w', 'ssm_glu_b', 'attn_out_norm', 'ssm_out_norm', 'w_out', 'norm_ffn2', 'ffn2_w_gate', 'ffn2_w_up', 'ffn2_w_down', 'final_norm']
TWIN_DIFF_INPUT = 'x'
TWIN_INPUTS = ['x', 'norm_ffn1', 'ffn1_w_gate', 'ffn1_w_up', 'ffn1_w_down', 'norm_mix', 'w_in', 'attn_sinks', 'ssm_lambda_re', 'ssm_lambda_im', 'ssm_log_dt', 'ssm_b_re', 'ssm_b_im', 'ssm_c_re', 'ssm_c_im', 'ssm_d', 'ssm_glu_w', 'ssm_glu_b', 'attn_out_norm', 'ssm_out_norm', 'w_out', 'norm_ffn2', 'ffn2_w_gate', 'ffn2_w_up', 'ffn2_w_down', 'final_norm', 'loss_target', 'm_norm_ffn1', 'm_ffn1_w_gate', 'm_ffn1_w_up', 'm_ffn1_w_down', 'm_norm_mix', 'm_w_in', 'm_attn_sinks', 'm_ssm_lambda_re', 'm_ssm_lambda_im', 'm_ssm_log_dt', 'm_ssm_b_re', 'm_ssm_b_im', 'm_ssm_c_re', 'm_ssm_c_im', 'm_ssm_d', 'm_ssm_glu_w', 'm_ssm_glu_b', 'm_attn_out_norm', 'm_ssm_out_norm', 'm_w_out', 'm_norm_ffn2', 'm_ffn2_w_gate', 'm_ffn2_w_up', 'm_ffn2_w_down', 'm_final_norm', 'v_norm_ffn1', 'v_ffn1_w_gate', 'v_ffn1_w_up', 'v_ffn1_w_down', 'v_norm_mix', 'v_w_in', 'v_attn_sinks', 'v_ssm_lambda_re', 'v_ssm_lambda_im', 'v_ssm_log_dt', 'v_ssm_b_re', 'v_ssm_b_im', 'v_ssm_c_re', 'v_ssm_c_im', 'v_ssm_d', 'v_ssm_glu_w', 'v_ssm_glu_b', 'v_attn_out_norm', 'v_ssm_out_norm', 'v_w_out', 'v_norm_ffn2', 'v_ffn2_w_gate', 'v_ffn2_w_up', 'v_ffn2_w_down', 'v_final_norm']
TWIN_OUTPUTS = ['loss', 'grad_x', 'grad_norm_ffn1', 'grad_ffn1_w_gate', 'grad_ffn1_w_up', 'grad_ffn1_w_down', 'grad_norm_mix', 'grad_w_in', 'grad_attn_sinks', 'grad_ssm_lambda_re', 'grad_ssm_lambda_im', 'grad_ssm_log_dt', 'grad_ssm_b_re', 'grad_ssm_b_im', 'grad_ssm_c_re', 'grad_ssm_c_im', 'grad_ssm_d', 'grad_ssm_glu_w', 'grad_ssm_glu_b', 'grad_attn_out_norm', 'grad_ssm_out_norm', 'grad_w_out', 'grad_norm_ffn2', 'grad_ffn2_w_gate', 'grad_ffn2_w_up', 'grad_ffn2_w_down', 'grad_final_norm', 'delta_norm_ffn1', 'delta_ffn1_w_gate', 'delta_ffn1_w_up', 'delta_ffn1_w_down', 'delta_norm_mix', 'delta_w_in', 'delta_attn_sinks', 'delta_ssm_lambda_re', 'delta_ssm_lambda_im', 'delta_ssm_log_dt', 'delta_ssm_b_re', 'delta_ssm_b_im', 'delta_ssm_c_re', 'delta_ssm_c_im', 'delta_ssm_d', 'delta_ssm_glu_w', 'delta_ssm_glu_b', 'delta_attn_out_norm', 'delta_ssm_out_norm', 'delta_w_out', 'delta_norm_ffn2', 'delta_ffn2_w_gate', 'delta_ffn2_w_up', 'delta_ffn2_w_down', 'delta_final_norm', 'new_m_norm_ffn1', 'new_m_ffn1_w_gate', 'new_m_ffn1_w_up', 'new_m_ffn1_w_down', 'new_m_norm_mix', 'new_m_w_in', 'new_m_attn_sinks', 'new_m_ssm_lambda_re', 'new_m_ssm_lambda_im', 'new_m_ssm_log_dt', 'new_m_ssm_b_re', 'new_m_ssm_b_im', 'new_m_ssm_c_re', 'new_m_ssm_c_im', 'new_m_ssm_d', 'new_m_ssm_glu_w', 'new_m_ssm_glu_b', 'new_m_attn_out_norm', 'new_m_ssm_out_norm', 'new_m_w_out', 'new_m_norm_ffn2', 'new_m_ffn2_w_gate', 'new_m_ffn2_w_up', 'new_m_ffn2_w_down', 'new_m_final_norm', 'new_v_norm_ffn1', 'new_v_ffn1_w_gate', 'new_v_ffn1_w_up', 'new_v_ffn1_w_down', 'new_v_norm_mix', 'new_v_w_in', 'new_v_attn_sinks', 'new_v_ssm_lambda_re', 'new_v_ssm_lambda_im', 'new_v_ssm_log_dt', 'new_v_ssm_b_re', 'new_v_ssm_b_im', 'new_v_ssm_c_re', 'new_v_ssm_c_im', 'new_v_ssm_d', 'new_v_ssm_glu_w', 'new_v_ssm_glu_b', 'new_v_attn_out_norm', 'new_v_ssm_out_norm', 'new_v_w_out', 'new_v_norm_ffn2', 'new_v_ffn2_w_gate', 'new_v_ffn2_w_up', 'new_v_ffn2_w_down', 'new_v_final_norm']
TWIN_LEAF_KINDS = {'loss': 'loss', 'grad_x': 'grad_x', 'grad_norm_ffn1': 'grad_w', 'grad_ffn1_w_gate': 'grad_w', 'grad_ffn1_w_up': 'grad_w', 'grad_ffn1_w_down': 'grad_w', 'grad_norm_mix': 'grad_w', 'grad_w_in': 'grad_w', 'grad_attn_sinks': 'grad_w', 'grad_ssm_lambda_re': 'grad_w', 'grad_ssm_lambda_im': 'grad_w', 'grad_ssm_log_dt': 'grad_w', 'grad_ssm_b_re': 'grad_w', 'grad_ssm_b_im': 'grad_w', 'grad_ssm_c_re': 'grad_w', 'grad_ssm_c_im': 'grad_w', 'grad_ssm_d': 'grad_w', 'grad_ssm_glu_w': 'grad_w', 'grad_ssm_glu_b': 'grad_w', 'grad_attn_out_norm': 'grad_w', 'grad_ssm_out_norm': 'grad_w', 'grad_w_out': 'grad_w', 'grad_norm_ffn2': 'grad_w', 'grad_ffn2_w_gate': 'grad_w', 'grad_ffn2_w_up': 'grad_w', 'grad_ffn2_w_down': 'grad_w', 'grad_final_norm': 'grad_w', 'delta_norm_ffn1': 'delta_w', 'delta_ffn1_w_gate': 'delta_w', 'delta_ffn1_w_up': 'delta_w', 'delta_ffn1_w_down': 'delta_w', 'delta_norm_mix': 'delta_w', 'delta_w_in': 'delta_w', 'delta_attn_sinks': 'delta_w', 'delta_ssm_lambda_re': 'delta_w', 'delta_ssm_lambda_im': 'delta_w', 'delta_ssm_log_dt': 'delta_w', 'delta_ssm_b_re': 'delta_w', 'delta_ssm_b_im': 'delta_w', 'delta_ssm_c_re': 'delta_w', 'delta_ssm_c_im': 'delta_w', 'delta_ssm_d': 'delta_w', 'delta_ssm_glu_w': 'delta_w', 'delta_ssm_glu_b': 'delta_w', 'delta_attn_out_norm': 'delta_w', 'delta_ssm_out_norm': 'delta_w', 'delta_w_out': 'delta_w', 'delta_norm_ffn2': 'delta_w', 'delta_ffn2_w_gate': 'delta_w', 'delta_ffn2_w_up': 'delta_w', 'delta_ffn2_w_down': 'delta_w', 'delta_final_norm': 'delta_w', 'new_m_norm_ffn1': 'new_m', 'new_m_ffn1_w_gate': 'new_m', 'new_m_ffn1_w_up': 'new_m', 'new_m_ffn1_w_down': 'new_m', 'new_m_norm_mix': 'new_m', 'new_m_w_in': 'new_m', 'new_m_attn_sinks': 'new_m', 'new_m_ssm_lambda_re': 'new_m', 'new_m_ssm_lambda_im': 'new_m', 'new_m_ssm_log_dt': 'new_m', 'new_m_ssm_b_re': 'new_m', 'new_m_ssm_b_im': 'new_m', 'new_m_ssm_c_re': 'new_m', 'new_m_ssm_c_im': 'new_m', 'new_m_ssm_d': 'new_m', 'new_m_ssm_glu_w': 'new_m', 'new_m_ssm_glu_b': 'new_m', 'new_m_attn_out_norm': 'new_m', 'new_m_ssm_out_norm': 'new_m', 'new_m_w_out': 'new_m', 'new_m_norm_ffn2': 'new_m', 'new_m_ffn2_w_gate': 'new_m', 'new_m_ffn2_w_up': 'new_m', 'new_m_ffn2_w_down': 'new_m', 'new_m_final_norm': 'new_m', 'new_v_norm_ffn1': 'new_v', 'new_v_ffn1_w_gate': 'new_v', 'new_v_ffn1_w_up': 'new_v', 'new_v_ffn1_w_down': 'new_v', 'new_v_norm_mix': 'new_v', 'new_v_w_in': 'new_v', 'new_v_attn_sinks': 'new_v', 'new_v_ssm_lambda_re': 'new_v', 'new_v_ssm_lambda_im': 'new_v', 'new_v_ssm_log_dt': 'new_v', 'new_v_ssm_b_re': 'new_v', 'new_v_ssm_b_im': 'new_v', 'new_v_ssm_c_re': 'new_v', 'new_v_ssm_c_im': 'new_v', 'new_v_ssm_d': 'new_v', 'new_v_ssm_glu_w': 'new_v', 'new_v_ssm_glu_b': 'new_v', 'new_v_attn_out_norm': 'new_v', 'new_v_ssm_out_norm': 'new_v', 'new_v_w_out': 'new_v', 'new_v_norm_ffn2': 'new_v', 'new_v_ffn2_w_gate': 'new_v', 'new_v_ffn2_w_up': 'new_v', 'new_v_ffn2_w_down': 'new_v', 'new_v_final_norm': 'new_v'}


def _forward(args):
    return _fwd_reference(*[args[k] for k in FWD_PARAMS])


def _output_shape():
    out = _jax.eval_shape(lambda: _forward(_fwd_setup_inputs(0)))
    return out.shape, out.dtype

N_MICROBATCH = 1
ADAM_LR = 0.001
ADAM_B1 = 0.9
ADAM_B2 = 0.999
ADAM_EPS = 1e-08
ADAM_WD = 0.01
ADAM_STEP = 10
PER_EXAMPLE_BATCH_AXIS = {'x': 0, 'loss_target': 0}
SHARED_INPUTS = []
_WEIGHT_DTYPES = {'norm_ffn1': _jnp.float32, 'ffn1_w_gate': _jnp.float32, 'ffn1_w_up': _jnp.float32, 'ffn1_w_down': _jnp.float32, 'norm_mix': _jnp.float32, 'w_in': _jnp.float32, 'attn_sinks': _jnp.float32, 'ssm_lambda_re': _jnp.float32, 'ssm_lambda_im': _jnp.float32, 'ssm_log_dt': _jnp.float32, 'ssm_b_re': _jnp.float32, 'ssm_b_im': _jnp.float32, 'ssm_c_re': _jnp.float32, 'ssm_c_im': _jnp.float32, 'ssm_d': _jnp.float32, 'ssm_glu_w': _jnp.float32, 'ssm_glu_b': _jnp.float32, 'attn_out_norm': _jnp.float32, 'ssm_out_norm': _jnp.float32, 'w_out': _jnp.float32, 'norm_ffn2': _jnp.float32, 'ffn2_w_gate': _jnp.float32, 'ffn2_w_up': _jnp.float32, 'ffn2_w_down': _jnp.float32, 'final_norm': _jnp.float32}
MOMENT_SCALE = {'norm_ffn1': 6.740086e-02, 'ffn1_w_gate': 2.684155e-02, 'ffn1_w_up': 2.596321e-02, 'ffn1_w_down': 4.312645e-02, 'norm_mix': 1.225462e-01, 'w_in': 1.071596e-01, 'attn_sinks': 5.831571e-02, 'ssm_lambda_re': 4.728951e-03, 'ssm_lambda_im': 4.642548e-03, 'ssm_log_dt': 3.031115e+00, 'ssm_b_re': 3.202385e-03, 'ssm_b_im': 3.188936e-03, 'ssm_c_re': 6.557480e-03, 'ssm_c_im': 6.480027e-03, 'ssm_d': 1.070505e-01, 'ssm_glu_w': 2.543870e-02, 'ssm_glu_b': 3.839865e-02, 'attn_out_norm': 9.439466e-02, 'ssm_out_norm': 8.866319e-02, 'w_out': 8.881190e-02, 'norm_ffn2': 3.695547e-02, 'ffn2_w_gate': 1.579602e-02, 'ffn2_w_up': 1.533072e-02, 'ffn2_w_down': 2.551041e-02, 'final_norm': 1.602719e+01}


def _to_microbatches(a, axis):
    t = _jnp.moveaxis(a, axis, 0)
    t = t.reshape((N_MICROBATCH, t.shape[0] // N_MICROBATCH) + t.shape[1:])
    return _jnp.moveaxis(t, 1, axis + 1)


def setup_inputs(seed: int = 0) -> dict:
    inp = _fwd_setup_inputs(seed)
    key = _jax.random.fold_in(_jax.random.key(seed), 7919)
    shape, _ = _output_shape()
    out = dict(inp)
    out["loss_target"] = _jax.random.normal(_jax.random.fold_in(key, 0), shape, _jnp.float32)
    for i, name in enumerate(TWIN_WEIGHTS):
        w = inp[name].astype(_jnp.float32)
        if MOMENT_SCALE is None:
            s = _jnp.sqrt(_jnp.mean(_jnp.square(w)) + 1e-30)
        else:
            s = MOMENT_SCALE[name]
        km, kv = _jax.random.split(_jax.random.fold_in(key, i + 1))
        out[name] = w
        out["m_" + name] = s * _jax.random.normal(km, w.shape, _jnp.float32)
        out["v_" + name] = (s * s) * _jax.random.uniform(kv, w.shape, _jnp.float32, 0.5, 1.5)
    if N_MICROBATCH > 1:
        for name, axis in PER_EXAMPLE_BATCH_AXIS.items():
            out[name] = _to_microbatches(out[name], axis)
    return {'x': out['x'], 'norm_ffn1': out['norm_ffn1'], 'ffn1_w_gate': out['ffn1_w_gate'], 'ffn1_w_up': out['ffn1_w_up'], 'ffn1_w_down': out['ffn1_w_down'], 'norm_mix': out['norm_mix'], 'w_in': out['w_in'], 'attn_sinks': out['attn_sinks'], 'ssm_lambda_re': out['ssm_lambda_re'], 'ssm_lambda_im': out['ssm_lambda_im'], 'ssm_log_dt': out['ssm_log_dt'], 'ssm_b_re': out['ssm_b_re'], 'ssm_b_im': out['ssm_b_im'], 'ssm_c_re': out['ssm_c_re'], 'ssm_c_im': out['ssm_c_im'], 'ssm_d': out['ssm_d'], 'ssm_glu_w': out['ssm_glu_w'], 'ssm_glu_b': out['ssm_glu_b'], 'attn_out_norm': out['attn_out_norm'], 'ssm_out_norm': out['ssm_out_norm'], 'w_out': out['w_out'], 'norm_ffn2': out['norm_ffn2'], 'ffn2_w_gate': out['ffn2_w_gate'], 'ffn2_w_up': out['ffn2_w_up'], 'ffn2_w_down': out['ffn2_w_down'], 'final_norm': out['final_norm'], 'loss_target': out['loss_target'], 'm_norm_ffn1': out['m_norm_ffn1'], 'm_ffn1_w_gate': out['m_ffn1_w_gate'], 'm_ffn1_w_up': out['m_ffn1_w_up'], 'm_ffn1_w_down': out['m_ffn1_w_down'], 'm_norm_mix': out['m_norm_mix'], 'm_w_in': out['m_w_in'], 'm_attn_sinks': out['m_attn_sinks'], 'm_ssm_lambda_re': out['m_ssm_lambda_re'], 'm_ssm_lambda_im': out['m_ssm_lambda_im'], 'm_ssm_log_dt': out['m_ssm_log_dt'], 'm_ssm_b_re': out['m_ssm_b_re'], 'm_ssm_b_im': out['m_ssm_b_im'], 'm_ssm_c_re': out['m_ssm_c_re'], 'm_ssm_c_im': out['m_ssm_c_im'], 'm_ssm_d': out['m_ssm_d'], 'm_ssm_glu_w': out['m_ssm_glu_w'], 'm_ssm_glu_b': out['m_ssm_glu_b'], 'm_attn_out_norm': out['m_attn_out_norm'], 'm_ssm_out_norm': out['m_ssm_out_norm'], 'm_w_out': out['m_w_out'], 'm_norm_ffn2': out['m_norm_ffn2'], 'm_ffn2_w_gate': out['m_ffn2_w_gate'], 'm_ffn2_w_up': out['m_ffn2_w_up'], 'm_ffn2_w_down': out['m_ffn2_w_down'], 'm_final_norm': out['m_final_norm'], 'v_norm_ffn1': out['v_norm_ffn1'], 'v_ffn1_w_gate': out['v_ffn1_w_gate'], 'v_ffn1_w_up': out['v_ffn1_w_up'], 'v_ffn1_w_down': out['v_ffn1_w_down'], 'v_norm_mix': out['v_norm_mix'], 'v_w_in': out['v_w_in'], 'v_attn_sinks': out['v_attn_sinks'], 'v_ssm_lambda_re': out['v_ssm_lambda_re'], 'v_ssm_lambda_im': out['v_ssm_lambda_im'], 'v_ssm_log_dt': out['v_ssm_log_dt'], 'v_ssm_b_re': out['v_ssm_b_re'], 'v_ssm_b_im': out['v_ssm_b_im'], 'v_ssm_c_re': out['v_ssm_c_re'], 'v_ssm_c_im': out['v_ssm_c_im'], 'v_ssm_d': out['v_ssm_d'], 'v_ssm_glu_w': out['v_ssm_glu_w'], 'v_ssm_glu_b': out['v_ssm_glu_b'], 'v_attn_out_norm': out['v_attn_out_norm'], 'v_ssm_out_norm': out['v_ssm_out_norm'], 'v_w_out': out['v_w_out'], 'v_norm_ffn2': out['v_norm_ffn2'], 'v_ffn2_w_gate': out['v_ffn2_w_gate'], 'v_ffn2_w_up': out['v_ffn2_w_up'], 'v_ffn2_w_down': out['v_ffn2_w_down'], 'v_final_norm': out['v_final_norm']}


def _loss(weights, diff, rest, loss_target):
    with _jax.named_scope("forward"):
        args = {**rest, TWIN_DIFF_INPUT: diff, **{k: w.astype(_WEIGHT_DTYPES[k]) for k, w in weights.items()}}
        y = _forward(args)
    with _jax.named_scope("loss_head"):
        err = _jnp.square(y.astype(_jnp.float32) - loss_target)
        return 0.5 * _jnp.sum(_jnp.mean(err, axis=-1)) if err.ndim else 0.5 * err


def _adamw(w, g, m, v):
    m = ADAM_B1 * m + (1.0 - ADAM_B1) * g
    v = ADAM_B2 * v + (1.0 - ADAM_B2) * _jnp.square(g)
    m_hat = m / (1.0 - ADAM_B1 ** ADAM_STEP)
    v_hat = v / (1.0 - ADAM_B2 ** ADAM_STEP)
    delta = -ADAM_LR * (m_hat / (_jnp.sqrt(v_hat) + ADAM_EPS) + ADAM_WD * w)
    return delta, m, v


def reference(x, norm_ffn1, ffn1_w_gate, ffn1_w_up, ffn1_w_down, norm_mix, w_in, attn_sinks, ssm_lambda_re, ssm_lambda_im, ssm_log_dt, ssm_b_re, ssm_b_im, ssm_c_re, ssm_c_im, ssm_d, ssm_glu_w, ssm_glu_b, attn_out_norm, ssm_out_norm, w_out, norm_ffn2, ffn2_w_gate, ffn2_w_up, ffn2_w_down, final_norm, loss_target, m_norm_ffn1, m_ffn1_w_gate, m_ffn1_w_up, m_ffn1_w_down, m_norm_mix, m_w_in, m_attn_sinks, m_ssm_lambda_re, m_ssm_lambda_im, m_ssm_log_dt, m_ssm_b_re, m_ssm_b_im, m_ssm_c_re, m_ssm_c_im, m_ssm_d, m_ssm_glu_w, m_ssm_glu_b, m_attn_out_norm, m_ssm_out_norm, m_w_out, m_norm_ffn2, m_ffn2_w_gate, m_ffn2_w_up, m_ffn2_w_down, m_final_norm, v_norm_ffn1, v_ffn1_w_gate, v_ffn1_w_up, v_ffn1_w_down, v_norm_mix, v_w_in, v_attn_sinks, v_ssm_lambda_re, v_ssm_lambda_im, v_ssm_log_dt, v_ssm_b_re, v_ssm_b_im, v_ssm_c_re, v_ssm_c_im, v_ssm_d, v_ssm_glu_w, v_ssm_glu_b, v_attn_out_norm, v_ssm_out_norm, v_w_out, v_norm_ffn2, v_ffn2_w_gate, v_ffn2_w_up, v_ffn2_w_down, v_final_norm):
    given = dict(x=x, norm_ffn1=norm_ffn1, ffn1_w_gate=ffn1_w_gate, ffn1_w_up=ffn1_w_up, ffn1_w_down=ffn1_w_down, norm_mix=norm_mix, w_in=w_in, attn_sinks=attn_sinks, ssm_lambda_re=ssm_lambda_re, ssm_lambda_im=ssm_lambda_im, ssm_log_dt=ssm_log_dt, ssm_b_re=ssm_b_re, ssm_b_im=ssm_b_im, ssm_c_re=ssm_c_re, ssm_c_im=ssm_c_im, ssm_d=ssm_d, ssm_glu_w=ssm_glu_w, ssm_glu_b=ssm_glu_b, attn_out_norm=attn_out_norm, ssm_out_norm=ssm_out_norm, w_out=w_out, norm_ffn2=norm_ffn2, ffn2_w_gate=ffn2_w_gate, ffn2_w_up=ffn2_w_up, ffn2_w_down=ffn2_w_down, final_norm=final_norm, loss_target=loss_target, m_norm_ffn1=m_norm_ffn1, m_ffn1_w_gate=m_ffn1_w_gate, m_ffn1_w_up=m_ffn1_w_up, m_ffn1_w_down=m_ffn1_w_down, m_norm_mix=m_norm_mix, m_w_in=m_w_in, m_attn_sinks=m_attn_sinks, m_ssm_lambda_re=m_ssm_lambda_re, m_ssm_lambda_im=m_ssm_lambda_im, m_ssm_log_dt=m_ssm_log_dt, m_ssm_b_re=m_ssm_b_re, m_ssm_b_im=m_ssm_b_im, m_ssm_c_re=m_ssm_c_re, m_ssm_c_im=m_ssm_c_im, m_ssm_d=m_ssm_d, m_ssm_glu_w=m_ssm_glu_w, m_ssm_glu_b=m_ssm_glu_b, m_attn_out_norm=m_attn_out_norm, m_ssm_out_norm=m_ssm_out_norm, m_w_out=m_w_out, m_norm_ffn2=m_norm_ffn2, m_ffn2_w_gate=m_ffn2_w_gate, m_ffn2_w_up=m_ffn2_w_up, m_ffn2_w_down=m_ffn2_w_down, m_final_norm=m_final_norm, v_norm_ffn1=v_norm_ffn1, v_ffn1_w_gate=v_ffn1_w_gate, v_ffn1_w_up=v_ffn1_w_up, v_ffn1_w_down=v_ffn1_w_down, v_norm_mix=v_norm_mix, v_w_in=v_w_in, v_attn_sinks=v_attn_sinks, v_ssm_lambda_re=v_ssm_lambda_re, v_ssm_lambda_im=v_ssm_lambda_im, v_ssm_log_dt=v_ssm_log_dt, v_ssm_b_re=v_ssm_b_re, v_ssm_b_im=v_ssm_b_im, v_ssm_c_re=v_ssm_c_re, v_ssm_c_im=v_ssm_c_im, v_ssm_d=v_ssm_d, v_ssm_glu_w=v_ssm_glu_w, v_ssm_glu_b=v_ssm_glu_b, v_attn_out_norm=v_attn_out_norm, v_ssm_out_norm=v_ssm_out_norm, v_w_out=v_w_out, v_norm_ffn2=v_norm_ffn2, v_ffn2_w_gate=v_ffn2_w_gate, v_ffn2_w_up=v_ffn2_w_up, v_ffn2_w_down=v_ffn2_w_down, v_final_norm=v_final_norm)
    weights = {n: given[n] for n in TWIN_WEIGHTS}
    shared = {n: given[n] for n in SHARED_INPUTS}
    per_example = {n: given[n] for n in ['x']}
    grad_fn = _jax.value_and_grad(_loss, argnums=(0, 1))

    def one_microbatch(ex, loss_target):
        ex = dict(ex)
        diff = ex.pop(TWIN_DIFF_INPUT)
        return grad_fn(weights, diff, {**shared, **ex}, loss_target)

    if N_MICROBATCH == 1:
        loss, (grad_w, grad_x) = one_microbatch(per_example, given["loss_target"])
    else:
        def body(carry, xs):
            loss_sum, grad_sum = carry
            l_k, (gw_k, gx_k) = one_microbatch(xs[0], xs[1])
            with _jax.named_scope("update"):
                return (loss_sum + l_k, _jax.tree.map(_jnp.add, grad_sum, gw_k)), gx_k

        init = (_jnp.zeros((), _jnp.float32), _jax.tree.map(_jnp.zeros_like, weights))
        (loss, grad_w), grad_x = _jax.lax.scan(body, init, (per_example, given["loss_target"]))
    with _jax.named_scope("update"):
        delta_w, new_m, new_v = {}, {}, {}
        for n in TWIN_WEIGHTS:
            delta_w[n], new_m[n], new_v[n] = _adamw(weights[n], grad_w[n], given["m_" + n], given["v_" + n])
    return (loss, grad_x, *[grad_w[n] for n in TWIN_WEIGHTS], *[delta_w[n] for n in TWIN_WEIGHTS],
            *[new_m[n] for n in TWIN_WEIGHTS], *[new_v[n] for n in TWIN_WEIGHTS])
```

```python
import functools
import math

import numpy as np
import jax
import jax.numpy as jnp
from jax import lax
from jax.experimental import pallas as pl
from jax.experimental.pallas import tpu as pltpu

F32 = jnp.float32
BF16 = jnp.bfloat16
MESH = pl.DeviceIdType.MESH

EPS = 1e-6
NEG_INF = -1e30
LAMBDA_RE_MAX = -1e-4
ATTN_HEADS = 8
KV_HEADS = 2
GQ = ATTN_HEADS // KV_HEADS
HEAD_DIM = 64
ATTN_WIDTH = 512
KV_WIDTH = 128
WINDOW = 128
QBLOCK = 128
SSM_WIDTH = 512
SSM_GROUPS = 32
SSM_CH = 16
SSM_STATE = 64
N_STRIPS = 4
STRIP_IN = SSM_WIDTH // N_STRIPS
STRIP_ST = SSM_GROUPS * SSM_STATE // N_STRIPS
SUBLANES = 8
LANES = 128
N_CHIPS = 4
N_DEV = 8

ADAM_LR = 0.001
ADAM_B1 = 0.9
ADAM_B2 = 0.999
ADAM_EPS = 1e-08
ADAM_WD = 0.01
ADAM_STEP = 10

VMEM_LIMIT = 48 * 1024 * 1024

WEIGHTS = ['norm_ffn1', 'ffn1_w_gate', 'ffn1_w_up', 'ffn1_w_down', 'norm_mix', 'w_in', 'attn_sinks',
           'ssm_lambda_re', 'ssm_lambda_im', 'ssm_log_dt', 'ssm_b_re', 'ssm_b_im', 'ssm_c_re', 'ssm_c_im',
           'ssm_d', 'ssm_glu_w', 'ssm_glu_b', 'attn_out_norm', 'ssm_out_norm', 'w_out', 'norm_ffn2',
           'ffn2_w_gate', 'ffn2_w_up', 'ffn2_w_down', 'final_norm']
BIG = ['ffn1_w_gate', 'ffn1_w_up', 'ffn1_w_down', 'w_in', 'ssm_glu_w', 'w_out',
       'ffn2_w_gate', 'ffn2_w_up', 'ffn2_w_down']
SMALL = [n for n in WEIGHTS if n not in BIG]


def _cparams(sem=None):
    return pltpu.CompilerParams(dimension_semantics=sem, vmem_limit_bytes=VMEM_LIMIT)


def _tile(n, pref):
    if n <= pref:
        return n
    for t in (pref, pref // 2, pref // 4):
        if t % LANES == 0 and n % t == 0:
            return t
    return n


def _sigmoid(x):
    return 1.0 / (1.0 + jnp.exp(-x))


def _mm(a, b, *, ta=False, tb=False, reduce_s=False, res=None, scale=1.0, out_dtype=F32, name):
    a3 = a if a.ndim == 3 else a[None]
    b3 = b if b.ndim == 3 else b[None]
    sa, sb = a3.shape[0], b3.shape[0]
    ns = max(sa, sb)
    (kk, m) = a3.shape[1:] if ta else a3.shape[1:][::-1]
    (n, kb) = b3.shape[1:] if tb else b3.shape[1:][::-1]
    assert kk == kb, (a3.shape, b3.shape)
    tm, tn, tk = _tile(m, 512), _tile(n, 512), _tile(kk, 512)
    nm, nn, nk = m // tm, n // tn, kk // tk
    has_res = res is not None

    if reduce_s:
        grid = (nm, nn, ns, nk)
        ids = lambda i, j, s, k: (s, i, j, k)
        sem = ("parallel", "parallel", "arbitrary", "arbitrary")
    else:
        grid = (ns, nm, nn, nk)
        ids = lambda s, i, j, k: (s, i, j, k)
        sem = ("parallel", "parallel", "parallel", "arbitrary")

    def a_map(*g):
        s, i, j, k = ids(*g)
        s = s if sa > 1 else 0
        return (s, k, i) if ta else (s, i, k)

    def b_map(*g):
        s, i, j, k = ids(*g)
        s = s if sb > 1 else 0
        return (s, j, k) if tb else (s, k, j)

    def o_map(*g):
        s, i, j, k = ids(*g)
        return (i, j) if reduce_s else (s, i, j)

    a_blk = (1, tk, tm) if ta else (1, tm, tk)
    b_blk = (1, tn, tk) if tb else (1, tk, tn)
    dims = (((0 if ta else 1,), (1 if tb else 0,)), ((), ()))

    def body(*refs):
        if has_res:
            a_ref, b_ref, r_ref, o_ref, acc_ref = refs
        else:
            a_ref, b_ref, o_ref, acc_ref = refs
        s, _, _, k = ids(*[pl.program_id(d) for d in range(4)])
        if reduce_s:
            first = jnp.logical_and(s == 0, k == 0)
            last = jnp.logical_and(s == ns - 1, k == nk - 1)
        else:
            first, last = k == 0, k == nk - 1

        @pl.when(first)
        def _():
            acc_ref[...] = jnp.zeros_like(acc_ref)

        acc_ref[...] += lax.dot_general(a_ref[0].astype(BF16), b_ref[0].astype(BF16), dims,
                                        preferred_element_type=F32)

        @pl.when(last)
        def _():
            out = acc_ref[...] * scale if scale != 1.0 else acc_ref[...]
            if has_res:
                out = r_ref[...].reshape(out.shape) + out
            o_ref[...] = out.astype(out_dtype).reshape(o_ref.shape)

    in_specs = [pl.BlockSpec(a_blk, a_map), pl.BlockSpec(b_blk, b_map)]
    args = [a3, b3]
    if reduce_s:
        out_shape = jax.ShapeDtypeStruct((m, n), out_dtype)
        o_spec = pl.BlockSpec((tm, tn), o_map)
    else:
        out_shape = jax.ShapeDtypeStruct((ns, m, n), out_dtype)
        o_spec = pl.BlockSpec((1, tm, tn), o_map)
    if has_res:
        assert res.shape == out_shape.shape
        in_specs.append(o_spec)
        args.append(res)
    return pl.pallas_call(body, out_shape=out_shape, grid=grid, in_specs=in_specs, out_specs=o_spec,
                          scratch_shapes=[pltpu.VMEM((tm, tn), F32)], compiler_params=_cparams(sem),
                          name=name)(*args)


def _row_tile(t):
    for tr in (256, 128, 64, 32, 16, 8):
        if t % tr == 0:
            return tr
    return t


def _rms_fwd(x, g, name):
    t, w = x.shape
    tr = _row_tile(t)

    def body(x_ref, g_ref, o_ref):
        xv = x_ref[...]
        r = lax.rsqrt(jnp.mean(xv * xv, axis=-1, keepdims=True) + EPS)
        o_ref[...] = (xv * r * g_ref[...]).astype(BF16)

    return pl.pallas_call(
        body, out_shape=jax.ShapeDtypeStruct((t, w), BF16), grid=(t // tr,),
        in_specs=[pl.BlockSpec((tr, w), lambda i: (i, 0)), pl.BlockSpec((1, w), lambda i: (0, 0))],
        out_specs=pl.BlockSpec((tr, w), lambda i: (i, 0)), compiler_params=_cparams(("parallel",)),
        name=name)(x, g)


def _rms_bwd(x, g, dh, dres, name):
    t, w = x.shape
    tr = _row_tile(t)
    has_res = dres is not None

    def body(*refs):
        if has_res:
            x_ref, g_ref, dh_ref, dr_ref, dx_ref, dg_ref = refs
        else:
            x_ref, g_ref, dh_ref, dx_ref, dg_ref = refs
        xv = x_ref[...]
        r = lax.rsqrt(jnp.mean(xv * xv, axis=-1, keepdims=True) + EPS)
        nrm = xv * r
        dhv = dh_ref[...]
        dn = dhv * g_ref[...]
        dx = r * (dn - nrm * jnp.mean(dn * nrm, axis=-1, keepdims=True))
        if has_res:
            dx = dx + dr_ref[...]
        dx_ref[...] = dx

        @pl.when(pl.program_id(0) == 0)
        def _():
            dg_ref[...] = jnp.zeros_like(dg_ref)

        dg_ref[...] += jnp.sum(dhv * nrm, axis=0, keepdims=True)

    row = pl.BlockSpec((tr, w), lambda i: (i, 0))
    vec = pl.BlockSpec((1, w), lambda i: (0, 0))
    ins = [x, g, dh] + ([dres] if has_res else [])
    return pl.pallas_call(
        body, out_shape=(jax.ShapeDtypeStruct((t, w), F32), jax.ShapeDtypeStruct((1, w), F32)),
        grid=(t // tr,), in_specs=[row, vec, row] + ([row] if has_res else []),
        out_specs=(row, vec), compiler_params=_cparams(("arbitrary",)), name=name)(*ins)


def _swiglu_fwd(gate, up, name):
    s, t, f = gate.shape
    tr = _row_tile(t)

    def body(g_ref, u_ref, o_ref):
        gv = g_ref[...]
        o_ref[...] = (gv * _sigmoid(gv) * u_ref[...]).astype(BF16)

    blk = pl.BlockSpec((1, tr, f), lambda si, i: (si, i, 0))
    return pl.pallas_call(body, out_shape=jax.ShapeDtypeStruct((s, t, f), BF16), grid=(s, t // tr),
                          in_specs=[blk, blk], out_specs=blk,
                          compiler_params=_cparams(("parallel", "parallel")), name=name)(gate, up)


def _swiglu_bwd(da, gate, up, name):
    s, t, f = gate.shape
    tr = _row_tile(t)

    def body(da_ref, g_ref, u_ref, dg_ref, du_ref):
        gv = g_ref[...]
        sg = _sigmoid(gv)
        dav = da_ref[...]
        du_ref[...] = (dav * gv * sg).astype(BF16)
        dg_ref[...] = (dav * u_ref[...] * sg * (1.0 + gv * (1.0 - sg))).astype(BF16)

    blk = pl.BlockSpec((1, tr, f), lambda si, i: (si, i, 0))
    sh = jax.ShapeDtypeStruct((s, t, f), BF16)
    return pl.pallas_call(body, out_shape=(sh, sh), grid=(s, t // tr), in_specs=[blk, blk, blk],
                          out_specs=(blk, blk), compiler_params=_cparams(("parallel", "parallel")),
                          name=name)(da, gate, up)


def _loss_head(x, g, tgt, name):
    t, w = x.shape
    tr = _row_tile(t)

    def body(x_ref, g_ref, t_ref, loss_ref, dx_ref, dg_ref):
        xv = x_ref[...]
        gv = g_ref[...]
        r = lax.rsqrt(jnp.mean(xv * xv, axis=-1, keepdims=True) + EPS)
        nrm = xv * r
        err = nrm * gv - t_ref[...]
        dout = err * (1.0 / w)
        dn = dout * gv
        dx_ref[...] = r * (dn - nrm * jnp.mean(dn * nrm, axis=-1, keepdims=True))

        @pl.when(pl.program_id(0) == 0)
        def _():
            dg_ref[...] = jnp.zeros_like(dg_ref)
            loss_ref[...] = jnp.zeros_like(loss_ref)

        dg_ref[...] += jnp.sum(dout * nrm, axis=0, keepdims=True)
        part = jnp.sum(jnp.sum(err * err, axis=-1, keepdims=True) * (0.5 / w), axis=0, keepdims=True)
        loss_ref[...] += jnp.broadcast_to(part, loss_ref.shape)

    row = pl.BlockSpec((tr, w), lambda i: (i, 0))
    vec = pl.BlockSpec((1, w), lambda i: (0, 0))
    return pl.pallas_call(
        body, out_shape=(jax.ShapeDtypeStruct((1, LANES), F32), jax.ShapeDtypeStruct((t, w), F32),
                         jax.ShapeDtypeStruct((1, w), F32)),
        grid=(t // tr,), in_specs=[row, vec, row],
        out_specs=(pl.BlockSpec((1, LANES), lambda i: (0, 0)), row, vec),
        compiler_params=_cparams(("arbitrary",)), name=name)(x, g, tgt)


def _attn_scores(q, k3, n, t, slope_ref):
    rows = GQ * QBLOCK
    s = lax.dot_general(q, k3, (((1,), (1,)), ((), ())), preferred_element_type=F32) * (HEAD_DIM ** -0.5)
    row = lax.broadcasted_iota(jnp.int32, (rows, 3 * QBLOCK), 0) & (QBLOCK - 1)
    col = lax.broadcasted_iota(jnp.int32, (rows, 3 * QBLOCK), 1)
    rel = jnp.abs(col - QBLOCK - row)
    key_pos = n * QBLOCK - QBLOCK + col
    valid = (rel <= WINDOW) & (key_pos >= 0) & (key_pos < t)
    return jnp.where(valid, s - slope_ref[0] * rel.astype(F32), NEG_INF)


def _attn_fwd(q4, kp, vp, sink_rows, slope_rows, name):
    _, _, t, _ = q4.shape
    nb = t // QBLOCK
    rows = GQ * QBLOCK

    def body(q_ref, k_ref, v_ref, sink_ref, slope_ref, o_ref, lse_ref):
        n = pl.program_id(1)
        start = pl.multiple_of(n * QBLOCK, QBLOCK)
        q = q_ref[0].reshape(rows, HEAD_DIM).astype(BF16)
        k3 = k_ref[0, pl.ds(start, 3 * QBLOCK), :].astype(BF16)
        v3 = v_ref[0, pl.ds(start, 3 * QBLOCK), :].astype(BF16)
        s = _attn_scores(q, k3, n, t, slope_ref)
        sink = sink_ref[0]
        mx = jnp.maximum(jnp.max(s, axis=-1, keepdims=True), sink)
        p = jnp.exp(s - mx)
        den = jnp.sum(p, axis=-1, keepdims=True) + jnp.exp(sink - mx)
        o = lax.dot_general(p.astype(BF16), v3, (((1,), (0,)), ((), ())), preferred_element_type=F32)
        o_ref[0] = (o / den).reshape(GQ, QBLOCK, HEAD_DIM)
        lse_ref[0] = (mx + jnp.log(den)).reshape(GQ, QBLOCK, 1)

    qspec = pl.BlockSpec((1, GQ, QBLOCK, HEAD_DIM), lambda h, n: (h, 0, n, 0))
    kvspec = pl.BlockSpec((1, t + 2 * QBLOCK, HEAD_DIM), lambda h, n: (h, 0, 0))
    rowspec = pl.BlockSpec((1, rows, 1), lambda h, n: (h, 0, 0))
    return pl.pallas_call(
        body, out_shape=(jax.ShapeDtypeStruct(q4.shape, F32), jax.ShapeDtypeStruct((KV_HEADS, GQ, t, 1), F32)),
        grid=(KV_HEADS, nb), in_specs=[qspec, kvspec, kvspec, rowspec, rowspec],
        out_specs=(qspec, pl.BlockSpec((1, GQ, QBLOCK, 1), lambda h, n: (h, 0, n, 0))),
        compiler_params=_cparams(("parallel", "parallel")), name=name)(q4, kp, vp, sink_rows, slope_rows)


def _attn_bwd(q4, kp, vp, sink_rows, slope_rows, o4, lse4, do4, name):
    _, _, t, _ = q4.shape
    nb = t // QBLOCK
    rows = GQ * QBLOCK
    scale = HEAD_DIM ** -0.5

    def body(q_ref, k_ref, v_ref, sink_ref, slope_ref, o_ref, lse_ref, do_ref, dq_ref, dk_ref, dv_ref, ds_ref):
        n = pl.program_id(1)
        start = pl.multiple_of(n * QBLOCK, QBLOCK)

        @pl.when(n == 0)
        def _():
            dk_ref[...] = jnp.zeros_like(dk_ref)
            dv_ref[...] = jnp.zeros_like(dv_ref)
            ds_ref[...] = jnp.zeros_like(ds_ref)

        q = q_ref[0].reshape(rows, HEAD_DIM).astype(BF16)
        k3 = k_ref[0, pl.ds(start, 3 * QBLOCK), :].astype(BF16)
        v3 = v_ref[0, pl.ds(start, 3 * QBLOCK), :].astype(BF16)
        do = do_ref[0].reshape(rows, HEAD_DIM)
        lse = lse_ref[0].reshape(rows, 1)
        s = _attn_scores(q, k3, n, t, slope_ref)
        p = jnp.exp(s - lse)
        delta = jnp.sum(do * o_ref[0].reshape(rows, HEAD_DIM), axis=-1, keepdims=True)
        dob = do.astype(BF16)
        dp = lax.dot_general(dob, v3, (((1,), (1,)), ((), ())), preferred_element_type=F32)
        dsb = (p * (dp - delta)).astype(BF16)
        dq = lax.dot_general(dsb, k3, (((1,), (0,)), ((), ())), preferred_element_type=F32) * scale
        dq_ref[0] = dq.reshape(GQ, QBLOCK, HEAD_DIM)
        dk3 = lax.dot_general(dsb, q, (((0,), (0,)), ((), ())), preferred_element_type=F32) * scale
        dv3 = lax.dot_general(p.astype(BF16), dob, (((0,), (0,)), ((), ())), preferred_element_type=F32)
        dk_ref[0, pl.ds(start, 3 * QBLOCK), :] += dk3
        dv_ref[0, pl.ds(start, 3 * QBLOCK), :] += dv3
        dsink_rows = -jnp.exp(sink_ref[0] - lse) * delta
        ds_ref[0] += jnp.sum(dsink_rows.reshape(GQ, QBLOCK, 1), axis=1)

    qspec = pl.BlockSpec((1, GQ, QBLOCK, HEAD_DIM), lambda h, n: (h, 0, n, 0))
    kvspec = pl.BlockSpec((1, t + 2 * QBLOCK, HEAD_DIM), lambda h, n: (h, 0, 0))
    rowspec = pl.BlockSpec((1, rows, 1), lambda h, n: (h, 0, 0))
    lsespec = pl.BlockSpec((1, GQ, QBLOCK, 1), lambda h, n: (h, 0, n, 0))
    return pl.pallas_call(
        body,
        out_shape=(jax.ShapeDtypeStruct(q4.shape, F32), jax.ShapeDtypeStruct(kp.shape, F32),
                   jax.ShapeDtypeStruct(vp.shape, F32), jax.ShapeDtypeStruct((KV_HEADS, GQ, 1), F32)),
        grid=(KV_HEADS, nb),
        in_specs=[qspec, kvspec, kvspec, rowspec, rowspec, qspec, lsespec, qspec],
        out_specs=(qspec, kvspec, kvspec, pl.BlockSpec((1, GQ, 1), lambda h, n: (h, 0, 0))),
        compiler_params=_cparams(("parallel", "arbitrary")), name=name)(
            q4, kp, vp, sink_rows, slope_rows, o4, lse4, do4)


def _scan_tables(a_re, a_im, reverse):
    pw = [(a_re, a_im)]
    for _ in range(SUBLANES - 1):
        pr, pi = pw[-1]
        pw.append((pr * a_re - pi * a_im, pr * a_im + pi * a_re))
    rows = np.arange(SUBLANES)
    tabs = []
    for d in (1, 2, 4):
        mask = (rows <= SUBLANES - 1 - d) if reverse else (rows >= d)
        m = jnp.asarray(mask, F32)[:, None]
        tabs += [m * pw[d - 1][0][None, :], m * pw[d - 1][1][None, :]]
    order = (SUBLANES - 1 - rows) if reverse else rows
    tabs += [jnp.stack([pw[j][0] for j in order]), jnp.stack([pw[j][1] for j in order])]
    tab = jnp.stack(tabs)
    return tab.reshape(8, SUBLANES, N_STRIPS, STRIP_ST).transpose(2, 0, 1, 3)


def _scan(v, mi_re, mi_im, tab, mo_re, mo_im, reverse, name):
    t = v.shape[0]
    tc = _tile(t, 256)
    nc = t // tc
    nblk = tc // SUBLANES

    def body(v_ref, mir_ref, mii_ref, tab_ref, mor_ref, moi_ref, y_ref, xr_ref, xi_ref, carry_ref):
        @pl.when(pl.program_id(1) == 0)
        def _():
            carry_ref[...] = jnp.zeros_like(carry_ref)

        vb = v_ref[...].astype(BF16)
        xr_ref[...] = jnp.dot(vb, mir_ref[0], preferred_element_type=F32)
        xi_ref[...] = jnp.dot(vb, mii_ref[0], preferred_element_type=F32)

        def blk(i, carry):
            cr, ci = carry
            b = (nblk - 1 - i) if reverse else i
            r0 = pl.multiple_of(b * SUBLANES, SUBLANES)
            xr = xr_ref[pl.ds(r0, SUBLANES), :]
            xi = xi_ref[pl.ds(r0, SUBLANES), :]
            for j, d in enumerate((1, 2, 4)):
                tr_, ti_ = tab_ref[0, 2 * j], tab_ref[0, 2 * j + 1]
                sh = (SUBLANES - d) if reverse else d
                sr = pltpu.roll(xr, sh, 0)
                si = pltpu.roll(xi, sh, 0)
                xr, xi = xr + tr_ * sr - ti_ * si, xi + tr_ * si + ti_ * sr
            pr, pi = tab_ref[0, 6], tab_ref[0, 7]
            xr, xi = xr + pr * cr - pi * ci, xi + pr * ci + pi * cr
            xr_ref[pl.ds(r0, SUBLANES), :] = xr
            xi_ref[pl.ds(r0, SUBLANES), :] = xi
            edge = 0 if reverse else SUBLANES - 1
            return (jnp.broadcast_to(xr[edge:edge + 1, :], xr.shape),
                    jnp.broadcast_to(xi[edge:edge + 1, :], xi.shape))

        cr, ci = lax.fori_loop(0, nblk, blk, (carry_ref[0], carry_ref[1]))
        carry_ref[0] = cr
        carry_ref[1] = ci
        y_ref[...] = (jnp.dot(xr_ref[...].astype(BF16), mor_ref[0], preferred_element_type=F32)
                      + jnp.dot(xi_ref[...].astype(BF16), moi_ref[0], preferred_element_type=F32))

    tmap = (lambda s, c: (nc - 1 - c, s)) if reverse else (lambda s, c: (c, s))
    smap3 = lambda s, c: (s, 0, 0)
    return pl.pallas_call(
        body,
        out_shape=(jax.ShapeDtypeStruct((t, SSM_WIDTH), F32),
                   jax.ShapeDtypeStruct((t, N_STRIPS * STRIP_ST), F32),
                   jax.ShapeDtypeStruct((t, N_STRIPS * STRIP_ST), F32)),
        grid=(N_STRIPS, nc),
        in_specs=[pl.BlockSpec((tc, STRIP_IN), tmap),
                  pl.BlockSpec((1, STRIP_IN, STRIP_ST), smap3), pl.BlockSpec((1, STRIP_IN, STRIP_ST), smap3),
                  pl.BlockSpec((1, 8, SUBLANES, STRIP_ST), lambda s, c: (s, 0, 0, 0)),
                  pl.BlockSpec((1, STRIP_ST, STRIP_IN), smap3), pl.BlockSpec((1, STRIP_ST, STRIP_IN), smap3)],
        out_specs=(pl.BlockSpec((tc, STRIP_IN), tmap), pl.BlockSpec((tc, STRIP_ST), tmap),
                   pl.BlockSpec((tc, STRIP_ST), tmap)),
        scratch_shapes=[pltpu.VMEM((2, SUBLANES, STRIP_ST), F32)],
        compiler_params=_cparams(("parallel", "arbitrary")), name=name)(v, mi_re, mi_im, tab, mo_re, mo_im)


def _scan_param_grads(v, dy, xr, xi, lr, li, reverse, name):
    t = v.shape[0]
    tc = _tile(t, 256)
    nc = t // tc
    hb = tc // SUBLANES

    def body(v_ref, dy_ref, xr_ref, xi_ref, lr_ref, li_ref, hr_ref, hi_ref,
             dmir_ref, dmii_ref, dmor_ref, dmoi_ref, da_ref):
        c = pl.program_id(1)

        @pl.when(c == 0)
        def _():
            for r in (dmir_ref, dmii_ref, dmor_ref, dmoi_ref, da_ref):
                r[...] = jnp.zeros_like(r)

        xrv, xiv, lrv, liv = xr_ref[...], xi_ref[...], lr_ref[...], li_ref[...]
        row = lax.broadcasted_iota(jnp.int32, xrv.shape, 0)
        if reverse:
            live = (c < nc - 1).astype(F32)
            edge_r, edge_i = hr_ref[0:1, :] * live, hi_ref[0:1, :] * live
            xpr = jnp.where(row == tc - 1, edge_r, pltpu.roll(xrv, tc - 1, 0))
            xpi = jnp.where(row == tc - 1, edge_i, pltpu.roll(xiv, tc - 1, 0))
        else:
            live = (c > 0).astype(F32)
            edge_r, edge_i = hr_ref[SUBLANES - 1:SUBLANES, :] * live, hi_ref[SUBLANES - 1:SUBLANES, :] * live
            xpr = jnp.where(row == 0, edge_r, pltpu.roll(xrv, 1, 0))
            xpi = jnp.where(row == 0, edge_i, pltpu.roll(xiv, 1, 0))
        da_ref[0, 0:1, :] += jnp.sum(xpr * lrv + xpi * liv, axis=0, keepdims=True)
        da_ref[0, 1:2, :] += jnp.sum(xpr * liv - xpi * lrv, axis=0, keepdims=True)
        tdims = (((0,), (0,)), ((), ()))
        vb, dyb = v_ref[...].astype(BF16), dy_ref[...].astype(BF16)
        dmir_ref[0] += lax.dot_general(vb, lrv.astype(BF16), tdims, preferred_element_type=F32)
        dmii_ref[0] += lax.dot_general(vb, liv.astype(BF16), tdims, preferred_element_type=F32)
        dmor_ref[0] += lax.dot_general(xrv.astype(BF16), dyb, tdims, preferred_element_type=F32)
        dmoi_ref[0] += lax.dot_general(xiv.astype(BF16), dyb, tdims, preferred_element_type=F32)

    tmap = lambda s, c: (c, s)
    if reverse:
        hmap = lambda s, c: (jnp.minimum((c + 1) * hb, t // SUBLANES - 1), s)
    else:
        hmap = lambda s, c: (jnp.maximum(c * hb - 1, 0), s)
    narrow = pl.BlockSpec((tc, STRIP_IN), tmap)
    wide = pl.BlockSpec((tc, STRIP_ST), tmap)
    halo = pl.BlockSpec((SUBLANES, STRIP_ST), hmap)
    smap3 = lambda s, c: (s, 0, 0)
    return pl.pallas_call(
        body,
        out_shape=(jax.ShapeDtypeStruct((N_STRIPS, STRIP_IN, STRIP_ST), F32),
                   jax.ShapeDtypeStruct((N_STRIPS, STRIP_IN, STRIP_ST), F32),
                   jax.ShapeDtypeStruct((N_STRIPS, STRIP_ST, STRIP_IN), F32),
                   jax.ShapeDtypeStruct((N_STRIPS, STRIP_ST, STRIP_IN), F32),
                   jax.ShapeDtypeStruct((N_STRIPS, SUBLANES, STRIP_ST), F32)),
        grid=(N_STRIPS, nc),
        in_specs=[narrow, narrow, wide, wide, wide, wide, halo, halo],
        out_specs=(pl.BlockSpec((1, STRIP_IN, STRIP_ST), smap3), pl.BlockSpec((1, STRIP_IN, STRIP_ST), smap3),
                   pl.BlockSpec((1, STRIP_ST, STRIP_IN), smap3), pl.BlockSpec((1, STRIP_ST, STRIP_IN), smap3),
                   pl.BlockSpec((1, SUBLANES, STRIP_ST), smap3)),
        compiler_params=_cparams(("parallel", "arbitrary")), name=name)(v, dy, xr, xi, lr, li, xr, xi)


def _ssm_prep(lam_re, lam_im, log_dt, b_re, b_im, c_re, c_im):
    lr = jnp.minimum(lam_re, LAMBDA_RE_MAX)
    li = lam_im
    dt = jnp.exp(log_dt)[:, None]
    mag = jnp.exp(lr * dt)
    a_re = mag * jnp.cos(li * dt)
    a_im = mag * jnp.sin(li * dt)
    den = lr * lr + li * li
    coef_re = ((a_re - 1.0) * lr + a_im * li) / den
    coef_im = (a_im * lr - (a_re - 1.0) * li) / den
    bb_re = coef_re[..., None] * b_re - coef_im[..., None] * b_im
    bb_im = coef_re[..., None] * b_im + coef_im[..., None] * b_re
    eye = jnp.eye(SSM_GROUPS // N_STRIPS, dtype=F32)

    def strips(m):
        g, a, b = m.shape
        m4 = m.reshape(N_STRIPS, g // N_STRIPS, a, b)
        return jnp.einsum('sgab,gk->sgakb', m4, eye).reshape(N_STRIPS, g // N_STRIPS * a, g // N_STRIPS * b)

    mi_re = strips(jnp.swapaxes(bb_re, 1, 2))
    mi_im = strips(jnp.swapaxes(bb_im, 1, 2))
    mo_re = strips(jnp.swapaxes(c_re, 1, 2))
    mo_im = strips(-jnp.swapaxes(c_im, 1, 2))
    return a_re.reshape(-1), a_im.reshape(-1), mi_re, mi_im, mo_re, mo_im


def _gelu(x):
    c = math.sqrt(2.0 / math.pi)
    return 0.5 * x * (1.0 + jnp.tanh(c * (x + 0.044715 * x * x * x)))


def _gelu_grad(x):
    c = math.sqrt(2.0 / math.pi)
    th = jnp.tanh(c * (x + 0.044715 * x * x * x))
    return 0.5 * (1.0 + th) + 0.5 * x * (1.0 - th * th) * c * (1.0 + 3.0 * 0.044715 * x * x)


def _ssm_post_fwd(u, yf, yb, d, wglu, bglu, name):
    t, w = u.shape
    tr = _row_tile(t)

    def body(u_ref, yf_ref, yb_ref, d_ref, w_ref, b_ref, s_ref, y0_ref, z_ref):
        y0 = d_ref[...] * u_ref[...] + yf_ref[...] + yb_ref[...]
        yg = _gelu(y0)
        z = jnp.dot(yg.astype(BF16), w_ref[...], preferred_element_type=F32) + b_ref[...]
        s_ref[...] = yg * _sigmoid(z)
        y0_ref[...] = y0
        z_ref[...] = z

    row = pl.BlockSpec((tr, w), lambda i: (i, 0))
    vec = pl.BlockSpec((1, w), lambda i: (0, 0))
    mat = pl.BlockSpec((w, w), lambda i: (0, 0))
    sh = jax.ShapeDtypeStruct((t, w), F32)
    return pl.pallas_call(body, out_shape=(sh, sh, sh), grid=(t // tr,),
                          in_specs=[row, row, row, vec, mat, vec], out_specs=(row, row, row),
                          compiler_params=_cparams(("parallel",)), name=name)(u, yf, yb, d, wglu, bglu)


def _ssm_post_bwd(ds, y0, z, u, d, wglu, name):
    t, w = u.shape
    tr = _row_tile(t)

    def body(ds_ref, y0_ref, z_ref, u_ref, d_ref, w_ref, dy0_ref, dw_ref, db_ref, dd_ref):
        @pl.when(pl.program_id(0) == 0)
        def _():
            dw_ref[...] = jnp.zeros_like(dw_ref)
            db_ref[...] = jnp.zeros_like(db_ref)
            dd_ref[...] = jnp.zeros_like(dd_ref)

        y0 = y0_ref[...]
        yg = _gelu(y0)
        sg = _sigmoid(z_ref[...])
        dsv = ds_ref[...]
        dz = dsv * yg * sg * (1.0 - sg)
        dzb = dz.astype(BF16)
        dyg = dsv * sg + lax.dot_general(dzb, w_ref[...], (((1,), (1,)), ((), ())), preferred_element_type=F32)
        dy0 = dyg * _gelu_grad(y0)
        dy0_ref[...] = dy0
        dw_ref[...] += lax.dot_general(yg.astype(BF16), dzb, (((0,), (0,)), ((), ())), preferred_element_type=F32)
        db_ref[...] += jnp.sum(dz, axis=0, keepdims=True)
        dd_ref[...] += jnp.sum(dy0 * u_ref[...], axis=0, keepdims=True)

    row = pl.BlockSpec((tr, w), lambda i: (i, 0))
    vec = pl.BlockSpec((1, w), lambda i: (0, 0))
    mat = pl.BlockSpec((w, w), lambda i: (0, 0))
    return pl.pallas_call(
        body, out_shape=(jax.ShapeDtypeStruct((t, w), F32), jax.ShapeDtypeStruct((w, w), F32),
                         jax.ShapeDtypeStruct((1, w), F32), jax.ShapeDtypeStruct((1, w), F32)),
        grid=(t // tr,), in_specs=[row, row, row, row, vec, mat], out_specs=(row, mat, vec, vec),
        compiler_params=_cparams(("arbitrary",)), name=name)(ds, y0, z, u, d, wglu)


def _du_combine(dy0, d, du_f, du_b, name):
    t, w = dy0.shape
    tr = _row_tile(t)

    def body(dy_ref, d_ref, a_ref, b_ref, o_ref):
        o_ref[...] = d_ref[...] * dy_ref[...] + a_ref[...] + b_ref[...]

    row = pl.BlockSpec((tr, w), lambda i: (i, 0))
    vec = pl.BlockSpec((1, w), lambda i: (0, 0))
    return pl.pallas_call(body, out_shape=jax.ShapeDtypeStruct((t, w), F32), grid=(t // tr,),
                          in_specs=[row, vec, row, row], out_specs=row, compiler_params=_cparams(("parallel",)),
                          name=name)(dy0, d, du_f, du_b)


def _ffn_fwd(x, g, wg, wu, wd, tag):
    h = _rms_fwd(x, g, f"{tag}_norm")
    gate = _mm(h, wg, name=f"{tag}_gate")
    up = _mm(h, wu, name=f"{tag}_up")
    act = _swiglu_fwd(gate, up, f"{tag}_act")
    xo = _mm(act, wd, reduce_s=True, res=x, scale=0.5, name=f"{tag}_down")
    return xo, (h, gate, up, act)


def _ffn_bwd(dxo, x, g, wg, wu, wd, saved, tag):
    h, gate, up, act = saved
    dact = _mm(dxo, wd, tb=True, scale=0.5, name=f"{tag}_dact")
    dwd = _mm(act, dxo, ta=True, scale=0.5, name=f"{tag}_dwd")
    dgate, dup = _swiglu_bwd(dact, gate, up, f"{tag}_dswiglu")
    dwg = _mm(h, dgate, ta=True, name=f"{tag}_dwg")
    dwu = _mm(h, dup, ta=True, name=f"{tag}_dwu")
    dh = _mm(dgate, wg, tb=True, reduce_s=True, name=f"{tag}_dh_gate")
    dh = _mm(dup, wu, tb=True, reduce_s=True, res=dh, name=f"{tag}_dh_up")
    dx, dg = _rms_bwd(x, g, dh, dxo, f"{tag}_dnorm")
    return dx, dg, dwg, dwu, dwd


def _heads_split(q, k, v):
    t = q.shape[0]
    q4 = q.reshape(t, KV_HEADS, GQ, HEAD_DIM).transpose(1, 2, 0, 3)
    pad = lambda a: jnp.pad(a.reshape(t, KV_HEADS, HEAD_DIM).transpose(1, 0, 2), ((0, 0), (QBLOCK, QBLOCK), (0, 0)))
    return q4, pad(k), pad(v)


def _local_step(x, tgt, w):
    t = x.shape[0]
    row = lambda a: a.reshape(1, -1)
    grads = {}

    x1, ffn1_saved = _ffn_fwd(x, w['norm_ffn1'], w['ffn1_w_gate'], w['ffn1_w_up'], w['ffn1_w_down'], "ffn1")

    h2 = _rms_fwd(x1, w['norm_mix'], "mix_norm")
    proj = _mm(h2, w['w_in'], name="in_proj")
    proj = proj.transpose(1, 0, 2).reshape(t, -1)
    q = proj[:, :ATTN_WIDTH]
    k = proj[:, ATTN_WIDTH:ATTN_WIDTH + KV_WIDTH]
    v = proj[:, ATTN_WIDTH + KV_WIDTH:ATTN_WIDTH + 2 * KV_WIDTH]
    u = proj[:, ATTN_WIDTH + 2 * KV_WIDTH:]

    q4, kp, vp = _heads_split(q, k, v)
    sink_rows = jnp.repeat(w['attn_sinks'].reshape(KV_HEADS, GQ), QBLOCK, axis=1)[..., None]
    slopes = jnp.asarray(2.0 ** (-8.0 * (np.arange(ATTN_HEADS) + 1) / ATTN_HEADS), F32)
    slope_rows = jnp.repeat(slopes.reshape(KV_HEADS, GQ), QBLOCK, axis=1)[..., None]
    o4, lse4 = _attn_fwd(q4, kp, vp, sink_rows, slope_rows, "attn_fwd")
    attn = o4.transpose(2, 0, 1, 3).reshape(t, ATTN_WIDTH)

    ssm_names = ['ssm_lambda_re', 'ssm_lambda_im', 'ssm_log_dt', 'ssm_b_re', 'ssm_b_im', 'ssm_c_re', 'ssm_c_im']
    ys, states, preps, vjps = [], [], [], []
    for direction in range(2):
        params = [w[n][direction] for n in ssm_names]
        prep, vjp = jax.vjp(_ssm_prep, *params)
        a_re, a_im = prep[0], prep[1]
        mi_re, mi_im, mo_re, mo_im = (m.astype(BF16) for m in prep[2:])
        prep = (a_re, a_im, mi_re, mi_im, mo_re, mo_im)
        rev = direction == 1
        tab = _scan_tables(a_re, a_im, rev)
        y, xr, xi = _scan(u, mi_re, mi_im, tab, mo_re, mo_im, rev, f"s5_fwd{direction}")
        ys.append(y)
        states.append((xr, xi))
        preps.append(prep)
        vjps.append(vjp)
    d_row = row(w['ssm_d'])
    s, y0, z = _ssm_post_fwd(u, ys[0], ys[1], d_row, w['ssm_glu_w'], row(w['ssm_glu_b']), "ssm_post")

    ma = _rms_fwd(attn, row(w['attn_out_norm']), "attn_out_norm")
    ms = _rms_fwd(s, row(w['ssm_out_norm']), "ssm_out_norm")
    mixed = jnp.concatenate([ma, ms], axis=-1)
    x2 = _mm(mixed, w['w_out'], res=x1, reduce_s=True, name="out_proj")

    x3, ffn2_saved = _ffn_fwd(x2, w['norm_ffn2'], w['ffn2_w_gate'], w['ffn2_w_up'], w['ffn2_w_down'], "ffn2")

    loss_row, dx3, dgf = _loss_head(x3, row(w['final_norm']), tgt, "loss_head")
    loss = loss_row[0, 0]
    grads['final_norm'] = dgf.reshape(w['final_norm'].shape)

    dx2, dg, dwg, dwu, dwd = _ffn_bwd(dx3, x2, w['norm_ffn2'], w['ffn2_w_gate'], w['ffn2_w_up'],
                                      w['ffn2_w_down'], ffn2_saved, "ffn2")
    grads.update(norm_ffn2=dg, ffn2_w_gate=dwg, ffn2_w_up=dwu, ffn2_w_down=dwd)

    dmixed = _mm(dx2, w['w_out'], tb=True, reduce_s=True, name="out_proj_dx")
    grads['w_out'] = _mm(mixed, dx2, ta=True, name="out_proj_dw").reshape(w['w_out'].shape)
    dattn, dga = _rms_bwd(attn, row(w['attn_out_norm']), dmixed[:, :ATTN_WIDTH], None, "attn_out_dnorm")
    ds, dgs = _rms_bwd(s, row(w['ssm_out_norm']), dmixed[:, ATTN_WIDTH:], None, "ssm_out_dnorm")
    grads.update(attn_out_norm=dga, ssm_out_norm=dgs)

    dy0, dwglu, dbglu, dd = _ssm_post_bwd(ds, y0, z, u, d_row, w['ssm_glu_w'], "ssm_post_bwd")
    grads['ssm_glu_w'] = dwglu.reshape(w['ssm_glu_w'].shape)
    grads['ssm_glu_b'] = dbglu
    grads['ssm_d'] = dd.reshape(w['ssm_d'].shape)
    dparams, du_dirs = [], []
    for direction in range(2):
        a_re, a_im, mi_re, mi_im, mo_re, mo_im = preps[direction]
        rev = direction == 1
        tab = _scan_tables(a_re, -a_im, not rev)
        tr3 = lambda m: jnp.swapaxes(m, 1, 2)
        du_dir, lr, li = _scan(dy0, tr3(mo_re), tr3(mo_im), tab, tr3(mi_re), tr3(mi_im), not rev,
                               f"s5_adj{direction}")
        du_dirs.append(du_dir)
        xr, xi = states[direction]
        dmir, dmii, dmor, dmoi, da = _scan_param_grads(u, dy0, xr, xi, lr, li, rev, f"s5_pgrad{direction}")
        da_re = da[:, 0, :].reshape(-1)
        da_im = da[:, 1, :].reshape(-1)
        dparams.append(vjps[direction]((da_re, da_im, dmir, dmii, dmor, dmoi)))
    du = _du_combine(dy0, d_row, du_dirs[0], du_dirs[1], "ssm_du")
    for i, n in enumerate(ssm_names):
        grads[n] = jnp.stack([dparams[0][i], dparams[1][i]])

    do4 = dattn.reshape(t, KV_HEADS, GQ, HEAD_DIM).transpose(1, 2, 0, 3)
    dq4, dkp, dvp, dsink = _attn_bwd(q4, kp, vp, sink_rows, slope_rows, o4, lse4, do4, "attn_bwd")
    grads['attn_sinks'] = dsink.reshape(w['attn_sinks'].shape)
    dq = dq4.transpose(2, 0, 1, 3).reshape(t, ATTN_WIDTH)
    unpad = lambda a: a[:, QBLOCK:QBLOCK + t, :].transpose(1, 0, 2).reshape(t, KV_WIDTH)
    dproj = jnp.concatenate([dq, unpad(dkp), unpad(dvp), du], axis=-1)
    dproj = dproj.reshape(t, N_CHIPS, -1).transpose(1, 0, 2)

    grads['w_in'] = _mm(h2, dproj, ta=True, name="in_proj_dw")
    dh2 = _mm(dproj, w['w_in'], tb=True, reduce_s=True, name="in_proj_dx")
    dx1, dgm = _rms_bwd(x1, w['norm_mix'], dh2, dx2, "mix_dnorm")
    grads['norm_mix'] = dgm

    dx0, dg, dwg, dwu, dwd = _ffn_bwd(dx1, x, w['norm_ffn1'], w['ffn1_w_gate'], w['ffn1_w_up'],
                                      w['ffn1_w_down'], ffn1_saved, "ffn1")
    grads.update(norm_ffn1=dg, ffn1_w_gate=dwg, ffn1_w_up=dwu, ffn1_w_down=dwd)
    return loss, dx0, grads


HBM_SPEC = pl.BlockSpec(memory_space=pl.ANY)


def _chip_peers(x, y):
    return [(1 - x, y), (x, 1 - y), (1 - x, 1 - y)]


def _allgather(shards, name):
    nw = len(shards)

    def body(*refs):
        ins, outs = refs[:nw], refs[nw:2 * nw]
        send_sems, recv_sems, local_sems = refs[2 * nw:]
        x, y, c = lax.axis_index("x"), lax.axis_index("y"), lax.axis_index("c")
        me = 2 * x + y
        peers = _chip_peers(x, y)
        local = [pltpu.make_async_copy(ins[i], outs[i].at[me], local_sems.at[i]) for i in range(nw)]
        for cp in local:
            cp.start()
        for i in range(nw):
            for j, (px, py) in enumerate(peers):
                pltpu.make_async_remote_copy(ins[i], outs[i].at[me], send_sems.at[i, j], recv_sems.at[i, j],
                                             device_id=(px, py, c), device_id_type=MESH).start()
        for i in range(nw):
            for j, (px, py) in enumerate(peers):
                pltpu.make_async_remote_copy(ins[i], outs[i].at[2 * px + py], send_sems.at[i, j],
                                             recv_sems.at[i, j], device_id=(px, py, c),
                                             device_id_type=MESH).wait()
        for cp in local:
            cp.wait()

    return pl.pallas_call(
        body, out_shape=[jax.ShapeDtypeStruct((N_CHIPS,) + s.shape, s.dtype) for s in shards],
        in_specs=[HBM_SPEC] * nw, out_specs=[HBM_SPEC] * nw,
        scratch_shapes=[pltpu.SemaphoreType.DMA((nw, 3)), pltpu.SemaphoreType.DMA((nw, 3)),
                        pltpu.SemaphoreType.DMA((nw,))],
        name=name)(*shards)


def _grad_exchange(parts, small, name):
    nw = len(parts)
    rels = [(fx, fy, fc) for fx in (0, 1) for fy in (0, 1) for fc in (0, 1)][1:]

    def body(*refs):
        ins, small_in = refs[:nw], refs[nw]
        outs, small_out = refs[nw + 1:2 * nw + 1], refs[2 * nw + 1]
        send_sems, recv_sems, local_sems, ssend, srecv, slocal = refs[2 * nw + 2:]
        x, y, c = lax.axis_index("x"), lax.axis_index("y"), lax.axis_index("c")
        me = 2 * x + y
        lin = 4 * x + 2 * y + c
        peers = _chip_peers(x, y)
        local = [pltpu.make_async_copy(ins[i].at[me], outs[i].at[me], local_sems.at[i]) for i in range(nw)]
        local.append(pltpu.make_async_copy(small_in, small_out.at[lin], slocal))
        for cp in local:
            cp.start()
        for i in range(nw):
            for j, (px, py) in enumerate(peers):
                pltpu.make_async_remote_copy(ins[i].at[2 * px + py], outs[i].at[me], send_sems.at[i, j],
                                             recv_sems.at[i, j], device_id=(px, py, c),
                                             device_id_type=MESH).start()
        for j, (fx, fy, fc) in enumerate(rels):
            pltpu.make_async_remote_copy(small_in, small_out.at[lin], ssend.at[j], srecv.at[j],
                                         device_id=(x ^ fx, y ^ fy, c ^ fc), device_id_type=MESH).start()
        for i in range(nw):
            for j, (px, py) in enumerate(peers):
                pltpu.make_async_remote_copy(ins[i].at[2 * px + py], outs[i].at[2 * px + py], send_sems.at[i, j],
                                             recv_sems.at[i, j], device_id=(px, py, c),
                                             device_id_type=MESH).wait()
        for j, (fx, fy, fc) in enumerate(rels):
            src = 4 * (x ^ fx) + 2 * (y ^ fy) + (c ^ fc)
            pltpu.make_async_remote_copy(small_in, small_out.at[src], ssend.at[j], srecv.at[j],
                                         device_id=(x ^ fx, y ^ fy, c ^ fc), device_id_type=MESH).wait()
        for cp in local:
            cp.wait()

    out_shape = [jax.ShapeDtypeStruct(p.shape, p.dtype) for p in parts]
    out_shape.append(jax.ShapeDtypeStruct((N_DEV,) + small.shape, small.dtype))
    res = pl.pallas_call(
        body, out_shape=out_shape, in_specs=[HBM_SPEC] * (nw + 1), out_specs=[HBM_SPEC] * (nw + 1),
        scratch_shapes=[pltpu.SemaphoreType.DMA((nw, 3)), pltpu.SemaphoreType.DMA((nw, 3)),
                        pltpu.SemaphoreType.DMA((nw,)), pltpu.SemaphoreType.DMA((7,)),
                        pltpu.SemaphoreType.DMA((7,)), pltpu.SemaphoreType.DMA],
        name=name)(*parts, small)
    return res[:nw], res[nw]


def _sibling_swap(arrs, name):
    nw = len(arrs)

    def body(*refs):
        ins, outs = refs[:nw], refs[nw:2 * nw]
        send_sems, recv_sems = refs[2 * nw:]
        x, y, c = lax.axis_index("x"), lax.axis_index("y"), lax.axis_index("c")
        cps = [pltpu.make_async_remote_copy(ins[i], outs[i], send_sems.at[i], recv_sems.at[i],
                                            device_id=(x, y, 1 - c), device_id_type=MESH) for i in range(nw)]
        for cp in cps:
            cp.start()
        for cp in cps:
            cp.wait()

    return pl.pallas_call(
        body, out_shape=[jax.ShapeDtypeStruct(a.shape, a.dtype) for a in arrs],
        in_specs=[HBM_SPEC] * nw, out_specs=[HBM_SPEC] * nw,
        scratch_shapes=[pltpu.SemaphoreType.DMA((nw,)), pltpu.SemaphoreType.DMA((nw,))],
        name=name)(*arrs)


def _sum_blocks(a, name):
    s, r, c = a.shape
    tr = _row_tile(r)

    def body(a_ref, o_ref):
        acc = a_ref[0].astype(F32)
        for k in range(1, s):
            acc = acc + a_ref[k].astype(F32)
        o_ref[...] = acc

    return pl.pallas_call(body, out_shape=jax.ShapeDtypeStruct((r, c), F32), grid=(r // tr,),
                          in_specs=[pl.BlockSpec((s, tr, c), lambda i: (0, i, 0))],
                          out_specs=pl.BlockSpec((tr, c), lambda i: (i, 0)),
                          compiler_params=_cparams(("parallel",)), name=name)(a)


def _adamw(w, m, v, g_mine, g_other, name):
    r, c = w.shape
    tr = _row_tile(r)
    two = g_other is not None
    c1 = 1.0 / (1.0 - ADAM_B1 ** ADAM_STEP)
    c2 = 1.0 / (1.0 - ADAM_B2 ** ADAM_STEP)

    def body(*refs):
        if two:
            w_ref, m_ref, v_ref, g1_ref, g2_ref, g_ref, d_ref, nm_ref, nv_ref = refs
            g = g1_ref[...] + g2_ref[...]
        else:
            w_ref, m_ref, v_ref, g1_ref, g_ref, d_ref, nm_ref, nv_ref = refs
            g = g1_ref[...]
        nm = ADAM_B1 * m_ref[...] + (1.0 - ADAM_B1) * g
        nv = ADAM_B2 * v_ref[...] + (1.0 - ADAM_B2) * (g * g)
        m_hat = nm * c1
        v_hat = nv * c2
        g_ref[...] = g
        d_ref[...] = -ADAM_LR * (m_hat / (jnp.sqrt(v_hat) + ADAM_EPS) + ADAM_WD * w_ref[...])
        nm_ref[...] = nm
        nv_ref[...] = nv

    blk = pl.BlockSpec((tr, c), lambda i: (i, 0))
    sh = jax.ShapeDtypeStruct((r, c), F32)
    ins = [w, m, v, g_mine] + ([g_other] if two else [])
    return pl.pallas_call(body, out_shape=(sh, sh, sh, sh), grid=(r // tr,), in_specs=[blk] * len(ins),
                          out_specs=(blk, blk, blk, blk), compiler_params=_cparams(("parallel",)),
                          name=name)(*ins)


def _pack(arrs):
    flat = jnp.concatenate([a.reshape(-1).astype(F32) for a in arrs])
    n = flat.shape[0]
    rows = -(-n // LANES)
    rows = -(-rows // 256) * 256
    return jnp.pad(flat, (0, rows * LANES - n)).reshape(rows, LANES)


def _unpack(packed, like):
    flat = packed.reshape(-1)
    out, off = [], 0
    for a in like:
        out.append(flat[off:off + a.size].reshape(a.shape))
        off += a.size
    return out


def kernel(x, norm_ffn1, ffn1_w_gate, ffn1_w_up, ffn1_w_down, norm_mix, w_in, attn_sinks, ssm_lambda_re, ssm_lambda_im, ssm_log_dt, ssm_b_re, ssm_b_im, ssm_c_re, ssm_c_im, ssm_d, ssm_glu_w, ssm_glu_b, attn_out_norm, ssm_out_norm, w_out, norm_ffn2, ffn2_w_gate, ffn2_w_up, ffn2_w_down, final_norm, loss_target, m_norm_ffn1, m_ffn1_w_gate, m_ffn1_w_up, m_ffn1_w_down, m_norm_mix, m_w_in, m_attn_sinks, m_ssm_lambda_re, m_ssm_lambda_im, m_ssm_log_dt, m_ssm_b_re, m_ssm_b_im, m_ssm_c_re, m_ssm_c_im, m_ssm_d, m_ssm_glu_w, m_ssm_glu_b, m_attn_out_norm, m_ssm_out_norm, m_w_out, m_norm_ffn2, m_ffn2_w_gate, m_ffn2_w_up, m_ffn2_w_down, m_final_norm, v_norm_ffn1, v_ffn1_w_gate, v_ffn1_w_up, v_ffn1_w_down, v_norm_mix, v_w_in, v_attn_sinks, v_ssm_lambda_re, v_ssm_lambda_im, v_ssm_log_dt, v_ssm_b_re, v_ssm_b_im, v_ssm_c_re, v_ssm_c_im, v_ssm_d, v_ssm_glu_w, v_ssm_glu_b, v_attn_out_norm, v_ssm_out_norm, v_w_out, v_norm_ffn2, v_ffn2_w_gate, v_ffn2_w_up, v_ffn2_w_down, v_final_norm):
    given = dict(locals())
    wts = {n: given[n] for n in WEIGHTS}

    shards = [wts[n][0].astype(BF16) for n in BIG]
    full = _allgather(shards, "weights_allgather")
    w = {}
    for n, f in zip(BIG, full):
        w[n] = f
    for n in ('ffn1_w_down', 'ssm_glu_w', 'w_out', 'ffn2_w_down'):
        if n in ('ssm_glu_w', 'w_out'):
            w[n] = w[n].reshape(-1, w[n].shape[-1])
    for n in SMALL:
        w[n] = wts[n][0] if wts[n].ndim > 1 else wts[n]
    w['norm_ffn1'], w['norm_mix'], w['norm_ffn2'] = wts['norm_ffn1'], wts['norm_mix'], wts['norm_ffn2']

    loss, dx, grads = _local_step(x[0], loss_target[0], w)
    loss = lax.psum(loss, ("x", "y", "c"))

    parts = []
    for n in BIG:
        g = grads[n]
        if g.ndim == 2:
            g = g.reshape(N_CHIPS, g.shape[0] // N_CHIPS, g.shape[1])
        parts.append(g.astype(BF16))
    small_grads = [grads[n].reshape(wts[n].shape) for n in SMALL]
    recv, smalls = _grad_exchange(parts, _pack(small_grads), "grad_exchange")
    chip_sums = [_sum_blocks(r, f"grad_sum_{n}") for n, r in zip(BIG, recv)]
    other = _sibling_swap(chip_sums, "grad_sibling_swap")
    small_sum = _sum_blocks(smalls, "grad_sum_small")

    out_g, out_d, out_m, out_v = {}, {}, {}, {}
    for n, mine, oth in zip(BIG, chip_sums, other):
        shp = wts[n].shape
        g, d, nm, nv = _adamw(wts[n][0], given['m_' + n][0], given['v_' + n][0], mine, oth, f"adamw_{n}")
        out_g[n], out_d[n], out_m[n], out_v[n] = (a.reshape(shp) for a in (g, d, nm, nv))
    sw = [wts[n] for n in SMALL]
    g, d, nm, nv = _adamw(_pack(sw), _pack([given['m_' + n] for n in SMALL]),
                          _pack([given['v_' + n] for n in SMALL]), small_sum, None, "adamw_small")
    for dst, packed in ((out_g, g), (out_d, d), (out_m, nm), (out_v, nv)):
        for n, a in zip(SMALL, _unpack(packed, sw)):
            dst[n] = a

    return (loss, dx[None], *[out_g[n] for n in WEIGHTS], *[out_d[n] for n in WEIGHTS],
            *[out_m[n] for n in WEIGHTS], *[out_v[n] for n in WEIGHTS])
```

```python
import functools
import math

import numpy as np
import jax
import jax.numpy as jnp
from jax import lax
from jax.experimental import pallas as pl
from jax.experimental.pallas import tpu as pltpu

F32 = jnp.float32
BF16 = jnp.bfloat16
MESH = pl.DeviceIdType.MESH

EPS = 1e-6
NEG_INF = -1e30
LAMBDA_RE_MAX = -1e-4
ATTN_HEADS = 8
KV_HEADS = 2
GQ = ATTN_HEADS // KV_HEADS
HEAD_DIM = 64
ATTN_WIDTH = 512
KV_WIDTH = 128
WINDOW = 128
QBLOCK = 128
SSM_WIDTH = 512
SSM_GROUPS = 32
SSM_CH = 16
SSM_STATE = 64
N_STRIPS = 4
STRIP_IN = SSM_WIDTH // N_STRIPS
STRIP_ST = SSM_GROUPS * SSM_STATE // N_STRIPS
SUBLANES = 8
LANES = 128
N_CHIPS = 4
N_DEV = 8

ADAM_LR = 0.001
ADAM_B1 = 0.9
ADAM_B2 = 0.999
ADAM_EPS = 1e-08
ADAM_WD = 0.01
ADAM_STEP = 10

VMEM_LIMIT = 48 * 1024 * 1024

WEIGHTS = ['norm_ffn1', 'ffn1_w_gate', 'ffn1_w_up', 'ffn1_w_down', 'norm_mix', 'w_in', 'attn_sinks',
           'ssm_lambda_re', 'ssm_lambda_im', 'ssm_log_dt', 'ssm_b_re', 'ssm_b_im', 'ssm_c_re', 'ssm_c_im',
           'ssm_d', 'ssm_glu_w', 'ssm_glu_b', 'attn_out_norm', 'ssm_out_norm', 'w_out', 'norm_ffn2',
           'ffn2_w_gate', 'ffn2_w_up', 'ffn2_w_down', 'final_norm']
BIG = ['ffn1_w_gate', 'ffn1_w_up', 'ffn1_w_down', 'w_in', 'ssm_glu_w', 'w_out',
       'ffn2_w_gate', 'ffn2_w_up', 'ffn2_w_down']
SMALL = [n for n in WEIGHTS if n not in BIG]


def _cparams(sem=None):
    return pltpu.CompilerParams(dimension_semantics=sem, vmem_limit_bytes=VMEM_LIMIT)


def _tile(n, pref):
    if n <= pref:
        return n
    for t in (pref, pref // 2, pref // 4):
        if t % LANES == 0 and n % t == 0:
            return t
    return n


def _sigmoid(x):
    return 1.0 / (1.0 + jnp.exp(-x))


def _mm(a, b, *, ta=False, tb=False, reduce_s=False, res=None, scale=1.0, out_dtype=F32, name):
    a3 = a if a.ndim == 3 else a[None]
    b3 = b if b.ndim == 3 else b[None]
    sa, sb = a3.shape[0], b3.shape[0]
    ns = max(sa, sb)
    (kk, m) = a3.shape[1:] if ta else a3.shape[1:][::-1]
    (n, kb) = b3.shape[1:] if tb else b3.shape[1:][::-1]
    assert kk == kb, (a3.shape, b3.shape)
    tm, tn, tk = _tile(m, 1024), _tile(n, 1024), _tile(kk, 2048)
    nm, nn, nk = m // tm, n // tn, kk // tk
    has_res = res is not None
    single = nk == 1 and not (reduce_s and ns > 1)

    if reduce_s:
        grid = (nm, nn, ns, nk)
        ids = lambda i, j, s, k: (s, i, j, k)
        sem = ("parallel", "parallel", "arbitrary", "arbitrary")
    else:
        grid = (ns, nm, nn, nk)
        ids = lambda s, i, j, k: (s, i, j, k)
        sem = ("parallel", "parallel", "parallel", "arbitrary")

    def a_map(*g):
        s, i, j, k = ids(*g)
        s = s if sa > 1 else 0
        return (s, k, i) if ta else (s, i, k)

    def b_map(*g):
        s, i, j, k = ids(*g)
        s = s if sb > 1 else 0
        return (s, j, k) if tb else (s, k, j)

    def o_map(*g):
        s, i, j, k = ids(*g)
        return (i, j) if reduce_s else (s, i, j)

    a_blk = (1, tk, tm) if ta else (1, tm, tk)
    b_blk = (1, tn, tk) if tb else (1, tk, tn)
    dims = (((0 if ta else 1,), (1 if tb else 0,)), ((), ()))

    def body(*refs):
        a_ref, b_ref = refs[0], refs[1]
        r_ref = refs[2] if has_res else None
        o_ref = refs[3] if has_res else refs[2]
        acc_ref = None if single else refs[-1]
        s, _, _, k = ids(*[pl.program_id(d) for d in range(4)])
        prod = lax.dot_general(a_ref[0].astype(BF16), b_ref[0].astype(BF16), dims, preferred_element_type=F32)

        def finish(out):
            if scale != 1.0:
                out = out * scale
            if has_res:
                out = r_ref[...].reshape(out.shape) + out
            o_ref[...] = out.astype(out_dtype).reshape(o_ref.shape)

        if single:
            finish(prod)
            return
        if reduce_s:
            first = jnp.logical_and(s == 0, k == 0)
            last = jnp.logical_and(s == ns - 1, k == nk - 1)
        else:
            first, last = k == 0, k == nk - 1

        @pl.when(first)
        def _():
            acc_ref[...] = prod

        @pl.when(jnp.logical_not(first))
        def _():
            acc_ref[...] += prod

        @pl.when(last)
        def _():
            finish(acc_ref[...])

    in_specs = [pl.BlockSpec(a_blk, a_map), pl.BlockSpec(b_blk, b_map)]
    args = [a3, b3]
    if reduce_s:
        out_shape = jax.ShapeDtypeStruct((m, n), out_dtype)
        o_spec = pl.BlockSpec((tm, tn), o_map)
    else:
        out_shape = jax.ShapeDtypeStruct((ns, m, n), out_dtype)
        o_spec = pl.BlockSpec((1, tm, tn), o_map)
    if has_res:
        assert res.shape == out_shape.shape
        in_specs.append(o_spec)
        args.append(res)
    return pl.pallas_call(body, out_shape=out_shape, grid=grid, in_specs=in_specs, out_specs=o_spec,
                          scratch_shapes=[] if single else [pltpu.VMEM((tm, tn), F32)],
                          compiler_params=_cparams(sem), name=name)(*args)


def _row_tile(t):
    for tr in (256, 128, 64, 32, 16, 8):
        if t % tr == 0:
            return tr
    return t


def _rms_fwd(x, g, name):
    t, w = x.shape
    tr = _row_tile(t)

    def body(x_ref, g_ref, o_ref):
        xv = x_ref[...]
        r = lax.rsqrt(jnp.mean(xv * xv, axis=-1, keepdims=True) + EPS)
        o_ref[...] = (xv * r * g_ref[...]).astype(BF16)

    return pl.pallas_call(
        body, out_shape=jax.ShapeDtypeStruct((t, w), BF16), grid=(t // tr,),
        in_specs=[pl.BlockSpec((tr, w), lambda i: (i, 0)), pl.BlockSpec((1, w), lambda i: (0, 0))],
        out_specs=pl.BlockSpec((tr, w), lambda i: (i, 0)), compiler_params=_cparams(("parallel",)),
        name=name)(x, g)


def _rms_bwd_rows(xv, gv, dhv):
    r = lax.rsqrt(jnp.mean(xv * xv, axis=-1, keepdims=True) + EPS)
    nrm = xv * r
    dn = dhv * gv
    return r * (dn - nrm * jnp.mean(dn * nrm, axis=-1, keepdims=True)), dhv * nrm


def _rms_bwd(x, g, dh, dres, name):
    t, w = x.shape
    tr = _row_tile(t)
    has_res = dres is not None

    def body(*refs):
        if has_res:
            x_ref, g_ref, dh_ref, dr_ref, dx_ref, dxb_ref, dg_ref = refs
        else:
            x_ref, g_ref, dh_ref, dx_ref, dxb_ref, dg_ref = refs
        dx, dgs = _rms_bwd_rows(x_ref[...], g_ref[...], dh_ref[...])
        if has_res:
            dx = dx + dr_ref[...]
        dx_ref[...] = dx
        dxb_ref[...] = dx.astype(BF16)

        @pl.when(pl.program_id(0) == 0)
        def _():
            dg_ref[...] = jnp.zeros_like(dg_ref)

        dg_ref[...] += jnp.sum(dgs, axis=0, keepdims=True)

    row = pl.BlockSpec((tr, w), lambda i: (i, 0))
    vec = pl.BlockSpec((1, w), lambda i: (0, 0))
    ins = [x, g, dh] + ([dres] if has_res else [])
    return pl.pallas_call(
        body, out_shape=(jax.ShapeDtypeStruct((t, w), F32), jax.ShapeDtypeStruct((t, w), BF16),
                         jax.ShapeDtypeStruct((1, w), F32)),
        grid=(t // tr,), in_specs=[row, vec, row] + ([row] if has_res else []),
        out_specs=(row, row, vec), compiler_params=_cparams(("arbitrary",)), name=name)(*ins)


FFN_ROWS = 512


def _ffn_fwd_call(x, g, wg, wu, wd, name):
    t, d = x.shape
    ns, _, f = wg.shape
    tm = _tile(t, FFN_ROWS)

    def body(x_ref, g_ref, wg_ref, wu_ref, wd_ref, xo_ref, h_ref, gate_ref, up_ref, h_sc, acc_ref):
        s = pl.program_id(1)

        @pl.when(s == 0)
        def _():
            xv = x_ref[...]
            r = lax.rsqrt(jnp.mean(xv * xv, axis=-1, keepdims=True) + EPS)
            hb = (xv * r * g_ref[...]).astype(BF16)
            h_sc[...] = hb
            h_ref[...] = hb

        hb = h_sc[...]
        gate = jnp.dot(hb, wg_ref[0], preferred_element_type=F32)
        up = jnp.dot(hb, wu_ref[0], preferred_element_type=F32)
        gate_ref[0] = gate.astype(BF16)
        up_ref[0] = up.astype(BF16)
        act = (gate * _sigmoid(gate) * up).astype(BF16)
        prod = jnp.dot(act, wd_ref[0], preferred_element_type=F32)

        @pl.when(s == 0)
        def _():
            acc_ref[...] = prod

        @pl.when(s > 0)
        def _():
            acc_ref[...] += prod

        @pl.when(s == ns - 1)
        def _():
            xo_ref[...] = x_ref[...] + 0.5 * acc_ref[...]

    row = pl.BlockSpec((tm, d), lambda i, s: (i, 0))
    vec = pl.BlockSpec((1, d), lambda i, s: (0, 0))
    wcol = pl.BlockSpec((1, d, f), lambda i, s: (s, 0, 0))
    wrow = pl.BlockSpec((1, f, d), lambda i, s: (s, 0, 0))
    hid = pl.BlockSpec((1, tm, f), lambda i, s: (s, i, 0))
    hid_sh = jax.ShapeDtypeStruct((ns, t, f), BF16)
    return pl.pallas_call(
        body, out_shape=(jax.ShapeDtypeStruct((t, d), F32), jax.ShapeDtypeStruct((t, d), BF16), hid_sh, hid_sh),
        grid=(t // tm, ns), in_specs=[row, vec, wcol, wcol, wrow], out_specs=(row, row, hid, hid),
        scratch_shapes=[pltpu.VMEM((tm, d), BF16), pltpu.VMEM((tm, d), F32)],
        compiler_params=_cparams(("parallel", "arbitrary")), name=name)(x, g, wg, wu, wd)


def _ffn_bwd_x_call(dxo, dxo_b, x, g, gate, up, wg, wu, wd, name):
    t, d = x.shape
    ns, _, f = wg.shape
    tm = _tile(t, FFN_ROWS)
    nt = (((1,), (1,)), ((), ()))

    def body(dxo_ref, dxb_ref, x_ref, g_ref, gate_ref, up_ref, wg_ref, wu_ref, wd_ref,
             dx_ref, dxob_ref, dgn_ref, dgate_ref, dup_ref, act_ref, dh_ref):
        i, s = pl.program_id(0), pl.program_id(1)
        dact = lax.dot_general(dxb_ref[...], wd_ref[0], nt, preferred_element_type=F32) * 0.5
        gv = gate_ref[0].astype(F32)
        uv = up_ref[0].astype(F32)
        sg = _sigmoid(gv)
        silu = gv * sg
        act_ref[0] = (silu * uv).astype(BF16)
        dub = (dact * silu).astype(BF16)
        dgb = (dact * uv * sg * (1.0 + gv * (1.0 - sg))).astype(BF16)
        dup_ref[0] = dub
        dgate_ref[0] = dgb
        prod = (lax.dot_general(dgb, wg_ref[0], nt, preferred_element_type=F32)
                + lax.dot_general(dub, wu_ref[0], nt, preferred_element_type=F32))

        @pl.when(s == 0)
        def _():
            dh_ref[...] = prod

        @pl.when(s > 0)
        def _():
            dh_ref[...] += prod

        @pl.when(jnp.logical_and(i == 0, s == 0))
        def _():
            dgn_ref[...] = jnp.zeros_like(dgn_ref)

        @pl.when(s == ns - 1)
        def _():
            dx, dgs = _rms_bwd_rows(x_ref[...], g_ref[...], dh_ref[...])
            dx = dx + dxo_ref[...]
            dx_ref[...] = dx
            dxob_ref[...] = dx.astype(BF16)
            dgn_ref[...] += jnp.sum(dgs, axis=0, keepdims=True)

    row = pl.BlockSpec((tm, d), lambda i, s: (i, 0))
    vec = pl.BlockSpec((1, d), lambda i, s: (0, 0))
    wcol = pl.BlockSpec((1, d, f), lambda i, s: (s, 0, 0))
    wrow = pl.BlockSpec((1, f, d), lambda i, s: (s, 0, 0))
    hid = pl.BlockSpec((1, tm, f), lambda i, s: (s, i, 0))
    hid_sh = jax.ShapeDtypeStruct((ns, t, f), BF16)
    return pl.pallas_call(
        body,
        out_shape=(jax.ShapeDtypeStruct((t, d), F32), jax.ShapeDtypeStruct((t, d), BF16),
                   jax.ShapeDtypeStruct((1, d), F32), hid_sh, hid_sh, hid_sh),
        grid=(t // tm, ns), in_specs=[row, row, row, vec, hid, hid, wcol, wcol, wrow],
        out_specs=(row, row, vec, hid, hid, hid), scratch_shapes=[pltpu.VMEM((tm, d), F32)],
        compiler_params=_cparams(("arbitrary", "arbitrary")), name=name)(dxo, dxo_b, x, g, gate, up, wg, wu, wd)


def _ffn_bwd_w_call(h, dxo_b, dgate, dup, act, name):
    t, d = h.shape
    ns, _, f = dgate.shape
    tm = _tile(t, FFN_ROWS)
    nm = t // tm
    tn = (((0,), (0,)), ((), ()))

    def body(h_ref, dxb_ref, dgate_ref, dup_ref, act_ref, dwg_ref, dwu_ref, dwd_ref, ag_ref, au_ref, ad_ref):
        i = pl.program_id(1)
        hv = h_ref[...]
        pg = lax.dot_general(hv, dgate_ref[0], tn, preferred_element_type=F32)
        pu = lax.dot_general(hv, dup_ref[0], tn, preferred_element_type=F32)
        pd = lax.dot_general(act_ref[0], dxb_ref[...], tn, preferred_element_type=F32)

        @pl.when(i == 0)
        def _():
            ag_ref[...] = pg
            au_ref[...] = pu
            ad_ref[...] = pd

        @pl.when(i > 0)
        def _():
            ag_ref[...] += pg
            au_ref[...] += pu
            ad_ref[...] += pd

        @pl.when(i == nm - 1)
        def _():
            dwg_ref[0] = ag_ref[...].astype(BF16)
            dwu_ref[0] = au_ref[...].astype(BF16)
            dwd_ref[0] = (0.5 * ad_ref[...]).astype(BF16)

    row = pl.BlockSpec((tm, d), lambda s, i: (i, 0))
    hid = pl.BlockSpec((1, tm, f), lambda s, i: (s, i, 0))
    wcol = pl.BlockSpec((1, d, f), lambda s, i: (s, 0, 0))
    wrow = pl.BlockSpec((1, f, d), lambda s, i: (s, 0, 0))
    return pl.pallas_call(
        body,
        out_shape=(jax.ShapeDtypeStruct((ns, d, f), BF16), jax.ShapeDtypeStruct((ns, d, f), BF16),
                   jax.ShapeDtypeStruct((ns, f, d), BF16)),
        grid=(ns, nm), in_specs=[row, row, hid, hid, hid], out_specs=(wcol, wcol, wrow),
        scratch_shapes=[pltpu.VMEM((d, f), F32), pltpu.VMEM((d, f), F32), pltpu.VMEM((f, d), F32)],
        compiler_params=_cparams(("parallel", "arbitrary")), name=name)(h, dxo_b, dgate, dup, act)


def _loss_head(x, g, tgt, name):
    t, w = x.shape
    tr = _row_tile(t)

    def body(x_ref, g_ref, t_ref, loss_ref, dx_ref, dxb_ref, dg_ref):
        xv = x_ref[...]
        gv = g_ref[...]
        r = lax.rsqrt(jnp.mean(xv * xv, axis=-1, keepdims=True) + EPS)
        nrm = xv * r
        err = nrm * gv - t_ref[...]
        dout = err * (1.0 / w)
        dn = dout * gv
        dx = r * (dn - nrm * jnp.mean(dn * nrm, axis=-1, keepdims=True))
        dx_ref[...] = dx
        dxb_ref[...] = dx.astype(BF16)

        @pl.when(pl.program_id(0) == 0)
        def _():
            dg_ref[...] = jnp.zeros_like(dg_ref)
            loss_ref[...] = jnp.zeros_like(loss_ref)

        dg_ref[...] += jnp.sum(dout * nrm, axis=0, keepdims=True)
        part = jnp.sum(jnp.sum(err * err, axis=-1, keepdims=True) * (0.5 / w), axis=0, keepdims=True)
        loss_ref[...] += jnp.broadcast_to(part, loss_ref.shape)

    row = pl.BlockSpec((tr, w), lambda i: (i, 0))
    vec = pl.BlockSpec((1, w), lambda i: (0, 0))
    return pl.pallas_call(
        body, out_shape=(jax.ShapeDtypeStruct((1, LANES), F32), jax.ShapeDtypeStruct((t, w), F32),
                         jax.ShapeDtypeStruct((t, w), BF16), jax.ShapeDtypeStruct((1, w), F32)),
        grid=(t // tr,), in_specs=[row, vec, row],
        out_specs=(pl.BlockSpec((1, LANES), lambda i: (0, 0)), row, row, vec),
        compiler_params=_cparams(("arbitrary",)), name=name)(x, g, tgt)


def _attn_scores(q, k3, n, t, slope_ref):
    rows = GQ * QBLOCK
    s = lax.dot_general(q, k3, (((1,), (1,)), ((), ())), preferred_element_type=F32) * (HEAD_DIM ** -0.5)
    row = lax.broadcasted_iota(jnp.int32, (rows, 3 * QBLOCK), 0) & (QBLOCK - 1)
    col = lax.broadcasted_iota(jnp.int32, (rows, 3 * QBLOCK), 1)
    rel = jnp.abs(col - QBLOCK - row)
    key_pos = n * QBLOCK - QBLOCK + col
    valid = (rel <= WINDOW) & (key_pos >= 0) & (key_pos < t)
    return jnp.where(valid, s - slope_ref[0] * rel.astype(F32), NEG_INF)


def _attn_fwd(q4, kp, vp, sink_rows, slope_rows, name):
    _, _, t, _ = q4.shape
    nb = t // QBLOCK
    rows = GQ * QBLOCK

    def body(q_ref, k_ref, v_ref, sink_ref, slope_ref, o_ref, lse_ref):
        n = pl.program_id(1)
        start = pl.multiple_of(n * QBLOCK, QBLOCK)
        q = q_ref[0].reshape(rows, HEAD_DIM).astype(BF16)
        k3 = k_ref[0, pl.ds(start, 3 * QBLOCK), :].astype(BF16)
        v3 = v_ref[0, pl.ds(start, 3 * QBLOCK), :].astype(BF16)
        s = _attn_scores(q, k3, n, t, slope_ref)
        sink = sink_ref[0]
        mx = jnp.maximum(jnp.max(s, axis=-1, keepdims=True), sink)
        p = jnp.exp(s - mx)
        den = jnp.sum(p, axis=-1, keepdims=True) + jnp.exp(sink - mx)
        o = lax.dot_general(p.astype(BF16), v3, (((1,), (0,)), ((), ())), preferred_element_type=F32)
        o_ref[0] = (o / den).reshape(GQ, QBLOCK, HEAD_DIM)
        lse_ref[0] = (mx + jnp.log(den)).reshape(GQ, QBLOCK, 1)

    qspec = pl.BlockSpec((1, GQ, QBLOCK, HEAD_DIM), lambda h, n: (h, 0, n, 0))
    kvspec = pl.BlockSpec((1, t + 2 * QBLOCK, HEAD_DIM), lambda h, n: (h, 0, 0))
    rowspec = pl.BlockSpec((1, rows, 1), lambda h, n: (h, 0, 0))
    return pl.pallas_call(
        body, out_shape=(jax.ShapeDtypeStruct(q4.shape, F32), jax.ShapeDtypeStruct((KV_HEADS, GQ, t, 1), F32)),
        grid=(KV_HEADS, nb), in_specs=[qspec, kvspec, kvspec, rowspec, rowspec],
        out_specs=(qspec, pl.BlockSpec((1, GQ, QBLOCK, 1), lambda h, n: (h, 0, n, 0))),
        compiler_params=_cparams(("parallel", "parallel")), name=name)(q4, kp, vp, sink_rows, slope_rows)


def _attn_bwd(q4, kp, vp, sink_rows, slope_rows, o4, lse4, do4, name):
    _, _, t, _ = q4.shape
    nb = t // QBLOCK
    rows = GQ * QBLOCK
    scale = HEAD_DIM ** -0.5

    def body(q_ref, k_ref, v_ref, sink_ref, slope_ref, o_ref, lse_ref, do_ref, dq_ref, dk_ref, dv_ref, ds_ref):
        n = pl.program_id(1)
        start = pl.multiple_of(n * QBLOCK, QBLOCK)

        @pl.when(n == 0)
        def _():
            dk_ref[...] = jnp.zeros_like(dk_ref)
            dv_ref[...] = jnp.zeros_like(dv_ref)
            ds_ref[...] = jnp.zeros_like(ds_ref)

        q = q_ref[0].reshape(rows, HEAD_DIM).astype(BF16)
        k3 = k_ref[0, pl.ds(start, 3 * QBLOCK), :].astype(BF16)
        v3 = v_ref[0, pl.ds(start, 3 * QBLOCK), :].astype(BF16)
        do = do_ref[0].reshape(rows, HEAD_DIM)
        lse = lse_ref[0].reshape(rows, 1)
        s = _attn_scores(q, k3, n, t, slope_ref)
        p = jnp.exp(s - lse)
        delta = jnp.sum(do * o_ref[0].reshape(rows, HEAD_DIM), axis=-1, keepdims=True)
        dob = do.astype(BF16)
        dp = lax.dot_general(dob, v3, (((1,), (1,)), ((), ())), preferred_element_type=F32)
        dsb = (p * (dp - delta)).astype(BF16)
        dq = lax.dot_general(dsb, k3, (((1,), (0,)), ((), ())), preferred_element_type=F32) * scale
        dq_ref[0] = dq.reshape(GQ, QBLOCK, HEAD_DIM)
        dk3 = lax.dot_general(dsb, q, (((0,), (0,)), ((), ())), preferred_element_type=F32) * scale
        dv3 = lax.dot_general(p.astype(BF16), dob, (((0,), (0,)), ((), ())), preferred_element_type=F32)
        dk_ref[0, pl.ds(start, 3 * QBLOCK), :] += dk3
        dv_ref[0, pl.ds(start, 3 * QBLOCK), :] += dv3
        dsink_rows = -jnp.exp(sink_ref[0] - lse) * delta
        ds_ref[0] += jnp.sum(dsink_rows.reshape(GQ, QBLOCK, 1), axis=1)

    qspec = pl.BlockSpec((1, GQ, QBLOCK, HEAD_DIM), lambda h, n: (h, 0, n, 0))
    kvspec = pl.BlockSpec((1, t + 2 * QBLOCK, HEAD_DIM), lambda h, n: (h, 0, 0))
    rowspec = pl.BlockSpec((1, rows, 1), lambda h, n: (h, 0, 0))
    lsespec = pl.BlockSpec((1, GQ, QBLOCK, 1), lambda h, n: (h, 0, n, 0))
    return pl.pallas_call(
        body,
        out_shape=(jax.ShapeDtypeStruct(q4.shape, F32), jax.ShapeDtypeStruct(kp.shape, F32),
                   jax.ShapeDtypeStruct(vp.shape, F32), jax.ShapeDtypeStruct((KV_HEADS, GQ, 1), F32)),
        grid=(KV_HEADS, nb),
        in_specs=[qspec, kvspec, kvspec, rowspec, rowspec, qspec, lsespec, qspec],
        out_specs=(qspec, kvspec, kvspec, pl.BlockSpec((1, GQ, 1), lambda h, n: (h, 0, 0))),
        compiler_params=_cparams(("parallel", "arbitrary")), name=name)(
            q4, kp, vp, sink_rows, slope_rows, o4, lse4, do4)


def _scan_tables(a_re, a_im, reverse):
    pw = [(a_re, a_im)]
    for _ in range(SUBLANES - 1):
        pr, pi = pw[-1]
        pw.append((pr * a_re - pi * a_im, pr * a_im + pi * a_re))
    rows = np.arange(SUBLANES)
    tabs = []
    for d in (1, 2, 4):
        mask = (rows <= SUBLANES - 1 - d) if reverse else (rows >= d)
        m = jnp.asarray(mask, F32)[:, None]
        tabs += [m * pw[d - 1][0][None, :], m * pw[d - 1][1][None, :]]
    order = (SUBLANES - 1 - rows) if reverse else rows
    tabs += [jnp.stack([pw[j][0] for j in order]), jnp.stack([pw[j][1] for j in order])]
    tab = jnp.stack(tabs)
    return tab.reshape(8, SUBLANES, N_STRIPS, STRIP_ST).transpose(2, 0, 1, 3)


def _scan(v, mi_re, mi_im, tab, mo_re, mo_im, reverse, name):
    t = v.shape[0]
    tc = _tile(t, 256)
    nc = t // tc
    nblk = tc // SUBLANES

    def body(v_ref, mir_ref, mii_ref, tab_ref, mor_ref, moi_ref, y_ref, xr_ref, xi_ref, carry_ref):
        @pl.when(pl.program_id(1) == 0)
        def _():
            carry_ref[...] = jnp.zeros_like(carry_ref)

        vb = v_ref[...].astype(BF16)
        xr_ref[...] = jnp.dot(vb, mir_ref[0], preferred_element_type=F32)
        xi_ref[...] = jnp.dot(vb, mii_ref[0], preferred_element_type=F32)

        def blk(i, carry):
            cr, ci = carry
            b = (nblk - 1 - i) if reverse else i
            r0 = pl.multiple_of(b * SUBLANES, SUBLANES)
            xr = xr_ref[pl.ds(r0, SUBLANES), :]
            xi = xi_ref[pl.ds(r0, SUBLANES), :]
            for j, d in enumerate((1, 2, 4)):
                tr_, ti_ = tab_ref[0, 2 * j], tab_ref[0, 2 * j + 1]
                sh = (SUBLANES - d) if reverse else d
                sr = pltpu.roll(xr, sh, 0)
                si = pltpu.roll(xi, sh, 0)
                xr, xi = xr + tr_ * sr - ti_ * si, xi + tr_ * si + ti_ * sr
            pr, pi = tab_ref[0, 6], tab_ref[0, 7]
            xr, xi = xr + pr * cr - pi * ci, xi + pr * ci + pi * cr
            xr_ref[pl.ds(r0, SUBLANES), :] = xr
            xi_ref[pl.ds(r0, SUBLANES), :] = xi
            edge = 0 if reverse else SUBLANES - 1
            return (jnp.broadcast_to(xr[edge:edge + 1, :], xr.shape),
                    jnp.broadcast_to(xi[edge:edge + 1, :], xi.shape))

        cr, ci = lax.fori_loop(0, nblk, blk, (carry_ref[0], carry_ref[1]))
        carry_ref[0] = cr
        carry_ref[1] = ci
        y_ref[...] = (jnp.dot(xr_ref[...].astype(BF16), mor_ref[0], preferred_element_type=F32)
                      + jnp.dot(xi_ref[...].astype(BF16), moi_ref[0], preferred_element_type=F32))

    tmap = (lambda s, c: (nc - 1 - c, s)) if reverse else (lambda s, c: (c, s))
    smap3 = lambda s, c: (s, 0, 0)
    return pl.pallas_call(
        body,
        out_shape=(jax.ShapeDtypeStruct((t, SSM_WIDTH), F32),
                   jax.ShapeDtypeStruct((t, N_STRIPS * STRIP_ST), F32),
                   jax.ShapeDtypeStruct((t, N_STRIPS * STRIP_ST), F32)),
        grid=(N_STRIPS, nc),
        in_specs=[pl.BlockSpec((tc, STRIP_IN), tmap),
                  pl.BlockSpec((1, STRIP_IN, STRIP_ST), smap3), pl.BlockSpec((1, STRIP_IN, STRIP_ST), smap3),
                  pl.BlockSpec((1, 8, SUBLANES, STRIP_ST), lambda s, c: (s, 0, 0, 0)),
                  pl.BlockSpec((1, STRIP_ST, STRIP_IN), smap3), pl.BlockSpec((1, STRIP_ST, STRIP_IN), smap3)],
        out_specs=(pl.BlockSpec((tc, STRIP_IN), tmap), pl.BlockSpec((tc, STRIP_ST), tmap),
                   pl.BlockSpec((tc, STRIP_ST), tmap)),
        scratch_shapes=[pltpu.VMEM((2, SUBLANES, STRIP_ST), F32)],
        compiler_params=_cparams(("parallel", "arbitrary")), name=name)(v, mi_re, mi_im, tab, mo_re, mo_im)


def _scan_param_grads(v, dy, xr, xi, lr, li, reverse, name):
    t = v.shape[0]
    tc = _tile(t, 256)
    nc = t // tc
    hb = tc // SUBLANES

    def body(v_ref, dy_ref, xr_ref, xi_ref, lr_ref, li_ref, hr_ref, hi_ref,
             dmir_ref, dmii_ref, dmor_ref, dmoi_ref, da_ref):
        c = pl.program_id(1)

        @pl.when(c == 0)
        def _():
            for r in (dmir_ref, dmii_ref, dmor_ref, dmoi_ref, da_ref):
                r[...] = jnp.zeros_like(r)

        xrv, xiv, lrv, liv = xr_ref[...], xi_ref[...], lr_ref[...], li_ref[...]
        row = lax.broadcasted_iota(jnp.int32, xrv.shape, 0)
        if reverse:
            live = (c < nc - 1).astype(F32)
            edge_r, edge_i = hr_ref[0:1, :] * live, hi_ref[0:1, :] * live
            xpr = jnp.where(row == tc - 1, edge_r, pltpu.roll(xrv, tc - 1, 0))
            xpi = jnp.where(row == tc - 1, edge_i, pltpu.roll(xiv, tc - 1, 0))
        else:
            live = (c > 0).astype(F32)
            edge_r, edge_i = hr_ref[SUBLANES - 1:SUBLANES, :] * live, hi_ref[SUBLANES - 1:SUBLANES, :] * live
            xpr = jnp.where(row == 0, edge_r, pltpu.roll(xrv, 1, 0))
            xpi = jnp.where(row == 0, edge_i, pltpu.roll(xiv, 1, 0))
        da_ref[0, 0:1, :] += jnp.sum(xpr * lrv + xpi * liv, axis=0, keepdims=True)
        da_ref[0, 1:2, :] += jnp.sum(xpr * liv - xpi * lrv, axis=0, keepdims=True)
        tdims = (((0,), (0,)), ((), ()))
        vb, dyb = v_ref[...].astype(BF16), dy_ref[...].astype(BF16)
        dmir_ref[0] += lax.dot_general(vb, lrv.astype(BF16), tdims, preferred_element_type=F32)
        dmii_ref[0] += lax.dot_general(vb, liv.astype(BF16), tdims, preferred_element_type=F32)
        dmor_ref[0] += lax.dot_general(xrv.astype(BF16), dyb, tdims, preferred_element_type=F32)
        dmoi_ref[0] += lax.dot_general(xiv.astype(BF16), dyb, tdims, preferred_element_type=F32)

    tmap = lambda s, c: (c, s)
    if reverse:
        hmap = lambda s, c: (jnp.minimum((c + 1) * hb, t // SUBLANES - 1), s)
    else:
        hmap = lambda s, c: (jnp.maximum(c * hb - 1, 0), s)
    narrow = pl.BlockSpec((tc, STRIP_IN), tmap)
    wide = pl.BlockSpec((tc, STRIP_ST), tmap)
    halo = pl.BlockSpec((SUBLANES, STRIP_ST), hmap)
    smap3 = lambda s, c: (s, 0, 0)
    return pl.pallas_call(
        body,
        out_shape=(jax.ShapeDtypeStruct((N_STRIPS, STRIP_IN, STRIP_ST), F32),
                   jax.ShapeDtypeStruct((N_STRIPS, STRIP_IN, STRIP_ST), F32),
                   jax.ShapeDtypeStruct((N_STRIPS, STRIP_ST, STRIP_IN), F32),
                   jax.ShapeDtypeStruct((N_STRIPS, STRIP_ST, STRIP_IN), F32),
                   jax.ShapeDtypeStruct((N_STRIPS, SUBLANES, STRIP_ST), F32)),
        grid=(N_STRIPS, nc),
        in_specs=[narrow, narrow, wide, wide, wide, wide, halo, halo],
        out_specs=(pl.BlockSpec((1, STRIP_IN, STRIP_ST), smap3), pl.BlockSpec((1, STRIP_IN, STRIP_ST), smap3),
                   pl.BlockSpec((1, STRIP_ST, STRIP_IN), smap3), pl.BlockSpec((1, STRIP_ST, STRIP_IN), smap3),
                   pl.BlockSpec((1, SUBLANES, STRIP_ST), smap3)),
        compiler_params=_cparams(("parallel", "arbitrary")), name=name)(v, dy, xr, xi, lr, li, xr, xi)


def _ssm_prep(lam_re, lam_im, log_dt, b_re, b_im, c_re, c_im):
    lr = jnp.minimum(lam_re, LAMBDA_RE_MAX)
    li = lam_im
    dt = jnp.exp(log_dt)[:, None]
    mag = jnp.exp(lr * dt)
    a_re = mag * jnp.cos(li * dt)
    a_im = mag * jnp.sin(li * dt)
    den = lr * lr + li * li
    coef_re = ((a_re - 1.0) * lr + a_im * li) / den
    coef_im = (a_im * lr - (a_re - 1.0) * li) / den
    bb_re = coef_re[..., None] * b_re - coef_im[..., None] * b_im
    bb_im = coef_re[..., None] * b_im + coef_im[..., None] * b_re
    eye = jnp.eye(SSM_GROUPS // N_STRIPS, dtype=F32)

    def strips(m):
        g, a, b = m.shape
        m4 = m.reshape(N_STRIPS, g // N_STRIPS, a, b)
        return jnp.einsum('sgab,gk->sgakb', m4, eye).reshape(N_STRIPS, g // N_STRIPS * a, g // N_STRIPS * b)

    mi_re = strips(jnp.swapaxes(bb_re, 1, 2))
    mi_im = strips(jnp.swapaxes(bb_im, 1, 2))
    mo_re = strips(jnp.swapaxes(c_re, 1, 2))
    mo_im = strips(-jnp.swapaxes(c_im, 1, 2))
    return a_re.reshape(-1), a_im.reshape(-1), mi_re, mi_im, mo_re, mo_im


def _gelu(x):
    c = math.sqrt(2.0 / math.pi)
    return 0.5 * x * (1.0 + jnp.tanh(c * (x + 0.044715 * x * x * x)))


def _gelu_grad(x):
    c = math.sqrt(2.0 / math.pi)
    th = jnp.tanh(c * (x + 0.044715 * x * x * x))
    return 0.5 * (1.0 + th) + 0.5 * x * (1.0 - th * th) * c * (1.0 + 3.0 * 0.044715 * x * x)


def _ssm_post_fwd(u, yf, yb, d, wglu, bglu, name):
    t, w = u.shape
    tr = _row_tile(t)

    def body(u_ref, yf_ref, yb_ref, d_ref, w_ref, b_ref, s_ref, y0_ref, z_ref):
        y0 = d_ref[...] * u_ref[...] + yf_ref[...] + yb_ref[...]
        yg = _gelu(y0)
        z = jnp.dot(yg.astype(BF16), w_ref[...], preferred_element_type=F32) + b_ref[...]
        s_ref[...] = yg * _sigmoid(z)
        y0_ref[...] = y0
        z_ref[...] = z

    row = pl.BlockSpec((tr, w), lambda i: (i, 0))
    vec = pl.BlockSpec((1, w), lambda i: (0, 0))
    mat = pl.BlockSpec((w, w), lambda i: (0, 0))
    sh = jax.ShapeDtypeStruct((t, w), F32)
    return pl.pallas_call(body, out_shape=(sh, sh, sh), grid=(t // tr,),
                          in_specs=[row, row, row, vec, mat, vec], out_specs=(row, row, row),
                          compiler_params=_cparams(("parallel",)), name=name)(u, yf, yb, d, wglu, bglu)


def _ssm_post_bwd(ds, y0, z, u, d, wglu, name):
    t, w = u.shape
    tr = _row_tile(t)

    def body(ds_ref, y0_ref, z_ref, u_ref, d_ref, w_ref, dy0_ref, dw_ref, db_ref, dd_ref):
        @pl.when(pl.program_id(0) == 0)
        def _():
            dw_ref[...] = jnp.zeros_like(dw_ref)
            db_ref[...] = jnp.zeros_like(db_ref)
            dd_ref[...] = jnp.zeros_like(dd_ref)

        y0 = y0_ref[...]
        yg = _gelu(y0)
        sg = _sigmoid(z_ref[...])
        dsv = ds_ref[...]
        dz = dsv * yg * sg * (1.0 - sg)
        dzb = dz.astype(BF16)
        dyg = dsv * sg + lax.dot_general(dzb, w_ref[...], (((1,), (1,)), ((), ())), preferred_element_type=F32)
        dy0 = dyg * _gelu_grad(y0)
        dy0_ref[...] = dy0
        dw_ref[...] += lax.dot_general(yg.astype(BF16), dzb, (((0,), (0,)), ((), ())), preferred_element_type=F32)
        db_ref[...] += jnp.sum(dz, axis=0, keepdims=True)
        dd_ref[...] += jnp.sum(dy0 * u_ref[...], axis=0, keepdims=True)

    row = pl.BlockSpec((tr, w), lambda i: (i, 0))
    vec = pl.BlockSpec((1, w), lambda i: (0, 0))
    mat = pl.BlockSpec((w, w), lambda i: (0, 0))
    return pl.pallas_call(
        body, out_shape=(jax.ShapeDtypeStruct((t, w), F32), jax.ShapeDtypeStruct((w, w), F32),
                         jax.ShapeDtypeStruct((1, w), F32), jax.ShapeDtypeStruct((1, w), F32)),
        grid=(t // tr,), in_specs=[row, row, row, row, vec, mat], out_specs=(row, mat, vec, vec),
        compiler_params=_cparams(("arbitrary",)), name=name)(ds, y0, z, u, d, wglu)


def _du_combine(dy0, d, du_f, du_b, name):
    t, w = dy0.shape
    tr = _row_tile(t)

    def body(dy_ref, d_ref, a_ref, b_ref, o_ref):
        o_ref[...] = d_ref[...] * dy_ref[...] + a_ref[...] + b_ref[...]

    row = pl.BlockSpec((tr, w), lambda i: (i, 0))
    vec = pl.BlockSpec((1, w), lambda i: (0, 0))
    return pl.pallas_call(body, out_shape=jax.ShapeDtypeStruct((t, w), F32), grid=(t // tr,),
                          in_specs=[row, vec, row, row], out_specs=row, compiler_params=_cparams(("parallel",)),
                          name=name)(dy0, d, du_f, du_b)


def _ffn_fwd(x, g, wg, wu, wd, tag):
    xo, h, gate, up = _ffn_fwd_call(x, g, wg, wu, wd, f"{tag}_fwd")
    return xo, (h, gate, up)


def _ffn_bwd(dxo, dxo_b, x, g, wg, wu, wd, saved, tag):
    h, gate, up = saved
    dx, dx_b, dg, dgate, dup, act = _ffn_bwd_x_call(dxo, dxo_b, x, g, gate, up, wg, wu, wd, f"{tag}_bwd_x")
    dwg, dwu, dwd = _ffn_bwd_w_call(h, dxo_b, dgate, dup, act, f"{tag}_bwd_w")
    return dx, dx_b, dg, dwg, dwu, dwd


def _heads_split(q, k, v):
    t = q.shape[0]
    q4 = q.reshape(t, KV_HEADS, GQ, HEAD_DIM).transpose(1, 2, 0, 3)
    pad = lambda a: jnp.pad(a.reshape(t, KV_HEADS, HEAD_DIM).transpose(1, 0, 2), ((0, 0), (QBLOCK, QBLOCK), (0, 0)))
    return q4, pad(k), pad(v)


def _local_step(x, tgt, w):
    t = x.shape[0]
    row = lambda a: a.reshape(1, -1)
    grads = {}

    x1, ffn1_saved = _ffn_fwd(x, w['norm_ffn1'], w['ffn1_w_gate'], w['ffn1_w_up'], w['ffn1_w_down'], "ffn1")

    h2 = _rms_fwd(x1, w['norm_mix'], "mix_norm")
    proj = _mm(h2, w['w_in'], name="in_proj")
    proj = proj.transpose(1, 0, 2).reshape(t, -1)
    q = proj[:, :ATTN_WIDTH]
    k = proj[:, ATTN_WIDTH:ATTN_WIDTH + KV_WIDTH]
    v = proj[:, ATTN_WIDTH + KV_WIDTH:ATTN_WIDTH + 2 * KV_WIDTH]
    u = proj[:, ATTN_WIDTH + 2 * KV_WIDTH:]

    q4, kp, vp = _heads_split(q, k, v)
    sink_rows = jnp.repeat(w['attn_sinks'].reshape(KV_HEADS, GQ), QBLOCK, axis=1)[..., None]
    slopes = jnp.asarray(2.0 ** (-8.0 * (np.arange(ATTN_HEADS) + 1) / ATTN_HEADS), F32)
    slope_rows = jnp.repeat(slopes.reshape(KV_HEADS, GQ), QBLOCK, axis=1)[..., None]
    o4, lse4 = _attn_fwd(q4, kp, vp, sink_rows, slope_rows, "attn_fwd")
    attn = o4.transpose(2, 0, 1, 3).reshape(t, ATTN_WIDTH)

    ssm_names = ['ssm_lambda_re', 'ssm_lambda_im', 'ssm_log_dt', 'ssm_b_re', 'ssm_b_im', 'ssm_c_re', 'ssm_c_im']
    ys, states, preps, vjps = [], [], [], []
    for direction in range(2):
        params = [w[n][direction] for n in ssm_names]
        prep, vjp = jax.vjp(_ssm_prep, *params)
        a_re, a_im = prep[0], prep[1]
        mi_re, mi_im, mo_re, mo_im = (m.astype(BF16) for m in prep[2:])
        prep = (a_re, a_im, mi_re, mi_im, mo_re, mo_im)
        rev = direction == 1
        tab = _scan_tables(a_re, a_im, rev)
        y, xr, xi = _scan(u, mi_re, mi_im, tab, mo_re, mo_im, rev, f"s5_fwd{direction}")
        ys.append(y)
        states.append((xr, xi))
        preps.append(prep)
        vjps.append(vjp)
    d_row = row(w['ssm_d'])
    s, y0, z = _ssm_post_fwd(u, ys[0], ys[1], d_row, w['ssm_glu_w'], row(w['ssm_glu_b']), "ssm_post")

    ma = _rms_fwd(attn, row(w['attn_out_norm']), "attn_out_norm")
    ms = _rms_fwd(s, row(w['ssm_out_norm']), "ssm_out_norm")
    mixed = jnp.concatenate([ma, ms], axis=-1)
    x2 = _mm(mixed, w['w_out'], res=x1, reduce_s=True, name="out_proj")

    x3, ffn2_saved = _ffn_fwd(x2, w['norm_ffn2'], w['ffn2_w_gate'], w['ffn2_w_up'], w['ffn2_w_down'], "ffn2")

    loss_row, dx3, dx3_b, dgf = _loss_head(x3, row(w['final_norm']), tgt, "loss_head")
    loss = loss_row[0, 0]
    grads['final_norm'] = dgf.reshape(w['final_norm'].shape)

    dx2, dx2_b, dg, dwg, dwu, dwd = _ffn_bwd(dx3, dx3_b, x2, w['norm_ffn2'], w['ffn2_w_gate'], w['ffn2_w_up'],
                                             w['ffn2_w_down'], ffn2_saved, "ffn2")
    grads.update(norm_ffn2=dg, ffn2_w_gate=dwg, ffn2_w_up=dwu, ffn2_w_down=dwd)

    dmixed = _mm(dx2_b, w['w_out'], tb=True, reduce_s=True, name="out_proj_dx")
    grads['w_out'] = _mm(mixed, dx2_b, ta=True, out_dtype=BF16, name="out_proj_dw").reshape(w['w_out'].shape)
    dattn, _, dga = _rms_bwd(attn, row(w['attn_out_norm']), dmixed[:, :ATTN_WIDTH], None, "attn_out_dnorm")
    ds, _, dgs = _rms_bwd(s, row(w['ssm_out_norm']), dmixed[:, ATTN_WIDTH:], None, "ssm_out_dnorm")
    grads.update(attn_out_norm=dga, ssm_out_norm=dgs)

    dy0, dwglu, dbglu, dd = _ssm_post_bwd(ds, y0, z, u, d_row, w['ssm_glu_w'], "ssm_post_bwd")
    grads['ssm_glu_w'] = dwglu.reshape(w['ssm_glu_w'].shape)
    grads['ssm_glu_b'] = dbglu
    grads['ssm_d'] = dd.reshape(w['ssm_d'].shape)
    dparams, du_dirs = [], []
    for direction in range(2):
        a_re, a_im, mi_re, mi_im, mo_re, mo_im = preps[direction]
        rev = direction == 1
        tab = _scan_tables(a_re, -a_im, not rev)
        tr3 = lambda m: jnp.swapaxes(m, 1, 2)
        du_dir, lr, li = _scan(dy0, tr3(mo_re), tr3(mo_im), tab, tr3(mi_re), tr3(mi_im), not rev,
                               f"s5_adj{direction}")
        du_dirs.append(du_dir)
        xr, xi = states[direction]
        dmir, dmii, dmor, dmoi, da = _scan_param_grads(u, dy0, xr, xi, lr, li, rev, f"s5_pgrad{direction}")
        da_re = da[:, 0, :].reshape(-1)
        da_im = da[:, 1, :].reshape(-1)
        dparams.append(vjps[direction]((da_re, da_im, dmir, dmii, dmor, dmoi)))
    du = _du_combine(dy0, d_row, du_dirs[0], du_dirs[1], "ssm_du")
    for i, n in enumerate(ssm_names):
        grads[n] = jnp.stack([dparams[0][i], dparams[1][i]])

    do4 = dattn.reshape(t, KV_HEADS, GQ, HEAD_DIM).transpose(1, 2, 0, 3)
    dq4, dkp, dvp, dsink = _attn_bwd(q4, kp, vp, sink_rows, slope_rows, o4, lse4, do4, "attn_bwd")
    grads['attn_sinks'] = dsink.reshape(w['attn_sinks'].shape)
    dq = dq4.transpose(2, 0, 1, 3).reshape(t, ATTN_WIDTH)
    unpad = lambda a: a[:, QBLOCK:QBLOCK + t, :].transpose(1, 0, 2).reshape(t, KV_WIDTH)
    dproj = jnp.concatenate([dq, unpad(dkp), unpad(dvp), du], axis=-1).astype(BF16)
    dproj = dproj.reshape(t, N_CHIPS, -1).transpose(1, 0, 2)

    grads['w_in'] = _mm(h2, dproj, ta=True, out_dtype=BF16, name="in_proj_dw")
    dh2 = _mm(dproj, w['w_in'], tb=True, reduce_s=True, name="in_proj_dx")
    dx1, dx1_b, dgm = _rms_bwd(x1, w['norm_mix'], dh2, dx2, "mix_dnorm")
    grads['norm_mix'] = dgm

    dx0, _, dg, dwg, dwu, dwd = _ffn_bwd(dx1, dx1_b, x, w['norm_ffn1'], w['ffn1_w_gate'], w['ffn1_w_up'],
                                         w['ffn1_w_down'], ffn1_saved, "ffn1")
    grads.update(norm_ffn1=dg, ffn1_w_gate=dwg, ffn1_w_up=dwu, ffn1_w_down=dwd)
    return loss, dx0, grads


HBM_SPEC = pl.BlockSpec(memory_space=pl.ANY)


def _chip_peers(x, y):
    return [(1 - x, y), (x, 1 - y), (1 - x, 1 - y)]


def _allgather(shards, name):
    nw = len(shards)

    def body(*refs):
        ins, outs = refs[:nw], refs[nw:2 * nw]
        send_sems, recv_sems, local_sems = refs[2 * nw:]
        x, y, c = lax.axis_index("x"), lax.axis_index("y"), lax.axis_index("c")
        me = 2 * x + y
        peers = _chip_peers(x, y)
        local = [pltpu.make_async_copy(ins[i], outs[i].at[me], local_sems.at[i]) for i in range(nw)]
        for cp in local:
            cp.start()
        for i in range(nw):
            for j, (px, py) in enumerate(peers):
                pltpu.make_async_remote_copy(ins[i], outs[i].at[me], send_sems.at[i, j], recv_sems.at[i, j],
                                             device_id=(px, py, c), device_id_type=MESH).start()
        for i in range(nw):
            for j, (px, py) in enumerate(peers):
                pltpu.make_async_remote_copy(ins[i], outs[i].at[2 * px + py], send_sems.at[i, j],
                                             recv_sems.at[i, j], device_id=(px, py, c),
                                             device_id_type=MESH).wait()
        for cp in local:
            cp.wait()

    return pl.pallas_call(
        body, out_shape=[jax.ShapeDtypeStruct((N_CHIPS,) + s.shape, s.dtype) for s in shards],
        in_specs=[HBM_SPEC] * nw, out_specs=[HBM_SPEC] * nw,
        scratch_shapes=[pltpu.SemaphoreType.DMA((nw, 3)), pltpu.SemaphoreType.DMA((nw, 3)),
                        pltpu.SemaphoreType.DMA((nw,))],
        name=name)(*shards)


def _grad_exchange(parts, smalls, name):
    nw, nsm = len(parts), len(smalls)
    rels = [(fx, fy, fc) for fx in (0, 1) for fy in (0, 1) for fc in (0, 1)][1:]

    def body(*refs):
        ins, sins = refs[:nw], refs[nw:nw + nsm]
        outs, souts = refs[nw + nsm:2 * nw + nsm], refs[2 * nw + nsm:2 * (nw + nsm)]
        send_sems, recv_sems, local_sems, ssend, srecv, slocal = refs[2 * (nw + nsm):]
        x, y, c = lax.axis_index("x"), lax.axis_index("y"), lax.axis_index("c")
        me = 2 * x + y
        lin = 4 * x + 2 * y + c
        peers = _chip_peers(x, y)
        local = [pltpu.make_async_copy(ins[i].at[me], outs[i].at[me], local_sems.at[i]) for i in range(nw)]
        local += [pltpu.make_async_copy(sins[i], souts[i].at[lin], slocal.at[i]) for i in range(nsm)]
        for cp in local:
            cp.start()
        for i in range(nw):
            for j, (px, py) in enumerate(peers):
                pltpu.make_async_remote_copy(ins[i].at[2 * px + py], outs[i].at[me], send_sems.at[i, j],
                                             recv_sems.at[i, j], device_id=(px, py, c),
                                             device_id_type=MESH).start()
        for i in range(nsm):
            for j, (fx, fy, fc) in enumerate(rels):
                pltpu.make_async_remote_copy(sins[i], souts[i].at[lin], ssend.at[i, j], srecv.at[i, j],
                                             device_id=(x ^ fx, y ^ fy, c ^ fc), device_id_type=MESH).start()
        for i in range(nw):
            for j, (px, py) in enumerate(peers):
                pltpu.make_async_remote_copy(ins[i].at[2 * px + py], outs[i].at[2 * px + py], send_sems.at[i, j],
                                             recv_sems.at[i, j], device_id=(px, py, c),
                                             device_id_type=MESH).wait()
        for i in range(nsm):
            for j, (fx, fy, fc) in enumerate(rels):
                src = 4 * (x ^ fx) + 2 * (y ^ fy) + (c ^ fc)
                pltpu.make_async_remote_copy(sins[i], souts[i].at[src], ssend.at[i, j], srecv.at[i, j],
                                             device_id=(x ^ fx, y ^ fy, c ^ fc), device_id_type=MESH).wait()
        for cp in local:
            cp.wait()

    out_shape = [jax.ShapeDtypeStruct(p.shape, p.dtype) for p in parts]
    out_shape += [jax.ShapeDtypeStruct((N_DEV,) + s.shape, s.dtype) for s in smalls]
    res = pl.pallas_call(
        body, out_shape=out_shape, in_specs=[HBM_SPEC] * (nw + nsm), out_specs=[HBM_SPEC] * (nw + nsm),
        scratch_shapes=[pltpu.SemaphoreType.DMA((nw, 3)), pltpu.SemaphoreType.DMA((nw, 3)),
                        pltpu.SemaphoreType.DMA((nw,)), pltpu.SemaphoreType.DMA((nsm, 7)),
                        pltpu.SemaphoreType.DMA((nsm, 7)), pltpu.SemaphoreType.DMA((nsm,))],
        name=name)(*parts, *smalls)
    return res[:nw], res[nw:]


def _sibling_swap(arrs, name):
    nw = len(arrs)

    def body(*refs):
        ins, outs = refs[:nw], refs[nw:2 * nw]
        send_sems, recv_sems = refs[2 * nw:]
        x, y, c = lax.axis_index("x"), lax.axis_index("y"), lax.axis_index("c")
        cps = [pltpu.make_async_remote_copy(ins[i], outs[i], send_sems.at[i], recv_sems.at[i],
                                            device_id=(x, y, 1 - c), device_id_type=MESH) for i in range(nw)]
        for cp in cps:
            cp.start()
        for cp in cps:
            cp.wait()

    return pl.pallas_call(
        body, out_shape=[jax.ShapeDtypeStruct(a.shape, a.dtype) for a in arrs],
        in_specs=[HBM_SPEC] * nw, out_specs=[HBM_SPEC] * nw,
        scratch_shapes=[pltpu.SemaphoreType.DMA((nw,)), pltpu.SemaphoreType.DMA((nw,))],
        name=name)(*arrs)


def _sum_blocks(a, name):
    s, r, c = a.shape
    tr = _row_tile(r)

    def body(a_ref, o_ref):
        acc = a_ref[0].astype(F32)
        for k in range(1, s):
            acc = acc + a_ref[k].astype(F32)
        o_ref[...] = acc

    return pl.pallas_call(body, out_shape=jax.ShapeDtypeStruct((r, c), F32), grid=(r // tr,),
                          in_specs=[pl.BlockSpec((s, tr, c), lambda i: (0, i, 0))],
                          out_specs=pl.BlockSpec((tr, c), lambda i: (i, 0)),
                          compiler_params=_cparams(("parallel",)), name=name)(a)


def _adamw_math(w, m, v, g):
    nm = ADAM_B1 * m + (1.0 - ADAM_B1) * g
    nv = ADAM_B2 * v + (1.0 - ADAM_B2) * (g * g)
    m_hat = nm * (1.0 / (1.0 - ADAM_B1 ** ADAM_STEP))
    v_hat = nv * (1.0 / (1.0 - ADAM_B2 ** ADAM_STEP))
    return -ADAM_LR * (m_hat / (jnp.sqrt(v_hat) + ADAM_EPS) + ADAM_WD * w), nm, nv


def _adamw(w, m, v, g_mine, g_other, name):
    r, c = w.shape
    tr = _row_tile(r)

    def body(w_ref, m_ref, v_ref, g1_ref, g2_ref, g_ref, d_ref, nm_ref, nv_ref):
        g = g1_ref[...] + g2_ref[...]
        g_ref[...] = g
        d_ref[...], nm_ref[...], nv_ref[...] = _adamw_math(w_ref[...], m_ref[...], v_ref[...], g)

    blk = pl.BlockSpec((tr, c), lambda i: (i, 0))
    sh = jax.ShapeDtypeStruct((r, c), F32)
    return pl.pallas_call(body, out_shape=(sh, sh, sh, sh), grid=(r // tr,), in_specs=[blk] * 5,
                          out_specs=(blk, blk, blk, blk), compiler_params=_cparams(("parallel",)),
                          name=name)(w, m, v, g_mine, g_other)


def _adamw_small(ws, ms, vs, alls, split, name):
    n = len(ws)
    lead = split if split is not None else ()
    nl = len(lead)

    def blocks(shape):
        if split is None:
            return tuple(shape), (lambda *g: (0,) * len(shape))
        blk = (shape[0], shape[1] // lead[0], shape[2] // lead[1]) + tuple(shape[3:])
        return blk, (lambda *g: (0, g[0], g[1]) + (0,) * (len(shape) - 3))

    def body(*refs):
        w_refs, m_refs, v_refs, a_refs = (refs[k * n:(k + 1) * n] for k in range(4))
        g_refs, d_refs, nm_refs, nv_refs = (refs[(4 + k) * n:(5 + k) * n] for k in range(4))
        k = pl.program_id(nl)
        for i in range(n):
            @pl.when(k == 0)
            def _(i=i):
                g_refs[i][...] = a_refs[i][0]

            @pl.when(k > 0)
            def _(i=i):
                g_refs[i][...] += a_refs[i][0]

            @pl.when(k == N_DEV - 1)
            def _(i=i):
                d_refs[i][...], nm_refs[i][...], nv_refs[i][...] = _adamw_math(
                    w_refs[i][...], m_refs[i][...], v_refs[i][...], g_refs[i][...])

    specs, aspecs, shapes = [], [], []
    for wa in ws:
        blk, imap = blocks(wa.shape)
        specs.append(pl.BlockSpec(blk, imap))
        aspecs.append(pl.BlockSpec((1,) + blk, (lambda *g, imap=imap: (g[nl],) + imap(*g))))
        shapes.append(jax.ShapeDtypeStruct(wa.shape, F32))
    res = pl.pallas_call(
        body, out_shape=shapes * 4, grid=tuple(lead) + (N_DEV,), in_specs=specs * 3 + aspecs,
        out_specs=specs * 4, compiler_params=_cparams(("parallel",) * nl + ("arbitrary",)),
        name=name)(*ws, *ms, *vs, *alls)
    return res[:n], res[n:2 * n], res[2 * n:3 * n], res[3 * n:]


def kernel(x, norm_ffn1, ffn1_w_gate, ffn1_w_up, ffn1_w_down, norm_mix, w_in, attn_sinks, ssm_lambda_re, ssm_lambda_im, ssm_log_dt, ssm_b_re, ssm_b_im, ssm_c_re, ssm_c_im, ssm_d, ssm_glu_w, ssm_glu_b, attn_out_norm, ssm_out_norm, w_out, norm_ffn2, ffn2_w_gate, ffn2_w_up, ffn2_w_down, final_norm, loss_target, m_norm_ffn1, m_ffn1_w_gate, m_ffn1_w_up, m_ffn1_w_down, m_norm_mix, m_w_in, m_attn_sinks, m_ssm_lambda_re, m_ssm_lambda_im, m_ssm_log_dt, m_ssm_b_re, m_ssm_b_im, m_ssm_c_re, m_ssm_c_im, m_ssm_d, m_ssm_glu_w, m_ssm_glu_b, m_attn_out_norm, m_ssm_out_norm, m_w_out, m_norm_ffn2, m_ffn2_w_gate, m_ffn2_w_up, m_ffn2_w_down, m_final_norm, v_norm_ffn1, v_ffn1_w_gate, v_ffn1_w_up, v_ffn1_w_down, v_norm_mix, v_w_in, v_attn_sinks, v_ssm_lambda_re, v_ssm_lambda_im, v_ssm_log_dt, v_ssm_b_re, v_ssm_b_im, v_ssm_c_re, v_ssm_c_im, v_ssm_d, v_ssm_glu_w, v_ssm_glu_b, v_attn_out_norm, v_ssm_out_norm, v_w_out, v_norm_ffn2, v_ffn2_w_gate, v_ffn2_w_up, v_ffn2_w_down, v_final_norm):
    given = dict(locals())
    wts = {n: given[n] for n in WEIGHTS}

    shards = [wts[n][0].astype(BF16) for n in BIG]
    full = _allgather(shards, "weights_allgather")
    w = {}
    for n, f in zip(BIG, full):
        w[n] = f
    for n in ('ffn1_w_down', 'ssm_glu_w', 'w_out', 'ffn2_w_down'):
        if n in ('ssm_glu_w', 'w_out'):
            w[n] = w[n].reshape(-1, w[n].shape[-1])
    for n in SMALL:
        w[n] = wts[n][0] if wts[n].ndim > 1 else wts[n]
    w['norm_ffn1'], w['norm_mix'], w['norm_ffn2'] = wts['norm_ffn1'], wts['norm_mix'], wts['norm_ffn2']

    loss, dx, grads = _local_step(x[0], loss_target[0], w)
    loss = lax.psum(loss, ("x", "y", "c"))

    parts = []
    for n in BIG:
        g = grads[n]
        if g.ndim == 2:
            g = g.reshape(N_CHIPS, g.shape[0] // N_CHIPS, g.shape[1])
        parts.append(g.astype(BF16))
    nat = {n: (1, wts[n].shape[0]) if wts[n].ndim == 1 else wts[n].shape for n in SMALL}
    small_grads = [grads[n].reshape(nat[n]) for n in SMALL]
    recv, alls = _grad_exchange(parts, small_grads, "grad_exchange")
    chip_sums = [_sum_blocks(r, f"grad_sum_{n}") for n, r in zip(BIG, recv)]
    other = _sibling_swap(chip_sums, "grad_sibling_swap")

    out_g, out_d, out_m, out_v = {}, {}, {}, {}
    for n, mine, oth in zip(BIG, chip_sums, other):
        shp = wts[n].shape
        g, d, nm, nv = _adamw(wts[n][0], given['m_' + n][0], given['v_' + n][0], mine, oth, f"adamw_{n}")
        out_g[n], out_d[n], out_m[n], out_v[n] = (a.reshape(shp) for a in (g, d, nm, nv))
    alls = dict(zip(SMALL, alls))
    wide = ['ssm_b_re', 'ssm_b_im', 'ssm_c_re', 'ssm_c_im']
    for group, split, tag in (([n for n in SMALL if n not in wide], None, "adamw_small"),
                              (wide, (2, 4), "adamw_ssm_bc")):
        res = _adamw_small([wts[n].reshape(nat[n]) for n in group], [given['m_' + n].reshape(nat[n]) for n in group],
                           [given['v_' + n].reshape(nat[n]) for n in group], [alls[n] for n in group], split, tag)
        for dst, vals in zip((out_g, out_d, out_m, out_v), res):
            for n, a in zip(group, vals):
                dst[n] = a.reshape(wts[n].shape)

    return (loss, dx[None], *[out_g[n] for n in WEIGHTS], *[out_d[n] for n in WEIGHTS],
            *[out_m[n] for n in WEIGHTS], *[out_v[n] for n in WEIGHTS])
```

```python
import functools
import math

import numpy as np
import jax
import jax.numpy as jnp
from jax import lax
from jax.experimental import pallas as pl
from jax.experimental.pallas import tpu as pltpu

F32 = jnp.float32
BF16 = jnp.bfloat16
MESH = pl.DeviceIdType.MESH

EPS = 1e-6
NEG_INF = -1e30
LAMBDA_RE_MAX = -1e-4
ATTN_HEADS = 8
KV_HEADS = 2
GQ = ATTN_HEADS // KV_HEADS
HEAD_DIM = 64
ATTN_WIDTH = 512
KV_WIDTH = 128
WINDOW = 128
QBLOCK = 128
SSM_WIDTH = 512
SSM_GROUPS = 32
SSM_CH = 16
SSM_STATE = 64
N_STRIPS = 4
STRIP_IN = SSM_WIDTH // N_STRIPS
STRIP_ST = SSM_GROUPS * SSM_STATE // N_STRIPS
SUBLANES = 8
LANES = 128
N_CHIPS = 4
N_DEV = 8

ADAM_LR = 0.001
ADAM_B1 = 0.9
ADAM_B2 = 0.999
ADAM_EPS = 1e-08
ADAM_WD = 0.01
ADAM_STEP = 10

VMEM_LIMIT = 48 * 1024 * 1024

WEIGHTS = ['norm_ffn1', 'ffn1_w_gate', 'ffn1_w_up', 'ffn1_w_down', 'norm_mix', 'w_in', 'attn_sinks',
           'ssm_lambda_re', 'ssm_lambda_im', 'ssm_log_dt', 'ssm_b_re', 'ssm_b_im', 'ssm_c_re', 'ssm_c_im',
           'ssm_d', 'ssm_glu_w', 'ssm_glu_b', 'attn_out_norm', 'ssm_out_norm', 'w_out', 'norm_ffn2',
           'ffn2_w_gate', 'ffn2_w_up', 'ffn2_w_down', 'final_norm']
BIG = ['ffn1_w_gate', 'ffn1_w_up', 'ffn1_w_down', 'w_in', 'ssm_glu_w', 'w_out',
       'ffn2_w_gate', 'ffn2_w_up', 'ffn2_w_down']
SMALL = [n for n in WEIGHTS if n not in BIG]
GROUPS = {'ffn1': ['ffn1_w_gate', 'ffn1_w_up', 'ffn1_w_down'],
          'mix': ['w_in', 'ssm_glu_w', 'w_out'],
          'ffn2': ['ffn2_w_gate', 'ffn2_w_up', 'ffn2_w_down']}


def _cparams(sem=None):
    return pltpu.CompilerParams(dimension_semantics=sem, vmem_limit_bytes=VMEM_LIMIT)


def _tile(n, pref):
    if n <= pref:
        return n
    for t in (pref, pref // 2, pref // 4):
        if t % LANES == 0 and n % t == 0:
            return t
    return n


def _sigmoid(x):
    return 1.0 / (1.0 + jnp.exp(-x))


def _mm(a, b, *, ta=False, tb=False, reduce_s=False, res=None, scale=1.0, out_dtype=F32, name):
    a3 = a if a.ndim == 3 else a[None]
    b3 = b if b.ndim == 3 else b[None]
    sa, sb = a3.shape[0], b3.shape[0]
    ns = max(sa, sb)
    (kk, m) = a3.shape[1:] if ta else a3.shape[1:][::-1]
    (n, kb) = b3.shape[1:] if tb else b3.shape[1:][::-1]
    assert kk == kb, (a3.shape, b3.shape)
    tm, tn, tk = _tile(m, 1024), _tile(n, 1024), _tile(kk, 2048)
    nm, nn, nk = m // tm, n // tn, kk // tk
    has_res = res is not None
    single = nk == 1 and not (reduce_s and ns > 1)

    if reduce_s:
        grid = (nm, nn, ns, nk)
        ids = lambda i, j, s, k: (s, i, j, k)
        sem = ("parallel", "parallel", "arbitrary", "arbitrary")
    else:
        grid = (ns, nm, nn, nk)
        ids = lambda s, i, j, k: (s, i, j, k)
        sem = ("parallel", "parallel", "parallel", "arbitrary")

    def a_map(*g):
        s, i, j, k = ids(*g)
        s = s if sa > 1 else 0
        return (s, k, i) if ta else (s, i, k)

    def b_map(*g):
        s, i, j, k = ids(*g)
        s = s if sb > 1 else 0
        return (s, j, k) if tb else (s, k, j)

    def o_map(*g):
        s, i, j, k = ids(*g)
        return (i, j) if reduce_s else (s, i, j)

    a_blk = (1, tk, tm) if ta else (1, tm, tk)
    b_blk = (1, tn, tk) if tb else (1, tk, tn)
    dims = (((0 if ta else 1,), (1 if tb else 0,)), ((), ()))

    def body(*refs):
        a_ref, b_ref = refs[0], refs[1]
        r_ref = refs[2] if has_res else None
        o_ref = refs[3] if has_res else refs[2]
        acc_ref = None if single else refs[-1]
        s, _, _, k = ids(*[pl.program_id(d) for d in range(4)])
        prod = lax.dot_general(a_ref[0].astype(BF16), b_ref[0].astype(BF16), dims, preferred_element_type=F32)

        def finish(out):
            if scale != 1.0:
                out = out * scale
            if has_res:
                out = r_ref[...].reshape(out.shape) + out
            o_ref[...] = out.astype(out_dtype).reshape(o_ref.shape)

        if single:
            finish(prod)
            return
        if reduce_s:
            first = jnp.logical_and(s == 0, k == 0)
            last = jnp.logical_and(s == ns - 1, k == nk - 1)
        else:
            first, last = k == 0, k == nk - 1

        @pl.when(first)
        def _():
            acc_ref[...] = prod

        @pl.when(jnp.logical_not(first))
        def _():
            acc_ref[...] += prod

        @pl.when(last)
        def _():
            finish(acc_ref[...])

    in_specs = [pl.BlockSpec(a_blk, a_map), pl.BlockSpec(b_blk, b_map)]
    args = [a3, b3]
    if reduce_s:
        out_shape = jax.ShapeDtypeStruct((m, n), out_dtype)
        o_spec = pl.BlockSpec((tm, tn), o_map)
    else:
        out_shape = jax.ShapeDtypeStruct((ns, m, n), out_dtype)
        o_spec = pl.BlockSpec((1, tm, tn), o_map)
    if has_res:
        assert res.shape == out_shape.shape
        in_specs.append(o_spec)
        args.append(res)
    return pl.pallas_call(body, out_shape=out_shape, grid=grid, in_specs=in_specs, out_specs=o_spec,
                          scratch_shapes=[] if single else [pltpu.VMEM((tm, tn), F32)],
                          compiler_params=_cparams(sem), name=name)(*args)


def _row_tile(t):
    for tr in (256, 128, 64, 32, 16, 8):
        if t % tr == 0:
            return tr
    return t


def _rms_fwd(x, g, name):
    t, w = x.shape
    tr = _row_tile(t)

    def body(x_ref, g_ref, o_ref):
        xv = x_ref[...]
        r = lax.rsqrt(jnp.mean(xv * xv, axis=-1, keepdims=True) + EPS)
        o_ref[...] = (xv * r * g_ref[...]).astype(BF16)

    return pl.pallas_call(
        body, out_shape=jax.ShapeDtypeStruct((t, w), BF16), grid=(t // tr,),
        in_specs=[pl.BlockSpec((tr, w), lambda i: (i, 0)), pl.BlockSpec((1, w), lambda i: (0, 0))],
        out_specs=pl.BlockSpec((tr, w), lambda i: (i, 0)), compiler_params=_cparams(("parallel",)),
        name=name)(x, g)


def _rms_bwd_rows(xv, gv, dhv):
    r = lax.rsqrt(jnp.mean(xv * xv, axis=-1, keepdims=True) + EPS)
    nrm = xv * r
    dn = dhv * gv
    return r * (dn - nrm * jnp.mean(dn * nrm, axis=-1, keepdims=True)), dhv * nrm


def _rms_bwd(x, g, dh, dres, name):
    t, w = x.shape
    tr = _row_tile(t)
    has_res = dres is not None

    def body(*refs):
        if has_res:
            x_ref, g_ref, dh_ref, dr_ref, dx_ref, dxb_ref, dg_ref = refs
        else:
            x_ref, g_ref, dh_ref, dx_ref, dxb_ref, dg_ref = refs
        dx, dgs = _rms_bwd_rows(x_ref[...], g_ref[...], dh_ref[...])
        if has_res:
            dx = dx + dr_ref[...]
        dx_ref[...] = dx
        dxb_ref[...] = dx.astype(BF16)

        @pl.when(pl.program_id(0) == 0)
        def _():
            dg_ref[...] = jnp.zeros_like(dg_ref)

        dg_ref[...] += jnp.sum(dgs, axis=0, keepdims=True)

    row = pl.BlockSpec((tr, w), lambda i: (i, 0))
    vec = pl.BlockSpec((1, w), lambda i: (0, 0))
    ins = [x, g, dh] + ([dres] if has_res else [])
    return pl.pallas_call(
        body, out_shape=(jax.ShapeDtypeStruct((t, w), F32), jax.ShapeDtypeStruct((t, w), BF16),
                         jax.ShapeDtypeStruct((1, w), F32)),
        grid=(t // tr,), in_specs=[row, vec, row] + ([row] if has_res else []),
        out_specs=(row, row, vec), compiler_params=_cparams(("arbitrary",)), name=name)(*ins)


FFN_ROWS = 512


def _ffn_fwd_call(x, g, wg, wu, wd, name):
    t, d = x.shape
    ns, _, f = wg.shape
    tm = _tile(t, FFN_ROWS)

    def body(x_ref, g_ref, wg_ref, wu_ref, wd_ref, xo_ref, h_ref, gate_ref, up_ref, h_sc, acc_ref):
        s = pl.program_id(1)

        @pl.when(s == 0)
        def _():
            xv = x_ref[...]
            r = lax.rsqrt(jnp.mean(xv * xv, axis=-1, keepdims=True) + EPS)
            hb = (xv * r * g_ref[...]).astype(BF16)
            h_sc[...] = hb
            h_ref[...] = hb

        hb = h_sc[...]
        gate = jnp.dot(hb, wg_ref[0], preferred_element_type=F32)
        up = jnp.dot(hb, wu_ref[0], preferred_element_type=F32)
        gate_ref[0] = gate.astype(BF16)
        up_ref[0] = up.astype(BF16)
        act = (gate * _sigmoid(gate) * up).astype(BF16)
        prod = jnp.dot(act, wd_ref[0], preferred_element_type=F32)

        @pl.when(s == 0)
        def _():
            acc_ref[...] = prod

        @pl.when(s > 0)
        def _():
            acc_ref[...] += prod

        @pl.when(s == ns - 1)
        def _():
            xo_ref[...] = x_ref[...] + 0.5 * acc_ref[...]

    row = pl.BlockSpec((tm, d), lambda i, s: (i, 0))
    vec = pl.BlockSpec((1, d), lambda i, s: (0, 0))
    wcol = pl.BlockSpec((1, d, f), lambda i, s: (s, 0, 0))
    wrow = pl.BlockSpec((1, f, d), lambda i, s: (s, 0, 0))
    hid = pl.BlockSpec((1, tm, f), lambda i, s: (s, i, 0))
    hid_sh = jax.ShapeDtypeStruct((ns, t, f), BF16)
    return pl.pallas_call(
        body, out_shape=(jax.ShapeDtypeStruct((t, d), F32), jax.ShapeDtypeStruct((t, d), BF16), hid_sh, hid_sh),
        grid=(t // tm, ns), in_specs=[row, vec, wcol, wcol, wrow], out_specs=(row, row, hid, hid),
        scratch_shapes=[pltpu.VMEM((tm, d), BF16), pltpu.VMEM((tm, d), F32)],
        compiler_params=_cparams(("parallel", "arbitrary")), name=name)(x, g, wg, wu, wd)


def _ffn_bwd_x_call(dxo, dxo_b, x, g, gate, up, wg, wu, wd, name):
    t, d = x.shape
    ns, _, f = wg.shape
    tm = _tile(t, FFN_ROWS)
    nt = (((1,), (1,)), ((), ()))

    def body(dxo_ref, dxb_ref, x_ref, g_ref, gate_ref, up_ref, wg_ref, wu_ref, wd_ref,
             dx_ref, dxob_ref, dgn_ref, dgate_ref, dup_ref, act_ref, dh_ref):
        i, s = pl.program_id(0), pl.program_id(1)
        dact = lax.dot_general(dxb_ref[...], wd_ref[0], nt, preferred_element_type=F32) * 0.5
        gv = gate_ref[0].astype(F32)
        uv = up_ref[0].astype(F32)
        sg = _sigmoid(gv)
        silu = gv * sg
        act_ref[0] = (silu * uv).astype(BF16)
        dub = (dact * silu).astype(BF16)
        dgb = (dact * uv * sg * (1.0 + gv * (1.0 - sg))).astype(BF16)
        dup_ref[0] = dub
        dgate_ref[0] = dgb
        prod = (lax.dot_general(dgb, wg_ref[0], nt, preferred_element_type=F32)
                + lax.dot_general(dub, wu_ref[0], nt, preferred_element_type=F32))

        @pl.when(s == 0)
        def _():
            dh_ref[...] = prod

        @pl.when(s > 0)
        def _():
            dh_ref[...] += prod

        @pl.when(jnp.logical_and(i == 0, s == 0))
        def _():
            dgn_ref[...] = jnp.zeros_like(dgn_ref)

        @pl.when(s == ns - 1)
        def _():
            dx, dgs = _rms_bwd_rows(x_ref[...], g_ref[...], dh_ref[...])
            dx = dx + dxo_ref[...]
            dx_ref[...] = dx
            dxob_ref[...] = dx.astype(BF16)
            dgn_ref[...] += jnp.sum(dgs, axis=0, keepdims=True)

    row = pl.BlockSpec((tm, d), lambda i, s: (i, 0))
    vec = pl.BlockSpec((1, d), lambda i, s: (0, 0))
    wcol = pl.BlockSpec((1, d, f), lambda i, s: (s, 0, 0))
    wrow = pl.BlockSpec((1, f, d), lambda i, s: (s, 0, 0))
    hid = pl.BlockSpec((1, tm, f), lambda i, s: (s, i, 0))
    hid_sh = jax.ShapeDtypeStruct((ns, t, f), BF16)
    return pl.pallas_call(
        body,
        out_shape=(jax.ShapeDtypeStruct((t, d), F32), jax.ShapeDtypeStruct((t, d), BF16),
                   jax.ShapeDtypeStruct((1, d), F32), hid_sh, hid_sh, hid_sh),
        grid=(t // tm, ns), in_specs=[row, row, row, vec, hid, hid, wcol, wcol, wrow],
        out_specs=(row, row, vec, hid, hid, hid), scratch_shapes=[pltpu.VMEM((tm, d), F32)],
        compiler_params=_cparams(("arbitrary", "arbitrary")), name=name)(dxo, dxo_b, x, g, gate, up, wg, wu, wd)


def _ffn_bwd_w_call(h, dxo_b, dgate, dup, act, name):
    t, d = h.shape
    ns, _, f = dgate.shape
    tm = _tile(t, FFN_ROWS)
    nm = t // tm
    tn = (((0,), (0,)), ((), ()))

    def body(h_ref, dxb_ref, dgate_ref, dup_ref, act_ref, dwg_ref, dwu_ref, dwd_ref, ag_ref, au_ref, ad_ref):
        i = pl.program_id(1)
        hv = h_ref[...]
        pg = lax.dot_general(hv, dgate_ref[0], tn, preferred_element_type=F32)
        pu = lax.dot_general(hv, dup_ref[0], tn, preferred_element_type=F32)
        pd = lax.dot_general(act_ref[0], dxb_ref[...], tn, preferred_element_type=F32)

        @pl.when(i == 0)
        def _():
            ag_ref[...] = pg
            au_ref[...] = pu
            ad_ref[...] = pd

        @pl.when(i > 0)
        def _():
            ag_ref[...] += pg
            au_ref[...] += pu
            ad_ref[...] += pd

        @pl.when(i == nm - 1)
        def _():
            dwg_ref[0] = ag_ref[...].astype(BF16)
            dwu_ref[0] = au_ref[...].astype(BF16)
            dwd_ref[0] = (0.5 * ad_ref[...]).astype(BF16)

    row = pl.BlockSpec((tm, d), lambda s, i: (i, 0))
    hid = pl.BlockSpec((1, tm, f), lambda s, i: (s, i, 0))
    wcol = pl.BlockSpec((1, d, f), lambda s, i: (s, 0, 0))
    wrow = pl.BlockSpec((1, f, d), lambda s, i: (s, 0, 0))
    return pl.pallas_call(
        body,
        out_shape=(jax.ShapeDtypeStruct((ns, d, f), BF16), jax.ShapeDtypeStruct((ns, d, f), BF16),
                   jax.ShapeDtypeStruct((ns, f, d), BF16)),
        grid=(ns, nm), in_specs=[row, row, hid, hid, hid], out_specs=(wcol, wcol, wrow),
        scratch_shapes=[pltpu.VMEM((d, f), F32), pltpu.VMEM((d, f), F32), pltpu.VMEM((f, d), F32)],
        compiler_params=_cparams(("parallel", "arbitrary")), name=name)(h, dxo_b, dgate, dup, act)


def _loss_head(x, g, tgt, name):
    t, w = x.shape
    tr = _row_tile(t)

    def body(x_ref, g_ref, t_ref, loss_ref, dx_ref, dxb_ref, dg_ref):
        xv = x_ref[...]
        gv = g_ref[...]
        r = lax.rsqrt(jnp.mean(xv * xv, axis=-1, keepdims=True) + EPS)
        nrm = xv * r
        err = nrm * gv - t_ref[...]
        dout = err * (1.0 / w)
        dn = dout * gv
        dx = r * (dn - nrm * jnp.mean(dn * nrm, axis=-1, keepdims=True))
        dx_ref[...] = dx
        dxb_ref[...] = dx.astype(BF16)

        @pl.when(pl.program_id(0) == 0)
        def _():
            dg_ref[...] = jnp.zeros_like(dg_ref)
            loss_ref[...] = jnp.zeros_like(loss_ref)

        dg_ref[...] += jnp.sum(dout * nrm, axis=0, keepdims=True)
        part = jnp.sum(jnp.sum(err * err, axis=-1, keepdims=True) * (0.5 / w), axis=0, keepdims=True)
        loss_ref[...] += jnp.broadcast_to(part, loss_ref.shape)

    row = pl.BlockSpec((tr, w), lambda i: (i, 0))
    vec = pl.BlockSpec((1, w), lambda i: (0, 0))
    return pl.pallas_call(
        body, out_shape=(jax.ShapeDtypeStruct((1, LANES), F32), jax.ShapeDtypeStruct((t, w), F32),
                         jax.ShapeDtypeStruct((t, w), BF16), jax.ShapeDtypeStruct((1, w), F32)),
        grid=(t // tr,), in_specs=[row, vec, row],
        out_specs=(pl.BlockSpec((1, LANES), lambda i: (0, 0)), row, row, vec),
        compiler_params=_cparams(("arbitrary",)), name=name)(x, g, tgt)


def _attn_scores(q, k3, n, t, slope_ref):
    rows = GQ * QBLOCK
    s = lax.dot_general(q, k3, (((1,), (1,)), ((), ())), preferred_element_type=F32) * (HEAD_DIM ** -0.5)
    row = lax.broadcasted_iota(jnp.int32, (rows, 3 * QBLOCK), 0) & (QBLOCK - 1)
    col = lax.broadcasted_iota(jnp.int32, (rows, 3 * QBLOCK), 1)
    rel = jnp.abs(col - QBLOCK - row)
    key_pos = n * QBLOCK - QBLOCK + col
    valid = (rel <= WINDOW) & (key_pos >= 0) & (key_pos < t)
    return jnp.where(valid, s - slope_ref[0] * rel.astype(F32), NEG_INF)


def _attn_fwd(q4, kp, vp, sink_rows, slope_rows, name):
    _, _, t, _ = q4.shape
    nb = t // QBLOCK
    rows = GQ * QBLOCK

    def body(q_ref, k_ref, v_ref, sink_ref, slope_ref, o_ref, lse_ref):
        n = pl.program_id(1)
        start = pl.multiple_of(n * QBLOCK, QBLOCK)
        q = q_ref[0].reshape(rows, HEAD_DIM).astype(BF16)
        k3 = k_ref[0, pl.ds(start, 3 * QBLOCK), :].astype(BF16)
        v3 = v_ref[0, pl.ds(start, 3 * QBLOCK), :].astype(BF16)
        s = _attn_scores(q, k3, n, t, slope_ref)
        sink = sink_ref[0]
        mx = jnp.maximum(jnp.max(s, axis=-1, keepdims=True), sink)
        p = jnp.exp(s - mx)
        den = jnp.sum(p, axis=-1, keepdims=True) + jnp.exp(sink - mx)
        o = lax.dot_general(p.astype(BF16), v3, (((1,), (0,)), ((), ())), preferred_element_type=F32)
        o_ref[0] = (o / den).reshape(GQ, QBLOCK, HEAD_DIM)
        lse_ref[0] = (mx + jnp.log(den)).reshape(GQ, QBLOCK, 1)

    qspec = pl.BlockSpec((1, GQ, QBLOCK, HEAD_DIM), lambda h, n: (h, 0, n, 0))
    kvspec = pl.BlockSpec((1, t + 2 * QBLOCK, HEAD_DIM), lambda h, n: (h, 0, 0))
    rowspec = pl.BlockSpec((1, rows, 1), lambda h, n: (h, 0, 0))
    return pl.pallas_call(
        body, out_shape=(jax.ShapeDtypeStruct(q4.shape, F32), jax.ShapeDtypeStruct((KV_HEADS, GQ, t, 1), F32)),
        grid=(KV_HEADS, nb), in_specs=[qspec, kvspec, kvspec, rowspec, rowspec],
        out_specs=(qspec, pl.BlockSpec((1, GQ, QBLOCK, 1), lambda h, n: (h, 0, n, 0))),
        compiler_params=_cparams(("parallel", "parallel")), name=name)(q4, kp, vp, sink_rows, slope_rows)


def _attn_bwd(q4, kp, vp, sink_rows, slope_rows, o4, lse4, do4, name):
    _, _, t, _ = q4.shape
    nb = t // QBLOCK
    rows = GQ * QBLOCK
    scale = HEAD_DIM ** -0.5

    def body(q_ref, k_ref, v_ref, sink_ref, slope_ref, o_ref, lse_ref, do_ref, dq_ref, dk_ref, dv_ref, ds_ref):
        n = pl.program_id(1)
        start = pl.multiple_of(n * QBLOCK, QBLOCK)

        @pl.when(n == 0)
        def _():
            dk_ref[...] = jnp.zeros_like(dk_ref)
            dv_ref[...] = jnp.zeros_like(dv_ref)
            ds_ref[...] = jnp.zeros_like(ds_ref)

        q = q_ref[0].reshape(rows, HEAD_DIM).astype(BF16)
        k3 = k_ref[0, pl.ds(start, 3 * QBLOCK), :].astype(BF16)
        v3 = v_ref[0, pl.ds(start, 3 * QBLOCK), :].astype(BF16)
        do = do_ref[0].reshape(rows, HEAD_DIM)
        lse = lse_ref[0].reshape(rows, 1)
        s = _attn_scores(q, k3, n, t, slope_ref)
        p = jnp.exp(s - lse)
        delta = jnp.sum(do * o_ref[0].reshape(rows, HEAD_DIM), axis=-1, keepdims=True)
        dob = do.astype(BF16)
        dp = lax.dot_general(dob, v3, (((1,), (1,)), ((), ())), preferred_element_type=F32)
        dsb = (p * (dp - delta)).astype(BF16)
        dq = lax.dot_general(dsb, k3, (((1,), (0,)), ((), ())), preferred_element_type=F32) * scale
        dq_ref[0] = dq.reshape(GQ, QBLOCK, HEAD_DIM)
        dk3 = lax.dot_general(dsb, q, (((0,), (0,)), ((), ())), preferred_element_type=F32) * scale
        dv3 = lax.dot_general(p.astype(BF16), dob, (((0,), (0,)), ((), ())), preferred_element_type=F32)
        dk_ref[0, pl.ds(start, 3 * QBLOCK), :] += dk3
        dv_ref[0, pl.ds(start, 3 * QBLOCK), :] += dv3
        dsink_rows = -jnp.exp(sink_ref[0] - lse) * delta
        ds_ref[0] += jnp.sum(dsink_rows.reshape(GQ, QBLOCK, 1), axis=1)

    qspec = pl.BlockSpec((1, GQ, QBLOCK, HEAD_DIM), lambda h, n: (h, 0, n, 0))
    kvspec = pl.BlockSpec((1, t + 2 * QBLOCK, HEAD_DIM), lambda h, n: (h, 0, 0))
    rowspec = pl.BlockSpec((1, rows, 1), lambda h, n: (h, 0, 0))
    lsespec = pl.BlockSpec((1, GQ, QBLOCK, 1), lambda h, n: (h, 0, n, 0))
    return pl.pallas_call(
        body,
        out_shape=(jax.ShapeDtypeStruct(q4.shape, F32), jax.ShapeDtypeStruct(kp.shape, F32),
                   jax.ShapeDtypeStruct(vp.shape, F32), jax.ShapeDtypeStruct((KV_HEADS, GQ, 1), F32)),
        grid=(KV_HEADS, nb),
        in_specs=[qspec, kvspec, kvspec, rowspec, rowspec, qspec, lsespec, qspec],
        out_specs=(qspec, kvspec, kvspec, pl.BlockSpec((1, GQ, 1), lambda h, n: (h, 0, 0))),
        compiler_params=_cparams(("parallel", "arbitrary")), name=name)(
            q4, kp, vp, sink_rows, slope_rows, o4, lse4, do4)


def _scan_tables(a_re, a_im, reverse):
    pw = [(a_re, a_im)]
    for _ in range(SUBLANES - 1):
        pr, pi = pw[-1]
        pw.append((pr * a_re - pi * a_im, pr * a_im + pi * a_re))
    rows = np.arange(SUBLANES)
    tabs = []
    for d in (1, 2, 4):
        mask = (rows <= SUBLANES - 1 - d) if reverse else (rows >= d)
        m = jnp.asarray(mask, F32)[:, None]
        tabs += [m * pw[d - 1][0][None, :], m * pw[d - 1][1][None, :]]
    order = (SUBLANES - 1 - rows) if reverse else rows
    tabs += [jnp.stack([pw[j][0] for j in order]), jnp.stack([pw[j][1] for j in order])]
    tab = jnp.stack(tabs)
    return tab.reshape(8, SUBLANES, N_STRIPS, STRIP_ST).transpose(2, 0, 1, 3)


def _scan(v, mi_re, mi_im, tab, mo_re, mo_im, reverse, name):
    t = v.shape[0]
    tc = _tile(t, 256)
    nc = t // tc
    nblk = tc // SUBLANES

    def body(v_ref, mir_ref, mii_ref, tab_ref, mor_ref, moi_ref, y_ref, xr_ref, xi_ref, carry_ref):
        @pl.when(pl.program_id(1) == 0)
        def _():
            carry_ref[...] = jnp.zeros_like(carry_ref)

        vb = v_ref[...].astype(BF16)
        xr_ref[...] = jnp.dot(vb, mir_ref[0], preferred_element_type=F32)
        xi_ref[...] = jnp.dot(vb, mii_ref[0], preferred_element_type=F32)

        def blk(i, carry):
            cr, ci = carry
            b = (nblk - 1 - i) if reverse else i
            r0 = pl.multiple_of(b * SUBLANES, SUBLANES)
            xr = xr_ref[pl.ds(r0, SUBLANES), :]
            xi = xi_ref[pl.ds(r0, SUBLANES), :]
            for j, d in enumerate((1, 2, 4)):
                tr_, ti_ = tab_ref[0, 2 * j], tab_ref[0, 2 * j + 1]
                sh = (SUBLANES - d) if reverse else d
                sr = pltpu.roll(xr, sh, 0)
                si = pltpu.roll(xi, sh, 0)
                xr, xi = xr + tr_ * sr - ti_ * si, xi + tr_ * si + ti_ * sr
            pr, pi = tab_ref[0, 6], tab_ref[0, 7]
            xr, xi = xr + pr * cr - pi * ci, xi + pr * ci + pi * cr
            xr_ref[pl.ds(r0, SUBLANES), :] = xr
            xi_ref[pl.ds(r0, SUBLANES), :] = xi
            edge = 0 if reverse else SUBLANES - 1
            return (jnp.broadcast_to(xr[edge:edge + 1, :], xr.shape),
                    jnp.broadcast_to(xi[edge:edge + 1, :], xi.shape))

        cr, ci = lax.fori_loop(0, nblk, blk, (carry_ref[0], carry_ref[1]))
        carry_ref[0] = cr
        carry_ref[1] = ci
        y_ref[...] = (jnp.dot(xr_ref[...].astype(BF16), mor_ref[0], preferred_element_type=F32)
                      + jnp.dot(xi_ref[...].astype(BF16), moi_ref[0], preferred_element_type=F32))

    tmap = (lambda s, c: (nc - 1 - c, s)) if reverse else (lambda s, c: (c, s))
    smap3 = lambda s, c: (s, 0, 0)
    return pl.pallas_call(
        body,
        out_shape=(jax.ShapeDtypeStruct((t, SSM_WIDTH), F32),
                   jax.ShapeDtypeStruct((t, N_STRIPS * STRIP_ST), F32),
                   jax.ShapeDtypeStruct((t, N_STRIPS * STRIP_ST), F32)),
        grid=(N_STRIPS, nc),
        in_specs=[pl.BlockSpec((tc, STRIP_IN), tmap),
                  pl.BlockSpec((1, STRIP_IN, STRIP_ST), smap3), pl.BlockSpec((1, STRIP_IN, STRIP_ST), smap3),
                  pl.BlockSpec((1, 8, SUBLANES, STRIP_ST), lambda s, c: (s, 0, 0, 0)),
                  pl.BlockSpec((1, STRIP_ST, STRIP_IN), smap3), pl.BlockSpec((1, STRIP_ST, STRIP_IN), smap3)],
        out_specs=(pl.BlockSpec((tc, STRIP_IN), tmap), pl.BlockSpec((tc, STRIP_ST), tmap),
                   pl.BlockSpec((tc, STRIP_ST), tmap)),
        scratch_shapes=[pltpu.VMEM((2, SUBLANES, STRIP_ST), F32)],
        compiler_params=_cparams(("parallel", "arbitrary")), name=name)(v, mi_re, mi_im, tab, mo_re, mo_im)


def _scan_param_grads(v, dy, xr, xi, lr, li, reverse, name):
    t = v.shape[0]
    tc = _tile(t, 256)
    nc = t // tc
    hb = tc // SUBLANES

    def body(v_ref, dy_ref, xr_ref, xi_ref, lr_ref, li_ref, hr_ref, hi_ref,
             dmir_ref, dmii_ref, dmor_ref, dmoi_ref, da_ref):
        c = pl.program_id(1)

        @pl.when(c == 0)
        def _():
            for r in (dmir_ref, dmii_ref, dmor_ref, dmoi_ref, da_ref):
                r[...] = jnp.zeros_like(r)

        xrv, xiv, lrv, liv = xr_ref[...], xi_ref[...], lr_ref[...], li_ref[...]
        row = lax.broadcasted_iota(jnp.int32, xrv.shape, 0)
        if reverse:
            live = (c < nc - 1).astype(F32)
            edge_r, edge_i = hr_ref[0:1, :] * live, hi_ref[0:1, :] * live
            xpr = jnp.where(row == tc - 1, edge_r, pltpu.roll(xrv, tc - 1, 0))
            xpi = jnp.where(row == tc - 1, edge_i, pltpu.roll(xiv, tc - 1, 0))
        else:
            live = (c > 0).astype(F32)
            edge_r, edge_i = hr_ref[SUBLANES - 1:SUBLANES, :] * live, hi_ref[SUBLANES - 1:SUBLANES, :] * live
            xpr = jnp.where(row == 0, edge_r, pltpu.roll(xrv, 1, 0))
            xpi = jnp.where(row == 0, edge_i, pltpu.roll(xiv, 1, 0))
        da_ref[0, 0:1, :] += jnp.sum(xpr * lrv + xpi * liv, axis=0, keepdims=True)
        da_ref[0, 1:2, :] += jnp.sum(xpr * liv - xpi * lrv, axis=0, keepdims=True)
        tdims = (((0,), (0,)), ((), ()))
        vb, dyb = v_ref[...].astype(BF16), dy_ref[...].astype(BF16)
        dmir_ref[0] += lax.dot_general(vb, lrv.astype(BF16), tdims, preferred_element_type=F32)
        dmii_ref[0] += lax.dot_general(vb, liv.astype(BF16), tdims, preferred_element_type=F32)
        dmor_ref[0] += lax.dot_general(xrv.astype(BF16), dyb, tdims, preferred_element_type=F32)
        dmoi_ref[0] += lax.dot_general(xiv.astype(BF16), dyb, tdims, preferred_element_type=F32)

    tmap = lambda s, c: (c, s)
    if reverse:
        hmap = lambda s, c: (jnp.minimum((c + 1) * hb, t // SUBLANES - 1), s)
    else:
        hmap = lambda s, c: (jnp.maximum(c * hb - 1, 0), s)
    narrow = pl.BlockSpec((tc, STRIP_IN), tmap)
    wide = pl.BlockSpec((tc, STRIP_ST), tmap)
    halo = pl.BlockSpec((SUBLANES, STRIP_ST), hmap)
    smap3 = lambda s, c: (s, 0, 0)
    return pl.pallas_call(
        body,
        out_shape=(jax.ShapeDtypeStruct((N_STRIPS, STRIP_IN, STRIP_ST), F32),
                   jax.ShapeDtypeStruct((N_STRIPS, STRIP_IN, STRIP_ST), F32),
                   jax.ShapeDtypeStruct((N_STRIPS, STRIP_ST, STRIP_IN), F32),
                   jax.ShapeDtypeStruct((N_STRIPS, STRIP_ST, STRIP_IN), F32),
                   jax.ShapeDtypeStruct((N_STRIPS, SUBLANES, STRIP_ST), F32)),
        grid=(N_STRIPS, nc),
        in_specs=[narrow, narrow, wide, wide, wide, wide, halo, halo],
        out_specs=(pl.BlockSpec((1, STRIP_IN, STRIP_ST), smap3), pl.BlockSpec((1, STRIP_IN, STRIP_ST), smap3),
                   pl.BlockSpec((1, STRIP_ST, STRIP_IN), smap3), pl.BlockSpec((1, STRIP_ST, STRIP_IN), smap3),
                   pl.BlockSpec((1, SUBLANES, STRIP_ST), smap3)),
        compiler_params=_cparams(("parallel", "arbitrary")), name=name)(v, dy, xr, xi, lr, li, xr, xi)


def _ssm_prep(lam_re, lam_im, log_dt, b_re, b_im, c_re, c_im):
    lr = jnp.minimum(lam_re, LAMBDA_RE_MAX)
    li = lam_im
    dt = jnp.exp(log_dt)[:, None]
    mag = jnp.exp(lr * dt)
    a_re = mag * jnp.cos(li * dt)
    a_im = mag * jnp.sin(li * dt)
    den = lr * lr + li * li
    coef_re = ((a_re - 1.0) * lr + a_im * li) / den
    coef_im = (a_im * lr - (a_re - 1.0) * li) / den
    bb_re = coef_re[..., None] * b_re - coef_im[..., None] * b_im
    bb_im = coef_re[..., None] * b_im + coef_im[..., None] * b_re
    eye = jnp.eye(SSM_GROUPS // N_STRIPS, dtype=F32)

    def strips(m):
        g, a, b = m.shape
        m4 = m.reshape(N_STRIPS, g // N_STRIPS, a, b)
        return jnp.einsum('sgab,gk->sgakb', m4, eye).reshape(N_STRIPS, g // N_STRIPS * a, g // N_STRIPS * b)

    mi_re = strips(jnp.swapaxes(bb_re, 1, 2))
    mi_im = strips(jnp.swapaxes(bb_im, 1, 2))
    mo_re = strips(jnp.swapaxes(c_re, 1, 2))
    mo_im = strips(-jnp.swapaxes(c_im, 1, 2))
    return a_re.reshape(-1), a_im.reshape(-1), mi_re, mi_im, mo_re, mo_im


def _gelu(x):
    c = math.sqrt(2.0 / math.pi)
    return 0.5 * x * (1.0 + jnp.tanh(c * (x + 0.044715 * x * x * x)))


def _gelu_grad(x):
    c = math.sqrt(2.0 / math.pi)
    th = jnp.tanh(c * (x + 0.044715 * x * x * x))
    return 0.5 * (1.0 + th) + 0.5 * x * (1.0 - th * th) * c * (1.0 + 3.0 * 0.044715 * x * x)


def _ssm_post_fwd(u, yf, yb, d, wglu, bglu, name):
    t, w = u.shape
    tr = _row_tile(t)

    def body(u_ref, yf_ref, yb_ref, d_ref, w_ref, b_ref, s_ref, y0_ref, z_ref):
        y0 = d_ref[...] * u_ref[...] + yf_ref[...] + yb_ref[...]
        yg = _gelu(y0)
        z = jnp.dot(yg.astype(BF16), w_ref[...], preferred_element_type=F32) + b_ref[...]
        s_ref[...] = yg * _sigmoid(z)
        y0_ref[...] = y0
        z_ref[...] = z

    row = pl.BlockSpec((tr, w), lambda i: (i, 0))
    vec = pl.BlockSpec((1, w), lambda i: (0, 0))
    mat = pl.BlockSpec((w, w), lambda i: (0, 0))
    sh = jax.ShapeDtypeStruct((t, w), F32)
    return pl.pallas_call(body, out_shape=(sh, sh, sh), grid=(t // tr,),
                          in_specs=[row, row, row, vec, mat, vec], out_specs=(row, row, row),
                          compiler_params=_cparams(("parallel",)), name=name)(u, yf, yb, d, wglu, bglu)


def _ssm_post_bwd(ds, y0, z, u, d, wglu, name):
    t, w = u.shape
    tr = _row_tile(t)

    def body(ds_ref, y0_ref, z_ref, u_ref, d_ref, w_ref, dy0_ref, dw_ref, db_ref, dd_ref):
        @pl.when(pl.program_id(0) == 0)
        def _():
            dw_ref[...] = jnp.zeros_like(dw_ref)
            db_ref[...] = jnp.zeros_like(db_ref)
            dd_ref[...] = jnp.zeros_like(dd_ref)

        y0 = y0_ref[...]
        yg = _gelu(y0)
        sg = _sigmoid(z_ref[...])
        dsv = ds_ref[...]
        dz = dsv * yg * sg * (1.0 - sg)
        dzb = dz.astype(BF16)
        dyg = dsv * sg + lax.dot_general(dzb, w_ref[...], (((1,), (1,)), ((), ())), preferred_element_type=F32)
        dy0 = dyg * _gelu_grad(y0)
        dy0_ref[...] = dy0
        dw_ref[...] += lax.dot_general(yg.astype(BF16), dzb, (((0,), (0,)), ((), ())), preferred_element_type=F32)
        db_ref[...] += jnp.sum(dz, axis=0, keepdims=True)
        dd_ref[...] += jnp.sum(dy0 * u_ref[...], axis=0, keepdims=True)

    row = pl.BlockSpec((tr, w), lambda i: (i, 0))
    vec = pl.BlockSpec((1, w), lambda i: (0, 0))
    mat = pl.BlockSpec((w, w), lambda i: (0, 0))
    return pl.pallas_call(
        body, out_shape=(jax.ShapeDtypeStruct((t, w), F32), jax.ShapeDtypeStruct((w, w), F32),
                         jax.ShapeDtypeStruct((1, w), F32), jax.ShapeDtypeStruct((1, w), F32)),
        grid=(t // tr,), in_specs=[row, row, row, row, vec, mat], out_specs=(row, mat, vec, vec),
        compiler_params=_cparams(("arbitrary",)), name=name)(ds, y0, z, u, d, wglu)


def _du_combine(dy0, d, du_f, du_b, name):
    t, w = dy0.shape
    tr = _row_tile(t)

    def body(dy_ref, d_ref, a_ref, b_ref, o_ref):
        o_ref[...] = d_ref[...] * dy_ref[...] + a_ref[...] + b_ref[...]

    row = pl.BlockSpec((tr, w), lambda i: (i, 0))
    vec = pl.BlockSpec((1, w), lambda i: (0, 0))
    return pl.pallas_call(body, out_shape=jax.ShapeDtypeStruct((t, w), F32), grid=(t // tr,),
                          in_specs=[row, vec, row, row], out_specs=row, compiler_params=_cparams(("parallel",)),
                          name=name)(dy0, d, du_f, du_b)


def _ffn_fwd(x, g, wg, wu, wd, tag):
    xo, h, gate, up = _ffn_fwd_call(x, g, wg, wu, wd, f"{tag}_fwd")
    return xo, (h, gate, up)


def _ffn_bwd(dxo, dxo_b, x, g, wg, wu, wd, saved, tag):
    h, gate, up = saved
    dx, dx_b, dg, dgate, dup, act = _ffn_bwd_x_call(dxo, dxo_b, x, g, gate, up, wg, wu, wd, f"{tag}_bwd_x")
    dwg, dwu, dwd = _ffn_bwd_w_call(h, dxo_b, dgate, dup, act, f"{tag}_bwd_w")
    return dx, dx_b, dg, dwg, dwu, dwd


def _heads_split(q, k, v):
    t = q.shape[0]
    q4 = q.reshape(t, KV_HEADS, GQ, HEAD_DIM).transpose(1, 2, 0, 3)
    pad = lambda a: jnp.pad(a.reshape(t, KV_HEADS, HEAD_DIM).transpose(1, 0, 2), ((0, 0), (QBLOCK, QBLOCK), (0, 0)))
    return q4, pad(k), pad(v)


def _local_step(x, tgt, w, get_weights, put_grads):
    t = x.shape[0]
    row = lambda a: a.reshape(1, -1)
    grads = {}

    w = dict(w)
    w.update(get_weights('ffn1', x))
    x1, ffn1_saved = _ffn_fwd(x, w['norm_ffn1'], w['ffn1_w_gate'], w['ffn1_w_up'], w['ffn1_w_down'], "ffn1")
    w.update(get_weights('mix', x1))

    h2 = _rms_fwd(x1, w['norm_mix'], "mix_norm")
    proj = _mm(h2, w['w_in'], name="in_proj")
    proj = proj.transpose(1, 0, 2).reshape(t, -1)
    q = proj[:, :ATTN_WIDTH]
    k = proj[:, ATTN_WIDTH:ATTN_WIDTH + KV_WIDTH]
    v = proj[:, ATTN_WIDTH + KV_WIDTH:ATTN_WIDTH + 2 * KV_WIDTH]
    u = proj[:, ATTN_WIDTH + 2 * KV_WIDTH:]

    q4, kp, vp = _heads_split(q, k, v)
    sink_rows = jnp.repeat(w['attn_sinks'].reshape(KV_HEADS, GQ), QBLOCK, axis=1)[..., None]
    slopes = jnp.asarray(2.0 ** (-8.0 * (np.arange(ATTN_HEADS) + 1) / ATTN_HEADS), F32)
    slope_rows = jnp.repeat(slopes.reshape(KV_HEADS, GQ), QBLOCK, axis=1)[..., None]
    o4, lse4 = _attn_fwd(q4, kp, vp, sink_rows, slope_rows, "attn_fwd")
    attn = o4.transpose(2, 0, 1, 3).reshape(t, ATTN_WIDTH)

    ssm_names = ['ssm_lambda_re', 'ssm_lambda_im', 'ssm_log_dt', 'ssm_b_re', 'ssm_b_im', 'ssm_c_re', 'ssm_c_im']
    ys, states, preps, vjps = [], [], [], []
    for direction in range(2):
        params = [w[n][direction] for n in ssm_names]
        prep, vjp = jax.vjp(_ssm_prep, *params)
        a_re, a_im = prep[0], prep[1]
        mi_re, mi_im, mo_re, mo_im = (m.astype(BF16) for m in prep[2:])
        prep = (a_re, a_im, mi_re, mi_im, mo_re, mo_im)
        rev = direction == 1
        tab = _scan_tables(a_re, a_im, rev)
        y, xr, xi = _scan(u, mi_re, mi_im, tab, mo_re, mo_im, rev, f"s5_fwd{direction}")
        ys.append(y)
        states.append((xr, xi))
        preps.append(prep)
        vjps.append(vjp)
    d_row = row(w['ssm_d'])
    s, y0, z = _ssm_post_fwd(u, ys[0], ys[1], d_row, w['ssm_glu_w'], row(w['ssm_glu_b']), "ssm_post")

    ma = _rms_fwd(attn, row(w['attn_out_norm']), "attn_out_norm")
    ms = _rms_fwd(s, row(w['ssm_out_norm']), "ssm_out_norm")
    mixed = jnp.concatenate([ma, ms], axis=-1)
    x2 = _mm(mixed, w['w_out'], res=x1, reduce_s=True, name="out_proj")

    w.update(get_weights('ffn2', x2))
    x3, ffn2_saved = _ffn_fwd(x2, w['norm_ffn2'], w['ffn2_w_gate'], w['ffn2_w_up'], w['ffn2_w_down'], "ffn2")

    loss_row, dx3, dx3_b, dgf = _loss_head(x3, row(w['final_norm']), tgt, "loss_head")
    loss = loss_row[0, 0]
    grads['final_norm'] = dgf.reshape(w['final_norm'].shape)

    dx2, dx2_b, dg, dwg, dwu, dwd = _ffn_bwd(dx3, dx3_b, x2, w['norm_ffn2'], w['ffn2_w_gate'], w['ffn2_w_up'],
                                             w['ffn2_w_down'], ffn2_saved, "ffn2")
    grads['norm_ffn2'] = dg
    put_grads('ffn2', dict(ffn2_w_gate=dwg, ffn2_w_up=dwu, ffn2_w_down=dwd))

    dmixed = _mm(dx2_b, w['w_out'], tb=True, reduce_s=True, name="out_proj_dx")
    dw_out = _mm(mixed, dx2_b, ta=True, out_dtype=BF16, name="out_proj_dw")[0]
    dattn, _, dga = _rms_bwd(attn, row(w['attn_out_norm']), dmixed[:, :ATTN_WIDTH], None, "attn_out_dnorm")
    ds, _, dgs = _rms_bwd(s, row(w['ssm_out_norm']), dmixed[:, ATTN_WIDTH:], None, "ssm_out_dnorm")
    grads.update(attn_out_norm=dga, ssm_out_norm=dgs)

    dy0, dwglu, dbglu, dd = _ssm_post_bwd(ds, y0, z, u, d_row, w['ssm_glu_w'], "ssm_post_bwd")
    grads['ssm_glu_b'] = dbglu
    grads['ssm_d'] = dd.reshape(w['ssm_d'].shape)
    dparams, du_dirs = [], []
    for direction in range(2):
        a_re, a_im, mi_re, mi_im, mo_re, mo_im = preps[direction]
        rev = direction == 1
        tab = _scan_tables(a_re, -a_im, not rev)
        tr3 = lambda m: jnp.swapaxes(m, 1, 2)
        du_dir, lr, li = _scan(dy0, tr3(mo_re), tr3(mo_im), tab, tr3(mi_re), tr3(mi_im), not rev,
                               f"s5_adj{direction}")
        du_dirs.append(du_dir)
        xr, xi = states[direction]
        dmir, dmii, dmor, dmoi, da = _scan_param_grads(u, dy0, xr, xi, lr, li, rev, f"s5_pgrad{direction}")
        da_re = da[:, 0, :].reshape(-1)
        da_im = da[:, 1, :].reshape(-1)
        dparams.append(vjps[direction]((da_re, da_im, dmir, dmii, dmor, dmoi)))
    du = _du_combine(dy0, d_row, du_dirs[0], du_dirs[1], "ssm_du")
    for i, n in enumerate(ssm_names):
        grads[n] = jnp.stack([dparams[0][i], dparams[1][i]])

    do4 = dattn.reshape(t, KV_HEADS, GQ, HEAD_DIM).transpose(1, 2, 0, 3)
    dq4, dkp, dvp, dsink = _attn_bwd(q4, kp, vp, sink_rows, slope_rows, o4, lse4, do4, "attn_bwd")
    grads['attn_sinks'] = dsink.reshape(w['attn_sinks'].shape)
    dq = dq4.transpose(2, 0, 1, 3).reshape(t, ATTN_WIDTH)
    unpad = lambda a: a[:, QBLOCK:QBLOCK + t, :].transpose(1, 0, 2).reshape(t, KV_WIDTH)
    dproj = jnp.concatenate([dq, unpad(dkp), unpad(dvp), du], axis=-1).astype(BF16)
    dproj = dproj.reshape(t, N_CHIPS, -1).transpose(1, 0, 2)

    dw_in = _mm(h2, dproj, ta=True, out_dtype=BF16, name="in_proj_dw")
    put_grads('mix', dict(w_in=dw_in, ssm_glu_w=dwglu, w_out=dw_out))
    dh2 = _mm(dproj, w['w_in'], tb=True, reduce_s=True, name="in_proj_dx")
    dx1, dx1_b, dgm = _rms_bwd(x1, w['norm_mix'], dh2, dx2, "mix_dnorm")
    grads['norm_mix'] = dgm

    dx0, _, dg, dwg, dwu, dwd = _ffn_bwd(dx1, dx1_b, x, w['norm_ffn1'], w['ffn1_w_gate'], w['ffn1_w_up'],
                                         w['ffn1_w_down'], ffn1_saved, "ffn1")
    grads['norm_ffn1'] = dg
    put_grads('ffn1', dict(ffn1_w_gate=dwg, ffn1_w_up=dwu, ffn1_w_down=dwd))
    return loss, dx0, grads


HBM_SPEC = pl.BlockSpec(memory_space=pl.ANY)


def _chip_peers(x, y):
    return [(1 - x, y), (x, 1 - y), (1 - x, 1 - y)]


HBM_ONLY = pl.BlockSpec(memory_space=pltpu.HBM)
SEM_SPEC = pl.BlockSpec(memory_space=pltpu.SEMAPHORE)
EFFECT = pltpu.SideEffectType.DATAFLOW_SIDE_EFFECTING


def _place_own(srcs, scatter, name):
    nw = len(srcs)

    def body(*refs):
        ins, outs, sems = refs[:nw], refs[nw:2 * nw], refs[2 * nw]
        me = 2 * lax.axis_index("x") + lax.axis_index("y")
        cps = [pltpu.make_async_copy(ins[i].at[me] if scatter else ins[i], outs[i].at[me], sems.at[i])
               for i in range(nw)]
        for cp in cps:
            cp.start()
        for cp in cps:
            cp.wait()

    shapes = [jax.ShapeDtypeStruct(s.shape if scatter else (N_CHIPS,) + s.shape, s.dtype) for s in srcs]
    return pl.pallas_call(body, out_shape=shapes, in_specs=[HBM_SPEC] * nw, out_specs=[HBM_SPEC] * nw,
                          scratch_shapes=[pltpu.SemaphoreType.DMA((nw,))], name=name)(*srcs)


def _chip_copies(srcs, lands, send_sems, recv_sems, scatter, landed):
    x, y, c = lax.axis_index("x"), lax.axis_index("y"), lax.axis_index("c")
    me = 2 * x + y
    out = []
    for i in range(len(srcs)):
        for j, (px, py) in enumerate(_chip_peers(x, y)):
            p = 2 * px + py
            src = srcs[i].at[p] if scatter else srcs[i]
            out.append(pltpu.make_async_remote_copy(src, lands[i].at[p if landed else me], send_sems.at[3 * i + j],
                                                    recv_sems.at[3 * i + j], device_id=(px, py, c),
                                                    device_id_type=MESH))
    return out


def _exchange_start(groups, scatter, name):
    sizes = [len(srcs) for srcs, _ in groups]
    flat_src = [a for srcs, _ in groups for a in srcs]
    flat_land = [a for _, lands in groups for a in lands]
    n = len(flat_src)
    ng = len(groups)

    def body(*refs):
        src_refs, land_refs = refs[:n], refs[n:2 * n]
        sems = refs[2 * n:2 * n + 2 * ng]
        off = 0
        for gi, sz in enumerate(sizes):
            for cp in _chip_copies(src_refs[off:off + sz], land_refs[off:off + sz], sems[2 * gi], sems[2 * gi + 1],
                                   scatter, landed=False):
                cp.start()
            off += sz

    sem_shapes = []
    for sz in sizes:
        sem_shapes += [pltpu.SemaphoreType.DMA((3 * sz,)), pltpu.SemaphoreType.DMA((3 * sz,))]
    hbm = lambda a: pltpu.HBM(a.shape, a.dtype)
    res = pl.pallas_call(
        body, name=name,
        out_shape=tuple(sem_shapes) + tuple(hbm(a) for a in flat_src) + tuple(hbm(a) for a in flat_land),
        in_specs=[HBM_ONLY] * (2 * n), out_specs=tuple([SEM_SPEC] * (2 * ng) + [HBM_ONLY] * (2 * n)),
        input_output_aliases={k: 2 * ng + k for k in range(2 * n)},
        compiler_params=pltpu.CompilerParams(has_side_effects=EFFECT),
    )(*[pltpu.with_memory_space_constraint(a, pltpu.HBM) for a in flat_src + flat_land])
    sems, thru_src, thru_land = res[:2 * ng], res[2 * ng:2 * ng + n], res[2 * ng + n:]
    out, off = [], 0
    for gi, sz in enumerate(sizes):
        out.append((sems[2 * gi], sems[2 * gi + 1], list(thru_src[off:off + sz]), list(thru_land[off:off + sz])))
        off += sz
    return out


def _exchange_wait(started, after, scatter, name):
    send_sems, recv_sems, srcs, lands = started
    n = len(srcs)

    def body(*refs):
        src_refs, land_refs = refs[:n], refs[n:2 * n]
        send_ref, recv_ref = refs[2 * n], refs[2 * n + 1]
        for cp in _chip_copies(src_refs, land_refs, send_ref, recv_ref, scatter, landed=True):
            cp.wait_send()
            cp.wait_recv()

    hbm = lambda a: pltpu.HBM(a.shape, a.dtype)
    res = pl.pallas_call(
        body, name=name, out_shape=tuple(hbm(a) for a in srcs) + tuple(hbm(a) for a in lands),
        in_specs=[HBM_ONLY] * (2 * n) + [SEM_SPEC, SEM_SPEC, HBM_SPEC], out_specs=tuple([HBM_ONLY] * (2 * n)),
        input_output_aliases={k: k for k in range(2 * n)},
        compiler_params=pltpu.CompilerParams(has_side_effects=EFFECT),
    )(*srcs, *lands, send_sems, recv_sems, after)
    return list(res[n:])


def _small_exchange(smalls, name):
    nsm = len(smalls)
    rels = [(fx, fy, fc) for fx in (0, 1) for fy in (0, 1) for fc in (0, 1)][1:]

    def body(*refs):
        sins, souts = refs[:nsm], refs[nsm:2 * nsm]
        ssend, srecv, slocal = refs[2 * nsm:]
        x, y, c = lax.axis_index("x"), lax.axis_index("y"), lax.axis_index("c")
        lin = 4 * x + 2 * y + c
        local = [pltpu.make_async_copy(sins[i], souts[i].at[lin], slocal.at[i]) for i in range(nsm)]
        for cp in local:
            cp.start()
        for i in range(nsm):
            for j, (fx, fy, fc) in enumerate(rels):
                pltpu.make_async_remote_copy(sins[i], souts[i].at[lin], ssend.at[i, j], srecv.at[i, j],
                                             device_id=(x ^ fx, y ^ fy, c ^ fc), device_id_type=MESH).start()
        for i in range(nsm):
            for j, (fx, fy, fc) in enumerate(rels):
                src = 4 * (x ^ fx) + 2 * (y ^ fy) + (c ^ fc)
                pltpu.make_async_remote_copy(sins[i], souts[i].at[src], ssend.at[i, j], srecv.at[i, j],
                                             device_id=(x ^ fx, y ^ fy, c ^ fc), device_id_type=MESH).wait()
        for cp in local:
            cp.wait()

    return pl.pallas_call(
        body, out_shape=[jax.ShapeDtypeStruct((N_DEV,) + s.shape, s.dtype) for s in smalls],
        in_specs=[HBM_SPEC] * nsm, out_specs=[HBM_SPEC] * nsm,
        scratch_shapes=[pltpu.SemaphoreType.DMA((nsm, 7)), pltpu.SemaphoreType.DMA((nsm, 7)),
                        pltpu.SemaphoreType.DMA((nsm,))],
        name=name)(*smalls)


def _sibling_swap(arrs, name):
    nw = len(arrs)

    def body(*refs):
        ins, outs = refs[:nw], refs[nw:2 * nw]
        send_sems, recv_sems = refs[2 * nw:]
        x, y, c = lax.axis_index("x"), lax.axis_index("y"), lax.axis_index("c")
        cps = [pltpu.make_async_remote_copy(ins[i], outs[i], send_sems.at[i], recv_sems.at[i],
                                            device_id=(x, y, 1 - c), device_id_type=MESH) for i in range(nw)]
        for cp in cps:
            cp.start()
        for cp in cps:
            cp.wait()

    return pl.pallas_call(
        body, out_shape=[jax.ShapeDtypeStruct(a.shape, a.dtype) for a in arrs],
        in_specs=[HBM_SPEC] * nw, out_specs=[HBM_SPEC] * nw,
        scratch_shapes=[pltpu.SemaphoreType.DMA((nw,)), pltpu.SemaphoreType.DMA((nw,))],
        name=name)(*arrs)


def _sum_blocks(a, name):
    s, r, c = a.shape
    tr = _row_tile(r)

    def body(a_ref, o_ref):
        acc = a_ref[0].astype(F32)
        for k in range(1, s):
            acc = acc + a_ref[k].astype(F32)
        o_ref[...] = acc

    return pl.pallas_call(body, out_shape=jax.ShapeDtypeStruct((r, c), F32), grid=(r // tr,),
                          in_specs=[pl.BlockSpec((s, tr, c), lambda i: (0, i, 0))],
                          out_specs=pl.BlockSpec((tr, c), lambda i: (i, 0)),
                          compiler_params=_cparams(("parallel",)), name=name)(a)


def _adamw_math(w, m, v, g):
    nm = ADAM_B1 * m + (1.0 - ADAM_B1) * g
    nv = ADAM_B2 * v + (1.0 - ADAM_B2) * (g * g)
    m_hat = nm * (1.0 / (1.0 - ADAM_B1 ** ADAM_STEP))
    v_hat = nv * (1.0 / (1.0 - ADAM_B2 ** ADAM_STEP))
    return -ADAM_LR * (m_hat / (jnp.sqrt(v_hat) + ADAM_EPS) + ADAM_WD * w), nm, nv


def _adamw(w, m, v, g_mine, g_other, name):
    r, c = w.shape
    tr = _row_tile(r)

    def body(w_ref, m_ref, v_ref, g1_ref, g2_ref, g_ref, d_ref, nm_ref, nv_ref):
        g = g1_ref[...] + g2_ref[...]
        g_ref[...] = g
        d_ref[...], nm_ref[...], nv_ref[...] = _adamw_math(w_ref[...], m_ref[...], v_ref[...], g)

    blk = pl.BlockSpec((tr, c), lambda i: (i, 0))
    sh = jax.ShapeDtypeStruct((r, c), F32)
    return pl.pallas_call(body, out_shape=(sh, sh, sh, sh), grid=(r // tr,), in_specs=[blk] * 5,
                          out_specs=(blk, blk, blk, blk), compiler_params=_cparams(("parallel",)),
                          name=name)(w, m, v, g_mine, g_other)


def _adamw_small(ws, ms, vs, alls, split, name):
    n = len(ws)
    lead = split if split is not None else ()
    nl = len(lead)
    nslots = alls[0].shape[0]

    def blocks(shape):
        if split is None:
            return tuple(shape), (lambda *g: (0,) * len(shape))
        blk = (shape[0], shape[1] // lead[0], shape[2] // lead[1]) + tuple(shape[3:])
        return blk, (lambda *g: (0, g[0], g[1]) + (0,) * (len(shape) - 3))

    def body(*refs):
        w_refs, m_refs, v_refs, a_refs = (refs[k * n:(k + 1) * n] for k in range(4))
        g_refs, d_refs, nm_refs, nv_refs = (refs[(4 + k) * n:(5 + k) * n] for k in range(4))
        k = pl.program_id(nl)
        for i in range(n):
            @pl.when(k == 0)
            def _(i=i):
                g_refs[i][...] = a_refs[i][0]

            @pl.when(k > 0)
            def _(i=i):
                g_refs[i][...] += a_refs[i][0]

            @pl.when(k == nslots - 1)
            def _(i=i):
                d_refs[i][...], nm_refs[i][...], nv_refs[i][...] = _adamw_math(
                    w_refs[i][...], m_refs[i][...], v_refs[i][...], g_refs[i][...])

    specs, aspecs, shapes = [], [], []
    for wa in ws:
        blk, imap = blocks(wa.shape)
        specs.append(pl.BlockSpec(blk, imap))
        aspecs.append(pl.BlockSpec((1,) + blk, (lambda *g, imap=imap: (g[nl],) + imap(*g))))
        shapes.append(jax.ShapeDtypeStruct(wa.shape, F32))
    res = pl.pallas_call(
        body, out_shape=shapes * 4, grid=tuple(lead) + (nslots,), in_specs=specs * 3 + aspecs,
        out_specs=specs * 4, compiler_params=_cparams(("parallel",) * nl + ("arbitrary",)),
        name=name)(*ws, *ms, *vs, *alls)
    return res[:n], res[n:2 * n], res[2 * n:3 * n], res[3 * n:]


def kernel(x, norm_ffn1, ffn1_w_gate, ffn1_w_up, ffn1_w_down, norm_mix, w_in, attn_sinks, ssm_lambda_re, ssm_lambda_im, ssm_log_dt, ssm_b_re, ssm_b_im, ssm_c_re, ssm_c_im, ssm_d, ssm_glu_w, ssm_glu_b, attn_out_norm, ssm_out_norm, w_out, norm_ffn2, ffn2_w_gate, ffn2_w_up, ffn2_w_down, final_norm, loss_target, m_norm_ffn1, m_ffn1_w_gate, m_ffn1_w_up, m_ffn1_w_down, m_norm_mix, m_w_in, m_attn_sinks, m_ssm_lambda_re, m_ssm_lambda_im, m_ssm_log_dt, m_ssm_b_re, m_ssm_b_im, m_ssm_c_re, m_ssm_c_im, m_ssm_d, m_ssm_glu_w, m_ssm_glu_b, m_attn_out_norm, m_ssm_out_norm, m_w_out, m_norm_ffn2, m_ffn2_w_gate, m_ffn2_w_up, m_ffn2_w_down, m_final_norm, v_norm_ffn1, v_ffn1_w_gate, v_ffn1_w_up, v_ffn1_w_down, v_norm_mix, v_w_in, v_attn_sinks, v_ssm_lambda_re, v_ssm_lambda_im, v_ssm_log_dt, v_ssm_b_re, v_ssm_b_im, v_ssm_c_re, v_ssm_c_im, v_ssm_d, v_ssm_glu_w, v_ssm_glu_b, v_attn_out_norm, v_ssm_out_norm, v_w_out, v_norm_ffn2, v_ffn2_w_gate, v_ffn2_w_up, v_ffn2_w_down, v_final_norm):
    given = dict(locals())
    wts = {n: given[n] for n in WEIGHTS}

    order = [g for g in GROUPS]
    shards = {n: wts[n][0].astype(BF16) for n in BIG}
    placed = dict(zip(BIG, _place_own([shards[n] for n in BIG], False, "weights_place")))
    started = _exchange_start([([shards[n] for n in GROUPS[g]], [placed[n] for n in GROUPS[g]]) for g in order],
                              False, "weights_start")
    started = dict(zip(order, started))

    def get_weights(group, after):
        lands = _exchange_wait(started[group], after, False, f"weights_wait_{group}")
        out = dict(zip(GROUPS[group], lands))
        for n in ('ssm_glu_w', 'w_out'):
            if n in out:
                out[n] = out[n].reshape(-1, out[n].shape[-1])
        return out

    sent = {}

    def put_grads(group, gd):
        parts = []
        for n in GROUPS[group]:
            g = gd[n]
            if g.ndim == 2:
                g = g.reshape(N_CHIPS, g.shape[0] // N_CHIPS, g.shape[1])
            parts.append(g.astype(BF16))
        lands = _place_own(parts, True, f"grads_place_{group}")
        sent[group] = _exchange_start([(parts, lands)], True, f"grads_start_{group}")[0]

    w = {n: (wts[n][0] if wts[n].ndim > 1 else wts[n]) for n in SMALL}
    w['norm_ffn1'], w['norm_mix'], w['norm_ffn2'] = wts['norm_ffn1'], wts['norm_mix'], wts['norm_ffn2']
    loss, dx, grads = _local_step(x[0], loss_target[0], w, get_weights, put_grads)
    loss = lax.psum(loss, ("x", "y", "c"))

    out_g, out_d, out_m, out_v = {}, {}, {}, {}

    nat = {n: (1, wts[n].shape[0]) if wts[n].ndim == 1 else wts[n].shape for n in SMALL}
    wide = ['ssm_b_re', 'ssm_b_im', 'ssm_c_re', 'ssm_c_im']
    narrow = [n for n in SMALL if n not in wide]
    packed = jnp.concatenate([grads[n].reshape(-1, LANES) for n in wide])
    alls = _small_exchange([grads[n].reshape(nat[n]) for n in narrow] + [packed], "small_grads_allgather")
    wide_sum = _sum_blocks(alls[-1], "small_grads_sum")
    rows = wide_sum.shape[0] // len(wide)
    wide_g = [wide_sum[i * rows:(i + 1) * rows].reshape((1,) + nat[n]) for i, n in enumerate(wide)]
    for group, gs, split, tag in ((narrow, alls[:-1], None, "adamw_small"), (wide, wide_g, (2, 4), "adamw_ssm_bc")):
        res = _adamw_small([wts[n].reshape(nat[n]) for n in group], [given['m_' + n].reshape(nat[n]) for n in group],
                           [given['v_' + n].reshape(nat[n]) for n in group], gs, split, tag)
        for dst, vals in zip((out_g, out_d, out_m, out_v), res):
            for n, a in zip(group, vals):
                dst[n] = a.reshape(wts[n].shape)

    for group in ('ffn2', 'mix', 'ffn1'):
        names = GROUPS[group]
        recv = _exchange_wait(sent[group], dx, True, f"grads_wait_{group}")
        chip_sums = [_sum_blocks(r, f"grad_sum_{n}") for n, r in zip(names, recv)]
        other = _sibling_swap(chip_sums, f"grad_sibling_swap_{group}")
        for n, mine, oth in zip(names, chip_sums, other):
            g, d, nm, nv = _adamw(wts[n][0], given['m_' + n][0], given['v_' + n][0], mine, oth, f"adamw_{n}")
            out_g[n], out_d[n], out_m[n], out_v[n] = (a.reshape(wts[n].shape) for a in (g, d, nm, nv))

    return (loss, dx[None], *[out_g[n] for n in WEIGHTS], *[out_d[n] for n in WEIGHTS],
            *[out_m[n] for n in WEIGHTS], *[out_v[n] for n in WEIGHTS])
```

```python
import functools
import math

import numpy as np
import jax
import jax.numpy as jnp
from jax import lax
from jax.experimental import pallas as pl
from jax.experimental.pallas import tpu as pltpu

F32 = jnp.float32
BF16 = jnp.bfloat16
MESH = pl.DeviceIdType.MESH

EPS = 1e-6
NEG_INF = -1e30
LAMBDA_RE_MAX = -1e-4
ATTN_HEADS = 8
KV_HEADS = 2
GQ = ATTN_HEADS // KV_HEADS
HEAD_DIM = 64
ATTN_WIDTH = 512
KV_WIDTH = 128
WINDOW = 128
QBLOCK = 128
SSM_WIDTH = 512
SSM_GROUPS = 32
SSM_CH = 16
SSM_STATE = 64
N_STRIPS = 4
STRIP_IN = SSM_WIDTH // N_STRIPS
STRIP_ST = SSM_GROUPS * SSM_STATE // N_STRIPS
SUBLANES = 8
LANES = 128
N_CHIPS = 4
N_DEV = 8

ADAM_LR = 0.001
ADAM_B1 = 0.9
ADAM_B2 = 0.999
ADAM_EPS = 1e-08
ADAM_WD = 0.01
ADAM_STEP = 10

VMEM_LIMIT = 48 * 1024 * 1024

WEIGHTS = ['norm_ffn1', 'ffn1_w_gate', 'ffn1_w_up', 'ffn1_w_down', 'norm_mix', 'w_in', 'attn_sinks',
           'ssm_lambda_re', 'ssm_lambda_im', 'ssm_log_dt', 'ssm_b_re', 'ssm_b_im', 'ssm_c_re', 'ssm_c_im',
           'ssm_d', 'ssm_glu_w', 'ssm_glu_b', 'attn_out_norm', 'ssm_out_norm', 'w_out', 'norm_ffn2',
           'ffn2_w_gate', 'ffn2_w_up', 'ffn2_w_down', 'final_norm']
BIG = ['ffn1_w_gate', 'ffn1_w_up', 'ffn1_w_down', 'w_in', 'ssm_glu_w', 'w_out',
       'ffn2_w_gate', 'ffn2_w_up', 'ffn2_w_down']
SMALL = [n for n in WEIGHTS if n not in BIG]
GROUPS = {'ffn1': ['ffn1_w_gate', 'ffn1_w_up', 'ffn1_w_down'],
          'mix': ['w_in', 'ssm_glu_w', 'w_out'],
          'ffn2': ['ffn2_w_gate', 'ffn2_w_up', 'ffn2_w_down']}


def _cparams(sem=None):
    return pltpu.CompilerParams(dimension_semantics=sem, vmem_limit_bytes=VMEM_LIMIT)


def _tile(n, pref):
    if n <= pref:
        return n
    for t in (pref, pref // 2, pref // 4):
        if t % LANES == 0 and n % t == 0:
            return t
    return n


def _sigmoid(x):
    return 1.0 / (1.0 + jnp.exp(-x))


def _mm(a, b, *, ta=False, tb=False, reduce_s=False, res=None, scale=1.0, out_dtype=F32, after=None, name):
    a3 = a if a.ndim == 3 else a[None]
    b3 = b if b.ndim == 3 else b[None]
    sa, sb = a3.shape[0], b3.shape[0]
    ns = max(sa, sb)
    (kk, m) = a3.shape[1:] if ta else a3.shape[1:][::-1]
    (n, kb) = b3.shape[1:] if tb else b3.shape[1:][::-1]
    assert kk == kb, (a3.shape, b3.shape)
    tm, tn, tk = _tile(m, 1024), _tile(n, 1024), _tile(kk, 2048)
    nm, nn, nk = m // tm, n // tn, kk // tk
    has_res = res is not None
    single = nk == 1 and not (reduce_s and ns > 1)

    if reduce_s:
        grid = (nm, nn, ns, nk)
        ids = lambda i, j, s, k: (s, i, j, k)
        sem = ("parallel", "parallel", "arbitrary", "arbitrary")
    else:
        grid = (ns, nm, nn, nk)
        ids = lambda s, i, j, k: (s, i, j, k)
        sem = ("parallel", "parallel", "parallel", "arbitrary")

    def a_map(*g):
        s, i, j, k = ids(*g)
        s = s if sa > 1 else 0
        return (s, k, i) if ta else (s, i, k)

    def b_map(*g):
        s, i, j, k = ids(*g)
        s = s if sb > 1 else 0
        return (s, j, k) if tb else (s, k, j)

    def o_map(*g):
        s, i, j, k = ids(*g)
        return (i, j) if reduce_s else (s, i, j)

    a_blk = (1, tk, tm) if ta else (1, tm, tk)
    b_blk = (1, tn, tk) if tb else (1, tk, tn)
    dims = (((0 if ta else 1,), (1 if tb else 0,)), ((), ()))

    def body(*refs):
        a_ref, b_ref = refs[0], refs[1]
        r_ref = refs[2] if has_res else None
        o_ref = refs[2 + has_res + (after is not None)]
        acc_ref = None if single else refs[-1]
        s, _, _, k = ids(*[pl.program_id(d) for d in range(4)])
        prod = lax.dot_general(a_ref[0].astype(BF16), b_ref[0].astype(BF16), dims, preferred_element_type=F32)

        def finish(out):
            if scale != 1.0:
                out = out * scale
            if has_res:
                out = r_ref[...].reshape(out.shape) + out
            o_ref[...] = out.astype(out_dtype).reshape(o_ref.shape)

        if single:
            finish(prod)
            return
        if reduce_s:
            first = jnp.logical_and(s == 0, k == 0)
            last = jnp.logical_and(s == ns - 1, k == nk - 1)
        else:
            first, last = k == 0, k == nk - 1

        @pl.when(first)
        def _():
            acc_ref[...] = prod

        @pl.when(jnp.logical_not(first))
        def _():
            acc_ref[...] += prod

        @pl.when(last)
        def _():
            finish(acc_ref[...])

    in_specs = [pl.BlockSpec(a_blk, a_map), pl.BlockSpec(b_blk, b_map)]
    args = [a3, b3]
    if reduce_s:
        out_shape = jax.ShapeDtypeStruct((m, n), out_dtype)
        o_spec = pl.BlockSpec((tm, tn), o_map)
    else:
        out_shape = jax.ShapeDtypeStruct((ns, m, n), out_dtype)
        o_spec = pl.BlockSpec((1, tm, tn), o_map)
    if has_res:
        assert res.shape == out_shape.shape
        in_specs.append(o_spec)
        args.append(res)
    if after is not None:
        in_specs.append(HBM_SPEC)
        args.append(after)
    return pl.pallas_call(body, out_shape=out_shape, grid=grid, in_specs=in_specs, out_specs=o_spec,
                          scratch_shapes=[] if single else [pltpu.VMEM((tm, tn), F32)],
                          compiler_params=_cparams(sem), name=name)(*args)


def _row_tile(t):
    for tr in (256, 128, 64, 32, 16, 8):
        if t % tr == 0:
            return tr
    return t


def _rms_fwd(x, g, name):
    t, w = x.shape
    tr = _row_tile(t)

    def body(x_ref, g_ref, o_ref):
        xv = x_ref[...]
        r = lax.rsqrt(jnp.mean(xv * xv, axis=-1, keepdims=True) + EPS)
        o_ref[...] = (xv * r * g_ref[...]).astype(BF16)

    return pl.pallas_call(
        body, out_shape=jax.ShapeDtypeStruct((t, w), BF16), grid=(t // tr,),
        in_specs=[pl.BlockSpec((tr, w), lambda i: (i, 0)), pl.BlockSpec((1, w), lambda i: (0, 0))],
        out_specs=pl.BlockSpec((tr, w), lambda i: (i, 0)), compiler_params=_cparams(("parallel",)),
        name=name)(x, g)


def _rms_bwd_rows(xv, gv, dhv):
    r = lax.rsqrt(jnp.mean(xv * xv, axis=-1, keepdims=True) + EPS)
    nrm = xv * r
    dn = dhv * gv
    return r * (dn - nrm * jnp.mean(dn * nrm, axis=-1, keepdims=True)), dhv * nrm


def _rms_bwd(x, g, dh, dres, name):
    t, w = x.shape
    tr = _row_tile(t)
    has_res = dres is not None

    def body(*refs):
        if has_res:
            x_ref, g_ref, dh_ref, dr_ref, dx_ref, dxb_ref, dg_ref = refs
        else:
            x_ref, g_ref, dh_ref, dx_ref, dxb_ref, dg_ref = refs
        dx, dgs = _rms_bwd_rows(x_ref[...], g_ref[...], dh_ref[...])
        if has_res:
            dx = dx + dr_ref[...]
        dx_ref[...] = dx
        dxb_ref[...] = dx.astype(BF16)

        @pl.when(pl.program_id(0) == 0)
        def _():
            dg_ref[...] = jnp.zeros_like(dg_ref)

        dg_ref[...] += jnp.sum(dgs, axis=0, keepdims=True)

    row = pl.BlockSpec((tr, w), lambda i: (i, 0))
    vec = pl.BlockSpec((1, w), lambda i: (0, 0))
    ins = [x, g, dh] + ([dres] if has_res else [])
    return pl.pallas_call(
        body, out_shape=(jax.ShapeDtypeStruct((t, w), F32), jax.ShapeDtypeStruct((t, w), BF16),
                         jax.ShapeDtypeStruct((1, w), F32)),
        grid=(t // tr,), in_specs=[row, vec, row] + ([row] if has_res else []),
        out_specs=(row, row, vec), compiler_params=_cparams(("arbitrary",)), name=name)(*ins)


FFN_ROWS = 512


def _ffn_fwd_call(x, g, wg, wu, wd, name):
    t, d = x.shape
    ns, _, f = wg.shape
    tm = _tile(t, FFN_ROWS)

    def body(x_ref, g_ref, wg_ref, wu_ref, wd_ref, xo_ref, h_ref, gate_ref, up_ref, h_sc, acc_ref):
        s = pl.program_id(1)

        @pl.when(s == 0)
        def _():
            xv = x_ref[...]
            r = lax.rsqrt(jnp.mean(xv * xv, axis=-1, keepdims=True) + EPS)
            hb = (xv * r * g_ref[...]).astype(BF16)
            h_sc[...] = hb
            h_ref[...] = hb

        hb = h_sc[...]
        gate = jnp.dot(hb, wg_ref[0], preferred_element_type=F32)
        up = jnp.dot(hb, wu_ref[0], preferred_element_type=F32)
        gate_ref[0] = gate.astype(BF16)
        up_ref[0] = up.astype(BF16)
        act = (gate * _sigmoid(gate) * up).astype(BF16)
        prod = jnp.dot(act, wd_ref[0], preferred_element_type=F32)

        @pl.when(s == 0)
        def _():
            acc_ref[...] = prod

        @pl.when(s > 0)
        def _():
            acc_ref[...] += prod

        @pl.when(s == ns - 1)
        def _():
            xo_ref[...] = x_ref[...] + 0.5 * acc_ref[...]

    row = pl.BlockSpec((tm, d), lambda i, s: (i, 0))
    vec = pl.BlockSpec((1, d), lambda i, s: (0, 0))
    wcol = pl.BlockSpec((1, d, f), lambda i, s: (s, 0, 0))
    wrow = pl.BlockSpec((1, f, d), lambda i, s: (s, 0, 0))
    hid = pl.BlockSpec((1, tm, f), lambda i, s: (s, i, 0))
    hid_sh = jax.ShapeDtypeStruct((ns, t, f), BF16)
    return pl.pallas_call(
        body, out_shape=(jax.ShapeDtypeStruct((t, d), F32), jax.ShapeDtypeStruct((t, d), BF16), hid_sh, hid_sh),
        grid=(t // tm, ns), in_specs=[row, vec, wcol, wcol, wrow], out_specs=(row, row, hid, hid),
        scratch_shapes=[pltpu.VMEM((tm, d), BF16), pltpu.VMEM((tm, d), F32)],
        compiler_params=_cparams(("parallel", "arbitrary")), name=name)(x, g, wg, wu, wd)


def _ffn_bwd_x_call(dxo, dxo_b, x, g, gate, up, wg, wu, wd, name):
    t, d = x.shape
    ns, _, f = wg.shape
    tm = _tile(t, FFN_ROWS)
    nt = (((1,), (1,)), ((), ()))

    def body(dxo_ref, dxb_ref, x_ref, g_ref, gate_ref, up_ref, wg_ref, wu_ref, wd_ref,
             dx_ref, dxob_ref, dgn_ref, dgate_ref, dup_ref, act_ref, dh_ref):
        i, s = pl.program_id(0), pl.program_id(1)
        dact = lax.dot_general(dxb_ref[...], wd_ref[0], nt, preferred_element_type=F32) * 0.5
        gv = gate_ref[0].astype(F32)
        uv = up_ref[0].astype(F32)
        sg = _sigmoid(gv)
        silu = gv * sg
        act_ref[0] = (silu * uv).astype(BF16)
        dub = (dact * silu).astype(BF16)
        dgb = (dact * uv * sg * (1.0 + gv * (1.0 - sg))).astype(BF16)
        dup_ref[0] = dub
        dgate_ref[0] = dgb
        prod = (lax.dot_general(dgb, wg_ref[0], nt, preferred_element_type=F32)
                + lax.dot_general(dub, wu_ref[0], nt, preferred_element_type=F32))

        @pl.when(s == 0)
        def _():
            dh_ref[...] = prod

        @pl.when(s > 0)
        def _():
            dh_ref[...] += prod

        @pl.when(jnp.logical_and(i == 0, s == 0))
        def _():
            dgn_ref[...] = jnp.zeros_like(dgn_ref)

        @pl.when(s == ns - 1)
        def _():
            dx, dgs = _rms_bwd_rows(x_ref[...], g_ref[...], dh_ref[...])
            dx = dx + dxo_ref[...]
            dx_ref[...] = dx
            dxob_ref[...] = dx.astype(BF16)
            dgn_ref[...] += jnp.sum(dgs, axis=0, keepdims=True)

    row = pl.BlockSpec((tm, d), lambda i, s: (i, 0))
    vec = pl.BlockSpec((1, d), lambda i, s: (0, 0))
    wcol = pl.BlockSpec((1, d, f), lambda i, s: (s, 0, 0))
    wrow = pl.BlockSpec((1, f, d), lambda i, s: (s, 0, 0))
    hid = pl.BlockSpec((1, tm, f), lambda i, s: (s, i, 0))
    hid_sh = jax.ShapeDtypeStruct((ns, t, f), BF16)
    return pl.pallas_call(
        body,
        out_shape=(jax.ShapeDtypeStruct((t, d), F32), jax.ShapeDtypeStruct((t, d), BF16),
                   jax.ShapeDtypeStruct((1, d), F32), hid_sh, hid_sh, hid_sh),
        grid=(t // tm, ns), in_specs=[row, row, row, vec, hid, hid, wcol, wcol, wrow],
        out_specs=(row, row, vec, hid, hid, hid), scratch_shapes=[pltpu.VMEM((tm, d), F32)],
        compiler_params=_cparams(("arbitrary", "arbitrary")), name=name)(dxo, dxo_b, x, g, gate, up, wg, wu, wd)


def _ffn_bwd_w_call(h, dxo_b, dgate, dup, act, name):
    t, d = h.shape
    ns, _, f = dgate.shape
    tm = _tile(t, FFN_ROWS)
    nm = t // tm
    tn = (((0,), (0,)), ((), ()))

    def body(h_ref, dxb_ref, dgate_ref, dup_ref, act_ref, dwg_ref, dwu_ref, dwd_ref, ag_ref, au_ref, ad_ref):
        i = pl.program_id(1)
        hv = h_ref[...]
        pg = lax.dot_general(hv, dgate_ref[0], tn, preferred_element_type=F32)
        pu = lax.dot_general(hv, dup_ref[0], tn, preferred_element_type=F32)
        pd = lax.dot_general(act_ref[0], dxb_ref[...], tn, preferred_element_type=F32)

        @pl.when(i == 0)
        def _():
            ag_ref[...] = pg
            au_ref[...] = pu
            ad_ref[...] = pd

        @pl.when(i > 0)
        def _():
            ag_ref[...] += pg
            au_ref[...] += pu
            ad_ref[...] += pd

        @pl.when(i == nm - 1)
        def _():
            dwg_ref[0] = ag_ref[...].astype(BF16)
            dwu_ref[0] = au_ref[...].astype(BF16)
            dwd_ref[0] = (0.5 * ad_ref[...]).astype(BF16)

    row = pl.BlockSpec((tm, d), lambda s, i: (i, 0))
    hid = pl.BlockSpec((1, tm, f), lambda s, i: (s, i, 0))
    wcol = pl.BlockSpec((1, d, f), lambda s, i: (s, 0, 0))
    wrow = pl.BlockSpec((1, f, d), lambda s, i: (s, 0, 0))
    return pl.pallas_call(
        body,
        out_shape=(jax.ShapeDtypeStruct((ns, d, f), BF16), jax.ShapeDtypeStruct((ns, d, f), BF16),
                   jax.ShapeDtypeStruct((ns, f, d), BF16)),
        grid=(ns, nm), in_specs=[row, row, hid, hid, hid], out_specs=(wcol, wcol, wrow),
        scratch_shapes=[pltpu.VMEM((d, f), F32), pltpu.VMEM((d, f), F32), pltpu.VMEM((f, d), F32)],
        compiler_params=_cparams(("parallel", "arbitrary")), name=name)(h, dxo_b, dgate, dup, act)


def _loss_head(x, g, tgt, name):
    t, w = x.shape
    tr = _row_tile(t)

    def body(x_ref, g_ref, t_ref, loss_ref, dx_ref, dxb_ref, dg_ref):
        xv = x_ref[...]
        gv = g_ref[...]
        r = lax.rsqrt(jnp.mean(xv * xv, axis=-1, keepdims=True) + EPS)
        nrm = xv * r
        err = nrm * gv - t_ref[...]
        dout = err * (1.0 / w)
        dn = dout * gv
        dx = r * (dn - nrm * jnp.mean(dn * nrm, axis=-1, keepdims=True))
        dx_ref[...] = dx
        dxb_ref[...] = dx.astype(BF16)

        @pl.when(pl.program_id(0) == 0)
        def _():
            dg_ref[...] = jnp.zeros_like(dg_ref)
            loss_ref[...] = jnp.zeros_like(loss_ref)

        dg_ref[...] += jnp.sum(dout * nrm, axis=0, keepdims=True)
        part = jnp.sum(jnp.sum(err * err, axis=-1, keepdims=True) * (0.5 / w), axis=0, keepdims=True)
        loss_ref[...] += jnp.broadcast_to(part, loss_ref.shape)

    row = pl.BlockSpec((tr, w), lambda i: (i, 0))
    vec = pl.BlockSpec((1, w), lambda i: (0, 0))
    return pl.pallas_call(
        body, out_shape=(jax.ShapeDtypeStruct((1, LANES), F32), jax.ShapeDtypeStruct((t, w), F32),
                         jax.ShapeDtypeStruct((t, w), BF16), jax.ShapeDtypeStruct((1, w), F32)),
        grid=(t // tr,), in_specs=[row, vec, row],
        out_specs=(pl.BlockSpec((1, LANES), lambda i: (0, 0)), row, row, vec),
        compiler_params=_cparams(("arbitrary",)), name=name)(x, g, tgt)


def _attn_scores(q, k3, n, t, slope_ref):
    rows = GQ * QBLOCK
    s = lax.dot_general(q, k3, (((1,), (1,)), ((), ())), preferred_element_type=F32) * (HEAD_DIM ** -0.5)
    row = lax.broadcasted_iota(jnp.int32, (rows, 3 * QBLOCK), 0) & (QBLOCK - 1)
    col = lax.broadcasted_iota(jnp.int32, (rows, 3 * QBLOCK), 1)
    rel = jnp.abs(col - QBLOCK - row)
    key_pos = n * QBLOCK - QBLOCK + col
    valid = (rel <= WINDOW) & (key_pos >= 0) & (key_pos < t)
    return jnp.where(valid, s - slope_ref[0] * rel.astype(F32), NEG_INF)


def _attn_fwd(q4, kp, vp, sink_rows, slope_rows, name):
    _, _, t, _ = q4.shape
    nb = t // QBLOCK
    rows = GQ * QBLOCK

    def body(q_ref, k_ref, v_ref, sink_ref, slope_ref, o_ref, lse_ref):
        n = pl.program_id(1)
        start = pl.multiple_of(n * QBLOCK, QBLOCK)
        q = q_ref[0].reshape(rows, HEAD_DIM).astype(BF16)
        k3 = k_ref[0, pl.ds(start, 3 * QBLOCK), :].astype(BF16)
        v3 = v_ref[0, pl.ds(start, 3 * QBLOCK), :].astype(BF16)
        s = _attn_scores(q, k3, n, t, slope_ref)
        sink = sink_ref[0]
        mx = jnp.maximum(jnp.max(s, axis=-1, keepdims=True), sink)
        p = jnp.exp(s - mx)
        den = jnp.sum(p, axis=-1, keepdims=True) + jnp.exp(sink - mx)
        o = lax.dot_general(p.astype(BF16), v3, (((1,), (0,)), ((), ())), preferred_element_type=F32)
        o_ref[0] = (o / den).reshape(GQ, QBLOCK, HEAD_DIM)
        lse_ref[0] = (mx + jnp.log(den)).reshape(GQ, QBLOCK, 1)

    qspec = pl.BlockSpec((1, GQ, QBLOCK, HEAD_DIM), lambda h, n: (h, 0, n, 0))
    kvspec = pl.BlockSpec((1, t + 2 * QBLOCK, HEAD_DIM), lambda h, n: (h, 0, 0))
    rowspec = pl.BlockSpec((1, rows, 1), lambda h, n: (h, 0, 0))
    return pl.pallas_call(
        body, out_shape=(jax.ShapeDtypeStruct(q4.shape, F32), jax.ShapeDtypeStruct((KV_HEADS, GQ, t, 1), F32)),
        grid=(KV_HEADS, nb), in_specs=[qspec, kvspec, kvspec, rowspec, rowspec],
        out_specs=(qspec, pl.BlockSpec((1, GQ, QBLOCK, 1), lambda h, n: (h, 0, n, 0))),
        compiler_params=_cparams(("parallel", "parallel")), name=name)(q4, kp, vp, sink_rows, slope_rows)


def _attn_bwd(q4, kp, vp, sink_rows, slope_rows, o4, lse4, do4, name):
    _, _, t, _ = q4.shape
    nb = t // QBLOCK
    rows = GQ * QBLOCK
    scale = HEAD_DIM ** -0.5

    def body(q_ref, k_ref, v_ref, sink_ref, slope_ref, o_ref, lse_ref, do_ref, dq_ref, dk_ref, dv_ref, ds_ref):
        n = pl.program_id(1)
        start = pl.multiple_of(n * QBLOCK, QBLOCK)

        @pl.when(n == 0)
        def _():
            dk_ref[...] = jnp.zeros_like(dk_ref)
            dv_ref[...] = jnp.zeros_like(dv_ref)
            ds_ref[...] = jnp.zeros_like(ds_ref)

        q = q_ref[0].reshape(rows, HEAD_DIM).astype(BF16)
        k3 = k_ref[0, pl.ds(start, 3 * QBLOCK), :].astype(BF16)
        v3 = v_ref[0, pl.ds(start, 3 * QBLOCK), :].astype(BF16)
        do = do_ref[0].reshape(rows, HEAD_DIM)
        lse = lse_ref[0].reshape(rows, 1)
        s = _attn_scores(q, k3, n, t, slope_ref)
        p = jnp.exp(s - lse)
        delta = jnp.sum(do * o_ref[0].reshape(rows, HEAD_DIM), axis=-1, keepdims=True)
        dob = do.astype(BF16)
        dp = lax.dot_general(dob, v3, (((1,), (1,)), ((), ())), preferred_element_type=F32)
        dsb = (p * (dp - delta)).astype(BF16)
        dq = lax.dot_general(dsb, k3, (((1,), (0,)), ((), ())), preferred_element_type=F32) * scale
        dq_ref[0] = dq.reshape(GQ, QBLOCK, HEAD_DIM)
        dk3 = lax.dot_general(dsb, q, (((0,), (0,)), ((), ())), preferred_element_type=F32) * scale
        dv3 = lax.dot_general(p.astype(BF16), dob, (((0,), (0,)), ((), ())), preferred_element_type=F32)
        dk_ref[0, pl.ds(start, 3 * QBLOCK), :] += dk3
        dv_ref[0, pl.ds(start, 3 * QBLOCK), :] += dv3
        dsink_rows = -jnp.exp(sink_ref[0] - lse) * delta
        ds_ref[0] += jnp.sum(dsink_rows.reshape(GQ, QBLOCK, 1), axis=1)

    qspec = pl.BlockSpec((1, GQ, QBLOCK, HEAD_DIM), lambda h, n: (h, 0, n, 0))
    kvspec = pl.BlockSpec((1, t + 2 * QBLOCK, HEAD_DIM), lambda h, n: (h, 0, 0))
    rowspec = pl.BlockSpec((1, rows, 1), lambda h, n: (h, 0, 0))
    lsespec = pl.BlockSpec((1, GQ, QBLOCK, 1), lambda h, n: (h, 0, n, 0))
    return pl.pallas_call(
        body,
        out_shape=(jax.ShapeDtypeStruct(q4.shape, F32), jax.ShapeDtypeStruct(kp.shape, F32),
                   jax.ShapeDtypeStruct(vp.shape, F32), jax.ShapeDtypeStruct((KV_HEADS, GQ, 1), F32)),
        grid=(KV_HEADS, nb),
        in_specs=[qspec, kvspec, kvspec, rowspec, rowspec, qspec, lsespec, qspec],
        out_specs=(qspec, kvspec, kvspec, pl.BlockSpec((1, GQ, 1), lambda h, n: (h, 0, 0))),
        compiler_params=_cparams(("parallel", "arbitrary")), name=name)(
            q4, kp, vp, sink_rows, slope_rows, o4, lse4, do4)


def _scan_tables(a_re, a_im, reverse):
    pw = [(a_re, a_im)]
    for _ in range(SUBLANES - 1):
        pr, pi = pw[-1]
        pw.append((pr * a_re - pi * a_im, pr * a_im + pi * a_re))
    rows = np.arange(SUBLANES)
    tabs = []
    for d in (1, 2, 4):
        mask = (rows <= SUBLANES - 1 - d) if reverse else (rows >= d)
        m = jnp.asarray(mask, F32)[:, None]
        tabs += [m * pw[d - 1][0][None, :], m * pw[d - 1][1][None, :]]
    order = (SUBLANES - 1 - rows) if reverse else rows
    tabs += [jnp.stack([pw[j][0] for j in order]), jnp.stack([pw[j][1] for j in order])]
    tab = jnp.stack(tabs)
    return tab.reshape(8, SUBLANES, N_STRIPS, STRIP_ST).transpose(2, 0, 1, 3)


def _scan(v, mi_re, mi_im, tab, mo_re, mo_im, reverse, name):
    t = v.shape[0]
    tc = _tile(t, 256)
    nc = t // tc
    nblk = tc // SUBLANES

    def body(v_ref, mir_ref, mii_ref, tab_ref, mor_ref, moi_ref, y_ref, xr_ref, xi_ref, carry_ref):
        @pl.when(pl.program_id(1) == 0)
        def _():
            carry_ref[...] = jnp.zeros_like(carry_ref)

        vb = v_ref[...].astype(BF16)
        xr_ref[...] = jnp.dot(vb, mir_ref[0], preferred_element_type=F32)
        xi_ref[...] = jnp.dot(vb, mii_ref[0], preferred_element_type=F32)

        def blk(i, carry):
            cr, ci = carry
            b = (nblk - 1 - i) if reverse else i
            r0 = pl.multiple_of(b * SUBLANES, SUBLANES)
            xr = xr_ref[pl.ds(r0, SUBLANES), :]
            xi = xi_ref[pl.ds(r0, SUBLANES), :]
            for j, d in enumerate((1, 2, 4)):
                tr_, ti_ = tab_ref[0, 2 * j], tab_ref[0, 2 * j + 1]
                sh = (SUBLANES - d) if reverse else d
                sr = pltpu.roll(xr, sh, 0)
                si = pltpu.roll(xi, sh, 0)
                xr, xi = xr + tr_ * sr - ti_ * si, xi + tr_ * si + ti_ * sr
            pr, pi = tab_ref[0, 6], tab_ref[0, 7]
            xr, xi = xr + pr * cr - pi * ci, xi + pr * ci + pi * cr
            xr_ref[pl.ds(r0, SUBLANES), :] = xr
            xi_ref[pl.ds(r0, SUBLANES), :] = xi
            edge = 0 if reverse else SUBLANES - 1
            return (jnp.broadcast_to(xr[edge:edge + 1, :], xr.shape),
                    jnp.broadcast_to(xi[edge:edge + 1, :], xi.shape))

        cr, ci = lax.fori_loop(0, nblk, blk, (carry_ref[0], carry_ref[1]))
        carry_ref[0] = cr
        carry_ref[1] = ci
        y_ref[...] = (jnp.dot(xr_ref[...].astype(BF16), mor_ref[0], preferred_element_type=F32)
                      + jnp.dot(xi_ref[...].astype(BF16), moi_ref[0], preferred_element_type=F32))

    tmap = (lambda s, c: (nc - 1 - c, s)) if reverse else (lambda s, c: (c, s))
    smap3 = lambda s, c: (s, 0, 0)
    return pl.pallas_call(
        body,
        out_shape=(jax.ShapeDtypeStruct((t, SSM_WIDTH), F32),
                   jax.ShapeDtypeStruct((t, N_STRIPS * STRIP_ST), F32),
                   jax.ShapeDtypeStruct((t, N_STRIPS * STRIP_ST), F32)),
        grid=(N_STRIPS, nc),
        in_specs=[pl.BlockSpec((tc, STRIP_IN), tmap),
                  pl.BlockSpec((1, STRIP_IN, STRIP_ST), smap3), pl.BlockSpec((1, STRIP_IN, STRIP_ST), smap3),
                  pl.BlockSpec((1, 8, SUBLANES, STRIP_ST), lambda s, c: (s, 0, 0, 0)),
                  pl.BlockSpec((1, STRIP_ST, STRIP_IN), smap3), pl.BlockSpec((1, STRIP_ST, STRIP_IN), smap3)],
        out_specs=(pl.BlockSpec((tc, STRIP_IN), tmap), pl.BlockSpec((tc, STRIP_ST), tmap),
                   pl.BlockSpec((tc, STRIP_ST), tmap)),
        scratch_shapes=[pltpu.VMEM((2, SUBLANES, STRIP_ST), F32)],
        compiler_params=_cparams(("parallel", "arbitrary")), name=name)(v, mi_re, mi_im, tab, mo_re, mo_im)


def _scan_param_grads(v, dy, xr, xi, lr, li, reverse, name):
    t = v.shape[0]
    tc = _tile(t, 256)
    nc = t // tc
    hb = tc // SUBLANES

    def body(v_ref, dy_ref, xr_ref, xi_ref, lr_ref, li_ref, hr_ref, hi_ref,
             dmir_ref, dmii_ref, dmor_ref, dmoi_ref, da_ref):
        c = pl.program_id(1)

        @pl.when(c == 0)
        def _():
            for r in (dmir_ref, dmii_ref, dmor_ref, dmoi_ref, da_ref):
                r[...] = jnp.zeros_like(r)

        xrv, xiv, lrv, liv = xr_ref[...], xi_ref[...], lr_ref[...], li_ref[...]
        row = lax.broadcasted_iota(jnp.int32, xrv.shape, 0)
        if reverse:
            live = (c < nc - 1).astype(F32)
            edge_r, edge_i = hr_ref[0:1, :] * live, hi_ref[0:1, :] * live
            xpr = jnp.where(row == tc - 1, edge_r, pltpu.roll(xrv, tc - 1, 0))
            xpi = jnp.where(row == tc - 1, edge_i, pltpu.roll(xiv, tc - 1, 0))
        else:
            live = (c > 0).astype(F32)
            edge_r, edge_i = hr_ref[SUBLANES - 1:SUBLANES, :] * live, hi_ref[SUBLANES - 1:SUBLANES, :] * live
            xpr = jnp.where(row == 0, edge_r, pltpu.roll(xrv, 1, 0))
            xpi = jnp.where(row == 0, edge_i, pltpu.roll(xiv, 1, 0))
        da_ref[0, 0:1, :] += jnp.sum(xpr * lrv + xpi * liv, axis=0, keepdims=True)
        da_ref[0, 1:2, :] += jnp.sum(xpr * liv - xpi * lrv, axis=0, keepdims=True)
        tdims = (((0,), (0,)), ((), ()))
        vb, dyb = v_ref[...].astype(BF16), dy_ref[...].astype(BF16)
        dmir_ref[0] += lax.dot_general(vb, lrv.astype(BF16), tdims, preferred_element_type=F32)
        dmii_ref[0] += lax.dot_general(vb, liv.astype(BF16), tdims, preferred_element_type=F32)
        dmor_ref[0] += lax.dot_general(xrv.astype(BF16), dyb, tdims, preferred_element_type=F32)
        dmoi_ref[0] += lax.dot_general(xiv.astype(BF16), dyb, tdims, preferred_element_type=F32)

    tmap = lambda s, c: (c, s)
    if reverse:
        hmap = lambda s, c: (jnp.minimum((c + 1) * hb, t // SUBLANES - 1), s)
    else:
        hmap = lambda s, c: (jnp.maximum(c * hb - 1, 0), s)
    narrow = pl.BlockSpec((tc, STRIP_IN), tmap)
    wide = pl.BlockSpec((tc, STRIP_ST), tmap)
    halo = pl.BlockSpec((SUBLANES, STRIP_ST), hmap)
    smap3 = lambda s, c: (s, 0, 0)
    return pl.pallas_call(
        body,
        out_shape=(jax.ShapeDtypeStruct((N_STRIPS, STRIP_IN, STRIP_ST), F32),
                   jax.ShapeDtypeStruct((N_STRIPS, STRIP_IN, STRIP_ST), F32),
                   jax.ShapeDtypeStruct((N_STRIPS, STRIP_ST, STRIP_IN), F32),
                   jax.ShapeDtypeStruct((N_STRIPS, STRIP_ST, STRIP_IN), F32),
                   jax.ShapeDtypeStruct((N_STRIPS, SUBLANES, STRIP_ST), F32)),
        grid=(N_STRIPS, nc),
        in_specs=[narrow, narrow, wide, wide, wide, wide, halo, halo],
        out_specs=(pl.BlockSpec((1, STRIP_IN, STRIP_ST), smap3), pl.BlockSpec((1, STRIP_IN, STRIP_ST), smap3),
                   pl.BlockSpec((1, STRIP_ST, STRIP_IN), smap3), pl.BlockSpec((1, STRIP_ST, STRIP_IN), smap3),
                   pl.BlockSpec((1, SUBLANES, STRIP_ST), smap3)),
        compiler_params=_cparams(("parallel", "arbitrary")), name=name)(v, dy, xr, xi, lr, li, xr, xi)


def _ssm_prep(lam_re, lam_im, log_dt, b_re, b_im, c_re, c_im):
    lr = jnp.minimum(lam_re, LAMBDA_RE_MAX)
    li = lam_im
    dt = jnp.exp(log_dt)[:, None]
    mag = jnp.exp(lr * dt)
    a_re = mag * jnp.cos(li * dt)
    a_im = mag * jnp.sin(li * dt)
    den = lr * lr + li * li
    coef_re = ((a_re - 1.0) * lr + a_im * li) / den
    coef_im = (a_im * lr - (a_re - 1.0) * li) / den
    bb_re = coef_re[..., None] * b_re - coef_im[..., None] * b_im
    bb_im = coef_re[..., None] * b_im + coef_im[..., None] * b_re
    eye = jnp.eye(SSM_GROUPS // N_STRIPS, dtype=F32)

    def strips(m):
        g, a, b = m.shape
        m4 = m.reshape(N_STRIPS, g // N_STRIPS, a, b)
        return jnp.einsum('sgab,gk->sgakb', m4, eye).reshape(N_STRIPS, g // N_STRIPS * a, g // N_STRIPS * b)

    mi_re = strips(jnp.swapaxes(bb_re, 1, 2))
    mi_im = strips(jnp.swapaxes(bb_im, 1, 2))
    mo_re = strips(jnp.swapaxes(c_re, 1, 2))
    mo_im = strips(-jnp.swapaxes(c_im, 1, 2))
    return a_re.reshape(-1), a_im.reshape(-1), mi_re, mi_im, mo_re, mo_im


def _gelu(x):
    c = math.sqrt(2.0 / math.pi)
    return 0.5 * x * (1.0 + jnp.tanh(c * (x + 0.044715 * x * x * x)))


def _gelu_grad(x):
    c = math.sqrt(2.0 / math.pi)
    th = jnp.tanh(c * (x + 0.044715 * x * x * x))
    return 0.5 * (1.0 + th) + 0.5 * x * (1.0 - th * th) * c * (1.0 + 3.0 * 0.044715 * x * x)


def _ssm_post_fwd(u, yf, yb, d, wglu, bglu, name):
    t, w = u.shape
    tr = _row_tile(t)

    def body(u_ref, yf_ref, yb_ref, d_ref, w_ref, b_ref, s_ref, y0_ref, z_ref):
        y0 = d_ref[...] * u_ref[...] + yf_ref[...] + yb_ref[...]
        yg = _gelu(y0)
        z = jnp.dot(yg.astype(BF16), w_ref[...], preferred_element_type=F32) + b_ref[...]
        s_ref[...] = yg * _sigmoid(z)
        y0_ref[...] = y0
        z_ref[...] = z

    row = pl.BlockSpec((tr, w), lambda i: (i, 0))
    vec = pl.BlockSpec((1, w), lambda i: (0, 0))
    mat = pl.BlockSpec((w, w), lambda i: (0, 0))
    sh = jax.ShapeDtypeStruct((t, w), F32)
    return pl.pallas_call(body, out_shape=(sh, sh, sh), grid=(t // tr,),
                          in_specs=[row, row, row, vec, mat, vec], out_specs=(row, row, row),
                          compiler_params=_cparams(("parallel",)), name=name)(u, yf, yb, d, wglu, bglu)


def _ssm_post_bwd(ds, y0, z, u, d, wglu, name):
    t, w = u.shape
    tr = _row_tile(t)

    def body(ds_ref, y0_ref, z_ref, u_ref, d_ref, w_ref, dy0_ref, dw_ref, db_ref, dd_ref):
        @pl.when(pl.program_id(0) == 0)
        def _():
            dw_ref[...] = jnp.zeros_like(dw_ref)
            db_ref[...] = jnp.zeros_like(db_ref)
            dd_ref[...] = jnp.zeros_like(dd_ref)

        y0 = y0_ref[...]
        yg = _gelu(y0)
        sg = _sigmoid(z_ref[...])
        dsv = ds_ref[...]
        dz = dsv * yg * sg * (1.0 - sg)
        dzb = dz.astype(BF16)
        dyg = dsv * sg + lax.dot_general(dzb, w_ref[...], (((1,), (1,)), ((), ())), preferred_element_type=F32)
        dy0 = dyg * _gelu_grad(y0)
        dy0_ref[...] = dy0
        dw_ref[...] += lax.dot_general(yg.astype(BF16), dzb, (((0,), (0,)), ((), ())), preferred_element_type=F32)
        db_ref[...] += jnp.sum(dz, axis=0, keepdims=True)
        dd_ref[...] += jnp.sum(dy0 * u_ref[...], axis=0, keepdims=True)

    row = pl.BlockSpec((tr, w), lambda i: (i, 0))
    vec = pl.BlockSpec((1, w), lambda i: (0, 0))
    mat = pl.BlockSpec((w, w), lambda i: (0, 0))
    return pl.pallas_call(
        body, out_shape=(jax.ShapeDtypeStruct((t, w), F32), jax.ShapeDtypeStruct((w, w), F32),
                         jax.ShapeDtypeStruct((1, w), F32), jax.ShapeDtypeStruct((1, w), F32)),
        grid=(t // tr,), in_specs=[row, row, row, row, vec, mat], out_specs=(row, mat, vec, vec),
        compiler_params=_cparams(("arbitrary",)), name=name)(ds, y0, z, u, d, wglu)


def _du_combine(dy0, d, du_f, du_b, name):
    t, w = dy0.shape
    tr = _row_tile(t)

    def body(dy_ref, d_ref, a_ref, b_ref, o_ref):
        o_ref[...] = d_ref[...] * dy_ref[...] + a_ref[...] + b_ref[...]

    row = pl.BlockSpec((tr, w), lambda i: (i, 0))
    vec = pl.BlockSpec((1, w), lambda i: (0, 0))
    return pl.pallas_call(body, out_shape=jax.ShapeDtypeStruct((t, w), F32), grid=(t // tr,),
                          in_specs=[row, vec, row, row], out_specs=row, compiler_params=_cparams(("parallel",)),
                          name=name)(dy0, d, du_f, du_b)


def _ffn_fwd(x, g, wg, wu, wd, tag):
    xo, h, gate, up = _ffn_fwd_call(x, g, wg, wu, wd, f"{tag}_fwd")
    return xo, (h, gate, up)


def _ffn_bwd(dxo, dxo_b, x, g, wg, wu, wd, saved, tag):
    h, gate, up = saved
    dx, dx_b, dg, dgate, dup, act = _ffn_bwd_x_call(dxo, dxo_b, x, g, gate, up, wg, wu, wd, f"{tag}_bwd_x")
    dwg, dwu, dwd = _ffn_bwd_w_call(h, dxo_b, dgate, dup, act, f"{tag}_bwd_w")
    return dx, dx_b, dg, dwg, dwu, dwd


def _heads_split(q, k, v):
    t = q.shape[0]
    q4 = q.reshape(t, KV_HEADS, GQ, HEAD_DIM).transpose(1, 2, 0, 3)
    pad = lambda a: jnp.pad(a.reshape(t, KV_HEADS, HEAD_DIM).transpose(1, 0, 2), ((0, 0), (QBLOCK, QBLOCK), (0, 0)))
    return q4, pad(k), pad(v)


def _local_step(x, tgt, w, get_weights, put_grads):
    t = x.shape[0]
    row = lambda a: a.reshape(1, -1)
    grads = {}

    w = dict(w)
    w.update(get_weights('ffn1', x))
    x1, ffn1_saved = _ffn_fwd(x, w['norm_ffn1'], w['ffn1_w_gate'], w['ffn1_w_up'], w['ffn1_w_down'], "ffn1")
    w.update(get_weights('mix', x1))

    h2 = _rms_fwd(x1, w['norm_mix'], "mix_norm")
    proj = _mm(h2, w['w_in'], name="in_proj")
    proj = proj.transpose(1, 0, 2).reshape(t, -1)
    q = proj[:, :ATTN_WIDTH]
    k = proj[:, ATTN_WIDTH:ATTN_WIDTH + KV_WIDTH]
    v = proj[:, ATTN_WIDTH + KV_WIDTH:ATTN_WIDTH + 2 * KV_WIDTH]
    u = proj[:, ATTN_WIDTH + 2 * KV_WIDTH:]

    q4, kp, vp = _heads_split(q, k, v)
    sink_rows = jnp.repeat(w['attn_sinks'].reshape(KV_HEADS, GQ), QBLOCK, axis=1)[..., None]
    slopes = jnp.asarray(2.0 ** (-8.0 * (np.arange(ATTN_HEADS) + 1) / ATTN_HEADS), F32)
    slope_rows = jnp.repeat(slopes.reshape(KV_HEADS, GQ), QBLOCK, axis=1)[..., None]
    o4, lse4 = _attn_fwd(q4, kp, vp, sink_rows, slope_rows, "attn_fwd")
    attn = o4.transpose(2, 0, 1, 3).reshape(t, ATTN_WIDTH)

    ssm_names = ['ssm_lambda_re', 'ssm_lambda_im', 'ssm_log_dt', 'ssm_b_re', 'ssm_b_im', 'ssm_c_re', 'ssm_c_im']
    ys, states, preps, vjps = [], [], [], []
    for direction in range(2):
        params = [w[n][direction] for n in ssm_names]
        prep, vjp = jax.vjp(_ssm_prep, *params)
        a_re, a_im = prep[0], prep[1]
        mi_re, mi_im, mo_re, mo_im = (m.astype(BF16) for m in prep[2:])
        prep = (a_re, a_im, mi_re, mi_im, mo_re, mo_im)
        rev = direction == 1
        tab = _scan_tables(a_re, a_im, rev)
        y, xr, xi = _scan(u, mi_re, mi_im, tab, mo_re, mo_im, rev, f"s5_fwd{direction}")
        ys.append(y)
        states.append((xr, xi))
        preps.append(prep)
        vjps.append(vjp)
    d_row = row(w['ssm_d'])
    s, y0, z = _ssm_post_fwd(u, ys[0], ys[1], d_row, w['ssm_glu_w'], row(w['ssm_glu_b']), "ssm_post")

    ma = _rms_fwd(attn, row(w['attn_out_norm']), "attn_out_norm")
    ms = _rms_fwd(s, row(w['ssm_out_norm']), "ssm_out_norm")
    mixed = jnp.concatenate([ma, ms], axis=-1)
    x2 = _mm(mixed, w['w_out'], res=x1, reduce_s=True, name="out_proj")

    w.update(get_weights('ffn2', x2))
    x3, ffn2_saved = _ffn_fwd(x2, w['norm_ffn2'], w['ffn2_w_gate'], w['ffn2_w_up'], w['ffn2_w_down'], "ffn2")

    loss_row, dx3, dx3_b, dgf = _loss_head(x3, row(w['final_norm']), tgt, "loss_head")
    loss = loss_row[0, 0]
    grads['final_norm'] = dgf.reshape(w['final_norm'].shape)

    dx2, dx2_b, dg, dwg, dwu, dwd = _ffn_bwd(dx3, dx3_b, x2, w['norm_ffn2'], w['ffn2_w_gate'], w['ffn2_w_up'],
                                             w['ffn2_w_down'], ffn2_saved, "ffn2")
    grads['norm_ffn2'] = dg
    sent = put_grads('ffn2', dict(ffn2_w_gate=dwg, ffn2_w_up=dwu, ffn2_w_down=dwd))

    dmixed = _mm(dx2_b, w['w_out'], tb=True, reduce_s=True, after=sent, name="out_proj_dx")
    dw_out = _mm(mixed, dx2_b, ta=True, out_dtype=BF16, name="out_proj_dw")[0]
    dattn, _, dga = _rms_bwd(attn, row(w['attn_out_norm']), dmixed[:, :ATTN_WIDTH], None, "attn_out_dnorm")
    ds, _, dgs = _rms_bwd(s, row(w['ssm_out_norm']), dmixed[:, ATTN_WIDTH:], None, "ssm_out_dnorm")
    grads.update(attn_out_norm=dga, ssm_out_norm=dgs)

    dy0, dwglu, dbglu, dd = _ssm_post_bwd(ds, y0, z, u, d_row, w['ssm_glu_w'], "ssm_post_bwd")
    grads['ssm_glu_b'] = dbglu
    grads['ssm_d'] = dd.reshape(w['ssm_d'].shape)
    dparams, du_dirs = [], []
    for direction in range(2):
        a_re, a_im, mi_re, mi_im, mo_re, mo_im = preps[direction]
        rev = direction == 1
        tab = _scan_tables(a_re, -a_im, not rev)
        tr3 = lambda m: jnp.swapaxes(m, 1, 2)
        du_dir, lr, li = _scan(dy0, tr3(mo_re), tr3(mo_im), tab, tr3(mi_re), tr3(mi_im), not rev,
                               f"s5_adj{direction}")
        du_dirs.append(du_dir)
        xr, xi = states[direction]
        dmir, dmii, dmor, dmoi, da = _scan_param_grads(u, dy0, xr, xi, lr, li, rev, f"s5_pgrad{direction}")
        da_re = da[:, 0, :].reshape(-1)
        da_im = da[:, 1, :].reshape(-1)
        dparams.append(vjps[direction]((da_re, da_im, dmir, dmii, dmor, dmoi)))
    du = _du_combine(dy0, d_row, du_dirs[0], du_dirs[1], "ssm_du")
    for i, n in enumerate(ssm_names):
        grads[n] = jnp.stack([dparams[0][i], dparams[1][i]])

    do4 = dattn.reshape(t, KV_HEADS, GQ, HEAD_DIM).transpose(1, 2, 0, 3)
    dq4, dkp, dvp, dsink = _attn_bwd(q4, kp, vp, sink_rows, slope_rows, o4, lse4, do4, "attn_bwd")
    grads['attn_sinks'] = dsink.reshape(w['attn_sinks'].shape)
    dq = dq4.transpose(2, 0, 1, 3).reshape(t, ATTN_WIDTH)
    unpad = lambda a: a[:, QBLOCK:QBLOCK + t, :].transpose(1, 0, 2).reshape(t, KV_WIDTH)
    dproj = jnp.concatenate([dq, unpad(dkp), unpad(dvp), du], axis=-1).astype(BF16)
    dproj = dproj.reshape(t, N_CHIPS, -1).transpose(1, 0, 2)

    dw_in = _mm(h2, dproj, ta=True, out_dtype=BF16, name="in_proj_dw")
    sent = put_grads('mix', dict(w_in=dw_in, ssm_glu_w=dwglu, w_out=dw_out))
    dh2 = _mm(dproj, w['w_in'], tb=True, reduce_s=True, after=sent, name="in_proj_dx")
    dx1, dx1_b, dgm = _rms_bwd(x1, w['norm_mix'], dh2, dx2, "mix_dnorm")
    grads['norm_mix'] = dgm

    dx0, _, dg, dwg, dwu, dwd = _ffn_bwd(dx1, dx1_b, x, w['norm_ffn1'], w['ffn1_w_gate'], w['ffn1_w_up'],
                                         w['ffn1_w_down'], ffn1_saved, "ffn1")
    grads['norm_ffn1'] = dg
    put_grads('ffn1', dict(ffn1_w_gate=dwg, ffn1_w_up=dwu, ffn1_w_down=dwd))
    return loss, dx0, grads


HBM_SPEC = pl.BlockSpec(memory_space=pl.ANY)


def _chip_peers(x, y):
    return [(1 - x, y), (x, 1 - y), (1 - x, 1 - y)]


HBM_ONLY = pl.BlockSpec(memory_space=pltpu.HBM)
SEM_SPEC = pl.BlockSpec(memory_space=pltpu.SEMAPHORE)
EFFECT = pltpu.SideEffectType.DATAFLOW_SIDE_EFFECTING


def _place_own(src, slot, name):
    r, c = src.shape
    tr = r // 2

    def body(slot_ref, s_ref, o_ref):
        o_ref[0] = s_ref[...]

    return pl.pallas_call(
        body, out_shape=jax.ShapeDtypeStruct((N_CHIPS, r, c), src.dtype),
        grid_spec=pltpu.PrefetchScalarGridSpec(
            num_scalar_prefetch=1, grid=(2,), in_specs=[pl.BlockSpec((tr, c), lambda i, s: (i, 0))],
            out_specs=pl.BlockSpec((1, tr, c), lambda i, s: (s[0], i, 0))),
        compiler_params=_cparams(("parallel",)), name=name)(slot, src)


def _chip_copies(srcs, lands, send_sems, recv_sems, scatter, landed):
    x, y, c = lax.axis_index("x"), lax.axis_index("y"), lax.axis_index("c")
    me = 2 * x + y
    out = []
    for i in range(len(srcs)):
        for j, (px, py) in enumerate(_chip_peers(x, y)):
            p = 2 * px + py
            src = srcs[i].at[p] if scatter else srcs[i]
            out.append(pltpu.make_async_remote_copy(src, lands[i].at[p if landed else me], send_sems.at[3 * i + j],
                                                    recv_sems.at[3 * i + j], device_id=(px, py, c),
                                                    device_id_type=MESH))
    return out


def _exchange_start(groups, scatter, name):
    sizes = [len(srcs) for srcs, _ in groups]
    flat_src = [a for srcs, _ in groups for a in srcs]
    flat_land = [a for _, lands in groups for a in lands]
    n = len(flat_src)
    ng = len(groups)

    def body(*refs):
        src_refs, land_refs = refs[:n], refs[n:2 * n]
        sems = refs[2 * n:2 * n + 2 * ng]
        token_ref = refs[-1]
        off = 0
        for gi, sz in enumerate(sizes):
            for cp in _chip_copies(src_refs[off:off + sz], land_refs[off:off + sz], sems[2 * gi], sems[2 * gi + 1],
                                   scatter, landed=False):
                cp.start()
            off += sz
        token_ref[...] = jnp.zeros_like(token_ref)

    sem_shapes = []
    for sz in sizes:
        sem_shapes += [pltpu.SemaphoreType.DMA((3 * sz,)), pltpu.SemaphoreType.DMA((3 * sz,))]
    hbm = lambda a: pltpu.HBM(a.shape, a.dtype)
    res = pl.pallas_call(
        body, name=name,
        out_shape=(tuple(sem_shapes) + tuple(hbm(a) for a in flat_src) + tuple(hbm(a) for a in flat_land)
                   + (jax.ShapeDtypeStruct((SUBLANES, LANES), F32),)),
        in_specs=[HBM_ONLY] * (2 * n),
        out_specs=tuple([SEM_SPEC] * (2 * ng) + [HBM_ONLY] * (2 * n) + [pl.BlockSpec(memory_space=pltpu.VMEM)]),
        input_output_aliases={k: 2 * ng + k for k in range(2 * n)},
        compiler_params=pltpu.CompilerParams(has_side_effects=EFFECT),
    )(*[pltpu.with_memory_space_constraint(a, pltpu.HBM) for a in flat_src + flat_land])
    sems, thru_src, thru_land = res[:2 * ng], res[2 * ng:2 * ng + n], res[2 * ng + n:2 * ng + 2 * n]
    out, off = [], 0
    for gi, sz in enumerate(sizes):
        out.append((sems[2 * gi], sems[2 * gi + 1], list(thru_src[off:off + sz]), list(thru_land[off:off + sz])))
        off += sz
    return out, res[-1]


def _exchange_wait(started, after, scatter, name):
    send_sems, recv_sems, srcs, lands = started
    n = len(srcs)

    def body(*refs):
        src_refs, land_refs = refs[:n], refs[n:2 * n]
        send_ref, recv_ref = refs[2 * n], refs[2 * n + 1]
        for cp in _chip_copies(src_refs, land_refs, send_ref, recv_ref, scatter, landed=True):
            cp.wait_send()
            cp.wait_recv()

    hbm = lambda a: pltpu.HBM(a.shape, a.dtype)
    res = pl.pallas_call(
        body, name=name, out_shape=tuple(hbm(a) for a in srcs) + tuple(hbm(a) for a in lands),
        in_specs=[HBM_ONLY] * (2 * n) + [SEM_SPEC, SEM_SPEC, HBM_SPEC], out_specs=tuple([HBM_ONLY] * (2 * n)),
        input_output_aliases={k: k for k in range(2 * n)},
        compiler_params=pltpu.CompilerParams(has_side_effects=EFFECT),
    )(*srcs, *lands, send_sems, recv_sems, after)
    return list(res[:n]), list(res[n:])


def _small_exchange(smalls, name):
    nsm = len(smalls)
    rels = [(fx, fy, fc) for fx in (0, 1) for fy in (0, 1) for fc in (0, 1)][1:]

    def body(*refs):
        sins, souts = refs[:nsm], refs[nsm:2 * nsm]
        ssend, srecv, slocal = refs[2 * nsm:]
        x, y, c = lax.axis_index("x"), lax.axis_index("y"), lax.axis_index("c")
        lin = 4 * x + 2 * y + c
        local = [pltpu.make_async_copy(sins[i], souts[i].at[lin], slocal.at[i]) for i in range(nsm)]
        for cp in local:
            cp.start()
        for i in range(nsm):
            for j, (fx, fy, fc) in enumerate(rels):
                pltpu.make_async_remote_copy(sins[i], souts[i].at[lin], ssend.at[i, j], srecv.at[i, j],
                                             device_id=(x ^ fx, y ^ fy, c ^ fc), device_id_type=MESH).start()
        for i in range(nsm):
            for j, (fx, fy, fc) in enumerate(rels):
                src = 4 * (x ^ fx) + 2 * (y ^ fy) + (c ^ fc)
                pltpu.make_async_remote_copy(sins[i], souts[i].at[src], ssend.at[i, j], srecv.at[i, j],
                                             device_id=(x ^ fx, y ^ fy, c ^ fc), device_id_type=MESH).wait()
        for cp in local:
            cp.wait()

    return pl.pallas_call(
        body, out_shape=[jax.ShapeDtypeStruct((N_DEV,) + s.shape, s.dtype) for s in smalls],
        in_specs=[HBM_SPEC] * nsm, out_specs=[HBM_SPEC] * nsm,
        scratch_shapes=[pltpu.SemaphoreType.DMA((nsm, 7)), pltpu.SemaphoreType.DMA((nsm, 7)),
                        pltpu.SemaphoreType.DMA((nsm,))],
        name=name)(*smalls)


def _sibling_swap(arrs, name):
    nw = len(arrs)

    def body(*refs):
        ins, outs = refs[:nw], refs[nw:2 * nw]
        send_sems, recv_sems = refs[2 * nw:]
        x, y, c = lax.axis_index("x"), lax.axis_index("y"), lax.axis_index("c")
        cps = [pltpu.make_async_remote_copy(ins[i], outs[i], send_sems.at[i], recv_sems.at[i],
                                            device_id=(x, y, 1 - c), device_id_type=MESH) for i in range(nw)]
        for cp in cps:
            cp.start()
        for cp in cps:
            cp.wait()

    return pl.pallas_call(
        body, out_shape=[jax.ShapeDtypeStruct(a.shape, a.dtype) for a in arrs],
        in_specs=[HBM_SPEC] * nw, out_specs=[HBM_SPEC] * nw,
        scratch_shapes=[pltpu.SemaphoreType.DMA((nw,)), pltpu.SemaphoreType.DMA((nw,))],
        name=name)(*arrs)


def _sum_parts(parts, recv, slots, name):
    _, r, c = parts.shape
    tr = _row_tile(r)

    def body(slot_ref, own_ref, r0_ref, r1_ref, r2_ref, o_ref):
        o_ref[...] = ((own_ref[0].astype(F32) + r0_ref[0].astype(F32))
                      + (r1_ref[0].astype(F32) + r2_ref[0].astype(F32)))

    blk = lambda k: pl.BlockSpec((1, tr, c), lambda i, s, k=k: (s[k], i, 0))
    return pl.pallas_call(
        body, out_shape=jax.ShapeDtypeStruct((r, c), F32),
        grid_spec=pltpu.PrefetchScalarGridSpec(
            num_scalar_prefetch=1, grid=(r // tr,), in_specs=[blk(0), blk(1), blk(2), blk(3)],
            out_specs=pl.BlockSpec((tr, c), lambda i, s: (i, 0))),
        compiler_params=_cparams(("parallel",)), name=name)(slots, parts, recv, recv, recv)


def _small_allreduce(packed, name):
    rows = packed.shape[0]
    pr = rows // N_DEV
    rels = [(fx, fy, fc) for fx in (0, 1) for fy in (0, 1) for fc in (0, 1)][1:]

    def body(in_ref, out_ref, recv_ref, send1, recv1, send2, recv2):
        x, y, c = lax.axis_index("x"), lax.axis_index("y"), lax.axis_index("c")
        lin = 4 * x + 2 * y + c
        piece = lambda ref, k: ref.at[pl.ds(pl.multiple_of(k * pr, pr), pr), :]
        peers = [((x ^ fx, y ^ fy, c ^ fc), 4 * (x ^ fx) + 2 * (y ^ fy) + (c ^ fc)) for fx, fy, fc in rels]
        for j, (dev, plin) in enumerate(peers):
            pltpu.make_async_remote_copy(piece(in_ref, plin), recv_ref.at[lin], send1.at[j], recv1.at[j],
                                         device_id=dev, device_id_type=MESH).start()
        recv_ref[lin] = piece(in_ref, lin)[...]
        for j, (dev, plin) in enumerate(peers):
            pltpu.make_async_remote_copy(piece(in_ref, plin), recv_ref.at[plin], send1.at[j], recv1.at[j],
                                         device_id=dev, device_id_type=MESH).wait()
        acc = recv_ref[0]
        for k in range(1, N_DEV):
            acc = acc + recv_ref[k]
        piece(out_ref, lin)[...] = acc
        for j, (dev, plin) in enumerate(peers):
            pltpu.make_async_remote_copy(piece(out_ref, lin), piece(out_ref, lin), send2.at[j], recv2.at[j],
                                         device_id=dev, device_id_type=MESH).start()
        for j, (dev, plin) in enumerate(peers):
            pltpu.make_async_remote_copy(piece(out_ref, lin), piece(out_ref, plin), send2.at[j], recv2.at[j],
                                         device_id=dev, device_id_type=MESH).wait()

    vm = pl.BlockSpec(memory_space=pltpu.VMEM)
    return pl.pallas_call(
        body, out_shape=jax.ShapeDtypeStruct(packed.shape, F32), in_specs=[vm], out_specs=vm,
        scratch_shapes=[pltpu.VMEM((N_DEV, pr, LANES), F32)] + [pltpu.SemaphoreType.DMA((7,))] * 4,
        compiler_params=pltpu.CompilerParams(vmem_limit_bytes=VMEM_LIMIT), name=name)(packed)


def _adamw_math(w, m, v, g):
    nm = ADAM_B1 * m + (1.0 - ADAM_B1) * g
    nv = ADAM_B2 * v + (1.0 - ADAM_B2) * (g * g)
    m_hat = nm * (1.0 / (1.0 - ADAM_B1 ** ADAM_STEP))
    v_hat = nv * (1.0 / (1.0 - ADAM_B2 ** ADAM_STEP))
    return -ADAM_LR * (m_hat / (jnp.sqrt(v_hat) + ADAM_EPS) + ADAM_WD * w), nm, nv


def _adamw(w, m, v, g_mine, g_other, name):
    r, c = w.shape
    tr = _row_tile(r)

    def body(w_ref, m_ref, v_ref, g1_ref, g2_ref, g_ref, d_ref, nm_ref, nv_ref):
        g = g1_ref[...] + g2_ref[...]
        g_ref[...] = g
        d_ref[...], nm_ref[...], nv_ref[...] = _adamw_math(w_ref[...], m_ref[...], v_ref[...], g)

    blk = pl.BlockSpec((tr, c), lambda i: (i, 0))
    sh = jax.ShapeDtypeStruct((r, c), F32)
    return pl.pallas_call(body, out_shape=(sh, sh, sh, sh), grid=(r // tr,), in_specs=[blk] * 5,
                          out_specs=(blk, blk, blk, blk), compiler_params=_cparams(("parallel",)),
                          name=name)(w, m, v, g_mine, g_other)


def _adamw_small(ws, ms, vs, alls, split, name):
    n = len(ws)
    lead = split if split is not None else ()
    nl = len(lead)
    nslots = alls[0].shape[0]

    def blocks(shape):
        if split is None:
            return tuple(shape), (lambda *g: (0,) * len(shape))
        blk = (shape[0], shape[1] // lead[0], shape[2] // lead[1]) + tuple(shape[3:])
        return blk, (lambda *g: (0, g[0], g[1]) + (0,) * (len(shape) - 3))

    def body(*refs):
        w_refs, m_refs, v_refs, a_refs = (refs[k * n:(k + 1) * n] for k in range(4))
        g_refs, d_refs, nm_refs, nv_refs = (refs[(4 + k) * n:(5 + k) * n] for k in range(4))
        k = pl.program_id(nl)
        for i in range(n):
            @pl.when(k == 0)
            def _(i=i):
                g_refs[i][...] = a_refs[i][0]

            @pl.when(k > 0)
            def _(i=i):
                g_refs[i][...] += a_refs[i][0]

            @pl.when(k == nslots - 1)
            def _(i=i):
                d_refs[i][...], nm_refs[i][...], nv_refs[i][...] = _adamw_math(
                    w_refs[i][...], m_refs[i][...], v_refs[i][...], g_refs[i][...])

    specs, aspecs, shapes = [], [], []
    for wa in ws:
        blk, imap = blocks(wa.shape)
        specs.append(pl.BlockSpec(blk, imap))
        aspecs.append(pl.BlockSpec((1,) + blk, (lambda *g, imap=imap: (g[nl],) + imap(*g))))
        shapes.append(jax.ShapeDtypeStruct(wa.shape, F32))
    res = pl.pallas_call(
        body, out_shape=shapes * 4, grid=tuple(lead) + (nslots,), in_specs=specs * 3 + aspecs,
        out_specs=specs * 4, compiler_params=_cparams(("parallel",) * nl + ("arbitrary",)),
        name=name)(*ws, *ms, *vs, *alls)
    return res[:n], res[n:2 * n], res[2 * n:3 * n], res[3 * n:]


def kernel(x, norm_ffn1, ffn1_w_gate, ffn1_w_up, ffn1_w_down, norm_mix, w_in, attn_sinks, ssm_lambda_re, ssm_lambda_im, ssm_log_dt, ssm_b_re, ssm_b_im, ssm_c_re, ssm_c_im, ssm_d, ssm_glu_w, ssm_glu_b, attn_out_norm, ssm_out_norm, w_out, norm_ffn2, ffn2_w_gate, ffn2_w_up, ffn2_w_down, final_norm, loss_target, m_norm_ffn1, m_ffn1_w_gate, m_ffn1_w_up, m_ffn1_w_down, m_norm_mix, m_w_in, m_attn_sinks, m_ssm_lambda_re, m_ssm_lambda_im, m_ssm_log_dt, m_ssm_b_re, m_ssm_b_im, m_ssm_c_re, m_ssm_c_im, m_ssm_d, m_ssm_glu_w, m_ssm_glu_b, m_attn_out_norm, m_ssm_out_norm, m_w_out, m_norm_ffn2, m_ffn2_w_gate, m_ffn2_w_up, m_ffn2_w_down, m_final_norm, v_norm_ffn1, v_ffn1_w_gate, v_ffn1_w_up, v_ffn1_w_down, v_norm_mix, v_w_in, v_attn_sinks, v_ssm_lambda_re, v_ssm_lambda_im, v_ssm_log_dt, v_ssm_b_re, v_ssm_b_im, v_ssm_c_re, v_ssm_c_im, v_ssm_d, v_ssm_glu_w, v_ssm_glu_b, v_attn_out_norm, v_ssm_out_norm, v_w_out, v_norm_ffn2, v_ffn2_w_gate, v_ffn2_w_up, v_ffn2_w_down, v_final_norm):
    given = dict(locals())
    wts = {n: given[n] for n in WEIGHTS}

    order = [g for g in GROUPS]
    cx, cy = lax.axis_index("x"), lax.axis_index("y")
    slots = jnp.stack([2 * cx + cy, 2 * (1 - cx) + cy, 2 * cx + 1 - cy, 2 * (1 - cx) + 1 - cy]).astype(jnp.int32)
    shards = {n: wts[n][0].astype(BF16) for n in BIG}
    placed = {n: _place_own(shards[n], slots, f"weights_place_{n}") for n in BIG}
    started, _ = _exchange_start([([shards[n] for n in GROUPS[g]], [placed[n] for n in GROUPS[g]]) for g in order],
                                 False, "weights_start")
    started = dict(zip(order, started))

    def get_weights(group, after):
        _, lands = _exchange_wait(started[group], after, False, f"weights_wait_{group}")
        out = dict(zip(GROUPS[group], lands))
        for n in ('ssm_glu_w', 'w_out'):
            if n in out:
                out[n] = out[n].reshape(-1, out[n].shape[-1])
        return out

    sent, tokens = {}, {}

    def put_grads(group, gd):
        parts = []
        for n in GROUPS[group]:
            g = gd[n]
            if g.ndim == 2:
                g = g.reshape(N_CHIPS, g.shape[0] // N_CHIPS, g.shape[1])
            parts.append(g.astype(BF16))
        lands = [lax.empty(p.shape, p.dtype) for p in parts]
        started_g, tokens[group] = _exchange_start([(parts, lands)], True, f"grads_start_{group}")
        sent[group] = started_g[0]
        return tokens[group]

    w = {n: (wts[n][0] if wts[n].ndim > 1 else wts[n]) for n in SMALL}
    w['norm_ffn1'], w['norm_mix'], w['norm_ffn2'] = wts['norm_ffn1'], wts['norm_mix'], wts['norm_ffn2']
    loss, dx, grads = _local_step(x[0], loss_target[0], w, get_weights, put_grads)
    loss = lax.psum(loss, ("x", "y", "c"))

    out_g, out_d, out_m, out_v = {}, {}, {}, {}

    def finish(group, after):
        names = GROUPS[group]
        parts, recv = _exchange_wait(sent[group], after, True, f"grads_wait_{group}")
        chip_sums = [_sum_parts(p, r, slots, f"grad_sum_{n}") for n, p, r in zip(names, parts, recv)]
        other = _sibling_swap(chip_sums, f"grad_sibling_swap_{group}")
        for n, mine, oth in zip(names, chip_sums, other):
            g, d, nm, nv = _adamw(wts[n][0], given['m_' + n][0], given['v_' + n][0], mine, oth, f"adamw_{n}")
            out_g[n], out_d[n], out_m[n], out_v[n] = (a.reshape(wts[n].shape) for a in (g, d, nm, nv))
        return nv

    done = finish('ffn2', tokens['ffn1'])
    done = finish('mix', done)

    nat = {n: (1, wts[n].shape[0]) if wts[n].ndim == 1 else wts[n].shape for n in SMALL}
    wide = ['ssm_b_re', 'ssm_b_im', 'ssm_c_re', 'ssm_c_im']
    narrow = [n for n in SMALL if n not in wide]
    packed = jnp.concatenate([grads[n].reshape(-1, LANES) for n in wide])
    wide_sum = _small_allreduce(packed, "small_grads_allreduce")
    alls = _small_exchange([grads[n].reshape(nat[n]) for n in narrow], "small_grads_allgather")
    rows = wide_sum.shape[0] // len(wide)
    wide_g = [wide_sum[i * rows:(i + 1) * rows].reshape((1,) + nat[n]) for i, n in enumerate(wide)]
    for group, gs, split, tag in ((narrow, alls, None, "adamw_small"), (wide, wide_g, (2, 4), "adamw_ssm_bc")):
        res = _adamw_small([wts[n].reshape(nat[n]) for n in group], [given['m_' + n].reshape(nat[n]) for n in group],
                           [given['v_' + n].reshape(nat[n]) for n in group], gs, split, tag)
        for dst, vals in zip((out_g, out_d, out_m, out_v), res):
            for n, a in zip(group, vals):
                dst[n] = a.reshape(wts[n].shape)

    finish('ffn1', out_v['ssm_b_re'])

    return (loss, dx[None], *[out_g[n] for n in WEIGHTS], *[out_d[n] for n in WEIGHTS],
            *[out_m[n] for n in WEIGHTS], *[out_v[n] for n in WEIGHTS])
```

```python
import functools
import math

import numpy as np
import jax
import jax.numpy as jnp
from jax import lax
from jax.experimental import pallas as pl
from jax.experimental.pallas import tpu as pltpu

F32 = jnp.float32
BF16 = jnp.bfloat16
MESH = pl.DeviceIdType.MESH

EPS = 1e-6
NEG_INF = -1e30
LAMBDA_RE_MAX = -1e-4
ATTN_HEADS = 8
KV_HEADS = 2
GQ = ATTN_HEADS // KV_HEADS
HEAD_DIM = 64
ATTN_WIDTH = 512
KV_WIDTH = 128
WINDOW = 128
QBLOCK = 128
SSM_WIDTH = 512
SSM_GROUPS = 32
SSM_CH = 16
SSM_STATE = 64
N_STRIPS = 4
STRIP_IN = SSM_WIDTH // N_STRIPS
STRIP_ST = SSM_GROUPS * SSM_STATE // N_STRIPS
SUBLANES = 8
LANES = 128
N_CHIPS = 4
N_DEV = 8

ADAM_LR = 0.001
ADAM_B1 = 0.9
ADAM_B2 = 0.999
ADAM_EPS = 1e-08
ADAM_WD = 0.01
ADAM_STEP = 10

VMEM_LIMIT = 48 * 1024 * 1024

WEIGHTS = ['norm_ffn1', 'ffn1_w_gate', 'ffn1_w_up', 'ffn1_w_down', 'norm_mix', 'w_in', 'attn_sinks',
           'ssm_lambda_re', 'ssm_lambda_im', 'ssm_log_dt', 'ssm_b_re', 'ssm_b_im', 'ssm_c_re', 'ssm_c_im',
           'ssm_d', 'ssm_glu_w', 'ssm_glu_b', 'attn_out_norm', 'ssm_out_norm', 'w_out', 'norm_ffn2',
           'ffn2_w_gate', 'ffn2_w_up', 'ffn2_w_down', 'final_norm']
BIG = ['ffn1_w_gate', 'ffn1_w_up', 'ffn1_w_down', 'w_in', 'ssm_glu_w', 'w_out',
       'ffn2_w_gate', 'ffn2_w_up', 'ffn2_w_down']
SMALL = [n for n in WEIGHTS if n not in BIG]
TRANSPOSED = ['ffn1_w_gate', 'ffn1_w_up', 'w_in', 'ffn2_w_gate', 'ffn2_w_up']
GROUPS = {'ffn1': ['ffn1_w_gate', 'ffn1_w_up', 'ffn1_w_down'],
          'mix': ['w_in', 'ssm_glu_w', 'w_out'],
          'ffn2': ['ffn2_w_gate', 'ffn2_w_up', 'ffn2_w_down']}


def _cparams(sem=None):
    return pltpu.CompilerParams(dimension_semantics=sem, vmem_limit_bytes=VMEM_LIMIT)


def _tile(n, pref):
    if n <= pref:
        return n
    for t in (pref, pref // 2, pref // 4):
        if t % LANES == 0 and n % t == 0:
            return t
    return n


def _sigmoid(x):
    return 1.0 / (1.0 + jnp.exp(-x))


def _mm(a, b, *, ta=False, tb=False, reduce_s=False, res=None, scale=1.0, out_dtype=F32, after=None, name):
    a3 = a if a.ndim == 3 else a[None]
    b3 = b if b.ndim == 3 else b[None]
    sa, sb = a3.shape[0], b3.shape[0]
    ns = max(sa, sb)
    (kk, m) = a3.shape[1:] if ta else a3.shape[1:][::-1]
    (n, kb) = b3.shape[1:] if tb else b3.shape[1:][::-1]
    assert kk == kb, (a3.shape, b3.shape)
    tm, tn, tk = _tile(m, 1024), _tile(n, 1024), _tile(kk, 2048)
    nm, nn, nk = m // tm, n // tn, kk // tk
    has_res = res is not None
    single = nk == 1 and not (reduce_s and ns > 1)

    if reduce_s:
        grid = (nm, nn, ns, nk)
        ids = lambda i, j, s, k: (s, i, j, k)
        sem = ("parallel", "parallel", "arbitrary", "arbitrary")
    else:
        grid = (ns, nm, nn, nk)
        ids = lambda s, i, j, k: (s, i, j, k)
        sem = ("parallel", "parallel", "parallel", "arbitrary")

    def a_map(*g):
        s, i, j, k = ids(*g)
        s = s if sa > 1 else 0
        return (s, k, i) if ta else (s, i, k)

    def b_map(*g):
        s, i, j, k = ids(*g)
        s = s if sb > 1 else 0
        return (s, j, k) if tb else (s, k, j)

    def o_map(*g):
        s, i, j, k = ids(*g)
        return (i, j) if reduce_s else (s, i, j)

    a_blk = (1, tk, tm) if ta else (1, tm, tk)
    b_blk = (1, tn, tk) if tb else (1, tk, tn)
    dims = (((0 if ta else 1,), (1 if tb else 0,)), ((), ()))

    def body(*refs):
        a_ref, b_ref = refs[0], refs[1]
        r_ref = refs[2] if has_res else None
        o_ref = refs[2 + has_res + (after is not None)]
        acc_ref = None if single else refs[-1]
        s, _, _, k = ids(*[pl.program_id(d) for d in range(4)])
        prod = lax.dot_general(a_ref[0].astype(BF16), b_ref[0].astype(BF16), dims, preferred_element_type=F32)

        def finish(out):
            if scale != 1.0:
                out = out * scale
            if has_res:
                out = r_ref[...].reshape(out.shape) + out
            o_ref[...] = out.astype(out_dtype).reshape(o_ref.shape)

        if single:
            finish(prod)
            return
        if reduce_s:
            first = jnp.logical_and(s == 0, k == 0)
            last = jnp.logical_and(s == ns - 1, k == nk - 1)
        else:
            first, last = k == 0, k == nk - 1

        @pl.when(first)
        def _():
            acc_ref[...] = prod

        @pl.when(jnp.logical_not(first))
        def _():
            acc_ref[...] += prod

        @pl.when(last)
        def _():
            finish(acc_ref[...])

    in_specs = [pl.BlockSpec(a_blk, a_map), pl.BlockSpec(b_blk, b_map)]
    args = [a3, b3]
    if reduce_s:
        out_shape = jax.ShapeDtypeStruct((m, n), out_dtype)
        o_spec = pl.BlockSpec((tm, tn), o_map)
    else:
        out_shape = jax.ShapeDtypeStruct((ns, m, n), out_dtype)
        o_spec = pl.BlockSpec((1, tm, tn), o_map)
    if has_res:
        assert res.shape == out_shape.shape
        in_specs.append(o_spec)
        args.append(res)
    if after is not None:
        in_specs.append(HBM_SPEC)
        args.append(after)
    return pl.pallas_call(body, out_shape=out_shape, grid=grid, in_specs=in_specs, out_specs=o_spec,
                          scratch_shapes=[] if single else [pltpu.VMEM((tm, tn), F32)],
                          compiler_params=_cparams(sem), name=name)(*args)


def _row_tile(t):
    for tr in (256, 128, 64, 32, 16, 8):
        if t % tr == 0:
            return tr
    return t


def _rms_fwd(x, g, name):
    t, w = x.shape
    tr = _row_tile(t)

    def body(x_ref, g_ref, o_ref):
        xv = x_ref[...]
        r = lax.rsqrt(jnp.mean(xv * xv, axis=-1, keepdims=True) + EPS)
        o_ref[...] = (xv * r * g_ref[...]).astype(BF16)

    return pl.pallas_call(
        body, out_shape=jax.ShapeDtypeStruct((t, w), BF16), grid=(t // tr,),
        in_specs=[pl.BlockSpec((tr, w), lambda i: (i, 0)), pl.BlockSpec((1, w), lambda i: (0, 0))],
        out_specs=pl.BlockSpec((tr, w), lambda i: (i, 0)), compiler_params=_cparams(("parallel",)),
        name=name)(x, g)


def _rms_bwd_rows(xv, gv, dhv):
    r = lax.rsqrt(jnp.mean(xv * xv, axis=-1, keepdims=True) + EPS)
    nrm = xv * r
    dn = dhv * gv
    return r * (dn - nrm * jnp.mean(dn * nrm, axis=-1, keepdims=True)), dhv * nrm


def _rms_bwd(x, g, dh, dres, name):
    t, w = x.shape
    tr = _row_tile(t)
    has_res = dres is not None

    def body(*refs):
        if has_res:
            x_ref, g_ref, dh_ref, dr_ref, dx_ref, dxb_ref, dg_ref = refs
        else:
            x_ref, g_ref, dh_ref, dx_ref, dxb_ref, dg_ref = refs
        dx, dgs = _rms_bwd_rows(x_ref[...], g_ref[...], dh_ref[...])
        if has_res:
            dx = dx + dr_ref[...]
        dx_ref[...] = dx
        dxb_ref[...] = dx.astype(BF16)

        @pl.when(pl.program_id(0) == 0)
        def _():
            dg_ref[...] = jnp.zeros_like(dg_ref)

        dg_ref[...] += jnp.sum(dgs, axis=0, keepdims=True)

    row = pl.BlockSpec((tr, w), lambda i: (i, 0))
    vec = pl.BlockSpec((1, w), lambda i: (0, 0))
    ins = [x, g, dh] + ([dres] if has_res else [])
    return pl.pallas_call(
        body, out_shape=(jax.ShapeDtypeStruct((t, w), F32), jax.ShapeDtypeStruct((t, w), BF16),
                         jax.ShapeDtypeStruct((1, w), F32)),
        grid=(t // tr,), in_specs=[row, vec, row] + ([row] if has_res else []),
        out_specs=(row, row, vec), compiler_params=_cparams(("arbitrary",)), name=name)(*ins)


FFN_ROWS = 512


NT_DIMS = (((1,), (1,)), ((), ()))
TN_DIMS = (((0,), (0,)), ((), ()))


def _ffn_fwd_call(x, g, wg, wu, wd, name):
    t, d = x.shape
    ns, f, _ = wg.shape
    tm = _tile(t, FFN_ROWS)

    def body(x_ref, g_ref, wg_ref, wu_ref, wd_ref, xo_ref, h_ref, gate_ref, up_ref, h_sc, acc_ref):
        s = pl.program_id(1)

        @pl.when(s == 0)
        def _():
            xv = x_ref[...]
            r = lax.rsqrt(jnp.mean(xv * xv, axis=-1, keepdims=True) + EPS)
            hb = (xv * r * g_ref[...]).astype(BF16)
            h_sc[...] = hb
            h_ref[...] = hb

        hb = h_sc[...]
        gate = lax.dot_general(hb, wg_ref[0], NT_DIMS, preferred_element_type=F32)
        up = lax.dot_general(hb, wu_ref[0], NT_DIMS, preferred_element_type=F32)
        gate_ref[0] = gate.astype(BF16)
        up_ref[0] = up.astype(BF16)
        act = (gate * _sigmoid(gate) * up).astype(BF16)
        prod = jnp.dot(act, wd_ref[0], preferred_element_type=F32)

        @pl.when(s == 0)
        def _():
            acc_ref[...] = prod

        @pl.when(s > 0)
        def _():
            acc_ref[...] += prod

        @pl.when(s == ns - 1)
        def _():
            xo_ref[...] = x_ref[...] + 0.5 * acc_ref[...]

    row = pl.BlockSpec((tm, d), lambda i, s: (i, 0))
    vec = pl.BlockSpec((1, d), lambda i, s: (0, 0))
    wrow = pl.BlockSpec((1, f, d), lambda i, s: (s, 0, 0))
    hid = pl.BlockSpec((1, tm, f), lambda i, s: (s, i, 0))
    hid_sh = jax.ShapeDtypeStruct((ns, t, f), BF16)
    return pl.pallas_call(
        body, out_shape=(jax.ShapeDtypeStruct((t, d), F32), jax.ShapeDtypeStruct((t, d), BF16), hid_sh, hid_sh),
        grid=(t // tm, ns), in_specs=[row, vec, wrow, wrow, wrow], out_specs=(row, row, hid, hid),
        scratch_shapes=[pltpu.VMEM((tm, d), BF16), pltpu.VMEM((tm, d), F32)],
        compiler_params=_cparams(("parallel", "arbitrary")), name=name)(x, g, wg, wu, wd)


def _ffn_bwd_x_call(dxo, dxo_b, x, g, gate, up, wg, wu, wd, name):
    t, d = x.shape
    ns, f, _ = wg.shape
    tm = _tile(t, FFN_ROWS)

    def body(dxo_ref, dxb_ref, x_ref, g_ref, gate_ref, up_ref, wg_ref, wu_ref, wd_ref,
             dx_ref, dxob_ref, dgn_ref, dgate_ref, dup_ref, act_ref, dh_ref):
        i, s = pl.program_id(0), pl.program_id(1)
        dact = lax.dot_general(dxb_ref[...], wd_ref[0], NT_DIMS, preferred_element_type=F32) * 0.5
        gv = gate_ref[0].astype(F32)
        uv = up_ref[0].astype(F32)
        sg = _sigmoid(gv)
        silu = gv * sg
        act_ref[0] = (silu * uv).astype(BF16)
        dub = (dact * silu).astype(BF16)
        dgb = (dact * uv * sg * (1.0 + gv * (1.0 - sg))).astype(BF16)
        dup_ref[0] = dub
        dgate_ref[0] = dgb
        prod = (jnp.dot(dgb, wg_ref[0], preferred_element_type=F32)
                + jnp.dot(dub, wu_ref[0], preferred_element_type=F32))

        @pl.when(s == 0)
        def _():
            dh_ref[...] = prod

        @pl.when(s > 0)
        def _():
            dh_ref[...] += prod

        @pl.when(jnp.logical_and(i == 0, s == 0))
        def _():
            dgn_ref[...] = jnp.zeros_like(dgn_ref)

        @pl.when(s == ns - 1)
        def _():
            dx, dgs = _rms_bwd_rows(x_ref[...], g_ref[...], dh_ref[...])
            dx = dx + dxo_ref[...]
            dx_ref[...] = dx
            dxob_ref[...] = dx.astype(BF16)
            dgn_ref[...] += jnp.sum(dgs, axis=0, keepdims=True)

    row = pl.BlockSpec((tm, d), lambda i, s: (i, 0))
    vec = pl.BlockSpec((1, d), lambda i, s: (0, 0))
    wrow = pl.BlockSpec((1, f, d), lambda i, s: (s, 0, 0))
    hid = pl.BlockSpec((1, tm, f), lambda i, s: (s, i, 0))
    hid_sh = jax.ShapeDtypeStruct((ns, t, f), BF16)
    return pl.pallas_call(
        body,
        out_shape=(jax.ShapeDtypeStruct((t, d), F32), jax.ShapeDtypeStruct((t, d), BF16),
                   jax.ShapeDtypeStruct((1, d), F32), hid_sh, hid_sh, hid_sh),
        grid=(t // tm, ns), in_specs=[row, row, row, vec, hid, hid, wrow, wrow, wrow],
        out_specs=(row, row, vec, hid, hid, hid), scratch_shapes=[pltpu.VMEM((tm, d), F32)],
        compiler_params=_cparams(("arbitrary", "arbitrary")), name=name)(dxo, dxo_b, x, g, gate, up, wg, wu, wd)


def _ffn_bwd_w_call(h, dxo_b, dgate, dup, act, name):
    t, d = h.shape
    ns, _, f = dgate.shape
    tm = _tile(t, FFN_ROWS)
    nm = t // tm

    def body(h_ref, dxb_ref, dgate_ref, dup_ref, act_ref, dwg_ref, dwu_ref, dwd_ref, ag_ref, au_ref, ad_ref):
        i = pl.program_id(1)
        hv = h_ref[...]
        pg = lax.dot_general(dgate_ref[0], hv, TN_DIMS, preferred_element_type=F32)
        pu = lax.dot_general(dup_ref[0], hv, TN_DIMS, preferred_element_type=F32)
        pd = lax.dot_general(act_ref[0], dxb_ref[...], TN_DIMS, preferred_element_type=F32)

        @pl.when(i == 0)
        def _():
            ag_ref[...] = pg
            au_ref[...] = pu
            ad_ref[...] = pd

        @pl.when(i > 0)
        def _():
            ag_ref[...] += pg
            au_ref[...] += pu
            ad_ref[...] += pd

        @pl.when(i == nm - 1)
        def _():
            dwg_ref[0] = ag_ref[...].astype(BF16)
            dwu_ref[0] = au_ref[...].astype(BF16)
            dwd_ref[0] = (0.5 * ad_ref[...]).astype(BF16)

    row = pl.BlockSpec((tm, d), lambda s, i: (i, 0))
    hid = pl.BlockSpec((1, tm, f), lambda s, i: (s, i, 0))
    wrow = pl.BlockSpec((1, f, d), lambda s, i: (s, 0, 0))
    wsh = jax.ShapeDtypeStruct((ns, f, d), BF16)
    return pl.pallas_call(
        body, out_shape=(wsh, wsh, wsh),
        grid=(ns, nm), in_specs=[row, row, hid, hid, hid], out_specs=(wrow, wrow, wrow),
        scratch_shapes=[pltpu.VMEM((f, d), F32), pltpu.VMEM((f, d), F32), pltpu.VMEM((f, d), F32)],
        compiler_params=_cparams(("parallel", "arbitrary")), name=name)(h, dxo_b, dgate, dup, act)


def _loss_head(x, g, tgt, name):
    t, w = x.shape
    tr = _row_tile(t)

    def body(x_ref, g_ref, t_ref, loss_ref, dx_ref, dxb_ref, dg_ref):
        xv = x_ref[...]
        gv = g_ref[...]
        r = lax.rsqrt(jnp.mean(xv * xv, axis=-1, keepdims=True) + EPS)
        nrm = xv * r
        err = nrm * gv - t_ref[...]
        dout = err * (1.0 / w)
        dn = dout * gv
        dx = r * (dn - nrm * jnp.mean(dn * nrm, axis=-1, keepdims=True))
        dx_ref[...] = dx
        dxb_ref[...] = dx.astype(BF16)

        @pl.when(pl.program_id(0) == 0)
        def _():
            dg_ref[...] = jnp.zeros_like(dg_ref)
            loss_ref[...] = jnp.zeros_like(loss_ref)

        dg_ref[...] += jnp.sum(dout * nrm, axis=0, keepdims=True)
        part = jnp.sum(jnp.sum(err * err, axis=-1, keepdims=True) * (0.5 / w), axis=0, keepdims=True)
        loss_ref[...] += jnp.broadcast_to(part, loss_ref.shape)

    row = pl.BlockSpec((tr, w), lambda i: (i, 0))
    vec = pl.BlockSpec((1, w), lambda i: (0, 0))
    return pl.pallas_call(
        body, out_shape=(jax.ShapeDtypeStruct((1, LANES), F32), jax.ShapeDtypeStruct((t, w), F32),
                         jax.ShapeDtypeStruct((t, w), BF16), jax.ShapeDtypeStruct((1, w), F32)),
        grid=(t // tr,), in_specs=[row, vec, row],
        out_specs=(pl.BlockSpec((1, LANES), lambda i: (0, 0)), row, row, vec),
        compiler_params=_cparams(("arbitrary",)), name=name)(x, g, tgt)


def _attn_scores(q, k3, n, t, slope_ref):
    rows = GQ * QBLOCK
    s = lax.dot_general(q, k3, (((1,), (1,)), ((), ())), preferred_element_type=F32) * (HEAD_DIM ** -0.5)
    row = lax.broadcasted_iota(jnp.int32, (rows, 3 * QBLOCK), 0) & (QBLOCK - 1)
    col = lax.broadcasted_iota(jnp.int32, (rows, 3 * QBLOCK), 1)
    rel = jnp.abs(col - QBLOCK - row)
    key_pos = n * QBLOCK - QBLOCK + col
    valid = (rel <= WINDOW) & (key_pos >= 0) & (key_pos < t)
    return jnp.where(valid, s - slope_ref[0] * rel.astype(F32), NEG_INF)


def _attn_fwd(q4, kp, vp, sink_rows, slope_rows, name):
    _, _, t, _ = q4.shape
    nb = t // QBLOCK
    rows = GQ * QBLOCK

    def body(q_ref, k_ref, v_ref, sink_ref, slope_ref, o_ref, lse_ref):
        n = pl.program_id(1)
        start = pl.multiple_of(n * QBLOCK, QBLOCK)
        q = q_ref[0].reshape(rows, HEAD_DIM).astype(BF16)
        k3 = k_ref[0, pl.ds(start, 3 * QBLOCK), :].astype(BF16)
        v3 = v_ref[0, pl.ds(start, 3 * QBLOCK), :].astype(BF16)
        s = _attn_scores(q, k3, n, t, slope_ref)
        sink = sink_ref[0]
        mx = jnp.maximum(jnp.max(s, axis=-1, keepdims=True), sink)
        p = jnp.exp(s - mx)
        den = jnp.sum(p, axis=-1, keepdims=True) + jnp.exp(sink - mx)
        o = lax.dot_general(p.astype(BF16), v3, (((1,), (0,)), ((), ())), preferred_element_type=F32)
        o_ref[0] = (o / den).reshape(GQ, QBLOCK, HEAD_DIM)
        lse_ref[0] = (mx + jnp.log(den)).reshape(GQ, QBLOCK, 1)

    qspec = pl.BlockSpec((1, GQ, QBLOCK, HEAD_DIM), lambda h, n: (h, 0, n, 0))
    kvspec = pl.BlockSpec((1, t + 2 * QBLOCK, HEAD_DIM), lambda h, n: (h, 0, 0))
    rowspec = pl.BlockSpec((1, rows, 1), lambda h, n: (h, 0, 0))
    return pl.pallas_call(
        body, out_shape=(jax.ShapeDtypeStruct(q4.shape, F32), jax.ShapeDtypeStruct((KV_HEADS, GQ, t, 1), F32)),
        grid=(KV_HEADS, nb), in_specs=[qspec, kvspec, kvspec, rowspec, rowspec],
        out_specs=(qspec, pl.BlockSpec((1, GQ, QBLOCK, 1), lambda h, n: (h, 0, n, 0))),
        compiler_params=_cparams(("parallel", "parallel")), name=name)(q4, kp, vp, sink_rows, slope_rows)


def _attn_bwd(q4, kp, vp, sink_rows, slope_rows, o4, lse4, do4, name):
    _, _, t, _ = q4.shape
    nb = t // QBLOCK
    rows = GQ * QBLOCK
    scale = HEAD_DIM ** -0.5

    def body(q_ref, k_ref, v_ref, sink_ref, slope_ref, o_ref, lse_ref, do_ref, dq_ref, dk_ref, dv_ref, ds_ref):
        n = pl.program_id(1)
        start = pl.multiple_of(n * QBLOCK, QBLOCK)

        @pl.when(n == 0)
        def _():
            dk_ref[...] = jnp.zeros_like(dk_ref)
            dv_ref[...] = jnp.zeros_like(dv_ref)
            ds_ref[...] = jnp.zeros_like(ds_ref)

        q = q_ref[0].reshape(rows, HEAD_DIM).astype(BF16)
        k3 = k_ref[0, pl.ds(start, 3 * QBLOCK), :].astype(BF16)
        v3 = v_ref[0, pl.ds(start, 3 * QBLOCK), :].astype(BF16)
        do = do_ref[0].reshape(rows, HEAD_DIM)
        lse = lse_ref[0].reshape(rows, 1)
        s = _attn_scores(q, k3, n, t, slope_ref)
        p = jnp.exp(s - lse)
        delta = jnp.sum(do * o_ref[0].reshape(rows, HEAD_DIM), axis=-1, keepdims=True)
        dob = do.astype(BF16)
        dp = lax.dot_general(dob, v3, (((1,), (1,)), ((), ())), preferred_element_type=F32)
        dsb = (p * (dp - delta)).astype(BF16)
        dq = lax.dot_general(dsb, k3, (((1,), (0,)), ((), ())), preferred_element_type=F32) * scale
        dq_ref[0] = dq.reshape(GQ, QBLOCK, HEAD_DIM)
        dk3 = lax.dot_general(dsb, q, (((0,), (0,)), ((), ())), preferred_element_type=F32) * scale
        dv3 = lax.dot_general(p.astype(BF16), dob, (((0,), (0,)), ((), ())), preferred_element_type=F32)
        dk_ref[0, pl.ds(start, 3 * QBLOCK), :] += dk3
        dv_ref[0, pl.ds(start, 3 * QBLOCK), :] += dv3
        dsink_rows = -jnp.exp(sink_ref[0] - lse) * delta
        ds_ref[0] += jnp.sum(dsink_rows.reshape(GQ, QBLOCK, 1), axis=1)

    qspec = pl.BlockSpec((1, GQ, QBLOCK, HEAD_DIM), lambda h, n: (h, 0, n, 0))
    kvspec = pl.BlockSpec((1, t + 2 * QBLOCK, HEAD_DIM), lambda h, n: (h, 0, 0))
    rowspec = pl.BlockSpec((1, rows, 1), lambda h, n: (h, 0, 0))
    lsespec = pl.BlockSpec((1, GQ, QBLOCK, 1), lambda h, n: (h, 0, n, 0))
    return pl.pallas_call(
        body,
        out_shape=(jax.ShapeDtypeStruct(q4.shape, F32), jax.ShapeDtypeStruct(kp.shape, F32),
                   jax.ShapeDtypeStruct(vp.shape, F32), jax.ShapeDtypeStruct((KV_HEADS, GQ, 1), F32)),
        grid=(KV_HEADS, nb),
        in_specs=[qspec, kvspec, kvspec, rowspec, rowspec, qspec, lsespec, qspec],
        out_specs=(qspec, kvspec, kvspec, pl.BlockSpec((1, GQ, 1), lambda h, n: (h, 0, 0))),
        compiler_params=_cparams(("parallel", "arbitrary")), name=name)(
            q4, kp, vp, sink_rows, slope_rows, o4, lse4, do4)


def _scan_tables(a_re, a_im, reverse):
    pw = [(a_re, a_im)]
    for _ in range(SUBLANES - 1):
        pr, pi = pw[-1]
        pw.append((pr * a_re - pi * a_im, pr * a_im + pi * a_re))
    rows = np.arange(SUBLANES)
    tabs = []
    for d in (1, 2, 4):
        mask = (rows <= SUBLANES - 1 - d) if reverse else (rows >= d)
        m = jnp.asarray(mask, F32)[:, None]
        tabs += [m * pw[d - 1][0][None, :], m * pw[d - 1][1][None, :]]
    order = (SUBLANES - 1 - rows) if reverse else rows
    tabs += [jnp.stack([pw[j][0] for j in order]), jnp.stack([pw[j][1] for j in order])]
    tab = jnp.stack(tabs)
    return tab.reshape(8, SUBLANES, N_STRIPS, STRIP_ST).transpose(2, 0, 1, 3)


def _scan(v, mi_re, mi_im, tab, mo_re, mo_im, reverse, name):
    t = v.shape[0]
    tc = _tile(t, 256)
    nc = t // tc
    nblk = tc // SUBLANES

    def body(v_ref, mir_ref, mii_ref, tab_ref, mor_ref, moi_ref, y_ref, xr_ref, xi_ref, carry_ref):
        @pl.when(pl.program_id(1) == 0)
        def _():
            carry_ref[...] = jnp.zeros_like(carry_ref)

        vb = v_ref[...].astype(BF16)
        xr_ref[...] = jnp.dot(vb, mir_ref[0], preferred_element_type=F32)
        xi_ref[...] = jnp.dot(vb, mii_ref[0], preferred_element_type=F32)

        def blk(i, carry):
            cr, ci = carry
            b = (nblk - 1 - i) if reverse else i
            r0 = pl.multiple_of(b * SUBLANES, SUBLANES)
            xr = xr_ref[pl.ds(r0, SUBLANES), :]
            xi = xi_ref[pl.ds(r0, SUBLANES), :]
            for j, d in enumerate((1, 2, 4)):
                tr_, ti_ = tab_ref[0, 2 * j], tab_ref[0, 2 * j + 1]
                sh = (SUBLANES - d) if reverse else d
                sr = pltpu.roll(xr, sh, 0)
                si = pltpu.roll(xi, sh, 0)
                xr, xi = xr + tr_ * sr - ti_ * si, xi + tr_ * si + ti_ * sr
            pr, pi = tab_ref[0, 6], tab_ref[0, 7]
            xr, xi = xr + pr * cr - pi * ci, xi + pr * ci + pi * cr
            xr_ref[pl.ds(r0, SUBLANES), :] = xr
            xi_ref[pl.ds(r0, SUBLANES), :] = xi
            edge = 0 if reverse else SUBLANES - 1
            return (jnp.broadcast_to(xr[edge:edge + 1, :], xr.shape),
                    jnp.broadcast_to(xi[edge:edge + 1, :], xi.shape))

        cr, ci = lax.fori_loop(0, nblk, blk, (carry_ref[0], carry_ref[1]))
        carry_ref[0] = cr
        carry_ref[1] = ci
        y_ref[...] = (jnp.dot(xr_ref[...].astype(BF16), mor_ref[0], preferred_element_type=F32)
                      + jnp.dot(xi_ref[...].astype(BF16), moi_ref[0], preferred_element_type=F32))

    tmap = (lambda s, c: (nc - 1 - c, s)) if reverse else (lambda s, c: (c, s))
    smap3 = lambda s, c: (s, 0, 0)
    return pl.pallas_call(
        body,
        out_shape=(jax.ShapeDtypeStruct((t, SSM_WIDTH), F32),
                   jax.ShapeDtypeStruct((t, N_STRIPS * STRIP_ST), F32),
                   jax.ShapeDtypeStruct((t, N_STRIPS * STRIP_ST), F32)),
        grid=(N_STRIPS, nc),
        in_specs=[pl.BlockSpec((tc, STRIP_IN), tmap),
                  pl.BlockSpec((1, STRIP_IN, STRIP_ST), smap3), pl.BlockSpec((1, STRIP_IN, STRIP_ST), smap3),
                  pl.BlockSpec((1, 8, SUBLANES, STRIP_ST), lambda s, c: (s, 0, 0, 0)),
                  pl.BlockSpec((1, STRIP_ST, STRIP_IN), smap3), pl.BlockSpec((1, STRIP_ST, STRIP_IN), smap3)],
        out_specs=(pl.BlockSpec((tc, STRIP_IN), tmap), pl.BlockSpec((tc, STRIP_ST), tmap),
                   pl.BlockSpec((tc, STRIP_ST), tmap)),
        scratch_shapes=[pltpu.VMEM((2, SUBLANES, STRIP_ST), F32)],
        compiler_params=_cparams(("parallel", "arbitrary")), name=name)(v, mi_re, mi_im, tab, mo_re, mo_im)


def _scan_param_grads(v, dy, xr, xi, lr, li, reverse, name):
    t = v.shape[0]
    tc = _tile(t, 256)
    nc = t // tc
    hb = tc // SUBLANES

    def body(v_ref, dy_ref, xr_ref, xi_ref, lr_ref, li_ref, hr_ref, hi_ref,
             dmir_ref, dmii_ref, dmor_ref, dmoi_ref, da_ref):
        c = pl.program_id(1)

        @pl.when(c == 0)
        def _():
            for r in (dmir_ref, dmii_ref, dmor_ref, dmoi_ref, da_ref):
                r[...] = jnp.zeros_like(r)

        xrv, xiv, lrv, liv = xr_ref[...], xi_ref[...], lr_ref[...], li_ref[...]
        row = lax.broadcasted_iota(jnp.int32, xrv.shape, 0)
        if reverse:
            live = (c < nc - 1).astype(F32)
            edge_r, edge_i = hr_ref[0:1, :] * live, hi_ref[0:1, :] * live
            xpr = jnp.where(row == tc - 1, edge_r, pltpu.roll(xrv, tc - 1, 0))
            xpi = jnp.where(row == tc - 1, edge_i, pltpu.roll(xiv, tc - 1, 0))
        else:
            live = (c > 0).astype(F32)
            edge_r, edge_i = hr_ref[SUBLANES - 1:SUBLANES, :] * live, hi_ref[SUBLANES - 1:SUBLANES, :] * live
            xpr = jnp.where(row == 0, edge_r, pltpu.roll(xrv, 1, 0))
            xpi = jnp.where(row == 0, edge_i, pltpu.roll(xiv, 1, 0))
        da_ref[0, 0:1, :] += jnp.sum(xpr * lrv + xpi * liv, axis=0, keepdims=True)
        da_ref[0, 1:2, :] += jnp.sum(xpr * liv - xpi * lrv, axis=0, keepdims=True)
        tdims = (((0,), (0,)), ((), ()))
        vb, dyb = v_ref[...].astype(BF16), dy_ref[...].astype(BF16)
        dmir_ref[0] += lax.dot_general(vb, lrv.astype(BF16), tdims, preferred_element_type=F32)
        dmii_ref[0] += lax.dot_general(vb, liv.astype(BF16), tdims, preferred_element_type=F32)
        dmor_ref[0] += lax.dot_general(xrv.astype(BF16), dyb, tdims, preferred_element_type=F32)
        dmoi_ref[0] += lax.dot_general(xiv.astype(BF16), dyb, tdims, preferred_element_type=F32)

    tmap = lambda s, c: (c, s)
    if reverse:
        hmap = lambda s, c: (jnp.minimum((c + 1) * hb, t // SUBLANES - 1), s)
    else:
        hmap = lambda s, c: (jnp.maximum(c * hb - 1, 0), s)
    narrow = pl.BlockSpec((tc, STRIP_IN), tmap)
    wide = pl.BlockSpec((tc, STRIP_ST), tmap)
    halo = pl.BlockSpec((SUBLANES, STRIP_ST), hmap)
    smap3 = lambda s, c: (s, 0, 0)
    return pl.pallas_call(
        body,
        out_shape=(jax.ShapeDtypeStruct((N_STRIPS, STRIP_IN, STRIP_ST), F32),
                   jax.ShapeDtypeStruct((N_STRIPS, STRIP_IN, STRIP_ST), F32),
                   jax.ShapeDtypeStruct((N_STRIPS, STRIP_ST, STRIP_IN), F32),
                   jax.ShapeDtypeStruct((N_STRIPS, STRIP_ST, STRIP_IN), F32),
                   jax.ShapeDtypeStruct((N_STRIPS, SUBLANES, STRIP_ST), F32)),
        grid=(N_STRIPS, nc),
        in_specs=[narrow, narrow, wide, wide, wide, wide, halo, halo],
        out_specs=(pl.BlockSpec((1, STRIP_IN, STRIP_ST), smap3), pl.BlockSpec((1, STRIP_IN, STRIP_ST), smap3),
                   pl.BlockSpec((1, STRIP_ST, STRIP_IN), smap3), pl.BlockSpec((1, STRIP_ST, STRIP_IN), smap3),
                   pl.BlockSpec((1, SUBLANES, STRIP_ST), smap3)),
        compiler_params=_cparams(("parallel", "arbitrary")), name=name)(v, dy, xr, xi, lr, li, xr, xi)


def _ssm_prep(lam_re, lam_im, log_dt, bt_re, bt_im, c_re, c_im):
    lr = jnp.minimum(lam_re, LAMBDA_RE_MAX)
    li = lam_im
    dt = jnp.exp(log_dt)[:, None]
    mag = jnp.exp(lr * dt)
    a_re = mag * jnp.cos(li * dt)
    a_im = mag * jnp.sin(li * dt)
    den = lr * lr + li * li
    coef_re = ((a_re - 1.0) * lr + a_im * li) / den
    coef_im = (a_im * lr - (a_re - 1.0) * li) / den
    bb_re = coef_re[:, None, :] * bt_re - coef_im[:, None, :] * bt_im
    bb_im = coef_re[:, None, :] * bt_im + coef_im[:, None, :] * bt_re
    eye = jnp.eye(SSM_GROUPS // N_STRIPS, dtype=F32)

    def strips(m):
        g, a, b = m.shape
        m4 = m.reshape(N_STRIPS, g // N_STRIPS, a, b)
        return jnp.einsum('sgab,gk->sgakb', m4, eye).reshape(N_STRIPS, g // N_STRIPS * a, g // N_STRIPS * b)

    mi_re = strips(bb_re)
    mi_im = strips(bb_im)
    mo_re = strips(jnp.swapaxes(c_re, 1, 2))
    mo_im = strips(-jnp.swapaxes(c_im, 1, 2))
    return a_re.reshape(-1), a_im.reshape(-1), mi_re, mi_im, mo_re, mo_im


def _gelu(x):
    c = math.sqrt(2.0 / math.pi)
    return 0.5 * x * (1.0 + jnp.tanh(c * (x + 0.044715 * x * x * x)))


def _gelu_grad(x):
    c = math.sqrt(2.0 / math.pi)
    th = jnp.tanh(c * (x + 0.044715 * x * x * x))
    return 0.5 * (1.0 + th) + 0.5 * x * (1.0 - th * th) * c * (1.0 + 3.0 * 0.044715 * x * x)


def _ssm_post_fwd(u, yf, yb, d, wglu, bglu, name):
    t, w = u.shape
    tr = _row_tile(t)

    def body(u_ref, yf_ref, yb_ref, d_ref, w_ref, b_ref, s_ref, y0_ref, z_ref):
        y0 = d_ref[...] * u_ref[...] + yf_ref[...] + yb_ref[...]
        yg = _gelu(y0)
        z = jnp.dot(yg.astype(BF16), w_ref[...], preferred_element_type=F32) + b_ref[...]
        s_ref[...] = yg * _sigmoid(z)
        y0_ref[...] = y0
        z_ref[...] = z

    row = pl.BlockSpec((tr, w), lambda i: (i, 0))
    vec = pl.BlockSpec((1, w), lambda i: (0, 0))
    mat = pl.BlockSpec((w, w), lambda i: (0, 0))
    sh = jax.ShapeDtypeStruct((t, w), F32)
    return pl.pallas_call(body, out_shape=(sh, sh, sh), grid=(t // tr,),
                          in_specs=[row, row, row, vec, mat, vec], out_specs=(row, row, row),
                          compiler_params=_cparams(("parallel",)), name=name)(u, yf, yb, d, wglu, bglu)


def _ssm_post_bwd(ds, y0, z, u, d, wglu, name):
    t, w = u.shape
    tr = _row_tile(t)

    def body(ds_ref, y0_ref, z_ref, u_ref, d_ref, w_ref, dy0_ref, dw_ref, db_ref, dd_ref):
        @pl.when(pl.program_id(0) == 0)
        def _():
            dw_ref[...] = jnp.zeros_like(dw_ref)
            db_ref[...] = jnp.zeros_like(db_ref)
            dd_ref[...] = jnp.zeros_like(dd_ref)

        y0 = y0_ref[...]
        yg = _gelu(y0)
        sg = _sigmoid(z_ref[...])
        dsv = ds_ref[...]
        dz = dsv * yg * sg * (1.0 - sg)
        dzb = dz.astype(BF16)
        dyg = dsv * sg + lax.dot_general(dzb, w_ref[...], (((1,), (1,)), ((), ())), preferred_element_type=F32)
        dy0 = dyg * _gelu_grad(y0)
        dy0_ref[...] = dy0
        dw_ref[...] += lax.dot_general(yg.astype(BF16), dzb, (((0,), (0,)), ((), ())), preferred_element_type=F32)
        db_ref[...] += jnp.sum(dz, axis=0, keepdims=True)
        dd_ref[...] += jnp.sum(dy0 * u_ref[...], axis=0, keepdims=True)

    row = pl.BlockSpec((tr, w), lambda i: (i, 0))
    vec = pl.BlockSpec((1, w), lambda i: (0, 0))
    mat = pl.BlockSpec((w, w), lambda i: (0, 0))
    return pl.pallas_call(
        body, out_shape=(jax.ShapeDtypeStruct((t, w), F32), jax.ShapeDtypeStruct((w, w), F32),
                         jax.ShapeDtypeStruct((1, w), F32), jax.ShapeDtypeStruct((1, w), F32)),
        grid=(t // tr,), in_specs=[row, row, row, row, vec, mat], out_specs=(row, mat, vec, vec),
        compiler_params=_cparams(("arbitrary",)), name=name)(ds, y0, z, u, d, wglu)


def _du_combine(dy0, d, du_f, du_b, name):
    t, w = dy0.shape
    tr = _row_tile(t)

    def body(dy_ref, d_ref, a_ref, b_ref, o_ref):
        o_ref[...] = d_ref[...] * dy_ref[...] + a_ref[...] + b_ref[...]

    row = pl.BlockSpec((tr, w), lambda i: (i, 0))
    vec = pl.BlockSpec((1, w), lambda i: (0, 0))
    return pl.pallas_call(body, out_shape=jax.ShapeDtypeStruct((t, w), F32), grid=(t // tr,),
                          in_specs=[row, vec, row, row], out_specs=row, compiler_params=_cparams(("parallel",)),
                          name=name)(dy0, d, du_f, du_b)


def _ffn_fwd(x, g, wg, wu, wd, tag):
    xo, h, gate, up = _ffn_fwd_call(x, g, wg, wu, wd, f"{tag}_fwd")
    return xo, (h, gate, up)


def _ffn_bwd(dxo, dxo_b, x, g, wg, wu, wd, saved, tag):
    h, gate, up = saved
    dx, dx_b, dg, dgate, dup, act = _ffn_bwd_x_call(dxo, dxo_b, x, g, gate, up, wg, wu, wd, f"{tag}_bwd_x")
    dwg, dwu, dwd = _ffn_bwd_w_call(h, dxo_b, dgate, dup, act, f"{tag}_bwd_w")
    return dx, dx_b, dg, dwg, dwu, dwd


def _heads_split(q, k, v):
    t = q.shape[0]
    q4 = q.reshape(t, KV_HEADS, GQ, HEAD_DIM).transpose(1, 2, 0, 3)
    pad = lambda a: jnp.pad(a.reshape(t, KV_HEADS, HEAD_DIM).transpose(1, 0, 2), ((0, 0), (QBLOCK, QBLOCK), (0, 0)))
    return q4, pad(k), pad(v)


def _local_step(x, tgt, w, get_weights, put_grads, reduce_wide):
    t = x.shape[0]
    row = lambda a: a.reshape(1, -1)
    grads = {}

    w = dict(w)
    w.update(get_weights('ffn1', x))
    x1, ffn1_saved = _ffn_fwd(x, w['norm_ffn1'], w['ffn1_w_gate'], w['ffn1_w_up'], w['ffn1_w_down'], "ffn1")
    w.update(get_weights('mix', x1))

    h2 = _rms_fwd(x1, w['norm_mix'], "mix_norm")
    proj = _mm(h2, w['w_in'], tb=True, name="in_proj")[0]
    q = proj[:, :ATTN_WIDTH]
    k = proj[:, ATTN_WIDTH:ATTN_WIDTH + KV_WIDTH]
    v = proj[:, ATTN_WIDTH + KV_WIDTH:ATTN_WIDTH + 2 * KV_WIDTH]
    u = proj[:, ATTN_WIDTH + 2 * KV_WIDTH:]

    q4, kp, vp = _heads_split(q, k, v)
    sink_rows = jnp.repeat(w['attn_sinks'].reshape(KV_HEADS, GQ), QBLOCK, axis=1)[..., None]
    slopes = jnp.asarray(2.0 ** (-8.0 * (np.arange(ATTN_HEADS) + 1) / ATTN_HEADS), F32)
    slope_rows = jnp.repeat(slopes.reshape(KV_HEADS, GQ), QBLOCK, axis=1)[..., None]
    o4, lse4 = _attn_fwd(q4, kp, vp, sink_rows, slope_rows, "attn_fwd")
    attn = o4.transpose(2, 0, 1, 3).reshape(t, ATTN_WIDTH)

    ssm_names = ['ssm_lambda_re', 'ssm_lambda_im', 'ssm_log_dt', 'ssm_b_re', 'ssm_b_im', 'ssm_c_re', 'ssm_c_im']
    ys, states, preps, vjps = [], [], [], []
    for direction in range(2):
        params = [w[n][direction] for n in ssm_names]
        prep, vjp = jax.vjp(_ssm_prep, *params)
        a_re, a_im = prep[0], prep[1]
        mi_re, mi_im, mo_re, mo_im = (m.astype(BF16) for m in prep[2:])
        prep = (a_re, a_im, mi_re, mi_im, mo_re, mo_im)
        rev = direction == 1
        tab = _scan_tables(a_re, a_im, rev)
        y, xr, xi = _scan(u, mi_re, mi_im, tab, mo_re, mo_im, rev, f"s5_fwd{direction}")
        ys.append(y)
        states.append((xr, xi))
        preps.append(prep)
        vjps.append(vjp)
    d_row = row(w['ssm_d'])
    s, y0, z = _ssm_post_fwd(u, ys[0], ys[1], d_row, w['ssm_glu_w'], row(w['ssm_glu_b']), "ssm_post")

    ma = _rms_fwd(attn, row(w['attn_out_norm']), "attn_out_norm")
    ms = _rms_fwd(s, row(w['ssm_out_norm']), "ssm_out_norm")
    mixed = jnp.concatenate([ma, ms], axis=-1)
    x2 = _mm(mixed, w['w_out'], res=x1, reduce_s=True, name="out_proj")

    w.update(get_weights('ffn2', x2))
    x3, ffn2_saved = _ffn_fwd(x2, w['norm_ffn2'], w['ffn2_w_gate'], w['ffn2_w_up'], w['ffn2_w_down'], "ffn2")

    loss_row, dx3, dx3_b, dgf = _loss_head(x3, row(w['final_norm']), tgt, "loss_head")
    loss = loss_row[0, 0]
    grads['final_norm'] = dgf.reshape(w['final_norm'].shape)

    dx2, dx2_b, dg, dwg, dwu, dwd = _ffn_bwd(dx3, dx3_b, x2, w['norm_ffn2'], w['ffn2_w_gate'], w['ffn2_w_up'],
                                             w['ffn2_w_down'], ffn2_saved, "ffn2")
    grads['norm_ffn2'] = dg
    sent = put_grads('ffn2', dict(ffn2_w_gate=dwg, ffn2_w_up=dwu, ffn2_w_down=dwd))

    dmixed = _mm(dx2_b, w['w_out'], tb=True, reduce_s=True, after=sent, name="out_proj_dx")
    dw_out = _mm(mixed, dx2_b, ta=True, out_dtype=BF16, name="out_proj_dw")[0]
    dattn, _, dga = _rms_bwd(attn, row(w['attn_out_norm']), dmixed[:, :ATTN_WIDTH], None, "attn_out_dnorm")
    ds, _, dgs = _rms_bwd(s, row(w['ssm_out_norm']), dmixed[:, ATTN_WIDTH:], None, "ssm_out_dnorm")
    grads.update(attn_out_norm=dga, ssm_out_norm=dgs)

    dy0, dwglu, dbglu, dd = _ssm_post_bwd(ds, y0, z, u, d_row, w['ssm_glu_w'], "ssm_post_bwd")
    grads['ssm_glu_b'] = dbglu
    grads['ssm_d'] = dd.reshape(w['ssm_d'].shape)
    dparams, du_dirs = [], []
    for direction in range(2):
        a_re, a_im, mi_re, mi_im, mo_re, mo_im = preps[direction]
        rev = direction == 1
        tab = _scan_tables(a_re, -a_im, not rev)
        tr3 = lambda m: jnp.swapaxes(m, 1, 2)
        du_dir, lr, li = _scan(dy0, tr3(mo_re), tr3(mo_im), tab, tr3(mi_re), tr3(mi_im), not rev,
                               f"s5_adj{direction}")
        du_dirs.append(du_dir)
        xr, xi = states[direction]
        dmir, dmii, dmor, dmoi, da = _scan_param_grads(u, dy0, xr, xi, lr, li, rev, f"s5_pgrad{direction}")
        da_re = da[:, 0, :].reshape(-1)
        da_im = da[:, 1, :].reshape(-1)
        dparams.append(vjps[direction]((da_re, da_im, dmir, dmii, dmor, dmoi)))
    du = _du_combine(dy0, d_row, du_dirs[0], du_dirs[1], "ssm_du")
    for i, n in enumerate(ssm_names):
        grads[n] = jnp.stack([dparams[0][i], dparams[1][i]])
    wide_sum = reduce_wide(grads)

    do4 = dattn.reshape(t, KV_HEADS, GQ, HEAD_DIM).transpose(1, 2, 0, 3)
    dq4, dkp, dvp, dsink = _attn_bwd(q4, kp, vp, sink_rows, slope_rows, o4, lse4, do4, "attn_bwd")
    grads['attn_sinks'] = dsink.reshape(w['attn_sinks'].shape)
    dq = dq4.transpose(2, 0, 1, 3).reshape(t, ATTN_WIDTH)
    unpad = lambda a: a[:, QBLOCK:QBLOCK + t, :].transpose(1, 0, 2).reshape(t, KV_WIDTH)
    dproj = jnp.concatenate([dq, unpad(dkp), unpad(dvp), du], axis=-1).astype(BF16)

    dw_in = _mm(dproj, h2, ta=True, out_dtype=BF16, after=wide_sum, name="in_proj_dw")[0]
    sent = put_grads('mix', dict(w_in=dw_in, ssm_glu_w=dwglu, w_out=dw_out))
    dh2 = _mm(dproj, w['w_in'], reduce_s=True, after=sent, name="in_proj_dx")
    dx1, dx1_b, dgm = _rms_bwd(x1, w['norm_mix'], dh2, dx2, "mix_dnorm")
    grads['norm_mix'] = dgm

    dx0, _, dg, dwg, dwu, dwd = _ffn_bwd(dx1, dx1_b, x, w['norm_ffn1'], w['ffn1_w_gate'], w['ffn1_w_up'],
                                         w['ffn1_w_down'], ffn1_saved, "ffn1")
    grads['norm_ffn1'] = dg
    put_grads('ffn1', dict(ffn1_w_gate=dwg, ffn1_w_up=dwu, ffn1_w_down=dwd))
    return loss, dx0, grads, wide_sum


HBM_SPEC = pl.BlockSpec(memory_space=pl.ANY)


def _chip_peers(x, y):
    return [(1 - x, y), (x, 1 - y), (1 - x, 1 - y)]


HBM_ONLY = pl.BlockSpec(memory_space=pltpu.HBM)
SEM_SPEC = pl.BlockSpec(memory_space=pltpu.SEMAPHORE)
EFFECT = pltpu.SideEffectType.DATAFLOW_SIDE_EFFECTING


def _place_own(src, slot, name):
    r, c = src.shape
    tr = r // 2

    def body(slot_ref, s_ref, o_ref):
        o_ref[0] = s_ref[...]

    return pl.pallas_call(
        body, out_shape=jax.ShapeDtypeStruct((N_CHIPS, r, c), src.dtype),
        grid_spec=pltpu.PrefetchScalarGridSpec(
            num_scalar_prefetch=1, grid=(2,), in_specs=[pl.BlockSpec((tr, c), lambda i, s: (i, 0))],
            out_specs=pl.BlockSpec((1, tr, c), lambda i, s: (s[0], i, 0))),
        compiler_params=_cparams(("parallel",)), name=name)(slot, src)


def _chip_copies(srcs, lands, send_sems, recv_sems, scatter, landed):
    x, y, c = lax.axis_index("x"), lax.axis_index("y"), lax.axis_index("c")
    me = 2 * x + y
    out = []
    for i in range(len(srcs)):
        for j, (px, py) in enumerate(_chip_peers(x, y)):
            p = 2 * px + py
            src = srcs[i].at[p] if scatter else srcs[i]
            out.append(pltpu.make_async_remote_copy(src, lands[i].at[p if landed else me], send_sems.at[3 * i + j],
                                                    recv_sems.at[3 * i + j], device_id=(px, py, c),
                                                    device_id_type=MESH))
    return out


def _exchange_start(groups, scatter, name):
    sizes = [len(srcs) for srcs, _ in groups]
    flat_src = [a for srcs, _ in groups for a in srcs]
    flat_land = [a for _, lands in groups for a in lands]
    n = len(flat_src)
    ng = len(groups)

    def body(*refs):
        src_refs, land_refs = refs[:n], refs[n:2 * n]
        sems = refs[2 * n:2 * n + 2 * ng]
        token_ref = refs[-1]
        off = 0
        for gi, sz in enumerate(sizes):
            for cp in _chip_copies(src_refs[off:off + sz], land_refs[off:off + sz], sems[2 * gi], sems[2 * gi + 1],
                                   scatter, landed=False):
                cp.start()
            off += sz
        token_ref[...] = jnp.zeros_like(token_ref)

    sem_shapes = []
    for sz in sizes:
        sem_shapes += [pltpu.SemaphoreType.DMA((3 * sz,)), pltpu.SemaphoreType.DMA((3 * sz,))]
    hbm = lambda a: pltpu.HBM(a.shape, a.dtype)
    res = pl.pallas_call(
        body, name=name,
        out_shape=(tuple(sem_shapes) + tuple(hbm(a) for a in flat_src) + tuple(hbm(a) for a in flat_land)
                   + (jax.ShapeDtypeStruct((SUBLANES, LANES), F32),)),
        in_specs=[HBM_ONLY] * (2 * n),
        out_specs=tuple([SEM_SPEC] * (2 * ng) + [HBM_ONLY] * (2 * n) + [pl.BlockSpec(memory_space=pltpu.VMEM)]),
        input_output_aliases={k: 2 * ng + k for k in range(2 * n)},
        compiler_params=pltpu.CompilerParams(has_side_effects=EFFECT),
    )(*[pltpu.with_memory_space_constraint(a, pltpu.HBM) for a in flat_src + flat_land])
    sems, thru_src, thru_land = res[:2 * ng], res[2 * ng:2 * ng + n], res[2 * ng + n:2 * ng + 2 * n]
    out, off = [], 0
    for gi, sz in enumerate(sizes):
        out.append((sems[2 * gi], sems[2 * gi + 1], list(thru_src[off:off + sz]), list(thru_land[off:off + sz])))
        off += sz
    return out, res[-1]


def _exchange_wait(started, after, scatter, name):
    send_sems, recv_sems, srcs, lands = started
    n = len(srcs)

    def body(*refs):
        src_refs, land_refs = refs[:n], refs[n:2 * n]
        send_ref, recv_ref = refs[2 * n], refs[2 * n + 1]
        for cp in _chip_copies(src_refs, land_refs, send_ref, recv_ref, scatter, landed=True):
            cp.wait_send()
            cp.wait_recv()

    hbm = lambda a: pltpu.HBM(a.shape, a.dtype)
    res = pl.pallas_call(
        body, name=name, out_shape=tuple(hbm(a) for a in srcs) + tuple(hbm(a) for a in lands),
        in_specs=[HBM_ONLY] * (2 * n) + [SEM_SPEC, SEM_SPEC, HBM_SPEC], out_specs=tuple([HBM_ONLY] * (2 * n)),
        input_output_aliases={k: k for k in range(2 * n)},
        compiler_params=pltpu.CompilerParams(has_side_effects=EFFECT),
    )(*srcs, *lands, send_sems, recv_sems, after)
    return list(res[:n]), list(res[n:])


def _small_exchange(smalls, name):
    nsm = len(smalls)
    rels = [(fx, fy, fc) for fx in (0, 1) for fy in (0, 1) for fc in (0, 1)][1:]

    def body(*refs):
        sins, souts = refs[:nsm], refs[nsm:2 * nsm]
        ssend, srecv, slocal = refs[2 * nsm:]
        x, y, c = lax.axis_index("x"), lax.axis_index("y"), lax.axis_index("c")
        lin = 4 * x + 2 * y + c
        local = [pltpu.make_async_copy(sins[i], souts[i].at[lin], slocal.at[i]) for i in range(nsm)]
        for cp in local:
            cp.start()
        for i in range(nsm):
            for j, (fx, fy, fc) in enumerate(rels):
                pltpu.make_async_remote_copy(sins[i], souts[i].at[lin], ssend.at[i, j], srecv.at[i, j],
                                             device_id=(x ^ fx, y ^ fy, c ^ fc), device_id_type=MESH).start()
        for i in range(nsm):
            for j, (fx, fy, fc) in enumerate(rels):
                src = 4 * (x ^ fx) + 2 * (y ^ fy) + (c ^ fc)
                pltpu.make_async_remote_copy(sins[i], souts[i].at[src], ssend.at[i, j], srecv.at[i, j],
                                             device_id=(x ^ fx, y ^ fy, c ^ fc), device_id_type=MESH).wait()
        for cp in local:
            cp.wait()

    return pl.pallas_call(
        body, out_shape=[jax.ShapeDtypeStruct((N_DEV,) + s.shape, s.dtype) for s in smalls],
        in_specs=[HBM_SPEC] * nsm, out_specs=[HBM_SPEC] * nsm,
        scratch_shapes=[pltpu.SemaphoreType.DMA((nsm, 7)), pltpu.SemaphoreType.DMA((nsm, 7)),
                        pltpu.SemaphoreType.DMA((nsm,))],
        name=name)(*smalls)


def _sibling_swap(arrs, name):
    nw = len(arrs)

    def body(*refs):
        ins, outs = refs[:nw], refs[nw:2 * nw]
        send_sems, recv_sems = refs[2 * nw:]
        x, y, c = lax.axis_index("x"), lax.axis_index("y"), lax.axis_index("c")
        cps = [pltpu.make_async_remote_copy(ins[i], outs[i], send_sems.at[i], recv_sems.at[i],
                                            device_id=(x, y, 1 - c), device_id_type=MESH) for i in range(nw)]
        for cp in cps:
            cp.start()
        for cp in cps:
            cp.wait()

    return pl.pallas_call(
        body, out_shape=[jax.ShapeDtypeStruct(a.shape, a.dtype) for a in arrs],
        in_specs=[HBM_SPEC] * nw, out_specs=[HBM_SPEC] * nw,
        scratch_shapes=[pltpu.SemaphoreType.DMA((nw,)), pltpu.SemaphoreType.DMA((nw,))],
        name=name)(*arrs)


def _sum_parts(parts, recv, slots, name):
    _, r, c = parts.shape
    tr = _row_tile(r)

    def body(slot_ref, own_ref, r0_ref, r1_ref, r2_ref, o_ref):
        o_ref[...] = ((own_ref[0].astype(F32) + r0_ref[0].astype(F32))
                      + (r1_ref[0].astype(F32) + r2_ref[0].astype(F32)))

    blk = lambda k: pl.BlockSpec((1, tr, c), lambda i, s, k=k: (s[k], i, 0))
    return pl.pallas_call(
        body, out_shape=jax.ShapeDtypeStruct((r, c), F32),
        grid_spec=pltpu.PrefetchScalarGridSpec(
            num_scalar_prefetch=1, grid=(r // tr,), in_specs=[blk(0), blk(1), blk(2), blk(3)],
            out_specs=pl.BlockSpec((tr, c), lambda i, s: (i, 0))),
        compiler_params=_cparams(("parallel",)), name=name)(slots, parts, recv, recv, recv)


def _small_allreduce(packed, name):
    rows = packed.shape[0]
    pr = rows // N_DEV
    rels = [(fx, fy, fc) for fx in (0, 1) for fy in (0, 1) for fc in (0, 1)][1:]

    def body(in_ref, out_ref, recv_ref, send1, recv1, send2, recv2):
        x, y, c = lax.axis_index("x"), lax.axis_index("y"), lax.axis_index("c")
        lin = 4 * x + 2 * y + c
        piece = lambda ref, k: ref.at[pl.ds(pl.multiple_of(k * pr, pr), pr), :]
        peers = [((x ^ fx, y ^ fy, c ^ fc), 4 * (x ^ fx) + 2 * (y ^ fy) + (c ^ fc)) for fx, fy, fc in rels]
        for j, (dev, plin) in enumerate(peers):
            pltpu.make_async_remote_copy(piece(in_ref, plin), recv_ref.at[lin], send1.at[j], recv1.at[j],
                                         device_id=dev, device_id_type=MESH).start()
        recv_ref[lin] = piece(in_ref, lin)[...]
        for j, (dev, plin) in enumerate(peers):
            pltpu.make_async_remote_copy(piece(in_ref, plin), recv_ref.at[plin], send1.at[j], recv1.at[j],
                                         device_id=dev, device_id_type=MESH).wait()
        acc = recv_ref[0]
        for k in range(1, N_DEV):
            acc = acc + recv_ref[k]
        piece(out_ref, lin)[...] = acc
        for j, (dev, plin) in enumerate(peers):
            pltpu.make_async_remote_copy(piece(out_ref, lin), piece(out_ref, lin), send2.at[j], recv2.at[j],
                                         device_id=dev, device_id_type=MESH).start()
        for j, (dev, plin) in enumerate(peers):
            pltpu.make_async_remote_copy(piece(out_ref, lin), piece(out_ref, plin), send2.at[j], recv2.at[j],
                                         device_id=dev, device_id_type=MESH).wait()

    vm = pl.BlockSpec(memory_space=pltpu.VMEM)
    return pl.pallas_call(
        body, out_shape=jax.ShapeDtypeStruct(packed.shape, F32), in_specs=[vm], out_specs=vm,
        scratch_shapes=[pltpu.VMEM((N_DEV, pr, LANES), F32)] + [pltpu.SemaphoreType.DMA((7,))] * 4,
        compiler_params=pltpu.CompilerParams(vmem_limit_bytes=VMEM_LIMIT), name=name)(packed)


def _adamw_math(w, m, v, g):
    nm = ADAM_B1 * m + (1.0 - ADAM_B1) * g
    nv = ADAM_B2 * v + (1.0 - ADAM_B2) * (g * g)
    m_hat = nm * (1.0 / (1.0 - ADAM_B1 ** ADAM_STEP))
    v_hat = nv * (1.0 / (1.0 - ADAM_B2 ** ADAM_STEP))
    return -ADAM_LR * (m_hat / (jnp.sqrt(v_hat) + ADAM_EPS) + ADAM_WD * w), nm, nv


def _adamw(w, m, v, g_mine, g_other, name):
    r, c = w.shape
    tr = _row_tile(r)

    def body(w_ref, m_ref, v_ref, g1_ref, g2_ref, g_ref, d_ref, nm_ref, nv_ref):
        g = g1_ref[...] + g2_ref[...]
        g_ref[...] = g
        d_ref[...], nm_ref[...], nv_ref[...] = _adamw_math(w_ref[...], m_ref[...], v_ref[...], g)

    blk = pl.BlockSpec((tr, c), lambda i: (i, 0))
    sh = jax.ShapeDtypeStruct((r, c), F32)
    return pl.pallas_call(body, out_shape=(sh, sh, sh, sh), grid=(r // tr,), in_specs=[blk] * 5,
                          out_specs=(blk, blk, blk, blk), compiler_params=_cparams(("parallel",)),
                          name=name)(w, m, v, g_mine, g_other)


def _adamw_small(ws, ms, vs, alls, split, name):
    n = len(ws)
    lead = split if split is not None else ()
    nl = len(lead)
    nslots = alls[0].shape[0]

    def blocks(shape):
        if split is None:
            return tuple(shape), (lambda *g: (0,) * len(shape))
        blk = (shape[0], shape[1] // lead[0], shape[2] // lead[1]) + tuple(shape[3:])
        return blk, (lambda *g: (0, g[0], g[1]) + (0,) * (len(shape) - 3))

    def body(*refs):
        w_refs, m_refs, v_refs, a_refs = (refs[k * n:(k + 1) * n] for k in range(4))
        g_refs, d_refs, nm_refs, nv_refs = (refs[(4 + k) * n:(5 + k) * n] for k in range(4))
        k = pl.program_id(nl)
        for i in range(n):
            @pl.when(k == 0)
            def _(i=i):
                g_refs[i][...] = a_refs[i][0]

            @pl.when(k > 0)
            def _(i=i):
                g_refs[i][...] += a_refs[i][0]

            @pl.when(k == nslots - 1)
            def _(i=i):
                d_refs[i][...], nm_refs[i][...], nv_refs[i][...] = _adamw_math(
                    w_refs[i][...], m_refs[i][...], v_refs[i][...], g_refs[i][...])

    specs, aspecs, shapes = [], [], []
    for wa in ws:
        blk, imap = blocks(wa.shape)
        specs.append(pl.BlockSpec(blk, imap))
        aspecs.append(pl.BlockSpec((1,) + blk, (lambda *g, imap=imap: (g[nl],) + imap(*g))))
        shapes.append(jax.ShapeDtypeStruct(wa.shape, F32))
    res = pl.pallas_call(
        body, out_shape=shapes * 4, grid=tuple(lead) + (nslots,), in_specs=specs * 3 + aspecs,
        out_specs=specs * 4, compiler_params=_cparams(("parallel",) * nl + ("arbitrary",)),
        name=name)(*ws, *ms, *vs, *alls)
    return res[:n], res[n:2 * n], res[2 * n:3 * n], res[3 * n:]


def kernel(x, norm_ffn1, ffn1_w_gate, ffn1_w_up, ffn1_w_down, norm_mix, w_in, attn_sinks, ssm_lambda_re, ssm_lambda_im, ssm_log_dt, ssm_b_re, ssm_b_im, ssm_c_re, ssm_c_im, ssm_d, ssm_glu_w, ssm_glu_b, attn_out_norm, ssm_out_norm, w_out, norm_ffn2, ffn2_w_gate, ffn2_w_up, ffn2_w_down, final_norm, loss_target, m_norm_ffn1, m_ffn1_w_gate, m_ffn1_w_up, m_ffn1_w_down, m_norm_mix, m_w_in, m_attn_sinks, m_ssm_lambda_re, m_ssm_lambda_im, m_ssm_log_dt, m_ssm_b_re, m_ssm_b_im, m_ssm_c_re, m_ssm_c_im, m_ssm_d, m_ssm_glu_w, m_ssm_glu_b, m_attn_out_norm, m_ssm_out_norm, m_w_out, m_norm_ffn2, m_ffn2_w_gate, m_ffn2_w_up, m_ffn2_w_down, m_final_norm, v_norm_ffn1, v_ffn1_w_gate, v_ffn1_w_up, v_ffn1_w_down, v_norm_mix, v_w_in, v_attn_sinks, v_ssm_lambda_re, v_ssm_lambda_im, v_ssm_log_dt, v_ssm_b_re, v_ssm_b_im, v_ssm_c_re, v_ssm_c_im, v_ssm_d, v_ssm_glu_w, v_ssm_glu_b, v_attn_out_norm, v_ssm_out_norm, v_w_out, v_norm_ffn2, v_ffn2_w_gate, v_ffn2_w_up, v_ffn2_w_down, v_final_norm):
    given = dict(locals())
    wts = {n: given[n] for n in WEIGHTS}

    order = [g for g in GROUPS]
    cx, cy = lax.axis_index("x"), lax.axis_index("y")
    slots = jnp.stack([2 * cx + cy, 2 * (1 - cx) + cy, 2 * cx + 1 - cy, 2 * (1 - cx) + 1 - cy]).astype(jnp.int32)
    def view(a, n):
        if n in TRANSPOSED:
            return jnp.swapaxes(a[0], 0, 1)
        if n in BIG:
            return a[0]
        if n in ('ssm_b_re', 'ssm_b_im'):
            return jnp.swapaxes(a, -1, -2)
        return a.reshape(1, -1) if a.ndim == 1 else a

    def unview(a, n):
        if n in TRANSPOSED:
            return jnp.swapaxes(a, 0, 1)[None]
        if n in ('ssm_b_re', 'ssm_b_im'):
            return jnp.swapaxes(a, -1, -2)
        return a.reshape(wts[n].shape)

    shards = {n: view(wts[n], n).astype(BF16) for n in BIG}
    placed = {n: _place_own(shards[n], slots, f"weights_place_{n}") for n in BIG}
    started, _ = _exchange_start([([shards[n] for n in GROUPS[g]], [placed[n] for n in GROUPS[g]]) for g in order],
                                 False, "weights_start")
    started = dict(zip(order, started))

    def get_weights(group, after):
        _, lands = _exchange_wait(started[group], after, False, f"weights_wait_{group}")
        out = dict(zip(GROUPS[group], lands))
        for n in ('w_in', 'ssm_glu_w', 'w_out'):
            if n in out:
                out[n] = out[n].reshape(-1, out[n].shape[-1])
        return out

    sent, tokens = {}, {}

    def put_grads(group, gd):
        parts = []
        for n in GROUPS[group]:
            g = gd[n]
            if g.ndim == 2:
                g = g.reshape(N_CHIPS, g.shape[0] // N_CHIPS, g.shape[1])
            parts.append(g.astype(BF16))
        lands = [lax.empty(p.shape, p.dtype) for p in parts]
        started_g, tokens[group] = _exchange_start([(parts, lands)], True, f"grads_start_{group}")
        sent[group] = started_g[0]
        return tokens[group]

    w = {n: (wts[n][0] if wts[n].ndim > 1 else wts[n]) for n in SMALL}
    w['norm_ffn1'], w['norm_mix'], w['norm_ffn2'] = wts['norm_ffn1'], wts['norm_mix'], wts['norm_ffn2']
    w['ssm_b_re'], w['ssm_b_im'] = view(wts['ssm_b_re'], 'ssm_b_re')[0], view(wts['ssm_b_im'], 'ssm_b_im')[0]
    wide = ['ssm_b_re', 'ssm_b_im', 'ssm_c_re', 'ssm_c_im']

    def reduce_wide(gd):
        packed = jnp.concatenate([gd[n].reshape(-1, LANES) for n in wide])
        return _small_allreduce(packed, "small_grads_allreduce")

    loss, dx, grads, wide_sum = _local_step(x[0], loss_target[0], w, get_weights, put_grads, reduce_wide)
    loss = lax.psum(loss, ("x", "y", "c"))

    out_g, out_d, out_m, out_v = {}, {}, {}, {}

    def finish(group, after):
        names = GROUPS[group]
        parts, recv = _exchange_wait(sent[group], after, True, f"grads_wait_{group}")
        chip_sums = [_sum_parts(p, r, slots, f"grad_sum_{n}") for n, p, r in zip(names, parts, recv)]
        other = _sibling_swap(chip_sums, f"grad_sibling_swap_{group}")
        for n, mine, oth in zip(names, chip_sums, other):
            g, d, nm, nv = _adamw(view(wts[n], n), view(given['m_' + n], n), view(given['v_' + n], n), mine, oth,
                                  f"adamw_{n}")
            out_g[n], out_d[n], out_m[n], out_v[n] = (unview(a, n) for a in (g, d, nm, nv))
        return nv

    done = finish('ffn2', tokens['ffn1'])
    done = finish('mix', done)

    nat = {n: view(wts[n], n).shape for n in SMALL}
    narrow = [n for n in SMALL if n not in wide]
    alls = _small_exchange([grads[n].reshape(nat[n]) for n in narrow], "small_grads_allgather")
    rows = wide_sum.shape[0] // len(wide)
    wide_g = [wide_sum[i * rows:(i + 1) * rows].reshape((1,) + nat[n]) for i, n in enumerate(wide)]
    for group, gs, split, tag in ((narrow, alls, None, "adamw_small"), (wide, wide_g, (2, 4), "adamw_ssm_bc")):
        res = _adamw_small([view(wts[n], n) for n in group], [view(given['m_' + n], n) for n in group],
                           [view(given['v_' + n], n) for n in group], gs, split, tag)
        for dst, vals in zip((out_g, out_d, out_m, out_v), res):
            for n, a in zip(group, vals):
                dst[n] = unview(a, n)

    finish('ffn1', out_v['ssm_b_re'])

    return (loss, dx[None], *[out_g[n] for n in WEIGHTS], *[out_d[n] for n in WEIGHTS],
            *[out_m[n] for n in WEIGHTS], *[out_v[n] for n in WEIGHTS])
```

```python
import functools
import math

import numpy as np
import jax
import jax.numpy as jnp
from jax import lax
from jax.experimental import pallas as pl
from jax.experimental.pallas import tpu as pltpu

F32 = jnp.float32
BF16 = jnp.bfloat16
MESH = pl.DeviceIdType.MESH

EPS = 1e-6
NEG_INF = -1e30
LAMBDA_RE_MAX = -1e-4
ATTN_HEADS = 8
KV_HEADS = 2
GQ = ATTN_HEADS // KV_HEADS
HEAD_DIM = 64
ATTN_WIDTH = 512
KV_WIDTH = 128
WINDOW = 128
QBLOCK = 128
SSM_WIDTH = 512
SSM_GROUPS = 32
SSM_CH = 16
SSM_STATE = 64
N_STRIPS = 4
STRIP_IN = SSM_WIDTH // N_STRIPS
STRIP_ST = SSM_GROUPS * SSM_STATE // N_STRIPS
SUBLANES = 8
LANES = 128
N_CHIPS = 4
N_DEV = 8

ADAM_LR = 0.001
ADAM_B1 = 0.9
ADAM_B2 = 0.999
ADAM_EPS = 1e-08
ADAM_WD = 0.01
ADAM_STEP = 10

VMEM_LIMIT = 48 * 1024 * 1024

WEIGHTS = ['norm_ffn1', 'ffn1_w_gate', 'ffn1_w_up', 'ffn1_w_down', 'norm_mix', 'w_in', 'attn_sinks',
           'ssm_lambda_re', 'ssm_lambda_im', 'ssm_log_dt', 'ssm_b_re', 'ssm_b_im', 'ssm_c_re', 'ssm_c_im',
           'ssm_d', 'ssm_glu_w', 'ssm_glu_b', 'attn_out_norm', 'ssm_out_norm', 'w_out', 'norm_ffn2',
           'ffn2_w_gate', 'ffn2_w_up', 'ffn2_w_down', 'final_norm']
BIG = ['ffn1_w_gate', 'ffn1_w_up', 'ffn1_w_down', 'w_in', 'ssm_glu_w', 'w_out',
       'ffn2_w_gate', 'ffn2_w_up', 'ffn2_w_down']
SMALL = [n for n in WEIGHTS if n not in BIG]
TRANSPOSED = ['ffn1_w_gate', 'ffn1_w_up', 'w_in', 'ffn2_w_gate', 'ffn2_w_up']
GROUPS = {'ffn1': ['ffn1_w_gate', 'ffn1_w_up', 'ffn1_w_down'],
          'mix': ['w_in', 'ssm_glu_w', 'w_out'],
          'ffn2': ['ffn2_w_gate', 'ffn2_w_up', 'ffn2_w_down']}


def _cparams(sem=None):
    return pltpu.CompilerParams(dimension_semantics=sem, vmem_limit_bytes=VMEM_LIMIT)


def _tile(n, pref):
    if n <= pref:
        return n
    for t in (pref, pref // 2, pref // 4):
        if t % LANES == 0 and n % t == 0:
            return t
    return n


def _sigmoid(x):
    return 1.0 / (1.0 + jnp.exp(-x))


def _mm(a, b, *, ta=False, tb=False, reduce_s=False, res=None, scale=1.0, out_dtype=F32, after=None, name):
    a3 = a if a.ndim == 3 else a[None]
    b3 = b if b.ndim == 3 else b[None]
    sa, sb = a3.shape[0], b3.shape[0]
    ns = max(sa, sb)
    (kk, m) = a3.shape[1:] if ta else a3.shape[1:][::-1]
    (n, kb) = b3.shape[1:] if tb else b3.shape[1:][::-1]
    assert kk == kb, (a3.shape, b3.shape)
    tm, tn, tk = _tile(m, 1024), _tile(n, 1024), _tile(kk, 2048)
    nm, nn, nk = m // tm, n // tn, kk // tk
    has_res = res is not None
    single = nk == 1 and not (reduce_s and ns > 1)

    if reduce_s:
        grid = (nm, nn, ns, nk)
        ids = lambda i, j, s, k: (s, i, j, k)
        sem = ("parallel", "parallel", "arbitrary", "arbitrary")
    else:
        grid = (ns, nm, nn, nk)
        ids = lambda s, i, j, k: (s, i, j, k)
        sem = ("parallel", "parallel", "parallel", "arbitrary")

    def a_map(*g):
        s, i, j, k = ids(*g)
        s = s if sa > 1 else 0
        return (s, k, i) if ta else (s, i, k)

    def b_map(*g):
        s, i, j, k = ids(*g)
        s = s if sb > 1 else 0
        return (s, j, k) if tb else (s, k, j)

    def o_map(*g):
        s, i, j, k = ids(*g)
        return (i, j) if reduce_s else (s, i, j)

    a_blk = (1, tk, tm) if ta else (1, tm, tk)
    b_blk = (1, tn, tk) if tb else (1, tk, tn)
    dims = (((0 if ta else 1,), (1 if tb else 0,)), ((), ()))

    def body(*refs):
        a_ref, b_ref = refs[0], refs[1]
        r_ref = refs[2] if has_res else None
        o_ref = refs[2 + has_res + (after is not None)]
        acc_ref = None if single else refs[-1]
        s, _, _, k = ids(*[pl.program_id(d) for d in range(4)])
        prod = lax.dot_general(a_ref[0].astype(BF16), b_ref[0].astype(BF16), dims, preferred_element_type=F32)

        def finish(out):
            if scale != 1.0:
                out = out * scale
            if has_res:
                out = r_ref[...].reshape(out.shape) + out
            o_ref[...] = out.astype(out_dtype).reshape(o_ref.shape)

        if single:
            finish(prod)
            return
        if reduce_s:
            first = jnp.logical_and(s == 0, k == 0)
            last = jnp.logical_and(s == ns - 1, k == nk - 1)
        else:
            first, last = k == 0, k == nk - 1

        @pl.when(first)
        def _():
            acc_ref[...] = prod

        @pl.when(jnp.logical_not(first))
        def _():
            acc_ref[...] += prod

        @pl.when(last)
        def _():
            finish(acc_ref[...])

    in_specs = [pl.BlockSpec(a_blk, a_map), pl.BlockSpec(b_blk, b_map)]
    args = [a3, b3]
    if reduce_s:
        out_shape = jax.ShapeDtypeStruct((m, n), out_dtype)
        o_spec = pl.BlockSpec((tm, tn), o_map)
    else:
        out_shape = jax.ShapeDtypeStruct((ns, m, n), out_dtype)
        o_spec = pl.BlockSpec((1, tm, tn), o_map)
    if has_res:
        assert res.shape == out_shape.shape
        in_specs.append(o_spec)
        args.append(res)
    if after is not None:
        in_specs.append(HBM_SPEC)
        args.append(after)
    return pl.pallas_call(body, out_shape=out_shape, grid=grid, in_specs=in_specs, out_specs=o_spec,
                          scratch_shapes=[] if single else [pltpu.VMEM((tm, tn), F32)],
                          compiler_params=_cparams(sem), name=name)(*args)


def _row_tile(t):
    for tr in (256, 128, 64, 32, 16, 8):
        if t % tr == 0:
            return tr
    return t


def _rms_fwd(x, g, name):
    t, w = x.shape
    tr = _row_tile(t)

    def body(x_ref, g_ref, o_ref):
        xv = x_ref[...]
        r = lax.rsqrt(jnp.mean(xv * xv, axis=-1, keepdims=True) + EPS)
        o_ref[...] = (xv * r * g_ref[...]).astype(BF16)

    return pl.pallas_call(
        body, out_shape=jax.ShapeDtypeStruct((t, w), BF16), grid=(t // tr,),
        in_specs=[pl.BlockSpec((tr, w), lambda i: (i, 0)), pl.BlockSpec((1, w), lambda i: (0, 0))],
        out_specs=pl.BlockSpec((tr, w), lambda i: (i, 0)), compiler_params=_cparams(("parallel",)),
        name=name)(x, g)


def _rms_bwd_rows(xv, gv, dhv):
    r = lax.rsqrt(jnp.mean(xv * xv, axis=-1, keepdims=True) + EPS)
    nrm = xv * r
    dn = dhv * gv
    return r * (dn - nrm * jnp.mean(dn * nrm, axis=-1, keepdims=True)), dhv * nrm


def _rms_bwd(x, g, dh, dres, name):
    t, w = x.shape
    tr = _row_tile(t)
    has_res = dres is not None

    def body(*refs):
        if has_res:
            x_ref, g_ref, dh_ref, dr_ref, dx_ref, dxb_ref, dg_ref = refs
        else:
            x_ref, g_ref, dh_ref, dx_ref, dxb_ref, dg_ref = refs
        dx, dgs = _rms_bwd_rows(x_ref[...], g_ref[...], dh_ref[...])
        if has_res:
            dx = dx + dr_ref[...]
        dx_ref[...] = dx
        dxb_ref[...] = dx.astype(BF16)

        @pl.when(pl.program_id(0) == 0)
        def _():
            dg_ref[...] = jnp.zeros_like(dg_ref)

        dg_ref[...] += jnp.sum(dgs, axis=0, keepdims=True)

    row = pl.BlockSpec((tr, w), lambda i: (i, 0))
    vec = pl.BlockSpec((1, w), lambda i: (0, 0))
    ins = [x, g, dh] + ([dres] if has_res else [])
    return pl.pallas_call(
        body, out_shape=(jax.ShapeDtypeStruct((t, w), F32), jax.ShapeDtypeStruct((t, w), BF16),
                         jax.ShapeDtypeStruct((1, w), F32)),
        grid=(t // tr,), in_specs=[row, vec, row] + ([row] if has_res else []),
        out_specs=(row, row, vec), compiler_params=_cparams(("arbitrary",)), name=name)(*ins)


FFN_ROWS = 512


NT_DIMS = (((1,), (1,)), ((), ()))
TN_DIMS = (((0,), (0,)), ((), ()))


def _ffn_fwd_call(x, g, wg, wu, wd, name):
    t, d = x.shape
    ns, f, _ = wg.shape
    tm = _tile(t, FFN_ROWS)

    def body(x_ref, g_ref, wg_ref, wu_ref, wd_ref, xo_ref, h_ref, gate_ref, up_ref, h_sc, acc_ref):
        s = pl.program_id(1)

        @pl.when(s == 0)
        def _():
            xv = x_ref[...]
            r = lax.rsqrt(jnp.mean(xv * xv, axis=-1, keepdims=True) + EPS)
            hb = (xv * r * g_ref[...]).astype(BF16)
            h_sc[...] = hb
            h_ref[...] = hb

        hb = h_sc[...]
        gate = lax.dot_general(hb, wg_ref[0], NT_DIMS, preferred_element_type=F32)
        up = lax.dot_general(hb, wu_ref[0], NT_DIMS, preferred_element_type=F32)
        gate_ref[0] = gate.astype(BF16)
        up_ref[0] = up.astype(BF16)
        act = (gate * _sigmoid(gate) * up).astype(BF16)
        prod = jnp.dot(act, wd_ref[0], preferred_element_type=F32)

        @pl.when(s == 0)
        def _():
            acc_ref[...] = prod

        @pl.when(s > 0)
        def _():
            acc_ref[...] += prod

        @pl.when(s == ns - 1)
        def _():
            xo_ref[...] = x_ref[...] + 0.5 * acc_ref[...]

    row = pl.BlockSpec((tm, d), lambda i, s: (i, 0))
    vec = pl.BlockSpec((1, d), lambda i, s: (0, 0))
    wrow = pl.BlockSpec((1, f, d), lambda i, s: (s, 0, 0))
    hid = pl.BlockSpec((1, tm, f), lambda i, s: (s, i, 0))
    hid_sh = jax.ShapeDtypeStruct((ns, t, f), BF16)
    return pl.pallas_call(
        body, out_shape=(jax.ShapeDtypeStruct((t, d), F32), jax.ShapeDtypeStruct((t, d), BF16), hid_sh, hid_sh),
        grid=(t // tm, ns), in_specs=[row, vec, wrow, wrow, wrow], out_specs=(row, row, hid, hid),
        scratch_shapes=[pltpu.VMEM((tm, d), BF16), pltpu.VMEM((tm, d), F32)],
        compiler_params=_cparams(("parallel", "arbitrary")), name=name)(x, g, wg, wu, wd)


def _ffn_bwd_x_call(dxo, dxo_b, x, g, gate, up, wg, wu, wd, name):
    t, d = x.shape
    ns, f, _ = wg.shape
    tm = _tile(t, FFN_ROWS)

    def body(dxo_ref, dxb_ref, x_ref, g_ref, gate_ref, up_ref, wg_ref, wu_ref, wd_ref,
             dx_ref, dxob_ref, dgn_ref, dgate_ref, dup_ref, act_ref, dh_ref):
        i, s = pl.program_id(0), pl.program_id(1)
        dact = lax.dot_general(dxb_ref[...], wd_ref[0], NT_DIMS, preferred_element_type=F32) * 0.5
        gv = gate_ref[0].astype(F32)
        uv = up_ref[0].astype(F32)
        sg = _sigmoid(gv)
        silu = gv * sg
        act_ref[0] = (silu * uv).astype(BF16)
        dub = (dact * silu).astype(BF16)
        dgb = (dact * uv * sg * (1.0 + gv * (1.0 - sg))).astype(BF16)
        dup_ref[0] = dub
        dgate_ref[0] = dgb
        prod = (jnp.dot(dgb, wg_ref[0], preferred_element_type=F32)
                + jnp.dot(dub, wu_ref[0], preferred_element_type=F32))

        @pl.when(s == 0)
        def _():
            dh_ref[...] = prod

        @pl.when(s > 0)
        def _():
            dh_ref[...] += prod

        @pl.when(jnp.logical_and(i == 0, s == 0))
        def _():
            dgn_ref[...] = jnp.zeros_like(dgn_ref)

        @pl.when(s == ns - 1)
        def _():
            dx, dgs = _rms_bwd_rows(x_ref[...], g_ref[...], dh_ref[...])
            dx = dx + dxo_ref[...]
            dx_ref[...] = dx
            dxob_ref[...] = dx.astype(BF16)
            dgn_ref[...] += jnp.sum(dgs, axis=0, keepdims=True)

    row = pl.BlockSpec((tm, d), lambda i, s: (i, 0))
    vec = pl.BlockSpec((1, d), lambda i, s: (0, 0))
    wrow = pl.BlockSpec((1, f, d), lambda i, s: (s, 0, 0))
    hid = pl.BlockSpec((1, tm, f), lambda i, s: (s, i, 0))
    hid_sh = jax.ShapeDtypeStruct((ns, t, f), BF16)
    return pl.pallas_call(
        body,
        out_shape=(jax.ShapeDtypeStruct((t, d), F32), jax.ShapeDtypeStruct((t, d), BF16),
                   jax.ShapeDtypeStruct((1, d), F32), hid_sh, hid_sh, hid_sh),
        grid=(t // tm, ns), in_specs=[row, row, row, vec, hid, hid, wrow, wrow, wrow],
        out_specs=(row, row, vec, hid, hid, hid), scratch_shapes=[pltpu.VMEM((tm, d), F32)],
        compiler_params=_cparams(("arbitrary", "arbitrary")), name=name)(dxo, dxo_b, x, g, gate, up, wg, wu, wd)


def _ffn_bwd_w_call(h, dxo_b, dgate, dup, act, name):
    t, d = h.shape
    ns, _, f = dgate.shape
    tm = _tile(t, FFN_ROWS)
    nm = t // tm

    def body(h_ref, dxb_ref, dgate_ref, dup_ref, act_ref, dwg_ref, dwu_ref, dwd_ref, ag_ref, au_ref, ad_ref):
        i = pl.program_id(1)
        hv = h_ref[...]
        pg = lax.dot_general(dgate_ref[0], hv, TN_DIMS, preferred_element_type=F32)
        pu = lax.dot_general(dup_ref[0], hv, TN_DIMS, preferred_element_type=F32)
        pd = lax.dot_general(act_ref[0], dxb_ref[...], TN_DIMS, preferred_element_type=F32)

        @pl.when(i == 0)
        def _():
            ag_ref[...] = pg
            au_ref[...] = pu
            ad_ref[...] = pd

        @pl.when(i > 0)
        def _():
            ag_ref[...] += pg
            au_ref[...] += pu
            ad_ref[...] += pd

        @pl.when(i == nm - 1)
        def _():
            dwg_ref[0] = ag_ref[...].astype(BF16)
            dwu_ref[0] = au_ref[...].astype(BF16)
            dwd_ref[0] = (0.5 * ad_ref[...]).astype(BF16)

    row = pl.BlockSpec((tm, d), lambda s, i: (i, 0))
    hid = pl.BlockSpec((1, tm, f), lambda s, i: (s, i, 0))
    wrow = pl.BlockSpec((1, f, d), lambda s, i: (s, 0, 0))
    wsh = jax.ShapeDtypeStruct((ns, f, d), BF16)
    return pl.pallas_call(
        body, out_shape=(wsh, wsh, wsh),
        grid=(ns, nm), in_specs=[row, row, hid, hid, hid], out_specs=(wrow, wrow, wrow),
        scratch_shapes=[pltpu.VMEM((f, d), F32), pltpu.VMEM((f, d), F32), pltpu.VMEM((f, d), F32)],
        compiler_params=_cparams(("parallel", "arbitrary")), name=name)(h, dxo_b, dgate, dup, act)


def _loss_head(x, g, tgt, name):
    t, w = x.shape
    tr = _row_tile(t)

    def body(x_ref, g_ref, t_ref, loss_ref, dx_ref, dxb_ref, dg_ref):
        xv = x_ref[...]
        gv = g_ref[...]
        r = lax.rsqrt(jnp.mean(xv * xv, axis=-1, keepdims=True) + EPS)
        nrm = xv * r
        err = nrm * gv - t_ref[...]
        dout = err * (1.0 / w)
        dn = dout * gv
        dx = r * (dn - nrm * jnp.mean(dn * nrm, axis=-1, keepdims=True))
        dx_ref[...] = dx
        dxb_ref[...] = dx.astype(BF16)

        @pl.when(pl.program_id(0) == 0)
        def _():
            dg_ref[...] = jnp.zeros_like(dg_ref)
            loss_ref[...] = jnp.zeros_like(loss_ref)

        dg_ref[...] += jnp.sum(dout * nrm, axis=0, keepdims=True)
        part = jnp.sum(jnp.sum(err * err, axis=-1, keepdims=True) * (0.5 / w), axis=0, keepdims=True)
        loss_ref[...] += jnp.broadcast_to(part, loss_ref.shape)

    row = pl.BlockSpec((tr, w), lambda i: (i, 0))
    vec = pl.BlockSpec((1, w), lambda i: (0, 0))
    return pl.pallas_call(
        body, out_shape=(jax.ShapeDtypeStruct((1, LANES), F32), jax.ShapeDtypeStruct((t, w), F32),
                         jax.ShapeDtypeStruct((t, w), BF16), jax.ShapeDtypeStruct((1, w), F32)),
        grid=(t // tr,), in_specs=[row, vec, row],
        out_specs=(pl.BlockSpec((1, LANES), lambda i: (0, 0)), row, row, vec),
        compiler_params=_cparams(("arbitrary",)), name=name)(x, g, tgt)


def _attn_scores(q, k3, n, t, slope_ref):
    rows = GQ * QBLOCK
    s = lax.dot_general(q, k3, (((1,), (1,)), ((), ())), preferred_element_type=F32) * (HEAD_DIM ** -0.5)
    row = lax.broadcasted_iota(jnp.int32, (rows, 3 * QBLOCK), 0) & (QBLOCK - 1)
    col = lax.broadcasted_iota(jnp.int32, (rows, 3 * QBLOCK), 1)
    rel = jnp.abs(col - QBLOCK - row)
    key_pos = n * QBLOCK - QBLOCK + col
    valid = (rel <= WINDOW) & (key_pos >= 0) & (key_pos < t)
    return jnp.where(valid, s - slope_ref[0] * rel.astype(F32), NEG_INF)


Q_COL, K_COL, V_COL, U_COL = 0, ATTN_WIDTH // LANES, ATTN_WIDTH // LANES + 1, ATTN_WIDTH // LANES + 2


def _key_rows(ref, n, nb):
    prev, nxt = jnp.maximum(n - 1, 0), jnp.minimum(n + 1, nb - 1)
    blk = lambda b: ref[pl.ds(pl.multiple_of(b * QBLOCK, QBLOCK), QBLOCK), :]
    return jnp.concatenate([blk(prev), blk(n), blk(nxt)], axis=0)


def _head_tiles(x, kh, low):
    tiles = []
    for g in range(GQ):
        h = GQ * kh + g
        t128 = x[:, LANES * (h // 2):LANES * (h // 2 + 1)]
        t128 = jnp.where(low if h % 2 == 0 else jnp.logical_not(low), t128, 0.0)
        if h % 2 != kh:
            t128 = pltpu.roll(t128, HEAD_DIM, 1)
        tiles.append(t128)
    return jnp.concatenate(tiles, axis=0)


def _head_merge(per_kh, low):
    out = []
    for j in range(ATTN_HEADS // 2):
        pair = []
        for h in (2 * j, 2 * j + 1):
            kh, g = h // GQ, h % GQ
            t128 = per_kh[kh][g * QBLOCK:(g + 1) * QBLOCK, :]
            if h % 2 != kh:
                t128 = pltpu.roll(t128, HEAD_DIM, 1)
            pair.append(t128)
        out.append(jnp.where(low, pair[0], pair[1]))
    return jnp.concatenate(out, axis=1)


def _attn_fwd_proj(proj, sink_rows, slope_rows, name):
    t = proj.shape[0]
    nb = t // QBLOCK
    rows = GQ * QBLOCK

    def body(q_ref, k_ref, v_ref, sink_ref, slope_ref, o_ref, lse_ref):
        n = pl.program_id(0)
        low = lax.broadcasted_iota(jnp.int32, (QBLOCK, LANES), 1) < HEAD_DIM
        k3 = _key_rows(k_ref, n, nb).astype(BF16)
        v3 = _key_rows(v_ref, n, nb).astype(BF16)
        q = q_ref[...]
        outs = []
        for kh in range(KV_HEADS):
            qs = _head_tiles(q, kh, low).astype(BF16)
            s = _attn_scores(qs, k3, n, t, slope_ref.at[pl.ds(kh, 1)])
            sink = sink_ref[kh]
            mx = jnp.maximum(jnp.max(s, axis=-1, keepdims=True), sink)
            p = jnp.exp(s - mx)
            den = jnp.sum(p, axis=-1, keepdims=True) + jnp.exp(sink - mx)
            outs.append(jnp.dot(p.astype(BF16), v3, preferred_element_type=F32) / den)
            lse_ref[0, kh] = mx + jnp.log(den)
        o_ref[...] = _head_merge(outs, low)

    strip = lambda col: pl.BlockSpec((t, LANES), lambda n, col=col: (0, col))
    rowspec = pl.BlockSpec((KV_HEADS, rows, 1), lambda n: (0, 0, 0))
    return pl.pallas_call(
        body, out_shape=(jax.ShapeDtypeStruct((t, ATTN_WIDTH), F32), jax.ShapeDtypeStruct((nb, KV_HEADS, rows, 1), F32)),
        grid=(nb,), in_specs=[pl.BlockSpec((QBLOCK, ATTN_WIDTH), lambda n: (n, 0)), strip(K_COL), strip(V_COL),
                              rowspec, rowspec],
        out_specs=(pl.BlockSpec((QBLOCK, ATTN_WIDTH), lambda n: (n, 0)),
                   pl.BlockSpec((1, KV_HEADS, rows, 1), lambda n: (n, 0, 0, 0))),
        compiler_params=_cparams(("parallel",)), name=name)(proj, proj, proj, sink_rows, slope_rows)


def _attn_bwd_proj(proj, sink_rows, slope_rows, o, lse, do, name):
    t = proj.shape[0]
    nb = t // QBLOCK
    rows = GQ * QBLOCK
    scale = HEAD_DIM ** -0.5

    def body(q_ref, k_ref, v_ref, sink_ref, slope_ref, o_ref, lse_ref, do_ref, dq_ref, dk_ref, dv_ref, ds_ref):
        n = pl.program_id(0)

        @pl.when(n == 0)
        def _():
            dk_ref[...] = jnp.zeros_like(dk_ref)
            dv_ref[...] = jnp.zeros_like(dv_ref)
            ds_ref[...] = jnp.zeros_like(ds_ref)

        low = lax.broadcasted_iota(jnp.int32, (QBLOCK, LANES), 1) < HEAD_DIM
        k3 = _key_rows(k_ref, n, nb).astype(BF16)
        v3 = _key_rows(v_ref, n, nb).astype(BF16)
        q, dov = q_ref[...], do_ref[...]
        dod = dov * o_ref[...]
        dqs = []
        dk3 = jnp.zeros((3 * QBLOCK, LANES), F32)
        dv3 = jnp.zeros((3 * QBLOCK, LANES), F32)
        for kh in range(KV_HEADS):
            qs = _head_tiles(q, kh, low).astype(BF16)
            dos = _head_tiles(dov, kh, low).astype(BF16)
            delta = jnp.sum(_head_tiles(dod, kh, low), axis=-1, keepdims=True)
            lse_kh = lse_ref[0, kh]
            s = _attn_scores(qs, k3, n, t, slope_ref.at[pl.ds(kh, 1)])
            p = jnp.exp(s - lse_kh)
            dp = lax.dot_general(dos, v3, NT_DIMS, preferred_element_type=F32)
            dsb = (p * (dp - delta)).astype(BF16)
            dqs.append(jnp.dot(dsb, k3, preferred_element_type=F32) * scale)
            dk3 = dk3 + lax.dot_general(dsb, qs, TN_DIMS, preferred_element_type=F32) * scale
            dv3 = dv3 + lax.dot_general(p.astype(BF16), dos, TN_DIMS, preferred_element_type=F32)
            dsink_rows = -jnp.exp(sink_ref[kh] - lse_kh) * delta
            ds_ref[kh] += jnp.sum(dsink_rows.reshape(GQ, QBLOCK, 1), axis=1)
        dq_ref[...] = _head_merge(dqs, low)
        prev, nxt = jnp.maximum(n - 1, 0), jnp.minimum(n + 1, nb - 1)
        for j, b in enumerate((prev, n, nxt)):
            blk = pl.ds(pl.multiple_of(b * QBLOCK, QBLOCK), QBLOCK)
            dk_ref[blk, :] += dk3[j * QBLOCK:(j + 1) * QBLOCK, :]
            dv_ref[blk, :] += dv3[j * QBLOCK:(j + 1) * QBLOCK, :]

    strip = lambda col: pl.BlockSpec((t, LANES), lambda n, col=col: (0, col))
    rowspec = pl.BlockSpec((KV_HEADS, rows, 1), lambda n: (0, 0, 0))
    qspec = pl.BlockSpec((QBLOCK, ATTN_WIDTH), lambda n: (n, 0))
    kv_out = pl.BlockSpec((t, LANES), lambda n: (0, 0))
    return pl.pallas_call(
        body,
        out_shape=(jax.ShapeDtypeStruct((t, ATTN_WIDTH), F32), jax.ShapeDtypeStruct((t, LANES), F32),
                   jax.ShapeDtypeStruct((t, LANES), F32), jax.ShapeDtypeStruct((KV_HEADS, GQ, 1), F32)),
        grid=(nb,),
        in_specs=[qspec, strip(K_COL), strip(V_COL), rowspec, rowspec, qspec,
                  pl.BlockSpec((1, KV_HEADS, rows, 1), lambda n: (n, 0, 0, 0)), qspec],
        out_specs=(qspec, kv_out, kv_out, pl.BlockSpec((KV_HEADS, GQ, 1), lambda n: (0, 0, 0))),
        compiler_params=_cparams(("arbitrary",)), name=name)(proj, proj, proj, sink_rows, slope_rows, o, lse, do)


def _scan_tables(a_re, a_im, reverse):
    pw = [(a_re, a_im)]
    for _ in range(SUBLANES - 1):
        pr, pi = pw[-1]
        pw.append((pr * a_re - pi * a_im, pr * a_im + pi * a_re))
    rows = np.arange(SUBLANES)
    tabs = []
    for d in (1, 2, 4):
        mask = (rows <= SUBLANES - 1 - d) if reverse else (rows >= d)
        m = jnp.asarray(mask, F32)[:, None]
        tabs += [m * pw[d - 1][0][None, :], m * pw[d - 1][1][None, :]]
    order = (SUBLANES - 1 - rows) if reverse else rows
    tabs += [jnp.stack([pw[j][0] for j in order]), jnp.stack([pw[j][1] for j in order])]
    tab = jnp.stack(tabs)
    return tab.reshape(8, SUBLANES, N_STRIPS, STRIP_ST).transpose(2, 0, 1, 3)


def _scan(v, mi_re, mi_im, tab, mo_re, mo_im, reverse, name):
    t = v.shape[0]
    tc = _tile(t, 256)
    nc = t // tc
    nblk = tc // SUBLANES

    def body(v_ref, mir_ref, mii_ref, tab_ref, mor_ref, moi_ref, y_ref, xr_ref, xi_ref, carry_ref):
        @pl.when(pl.program_id(1) == 0)
        def _():
            carry_ref[...] = jnp.zeros_like(carry_ref)

        vb = v_ref[...].astype(BF16)
        xr_ref[...] = jnp.dot(vb, mir_ref[0], preferred_element_type=F32)
        xi_ref[...] = jnp.dot(vb, mii_ref[0], preferred_element_type=F32)

        def blk(i, carry):
            cr, ci = carry
            b = (nblk - 1 - i) if reverse else i
            r0 = pl.multiple_of(b * SUBLANES, SUBLANES)
            xr = xr_ref[pl.ds(r0, SUBLANES), :]
            xi = xi_ref[pl.ds(r0, SUBLANES), :]
            for j, d in enumerate((1, 2, 4)):
                tr_, ti_ = tab_ref[0, 2 * j], tab_ref[0, 2 * j + 1]
                sh = (SUBLANES - d) if reverse else d
                sr = pltpu.roll(xr, sh, 0)
                si = pltpu.roll(xi, sh, 0)
                xr, xi = xr + tr_ * sr - ti_ * si, xi + tr_ * si + ti_ * sr
            pr, pi = tab_ref[0, 6], tab_ref[0, 7]
            xr, xi = xr + pr * cr - pi * ci, xi + pr * ci + pi * cr
            xr_ref[pl.ds(r0, SUBLANES), :] = xr
            xi_ref[pl.ds(r0, SUBLANES), :] = xi
            edge = 0 if reverse else SUBLANES - 1
            return (jnp.broadcast_to(xr[edge:edge + 1, :], xr.shape),
                    jnp.broadcast_to(xi[edge:edge + 1, :], xi.shape))

        cr, ci = lax.fori_loop(0, nblk, blk, (carry_ref[0], carry_ref[1]))
        carry_ref[0] = cr
        carry_ref[1] = ci
        y_ref[...] = (jnp.dot(xr_ref[...].astype(BF16), mor_ref[0], preferred_element_type=F32)
                      + jnp.dot(xi_ref[...].astype(BF16), moi_ref[0], preferred_element_type=F32))

    tmap = (lambda s, c: (nc - 1 - c, s)) if reverse else (lambda s, c: (c, s))
    col0 = v.shape[1] // STRIP_IN - N_STRIPS
    vmap = lambda s, c: (tmap(s, c)[0], s + col0)
    smap3 = lambda s, c: (s, 0, 0)
    return pl.pallas_call(
        body,
        out_shape=(jax.ShapeDtypeStruct((t, SSM_WIDTH), F32),
                   jax.ShapeDtypeStruct((t, N_STRIPS * STRIP_ST), F32),
                   jax.ShapeDtypeStruct((t, N_STRIPS * STRIP_ST), F32)),
        grid=(N_STRIPS, nc),
        in_specs=[pl.BlockSpec((tc, STRIP_IN), vmap),
                  pl.BlockSpec((1, STRIP_IN, STRIP_ST), smap3), pl.BlockSpec((1, STRIP_IN, STRIP_ST), smap3),
                  pl.BlockSpec((1, 8, SUBLANES, STRIP_ST), lambda s, c: (s, 0, 0, 0)),
                  pl.BlockSpec((1, STRIP_ST, STRIP_IN), smap3), pl.BlockSpec((1, STRIP_ST, STRIP_IN), smap3)],
        out_specs=(pl.BlockSpec((tc, STRIP_IN), tmap), pl.BlockSpec((tc, STRIP_ST), tmap),
                   pl.BlockSpec((tc, STRIP_ST), tmap)),
        scratch_shapes=[pltpu.VMEM((2, SUBLANES, STRIP_ST), F32)],
        compiler_params=_cparams(("parallel", "arbitrary")), name=name)(v, mi_re, mi_im, tab, mo_re, mo_im)


def _scan_param_grads(v, dy, xr, xi, lr, li, reverse, name):
    t = v.shape[0]
    tc = _tile(t, 256)
    nc = t // tc
    hb = tc // SUBLANES

    def body(v_ref, dy_ref, xr_ref, xi_ref, lr_ref, li_ref, hr_ref, hi_ref,
             dmir_ref, dmii_ref, dmor_ref, dmoi_ref, da_ref):
        c = pl.program_id(1)

        @pl.when(c == 0)
        def _():
            for r in (dmir_ref, dmii_ref, dmor_ref, dmoi_ref, da_ref):
                r[...] = jnp.zeros_like(r)

        xrv, xiv, lrv, liv = xr_ref[...], xi_ref[...], lr_ref[...], li_ref[...]
        row = lax.broadcasted_iota(jnp.int32, xrv.shape, 0)
        if reverse:
            live = (c < nc - 1).astype(F32)
            edge_r, edge_i = hr_ref[0:1, :] * live, hi_ref[0:1, :] * live
            xpr = jnp.where(row == tc - 1, edge_r, pltpu.roll(xrv, tc - 1, 0))
            xpi = jnp.where(row == tc - 1, edge_i, pltpu.roll(xiv, tc - 1, 0))
        else:
            live = (c > 0).astype(F32)
            edge_r, edge_i = hr_ref[SUBLANES - 1:SUBLANES, :] * live, hi_ref[SUBLANES - 1:SUBLANES, :] * live
            xpr = jnp.where(row == 0, edge_r, pltpu.roll(xrv, 1, 0))
            xpi = jnp.where(row == 0, edge_i, pltpu.roll(xiv, 1, 0))
        da_ref[0, 0:1, :] += jnp.sum(xpr * lrv + xpi * liv, axis=0, keepdims=True)
        da_ref[0, 1:2, :] += jnp.sum(xpr * liv - xpi * lrv, axis=0, keepdims=True)
        tdims = (((0,), (0,)), ((), ()))
        vb, dyb = v_ref[...].astype(BF16), dy_ref[...].astype(BF16)
        dmir_ref[0] += lax.dot_general(vb, lrv.astype(BF16), tdims, preferred_element_type=F32)
        dmii_ref[0] += lax.dot_general(vb, liv.astype(BF16), tdims, preferred_element_type=F32)
        dmor_ref[0] += lax.dot_general(xrv.astype(BF16), dyb, tdims, preferred_element_type=F32)
        dmoi_ref[0] += lax.dot_general(xiv.astype(BF16), dyb, tdims, preferred_element_type=F32)

    tmap = lambda s, c: (c, s)
    if reverse:
        hmap = lambda s, c: (jnp.minimum((c + 1) * hb, t // SUBLANES - 1), s)
    else:
        hmap = lambda s, c: (jnp.maximum(c * hb - 1, 0), s)
    narrow = pl.BlockSpec((tc, STRIP_IN), tmap)
    col0 = v.shape[1] // STRIP_IN - N_STRIPS
    vspec = pl.BlockSpec((tc, STRIP_IN), lambda s, c: (c, s + col0))
    wide = pl.BlockSpec((tc, STRIP_ST), tmap)
    halo = pl.BlockSpec((SUBLANES, STRIP_ST), hmap)
    smap3 = lambda s, c: (s, 0, 0)
    return pl.pallas_call(
        body,
        out_shape=(jax.ShapeDtypeStruct((N_STRIPS, STRIP_IN, STRIP_ST), F32),
                   jax.ShapeDtypeStruct((N_STRIPS, STRIP_IN, STRIP_ST), F32),
                   jax.ShapeDtypeStruct((N_STRIPS, STRIP_ST, STRIP_IN), F32),
                   jax.ShapeDtypeStruct((N_STRIPS, STRIP_ST, STRIP_IN), F32),
                   jax.ShapeDtypeStruct((N_STRIPS, SUBLANES, STRIP_ST), F32)),
        grid=(N_STRIPS, nc),
        in_specs=[vspec, narrow, wide, wide, wide, wide, halo, halo],
        out_specs=(pl.BlockSpec((1, STRIP_IN, STRIP_ST), smap3), pl.BlockSpec((1, STRIP_IN, STRIP_ST), smap3),
                   pl.BlockSpec((1, STRIP_ST, STRIP_IN), smap3), pl.BlockSpec((1, STRIP_ST, STRIP_IN), smap3),
                   pl.BlockSpec((1, SUBLANES, STRIP_ST), smap3)),
        compiler_params=_cparams(("parallel", "arbitrary")), name=name)(v, dy, xr, xi, lr, li, xr, xi)


def _ssm_prep(lam_re, lam_im, log_dt, bt_re, bt_im, c_re, c_im):
    lr = jnp.minimum(lam_re, LAMBDA_RE_MAX)
    li = lam_im
    dt = jnp.exp(log_dt)[:, None]
    mag = jnp.exp(lr * dt)
    a_re = mag * jnp.cos(li * dt)
    a_im = mag * jnp.sin(li * dt)
    den = lr * lr + li * li
    coef_re = ((a_re - 1.0) * lr + a_im * li) / den
    coef_im = (a_im * lr - (a_re - 1.0) * li) / den
    bb_re = coef_re[:, None, :] * bt_re - coef_im[:, None, :] * bt_im
    bb_im = coef_re[:, None, :] * bt_im + coef_im[:, None, :] * bt_re
    eye = jnp.eye(SSM_GROUPS // N_STRIPS, dtype=F32)

    def strips(m):
        g, a, b = m.shape
        m4 = m.reshape(N_STRIPS, g // N_STRIPS, a, b)
        return jnp.einsum('sgab,gk->sgakb', m4, eye).reshape(N_STRIPS, g // N_STRIPS * a, g // N_STRIPS * b)

    mi_re = strips(bb_re)
    mi_im = strips(bb_im)
    mo_re = strips(jnp.swapaxes(c_re, 1, 2))
    mo_im = strips(-jnp.swapaxes(c_im, 1, 2))
    return a_re.reshape(-1), a_im.reshape(-1), mi_re, mi_im, mo_re, mo_im


def _gelu(x):
    c = math.sqrt(2.0 / math.pi)
    return 0.5 * x * (1.0 + jnp.tanh(c * (x + 0.044715 * x * x * x)))


def _gelu_grad(x):
    c = math.sqrt(2.0 / math.pi)
    th = jnp.tanh(c * (x + 0.044715 * x * x * x))
    return 0.5 * (1.0 + th) + 0.5 * x * (1.0 - th * th) * c * (1.0 + 3.0 * 0.044715 * x * x)


def _ssm_post_fwd(u, yf, yb, d, wglu, bglu, name):
    t, w = yf.shape
    tr = _row_tile(t)
    u0 = u.shape[1] - w

    def body(u_ref, yf_ref, yb_ref, d_ref, w_ref, b_ref, s_ref, y0_ref, z_ref):
        y0 = d_ref[...] * u_ref[:, u0:] + yf_ref[...] + yb_ref[...]
        yg = _gelu(y0)
        z = jnp.dot(yg.astype(BF16), w_ref[...], preferred_element_type=F32) + b_ref[...]
        s_ref[...] = yg * _sigmoid(z)
        y0_ref[...] = y0
        z_ref[...] = z

    row = pl.BlockSpec((tr, w), lambda i: (i, 0))
    vec = pl.BlockSpec((1, w), lambda i: (0, 0))
    mat = pl.BlockSpec((w, w), lambda i: (0, 0))
    sh = jax.ShapeDtypeStruct((t, w), F32)
    urow = pl.BlockSpec((tr, u.shape[1]), lambda i: (i, 0))
    return pl.pallas_call(body, out_shape=(sh, sh, sh), grid=(t // tr,),
                          in_specs=[urow, row, row, vec, mat, vec], out_specs=(row, row, row),
                          compiler_params=_cparams(("parallel",)), name=name)(u, yf, yb, d, wglu, bglu)


def _ssm_post_bwd(ds, y0, z, u, d, wglu, name):
    t, w = ds.shape
    tr = _row_tile(t)
    u0 = u.shape[1] - w

    def body(ds_ref, y0_ref, z_ref, u_ref, d_ref, w_ref, dy0_ref, dw_ref, db_ref, dd_ref):
        @pl.when(pl.program_id(0) == 0)
        def _():
            dw_ref[...] = jnp.zeros_like(dw_ref)
            db_ref[...] = jnp.zeros_like(db_ref)
            dd_ref[...] = jnp.zeros_like(dd_ref)

        y0 = y0_ref[...]
        yg = _gelu(y0)
        sg = _sigmoid(z_ref[...])
        dsv = ds_ref[...]
        dz = dsv * yg * sg * (1.0 - sg)
        dzb = dz.astype(BF16)
        dyg = dsv * sg + lax.dot_general(dzb, w_ref[...], (((1,), (1,)), ((), ())), preferred_element_type=F32)
        dy0 = dyg * _gelu_grad(y0)
        dy0_ref[...] = dy0
        dw_ref[...] += lax.dot_general(yg.astype(BF16), dzb, (((0,), (0,)), ((), ())), preferred_element_type=F32)
        db_ref[...] += jnp.sum(dz, axis=0, keepdims=True)
        dd_ref[...] += jnp.sum(dy0 * u_ref[:, u0:], axis=0, keepdims=True)

    row = pl.BlockSpec((tr, w), lambda i: (i, 0))
    urow = pl.BlockSpec((tr, u.shape[1]), lambda i: (i, 0))
    vec = pl.BlockSpec((1, w), lambda i: (0, 0))
    mat = pl.BlockSpec((w, w), lambda i: (0, 0))
    return pl.pallas_call(
        body, out_shape=(jax.ShapeDtypeStruct((t, w), F32), jax.ShapeDtypeStruct((w, w), F32),
                         jax.ShapeDtypeStruct((1, w), F32), jax.ShapeDtypeStruct((1, w), F32)),
        grid=(t // tr,), in_specs=[row, row, row, urow, vec, mat], out_specs=(row, mat, vec, vec),
        compiler_params=_cparams(("arbitrary",)), name=name)(ds, y0, z, u, d, wglu)


def _du_combine(dy0, d, du_f, du_b, name):
    t, w = dy0.shape
    tr = _row_tile(t)

    def body(dy_ref, d_ref, a_ref, b_ref, o_ref):
        o_ref[...] = d_ref[...] * dy_ref[...] + a_ref[...] + b_ref[...]

    row = pl.BlockSpec((tr, w), lambda i: (i, 0))
    vec = pl.BlockSpec((1, w), lambda i: (0, 0))
    return pl.pallas_call(body, out_shape=jax.ShapeDtypeStruct((t, w), F32), grid=(t // tr,),
                          in_specs=[row, vec, row, row], out_specs=row, compiler_params=_cparams(("parallel",)),
                          name=name)(dy0, d, du_f, du_b)


def _ffn_fwd(x, g, wg, wu, wd, tag):
    xo, h, gate, up = _ffn_fwd_call(x, g, wg, wu, wd, f"{tag}_fwd")
    return xo, (h, gate, up)


def _ffn_bwd(dxo, dxo_b, x, g, wg, wu, wd, saved, tag):
    h, gate, up = saved
    dx, dx_b, dg, dgate, dup, act = _ffn_bwd_x_call(dxo, dxo_b, x, g, gate, up, wg, wu, wd, f"{tag}_bwd_x")
    dwg, dwu, dwd = _ffn_bwd_w_call(h, dxo_b, dgate, dup, act, f"{tag}_bwd_w")
    return dx, dx_b, dg, dwg, dwu, dwd


def _local_step(x, tgt, w, get_weights, put_grads, reduce_wide):
    t = x.shape[0]
    row = lambda a: a.reshape(1, -1)
    grads = {}

    w = dict(w)
    w.update(get_weights('ffn1', x))
    x1, ffn1_saved = _ffn_fwd(x, w['norm_ffn1'], w['ffn1_w_gate'], w['ffn1_w_up'], w['ffn1_w_down'], "ffn1")
    w.update(get_weights('mix', x1))

    h2 = _rms_fwd(x1, w['norm_mix'], "mix_norm")
    proj = _mm(h2, w['w_in'], tb=True, name="in_proj")[0]
    u = proj

    sink_rows = jnp.repeat(w['attn_sinks'].reshape(KV_HEADS, GQ), QBLOCK, axis=1)[..., None]
    slopes = jnp.asarray(2.0 ** (-8.0 * (np.arange(ATTN_HEADS) + 1) / ATTN_HEADS), F32)
    slope_rows = jnp.repeat(slopes.reshape(KV_HEADS, GQ), QBLOCK, axis=1)[..., None]
    attn, lse = _attn_fwd_proj(proj, sink_rows, slope_rows, "attn_fwd")

    ssm_names = ['ssm_lambda_re', 'ssm_lambda_im', 'ssm_log_dt', 'ssm_b_re', 'ssm_b_im', 'ssm_c_re', 'ssm_c_im']
    ys, states, preps, vjps = [], [], [], []
    for direction in range(2):
        params = [w[n][direction] for n in ssm_names]
        prep, vjp = jax.vjp(_ssm_prep, *params)
        a_re, a_im = prep[0], prep[1]
        mi_re, mi_im, mo_re, mo_im = (m.astype(BF16) for m in prep[2:])
        prep = (a_re, a_im, mi_re, mi_im, mo_re, mo_im)
        rev = direction == 1
        tab = _scan_tables(a_re, a_im, rev)
        y, xr, xi = _scan(u, mi_re, mi_im, tab, mo_re, mo_im, rev, f"s5_fwd{direction}")
        ys.append(y)
        states.append((xr, xi))
        preps.append(prep)
        vjps.append(vjp)
    d_row = row(w['ssm_d'])
    s, y0, z = _ssm_post_fwd(u, ys[0], ys[1], d_row, w['ssm_glu_w'], row(w['ssm_glu_b']), "ssm_post")

    ma = _rms_fwd(attn, row(w['attn_out_norm']), "attn_out_norm")
    ms = _rms_fwd(s, row(w['ssm_out_norm']), "ssm_out_norm")
    mixed = jnp.concatenate([ma, ms], axis=-1)
    x2 = _mm(mixed, w['w_out'], res=x1, reduce_s=True, name="out_proj")

    w.update(get_weights('ffn2', x2))
    x3, ffn2_saved = _ffn_fwd(x2, w['norm_ffn2'], w['ffn2_w_gate'], w['ffn2_w_up'], w['ffn2_w_down'], "ffn2")

    loss_row, dx3, dx3_b, dgf = _loss_head(x3, row(w['final_norm']), tgt, "loss_head")
    loss = loss_row[0, 0]
    grads['final_norm'] = dgf.reshape(w['final_norm'].shape)

    dx2, dx2_b, dg, dwg, dwu, dwd = _ffn_bwd(dx3, dx3_b, x2, w['norm_ffn2'], w['ffn2_w_gate'], w['ffn2_w_up'],
                                             w['ffn2_w_down'], ffn2_saved, "ffn2")
    grads['norm_ffn2'] = dg
    sent = put_grads('ffn2', dict(ffn2_w_gate=dwg, ffn2_w_up=dwu, ffn2_w_down=dwd))

    dmixed = _mm(dx2_b, w['w_out'], tb=True, reduce_s=True, after=sent, name="out_proj_dx")
    dw_out = _mm(mixed, dx2_b, ta=True, out_dtype=BF16, name="out_proj_dw")[0]
    dattn, _, dga = _rms_bwd(attn, row(w['attn_out_norm']), dmixed[:, :ATTN_WIDTH], None, "attn_out_dnorm")
    ds, _, dgs = _rms_bwd(s, row(w['ssm_out_norm']), dmixed[:, ATTN_WIDTH:], None, "ssm_out_dnorm")
    grads.update(attn_out_norm=dga, ssm_out_norm=dgs)

    dy0, dwglu, dbglu, dd = _ssm_post_bwd(ds, y0, z, u, d_row, w['ssm_glu_w'], "ssm_post_bwd")
    grads['ssm_glu_b'] = dbglu
    grads['ssm_d'] = dd.reshape(w['ssm_d'].shape)
    dparams, du_dirs = [], []
    for direction in range(2):
        a_re, a_im, mi_re, mi_im, mo_re, mo_im = preps[direction]
        rev = direction == 1
        tab = _scan_tables(a_re, -a_im, not rev)
        tr3 = lambda m: jnp.swapaxes(m, 1, 2)
        du_dir, lr, li = _scan(dy0, tr3(mo_re), tr3(mo_im), tab, tr3(mi_re), tr3(mi_im), not rev,
                               f"s5_adj{direction}")
        du_dirs.append(du_dir)
        xr, xi = states[direction]
        dmir, dmii, dmor, dmoi, da = _scan_param_grads(u, dy0, xr, xi, lr, li, rev, f"s5_pgrad{direction}")
        da_re = da[:, 0, :].reshape(-1)
        da_im = da[:, 1, :].reshape(-1)
        dparams.append(vjps[direction]((da_re, da_im, dmir, dmii, dmor, dmoi)))
    du = _du_combine(dy0, d_row, du_dirs[0], du_dirs[1], "ssm_du")
    for i, n in enumerate(ssm_names):
        grads[n] = jnp.stack([dparams[0][i], dparams[1][i]])
    wide_sum = reduce_wide(grads)

    dq, dk, dv, dsink = _attn_bwd_proj(proj, sink_rows, slope_rows, attn, lse, dattn, "attn_bwd")
    grads['attn_sinks'] = dsink.reshape(w['attn_sinks'].shape)
    dproj = jnp.concatenate([dq, dk, dv, du], axis=-1).astype(BF16)

    dw_in = _mm(dproj, h2, ta=True, out_dtype=BF16, after=wide_sum, name="in_proj_dw")[0]
    sent = put_grads('mix', dict(w_in=dw_in, ssm_glu_w=dwglu, w_out=dw_out))
    dh2 = _mm(dproj, w['w_in'], reduce_s=True, after=sent, name="in_proj_dx")
    dx1, dx1_b, dgm = _rms_bwd(x1, w['norm_mix'], dh2, dx2, "mix_dnorm")
    grads['norm_mix'] = dgm

    dx0, _, dg, dwg, dwu, dwd = _ffn_bwd(dx1, dx1_b, x, w['norm_ffn1'], w['ffn1_w_gate'], w['ffn1_w_up'],
                                         w['ffn1_w_down'], ffn1_saved, "ffn1")
    grads['norm_ffn1'] = dg
    put_grads('ffn1', dict(ffn1_w_gate=dwg, ffn1_w_up=dwu, ffn1_w_down=dwd))
    return loss, dx0, grads, wide_sum


HBM_SPEC = pl.BlockSpec(memory_space=pl.ANY)


def _chip_peers(x, y):
    return [(1 - x, y), (x, 1 - y), (1 - x, 1 - y)]


HBM_ONLY = pl.BlockSpec(memory_space=pltpu.HBM)
SEM_SPEC = pl.BlockSpec(memory_space=pltpu.SEMAPHORE)
EFFECT = pltpu.SideEffectType.DATAFLOW_SIDE_EFFECTING


def _place_own(src, slot, name):
    r, c = src.shape
    tr = r // 2

    def body(slot_ref, s_ref, o_ref):
        o_ref[0] = s_ref[...]

    return pl.pallas_call(
        body, out_shape=jax.ShapeDtypeStruct((N_CHIPS, r, c), src.dtype),
        grid_spec=pltpu.PrefetchScalarGridSpec(
            num_scalar_prefetch=1, grid=(2,), in_specs=[pl.BlockSpec((tr, c), lambda i, s: (i, 0))],
            out_specs=pl.BlockSpec((1, tr, c), lambda i, s: (s[0], i, 0))),
        compiler_params=_cparams(("parallel",)), name=name)(slot, src)


def _chip_copies(srcs, lands, send_sems, recv_sems, scatter, landed):
    x, y, c = lax.axis_index("x"), lax.axis_index("y"), lax.axis_index("c")
    me = 2 * x + y
    out = []
    for i in range(len(srcs)):
        for j, (px, py) in enumerate(_chip_peers(x, y)):
            p = 2 * px + py
            slot = p if landed else me
            if scatter:
                src, dst = srcs[i].at[p], lands[i].at[slot]
            else:
                rows = _core_half(srcs[i].shape[0], c)
                src, dst = srcs[i].at[rows], lands[i].at[slot, rows]
            out.append(pltpu.make_async_remote_copy(src, dst, send_sems.at[3 * i + j], recv_sems.at[3 * i + j],
                                                    device_id=(px, py, c), device_id_type=MESH))
    return out


def _core_half(nrows, c):
    half = nrows // 2
    return pl.ds(pl.multiple_of(c * half, 16), half)


def _sibling_forward(lands, name):
    n = len(lands)

    def body(*refs):
        bufs = refs[n:2 * n]
        send_sems, recv_sems = refs[2 * n:]
        x, y, c = lax.axis_index("x"), lax.axis_index("y"), lax.axis_index("c")
        mine = [_core_half(b.shape[1], c) for b in bufs]
        theirs = [_core_half(b.shape[1], 1 - c) for b in bufs]
        chips = [2 * px + py for px, py in _chip_peers(x, y)]
        cps = [pltpu.make_async_remote_copy(bufs[i].at[p, mine[i]], bufs[i].at[p, mine[i]], send_sems.at[3 * i + j],
                                            recv_sems.at[3 * i + j], device_id=(x, y, 1 - c), device_id_type=MESH)
               for i in range(n) for j, p in enumerate(chips)]
        for cp in cps:
            cp.start()
        for i in range(n):
            for j, p in enumerate(chips):
                pltpu.make_async_remote_copy(bufs[i].at[p, mine[i]], bufs[i].at[p, theirs[i]], send_sems.at[3 * i + j],
                                             recv_sems.at[3 * i + j], device_id=(x, y, 1 - c),
                                             device_id_type=MESH).wait()

    return pl.pallas_call(
        body, out_shape=[jax.ShapeDtypeStruct(a.shape, a.dtype) for a in lands],
        in_specs=[HBM_SPEC] * n, out_specs=[HBM_SPEC] * n, input_output_aliases={k: k for k in range(n)},
        scratch_shapes=[pltpu.SemaphoreType.DMA((3 * n,)), pltpu.SemaphoreType.DMA((3 * n,))],
        name=name)(*lands)


def _exchange_start(groups, scatter, name):
    sizes = [len(srcs) for srcs, _ in groups]
    flat_src = [a for srcs, _ in groups for a in srcs]
    flat_land = [a for _, lands in groups for a in lands]
    n = len(flat_src)
    ng = len(groups)

    def body(*refs):
        src_refs, land_refs = refs[:n], refs[n:2 * n]
        sems = refs[2 * n:2 * n + 2 * ng]
        token_ref = refs[-1]
        off = 0
        for gi, sz in enumerate(sizes):
            for cp in _chip_copies(src_refs[off:off + sz], land_refs[off:off + sz], sems[2 * gi], sems[2 * gi + 1],
                                   scatter, landed=False):
                cp.start()
            off += sz
        token_ref[...] = jnp.zeros_like(token_ref)

    sem_shapes = []
    for sz in sizes:
        sem_shapes += [pltpu.SemaphoreType.DMA((3 * sz,)), pltpu.SemaphoreType.DMA((3 * sz,))]
    hbm = lambda a: pltpu.HBM(a.shape, a.dtype)
    res = pl.pallas_call(
        body, name=name,
        out_shape=(tuple(sem_shapes) + tuple(hbm(a) for a in flat_src) + tuple(hbm(a) for a in flat_land)
                   + (jax.ShapeDtypeStruct((SUBLANES, LANES), F32),)),
        in_specs=[HBM_ONLY] * (2 * n),
        out_specs=tuple([SEM_SPEC] * (2 * ng) + [HBM_ONLY] * (2 * n) + [pl.BlockSpec(memory_space=pltpu.VMEM)]),
        input_output_aliases={k: 2 * ng + k for k in range(2 * n)},
        compiler_params=pltpu.CompilerParams(has_side_effects=EFFECT),
    )(*[pltpu.with_memory_space_constraint(a, pltpu.HBM) for a in flat_src + flat_land])
    sems, thru_src, thru_land = res[:2 * ng], res[2 * ng:2 * ng + n], res[2 * ng + n:2 * ng + 2 * n]
    out, off = [], 0
    for gi, sz in enumerate(sizes):
        out.append((sems[2 * gi], sems[2 * gi + 1], list(thru_src[off:off + sz]), list(thru_land[off:off + sz])))
        off += sz
    return out, res[-1]


def _exchange_wait(started, after, scatter, name):
    send_sems, recv_sems, srcs, lands = started
    n = len(srcs)

    def body(*refs):
        src_refs, land_refs = refs[:n], refs[n:2 * n]
        send_ref, recv_ref = refs[2 * n], refs[2 * n + 1]
        for cp in _chip_copies(src_refs, land_refs, send_ref, recv_ref, scatter, landed=True):
            cp.wait_send()
            cp.wait_recv()

    hbm = lambda a: pltpu.HBM(a.shape, a.dtype)
    res = pl.pallas_call(
        body, name=name, out_shape=tuple(hbm(a) for a in srcs) + tuple(hbm(a) for a in lands),
        in_specs=[HBM_ONLY] * (2 * n) + [SEM_SPEC, SEM_SPEC, HBM_SPEC], out_specs=tuple([HBM_ONLY] * (2 * n)),
        input_output_aliases={k: k for k in range(2 * n)},
        compiler_params=pltpu.CompilerParams(has_side_effects=EFFECT),
    )(*srcs, *lands, send_sems, recv_sems, after)
    return list(res[:n]), list(res[n:])


def _small_exchange(smalls, name):
    nsm = len(smalls)
    rels = [(fx, fy, fc) for fx in (0, 1) for fy in (0, 1) for fc in (0, 1)][1:]

    def body(*refs):
        sins, souts = refs[:nsm], refs[nsm:2 * nsm]
        ssend, srecv, slocal = refs[2 * nsm:]
        x, y, c = lax.axis_index("x"), lax.axis_index("y"), lax.axis_index("c")
        lin = 4 * x + 2 * y + c
        local = [pltpu.make_async_copy(sins[i], souts[i].at[lin], slocal.at[i]) for i in range(nsm)]
        for cp in local:
            cp.start()
        for i in range(nsm):
            for j, (fx, fy, fc) in enumerate(rels):
                pltpu.make_async_remote_copy(sins[i], souts[i].at[lin], ssend.at[i, j], srecv.at[i, j],
                                             device_id=(x ^ fx, y ^ fy, c ^ fc), device_id_type=MESH).start()
        for i in range(nsm):
            for j, (fx, fy, fc) in enumerate(rels):
                src = 4 * (x ^ fx) + 2 * (y ^ fy) + (c ^ fc)
                pltpu.make_async_remote_copy(sins[i], souts[i].at[src], ssend.at[i, j], srecv.at[i, j],
                                             device_id=(x ^ fx, y ^ fy, c ^ fc), device_id_type=MESH).wait()
        for cp in local:
            cp.wait()

    return pl.pallas_call(
        body, out_shape=[jax.ShapeDtypeStruct((N_DEV,) + s.shape, s.dtype) for s in smalls],
        in_specs=[HBM_SPEC] * nsm, out_specs=[HBM_SPEC] * nsm,
        scratch_shapes=[pltpu.SemaphoreType.DMA((nsm, 7)), pltpu.SemaphoreType.DMA((nsm, 7)),
                        pltpu.SemaphoreType.DMA((nsm,))],
        name=name)(*smalls)


def _sibling_swap(arrs, name):
    nw = len(arrs)

    def body(*refs):
        ins, outs = refs[:nw], refs[nw:2 * nw]
        send_sems, recv_sems = refs[2 * nw:]
        x, y, c = lax.axis_index("x"), lax.axis_index("y"), lax.axis_index("c")
        cps = [pltpu.make_async_remote_copy(ins[i], outs[i], send_sems.at[i], recv_sems.at[i],
                                            device_id=(x, y, 1 - c), device_id_type=MESH) for i in range(nw)]
        for cp in cps:
            cp.start()
        for cp in cps:
            cp.wait()

    return pl.pallas_call(
        body, out_shape=[jax.ShapeDtypeStruct(a.shape, a.dtype) for a in arrs],
        in_specs=[HBM_SPEC] * nw, out_specs=[HBM_SPEC] * nw,
        scratch_shapes=[pltpu.SemaphoreType.DMA((nw,)), pltpu.SemaphoreType.DMA((nw,))],
        name=name)(*arrs)


def _sum_parts(parts, recv, slots, name):
    _, r, c = parts.shape
    tr = _row_tile(r)

    def body(slot_ref, own_ref, r0_ref, r1_ref, r2_ref, o_ref):
        o_ref[...] = ((own_ref[0].astype(F32) + r0_ref[0].astype(F32))
                      + (r1_ref[0].astype(F32) + r2_ref[0].astype(F32)))

    blk = lambda k: pl.BlockSpec((1, tr, c), lambda i, s, k=k: (s[k], i, 0))
    return pl.pallas_call(
        body, out_shape=jax.ShapeDtypeStruct((r, c), F32),
        grid_spec=pltpu.PrefetchScalarGridSpec(
            num_scalar_prefetch=1, grid=(r // tr,), in_specs=[blk(0), blk(1), blk(2), blk(3)],
            out_specs=pl.BlockSpec((tr, c), lambda i, s: (i, 0))),
        compiler_params=_cparams(("parallel",)), name=name)(slots, parts, recv, recv, recv)


def _small_allreduce(packed, name):
    rows = packed.shape[0]
    pr = rows // N_DEV
    rels = [(fx, fy, fc) for fx in (0, 1) for fy in (0, 1) for fc in (0, 1)][1:]

    def body(in_ref, out_ref, recv_ref, send1, recv1, send2, recv2):
        x, y, c = lax.axis_index("x"), lax.axis_index("y"), lax.axis_index("c")
        lin = 4 * x + 2 * y + c
        piece = lambda ref, k: ref.at[pl.ds(pl.multiple_of(k * pr, pr), pr), :]
        peers = [((x ^ fx, y ^ fy, c ^ fc), 4 * (x ^ fx) + 2 * (y ^ fy) + (c ^ fc)) for fx, fy, fc in rels]
        for j, (dev, plin) in enumerate(peers):
            pltpu.make_async_remote_copy(piece(in_ref, plin), recv_ref.at[lin], send1.at[j], recv1.at[j],
                                         device_id=dev, device_id_type=MESH).start()
        recv_ref[lin] = piece(in_ref, lin)[...]
        for j, (dev, plin) in enumerate(peers):
            pltpu.make_async_remote_copy(piece(in_ref, plin), recv_ref.at[plin], send1.at[j], recv1.at[j],
                                         device_id=dev, device_id_type=MESH).wait()
        acc = recv_ref[0]
        for k in range(1, N_DEV):
            acc = acc + recv_ref[k]
        piece(out_ref, lin)[...] = acc
        for j, (dev, plin) in enumerate(peers):
            pltpu.make_async_remote_copy(piece(out_ref, lin), piece(out_ref, lin), send2.at[j], recv2.at[j],
                                         device_id=dev, device_id_type=MESH).start()
        for j, (dev, plin) in enumerate(peers):
            pltpu.make_async_remote_copy(piece(out_ref, lin), piece(out_ref, plin), send2.at[j], recv2.at[j],
                                         device_id=dev, device_id_type=MESH).wait()

    vm = pl.BlockSpec(memory_space=pltpu.VMEM)
    return pl.pallas_call(
        body, out_shape=jax.ShapeDtypeStruct(packed.shape, F32), in_specs=[vm], out_specs=vm,
        scratch_shapes=[pltpu.VMEM((N_DEV, pr, LANES), F32)] + [pltpu.SemaphoreType.DMA((7,))] * 4,
        compiler_params=pltpu.CompilerParams(vmem_limit_bytes=VMEM_LIMIT), name=name)(packed)


def _adamw_math(w, m, v, g):
    nm = ADAM_B1 * m + (1.0 - ADAM_B1) * g
    nv = ADAM_B2 * v + (1.0 - ADAM_B2) * (g * g)
    m_hat = nm * (1.0 / (1.0 - ADAM_B1 ** ADAM_STEP))
    v_hat = nv * (1.0 / (1.0 - ADAM_B2 ** ADAM_STEP))
    return -ADAM_LR * (m_hat / (jnp.sqrt(v_hat) + ADAM_EPS) + ADAM_WD * w), nm, nv


def _adamw(w, m, v, g_mine, g_other, name):
    r, c = w.shape
    tr = _row_tile(r)

    def body(w_ref, m_ref, v_ref, g1_ref, g2_ref, g_ref, d_ref, nm_ref, nv_ref):
        g = g1_ref[...] + g2_ref[...]
        g_ref[...] = g
        d_ref[...], nm_ref[...], nv_ref[...] = _adamw_math(w_ref[...], m_ref[...], v_ref[...], g)

    blk = pl.BlockSpec((tr, c), lambda i: (i, 0))
    sh = jax.ShapeDtypeStruct((r, c), F32)
    return pl.pallas_call(body, out_shape=(sh, sh, sh, sh), grid=(r // tr,), in_specs=[blk] * 5,
                          out_specs=(blk, blk, blk, blk), compiler_params=_cparams(("parallel",)),
                          name=name)(w, m, v, g_mine, g_other)


def _adamw_small(ws, ms, vs, alls, split, name):
    n = len(ws)
    lead = split if split is not None else ()
    nl = len(lead)
    nslots = alls[0].shape[0]

    def blocks(shape):
        if split is None:
            return tuple(shape), (lambda *g: (0,) * len(shape))
        blk = (shape[0], shape[1] // lead[0], shape[2] // lead[1]) + tuple(shape[3:])
        return blk, (lambda *g: (0, g[0], g[1]) + (0,) * (len(shape) - 3))

    def body(*refs):
        w_refs, m_refs, v_refs, a_refs = (refs[k * n:(k + 1) * n] for k in range(4))
        g_refs, d_refs, nm_refs, nv_refs = (refs[(4 + k) * n:(5 + k) * n] for k in range(4))
        k = pl.program_id(nl)
        for i in range(n):
            @pl.when(k == 0)
            def _(i=i):
                g_refs[i][...] = a_refs[i][0]

            @pl.when(k > 0)
            def _(i=i):
                g_refs[i][...] += a_refs[i][0]

            @pl.when(k == nslots - 1)
            def _(i=i):
                d_refs[i][...], nm_refs[i][...], nv_refs[i][...] = _adamw_math(
                    w_refs[i][...], m_refs[i][...], v_refs[i][...], g_refs[i][...])

    specs, aspecs, shapes = [], [], []
    for wa in ws:
        blk, imap = blocks(wa.shape)
        specs.append(pl.BlockSpec(blk, imap))
        aspecs.append(pl.BlockSpec((1,) + blk, (lambda *g, imap=imap: (g[nl],) + imap(*g))))
        shapes.append(jax.ShapeDtypeStruct(wa.shape, F32))
    res = pl.pallas_call(
        body, out_shape=shapes * 4, grid=tuple(lead) + (nslots,), in_specs=specs * 3 + aspecs,
        out_specs=specs * 4, compiler_params=_cparams(("parallel",) * nl + ("arbitrary",)),
        name=name)(*ws, *ms, *vs, *alls)
    return res[:n], res[n:2 * n], res[2 * n:3 * n], res[3 * n:]


def kernel(x, norm_ffn1, ffn1_w_gate, ffn1_w_up, ffn1_w_down, norm_mix, w_in, attn_sinks, ssm_lambda_re, ssm_lambda_im, ssm_log_dt, ssm_b_re, ssm_b_im, ssm_c_re, ssm_c_im, ssm_d, ssm_glu_w, ssm_glu_b, attn_out_norm, ssm_out_norm, w_out, norm_ffn2, ffn2_w_gate, ffn2_w_up, ffn2_w_down, final_norm, loss_target, m_norm_ffn1, m_ffn1_w_gate, m_ffn1_w_up, m_ffn1_w_down, m_norm_mix, m_w_in, m_attn_sinks, m_ssm_lambda_re, m_ssm_lambda_im, m_ssm_log_dt, m_ssm_b_re, m_ssm_b_im, m_ssm_c_re, m_ssm_c_im, m_ssm_d, m_ssm_glu_w, m_ssm_glu_b, m_attn_out_norm, m_ssm_out_norm, m_w_out, m_norm_ffn2, m_ffn2_w_gate, m_ffn2_w_up, m_ffn2_w_down, m_final_norm, v_norm_ffn1, v_ffn1_w_gate, v_ffn1_w_up, v_ffn1_w_down, v_norm_mix, v_w_in, v_attn_sinks, v_ssm_lambda_re, v_ssm_lambda_im, v_ssm_log_dt, v_ssm_b_re, v_ssm_b_im, v_ssm_c_re, v_ssm_c_im, v_ssm_d, v_ssm_glu_w, v_ssm_glu_b, v_attn_out_norm, v_ssm_out_norm, v_w_out, v_norm_ffn2, v_ffn2_w_gate, v_ffn2_w_up, v_ffn2_w_down, v_final_norm):
    given = dict(locals())
    wts = {n: given[n] for n in WEIGHTS}

    order = [g for g in GROUPS]
    cx, cy = lax.axis_index("x"), lax.axis_index("y")
    slots = jnp.stack([2 * cx + cy, 2 * (1 - cx) + cy, 2 * cx + 1 - cy, 2 * (1 - cx) + 1 - cy]).astype(jnp.int32)
    def view(a, n):
        if n in TRANSPOSED:
            return jnp.swapaxes(a[0], 0, 1)
        if n in BIG:
            return a[0]
        if n in ('ssm_b_re', 'ssm_b_im'):
            return jnp.swapaxes(a, -1, -2)
        return a.reshape(1, -1) if a.ndim == 1 else a

    def unview(a, n):
        if n in TRANSPOSED:
            return jnp.swapaxes(a, 0, 1)[None]
        if n in ('ssm_b_re', 'ssm_b_im'):
            return jnp.swapaxes(a, -1, -2)
        return a.reshape(wts[n].shape)

    shards = {n: view(wts[n], n).astype(BF16) for n in BIG}
    placed = {n: _place_own(shards[n], slots, f"weights_place_{n}") for n in BIG}
    started, _ = _exchange_start([([shards[n] for n in GROUPS[g]], [placed[n] for n in GROUPS[g]]) for g in order],
                                 False, "weights_start")
    started = dict(zip(order, started))

    def get_weights(group, after):
        _, lands = _exchange_wait(started[group], after, False, f"weights_wait_{group}")
        lands = _sibling_forward(lands, f"weights_forward_{group}")
        out = dict(zip(GROUPS[group], lands))
        for n in ('w_in', 'ssm_glu_w', 'w_out'):
            if n in out:
                out[n] = out[n].reshape(-1, out[n].shape[-1])
        return out

    sent, tokens = {}, {}

    def put_grads(group, gd):
        parts = []
        for n in GROUPS[group]:
            g = gd[n]
            if g.ndim == 2:
                g = g.reshape(N_CHIPS, g.shape[0] // N_CHIPS, g.shape[1])
            parts.append(g.astype(BF16))
        lands = [lax.empty(p.shape, p.dtype) for p in parts]
        started_g, tokens[group] = _exchange_start([(parts, lands)], True, f"grads_start_{group}")
        sent[group] = started_g[0]
        return tokens[group]

    w = {n: (wts[n][0] if wts[n].ndim > 1 else wts[n]) for n in SMALL}
    w['norm_ffn1'], w['norm_mix'], w['norm_ffn2'] = wts['norm_ffn1'], wts['norm_mix'], wts['norm_ffn2']
    w['ssm_b_re'], w['ssm_b_im'] = view(wts['ssm_b_re'], 'ssm_b_re')[0], view(wts['ssm_b_im'], 'ssm_b_im')[0]
    wide = ['ssm_b_re', 'ssm_b_im', 'ssm_c_re', 'ssm_c_im']

    def reduce_wide(gd):
        packed = jnp.concatenate([gd[n].reshape(-1, LANES) for n in wide])
        return _small_allreduce(packed, "small_grads_allreduce")

    loss, dx, grads, wide_sum = _local_step(x[0], loss_target[0], w, get_weights, put_grads, reduce_wide)
    loss = lax.psum(loss, ("x", "y", "c"))

    out_g, out_d, out_m, out_v = {}, {}, {}, {}

    def finish(group, after):
        names = GROUPS[group]
        parts, recv = _exchange_wait(sent[group], after, True, f"grads_wait_{group}")
        chip_sums = [_sum_parts(p, r, slots, f"grad_sum_{n}") for n, p, r in zip(names, parts, recv)]
        other = _sibling_swap(chip_sums, f"grad_sibling_swap_{group}")
        for n, mine, oth in zip(names, chip_sums, other):
            g, d, nm, nv = _adamw(view(wts[n], n), view(given['m_' + n], n), view(given['v_' + n], n), mine, oth,
                                  f"adamw_{n}")
            out_g[n], out_d[n], out_m[n], out_v[n] = (unview(a, n) for a in (g, d, nm, nv))
        return nv

    done = finish('ffn2', tokens['ffn1'])
    done = finish('mix', done)

    nat = {n: view(wts[n], n).shape for n in SMALL}
    narrow = [n for n in SMALL if n not in wide]
    alls = _small_exchange([grads[n].reshape(nat[n]) for n in narrow], "small_grads_allgather")
    rows = wide_sum.shape[0] // len(wide)
    wide_g = [wide_sum[i * rows:(i + 1) * rows].reshape((1,) + nat[n]) for i, n in enumerate(wide)]
    for group, gs, split, tag in ((narrow, alls, None, "adamw_small"), (wide, wide_g, (2, 4), "adamw_ssm_bc")):
        res = _adamw_small([view(wts[n], n) for n in group], [view(given['m_' + n], n) for n in group],
                           [view(given['v_' + n], n) for n in group], gs, split, tag)
        for dst, vals in zip((out_g, out_d, out_m, out_v), res):
            for n, a in zip(group, vals):
                dst[n] = unview(a, n)

    finish('ffn1', out_v['ssm_b_re'])

    return (loss, dx[None], *[out_g[n] for n in WEIGHTS], *[out_d[n] for n in WEIGHTS],
            *[out_m[n] for n in WEIGHTS], *[out_v[n] for n in WEIGHTS])
```

```python
import functools
import math

import numpy as np
import jax
import jax.numpy as jnp
from jax import lax
from jax.experimental import pallas as pl
from jax.experimental.pallas import tpu as pltpu

F32 = jnp.float32
BF16 = jnp.bfloat16
MESH = pl.DeviceIdType.MESH

EPS = 1e-6
NEG_INF = -1e30
LAMBDA_RE_MAX = -1e-4
ATTN_HEADS = 8
KV_HEADS = 2
GQ = ATTN_HEADS // KV_HEADS
HEAD_DIM = 64
ATTN_WIDTH = 512
KV_WIDTH = 128
WINDOW = 128
QBLOCK = 128
SSM_WIDTH = 512
SSM_GROUPS = 32
SSM_CH = 16
SSM_STATE = 64
N_STRIPS = 4
STRIP_IN = SSM_WIDTH // N_STRIPS
STRIP_ST = SSM_GROUPS * SSM_STATE // N_STRIPS
SUBLANES = 8
LANES = 128
N_CHIPS = 4
N_DEV = 8

ADAM_LR = 0.001
ADAM_B1 = 0.9
ADAM_B2 = 0.999
ADAM_EPS = 1e-08
ADAM_WD = 0.01
ADAM_STEP = 10

VMEM_LIMIT = 48 * 1024 * 1024

WEIGHTS = ['norm_ffn1', 'ffn1_w_gate', 'ffn1_w_up', 'ffn1_w_down', 'norm_mix', 'w_in', 'attn_sinks',
           'ssm_lambda_re', 'ssm_lambda_im', 'ssm_log_dt', 'ssm_b_re', 'ssm_b_im', 'ssm_c_re', 'ssm_c_im',
           'ssm_d', 'ssm_glu_w', 'ssm_glu_b', 'attn_out_norm', 'ssm_out_norm', 'w_out', 'norm_ffn2',
           'ffn2_w_gate', 'ffn2_w_up', 'ffn2_w_down', 'final_norm']
BIG = ['ffn1_w_gate', 'ffn1_w_up', 'ffn1_w_down', 'w_in', 'ssm_glu_w', 'w_out',
       'ffn2_w_gate', 'ffn2_w_up', 'ffn2_w_down']
SMALL = [n for n in WEIGHTS if n not in BIG]
TRANSPOSED = ['ffn1_w_gate', 'ffn1_w_up', 'w_in', 'ffn2_w_gate', 'ffn2_w_up']
GROUPS = {'ffn1': ['ffn1_w_gate', 'ffn1_w_up', 'ffn1_w_down'],
          'mix': ['w_in', 'ssm_glu_w', 'w_out'],
          'ffn2': ['ffn2_w_gate', 'ffn2_w_up', 'ffn2_w_down']}


def _cparams(sem=None):
    return pltpu.CompilerParams(dimension_semantics=sem, vmem_limit_bytes=VMEM_LIMIT)


def _tile(n, pref):
    if n <= pref:
        return n
    for t in (pref, pref // 2, pref // 4):
        if t % LANES == 0 and n % t == 0:
            return t
    return n


def _sigmoid(x):
    return 1.0 / (1.0 + jnp.exp(-x))


def _mm(a, b, *, ta=False, tb=False, reduce_s=False, res=None, scale=1.0, out_dtype=F32, after=None, name):
    a3 = a if a.ndim == 3 else a[None]
    b3 = b if b.ndim == 3 else b[None]
    sa, sb = a3.shape[0], b3.shape[0]
    ns = max(sa, sb)
    (kk, m) = a3.shape[1:] if ta else a3.shape[1:][::-1]
    (n, kb) = b3.shape[1:] if tb else b3.shape[1:][::-1]
    assert kk == kb, (a3.shape, b3.shape)
    tm, tn, tk = _tile(m, 1024), _tile(n, 1024), _tile(kk, 2048)
    nm, nn, nk = m // tm, n // tn, kk // tk
    has_res = res is not None
    single = nk == 1 and not (reduce_s and ns > 1)

    if reduce_s:
        grid = (nm, nn, ns, nk)
        ids = lambda i, j, s, k: (s, i, j, k)
        sem = ("parallel", "parallel", "arbitrary", "arbitrary")
    else:
        grid = (ns, nm, nn, nk)
        ids = lambda s, i, j, k: (s, i, j, k)
        sem = ("parallel", "parallel", "parallel", "arbitrary")

    def a_map(*g):
        s, i, j, k = ids(*g)
        s = s if sa > 1 else 0
        return (s, k, i) if ta else (s, i, k)

    def b_map(*g):
        s, i, j, k = ids(*g)
        s = s if sb > 1 else 0
        return (s, j, k) if tb else (s, k, j)

    def o_map(*g):
        s, i, j, k = ids(*g)
        return (i, j) if reduce_s else (s, i, j)

    a_blk = (1, tk, tm) if ta else (1, tm, tk)
    b_blk = (1, tn, tk) if tb else (1, tk, tn)
    dims = (((0 if ta else 1,), (1 if tb else 0,)), ((), ()))

    def body(*refs):
        a_ref, b_ref = refs[0], refs[1]
        r_ref = refs[2] if has_res else None
        o_ref = refs[2 + has_res + (after is not None)]
        acc_ref = None if single else refs[-1]
        s, _, _, k = ids(*[pl.program_id(d) for d in range(4)])
        prod = lax.dot_general(a_ref[0].astype(BF16), b_ref[0].astype(BF16), dims, preferred_element_type=F32)

        def finish(out):
            if scale != 1.0:
                out = out * scale
            if has_res:
                out = r_ref[...].reshape(out.shape) + out
            o_ref[...] = out.astype(out_dtype).reshape(o_ref.shape)

        if single:
            finish(prod)
            return
        if reduce_s:
            first = jnp.logical_and(s == 0, k == 0)
            last = jnp.logical_and(s == ns - 1, k == nk - 1)
        else:
            first, last = k == 0, k == nk - 1

        @pl.when(first)
        def _():
            acc_ref[...] = prod

        @pl.when(jnp.logical_not(first))
        def _():
            acc_ref[...] += prod

        @pl.when(last)
        def _():
            finish(acc_ref[...])

    in_specs = [pl.BlockSpec(a_blk, a_map), pl.BlockSpec(b_blk, b_map)]
    args = [a3, b3]
    if reduce_s:
        out_shape = jax.ShapeDtypeStruct((m, n), out_dtype)
        o_spec = pl.BlockSpec((tm, tn), o_map)
    else:
        out_shape = jax.ShapeDtypeStruct((ns, m, n), out_dtype)
        o_spec = pl.BlockSpec((1, tm, tn), o_map)
    if has_res:
        assert res.shape == out_shape.shape
        in_specs.append(o_spec)
        args.append(res)
    if after is not None:
        in_specs.append(HBM_SPEC)
        args.append(after)
    return pl.pallas_call(body, out_shape=out_shape, grid=grid, in_specs=in_specs, out_specs=o_spec,
                          scratch_shapes=[] if single else [pltpu.VMEM((tm, tn), F32)],
                          compiler_params=_cparams(sem), name=name)(*args)


def _row_tile(t):
    for tr in (256, 128, 64, 32, 16, 8):
        if t % tr == 0:
            return tr
    return t


def _rms_fwd(x, g, name):
    t, w = x.shape
    tr = _row_tile(t)

    def body(x_ref, g_ref, o_ref):
        xv = x_ref[...]
        r = lax.rsqrt(jnp.mean(xv * xv, axis=-1, keepdims=True) + EPS)
        o_ref[...] = (xv * r * g_ref[...]).astype(BF16)

    return pl.pallas_call(
        body, out_shape=jax.ShapeDtypeStruct((t, w), BF16), grid=(t // tr,),
        in_specs=[pl.BlockSpec((tr, w), lambda i: (i, 0)), pl.BlockSpec((1, w), lambda i: (0, 0))],
        out_specs=pl.BlockSpec((tr, w), lambda i: (i, 0)), compiler_params=_cparams(("parallel",)),
        name=name)(x, g)


def _rms_bwd_rows(xv, gv, dhv):
    r = lax.rsqrt(jnp.mean(xv * xv, axis=-1, keepdims=True) + EPS)
    nrm = xv * r
    dn = dhv * gv
    return r * (dn - nrm * jnp.mean(dn * nrm, axis=-1, keepdims=True)), dhv * nrm


def _rms_bwd(x, g, dh, dres, name):
    t, w = x.shape
    tr = _row_tile(t)
    has_res = dres is not None

    def body(*refs):
        if has_res:
            x_ref, g_ref, dh_ref, dr_ref, dx_ref, dxb_ref, dg_ref = refs
        else:
            x_ref, g_ref, dh_ref, dx_ref, dxb_ref, dg_ref = refs
        dx, dgs = _rms_bwd_rows(x_ref[...], g_ref[...], dh_ref[...])
        if has_res:
            dx = dx + dr_ref[...]
        dx_ref[...] = dx
        dxb_ref[...] = dx.astype(BF16)

        @pl.when(pl.program_id(0) == 0)
        def _():
            dg_ref[...] = jnp.zeros_like(dg_ref)

        dg_ref[...] += jnp.sum(dgs, axis=0, keepdims=True)

    row = pl.BlockSpec((tr, w), lambda i: (i, 0))
    vec = pl.BlockSpec((1, w), lambda i: (0, 0))
    ins = [x, g, dh] + ([dres] if has_res else [])
    return pl.pallas_call(
        body, out_shape=(jax.ShapeDtypeStruct((t, w), F32), jax.ShapeDtypeStruct((t, w), BF16),
                         jax.ShapeDtypeStruct((1, w), F32)),
        grid=(t // tr,), in_specs=[row, vec, row] + ([row] if has_res else []),
        out_specs=(row, row, vec), compiler_params=_cparams(("arbitrary",)), name=name)(*ins)


FFN_ROWS = 512


NT_DIMS = (((1,), (1,)), ((), ()))
TN_DIMS = (((0,), (0,)), ((), ()))


def _ffn_fwd_call(x, g, wg, wu, wd, name):
    t, d = x.shape
    ns, f, _ = wg.shape
    tm = _tile(t, FFN_ROWS)

    def body(x_ref, g_ref, wg_ref, wu_ref, wd_ref, xo_ref, h_ref, gate_ref, up_ref, h_sc, acc_ref):
        s = pl.program_id(1)

        @pl.when(s == 0)
        def _():
            xv = x_ref[...]
            r = lax.rsqrt(jnp.mean(xv * xv, axis=-1, keepdims=True) + EPS)
            hb = (xv * r * g_ref[...]).astype(BF16)
            h_sc[...] = hb
            h_ref[...] = hb

        hb = h_sc[...]
        gate = lax.dot_general(hb, wg_ref[0], NT_DIMS, preferred_element_type=F32)
        up = lax.dot_general(hb, wu_ref[0], NT_DIMS, preferred_element_type=F32)
        gate_ref[0] = gate.astype(BF16)
        up_ref[0] = up.astype(BF16)
        act = (gate * _sigmoid(gate) * up).astype(BF16)
        prod = jnp.dot(act, wd_ref[0], preferred_element_type=F32)

        @pl.when(s == 0)
        def _():
            acc_ref[...] = prod

        @pl.when(s > 0)
        def _():
            acc_ref[...] += prod

        @pl.when(s == ns - 1)
        def _():
            xo_ref[...] = x_ref[...] + 0.5 * acc_ref[...]

    row = pl.BlockSpec((tm, d), lambda i, s: (i, 0))
    vec = pl.BlockSpec((1, d), lambda i, s: (0, 0))
    wrow = pl.BlockSpec((1, f, d), lambda i, s: (s, 0, 0))
    hid = pl.BlockSpec((1, tm, f), lambda i, s: (s, i, 0))
    hid_sh = jax.ShapeDtypeStruct((ns, t, f), BF16)
    return pl.pallas_call(
        body, out_shape=(jax.ShapeDtypeStruct((t, d), F32), jax.ShapeDtypeStruct((t, d), BF16), hid_sh, hid_sh),
        grid=(t // tm, ns), in_specs=[row, vec, wrow, wrow, wrow], out_specs=(row, row, hid, hid),
        scratch_shapes=[pltpu.VMEM((tm, d), BF16), pltpu.VMEM((tm, d), F32)],
        compiler_params=_cparams(("parallel", "arbitrary")), name=name)(x, g, wg, wu, wd)


def _ffn_bwd_x_call(dxo, dxo_b, x, g, gate, up, wg, wu, wd, name):
    t, d = x.shape
    ns, f, _ = wg.shape
    tm = _tile(t, FFN_ROWS)

    def body(dxo_ref, dxb_ref, x_ref, g_ref, gate_ref, up_ref, wg_ref, wu_ref, wd_ref,
             dx_ref, dxob_ref, dgn_ref, dgate_ref, dup_ref, act_ref, dh_ref):
        i, s = pl.program_id(0), pl.program_id(1)
        dact = lax.dot_general(dxb_ref[...], wd_ref[0], NT_DIMS, preferred_element_type=F32) * 0.5
        gv = gate_ref[0].astype(F32)
        uv = up_ref[0].astype(F32)
        sg = _sigmoid(gv)
        silu = gv * sg
        act_ref[0] = (silu * uv).astype(BF16)
        dub = (dact * silu).astype(BF16)
        dgb = (dact * uv * sg * (1.0 + gv * (1.0 - sg))).astype(BF16)
        dup_ref[0] = dub
        dgate_ref[0] = dgb
        prod = (jnp.dot(dgb, wg_ref[0], preferred_element_type=F32)
                + jnp.dot(dub, wu_ref[0], preferred_element_type=F32))

        @pl.when(s == 0)
        def _():
            dh_ref[...] = prod

        @pl.when(s > 0)
        def _():
            dh_ref[...] += prod

        @pl.when(jnp.logical_and(i == 0, s == 0))
        def _():
            dgn_ref[...] = jnp.zeros_like(dgn_ref)

        @pl.when(s == ns - 1)
        def _():
            dx, dgs = _rms_bwd_rows(x_ref[...], g_ref[...], dh_ref[...])
            dx = dx + dxo_ref[...]
            dx_ref[...] = dx
            dxob_ref[...] = dx.astype(BF16)
            dgn_ref[...] += jnp.sum(dgs, axis=0, keepdims=True)

    row = pl.BlockSpec((tm, d), lambda i, s: (i, 0))
    vec = pl.BlockSpec((1, d), lambda i, s: (0, 0))
    wrow = pl.BlockSpec((1, f, d), lambda i, s: (s, 0, 0))
    hid = pl.BlockSpec((1, tm, f), lambda i, s: (s, i, 0))
    hid_sh = jax.ShapeDtypeStruct((ns, t, f), BF16)
    return pl.pallas_call(
        body,
        out_shape=(jax.ShapeDtypeStruct((t, d), F32), jax.ShapeDtypeStruct((t, d), BF16),
                   jax.ShapeDtypeStruct((1, d), F32), hid_sh, hid_sh, hid_sh),
        grid=(t // tm, ns), in_specs=[row, row, row, vec, hid, hid, wrow, wrow, wrow],
        out_specs=(row, row, vec, hid, hid, hid), scratch_shapes=[pltpu.VMEM((tm, d), F32)],
        compiler_params=_cparams(("arbitrary", "arbitrary")), name=name)(dxo, dxo_b, x, g, gate, up, wg, wu, wd)


def _ffn_bwd_w_call(h, dxo_b, dgate, dup, act, name):
    t, d = h.shape
    ns, _, f = dgate.shape
    tm = _tile(t, FFN_ROWS)
    nm = t // tm

    def body(h_ref, dxb_ref, dgate_ref, dup_ref, act_ref, dwg_ref, dwu_ref, dwd_ref, ag_ref, au_ref, ad_ref):
        i = pl.program_id(1)
        hv = h_ref[...]
        pg = lax.dot_general(dgate_ref[0], hv, TN_DIMS, preferred_element_type=F32)
        pu = lax.dot_general(dup_ref[0], hv, TN_DIMS, preferred_element_type=F32)
        pd = lax.dot_general(act_ref[0], dxb_ref[...], TN_DIMS, preferred_element_type=F32)

        @pl.when(i == 0)
        def _():
            ag_ref[...] = pg
            au_ref[...] = pu
            ad_ref[...] = pd

        @pl.when(i > 0)
        def _():
            ag_ref[...] += pg
            au_ref[...] += pu
            ad_ref[...] += pd

        @pl.when(i == nm - 1)
        def _():
            dwg_ref[0] = ag_ref[...].astype(BF16)
            dwu_ref[0] = au_ref[...].astype(BF16)
            dwd_ref[0] = (0.5 * ad_ref[...]).astype(BF16)

    row = pl.BlockSpec((tm, d), lambda s, i: (i, 0))
    hid = pl.BlockSpec((1, tm, f), lambda s, i: (s, i, 0))
    wrow = pl.BlockSpec((1, f, d), lambda s, i: (s, 0, 0))
    wsh = jax.ShapeDtypeStruct((ns, f, d), BF16)
    return pl.pallas_call(
        body, out_shape=(wsh, wsh, wsh),
        grid=(ns, nm), in_specs=[row, row, hid, hid, hid], out_specs=(wrow, wrow, wrow),
        scratch_shapes=[pltpu.VMEM((f, d), F32), pltpu.VMEM((f, d), F32), pltpu.VMEM((f, d), F32)],
        compiler_params=_cparams(("parallel", "arbitrary")), name=name)(h, dxo_b, dgate, dup, act)


def _loss_head(x, g, tgt, name):
    t, w = x.shape
    tr = _row_tile(t)

    def body(x_ref, g_ref, t_ref, loss_ref, dx_ref, dxb_ref, dg_ref):
        xv = x_ref[...]
        gv = g_ref[...]
        r = lax.rsqrt(jnp.mean(xv * xv, axis=-1, keepdims=True) + EPS)
        nrm = xv * r
        err = nrm * gv - t_ref[...]
        dout = err * (1.0 / w)
        dn = dout * gv
        dx = r * (dn - nrm * jnp.mean(dn * nrm, axis=-1, keepdims=True))
        dx_ref[...] = dx
        dxb_ref[...] = dx.astype(BF16)

        @pl.when(pl.program_id(0) == 0)
        def _():
            dg_ref[...] = jnp.zeros_like(dg_ref)
            loss_ref[...] = jnp.zeros_like(loss_ref)

        dg_ref[...] += jnp.sum(dout * nrm, axis=0, keepdims=True)
        part = jnp.sum(jnp.sum(err * err, axis=-1, keepdims=True) * (0.5 / w), axis=0, keepdims=True)
        loss_ref[...] += jnp.broadcast_to(part, loss_ref.shape)

    row = pl.BlockSpec((tr, w), lambda i: (i, 0))
    vec = pl.BlockSpec((1, w), lambda i: (0, 0))
    return pl.pallas_call(
        body, out_shape=(jax.ShapeDtypeStruct((1, LANES), F32), jax.ShapeDtypeStruct((t, w), F32),
                         jax.ShapeDtypeStruct((t, w), BF16), jax.ShapeDtypeStruct((1, w), F32)),
        grid=(t // tr,), in_specs=[row, vec, row],
        out_specs=(pl.BlockSpec((1, LANES), lambda i: (0, 0)), row, row, vec),
        compiler_params=_cparams(("arbitrary",)), name=name)(x, g, tgt)


def _attn_scores(q, k3, n, t, slope_ref):
    rows = GQ * QBLOCK
    s = lax.dot_general(q, k3, (((1,), (1,)), ((), ())), preferred_element_type=F32) * (HEAD_DIM ** -0.5)
    row = lax.broadcasted_iota(jnp.int32, (rows, 3 * QBLOCK), 0) & (QBLOCK - 1)
    col = lax.broadcasted_iota(jnp.int32, (rows, 3 * QBLOCK), 1)
    rel = jnp.abs(col - QBLOCK - row)
    key_pos = n * QBLOCK - QBLOCK + col
    valid = (rel <= WINDOW) & (key_pos >= 0) & (key_pos < t)
    return jnp.where(valid, s - slope_ref[0] * rel.astype(F32), NEG_INF)


Q_COL, K_COL, V_COL, U_COL = 0, ATTN_WIDTH // LANES, ATTN_WIDTH // LANES + 1, ATTN_WIDTH // LANES + 2


def _key_rows(ref, n, nb):
    prev, nxt = jnp.maximum(n - 1, 0), jnp.minimum(n + 1, nb - 1)
    blk = lambda b: ref[pl.ds(pl.multiple_of(b * QBLOCK, QBLOCK), QBLOCK), :]
    return jnp.concatenate([blk(prev), blk(n), blk(nxt)], axis=0)


def _head_tiles(x, kh, low):
    tiles = []
    for g in range(GQ):
        h = GQ * kh + g
        t128 = x[:, LANES * (h // 2):LANES * (h // 2 + 1)]
        t128 = jnp.where(low if h % 2 == 0 else jnp.logical_not(low), t128, 0.0)
        if h % 2 != kh:
            t128 = pltpu.roll(t128, HEAD_DIM, 1)
        tiles.append(t128)
    return jnp.concatenate(tiles, axis=0)


def _head_merge(per_kh, low):
    out = []
    for j in range(ATTN_HEADS // 2):
        pair = []
        for h in (2 * j, 2 * j + 1):
            kh, g = h // GQ, h % GQ
            t128 = per_kh[kh][g * QBLOCK:(g + 1) * QBLOCK, :]
            if h % 2 != kh:
                t128 = pltpu.roll(t128, HEAD_DIM, 1)
            pair.append(t128)
        out.append(jnp.where(low, pair[0], pair[1]))
    return jnp.concatenate(out, axis=1)


def _attn_fwd_proj(proj, sink_rows, slope_rows, name):
    t = proj.shape[0]
    nb = t // QBLOCK
    rows = GQ * QBLOCK

    def body(q_ref, k_ref, v_ref, sink_ref, slope_ref, o_ref, lse_ref):
        n = pl.program_id(0)
        low = lax.broadcasted_iota(jnp.int32, (QBLOCK, LANES), 1) < HEAD_DIM
        k3 = _key_rows(k_ref, n, nb).astype(BF16)
        v3 = _key_rows(v_ref, n, nb).astype(BF16)
        q = q_ref[...]
        outs = []
        for kh in range(KV_HEADS):
            qs = _head_tiles(q, kh, low).astype(BF16)
            s = _attn_scores(qs, k3, n, t, slope_ref.at[pl.ds(kh, 1)])
            sink = sink_ref[kh]
            mx = jnp.maximum(jnp.max(s, axis=-1, keepdims=True), sink)
            p = jnp.exp(s - mx)
            den = jnp.sum(p, axis=-1, keepdims=True) + jnp.exp(sink - mx)
            outs.append(jnp.dot(p.astype(BF16), v3, preferred_element_type=F32) / den)
            lse_ref[0, kh] = mx + jnp.log(den)
        o_ref[...] = _head_merge(outs, low)

    strip = lambda col: pl.BlockSpec((t, LANES), lambda n, col=col: (0, col))
    rowspec = pl.BlockSpec((KV_HEADS, rows, 1), lambda n: (0, 0, 0))
    return pl.pallas_call(
        body, out_shape=(jax.ShapeDtypeStruct((t, ATTN_WIDTH), F32), jax.ShapeDtypeStruct((nb, KV_HEADS, rows, 1), F32)),
        grid=(nb,), in_specs=[pl.BlockSpec((QBLOCK, ATTN_WIDTH), lambda n: (n, 0)), strip(K_COL), strip(V_COL),
                              rowspec, rowspec],
        out_specs=(pl.BlockSpec((QBLOCK, ATTN_WIDTH), lambda n: (n, 0)),
                   pl.BlockSpec((1, KV_HEADS, rows, 1), lambda n: (n, 0, 0, 0))),
        compiler_params=_cparams(("parallel",)), name=name)(proj, proj, proj, sink_rows, slope_rows)


def _attn_bwd_proj(proj, sink_rows, slope_rows, o, lse, do, name):
    t = proj.shape[0]
    nb = t // QBLOCK
    rows = GQ * QBLOCK
    scale = HEAD_DIM ** -0.5

    def body(q_ref, k_ref, v_ref, sink_ref, slope_ref, o_ref, lse_ref, do_ref, dq_ref, dk_ref, dv_ref, ds_ref):
        n = pl.program_id(0)

        @pl.when(n == 0)
        def _():
            dk_ref[...] = jnp.zeros_like(dk_ref)
            dv_ref[...] = jnp.zeros_like(dv_ref)
            ds_ref[...] = jnp.zeros_like(ds_ref)

        low = lax.broadcasted_iota(jnp.int32, (QBLOCK, LANES), 1) < HEAD_DIM
        k3 = _key_rows(k_ref, n, nb).astype(BF16)
        v3 = _key_rows(v_ref, n, nb).astype(BF16)
        q, dov = q_ref[...], do_ref[...]
        dod = dov * o_ref[...]
        dqs = []
        dk3 = jnp.zeros((3 * QBLOCK, LANES), F32)
        dv3 = jnp.zeros((3 * QBLOCK, LANES), F32)
        for kh in range(KV_HEADS):
            qs = _head_tiles(q, kh, low).astype(BF16)
            dos = _head_tiles(dov, kh, low).astype(BF16)
            delta = jnp.sum(_head_tiles(dod, kh, low), axis=-1, keepdims=True)
            lse_kh = lse_ref[0, kh]
            s = _attn_scores(qs, k3, n, t, slope_ref.at[pl.ds(kh, 1)])
            p = jnp.exp(s - lse_kh)
            dp = lax.dot_general(dos, v3, NT_DIMS, preferred_element_type=F32)
            dsb = (p * (dp - delta)).astype(BF16)
            dqs.append(jnp.dot(dsb, k3, preferred_element_type=F32) * scale)
            dk3 = dk3 + lax.dot_general(dsb, qs, TN_DIMS, preferred_element_type=F32) * scale
            dv3 = dv3 + lax.dot_general(p.astype(BF16), dos, TN_DIMS, preferred_element_type=F32)
            dsink_rows = -jnp.exp(sink_ref[kh] - lse_kh) * delta
            ds_ref[kh] += jnp.sum(dsink_rows.reshape(GQ, QBLOCK, 1), axis=1)
        dq_ref[...] = _head_merge(dqs, low)
        prev, nxt = jnp.maximum(n - 1, 0), jnp.minimum(n + 1, nb - 1)
        for j, b in enumerate((prev, n, nxt)):
            blk = pl.ds(pl.multiple_of(b * QBLOCK, QBLOCK), QBLOCK)
            dk_ref[blk, :] += dk3[j * QBLOCK:(j + 1) * QBLOCK, :]
            dv_ref[blk, :] += dv3[j * QBLOCK:(j + 1) * QBLOCK, :]

    strip = lambda col: pl.BlockSpec((t, LANES), lambda n, col=col: (0, col))
    rowspec = pl.BlockSpec((KV_HEADS, rows, 1), lambda n: (0, 0, 0))
    qspec = pl.BlockSpec((QBLOCK, ATTN_WIDTH), lambda n: (n, 0))
    kv_out = pl.BlockSpec((t, LANES), lambda n: (0, 0))
    return pl.pallas_call(
        body,
        out_shape=(jax.ShapeDtypeStruct((t, ATTN_WIDTH), F32), jax.ShapeDtypeStruct((t, LANES), F32),
                   jax.ShapeDtypeStruct((t, LANES), F32), jax.ShapeDtypeStruct((KV_HEADS, GQ, 1), F32)),
        grid=(nb,),
        in_specs=[qspec, strip(K_COL), strip(V_COL), rowspec, rowspec, qspec,
                  pl.BlockSpec((1, KV_HEADS, rows, 1), lambda n: (n, 0, 0, 0)), qspec],
        out_specs=(qspec, kv_out, kv_out, pl.BlockSpec((KV_HEADS, GQ, 1), lambda n: (0, 0, 0))),
        compiler_params=_cparams(("arbitrary",)), name=name)(proj, proj, proj, sink_rows, slope_rows, o, lse, do)


def _scan_tables(a_re, a_im, reverse):
    pw = [(a_re, a_im)]
    for _ in range(SUBLANES - 1):
        pr, pi = pw[-1]
        pw.append((pr * a_re - pi * a_im, pr * a_im + pi * a_re))
    rows = np.arange(SUBLANES)
    tabs = []
    for d in (1, 2, 4):
        mask = (rows <= SUBLANES - 1 - d) if reverse else (rows >= d)
        m = jnp.asarray(mask, F32)[:, None]
        tabs += [m * pw[d - 1][0][None, :], m * pw[d - 1][1][None, :]]
    order = (SUBLANES - 1 - rows) if reverse else rows
    tabs += [jnp.stack([pw[j][0] for j in order]), jnp.stack([pw[j][1] for j in order])]
    tab = jnp.stack(tabs)
    return tab.reshape(8, SUBLANES, N_STRIPS, STRIP_ST).transpose(2, 0, 1, 3)


def _scan(v, mi_re, mi_im, tab, mo_re, mo_im, reverse, name):
    t = v.shape[0]
    tc = _tile(t, 256)
    nc = t // tc
    nblk = tc // SUBLANES

    def body(v_ref, mir_ref, mii_ref, tab_ref, mor_ref, moi_ref, y_ref, xr_ref, xi_ref, carry_ref):
        @pl.when(pl.program_id(1) == 0)
        def _():
            carry_ref[...] = jnp.zeros_like(carry_ref)

        vb = v_ref[...].astype(BF16)
        xr_ref[...] = jnp.dot(vb, mir_ref[0], preferred_element_type=F32)
        xi_ref[...] = jnp.dot(vb, mii_ref[0], preferred_element_type=F32)

        def blk(i, carry):
            cr, ci = carry
            b = (nblk - 1 - i) if reverse else i
            r0 = pl.multiple_of(b * SUBLANES, SUBLANES)
            xr = xr_ref[pl.ds(r0, SUBLANES), :]
            xi = xi_ref[pl.ds(r0, SUBLANES), :]
            for j, d in enumerate((1, 2, 4)):
                tr_, ti_ = tab_ref[0, 2 * j], tab_ref[0, 2 * j + 1]
                sh = (SUBLANES - d) if reverse else d
                sr = pltpu.roll(xr, sh, 0)
                si = pltpu.roll(xi, sh, 0)
                xr, xi = xr + tr_ * sr - ti_ * si, xi + tr_ * si + ti_ * sr
            pr, pi = tab_ref[0, 6], tab_ref[0, 7]
            xr, xi = xr + pr * cr - pi * ci, xi + pr * ci + pi * cr
            xr_ref[pl.ds(r0, SUBLANES), :] = xr
            xi_ref[pl.ds(r0, SUBLANES), :] = xi
            edge = 0 if reverse else SUBLANES - 1
            return (jnp.broadcast_to(xr[edge:edge + 1, :], xr.shape),
                    jnp.broadcast_to(xi[edge:edge + 1, :], xi.shape))

        cr, ci = lax.fori_loop(0, nblk, blk, (carry_ref[0], carry_ref[1]))
        carry_ref[0] = cr
        carry_ref[1] = ci
        y_ref[...] = (jnp.dot(xr_ref[...].astype(BF16), mor_ref[0], preferred_element_type=F32)
                      + jnp.dot(xi_ref[...].astype(BF16), moi_ref[0], preferred_element_type=F32))

    tmap = (lambda s, c: (nc - 1 - c, s)) if reverse else (lambda s, c: (c, s))
    col0 = v.shape[1] // STRIP_IN - N_STRIPS
    vmap = lambda s, c: (tmap(s, c)[0], s + col0)
    smap3 = lambda s, c: (s, 0, 0)
    return pl.pallas_call(
        body,
        out_shape=(jax.ShapeDtypeStruct((t, SSM_WIDTH), F32),
                   jax.ShapeDtypeStruct((t, N_STRIPS * STRIP_ST), F32),
                   jax.ShapeDtypeStruct((t, N_STRIPS * STRIP_ST), F32)),
        grid=(N_STRIPS, nc),
        in_specs=[pl.BlockSpec((tc, STRIP_IN), vmap),
                  pl.BlockSpec((1, STRIP_IN, STRIP_ST), smap3), pl.BlockSpec((1, STRIP_IN, STRIP_ST), smap3),
                  pl.BlockSpec((1, 8, SUBLANES, STRIP_ST), lambda s, c: (s, 0, 0, 0)),
                  pl.BlockSpec((1, STRIP_ST, STRIP_IN), smap3), pl.BlockSpec((1, STRIP_ST, STRIP_IN), smap3)],
        out_specs=(pl.BlockSpec((tc, STRIP_IN), tmap), pl.BlockSpec((tc, STRIP_ST), tmap),
                   pl.BlockSpec((tc, STRIP_ST), tmap)),
        scratch_shapes=[pltpu.VMEM((2, SUBLANES, STRIP_ST), F32)],
        compiler_params=_cparams(("parallel", "arbitrary")), name=name)(v, mi_re, mi_im, tab, mo_re, mo_im)


def _scan_param_grads(v, dy, xr, xi, lr, li, reverse, name):
    t = v.shape[0]
    tc = _tile(t, 256)
    nc = t // tc
    hb = tc // SUBLANES

    def body(v_ref, dy_ref, xr_ref, xi_ref, lr_ref, li_ref, hr_ref, hi_ref,
             dmir_ref, dmii_ref, dmor_ref, dmoi_ref, da_ref):
        c = pl.program_id(1)

        @pl.when(c == 0)
        def _():
            for r in (dmir_ref, dmii_ref, dmor_ref, dmoi_ref, da_ref):
                r[...] = jnp.zeros_like(r)

        xrv, xiv, lrv, liv = xr_ref[...], xi_ref[...], lr_ref[...], li_ref[...]
        row = lax.broadcasted_iota(jnp.int32, xrv.shape, 0)
        if reverse:
            live = (c < nc - 1).astype(F32)
            edge_r, edge_i = hr_ref[0:1, :] * live, hi_ref[0:1, :] * live
            xpr = jnp.where(row == tc - 1, edge_r, pltpu.roll(xrv, tc - 1, 0))
            xpi = jnp.where(row == tc - 1, edge_i, pltpu.roll(xiv, tc - 1, 0))
        else:
            live = (c > 0).astype(F32)
            edge_r, edge_i = hr_ref[SUBLANES - 1:SUBLANES, :] * live, hi_ref[SUBLANES - 1:SUBLANES, :] * live
            xpr = jnp.where(row == 0, edge_r, pltpu.roll(xrv, 1, 0))
            xpi = jnp.where(row == 0, edge_i, pltpu.roll(xiv, 1, 0))
        da_ref[0, 0:1, :] += jnp.sum(xpr * lrv + xpi * liv, axis=0, keepdims=True)
        da_ref[0, 1:2, :] += jnp.sum(xpr * liv - xpi * lrv, axis=0, keepdims=True)
        tdims = (((0,), (0,)), ((), ()))
        vb, dyb = v_ref[...].astype(BF16), dy_ref[...].astype(BF16)
        dmir_ref[0] += lax.dot_general(vb, lrv.astype(BF16), tdims, preferred_element_type=F32)
        dmii_ref[0] += lax.dot_general(vb, liv.astype(BF16), tdims, preferred_element_type=F32)
        dmor_ref[0] += lax.dot_general(xrv.astype(BF16), dyb, tdims, preferred_element_type=F32)
        dmoi_ref[0] += lax.dot_general(xiv.astype(BF16), dyb, tdims, preferred_element_type=F32)

    tmap = lambda s, c: (c, s)
    if reverse:
        hmap = lambda s, c: (jnp.minimum((c + 1) * hb, t // SUBLANES - 1), s)
    else:
        hmap = lambda s, c: (jnp.maximum(c * hb - 1, 0), s)
    narrow = pl.BlockSpec((tc, STRIP_IN), tmap)
    col0 = v.shape[1] // STRIP_IN - N_STRIPS
    vspec = pl.BlockSpec((tc, STRIP_IN), lambda s, c: (c, s + col0))
    wide = pl.BlockSpec((tc, STRIP_ST), tmap)
    halo = pl.BlockSpec((SUBLANES, STRIP_ST), hmap)
    smap3 = lambda s, c: (s, 0, 0)
    return pl.pallas_call(
        body,
        out_shape=(jax.ShapeDtypeStruct((N_STRIPS, STRIP_IN, STRIP_ST), F32),
                   jax.ShapeDtypeStruct((N_STRIPS, STRIP_IN, STRIP_ST), F32),
                   jax.ShapeDtypeStruct((N_STRIPS, STRIP_ST, STRIP_IN), F32),
                   jax.ShapeDtypeStruct((N_STRIPS, STRIP_ST, STRIP_IN), F32),
                   jax.ShapeDtypeStruct((N_STRIPS, SUBLANES, STRIP_ST), F32)),
        grid=(N_STRIPS, nc),
        in_specs=[vspec, narrow, wide, wide, wide, wide, halo, halo],
        out_specs=(pl.BlockSpec((1, STRIP_IN, STRIP_ST), smap3), pl.BlockSpec((1, STRIP_IN, STRIP_ST), smap3),
                   pl.BlockSpec((1, STRIP_ST, STRIP_IN), smap3), pl.BlockSpec((1, STRIP_ST, STRIP_IN), smap3),
                   pl.BlockSpec((1, SUBLANES, STRIP_ST), smap3)),
        compiler_params=_cparams(("parallel", "arbitrary")), name=name)(v, dy, xr, xi, lr, li, xr, xi)


def _ssm_prep(lam_re, lam_im, log_dt, bt_re, bt_im, c_re, c_im):
    lr = jnp.minimum(lam_re, LAMBDA_RE_MAX)
    li = lam_im
    dt = jnp.exp(log_dt)[:, None]
    mag = jnp.exp(lr * dt)
    a_re = mag * jnp.cos(li * dt)
    a_im = mag * jnp.sin(li * dt)
    den = lr * lr + li * li
    coef_re = ((a_re - 1.0) * lr + a_im * li) / den
    coef_im = (a_im * lr - (a_re - 1.0) * li) / den
    bb_re = coef_re[:, None, :] * bt_re - coef_im[:, None, :] * bt_im
    bb_im = coef_re[:, None, :] * bt_im + coef_im[:, None, :] * bt_re
    eye = jnp.eye(SSM_GROUPS // N_STRIPS, dtype=F32)

    def strips(m):
        g, a, b = m.shape
        m4 = m.reshape(N_STRIPS, g // N_STRIPS, a, b)
        return jnp.einsum('sgab,gk->sgakb', m4, eye).reshape(N_STRIPS, g // N_STRIPS * a, g // N_STRIPS * b)

    mi_re = strips(bb_re)
    mi_im = strips(bb_im)
    mo_re = strips(jnp.swapaxes(c_re, 1, 2))
    mo_im = strips(-jnp.swapaxes(c_im, 1, 2))
    return a_re.reshape(-1), a_im.reshape(-1), mi_re, mi_im, mo_re, mo_im


def _gelu(x):
    c = math.sqrt(2.0 / math.pi)
    return 0.5 * x * (1.0 + jnp.tanh(c * (x + 0.044715 * x * x * x)))


def _gelu_grad(x):
    c = math.sqrt(2.0 / math.pi)
    th = jnp.tanh(c * (x + 0.044715 * x * x * x))
    return 0.5 * (1.0 + th) + 0.5 * x * (1.0 - th * th) * c * (1.0 + 3.0 * 0.044715 * x * x)


def _last_cols_specs(u, w, tr):
    half = w // 2
    first = (u.shape[1] - w) // half
    assert first * half == u.shape[1] - w
    return [pl.BlockSpec((tr, half), lambda i, k=k: (i, first + k)) for k in range(2)]


def _ssm_post_fwd(u, yf, yb, d, wglu, bglu, name):
    t, w = yf.shape
    tr = _row_tile(t)

    def body(ua_ref, ub_ref, yf_ref, yb_ref, d_ref, w_ref, b_ref, s_ref, y0_ref, z_ref):
        uv = jnp.concatenate([ua_ref[...], ub_ref[...]], axis=1)
        y0 = d_ref[...] * uv + yf_ref[...] + yb_ref[...]
        yg = _gelu(y0)
        z = jnp.dot(yg.astype(BF16), w_ref[...], preferred_element_type=F32) + b_ref[...]
        s_ref[...] = yg * _sigmoid(z)
        y0_ref[...] = y0
        z_ref[...] = z

    row = pl.BlockSpec((tr, w), lambda i: (i, 0))
    vec = pl.BlockSpec((1, w), lambda i: (0, 0))
    mat = pl.BlockSpec((w, w), lambda i: (0, 0))
    sh = jax.ShapeDtypeStruct((t, w), F32)
    return pl.pallas_call(body, out_shape=(sh, sh, sh), grid=(t // tr,),
                          in_specs=[*_last_cols_specs(u, w, tr), row, row, vec, mat, vec], out_specs=(row, row, row),
                          compiler_params=_cparams(("parallel",)), name=name)(u, u, yf, yb, d, wglu, bglu)


def _ssm_post_bwd(ds, y0, z, u, d, wglu, name):
    t, w = ds.shape
    tr = _row_tile(t)

    def body(ds_ref, y0_ref, z_ref, ua_ref, ub_ref, d_ref, w_ref, dy0_ref, dw_ref, db_ref, dd_ref):
        @pl.when(pl.program_id(0) == 0)
        def _():
            dw_ref[...] = jnp.zeros_like(dw_ref)
            db_ref[...] = jnp.zeros_like(db_ref)
            dd_ref[...] = jnp.zeros_like(dd_ref)

        y0 = y0_ref[...]
        yg = _gelu(y0)
        sg = _sigmoid(z_ref[...])
        dsv = ds_ref[...]
        dz = dsv * yg * sg * (1.0 - sg)
        dzb = dz.astype(BF16)
        dyg = dsv * sg + lax.dot_general(dzb, w_ref[...], (((1,), (1,)), ((), ())), preferred_element_type=F32)
        dy0 = dyg * _gelu_grad(y0)
        dy0_ref[...] = dy0
        dw_ref[...] += lax.dot_general(yg.astype(BF16), dzb, (((0,), (0,)), ((), ())), preferred_element_type=F32)
        db_ref[...] += jnp.sum(dz, axis=0, keepdims=True)
        uv = jnp.concatenate([ua_ref[...], ub_ref[...]], axis=1)
        dd_ref[...] += jnp.sum(dy0 * uv, axis=0, keepdims=True)

    row = pl.BlockSpec((tr, w), lambda i: (i, 0))
    vec = pl.BlockSpec((1, w), lambda i: (0, 0))
    mat = pl.BlockSpec((w, w), lambda i: (0, 0))
    return pl.pallas_call(
        body, out_shape=(jax.ShapeDtypeStruct((t, w), F32), jax.ShapeDtypeStruct((w, w), F32),
                         jax.ShapeDtypeStruct((1, w), F32), jax.ShapeDtypeStruct((1, w), F32)),
        grid=(t // tr,), in_specs=[row, row, row, *_last_cols_specs(u, w, tr), vec, mat],
        out_specs=(row, mat, vec, vec),
        compiler_params=_cparams(("arbitrary",)), name=name)(ds, y0, z, u, u, d, wglu)


def _du_combine(dy0, d, du_f, du_b, name):
    t, w = dy0.shape
    tr = _row_tile(t)

    def body(dy_ref, d_ref, a_ref, b_ref, o_ref):
        o_ref[...] = d_ref[...] * dy_ref[...] + a_ref[...] + b_ref[...]

    row = pl.BlockSpec((tr, w), lambda i: (i, 0))
    vec = pl.BlockSpec((1, w), lambda i: (0, 0))
    return pl.pallas_call(body, out_shape=jax.ShapeDtypeStruct((t, w), F32), grid=(t // tr,),
                          in_specs=[row, vec, row, row], out_specs=row, compiler_params=_cparams(("parallel",)),
                          name=name)(dy0, d, du_f, du_b)


def _ffn_fwd(x, g, wg, wu, wd, tag):
    xo, h, gate, up = _ffn_fwd_call(x, g, wg, wu, wd, f"{tag}_fwd")
    return xo, (h, gate, up)


def _ffn_bwd(dxo, dxo_b, x, g, wg, wu, wd, saved, tag):
    h, gate, up = saved
    dx, dx_b, dg, dgate, dup, act = _ffn_bwd_x_call(dxo, dxo_b, x, g, gate, up, wg, wu, wd, f"{tag}_bwd_x")
    dwg, dwu, dwd = _ffn_bwd_w_call(h, dxo_b, dgate, dup, act, f"{tag}_bwd_w")
    return dx, dx_b, dg, dwg, dwu, dwd


def _local_step(x, tgt, w, get_weights, put_grads, reduce_wide):
    t = x.shape[0]
    row = lambda a: a.reshape(1, -1)
    grads = {}

    w = dict(w)

    ssm_names = ['ssm_lambda_re', 'ssm_lambda_im', 'ssm_log_dt', 'ssm_b_re', 'ssm_b_im', 'ssm_c_re', 'ssm_c_im']
    tr3 = lambda m: jnp.swapaxes(m, 1, 2)
    fwd_ops, adj_ops, vjps = [], [], []
    for direction in range(2):
        rev = direction == 1
        prep, vjp = jax.vjp(_ssm_prep, *[w[n][direction] for n in ssm_names])
        a_re, a_im = prep[0], prep[1]
        mi_re, mi_im, mo_re, mo_im = (m.astype(BF16) for m in prep[2:])
        fwd_ops.append((mi_re, mi_im, _scan_tables(a_re, a_im, rev), mo_re, mo_im))
        adj_ops.append((tr3(mo_re), tr3(mo_im), _scan_tables(a_re, -a_im, not rev), tr3(mi_re), tr3(mi_im)))
        vjps.append(vjp)
    sink_rows = jnp.repeat(w['attn_sinks'].reshape(KV_HEADS, GQ), QBLOCK, axis=1)[..., None]
    slopes = jnp.asarray(2.0 ** (-8.0 * (np.arange(ATTN_HEADS) + 1) / ATTN_HEADS), F32)
    slope_rows = jnp.repeat(slopes.reshape(KV_HEADS, GQ), QBLOCK, axis=1)[..., None]
    prepared = sum(jnp.sum(op[:1, :1].astype(F32)) for ops in fwd_ops + adj_ops for op in ops) + sink_rows[0, 0, 0]

    w.update(get_weights('ffn1', prepared.reshape(1, 1)))
    x1, ffn1_saved = _ffn_fwd(x, w['norm_ffn1'], w['ffn1_w_gate'], w['ffn1_w_up'], w['ffn1_w_down'], "ffn1")
    w.update(get_weights('mix', x1))

    h2 = _rms_fwd(x1, w['norm_mix'], "mix_norm")
    proj = _mm(h2, w['w_in'], tb=True, name="in_proj")[0]
    u = proj

    attn, lse = _attn_fwd_proj(proj, sink_rows, slope_rows, "attn_fwd")

    ys, states = [], []
    for direction in range(2):
        y, xr, xi = _scan(u, *fwd_ops[direction], direction == 1, f"s5_fwd{direction}")
        ys.append(y)
        states.append((xr, xi))
    d_row = row(w['ssm_d'])
    s, y0, z = _ssm_post_fwd(u, ys[0], ys[1], d_row, w['ssm_glu_w'], row(w['ssm_glu_b']), "ssm_post")

    ma = _rms_fwd(attn, row(w['attn_out_norm']), "attn_out_norm")
    ms = _rms_fwd(s, row(w['ssm_out_norm']), "ssm_out_norm")
    mixed = jnp.concatenate([ma, ms], axis=-1)
    x2 = _mm(mixed, w['w_out'], res=x1, reduce_s=True, name="out_proj")

    w.update(get_weights('ffn2', x2))
    x3, ffn2_saved = _ffn_fwd(x2, w['norm_ffn2'], w['ffn2_w_gate'], w['ffn2_w_up'], w['ffn2_w_down'], "ffn2")

    loss_row, dx3, dx3_b, dgf = _loss_head(x3, row(w['final_norm']), tgt, "loss_head")
    loss = loss_row[0, 0]
    grads['final_norm'] = dgf.reshape(w['final_norm'].shape)

    dx2, dx2_b, dg, dwg, dwu, dwd = _ffn_bwd(dx3, dx3_b, x2, w['norm_ffn2'], w['ffn2_w_gate'], w['ffn2_w_up'],
                                             w['ffn2_w_down'], ffn2_saved, "ffn2")
    grads['norm_ffn2'] = dg
    sent = put_grads('ffn2', dict(ffn2_w_gate=dwg, ffn2_w_up=dwu, ffn2_w_down=dwd))

    dmixed = _mm(dx2_b, w['w_out'], tb=True, reduce_s=True, after=sent, name="out_proj_dx")
    dw_out = _mm(mixed, dx2_b, ta=True, out_dtype=BF16, name="out_proj_dw")[0]
    dattn, _, dga = _rms_bwd(attn, row(w['attn_out_norm']), dmixed[:, :ATTN_WIDTH], None, "attn_out_dnorm")
    ds, _, dgs = _rms_bwd(s, row(w['ssm_out_norm']), dmixed[:, ATTN_WIDTH:], None, "ssm_out_dnorm")
    grads.update(attn_out_norm=dga, ssm_out_norm=dgs)

    dy0, dwglu, dbglu, dd = _ssm_post_bwd(ds, y0, z, u, d_row, w['ssm_glu_w'], "ssm_post_bwd")
    grads['ssm_glu_b'] = dbglu
    grads['ssm_d'] = dd.reshape(w['ssm_d'].shape)
    dparams, du_dirs = [], []
    for direction in range(2):
        rev = direction == 1
        du_dir, lr, li = _scan(dy0, *adj_ops[direction], not rev, f"s5_adj{direction}")
        du_dirs.append(du_dir)
        xr, xi = states[direction]
        dmir, dmii, dmor, dmoi, da = _scan_param_grads(u, dy0, xr, xi, lr, li, rev, f"s5_pgrad{direction}")
        da_re = da[:, 0, :].reshape(-1)
        da_im = da[:, 1, :].reshape(-1)
        dparams.append(vjps[direction]((da_re, da_im, dmir, dmii, dmor, dmoi)))
    du = _du_combine(dy0, d_row, du_dirs[0], du_dirs[1], "ssm_du")
    for i, n in enumerate(ssm_names):
        grads[n] = jnp.stack([dparams[0][i], dparams[1][i]])
    wide_sum = reduce_wide(grads)

    dq, dk, dv, dsink = _attn_bwd_proj(proj, sink_rows, slope_rows, attn, lse, dattn, "attn_bwd")
    grads['attn_sinks'] = dsink.reshape(w['attn_sinks'].shape)
    dproj = jnp.concatenate([dq, dk, dv, du], axis=-1).astype(BF16)

    dw_in = _mm(dproj, h2, ta=True, out_dtype=BF16, after=wide_sum, name="in_proj_dw")[0]
    sent = put_grads('mix', dict(w_in=dw_in, ssm_glu_w=dwglu, w_out=dw_out))
    dh2 = _mm(dproj, w['w_in'], reduce_s=True, after=sent, name="in_proj_dx")
    dx1, dx1_b, dgm = _rms_bwd(x1, w['norm_mix'], dh2, dx2, "mix_dnorm")
    grads['norm_mix'] = dgm

    dx0, _, dg, dwg, dwu, dwd = _ffn_bwd(dx1, dx1_b, x, w['norm_ffn1'], w['ffn1_w_gate'], w['ffn1_w_up'],
                                         w['ffn1_w_down'], ffn1_saved, "ffn1")
    grads['norm_ffn1'] = dg
    put_grads('ffn1', dict(ffn1_w_gate=dwg, ffn1_w_up=dwu, ffn1_w_down=dwd))
    return loss, dx0, grads, wide_sum


HBM_SPEC = pl.BlockSpec(memory_space=pl.ANY)


def _chip_peers(x, y):
    return [(1 - x, y), (x, 1 - y), (1 - x, 1 - y)]


HBM_ONLY = pl.BlockSpec(memory_space=pltpu.HBM)
SEM_SPEC = pl.BlockSpec(memory_space=pltpu.SEMAPHORE)
EFFECT = pltpu.SideEffectType.DATAFLOW_SIDE_EFFECTING


def _place_own(src, slot, name):
    r, c = src.shape
    tr = r // 2

    def body(slot_ref, s_ref, o_ref):
        o_ref[0] = s_ref[...]

    return pl.pallas_call(
        body, out_shape=jax.ShapeDtypeStruct((N_CHIPS, r, c), src.dtype),
        grid_spec=pltpu.PrefetchScalarGridSpec(
            num_scalar_prefetch=1, grid=(2,), in_specs=[pl.BlockSpec((tr, c), lambda i, s: (i, 0))],
            out_specs=pl.BlockSpec((1, tr, c), lambda i, s: (s[0], i, 0))),
        compiler_params=_cparams(("parallel",)), name=name)(slot, src)


def _chip_copies(srcs, lands, send_sems, recv_sems, scatter, landed):
    x, y, c = lax.axis_index("x"), lax.axis_index("y"), lax.axis_index("c")
    me = 2 * x + y
    out = []
    for i in range(len(srcs)):
        for j, (px, py) in enumerate(_chip_peers(x, y)):
            p = 2 * px + py
            slot = p if landed else me
            if scatter:
                src, dst = srcs[i].at[p], lands[i].at[slot]
            else:
                rows = _core_half(srcs[i].shape[0], c)
                src, dst = srcs[i].at[rows], lands[i].at[slot, rows]
            out.append(pltpu.make_async_remote_copy(src, dst, send_sems.at[3 * i + j], recv_sems.at[3 * i + j],
                                                    device_id=(px, py, c), device_id_type=MESH))
    return out


def _core_half(nrows, c):
    half = nrows // 2
    return pl.ds(pl.multiple_of(c * half, 16), half)


def _sibling_forward(lands, name):
    n = len(lands)

    def body(*refs):
        bufs = refs[n:2 * n]
        send_sems, recv_sems = refs[2 * n:]
        x, y, c = lax.axis_index("x"), lax.axis_index("y"), lax.axis_index("c")
        mine = [_core_half(b.shape[1], c) for b in bufs]
        theirs = [_core_half(b.shape[1], 1 - c) for b in bufs]
        chips = [2 * px + py for px, py in _chip_peers(x, y)]
        cps = [pltpu.make_async_remote_copy(bufs[i].at[p, mine[i]], bufs[i].at[p, mine[i]], send_sems.at[3 * i + j],
                                            recv_sems.at[3 * i + j], device_id=(x, y, 1 - c), device_id_type=MESH)
               for i in range(n) for j, p in enumerate(chips)]
        for cp in cps:
            cp.start()
        for i in range(n):
            for j, p in enumerate(chips):
                pltpu.make_async_remote_copy(bufs[i].at[p, mine[i]], bufs[i].at[p, theirs[i]], send_sems.at[3 * i + j],
                                             recv_sems.at[3 * i + j], device_id=(x, y, 1 - c),
                                             device_id_type=MESH).wait()

    return pl.pallas_call(
        body, out_shape=[jax.ShapeDtypeStruct(a.shape, a.dtype) for a in lands],
        in_specs=[HBM_SPEC] * n, out_specs=[HBM_SPEC] * n, input_output_aliases={k: k for k in range(n)},
        scratch_shapes=[pltpu.SemaphoreType.DMA((3 * n,)), pltpu.SemaphoreType.DMA((3 * n,))],
        name=name)(*lands)


def _exchange_start(groups, scatter, name, after=None):
    sizes = [len(srcs) for srcs, _ in groups]
    flat_src = [a for srcs, _ in groups for a in srcs]
    flat_land = [a for _, lands in groups for a in lands]
    n = len(flat_src)
    ng = len(groups)

    def body(*refs):
        src_refs, land_refs = refs[:n], refs[n:2 * n]
        n_in = 2 * n + (after is not None)
        sems = refs[n_in:n_in + 2 * ng]
        token_ref = refs[-1]
        off = 0
        for gi, sz in enumerate(sizes):
            for cp in _chip_copies(src_refs[off:off + sz], land_refs[off:off + sz], sems[2 * gi], sems[2 * gi + 1],
                                   scatter, landed=False):
                cp.start()
            off += sz
        token_ref[...] = jnp.zeros_like(token_ref)

    sem_shapes = []
    for sz in sizes:
        sem_shapes += [pltpu.SemaphoreType.DMA((3 * sz,)), pltpu.SemaphoreType.DMA((3 * sz,))]
    hbm = lambda a: pltpu.HBM(a.shape, a.dtype)
    res = pl.pallas_call(
        body, name=name,
        out_shape=(tuple(sem_shapes) + tuple(hbm(a) for a in flat_src) + tuple(hbm(a) for a in flat_land)
                   + (jax.ShapeDtypeStruct((SUBLANES, LANES), F32),)),
        in_specs=[HBM_ONLY] * (2 * n) + [HBM_SPEC] * (after is not None),
        out_specs=tuple([SEM_SPEC] * (2 * ng) + [HBM_ONLY] * (2 * n) + [pl.BlockSpec(memory_space=pltpu.VMEM)]),
        input_output_aliases={k: 2 * ng + k for k in range(2 * n)},
        compiler_params=pltpu.CompilerParams(has_side_effects=EFFECT),
    )(*[pltpu.with_memory_space_constraint(a, pltpu.HBM) for a in flat_src + flat_land],
      *([after] if after is not None else []))
    sems, thru_src, thru_land = res[:2 * ng], res[2 * ng:2 * ng + n], res[2 * ng + n:2 * ng + 2 * n]
    out, off = [], 0
    for gi, sz in enumerate(sizes):
        out.append((sems[2 * gi], sems[2 * gi + 1], list(thru_src[off:off + sz]), list(thru_land[off:off + sz])))
        off += sz
    return out, res[-1]


def _exchange_wait(started, after, scatter, name):
    send_sems, recv_sems, srcs, lands = started
    n = len(srcs)

    def body(*refs):
        src_refs, land_refs = refs[:n], refs[n:2 * n]
        send_ref, recv_ref = refs[2 * n], refs[2 * n + 1]
        for cp in _chip_copies(src_refs, land_refs, send_ref, recv_ref, scatter, landed=True):
            cp.wait_send()
            cp.wait_recv()

    hbm = lambda a: pltpu.HBM(a.shape, a.dtype)
    res = pl.pallas_call(
        body, name=name, out_shape=tuple(hbm(a) for a in srcs) + tuple(hbm(a) for a in lands),
        in_specs=[HBM_ONLY] * (2 * n) + [SEM_SPEC, SEM_SPEC, HBM_SPEC], out_specs=tuple([HBM_ONLY] * (2 * n)),
        input_output_aliases={k: k for k in range(2 * n)},
        compiler_params=pltpu.CompilerParams(has_side_effects=EFFECT),
    )(*srcs, *lands, send_sems, recv_sems, after)
    return list(res[:n]), list(res[n:])


def _small_exchange(smalls, name):
    nsm = len(smalls)
    rels = [(fx, fy, fc) for fx in (0, 1) for fy in (0, 1) for fc in (0, 1)][1:]

    def body(*refs):
        sins, souts = refs[:nsm], refs[nsm:2 * nsm]
        ssend, srecv, slocal = refs[2 * nsm:]
        x, y, c = lax.axis_index("x"), lax.axis_index("y"), lax.axis_index("c")
        lin = 4 * x + 2 * y + c
        local = [pltpu.make_async_copy(sins[i], souts[i].at[lin], slocal.at[i]) for i in range(nsm)]
        for cp in local:
            cp.start()
        for i in range(nsm):
            for j, (fx, fy, fc) in enumerate(rels):
                pltpu.make_async_remote_copy(sins[i], souts[i].at[lin], ssend.at[i, j], srecv.at[i, j],
                                             device_id=(x ^ fx, y ^ fy, c ^ fc), device_id_type=MESH).start()
        for i in range(nsm):
            for j, (fx, fy, fc) in enumerate(rels):
                src = 4 * (x ^ fx) + 2 * (y ^ fy) + (c ^ fc)
                pltpu.make_async_remote_copy(sins[i], souts[i].at[src], ssend.at[i, j], srecv.at[i, j],
                                             device_id=(x ^ fx, y ^ fy, c ^ fc), device_id_type=MESH).wait()
        for cp in local:
            cp.wait()

    return pl.pallas_call(
        body, out_shape=[jax.ShapeDtypeStruct((N_DEV,) + s.shape, s.dtype) for s in smalls],
        in_specs=[HBM_SPEC] * nsm, out_specs=[HBM_SPEC] * nsm,
        scratch_shapes=[pltpu.SemaphoreType.DMA((nsm, 7)), pltpu.SemaphoreType.DMA((nsm, 7)),
                        pltpu.SemaphoreType.DMA((nsm,))],
        name=name)(*smalls)


def _sibling_swap(arrs, name):
    nw = len(arrs)

    def body(*refs):
        ins, outs = refs[:nw], refs[nw:2 * nw]
        send_sems, recv_sems = refs[2 * nw:]
        x, y, c = lax.axis_index("x"), lax.axis_index("y"), lax.axis_index("c")
        cps = [pltpu.make_async_remote_copy(ins[i], outs[i], send_sems.at[i], recv_sems.at[i],
                                            device_id=(x, y, 1 - c), device_id_type=MESH) for i in range(nw)]
        for cp in cps:
            cp.start()
        for cp in cps:
            cp.wait()

    return pl.pallas_call(
        body, out_shape=[jax.ShapeDtypeStruct(a.shape, a.dtype) for a in arrs],
        in_specs=[HBM_SPEC] * nw, out_specs=[HBM_SPEC] * nw,
        scratch_shapes=[pltpu.SemaphoreType.DMA((nw,)), pltpu.SemaphoreType.DMA((nw,))],
        name=name)(*arrs)


def _sum_parts(parts, recv, slots, name):
    na = len(parts)
    _, r, c = parts[0].shape
    tr = _row_tile(r)

    def body(slot_ref, *refs):
        for a in range(na):
            own_ref, r0_ref, r1_ref, r2_ref = refs[4 * a:4 * a + 4]
            refs[4 * na + a][...] = ((own_ref[0].astype(F32) + r0_ref[0].astype(F32))
                                     + (r1_ref[0].astype(F32) + r2_ref[0].astype(F32)))

    blk = lambda k: pl.BlockSpec((1, tr, c), lambda i, s, k=k: (s[k], i, 0))
    out_blk = pl.BlockSpec((tr, c), lambda i, s: (i, 0))
    args = [a for p, rv in zip(parts, recv) for a in (p, rv, rv, rv)]
    return pl.pallas_call(
        body, out_shape=[jax.ShapeDtypeStruct((r, c), F32)] * na,
        grid_spec=pltpu.PrefetchScalarGridSpec(
            num_scalar_prefetch=1, grid=(r // tr,), in_specs=[blk(0), blk(1), blk(2), blk(3)] * na,
            out_specs=[out_blk] * na),
        compiler_params=_cparams(("parallel",)), name=name)(slots, *args)


def _small_allreduce(packed, name):
    rows = packed.shape[0]
    pr = rows // N_DEV
    rels = [(fx, fy, fc) for fx in (0, 1) for fy in (0, 1) for fc in (0, 1)][1:]

    def body(in_ref, out_ref, recv_ref, send1, recv1, send2, recv2):
        x, y, c = lax.axis_index("x"), lax.axis_index("y"), lax.axis_index("c")
        lin = 4 * x + 2 * y + c
        piece = lambda ref, k: ref.at[pl.ds(pl.multiple_of(k * pr, pr), pr), :]
        peers = [((x ^ fx, y ^ fy, c ^ fc), 4 * (x ^ fx) + 2 * (y ^ fy) + (c ^ fc)) for fx, fy, fc in rels]
        for j, (dev, plin) in enumerate(peers):
            pltpu.make_async_remote_copy(piece(in_ref, plin), recv_ref.at[lin], send1.at[j], recv1.at[j],
                                         device_id=dev, device_id_type=MESH).start()
        recv_ref[lin] = piece(in_ref, lin)[...]
        for j, (dev, plin) in enumerate(peers):
            pltpu.make_async_remote_copy(piece(in_ref, plin), recv_ref.at[plin], send1.at[j], recv1.at[j],
                                         device_id=dev, device_id_type=MESH).wait()
        acc = recv_ref[0]
        for k in range(1, N_DEV):
            acc = acc + recv_ref[k]
        piece(out_ref, lin)[...] = acc
        for j, (dev, plin) in enumerate(peers):
            pltpu.make_async_remote_copy(piece(out_ref, lin), piece(out_ref, lin), send2.at[j], recv2.at[j],
                                         device_id=dev, device_id_type=MESH).start()
        for j, (dev, plin) in enumerate(peers):
            pltpu.make_async_remote_copy(piece(out_ref, lin), piece(out_ref, plin), send2.at[j], recv2.at[j],
                                         device_id=dev, device_id_type=MESH).wait()

    vm = pl.BlockSpec(memory_space=pltpu.VMEM)
    return pl.pallas_call(
        body, out_shape=jax.ShapeDtypeStruct(packed.shape, F32), in_specs=[vm], out_specs=vm,
        scratch_shapes=[pltpu.VMEM((N_DEV, pr, LANES), F32)] + [pltpu.SemaphoreType.DMA((7,))] * 4,
        compiler_params=pltpu.CompilerParams(vmem_limit_bytes=VMEM_LIMIT), name=name)(packed)


def _adamw_math(w, m, v, g):
    nm = ADAM_B1 * m + (1.0 - ADAM_B1) * g
    nv = ADAM_B2 * v + (1.0 - ADAM_B2) * (g * g)
    m_hat = nm * (1.0 / (1.0 - ADAM_B1 ** ADAM_STEP))
    v_hat = nv * (1.0 / (1.0 - ADAM_B2 ** ADAM_STEP))
    return -ADAM_LR * (m_hat / (jnp.sqrt(v_hat) + ADAM_EPS) + ADAM_WD * w), nm, nv


def _adamw(ws, ms, vs, g_mine, g_other, name):
    na = len(ws)
    r, c = ws[0].shape
    tr = _row_tile(r)

    def body(*refs):
        for a in range(na):
            w_ref, m_ref, v_ref, g1_ref, g2_ref = refs[5 * a:5 * a + 5]
            g_ref, d_ref, nm_ref, nv_ref = refs[5 * na + 4 * a:5 * na + 4 * a + 4]
            g = g1_ref[...] + g2_ref[...]
            g_ref[...] = g
            d_ref[...], nm_ref[...], nv_ref[...] = _adamw_math(w_ref[...], m_ref[...], v_ref[...], g)

    blk = pl.BlockSpec((tr, c), lambda i: (i, 0))
    sh = jax.ShapeDtypeStruct((r, c), F32)
    args = [a for group in zip(ws, ms, vs, g_mine, g_other) for a in group]
    res = pl.pallas_call(body, out_shape=[sh] * (4 * na), grid=(r // tr,), in_specs=[blk] * (5 * na),
                         out_specs=[blk] * (4 * na), compiler_params=_cparams(("parallel",)), name=name)(*args)
    return [tuple(res[4 * a:4 * a + 4]) for a in range(na)]


def _adamw_small(ws, ms, vs, alls, split, name):
    n = len(ws)
    lead = split if split is not None else ()
    nl = len(lead)
    nslots = alls[0].shape[0]

    def blocks(shape):
        if split is None:
            return tuple(shape), (lambda *g: (0,) * len(shape))
        blk = (shape[0], shape[1] // lead[0], shape[2] // lead[1]) + tuple(shape[3:])
        return blk, (lambda *g: (0, g[0], g[1]) + (0,) * (len(shape) - 3))

    def body(*refs):
        w_refs, m_refs, v_refs, a_refs = (refs[k * n:(k + 1) * n] for k in range(4))
        g_refs, d_refs, nm_refs, nv_refs = (refs[(4 + k) * n:(5 + k) * n] for k in range(4))
        k = pl.program_id(nl)
        for i in range(n):
            @pl.when(k == 0)
            def _(i=i):
                g_refs[i][...] = a_refs[i][0]

            @pl.when(k > 0)
            def _(i=i):
                g_refs[i][...] += a_refs[i][0]

            @pl.when(k == nslots - 1)
            def _(i=i):
                d_refs[i][...], nm_refs[i][...], nv_refs[i][...] = _adamw_math(
                    w_refs[i][...], m_refs[i][...], v_refs[i][...], g_refs[i][...])

    specs, aspecs, shapes = [], [], []
    for wa in ws:
        blk, imap = blocks(wa.shape)
        specs.append(pl.BlockSpec(blk, imap))
        aspecs.append(pl.BlockSpec((1,) + blk, (lambda *g, imap=imap: (g[nl],) + imap(*g))))
        shapes.append(jax.ShapeDtypeStruct(wa.shape, F32))
    res = pl.pallas_call(
        body, out_shape=shapes * 4, grid=tuple(lead) + (nslots,), in_specs=specs * 3 + aspecs,
        out_specs=specs * 4, compiler_params=_cparams(("parallel",) * nl + ("arbitrary",)),
        name=name)(*ws, *ms, *vs, *alls)
    return res[:n], res[n:2 * n], res[2 * n:3 * n], res[3 * n:]


def kernel(x, norm_ffn1, ffn1_w_gate, ffn1_w_up, ffn1_w_down, norm_mix, w_in, attn_sinks, ssm_lambda_re, ssm_lambda_im, ssm_log_dt, ssm_b_re, ssm_b_im, ssm_c_re, ssm_c_im, ssm_d, ssm_glu_w, ssm_glu_b, attn_out_norm, ssm_out_norm, w_out, norm_ffn2, ffn2_w_gate, ffn2_w_up, ffn2_w_down, final_norm, loss_target, m_norm_ffn1, m_ffn1_w_gate, m_ffn1_w_up, m_ffn1_w_down, m_norm_mix, m_w_in, m_attn_sinks, m_ssm_lambda_re, m_ssm_lambda_im, m_ssm_log_dt, m_ssm_b_re, m_ssm_b_im, m_ssm_c_re, m_ssm_c_im, m_ssm_d, m_ssm_glu_w, m_ssm_glu_b, m_attn_out_norm, m_ssm_out_norm, m_w_out, m_norm_ffn2, m_ffn2_w_gate, m_ffn2_w_up, m_ffn2_w_down, m_final_norm, v_norm_ffn1, v_ffn1_w_gate, v_ffn1_w_up, v_ffn1_w_down, v_norm_mix, v_w_in, v_attn_sinks, v_ssm_lambda_re, v_ssm_lambda_im, v_ssm_log_dt, v_ssm_b_re, v_ssm_b_im, v_ssm_c_re, v_ssm_c_im, v_ssm_d, v_ssm_glu_w, v_ssm_glu_b, v_attn_out_norm, v_ssm_out_norm, v_w_out, v_norm_ffn2, v_ffn2_w_gate, v_ffn2_w_up, v_ffn2_w_down, v_final_norm):
    given = dict(locals())
    wts = {n: given[n] for n in WEIGHTS}

    order = [g for g in GROUPS]
    cx, cy = lax.axis_index("x"), lax.axis_index("y")
    slots = jnp.stack([2 * cx + cy, 2 * (1 - cx) + cy, 2 * cx + 1 - cy, 2 * (1 - cx) + 1 - cy]).astype(jnp.int32)
    def view(a, n):
        if n in TRANSPOSED:
            return jnp.swapaxes(a[0], 0, 1)
        if n in BIG:
            return a[0]
        if n in ('ssm_b_re', 'ssm_b_im'):
            return jnp.swapaxes(a, -1, -2)
        return a.reshape(1, -1) if a.ndim == 1 else a

    def unview(a, n):
        if n in TRANSPOSED:
            return jnp.swapaxes(a, 0, 1)[None]
        if n in ('ssm_b_re', 'ssm_b_im'):
            return jnp.swapaxes(a, -1, -2)
        return a.reshape(wts[n].shape)

    started, gather_token = {}, None
    for g in order:
        shards = [view(wts[n], n).astype(BF16) for n in GROUPS[g]]
        placed = [_place_own(s, slots, f"weights_place_{n}") for n, s in zip(GROUPS[g], shards)]
        st, gather_token = _exchange_start([(shards, placed)], False, f"weights_start_{g}", after=gather_token)
        started[g] = st[0]

    def get_weights(group, after):
        if group == order[0]:
            after = after + gather_token[:1, :1]
        _, lands = _exchange_wait(started[group], after, False, f"weights_wait_{group}")
        lands = _sibling_forward(lands, f"weights_forward_{group}")
        out = dict(zip(GROUPS[group], lands))
        for n in ('w_in', 'ssm_glu_w', 'w_out'):
            if n in out:
                out[n] = out[n].reshape(-1, out[n].shape[-1])
        return out

    sent, tokens = {}, {}

    def put_grads(group, gd):
        parts = []
        for n in GROUPS[group]:
            g = gd[n]
            if g.ndim == 2:
                g = g.reshape(N_CHIPS, g.shape[0] // N_CHIPS, g.shape[1])
            parts.append(g.astype(BF16))
        lands = [lax.empty(p.shape, p.dtype) for p in parts]
        started_g, tokens[group] = _exchange_start([(parts, lands)], True, f"grads_start_{group}")
        sent[group] = started_g[0]
        return tokens[group]

    w = {n: (wts[n][0] if wts[n].ndim > 1 else wts[n]) for n in SMALL}
    w['norm_ffn1'], w['norm_mix'], w['norm_ffn2'] = wts['norm_ffn1'], wts['norm_mix'], wts['norm_ffn2']
    w['ssm_b_re'], w['ssm_b_im'] = view(wts['ssm_b_re'], 'ssm_b_re')[0], view(wts['ssm_b_im'], 'ssm_b_im')[0]
    wide = ['ssm_b_re', 'ssm_b_im', 'ssm_c_re', 'ssm_c_im']

    def reduce_wide(gd):
        packed = jnp.concatenate([gd[n].reshape(-1, LANES) for n in wide])
        return _small_allreduce(packed, "small_grads_allreduce")

    loss, dx, grads, wide_sum = _local_step(x[0], loss_target[0], w, get_weights, put_grads, reduce_wide)
    loss = lax.psum(loss, ("x", "y", "c"))

    out_g, out_d, out_m, out_v = {}, {}, {}, {}

    def finish(group, after):
        names = GROUPS[group]
        parts, recv = _exchange_wait(sent[group], after, True, f"grads_wait_{group}")
        same = len({p.shape for p in parts}) == 1
        batches = [list(range(len(names)))] if same else [[i] for i in range(len(names))]
        chip_sums = [None] * len(names)
        for b in batches:
            sums = _sum_parts([parts[i] for i in b], [recv[i] for i in b], slots, f"grad_sum_{names[b[0]]}")
            for i, sm in zip(b, sums):
                chip_sums[i] = sm
        other = _sibling_swap(chip_sums, f"grad_sibling_swap_{group}")
        for b in batches:
            res = _adamw([view(wts[names[i]], names[i]) for i in b], [view(given['m_' + names[i]], names[i]) for i in b],
                         [view(given['v_' + names[i]], names[i]) for i in b], [chip_sums[i] for i in b],
                         [other[i] for i in b], f"adamw_{names[b[0]]}")
            for i, (g, d, nm, nv) in zip(b, res):
                n = names[i]
                out_g[n], out_d[n], out_m[n], out_v[n] = (unview(a, n) for a in (g, d, nm, nv))
        return nv

    done = finish('ffn2', tokens['ffn1'])
    done = finish('mix', done)

    nat = {n: view(wts[n], n).shape for n in SMALL}
    narrow = [n for n in SMALL if n not in wide]
    alls = _small_exchange([grads[n].reshape(nat[n]) for n in narrow], "small_grads_allgather")
    rows = wide_sum.shape[0] // len(wide)
    wide_g = [wide_sum[i * rows:(i + 1) * rows].reshape((1,) + nat[n]) for i, n in enumerate(wide)]
    for group, gs, split, tag in ((narrow, alls, None, "adamw_small"), (wide, wide_g, (2, 4), "adamw_ssm_bc")):
        res = _adamw_small([view(wts[n], n) for n in group], [view(given['m_' + n], n) for n in group],
                           [view(given['v_' + n], n) for n in group], gs, split, tag)
        for dst, vals in zip((out_g, out_d, out_m, out_v), res):
            for n, a in zip(group, vals):
                dst[n] = unview(a, n)

    finish('ffn1', out_v['ssm_b_re'])

    return (loss, dx[None], *[out_g[n] for n in WEIGHTS], *[out_d[n] for n in WEIGHTS],
            *[out_m[n] for n in WEIGHTS], *[out_v[n] for n in WEIGHTS])
```

```python
import functools
import math

import numpy as np
import jax
import jax.numpy as jnp
from jax import lax
from jax.experimental import pallas as pl
from jax.experimental.pallas import tpu as pltpu

F32 = jnp.float32
BF16 = jnp.bfloat16
MESH = pl.DeviceIdType.MESH

EPS = 1e-6
NEG_INF = -1e30
LAMBDA_RE_MAX = -1e-4
ATTN_HEADS = 8
KV_HEADS = 2
GQ = ATTN_HEADS // KV_HEADS
HEAD_DIM = 64
ATTN_WIDTH = 512
KV_WIDTH = 128
WINDOW = 128
QBLOCK = 128
SSM_WIDTH = 512
SSM_GROUPS = 32
SSM_CH = 16
SSM_STATE = 64
N_STRIPS = 4
STRIP_IN = SSM_WIDTH // N_STRIPS
STRIP_ST = SSM_GROUPS * SSM_STATE // N_STRIPS
SUBLANES = 8
LANES = 128
N_CHIPS = 4
N_DEV = 8

ADAM_LR = 0.001
ADAM_B1 = 0.9
ADAM_B2 = 0.999
ADAM_EPS = 1e-08
ADAM_WD = 0.01
ADAM_STEP = 10

VMEM_LIMIT = 48 * 1024 * 1024

WEIGHTS = ['norm_ffn1', 'ffn1_w_gate', 'ffn1_w_up', 'ffn1_w_down', 'norm_mix', 'w_in', 'attn_sinks',
           'ssm_lambda_re', 'ssm_lambda_im', 'ssm_log_dt', 'ssm_b_re', 'ssm_b_im', 'ssm_c_re', 'ssm_c_im',
           'ssm_d', 'ssm_glu_w', 'ssm_glu_b', 'attn_out_norm', 'ssm_out_norm', 'w_out', 'norm_ffn2',
           'ffn2_w_gate', 'ffn2_w_up', 'ffn2_w_down', 'final_norm']
BIG = ['ffn1_w_gate', 'ffn1_w_up', 'ffn1_w_down', 'w_in', 'ssm_glu_w', 'w_out',
       'ffn2_w_gate', 'ffn2_w_up', 'ffn2_w_down']
SMALL = [n for n in WEIGHTS if n not in BIG]
TRANSPOSED = ['ffn1_w_gate', 'ffn1_w_up', 'w_in', 'ffn2_w_gate', 'ffn2_w_up']
GROUPS = {'ffn1': ['ffn1_w_gate', 'ffn1_w_up', 'ffn1_w_down'],
          'mix': ['w_in', 'ssm_glu_w', 'w_out'],
          'ffn2': ['ffn2_w_gate', 'ffn2_w_up', 'ffn2_w_down']}


def _cparams(sem=None):
    return pltpu.CompilerParams(dimension_semantics=sem, vmem_limit_bytes=VMEM_LIMIT)


def _tile(n, pref):
    if n <= pref:
        return n
    for t in (pref, pref // 2, pref // 4):
        if t % LANES == 0 and n % t == 0:
            return t
    return n


def _sigmoid(x):
    return 1.0 / (1.0 + jnp.exp(-x))


def _mm(a, b, *, ta=False, tb=False, reduce_s=False, res=None, scale=1.0, out_dtype=F32, after=None, name):
    a3 = a if a.ndim == 3 else a[None]
    b3 = b if b.ndim == 3 else b[None]
    sa, sb = a3.shape[0], b3.shape[0]
    ns = max(sa, sb)
    (kk, m) = a3.shape[1:] if ta else a3.shape[1:][::-1]
    (n, kb) = b3.shape[1:] if tb else b3.shape[1:][::-1]
    assert kk == kb, (a3.shape, b3.shape)
    tm, tn, tk = _tile(m, 1024), _tile(n, 1024), _tile(kk, 2048)
    nm, nn, nk = m // tm, n // tn, kk // tk
    has_res = res is not None
    single = nk == 1 and not (reduce_s and ns > 1)

    if reduce_s:
        grid = (nm, nn, ns, nk)
        ids = lambda i, j, s, k: (s, i, j, k)
        sem = ("parallel", "parallel", "arbitrary", "arbitrary")
    else:
        grid = (ns, nm, nn, nk)
        ids = lambda s, i, j, k: (s, i, j, k)
        sem = ("parallel", "parallel", "parallel", "arbitrary")

    def a_map(*g):
        s, i, j, k = ids(*g)
        s = s if sa > 1 else 0
        return (s, k, i) if ta else (s, i, k)

    def b_map(*g):
        s, i, j, k = ids(*g)
        s = s if sb > 1 else 0
        return (s, j, k) if tb else (s, k, j)

    def o_map(*g):
        s, i, j, k = ids(*g)
        return (i, j) if reduce_s else (s, i, j)

    a_blk = (1, tk, tm) if ta else (1, tm, tk)
    b_blk = (1, tn, tk) if tb else (1, tk, tn)
    dims = (((0 if ta else 1,), (1 if tb else 0,)), ((), ()))

    def body(*refs):
        a_ref, b_ref = refs[0], refs[1]
        r_ref = refs[2] if has_res else None
        o_ref = refs[2 + has_res + (after is not None)]
        acc_ref = None if single else refs[-1]
        s, _, _, k = ids(*[pl.program_id(d) for d in range(4)])
        prod = lax.dot_general(a_ref[0].astype(BF16), b_ref[0].astype(BF16), dims, preferred_element_type=F32)

        def finish(out):
            if scale != 1.0:
                out = out * scale
            if has_res:
                out = r_ref[...].reshape(out.shape) + out
            o_ref[...] = out.astype(out_dtype).reshape(o_ref.shape)

        if single:
            finish(prod)
            return
        if reduce_s:
            first = jnp.logical_and(s == 0, k == 0)
            last = jnp.logical_and(s == ns - 1, k == nk - 1)
        else:
            first, last = k == 0, k == nk - 1

        @pl.when(first)
        def _():
            acc_ref[...] = prod

        @pl.when(jnp.logical_not(first))
        def _():
            acc_ref[...] += prod

        @pl.when(last)
        def _():
            finish(acc_ref[...])

    in_specs = [pl.BlockSpec(a_blk, a_map), pl.BlockSpec(b_blk, b_map)]
    args = [a3, b3]
    if reduce_s:
        out_shape = jax.ShapeDtypeStruct((m, n), out_dtype)
        o_spec = pl.BlockSpec((tm, tn), o_map)
    else:
        out_shape = jax.ShapeDtypeStruct((ns, m, n), out_dtype)
        o_spec = pl.BlockSpec((1, tm, tn), o_map)
    if has_res:
        assert res.shape == out_shape.shape
        in_specs.append(o_spec)
        args.append(res)
    if after is not None:
        in_specs.append(HBM_SPEC)
        args.append(after)
    return pl.pallas_call(body, out_shape=out_shape, grid=grid, in_specs=in_specs, out_specs=o_spec,
                          scratch_shapes=[] if single else [pltpu.VMEM((tm, tn), F32)],
                          compiler_params=_cparams(sem), name=name)(*args)


def _row_tile(t):
    for tr in (256, 128, 64, 32, 16, 8):
        if t % tr == 0:
            return tr
    return t


def _rms_fwd(x, g, name):
    t, w = x.shape
    tr = _row_tile(t)

    def body(x_ref, g_ref, o_ref):
        xv = x_ref[...]
        r = lax.rsqrt(jnp.mean(xv * xv, axis=-1, keepdims=True) + EPS)
        o_ref[...] = (xv * r * g_ref[...]).astype(BF16)

    return pl.pallas_call(
        body, out_shape=jax.ShapeDtypeStruct((t, w), BF16), grid=(t // tr,),
        in_specs=[pl.BlockSpec((tr, w), lambda i: (i, 0)), pl.BlockSpec((1, w), lambda i: (0, 0))],
        out_specs=pl.BlockSpec((tr, w), lambda i: (i, 0)), compiler_params=_cparams(("parallel",)),
        name=name)(x, g)


def _rms_bwd_rows(xv, gv, dhv):
    r = lax.rsqrt(jnp.mean(xv * xv, axis=-1, keepdims=True) + EPS)
    nrm = xv * r
    dn = dhv * gv
    return r * (dn - nrm * jnp.mean(dn * nrm, axis=-1, keepdims=True)), dhv * nrm


def _rms_bwd(x, g, dh, dres, name):
    t, w = x.shape
    tr = _row_tile(t)
    has_res = dres is not None

    def body(*refs):
        if has_res:
            x_ref, g_ref, dh_ref, dr_ref, dx_ref, dxb_ref, dg_ref = refs
        else:
            x_ref, g_ref, dh_ref, dx_ref, dxb_ref, dg_ref = refs
        dx, dgs = _rms_bwd_rows(x_ref[...], g_ref[...], dh_ref[...])
        if has_res:
            dx = dx + dr_ref[...]
        dx_ref[...] = dx
        dxb_ref[...] = dx.astype(BF16)

        @pl.when(pl.program_id(0) == 0)
        def _():
            dg_ref[...] = jnp.zeros_like(dg_ref)

        dg_ref[...] += jnp.sum(dgs, axis=0, keepdims=True)

    row = pl.BlockSpec((tr, w), lambda i: (i, 0))
    vec = pl.BlockSpec((1, w), lambda i: (0, 0))
    ins = [x, g, dh] + ([dres] if has_res else [])
    return pl.pallas_call(
        body, out_shape=(jax.ShapeDtypeStruct((t, w), F32), jax.ShapeDtypeStruct((t, w), BF16),
                         jax.ShapeDtypeStruct((1, w), F32)),
        grid=(t // tr,), in_specs=[row, vec, row] + ([row] if has_res else []),
        out_specs=(row, row, vec), compiler_params=_cparams(("arbitrary",)), name=name)(*ins)


FFN_ROWS = 512


NT_DIMS = (((1,), (1,)), ((), ()))
TN_DIMS = (((0,), (0,)), ((), ()))


def _ffn_fwd_call(x, g, wg, wu, wd, name):
    t, d = x.shape
    ns, f, _ = wg.shape
    tm = _tile(t, FFN_ROWS)

    def body(x_ref, g_ref, wg_ref, wu_ref, wd_ref, xo_ref, h_ref, gate_ref, up_ref, h_sc, acc_ref):
        s = pl.program_id(1)

        @pl.when(s == 0)
        def _():
            xv = x_ref[...]
            r = lax.rsqrt(jnp.mean(xv * xv, axis=-1, keepdims=True) + EPS)
            hb = (xv * r * g_ref[...]).astype(BF16)
            h_sc[...] = hb
            h_ref[...] = hb

        hb = h_sc[...]
        gate = lax.dot_general(hb, wg_ref[0], NT_DIMS, preferred_element_type=F32)
        up = lax.dot_general(hb, wu_ref[0], NT_DIMS, preferred_element_type=F32)
        gate_ref[0] = gate.astype(BF16)
        up_ref[0] = up.astype(BF16)
        act = (gate * _sigmoid(gate) * up).astype(BF16)
        prod = jnp.dot(act, wd_ref[0], preferred_element_type=F32)

        @pl.when(s == 0)
        def _():
            acc_ref[...] = prod

        @pl.when(s > 0)
        def _():
            acc_ref[...] += prod

        @pl.when(s == ns - 1)
        def _():
            xo_ref[...] = x_ref[...] + 0.5 * acc_ref[...]

    row = pl.BlockSpec((tm, d), lambda i, s: (i, 0))
    vec = pl.BlockSpec((1, d), lambda i, s: (0, 0))
    wrow = pl.BlockSpec((1, f, d), lambda i, s: (s, 0, 0))
    hid = pl.BlockSpec((1, tm, f), lambda i, s: (s, i, 0))
    hid_sh = jax.ShapeDtypeStruct((ns, t, f), BF16)
    return pl.pallas_call(
        body, out_shape=(jax.ShapeDtypeStruct((t, d), F32), jax.ShapeDtypeStruct((t, d), BF16), hid_sh, hid_sh),
        grid=(t // tm, ns), in_specs=[row, vec, wrow, wrow, wrow], out_specs=(row, row, hid, hid),
        scratch_shapes=[pltpu.VMEM((tm, d), BF16), pltpu.VMEM((tm, d), F32)],
        compiler_params=_cparams(("parallel", "arbitrary")), name=name)(x, g, wg, wu, wd)


def _ffn_bwd_x_call(dxo, dxo_b, x, g, gate, up, wg, wu, wd, name):
    t, d = x.shape
    ns, f, _ = wg.shape
    tm = _tile(t, FFN_ROWS)

    def body(dxo_ref, dxb_ref, x_ref, g_ref, gate_ref, up_ref, wg_ref, wu_ref, wd_ref,
             dx_ref, dxob_ref, dgn_ref, dgate_ref, dup_ref, act_ref, dh_ref):
        i, s = pl.program_id(0), pl.program_id(1)
        dact = lax.dot_general(dxb_ref[...], wd_ref[0], NT_DIMS, preferred_element_type=F32) * 0.5
        gv = gate_ref[0].astype(F32)
        uv = up_ref[0].astype(F32)
        sg = _sigmoid(gv)
        silu = gv * sg
        act_ref[0] = (silu * uv).astype(BF16)
        dub = (dact * silu).astype(BF16)
        dgb = (dact * uv * sg * (1.0 + gv * (1.0 - sg))).astype(BF16)
        dup_ref[0] = dub
        dgate_ref[0] = dgb
        prod = (jnp.dot(dgb, wg_ref[0], preferred_element_type=F32)
                + jnp.dot(dub, wu_ref[0], preferred_element_type=F32))

        @pl.when(s == 0)
        def _():
            dh_ref[...] = prod

        @pl.when(s > 0)
        def _():
            dh_ref[...] += prod

        @pl.when(jnp.logical_and(i == 0, s == 0))
        def _():
            dgn_ref[...] = jnp.zeros_like(dgn_ref)

        @pl.when(s == ns - 1)
        def _():
            dx, dgs = _rms_bwd_rows(x_ref[...], g_ref[...], dh_ref[...])
            dx = dx + dxo_ref[...]
            dx_ref[...] = dx
            dxob_ref[...] = dx.astype(BF16)
            dgn_ref[...] += jnp.sum(dgs, axis=0, keepdims=True)

    row = pl.BlockSpec((tm, d), lambda i, s: (i, 0))
    vec = pl.BlockSpec((1, d), lambda i, s: (0, 0))
    wrow = pl.BlockSpec((1, f, d), lambda i, s: (s, 0, 0))
    hid = pl.BlockSpec((1, tm, f), lambda i, s: (s, i, 0))
    hid_sh = jax.ShapeDtypeStruct((ns, t, f), BF16)
    return pl.pallas_call(
        body,
        out_shape=(jax.ShapeDtypeStruct((t, d), F32), jax.ShapeDtypeStruct((t, d), BF16),
                   jax.ShapeDtypeStruct((1, d), F32), hid_sh, hid_sh, hid_sh),
        grid=(t // tm, ns), in_specs=[row, row, row, vec, hid, hid, wrow, wrow, wrow],
        out_specs=(row, row, vec, hid, hid, hid), scratch_shapes=[pltpu.VMEM((tm, d), F32)],
        compiler_params=_cparams(("arbitrary", "arbitrary")), name=name)(dxo, dxo_b, x, g, gate, up, wg, wu, wd)


def _ffn_bwd_w_call(h, dxo_b, dgate, dup, act, name):
    t, d = h.shape
    ns, _, f = dgate.shape
    tm = _tile(t, FFN_ROWS)
    nm = t // tm

    def body(h_ref, dxb_ref, dgate_ref, dup_ref, act_ref, dwg_ref, dwu_ref, dwd_ref, ag_ref, au_ref, ad_ref):
        i = pl.program_id(1)
        hv = h_ref[...]
        pg = lax.dot_general(dgate_ref[0], hv, TN_DIMS, preferred_element_type=F32)
        pu = lax.dot_general(dup_ref[0], hv, TN_DIMS, preferred_element_type=F32)
        pd = lax.dot_general(act_ref[0], dxb_ref[...], TN_DIMS, preferred_element_type=F32)

        @pl.when(i == 0)
        def _():
            ag_ref[...] = pg
            au_ref[...] = pu
            ad_ref[...] = pd

        @pl.when(i > 0)
        def _():
            ag_ref[...] += pg
            au_ref[...] += pu
            ad_ref[...] += pd

        @pl.when(i == nm - 1)
        def _():
            dwg_ref[0] = ag_ref[...].astype(BF16)
            dwu_ref[0] = au_ref[...].astype(BF16)
            dwd_ref[0] = (0.5 * ad_ref[...]).astype(BF16)

    row = pl.BlockSpec((tm, d), lambda s, i: (i, 0))
    hid = pl.BlockSpec((1, tm, f), lambda s, i: (s, i, 0))
    wrow = pl.BlockSpec((1, f, d), lambda s, i: (s, 0, 0))
    wsh = jax.ShapeDtypeStruct((ns, f, d), BF16)
    return pl.pallas_call(
        body, out_shape=(wsh, wsh, wsh),
        grid=(ns, nm), in_specs=[row, row, hid, hid, hid], out_specs=(wrow, wrow, wrow),
        scratch_shapes=[pltpu.VMEM((f, d), F32), pltpu.VMEM((f, d), F32), pltpu.VMEM((f, d), F32)],
        compiler_params=_cparams(("parallel", "arbitrary")), name=name)(h, dxo_b, dgate, dup, act)


def _loss_head(x, g, tgt, name):
    t, w = x.shape
    tr = _row_tile(t)

    def body(x_ref, g_ref, t_ref, loss_ref, dx_ref, dxb_ref, dg_ref):
        xv = x_ref[...]
        gv = g_ref[...]
        r = lax.rsqrt(jnp.mean(xv * xv, axis=-1, keepdims=True) + EPS)
        nrm = xv * r
        err = nrm * gv - t_ref[...]
        dout = err * (1.0 / w)
        dn = dout * gv
        dx = r * (dn - nrm * jnp.mean(dn * nrm, axis=-1, keepdims=True))
        dx_ref[...] = dx
        dxb_ref[...] = dx.astype(BF16)

        @pl.when(pl.program_id(0) == 0)
        def _():
            dg_ref[...] = jnp.zeros_like(dg_ref)
            loss_ref[...] = jnp.zeros_like(loss_ref)

        dg_ref[...] += jnp.sum(dout * nrm, axis=0, keepdims=True)
        part = jnp.sum(jnp.sum(err * err, axis=-1, keepdims=True) * (0.5 / w), axis=0, keepdims=True)
        loss_ref[...] += jnp.broadcast_to(part, loss_ref.shape)

    row = pl.BlockSpec((tr, w), lambda i: (i, 0))
    vec = pl.BlockSpec((1, w), lambda i: (0, 0))
    return pl.pallas_call(
        body, out_shape=(jax.ShapeDtypeStruct((1, LANES), F32), jax.ShapeDtypeStruct((t, w), F32),
                         jax.ShapeDtypeStruct((t, w), BF16), jax.ShapeDtypeStruct((1, w), F32)),
        grid=(t // tr,), in_specs=[row, vec, row],
        out_specs=(pl.BlockSpec((1, LANES), lambda i: (0, 0)), row, row, vec),
        compiler_params=_cparams(("arbitrary",)), name=name)(x, g, tgt)


def _attn_scores(q, k3, n, t, slope_ref):
    rows = GQ * QBLOCK
    s = lax.dot_general(q, k3, (((1,), (1,)), ((), ())), preferred_element_type=F32) * (HEAD_DIM ** -0.5)
    row = lax.broadcasted_iota(jnp.int32, (rows, 3 * QBLOCK), 0) & (QBLOCK - 1)
    col = lax.broadcasted_iota(jnp.int32, (rows, 3 * QBLOCK), 1)
    rel = jnp.abs(col - QBLOCK - row)
    key_pos = n * QBLOCK - QBLOCK + col
    valid = (rel <= WINDOW) & (key_pos >= 0) & (key_pos < t)
    return jnp.where(valid, s - slope_ref[0] * rel.astype(F32), NEG_INF)


Q_COL, K_COL, V_COL, U_COL = 0, ATTN_WIDTH // LANES, ATTN_WIDTH // LANES + 1, ATTN_WIDTH // LANES + 2


def _key_rows(ref, n, nb):
    prev, nxt = jnp.maximum(n - 1, 0), jnp.minimum(n + 1, nb - 1)
    blk = lambda b: ref[pl.ds(pl.multiple_of(b * QBLOCK, QBLOCK), QBLOCK), :]
    return jnp.concatenate([blk(prev), blk(n), blk(nxt)], axis=0)


def _head_tiles(x, kh, low):
    tiles = []
    for g in range(GQ):
        h = GQ * kh + g
        t128 = x[:, LANES * (h // 2):LANES * (h // 2 + 1)]
        t128 = jnp.where(low if h % 2 == 0 else jnp.logical_not(low), t128, 0.0)
        if h % 2 != kh:
            t128 = pltpu.roll(t128, HEAD_DIM, 1)
        tiles.append(t128)
    return jnp.concatenate(tiles, axis=0)


def _head_merge(per_kh, low):
    out = []
    for j in range(ATTN_HEADS // 2):
        pair = []
        for h in (2 * j, 2 * j + 1):
            kh, g = h // GQ, h % GQ
            t128 = per_kh[kh][g * QBLOCK:(g + 1) * QBLOCK, :]
            if h % 2 != kh:
                t128 = pltpu.roll(t128, HEAD_DIM, 1)
            pair.append(t128)
        out.append(jnp.where(low, pair[0], pair[1]))
    return jnp.concatenate(out, axis=1)


def _attn_fwd_proj(proj, sink_rows, slope_rows, name):
    t = proj.shape[0]
    nb = t // QBLOCK
    rows = GQ * QBLOCK

    def body(q_ref, k_ref, v_ref, sink_ref, slope_ref, o_ref, lse_ref):
        n = pl.program_id(0)
        low = lax.broadcasted_iota(jnp.int32, (QBLOCK, LANES), 1) < HEAD_DIM
        k3 = _key_rows(k_ref, n, nb).astype(BF16)
        v3 = _key_rows(v_ref, n, nb).astype(BF16)
        q = q_ref[...]
        outs = []
        for kh in range(KV_HEADS):
            qs = _head_tiles(q, kh, low).astype(BF16)
            s = _attn_scores(qs, k3, n, t, slope_ref.at[pl.ds(kh, 1)])
            sink = sink_ref[kh]
            mx = jnp.maximum(jnp.max(s, axis=-1, keepdims=True), sink)
            p = jnp.exp(s - mx)
            den = jnp.sum(p, axis=-1, keepdims=True) + jnp.exp(sink - mx)
            outs.append(jnp.dot(p.astype(BF16), v3, preferred_element_type=F32) / den)
            lse_ref[0, kh] = mx + jnp.log(den)
        o_ref[...] = _head_merge(outs, low)

    strip = lambda col: pl.BlockSpec((t, LANES), lambda n, col=col: (0, col))
    rowspec = pl.BlockSpec((KV_HEADS, rows, 1), lambda n: (0, 0, 0))
    return pl.pallas_call(
        body, out_shape=(jax.ShapeDtypeStruct((t, ATTN_WIDTH), F32), jax.ShapeDtypeStruct((nb, KV_HEADS, rows, 1), F32)),
        grid=(nb,), in_specs=[pl.BlockSpec((QBLOCK, ATTN_WIDTH), lambda n: (n, 0)), strip(K_COL), strip(V_COL),
                              rowspec, rowspec],
        out_specs=(pl.BlockSpec((QBLOCK, ATTN_WIDTH), lambda n: (n, 0)),
                   pl.BlockSpec((1, KV_HEADS, rows, 1), lambda n: (n, 0, 0, 0))),
        compiler_params=_cparams(("parallel",)), name=name)(proj, proj, proj, sink_rows, slope_rows)


def _attn_bwd_proj(proj, sink_rows, slope_rows, o, lse, do, name):
    t = proj.shape[0]
    nb = t // QBLOCK
    rows = GQ * QBLOCK
    scale = HEAD_DIM ** -0.5

    def body(q_ref, k_ref, v_ref, sink_ref, slope_ref, o_ref, lse_ref, do_ref, dq_ref, dk_ref, dv_ref, ds_ref):
        n = pl.program_id(0)

        @pl.when(n == 0)
        def _():
            dk_ref[...] = jnp.zeros_like(dk_ref)
            dv_ref[...] = jnp.zeros_like(dv_ref)
            ds_ref[...] = jnp.zeros_like(ds_ref)

        low = lax.broadcasted_iota(jnp.int32, (QBLOCK, LANES), 1) < HEAD_DIM
        k3 = _key_rows(k_ref, n, nb).astype(BF16)
        v3 = _key_rows(v_ref, n, nb).astype(BF16)
        q, dov = q_ref[...], do_ref[...]
        dod = dov * o_ref[...]
        dqs = []
        dk3 = jnp.zeros((3 * QBLOCK, LANES), F32)
        dv3 = jnp.zeros((3 * QBLOCK, LANES), F32)
        for kh in range(KV_HEADS):
            qs = _head_tiles(q, kh, low).astype(BF16)
            dos = _head_tiles(dov, kh, low).astype(BF16)
            delta = jnp.sum(_head_tiles(dod, kh, low), axis=-1, keepdims=True)
            lse_kh = lse_ref[0, kh]
            s = _attn_scores(qs, k3, n, t, slope_ref.at[pl.ds(kh, 1)])
            p = jnp.exp(s - lse_kh)
            dp = lax.dot_general(dos, v3, NT_DIMS, preferred_element_type=F32)
            dsb = (p * (dp - delta)).astype(BF16)
            dqs.append(jnp.dot(dsb, k3, preferred_element_type=F32) * scale)
            dk3 = dk3 + lax.dot_general(dsb, qs, TN_DIMS, preferred_element_type=F32) * scale
            dv3 = dv3 + lax.dot_general(p.astype(BF16), dos, TN_DIMS, preferred_element_type=F32)
            dsink_rows = -jnp.exp(sink_ref[kh] - lse_kh) * delta
            ds_ref[kh] += jnp.sum(dsink_rows.reshape(GQ, QBLOCK, 1), axis=1)
        dq_ref[...] = _head_merge(dqs, low)
        prev, nxt = jnp.maximum(n - 1, 0), jnp.minimum(n + 1, nb - 1)
        for j, b in enumerate((prev, n, nxt)):
            blk = pl.ds(pl.multiple_of(b * QBLOCK, QBLOCK), QBLOCK)
            dk_ref[blk, :] += dk3[j * QBLOCK:(j + 1) * QBLOCK, :]
            dv_ref[blk, :] += dv3[j * QBLOCK:(j + 1) * QBLOCK, :]

    strip = lambda col: pl.BlockSpec((t, LANES), lambda n, col=col: (0, col))
    rowspec = pl.BlockSpec((KV_HEADS, rows, 1), lambda n: (0, 0, 0))
    qspec = pl.BlockSpec((QBLOCK, ATTN_WIDTH), lambda n: (n, 0))
    kv_out = pl.BlockSpec((t, LANES), lambda n: (0, 0))
    return pl.pallas_call(
        body,
        out_shape=(jax.ShapeDtypeStruct((t, ATTN_WIDTH), F32), jax.ShapeDtypeStruct((t, LANES), F32),
                   jax.ShapeDtypeStruct((t, LANES), F32), jax.ShapeDtypeStruct((KV_HEADS, GQ, 1), F32)),
        grid=(nb,),
        in_specs=[qspec, strip(K_COL), strip(V_COL), rowspec, rowspec, qspec,
                  pl.BlockSpec((1, KV_HEADS, rows, 1), lambda n: (n, 0, 0, 0)), qspec],
        out_specs=(qspec, kv_out, kv_out, pl.BlockSpec((KV_HEADS, GQ, 1), lambda n: (0, 0, 0))),
        compiler_params=_cparams(("arbitrary",)), name=name)(proj, proj, proj, sink_rows, slope_rows, o, lse, do)


def _scan_tables(a_re, a_im, reverse):
    pw = [(a_re, a_im)]
    for _ in range(SUBLANES - 1):
        pr, pi = pw[-1]
        pw.append((pr * a_re - pi * a_im, pr * a_im + pi * a_re))
    rows = np.arange(SUBLANES)
    tabs = []
    for d in (1, 2, 4):
        mask = (rows <= SUBLANES - 1 - d) if reverse else (rows >= d)
        m = jnp.asarray(mask, F32)[:, None]
        tabs += [m * pw[d - 1][0][None, :], m * pw[d - 1][1][None, :]]
    order = (SUBLANES - 1 - rows) if reverse else rows
    tabs += [jnp.stack([pw[j][0] for j in order]), jnp.stack([pw[j][1] for j in order])]
    tab = jnp.stack(tabs)
    return tab.reshape(8, SUBLANES, N_STRIPS, STRIP_ST).transpose(2, 0, 1, 3)


def _scan(v, mi_re, mi_im, tab, mo_re, mo_im, reverse, name):
    t = v.shape[0]
    tc = _tile(t, 256)
    nc = t // tc
    nblk = tc // SUBLANES

    def body(v_ref, mir_ref, mii_ref, tab_ref, mor_ref, moi_ref, y_ref, xr_ref, xi_ref, carry_ref):
        @pl.when(pl.program_id(1) == 0)
        def _():
            carry_ref[...] = jnp.zeros_like(carry_ref)

        vb = v_ref[...].astype(BF16)
        xr_ref[...] = jnp.dot(vb, mir_ref[0], preferred_element_type=F32)
        xi_ref[...] = jnp.dot(vb, mii_ref[0], preferred_element_type=F32)

        def blk(i, carry):
            cr, ci = carry
            b = (nblk - 1 - i) if reverse else i
            r0 = pl.multiple_of(b * SUBLANES, SUBLANES)
            xr = xr_ref[pl.ds(r0, SUBLANES), :]
            xi = xi_ref[pl.ds(r0, SUBLANES), :]
            for j, d in enumerate((1, 2, 4)):
                tr_, ti_ = tab_ref[0, 2 * j], tab_ref[0, 2 * j + 1]
                sh = (SUBLANES - d) if reverse else d
                sr = pltpu.roll(xr, sh, 0)
                si = pltpu.roll(xi, sh, 0)
                xr, xi = xr + tr_ * sr - ti_ * si, xi + tr_ * si + ti_ * sr
            pr, pi = tab_ref[0, 6], tab_ref[0, 7]
            xr, xi = xr + pr * cr - pi * ci, xi + pr * ci + pi * cr
            xr_ref[pl.ds(r0, SUBLANES), :] = xr
            xi_ref[pl.ds(r0, SUBLANES), :] = xi
            edge = 0 if reverse else SUBLANES - 1
            return (jnp.broadcast_to(xr[edge:edge + 1, :], xr.shape),
                    jnp.broadcast_to(xi[edge:edge + 1, :], xi.shape))

        cr, ci = lax.fori_loop(0, nblk, blk, (carry_ref[0], carry_ref[1]))
        carry_ref[0] = cr
        carry_ref[1] = ci
        y_ref[...] = (jnp.dot(xr_ref[...].astype(BF16), mor_ref[0], preferred_element_type=F32)
                      + jnp.dot(xi_ref[...].astype(BF16), moi_ref[0], preferred_element_type=F32))

    tmap = (lambda s, c: (nc - 1 - c, s)) if reverse else (lambda s, c: (c, s))
    col0 = v.shape[1] // STRIP_IN - N_STRIPS
    vmap = lambda s, c: (tmap(s, c)[0], s + col0)
    smap3 = lambda s, c: (s, 0, 0)
    return pl.pallas_call(
        body,
        out_shape=(jax.ShapeDtypeStruct((t, SSM_WIDTH), F32),
                   jax.ShapeDtypeStruct((t, N_STRIPS * STRIP_ST), F32),
                   jax.ShapeDtypeStruct((t, N_STRIPS * STRIP_ST), F32)),
        grid=(N_STRIPS, nc),
        in_specs=[pl.BlockSpec((tc, STRIP_IN), vmap),
                  pl.BlockSpec((1, STRIP_IN, STRIP_ST), smap3), pl.BlockSpec((1, STRIP_IN, STRIP_ST), smap3),
                  pl.BlockSpec((1, 8, SUBLANES, STRIP_ST), lambda s, c: (s, 0, 0, 0)),
                  pl.BlockSpec((1, STRIP_ST, STRIP_IN), smap3), pl.BlockSpec((1, STRIP_ST, STRIP_IN), smap3)],
        out_specs=(pl.BlockSpec((tc, STRIP_IN), tmap), pl.BlockSpec((tc, STRIP_ST), tmap),
                   pl.BlockSpec((tc, STRIP_ST), tmap)),
        scratch_shapes=[pltpu.VMEM((2, SUBLANES, STRIP_ST), F32)],
        compiler_params=_cparams(("parallel", "arbitrary")), name=name)(v, mi_re, mi_im, tab, mo_re, mo_im)


def _scan_param_grads(v, dy, xr, xi, lr, li, reverse, name):
    t = v.shape[0]
    tc = _tile(t, 256)
    nc = t // tc
    hb = tc // SUBLANES

    def body(v_ref, dy_ref, xr_ref, xi_ref, lr_ref, li_ref, hr_ref, hi_ref,
             dmir_ref, dmii_ref, dmor_ref, dmoi_ref, da_ref):
        c = pl.program_id(1)

        @pl.when(c == 0)
        def _():
            for r in (dmir_ref, dmii_ref, dmor_ref, dmoi_ref, da_ref):
                r[...] = jnp.zeros_like(r)

        xrv, xiv, lrv, liv = xr_ref[...], xi_ref[...], lr_ref[...], li_ref[...]
        row = lax.broadcasted_iota(jnp.int32, xrv.shape, 0)
        if reverse:
            live = (c < nc - 1).astype(F32)
            edge_r, edge_i = hr_ref[0:1, :] * live, hi_ref[0:1, :] * live
            xpr = jnp.where(row == tc - 1, edge_r, pltpu.roll(xrv, tc - 1, 0))
            xpi = jnp.where(row == tc - 1, edge_i, pltpu.roll(xiv, tc - 1, 0))
        else:
            live = (c > 0).astype(F32)
            edge_r, edge_i = hr_ref[SUBLANES - 1:SUBLANES, :] * live, hi_ref[SUBLANES - 1:SUBLANES, :] * live
            xpr = jnp.where(row == 0, edge_r, pltpu.roll(xrv, 1, 0))
            xpi = jnp.where(row == 0, edge_i, pltpu.roll(xiv, 1, 0))
        da_ref[0, 0:1, :] += jnp.sum(xpr * lrv + xpi * liv, axis=0, keepdims=True)
        da_ref[0, 1:2, :] += jnp.sum(xpr * liv - xpi * lrv, axis=0, keepdims=True)
        tdims = (((0,), (0,)), ((), ()))
        vb, dyb = v_ref[...].astype(BF16), dy_ref[...].astype(BF16)
        dmir_ref[0] += lax.dot_general(vb, lrv.astype(BF16), tdims, preferred_element_type=F32)
        dmii_ref[0] += lax.dot_general(vb, liv.astype(BF16), tdims, preferred_element_type=F32)
        dmor_ref[0] += lax.dot_general(xrv.astype(BF16), dyb, tdims, preferred_element_type=F32)
        dmoi_ref[0] += lax.dot_general(xiv.astype(BF16), dyb, tdims, preferred_element_type=F32)

    tmap = lambda s, c: (c, s)
    if reverse:
        hmap = lambda s, c: (jnp.minimum((c + 1) * hb, t // SUBLANES - 1), s)
    else:
        hmap = lambda s, c: (jnp.maximum(c * hb - 1, 0), s)
    narrow = pl.BlockSpec((tc, STRIP_IN), tmap)
    col0 = v.shape[1] // STRIP_IN - N_STRIPS
    vspec = pl.BlockSpec((tc, STRIP_IN), lambda s, c: (c, s + col0))
    wide = pl.BlockSpec((tc, STRIP_ST), tmap)
    halo = pl.BlockSpec((SUBLANES, STRIP_ST), hmap)
    smap3 = lambda s, c: (s, 0, 0)
    return pl.pallas_call(
        body,
        out_shape=(jax.ShapeDtypeStruct((N_STRIPS, STRIP_IN, STRIP_ST), F32),
                   jax.ShapeDtypeStruct((N_STRIPS, STRIP_IN, STRIP_ST), F32),
                   jax.ShapeDtypeStruct((N_STRIPS, STRIP_ST, STRIP_IN), F32),
                   jax.ShapeDtypeStruct((N_STRIPS, STRIP_ST, STRIP_IN), F32),
                   jax.ShapeDtypeStruct((N_STRIPS, SUBLANES, STRIP_ST), F32)),
        grid=(N_STRIPS, nc),
        in_specs=[vspec, narrow, wide, wide, wide, wide, halo, halo],
        out_specs=(pl.BlockSpec((1, STRIP_IN, STRIP_ST), smap3), pl.BlockSpec((1, STRIP_IN, STRIP_ST), smap3),
                   pl.BlockSpec((1, STRIP_ST, STRIP_IN), smap3), pl.BlockSpec((1, STRIP_ST, STRIP_IN), smap3),
                   pl.BlockSpec((1, SUBLANES, STRIP_ST), smap3)),
        compiler_params=_cparams(("parallel", "arbitrary")), name=name)(v, dy, xr, xi, lr, li, xr, xi)


def _ssm_prep(lam_re, lam_im, log_dt, bt_re, bt_im, c_re, c_im):
    lr = jnp.minimum(lam_re, LAMBDA_RE_MAX)
    li = lam_im
    dt = jnp.exp(log_dt)[:, None]
    mag = jnp.exp(lr * dt)
    a_re = mag * jnp.cos(li * dt)
    a_im = mag * jnp.sin(li * dt)
    den = lr * lr + li * li
    coef_re = ((a_re - 1.0) * lr + a_im * li) / den
    coef_im = (a_im * lr - (a_re - 1.0) * li) / den
    bb_re = coef_re[:, None, :] * bt_re - coef_im[:, None, :] * bt_im
    bb_im = coef_re[:, None, :] * bt_im + coef_im[:, None, :] * bt_re
    eye = jnp.eye(SSM_GROUPS // N_STRIPS, dtype=F32)

    def strips(m):
        g, a, b = m.shape
        m4 = m.reshape(N_STRIPS, g // N_STRIPS, a, b)
        return jnp.einsum('sgab,gk->sgakb', m4, eye).reshape(N_STRIPS, g // N_STRIPS * a, g // N_STRIPS * b)

    mi_re = strips(bb_re)
    mi_im = strips(bb_im)
    mo_re = strips(jnp.swapaxes(c_re, 1, 2))
    mo_im = strips(-jnp.swapaxes(c_im, 1, 2))
    return a_re.reshape(-1), a_im.reshape(-1), mi_re, mi_im, mo_re, mo_im


def _gelu(x):
    c = math.sqrt(2.0 / math.pi)
    return 0.5 * x * (1.0 + jnp.tanh(c * (x + 0.044715 * x * x * x)))


def _gelu_grad(x):
    c = math.sqrt(2.0 / math.pi)
    th = jnp.tanh(c * (x + 0.044715 * x * x * x))
    return 0.5 * (1.0 + th) + 0.5 * x * (1.0 - th * th) * c * (1.0 + 3.0 * 0.044715 * x * x)


def _last_cols_specs(u, w, tr):
    half = w // 2
    first = (u.shape[1] - w) // half
    assert first * half == u.shape[1] - w
    return [pl.BlockSpec((tr, half), lambda i, k=k: (i, first + k)) for k in range(2)]


def _ssm_post_fwd(u, yf, yb, d, wglu, bglu, name):
    t, w = yf.shape
    tr = _row_tile(t)

    def body(ua_ref, ub_ref, yf_ref, yb_ref, d_ref, w_ref, b_ref, s_ref, y0_ref, z_ref):
        uv = jnp.concatenate([ua_ref[...], ub_ref[...]], axis=1)
        y0 = d_ref[...] * uv + yf_ref[...] + yb_ref[...]
        yg = _gelu(y0)
        z = jnp.dot(yg.astype(BF16), w_ref[...], preferred_element_type=F32) + b_ref[...]
        s_ref[...] = yg * _sigmoid(z)
        y0_ref[...] = y0
        z_ref[...] = z

    row = pl.BlockSpec((tr, w), lambda i: (i, 0))
    vec = pl.BlockSpec((1, w), lambda i: (0, 0))
    mat = pl.BlockSpec((w, w), lambda i: (0, 0))
    sh = jax.ShapeDtypeStruct((t, w), F32)
    return pl.pallas_call(body, out_shape=(sh, sh, sh), grid=(t // tr,),
                          in_specs=[*_last_cols_specs(u, w, tr), row, row, vec, mat, vec], out_specs=(row, row, row),
                          compiler_params=_cparams(("parallel",)), name=name)(u, u, yf, yb, d, wglu, bglu)


def _ssm_post_bwd(ds, y0, z, u, d, wglu, name):
    t, w = ds.shape
    tr = _row_tile(t)

    def body(ds_ref, y0_ref, z_ref, ua_ref, ub_ref, d_ref, w_ref, dy0_ref, dw_ref, db_ref, dd_ref):
        @pl.when(pl.program_id(0) == 0)
        def _():
            dw_ref[...] = jnp.zeros_like(dw_ref)
            db_ref[...] = jnp.zeros_like(db_ref)
            dd_ref[...] = jnp.zeros_like(dd_ref)

        y0 = y0_ref[...]
        yg = _gelu(y0)
        sg = _sigmoid(z_ref[...])
        dsv = ds_ref[...]
        dz = dsv * yg * sg * (1.0 - sg)
        dzb = dz.astype(BF16)
        dyg = dsv * sg + lax.dot_general(dzb, w_ref[...], (((1,), (1,)), ((), ())), preferred_element_type=F32)
        dy0 = dyg * _gelu_grad(y0)
        dy0_ref[...] = dy0
        dw_ref[...] += lax.dot_general(yg.astype(BF16), dzb, (((0,), (0,)), ((), ())), preferred_element_type=F32)
        db_ref[...] += jnp.sum(dz, axis=0, keepdims=True)
        uv = jnp.concatenate([ua_ref[...], ub_ref[...]], axis=1)
        dd_ref[...] += jnp.sum(dy0 * uv, axis=0, keepdims=True)

    row = pl.BlockSpec((tr, w), lambda i: (i, 0))
    vec = pl.BlockSpec((1, w), lambda i: (0, 0))
    mat = pl.BlockSpec((w, w), lambda i: (0, 0))
    return pl.pallas_call(
        body, out_shape=(jax.ShapeDtypeStruct((t, w), F32), jax.ShapeDtypeStruct((w, w), F32),
                         jax.ShapeDtypeStruct((1, w), F32), jax.ShapeDtypeStruct((1, w), F32)),
        grid=(t // tr,), in_specs=[row, row, row, *_last_cols_specs(u, w, tr), vec, mat],
        out_specs=(row, mat, vec, vec),
        compiler_params=_cparams(("arbitrary",)), name=name)(ds, y0, z, u, u, d, wglu)


def _du_combine(dy0, d, du_f, du_b, name):
    t, w = dy0.shape
    tr = _row_tile(t)

    def body(dy_ref, d_ref, a_ref, b_ref, o_ref):
        o_ref[...] = d_ref[...] * dy_ref[...] + a_ref[...] + b_ref[...]

    row = pl.BlockSpec((tr, w), lambda i: (i, 0))
    vec = pl.BlockSpec((1, w), lambda i: (0, 0))
    return pl.pallas_call(body, out_shape=jax.ShapeDtypeStruct((t, w), F32), grid=(t // tr,),
                          in_specs=[row, vec, row, row], out_specs=row, compiler_params=_cparams(("parallel",)),
                          name=name)(dy0, d, du_f, du_b)


def _ffn_fwd(x, g, wg, wu, wd, tag):
    xo, h, gate, up = _ffn_fwd_call(x, g, wg, wu, wd, f"{tag}_fwd")
    return xo, (h, gate, up)


def _ffn_bwd(dxo, dxo_b, x, g, wg, wu, wd, saved, tag):
    h, gate, up = saved
    dx, dx_b, dg, dgate, dup, act = _ffn_bwd_x_call(dxo, dxo_b, x, g, gate, up, wg, wu, wd, f"{tag}_bwd_x")
    dwg, dwu, dwd = _ffn_bwd_w_call(h, dxo_b, dgate, dup, act, f"{tag}_bwd_w")
    return dx, dx_b, dg, dwg, dwu, dwd


def _local_step(x, tgt, w, get_weights, put_grads, reduce_wide):
    t = x.shape[0]
    row = lambda a: a.reshape(1, -1)
    grads = {}

    w = dict(w)

    ssm_names = ['ssm_lambda_re', 'ssm_lambda_im', 'ssm_log_dt', 'ssm_b_re', 'ssm_b_im', 'ssm_c_re', 'ssm_c_im']
    tr3 = lambda m: jnp.swapaxes(m, 1, 2)
    fwd_ops, adj_ops, vjps = [], [], []
    for direction in range(2):
        rev = direction == 1
        prep, vjp = jax.vjp(_ssm_prep, *[w[n][direction] for n in ssm_names])
        a_re, a_im = prep[0], prep[1]
        mi_re, mi_im, mo_re, mo_im = (m.astype(BF16) for m in prep[2:])
        fwd_ops.append((mi_re, mi_im, _scan_tables(a_re, a_im, rev), mo_re, mo_im))
        adj_ops.append((tr3(mo_re), tr3(mo_im), _scan_tables(a_re, -a_im, not rev), tr3(mi_re), tr3(mi_im)))
        vjps.append(vjp)
    sink_rows = jnp.repeat(w['attn_sinks'].reshape(KV_HEADS, GQ), QBLOCK, axis=1)[..., None]
    slopes = jnp.asarray(2.0 ** (-8.0 * (np.arange(ATTN_HEADS) + 1) / ATTN_HEADS), F32)
    slope_rows = jnp.repeat(slopes.reshape(KV_HEADS, GQ), QBLOCK, axis=1)[..., None]
    prepared = sum(jnp.sum(op[:1, :1].astype(F32)) for ops in fwd_ops + adj_ops for op in ops) + sink_rows[0, 0, 0]

    w.update(get_weights('ffn1', prepared.reshape(1, 1)))
    x1, ffn1_saved = _ffn_fwd(x, w['norm_ffn1'], w['ffn1_w_gate'], w['ffn1_w_up'], w['ffn1_w_down'], "ffn1")
    w.update(get_weights('mix', x1))

    h2 = _rms_fwd(x1, w['norm_mix'], "mix_norm")
    proj = _mm(h2, w['w_in'], tb=True, name="in_proj")[0]
    u = proj

    attn, lse = _attn_fwd_proj(proj, sink_rows, slope_rows, "attn_fwd")

    ys, states = [], []
    for direction in range(2):
        y, xr, xi = _scan(u, *fwd_ops[direction], direction == 1, f"s5_fwd{direction}")
        ys.append(y)
        states.append((xr, xi))
    d_row = row(w['ssm_d'])
    s, y0, z = _ssm_post_fwd(u, ys[0], ys[1], d_row, w['ssm_glu_w'], row(w['ssm_glu_b']), "ssm_post")

    ma = _rms_fwd(attn, row(w['attn_out_norm']), "attn_out_norm")
    ms = _rms_fwd(s, row(w['ssm_out_norm']), "ssm_out_norm")
    mixed = jnp.concatenate([ma, ms], axis=-1)
    x2 = _mm(mixed, w['w_out'], res=x1, reduce_s=True, name="out_proj")

    w.update(get_weights('ffn2', x2))
    x3, ffn2_saved = _ffn_fwd(x2, w['norm_ffn2'], w['ffn2_w_gate'], w['ffn2_w_up'], w['ffn2_w_down'], "ffn2")

    loss_row, dx3, dx3_b, dgf = _loss_head(x3, row(w['final_norm']), tgt, "loss_head")
    loss = loss_row[0, 0]
    grads['final_norm'] = dgf.reshape(w['final_norm'].shape)

    dx2, dx2_b, dg, dwg, dwu, dwd = _ffn_bwd(dx3, dx3_b, x2, w['norm_ffn2'], w['ffn2_w_gate'], w['ffn2_w_up'],
                                             w['ffn2_w_down'], ffn2_saved, "ffn2")
    grads['norm_ffn2'] = dg
    sent = put_grads('ffn2', dict(ffn2_w_gate=dwg, ffn2_w_up=dwu, ffn2_w_down=dwd))

    dmixed = _mm(dx2_b, w['w_out'], tb=True, reduce_s=True, after=sent, name="out_proj_dx")
    dw_out = _mm(mixed, dx2_b, ta=True, out_dtype=BF16, name="out_proj_dw")[0]
    dattn, _, dga = _rms_bwd(attn, row(w['attn_out_norm']), dmixed[:, :ATTN_WIDTH], None, "attn_out_dnorm")
    ds, _, dgs = _rms_bwd(s, row(w['ssm_out_norm']), dmixed[:, ATTN_WIDTH:], None, "ssm_out_dnorm")
    grads.update(attn_out_norm=dga, ssm_out_norm=dgs)

    dy0, dwglu, dbglu, dd = _ssm_post_bwd(ds, y0, z, u, d_row, w['ssm_glu_w'], "ssm_post_bwd")
    grads['ssm_glu_b'] = dbglu
    grads['ssm_d'] = dd.reshape(w['ssm_d'].shape)
    dparams, du_dirs = [], []
    for direction in range(2):
        rev = direction == 1
        du_dir, lr, li = _scan(dy0, *adj_ops[direction], not rev, f"s5_adj{direction}")
        du_dirs.append(du_dir)
        xr, xi = states[direction]
        dmir, dmii, dmor, dmoi, da = _scan_param_grads(u, dy0, xr, xi, lr, li, rev, f"s5_pgrad{direction}")
        da_re = da[:, 0, :].reshape(-1)
        da_im = da[:, 1, :].reshape(-1)
        dparams.append(vjps[direction]((da_re, da_im, dmir, dmii, dmor, dmoi)))
    du = _du_combine(dy0, d_row, du_dirs[0], du_dirs[1], "ssm_du")
    for i, n in enumerate(ssm_names):
        grads[n] = jnp.stack([dparams[0][i], dparams[1][i]])
    wide_sum = reduce_wide(grads)

    dq, dk, dv, dsink = _attn_bwd_proj(proj, sink_rows, slope_rows, attn, lse, dattn, "attn_bwd")
    grads['attn_sinks'] = dsink.reshape(w['attn_sinks'].shape)
    dproj = jnp.concatenate([dq, dk, dv, du], axis=-1).astype(BF16)

    dw_in = _mm(dproj, h2, ta=True, out_dtype=BF16, after=wide_sum, name="in_proj_dw")[0]
    sent = put_grads('mix', dict(w_in=dw_in, ssm_glu_w=dwglu, w_out=dw_out))
    dh2 = _mm(dproj, w['w_in'], reduce_s=True, after=sent, name="in_proj_dx")
    dx1, dx1_b, dgm = _rms_bwd(x1, w['norm_mix'], dh2, dx2, "mix_dnorm")
    grads['norm_mix'] = dgm

    dx0, _, dg, dwg, dwu, dwd = _ffn_bwd(dx1, dx1_b, x, w['norm_ffn1'], w['ffn1_w_gate'], w['ffn1_w_up'],
                                         w['ffn1_w_down'], ffn1_saved, "ffn1")
    grads['norm_ffn1'] = dg
    put_grads('ffn1', dict(ffn1_w_gate=dwg, ffn1_w_up=dwu, ffn1_w_down=dwd))
    return loss, dx0, grads, wide_sum


HBM_SPEC = pl.BlockSpec(memory_space=pl.ANY)


def _chip_peers(x, y):
    return [(1 - x, y), (x, 1 - y), (1 - x, 1 - y)]


HBM_ONLY = pl.BlockSpec(memory_space=pltpu.HBM)
SEM_SPEC = pl.BlockSpec(memory_space=pltpu.SEMAPHORE)
EFFECT = pltpu.SideEffectType.DATAFLOW_SIDE_EFFECTING


def _place_own(src, slot, name):
    r, c = src.shape
    tr = r // 2

    def body(slot_ref, s_ref, o_ref):
        o_ref[0] = s_ref[...]

    return pl.pallas_call(
        body, out_shape=jax.ShapeDtypeStruct((N_CHIPS, r, c), src.dtype),
        grid_spec=pltpu.PrefetchScalarGridSpec(
            num_scalar_prefetch=1, grid=(2,), in_specs=[pl.BlockSpec((tr, c), lambda i, s: (i, 0))],
            out_specs=pl.BlockSpec((1, tr, c), lambda i, s: (s[0], i, 0))),
        compiler_params=_cparams(("parallel",)), name=name)(slot, src)


def _chip_copies(srcs, lands, send_sems, recv_sems, scatter, landed):
    x, y, c = lax.axis_index("x"), lax.axis_index("y"), lax.axis_index("c")
    me = 2 * x + y
    out = []
    for i in range(len(srcs)):
        for j, (px, py) in enumerate(_chip_peers(x, y)):
            p = 2 * px + py
            slot = p if landed else me
            if scatter:
                src, dst = srcs[i].at[p], lands[i].at[slot]
            else:
                rows = _core_half(srcs[i].shape[0], c)
                src, dst = srcs[i].at[rows], lands[i].at[slot, rows]
            out.append(pltpu.make_async_remote_copy(src, dst, send_sems.at[3 * i + j], recv_sems.at[3 * i + j],
                                                    device_id=(px, py, c), device_id_type=MESH))
    return out


def _core_half(nrows, c):
    half = nrows // 2
    return pl.ds(pl.multiple_of(c * half, 16), half)


def _sibling_forward(lands, name):
    n = len(lands)

    def body(*refs):
        bufs = refs[n:2 * n]
        send_sems, recv_sems = refs[2 * n:]
        x, y, c = lax.axis_index("x"), lax.axis_index("y"), lax.axis_index("c")
        mine = [_core_half(b.shape[1], c) for b in bufs]
        theirs = [_core_half(b.shape[1], 1 - c) for b in bufs]
        chips = [2 * px + py for px, py in _chip_peers(x, y)]
        cps = [pltpu.make_async_remote_copy(bufs[i].at[p, mine[i]], bufs[i].at[p, mine[i]], send_sems.at[3 * i + j],
                                            recv_sems.at[3 * i + j], device_id=(x, y, 1 - c), device_id_type=MESH)
               for i in range(n) for j, p in enumerate(chips)]
        for cp in cps:
            cp.start()
        for i in range(n):
            for j, p in enumerate(chips):
                pltpu.make_async_remote_copy(bufs[i].at[p, mine[i]], bufs[i].at[p, theirs[i]], send_sems.at[3 * i + j],
                                             recv_sems.at[3 * i + j], device_id=(x, y, 1 - c),
                                             device_id_type=MESH).wait()

    return pl.pallas_call(
        body, out_shape=[jax.ShapeDtypeStruct(a.shape, a.dtype) for a in lands],
        in_specs=[HBM_SPEC] * n, out_specs=[HBM_SPEC] * n, input_output_aliases={k: k for k in range(n)},
        scratch_shapes=[pltpu.SemaphoreType.DMA((3 * n,)), pltpu.SemaphoreType.DMA((3 * n,))],
        name=name)(*lands)


def _exchange_start(groups, scatter, name, after=None):
    sizes = [len(srcs) for srcs, _ in groups]
    flat_src = [a for srcs, _ in groups for a in srcs]
    flat_land = [a for _, lands in groups for a in lands]
    n = len(flat_src)
    ng = len(groups)

    def body(*refs):
        src_refs, land_refs = refs[:n], refs[n:2 * n]
        n_in = 2 * n + (after is not None)
        sems = refs[n_in:n_in + 2 * ng]
        token_ref = refs[-1]
        off = 0
        for gi, sz in enumerate(sizes):
            for cp in _chip_copies(src_refs[off:off + sz], land_refs[off:off + sz], sems[2 * gi], sems[2 * gi + 1],
                                   scatter, landed=False):
                cp.start()
            off += sz
        token_ref[...] = jnp.zeros_like(token_ref)

    sem_shapes = []
    for sz in sizes:
        sem_shapes += [pltpu.SemaphoreType.DMA((3 * sz,)), pltpu.SemaphoreType.DMA((3 * sz,))]
    hbm = lambda a: pltpu.HBM(a.shape, a.dtype)
    res = pl.pallas_call(
        body, name=name,
        out_shape=(tuple(sem_shapes) + tuple(hbm(a) for a in flat_src) + tuple(hbm(a) for a in flat_land)
                   + (jax.ShapeDtypeStruct((SUBLANES, LANES), F32),)),
        in_specs=[HBM_ONLY] * (2 * n) + [HBM_SPEC] * (after is not None),
        out_specs=tuple([SEM_SPEC] * (2 * ng) + [HBM_ONLY] * (2 * n) + [pl.BlockSpec(memory_space=pltpu.VMEM)]),
        input_output_aliases={k: 2 * ng + k for k in range(2 * n)},
        compiler_params=pltpu.CompilerParams(has_side_effects=EFFECT),
    )(*[pltpu.with_memory_space_constraint(a, pltpu.HBM) for a in flat_src + flat_land],
      *([after] if after is not None else []))
    sems, thru_src, thru_land = res[:2 * ng], res[2 * ng:2 * ng + n], res[2 * ng + n:2 * ng + 2 * n]
    out, off = [], 0
    for gi, sz in enumerate(sizes):
        out.append((sems[2 * gi], sems[2 * gi + 1], list(thru_src[off:off + sz]), list(thru_land[off:off + sz])))
        off += sz
    return out, res[-1]


def _exchange_wait(started, after, scatter, name):
    send_sems, recv_sems, srcs, lands = started
    n = len(srcs)

    def body(*refs):
        src_refs, land_refs = refs[:n], refs[n:2 * n]
        send_ref, recv_ref = refs[2 * n], refs[2 * n + 1]
        for cp in _chip_copies(src_refs, land_refs, send_ref, recv_ref, scatter, landed=True):
            cp.wait_send()
            cp.wait_recv()

    hbm = lambda a: pltpu.HBM(a.shape, a.dtype)
    res = pl.pallas_call(
        body, name=name, out_shape=tuple(hbm(a) for a in srcs) + tuple(hbm(a) for a in lands),
        in_specs=[HBM_ONLY] * (2 * n) + [SEM_SPEC, SEM_SPEC, HBM_SPEC], out_specs=tuple([HBM_ONLY] * (2 * n)),
        input_output_aliases={k: k for k in range(2 * n)},
        compiler_params=pltpu.CompilerParams(has_side_effects=EFFECT),
    )(*srcs, *lands, send_sems, recv_sems, after)
    return list(res[:n]), list(res[n:])


def _half_swap(parts, name):
    n = len(parts)

    def body(*refs):
        ins, outs = refs[:n], refs[n:2 * n]
        send_sems, recv_sems = refs[2 * n:]
        x, y, c = lax.axis_index("x"), lax.axis_index("y"), lax.axis_index("c")
        cps = [pltpu.make_async_remote_copy(ins[i].at[k, _core_half(ins[i].shape[1], 1 - c)], outs[i].at[k],
                                            send_sems.at[N_CHIPS * i + k], recv_sems.at[N_CHIPS * i + k],
                                            device_id=(x, y, 1 - c), device_id_type=MESH)
               for i in range(n) for k in range(N_CHIPS)]
        for cp in cps:
            cp.start()
        for cp in cps:
            cp.wait()

    return pl.pallas_call(
        body, out_shape=[jax.ShapeDtypeStruct((N_CHIPS, p.shape[1] // 2, p.shape[2]), p.dtype) for p in parts],
        in_specs=[HBM_SPEC] * n, out_specs=[HBM_SPEC] * n,
        scratch_shapes=[pltpu.SemaphoreType.DMA((N_CHIPS * n,)), pltpu.SemaphoreType.DMA((N_CHIPS * n,))],
        name=name)(*parts)


def _half_add(parts, sib, slots, name):
    na = len(parts)
    _, r, c = parts[0].shape
    hr = r // 2
    tr = _row_tile(hr)
    nt = hr // tr

    def body(slot_ref, *refs):
        for a in range(na):
            refs[2 * na + a][...] = (refs[2 * a][...].astype(F32) + refs[2 * a + 1][...].astype(F32)).astype(BF16)

    mine = pl.BlockSpec((1, tr, c), lambda k, i, s: (k, i + s[4] * nt, 0))
    half = pl.BlockSpec((1, tr, c), lambda k, i, s: (k, i, 0))
    args = [a for p, sb in zip(parts, sib) for a in (p, sb)]
    return pl.pallas_call(
        body, out_shape=[jax.ShapeDtypeStruct((N_CHIPS, hr, c), BF16)] * na,
        grid_spec=pltpu.PrefetchScalarGridSpec(
            num_scalar_prefetch=1, grid=(N_CHIPS, nt), in_specs=[mine, half] * na, out_specs=[half] * na),
        compiler_params=_cparams(("parallel", "parallel")), name=name)(slots, *args)


def _half_forward(arrs, name):
    n = len(arrs)

    def body(*refs):
        bufs = refs[n:2 * n]
        send_sems, recv_sems = refs[2 * n:]
        x, y, c = lax.axis_index("x"), lax.axis_index("y"), lax.axis_index("c")
        cps = [pltpu.make_async_remote_copy(b.at[_core_half(b.shape[0], c)], b.at[_core_half(b.shape[0], c)],
                                            send_sems.at[i], recv_sems.at[i], device_id=(x, y, 1 - c),
                                            device_id_type=MESH) for i, b in enumerate(bufs)]
        for cp in cps:
            cp.start()
        for i, b in enumerate(bufs):
            pltpu.make_async_remote_copy(b.at[_core_half(b.shape[0], c)], b.at[_core_half(b.shape[0], 1 - c)],
                                         send_sems.at[i], recv_sems.at[i], device_id=(x, y, 1 - c),
                                         device_id_type=MESH).wait()

    return pl.pallas_call(
        body, out_shape=[jax.ShapeDtypeStruct(a.shape, a.dtype) for a in arrs],
        in_specs=[HBM_SPEC] * n, out_specs=[HBM_SPEC] * n, input_output_aliases={k: k for k in range(n)},
        scratch_shapes=[pltpu.SemaphoreType.DMA((n,)), pltpu.SemaphoreType.DMA((n,))],
        name=name)(*arrs)


def _small_exchange(smalls, name):
    nsm = len(smalls)
    rels = [(fx, fy, fc) for fx in (0, 1) for fy in (0, 1) for fc in (0, 1)][1:]

    def body(*refs):
        sins, souts = refs[:nsm], refs[nsm:2 * nsm]
        ssend, srecv, slocal = refs[2 * nsm:]
        x, y, c = lax.axis_index("x"), lax.axis_index("y"), lax.axis_index("c")
        lin = 4 * x + 2 * y + c
        local = [pltpu.make_async_copy(sins[i], souts[i].at[lin], slocal.at[i]) for i in range(nsm)]
        for cp in local:
            cp.start()
        for i in range(nsm):
            for j, (fx, fy, fc) in enumerate(rels):
                pltpu.make_async_remote_copy(sins[i], souts[i].at[lin], ssend.at[i, j], srecv.at[i, j],
                                             device_id=(x ^ fx, y ^ fy, c ^ fc), device_id_type=MESH).start()
        for i in range(nsm):
            for j, (fx, fy, fc) in enumerate(rels):
                src = 4 * (x ^ fx) + 2 * (y ^ fy) + (c ^ fc)
                pltpu.make_async_remote_copy(sins[i], souts[i].at[src], ssend.at[i, j], srecv.at[i, j],
                                             device_id=(x ^ fx, y ^ fy, c ^ fc), device_id_type=MESH).wait()
        for cp in local:
            cp.wait()

    return pl.pallas_call(
        body, out_shape=[jax.ShapeDtypeStruct((N_DEV,) + s.shape, s.dtype) for s in smalls],
        in_specs=[HBM_SPEC] * nsm, out_specs=[HBM_SPEC] * nsm,
        scratch_shapes=[pltpu.SemaphoreType.DMA((nsm, 7)), pltpu.SemaphoreType.DMA((nsm, 7)),
                        pltpu.SemaphoreType.DMA((nsm,))],
        name=name)(*smalls)


def _sibling_swap(arrs, name):
    nw = len(arrs)

    def body(*refs):
        ins, outs = refs[:nw], refs[nw:2 * nw]
        send_sems, recv_sems = refs[2 * nw:]
        x, y, c = lax.axis_index("x"), lax.axis_index("y"), lax.axis_index("c")
        cps = [pltpu.make_async_remote_copy(ins[i], outs[i], send_sems.at[i], recv_sems.at[i],
                                            device_id=(x, y, 1 - c), device_id_type=MESH) for i in range(nw)]
        for cp in cps:
            cp.start()
        for cp in cps:
            cp.wait()

    return pl.pallas_call(
        body, out_shape=[jax.ShapeDtypeStruct(a.shape, a.dtype) for a in arrs],
        in_specs=[HBM_SPEC] * nw, out_specs=[HBM_SPEC] * nw,
        scratch_shapes=[pltpu.SemaphoreType.DMA((nw,)), pltpu.SemaphoreType.DMA((nw,))],
        name=name)(*arrs)


def _sum_parts(parts, recv, slots, name):
    na = len(parts)
    _, r, c = parts[0].shape
    tr = _row_tile(r)

    def body(slot_ref, *refs):
        for a in range(na):
            own_ref, r0_ref, r1_ref, r2_ref = refs[4 * a:4 * a + 4]
            refs[4 * na + a][...] = ((own_ref[0].astype(F32) + r0_ref[0].astype(F32))
                                     + (r1_ref[0].astype(F32) + r2_ref[0].astype(F32)))

    blk = lambda k: pl.BlockSpec((1, tr, c), lambda i, s, k=k: (s[k], i, 0))
    out_blk = pl.BlockSpec((tr, c), lambda i, s: (i + s[4] * (r // tr), 0))
    args = [a for p, rv in zip(parts, recv) for a in (p, rv, rv, rv)]
    return pl.pallas_call(
        body, out_shape=[jax.ShapeDtypeStruct((2 * r, c), F32)] * na,
        grid_spec=pltpu.PrefetchScalarGridSpec(
            num_scalar_prefetch=1, grid=(r // tr,), in_specs=[blk(0), blk(1), blk(2), blk(3)] * na,
            out_specs=[out_blk] * na),
        compiler_params=_cparams(("parallel",)), name=name)(slots, *args)


def _small_allreduce(packed, name):
    rows = packed.shape[0]
    pr = rows // N_DEV
    rels = [(fx, fy, fc) for fx in (0, 1) for fy in (0, 1) for fc in (0, 1)][1:]

    def body(in_ref, out_ref, recv_ref, send1, recv1, send2, recv2):
        x, y, c = lax.axis_index("x"), lax.axis_index("y"), lax.axis_index("c")
        lin = 4 * x + 2 * y + c
        piece = lambda ref, k: ref.at[pl.ds(pl.multiple_of(k * pr, pr), pr), :]
        peers = [((x ^ fx, y ^ fy, c ^ fc), 4 * (x ^ fx) + 2 * (y ^ fy) + (c ^ fc)) for fx, fy, fc in rels]
        for j, (dev, plin) in enumerate(peers):
            pltpu.make_async_remote_copy(piece(in_ref, plin), recv_ref.at[lin], send1.at[j], recv1.at[j],
                                         device_id=dev, device_id_type=MESH).start()
        recv_ref[lin] = piece(in_ref, lin)[...]
        for j, (dev, plin) in enumerate(peers):
            pltpu.make_async_remote_copy(piece(in_ref, plin), recv_ref.at[plin], send1.at[j], recv1.at[j],
                                         device_id=dev, device_id_type=MESH).wait()
        acc = recv_ref[0]
        for k in range(1, N_DEV):
            acc = acc + recv_ref[k]
        piece(out_ref, lin)[...] = acc
        for j, (dev, plin) in enumerate(peers):
            pltpu.make_async_remote_copy(piece(out_ref, lin), piece(out_ref, lin), send2.at[j], recv2.at[j],
                                         device_id=dev, device_id_type=MESH).start()
        for j, (dev, plin) in enumerate(peers):
            pltpu.make_async_remote_copy(piece(out_ref, lin), piece(out_ref, plin), send2.at[j], recv2.at[j],
                                         device_id=dev, device_id_type=MESH).wait()

    vm = pl.BlockSpec(memory_space=pltpu.VMEM)
    return pl.pallas_call(
        body, out_shape=jax.ShapeDtypeStruct(packed.shape, F32), in_specs=[vm], out_specs=vm,
        scratch_shapes=[pltpu.VMEM((N_DEV, pr, LANES), F32)] + [pltpu.SemaphoreType.DMA((7,))] * 4,
        compiler_params=pltpu.CompilerParams(vmem_limit_bytes=VMEM_LIMIT), name=name)(packed)


def _adamw_math(w, m, v, g):
    nm = ADAM_B1 * m + (1.0 - ADAM_B1) * g
    nv = ADAM_B2 * v + (1.0 - ADAM_B2) * (g * g)
    m_hat = nm * (1.0 / (1.0 - ADAM_B1 ** ADAM_STEP))
    v_hat = nv * (1.0 / (1.0 - ADAM_B2 ** ADAM_STEP))
    return -ADAM_LR * (m_hat / (jnp.sqrt(v_hat) + ADAM_EPS) + ADAM_WD * w), nm, nv


def _adamw(ws, ms, vs, gs, name):
    na = len(ws)
    r, c = ws[0].shape
    tr = _row_tile(r)

    def body(*refs):
        for a in range(na):
            w_ref, m_ref, v_ref, g_ref = refs[4 * a:4 * a + 4]
            d_ref, nm_ref, nv_ref = refs[4 * na + 3 * a:4 * na + 3 * a + 3]
            d_ref[...], nm_ref[...], nv_ref[...] = _adamw_math(w_ref[...], m_ref[...], v_ref[...], g_ref[...])

    blk = pl.BlockSpec((tr, c), lambda i: (i, 0))
    sh = jax.ShapeDtypeStruct((r, c), F32)
    args = [a for group in zip(ws, ms, vs, gs) for a in group]
    res = pl.pallas_call(body, out_shape=[sh] * (3 * na), grid=(r // tr,), in_specs=[blk] * (4 * na),
                         out_specs=[blk] * (3 * na), compiler_params=_cparams(("parallel",)), name=name)(*args)
    return [tuple(res[3 * a:3 * a + 3]) for a in range(na)]


def _adamw_small(ws, ms, vs, alls, split, name):
    n = len(ws)
    lead = split if split is not None else ()
    nl = len(lead)
    nslots = alls[0].shape[0]

    def blocks(shape):
        if split is None:
            return tuple(shape), (lambda *g: (0,) * len(shape))
        blk = (shape[0], shape[1] // lead[0], shape[2] // lead[1]) + tuple(shape[3:])
        return blk, (lambda *g: (0, g[0], g[1]) + (0,) * (len(shape) - 3))

    def body(*refs):
        w_refs, m_refs, v_refs, a_refs = (refs[k * n:(k + 1) * n] for k in range(4))
        g_refs, d_refs, nm_refs, nv_refs = (refs[(4 + k) * n:(5 + k) * n] for k in range(4))
        k = pl.program_id(nl)
        for i in range(n):
            @pl.when(k == 0)
            def _(i=i):
                g_refs[i][...] = a_refs[i][0]

            @pl.when(k > 0)
            def _(i=i):
                g_refs[i][...] += a_refs[i][0]

            @pl.when(k == nslots - 1)
            def _(i=i):
                d_refs[i][...], nm_refs[i][...], nv_refs[i][...] = _adamw_math(
                    w_refs[i][...], m_refs[i][...], v_refs[i][...], g_refs[i][...])

    specs, aspecs, shapes = [], [], []
    for wa in ws:
        blk, imap = blocks(wa.shape)
        specs.append(pl.BlockSpec(blk, imap))
        aspecs.append(pl.BlockSpec((1,) + blk, (lambda *g, imap=imap: (g[nl],) + imap(*g))))
        shapes.append(jax.ShapeDtypeStruct(wa.shape, F32))
    res = pl.pallas_call(
        body, out_shape=shapes * 4, grid=tuple(lead) + (nslots,), in_specs=specs * 3 + aspecs,
        out_specs=specs * 4, compiler_params=_cparams(("parallel",) * nl + ("arbitrary",)),
        name=name)(*ws, *ms, *vs, *alls)
    return res[:n], res[n:2 * n], res[2 * n:3 * n], res[3 * n:]


def kernel(x, norm_ffn1, ffn1_w_gate, ffn1_w_up, ffn1_w_down, norm_mix, w_in, attn_sinks, ssm_lambda_re, ssm_lambda_im, ssm_log_dt, ssm_b_re, ssm_b_im, ssm_c_re, ssm_c_im, ssm_d, ssm_glu_w, ssm_glu_b, attn_out_norm, ssm_out_norm, w_out, norm_ffn2, ffn2_w_gate, ffn2_w_up, ffn2_w_down, final_norm, loss_target, m_norm_ffn1, m_ffn1_w_gate, m_ffn1_w_up, m_ffn1_w_down, m_norm_mix, m_w_in, m_attn_sinks, m_ssm_lambda_re, m_ssm_lambda_im, m_ssm_log_dt, m_ssm_b_re, m_ssm_b_im, m_ssm_c_re, m_ssm_c_im, m_ssm_d, m_ssm_glu_w, m_ssm_glu_b, m_attn_out_norm, m_ssm_out_norm, m_w_out, m_norm_ffn2, m_ffn2_w_gate, m_ffn2_w_up, m_ffn2_w_down, m_final_norm, v_norm_ffn1, v_ffn1_w_gate, v_ffn1_w_up, v_ffn1_w_down, v_norm_mix, v_w_in, v_attn_sinks, v_ssm_lambda_re, v_ssm_lambda_im, v_ssm_log_dt, v_ssm_b_re, v_ssm_b_im, v_ssm_c_re, v_ssm_c_im, v_ssm_d, v_ssm_glu_w, v_ssm_glu_b, v_attn_out_norm, v_ssm_out_norm, v_w_out, v_norm_ffn2, v_ffn2_w_gate, v_ffn2_w_up, v_ffn2_w_down, v_final_norm):
    given = dict(locals())
    wts = {n: given[n] for n in WEIGHTS}

    order = [g for g in GROUPS]
    cx, cy = lax.axis_index("x"), lax.axis_index("y")
    slots = jnp.stack([2 * cx + cy, 2 * (1 - cx) + cy, 2 * cx + 1 - cy, 2 * (1 - cx) + 1 - cy,
                       lax.axis_index("c")]).astype(jnp.int32)
    def view(a, n):
        if n in TRANSPOSED:
            return jnp.swapaxes(a[0], 0, 1)
        if n in BIG:
            return a[0]
        if n in ('ssm_b_re', 'ssm_b_im'):
            return jnp.swapaxes(a, -1, -2)
        return a.reshape(1, -1) if a.ndim == 1 else a

    def unview(a, n):
        if n in TRANSPOSED:
            return jnp.swapaxes(a, 0, 1)[None]
        if n in ('ssm_b_re', 'ssm_b_im'):
            return jnp.swapaxes(a, -1, -2)
        return a.reshape(wts[n].shape)

    started, gather_token = {}, None
    for g in order:
        shards = [view(wts[n], n).astype(BF16) for n in GROUPS[g]]
        placed = [_place_own(s, slots, f"weights_place_{n}") for n, s in zip(GROUPS[g], shards)]
        st, gather_token = _exchange_start([(shards, placed)], False, f"weights_start_{g}", after=gather_token)
        started[g] = st[0]

    def get_weights(group, after):
        if group == order[0]:
            after = after + gather_token[:1, :1]
        _, lands = _exchange_wait(started[group], after, False, f"weights_wait_{group}")
        lands = _sibling_forward(lands, f"weights_forward_{group}")
        out = dict(zip(GROUPS[group], lands))
        for n in ('w_in', 'ssm_glu_w', 'w_out'):
            if n in out:
                out[n] = out[n].reshape(-1, out[n].shape[-1])
        return out

    sent, tokens = {}, {}

    def put_grads(group, gd):
        parts = []
        for n in GROUPS[group]:
            g = gd[n]
            if g.ndim == 2:
                g = g.reshape(N_CHIPS, g.shape[0] // N_CHIPS, g.shape[1])
            parts.append(g.astype(BF16))
        sib = _half_swap(parts, f"grads_half_swap_{group}")
        same = len({p.shape for p in parts}) == 1
        batches = [list(range(len(parts)))] if same else [[i] for i in range(len(parts))]
        halves = [None] * len(parts)
        for b in batches:
            res = _half_add([parts[i] for i in b], [sib[i] for i in b], slots, f"grads_half_add_{GROUPS[group][b[0]]}")
            for i, h in zip(b, res):
                halves[i] = h
        parts = halves
        lands = [lax.empty(p.shape, p.dtype) for p in parts]
        started_g, tokens[group] = _exchange_start([(parts, lands)], True, f"grads_start_{group}")
        sent[group] = started_g[0]
        return tokens[group]

    w = {n: (wts[n][0] if wts[n].ndim > 1 else wts[n]) for n in SMALL}
    w['norm_ffn1'], w['norm_mix'], w['norm_ffn2'] = wts['norm_ffn1'], wts['norm_mix'], wts['norm_ffn2']
    w['ssm_b_re'], w['ssm_b_im'] = view(wts['ssm_b_re'], 'ssm_b_re')[0], view(wts['ssm_b_im'], 'ssm_b_im')[0]
    w['ssm_log_dt'] = w['ssm_log_dt'] + gather_token[0, 0]
    wide =['ssm_b_re', 'ssm_b_im', 'ssm_c_re', 'ssm_c_im']

    def reduce_wide(gd):
        packed = jnp.concatenate([gd[n].reshape(-1, LANES) for n in wide])
        return _small_allreduce(packed, "small_grads_allreduce")

    loss, dx, grads, wide_sum = _local_step(x[0], loss_target[0], w, get_weights, put_grads, reduce_wide)
    loss = lax.psum(loss, ("x", "y", "c"))

    out_g, out_d, out_m, out_v = {}, {}, {}, {}

    def finish(group, after):
        names = GROUPS[group]
        parts, recv = _exchange_wait(sent[group], after, True, f"grads_wait_{group}")
        same = len({p.shape for p in parts}) == 1
        batches = [list(range(len(names)))] if same else [[i] for i in range(len(names))]
        sums = [None] * len(names)
        for b in batches:
            res = _sum_parts([parts[i] for i in b], [recv[i] for i in b], slots, f"grad_sum_{names[b[0]]}")
            for i, sm in zip(b, res):
                sums[i] = sm
        full = _half_forward(sums, f"grad_half_forward_{group}")
        for b in batches:
            res = _adamw([view(wts[names[i]], names[i]) for i in b], [view(given['m_' + names[i]], names[i]) for i in b],
                         [view(given['v_' + names[i]], names[i]) for i in b], [full[i] for i in b],
                         f"adamw_{names[b[0]]}")
            for i, (d, nm, nv) in zip(b, res):
                n = names[i]
                out_g[n], out_d[n], out_m[n], out_v[n] = (unview(a, n) for a in (full[i], d, nm, nv))
        return nv

    done = finish('ffn2', tokens['ffn1'])
    done = finish('mix', done)

    nat = {n: view(wts[n], n).shape for n in SMALL}
    narrow = [n for n in SMALL if n not in wide]
    alls = _small_exchange([grads[n].reshape(nat[n]) for n in narrow], "small_grads_allgather")
    rows = wide_sum.shape[0] // len(wide)
    wide_g = [wide_sum[i * rows:(i + 1) * rows].reshape((1,) + nat[n]) for i, n in enumerate(wide)]
    for group, gs, split, tag in ((narrow, alls, None, "adamw_small"), (wide, wide_g, (2, 4), "adamw_ssm_bc")):
        res = _adamw_small([view(wts[n], n) for n in group], [view(given['m_' + n], n) for n in group],
                           [view(given['v_' + n], n) for n in group], gs, split, tag)
        for dst, vals in zip((out_g, out_d, out_m, out_v), res):
            for n, a in zip(group, vals):
                dst[n] = unview(a, n)

    finish('ffn1', out_v['norm_ffn1'][:, :1] + out_v['ssm_c_re'].reshape(1, -1)[:, :1] + loss)

    return (loss, dx[None], *[out_g[n] for n in WEIGHTS], *[out_d[n] for n in WEIGHTS],
            *[out_m[n] for n in WEIGHTS], *[out_v[n] for n in WEIGHTS])
```

```python
import functools
import math

import numpy as np
import jax
import jax.numpy as jnp
from jax import lax
from jax.experimental import pallas as pl
from jax.experimental.pallas import tpu as pltpu

F32 = jnp.float32
BF16 = jnp.bfloat16
MESH = pl.DeviceIdType.MESH

EPS = 1e-6
NEG_INF = -1e30
LAMBDA_RE_MAX = -1e-4
ATTN_HEADS = 8
KV_HEADS = 2
GQ = ATTN_HEADS // KV_HEADS
HEAD_DIM = 64
ATTN_WIDTH = 512
KV_WIDTH = 128
WINDOW = 128
QBLOCK = 128
SSM_WIDTH = 512
SSM_GROUPS = 32
SSM_CH = 16
SSM_STATE = 64
N_STRIPS = 4
STRIP_IN = SSM_WIDTH // N_STRIPS
STRIP_ST = SSM_GROUPS * SSM_STATE // N_STRIPS
SUBLANES = 8
LANES = 128
N_CHIPS = 4
N_DEV = 8

ADAM_LR = 0.001
ADAM_B1 = 0.9
ADAM_B2 = 0.999
ADAM_EPS = 1e-08
ADAM_WD = 0.01
ADAM_STEP = 10

VMEM_LIMIT = 48 * 1024 * 1024

WEIGHTS = ['norm_ffn1', 'ffn1_w_gate', 'ffn1_w_up', 'ffn1_w_down', 'norm_mix', 'w_in', 'attn_sinks',
           'ssm_lambda_re', 'ssm_lambda_im', 'ssm_log_dt', 'ssm_b_re', 'ssm_b_im', 'ssm_c_re', 'ssm_c_im',
           'ssm_d', 'ssm_glu_w', 'ssm_glu_b', 'attn_out_norm', 'ssm_out_norm', 'w_out', 'norm_ffn2',
           'ffn2_w_gate', 'ffn2_w_up', 'ffn2_w_down', 'final_norm']
BIG = ['ffn1_w_gate', 'ffn1_w_up', 'ffn1_w_down', 'w_in', 'ssm_glu_w', 'w_out',
       'ffn2_w_gate', 'ffn2_w_up', 'ffn2_w_down']
SMALL = [n for n in WEIGHTS if n not in BIG]
TRANSPOSED = ['ffn1_w_gate', 'ffn1_w_up', 'w_in', 'ffn2_w_gate', 'ffn2_w_up']
GROUPS = {'ffn1': ['ffn1_w_gate', 'ffn1_w_up', 'ffn1_w_down'],
          'mix': ['w_in', 'ssm_glu_w', 'w_out'],
          'ffn2': ['ffn2_w_gate', 'ffn2_w_up', 'ffn2_w_down']}


def _cparams(sem=None):
    return pltpu.CompilerParams(dimension_semantics=sem, vmem_limit_bytes=VMEM_LIMIT)


def _tile(n, pref):
    if n <= pref:
        return n
    for t in (pref, pref // 2, pref // 4):
        if t % LANES == 0 and n % t == 0:
            return t
    return n


def _sigmoid(x):
    return 1.0 / (1.0 + jnp.exp(-x))


def _mm(a, b, *, ta=False, tb=False, reduce_s=False, res=None, scale=1.0, out_dtype=F32, after=None, name):
    a3 = a if a.ndim == 3 else a[None]
    b3 = b if b.ndim == 3 else b[None]
    sa, sb = a3.shape[0], b3.shape[0]
    ns = max(sa, sb)
    (kk, m) = a3.shape[1:] if ta else a3.shape[1:][::-1]
    (n, kb) = b3.shape[1:] if tb else b3.shape[1:][::-1]
    assert kk == kb, (a3.shape, b3.shape)
    tm, tn, tk = _tile(m, 1024), _tile(n, 1024), _tile(kk, 2048)
    nm, nn, nk = m // tm, n // tn, kk // tk
    has_res = res is not None
    single = nk == 1 and not (reduce_s and ns > 1)

    if reduce_s:
        grid = (nm, nn, ns, nk)
        ids = lambda i, j, s, k: (s, i, j, k)
        sem = ("parallel", "parallel", "arbitrary", "arbitrary")
    else:
        grid = (ns, nm, nn, nk)
        ids = lambda s, i, j, k: (s, i, j, k)
        sem = ("parallel", "parallel", "parallel", "arbitrary")

    def a_map(*g):
        s, i, j, k = ids(*g)
        s = s if sa > 1 else 0
        return (s, k, i) if ta else (s, i, k)

    def b_map(*g):
        s, i, j, k = ids(*g)
        s = s if sb > 1 else 0
        return (s, j, k) if tb else (s, k, j)

    def o_map(*g):
        s, i, j, k = ids(*g)
        return (i, j) if reduce_s else (s, i, j)

    a_blk = (1, tk, tm) if ta else (1, tm, tk)
    b_blk = (1, tn, tk) if tb else (1, tk, tn)
    dims = (((0 if ta else 1,), (1 if tb else 0,)), ((), ()))

    def body(*refs):
        a_ref, b_ref = refs[0], refs[1]
        r_ref = refs[2] if has_res else None
        o_ref = refs[2 + has_res + (after is not None)]
        acc_ref = None if single else refs[-1]
        s, _, _, k = ids(*[pl.program_id(d) for d in range(4)])
        prod = lax.dot_general(a_ref[0].astype(BF16), b_ref[0].astype(BF16), dims, preferred_element_type=F32)

        def finish(out):
            if scale != 1.0:
                out = out * scale
            if has_res:
                out = r_ref[...].reshape(out.shape) + out
            o_ref[...] = out.astype(out_dtype).reshape(o_ref.shape)

        if single:
            finish(prod)
            return
        if reduce_s:
            first = jnp.logical_and(s == 0, k == 0)
            last = jnp.logical_and(s == ns - 1, k == nk - 1)
        else:
            first, last = k == 0, k == nk - 1

        @pl.when(first)
        def _():
            acc_ref[...] = prod

        @pl.when(jnp.logical_not(first))
        def _():
            acc_ref[...] += prod

        @pl.when(last)
        def _():
            finish(acc_ref[...])

    in_specs = [pl.BlockSpec(a_blk, a_map), pl.BlockSpec(b_blk, b_map)]
    args = [a3, b3]
    if reduce_s:
        out_shape = jax.ShapeDtypeStruct((m, n), out_dtype)
        o_spec = pl.BlockSpec((tm, tn), o_map)
    else:
        out_shape = jax.ShapeDtypeStruct((ns, m, n), out_dtype)
        o_spec = pl.BlockSpec((1, tm, tn), o_map)
    if has_res:
        assert res.shape == out_shape.shape
        in_specs.append(o_spec)
        args.append(res)
    if after is not None:
        in_specs.append(HBM_SPEC)
        args.append(after)
    return pl.pallas_call(body, out_shape=out_shape, grid=grid, in_specs=in_specs, out_specs=o_spec,
                          scratch_shapes=[] if single else [pltpu.VMEM((tm, tn), F32)],
                          compiler_params=_cparams(sem), name=name)(*args)


def _row_tile(t, cap=256):
    for step in (16, SUBLANES):
        for tr in range(min(cap, t) // step * step, 0, -step):
            if t % tr == 0:
                return tr
    return t


def _rms_fwd(x, g, name):
    t, w = x.shape
    tr = _row_tile(t)

    def body(x_ref, g_ref, o_ref):
        xv = x_ref[...]
        r = lax.rsqrt(jnp.mean(xv * xv, axis=-1, keepdims=True) + EPS)
        o_ref[...] = (xv * r * g_ref[...]).astype(BF16)

    return pl.pallas_call(
        body, out_shape=jax.ShapeDtypeStruct((t, w), BF16), grid=(t // tr,),
        in_specs=[pl.BlockSpec((tr, w), lambda i: (i, 0)), pl.BlockSpec((1, w), lambda i: (0, 0))],
        out_specs=pl.BlockSpec((tr, w), lambda i: (i, 0)), compiler_params=_cparams(("parallel",)),
        name=name)(x, g)


def _rms_bwd_rows(xv, gv, dhv):
    r = lax.rsqrt(jnp.mean(xv * xv, axis=-1, keepdims=True) + EPS)
    nrm = xv * r
    dn = dhv * gv
    return r * (dn - nrm * jnp.mean(dn * nrm, axis=-1, keepdims=True)), dhv * nrm


def _rms_bwd(x, g, dh, dres, name):
    t, w = x.shape
    tr = _row_tile(t)
    has_res = dres is not None

    def body(*refs):
        if has_res:
            x_ref, g_ref, dh_ref, dr_ref, dx_ref, dxb_ref, dg_ref = refs
        else:
            x_ref, g_ref, dh_ref, dx_ref, dxb_ref, dg_ref = refs
        dx, dgs = _rms_bwd_rows(x_ref[...], g_ref[...], dh_ref[...])
        if has_res:
            dx = dx + dr_ref[...]
        dx_ref[...] = dx
        dxb_ref[...] = dx.astype(BF16)

        @pl.when(pl.program_id(0) == 0)
        def _():
            dg_ref[...] = jnp.zeros_like(dg_ref)

        dg_ref[...] += jnp.sum(dgs, axis=0, keepdims=True)

    row = pl.BlockSpec((tr, w), lambda i: (i, 0))
    vec = pl.BlockSpec((1, w), lambda i: (0, 0))
    ins = [x, g, dh] + ([dres] if has_res else [])
    return pl.pallas_call(
        body, out_shape=(jax.ShapeDtypeStruct((t, w), F32), jax.ShapeDtypeStruct((t, w), BF16),
                         jax.ShapeDtypeStruct((1, w), F32)),
        grid=(t // tr,), in_specs=[row, vec, row] + ([row] if has_res else []),
        out_specs=(row, row, vec), compiler_params=_cparams(("arbitrary",)), name=name)(*ins)


FFN_ROWS = 512


NT_DIMS = (((1,), (1,)), ((), ()))
TN_DIMS = (((0,), (0,)), ((), ()))


def _ffn_fwd_call(x, g, wg, wu, wd, name):
    t, d = x.shape
    ns, f, _ = wg.shape
    tm = _tile(t, FFN_ROWS)

    def body(x_ref, g_ref, wg_ref, wu_ref, wd_ref, xo_ref, h_ref, gate_ref, up_ref, h_sc, acc_ref):
        s = pl.program_id(1)

        @pl.when(s == 0)
        def _():
            xv = x_ref[...]
            r = lax.rsqrt(jnp.mean(xv * xv, axis=-1, keepdims=True) + EPS)
            hb = (xv * r * g_ref[...]).astype(BF16)
            h_sc[...] = hb
            h_ref[...] = hb

        hb = h_sc[...]
        gate = lax.dot_general(hb, wg_ref[0], NT_DIMS, preferred_element_type=F32)
        up = lax.dot_general(hb, wu_ref[0], NT_DIMS, preferred_element_type=F32)
        gate_ref[0] = gate.astype(BF16)
        up_ref[0] = up.astype(BF16)
        act = (gate * _sigmoid(gate) * up).astype(BF16)
        prod = jnp.dot(act, wd_ref[0], preferred_element_type=F32)

        @pl.when(s == 0)
        def _():
            acc_ref[...] = prod

        @pl.when(s > 0)
        def _():
            acc_ref[...] += prod

        @pl.when(s == ns - 1)
        def _():
            xo_ref[...] = x_ref[...] + 0.5 * acc_ref[...]

    row = pl.BlockSpec((tm, d), lambda i, s: (i, 0))
    vec = pl.BlockSpec((1, d), lambda i, s: (0, 0))
    wrow = pl.BlockSpec((1, f, d), lambda i, s: (s, 0, 0))
    hid = pl.BlockSpec((1, tm, f), lambda i, s: (s, i, 0))
    hid_sh = jax.ShapeDtypeStruct((ns, t, f), BF16)
    return pl.pallas_call(
        body, out_shape=(jax.ShapeDtypeStruct((t, d), F32), jax.ShapeDtypeStruct((t, d), BF16), hid_sh, hid_sh),
        grid=(t // tm, ns), in_specs=[row, vec, wrow, wrow, wrow], out_specs=(row, row, hid, hid),
        scratch_shapes=[pltpu.VMEM((tm, d), BF16), pltpu.VMEM((tm, d), F32)],
        compiler_params=_cparams(("parallel", "arbitrary")), name=name)(x, g, wg, wu, wd)


def _ffn_bwd_x_call(dxo, dxo_b, x, g, gate, up, wg, wu, wd, name):
    t, d = x.shape
    ns, f, _ = wg.shape
    tm = _tile(t, FFN_ROWS)

    def body(dxo_ref, dxb_ref, x_ref, g_ref, gate_ref, up_ref, wg_ref, wu_ref, wd_ref,
             dx_ref, dxob_ref, dgn_ref, dgate_ref, dup_ref, act_ref, dh_ref):
        i, s = pl.program_id(0), pl.program_id(1)
        dact = lax.dot_general(dxb_ref[...], wd_ref[0], NT_DIMS, preferred_element_type=F32) * 0.5
        gv = gate_ref[0].astype(F32)
        uv = up_ref[0].astype(F32)
        sg = _sigmoid(gv)
        silu = gv * sg
        act_ref[0] = (silu * uv).astype(BF16)
        dub = (dact * silu).astype(BF16)
        dgb = (dact * uv * sg * (1.0 + gv * (1.0 - sg))).astype(BF16)
        dup_ref[0] = dub
        dgate_ref[0] = dgb
        prod = (jnp.dot(dgb, wg_ref[0], preferred_element_type=F32)
                + jnp.dot(dub, wu_ref[0], preferred_element_type=F32))

        @pl.when(s == 0)
        def _():
            dh_ref[...] = prod

        @pl.when(s > 0)
        def _():
            dh_ref[...] += prod

        @pl.when(jnp.logical_and(i == 0, s == 0))
        def _():
            dgn_ref[...] = jnp.zeros_like(dgn_ref)

        @pl.when(s == ns - 1)
        def _():
            dx, dgs = _rms_bwd_rows(x_ref[...], g_ref[...], dh_ref[...])
            dx = dx + dxo_ref[...]
            dx_ref[...] = dx
            dxob_ref[...] = dx.astype(BF16)
            dgn_ref[...] += jnp.sum(dgs, axis=0, keepdims=True)

    row = pl.BlockSpec((tm, d), lambda i, s: (i, 0))
    vec = pl.BlockSpec((1, d), lambda i, s: (0, 0))
    wrow = pl.BlockSpec((1, f, d), lambda i, s: (s, 0, 0))
    hid = pl.BlockSpec((1, tm, f), lambda i, s: (s, i, 0))
    hid_sh = jax.ShapeDtypeStruct((ns, t, f), BF16)
    return pl.pallas_call(
        body,
        out_shape=(jax.ShapeDtypeStruct((t, d), F32), jax.ShapeDtypeStruct((t, d), BF16),
                   jax.ShapeDtypeStruct((1, d), F32), hid_sh, hid_sh, hid_sh),
        grid=(t // tm, ns), in_specs=[row, row, row, vec, hid, hid, wrow, wrow, wrow],
        out_specs=(row, row, vec, hid, hid, hid), scratch_shapes=[pltpu.VMEM((tm, d), F32)],
        compiler_params=_cparams(("arbitrary", "arbitrary")), name=name)(dxo, dxo_b, x, g, gate, up, wg, wu, wd)


def _ffn_bwd_w_call(h, dxo_b, dgate, dup, act, name):
    t, d = h.shape
    ns, _, f = dgate.shape
    tm = _tile(t, FFN_ROWS)
    nm = t // tm

    def body(h_ref, dxb_ref, dgate_ref, dup_ref, act_ref, dwg_ref, dwu_ref, dwd_ref, ag_ref, au_ref, ad_ref):
        i = pl.program_id(1)
        hv = h_ref[...]
        pg = lax.dot_general(dgate_ref[0], hv, TN_DIMS, preferred_element_type=F32)
        pu = lax.dot_general(dup_ref[0], hv, TN_DIMS, preferred_element_type=F32)
        pd = lax.dot_general(act_ref[0], dxb_ref[...], TN_DIMS, preferred_element_type=F32)

        @pl.when(i == 0)
        def _():
            ag_ref[...] = pg
            au_ref[...] = pu
            ad_ref[...] = pd

        @pl.when(i > 0)
        def _():
            ag_ref[...] += pg
            au_ref[...] += pu
            ad_ref[...] += pd

        @pl.when(i == nm - 1)
        def _():
            dwg_ref[0] = ag_ref[...].astype(BF16)
            dwu_ref[0] = au_ref[...].astype(BF16)
            dwd_ref[0] = (0.5 * ad_ref[...]).astype(BF16)

    row = pl.BlockSpec((tm, d), lambda s, i: (i, 0))
    hid = pl.BlockSpec((1, tm, f), lambda s, i: (s, i, 0))
    wrow = pl.BlockSpec((1, f, d), lambda s, i: (s, 0, 0))
    wsh = jax.ShapeDtypeStruct((ns, f, d), BF16)
    return pl.pallas_call(
        body, out_shape=(wsh, wsh, wsh),
        grid=(ns, nm), in_specs=[row, row, hid, hid, hid], out_specs=(wrow, wrow, wrow),
        scratch_shapes=[pltpu.VMEM((f, d), F32), pltpu.VMEM((f, d), F32), pltpu.VMEM((f, d), F32)],
        compiler_params=_cparams(("parallel", "arbitrary")), name=name)(h, dxo_b, dgate, dup, act)


def _loss_head(x, g, tgt, name):
    t, w = x.shape
    tr = _row_tile(t)

    def body(x_ref, g_ref, t_ref, loss_ref, dx_ref, dxb_ref, dg_ref):
        xv = x_ref[...]
        gv = g_ref[...]
        r = lax.rsqrt(jnp.mean(xv * xv, axis=-1, keepdims=True) + EPS)
        nrm = xv * r
        err = nrm * gv - t_ref[...]
        dout = err * (1.0 / w)
        dn = dout * gv
        dx = r * (dn - nrm * jnp.mean(dn * nrm, axis=-1, keepdims=True))
        dx_ref[...] = dx
        dxb_ref[...] = dx.astype(BF16)

        @pl.when(pl.program_id(0) == 0)
        def _():
            dg_ref[...] = jnp.zeros_like(dg_ref)
            loss_ref[...] = jnp.zeros_like(loss_ref)

        dg_ref[...] += jnp.sum(dout * nrm, axis=0, keepdims=True)
        part = jnp.sum(jnp.sum(err * err, axis=-1, keepdims=True) * (0.5 / w), axis=0, keepdims=True)
        loss_ref[...] += jnp.broadcast_to(part, loss_ref.shape)

    row = pl.BlockSpec((tr, w), lambda i: (i, 0))
    vec = pl.BlockSpec((1, w), lambda i: (0, 0))
    return pl.pallas_call(
        body, out_shape=(jax.ShapeDtypeStruct((1, LANES), F32), jax.ShapeDtypeStruct((t, w), F32),
                         jax.ShapeDtypeStruct((t, w), BF16), jax.ShapeDtypeStruct((1, w), F32)),
        grid=(t // tr,), in_specs=[row, vec, row],
        out_specs=(pl.BlockSpec((1, LANES), lambda i: (0, 0)), row, row, vec),
        compiler_params=_cparams(("arbitrary",)), name=name)(x, g, tgt)


def _attn_scores(q, k3, n, t, slope_ref):
    rows = GQ * QBLOCK
    s = lax.dot_general(q, k3, (((1,), (1,)), ((), ())), preferred_element_type=F32) * (HEAD_DIM ** -0.5)
    row = lax.broadcasted_iota(jnp.int32, (rows, 3 * QBLOCK), 0) & (QBLOCK - 1)
    col = lax.broadcasted_iota(jnp.int32, (rows, 3 * QBLOCK), 1)
    rel = jnp.abs(col - QBLOCK - row)
    key_pos = n * QBLOCK - QBLOCK + col
    valid = (rel <= WINDOW) & (key_pos >= 0) & (key_pos < t)
    return jnp.where(valid, s - slope_ref[0] * rel.astype(F32), NEG_INF)


Q_COL, K_COL, V_COL, U_COL = 0, ATTN_WIDTH // LANES, ATTN_WIDTH // LANES + 1, ATTN_WIDTH // LANES + 2


def _key_rows(ref, n, nb):
    prev, nxt = jnp.maximum(n - 1, 0), jnp.minimum(n + 1, nb - 1)
    blk = lambda b: ref[pl.ds(pl.multiple_of(b * QBLOCK, QBLOCK), QBLOCK), :]
    return jnp.concatenate([blk(prev), blk(n), blk(nxt)], axis=0)


def _head_tiles(x, kh, low):
    tiles = []
    for g in range(GQ):
        h = GQ * kh + g
        t128 = x[:, LANES * (h // 2):LANES * (h // 2 + 1)]
        t128 = jnp.where(low if h % 2 == 0 else jnp.logical_not(low), t128, 0.0)
        if h % 2 != kh:
            t128 = pltpu.roll(t128, HEAD_DIM, 1)
        tiles.append(t128)
    return jnp.concatenate(tiles, axis=0)


def _head_merge(per_kh, low):
    out = []
    for j in range(ATTN_HEADS // 2):
        pair = []
        for h in (2 * j, 2 * j + 1):
            kh, g = h // GQ, h % GQ
            t128 = per_kh[kh][g * QBLOCK:(g + 1) * QBLOCK, :]
            if h % 2 != kh:
                t128 = pltpu.roll(t128, HEAD_DIM, 1)
            pair.append(t128)
        out.append(jnp.where(low, pair[0], pair[1]))
    return jnp.concatenate(out, axis=1)


def _attn_fwd_proj(proj, sink_rows, slope_rows, name):
    t = proj.shape[0]
    nb = t // QBLOCK
    rows = GQ * QBLOCK

    def body(q_ref, k_ref, v_ref, sink_ref, slope_ref, o_ref, lse_ref):
        n = pl.program_id(0)
        low = lax.broadcasted_iota(jnp.int32, (QBLOCK, LANES), 1) < HEAD_DIM
        k3 = _key_rows(k_ref, n, nb).astype(BF16)
        v3 = _key_rows(v_ref, n, nb).astype(BF16)
        q = q_ref[...]
        outs = []
        for kh in range(KV_HEADS):
            qs = _head_tiles(q, kh, low).astype(BF16)
            s = _attn_scores(qs, k3, n, t, slope_ref.at[pl.ds(kh, 1)])
            sink = sink_ref[kh]
            mx = jnp.maximum(jnp.max(s, axis=-1, keepdims=True), sink)
            p = jnp.exp(s - mx)
            den = jnp.sum(p, axis=-1, keepdims=True) + jnp.exp(sink - mx)
            outs.append(jnp.dot(p.astype(BF16), v3, preferred_element_type=F32) / den)
            lse_ref[0, kh] = mx + jnp.log(den)
        o_ref[...] = _head_merge(outs, low)

    strip = lambda col: pl.BlockSpec((t, LANES), lambda n, col=col: (0, col))
    rowspec = pl.BlockSpec((KV_HEADS, rows, 1), lambda n: (0, 0, 0))
    return pl.pallas_call(
        body, out_shape=(jax.ShapeDtypeStruct((t, ATTN_WIDTH), F32), jax.ShapeDtypeStruct((nb, KV_HEADS, rows, 1), F32)),
        grid=(nb,), in_specs=[pl.BlockSpec((QBLOCK, ATTN_WIDTH), lambda n: (n, 0)), strip(K_COL), strip(V_COL),
                              rowspec, rowspec],
        out_specs=(pl.BlockSpec((QBLOCK, ATTN_WIDTH), lambda n: (n, 0)),
                   pl.BlockSpec((1, KV_HEADS, rows, 1), lambda n: (n, 0, 0, 0))),
        compiler_params=_cparams(("parallel",)), name=name)(proj, proj, proj, sink_rows, slope_rows)


def _attn_bwd_proj(proj, sink_rows, slope_rows, o, lse, do, name):
    t = proj.shape[0]
    nb = t // QBLOCK
    rows = GQ * QBLOCK
    scale = HEAD_DIM ** -0.5

    def body(q_ref, k_ref, v_ref, sink_ref, slope_ref, o_ref, lse_ref, do_ref, dq_ref, dk_ref, dv_ref, ds_ref):
        n = pl.program_id(0)

        @pl.when(n == 0)
        def _():
            dk_ref[...] = jnp.zeros_like(dk_ref)
            dv_ref[...] = jnp.zeros_like(dv_ref)
            ds_ref[...] = jnp.zeros_like(ds_ref)

        low = lax.broadcasted_iota(jnp.int32, (QBLOCK, LANES), 1) < HEAD_DIM
        k3 = _key_rows(k_ref, n, nb).astype(BF16)
        v3 = _key_rows(v_ref, n, nb).astype(BF16)
        q, dov = q_ref[...], do_ref[...]
        dod = dov * o_ref[...]
        dqs = []
        dk3 = jnp.zeros((3 * QBLOCK, LANES), F32)
        dv3 = jnp.zeros((3 * QBLOCK, LANES), F32)
        for kh in range(KV_HEADS):
            qs = _head_tiles(q, kh, low).astype(BF16)
            dos = _head_tiles(dov, kh, low).astype(BF16)
            delta = jnp.sum(_head_tiles(dod, kh, low), axis=-1, keepdims=True)
            lse_kh = lse_ref[0, kh]
            s = _attn_scores(qs, k3, n, t, slope_ref.at[pl.ds(kh, 1)])
            p = jnp.exp(s - lse_kh)
            dp = lax.dot_general(dos, v3, NT_DIMS, preferred_element_type=F32)
            dsb = (p * (dp - delta)).astype(BF16)
            dqs.append(jnp.dot(dsb, k3, preferred_element_type=F32) * scale)
            dk3 = dk3 + lax.dot_general(dsb, qs, TN_DIMS, preferred_element_type=F32) * scale
            dv3 = dv3 + lax.dot_general(p.astype(BF16), dos, TN_DIMS, preferred_element_type=F32)
            dsink_rows = -jnp.exp(sink_ref[kh] - lse_kh) * delta
            ds_ref[kh] += jnp.sum(dsink_rows.reshape(GQ, QBLOCK, 1), axis=1)
        dq_ref[...] = _head_merge(dqs, low)
        prev, nxt = jnp.maximum(n - 1, 0), jnp.minimum(n + 1, nb - 1)
        for j, b in enumerate((prev, n, nxt)):
            blk = pl.ds(pl.multiple_of(b * QBLOCK, QBLOCK), QBLOCK)
            dk_ref[blk, :] += dk3[j * QBLOCK:(j + 1) * QBLOCK, :]
            dv_ref[blk, :] += dv3[j * QBLOCK:(j + 1) * QBLOCK, :]

    strip = lambda col: pl.BlockSpec((t, LANES), lambda n, col=col: (0, col))
    rowspec = pl.BlockSpec((KV_HEADS, rows, 1), lambda n: (0, 0, 0))
    qspec = pl.BlockSpec((QBLOCK, ATTN_WIDTH), lambda n: (n, 0))
    kv_out = pl.BlockSpec((t, LANES), lambda n: (0, 0))
    return pl.pallas_call(
        body,
        out_shape=(jax.ShapeDtypeStruct((t, ATTN_WIDTH), F32), jax.ShapeDtypeStruct((t, LANES), F32),
                   jax.ShapeDtypeStruct((t, LANES), F32), jax.ShapeDtypeStruct((KV_HEADS, GQ, 1), F32)),
        grid=(nb,),
        in_specs=[qspec, strip(K_COL), strip(V_COL), rowspec, rowspec, qspec,
                  pl.BlockSpec((1, KV_HEADS, rows, 1), lambda n: (n, 0, 0, 0)), qspec],
        out_specs=(qspec, kv_out, kv_out, pl.BlockSpec((KV_HEADS, GQ, 1), lambda n: (0, 0, 0))),
        compiler_params=_cparams(("arbitrary",)), name=name)(proj, proj, proj, sink_rows, slope_rows, o, lse, do)


def _scan_tables(a_re, a_im, reverse):
    pw = [(a_re, a_im)]
    for _ in range(SUBLANES - 1):
        pr, pi = pw[-1]
        pw.append((pr * a_re - pi * a_im, pr * a_im + pi * a_re))
    rows = np.arange(SUBLANES)
    tabs = []
    for d in (1, 2, 4):
        mask = (rows <= SUBLANES - 1 - d) if reverse else (rows >= d)
        m = jnp.asarray(mask, F32)[:, None]
        tabs += [m * pw[d - 1][0][None, :], m * pw[d - 1][1][None, :]]
    order = (SUBLANES - 1 - rows) if reverse else rows
    tabs += [jnp.stack([pw[j][0] for j in order]), jnp.stack([pw[j][1] for j in order])]
    tab = jnp.stack(tabs)
    return tab.reshape(8, SUBLANES, N_STRIPS, STRIP_ST).transpose(2, 0, 1, 3)


def _scan(v, mi_re, mi_im, tab, mo_re, mo_im, reverse, name):
    t = v.shape[0]
    tc = _tile(t, 256)
    nc = t // tc
    nblk = tc // SUBLANES

    def body(v_ref, mir_ref, mii_ref, tab_ref, mor_ref, moi_ref, y_ref, xr_ref, xi_ref, carry_ref):
        @pl.when(pl.program_id(1) == 0)
        def _():
            carry_ref[...] = jnp.zeros_like(carry_ref)

        vb = v_ref[...].astype(BF16)
        xr_ref[...] = jnp.dot(vb, mir_ref[0], preferred_element_type=F32)
        xi_ref[...] = jnp.dot(vb, mii_ref[0], preferred_element_type=F32)

        def blk(i, carry):
            cr, ci = carry
            b = (nblk - 1 - i) if reverse else i
            r0 = pl.multiple_of(b * SUBLANES, SUBLANES)
            xr = xr_ref[pl.ds(r0, SUBLANES), :]
            xi = xi_ref[pl.ds(r0, SUBLANES), :]
            for j, d in enumerate((1, 2, 4)):
                tr_, ti_ = tab_ref[0, 2 * j], tab_ref[0, 2 * j + 1]
                sh = (SUBLANES - d) if reverse else d
                sr = pltpu.roll(xr, sh, 0)
                si = pltpu.roll(xi, sh, 0)
                xr, xi = xr + tr_ * sr - ti_ * si, xi + tr_ * si + ti_ * sr
            pr, pi = tab_ref[0, 6], tab_ref[0, 7]
            xr, xi = xr + pr * cr - pi * ci, xi + pr * ci + pi * cr
            xr_ref[pl.ds(r0, SUBLANES), :] = xr
            xi_ref[pl.ds(r0, SUBLANES), :] = xi
            edge = 0 if reverse else SUBLANES - 1
            return (jnp.broadcast_to(xr[edge:edge + 1, :], xr.shape),
                    jnp.broadcast_to(xi[edge:edge + 1, :], xi.shape))

        cr, ci = lax.fori_loop(0, nblk, blk, (carry_ref[0], carry_ref[1]))
        carry_ref[0] = cr
        carry_ref[1] = ci
        y_ref[...] = (jnp.dot(xr_ref[...].astype(BF16), mor_ref[0], preferred_element_type=F32)
                      + jnp.dot(xi_ref[...].astype(BF16), moi_ref[0], preferred_element_type=F32))

    tmap = (lambda s, c: (nc - 1 - c, s)) if reverse else (lambda s, c: (c, s))
    col0 = v.shape[1] // STRIP_IN - N_STRIPS
    vmap = lambda s, c: (tmap(s, c)[0], s + col0)
    smap3 = lambda s, c: (s, 0, 0)
    return pl.pallas_call(
        body,
        out_shape=(jax.ShapeDtypeStruct((t, SSM_WIDTH), F32),
                   jax.ShapeDtypeStruct((t, N_STRIPS * STRIP_ST), F32),
                   jax.ShapeDtypeStruct((t, N_STRIPS * STRIP_ST), F32)),
        grid=(N_STRIPS, nc),
        in_specs=[pl.BlockSpec((tc, STRIP_IN), vmap),
                  pl.BlockSpec((1, STRIP_IN, STRIP_ST), smap3), pl.BlockSpec((1, STRIP_IN, STRIP_ST), smap3),
                  pl.BlockSpec((1, 8, SUBLANES, STRIP_ST), lambda s, c: (s, 0, 0, 0)),
                  pl.BlockSpec((1, STRIP_ST, STRIP_IN), smap3), pl.BlockSpec((1, STRIP_ST, STRIP_IN), smap3)],
        out_specs=(pl.BlockSpec((tc, STRIP_IN), tmap), pl.BlockSpec((tc, STRIP_ST), tmap),
                   pl.BlockSpec((tc, STRIP_ST), tmap)),
        scratch_shapes=[pltpu.VMEM((2, SUBLANES, STRIP_ST), F32)],
        compiler_params=_cparams(("parallel", "arbitrary")), name=name)(v, mi_re, mi_im, tab, mo_re, mo_im)


def _scan_param_grads(v, dy, xr, xi, lr, li, reverse, name):
    t = v.shape[0]
    tc = _tile(t, 256)
    nc = t // tc
    hb = tc // SUBLANES

    def body(v_ref, dy_ref, xr_ref, xi_ref, lr_ref, li_ref, hr_ref, hi_ref,
             dmir_ref, dmii_ref, dmor_ref, dmoi_ref, da_ref):
        c = pl.program_id(1)

        @pl.when(c == 0)
        def _():
            for r in (dmir_ref, dmii_ref, dmor_ref, dmoi_ref, da_ref):
                r[...] = jnp.zeros_like(r)

        xrv, xiv, lrv, liv = xr_ref[...], xi_ref[...], lr_ref[...], li_ref[...]
        row = lax.broadcasted_iota(jnp.int32, xrv.shape, 0)
        if reverse:
            live = (c < nc - 1).astype(F32)
            edge_r, edge_i = hr_ref[0:1, :] * live, hi_ref[0:1, :] * live
            xpr = jnp.where(row == tc - 1, edge_r, pltpu.roll(xrv, tc - 1, 0))
            xpi = jnp.where(row == tc - 1, edge_i, pltpu.roll(xiv, tc - 1, 0))
        else:
            live = (c > 0).astype(F32)
            edge_r, edge_i = hr_ref[SUBLANES - 1:SUBLANES, :] * live, hi_ref[SUBLANES - 1:SUBLANES, :] * live
            xpr = jnp.where(row == 0, edge_r, pltpu.roll(xrv, 1, 0))
            xpi = jnp.where(row == 0, edge_i, pltpu.roll(xiv, 1, 0))
        da_ref[0, 0:1, :] += jnp.sum(xpr * lrv + xpi * liv, axis=0, keepdims=True)
        da_ref[0, 1:2, :] += jnp.sum(xpr * liv - xpi * lrv, axis=0, keepdims=True)
        tdims = (((0,), (0,)), ((), ()))
        vb, dyb = v_ref[...].astype(BF16), dy_ref[...].astype(BF16)
        dmir_ref[0] += lax.dot_general(vb, lrv.astype(BF16), tdims, preferred_element_type=F32)
        dmii_ref[0] += lax.dot_general(vb, liv.astype(BF16), tdims, preferred_element_type=F32)
        dmor_ref[0] += lax.dot_general(xrv.astype(BF16), dyb, tdims, preferred_element_type=F32)
        dmoi_ref[0] += lax.dot_general(xiv.astype(BF16), dyb, tdims, preferred_element_type=F32)

    tmap = lambda s, c: (c, s)
    if reverse:
        hmap = lambda s, c: (jnp.minimum((c + 1) * hb, t // SUBLANES - 1), s)
    else:
        hmap = lambda s, c: (jnp.maximum(c * hb - 1, 0), s)
    narrow = pl.BlockSpec((tc, STRIP_IN), tmap)
    col0 = v.shape[1] // STRIP_IN - N_STRIPS
    vspec = pl.BlockSpec((tc, STRIP_IN), lambda s, c: (c, s + col0))
    wide = pl.BlockSpec((tc, STRIP_ST), tmap)
    halo = pl.BlockSpec((SUBLANES, STRIP_ST), hmap)
    smap3 = lambda s, c: (s, 0, 0)
    return pl.pallas_call(
        body,
        out_shape=(jax.ShapeDtypeStruct((N_STRIPS, STRIP_IN, STRIP_ST), F32),
                   jax.ShapeDtypeStruct((N_STRIPS, STRIP_IN, STRIP_ST), F32),
                   jax.ShapeDtypeStruct((N_STRIPS, STRIP_ST, STRIP_IN), F32),
                   jax.ShapeDtypeStruct((N_STRIPS, STRIP_ST, STRIP_IN), F32),
                   jax.ShapeDtypeStruct((N_STRIPS, SUBLANES, STRIP_ST), F32)),
        grid=(N_STRIPS, nc),
        in_specs=[vspec, narrow, wide, wide, wide, wide, halo, halo],
        out_specs=(pl.BlockSpec((1, STRIP_IN, STRIP_ST), smap3), pl.BlockSpec((1, STRIP_IN, STRIP_ST), smap3),
                   pl.BlockSpec((1, STRIP_ST, STRIP_IN), smap3), pl.BlockSpec((1, STRIP_ST, STRIP_IN), smap3),
                   pl.BlockSpec((1, SUBLANES, STRIP_ST), smap3)),
        compiler_params=_cparams(("parallel", "arbitrary")), name=name)(v, dy, xr, xi, lr, li, xr, xi)


def _ssm_prep(lam_re, lam_im, log_dt, bt_re, bt_im, c_re, c_im):
    lr = jnp.minimum(lam_re, LAMBDA_RE_MAX)
    li = lam_im
    dt = jnp.exp(log_dt)[:, None]
    mag = jnp.exp(lr * dt)
    a_re = mag * jnp.cos(li * dt)
    a_im = mag * jnp.sin(li * dt)
    den = lr * lr + li * li
    coef_re = ((a_re - 1.0) * lr + a_im * li) / den
    coef_im = (a_im * lr - (a_re - 1.0) * li) / den
    bb_re = coef_re[:, None, :] * bt_re - coef_im[:, None, :] * bt_im
    bb_im = coef_re[:, None, :] * bt_im + coef_im[:, None, :] * bt_re
    eye = jnp.eye(SSM_GROUPS // N_STRIPS, dtype=F32)

    def strips(m):
        g, a, b = m.shape
        m4 = m.reshape(N_STRIPS, g // N_STRIPS, a, b)
        return jnp.einsum('sgab,gk->sgakb', m4, eye).reshape(N_STRIPS, g // N_STRIPS * a, g // N_STRIPS * b)

    mi_re = strips(bb_re)
    mi_im = strips(bb_im)
    mo_re = strips(jnp.swapaxes(c_re, 1, 2))
    mo_im = strips(-jnp.swapaxes(c_im, 1, 2))
    return a_re.reshape(-1), a_im.reshape(-1), mi_re, mi_im, mo_re, mo_im


def _gelu(x):
    c = math.sqrt(2.0 / math.pi)
    return 0.5 * x * (1.0 + jnp.tanh(c * (x + 0.044715 * x * x * x)))


def _gelu_grad(x):
    c = math.sqrt(2.0 / math.pi)
    th = jnp.tanh(c * (x + 0.044715 * x * x * x))
    return 0.5 * (1.0 + th) + 0.5 * x * (1.0 - th * th) * c * (1.0 + 3.0 * 0.044715 * x * x)


def _last_cols_specs(u, w, tr):
    half = w // 2
    first = (u.shape[1] - w) // half
    assert first * half == u.shape[1] - w
    return [pl.BlockSpec((tr, half), lambda i, k=k: (i, first + k)) for k in range(2)]


def _ssm_post_fwd(u, yf, yb, d, wglu, bglu, name):
    t, w = yf.shape
    tr = _row_tile(t)

    def body(ua_ref, ub_ref, yf_ref, yb_ref, d_ref, w_ref, b_ref, s_ref, y0_ref, z_ref):
        uv = jnp.concatenate([ua_ref[...], ub_ref[...]], axis=1)
        y0 = d_ref[...] * uv + yf_ref[...] + yb_ref[...]
        yg = _gelu(y0)
        z = jnp.dot(yg.astype(BF16), w_ref[...], preferred_element_type=F32) + b_ref[...]
        s_ref[...] = yg * _sigmoid(z)
        y0_ref[...] = y0
        z_ref[...] = z

    row = pl.BlockSpec((tr, w), lambda i: (i, 0))
    vec = pl.BlockSpec((1, w), lambda i: (0, 0))
    mat = pl.BlockSpec((w, w), lambda i: (0, 0))
    sh = jax.ShapeDtypeStruct((t, w), F32)
    return pl.pallas_call(body, out_shape=(sh, sh, sh), grid=(t // tr,),
                          in_specs=[*_last_cols_specs(u, w, tr), row, row, vec, mat, vec], out_specs=(row, row, row),
                          compiler_params=_cparams(("parallel",)), name=name)(u, u, yf, yb, d, wglu, bglu)


def _ssm_post_bwd(ds, y0, z, u, d, wglu, name):
    t, w = ds.shape
    tr = _row_tile(t)

    def body(ds_ref, y0_ref, z_ref, ua_ref, ub_ref, d_ref, w_ref, dy0_ref, dw_ref, db_ref, dd_ref):
        @pl.when(pl.program_id(0) == 0)
        def _():
            dw_ref[...] = jnp.zeros_like(dw_ref)
            db_ref[...] = jnp.zeros_like(db_ref)
            dd_ref[...] = jnp.zeros_like(dd_ref)

        y0 = y0_ref[...]
        yg = _gelu(y0)
        sg = _sigmoid(z_ref[...])
        dsv = ds_ref[...]
        dz = dsv * yg * sg * (1.0 - sg)
        dzb = dz.astype(BF16)
        dyg = dsv * sg + lax.dot_general(dzb, w_ref[...], (((1,), (1,)), ((), ())), preferred_element_type=F32)
        dy0 = dyg * _gelu_grad(y0)
        dy0_ref[...] = dy0
        dw_ref[...] += lax.dot_general(yg.astype(BF16), dzb, (((0,), (0,)), ((), ())), preferred_element_type=F32)
        db_ref[...] += jnp.sum(dz, axis=0, keepdims=True)
        uv = jnp.concatenate([ua_ref[...], ub_ref[...]], axis=1)
        dd_ref[...] += jnp.sum(dy0 * uv, axis=0, keepdims=True)

    row = pl.BlockSpec((tr, w), lambda i: (i, 0))
    vec = pl.BlockSpec((1, w), lambda i: (0, 0))
    mat = pl.BlockSpec((w, w), lambda i: (0, 0))
    return pl.pallas_call(
        body, out_shape=(jax.ShapeDtypeStruct((t, w), F32), jax.ShapeDtypeStruct((w, w), F32),
                         jax.ShapeDtypeStruct((1, w), F32), jax.ShapeDtypeStruct((1, w), F32)),
        grid=(t // tr,), in_specs=[row, row, row, *_last_cols_specs(u, w, tr), vec, mat],
        out_specs=(row, mat, vec, vec),
        compiler_params=_cparams(("arbitrary",)), name=name)(ds, y0, z, u, u, d, wglu)


def _du_combine(dy0, d, du_f, du_b, name):
    t, w = dy0.shape
    tr = _row_tile(t)

    def body(dy_ref, d_ref, a_ref, b_ref, o_ref):
        o_ref[...] = d_ref[...] * dy_ref[...] + a_ref[...] + b_ref[...]

    row = pl.BlockSpec((tr, w), lambda i: (i, 0))
    vec = pl.BlockSpec((1, w), lambda i: (0, 0))
    return pl.pallas_call(body, out_shape=jax.ShapeDtypeStruct((t, w), F32), grid=(t // tr,),
                          in_specs=[row, vec, row, row], out_specs=row, compiler_params=_cparams(("parallel",)),
                          name=name)(dy0, d, du_f, du_b)


def _ffn_fwd(x, g, wg, wu, wd, tag):
    xo, h, gate, up = _ffn_fwd_call(x, g, wg, wu, wd, f"{tag}_fwd")
    return xo, (h, gate, up)


def _ffn_bwd(dxo, dxo_b, x, g, wg, wu, wd, saved, tag):
    h, gate, up = saved
    dx, dx_b, dg, dgate, dup, act = _ffn_bwd_x_call(dxo, dxo_b, x, g, gate, up, wg, wu, wd, f"{tag}_bwd_x")
    dwg, dwu, dwd = _ffn_bwd_w_call(h, dxo_b, dgate, dup, act, f"{tag}_bwd_w")
    return dx, dx_b, dg, dwg, dwu, dwd


def _local_step(x, tgt, w, get_weights, put_grads, reduce_wide):
    t = x.shape[0]
    row = lambda a: a.reshape(1, -1)
    grads = {}

    w = dict(w)

    ssm_names = ['ssm_lambda_re', 'ssm_lambda_im', 'ssm_log_dt', 'ssm_b_re', 'ssm_b_im', 'ssm_c_re', 'ssm_c_im']
    tr3 = lambda m: jnp.swapaxes(m, 1, 2)
    fwd_ops, adj_ops, vjps = [], [], []
    for direction in range(2):
        rev = direction == 1
        prep, vjp = jax.vjp(_ssm_prep, *[w[n][direction] for n in ssm_names])
        a_re, a_im = prep[0], prep[1]
        mi_re, mi_im, mo_re, mo_im = (m.astype(BF16) for m in prep[2:])
        fwd_ops.append((mi_re, mi_im, _scan_tables(a_re, a_im, rev), mo_re, mo_im))
        adj_ops.append((tr3(mo_re), tr3(mo_im), _scan_tables(a_re, -a_im, not rev), tr3(mi_re), tr3(mi_im)))
        vjps.append(vjp)
    sink_rows = jnp.repeat(w['attn_sinks'].reshape(KV_HEADS, GQ), QBLOCK, axis=1)[..., None]
    slopes = jnp.asarray(2.0 ** (-8.0 * (np.arange(ATTN_HEADS) + 1) / ATTN_HEADS), F32)
    slope_rows = jnp.repeat(slopes.reshape(KV_HEADS, GQ), QBLOCK, axis=1)[..., None]
    prepared = sum(jnp.sum(op[:1, :1].astype(F32)) for ops in fwd_ops + adj_ops for op in ops) + sink_rows[0, 0, 0]

    w.update(get_weights('ffn1', prepared.reshape(1, 1)))
    x1, ffn1_saved = _ffn_fwd(x, w['norm_ffn1'], w['ffn1_w_gate'], w['ffn1_w_up'], w['ffn1_w_down'], "ffn1")
    w.update(get_weights('mix', x1))

    h2 = _rms_fwd(x1, w['norm_mix'], "mix_norm")
    proj = _mm(h2, w['w_in'], tb=True, name="in_proj")[0]
    u = proj

    attn, lse = _attn_fwd_proj(proj, sink_rows, slope_rows, "attn_fwd")

    ys, states = [], []
    for direction in range(2):
        y, xr, xi = _scan(u, *fwd_ops[direction], direction == 1, f"s5_fwd{direction}")
        ys.append(y)
        states.append((xr, xi))
    d_row = row(w['ssm_d'])
    s, y0, z = _ssm_post_fwd(u, ys[0], ys[1], d_row, w['ssm_glu_w'], row(w['ssm_glu_b']), "ssm_post")

    ma = _rms_fwd(attn, row(w['attn_out_norm']), "attn_out_norm")
    ms = _rms_fwd(s, row(w['ssm_out_norm']), "ssm_out_norm")
    mixed = jnp.concatenate([ma, ms], axis=-1)
    x2 = _mm(mixed, w['w_out'], res=x1, reduce_s=True, name="out_proj")

    w.update(get_weights('ffn2', x2))
    x3, ffn2_saved = _ffn_fwd(x2, w['norm_ffn2'], w['ffn2_w_gate'], w['ffn2_w_up'], w['ffn2_w_down'], "ffn2")

    loss_row, dx3, dx3_b, dgf = _loss_head(x3, row(w['final_norm']), tgt, "loss_head")
    loss = loss_row[0, 0]
    grads['final_norm'] = dgf.reshape(w['final_norm'].shape)

    dx2, dx2_b, dg, dwg, dwu, dwd = _ffn_bwd(dx3, dx3_b, x2, w['norm_ffn2'], w['ffn2_w_gate'], w['ffn2_w_up'],
                                             w['ffn2_w_down'], ffn2_saved, "ffn2")
    grads['norm_ffn2'] = dg
    sent = put_grads('ffn2', dict(ffn2_w_gate=dwg, ffn2_w_up=dwu, ffn2_w_down=dwd))

    dmixed = _mm(dx2_b, w['w_out'], tb=True, reduce_s=True, after=sent, name="out_proj_dx")
    dw_out = _mm(mixed, dx2_b, ta=True, out_dtype=BF16, name="out_proj_dw")[0]
    dattn, _, dga = _rms_bwd(attn, row(w['attn_out_norm']), dmixed[:, :ATTN_WIDTH], None, "attn_out_dnorm")
    ds, _, dgs = _rms_bwd(s, row(w['ssm_out_norm']), dmixed[:, ATTN_WIDTH:], None, "ssm_out_dnorm")
    grads.update(attn_out_norm=dga, ssm_out_norm=dgs)

    dy0, dwglu, dbglu, dd = _ssm_post_bwd(ds, y0, z, u, d_row, w['ssm_glu_w'], "ssm_post_bwd")
    grads['ssm_glu_b'] = dbglu
    grads['ssm_d'] = dd.reshape(w['ssm_d'].shape)
    dparams, du_dirs = [], []
    for direction in range(2):
        rev = direction == 1
        du_dir, lr, li = _scan(dy0, *adj_ops[direction], not rev, f"s5_adj{direction}")
        du_dirs.append(du_dir)
        xr, xi = states[direction]
        dmir, dmii, dmor, dmoi, da = _scan_param_grads(u, dy0, xr, xi, lr, li, rev, f"s5_pgrad{direction}")
        da_re = da[:, 0, :].reshape(-1)
        da_im = da[:, 1, :].reshape(-1)
        dparams.append(vjps[direction]((da_re, da_im, dmir, dmii, dmor, dmoi)))
    du = _du_combine(dy0, d_row, du_dirs[0], du_dirs[1], "ssm_du")
    for i, n in enumerate(ssm_names):
        grads[n] = jnp.stack([dparams[0][i], dparams[1][i]])
    wide_sum = reduce_wide(grads)

    dq, dk, dv, dsink = _attn_bwd_proj(proj, sink_rows, slope_rows, attn, lse, dattn, "attn_bwd")
    grads['attn_sinks'] = dsink.reshape(w['attn_sinks'].shape)
    dproj = jnp.concatenate([dq, dk, dv, du], axis=-1).astype(BF16)

    dw_in = _mm(dproj, h2, ta=True, out_dtype=BF16, after=wide_sum, name="in_proj_dw")[0]
    sent = put_grads('mix', dict(w_in=dw_in, ssm_glu_w=dwglu, w_out=dw_out))
    dh2 = _mm(dproj, w['w_in'], reduce_s=True, after=sent, name="in_proj_dx")
    dx1, dx1_b, dgm = _rms_bwd(x1, w['norm_mix'], dh2, dx2, "mix_dnorm")
    grads['norm_mix'] = dgm

    dx0, _, dg, dwg, dwu, dwd = _ffn_bwd(dx1, dx1_b, x, w['norm_ffn1'], w['ffn1_w_gate'], w['ffn1_w_up'],
                                         w['ffn1_w_down'], ffn1_saved, "ffn1")
    grads['norm_ffn1'] = dg
    put_grads('ffn1', dict(ffn1_w_gate=dwg, ffn1_w_up=dwu, ffn1_w_down=dwd))
    return loss, dx0, grads, wide_sum


HBM_SPEC = pl.BlockSpec(memory_space=pl.ANY)


def _chip_peers(x, y):
    return [(1 - x, y), (x, 1 - y), (1 - x, 1 - y)]


HBM_ONLY = pl.BlockSpec(memory_space=pltpu.HBM)
SEM_SPEC = pl.BlockSpec(memory_space=pltpu.SEMAPHORE)
EFFECT = pltpu.SideEffectType.DATAFLOW_SIDE_EFFECTING


def _place_own(src, slot, name):
    r, c = src.shape
    tr = r // 2

    def body(slot_ref, s_ref, o_ref):
        o_ref[0] = s_ref[...]

    return pl.pallas_call(
        body, out_shape=jax.ShapeDtypeStruct((N_CHIPS, r, c), src.dtype),
        grid_spec=pltpu.PrefetchScalarGridSpec(
            num_scalar_prefetch=1, grid=(2,), in_specs=[pl.BlockSpec((tr, c), lambda i, s: (i, 0))],
            out_specs=pl.BlockSpec((1, tr, c), lambda i, s: (s[0], i, 0))),
        compiler_params=_cparams(("parallel",)), name=name)(slot, src)


def _chip_copies(srcs, lands, send_sems, recv_sems, scatter, landed):
    x, y, c = lax.axis_index("x"), lax.axis_index("y"), lax.axis_index("c")
    me = 2 * x + y
    out = []
    for i in range(len(srcs)):
        for j, (px, py) in enumerate(_chip_peers(x, y)):
            p = 2 * px + py
            slot = p if landed else me
            if scatter:
                src, dst = srcs[i].at[p], lands[i].at[slot]
            else:
                rows = _core_half(srcs[i].shape[0], c)
                src, dst = srcs[i].at[rows], lands[i].at[slot, rows]
            out.append(pltpu.make_async_remote_copy(src, dst, send_sems.at[3 * i + j], recv_sems.at[3 * i + j],
                                                    device_id=(px, py, c), device_id_type=MESH))
    return out


def _core_half(nrows, c):
    half = nrows // 2
    return pl.ds(pl.multiple_of(c * half, 16), half)


def _sibling_forward(lands, name):
    n = len(lands)

    def body(*refs):
        bufs = refs[n:2 * n]
        send_sems, recv_sems = refs[2 * n:]
        x, y, c = lax.axis_index("x"), lax.axis_index("y"), lax.axis_index("c")
        mine = [_core_half(b.shape[1], c) for b in bufs]
        theirs = [_core_half(b.shape[1], 1 - c) for b in bufs]
        chips = [2 * px + py for px, py in _chip_peers(x, y)]
        cps = [pltpu.make_async_remote_copy(bufs[i].at[p, mine[i]], bufs[i].at[p, mine[i]], send_sems.at[3 * i + j],
                                            recv_sems.at[3 * i + j], device_id=(x, y, 1 - c), device_id_type=MESH)
               for i in range(n) for j, p in enumerate(chips)]
        for cp in cps:
            cp.start()
        for i in range(n):
            for j, p in enumerate(chips):
                pltpu.make_async_remote_copy(bufs[i].at[p, mine[i]], bufs[i].at[p, theirs[i]], send_sems.at[3 * i + j],
                                             recv_sems.at[3 * i + j], device_id=(x, y, 1 - c),
                                             device_id_type=MESH).wait()

    return pl.pallas_call(
        body, out_shape=[jax.ShapeDtypeStruct(a.shape, a.dtype) for a in lands],
        in_specs=[HBM_SPEC] * n, out_specs=[HBM_SPEC] * n, input_output_aliases={k: k for k in range(n)},
        scratch_shapes=[pltpu.SemaphoreType.DMA((3 * n,)), pltpu.SemaphoreType.DMA((3 * n,))],
        name=name)(*lands)


def _exchange_start(groups, scatter, name, after=None):
    sizes = [len(srcs) for srcs, _ in groups]
    flat_src = [a for srcs, _ in groups for a in srcs]
    flat_land = [a for _, lands in groups for a in lands]
    n = len(flat_src)
    ng = len(groups)

    def body(*refs):
        src_refs, land_refs = refs[:n], refs[n:2 * n]
        n_in = 2 * n + (after is not None)
        sems = refs[n_in:n_in + 2 * ng]
        token_ref = refs[-1]
        off = 0
        for gi, sz in enumerate(sizes):
            for cp in _chip_copies(src_refs[off:off + sz], land_refs[off:off + sz], sems[2 * gi], sems[2 * gi + 1],
                                   scatter, landed=False):
                cp.start()
            off += sz
        token_ref[...] = jnp.zeros_like(token_ref)

    sem_shapes = []
    for sz in sizes:
        sem_shapes += [pltpu.SemaphoreType.DMA((3 * sz,)), pltpu.SemaphoreType.DMA((3 * sz,))]
    hbm = lambda a: pltpu.HBM(a.shape, a.dtype)
    res = pl.pallas_call(
        body, name=name,
        out_shape=(tuple(sem_shapes) + tuple(hbm(a) for a in flat_src) + tuple(hbm(a) for a in flat_land)
                   + (jax.ShapeDtypeStruct((SUBLANES, LANES), F32),)),
        in_specs=[HBM_ONLY] * (2 * n) + [HBM_SPEC] * (after is not None),
        out_specs=tuple([SEM_SPEC] * (2 * ng) + [HBM_ONLY] * (2 * n) + [pl.BlockSpec(memory_space=pltpu.VMEM)]),
        input_output_aliases={k: 2 * ng + k for k in range(2 * n)},
        compiler_params=pltpu.CompilerParams(has_side_effects=EFFECT),
    )(*[pltpu.with_memory_space_constraint(a, pltpu.HBM) for a in flat_src + flat_land],
      *([after] if after is not None else []))
    sems, thru_src, thru_land = res[:2 * ng], res[2 * ng:2 * ng + n], res[2 * ng + n:2 * ng + 2 * n]
    out, off = [], 0
    for gi, sz in enumerate(sizes):
        out.append((sems[2 * gi], sems[2 * gi + 1], list(thru_src[off:off + sz]), list(thru_land[off:off + sz])))
        off += sz
    return out, res[-1]


def _exchange_wait(started, after, scatter, name):
    send_sems, recv_sems, srcs, lands = started
    n = len(srcs)

    def body(*refs):
        src_refs, land_refs = refs[:n], refs[n:2 * n]
        send_ref, recv_ref = refs[2 * n], refs[2 * n + 1]
        for cp in _chip_copies(src_refs, land_refs, send_ref, recv_ref, scatter, landed=True):
            cp.wait_send()
            cp.wait_recv()

    hbm = lambda a: pltpu.HBM(a.shape, a.dtype)
    res = pl.pallas_call(
        body, name=name, out_shape=tuple(hbm(a) for a in srcs) + tuple(hbm(a) for a in lands),
        in_specs=[HBM_ONLY] * (2 * n) + [SEM_SPEC, SEM_SPEC, HBM_SPEC], out_specs=tuple([HBM_ONLY] * (2 * n)),
        input_output_aliases={k: k for k in range(2 * n)},
        compiler_params=pltpu.CompilerParams(has_side_effects=EFFECT),
    )(*srcs, *lands, send_sems, recv_sems, after)
    return list(res[:n]), list(res[n:])


def _half_swap(parts, name):
    n = len(parts)

    def body(*refs):
        ins, outs = refs[:n], refs[n:2 * n]
        send_sems, recv_sems = refs[2 * n:]
        x, y, c = lax.axis_index("x"), lax.axis_index("y"), lax.axis_index("c")
        cps = [pltpu.make_async_remote_copy(ins[i].at[k, _core_half(ins[i].shape[1], 1 - c)], outs[i].at[k],
                                            send_sems.at[N_CHIPS * i + k], recv_sems.at[N_CHIPS * i + k],
                                            device_id=(x, y, 1 - c), device_id_type=MESH)
               for i in range(n) for k in range(N_CHIPS)]
        for cp in cps:
            cp.start()
        for cp in cps:
            cp.wait()

    return pl.pallas_call(
        body, out_shape=[jax.ShapeDtypeStruct((N_CHIPS, p.shape[1] // 2, p.shape[2]), p.dtype) for p in parts],
        in_specs=[HBM_SPEC] * n, out_specs=[HBM_SPEC] * n,
        scratch_shapes=[pltpu.SemaphoreType.DMA((N_CHIPS * n,)), pltpu.SemaphoreType.DMA((N_CHIPS * n,))],
        name=name)(*parts)


def _half_add(parts, sib, slots, name):
    na = len(parts)
    _, r, c = parts[0].shape
    hr = r // 2
    tr = _row_tile(hr, 512)
    nt = hr // tr

    def body(slot_ref, *refs):
        for a in range(na):
            refs[2 * na + a][...] = (refs[2 * a][...].astype(F32) + refs[2 * a + 1][...].astype(F32)).astype(BF16)

    mine = pl.BlockSpec((1, tr, c), lambda k, i, s: (k, i + s[4] * nt, 0))
    half = pl.BlockSpec((1, tr, c), lambda k, i, s: (k, i, 0))
    args = [a for p, sb in zip(parts, sib) for a in (p, sb)]
    return pl.pallas_call(
        body, out_shape=[jax.ShapeDtypeStruct((N_CHIPS, hr, c), BF16)] * na,
        grid_spec=pltpu.PrefetchScalarGridSpec(
            num_scalar_prefetch=1, grid=(N_CHIPS, nt), in_specs=[mine, half] * na, out_specs=[half] * na),
        compiler_params=_cparams(("parallel", "parallel")), name=name)(slots, *args)


def _half_forward(arrs, name):
    n = len(arrs)

    def body(*refs):
        bufs = refs[n:2 * n]
        send_sems, recv_sems = refs[2 * n:]
        x, y, c = lax.axis_index("x"), lax.axis_index("y"), lax.axis_index("c")
        cps = [pltpu.make_async_remote_copy(b.at[_core_half(b.shape[0], c)], b.at[_core_half(b.shape[0], c)],
                                            send_sems.at[i], recv_sems.at[i], device_id=(x, y, 1 - c),
                                            device_id_type=MESH) for i, b in enumerate(bufs)]
        for cp in cps:
            cp.start()
        for i, b in enumerate(bufs):
            pltpu.make_async_remote_copy(b.at[_core_half(b.shape[0], c)], b.at[_core_half(b.shape[0], 1 - c)],
                                         send_sems.at[i], recv_sems.at[i], device_id=(x, y, 1 - c),
                                         device_id_type=MESH).wait()

    return pl.pallas_call(
        body, out_shape=[jax.ShapeDtypeStruct(a.shape, a.dtype) for a in arrs],
        in_specs=[HBM_SPEC] * n, out_specs=[HBM_SPEC] * n, input_output_aliases={k: k for k in range(n)},
        scratch_shapes=[pltpu.SemaphoreType.DMA((n,)), pltpu.SemaphoreType.DMA((n,))],
        name=name)(*arrs)


def _small_exchange(smalls, name):
    nsm = len(smalls)
    rels = [(fx, fy, fc) for fx in (0, 1) for fy in (0, 1) for fc in (0, 1)][1:]

    def body(*refs):
        sins, souts = refs[:nsm], refs[nsm:2 * nsm]
        ssend, srecv, slocal = refs[2 * nsm:]
        x, y, c = lax.axis_index("x"), lax.axis_index("y"), lax.axis_index("c")
        lin = 4 * x + 2 * y + c
        local = [pltpu.make_async_copy(sins[i], souts[i].at[lin], slocal.at[i]) for i in range(nsm)]
        for cp in local:
            cp.start()
        for i in range(nsm):
            for j, (fx, fy, fc) in enumerate(rels):
                pltpu.make_async_remote_copy(sins[i], souts[i].at[lin], ssend.at[i, j], srecv.at[i, j],
                                             device_id=(x ^ fx, y ^ fy, c ^ fc), device_id_type=MESH).start()
        for i in range(nsm):
            for j, (fx, fy, fc) in enumerate(rels):
                src = 4 * (x ^ fx) + 2 * (y ^ fy) + (c ^ fc)
                pltpu.make_async_remote_copy(sins[i], souts[i].at[src], ssend.at[i, j], srecv.at[i, j],
                                             device_id=(x ^ fx, y ^ fy, c ^ fc), device_id_type=MESH).wait()
        for cp in local:
            cp.wait()

    return pl.pallas_call(
        body, out_shape=[jax.ShapeDtypeStruct((N_DEV,) + s.shape, s.dtype) for s in smalls],
        in_specs=[HBM_SPEC] * nsm, out_specs=[HBM_SPEC] * nsm,
        scratch_shapes=[pltpu.SemaphoreType.DMA((nsm, 7)), pltpu.SemaphoreType.DMA((nsm, 7)),
                        pltpu.SemaphoreType.DMA((nsm,))],
        name=name)(*smalls)


def _sibling_swap(arrs, name):
    nw = len(arrs)

    def body(*refs):
        ins, outs = refs[:nw], refs[nw:2 * nw]
        send_sems, recv_sems = refs[2 * nw:]
        x, y, c = lax.axis_index("x"), lax.axis_index("y"), lax.axis_index("c")
        cps = [pltpu.make_async_remote_copy(ins[i], outs[i], send_sems.at[i], recv_sems.at[i],
                                            device_id=(x, y, 1 - c), device_id_type=MESH) for i in range(nw)]
        for cp in cps:
            cp.start()
        for cp in cps:
            cp.wait()

    return pl.pallas_call(
        body, out_shape=[jax.ShapeDtypeStruct(a.shape, a.dtype) for a in arrs],
        in_specs=[HBM_SPEC] * nw, out_specs=[HBM_SPEC] * nw,
        scratch_shapes=[pltpu.SemaphoreType.DMA((nw,)), pltpu.SemaphoreType.DMA((nw,))],
        name=name)(*arrs)


def _sum_parts(parts, recv, slots, name):
    na = len(parts)
    _, r, c = parts[0].shape
    tr = _row_tile(r, 192)

    def body(slot_ref, *refs):
        for a in range(na):
            own_ref, r0_ref, r1_ref, r2_ref = refs[4 * a:4 * a + 4]
            refs[4 * na + a][...] = ((own_ref[0].astype(F32) + r0_ref[0].astype(F32))
                                     + (r1_ref[0].astype(F32) + r2_ref[0].astype(F32)))

    blk = lambda k: pl.BlockSpec((1, tr, c), lambda i, s, k=k: (s[k], i, 0))
    out_blk = pl.BlockSpec((tr, c), lambda i, s: (i + s[4] * (r // tr), 0))
    args = [a for p, rv in zip(parts, recv) for a in (p, rv, rv, rv)]
    return pl.pallas_call(
        body, out_shape=[jax.ShapeDtypeStruct((2 * r, c), F32)] * na,
        grid_spec=pltpu.PrefetchScalarGridSpec(
            num_scalar_prefetch=1, grid=(r // tr,), in_specs=[blk(0), blk(1), blk(2), blk(3)] * na,
            out_specs=[out_blk] * na),
        compiler_params=_cparams(("parallel",)), name=name)(slots, *args)


def _small_allreduce(packed, name):
    rows = packed.shape[0]
    pr = rows // N_DEV
    rels = [(fx, fy, fc) for fx in (0, 1) for fy in (0, 1) for fc in (0, 1)][1:]

    def body(in_ref, out_ref, recv_ref, send1, recv1, send2, recv2):
        x, y, c = lax.axis_index("x"), lax.axis_index("y"), lax.axis_index("c")
        lin = 4 * x + 2 * y + c
        piece = lambda ref, k: ref.at[pl.ds(pl.multiple_of(k * pr, pr), pr), :]
        peers = [((x ^ fx, y ^ fy, c ^ fc), 4 * (x ^ fx) + 2 * (y ^ fy) + (c ^ fc)) for fx, fy, fc in rels]
        for j, (dev, plin) in enumerate(peers):
            pltpu.make_async_remote_copy(piece(in_ref, plin), recv_ref.at[lin], send1.at[j], recv1.at[j],
                                         device_id=dev, device_id_type=MESH).start()
        recv_ref[lin] = piece(in_ref, lin)[...]
        for j, (dev, plin) in enumerate(peers):
            pltpu.make_async_remote_copy(piece(in_ref, plin), recv_ref.at[plin], send1.at[j], recv1.at[j],
                                         device_id=dev, device_id_type=MESH).wait()
        acc = recv_ref[0]
        for k in range(1, N_DEV):
            acc = acc + recv_ref[k]
        piece(out_ref, lin)[...] = acc
        for j, (dev, plin) in enumerate(peers):
            pltpu.make_async_remote_copy(piece(out_ref, lin), piece(out_ref, lin), send2.at[j], recv2.at[j],
                                         device_id=dev, device_id_type=MESH).start()
        for j, (dev, plin) in enumerate(peers):
            pltpu.make_async_remote_copy(piece(out_ref, lin), piece(out_ref, plin), send2.at[j], recv2.at[j],
                                         device_id=dev, device_id_type=MESH).wait()

    vm = pl.BlockSpec(memory_space=pltpu.VMEM)
    return pl.pallas_call(
        body, out_shape=jax.ShapeDtypeStruct(packed.shape, F32), in_specs=[vm], out_specs=vm,
        scratch_shapes=[pltpu.VMEM((N_DEV, pr, LANES), F32)] + [pltpu.SemaphoreType.DMA((7,))] * 4,
        compiler_params=pltpu.CompilerParams(vmem_limit_bytes=VMEM_LIMIT), name=name)(packed)


def _adamw_math(w, m, v, g):
    nm = ADAM_B1 * m + (1.0 - ADAM_B1) * g
    nv = ADAM_B2 * v + (1.0 - ADAM_B2) * (g * g)
    m_hat = nm * (1.0 / (1.0 - ADAM_B1 ** ADAM_STEP))
    v_hat = nv * (1.0 / (1.0 - ADAM_B2 ** ADAM_STEP))
    return -ADAM_LR * (m_hat / (jnp.sqrt(v_hat) + ADAM_EPS) + ADAM_WD * w), nm, nv


def _adamw(ws, ms, vs, gs, name):
    na = len(ws)
    r, c = ws[0].shape
    tr = _row_tile(r)

    def body(*refs):
        for a in range(na):
            w_ref, m_ref, v_ref, g_ref = refs[4 * a:4 * a + 4]
            d_ref, nm_ref, nv_ref = refs[4 * na + 3 * a:4 * na + 3 * a + 3]
            d_ref[...], nm_ref[...], nv_ref[...] = _adamw_math(w_ref[...], m_ref[...], v_ref[...], g_ref[...])

    blk = pl.BlockSpec((tr, c), lambda i: (i, 0))
    sh = jax.ShapeDtypeStruct((r, c), F32)
    args = [a for group in zip(ws, ms, vs, gs) for a in group]
    res = pl.pallas_call(body, out_shape=[sh] * (3 * na), grid=(r // tr,), in_specs=[blk] * (4 * na),
                         out_specs=[blk] * (3 * na), compiler_params=_cparams(("parallel",)), name=name)(*args)
    return [tuple(res[3 * a:3 * a + 3]) for a in range(na)]


def _adamw_small(ws, ms, vs, alls, split, name):
    n = len(ws)
    lead = split if split is not None else ()
    nl = len(lead)
    nslots = alls[0].shape[0]

    def blocks(shape):
        if split is None:
            return tuple(shape), (lambda *g: (0,) * len(shape))
        blk = (shape[0], shape[1] // lead[0], shape[2] // lead[1]) + tuple(shape[3:])
        return blk, (lambda *g: (0, g[0], g[1]) + (0,) * (len(shape) - 3))

    def body(*refs):
        w_refs, m_refs, v_refs, a_refs = (refs[k * n:(k + 1) * n] for k in range(4))
        g_refs, d_refs, nm_refs, nv_refs = (refs[(4 + k) * n:(5 + k) * n] for k in range(4))
        k = pl.program_id(nl)
        for i in range(n):
            @pl.when(k == 0)
            def _(i=i):
                g_refs[i][...] = a_refs[i][0]

            @pl.when(k > 0)
            def _(i=i):
                g_refs[i][...] += a_refs[i][0]

            @pl.when(k == nslots - 1)
            def _(i=i):
                d_refs[i][...], nm_refs[i][...], nv_refs[i][...] = _adamw_math(
                    w_refs[i][...], m_refs[i][...], v_refs[i][...], g_refs[i][...])

    specs, aspecs, shapes = [], [], []
    for wa in ws:
        blk, imap = blocks(wa.shape)
        specs.append(pl.BlockSpec(blk, imap))
        aspecs.append(pl.BlockSpec((1,) + blk, (lambda *g, imap=imap: (g[nl],) + imap(*g))))
        shapes.append(jax.ShapeDtypeStruct(wa.shape, F32))
    res = pl.pallas_call(
        body, out_shape=shapes * 4, grid=tuple(lead) + (nslots,), in_specs=specs * 3 + aspecs,
        out_specs=specs * 4, compiler_params=_cparams(("parallel",) * nl + ("arbitrary",)),
        name=name)(*ws, *ms, *vs, *alls)
    return res[:n], res[n:2 * n], res[2 * n:3 * n], res[3 * n:]


def kernel(x, norm_ffn1, ffn1_w_gate, ffn1_w_up, ffn1_w_down, norm_mix, w_in, attn_sinks, ssm_lambda_re, ssm_lambda_im, ssm_log_dt, ssm_b_re, ssm_b_im, ssm_c_re, ssm_c_im, ssm_d, ssm_glu_w, ssm_glu_b, attn_out_norm, ssm_out_norm, w_out, norm_ffn2, ffn2_w_gate, ffn2_w_up, ffn2_w_down, final_norm, loss_target, m_norm_ffn1, m_ffn1_w_gate, m_ffn1_w_up, m_ffn1_w_down, m_norm_mix, m_w_in, m_attn_sinks, m_ssm_lambda_re, m_ssm_lambda_im, m_ssm_log_dt, m_ssm_b_re, m_ssm_b_im, m_ssm_c_re, m_ssm_c_im, m_ssm_d, m_ssm_glu_w, m_ssm_glu_b, m_attn_out_norm, m_ssm_out_norm, m_w_out, m_norm_ffn2, m_ffn2_w_gate, m_ffn2_w_up, m_ffn2_w_down, m_final_norm, v_norm_ffn1, v_ffn1_w_gate, v_ffn1_w_up, v_ffn1_w_down, v_norm_mix, v_w_in, v_attn_sinks, v_ssm_lambda_re, v_ssm_lambda_im, v_ssm_log_dt, v_ssm_b_re, v_ssm_b_im, v_ssm_c_re, v_ssm_c_im, v_ssm_d, v_ssm_glu_w, v_ssm_glu_b, v_attn_out_norm, v_ssm_out_norm, v_w_out, v_norm_ffn2, v_ffn2_w_gate, v_ffn2_w_up, v_ffn2_w_down, v_final_norm):
    given = dict(locals())
    wts = {n: given[n] for n in WEIGHTS}

    order = [g for g in GROUPS]
    cx, cy = lax.axis_index("x"), lax.axis_index("y")
    slots = jnp.stack([2 * cx + cy, 2 * (1 - cx) + cy, 2 * cx + 1 - cy, 2 * (1 - cx) + 1 - cy,
                       lax.axis_index("c")]).astype(jnp.int32)
    def view(a, n):
        if n in TRANSPOSED:
            return jnp.swapaxes(a[0], 0, 1)
        if n in BIG:
            return a[0]
        if n in ('ssm_b_re', 'ssm_b_im'):
            return jnp.swapaxes(a, -1, -2)
        return a.reshape(1, -1) if a.ndim == 1 else a

    def unview(a, n):
        if n in TRANSPOSED:
            return jnp.swapaxes(a, 0, 1)[None]
        if n in ('ssm_b_re', 'ssm_b_im'):
            return jnp.swapaxes(a, -1, -2)
        return a.reshape(wts[n].shape)

    started, gather_token = {}, None
    for g in order:
        shards = [view(wts[n], n).astype(BF16) for n in GROUPS[g]]
        placed = [_place_own(s, slots, f"weights_place_{n}") for n, s in zip(GROUPS[g], shards)]
        st, gather_token = _exchange_start([(shards, placed)], False, f"weights_start_{g}", after=gather_token)
        started[g] = st[0]

    def get_weights(group, after):
        if group == order[0]:
            after = after + gather_token[:1, :1]
        _, lands = _exchange_wait(started[group], after, False, f"weights_wait_{group}")
        lands = _sibling_forward(lands, f"weights_forward_{group}")
        out = dict(zip(GROUPS[group], lands))
        for n in ('w_in', 'ssm_glu_w', 'w_out'):
            if n in out:
                out[n] = out[n].reshape(-1, out[n].shape[-1])
        return out

    sent, tokens = {}, {}

    def put_grads(group, gd):
        parts = []
        for n in GROUPS[group]:
            g = gd[n]
            if g.ndim == 2:
                g = g.reshape(N_CHIPS, g.shape[0] // N_CHIPS, g.shape[1])
            parts.append(g.astype(BF16))
        sib = _half_swap(parts, f"grads_half_swap_{group}")
        same = len({p.shape for p in parts}) == 1
        batches = [list(range(len(parts)))] if same else [[i] for i in range(len(parts))]
        halves = [None] * len(parts)
        for b in batches:
            res = _half_add([parts[i] for i in b], [sib[i] for i in b], slots, f"grads_half_add_{GROUPS[group][b[0]]}")
            for i, h in zip(b, res):
                halves[i] = h
        parts = halves
        lands = [lax.empty(p.shape, p.dtype) for p in parts]
        started_g, tokens[group] = _exchange_start([(parts, lands)], True, f"grads_start_{group}")
        sent[group] = started_g[0]
        return tokens[group]

    w = {n: (wts[n][0] if wts[n].ndim > 1 else wts[n]) for n in SMALL}
    w['norm_ffn1'], w['norm_mix'], w['norm_ffn2'] = wts['norm_ffn1'], wts['norm_mix'], wts['norm_ffn2']
    w['ssm_b_re'], w['ssm_b_im'] = view(wts['ssm_b_re'], 'ssm_b_re')[0], view(wts['ssm_b_im'], 'ssm_b_im')[0]
    w['ssm_log_dt'] = w['ssm_log_dt'] + gather_token[0, 0]
    wide =['ssm_b_re', 'ssm_b_im', 'ssm_c_re', 'ssm_c_im']

    def reduce_wide(gd):
        packed = jnp.concatenate([gd[n].reshape(-1, LANES) for n in wide])
        return _small_allreduce(packed, "small_grads_allreduce")

    loss, dx, grads, wide_sum = _local_step(x[0], loss_target[0], w, get_weights, put_grads, reduce_wide)
    loss = lax.psum(loss, ("x", "y", "c"))

    out_g, out_d, out_m, out_v = {}, {}, {}, {}

    def finish(group, after):
        names = GROUPS[group]
        parts, recv = _exchange_wait(sent[group], after, True, f"grads_wait_{group}")
        same = len({p.shape for p in parts}) == 1
        batches = [list(range(len(names)))] if same else [[i] for i in range(len(names))]
        sums = [None] * len(names)
        for b in batches:
            res = _sum_parts([parts[i] for i in b], [recv[i] for i in b], slots, f"grad_sum_{names[b[0]]}")
            for i, sm in zip(b, res):
                sums[i] = sm
        full = _half_forward(sums, f"grad_half_forward_{group}")
        for b in batches:
            res = _adamw([view(wts[names[i]], names[i]) for i in b], [view(given['m_' + names[i]], names[i]) for i in b],
                         [view(given['v_' + names[i]], names[i]) for i in b], [full[i] for i in b],
                         f"adamw_{names[b[0]]}")
            for i, (d, nm, nv) in zip(b, res):
                n = names[i]
                out_g[n], out_d[n], out_m[n], out_v[n] = (unview(a, n) for a in (full[i], d, nm, nv))
        return nv

    done = finish('ffn2', tokens['ffn1'])
    done = finish('mix', done)

    nat = {n: view(wts[n], n).shape for n in SMALL}
    narrow = [n for n in SMALL if n not in wide]
    alls = _small_exchange([grads[n].reshape(nat[n]) for n in narrow], "small_grads_allgather")
    rows = wide_sum.shape[0] // len(wide)
    wide_g = [wide_sum[i * rows:(i + 1) * rows].reshape((1,) + nat[n]) for i, n in enumerate(wide)]
    for group, gs, split, tag in ((narrow, alls, None, "adamw_small"), (wide, wide_g, (2, 4), "adamw_ssm_bc")):
        res = _adamw_small([view(wts[n], n) for n in group], [view(given['m_' + n], n) for n in group],
                           [view(given['v_' + n], n) for n in group], gs, split, tag)
        for dst, vals in zip((out_g, out_d, out_m, out_v), res):
            for n, a in zip(group, vals):
                dst[n] = unview(a, n)

    finish('ffn1', out_v['norm_ffn1'][:, :1] + out_v['ssm_c_re'].reshape(1, -1)[:, :1] + done[:1, :1] + loss)

    return (loss, dx[None], *[out_g[n] for n in WEIGHTS], *[out_d[n] for n in WEIGHTS],
            *[out_m[n] for n in WEIGHTS], *[out_v[n] for n in WEIGHTS])
```

```python
import functools
import math

import numpy as np
import jax
import jax.numpy as jnp
from jax import lax
from jax.experimental import pallas as pl
from jax.experimental.pallas import tpu as pltpu

F32 = jnp.float32
BF16 = jnp.bfloat16
MESH = pl.DeviceIdType.MESH

EPS = 1e-6
NEG_INF = -1e30
LAMBDA_RE_MAX = -1e-4
ATTN_HEADS = 8
KV_HEADS = 2
GQ = ATTN_HEADS // KV_HEADS
HEAD_DIM = 64
ATTN_WIDTH = 512
KV_WIDTH = 128
WINDOW = 128
QBLOCK = 128
SSM_WIDTH = 512
SSM_GROUPS = 32
SSM_CH = 16
SSM_STATE = 64
N_STRIPS = 4
STRIP_IN = SSM_WIDTH // N_STRIPS
STRIP_ST = SSM_GROUPS * SSM_STATE // N_STRIPS
SUBLANES = 8
LANES = 128
N_CHIPS = 4
N_DEV = 8

ADAM_LR = 0.001
ADAM_B1 = 0.9
ADAM_B2 = 0.999
ADAM_EPS = 1e-08
ADAM_WD = 0.01
ADAM_STEP = 10

VMEM_LIMIT = 48 * 1024 * 1024

WEIGHTS = ['norm_ffn1', 'ffn1_w_gate', 'ffn1_w_up', 'ffn1_w_down', 'norm_mix', 'w_in', 'attn_sinks',
           'ssm_lambda_re', 'ssm_lambda_im', 'ssm_log_dt', 'ssm_b_re', 'ssm_b_im', 'ssm_c_re', 'ssm_c_im',
           'ssm_d', 'ssm_glu_w', 'ssm_glu_b', 'attn_out_norm', 'ssm_out_norm', 'w_out', 'norm_ffn2',
           'ffn2_w_gate', 'ffn2_w_up', 'ffn2_w_down', 'final_norm']
BIG = ['ffn1_w_gate', 'ffn1_w_up', 'ffn1_w_down', 'w_in', 'ssm_glu_w', 'w_out',
       'ffn2_w_gate', 'ffn2_w_up', 'ffn2_w_down']
SMALL = [n for n in WEIGHTS if n not in BIG]
TRANSPOSED = ['ffn1_w_gate', 'ffn1_w_up', 'w_in', 'ffn2_w_gate', 'ffn2_w_up']
GROUPS = {'ffn1': ['ffn1_w_gate', 'ffn1_w_up', 'ffn1_w_down'],
          'mix': ['w_in', 'ssm_glu_w', 'w_out'],
          'ffn2': ['ffn2_w_gate', 'ffn2_w_up', 'ffn2_w_down']}


def _cparams(sem=None):
    return pltpu.CompilerParams(dimension_semantics=sem, vmem_limit_bytes=VMEM_LIMIT)


def _tile(n, pref):
    if n <= pref:
        return n
    for t in (pref, pref // 2, pref // 4):
        if t % LANES == 0 and n % t == 0:
            return t
    return n


def _sigmoid(x):
    return 1.0 / (1.0 + jnp.exp(-x))


def _mm(a, b, *, ta=False, tb=False, reduce_s=False, res=None, scale=1.0, out_dtype=F32, after=None, name):
    a3 = a if a.ndim == 3 else a[None]
    b3 = b if b.ndim == 3 else b[None]
    sa, sb = a3.shape[0], b3.shape[0]
    ns = max(sa, sb)
    (kk, m) = a3.shape[1:] if ta else a3.shape[1:][::-1]
    (n, kb) = b3.shape[1:] if tb else b3.shape[1:][::-1]
    assert kk == kb, (a3.shape, b3.shape)
    tm, tn, tk = _tile(m, 1024), _tile(n, 1024), _tile(kk, 2048)
    nm, nn, nk = m // tm, n // tn, kk // tk
    has_res = res is not None
    single = nk == 1 and not (reduce_s and ns > 1)

    if reduce_s:
        grid = (nm, nn, ns, nk)
        ids = lambda i, j, s, k: (s, i, j, k)
        sem = ("parallel", "parallel", "arbitrary", "arbitrary")
    else:
        grid = (ns, nm, nn, nk)
        ids = lambda s, i, j, k: (s, i, j, k)
        sem = ("parallel", "parallel", "parallel", "arbitrary")

    def a_map(*g):
        s, i, j, k = ids(*g)
        s = s if sa > 1 else 0
        return (s, k, i) if ta else (s, i, k)

    def b_map(*g):
        s, i, j, k = ids(*g)
        s = s if sb > 1 else 0
        return (s, j, k) if tb else (s, k, j)

    def o_map(*g):
        s, i, j, k = ids(*g)
        return (i, j) if reduce_s else (s, i, j)

    a_blk = (1, tk, tm) if ta else (1, tm, tk)
    b_blk = (1, tn, tk) if tb else (1, tk, tn)
    dims = (((0 if ta else 1,), (1 if tb else 0,)), ((), ()))

    def body(*refs):
        a_ref, b_ref = refs[0], refs[1]
        r_ref = refs[2] if has_res else None
        o_ref = refs[2 + has_res + (after is not None)]
        acc_ref = None if single else refs[-1]
        s, _, _, k = ids(*[pl.program_id(d) for d in range(4)])
        prod = lax.dot_general(a_ref[0].astype(BF16), b_ref[0].astype(BF16), dims, preferred_element_type=F32)

        def finish(out):
            if scale != 1.0:
                out = out * scale
            if has_res:
                out = r_ref[...].reshape(out.shape) + out
            o_ref[...] = out.astype(out_dtype).reshape(o_ref.shape)

        if single:
            finish(prod)
            return
        if reduce_s:
            first = jnp.logical_and(s == 0, k == 0)
            last = jnp.logical_and(s == ns - 1, k == nk - 1)
        else:
            first, last = k == 0, k == nk - 1

        @pl.when(first)
        def _():
            acc_ref[...] = prod

        @pl.when(jnp.logical_not(first))
        def _():
            acc_ref[...] += prod

        @pl.when(last)
        def _():
            finish(acc_ref[...])

    in_specs = [pl.BlockSpec(a_blk, a_map), pl.BlockSpec(b_blk, b_map)]
    args = [a3, b3]
    if reduce_s:
        out_shape = jax.ShapeDtypeStruct((m, n), out_dtype)
        o_spec = pl.BlockSpec((tm, tn), o_map)
    else:
        out_shape = jax.ShapeDtypeStruct((ns, m, n), out_dtype)
        o_spec = pl.BlockSpec((1, tm, tn), o_map)
    if has_res:
        assert res.shape == out_shape.shape
        in_specs.append(o_spec)
        args.append(res)
    if after is not None:
        in_specs.append(HBM_SPEC)
        args.append(after)
    return pl.pallas_call(body, out_shape=out_shape, grid=grid, in_specs=in_specs, out_specs=o_spec,
                          scratch_shapes=[] if single else [pltpu.VMEM((tm, tn), F32)],
                          compiler_params=_cparams(sem), name=name)(*args)


def _row_tile(t, cap=256):
    for step in (16, SUBLANES):
        for tr in range(min(cap, t) // step * step, 0, -step):
            if t % tr == 0:
                return tr
    return t


def _rms_fwd(x, g, name):
    t, w = x.shape
    tr = _row_tile(t)

    def body(x_ref, g_ref, o_ref):
        xv = x_ref[...]
        r = lax.rsqrt(jnp.mean(xv * xv, axis=-1, keepdims=True) + EPS)
        o_ref[...] = (xv * r * g_ref[...]).astype(BF16)

    return pl.pallas_call(
        body, out_shape=jax.ShapeDtypeStruct((t, w), BF16), grid=(t // tr,),
        in_specs=[pl.BlockSpec((tr, w), lambda i: (i, 0)), pl.BlockSpec((1, w), lambda i: (0, 0))],
        out_specs=pl.BlockSpec((tr, w), lambda i: (i, 0)), compiler_params=_cparams(("parallel",)),
        name=name)(x, g)


def _rms_bwd_rows(xv, gv, dhv):
    r = lax.rsqrt(jnp.mean(xv * xv, axis=-1, keepdims=True) + EPS)
    nrm = xv * r
    dn = dhv * gv
    return r * (dn - nrm * jnp.mean(dn * nrm, axis=-1, keepdims=True)), dhv * nrm


def _rms_bwd(x, g, dh, dres, name):
    t, w = x.shape
    tr = _row_tile(t)
    has_res = dres is not None

    def body(*refs):
        if has_res:
            x_ref, g_ref, dh_ref, dr_ref, dx_ref, dxb_ref, dg_ref = refs
        else:
            x_ref, g_ref, dh_ref, dx_ref, dxb_ref, dg_ref = refs
        dx, dgs = _rms_bwd_rows(x_ref[...], g_ref[...], dh_ref[...])
        if has_res:
            dx = dx + dr_ref[...]
        dx_ref[...] = dx
        dxb_ref[...] = dx.astype(BF16)

        @pl.when(pl.program_id(0) == 0)
        def _():
            dg_ref[...] = jnp.zeros_like(dg_ref)

        dg_ref[...] += jnp.sum(dgs, axis=0, keepdims=True)

    row = pl.BlockSpec((tr, w), lambda i: (i, 0))
    vec = pl.BlockSpec((1, w), lambda i: (0, 0))
    ins = [x, g, dh] + ([dres] if has_res else [])
    return pl.pallas_call(
        body, out_shape=(jax.ShapeDtypeStruct((t, w), F32), jax.ShapeDtypeStruct((t, w), BF16),
                         jax.ShapeDtypeStruct((1, w), F32)),
        grid=(t // tr,), in_specs=[row, vec, row] + ([row] if has_res else []),
        out_specs=(row, row, vec), compiler_params=_cparams(("arbitrary",)), name=name)(*ins)


FFN_ROWS = 512
FFN_W_ROWS = 1024
SCAN_ROWS = 256


NT_DIMS = (((1,), (1,)), ((), ()))
TN_DIMS = (((0,), (0,)), ((), ()))


def _ffn_fwd_call(x, g, wg, wu, wd, name):
    t, d = x.shape
    ns, f, _ = wg.shape
    tm = _tile(t, FFN_ROWS)

    def body(x_ref, g_ref, wg_ref, wu_ref, wd_ref, xo_ref, h_ref, gate_ref, up_ref, h_sc, acc_ref):
        s = pl.program_id(1)

        @pl.when(s == 0)
        def _():
            xv = x_ref[...]
            r = lax.rsqrt(jnp.mean(xv * xv, axis=-1, keepdims=True) + EPS)
            hb = (xv * r * g_ref[...]).astype(BF16)
            h_sc[...] = hb
            h_ref[...] = hb

        hb = h_sc[...]
        gate = lax.dot_general(hb, wg_ref[0], NT_DIMS, preferred_element_type=F32)
        up = lax.dot_general(hb, wu_ref[0], NT_DIMS, preferred_element_type=F32)
        gate_ref[0] = gate.astype(BF16)
        up_ref[0] = up.astype(BF16)
        act = (gate * _sigmoid(gate) * up).astype(BF16)
        prod = jnp.dot(act, wd_ref[0], preferred_element_type=F32)

        @pl.when(s == 0)
        def _():
            acc_ref[...] = prod

        @pl.when(s > 0)
        def _():
            acc_ref[...] += prod

        @pl.when(s == ns - 1)
        def _():
            xo_ref[...] = x_ref[...] + 0.5 * acc_ref[...]

    row = pl.BlockSpec((tm, d), lambda i, s: (i, 0))
    vec = pl.BlockSpec((1, d), lambda i, s: (0, 0))
    wrow = pl.BlockSpec((1, f, d), lambda i, s: (s, 0, 0))
    hid = pl.BlockSpec((1, tm, f), lambda i, s: (s, i, 0))
    hid_sh = jax.ShapeDtypeStruct((ns, t, f), BF16)
    return pl.pallas_call(
        body, out_shape=(jax.ShapeDtypeStruct((t, d), F32), jax.ShapeDtypeStruct((t, d), BF16), hid_sh, hid_sh),
        grid=(t // tm, ns), in_specs=[row, vec, wrow, wrow, wrow], out_specs=(row, row, hid, hid),
        scratch_shapes=[pltpu.VMEM((tm, d), BF16), pltpu.VMEM((tm, d), F32)],
        compiler_params=_cparams(("parallel", "arbitrary")), name=name)(x, g, wg, wu, wd)


def _ffn_bwd_x_call(dxo, dxo_b, x, g, gate, up, wg, wu, wd, name):
    t, d = x.shape
    ns, f, _ = wg.shape
    tm = _tile(t, FFN_ROWS)

    def body(dxo_ref, dxb_ref, x_ref, g_ref, gate_ref, up_ref, wg_ref, wu_ref, wd_ref,
             dx_ref, dxob_ref, dgn_ref, dgate_ref, dup_ref, act_ref, dh_ref):
        i, s = pl.program_id(0), pl.program_id(1)
        dact = lax.dot_general(dxb_ref[...], wd_ref[0], NT_DIMS, preferred_element_type=F32) * 0.5
        gv = gate_ref[0].astype(F32)
        uv = up_ref[0].astype(F32)
        sg = _sigmoid(gv)
        silu = gv * sg
        act_ref[0] = (silu * uv).astype(BF16)
        dub = (dact * silu).astype(BF16)
        dgb = (dact * uv * sg * (1.0 + gv * (1.0 - sg))).astype(BF16)
        dup_ref[0] = dub
        dgate_ref[0] = dgb
        prod = (jnp.dot(dgb, wg_ref[0], preferred_element_type=F32)
                + jnp.dot(dub, wu_ref[0], preferred_element_type=F32))

        @pl.when(s == 0)
        def _():
            dh_ref[...] = prod

        @pl.when(s > 0)
        def _():
            dh_ref[...] += prod

        @pl.when(jnp.logical_and(i == 0, s == 0))
        def _():
            dgn_ref[...] = jnp.zeros_like(dgn_ref)

        @pl.when(s == ns - 1)
        def _():
            dx, dgs = _rms_bwd_rows(x_ref[...], g_ref[...], dh_ref[...])
            dx = dx + dxo_ref[...]
            dx_ref[...] = dx
            dxob_ref[...] = dx.astype(BF16)
            dgn_ref[...] += jnp.sum(dgs, axis=0, keepdims=True)

    row = pl.BlockSpec((tm, d), lambda i, s: (i, 0))
    vec = pl.BlockSpec((1, d), lambda i, s: (0, 0))
    wrow = pl.BlockSpec((1, f, d), lambda i, s: (s, 0, 0))
    hid = pl.BlockSpec((1, tm, f), lambda i, s: (s, i, 0))
    hid_sh = jax.ShapeDtypeStruct((ns, t, f), BF16)
    return pl.pallas_call(
        body,
        out_shape=(jax.ShapeDtypeStruct((t, d), F32), jax.ShapeDtypeStruct((t, d), BF16),
                   jax.ShapeDtypeStruct((1, d), F32), hid_sh, hid_sh, hid_sh),
        grid=(t // tm, ns), in_specs=[row, row, row, vec, hid, hid, wrow, wrow, wrow],
        out_specs=(row, row, vec, hid, hid, hid), scratch_shapes=[pltpu.VMEM((tm, d), F32)],
        compiler_params=_cparams(("arbitrary", "arbitrary")), name=name)(dxo, dxo_b, x, g, gate, up, wg, wu, wd)


def _ffn_bwd_w_call(h, dxo_b, dgate, dup, act, name):
    t, d = h.shape
    ns, _, f = dgate.shape
    tm = _tile(t, FFN_W_ROWS)
    nm = t // tm

    def body(h_ref, dxb_ref, dgate_ref, dup_ref, act_ref, dwg_ref, dwu_ref, dwd_ref, ag_ref, au_ref, ad_ref):
        i = pl.program_id(1)
        hv = h_ref[...]
        pg = lax.dot_general(dgate_ref[0], hv, TN_DIMS, preferred_element_type=F32)
        pu = lax.dot_general(dup_ref[0], hv, TN_DIMS, preferred_element_type=F32)
        pd = lax.dot_general(act_ref[0], dxb_ref[...], TN_DIMS, preferred_element_type=F32)

        @pl.when(i == 0)
        def _():
            ag_ref[...] = pg
            au_ref[...] = pu
            ad_ref[...] = pd

        @pl.when(i > 0)
        def _():
            ag_ref[...] += pg
            au_ref[...] += pu
            ad_ref[...] += pd

        @pl.when(i == nm - 1)
        def _():
            dwg_ref[0] = ag_ref[...].astype(BF16)
            dwu_ref[0] = au_ref[...].astype(BF16)
            dwd_ref[0] = (0.5 * ad_ref[...]).astype(BF16)

    row = pl.BlockSpec((tm, d), lambda s, i: (i, 0))
    hid = pl.BlockSpec((1, tm, f), lambda s, i: (s, i, 0))
    wrow = pl.BlockSpec((1, f, d), lambda s, i: (s, 0, 0))
    wsh = jax.ShapeDtypeStruct((ns, f, d), BF16)
    return pl.pallas_call(
        body, out_shape=(wsh, wsh, wsh),
        grid=(ns, nm), in_specs=[row, row, hid, hid, hid], out_specs=(wrow, wrow, wrow),
        scratch_shapes=[pltpu.VMEM((f, d), F32), pltpu.VMEM((f, d), F32), pltpu.VMEM((f, d), F32)],
        compiler_params=_cparams(("parallel", "arbitrary")), name=name)(h, dxo_b, dgate, dup, act)


def _loss_head(x, g, tgt, name):
    t, w = x.shape
    tr = _row_tile(t)

    def body(x_ref, g_ref, t_ref, loss_ref, dx_ref, dxb_ref, dg_ref):
        xv = x_ref[...]
        gv = g_ref[...]
        r = lax.rsqrt(jnp.mean(xv * xv, axis=-1, keepdims=True) + EPS)
        nrm = xv * r
        err = nrm * gv - t_ref[...]
        dout = err * (1.0 / w)
        dn = dout * gv
        dx = r * (dn - nrm * jnp.mean(dn * nrm, axis=-1, keepdims=True))
        dx_ref[...] = dx
        dxb_ref[...] = dx.astype(BF16)

        @pl.when(pl.program_id(0) == 0)
        def _():
            dg_ref[...] = jnp.zeros_like(dg_ref)
            loss_ref[...] = jnp.zeros_like(loss_ref)

        dg_ref[...] += jnp.sum(dout * nrm, axis=0, keepdims=True)
        part = jnp.sum(jnp.sum(err * err, axis=-1, keepdims=True) * (0.5 / w), axis=0, keepdims=True)
        loss_ref[...] += jnp.broadcast_to(part, loss_ref.shape)

    row = pl.BlockSpec((tr, w), lambda i: (i, 0))
    vec = pl.BlockSpec((1, w), lambda i: (0, 0))
    return pl.pallas_call(
        body, out_shape=(jax.ShapeDtypeStruct((1, LANES), F32), jax.ShapeDtypeStruct((t, w), F32),
                         jax.ShapeDtypeStruct((t, w), BF16), jax.ShapeDtypeStruct((1, w), F32)),
        grid=(t // tr,), in_specs=[row, vec, row],
        out_specs=(pl.BlockSpec((1, LANES), lambda i: (0, 0)), row, row, vec),
        compiler_params=_cparams(("arbitrary",)), name=name)(x, g, tgt)


def _attn_scores(q, k3, n, t, slope_ref):
    rows = GQ * QBLOCK
    s = lax.dot_general(q, k3, (((1,), (1,)), ((), ())), preferred_element_type=F32) * (HEAD_DIM ** -0.5)
    row = lax.broadcasted_iota(jnp.int32, (rows, 3 * QBLOCK), 0) & (QBLOCK - 1)
    col = lax.broadcasted_iota(jnp.int32, (rows, 3 * QBLOCK), 1)
    rel = jnp.abs(col - QBLOCK - row)
    key_pos = n * QBLOCK - QBLOCK + col
    valid = (rel <= WINDOW) & (key_pos >= 0) & (key_pos < t)
    return jnp.where(valid, s - slope_ref[0] * rel.astype(F32), NEG_INF)


Q_COL, K_COL, V_COL, U_COL = 0, ATTN_WIDTH // LANES, ATTN_WIDTH // LANES + 1, ATTN_WIDTH // LANES + 2


def _key_rows(ref, n, nb):
    prev, nxt = jnp.maximum(n - 1, 0), jnp.minimum(n + 1, nb - 1)
    blk = lambda b: ref[pl.ds(pl.multiple_of(b * QBLOCK, QBLOCK), QBLOCK), :]
    return jnp.concatenate([blk(prev), blk(n), blk(nxt)], axis=0)


def _head_tiles(x, kh, low):
    tiles = []
    for g in range(GQ):
        h = GQ * kh + g
        t128 = x[:, LANES * (h // 2):LANES * (h // 2 + 1)]
        t128 = jnp.where(low if h % 2 == 0 else jnp.logical_not(low), t128, 0.0)
        if h % 2 != kh:
            t128 = pltpu.roll(t128, HEAD_DIM, 1)
        tiles.append(t128)
    return jnp.concatenate(tiles, axis=0)


def _head_merge(per_kh, low):
    out = []
    for j in range(ATTN_HEADS // 2):
        pair = []
        for h in (2 * j, 2 * j + 1):
            kh, g = h // GQ, h % GQ
            t128 = per_kh[kh][g * QBLOCK:(g + 1) * QBLOCK, :]
            if h % 2 != kh:
                t128 = pltpu.roll(t128, HEAD_DIM, 1)
            pair.append(t128)
        out.append(jnp.where(low, pair[0], pair[1]))
    return jnp.concatenate(out, axis=1)


def _attn_fwd_proj(proj, sink_rows, slope_rows, name):
    t = proj.shape[0]
    nb = t // QBLOCK
    rows = GQ * QBLOCK

    def body(q_ref, k_ref, v_ref, sink_ref, slope_ref, o_ref, lse_ref):
        n = pl.program_id(0)
        low = lax.broadcasted_iota(jnp.int32, (QBLOCK, LANES), 1) < HEAD_DIM
        k3 = _key_rows(k_ref, n, nb).astype(BF16)
        v3 = _key_rows(v_ref, n, nb).astype(BF16)
        q = q_ref[...]
        outs = []
        for kh in range(KV_HEADS):
            qs = _head_tiles(q, kh, low).astype(BF16)
            s = _attn_scores(qs, k3, n, t, slope_ref.at[pl.ds(kh, 1)])
            sink = sink_ref[kh]
            mx = jnp.maximum(jnp.max(s, axis=-1, keepdims=True), sink)
            p = jnp.exp(s - mx)
            den = jnp.sum(p, axis=-1, keepdims=True) + jnp.exp(sink - mx)
            outs.append(jnp.dot(p.astype(BF16), v3, preferred_element_type=F32) / den)
            lse_ref[0, kh] = mx + jnp.log(den)
        o_ref[...] = _head_merge(outs, low)

    strip = lambda col: pl.BlockSpec((t, LANES), lambda n, col=col: (0, col))
    rowspec = pl.BlockSpec((KV_HEADS, rows, 1), lambda n: (0, 0, 0))
    return pl.pallas_call(
        body, out_shape=(jax.ShapeDtypeStruct((t, ATTN_WIDTH), F32), jax.ShapeDtypeStruct((nb, KV_HEADS, rows, 1), F32)),
        grid=(nb,), in_specs=[pl.BlockSpec((QBLOCK, ATTN_WIDTH), lambda n: (n, 0)), strip(K_COL), strip(V_COL),
                              rowspec, rowspec],
        out_specs=(pl.BlockSpec((QBLOCK, ATTN_WIDTH), lambda n: (n, 0)),
                   pl.BlockSpec((1, KV_HEADS, rows, 1), lambda n: (n, 0, 0, 0))),
        compiler_params=_cparams(("parallel",)), name=name)(proj, proj, proj, sink_rows, slope_rows)


def _attn_bwd_proj(proj, sink_rows, slope_rows, o, lse, do, name):
    t = proj.shape[0]
    nb = t // QBLOCK
    rows = GQ * QBLOCK
    scale = HEAD_DIM ** -0.5

    def body(q_ref, k_ref, v_ref, sink_ref, slope_ref, o_ref, lse_ref, do_ref, dq_ref, dk_ref, dv_ref, ds_ref):
        n = pl.program_id(0)

        @pl.when(n == 0)
        def _():
            dk_ref[...] = jnp.zeros_like(dk_ref)
            dv_ref[...] = jnp.zeros_like(dv_ref)
            ds_ref[...] = jnp.zeros_like(ds_ref)

        low = lax.broadcasted_iota(jnp.int32, (QBLOCK, LANES), 1) < HEAD_DIM
        k3 = _key_rows(k_ref, n, nb).astype(BF16)
        v3 = _key_rows(v_ref, n, nb).astype(BF16)
        q, dov = q_ref[...], do_ref[...]
        dod = dov * o_ref[...]
        dqs = []
        dk3 = jnp.zeros((3 * QBLOCK, LANES), F32)
        dv3 = jnp.zeros((3 * QBLOCK, LANES), F32)
        for kh in range(KV_HEADS):
            qs = _head_tiles(q, kh, low).astype(BF16)
            dos = _head_tiles(dov, kh, low).astype(BF16)
            delta = jnp.sum(_head_tiles(dod, kh, low), axis=-1, keepdims=True)
            lse_kh = lse_ref[0, kh]
            s = _attn_scores(qs, k3, n, t, slope_ref.at[pl.ds(kh, 1)])
            p = jnp.exp(s - lse_kh)
            dp = lax.dot_general(dos, v3, NT_DIMS, preferred_element_type=F32)
            dsb = (p * (dp - delta)).astype(BF16)
            dqs.append(jnp.dot(dsb, k3, preferred_element_type=F32) * scale)
            dk3 = dk3 + lax.dot_general(dsb, qs, TN_DIMS, preferred_element_type=F32) * scale
            dv3 = dv3 + lax.dot_general(p.astype(BF16), dos, TN_DIMS, preferred_element_type=F32)
            dsink_rows = -jnp.exp(sink_ref[kh] - lse_kh) * delta
            ds_ref[kh] += jnp.sum(dsink_rows.reshape(GQ, QBLOCK, 1), axis=1)
        dq_ref[...] = _head_merge(dqs, low)
        prev, nxt = jnp.maximum(n - 1, 0), jnp.minimum(n + 1, nb - 1)
        for j, b in enumerate((prev, n, nxt)):
            blk = pl.ds(pl.multiple_of(b * QBLOCK, QBLOCK), QBLOCK)
            dk_ref[blk, :] += dk3[j * QBLOCK:(j + 1) * QBLOCK, :]
            dv_ref[blk, :] += dv3[j * QBLOCK:(j + 1) * QBLOCK, :]

    strip = lambda col: pl.BlockSpec((t, LANES), lambda n, col=col: (0, col))
    rowspec = pl.BlockSpec((KV_HEADS, rows, 1), lambda n: (0, 0, 0))
    qspec = pl.BlockSpec((QBLOCK, ATTN_WIDTH), lambda n: (n, 0))
    kv_out = pl.BlockSpec((t, LANES), lambda n: (0, 0))
    return pl.pallas_call(
        body,
        out_shape=(jax.ShapeDtypeStruct((t, ATTN_WIDTH), F32), jax.ShapeDtypeStruct((t, LANES), F32),
                   jax.ShapeDtypeStruct((t, LANES), F32), jax.ShapeDtypeStruct((KV_HEADS, GQ, 1), F32)),
        grid=(nb,),
        in_specs=[qspec, strip(K_COL), strip(V_COL), rowspec, rowspec, qspec,
                  pl.BlockSpec((1, KV_HEADS, rows, 1), lambda n: (n, 0, 0, 0)), qspec],
        out_specs=(qspec, kv_out, kv_out, pl.BlockSpec((KV_HEADS, GQ, 1), lambda n: (0, 0, 0))),
        compiler_params=_cparams(("arbitrary",)), name=name)(proj, proj, proj, sink_rows, slope_rows, o, lse, do)


def _scan_tables(a_re, a_im, reverse):
    pw = [(a_re, a_im)]
    for _ in range(SUBLANES - 1):
        pr, pi = pw[-1]
        pw.append((pr * a_re - pi * a_im, pr * a_im + pi * a_re))
    rows = np.arange(SUBLANES)
    tabs = []
    for d in (1, 2, 4):
        mask = (rows <= SUBLANES - 1 - d) if reverse else (rows >= d)
        m = jnp.asarray(mask, F32)[:, None]
        tabs += [m * pw[d - 1][0][None, :], m * pw[d - 1][1][None, :]]
    order = (SUBLANES - 1 - rows) if reverse else rows
    tabs += [jnp.stack([pw[j][0] for j in order]), jnp.stack([pw[j][1] for j in order])]
    tab = jnp.stack(tabs)
    return tab.reshape(8, SUBLANES, N_STRIPS, STRIP_ST).transpose(2, 0, 1, 3)


def _scan_chunk(v_ref, mir_ref, mii_ref, tab_ref, mor_ref, moi_ref, y_ref, xr_ref, xi_ref, carry_ref, reverse):
    nblk = xr_ref.shape[0] // SUBLANES

    @pl.when(pl.program_id(1) == 0)
    def _():
        carry_ref[...] = jnp.zeros_like(carry_ref)

    vb = v_ref[...].astype(BF16)
    xr_ref[...] = jnp.dot(vb, mir_ref[0], preferred_element_type=F32)
    xi_ref[...] = jnp.dot(vb, mii_ref[0], preferred_element_type=F32)

    def blk(i, carry):
        cr, ci = carry
        b = (nblk - 1 - i) if reverse else i
        r0 = pl.multiple_of(b * SUBLANES, SUBLANES)
        xr = xr_ref[pl.ds(r0, SUBLANES), :]
        xi = xi_ref[pl.ds(r0, SUBLANES), :]
        for j, d in enumerate((1, 2, 4)):
            tr_, ti_ = tab_ref[0, 2 * j], tab_ref[0, 2 * j + 1]
            sh = (SUBLANES - d) if reverse else d
            sr = pltpu.roll(xr, sh, 0)
            si = pltpu.roll(xi, sh, 0)
            xr, xi = xr + tr_ * sr - ti_ * si, xi + tr_ * si + ti_ * sr
        pr, pi = tab_ref[0, 6], tab_ref[0, 7]
        xr, xi = xr + pr * cr - pi * ci, xi + pr * ci + pi * cr
        xr_ref[pl.ds(r0, SUBLANES), :] = xr
        xi_ref[pl.ds(r0, SUBLANES), :] = xi
        edge = 0 if reverse else SUBLANES - 1
        return (jnp.broadcast_to(xr[edge:edge + 1, :], xr.shape),
                jnp.broadcast_to(xi[edge:edge + 1, :], xi.shape))

    cr, ci = lax.fori_loop(0, nblk, blk, (carry_ref[0], carry_ref[1]))
    carry_ref[0] = cr
    carry_ref[1] = ci
    y_ref[...] = (jnp.dot(xr_ref[...].astype(BF16), mor_ref[0], preferred_element_type=F32)
                  + jnp.dot(xi_ref[...].astype(BF16), moi_ref[0], preferred_element_type=F32))


def _scan_adjoint(dy, u, xr, xi, mi_re, mi_im, tab, mo_re, mo_im, reverse, name):
    t = dy.shape[0]
    tc = _tile(t, SCAN_ROWS)
    nc = t // tc
    hb = tc // SUBLANES
    fwd_reverse = not reverse

    def body(dy_ref, mir_ref, mii_ref, tab_ref, mor_ref, moi_ref, u_ref, xr_ref, xi_ref, hr_ref, hi_ref,
             du_ref, dmir_ref, dmii_ref, dmor_ref, dmoi_ref, da_ref, lr_ref, li_ref, carry_ref):
        _scan_chunk(dy_ref, mir_ref, mii_ref, tab_ref, mor_ref, moi_ref, du_ref, lr_ref, li_ref, carry_ref, reverse)
        c = pl.program_id(1)
        rc = (nc - 1 - c) if reverse else c

        @pl.when(c == 0)
        def _():
            for r in (dmir_ref, dmii_ref, dmor_ref, dmoi_ref, da_ref):
                r[...] = jnp.zeros_like(r)

        xrv, xiv, lrv, liv = xr_ref[...], xi_ref[...], lr_ref[...], li_ref[...]
        row = lax.broadcasted_iota(jnp.int32, xrv.shape, 0)
        if fwd_reverse:
            live = (rc < nc - 1).astype(F32)
            edge_r, edge_i = hr_ref[0:1, :] * live, hi_ref[0:1, :] * live
            xpr = jnp.where(row == tc - 1, edge_r, pltpu.roll(xrv, tc - 1, 0))
            xpi = jnp.where(row == tc - 1, edge_i, pltpu.roll(xiv, tc - 1, 0))
        else:
            live = (rc > 0).astype(F32)
            edge_r, edge_i = hr_ref[SUBLANES - 1:SUBLANES, :] * live, hi_ref[SUBLANES - 1:SUBLANES, :] * live
            xpr = jnp.where(row == 0, edge_r, pltpu.roll(xrv, 1, 0))
            xpi = jnp.where(row == 0, edge_i, pltpu.roll(xiv, 1, 0))
        da_ref[0, 0:1, :] += jnp.sum(xpr * lrv + xpi * liv, axis=0, keepdims=True)
        da_ref[0, 1:2, :] += jnp.sum(xpr * liv - xpi * lrv, axis=0, keepdims=True)
        ub, dyb = u_ref[...].astype(BF16), dy_ref[...].astype(BF16)
        dmir_ref[0] += lax.dot_general(ub, lrv.astype(BF16), TN_DIMS, preferred_element_type=F32)
        dmii_ref[0] += lax.dot_general(ub, liv.astype(BF16), TN_DIMS, preferred_element_type=F32)
        dmor_ref[0] += lax.dot_general(xrv.astype(BF16), dyb, TN_DIMS, preferred_element_type=F32)
        dmoi_ref[0] += lax.dot_general(xiv.astype(BF16), dyb, TN_DIMS, preferred_element_type=F32)

    rowblk = (lambda c: nc - 1 - c) if reverse else (lambda c: c)
    tmap = lambda s, c: (rowblk(c), s)
    col0 = u.shape[1] // STRIP_IN - N_STRIPS
    umap = lambda s, c: (rowblk(c), s + col0)
    if fwd_reverse:
        hmap = lambda s, c: (jnp.minimum((rowblk(c) + 1) * hb, t // SUBLANES - 1), s)
    else:
        hmap = lambda s, c: (jnp.maximum(rowblk(c) * hb - 1, 0), s)
    smap3 = lambda s, c: (s, 0, 0)
    narrow = pl.BlockSpec((tc, STRIP_IN), tmap)
    wide = pl.BlockSpec((tc, STRIP_ST), tmap)
    halo = pl.BlockSpec((SUBLANES, STRIP_ST), hmap)
    m_in = pl.BlockSpec((1, STRIP_IN, STRIP_ST), smap3)
    m_out = pl.BlockSpec((1, STRIP_ST, STRIP_IN), smap3)
    return pl.pallas_call(
        body,
        out_shape=(jax.ShapeDtypeStruct((t, SSM_WIDTH), F32),
                   jax.ShapeDtypeStruct((N_STRIPS, STRIP_IN, STRIP_ST), F32),
                   jax.ShapeDtypeStruct((N_STRIPS, STRIP_IN, STRIP_ST), F32),
                   jax.ShapeDtypeStruct((N_STRIPS, STRIP_ST, STRIP_IN), F32),
                   jax.ShapeDtypeStruct((N_STRIPS, STRIP_ST, STRIP_IN), F32),
                   jax.ShapeDtypeStruct((N_STRIPS, SUBLANES, STRIP_ST), F32)),
        grid=(N_STRIPS, nc),
        in_specs=[narrow, m_in, m_in, pl.BlockSpec((1, 8, SUBLANES, STRIP_ST), lambda s, c: (s, 0, 0, 0)),
                  m_out, m_out, pl.BlockSpec((tc, STRIP_IN), umap), wide, wide, halo, halo],
        out_specs=(narrow, m_in, m_in, m_out, m_out, pl.BlockSpec((1, SUBLANES, STRIP_ST), smap3)),
        scratch_shapes=[pltpu.VMEM((tc, STRIP_ST), F32), pltpu.VMEM((tc, STRIP_ST), F32),
                        pltpu.VMEM((2, SUBLANES, STRIP_ST), F32)],
        compiler_params=_cparams(("parallel", "arbitrary")), name=name)(
            dy, mi_re, mi_im, tab, mo_re, mo_im, u, xr, xi, xr, xi)


def _scan(v, mi_re, mi_im, tab, mo_re, mo_im, reverse, name):
    t = v.shape[0]
    tc = _tile(t, SCAN_ROWS)
    nc = t // tc

    def body(v_ref, mir_ref, mii_ref, tab_ref, mor_ref, moi_ref, y_ref, xr_ref, xi_ref, carry_ref):
        _scan_chunk(v_ref, mir_ref, mii_ref, tab_ref, mor_ref, moi_ref, y_ref, xr_ref, xi_ref, carry_ref, reverse)

    tmap = (lambda s, c: (nc - 1 - c, s)) if reverse else (lambda s, c: (c, s))
    col0 = v.shape[1] // STRIP_IN - N_STRIPS
    vmap = lambda s, c: (tmap(s, c)[0], s + col0)
    smap3 = lambda s, c: (s, 0, 0)
    return pl.pallas_call(
        body,
        out_shape=(jax.ShapeDtypeStruct((t, SSM_WIDTH), F32),
                   jax.ShapeDtypeStruct((t, N_STRIPS * STRIP_ST), F32),
                   jax.ShapeDtypeStruct((t, N_STRIPS * STRIP_ST), F32)),
        grid=(N_STRIPS, nc),
        in_specs=[pl.BlockSpec((tc, STRIP_IN), vmap),
                  pl.BlockSpec((1, STRIP_IN, STRIP_ST), smap3), pl.BlockSpec((1, STRIP_IN, STRIP_ST), smap3),
                  pl.BlockSpec((1, 8, SUBLANES, STRIP_ST), lambda s, c: (s, 0, 0, 0)),
                  pl.BlockSpec((1, STRIP_ST, STRIP_IN), smap3), pl.BlockSpec((1, STRIP_ST, STRIP_IN), smap3)],
        out_specs=(pl.BlockSpec((tc, STRIP_IN), tmap), pl.BlockSpec((tc, STRIP_ST), tmap),
                   pl.BlockSpec((tc, STRIP_ST), tmap)),
        scratch_shapes=[pltpu.VMEM((2, SUBLANES, STRIP_ST), F32)],
        compiler_params=_cparams(("parallel", "arbitrary")), name=name)(v, mi_re, mi_im, tab, mo_re, mo_im)


def _ssm_prep(lam_re, lam_im, log_dt, bt_re, bt_im, c_re, c_im):
    lr = jnp.minimum(lam_re, LAMBDA_RE_MAX)
    li = lam_im
    dt = jnp.exp(log_dt)[:, None]
    mag = jnp.exp(lr * dt)
    a_re = mag * jnp.cos(li * dt)
    a_im = mag * jnp.sin(li * dt)
    den = lr * lr + li * li
    coef_re = ((a_re - 1.0) * lr + a_im * li) / den
    coef_im = (a_im * lr - (a_re - 1.0) * li) / den
    bb_re = coef_re[:, None, :] * bt_re - coef_im[:, None, :] * bt_im
    bb_im = coef_re[:, None, :] * bt_im + coef_im[:, None, :] * bt_re
    eye = jnp.eye(SSM_GROUPS // N_STRIPS, dtype=F32)

    def strips(m):
        g, a, b = m.shape
        m4 = m.reshape(N_STRIPS, g // N_STRIPS, a, b)
        return jnp.einsum('sgab,gk->sgakb', m4, eye).reshape(N_STRIPS, g // N_STRIPS * a, g // N_STRIPS * b)

    mi_re = strips(bb_re)
    mi_im = strips(bb_im)
    mo_re = strips(jnp.swapaxes(c_re, 1, 2))
    mo_im = strips(-jnp.swapaxes(c_im, 1, 2))
    return a_re.reshape(-1), a_im.reshape(-1), mi_re, mi_im, mo_re, mo_im


def _gelu(x):
    c = math.sqrt(2.0 / math.pi)
    return 0.5 * x * (1.0 + jnp.tanh(c * (x + 0.044715 * x * x * x)))


def _gelu_grad(x):
    c = math.sqrt(2.0 / math.pi)
    th = jnp.tanh(c * (x + 0.044715 * x * x * x))
    return 0.5 * (1.0 + th) + 0.5 * x * (1.0 - th * th) * c * (1.0 + 3.0 * 0.044715 * x * x)


def _last_cols_specs(u, w, tr):
    half = w // 2
    first = (u.shape[1] - w) // half
    assert first * half == u.shape[1] - w
    return [pl.BlockSpec((tr, half), lambda i, k=k: (i, first + k)) for k in range(2)]


def _ssm_post_fwd(u, yf, yb, d, wglu, bglu, name):
    t, w = yf.shape
    tr = _row_tile(t)

    def body(ua_ref, ub_ref, yf_ref, yb_ref, d_ref, w_ref, b_ref, s_ref, y0_ref, z_ref):
        uv = jnp.concatenate([ua_ref[...], ub_ref[...]], axis=1)
        y0 = d_ref[...] * uv + yf_ref[...] + yb_ref[...]
        yg = _gelu(y0)
        z = jnp.dot(yg.astype(BF16), w_ref[...], preferred_element_type=F32) + b_ref[...]
        s_ref[...] = yg * _sigmoid(z)
        y0_ref[...] = y0
        z_ref[...] = z

    row = pl.BlockSpec((tr, w), lambda i: (i, 0))
    vec = pl.BlockSpec((1, w), lambda i: (0, 0))
    mat = pl.BlockSpec((w, w), lambda i: (0, 0))
    sh = jax.ShapeDtypeStruct((t, w), F32)
    return pl.pallas_call(body, out_shape=(sh, sh, sh), grid=(t // tr,),
                          in_specs=[*_last_cols_specs(u, w, tr), row, row, vec, mat, vec], out_specs=(row, row, row),
                          compiler_params=_cparams(("parallel",)), name=name)(u, u, yf, yb, d, wglu, bglu)


def _ssm_post_bwd(ds, y0, z, u, d, wglu, name):
    t, w = ds.shape
    tr = _row_tile(t)

    def body(ds_ref, y0_ref, z_ref, ua_ref, ub_ref, d_ref, w_ref, dy0_ref, dw_ref, db_ref, dd_ref):
        @pl.when(pl.program_id(0) == 0)
        def _():
            dw_ref[...] = jnp.zeros_like(dw_ref)
            db_ref[...] = jnp.zeros_like(db_ref)
            dd_ref[...] = jnp.zeros_like(dd_ref)

        y0 = y0_ref[...]
        yg = _gelu(y0)
        sg = _sigmoid(z_ref[...])
        dsv = ds_ref[...]
        dz = dsv * yg * sg * (1.0 - sg)
        dzb = dz.astype(BF16)
        dyg = dsv * sg + lax.dot_general(dzb, w_ref[...], (((1,), (1,)), ((), ())), preferred_element_type=F32)
        dy0 = dyg * _gelu_grad(y0)
        dy0_ref[...] = dy0
        dw_ref[...] += lax.dot_general(yg.astype(BF16), dzb, (((0,), (0,)), ((), ())), preferred_element_type=F32)
        db_ref[...] += jnp.sum(dz, axis=0, keepdims=True)
        uv = jnp.concatenate([ua_ref[...], ub_ref[...]], axis=1)
        dd_ref[...] += jnp.sum(dy0 * uv, axis=0, keepdims=True)

    row = pl.BlockSpec((tr, w), lambda i: (i, 0))
    vec = pl.BlockSpec((1, w), lambda i: (0, 0))
    mat = pl.BlockSpec((w, w), lambda i: (0, 0))
    return pl.pallas_call(
        body, out_shape=(jax.ShapeDtypeStruct((t, w), F32), jax.ShapeDtypeStruct((w, w), F32),
                         jax.ShapeDtypeStruct((1, w), F32), jax.ShapeDtypeStruct((1, w), F32)),
        grid=(t // tr,), in_specs=[row, row, row, *_last_cols_specs(u, w, tr), vec, mat],
        out_specs=(row, mat, vec, vec),
        compiler_params=_cparams(("arbitrary",)), name=name)(ds, y0, z, u, u, d, wglu)


def _du_combine(dy0, d, du_f, du_b, name):
    t, w = dy0.shape
    tr = _row_tile(t)

    def body(dy_ref, d_ref, a_ref, b_ref, o_ref):
        o_ref[...] = d_ref[...] * dy_ref[...] + a_ref[...] + b_ref[...]

    row = pl.BlockSpec((tr, w), lambda i: (i, 0))
    vec = pl.BlockSpec((1, w), lambda i: (0, 0))
    return pl.pallas_call(body, out_shape=jax.ShapeDtypeStruct((t, w), F32), grid=(t // tr,),
                          in_specs=[row, vec, row, row], out_specs=row, compiler_params=_cparams(("parallel",)),
                          name=name)(dy0, d, du_f, du_b)


def _ffn_fwd(x, g, wg, wu, wd, tag):
    xo, h, gate, up = _ffn_fwd_call(x, g, wg, wu, wd, f"{tag}_fwd")
    return xo, (h, gate, up)


def _ffn_bwd(dxo, dxo_b, x, g, wg, wu, wd, saved, tag):
    h, gate, up = saved
    dx, dx_b, dg, dgate, dup, act = _ffn_bwd_x_call(dxo, dxo_b, x, g, gate, up, wg, wu, wd, f"{tag}_bwd_x")
    dwg, dwu, dwd = _ffn_bwd_w_call(h, dxo_b, dgate, dup, act, f"{tag}_bwd_w")
    return dx, dx_b, dg, dwg, dwu, dwd


def _local_step(x, tgt, w, get_weights, put_grads, reduce_wide):
    t = x.shape[0]
    row = lambda a: a.reshape(1, -1)
    grads = {}

    w = dict(w)

    ssm_names = ['ssm_lambda_re', 'ssm_lambda_im', 'ssm_log_dt', 'ssm_b_re', 'ssm_b_im', 'ssm_c_re', 'ssm_c_im']
    tr3 = lambda m: jnp.swapaxes(m, 1, 2)
    fwd_ops, adj_ops, vjps = [], [], []
    for direction in range(2):
        rev = direction == 1
        prep, vjp = jax.vjp(_ssm_prep, *[w[n][direction] for n in ssm_names])
        a_re, a_im = prep[0], prep[1]
        mi_re, mi_im, mo_re, mo_im = (m.astype(BF16) for m in prep[2:])
        fwd_ops.append((mi_re, mi_im, _scan_tables(a_re, a_im, rev), mo_re, mo_im))
        adj_ops.append((tr3(mo_re), tr3(mo_im), _scan_tables(a_re, -a_im, not rev), tr3(mi_re), tr3(mi_im)))
        vjps.append(vjp)
    sink_rows = jnp.repeat(w['attn_sinks'].reshape(KV_HEADS, GQ), QBLOCK, axis=1)[..., None]
    slopes = jnp.asarray(2.0 ** (-8.0 * (np.arange(ATTN_HEADS) + 1) / ATTN_HEADS), F32)
    slope_rows = jnp.repeat(slopes.reshape(KV_HEADS, GQ), QBLOCK, axis=1)[..., None]
    prepared = sum(jnp.sum(op[:1, :1].astype(F32)) for ops in fwd_ops + adj_ops for op in ops) + sink_rows[0, 0, 0]

    w.update(get_weights('ffn1', prepared.reshape(1, 1)))
    x1, ffn1_saved = _ffn_fwd(x, w['norm_ffn1'], w['ffn1_w_gate'], w['ffn1_w_up'], w['ffn1_w_down'], "ffn1")
    w.update(get_weights('mix', x1))

    h2 = _rms_fwd(x1, w['norm_mix'], "mix_norm")
    proj = _mm(h2, w['w_in'], tb=True, name="in_proj")[0]
    u = proj

    attn, lse = _attn_fwd_proj(proj, sink_rows, slope_rows, "attn_fwd")

    ys, states = [], []
    for direction in range(2):
        y, xr, xi = _scan(u, *fwd_ops[direction], direction == 1, f"s5_fwd{direction}")
        ys.append(y)
        states.append((xr, xi))
    d_row = row(w['ssm_d'])
    s, y0, z = _ssm_post_fwd(u, ys[0], ys[1], d_row, w['ssm_glu_w'], row(w['ssm_glu_b']), "ssm_post")

    ma = _rms_fwd(attn, row(w['attn_out_norm']), "attn_out_norm")
    ms = _rms_fwd(s, row(w['ssm_out_norm']), "ssm_out_norm")
    mixed = jnp.concatenate([ma, ms], axis=-1)
    x2 = _mm(mixed, w['w_out'], res=x1, reduce_s=True, name="out_proj")

    w.update(get_weights('ffn2', x2))
    x3, ffn2_saved = _ffn_fwd(x2, w['norm_ffn2'], w['ffn2_w_gate'], w['ffn2_w_up'], w['ffn2_w_down'], "ffn2")

    loss_row, dx3, dx3_b, dgf = _loss_head(x3, row(w['final_norm']), tgt, "loss_head")
    loss = loss_row[0, 0]
    grads['final_norm'] = dgf.reshape(w['final_norm'].shape)

    dx2, dx2_b, dg, dwg, dwu, dwd = _ffn_bwd(dx3, dx3_b, x2, w['norm_ffn2'], w['ffn2_w_gate'], w['ffn2_w_up'],
                                             w['ffn2_w_down'], ffn2_saved, "ffn2")
    grads['norm_ffn2'] = dg
    sent = put_grads('ffn2', dict(ffn2_w_gate=dwg, ffn2_w_up=dwu, ffn2_w_down=dwd))

    dmixed = _mm(dx2_b, w['w_out'], tb=True, reduce_s=True, after=sent, name="out_proj_dx")
    dw_out = _mm(mixed, dx2_b, ta=True, out_dtype=BF16, name="out_proj_dw")[0]
    dattn, _, dga = _rms_bwd(attn, row(w['attn_out_norm']), dmixed[:, :ATTN_WIDTH], None, "attn_out_dnorm")
    ds, _, dgs = _rms_bwd(s, row(w['ssm_out_norm']), dmixed[:, ATTN_WIDTH:], None, "ssm_out_dnorm")
    grads.update(attn_out_norm=dga, ssm_out_norm=dgs)

    dy0, dwglu, dbglu, dd = _ssm_post_bwd(ds, y0, z, u, d_row, w['ssm_glu_w'], "ssm_post_bwd")
    grads['ssm_glu_b'] = dbglu
    grads['ssm_d'] = dd.reshape(w['ssm_d'].shape)
    dparams, du_dirs = [], []
    for direction in range(2):
        rev = direction == 1
        xr, xi = states[direction]
        du_dir, dmir, dmii, dmor, dmoi, da = _scan_adjoint(dy0, u, xr, xi, *adj_ops[direction], not rev,
                                                            f"s5_adj{direction}")
        du_dirs.append(du_dir)
        da_re = da[:, 0, :].reshape(-1)
        da_im = da[:, 1, :].reshape(-1)
        dparams.append(vjps[direction]((da_re, da_im, dmir, dmii, dmor, dmoi)))
    du = _du_combine(dy0, d_row, du_dirs[0], du_dirs[1], "ssm_du")
    for i, n in enumerate(ssm_names):
        grads[n] = jnp.stack([dparams[0][i], dparams[1][i]])
    wide_sum = reduce_wide(grads)

    dq, dk, dv, dsink = _attn_bwd_proj(proj, sink_rows, slope_rows, attn, lse, dattn, "attn_bwd")
    grads['attn_sinks'] = dsink.reshape(w['attn_sinks'].shape)
    dproj = jnp.concatenate([dq, dk, dv, du], axis=-1).astype(BF16)

    dw_in = _mm(dproj, h2, ta=True, out_dtype=BF16, after=wide_sum, name="in_proj_dw")[0]
    sent = put_grads('mix', dict(w_in=dw_in, ssm_glu_w=dwglu, w_out=dw_out))
    dh2 = _mm(dproj, w['w_in'], reduce_s=True, after=sent, name="in_proj_dx")
    dx1, dx1_b, dgm = _rms_bwd(x1, w['norm_mix'], dh2, dx2, "mix_dnorm")
    grads['norm_mix'] = dgm

    dx0, _, dg, dwg, dwu, dwd = _ffn_bwd(dx1, dx1_b, x, w['norm_ffn1'], w['ffn1_w_gate'], w['ffn1_w_up'],
                                         w['ffn1_w_down'], ffn1_saved, "ffn1")
    grads['norm_ffn1'] = dg
    put_grads('ffn1', dict(ffn1_w_gate=dwg, ffn1_w_up=dwu, ffn1_w_down=dwd))
    return loss, dx0, grads, wide_sum


HBM_SPEC = pl.BlockSpec(memory_space=pl.ANY)


def _chip_peers(x, y):
    return [(1 - x, y), (x, 1 - y), (1 - x, 1 - y)]


HBM_ONLY = pl.BlockSpec(memory_space=pltpu.HBM)
SEM_SPEC = pl.BlockSpec(memory_space=pltpu.SEMAPHORE)
EFFECT = pltpu.SideEffectType.DATAFLOW_SIDE_EFFECTING


def _place_own(src, slot, name):
    r, c = src.shape
    tr = r // 2

    def body(slot_ref, s_ref, o_ref):
        o_ref[0] = s_ref[...]

    return pl.pallas_call(
        body, out_shape=jax.ShapeDtypeStruct((N_CHIPS, r, c), src.dtype),
        grid_spec=pltpu.PrefetchScalarGridSpec(
            num_scalar_prefetch=1, grid=(2,), in_specs=[pl.BlockSpec((tr, c), lambda i, s: (i, 0))],
            out_specs=pl.BlockSpec((1, tr, c), lambda i, s: (s[0], i, 0))),
        compiler_params=_cparams(("parallel",)), name=name)(slot, src)


def _chip_copies(srcs, lands, send_sems, recv_sems, scatter, landed):
    x, y, c = lax.axis_index("x"), lax.axis_index("y"), lax.axis_index("c")
    me = 2 * x + y
    out = []
    for i in range(len(srcs)):
        for j, (px, py) in enumerate(_chip_peers(x, y)):
            p = 2 * px + py
            slot = p if landed else me
            if scatter:
                src, dst = srcs[i].at[p], lands[i].at[slot]
            else:
                rows = _core_half(srcs[i].shape[0], c)
                src, dst = srcs[i].at[rows], lands[i].at[slot, rows]
            out.append(pltpu.make_async_remote_copy(src, dst, send_sems.at[3 * i + j], recv_sems.at[3 * i + j],
                                                    device_id=(px, py, c), device_id_type=MESH))
    return out


def _core_half(nrows, c):
    half = nrows // 2
    return pl.ds(pl.multiple_of(c * half, 16), half)


def _sibling_forward(lands, name):
    n = len(lands)

    def body(*refs):
        bufs = refs[n:2 * n]
        send_sems, recv_sems = refs[2 * n:]
        x, y, c = lax.axis_index("x"), lax.axis_index("y"), lax.axis_index("c")
        mine = [_core_half(b.shape[1], c) for b in bufs]
        theirs = [_core_half(b.shape[1], 1 - c) for b in bufs]
        chips = [2 * px + py for px, py in _chip_peers(x, y)]
        cps = [pltpu.make_async_remote_copy(bufs[i].at[p, mine[i]], bufs[i].at[p, mine[i]], send_sems.at[3 * i + j],
                                            recv_sems.at[3 * i + j], device_id=(x, y, 1 - c), device_id_type=MESH)
               for i in range(n) for j, p in enumerate(chips)]
        for cp in cps:
            cp.start()
        for i in range(n):
            for j, p in enumerate(chips):
                pltpu.make_async_remote_copy(bufs[i].at[p, mine[i]], bufs[i].at[p, theirs[i]], send_sems.at[3 * i + j],
                                             recv_sems.at[3 * i + j], device_id=(x, y, 1 - c),
                                             device_id_type=MESH).wait()

    return pl.pallas_call(
        body, out_shape=[jax.ShapeDtypeStruct(a.shape, a.dtype) for a in lands],
        in_specs=[HBM_SPEC] * n, out_specs=[HBM_SPEC] * n, input_output_aliases={k: k for k in range(n)},
        scratch_shapes=[pltpu.SemaphoreType.DMA((3 * n,)), pltpu.SemaphoreType.DMA((3 * n,))],
        name=name)(*lands)


def _exchange_start(groups, scatter, name, after=None):
    sizes = [len(srcs) for srcs, _ in groups]
    flat_src = [a for srcs, _ in groups for a in srcs]
    flat_land = [a for _, lands in groups for a in lands]
    n = len(flat_src)
    ng = len(groups)

    def body(*refs):
        src_refs, land_refs = refs[:n], refs[n:2 * n]
        n_in = 2 * n + (after is not None)
        sems = refs[n_in:n_in + 2 * ng]
        token_ref = refs[-1]
        off = 0
        for gi, sz in enumerate(sizes):
            for cp in _chip_copies(src_refs[off:off + sz], land_refs[off:off + sz], sems[2 * gi], sems[2 * gi + 1],
                                   scatter, landed=False):
                cp.start()
            off += sz
        token_ref[...] = jnp.zeros_like(token_ref)

    sem_shapes = []
    for sz in sizes:
        sem_shapes += [pltpu.SemaphoreType.DMA((3 * sz,)), pltpu.SemaphoreType.DMA((3 * sz,))]
    hbm = lambda a: pltpu.HBM(a.shape, a.dtype)
    res = pl.pallas_call(
        body, name=name,
        out_shape=(tuple(sem_shapes) + tuple(hbm(a) for a in flat_src) + tuple(hbm(a) for a in flat_land)
                   + (jax.ShapeDtypeStruct((SUBLANES, LANES), F32),)),
        in_specs=[HBM_ONLY] * (2 * n) + [HBM_SPEC] * (after is not None),
        out_specs=tuple([SEM_SPEC] * (2 * ng) + [HBM_ONLY] * (2 * n) + [pl.BlockSpec(memory_space=pltpu.VMEM)]),
        input_output_aliases={k: 2 * ng + k for k in range(2 * n)},
        compiler_params=pltpu.CompilerParams(has_side_effects=EFFECT),
    )(*[pltpu.with_memory_space_constraint(a, pltpu.HBM) for a in flat_src + flat_land],
      *([after] if after is not None else []))
    sems, thru_src, thru_land = res[:2 * ng], res[2 * ng:2 * ng + n], res[2 * ng + n:2 * ng + 2 * n]
    out, off = [], 0
    for gi, sz in enumerate(sizes):
        out.append((sems[2 * gi], sems[2 * gi + 1], list(thru_src[off:off + sz]), list(thru_land[off:off + sz])))
        off += sz
    return out, res[-1]


def _exchange_wait(started, after, scatter, name):
    send_sems, recv_sems, srcs, lands = started
    n = len(srcs)

    def body(*refs):
        src_refs, land_refs = refs[:n], refs[n:2 * n]
        send_ref, recv_ref = refs[2 * n], refs[2 * n + 1]
        for cp in _chip_copies(src_refs, land_refs, send_ref, recv_ref, scatter, landed=True):
            cp.wait_send()
            cp.wait_recv()

    hbm = lambda a: pltpu.HBM(a.shape, a.dtype)
    res = pl.pallas_call(
        body, name=name, out_shape=tuple(hbm(a) for a in srcs) + tuple(hbm(a) for a in lands),
        in_specs=[HBM_ONLY] * (2 * n) + [SEM_SPEC, SEM_SPEC, HBM_SPEC], out_specs=tuple([HBM_ONLY] * (2 * n)),
        input_output_aliases={k: k for k in range(2 * n)},
        compiler_params=pltpu.CompilerParams(has_side_effects=EFFECT),
    )(*srcs, *lands, send_sems, recv_sems, after)
    return list(res[:n]), list(res[n:])


def _half_swap(parts, name):
    n = len(parts)

    def body(*refs):
        ins, outs = refs[:n], refs[n:2 * n]
        send_sems, recv_sems = refs[2 * n:]
        x, y, c = lax.axis_index("x"), lax.axis_index("y"), lax.axis_index("c")
        cps = [pltpu.make_async_remote_copy(ins[i].at[k, _core_half(ins[i].shape[1], 1 - c)], outs[i].at[k],
                                            send_sems.at[N_CHIPS * i + k], recv_sems.at[N_CHIPS * i + k],
                                            device_id=(x, y, 1 - c), device_id_type=MESH)
               for i in range(n) for k in range(N_CHIPS)]
        for cp in cps:
            cp.start()
        for cp in cps:
            cp.wait()

    return pl.pallas_call(
        body, out_shape=[jax.ShapeDtypeStruct((N_CHIPS, p.shape[1] // 2, p.shape[2]), p.dtype) for p in parts],
        in_specs=[HBM_SPEC] * n, out_specs=[HBM_SPEC] * n,
        scratch_shapes=[pltpu.SemaphoreType.DMA((N_CHIPS * n,)), pltpu.SemaphoreType.DMA((N_CHIPS * n,))],
        name=name)(*parts)


def _half_add(parts, sib, slots, name):
    na = len(parts)
    _, r, c = parts[0].shape
    hr = r // 2
    tr = _row_tile(hr, 512)
    nt = hr // tr

    def body(slot_ref, *refs):
        for a in range(na):
            refs[2 * na + a][...] = (refs[2 * a][...].astype(F32) + refs[2 * a + 1][...].astype(F32)).astype(BF16)

    mine = pl.BlockSpec((1, tr, c), lambda k, i, s: (k, i + s[4] * nt, 0))
    half = pl.BlockSpec((1, tr, c), lambda k, i, s: (k, i, 0))
    args = [a for p, sb in zip(parts, sib) for a in (p, sb)]
    return pl.pallas_call(
        body, out_shape=[jax.ShapeDtypeStruct((N_CHIPS, hr, c), BF16)] * na,
        grid_spec=pltpu.PrefetchScalarGridSpec(
            num_scalar_prefetch=1, grid=(N_CHIPS, nt), in_specs=[mine, half] * na, out_specs=[half] * na),
        compiler_params=_cparams(("parallel", "parallel")), name=name)(slots, *args)


def _half_forward(arrs, name):
    n = len(arrs)

    def body(*refs):
        bufs = refs[n:2 * n]
        send_sems, recv_sems = refs[2 * n:]
        x, y, c = lax.axis_index("x"), lax.axis_index("y"), lax.axis_index("c")
        cps = [pltpu.make_async_remote_copy(b.at[_core_half(b.shape[0], c)], b.at[_core_half(b.shape[0], c)],
                                            send_sems.at[i], recv_sems.at[i], device_id=(x, y, 1 - c),
                                            device_id_type=MESH) for i, b in enumerate(bufs)]
        for cp in cps:
            cp.start()
        for i, b in enumerate(bufs):
            pltpu.make_async_remote_copy(b.at[_core_half(b.shape[0], c)], b.at[_core_half(b.shape[0], 1 - c)],
                                         send_sems.at[i], recv_sems.at[i], device_id=(x, y, 1 - c),
                                         device_id_type=MESH).wait()

    return pl.pallas_call(
        body, out_shape=[jax.ShapeDtypeStruct(a.shape, a.dtype) for a in arrs],
        in_specs=[HBM_SPEC] * n, out_specs=[HBM_SPEC] * n, input_output_aliases={k: k for k in range(n)},
        scratch_shapes=[pltpu.SemaphoreType.DMA((n,)), pltpu.SemaphoreType.DMA((n,))],
        name=name)(*arrs)


def _small_exchange(smalls, name):
    nsm = len(smalls)
    rels = [(fx, fy, fc) for fx in (0, 1) for fy in (0, 1) for fc in (0, 1)][1:]

    def body(*refs):
        sins, souts = refs[:nsm], refs[nsm:2 * nsm]
        ssend, srecv, slocal = refs[2 * nsm:]
        x, y, c = lax.axis_index("x"), lax.axis_index("y"), lax.axis_index("c")
        lin = 4 * x + 2 * y + c
        local = [pltpu.make_async_copy(sins[i], souts[i].at[lin], slocal.at[i]) for i in range(nsm)]
        for cp in local:
            cp.start()
        for i in range(nsm):
            for j, (fx, fy, fc) in enumerate(rels):
                pltpu.make_async_remote_copy(sins[i], souts[i].at[lin], ssend.at[i, j], srecv.at[i, j],
                                             device_id=(x ^ fx, y ^ fy, c ^ fc), device_id_type=MESH).start()
        for i in range(nsm):
            for j, (fx, fy, fc) in enumerate(rels):
                src = 4 * (x ^ fx) + 2 * (y ^ fy) + (c ^ fc)
                pltpu.make_async_remote_copy(sins[i], souts[i].at[src], ssend.at[i, j], srecv.at[i, j],
                                             device_id=(x ^ fx, y ^ fy, c ^ fc), device_id_type=MESH).wait()
        for cp in local:
            cp.wait()

    return pl.pallas_call(
        body, out_shape=[jax.ShapeDtypeStruct((N_DEV,) + s.shape, s.dtype) for s in smalls],
        in_specs=[HBM_SPEC] * nsm, out_specs=[HBM_SPEC] * nsm,
        scratch_shapes=[pltpu.SemaphoreType.DMA((nsm, 7)), pltpu.SemaphoreType.DMA((nsm, 7)),
                        pltpu.SemaphoreType.DMA((nsm,))],
        name=name)(*smalls)


def _sum_parts(parts, recv, slots, name):
    na = len(parts)
    _, r, c = parts[0].shape
    tr = _row_tile(r, 192)

    def body(slot_ref, *refs):
        for a in range(na):
            own_ref, r0_ref, r1_ref, r2_ref = refs[4 * a:4 * a + 4]
            refs[4 * na + a][...] = ((own_ref[0].astype(F32) + r0_ref[0].astype(F32))
                                     + (r1_ref[0].astype(F32) + r2_ref[0].astype(F32)))

    blk = lambda k: pl.BlockSpec((1, tr, c), lambda i, s, k=k: (s[k], i, 0))
    out_blk = pl.BlockSpec((tr, c), lambda i, s: (i + s[4] * (r // tr), 0))
    args = [a for p, rv in zip(parts, recv) for a in (p, rv, rv, rv)]
    return pl.pallas_call(
        body, out_shape=[jax.ShapeDtypeStruct((2 * r, c), F32)] * na,
        grid_spec=pltpu.PrefetchScalarGridSpec(
            num_scalar_prefetch=1, grid=(r // tr,), in_specs=[blk(0), blk(1), blk(2), blk(3)] * na,
            out_specs=[out_blk] * na),
        compiler_params=_cparams(("parallel",)), name=name)(slots, *args)


def _small_allreduce(packed, name):
    rows = packed.shape[0]
    pr = rows // N_DEV
    rels = [(fx, fy, fc) for fx in (0, 1) for fy in (0, 1) for fc in (0, 1)][1:]

    def body(in_ref, out_ref, recv_ref, send1, recv1, send2, recv2):
        x, y, c = lax.axis_index("x"), lax.axis_index("y"), lax.axis_index("c")
        lin = 4 * x + 2 * y + c
        piece = lambda ref, k: ref.at[pl.ds(pl.multiple_of(k * pr, pr), pr), :]
        peers = [((x ^ fx, y ^ fy, c ^ fc), 4 * (x ^ fx) + 2 * (y ^ fy) + (c ^ fc)) for fx, fy, fc in rels]
        for j, (dev, plin) in enumerate(peers):
            pltpu.make_async_remote_copy(piece(in_ref, plin), recv_ref.at[lin], send1.at[j], recv1.at[j],
                                         device_id=dev, device_id_type=MESH).start()
        recv_ref[lin] = piece(in_ref, lin)[...]
        for j, (dev, plin) in enumerate(peers):
            pltpu.make_async_remote_copy(piece(in_ref, plin), recv_ref.at[plin], send1.at[j], recv1.at[j],
                                         device_id=dev, device_id_type=MESH).wait()
        acc = recv_ref[0]
        for k in range(1, N_DEV):
            acc = acc + recv_ref[k]
        piece(out_ref, lin)[...] = acc
        for j, (dev, plin) in enumerate(peers):
            pltpu.make_async_remote_copy(piece(out_ref, lin), piece(out_ref, lin), send2.at[j], recv2.at[j],
                                         device_id=dev, device_id_type=MESH).start()
        for j, (dev, plin) in enumerate(peers):
            pltpu.make_async_remote_copy(piece(out_ref, lin), piece(out_ref, plin), send2.at[j], recv2.at[j],
                                         device_id=dev, device_id_type=MESH).wait()

    vm = pl.BlockSpec(memory_space=pltpu.VMEM)
    return pl.pallas_call(
        body, out_shape=jax.ShapeDtypeStruct(packed.shape, F32), in_specs=[vm], out_specs=vm,
        scratch_shapes=[pltpu.VMEM((N_DEV, pr, LANES), F32)] + [pltpu.SemaphoreType.DMA((7,))] * 4,
        compiler_params=pltpu.CompilerParams(vmem_limit_bytes=VMEM_LIMIT), name=name)(packed)


def _adamw_math(w, m, v, g):
    nm = ADAM_B1 * m + (1.0 - ADAM_B1) * g
    nv = ADAM_B2 * v + (1.0 - ADAM_B2) * (g * g)
    m_hat = nm * (1.0 / (1.0 - ADAM_B1 ** ADAM_STEP))
    v_hat = nv * (1.0 / (1.0 - ADAM_B2 ** ADAM_STEP))
    return -ADAM_LR * (m_hat / (jnp.sqrt(v_hat) + ADAM_EPS) + ADAM_WD * w), nm, nv


def _adamw(ws, ms, vs, gs, name):
    na = len(ws)
    r, c = ws[0].shape
    tr = _row_tile(r)

    def body(*refs):
        for a in range(na):
            w_ref, m_ref, v_ref, g_ref = refs[4 * a:4 * a + 4]
            d_ref, nm_ref, nv_ref = refs[4 * na + 3 * a:4 * na + 3 * a + 3]
            d_ref[...], nm_ref[...], nv_ref[...] = _adamw_math(w_ref[...], m_ref[...], v_ref[...], g_ref[...])

    blk = pl.BlockSpec((tr, c), lambda i: (i, 0))
    sh = jax.ShapeDtypeStruct((r, c), F32)
    args = [a for group in zip(ws, ms, vs, gs) for a in group]
    res = pl.pallas_call(body, out_shape=[sh] * (3 * na), grid=(r // tr,), in_specs=[blk] * (4 * na),
                         out_specs=[blk] * (3 * na), compiler_params=_cparams(("parallel",)), name=name)(*args)
    return [tuple(res[3 * a:3 * a + 3]) for a in range(na)]


def _adamw_small(ws, ms, vs, alls, split, name):
    n = len(ws)
    lead = split if split is not None else ()
    nl = len(lead)
    nslots = alls[0].shape[0]

    def blocks(shape):
        if split is None:
            return tuple(shape), (lambda *g: (0,) * len(shape))
        blk = (shape[0], shape[1] // lead[0], shape[2] // lead[1]) + tuple(shape[3:])
        return blk, (lambda *g: (0, g[0], g[1]) + (0,) * (len(shape) - 3))

    def body(*refs):
        w_refs, m_refs, v_refs, a_refs = (refs[k * n:(k + 1) * n] for k in range(4))
        g_refs, d_refs, nm_refs, nv_refs = (refs[(4 + k) * n:(5 + k) * n] for k in range(4))
        k = pl.program_id(nl)
        for i in range(n):
            @pl.when(k == 0)
            def _(i=i):
                g_refs[i][...] = a_refs[i][0]

            @pl.when(k > 0)
            def _(i=i):
                g_refs[i][...] += a_refs[i][0]

            @pl.when(k == nslots - 1)
            def _(i=i):
                d_refs[i][...], nm_refs[i][...], nv_refs[i][...] = _adamw_math(
                    w_refs[i][...], m_refs[i][...], v_refs[i][...], g_refs[i][...])

    specs, aspecs, shapes = [], [], []
    for wa in ws:
        blk, imap = blocks(wa.shape)
        specs.append(pl.BlockSpec(blk, imap))
        aspecs.append(pl.BlockSpec((1,) + blk, (lambda *g, imap=imap: (g[nl],) + imap(*g))))
        shapes.append(jax.ShapeDtypeStruct(wa.shape, F32))
    res = pl.pallas_call(
        body, out_shape=shapes * 4, grid=tuple(lead) + (nslots,), in_specs=specs * 3 + aspecs,
        out_specs=specs * 4, compiler_params=_cparams(("parallel",) * nl + ("arbitrary",)),
        name=name)(*ws, *ms, *vs, *alls)
    return res[:n], res[n:2 * n], res[2 * n:3 * n], res[3 * n:]


def kernel(x, norm_ffn1, ffn1_w_gate, ffn1_w_up, ffn1_w_down, norm_mix, w_in, attn_sinks, ssm_lambda_re, ssm_lambda_im, ssm_log_dt, ssm_b_re, ssm_b_im, ssm_c_re, ssm_c_im, ssm_d, ssm_glu_w, ssm_glu_b, attn_out_norm, ssm_out_norm, w_out, norm_ffn2, ffn2_w_gate, ffn2_w_up, ffn2_w_down, final_norm, loss_target, m_norm_ffn1, m_ffn1_w_gate, m_ffn1_w_up, m_ffn1_w_down, m_norm_mix, m_w_in, m_attn_sinks, m_ssm_lambda_re, m_ssm_lambda_im, m_ssm_log_dt, m_ssm_b_re, m_ssm_b_im, m_ssm_c_re, m_ssm_c_im, m_ssm_d, m_ssm_glu_w, m_ssm_glu_b, m_attn_out_norm, m_ssm_out_norm, m_w_out, m_norm_ffn2, m_ffn2_w_gate, m_ffn2_w_up, m_ffn2_w_down, m_final_norm, v_norm_ffn1, v_ffn1_w_gate, v_ffn1_w_up, v_ffn1_w_down, v_norm_mix, v_w_in, v_attn_sinks, v_ssm_lambda_re, v_ssm_lambda_im, v_ssm_log_dt, v_ssm_b_re, v_ssm_b_im, v_ssm_c_re, v_ssm_c_im, v_ssm_d, v_ssm_glu_w, v_ssm_glu_b, v_attn_out_norm, v_ssm_out_norm, v_w_out, v_norm_ffn2, v_ffn2_w_gate, v_ffn2_w_up, v_ffn2_w_down, v_final_norm):
    given = dict(locals())
    wts = {n: given[n] for n in WEIGHTS}

    order = [g for g in GROUPS]
    cx, cy = lax.axis_index("x"), lax.axis_index("y")
    slots = jnp.stack([2 * cx + cy, 2 * (1 - cx) + cy, 2 * cx + 1 - cy, 2 * (1 - cx) + 1 - cy,
                       lax.axis_index("c")]).astype(jnp.int32)
    def view(a, n):
        if n in TRANSPOSED:
            return jnp.swapaxes(a[0], 0, 1)
        if n in BIG:
            return a[0]
        if n in ('ssm_b_re', 'ssm_b_im'):
            return jnp.swapaxes(a, -1, -2)
        return a.reshape(1, -1) if a.ndim == 1 else a

    def unview(a, n):
        if n in TRANSPOSED:
            return jnp.swapaxes(a, 0, 1)[None]
        if n in ('ssm_b_re', 'ssm_b_im'):
            return jnp.swapaxes(a, -1, -2)
        return a.reshape(wts[n].shape)

    started, gather_token = {}, None
    for g in order:
        shards = [view(wts[n], n).astype(BF16) for n in GROUPS[g]]
        placed = [_place_own(s, slots, f"weights_place_{n}") for n, s in zip(GROUPS[g], shards)]
        st, gather_token = _exchange_start([(shards, placed)], False, f"weights_start_{g}", after=gather_token)
        started[g] = st[0]

    def get_weights(group, after):
        if group == order[0]:
            after = after + gather_token[:1, :1]
        _, lands = _exchange_wait(started[group], after, False, f"weights_wait_{group}")
        lands = _sibling_forward(lands, f"weights_forward_{group}")
        out = dict(zip(GROUPS[group], lands))
        for n in ('w_in', 'ssm_glu_w', 'w_out'):
            if n in out:
                out[n] = out[n].reshape(-1, out[n].shape[-1])
        return out

    sent, tokens = {}, {}

    def put_grads(group, gd):
        parts = []
        for n in GROUPS[group]:
            g = gd[n]
            if g.ndim == 2:
                g = g.reshape(N_CHIPS, g.shape[0] // N_CHIPS, g.shape[1])
            parts.append(g.astype(BF16))
        sib = _half_swap(parts, f"grads_half_swap_{group}")
        same = len({p.shape for p in parts}) == 1
        batches = [list(range(len(parts)))] if same else [[i] for i in range(len(parts))]
        halves = [None] * len(parts)
        for b in batches:
            res = _half_add([parts[i] for i in b], [sib[i] for i in b], slots, f"grads_half_add_{GROUPS[group][b[0]]}")
            for i, h in zip(b, res):
                halves[i] = h
        parts = halves
        lands = [lax.empty(p.shape, p.dtype) for p in parts]
        started_g, tokens[group] = _exchange_start([(parts, lands)], True, f"grads_start_{group}")
        sent[group] = started_g[0]
        return tokens[group]

    w = {n: (wts[n][0] if wts[n].ndim > 1 else wts[n]) for n in SMALL}
    w['norm_ffn1'], w['norm_mix'], w['norm_ffn2'] = wts['norm_ffn1'], wts['norm_mix'], wts['norm_ffn2']
    w['ssm_b_re'], w['ssm_b_im'] = view(wts['ssm_b_re'], 'ssm_b_re')[0], view(wts['ssm_b_im'], 'ssm_b_im')[0]
    w['ssm_log_dt'] = w['ssm_log_dt'] + gather_token[0, 0]
    wide =['ssm_b_re', 'ssm_b_im', 'ssm_c_re', 'ssm_c_im']

    def reduce_wide(gd):
        packed = jnp.concatenate([gd[n].reshape(-1, LANES) for n in wide])
        return _small_allreduce(packed, "small_grads_allreduce")

    loss, dx, grads, wide_sum = _local_step(x[0], loss_target[0], w, get_weights, put_grads, reduce_wide)
    loss = lax.psum(loss, ("x", "y", "c"))

    out_g, out_d, out_m, out_v = {}, {}, {}, {}

    def finish(group, after):
        names = GROUPS[group]
        parts, recv = _exchange_wait(sent[group], after, True, f"grads_wait_{group}")
        same = len({p.shape for p in parts}) == 1
        batches = [list(range(len(names)))] if same else [[i] for i in range(len(names))]
        sums = [None] * len(names)
        for b in batches:
            res = _sum_parts([parts[i] for i in b], [recv[i] for i in b], slots, f"grad_sum_{names[b[0]]}")
            for i, sm in zip(b, res):
                sums[i] = sm
        full = _half_forward(sums, f"grad_half_forward_{group}")
        for b in batches:
            res = _adamw([view(wts[names[i]], names[i]) for i in b], [view(given['m_' + names[i]], names[i]) for i in b],
                         [view(given['v_' + names[i]], names[i]) for i in b], [full[i] for i in b],
                         f"adamw_{names[b[0]]}")
            for i, (d, nm, nv) in zip(b, res):
                n = names[i]
                out_g[n], out_d[n], out_m[n], out_v[n] = (unview(a, n) for a in (full[i], d, nm, nv))
        return nv

    done = finish('ffn2', tokens['ffn1'])
    done = finish('mix', done)

    nat = {n: view(wts[n], n).shape for n in SMALL}
    narrow = [n for n in SMALL if n not in wide]
    alls = _small_exchange([grads[n].reshape(nat[n]) for n in narrow], "small_grads_allgather")
    rows = wide_sum.shape[0] // len(wide)
    wide_g = [wide_sum[i * rows:(i + 1) * rows].reshape((1,) + nat[n]) for i, n in enumerate(wide)]
    for group, gs, split, tag in ((narrow, alls, None, "adamw_small"), (wide, wide_g, (2, 4), "adamw_ssm_bc")):
        res = _adamw_small([view(wts[n], n) for n in group], [view(given['m_' + n], n) for n in group],
                           [view(given['v_' + n], n) for n in group], gs, split, tag)
        for dst, vals in zip((out_g, out_d, out_m, out_v), res):
            for n, a in zip(group, vals):
                dst[n] = unview(a, n)

    finish('ffn1', out_v['norm_ffn1'][:, :1] + out_v['ssm_c_re'].reshape(1, -1)[:, :1] + done[:1, :1] + loss)

    return (loss, dx[None], *[out_g[n] for n in WEIGHTS], *[out_d[n] for n in WEIGHTS],
            *[out_m[n] for n in WEIGHTS], *[out_v[n] for n in WEIGHTS])
```

```python
import functools
import math

import numpy as np
import jax
import jax.numpy as jnp
from jax import lax
from jax.experimental import pallas as pl
from jax.experimental.pallas import tpu as pltpu

F32 = jnp.float32
BF16 = jnp.bfloat16
MESH = pl.DeviceIdType.MESH

EPS = 1e-6
NEG_INF = -1e30
LAMBDA_RE_MAX = -1e-4
ATTN_HEADS = 8
KV_HEADS = 2
GQ = ATTN_HEADS // KV_HEADS
HEAD_DIM = 64
ATTN_WIDTH = 512
KV_WIDTH = 128
WINDOW = 128
QBLOCK = 128
SSM_WIDTH = 512
SSM_GROUPS = 32
SSM_CH = 16
SSM_STATE = 64
N_STRIPS = 4
STRIP_IN = SSM_WIDTH // N_STRIPS
STRIP_ST = SSM_GROUPS * SSM_STATE // N_STRIPS
SUBLANES = 8
LANES = 128
N_CHIPS = 4
N_DEV = 8

ADAM_LR = 0.001
ADAM_B1 = 0.9
ADAM_B2 = 0.999
ADAM_EPS = 1e-08
ADAM_WD = 0.01
ADAM_STEP = 10

VMEM_LIMIT = 48 * 1024 * 1024

WEIGHTS = ['norm_ffn1', 'ffn1_w_gate', 'ffn1_w_up', 'ffn1_w_down', 'norm_mix', 'w_in', 'attn_sinks',
           'ssm_lambda_re', 'ssm_lambda_im', 'ssm_log_dt', 'ssm_b_re', 'ssm_b_im', 'ssm_c_re', 'ssm_c_im',
           'ssm_d', 'ssm_glu_w', 'ssm_glu_b', 'attn_out_norm', 'ssm_out_norm', 'w_out', 'norm_ffn2',
           'ffn2_w_gate', 'ffn2_w_up', 'ffn2_w_down', 'final_norm']
BIG = ['ffn1_w_gate', 'ffn1_w_up', 'ffn1_w_down', 'w_in', 'ssm_glu_w', 'w_out',
       'ffn2_w_gate', 'ffn2_w_up', 'ffn2_w_down']
SMALL = [n for n in WEIGHTS if n not in BIG]
TRANSPOSED = ['ffn1_w_gate', 'ffn1_w_up', 'w_in', 'ffn2_w_gate', 'ffn2_w_up']
GROUPS = {'ffn1': ['ffn1_w_gate', 'ffn1_w_up', 'ffn1_w_down'],
          'mix': ['w_in', 'ssm_glu_w', 'w_out'],
          'ffn2': ['ffn2_w_gate', 'ffn2_w_up', 'ffn2_w_down']}


def _cparams(sem=None):
    return pltpu.CompilerParams(dimension_semantics=sem, vmem_limit_bytes=VMEM_LIMIT)


def _tile(n, pref):
    if n <= pref:
        return n
    for t in (pref, pref // 2, pref // 4):
        if t % LANES == 0 and n % t == 0:
            return t
    return n


def _sigmoid(x):
    return 1.0 / (1.0 + jnp.exp(-x))


def _sigmoid_tanh(x):
    return 0.5 * jnp.tanh(0.5 * x) + 0.5


def _mm(a, b, *, ta=False, tb=False, reduce_s=False, res=None, scale=1.0, out_dtype=F32, after=None, name):
    a3 = a if a.ndim == 3 else a[None]
    b3 = b if b.ndim == 3 else b[None]
    sa, sb = a3.shape[0], b3.shape[0]
    ns = max(sa, sb)
    (kk, m) = a3.shape[1:] if ta else a3.shape[1:][::-1]
    (n, kb) = b3.shape[1:] if tb else b3.shape[1:][::-1]
    assert kk == kb, (a3.shape, b3.shape)
    tm, tn, tk = _tile(m, 1024), _tile(n, 1024), _tile(kk, 2048)
    nm, nn, nk = m // tm, n // tn, kk // tk
    has_res = res is not None
    single = nk == 1 and not (reduce_s and ns > 1)

    if reduce_s:
        grid = (nm, nn, ns, nk)
        ids = lambda i, j, s, k: (s, i, j, k)
        sem = ("parallel", "parallel", "arbitrary", "arbitrary")
    else:
        grid = (ns, nm, nn, nk)
        ids = lambda s, i, j, k: (s, i, j, k)
        sem = ("parallel", "parallel", "parallel", "arbitrary")

    def a_map(*g):
        s, i, j, k = ids(*g)
        s = s if sa > 1 else 0
        return (s, k, i) if ta else (s, i, k)

    def b_map(*g):
        s, i, j, k = ids(*g)
        s = s if sb > 1 else 0
        return (s, j, k) if tb else (s, k, j)

    def o_map(*g):
        s, i, j, k = ids(*g)
        return (i, j) if reduce_s else (s, i, j)

    a_blk = (1, tk, tm) if ta else (1, tm, tk)
    b_blk = (1, tn, tk) if tb else (1, tk, tn)
    dims = (((0 if ta else 1,), (1 if tb else 0,)), ((), ()))

    def body(*refs):
        a_ref, b_ref = refs[0], refs[1]
        r_ref = refs[2] if has_res else None
        o_ref = refs[2 + has_res + (after is not None)]
        acc_ref = None if single else refs[-1]
        s, _, _, k = ids(*[pl.program_id(d) for d in range(4)])
        prod = lax.dot_general(a_ref[0].astype(BF16), b_ref[0].astype(BF16), dims, preferred_element_type=F32)

        def finish(out):
            if scale != 1.0:
                out = out * scale
            if has_res:
                out = r_ref[...].reshape(out.shape) + out
            o_ref[...] = out.astype(out_dtype).reshape(o_ref.shape)

        if single:
            finish(prod)
            return
        if reduce_s:
            first = jnp.logical_and(s == 0, k == 0)
            last = jnp.logical_and(s == ns - 1, k == nk - 1)
        else:
            first, last = k == 0, k == nk - 1

        @pl.when(first)
        def _():
            acc_ref[...] = prod

        @pl.when(jnp.logical_not(first))
        def _():
            acc_ref[...] += prod

        @pl.when(last)
        def _():
            finish(acc_ref[...])

    in_specs = [pl.BlockSpec(a_blk, a_map), pl.BlockSpec(b_blk, b_map)]
    args = [a3, b3]
    if reduce_s:
        out_shape = jax.ShapeDtypeStruct((m, n), out_dtype)
        o_spec = pl.BlockSpec((tm, tn), o_map)
    else:
        out_shape = jax.ShapeDtypeStruct((ns, m, n), out_dtype)
        o_spec = pl.BlockSpec((1, tm, tn), o_map)
    if has_res:
        assert res.shape == out_shape.shape
        in_specs.append(o_spec)
        args.append(res)
    if after is not None:
        in_specs.append(HBM_SPEC)
        args.append(after)
    return pl.pallas_call(body, out_shape=out_shape, grid=grid, in_specs=in_specs, out_specs=o_spec,
                          scratch_shapes=[] if single else [pltpu.VMEM((tm, tn), F32)],
                          compiler_params=_cparams(sem), name=name)(*args)


def _row_tile(t, cap=256):
    for step in (16, SUBLANES):
        for tr in range(min(cap, t) // step * step, 0, -step):
            if t % tr == 0:
                return tr
    return t


def _rms_fwd(x, g, name):
    t, w = x.shape
    tr = _row_tile(t)

    def body(x_ref, g_ref, o_ref):
        xv = x_ref[...]
        r = lax.rsqrt(jnp.mean(xv * xv, axis=-1, keepdims=True) + EPS)
        o_ref[...] = (xv * r * g_ref[...]).astype(BF16)

    return pl.pallas_call(
        body, out_shape=jax.ShapeDtypeStruct((t, w), BF16), grid=(t // tr,),
        in_specs=[pl.BlockSpec((tr, w), lambda i: (i, 0)), pl.BlockSpec((1, w), lambda i: (0, 0))],
        out_specs=pl.BlockSpec((tr, w), lambda i: (i, 0)), compiler_params=_cparams(("parallel",)),
        name=name)(x, g)


def _rms_bwd_rows(xv, gv, dhv):
    r = lax.rsqrt(jnp.mean(xv * xv, axis=-1, keepdims=True) + EPS)
    nrm = xv * r
    dn = dhv * gv
    return r * (dn - nrm * jnp.mean(dn * nrm, axis=-1, keepdims=True)), dhv * nrm


def _rms_bwd(x, g, dh, dres, name):
    t, w = x.shape
    tr = _row_tile(t)
    has_res = dres is not None

    def body(*refs):
        if has_res:
            x_ref, g_ref, dh_ref, dr_ref, dx_ref, dxb_ref, dg_ref = refs
        else:
            x_ref, g_ref, dh_ref, dx_ref, dxb_ref, dg_ref = refs
        dx, dgs = _rms_bwd_rows(x_ref[...], g_ref[...], dh_ref[...])
        if has_res:
            dx = dx + dr_ref[...]
        dx_ref[...] = dx
        dxb_ref[...] = dx.astype(BF16)

        @pl.when(pl.program_id(0) == 0)
        def _():
            dg_ref[...] = jnp.zeros_like(dg_ref)

        dg_ref[...] += jnp.sum(dgs, axis=0, keepdims=True)

    row = pl.BlockSpec((tr, w), lambda i: (i, 0))
    vec = pl.BlockSpec((1, w), lambda i: (0, 0))
    ins = [x, g, dh] + ([dres] if has_res else [])
    return pl.pallas_call(
        body, out_shape=(jax.ShapeDtypeStruct((t, w), F32), jax.ShapeDtypeStruct((t, w), BF16),
                         jax.ShapeDtypeStruct((1, w), F32)),
        grid=(t // tr,), in_specs=[row, vec, row] + ([row] if has_res else []),
        out_specs=(row, row, vec), compiler_params=_cparams(("arbitrary",)), name=name)(*ins)


FFN_ROWS = 512
FFN_W_ROWS = 1024
SCAN_ROWS = 256


NT_DIMS = (((1,), (1,)), ((), ()))
TN_DIMS = (((0,), (0,)), ((), ()))


def _ffn_fwd_call(x, g, wg, wu, wd, name):
    t, d = x.shape
    ns, f, _ = wg.shape
    tm = _tile(t, FFN_ROWS)

    def body(x_ref, g_ref, wg_ref, wu_ref, wd_ref, xo_ref, h_ref, gate_ref, up_ref, h_sc, acc_ref):
        s = pl.program_id(1)

        @pl.when(s == 0)
        def _():
            xv = x_ref[...]
            r = lax.rsqrt(jnp.mean(xv * xv, axis=-1, keepdims=True) + EPS)
            hb = (xv * r * g_ref[...]).astype(BF16)
            h_sc[...] = hb
            h_ref[...] = hb

        hb = h_sc[...]
        gate = lax.dot_general(hb, wg_ref[0], NT_DIMS, preferred_element_type=F32)
        up = lax.dot_general(hb, wu_ref[0], NT_DIMS, preferred_element_type=F32)
        gate_ref[0] = gate.astype(BF16)
        up_ref[0] = up.astype(BF16)
        act = (gate * _sigmoid_tanh(gate) * up).astype(BF16)
        prod = jnp.dot(act, wd_ref[0], preferred_element_type=F32)

        @pl.when(s == 0)
        def _():
            acc_ref[...] = prod

        @pl.when(s > 0)
        def _():
            acc_ref[...] += prod

        @pl.when(s == ns - 1)
        def _():
            xo_ref[...] = x_ref[...] + 0.5 * acc_ref[...]

    row = pl.BlockSpec((tm, d), lambda i, s: (i, 0))
    vec = pl.BlockSpec((1, d), lambda i, s: (0, 0))
    wrow = pl.BlockSpec((1, f, d), lambda i, s: (s, 0, 0))
    hid = pl.BlockSpec((1, tm, f), lambda i, s: (s, i, 0))
    hid_sh = jax.ShapeDtypeStruct((ns, t, f), BF16)
    return pl.pallas_call(
        body, out_shape=(jax.ShapeDtypeStruct((t, d), F32), jax.ShapeDtypeStruct((t, d), BF16), hid_sh, hid_sh),
        grid=(t // tm, ns), in_specs=[row, vec, wrow, wrow, wrow], out_specs=(row, row, hid, hid),
        scratch_shapes=[pltpu.VMEM((tm, d), BF16), pltpu.VMEM((tm, d), F32)],
        compiler_params=_cparams(("parallel", "arbitrary")), name=name)(x, g, wg, wu, wd)


def _ffn_bwd_x_call(dxo, dxo_b, x, g, gate, up, wg, wu, wd, name):
    t, d = x.shape
    ns, f, _ = wg.shape
    tm = _tile(t, FFN_ROWS)

    def body(dxo_ref, dxb_ref, x_ref, g_ref, gate_ref, up_ref, wg_ref, wu_ref, wd_ref,
             dx_ref, dxob_ref, dgn_ref, dgate_ref, dup_ref, act_ref, dh_ref):
        i, s = pl.program_id(0), pl.program_id(1)
        dact = lax.dot_general(dxb_ref[...], wd_ref[0], NT_DIMS, preferred_element_type=F32) * 0.5
        gv = gate_ref[0].astype(F32)
        uv = up_ref[0].astype(F32)
        sg = _sigmoid_tanh(gv)
        silu = gv * sg
        act_ref[0] = (silu * uv).astype(BF16)
        dub = (dact * silu).astype(BF16)
        dgb = (dact * uv * sg * (1.0 + gv * (1.0 - sg))).astype(BF16)
        dup_ref[0] = dub
        dgate_ref[0] = dgb
        prod = (jnp.dot(dgb, wg_ref[0], preferred_element_type=F32)
                + jnp.dot(dub, wu_ref[0], preferred_element_type=F32))

        @pl.when(s == 0)
        def _():
            dh_ref[...] = prod

        @pl.when(s > 0)
        def _():
            dh_ref[...] += prod

        @pl.when(jnp.logical_and(i == 0, s == 0))
        def _():
            dgn_ref[...] = jnp.zeros_like(dgn_ref)

        @pl.when(s == ns - 1)
        def _():
            dx, dgs = _rms_bwd_rows(x_ref[...], g_ref[...], dh_ref[...])
            dx = dx + dxo_ref[...]
            dx_ref[...] = dx
            dxob_ref[...] = dx.astype(BF16)
            dgn_ref[...] += jnp.sum(dgs, axis=0, keepdims=True)

    row = pl.BlockSpec((tm, d), lambda i, s: (i, 0))
    vec = pl.BlockSpec((1, d), lambda i, s: (0, 0))
    wrow = pl.BlockSpec((1, f, d), lambda i, s: (s, 0, 0))
    hid = pl.BlockSpec((1, tm, f), lambda i, s: (s, i, 0))
    hid_sh = jax.ShapeDtypeStruct((ns, t, f), BF16)
    return pl.pallas_call(
        body,
        out_shape=(jax.ShapeDtypeStruct((t, d), F32), jax.ShapeDtypeStruct((t, d), BF16),
                   jax.ShapeDtypeStruct((1, d), F32), hid_sh, hid_sh, hid_sh),
        grid=(t // tm, ns), in_specs=[row, row, row, vec, hid, hid, wrow, wrow, wrow],
        out_specs=(row, row, vec, hid, hid, hid), scratch_shapes=[pltpu.VMEM((tm, d), F32)],
        compiler_params=_cparams(("arbitrary", "arbitrary")), name=name)(dxo, dxo_b, x, g, gate, up, wg, wu, wd)


def _ffn_bwd_w_call(h, dxo_b, dgate, dup, act, name):
    t, d = h.shape
    ns, _, f = dgate.shape
    tm = _tile(t, FFN_W_ROWS)
    nm = t // tm

    def body(h_ref, dxb_ref, dgate_ref, dup_ref, act_ref, dwg_ref, dwu_ref, dwd_ref, ag_ref, au_ref, ad_ref):
        i = pl.program_id(1)
        hv = h_ref[...]
        pg = lax.dot_general(dgate_ref[0], hv, TN_DIMS, preferred_element_type=F32)
        pu = lax.dot_general(dup_ref[0], hv, TN_DIMS, preferred_element_type=F32)
        pd = lax.dot_general(act_ref[0], dxb_ref[...], TN_DIMS, preferred_element_type=F32)

        @pl.when(i == 0)
        def _():
            ag_ref[...] = pg
            au_ref[...] = pu
            ad_ref[...] = pd

        @pl.when(i > 0)
        def _():
            ag_ref[...] += pg
            au_ref[...] += pu
            ad_ref[...] += pd

        @pl.when(i == nm - 1)
        def _():
            dwg_ref[0] = ag_ref[...].astype(BF16)
            dwu_ref[0] = au_ref[...].astype(BF16)
            dwd_ref[0] = (0.5 * ad_ref[...]).astype(BF16)

    row = pl.BlockSpec((tm, d), lambda s, i: (i, 0))
    hid = pl.BlockSpec((1, tm, f), lambda s, i: (s, i, 0))
    wrow = pl.BlockSpec((1, f, d), lambda s, i: (s, 0, 0))
    wsh = jax.ShapeDtypeStruct((ns, f, d), BF16)
    return pl.pallas_call(
        body, out_shape=(wsh, wsh, wsh),
        grid=(ns, nm), in_specs=[row, row, hid, hid, hid], out_specs=(wrow, wrow, wrow),
        scratch_shapes=[pltpu.VMEM((f, d), F32), pltpu.VMEM((f, d), F32), pltpu.VMEM((f, d), F32)],
        compiler_params=_cparams(("parallel", "arbitrary")), name=name)(h, dxo_b, dgate, dup, act)


def _loss_head(x, g, tgt, name):
    t, w = x.shape
    tr = _row_tile(t)

    def body(x_ref, g_ref, t_ref, loss_ref, dx_ref, dxb_ref, dg_ref):
        xv = x_ref[...]
        gv = g_ref[...]
        r = lax.rsqrt(jnp.mean(xv * xv, axis=-1, keepdims=True) + EPS)
        nrm = xv * r
        err = nrm * gv - t_ref[...]
        dout = err * (1.0 / w)
        dn = dout * gv
        dx = r * (dn - nrm * jnp.mean(dn * nrm, axis=-1, keepdims=True))
        dx_ref[...] = dx
        dxb_ref[...] = dx.astype(BF16)

        @pl.when(pl.program_id(0) == 0)
        def _():
            dg_ref[...] = jnp.zeros_like(dg_ref)
            loss_ref[...] = jnp.zeros_like(loss_ref)

        dg_ref[...] += jnp.sum(dout * nrm, axis=0, keepdims=True)
        part = jnp.sum(jnp.sum(err * err, axis=-1, keepdims=True) * (0.5 / w), axis=0, keepdims=True)
        loss_ref[...] += jnp.broadcast_to(part, loss_ref.shape)

    row = pl.BlockSpec((tr, w), lambda i: (i, 0))
    vec = pl.BlockSpec((1, w), lambda i: (0, 0))
    return pl.pallas_call(
        body, out_shape=(jax.ShapeDtypeStruct((1, LANES), F32), jax.ShapeDtypeStruct((t, w), F32),
                         jax.ShapeDtypeStruct((t, w), BF16), jax.ShapeDtypeStruct((1, w), F32)),
        grid=(t // tr,), in_specs=[row, vec, row],
        out_specs=(pl.BlockSpec((1, LANES), lambda i: (0, 0)), row, row, vec),
        compiler_params=_cparams(("arbitrary",)), name=name)(x, g, tgt)


def _attn_bias():
    slopes = np.asarray(2.0 ** (-8.0 * (np.arange(ATTN_HEADS) + 1) / ATTN_HEADS), np.float32)
    qi = np.arange(QBLOCK)[:, None]
    kj = np.arange(3 * QBLOCK)[None, :]
    rel = np.abs(kj - QBLOCK - qi).astype(np.float32)
    tile = np.where(rel <= WINDOW, -slopes[:, None, None] * rel[None], np.float32(NEG_INF)).astype(np.float32)
    return jnp.asarray(tile.reshape(KV_HEADS, GQ * QBLOCK, 3 * QBLOCK))


def _attn_scores(q, k3, n, nb, bias):
    s = lax.dot_general(q, k3, NT_DIMS, preferred_element_type=F32) * (HEAD_DIM ** -0.5)
    col = lax.broadcasted_iota(jnp.int32, (1, 3 * QBLOCK), 1)
    inside = (col >= jnp.where(n == 0, QBLOCK, 0)) & (col < jnp.where(n == nb - 1, 2 * QBLOCK, 3 * QBLOCK))
    return jnp.where(inside, s + bias, NEG_INF)


Q_COL, K_COL, V_COL, U_COL = 0, ATTN_WIDTH // LANES, ATTN_WIDTH // LANES + 1, ATTN_WIDTH // LANES + 2


def _key_rows(ref, n, nb):
    prev, nxt = jnp.maximum(n - 1, 0), jnp.minimum(n + 1, nb - 1)
    blk = lambda b: ref[pl.ds(pl.multiple_of(b * QBLOCK, QBLOCK), QBLOCK), :]
    return jnp.concatenate([blk(prev), blk(n), blk(nxt)], axis=0)


def _head_tiles(x, kh, low):
    tiles = []
    for g in range(GQ):
        h = GQ * kh + g
        t128 = x[:, LANES * (h // 2):LANES * (h // 2 + 1)]
        t128 = jnp.where(low if h % 2 == 0 else jnp.logical_not(low), t128, 0.0)
        if h % 2 != kh:
            t128 = pltpu.roll(t128, HEAD_DIM, 1)
        tiles.append(t128)
    return jnp.concatenate(tiles, axis=0)


def _head_merge(per_kh, low):
    out = []
    for j in range(ATTN_HEADS // 2):
        pair = []
        for h in (2 * j, 2 * j + 1):
            kh, g = h // GQ, h % GQ
            t128 = per_kh[kh][g * QBLOCK:(g + 1) * QBLOCK, :]
            if h % 2 != kh:
                t128 = pltpu.roll(t128, HEAD_DIM, 1)
            pair.append(t128)
        out.append(jnp.where(low, pair[0], pair[1]))
    return jnp.concatenate(out, axis=1)


def _attn_fwd_proj(proj, sink_rows, bias, name):
    t = proj.shape[0]
    nb = t // QBLOCK
    rows = GQ * QBLOCK

    def body(q_ref, k_ref, v_ref, sink_ref, bias_ref, o_ref, lse_ref):
        n = pl.program_id(0)
        low = lax.broadcasted_iota(jnp.int32, (QBLOCK, LANES), 1) < HEAD_DIM
        k3 = _key_rows(k_ref, n, nb).astype(BF16)
        v3 = _key_rows(v_ref, n, nb).astype(BF16)
        q = q_ref[...]
        outs = []
        for kh in range(KV_HEADS):
            qs = _head_tiles(q, kh, low).astype(BF16)
            s = _attn_scores(qs, k3, n, nb, bias_ref[kh])
            sink = sink_ref[kh]
            mx = jnp.maximum(jnp.max(s, axis=-1, keepdims=True), sink)
            p = jnp.exp(s - mx)
            den = jnp.sum(p, axis=-1, keepdims=True) + jnp.exp(sink - mx)
            outs.append(jnp.dot(p.astype(BF16), v3, preferred_element_type=F32) / den)
            lse_ref[0, kh] = mx + jnp.log(den)
        o_ref[...] = _head_merge(outs, low)

    strip = lambda col: pl.BlockSpec((t, LANES), lambda n, col=col: (0, col))
    rowspec = pl.BlockSpec((KV_HEADS, rows, 1), lambda n: (0, 0, 0))
    biasspec = pl.BlockSpec((KV_HEADS, rows, 3 * QBLOCK), lambda n: (0, 0, 0))
    return pl.pallas_call(
        body, out_shape=(jax.ShapeDtypeStruct((t, ATTN_WIDTH), F32), jax.ShapeDtypeStruct((nb, KV_HEADS, rows, 1), F32)),
        grid=(nb,), in_specs=[pl.BlockSpec((QBLOCK, ATTN_WIDTH), lambda n: (n, 0)), strip(K_COL), strip(V_COL),
                              rowspec, biasspec],
        out_specs=(pl.BlockSpec((QBLOCK, ATTN_WIDTH), lambda n: (n, 0)),
                   pl.BlockSpec((1, KV_HEADS, rows, 1), lambda n: (n, 0, 0, 0))),
        compiler_params=_cparams(("parallel",)), name=name)(proj, proj, proj, sink_rows, bias)


def _attn_bwd_proj(proj, sink_rows, bias, o, lse, do, name):
    t = proj.shape[0]
    nb = t // QBLOCK
    rows = GQ * QBLOCK
    scale = HEAD_DIM ** -0.5

    def body(q_ref, k_ref, v_ref, sink_ref, bias_ref, o_ref, lse_ref, do_ref, dq_ref, dk_ref, dv_ref, ds_ref):
        n = pl.program_id(0)

        @pl.when(n == 0)
        def _():
            dk_ref[...] = jnp.zeros_like(dk_ref)
            dv_ref[...] = jnp.zeros_like(dv_ref)
            ds_ref[...] = jnp.zeros_like(ds_ref)

        low = lax.broadcasted_iota(jnp.int32, (QBLOCK, LANES), 1) < HEAD_DIM
        k3 = _key_rows(k_ref, n, nb).astype(BF16)
        v3 = _key_rows(v_ref, n, nb).astype(BF16)
        q, dov = q_ref[...], do_ref[...]
        dod = dov * o_ref[...]
        dqs = []
        dk3 = jnp.zeros((3 * QBLOCK, LANES), F32)
        dv3 = jnp.zeros((3 * QBLOCK, LANES), F32)
        for kh in range(KV_HEADS):
            qs = _head_tiles(q, kh, low).astype(BF16)
            dos = _head_tiles(dov, kh, low).astype(BF16)
            delta = jnp.sum(_head_tiles(dod, kh, low), axis=-1, keepdims=True)
            lse_kh = lse_ref[0, kh]
            s = _attn_scores(qs, k3, n, nb, bias_ref[kh])
            p = jnp.exp(s - lse_kh)
            dp = lax.dot_general(dos, v3, NT_DIMS, preferred_element_type=F32)
            dsb = (p * (dp - delta)).astype(BF16)
            dqs.append(jnp.dot(dsb, k3, preferred_element_type=F32) * scale)
            dk3 = dk3 + lax.dot_general(dsb, qs, TN_DIMS, preferred_element_type=F32) * scale
            dv3 = dv3 + lax.dot_general(p.astype(BF16), dos, TN_DIMS, preferred_element_type=F32)
            dsink_rows = -jnp.exp(sink_ref[kh] - lse_kh) * delta
            ds_ref[kh] += jnp.sum(dsink_rows.reshape(GQ, QBLOCK, 1), axis=1)
        dq_ref[...] = _head_merge(dqs, low)
        prev, nxt = jnp.maximum(n - 1, 0), jnp.minimum(n + 1, nb - 1)
        for j, b in enumerate((prev, n, nxt)):
            blk = pl.ds(pl.multiple_of(b * QBLOCK, QBLOCK), QBLOCK)
            dk_ref[blk, :] += dk3[j * QBLOCK:(j + 1) * QBLOCK, :]
            dv_ref[blk, :] += dv3[j * QBLOCK:(j + 1) * QBLOCK, :]

    strip = lambda col: pl.BlockSpec((t, LANES), lambda n, col=col: (0, col))
    rowspec = pl.BlockSpec((KV_HEADS, rows, 1), lambda n: (0, 0, 0))
    qspec = pl.BlockSpec((QBLOCK, ATTN_WIDTH), lambda n: (n, 0))
    kv_out = pl.BlockSpec((t, LANES), lambda n: (0, 0))
    biasspec = pl.BlockSpec((KV_HEADS, rows, 3 * QBLOCK), lambda n: (0, 0, 0))
    return pl.pallas_call(
        body,
        out_shape=(jax.ShapeDtypeStruct((t, ATTN_WIDTH), F32), jax.ShapeDtypeStruct((t, LANES), F32),
                   jax.ShapeDtypeStruct((t, LANES), F32), jax.ShapeDtypeStruct((KV_HEADS, GQ, 1), F32)),
        grid=(nb,),
        in_specs=[qspec, strip(K_COL), strip(V_COL), rowspec, biasspec, qspec,
                  pl.BlockSpec((1, KV_HEADS, rows, 1), lambda n: (n, 0, 0, 0)), qspec],
        out_specs=(qspec, kv_out, kv_out, pl.BlockSpec((KV_HEADS, GQ, 1), lambda n: (0, 0, 0))),
        compiler_params=_cparams(("arbitrary",)), name=name)(proj, proj, proj, sink_rows, bias, o, lse, do)


def _scan_tables(a_re, a_im, reverse):
    pw = [(a_re, a_im)]
    for _ in range(SUBLANES - 1):
        pr, pi = pw[-1]
        pw.append((pr * a_re - pi * a_im, pr * a_im + pi * a_re))
    rows = np.arange(SUBLANES)
    tabs = []
    for d in (1, 2, 4):
        mask = (rows <= SUBLANES - 1 - d) if reverse else (rows >= d)
        m = jnp.asarray(mask, F32)[:, None]
        tabs += [m * pw[d - 1][0][None, :], m * pw[d - 1][1][None, :]]
    order = (SUBLANES - 1 - rows) if reverse else rows
    tabs += [jnp.stack([pw[j][0] for j in order]), jnp.stack([pw[j][1] for j in order])]
    tab = jnp.stack(tabs)
    return tab.reshape(8, SUBLANES, N_STRIPS, STRIP_ST).transpose(2, 0, 1, 3)


def _scan_chunk(v_ref, mir_ref, mii_ref, tab_ref, mor_ref, moi_ref, y_ref, xr_ref, xi_ref, carry_ref, reverse):
    nblk = xr_ref.shape[0] // SUBLANES

    @pl.when(pl.program_id(1) == 0)
    def _():
        carry_ref[...] = jnp.zeros_like(carry_ref)

    vb = v_ref[...].astype(BF16)
    xr_ref[...] = jnp.dot(vb, mir_ref[0], preferred_element_type=F32)
    xi_ref[...] = jnp.dot(vb, mii_ref[0], preferred_element_type=F32)

    def blk(i, carry):
        cr, ci = carry
        b = (nblk - 1 - i) if reverse else i
        r0 = pl.multiple_of(b * SUBLANES, SUBLANES)
        xr = xr_ref[pl.ds(r0, SUBLANES), :]
        xi = xi_ref[pl.ds(r0, SUBLANES), :]
        for j, d in enumerate((1, 2, 4)):
            tr_, ti_ = tab_ref[0, 2 * j], tab_ref[0, 2 * j + 1]
            sh = (SUBLANES - d) if reverse else d
            sr = pltpu.roll(xr, sh, 0)
            si = pltpu.roll(xi, sh, 0)
            xr, xi = xr + tr_ * sr - ti_ * si, xi + tr_ * si + ti_ * sr
        pr, pi = tab_ref[0, 6], tab_ref[0, 7]
        xr, xi = xr + pr * cr - pi * ci, xi + pr * ci + pi * cr
        xr_ref[pl.ds(r0, SUBLANES), :] = xr
        xi_ref[pl.ds(r0, SUBLANES), :] = xi
        edge = 0 if reverse else SUBLANES - 1
        return (jnp.broadcast_to(xr[edge:edge + 1, :], xr.shape),
                jnp.broadcast_to(xi[edge:edge + 1, :], xi.shape))

    cr, ci = lax.fori_loop(0, nblk, blk, (carry_ref[0], carry_ref[1]))
    carry_ref[0] = cr
    carry_ref[1] = ci
    y_ref[...] = (jnp.dot(xr_ref[...].astype(BF16), mor_ref[0], preferred_element_type=F32)
                  + jnp.dot(xi_ref[...].astype(BF16), moi_ref[0], preferred_element_type=F32))


def _scan_adjoint(dy, u, xr, xi, mi_re, mi_im, tab, mo_re, mo_im, reverse, name):
    t = dy.shape[0]
    tc = _tile(t, SCAN_ROWS)
    nc = t // tc
    hb = tc // SUBLANES
    fwd_reverse = not reverse

    def body(dy_ref, mir_ref, mii_ref, tab_ref, mor_ref, moi_ref, u_ref, xr_ref, xi_ref, hr_ref, hi_ref,
             du_ref, dmir_ref, dmii_ref, dmor_ref, dmoi_ref, da_ref, lr_ref, li_ref, carry_ref):
        _scan_chunk(dy_ref, mir_ref, mii_ref, tab_ref, mor_ref, moi_ref, du_ref, lr_ref, li_ref, carry_ref, reverse)
        c = pl.program_id(1)
        rc = (nc - 1 - c) if reverse else c

        @pl.when(c == 0)
        def _():
            for r in (dmir_ref, dmii_ref, dmor_ref, dmoi_ref, da_ref):
                r[...] = jnp.zeros_like(r)

        xrv, xiv, lrv, liv = xr_ref[...], xi_ref[...], lr_ref[...], li_ref[...]
        row = lax.broadcasted_iota(jnp.int32, xrv.shape, 0)
        if fwd_reverse:
            live = (rc < nc - 1).astype(F32)
            edge_r, edge_i = hr_ref[0:1, :] * live, hi_ref[0:1, :] * live
            xpr = jnp.where(row == tc - 1, edge_r, pltpu.roll(xrv, tc - 1, 0))
            xpi = jnp.where(row == tc - 1, edge_i, pltpu.roll(xiv, tc - 1, 0))
        else:
            live = (rc > 0).astype(F32)
            edge_r, edge_i = hr_ref[SUBLANES - 1:SUBLANES, :] * live, hi_ref[SUBLANES - 1:SUBLANES, :] * live
            xpr = jnp.where(row == 0, edge_r, pltpu.roll(xrv, 1, 0))
            xpi = jnp.where(row == 0, edge_i, pltpu.roll(xiv, 1, 0))
        da_ref[0, 0:1, :] += jnp.sum(xpr * lrv + xpi * liv, axis=0, keepdims=True)
        da_ref[0, 1:2, :] += jnp.sum(xpr * liv - xpi * lrv, axis=0, keepdims=True)
        ub, dyb = u_ref[...].astype(BF16), dy_ref[...].astype(BF16)
        dmir_ref[0] += lax.dot_general(ub, lrv.astype(BF16), TN_DIMS, preferred_element_type=F32)
        dmii_ref[0] += lax.dot_general(ub, liv.astype(BF16), TN_DIMS, preferred_element_type=F32)
        dmor_ref[0] += lax.dot_general(xrv.astype(BF16), dyb, TN_DIMS, preferred_element_type=F32)
        dmoi_ref[0] += lax.dot_general(xiv.astype(BF16), dyb, TN_DIMS, preferred_element_type=F32)

    rowblk = (lambda c: nc - 1 - c) if reverse else (lambda c: c)
    tmap = lambda s, c: (rowblk(c), s)
    col0 = u.shape[1] // STRIP_IN - N_STRIPS
    umap = lambda s, c: (rowblk(c), s + col0)
    if fwd_reverse:
        hmap = lambda s, c: (jnp.minimum((rowblk(c) + 1) * hb, t // SUBLANES - 1), s)
    else:
        hmap = lambda s, c: (jnp.maximum(rowblk(c) * hb - 1, 0), s)
    smap3 = lambda s, c: (s, 0, 0)
    narrow = pl.BlockSpec((tc, STRIP_IN), tmap)
    wide = pl.BlockSpec((tc, STRIP_ST), tmap)
    halo = pl.BlockSpec((SUBLANES, STRIP_ST), hmap)
    m_in = pl.BlockSpec((1, STRIP_IN, STRIP_ST), smap3)
    m_out = pl.BlockSpec((1, STRIP_ST, STRIP_IN), smap3)
    return pl.pallas_call(
        body,
        out_shape=(jax.ShapeDtypeStruct((t, SSM_WIDTH), F32),
                   jax.ShapeDtypeStruct((N_STRIPS, STRIP_IN, STRIP_ST), F32),
                   jax.ShapeDtypeStruct((N_STRIPS, STRIP_IN, STRIP_ST), F32),
                   jax.ShapeDtypeStruct((N_STRIPS, STRIP_ST, STRIP_IN), F32),
                   jax.ShapeDtypeStruct((N_STRIPS, STRIP_ST, STRIP_IN), F32),
                   jax.ShapeDtypeStruct((N_STRIPS, SUBLANES, STRIP_ST), F32)),
        grid=(N_STRIPS, nc),
        in_specs=[narrow, m_in, m_in, pl.BlockSpec((1, 8, SUBLANES, STRIP_ST), lambda s, c: (s, 0, 0, 0)),
                  m_out, m_out, pl.BlockSpec((tc, STRIP_IN), umap), wide, wide, halo, halo],
        out_specs=(narrow, m_in, m_in, m_out, m_out, pl.BlockSpec((1, SUBLANES, STRIP_ST), smap3)),
        scratch_shapes=[pltpu.VMEM((tc, STRIP_ST), F32), pltpu.VMEM((tc, STRIP_ST), F32),
                        pltpu.VMEM((2, SUBLANES, STRIP_ST), F32)],
        compiler_params=_cparams(("parallel", "arbitrary")), name=name)(
            dy, mi_re, mi_im, tab, mo_re, mo_im, u, xr, xi, xr, xi)


def _scan(v, mi_re, mi_im, tab, mo_re, mo_im, reverse, name):
    t = v.shape[0]
    tc = _tile(t, SCAN_ROWS)
    nc = t // tc

    def body(v_ref, mir_ref, mii_ref, tab_ref, mor_ref, moi_ref, y_ref, xr_ref, xi_ref, carry_ref):
        _scan_chunk(v_ref, mir_ref, mii_ref, tab_ref, mor_ref, moi_ref, y_ref, xr_ref, xi_ref, carry_ref, reverse)

    tmap = (lambda s, c: (nc - 1 - c, s)) if reverse else (lambda s, c: (c, s))
    col0 = v.shape[1] // STRIP_IN - N_STRIPS
    vmap = lambda s, c: (tmap(s, c)[0], s + col0)
    smap3 = lambda s, c: (s, 0, 0)
    return pl.pallas_call(
        body,
        out_shape=(jax.ShapeDtypeStruct((t, SSM_WIDTH), F32),
                   jax.ShapeDtypeStruct((t, N_STRIPS * STRIP_ST), F32),
                   jax.ShapeDtypeStruct((t, N_STRIPS * STRIP_ST), F32)),
        grid=(N_STRIPS, nc),
        in_specs=[pl.BlockSpec((tc, STRIP_IN), vmap),
                  pl.BlockSpec((1, STRIP_IN, STRIP_ST), smap3), pl.BlockSpec((1, STRIP_IN, STRIP_ST), smap3),
                  pl.BlockSpec((1, 8, SUBLANES, STRIP_ST), lambda s, c: (s, 0, 0, 0)),
                  pl.BlockSpec((1, STRIP_ST, STRIP_IN), smap3), pl.BlockSpec((1, STRIP_ST, STRIP_IN), smap3)],
        out_specs=(pl.BlockSpec((tc, STRIP_IN), tmap), pl.BlockSpec((tc, STRIP_ST), tmap),
                   pl.BlockSpec((tc, STRIP_ST), tmap)),
        scratch_shapes=[pltpu.VMEM((2, SUBLANES, STRIP_ST), F32)],
        compiler_params=_cparams(("parallel", "arbitrary")), name=name)(v, mi_re, mi_im, tab, mo_re, mo_im)


def _ssm_prep(lam_re, lam_im, log_dt, bt_re, bt_im, c_re, c_im):
    lr = jnp.minimum(lam_re, LAMBDA_RE_MAX)
    li = lam_im
    dt = jnp.exp(log_dt)[:, None]
    mag = jnp.exp(lr * dt)
    a_re = mag * jnp.cos(li * dt)
    a_im = mag * jnp.sin(li * dt)
    den = lr * lr + li * li
    coef_re = ((a_re - 1.0) * lr + a_im * li) / den
    coef_im = (a_im * lr - (a_re - 1.0) * li) / den
    bb_re = coef_re[:, None, :] * bt_re - coef_im[:, None, :] * bt_im
    bb_im = coef_re[:, None, :] * bt_im + coef_im[:, None, :] * bt_re
    eye = jnp.eye(SSM_GROUPS // N_STRIPS, dtype=F32)

    def strips(m):
        g, a, b = m.shape
        m4 = m.reshape(N_STRIPS, g // N_STRIPS, a, b)
        return jnp.einsum('sgab,gk->sgakb', m4, eye).reshape(N_STRIPS, g // N_STRIPS * a, g // N_STRIPS * b)

    mi_re = strips(bb_re)
    mi_im = strips(bb_im)
    mo_re = strips(jnp.swapaxes(c_re, 1, 2))
    mo_im = strips(-jnp.swapaxes(c_im, 1, 2))
    return a_re.reshape(-1), a_im.reshape(-1), mi_re, mi_im, mo_re, mo_im


def _gelu(x):
    c = math.sqrt(2.0 / math.pi)
    return 0.5 * x * (1.0 + jnp.tanh(c * (x + 0.044715 * x * x * x)))


def _gelu_grad(x):
    c = math.sqrt(2.0 / math.pi)
    th = jnp.tanh(c * (x + 0.044715 * x * x * x))
    return 0.5 * (1.0 + th) + 0.5 * x * (1.0 - th * th) * c * (1.0 + 3.0 * 0.044715 * x * x)


def _last_cols_specs(u, w, tr):
    half = w // 2
    first = (u.shape[1] - w) // half
    assert first * half == u.shape[1] - w
    return [pl.BlockSpec((tr, half), lambda i, k=k: (i, first + k)) for k in range(2)]


def _ssm_post_fwd(u, yf, yb, d, wglu, bglu, name):
    t, w = yf.shape
    tr = _row_tile(t)

    def body(ua_ref, ub_ref, yf_ref, yb_ref, d_ref, w_ref, b_ref, s_ref, y0_ref, z_ref):
        uv = jnp.concatenate([ua_ref[...], ub_ref[...]], axis=1)
        y0 = d_ref[...] * uv + yf_ref[...] + yb_ref[...]
        yg = _gelu(y0)
        z = jnp.dot(yg.astype(BF16), w_ref[...], preferred_element_type=F32) + b_ref[...]
        s_ref[...] = yg * _sigmoid(z)
        y0_ref[...] = y0
        z_ref[...] = z

    row = pl.BlockSpec((tr, w), lambda i: (i, 0))
    vec = pl.BlockSpec((1, w), lambda i: (0, 0))
    mat = pl.BlockSpec((w, w), lambda i: (0, 0))
    sh = jax.ShapeDtypeStruct((t, w), F32)
    return pl.pallas_call(body, out_shape=(sh, sh, sh), grid=(t // tr,),
                          in_specs=[*_last_cols_specs(u, w, tr), row, row, vec, mat, vec], out_specs=(row, row, row),
                          compiler_params=_cparams(("parallel",)), name=name)(u, u, yf, yb, d, wglu, bglu)


def _ssm_post_bwd(ds, y0, z, u, d, wglu, name):
    t, w = ds.shape
    tr = _row_tile(t)

    def body(ds_ref, y0_ref, z_ref, ua_ref, ub_ref, d_ref, w_ref, dy0_ref, dw_ref, db_ref, dd_ref):
        @pl.when(pl.program_id(0) == 0)
        def _():
            dw_ref[...] = jnp.zeros_like(dw_ref)
            db_ref[...] = jnp.zeros_like(db_ref)
            dd_ref[...] = jnp.zeros_like(dd_ref)

        y0 = y0_ref[...]
        yg = _gelu(y0)
        sg = _sigmoid(z_ref[...])
        dsv = ds_ref[...]
        dz = dsv * yg * sg * (1.0 - sg)
        dzb = dz.astype(BF16)
        dyg = dsv * sg + lax.dot_general(dzb, w_ref[...], (((1,), (1,)), ((), ())), preferred_element_type=F32)
        dy0 = dyg * _gelu_grad(y0)
        dy0_ref[...] = dy0
        dw_ref[...] += lax.dot_general(yg.astype(BF16), dzb, (((0,), (0,)), ((), ())), preferred_element_type=F32)
        db_ref[...] += jnp.sum(dz, axis=0, keepdims=True)
        uv = jnp.concatenate([ua_ref[...], ub_ref[...]], axis=1)
        dd_ref[...] += jnp.sum(dy0 * uv, axis=0, keepdims=True)

    row = pl.BlockSpec((tr, w), lambda i: (i, 0))
    vec = pl.BlockSpec((1, w), lambda i: (0, 0))
    mat = pl.BlockSpec((w, w), lambda i: (0, 0))
    return pl.pallas_call(
        body, out_shape=(jax.ShapeDtypeStruct((t, w), F32), jax.ShapeDtypeStruct((w, w), F32),
                         jax.ShapeDtypeStruct((1, w), F32), jax.ShapeDtypeStruct((1, w), F32)),
        grid=(t // tr,), in_specs=[row, row, row, *_last_cols_specs(u, w, tr), vec, mat],
        out_specs=(row, mat, vec, vec),
        compiler_params=_cparams(("arbitrary",)), name=name)(ds, y0, z, u, u, d, wglu)


def _du_combine(dy0, d, du_f, du_b, name):
    t, w = dy0.shape
    tr = _row_tile(t)

    def body(dy_ref, d_ref, a_ref, b_ref, o_ref):
        o_ref[...] = d_ref[...] * dy_ref[...] + a_ref[...] + b_ref[...]

    row = pl.BlockSpec((tr, w), lambda i: (i, 0))
    vec = pl.BlockSpec((1, w), lambda i: (0, 0))
    return pl.pallas_call(body, out_shape=jax.ShapeDtypeStruct((t, w), F32), grid=(t // tr,),
                          in_specs=[row, vec, row, row], out_specs=row, compiler_params=_cparams(("parallel",)),
                          name=name)(dy0, d, du_f, du_b)


def _ffn_fwd(x, g, wg, wu, wd, tag):
    xo, h, gate, up = _ffn_fwd_call(x, g, wg, wu, wd, f"{tag}_fwd")
    return xo, (h, gate, up)


def _ffn_bwd(dxo, dxo_b, x, g, wg, wu, wd, saved, tag):
    h, gate, up = saved
    dx, dx_b, dg, dgate, dup, act = _ffn_bwd_x_call(dxo, dxo_b, x, g, gate, up, wg, wu, wd, f"{tag}_bwd_x")
    dwg, dwu, dwd = _ffn_bwd_w_call(h, dxo_b, dgate, dup, act, f"{tag}_bwd_w")
    return dx, dx_b, dg, dwg, dwu, dwd


def _local_step(x, tgt, w, get_weights, put_grads, reduce_wide):
    t = x.shape[0]
    row = lambda a: a.reshape(1, -1)
    grads = {}

    w = dict(w)

    ssm_names = ['ssm_lambda_re', 'ssm_lambda_im', 'ssm_log_dt', 'ssm_b_re', 'ssm_b_im', 'ssm_c_re', 'ssm_c_im']
    tr3 = lambda m: jnp.swapaxes(m, 1, 2)
    fwd_ops, adj_ops, vjps = [], [], []
    for direction in range(2):
        rev = direction == 1
        prep, vjp = jax.vjp(_ssm_prep, *[w[n][direction] for n in ssm_names])
        a_re, a_im = prep[0], prep[1]
        mi_re, mi_im, mo_re, mo_im = (m.astype(BF16) for m in prep[2:])
        fwd_ops.append((mi_re, mi_im, _scan_tables(a_re, a_im, rev), mo_re, mo_im))
        adj_ops.append((tr3(mo_re), tr3(mo_im), _scan_tables(a_re, -a_im, not rev), tr3(mi_re), tr3(mi_im)))
        vjps.append(vjp)
    sink_rows = jnp.repeat(w['attn_sinks'].reshape(KV_HEADS, GQ), QBLOCK, axis=1)[..., None]
    bias = _attn_bias()
    prepared = sum(jnp.sum(op[:1, :1].astype(F32)) for ops in fwd_ops + adj_ops for op in ops) + sink_rows[0, 0, 0]

    w.update(get_weights('ffn1', prepared.reshape(1, 1)))
    x1, ffn1_saved = _ffn_fwd(x, w['norm_ffn1'], w['ffn1_w_gate'], w['ffn1_w_up'], w['ffn1_w_down'], "ffn1")
    w.update(get_weights('mix', x1))

    h2 = _rms_fwd(x1, w['norm_mix'], "mix_norm")
    proj = _mm(h2, w['w_in'], tb=True, name="in_proj")[0]
    u = proj

    attn, lse = _attn_fwd_proj(proj, sink_rows, bias, "attn_fwd")

    ys, states = [], []
    for direction in range(2):
        y, xr, xi = _scan(u, *fwd_ops[direction], direction == 1, f"s5_fwd{direction}")
        ys.append(y)
        states.append((xr, xi))
    d_row = row(w['ssm_d'])
    s, y0, z = _ssm_post_fwd(u, ys[0], ys[1], d_row, w['ssm_glu_w'], row(w['ssm_glu_b']), "ssm_post")

    ma = _rms_fwd(attn, row(w['attn_out_norm']), "attn_out_norm")
    ms = _rms_fwd(s, row(w['ssm_out_norm']), "ssm_out_norm")
    mixed = jnp.concatenate([ma, ms], axis=-1)
    x2 = _mm(mixed, w['w_out'], res=x1, reduce_s=True, name="out_proj")

    w.update(get_weights('ffn2', x2))
    x3, ffn2_saved = _ffn_fwd(x2, w['norm_ffn2'], w['ffn2_w_gate'], w['ffn2_w_up'], w['ffn2_w_down'], "ffn2")

    loss, dx3, dx3_b, dgf = _loss_head(x3, row(w['final_norm']), tgt, "loss_head")
    grads['final_norm'] = dgf.reshape(w['final_norm'].shape)

    dx2, dx2_b, dg, dwg, dwu, dwd = _ffn_bwd(dx3, dx3_b, x2, w['norm_ffn2'], w['ffn2_w_gate'], w['ffn2_w_up'],
                                             w['ffn2_w_down'], ffn2_saved, "ffn2")
    grads['norm_ffn2'] = dg
    sent = put_grads('ffn2', dict(ffn2_w_gate=dwg, ffn2_w_up=dwu, ffn2_w_down=dwd))

    dmixed = _mm(dx2_b, w['w_out'], tb=True, reduce_s=True, after=sent, name="out_proj_dx")
    dw_out = _mm(mixed, dx2_b, ta=True, out_dtype=BF16, name="out_proj_dw")[0]
    dattn, _, dga = _rms_bwd(attn, row(w['attn_out_norm']), dmixed[:, :ATTN_WIDTH], None, "attn_out_dnorm")
    ds, _, dgs = _rms_bwd(s, row(w['ssm_out_norm']), dmixed[:, ATTN_WIDTH:], None, "ssm_out_dnorm")
    grads.update(attn_out_norm=dga, ssm_out_norm=dgs)

    dy0, dwglu, dbglu, dd = _ssm_post_bwd(ds, y0, z, u, d_row, w['ssm_glu_w'], "ssm_post_bwd")
    grads['ssm_glu_b'] = dbglu
    grads['ssm_d'] = dd.reshape(w['ssm_d'].shape)
    dparams, du_dirs = [], []
    for direction in range(2):
        rev = direction == 1
        xr, xi = states[direction]
        du_dir, dmir, dmii, dmor, dmoi, da = _scan_adjoint(dy0, u, xr, xi, *adj_ops[direction], not rev,
                                                            f"s5_adj{direction}")
        du_dirs.append(du_dir)
        da_re = da[:, 0, :].reshape(-1)
        da_im = da[:, 1, :].reshape(-1)
        dparams.append(vjps[direction]((da_re, da_im, dmir, dmii, dmor, dmoi)))
    du = _du_combine(dy0, d_row, du_dirs[0], du_dirs[1], "ssm_du")
    for i, n in enumerate(ssm_names):
        grads[n] = jnp.stack([dparams[0][i], dparams[1][i]])
    wide_sum = reduce_wide(grads)

    dq, dk, dv, dsink = _attn_bwd_proj(proj, sink_rows, bias, attn, lse, dattn, "attn_bwd")
    grads['attn_sinks'] = dsink.reshape(w['attn_sinks'].shape)
    dproj = jnp.concatenate([dq, dk, dv, du], axis=-1).astype(BF16)

    dw_in = _mm(dproj, h2, ta=True, out_dtype=BF16, after=wide_sum, name="in_proj_dw")[0]
    sent = put_grads('mix', dict(w_in=dw_in, ssm_glu_w=dwglu, w_out=dw_out))
    dh2 = _mm(dproj, w['w_in'], reduce_s=True, after=sent, name="in_proj_dx")
    dx1, dx1_b, dgm = _rms_bwd(x1, w['norm_mix'], dh2, dx2, "mix_dnorm")
    grads['norm_mix'] = dgm

    dx0, _, dg, dwg, dwu, dwd = _ffn_bwd(dx1, dx1_b, x, w['norm_ffn1'], w['ffn1_w_gate'], w['ffn1_w_up'],
                                         w['ffn1_w_down'], ffn1_saved, "ffn1")
    grads['norm_ffn1'] = dg
    put_grads('ffn1', dict(ffn1_w_gate=dwg, ffn1_w_up=dwu, ffn1_w_down=dwd))
    return loss, dx0, grads, wide_sum


HBM_SPEC = pl.BlockSpec(memory_space=pl.ANY)


def _chip_peers(x, y):
    return [(1 - x, y), (x, 1 - y), (1 - x, 1 - y)]


HBM_ONLY = pl.BlockSpec(memory_space=pltpu.HBM)
SEM_SPEC = pl.BlockSpec(memory_space=pltpu.SEMAPHORE)
EFFECT = pltpu.SideEffectType.DATAFLOW_SIDE_EFFECTING


def _place_own(src, slot, name):
    r, c = src.shape
    tr = r // 2

    def body(slot_ref, s_ref, o_ref):
        o_ref[0] = s_ref[...]

    return pl.pallas_call(
        body, out_shape=jax.ShapeDtypeStruct((N_CHIPS, r, c), src.dtype),
        grid_spec=pltpu.PrefetchScalarGridSpec(
            num_scalar_prefetch=1, grid=(2,), in_specs=[pl.BlockSpec((tr, c), lambda i, s: (i, 0))],
            out_specs=pl.BlockSpec((1, tr, c), lambda i, s: (s[0], i, 0))),
        compiler_params=_cparams(("parallel",)), name=name)(slot, src)


def _chip_copies(srcs, lands, send_sems, recv_sems, scatter, landed):
    x, y, c = lax.axis_index("x"), lax.axis_index("y"), lax.axis_index("c")
    me = 2 * x + y
    out = []
    for i in range(len(srcs)):
        for j, (px, py) in enumerate(_chip_peers(x, y)):
            p = 2 * px + py
            slot = p if landed else me
            if scatter:
                src, dst = srcs[i].at[p], lands[i].at[slot]
            else:
                rows = _core_half(srcs[i].shape[0], c)
                src, dst = srcs[i].at[rows], lands[i].at[slot, rows]
            out.append(pltpu.make_async_remote_copy(src, dst, send_sems.at[3 * i + j], recv_sems.at[3 * i + j],
                                                    device_id=(px, py, c), device_id_type=MESH))
    return out


def _core_half(nrows, c):
    half = nrows // 2
    return pl.ds(pl.multiple_of(c * half, 16), half)


def _sibling_forward(lands, name):
    n = len(lands)

    def body(*refs):
        bufs = refs[n:2 * n]
        send_sems, recv_sems = refs[2 * n:]
        x, y, c = lax.axis_index("x"), lax.axis_index("y"), lax.axis_index("c")
        mine = [_core_half(b.shape[1], c) for b in bufs]
        theirs = [_core_half(b.shape[1], 1 - c) for b in bufs]
        chips = [2 * px + py for px, py in _chip_peers(x, y)]
        cps = [pltpu.make_async_remote_copy(bufs[i].at[p, mine[i]], bufs[i].at[p, mine[i]], send_sems.at[3 * i + j],
                                            recv_sems.at[3 * i + j], device_id=(x, y, 1 - c), device_id_type=MESH)
               for i in range(n) for j, p in enumerate(chips)]
        for cp in cps:
            cp.start()
        for i in range(n):
            for j, p in enumerate(chips):
                pltpu.make_async_remote_copy(bufs[i].at[p, mine[i]], bufs[i].at[p, theirs[i]], send_sems.at[3 * i + j],
                                             recv_sems.at[3 * i + j], device_id=(x, y, 1 - c),
                                             device_id_type=MESH).wait()

    return pl.pallas_call(
        body, out_shape=[jax.ShapeDtypeStruct(a.shape, a.dtype) for a in lands],
        in_specs=[HBM_SPEC] * n, out_specs=[HBM_SPEC] * n, input_output_aliases={k: k for k in range(n)},
        scratch_shapes=[pltpu.SemaphoreType.DMA((3 * n,)), pltpu.SemaphoreType.DMA((3 * n,))],
        name=name)(*lands)


def _exchange_start(groups, scatter, name, after=None):
    sizes = [len(srcs) for srcs, _ in groups]
    flat_src = [a for srcs, _ in groups for a in srcs]
    flat_land = [a for _, lands in groups for a in lands]
    n = len(flat_src)
    ng = len(groups)

    def body(*refs):
        src_refs, land_refs = refs[:n], refs[n:2 * n]
        n_in = 2 * n + (after is not None)
        sems = refs[n_in:n_in + 2 * ng]
        token_ref = refs[-1]
        off = 0
        for gi, sz in enumerate(sizes):
            for cp in _chip_copies(src_refs[off:off + sz], land_refs[off:off + sz], sems[2 * gi], sems[2 * gi + 1],
                                   scatter, landed=False):
                cp.start()
            off += sz
        token_ref[...] = jnp.zeros_like(token_ref)

    sem_shapes = []
    for sz in sizes:
        sem_shapes += [pltpu.SemaphoreType.DMA((3 * sz,)), pltpu.SemaphoreType.DMA((3 * sz,))]
    hbm = lambda a: pltpu.HBM(a.shape, a.dtype)
    res = pl.pallas_call(
        body, name=name,
        out_shape=(tuple(sem_shapes) + tuple(hbm(a) for a in flat_src) + tuple(hbm(a) for a in flat_land)
                   + (jax.ShapeDtypeStruct((SUBLANES, LANES), F32),)),
        in_specs=[HBM_ONLY] * (2 * n) + [HBM_SPEC] * (after is not None),
        out_specs=tuple([SEM_SPEC] * (2 * ng) + [HBM_ONLY] * (2 * n) + [pl.BlockSpec(memory_space=pltpu.VMEM)]),
        input_output_aliases={k: 2 * ng + k for k in range(2 * n)},
        compiler_params=pltpu.CompilerParams(has_side_effects=EFFECT),
    )(*[pltpu.with_memory_space_constraint(a, pltpu.HBM) for a in flat_src + flat_land],
      *([after] if after is not None else []))
    sems, thru_src, thru_land = res[:2 * ng], res[2 * ng:2 * ng + n], res[2 * ng + n:2 * ng + 2 * n]
    out, off = [], 0
    for gi, sz in enumerate(sizes):
        out.append((sems[2 * gi], sems[2 * gi + 1], list(thru_src[off:off + sz]), list(thru_land[off:off + sz])))
        off += sz
    return out, res[-1]


def _exchange_wait(started, after, scatter, name):
    send_sems, recv_sems, srcs, lands = started
    n = len(srcs)

    def body(*refs):
        src_refs, land_refs = refs[:n], refs[n:2 * n]
        send_ref, recv_ref = refs[2 * n], refs[2 * n + 1]
        for cp in _chip_copies(src_refs, land_refs, send_ref, recv_ref, scatter, landed=True):
            cp.wait_send()
            cp.wait_recv()

    hbm = lambda a: pltpu.HBM(a.shape, a.dtype)
    res = pl.pallas_call(
        body, name=name, out_shape=tuple(hbm(a) for a in srcs) + tuple(hbm(a) for a in lands),
        in_specs=[HBM_ONLY] * (2 * n) + [SEM_SPEC, SEM_SPEC, HBM_SPEC], out_specs=tuple([HBM_ONLY] * (2 * n)),
        input_output_aliases={k: k for k in range(2 * n)},
        compiler_params=pltpu.CompilerParams(has_side_effects=EFFECT),
    )(*srcs, *lands, send_sems, recv_sems, after)
    return list(res[:n]), list(res[n:])


def _half_swap(parts, name):
    n = len(parts)

    def body(*refs):
        ins, outs = refs[:n], refs[n:2 * n]
        send_sems, recv_sems = refs[2 * n:]
        x, y, c = lax.axis_index("x"), lax.axis_index("y"), lax.axis_index("c")
        cps = [pltpu.make_async_remote_copy(ins[i].at[k, _core_half(ins[i].shape[1], 1 - c)], outs[i].at[k],
                                            send_sems.at[N_CHIPS * i + k], recv_sems.at[N_CHIPS * i + k],
                                            device_id=(x, y, 1 - c), device_id_type=MESH)
               for i in range(n) for k in range(N_CHIPS)]
        for cp in cps:
            cp.start()
        for cp in cps:
            cp.wait()

    return pl.pallas_call(
        body, out_shape=[jax.ShapeDtypeStruct((N_CHIPS, p.shape[1] // 2, p.shape[2]), p.dtype) for p in parts],
        in_specs=[HBM_SPEC] * n, out_specs=[HBM_SPEC] * n,
        scratch_shapes=[pltpu.SemaphoreType.DMA((N_CHIPS * n,)), pltpu.SemaphoreType.DMA((N_CHIPS * n,))],
        name=name)(*parts)


def _half_add(parts, sib, slots, name):
    na = len(parts)
    _, r, c = parts[0].shape
    hr = r // 2
    tr = _row_tile(hr, 512)
    nt = hr // tr

    def body(slot_ref, *refs):
        for a in range(na):
            refs[2 * na + a][...] = (refs[2 * a][...].astype(F32) + refs[2 * a + 1][...].astype(F32)).astype(BF16)

    mine = pl.BlockSpec((1, tr, c), lambda k, i, s: (k, i + s[4] * nt, 0))
    half = pl.BlockSpec((1, tr, c), lambda k, i, s: (k, i, 0))
    args = [a for p, sb in zip(parts, sib) for a in (p, sb)]
    return pl.pallas_call(
        body, out_shape=[jax.ShapeDtypeStruct((N_CHIPS, hr, c), BF16)] * na,
        grid_spec=pltpu.PrefetchScalarGridSpec(
            num_scalar_prefetch=1, grid=(N_CHIPS, nt), in_specs=[mine, half] * na, out_specs=[half] * na),
        compiler_params=_cparams(("parallel", "parallel")), name=name)(slots, *args)


def _half_forward(arrs, name):
    n = len(arrs)

    def body(*refs):
        bufs = refs[n:2 * n]
        send_sems, recv_sems = refs[2 * n:]
        x, y, c = lax.axis_index("x"), lax.axis_index("y"), lax.axis_index("c")
        cps = [pltpu.make_async_remote_copy(b.at[_core_half(b.shape[0], c)], b.at[_core_half(b.shape[0], c)],
                                            send_sems.at[i], recv_sems.at[i], device_id=(x, y, 1 - c),
                                            device_id_type=MESH) for i, b in enumerate(bufs)]
        for cp in cps:
            cp.start()
        for i, b in enumerate(bufs):
            pltpu.make_async_remote_copy(b.at[_core_half(b.shape[0], c)], b.at[_core_half(b.shape[0], 1 - c)],
                                         send_sems.at[i], recv_sems.at[i], device_id=(x, y, 1 - c),
                                         device_id_type=MESH).wait()

    return pl.pallas_call(
        body, out_shape=[jax.ShapeDtypeStruct(a.shape, a.dtype) for a in arrs],
        in_specs=[HBM_SPEC] * n, out_specs=[HBM_SPEC] * n, input_output_aliases={k: k for k in range(n)},
        scratch_shapes=[pltpu.SemaphoreType.DMA((n,)), pltpu.SemaphoreType.DMA((n,))],
        name=name)(*arrs)


def _small_exchange(smalls, name):
    nsm = len(smalls)
    rels = [(fx, fy, fc) for fx in (0, 1) for fy in (0, 1) for fc in (0, 1)][1:]

    def body(*refs):
        sins, souts = refs[:nsm], refs[nsm:2 * nsm]
        ssend, srecv, slocal = refs[2 * nsm:]
        x, y, c = lax.axis_index("x"), lax.axis_index("y"), lax.axis_index("c")
        lin = 4 * x + 2 * y + c
        local = [pltpu.make_async_copy(sins[i], souts[i].at[lin], slocal.at[i]) for i in range(nsm)]
        for cp in local:
            cp.start()
        for i in range(nsm):
            for j, (fx, fy, fc) in enumerate(rels):
                pltpu.make_async_remote_copy(sins[i], souts[i].at[lin], ssend.at[i, j], srecv.at[i, j],
                                             device_id=(x ^ fx, y ^ fy, c ^ fc), device_id_type=MESH).start()
        for i in range(nsm):
            for j, (fx, fy, fc) in enumerate(rels):
                src = 4 * (x ^ fx) + 2 * (y ^ fy) + (c ^ fc)
                pltpu.make_async_remote_copy(sins[i], souts[i].at[src], ssend.at[i, j], srecv.at[i, j],
                                             device_id=(x ^ fx, y ^ fy, c ^ fc), device_id_type=MESH).wait()
        for cp in local:
            cp.wait()

    return pl.pallas_call(
        body, out_shape=[jax.ShapeDtypeStruct((N_DEV,) + s.shape, s.dtype) for s in smalls],
        in_specs=[HBM_SPEC] * nsm, out_specs=[HBM_SPEC] * nsm,
        scratch_shapes=[pltpu.SemaphoreType.DMA((nsm, 7)), pltpu.SemaphoreType.DMA((nsm, 7)),
                        pltpu.SemaphoreType.DMA((nsm,))],
        name=name)(*smalls)


def _sum_parts(parts, recv, slots, name):
    na = len(parts)
    _, r, c = parts[0].shape
    tr = _row_tile(r, 192)

    def body(slot_ref, *refs):
        for a in range(na):
            own_ref, r0_ref, r1_ref, r2_ref = refs[4 * a:4 * a + 4]
            refs[4 * na + a][...] = ((own_ref[0].astype(F32) + r0_ref[0].astype(F32))
                                     + (r1_ref[0].astype(F32) + r2_ref[0].astype(F32)))

    blk = lambda k: pl.BlockSpec((1, tr, c), lambda i, s, k=k: (s[k], i, 0))
    out_blk = pl.BlockSpec((tr, c), lambda i, s: (i + s[4] * (r // tr), 0))
    args = [a for p, rv in zip(parts, recv) for a in (p, rv, rv, rv)]
    return pl.pallas_call(
        body, out_shape=[jax.ShapeDtypeStruct((2 * r, c), F32)] * na,
        grid_spec=pltpu.PrefetchScalarGridSpec(
            num_scalar_prefetch=1, grid=(r // tr,), in_specs=[blk(0), blk(1), blk(2), blk(3)] * na,
            out_specs=[out_blk] * na),
        compiler_params=_cparams(("parallel",)), name=name)(slots, *args)


def _small_allreduce(packed, name):
    rows = packed.shape[0]
    pr = rows // N_DEV
    rels = [(fx, fy, fc) for fx in (0, 1) for fy in (0, 1) for fc in (0, 1)][1:]

    def body(in_ref, out_ref, recv_ref, send1, recv1, send2, recv2):
        x, y, c = lax.axis_index("x"), lax.axis_index("y"), lax.axis_index("c")
        lin = 4 * x + 2 * y + c
        piece = lambda ref, k: ref.at[pl.ds(pl.multiple_of(k * pr, pr), pr), :]
        peers = [((x ^ fx, y ^ fy, c ^ fc), 4 * (x ^ fx) + 2 * (y ^ fy) + (c ^ fc)) for fx, fy, fc in rels]
        for j, (dev, plin) in enumerate(peers):
            pltpu.make_async_remote_copy(piece(in_ref, plin), recv_ref.at[lin], send1.at[j], recv1.at[j],
                                         device_id=dev, device_id_type=MESH).start()
        recv_ref[lin] = piece(in_ref, lin)[...]
        for j, (dev, plin) in enumerate(peers):
            pltpu.make_async_remote_copy(piece(in_ref, plin), recv_ref.at[plin], send1.at[j], recv1.at[j],
                                         device_id=dev, device_id_type=MESH).wait()
        acc = recv_ref[0]
        for k in range(1, N_DEV):
            acc = acc + recv_ref[k]
        piece(out_ref, lin)[...] = acc
        for j, (dev, plin) in enumerate(peers):
            pltpu.make_async_remote_copy(piece(out_ref, lin), piece(out_ref, lin), send2.at[j], recv2.at[j],
                                         device_id=dev, device_id_type=MESH).start()
        for j, (dev, plin) in enumerate(peers):
            pltpu.make_async_remote_copy(piece(out_ref, lin), piece(out_ref, plin), send2.at[j], recv2.at[j],
                                         device_id=dev, device_id_type=MESH).wait()

    vm = pl.BlockSpec(memory_space=pltpu.VMEM)
    return pl.pallas_call(
        body, out_shape=jax.ShapeDtypeStruct(packed.shape, F32), in_specs=[vm], out_specs=vm,
        scratch_shapes=[pltpu.VMEM((N_DEV, pr, LANES), F32)] + [pltpu.SemaphoreType.DMA((7,))] * 4,
        compiler_params=pltpu.CompilerParams(vmem_limit_bytes=VMEM_LIMIT), name=name)(packed)


def _adamw_math(w, m, v, g):
    nm = ADAM_B1 * m + (1.0 - ADAM_B1) * g
    nv = ADAM_B2 * v + (1.0 - ADAM_B2) * (g * g)
    m_hat = nm * (1.0 / (1.0 - ADAM_B1 ** ADAM_STEP))
    v_hat = nv * (1.0 / (1.0 - ADAM_B2 ** ADAM_STEP))
    return -ADAM_LR * (m_hat / (jnp.sqrt(v_hat) + ADAM_EPS) + ADAM_WD * w), nm, nv


def _adamw(ws, ms, vs, gs, name):
    na = len(ws)
    r, c = ws[0].shape
    tr = _row_tile(r)

    def body(*refs):
        for a in range(na):
            w_ref, m_ref, v_ref, g_ref = refs[4 * a:4 * a + 4]
            d_ref, nm_ref, nv_ref = refs[4 * na + 3 * a:4 * na + 3 * a + 3]
            d_ref[...], nm_ref[...], nv_ref[...] = _adamw_math(w_ref[...], m_ref[...], v_ref[...], g_ref[...])

    blk = pl.BlockSpec((tr, c), lambda i: (i, 0))
    sh = jax.ShapeDtypeStruct((r, c), F32)
    args = [a for group in zip(ws, ms, vs, gs) for a in group]
    res = pl.pallas_call(body, out_shape=[sh] * (3 * na), grid=(r // tr,), in_specs=[blk] * (4 * na),
                         out_specs=[blk] * (3 * na), compiler_params=_cparams(("parallel",)), name=name)(*args)
    return [tuple(res[3 * a:3 * a + 3]) for a in range(na)]


def _adamw_small(ws, ms, vs, alls, split, name):
    n = len(ws)
    lead = split if split is not None else ()
    nl = len(lead)
    nslots = alls[0].shape[0]

    def blocks(shape):
        if split is None:
            return tuple(shape), (lambda *g: (0,) * len(shape))
        blk = (shape[0], shape[1] // lead[0], shape[2] // lead[1]) + tuple(shape[3:])
        return blk, (lambda *g: (0, g[0], g[1]) + (0,) * (len(shape) - 3))

    def body(*refs):
        w_refs, m_refs, v_refs, a_refs = (refs[k * n:(k + 1) * n] for k in range(4))
        g_refs, d_refs, nm_refs, nv_refs = (refs[(4 + k) * n:(5 + k) * n] for k in range(4))
        k = pl.program_id(nl)
        for i in range(n):
            @pl.when(k == 0)
            def _(i=i):
                g_refs[i][...] = a_refs[i][0]

            @pl.when(k > 0)
            def _(i=i):
                g_refs[i][...] += a_refs[i][0]

            @pl.when(k == nslots - 1)
            def _(i=i):
                d_refs[i][...], nm_refs[i][...], nv_refs[i][...] = _adamw_math(
                    w_refs[i][...], m_refs[i][...], v_refs[i][...], g_refs[i][...])

    specs, aspecs, shapes = [], [], []
    for wa in ws:
        blk, imap = blocks(wa.shape)
        specs.append(pl.BlockSpec(blk, imap))
        aspecs.append(pl.BlockSpec((1,) + blk, (lambda *g, imap=imap: (g[nl],) + imap(*g))))
        shapes.append(jax.ShapeDtypeStruct(wa.shape, F32))
    res = pl.pallas_call(
        body, out_shape=shapes * 4, grid=tuple(lead) + (nslots,), in_specs=specs * 3 + aspecs,
        out_specs=specs * 4, compiler_params=_cparams(("parallel",) * nl + ("arbitrary",)),
        name=name)(*ws, *ms, *vs, *alls)
    return res[:n], res[n:2 * n], res[2 * n:3 * n], res[3 * n:]


def kernel(x, norm_ffn1, ffn1_w_gate, ffn1_w_up, ffn1_w_down, norm_mix, w_in, attn_sinks, ssm_lambda_re, ssm_lambda_im, ssm_log_dt, ssm_b_re, ssm_b_im, ssm_c_re, ssm_c_im, ssm_d, ssm_glu_w, ssm_glu_b, attn_out_norm, ssm_out_norm, w_out, norm_ffn2, ffn2_w_gate, ffn2_w_up, ffn2_w_down, final_norm, loss_target, m_norm_ffn1, m_ffn1_w_gate, m_ffn1_w_up, m_ffn1_w_down, m_norm_mix, m_w_in, m_attn_sinks, m_ssm_lambda_re, m_ssm_lambda_im, m_ssm_log_dt, m_ssm_b_re, m_ssm_b_im, m_ssm_c_re, m_ssm_c_im, m_ssm_d, m_ssm_glu_w, m_ssm_glu_b, m_attn_out_norm, m_ssm_out_norm, m_w_out, m_norm_ffn2, m_ffn2_w_gate, m_ffn2_w_up, m_ffn2_w_down, m_final_norm, v_norm_ffn1, v_ffn1_w_gate, v_ffn1_w_up, v_ffn1_w_down, v_norm_mix, v_w_in, v_attn_sinks, v_ssm_lambda_re, v_ssm_lambda_im, v_ssm_log_dt, v_ssm_b_re, v_ssm_b_im, v_ssm_c_re, v_ssm_c_im, v_ssm_d, v_ssm_glu_w, v_ssm_glu_b, v_attn_out_norm, v_ssm_out_norm, v_w_out, v_norm_ffn2, v_ffn2_w_gate, v_ffn2_w_up, v_ffn2_w_down, v_final_norm):
    given = dict(locals())
    wts = {n: given[n] for n in WEIGHTS}

    order = [g for g in GROUPS]
    cx, cy = lax.axis_index("x"), lax.axis_index("y")
    slots = jnp.stack([2 * cx + cy, 2 * (1 - cx) + cy, 2 * cx + 1 - cy, 2 * (1 - cx) + 1 - cy,
                       lax.axis_index("c")]).astype(jnp.int32)
    def view(a, n):
        if n in TRANSPOSED:
            return jnp.swapaxes(a[0], 0, 1)
        if n in BIG:
            return a[0]
        if n in ('ssm_b_re', 'ssm_b_im'):
            return jnp.swapaxes(a, -1, -2)
        return a.reshape(1, -1) if a.ndim == 1 else a

    def unview(a, n):
        if n in TRANSPOSED:
            return jnp.swapaxes(a, 0, 1)[None]
        if n in ('ssm_b_re', 'ssm_b_im'):
            return jnp.swapaxes(a, -1, -2)
        return a.reshape(wts[n].shape)

    started, gather_token = {}, None
    for g in order:
        shards = [view(wts[n], n).astype(BF16) for n in GROUPS[g]]
        placed = [_place_own(s, slots, f"weights_place_{n}") for n, s in zip(GROUPS[g], shards)]
        st, gather_token = _exchange_start([(shards, placed)], False, f"weights_start_{g}", after=gather_token)
        started[g] = st[0]

    def get_weights(group, after):
        if group == order[0]:
            after = after + gather_token[:1, :1]
        _, lands = _exchange_wait(started[group], after, False, f"weights_wait_{group}")
        lands = _sibling_forward(lands, f"weights_forward_{group}")
        out = dict(zip(GROUPS[group], lands))
        for n in ('w_in', 'ssm_glu_w', 'w_out'):
            if n in out:
                out[n] = out[n].reshape(-1, out[n].shape[-1])
        return out

    sent, tokens = {}, {}

    def put_grads(group, gd):
        parts = []
        for n in GROUPS[group]:
            g = gd[n]
            if g.ndim == 2:
                g = g.reshape(N_CHIPS, g.shape[0] // N_CHIPS, g.shape[1])
            parts.append(g.astype(BF16))
        sib = _half_swap(parts, f"grads_half_swap_{group}")
        same = len({p.shape for p in parts}) == 1
        batches = [list(range(len(parts)))] if same else [[i] for i in range(len(parts))]
        halves = [None] * len(parts)
        for b in batches:
            res = _half_add([parts[i] for i in b], [sib[i] for i in b], slots, f"grads_half_add_{GROUPS[group][b[0]]}")
            for i, h in zip(b, res):
                halves[i] = h
        parts = halves
        lands = [lax.empty(p.shape, p.dtype) for p in parts]
        started_g, tokens[group] = _exchange_start([(parts, lands)], True, f"grads_start_{group}")
        sent[group] = started_g[0]
        return tokens[group]

    w = {n: (wts[n][0] if wts[n].ndim > 1 else wts[n]) for n in SMALL}
    w['norm_ffn1'], w['norm_mix'], w['norm_ffn2'] = wts['norm_ffn1'], wts['norm_mix'], wts['norm_ffn2']
    w['ssm_b_re'], w['ssm_b_im'] = view(wts['ssm_b_re'], 'ssm_b_re')[0], view(wts['ssm_b_im'], 'ssm_b_im')[0]
    w['ssm_log_dt'] = w['ssm_log_dt'] + gather_token[0, 0]
    wide =['ssm_b_re', 'ssm_b_im', 'ssm_c_re', 'ssm_c_im']

    def reduce_wide(gd):
        packed = jnp.concatenate([gd[n].reshape(-1, LANES) for n in wide])
        return _small_allreduce(packed, "small_grads_allreduce")

    loss_row, dx, grads, wide_sum = _local_step(x[0], loss_target[0], w, get_weights, put_grads, reduce_wide)

    out_g, out_d, out_m, out_v = {}, {}, {}, {}

    def finish(group, after):
        names = GROUPS[group]
        parts, recv = _exchange_wait(sent[group], after, True, f"grads_wait_{group}")
        same = len({p.shape for p in parts}) == 1
        batches = [list(range(len(names)))] if same else [[i] for i in range(len(names))]
        sums = [None] * len(names)
        for b in batches:
            res = _sum_parts([parts[i] for i in b], [recv[i] for i in b], slots, f"grad_sum_{names[b[0]]}")
            for i, sm in zip(b, res):
                sums[i] = sm
        full = _half_forward(sums, f"grad_half_forward_{group}")
        for b in batches:
            res = _adamw([view(wts[names[i]], names[i]) for i in b], [view(given['m_' + names[i]], names[i]) for i in b],
                         [view(given['v_' + names[i]], names[i]) for i in b], [full[i] for i in b],
                         f"adamw_{names[b[0]]}")
            for i, (d, nm, nv) in zip(b, res):
                n = names[i]
                out_g[n], out_d[n], out_m[n], out_v[n] = (unview(a, n) for a in (full[i], d, nm, nv))
        return nv

    done = finish('ffn2', tokens['ffn1'])
    done = finish('mix', done)

    nat = {n: view(wts[n], n).shape for n in SMALL}
    narrow = [n for n in SMALL if n not in wide]
    alls = list(_small_exchange([grads[n].reshape(nat[n]) for n in narrow] + [loss_row], "small_grads_allgather"))
    loss = jnp.sum(alls.pop()[:, 0, 0])
    rows = wide_sum.shape[0] // len(wide)
    wide_g = [wide_sum[i * rows:(i + 1) * rows].reshape((1,) + nat[n]) for i, n in enumerate(wide)]
    for group, gs, split, tag in ((narrow, alls, None, "adamw_small"), (wide, wide_g, (2, 4), "adamw_ssm_bc")):
        res = _adamw_small([view(wts[n], n) for n in group], [view(given['m_' + n], n) for n in group],
                           [view(given['v_' + n], n) for n in group], gs, split, tag)
        for dst, vals in zip((out_g, out_d, out_m, out_v), res):
            for n, a in zip(group, vals):
                dst[n] = unview(a, n)

    finish('ffn1', out_v['norm_ffn1'][:, :1] + out_v['ssm_c_re'].reshape(1, -1)[:, :1] + done[:1, :1] + loss)

    return (loss, dx[None], *[out_g[n] for n in WEIGHTS], *[out_d[n] for n in WEIGHTS],
            *[out_m[n] for n in WEIGHTS], *[out_v[n] for n in WEIGHTS])
```

```python
import functools
import math

import numpy as np
import jax
import jax.numpy as jnp
from jax import lax
from jax.experimental import pallas as pl
from jax.experimental.pallas import tpu as pltpu

F32 = jnp.float32
BF16 = jnp.bfloat16
MESH = pl.DeviceIdType.MESH

EPS = 1e-6
NEG_INF = -1e30
LAMBDA_RE_MAX = -1e-4
ATTN_HEADS = 8
KV_HEADS = 2
GQ = ATTN_HEADS // KV_HEADS
HEAD_DIM = 64
ATTN_WIDTH = 512
KV_WIDTH = 128
WINDOW = 128
QBLOCK = 128
SSM_WIDTH = 512
SSM_GROUPS = 32
SSM_CH = 16
SSM_STATE = 64
N_STRIPS = 4
STRIP_IN = SSM_WIDTH // N_STRIPS
STRIP_ST = SSM_GROUPS * SSM_STATE // N_STRIPS
SUBLANES = 8
LANES = 128
N_CHIPS = 4
N_DEV = 8

ADAM_LR = 0.001
ADAM_B1 = 0.9
ADAM_B2 = 0.999
ADAM_EPS = 1e-08
ADAM_WD = 0.01
ADAM_STEP = 10

VMEM_LIMIT = 48 * 1024 * 1024

WEIGHTS = ['norm_ffn1', 'ffn1_w_gate', 'ffn1_w_up', 'ffn1_w_down', 'norm_mix', 'w_in', 'attn_sinks',
           'ssm_lambda_re', 'ssm_lambda_im', 'ssm_log_dt', 'ssm_b_re', 'ssm_b_im', 'ssm_c_re', 'ssm_c_im',
           'ssm_d', 'ssm_glu_w', 'ssm_glu_b', 'attn_out_norm', 'ssm_out_norm', 'w_out', 'norm_ffn2',
           'ffn2_w_gate', 'ffn2_w_up', 'ffn2_w_down', 'final_norm']
BIG = ['ffn1_w_gate', 'ffn1_w_up', 'ffn1_w_down', 'w_in', 'ssm_glu_w', 'w_out',
       'ffn2_w_gate', 'ffn2_w_up', 'ffn2_w_down']
SMALL = [n for n in WEIGHTS if n not in BIG]
TRANSPOSED = ['ffn1_w_gate', 'ffn1_w_up', 'w_in', 'ffn2_w_gate', 'ffn2_w_up']
GROUPS = {'ffn1': ['ffn1_w_gate', 'ffn1_w_up', 'ffn1_w_down'],
          'mix': ['w_in', 'ssm_glu_w', 'w_out'],
          'ffn2': ['ffn2_w_gate', 'ffn2_w_up', 'ffn2_w_down']}


def _cparams(sem=None):
    return pltpu.CompilerParams(dimension_semantics=sem, vmem_limit_bytes=VMEM_LIMIT)


def _tile(n, pref):
    if n <= pref:
        return n
    for t in (pref, pref // 2, pref // 4):
        if t % LANES == 0 and n % t == 0:
            return t
    return n


def _sigmoid(x):
    return 1.0 / (1.0 + jnp.exp(-x))


def _sigmoid_tanh(x):
    return 0.5 * jnp.tanh(0.5 * x) + 0.5


def _mm(a, b, *, ta=False, tb=False, reduce_s=False, res=None, scale=1.0, out_dtype=F32, after=None, name):
    a3 = a if a.ndim == 3 else a[None]
    b3 = b if b.ndim == 3 else b[None]
    sa, sb = a3.shape[0], b3.shape[0]
    ns = max(sa, sb)
    (kk, m) = a3.shape[1:] if ta else a3.shape[1:][::-1]
    (n, kb) = b3.shape[1:] if tb else b3.shape[1:][::-1]
    assert kk == kb, (a3.shape, b3.shape)
    tm, tn, tk = _tile(m, 1024), _tile(n, 1024), _tile(kk, 2048)
    nm, nn, nk = m // tm, n // tn, kk // tk
    has_res = res is not None
    single = nk == 1 and not (reduce_s and ns > 1)

    if reduce_s:
        grid = (nm, nn, ns, nk)
        ids = lambda i, j, s, k: (s, i, j, k)
        sem = ("parallel", "parallel", "arbitrary", "arbitrary")
    else:
        grid = (ns, nm, nn, nk)
        ids = lambda s, i, j, k: (s, i, j, k)
        sem = ("parallel", "parallel", "parallel", "arbitrary")

    def a_map(*g):
        s, i, j, k = ids(*g)
        s = s if sa > 1 else 0
        return (s, k, i) if ta else (s, i, k)

    def b_map(*g):
        s, i, j, k = ids(*g)
        s = s if sb > 1 else 0
        return (s, j, k) if tb else (s, k, j)

    def o_map(*g):
        s, i, j, k = ids(*g)
        return (i, j) if reduce_s else (s, i, j)

    a_blk = (1, tk, tm) if ta else (1, tm, tk)
    b_blk = (1, tn, tk) if tb else (1, tk, tn)
    dims = (((0 if ta else 1,), (1 if tb else 0,)), ((), ()))

    def body(*refs):
        a_ref, b_ref = refs[0], refs[1]
        r_ref = refs[2] if has_res else None
        o_ref = refs[2 + has_res + (after is not None)]
        acc_ref = None if single else refs[-1]
        s, _, _, k = ids(*[pl.program_id(d) for d in range(4)])
        prod = lax.dot_general(a_ref[0].astype(BF16), b_ref[0].astype(BF16), dims, preferred_element_type=F32)

        def finish(out):
            if scale != 1.0:
                out = out * scale
            if has_res:
                out = r_ref[...].reshape(out.shape) + out
            o_ref[...] = out.astype(out_dtype).reshape(o_ref.shape)

        if single:
            finish(prod)
            return
        if reduce_s:
            first = jnp.logical_and(s == 0, k == 0)
            last = jnp.logical_and(s == ns - 1, k == nk - 1)
        else:
            first, last = k == 0, k == nk - 1

        acc_ref[...] = prod + jnp.where(first, 0.0, acc_ref[...])

        @pl.when(last)
        def _():
            finish(acc_ref[...])

    in_specs = [pl.BlockSpec(a_blk, a_map), pl.BlockSpec(b_blk, b_map)]
    args = [a3, b3]
    if reduce_s:
        out_shape = jax.ShapeDtypeStruct((m, n), out_dtype)
        o_spec = pl.BlockSpec((tm, tn), o_map)
    else:
        out_shape = jax.ShapeDtypeStruct((ns, m, n), out_dtype)
        o_spec = pl.BlockSpec((1, tm, tn), o_map)
    if has_res:
        assert res.shape == out_shape.shape
        in_specs.append(o_spec)
        args.append(res)
    if after is not None:
        in_specs.append(HBM_SPEC)
        args.append(after)
    return pl.pallas_call(body, out_shape=out_shape, grid=grid, in_specs=in_specs, out_specs=o_spec,
                          scratch_shapes=[] if single else [pltpu.VMEM((tm, tn), F32)],
                          compiler_params=_cparams(sem), name=name)(*args)


def _row_tile(t, cap=256):
    for step in (16, SUBLANES):
        for tr in range(min(cap, t) // step * step, 0, -step):
            if t % tr == 0:
                return tr
    return t


def _rms_fwd(x, g, name):
    t, w = x.shape
    tr = _row_tile(t)

    def body(x_ref, g_ref, o_ref):
        xv = x_ref[...]
        r = lax.rsqrt(jnp.mean(xv * xv, axis=-1, keepdims=True) + EPS)
        o_ref[...] = (xv * r * g_ref[...]).astype(BF16)

    return pl.pallas_call(
        body, out_shape=jax.ShapeDtypeStruct((t, w), BF16), grid=(t // tr,),
        in_specs=[pl.BlockSpec((tr, w), lambda i: (i, 0)), pl.BlockSpec((1, w), lambda i: (0, 0))],
        out_specs=pl.BlockSpec((tr, w), lambda i: (i, 0)), compiler_params=_cparams(("parallel",)),
        name=name)(x, g)


def _rms_bwd_rows(xv, gv, dhv):
    r = lax.rsqrt(jnp.mean(xv * xv, axis=-1, keepdims=True) + EPS)
    nrm = xv * r
    dn = dhv * gv
    return r * (dn - nrm * jnp.mean(dn * nrm, axis=-1, keepdims=True)), dhv * nrm


def _rms_bwd(x, g, dh, dres, name):
    t, w = x.shape
    tr = _row_tile(t)
    has_res = dres is not None

    def body(*refs):
        if has_res:
            x_ref, g_ref, dh_ref, dr_ref, dx_ref, dxb_ref, dg_ref = refs
        else:
            x_ref, g_ref, dh_ref, dx_ref, dxb_ref, dg_ref = refs
        dx, dgs = _rms_bwd_rows(x_ref[...], g_ref[...], dh_ref[...])
        if has_res:
            dx = dx + dr_ref[...]
        dx_ref[...] = dx
        dxb_ref[...] = dx.astype(BF16)

        @pl.when(pl.program_id(0) == 0)
        def _():
            dg_ref[...] = jnp.zeros_like(dg_ref)

        dg_ref[...] += jnp.sum(dgs, axis=0, keepdims=True)

    row = pl.BlockSpec((tr, w), lambda i: (i, 0))
    vec = pl.BlockSpec((1, w), lambda i: (0, 0))
    ins = [x, g, dh] + ([dres] if has_res else [])
    return pl.pallas_call(
        body, out_shape=(jax.ShapeDtypeStruct((t, w), F32), jax.ShapeDtypeStruct((t, w), BF16),
                         jax.ShapeDtypeStruct((1, w), F32)),
        grid=(t // tr,), in_specs=[row, vec, row] + ([row] if has_res else []),
        out_specs=(row, row, vec), compiler_params=_cparams(("arbitrary",)), name=name)(*ins)


FFN_ROWS = 512
FFN_W_ROWS = 1024
SCAN_ROWS = 256


NT_DIMS = (((1,), (1,)), ((), ()))
TN_DIMS = (((0,), (0,)), ((), ()))


def _ffn_fwd_call(x, g, wg, wu, wd, name):
    t, d = x.shape
    ns, f, _ = wg.shape
    tm = _tile(t, FFN_ROWS)

    def body(x_ref, g_ref, wg_ref, wu_ref, wd_ref, xo_ref, h_ref, gate_ref, up_ref, h_sc, acc_ref):
        s = pl.program_id(1)

        @pl.when(s == 0)
        def _():
            xv = x_ref[...]
            r = lax.rsqrt(jnp.mean(xv * xv, axis=-1, keepdims=True) + EPS)
            hb = (xv * r * g_ref[...]).astype(BF16)
            h_sc[...] = hb
            h_ref[...] = hb

        for r0 in range(0, tm, tm // 2):
            rows = slice(r0, r0 + tm // 2)
            hb = h_sc[rows, :]
            gate = lax.dot_general(hb, wg_ref[0], NT_DIMS, preferred_element_type=F32)
            up = lax.dot_general(hb, wu_ref[0], NT_DIMS, preferred_element_type=F32)
            gate_ref[0, rows, :] = gate.astype(BF16)
            up_ref[0, rows, :] = up.astype(BF16)
            act = (gate * _sigmoid_tanh(gate) * up).astype(BF16)
            prod = jnp.dot(act, wd_ref[0], preferred_element_type=F32)
            acc_ref[rows, :] = prod + jnp.where(s > 0, acc_ref[rows, :], 0.0)

        @pl.when(s == ns - 1)
        def _():
            xo_ref[...] = x_ref[...] + 0.5 * acc_ref[...]

    row = pl.BlockSpec((tm, d), lambda i, s: (i, 0))
    vec = pl.BlockSpec((1, d), lambda i, s: (0, 0))
    wrow = pl.BlockSpec((1, f, d), lambda i, s: (s, 0, 0))
    hid = pl.BlockSpec((1, tm, f), lambda i, s: (s, i, 0))
    hid_sh = jax.ShapeDtypeStruct((ns, t, f), BF16)
    return pl.pallas_call(
        body, out_shape=(jax.ShapeDtypeStruct((t, d), F32), jax.ShapeDtypeStruct((t, d), BF16), hid_sh, hid_sh),
        grid=(t // tm, ns), in_specs=[row, vec, wrow, wrow, wrow], out_specs=(row, row, hid, hid),
        scratch_shapes=[pltpu.VMEM((tm, d), BF16), pltpu.VMEM((tm, d), F32)],
        compiler_params=_cparams(("parallel", "arbitrary")), name=name)(x, g, wg, wu, wd)


def _ffn_bwd_x_call(dxo, dxo_b, x, g, gate, up, wg, wu, wd, name):
    t, d = x.shape
    ns, f, _ = wg.shape
    tm = _tile(t, FFN_ROWS)

    def body(dxo_ref, dxb_ref, x_ref, g_ref, gate_ref, up_ref, wg_ref, wu_ref, wd_ref,
             dx_ref, dxob_ref, dgn_ref, dgate_ref, dup_ref, act_ref, dh_ref):
        i, s = pl.program_id(0), pl.program_id(1)
        for r0 in range(0, tm, tm // 2):
            rows = slice(r0, r0 + tm // 2)
            dact = lax.dot_general(dxb_ref[rows, :], wd_ref[0], NT_DIMS, preferred_element_type=F32) * 0.5
            gv = gate_ref[0, rows, :].astype(F32)
            uv = up_ref[0, rows, :].astype(F32)
            sg = _sigmoid_tanh(gv)
            silu = gv * sg
            act_ref[0, rows, :] = (silu * uv).astype(BF16)
            dub = (dact * silu).astype(BF16)
            dgb = (dact * uv * sg * (1.0 + gv * (1.0 - sg))).astype(BF16)
            dup_ref[0, rows, :] = dub
            dgate_ref[0, rows, :] = dgb
            prod = (jnp.dot(dgb, wg_ref[0], preferred_element_type=F32)
                    + jnp.dot(dub, wu_ref[0], preferred_element_type=F32))

            dh_ref[rows, :] = prod + jnp.where(s > 0, dh_ref[rows, :], 0.0)

        @pl.when(jnp.logical_and(i == 0, s == 0))
        def _():
            dgn_ref[...] = jnp.zeros_like(dgn_ref)

        @pl.when(s == ns - 1)
        def _():
            dx, dgs = _rms_bwd_rows(x_ref[...], g_ref[...], dh_ref[...])
            dx = dx + dxo_ref[...]
            dx_ref[...] = dx
            dxob_ref[...] = dx.astype(BF16)
            dgn_ref[...] += jnp.sum(dgs, axis=0, keepdims=True)

    row = pl.BlockSpec((tm, d), lambda i, s: (i, 0))
    vec = pl.BlockSpec((1, d), lambda i, s: (0, 0))
    wrow = pl.BlockSpec((1, f, d), lambda i, s: (s, 0, 0))
    hid = pl.BlockSpec((1, tm, f), lambda i, s: (s, i, 0))
    hid_sh = jax.ShapeDtypeStruct((ns, t, f), BF16)
    return pl.pallas_call(
        body,
        out_shape=(jax.ShapeDtypeStruct((t, d), F32), jax.ShapeDtypeStruct((t, d), BF16),
                   jax.ShapeDtypeStruct((1, d), F32), hid_sh, hid_sh, hid_sh),
        grid=(t // tm, ns), in_specs=[row, row, row, vec, hid, hid, wrow, wrow, wrow],
        out_specs=(row, row, vec, hid, hid, hid), scratch_shapes=[pltpu.VMEM((tm, d), F32)],
        compiler_params=_cparams(("arbitrary", "arbitrary")), name=name)(dxo, dxo_b, x, g, gate, up, wg, wu, wd)


def _ffn_bwd_w_call(h, dxo_b, dgate, dup, act, name):
    t, d = h.shape
    ns, _, f = dgate.shape
    tm = _tile(t, FFN_W_ROWS)
    nm = t // tm

    def body(h_ref, dxb_ref, dgate_ref, dup_ref, act_ref, dwg_ref, dwu_ref, dwd_ref, ag_ref, au_ref, ad_ref):
        i = pl.program_id(1)
        hv = h_ref[...]
        pg = lax.dot_general(dgate_ref[0], hv, TN_DIMS, preferred_element_type=F32)
        pu = lax.dot_general(dup_ref[0], hv, TN_DIMS, preferred_element_type=F32)
        pd = lax.dot_general(act_ref[0], dxb_ref[...], TN_DIMS, preferred_element_type=F32)

        ag_ref[...] = pg + jnp.where(i > 0, ag_ref[...], 0.0)
        au_ref[...] = pu + jnp.where(i > 0, au_ref[...], 0.0)
        ad_ref[...] = pd + jnp.where(i > 0, ad_ref[...], 0.0)

        @pl.when(i == nm - 1)
        def _():
            dwg_ref[0] = ag_ref[...].astype(BF16)
            dwu_ref[0] = au_ref[...].astype(BF16)
            dwd_ref[0] = (0.5 * ad_ref[...]).astype(BF16)

    row = pl.BlockSpec((tm, d), lambda s, i: (i, 0))
    hid = pl.BlockSpec((1, tm, f), lambda s, i: (s, i, 0))
    wrow = pl.BlockSpec((1, f, d), lambda s, i: (s, 0, 0))
    wsh = jax.ShapeDtypeStruct((ns, f, d), BF16)
    return pl.pallas_call(
        body, out_shape=(wsh, wsh, wsh),
        grid=(ns, nm), in_specs=[row, row, hid, hid, hid], out_specs=(wrow, wrow, wrow),
        scratch_shapes=[pltpu.VMEM((f, d), F32), pltpu.VMEM((f, d), F32), pltpu.VMEM((f, d), F32)],
        compiler_params=_cparams(("parallel", "arbitrary")), name=name)(h, dxo_b, dgate, dup, act)


def _loss_head(x, g, tgt, name):
    t, w = x.shape
    tr = _row_tile(t)

    def body(x_ref, g_ref, t_ref, loss_ref, dx_ref, dxb_ref, dg_ref):
        xv = x_ref[...]
        gv = g_ref[...]
        r = lax.rsqrt(jnp.mean(xv * xv, axis=-1, keepdims=True) + EPS)
        nrm = xv * r
        err = nrm * gv - t_ref[...]
        dout = err * (1.0 / w)
        dn = dout * gv
        dx = r * (dn - nrm * jnp.mean(dn * nrm, axis=-1, keepdims=True))
        dx_ref[...] = dx
        dxb_ref[...] = dx.astype(BF16)

        @pl.when(pl.program_id(0) == 0)
        def _():
            dg_ref[...] = jnp.zeros_like(dg_ref)
            loss_ref[...] = jnp.zeros_like(loss_ref)

        dg_ref[...] += jnp.sum(dout * nrm, axis=0, keepdims=True)
        part = jnp.sum(jnp.sum(err * err, axis=-1, keepdims=True) * (0.5 / w), axis=0, keepdims=True)
        loss_ref[...] += jnp.broadcast_to(part, loss_ref.shape)

    row = pl.BlockSpec((tr, w), lambda i: (i, 0))
    vec = pl.BlockSpec((1, w), lambda i: (0, 0))
    return pl.pallas_call(
        body, out_shape=(jax.ShapeDtypeStruct((1, LANES), F32), jax.ShapeDtypeStruct((t, w), F32),
                         jax.ShapeDtypeStruct((t, w), BF16), jax.ShapeDtypeStruct((1, w), F32)),
        grid=(t // tr,), in_specs=[row, vec, row],
        out_specs=(pl.BlockSpec((1, LANES), lambda i: (0, 0)), row, row, vec),
        compiler_params=_cparams(("arbitrary",)), name=name)(x, g, tgt)


def _attn_bias():
    slopes = np.asarray(2.0 ** (-8.0 * (np.arange(ATTN_HEADS) + 1) / ATTN_HEADS), np.float32)
    qi = np.arange(QBLOCK)[:, None]
    kj = np.arange(3 * QBLOCK)[None, :]
    rel = np.abs(kj - QBLOCK - qi).astype(np.float32)
    tile = np.where(rel <= WINDOW, -slopes[:, None, None] * rel[None], np.float32(NEG_INF)).astype(np.float32)
    return jnp.asarray(tile.reshape(KV_HEADS, GQ * QBLOCK, 3 * QBLOCK))


def _attn_scores(q, k3, n, nb, bias):
    s = lax.dot_general(q, k3, NT_DIMS, preferred_element_type=F32) * (HEAD_DIM ** -0.5)
    col = lax.broadcasted_iota(jnp.int32, (1, 3 * QBLOCK), 1)
    inside = (col >= jnp.where(n == 0, QBLOCK, 0)) & (col < jnp.where(n == nb - 1, 2 * QBLOCK, 3 * QBLOCK))
    return jnp.where(inside, s + bias, NEG_INF)


Q_COL, K_COL, V_COL, U_COL = 0, ATTN_WIDTH // LANES, ATTN_WIDTH // LANES + 1, ATTN_WIDTH // LANES + 2


def _key_rows(ref, n, nb):
    prev, nxt = jnp.maximum(n - 1, 0), jnp.minimum(n + 1, nb - 1)
    blk = lambda b: ref[pl.ds(pl.multiple_of(b * QBLOCK, QBLOCK), QBLOCK), :]
    return jnp.concatenate([blk(prev), blk(n), blk(nxt)], axis=0)


def _head_tiles(x, kh, low):
    tiles = []
    for g in range(GQ):
        h = GQ * kh + g
        t128 = x[:, LANES * (h // 2):LANES * (h // 2 + 1)]
        t128 = jnp.where(low if h % 2 == 0 else jnp.logical_not(low), t128, 0.0)
        if h % 2 != kh:
            t128 = pltpu.roll(t128, HEAD_DIM, 1)
        tiles.append(t128)
    return jnp.concatenate(tiles, axis=0)


def _head_merge(per_kh, low):
    out = []
    for j in range(ATTN_HEADS // 2):
        pair = []
        for h in (2 * j, 2 * j + 1):
            kh, g = h // GQ, h % GQ
            t128 = per_kh[kh][g * QBLOCK:(g + 1) * QBLOCK, :]
            if h % 2 != kh:
                t128 = pltpu.roll(t128, HEAD_DIM, 1)
            pair.append(t128)
        out.append(jnp.where(low, pair[0], pair[1]))
    return jnp.concatenate(out, axis=1)


def _attn_fwd_proj(proj, sink_rows, bias, name):
    t = proj.shape[0]
    nb = t // QBLOCK
    rows = GQ * QBLOCK

    def body(q_ref, k_ref, v_ref, sink_ref, bias_ref, o_ref, lse_ref):
        n = pl.program_id(0)
        low = lax.broadcasted_iota(jnp.int32, (QBLOCK, LANES), 1) < HEAD_DIM
        k3 = _key_rows(k_ref, n, nb).astype(BF16)
        v3 = _key_rows(v_ref, n, nb).astype(BF16)
        q = q_ref[...]
        outs = []
        for kh in range(KV_HEADS):
            qs = _head_tiles(q, kh, low).astype(BF16)
            s = _attn_scores(qs, k3, n, nb, bias_ref[kh])
            sink = sink_ref[kh]
            mx = jnp.maximum(jnp.max(s, axis=-1, keepdims=True), sink)
            p = jnp.exp(s - mx)
            den = jnp.sum(p, axis=-1, keepdims=True) + jnp.exp(sink - mx)
            outs.append(jnp.dot(p.astype(BF16), v3, preferred_element_type=F32) / den)
            lse_ref[0, kh] = mx + jnp.log(den)
        o_ref[...] = _head_merge(outs, low)

    strip = lambda col: pl.BlockSpec((t, LANES), lambda n, col=col: (0, col))
    rowspec = pl.BlockSpec((KV_HEADS, rows, 1), lambda n: (0, 0, 0))
    biasspec = pl.BlockSpec((KV_HEADS, rows, 3 * QBLOCK), lambda n: (0, 0, 0))
    return pl.pallas_call(
        body, out_shape=(jax.ShapeDtypeStruct((t, ATTN_WIDTH), F32), jax.ShapeDtypeStruct((nb, KV_HEADS, rows, 1), F32)),
        grid=(nb,), in_specs=[pl.BlockSpec((QBLOCK, ATTN_WIDTH), lambda n: (n, 0)), strip(K_COL), strip(V_COL),
                              rowspec, biasspec],
        out_specs=(pl.BlockSpec((QBLOCK, ATTN_WIDTH), lambda n: (n, 0)),
                   pl.BlockSpec((1, KV_HEADS, rows, 1), lambda n: (n, 0, 0, 0))),
        compiler_params=_cparams(("parallel",)), name=name)(proj, proj, proj, sink_rows, bias)


def _attn_bwd_proj(proj, sink_rows, bias, o, lse, do, name):
    t = proj.shape[0]
    nb = t // QBLOCK
    rows = GQ * QBLOCK
    scale = HEAD_DIM ** -0.5

    def body(q_ref, k_ref, v_ref, sink_ref, bias_ref, o_ref, lse_ref, do_ref, dq_ref, dk_ref, dv_ref, ds_ref):
        n = pl.program_id(0)

        @pl.when(n == 0)
        def _():
            dk_ref[...] = jnp.zeros_like(dk_ref)
            dv_ref[...] = jnp.zeros_like(dv_ref)
            ds_ref[...] = jnp.zeros_like(ds_ref)

        low = lax.broadcasted_iota(jnp.int32, (QBLOCK, LANES), 1) < HEAD_DIM
        k3 = _key_rows(k_ref, n, nb).astype(BF16)
        v3 = _key_rows(v_ref, n, nb).astype(BF16)
        q, dov = q_ref[...], do_ref[...]
        dod = dov * o_ref[...]
        dqs = []
        dk3 = jnp.zeros((3 * QBLOCK, LANES), F32)
        dv3 = jnp.zeros((3 * QBLOCK, LANES), F32)
        for kh in range(KV_HEADS):
            qs = _head_tiles(q, kh, low).astype(BF16)
            dos = _head_tiles(dov, kh, low).astype(BF16)
            delta = jnp.sum(_head_tiles(dod, kh, low), axis=-1, keepdims=True)
            lse_kh = lse_ref[0, kh]
            s = _attn_scores(qs, k3, n, nb, bias_ref[kh])
            p = jnp.exp(s - lse_kh)
            dp = lax.dot_general(dos, v3, NT_DIMS, preferred_element_type=F32)
            dsb = (p * (dp - delta)).astype(BF16)
            dqs.append(jnp.dot(dsb, k3, preferred_element_type=F32) * scale)
            dk3 = dk3 + lax.dot_general(dsb, qs, TN_DIMS, preferred_element_type=F32) * scale
            dv3 = dv3 + lax.dot_general(p.astype(BF16), dos, TN_DIMS, preferred_element_type=F32)
            dsink_rows = -jnp.exp(sink_ref[kh] - lse_kh) * delta
            ds_ref[kh] += jnp.sum(dsink_rows.reshape(GQ, QBLOCK, 1), axis=1)
        dq_ref[...] = _head_merge(dqs, low)
        prev, nxt = jnp.maximum(n - 1, 0), jnp.minimum(n + 1, nb - 1)
        for j, b in enumerate((prev, n, nxt)):
            blk = pl.ds(pl.multiple_of(b * QBLOCK, QBLOCK), QBLOCK)
            dk_ref[blk, :] += dk3[j * QBLOCK:(j + 1) * QBLOCK, :]
            dv_ref[blk, :] += dv3[j * QBLOCK:(j + 1) * QBLOCK, :]

    strip = lambda col: pl.BlockSpec((t, LANES), lambda n, col=col: (0, col))
    rowspec = pl.BlockSpec((KV_HEADS, rows, 1), lambda n: (0, 0, 0))
    qspec = pl.BlockSpec((QBLOCK, ATTN_WIDTH), lambda n: (n, 0))
    kv_out = pl.BlockSpec((t, LANES), lambda n: (0, 0))
    biasspec = pl.BlockSpec((KV_HEADS, rows, 3 * QBLOCK), lambda n: (0, 0, 0))
    return pl.pallas_call(
        body,
        out_shape=(jax.ShapeDtypeStruct((t, ATTN_WIDTH), F32), jax.ShapeDtypeStruct((t, LANES), F32),
                   jax.ShapeDtypeStruct((t, LANES), F32), jax.ShapeDtypeStruct((KV_HEADS, GQ, 1), F32)),
        grid=(nb,),
        in_specs=[qspec, strip(K_COL), strip(V_COL), rowspec, biasspec, qspec,
                  pl.BlockSpec((1, KV_HEADS, rows, 1), lambda n: (n, 0, 0, 0)), qspec],
        out_specs=(qspec, kv_out, kv_out, pl.BlockSpec((KV_HEADS, GQ, 1), lambda n: (0, 0, 0))),
        compiler_params=_cparams(("arbitrary",)), name=name)(proj, proj, proj, sink_rows, bias, o, lse, do)


def _scan_tables(a_re, a_im, reverse):
    pw = [(a_re, a_im)]
    for _ in range(SUBLANES - 1):
        pr, pi = pw[-1]
        pw.append((pr * a_re - pi * a_im, pr * a_im + pi * a_re))
    rows = np.arange(SUBLANES)
    tabs = []
    for d in (1, 2, 4):
        mask = (rows <= SUBLANES - 1 - d) if reverse else (rows >= d)
        m = jnp.asarray(mask, F32)[:, None]
        tabs += [m * pw[d - 1][0][None, :], m * pw[d - 1][1][None, :]]
    order = (SUBLANES - 1 - rows) if reverse else rows
    tabs += [jnp.stack([pw[j][0] for j in order]), jnp.stack([pw[j][1] for j in order])]
    tab = jnp.stack(tabs)
    return tab.reshape(8, SUBLANES, N_STRIPS, STRIP_ST).transpose(2, 0, 1, 3)


def _scan_chunk(v_ref, mir_ref, mii_ref, tab_ref, mor_ref, moi_ref, y_ref, xr_ref, xi_ref, carry_ref, reverse):
    nblk = xr_ref.shape[0] // SUBLANES

    @pl.when(pl.program_id(1) == 0)
    def _():
        carry_ref[...] = jnp.zeros_like(carry_ref)

    vb = v_ref[...].astype(BF16)
    xr_ref[...] = jnp.dot(vb, mir_ref[0], preferred_element_type=F32)
    xi_ref[...] = jnp.dot(vb, mii_ref[0], preferred_element_type=F32)

    def blk(i, carry):
        cr, ci = carry
        b = (nblk - 1 - i) if reverse else i
        r0 = pl.multiple_of(b * SUBLANES, SUBLANES)
        xr = xr_ref[pl.ds(r0, SUBLANES), :]
        xi = xi_ref[pl.ds(r0, SUBLANES), :]
        for j, d in enumerate((1, 2, 4)):
            tr_, ti_ = tab_ref[0, 2 * j], tab_ref[0, 2 * j + 1]
            sh = (SUBLANES - d) if reverse else d
            sr = pltpu.roll(xr, sh, 0)
            si = pltpu.roll(xi, sh, 0)
            xr, xi = xr + tr_ * sr - ti_ * si, xi + tr_ * si + ti_ * sr
        pr, pi = tab_ref[0, 6], tab_ref[0, 7]
        xr, xi = xr + pr * cr - pi * ci, xi + pr * ci + pi * cr
        xr_ref[pl.ds(r0, SUBLANES), :] = xr
        xi_ref[pl.ds(r0, SUBLANES), :] = xi
        edge = 0 if reverse else SUBLANES - 1
        return (jnp.broadcast_to(xr[edge:edge + 1, :], xr.shape),
                jnp.broadcast_to(xi[edge:edge + 1, :], xi.shape))

    cr, ci = lax.fori_loop(0, nblk, blk, (carry_ref[0], carry_ref[1]))
    carry_ref[0] = cr
    carry_ref[1] = ci
    y_ref[...] = (jnp.dot(xr_ref[...].astype(BF16), mor_ref[0], preferred_element_type=F32)
                  + jnp.dot(xi_ref[...].astype(BF16), moi_ref[0], preferred_element_type=F32))


def _scan_adjoint(dy, u, xr, xi, mi_re, mi_im, tab, mo_re, mo_im, reverse, name):
    t = dy.shape[0]
    tc = _tile(t, SCAN_ROWS)
    nc = t // tc
    hb = tc // SUBLANES
    fwd_reverse = not reverse

    def body(dy_ref, mir_ref, mii_ref, tab_ref, mor_ref, moi_ref, u_ref, xr_ref, xi_ref, hr_ref, hi_ref,
             du_ref, dmir_ref, dmii_ref, dmor_ref, dmoi_ref, da_ref, lr_ref, li_ref, carry_ref):
        _scan_chunk(dy_ref, mir_ref, mii_ref, tab_ref, mor_ref, moi_ref, du_ref, lr_ref, li_ref, carry_ref, reverse)
        c = pl.program_id(1)
        rc = (nc - 1 - c) if reverse else c

        @pl.when(c == 0)
        def _():
            for r in (dmir_ref, dmii_ref, dmor_ref, dmoi_ref, da_ref):
                r[...] = jnp.zeros_like(r)

        xrv, xiv, lrv, liv = xr_ref[...], xi_ref[...], lr_ref[...], li_ref[...]
        row = lax.broadcasted_iota(jnp.int32, xrv.shape, 0)
        if fwd_reverse:
            live = (rc < nc - 1).astype(F32)
            edge_r, edge_i = hr_ref[0:1, :] * live, hi_ref[0:1, :] * live
            xpr = jnp.where(row == tc - 1, edge_r, pltpu.roll(xrv, tc - 1, 0))
            xpi = jnp.where(row == tc - 1, edge_i, pltpu.roll(xiv, tc - 1, 0))
        else:
            live = (rc > 0).astype(F32)
            edge_r, edge_i = hr_ref[SUBLANES - 1:SUBLANES, :] * live, hi_ref[SUBLANES - 1:SUBLANES, :] * live
            xpr = jnp.where(row == 0, edge_r, pltpu.roll(xrv, 1, 0))
            xpi = jnp.where(row == 0, edge_i, pltpu.roll(xiv, 1, 0))
        da_ref[0, 0:1, :] += jnp.sum(xpr * lrv + xpi * liv, axis=0, keepdims=True)
        da_ref[0, 1:2, :] += jnp.sum(xpr * liv - xpi * lrv, axis=0, keepdims=True)
        ub, dyb = u_ref[...].astype(BF16), dy_ref[...].astype(BF16)
        dmir_ref[0] += lax.dot_general(ub, lrv.astype(BF16), TN_DIMS, preferred_element_type=F32)
        dmii_ref[0] += lax.dot_general(ub, liv.astype(BF16), TN_DIMS, preferred_element_type=F32)
        dmor_ref[0] += lax.dot_general(xrv.astype(BF16), dyb, TN_DIMS, preferred_element_type=F32)
        dmoi_ref[0] += lax.dot_general(xiv.astype(BF16), dyb, TN_DIMS, preferred_element_type=F32)

    rowblk = (lambda c: nc - 1 - c) if reverse else (lambda c: c)
    tmap = lambda s, c: (rowblk(c), s)
    col0 = u.shape[1] // STRIP_IN - N_STRIPS
    umap = lambda s, c: (rowblk(c), s + col0)
    if fwd_reverse:
        hmap = lambda s, c: (jnp.minimum((rowblk(c) + 1) * hb, t // SUBLANES - 1), s)
    else:
        hmap = lambda s, c: (jnp.maximum(rowblk(c) * hb - 1, 0), s)
    smap3 = lambda s, c: (s, 0, 0)
    narrow = pl.BlockSpec((tc, STRIP_IN), tmap)
    wide = pl.BlockSpec((tc, STRIP_ST), tmap)
    halo = pl.BlockSpec((SUBLANES, STRIP_ST), hmap)
    m_in = pl.BlockSpec((1, STRIP_IN, STRIP_ST), smap3)
    m_out = pl.BlockSpec((1, STRIP_ST, STRIP_IN), smap3)
    return pl.pallas_call(
        body,
        out_shape=(jax.ShapeDtypeStruct((t, SSM_WIDTH), F32),
                   jax.ShapeDtypeStruct((N_STRIPS, STRIP_IN, STRIP_ST), F32),
                   jax.ShapeDtypeStruct((N_STRIPS, STRIP_IN, STRIP_ST), F32),
                   jax.ShapeDtypeStruct((N_STRIPS, STRIP_ST, STRIP_IN), F32),
                   jax.ShapeDtypeStruct((N_STRIPS, STRIP_ST, STRIP_IN), F32),
                   jax.ShapeDtypeStruct((N_STRIPS, SUBLANES, STRIP_ST), F32)),
        grid=(N_STRIPS, nc),
        in_specs=[narrow, m_in, m_in, pl.BlockSpec((1, 8, SUBLANES, STRIP_ST), lambda s, c: (s, 0, 0, 0)),
                  m_out, m_out, pl.BlockSpec((tc, STRIP_IN), umap), wide, wide, halo, halo],
        out_specs=(narrow, m_in, m_in, m_out, m_out, pl.BlockSpec((1, SUBLANES, STRIP_ST), smap3)),
        scratch_shapes=[pltpu.VMEM((tc, STRIP_ST), F32), pltpu.VMEM((tc, STRIP_ST), F32),
                        pltpu.VMEM((2, SUBLANES, STRIP_ST), F32)],
        compiler_params=_cparams(("parallel", "arbitrary")), name=name)(
            dy, mi_re, mi_im, tab, mo_re, mo_im, u, xr, xi, xr, xi)


def _scan(v, mi_re, mi_im, tab, mo_re, mo_im, reverse, name):
    t = v.shape[0]
    tc = _tile(t, SCAN_ROWS)
    nc = t // tc

    def body(v_ref, mir_ref, mii_ref, tab_ref, mor_ref, moi_ref, y_ref, xr_ref, xi_ref, carry_ref):
        _scan_chunk(v_ref, mir_ref, mii_ref, tab_ref, mor_ref, moi_ref, y_ref, xr_ref, xi_ref, carry_ref, reverse)

    tmap = (lambda s, c: (nc - 1 - c, s)) if reverse else (lambda s, c: (c, s))
    col0 = v.shape[1] // STRIP_IN - N_STRIPS
    vmap = lambda s, c: (tmap(s, c)[0], s + col0)
    smap3 = lambda s, c: (s, 0, 0)
    return pl.pallas_call(
        body,
        out_shape=(jax.ShapeDtypeStruct((t, SSM_WIDTH), F32),
                   jax.ShapeDtypeStruct((t, N_STRIPS * STRIP_ST), F32),
                   jax.ShapeDtypeStruct((t, N_STRIPS * STRIP_ST), F32)),
        grid=(N_STRIPS, nc),
        in_specs=[pl.BlockSpec((tc, STRIP_IN), vmap),
                  pl.BlockSpec((1, STRIP_IN, STRIP_ST), smap3), pl.BlockSpec((1, STRIP_IN, STRIP_ST), smap3),
                  pl.BlockSpec((1, 8, SUBLANES, STRIP_ST), lambda s, c: (s, 0, 0, 0)),
                  pl.BlockSpec((1, STRIP_ST, STRIP_IN), smap3), pl.BlockSpec((1, STRIP_ST, STRIP_IN), smap3)],
        out_specs=(pl.BlockSpec((tc, STRIP_IN), tmap), pl.BlockSpec((tc, STRIP_ST), tmap),
                   pl.BlockSpec((tc, STRIP_ST), tmap)),
        scratch_shapes=[pltpu.VMEM((2, SUBLANES, STRIP_ST), F32)],
        compiler_params=_cparams(("parallel", "arbitrary")), name=name)(v, mi_re, mi_im, tab, mo_re, mo_im)


def _ssm_prep(lam_re, lam_im, log_dt, bt_re, bt_im, c_re, c_im):
    lr = jnp.minimum(lam_re, LAMBDA_RE_MAX)
    li = lam_im
    dt = jnp.exp(log_dt)[:, None]
    mag = jnp.exp(lr * dt)
    a_re = mag * jnp.cos(li * dt)
    a_im = mag * jnp.sin(li * dt)
    den = lr * lr + li * li
    coef_re = ((a_re - 1.0) * lr + a_im * li) / den
    coef_im = (a_im * lr - (a_re - 1.0) * li) / den
    bb_re = coef_re[:, None, :] * bt_re - coef_im[:, None, :] * bt_im
    bb_im = coef_re[:, None, :] * bt_im + coef_im[:, None, :] * bt_re
    eye = jnp.eye(SSM_GROUPS // N_STRIPS, dtype=F32)

    def strips(m):
        g, a, b = m.shape
        m4 = m.reshape(N_STRIPS, g // N_STRIPS, a, b)
        return jnp.einsum('sgab,gk->sgakb', m4, eye).reshape(N_STRIPS, g // N_STRIPS * a, g // N_STRIPS * b)

    mi_re = strips(bb_re)
    mi_im = strips(bb_im)
    mo_re = strips(jnp.swapaxes(c_re, 1, 2))
    mo_im = strips(-jnp.swapaxes(c_im, 1, 2))
    return a_re.reshape(-1), a_im.reshape(-1), mi_re, mi_im, mo_re, mo_im


def _gelu(x):
    c = math.sqrt(2.0 / math.pi)
    return 0.5 * x * (1.0 + jnp.tanh(c * (x + 0.044715 * x * x * x)))


def _gelu_grad(x):
    c = math.sqrt(2.0 / math.pi)
    th = jnp.tanh(c * (x + 0.044715 * x * x * x))
    return 0.5 * (1.0 + th) + 0.5 * x * (1.0 - th * th) * c * (1.0 + 3.0 * 0.044715 * x * x)


def _last_cols_specs(u, w, tr):
    half = w // 2
    first = (u.shape[1] - w) // half
    assert first * half == u.shape[1] - w
    return [pl.BlockSpec((tr, half), lambda i, k=k: (i, first + k)) for k in range(2)]


def _ssm_post_fwd(u, yf, yb, d, wglu, bglu, name):
    t, w = yf.shape
    tr = _row_tile(t)

    def body(ua_ref, ub_ref, yf_ref, yb_ref, d_ref, w_ref, b_ref, s_ref, y0_ref, z_ref):
        uv = jnp.concatenate([ua_ref[...], ub_ref[...]], axis=1)
        y0 = d_ref[...] * uv + yf_ref[...] + yb_ref[...]
        yg = _gelu(y0)
        z = jnp.dot(yg.astype(BF16), w_ref[...], preferred_element_type=F32) + b_ref[...]
        s_ref[...] = yg * _sigmoid(z)
        y0_ref[...] = y0
        z_ref[...] = z

    row = pl.BlockSpec((tr, w), lambda i: (i, 0))
    vec = pl.BlockSpec((1, w), lambda i: (0, 0))
    mat = pl.BlockSpec((w, w), lambda i: (0, 0))
    sh = jax.ShapeDtypeStruct((t, w), F32)
    return pl.pallas_call(body, out_shape=(sh, sh, sh), grid=(t // tr,),
                          in_specs=[*_last_cols_specs(u, w, tr), row, row, vec, mat, vec], out_specs=(row, row, row),
                          compiler_params=_cparams(("parallel",)), name=name)(u, u, yf, yb, d, wglu, bglu)


def _ssm_post_bwd(ds, y0, z, u, d, wglu, name):
    t, w = ds.shape
    tr = _row_tile(t)

    def body(ds_ref, y0_ref, z_ref, ua_ref, ub_ref, d_ref, w_ref, dy0_ref, dw_ref, db_ref, dd_ref):
        @pl.when(pl.program_id(0) == 0)
        def _():
            dw_ref[...] = jnp.zeros_like(dw_ref)
            db_ref[...] = jnp.zeros_like(db_ref)
            dd_ref[...] = jnp.zeros_like(dd_ref)

        y0 = y0_ref[...]
        yg = _gelu(y0)
        sg = _sigmoid(z_ref[...])
        dsv = ds_ref[...]
        dz = dsv * yg * sg * (1.0 - sg)
        dzb = dz.astype(BF16)
        dyg = dsv * sg + lax.dot_general(dzb, w_ref[...], (((1,), (1,)), ((), ())), preferred_element_type=F32)
        dy0 = dyg * _gelu_grad(y0)
        dy0_ref[...] = dy0
        dw_ref[...] += lax.dot_general(yg.astype(BF16), dzb, (((0,), (0,)), ((), ())), preferred_element_type=F32)
        db_ref[...] += jnp.sum(dz, axis=0, keepdims=True)
        uv = jnp.concatenate([ua_ref[...], ub_ref[...]], axis=1)
        dd_ref[...] += jnp.sum(dy0 * uv, axis=0, keepdims=True)

    row = pl.BlockSpec((tr, w), lambda i: (i, 0))
    vec = pl.BlockSpec((1, w), lambda i: (0, 0))
    mat = pl.BlockSpec((w, w), lambda i: (0, 0))
    return pl.pallas_call(
        body, out_shape=(jax.ShapeDtypeStruct((t, w), F32), jax.ShapeDtypeStruct((w, w), F32),
                         jax.ShapeDtypeStruct((1, w), F32), jax.ShapeDtypeStruct((1, w), F32)),
        grid=(t // tr,), in_specs=[row, row, row, *_last_cols_specs(u, w, tr), vec, mat],
        out_specs=(row, mat, vec, vec),
        compiler_params=_cparams(("arbitrary",)), name=name)(ds, y0, z, u, u, d, wglu)


def _du_combine(dy0, d, du_f, du_b, name):
    t, w = dy0.shape
    tr = _row_tile(t)

    def body(dy_ref, d_ref, a_ref, b_ref, o_ref):
        o_ref[...] = d_ref[...] * dy_ref[...] + a_ref[...] + b_ref[...]

    row = pl.BlockSpec((tr, w), lambda i: (i, 0))
    vec = pl.BlockSpec((1, w), lambda i: (0, 0))
    return pl.pallas_call(body, out_shape=jax.ShapeDtypeStruct((t, w), F32), grid=(t // tr,),
                          in_specs=[row, vec, row, row], out_specs=row, compiler_params=_cparams(("parallel",)),
                          name=name)(dy0, d, du_f, du_b)


def _ffn_fwd(x, g, wg, wu, wd, tag):
    xo, h, gate, up = _ffn_fwd_call(x, g, wg, wu, wd, f"{tag}_fwd")
    return xo, (h, gate, up)


def _ffn_bwd(dxo, dxo_b, x, g, wg, wu, wd, saved, tag):
    h, gate, up = saved
    dx, dx_b, dg, dgate, dup, act = _ffn_bwd_x_call(dxo, dxo_b, x, g, gate, up, wg, wu, wd, f"{tag}_bwd_x")
    dwg, dwu, dwd = _ffn_bwd_w_call(h, dxo_b, dgate, dup, act, f"{tag}_bwd_w")
    return dx, dx_b, dg, dwg, dwu, dwd


def _local_step(x, tgt, w, get_weights, put_grads, reduce_wide):
    t = x.shape[0]
    row = lambda a: a.reshape(1, -1)
    grads = {}

    w = dict(w)

    ssm_names = ['ssm_lambda_re', 'ssm_lambda_im', 'ssm_log_dt', 'ssm_b_re', 'ssm_b_im', 'ssm_c_re', 'ssm_c_im']
    tr3 = lambda m: jnp.swapaxes(m, 1, 2)
    fwd_ops, adj_ops, vjps = [], [], []
    for direction in range(2):
        rev = direction == 1
        prep, vjp = jax.vjp(_ssm_prep, *[w[n][direction] for n in ssm_names])
        a_re, a_im = prep[0], prep[1]
        mi_re, mi_im, mo_re, mo_im = (m.astype(BF16) for m in prep[2:])
        fwd_ops.append((mi_re, mi_im, _scan_tables(a_re, a_im, rev), mo_re, mo_im))
        adj_ops.append((tr3(mo_re), tr3(mo_im), _scan_tables(a_re, -a_im, not rev), tr3(mi_re), tr3(mi_im)))
        vjps.append(vjp)
    sink_rows = jnp.repeat(w['attn_sinks'].reshape(KV_HEADS, GQ), QBLOCK, axis=1)[..., None]
    bias = _attn_bias()
    prepared = sum(jnp.sum(op[:1, :1].astype(F32)) for ops in fwd_ops + adj_ops for op in ops) + sink_rows[0, 0, 0]

    w.update(get_weights('ffn1', prepared.reshape(1, 1)))
    x1, ffn1_saved = _ffn_fwd(x, w['norm_ffn1'], w['ffn1_w_gate'], w['ffn1_w_up'], w['ffn1_w_down'], "ffn1")
    w.update(get_weights('mix', x1))

    h2 = _rms_fwd(x1, w['norm_mix'], "mix_norm")
    proj = _mm(h2, w['w_in'], tb=True, name="in_proj")[0]
    u = proj

    attn, lse = _attn_fwd_proj(proj, sink_rows, bias, "attn_fwd")

    ys, states = [], []
    for direction in range(2):
        y, xr, xi = _scan(u, *fwd_ops[direction], direction == 1, f"s5_fwd{direction}")
        ys.append(y)
        states.append((xr, xi))
    d_row = row(w['ssm_d'])
    s, y0, z = _ssm_post_fwd(u, ys[0], ys[1], d_row, w['ssm_glu_w'], row(w['ssm_glu_b']), "ssm_post")

    ma = _rms_fwd(attn, row(w['attn_out_norm']), "attn_out_norm")
    ms = _rms_fwd(s, row(w['ssm_out_norm']), "ssm_out_norm")
    mixed = jnp.concatenate([ma, ms], axis=-1)
    x2 = _mm(mixed, w['w_out'], res=x1, reduce_s=True, name="out_proj")

    w.update(get_weights('ffn2', x2))
    x3, ffn2_saved = _ffn_fwd(x2, w['norm_ffn2'], w['ffn2_w_gate'], w['ffn2_w_up'], w['ffn2_w_down'], "ffn2")

    loss, dx3, dx3_b, dgf = _loss_head(x3, row(w['final_norm']), tgt, "loss_head")
    grads['final_norm'] = dgf.reshape(w['final_norm'].shape)

    dx2, dx2_b, dg, dwg, dwu, dwd = _ffn_bwd(dx3, dx3_b, x2, w['norm_ffn2'], w['ffn2_w_gate'], w['ffn2_w_up'],
                                             w['ffn2_w_down'], ffn2_saved, "ffn2")
    grads['norm_ffn2'] = dg
    sent = put_grads('ffn2', dict(ffn2_w_gate=dwg, ffn2_w_up=dwu, ffn2_w_down=dwd))

    dmixed = _mm(dx2_b, w['w_out'], tb=True, reduce_s=True, after=sent, name="out_proj_dx")
    dw_out = _mm(mixed, dx2_b, ta=True, out_dtype=BF16, name="out_proj_dw")[0]
    dattn, _, dga = _rms_bwd(attn, row(w['attn_out_norm']), dmixed[:, :ATTN_WIDTH], None, "attn_out_dnorm")
    ds, _, dgs = _rms_bwd(s, row(w['ssm_out_norm']), dmixed[:, ATTN_WIDTH:], None, "ssm_out_dnorm")
    grads.update(attn_out_norm=dga, ssm_out_norm=dgs)

    dy0, dwglu, dbglu, dd = _ssm_post_bwd(ds, y0, z, u, d_row, w['ssm_glu_w'], "ssm_post_bwd")
    grads['ssm_glu_b'] = dbglu
    grads['ssm_d'] = dd.reshape(w['ssm_d'].shape)
    dparams, du_dirs = [], []
    for direction in range(2):
        rev = direction == 1
        xr, xi = states[direction]
        du_dir, dmir, dmii, dmor, dmoi, da = _scan_adjoint(dy0, u, xr, xi, *adj_ops[direction], not rev,
                                                            f"s5_adj{direction}")
        du_dirs.append(du_dir)
        da_re = da[:, 0, :].reshape(-1)
        da_im = da[:, 1, :].reshape(-1)
        dparams.append(vjps[direction]((da_re, da_im, dmir, dmii, dmor, dmoi)))
    du = _du_combine(dy0, d_row, du_dirs[0], du_dirs[1], "ssm_du")
    for i, n in enumerate(ssm_names):
        grads[n] = jnp.stack([dparams[0][i], dparams[1][i]])
    wide_sum = reduce_wide(grads)

    dq, dk, dv, dsink = _attn_bwd_proj(proj, sink_rows, bias, attn, lse, dattn, "attn_bwd")
    grads['attn_sinks'] = dsink.reshape(w['attn_sinks'].shape)
    dproj = jnp.concatenate([dq, dk, dv, du], axis=-1).astype(BF16)

    dw_in = _mm(dproj, h2, ta=True, out_dtype=BF16, after=wide_sum, name="in_proj_dw")[0]
    sent = put_grads('mix', dict(w_in=dw_in, ssm_glu_w=dwglu, w_out=dw_out))
    dh2 = _mm(dproj, w['w_in'], reduce_s=True, after=sent, name="in_proj_dx")
    dx1, dx1_b, dgm = _rms_bwd(x1, w['norm_mix'], dh2, dx2, "mix_dnorm")
    grads['norm_mix'] = dgm

    dx0, _, dg, dwg, dwu, dwd = _ffn_bwd(dx1, dx1_b, x, w['norm_ffn1'], w['ffn1_w_gate'], w['ffn1_w_up'],
                                         w['ffn1_w_down'], ffn1_saved, "ffn1")
    grads['norm_ffn1'] = dg
    put_grads('ffn1', dict(ffn1_w_gate=dwg, ffn1_w_up=dwu, ffn1_w_down=dwd))
    return loss, dx0, grads, wide_sum


HBM_SPEC = pl.BlockSpec(memory_space=pl.ANY)


def _chip_peers(x, y):
    return [(1 - x, y), (x, 1 - y), (1 - x, 1 - y)]


HBM_ONLY = pl.BlockSpec(memory_space=pltpu.HBM)
SEM_SPEC = pl.BlockSpec(memory_space=pltpu.SEMAPHORE)
EFFECT = pltpu.SideEffectType.DATAFLOW_SIDE_EFFECTING


def _place_own(src, slot, name):
    r, c = src.shape
    tr = r // 2

    def body(slot_ref, s_ref, o_ref):
        o_ref[0] = s_ref[...]

    return pl.pallas_call(
        body, out_shape=jax.ShapeDtypeStruct((N_CHIPS, r, c), src.dtype),
        grid_spec=pltpu.PrefetchScalarGridSpec(
            num_scalar_prefetch=1, grid=(2,), in_specs=[pl.BlockSpec((tr, c), lambda i, s: (i, 0))],
            out_specs=pl.BlockSpec((1, tr, c), lambda i, s: (s[0], i, 0))),
        compiler_params=_cparams(("parallel",)), name=name)(slot, src)


def _chip_copies(srcs, lands, send_sems, recv_sems, scatter, landed):
    x, y, c = lax.axis_index("x"), lax.axis_index("y"), lax.axis_index("c")
    me = 2 * x + y
    out = []
    for i in range(len(srcs)):
        for j, (px, py) in enumerate(_chip_peers(x, y)):
            p = 2 * px + py
            slot = p if landed else me
            if scatter:
                src, dst = srcs[i].at[p], lands[i].at[slot]
            else:
                rows = _core_half(srcs[i].shape[0], c)
                src, dst = srcs[i].at[rows], lands[i].at[slot, rows]
            out.append(pltpu.make_async_remote_copy(src, dst, send_sems.at[3 * i + j], recv_sems.at[3 * i + j],
                                                    device_id=(px, py, c), device_id_type=MESH))
    return out


def _core_half(nrows, c):
    half = nrows // 2
    return pl.ds(pl.multiple_of(c * half, 16), half)


def _sibling_forward(lands, name):
    n = len(lands)

    def body(*refs):
        bufs = refs[n:2 * n]
        send_sems, recv_sems = refs[2 * n:]
        x, y, c = lax.axis_index("x"), lax.axis_index("y"), lax.axis_index("c")
        mine = [_core_half(b.shape[1], c) for b in bufs]
        theirs = [_core_half(b.shape[1], 1 - c) for b in bufs]
        chips = [2 * px + py for px, py in _chip_peers(x, y)]
        cps = [pltpu.make_async_remote_copy(bufs[i].at[p, mine[i]], bufs[i].at[p, mine[i]], send_sems.at[3 * i + j],
                                            recv_sems.at[3 * i + j], device_id=(x, y, 1 - c), device_id_type=MESH)
               for i in range(n) for j, p in enumerate(chips)]
        for cp in cps:
            cp.start()
        for i in range(n):
            for j, p in enumerate(chips):
                pltpu.make_async_remote_copy(bufs[i].at[p, mine[i]], bufs[i].at[p, theirs[i]], send_sems.at[3 * i + j],
                                             recv_sems.at[3 * i + j], device_id=(x, y, 1 - c),
                                             device_id_type=MESH).wait()

    return pl.pallas_call(
        body, out_shape=[jax.ShapeDtypeStruct(a.shape, a.dtype) for a in lands],
        in_specs=[HBM_SPEC] * n, out_specs=[HBM_SPEC] * n, input_output_aliases={k: k for k in range(n)},
        scratch_shapes=[pltpu.SemaphoreType.DMA((3 * n,)), pltpu.SemaphoreType.DMA((3 * n,))],
        name=name)(*lands)


def _exchange_start(groups, scatter, name, after=None):
    sizes = [len(srcs) for srcs, _ in groups]
    flat_src = [a for srcs, _ in groups for a in srcs]
    flat_land = [a for _, lands in groups for a in lands]
    n = len(flat_src)
    ng = len(groups)

    def body(*refs):
        src_refs, land_refs = refs[:n], refs[n:2 * n]
        n_in = 2 * n + (after is not None)
        sems = refs[n_in:n_in + 2 * ng]
        token_ref = refs[-1]
        off = 0
        for gi, sz in enumerate(sizes):
            for cp in _chip_copies(src_refs[off:off + sz], land_refs[off:off + sz], sems[2 * gi], sems[2 * gi + 1],
                                   scatter, landed=False):
                cp.start()
            off += sz
        token_ref[...] = jnp.zeros_like(token_ref)

    sem_shapes = []
    for sz in sizes:
        sem_shapes += [pltpu.SemaphoreType.DMA((3 * sz,)), pltpu.SemaphoreType.DMA((3 * sz,))]
    hbm = lambda a: pltpu.HBM(a.shape, a.dtype)
    res = pl.pallas_call(
        body, name=name,
        out_shape=(tuple(sem_shapes) + tuple(hbm(a) for a in flat_src) + tuple(hbm(a) for a in flat_land)
                   + (jax.ShapeDtypeStruct((SUBLANES, LANES), F32),)),
        in_specs=[HBM_ONLY] * (2 * n) + [HBM_SPEC] * (after is not None),
        out_specs=tuple([SEM_SPEC] * (2 * ng) + [HBM_ONLY] * (2 * n) + [pl.BlockSpec(memory_space=pltpu.VMEM)]),
        input_output_aliases={k: 2 * ng + k for k in range(2 * n)},
        compiler_params=pltpu.CompilerParams(has_side_effects=EFFECT),
    )(*[pltpu.with_memory_space_constraint(a, pltpu.HBM) for a in flat_src + flat_land],
      *([after] if after is not None else []))
    sems, thru_src, thru_land = res[:2 * ng], res[2 * ng:2 * ng + n], res[2 * ng + n:2 * ng + 2 * n]
    out, off = [], 0
    for gi, sz in enumerate(sizes):
        out.append((sems[2 * gi], sems[2 * gi + 1], list(thru_src[off:off + sz]), list(thru_land[off:off + sz])))
        off += sz
    return out, res[-1]


def _exchange_wait(started, after, scatter, name):
    send_sems, recv_sems, srcs, lands = started
    n = len(srcs)

    def body(*refs):
        src_refs, land_refs = refs[:n], refs[n:2 * n]
        send_ref, recv_ref = refs[2 * n], refs[2 * n + 1]
        for cp in _chip_copies(src_refs, land_refs, send_ref, recv_ref, scatter, landed=True):
            cp.wait_send()
            cp.wait_recv()

    hbm = lambda a: pltpu.HBM(a.shape, a.dtype)
    res = pl.pallas_call(
        body, name=name, out_shape=tuple(hbm(a) for a in srcs) + tuple(hbm(a) for a in lands),
        in_specs=[HBM_ONLY] * (2 * n) + [SEM_SPEC, SEM_SPEC, HBM_SPEC], out_specs=tuple([HBM_ONLY] * (2 * n)),
        input_output_aliases={k: k for k in range(2 * n)},
        compiler_params=pltpu.CompilerParams(has_side_effects=EFFECT),
    )(*srcs, *lands, send_sems, recv_sems, after)
    return list(res[:n]), list(res[n:])


def _half_swap(parts, name):
    n = len(parts)

    def body(*refs):
        ins, outs = refs[:n], refs[n:2 * n]
        send_sems, recv_sems = refs[2 * n:]
        x, y, c = lax.axis_index("x"), lax.axis_index("y"), lax.axis_index("c")
        cps = [pltpu.make_async_remote_copy(ins[i].at[k, _core_half(ins[i].shape[1], 1 - c)], outs[i].at[k],
                                            send_sems.at[N_CHIPS * i + k], recv_sems.at[N_CHIPS * i + k],
                                            device_id=(x, y, 1 - c), device_id_type=MESH)
               for i in range(n) for k in range(N_CHIPS)]
        for cp in cps:
            cp.start()
        for cp in cps:
            cp.wait()

    return pl.pallas_call(
        body, out_shape=[jax.ShapeDtypeStruct((N_CHIPS, p.shape[1] // 2, p.shape[2]), p.dtype) for p in parts],
        in_specs=[HBM_SPEC] * n, out_specs=[HBM_SPEC] * n,
        scratch_shapes=[pltpu.SemaphoreType.DMA((N_CHIPS * n,)), pltpu.SemaphoreType.DMA((N_CHIPS * n,))],
        name=name)(*parts)


def _half_add(parts, sib, slots, name):
    na = len(parts)
    _, r, c = parts[0].shape
    hr = r // 2
    tr = _row_tile(hr, 512)
    nt = hr // tr

    def body(slot_ref, *refs):
        for a in range(na):
            refs[2 * na + a][...] = (refs[2 * a][...].astype(F32) + refs[2 * a + 1][...].astype(F32)).astype(BF16)

    mine = pl.BlockSpec((1, tr, c), lambda k, i, s: (k, i + s[4] * nt, 0))
    half = pl.BlockSpec((1, tr, c), lambda k, i, s: (k, i, 0))
    args = [a for p, sb in zip(parts, sib) for a in (p, sb)]
    return pl.pallas_call(
        body, out_shape=[jax.ShapeDtypeStruct((N_CHIPS, hr, c), BF16)] * na,
        grid_spec=pltpu.PrefetchScalarGridSpec(
            num_scalar_prefetch=1, grid=(N_CHIPS, nt), in_specs=[mine, half] * na, out_specs=[half] * na),
        compiler_params=_cparams(("parallel", "parallel")), name=name)(slots, *args)


def _half_forward(arrs, name):
    n = len(arrs)

    def body(*refs):
        bufs = refs[n:2 * n]
        send_sems, recv_sems = refs[2 * n:]
        x, y, c = lax.axis_index("x"), lax.axis_index("y"), lax.axis_index("c")
        cps = [pltpu.make_async_remote_copy(b.at[_core_half(b.shape[0], c)], b.at[_core_half(b.shape[0], c)],
                                            send_sems.at[i], recv_sems.at[i], device_id=(x, y, 1 - c),
                                            device_id_type=MESH) for i, b in enumerate(bufs)]
        for cp in cps:
            cp.start()
        for i, b in enumerate(bufs):
            pltpu.make_async_remote_copy(b.at[_core_half(b.shape[0], c)], b.at[_core_half(b.shape[0], 1 - c)],
                                         send_sems.at[i], recv_sems.at[i], device_id=(x, y, 1 - c),
                                         device_id_type=MESH).wait()

    return pl.pallas_call(
        body, out_shape=[jax.ShapeDtypeStruct(a.shape, a.dtype) for a in arrs],
        in_specs=[HBM_SPEC] * n, out_specs=[HBM_SPEC] * n, input_output_aliases={k: k for k in range(n)},
        scratch_shapes=[pltpu.SemaphoreType.DMA((n,)), pltpu.SemaphoreType.DMA((n,))],
        name=name)(*arrs)


def _small_exchange(smalls, name):
    nsm = len(smalls)
    rels = [(fx, fy, fc) for fx in (0, 1) for fy in (0, 1) for fc in (0, 1)][1:]

    def body(*refs):
        sins, souts = refs[:nsm], refs[nsm:2 * nsm]
        ssend, srecv, slocal = refs[2 * nsm:]
        x, y, c = lax.axis_index("x"), lax.axis_index("y"), lax.axis_index("c")
        lin = 4 * x + 2 * y + c
        local = [pltpu.make_async_copy(sins[i], souts[i].at[lin], slocal.at[i]) for i in range(nsm)]
        for cp in local:
            cp.start()
        for i in range(nsm):
            for j, (fx, fy, fc) in enumerate(rels):
                pltpu.make_async_remote_copy(sins[i], souts[i].at[lin], ssend.at[i, j], srecv.at[i, j],
                                             device_id=(x ^ fx, y ^ fy, c ^ fc), device_id_type=MESH).start()
        for i in range(nsm):
            for j, (fx, fy, fc) in enumerate(rels):
                src = 4 * (x ^ fx) + 2 * (y ^ fy) + (c ^ fc)
                pltpu.make_async_remote_copy(sins[i], souts[i].at[src], ssend.at[i, j], srecv.at[i, j],
                                             device_id=(x ^ fx, y ^ fy, c ^ fc), device_id_type=MESH).wait()
        for cp in local:
            cp.wait()

    return pl.pallas_call(
        body, out_shape=[jax.ShapeDtypeStruct((N_DEV,) + s.shape, s.dtype) for s in smalls],
        in_specs=[HBM_SPEC] * nsm, out_specs=[HBM_SPEC] * nsm,
        scratch_shapes=[pltpu.SemaphoreType.DMA((nsm, 7)), pltpu.SemaphoreType.DMA((nsm, 7)),
                        pltpu.SemaphoreType.DMA((nsm,))],
        name=name)(*smalls)


def _sum_parts(parts, recv, slots, name):
    na = len(parts)
    _, r, c = parts[0].shape
    tr = _row_tile(r, 192)

    def body(slot_ref, *refs):
        for a in range(na):
            own_ref, r0_ref, r1_ref, r2_ref = refs[4 * a:4 * a + 4]
            refs[4 * na + a][...] = ((own_ref[0].astype(F32) + r0_ref[0].astype(F32))
                                     + (r1_ref[0].astype(F32) + r2_ref[0].astype(F32)))

    blk = lambda k: pl.BlockSpec((1, tr, c), lambda i, s, k=k: (s[k], i, 0))
    out_blk = pl.BlockSpec((tr, c), lambda i, s: (i + s[4] * (r // tr), 0))
    args = [a for p, rv in zip(parts, recv) for a in (p, rv, rv, rv)]
    return pl.pallas_call(
        body, out_shape=[jax.ShapeDtypeStruct((2 * r, c), F32)] * na,
        grid_spec=pltpu.PrefetchScalarGridSpec(
            num_scalar_prefetch=1, grid=(r // tr,), in_specs=[blk(0), blk(1), blk(2), blk(3)] * na,
            out_specs=[out_blk] * na),
        compiler_params=_cparams(("parallel",)), name=name)(slots, *args)


def _small_allreduce(packed, name):
    rows = packed.shape[0]
    pr = rows // N_DEV
    rels = [(fx, fy, fc) for fx in (0, 1) for fy in (0, 1) for fc in (0, 1)][1:]

    def body(in_ref, out_ref, recv_ref, send1, recv1, send2, recv2):
        x, y, c = lax.axis_index("x"), lax.axis_index("y"), lax.axis_index("c")
        lin = 4 * x + 2 * y + c
        piece = lambda ref, k: ref.at[pl.ds(pl.multiple_of(k * pr, pr), pr), :]
        peers = [((x ^ fx, y ^ fy, c ^ fc), 4 * (x ^ fx) + 2 * (y ^ fy) + (c ^ fc)) for fx, fy, fc in rels]
        for j, (dev, plin) in enumerate(peers):
            pltpu.make_async_remote_copy(piece(in_ref, plin), recv_ref.at[lin], send1.at[j], recv1.at[j],
                                         device_id=dev, device_id_type=MESH).start()
        recv_ref[lin] = piece(in_ref, lin)[...]
        for j, (dev, plin) in enumerate(peers):
            pltpu.make_async_remote_copy(piece(in_ref, plin), recv_ref.at[plin], send1.at[j], recv1.at[j],
                                         device_id=dev, device_id_type=MESH).wait()
        acc = recv_ref[0]
        for k in range(1, N_DEV):
            acc = acc + recv_ref[k]
        piece(out_ref, lin)[...] = acc
        for j, (dev, plin) in enumerate(peers):
            pltpu.make_async_remote_copy(piece(out_ref, lin), piece(out_ref, lin), send2.at[j], recv2.at[j],
                                         device_id=dev, device_id_type=MESH).start()
        for j, (dev, plin) in enumerate(peers):
            pltpu.make_async_remote_copy(piece(out_ref, lin), piece(out_ref, plin), send2.at[j], recv2.at[j],
                                         device_id=dev, device_id_type=MESH).wait()

    vm = pl.BlockSpec(memory_space=pltpu.VMEM)
    return pl.pallas_call(
        body, out_shape=jax.ShapeDtypeStruct(packed.shape, F32), in_specs=[vm], out_specs=vm,
        scratch_shapes=[pltpu.VMEM((N_DEV, pr, LANES), F32)] + [pltpu.SemaphoreType.DMA((7,))] * 4,
        compiler_params=pltpu.CompilerParams(vmem_limit_bytes=VMEM_LIMIT), name=name)(packed)


def _adamw_math(w, m, v, g):
    nm = ADAM_B1 * m + (1.0 - ADAM_B1) * g
    nv = ADAM_B2 * v + (1.0 - ADAM_B2) * (g * g)
    m_hat = nm * (1.0 / (1.0 - ADAM_B1 ** ADAM_STEP))
    v_hat = nv * (1.0 / (1.0 - ADAM_B2 ** ADAM_STEP))
    return -ADAM_LR * (m_hat / (jnp.sqrt(v_hat) + ADAM_EPS) + ADAM_WD * w), nm, nv


def _adamw(ws, ms, vs, gs, name):
    na = len(ws)
    r, c = ws[0].shape
    tr = _row_tile(r)

    def body(*refs):
        for a in range(na):
            w_ref, m_ref, v_ref, g_ref = refs[4 * a:4 * a + 4]
            d_ref, nm_ref, nv_ref = refs[4 * na + 3 * a:4 * na + 3 * a + 3]
            d_ref[...], nm_ref[...], nv_ref[...] = _adamw_math(w_ref[...], m_ref[...], v_ref[...], g_ref[...])

    blk = pl.BlockSpec((tr, c), lambda i: (i, 0))
    sh = jax.ShapeDtypeStruct((r, c), F32)
    args = [a for group in zip(ws, ms, vs, gs) for a in group]
    res = pl.pallas_call(body, out_shape=[sh] * (3 * na), grid=(r // tr,), in_specs=[blk] * (4 * na),
                         out_specs=[blk] * (3 * na), compiler_params=_cparams(("parallel",)), name=name)(*args)
    return [tuple(res[3 * a:3 * a + 3]) for a in range(na)]


def _adamw_small(ws, ms, vs, alls, split, name):
    n = len(ws)
    lead = split if split is not None else ()
    nl = len(lead)
    nslots = alls[0].shape[0]

    def blocks(shape):
        if split is None:
            return tuple(shape), (lambda *g: (0,) * len(shape))
        blk = (shape[0], shape[1] // lead[0], shape[2] // lead[1]) + tuple(shape[3:])
        return blk, (lambda *g: (0, g[0], g[1]) + (0,) * (len(shape) - 3))

    def body(*refs):
        w_refs, m_refs, v_refs, a_refs = (refs[k * n:(k + 1) * n] for k in range(4))
        g_refs, d_refs, nm_refs, nv_refs = (refs[(4 + k) * n:(5 + k) * n] for k in range(4))
        k = pl.program_id(nl)
        for i in range(n):
            @pl.when(k == 0)
            def _(i=i):
                g_refs[i][...] = a_refs[i][0]

            @pl.when(k > 0)
            def _(i=i):
                g_refs[i][...] += a_refs[i][0]

            @pl.when(k == nslots - 1)
            def _(i=i):
                d_refs[i][...], nm_refs[i][...], nv_refs[i][...] = _adamw_math(
                    w_refs[i][...], m_refs[i][...], v_refs[i][...], g_refs[i][...])

    specs, aspecs, shapes = [], [], []
    for wa in ws:
        blk, imap = blocks(wa.shape)
        specs.append(pl.BlockSpec(blk, imap))
        aspecs.append(pl.BlockSpec((1,) + blk, (lambda *g, imap=imap: (g[nl],) + imap(*g))))
        shapes.append(jax.ShapeDtypeStruct(wa.shape, F32))
    res = pl.pallas_call(
        body, out_shape=shapes * 4, grid=tuple(lead) + (nslots,), in_specs=specs * 3 + aspecs,
        out_specs=specs * 4, compiler_params=_cparams(("parallel",) * nl + ("arbitrary",)),
        name=name)(*ws, *ms, *vs, *alls)
    return res[:n], res[n:2 * n], res[2 * n:3 * n], res[3 * n:]


def kernel(x, norm_ffn1, ffn1_w_gate, ffn1_w_up, ffn1_w_down, norm_mix, w_in, attn_sinks, ssm_lambda_re, ssm_lambda_im, ssm_log_dt, ssm_b_re, ssm_b_im, ssm_c_re, ssm_c_im, ssm_d, ssm_glu_w, ssm_glu_b, attn_out_norm, ssm_out_norm, w_out, norm_ffn2, ffn2_w_gate, ffn2_w_up, ffn2_w_down, final_norm, loss_target, m_norm_ffn1, m_ffn1_w_gate, m_ffn1_w_up, m_ffn1_w_down, m_norm_mix, m_w_in, m_attn_sinks, m_ssm_lambda_re, m_ssm_lambda_im, m_ssm_log_dt, m_ssm_b_re, m_ssm_b_im, m_ssm_c_re, m_ssm_c_im, m_ssm_d, m_ssm_glu_w, m_ssm_glu_b, m_attn_out_norm, m_ssm_out_norm, m_w_out, m_norm_ffn2, m_ffn2_w_gate, m_ffn2_w_up, m_ffn2_w_down, m_final_norm, v_norm_ffn1, v_ffn1_w_gate, v_ffn1_w_up, v_ffn1_w_down, v_norm_mix, v_w_in, v_attn_sinks, v_ssm_lambda_re, v_ssm_lambda_im, v_ssm_log_dt, v_ssm_b_re, v_ssm_b_im, v_ssm_c_re, v_ssm_c_im, v_ssm_d, v_ssm_glu_w, v_ssm_glu_b, v_attn_out_norm, v_ssm_out_norm, v_w_out, v_norm_ffn2, v_ffn2_w_gate, v_ffn2_w_up, v_ffn2_w_down, v_final_norm):
    given = dict(locals())
    wts = {n: given[n] for n in WEIGHTS}

    order = [g for g in GROUPS]
    cx, cy = lax.axis_index("x"), lax.axis_index("y")
    slots = jnp.stack([2 * cx + cy, 2 * (1 - cx) + cy, 2 * cx + 1 - cy, 2 * (1 - cx) + 1 - cy,
                       lax.axis_index("c")]).astype(jnp.int32)
    def view(a, n):
        if n in TRANSPOSED:
            return jnp.swapaxes(a[0], 0, 1)
        if n in BIG:
            return a[0]
        if n in ('ssm_b_re', 'ssm_b_im'):
            return jnp.swapaxes(a, -1, -2)
        return a.reshape(1, -1) if a.ndim == 1 else a

    def unview(a, n):
        if n in TRANSPOSED:
            return jnp.swapaxes(a, 0, 1)[None]
        if n in ('ssm_b_re', 'ssm_b_im'):
            return jnp.swapaxes(a, -1, -2)
        return a.reshape(wts[n].shape)

    started, gather_token = {}, None
    for g in order:
        shards = [view(wts[n], n).astype(BF16) for n in GROUPS[g]]
        placed = [_place_own(s, slots, f"weights_place_{n}") for n, s in zip(GROUPS[g], shards)]
        st, gather_token = _exchange_start([(shards, placed)], False, f"weights_start_{g}", after=gather_token)
        started[g] = st[0]

    def get_weights(group, after):
        if group == order[0]:
            after = after + gather_token[:1, :1]
        _, lands = _exchange_wait(started[group], after, False, f"weights_wait_{group}")
        lands = _sibling_forward(lands, f"weights_forward_{group}")
        out = dict(zip(GROUPS[group], lands))
        for n in ('w_in', 'ssm_glu_w', 'w_out'):
            if n in out:
                out[n] = out[n].reshape(-1, out[n].shape[-1])
        return out

    sent, tokens = {}, {}

    def put_grads(group, gd):
        parts = []
        for n in GROUPS[group]:
            g = gd[n]
            if g.ndim == 2:
                g = g.reshape(N_CHIPS, g.shape[0] // N_CHIPS, g.shape[1])
            parts.append(g.astype(BF16))
        sib = _half_swap(parts, f"grads_half_swap_{group}")
        same = len({p.shape for p in parts}) == 1
        batches = [list(range(len(parts)))] if same else [[i] for i in range(len(parts))]
        halves = [None] * len(parts)
        for b in batches:
            res = _half_add([parts[i] for i in b], [sib[i] for i in b], slots, f"grads_half_add_{GROUPS[group][b[0]]}")
            for i, h in zip(b, res):
                halves[i] = h
        parts = halves
        lands = [lax.empty(p.shape, p.dtype) for p in parts]
        started_g, tokens[group] = _exchange_start([(parts, lands)], True, f"grads_start_{group}")
        sent[group] = started_g[0]
        return tokens[group]

    w = {n: (wts[n][0] if wts[n].ndim > 1 else wts[n]) for n in SMALL}
    w['norm_ffn1'], w['norm_mix'], w['norm_ffn2'] = wts['norm_ffn1'], wts['norm_mix'], wts['norm_ffn2']
    w['ssm_b_re'], w['ssm_b_im'] = view(wts['ssm_b_re'], 'ssm_b_re')[0], view(wts['ssm_b_im'], 'ssm_b_im')[0]
    w['ssm_log_dt'] = w['ssm_log_dt'] + gather_token[0, 0]
    wide =['ssm_b_re', 'ssm_b_im', 'ssm_c_re', 'ssm_c_im']

    def reduce_wide(gd):
        packed = jnp.concatenate([gd[n].reshape(-1, LANES) for n in wide])
        return _small_allreduce(packed, "small_grads_allreduce")

    loss_row, dx, grads, wide_sum = _local_step(x[0], loss_target[0], w, get_weights, put_grads, reduce_wide)

    out_g, out_d, out_m, out_v = {}, {}, {}, {}

    def finish(group, after):
        names = GROUPS[group]
        parts, recv = _exchange_wait(sent[group], after, True, f"grads_wait_{group}")
        same = len({p.shape for p in parts}) == 1
        batches = [list(range(len(names)))] if same else [[i] for i in range(len(names))]
        sums = [None] * len(names)
        for b in batches:
            res = _sum_parts([parts[i] for i in b], [recv[i] for i in b], slots, f"grad_sum_{names[b[0]]}")
            for i, sm in zip(b, res):
                sums[i] = sm
        full = _half_forward(sums, f"grad_half_forward_{group}")
        for b in batches:
            res = _adamw([view(wts[names[i]], names[i]) for i in b], [view(given['m_' + names[i]], names[i]) for i in b],
                         [view(given['v_' + names[i]], names[i]) for i in b], [full[i] for i in b],
                         f"adamw_{names[b[0]]}")
            for i, (d, nm, nv) in zip(b, res):
                n = names[i]
                out_g[n], out_d[n], out_m[n], out_v[n] = (unview(a, n) for a in (full[i], d, nm, nv))
        return nv

    done = finish('ffn2', tokens['ffn1'])
    done = finish('mix', done)

    nat = {n: view(wts[n], n).shape for n in SMALL}
    narrow = [n for n in SMALL if n not in wide]
    alls = list(_small_exchange([grads[n].reshape(nat[n]) for n in narrow] + [loss_row], "small_grads_allgather"))
    loss = jnp.sum(alls.pop()[:, 0, 0])
    rows = wide_sum.shape[0] // len(wide)
    wide_g = [wide_sum[i * rows:(i + 1) * rows].reshape((1,) + nat[n]) for i, n in enumerate(wide)]
    for group, gs, split, tag in ((narrow, alls, None, "adamw_small"), (wide, wide_g, (2, 4), "adamw_ssm_bc")):
        res = _adamw_small([view(wts[n], n) for n in group], [view(given['m_' + n], n) for n in group],
                           [view(given['v_' + n], n) for n in group], gs, split, tag)
        for dst, vals in zip((out_g, out_d, out_m, out_v), res):
            for n, a in zip(group, vals):
                dst[n] = unview(a, n)

    finish('ffn1', out_v['norm_ffn1'][:, :1] + out_v['ssm_c_re'].reshape(1, -1)[:, :1] + done[:1, :1] + loss)

    return (loss, dx[None], *[out_g[n] for n in WEIGHTS], *[out_d[n] for n in WEIGHTS],
            *[out_m[n] for n in WEIGHTS], *[out_v[n] for n in WEIGHTS])
```

```python
import functools
import math

import numpy as np
import jax
import jax.numpy as jnp
from jax import lax
from jax.experimental import pallas as pl
from jax.experimental.pallas import tpu as pltpu

F32 = jnp.float32
BF16 = jnp.bfloat16
MESH = pl.DeviceIdType.MESH

EPS = 1e-6
NEG_INF = -1e30
LAMBDA_RE_MAX = -1e-4
ATTN_HEADS = 8
KV_HEADS = 2
GQ = ATTN_HEADS // KV_HEADS
HEAD_DIM = 64
ATTN_WIDTH = 512
KV_WIDTH = 128
WINDOW = 128
QBLOCK = 128
SSM_WIDTH = 512
SSM_GROUPS = 32
SSM_CH = 16
SSM_STATE = 64
N_STRIPS = 4
STRIP_IN = SSM_WIDTH // N_STRIPS
STRIP_ST = SSM_GROUPS * SSM_STATE // N_STRIPS
SUBLANES = 8
LANES = 128
N_CHIPS = 4
N_DEV = 8

ADAM_LR = 0.001
ADAM_B1 = 0.9
ADAM_B2 = 0.999
ADAM_EPS = 1e-08
ADAM_WD = 0.01
ADAM_STEP = 10

VMEM_LIMIT = 48 * 1024 * 1024

WEIGHTS = ['norm_ffn1', 'ffn1_w_gate', 'ffn1_w_up', 'ffn1_w_down', 'norm_mix', 'w_in', 'attn_sinks',
           'ssm_lambda_re', 'ssm_lambda_im', 'ssm_log_dt', 'ssm_b_re', 'ssm_b_im', 'ssm_c_re', 'ssm_c_im',
           'ssm_d', 'ssm_glu_w', 'ssm_glu_b', 'attn_out_norm', 'ssm_out_norm', 'w_out', 'norm_ffn2',
           'ffn2_w_gate', 'ffn2_w_up', 'ffn2_w_down', 'final_norm']
BIG = ['ffn1_w_gate', 'ffn1_w_up', 'ffn1_w_down', 'w_in', 'ssm_glu_w', 'w_out',
       'ffn2_w_gate', 'ffn2_w_up', 'ffn2_w_down']
SMALL = [n for n in WEIGHTS if n not in BIG]
TRANSPOSED = ['ffn1_w_gate', 'ffn1_w_up', 'w_in', 'ffn2_w_gate', 'ffn2_w_up']
GROUPS = {'ffn1': ['ffn1_w_gate', 'ffn1_w_up', 'ffn1_w_down'],
          'mix': ['w_in', 'ssm_glu_w', 'w_out'],
          'ffn2': ['ffn2_w_gate', 'ffn2_w_up', 'ffn2_w_down']}


def _cparams(sem=None):
    return pltpu.CompilerParams(dimension_semantics=sem, vmem_limit_bytes=VMEM_LIMIT)


def _tile(n, pref):
    if n <= pref:
        return n
    for t in (pref, pref // 2, pref // 4):
        if t % LANES == 0 and n % t == 0:
            return t
    return n


def _sigmoid(x):
    return 1.0 / (1.0 + jnp.exp(-x))


def _sigmoid_tanh(x):
    return 0.5 * jnp.tanh(0.5 * x) + 0.5


def _mm(a, b, *, ta=False, tb=False, reduce_s=False, res=None, scale=1.0, out_dtype=F32, after=None, name):
    a3 = a if a.ndim == 3 else a[None]
    b3 = b if b.ndim == 3 else b[None]
    sa, sb = a3.shape[0], b3.shape[0]
    ns = max(sa, sb)
    (kk, m) = a3.shape[1:] if ta else a3.shape[1:][::-1]
    (n, kb) = b3.shape[1:] if tb else b3.shape[1:][::-1]
    assert kk == kb, (a3.shape, b3.shape)
    tm, tn, tk = _tile(m, 1024), _tile(n, 1024), _tile(kk, 2048)
    nm, nn, nk = m // tm, n // tn, kk // tk
    has_res = res is not None
    single = nk == 1 and not (reduce_s and ns > 1)

    if reduce_s:
        grid = (nm, nn, ns, nk)
        ids = lambda i, j, s, k: (s, i, j, k)
        sem = ("parallel", "parallel", "arbitrary", "arbitrary")
    else:
        grid = (ns, nm, nn, nk)
        ids = lambda s, i, j, k: (s, i, j, k)
        sem = ("parallel", "parallel", "parallel", "arbitrary")

    def a_map(*g):
        s, i, j, k = ids(*g)
        s = s if sa > 1 else 0
        return (s, k, i) if ta else (s, i, k)

    def b_map(*g):
        s, i, j, k = ids(*g)
        s = s if sb > 1 else 0
        return (s, j, k) if tb else (s, k, j)

    def o_map(*g):
        s, i, j, k = ids(*g)
        return (i, j) if reduce_s else (s, i, j)

    a_blk = (1, tk, tm) if ta else (1, tm, tk)
    b_blk = (1, tn, tk) if tb else (1, tk, tn)
    dims = (((0 if ta else 1,), (1 if tb else 0,)), ((), ()))

    def body(*refs):
        a_ref, b_ref = refs[0], refs[1]
        r_ref = refs[2] if has_res else None
        o_ref = refs[2 + has_res + (after is not None)]
        acc_ref = None if single else refs[-1]
        s, _, _, k = ids(*[pl.program_id(d) for d in range(4)])
        prod = lax.dot_general(a_ref[0].astype(BF16), b_ref[0].astype(BF16), dims, preferred_element_type=F32)

        def finish(out):
            if scale != 1.0:
                out = out * scale
            if has_res:
                out = r_ref[...].reshape(out.shape) + out
            o_ref[...] = out.astype(out_dtype).reshape(o_ref.shape)

        if single:
            finish(prod)
            return
        if reduce_s:
            first = jnp.logical_and(s == 0, k == 0)
            last = jnp.logical_and(s == ns - 1, k == nk - 1)
        else:
            first, last = k == 0, k == nk - 1

        acc_ref[...] = prod + jnp.where(first, 0.0, acc_ref[...])

        @pl.when(last)
        def _():
            finish(acc_ref[...])

    in_specs = [pl.BlockSpec(a_blk, a_map), pl.BlockSpec(b_blk, b_map)]
    args = [a3, b3]
    if reduce_s:
        out_shape = jax.ShapeDtypeStruct((m, n), out_dtype)
        o_spec = pl.BlockSpec((tm, tn), o_map)
    else:
        out_shape = jax.ShapeDtypeStruct((ns, m, n), out_dtype)
        o_spec = pl.BlockSpec((1, tm, tn), o_map)
    if has_res:
        assert res.shape == out_shape.shape
        in_specs.append(o_spec)
        args.append(res)
    if after is not None:
        in_specs.append(HBM_SPEC)
        args.append(after)
    return pl.pallas_call(body, out_shape=out_shape, grid=grid, in_specs=in_specs, out_specs=o_spec,
                          scratch_shapes=[] if single else [pltpu.VMEM((tm, tn), F32)],
                          compiler_params=_cparams(sem), name=name)(*args)


def _row_tile(t, cap=256):
    for step in (16, SUBLANES):
        for tr in range(min(cap, t) // step * step, 0, -step):
            if t % tr == 0:
                return tr
    return t


def _rms_fwd(x, g, name):
    t, w = x.shape
    tr = _row_tile(t)

    def body(x_ref, g_ref, o_ref):
        xv = x_ref[...]
        r = lax.rsqrt(jnp.mean(xv * xv, axis=-1, keepdims=True) + EPS)
        o_ref[...] = (xv * r * g_ref[...]).astype(BF16)

    return pl.pallas_call(
        body, out_shape=jax.ShapeDtypeStruct((t, w), BF16), grid=(t // tr,),
        in_specs=[pl.BlockSpec((tr, w), lambda i: (i, 0)), pl.BlockSpec((1, w), lambda i: (0, 0))],
        out_specs=pl.BlockSpec((tr, w), lambda i: (i, 0)), compiler_params=_cparams(("parallel",)),
        name=name)(x, g)


def _rms_bwd_rows(xv, gv, dhv):
    r = lax.rsqrt(jnp.mean(xv * xv, axis=-1, keepdims=True) + EPS)
    nrm = xv * r
    dn = dhv * gv
    return r * (dn - nrm * jnp.mean(dn * nrm, axis=-1, keepdims=True)), dhv * nrm


def _rms_bwd(x, g, dh, dres, name):
    t, w = x.shape
    tr = _row_tile(t)
    has_res = dres is not None

    def body(*refs):
        if has_res:
            x_ref, g_ref, dh_ref, dr_ref, dx_ref, dxb_ref, dg_ref = refs
        else:
            x_ref, g_ref, dh_ref, dx_ref, dxb_ref, dg_ref = refs
        dx, dgs = _rms_bwd_rows(x_ref[...], g_ref[...], dh_ref[...])
        if has_res:
            dx = dx + dr_ref[...]
        dx_ref[...] = dx
        dxb_ref[...] = dx.astype(BF16)

        @pl.when(pl.program_id(0) == 0)
        def _():
            dg_ref[...] = jnp.zeros_like(dg_ref)

        dg_ref[...] += jnp.sum(dgs, axis=0, keepdims=True)

    row = pl.BlockSpec((tr, w), lambda i: (i, 0))
    vec = pl.BlockSpec((1, w), lambda i: (0, 0))
    ins = [x, g, dh] + ([dres] if has_res else [])
    return pl.pallas_call(
        body, out_shape=(jax.ShapeDtypeStruct((t, w), F32), jax.ShapeDtypeStruct((t, w), BF16),
                         jax.ShapeDtypeStruct((1, w), F32)),
        grid=(t // tr,), in_specs=[row, vec, row] + ([row] if has_res else []),
        out_specs=(row, row, vec), compiler_params=_cparams(("arbitrary",)), name=name)(*ins)


FFN_ROWS = 512
FFN_SPLIT = 2
FFN_W_ROWS = 1024
SCAN_ROWS = 256


NT_DIMS = (((1,), (1,)), ((), ()))
TN_DIMS = (((0,), (0,)), ((), ()))


def _ffn_fwd_call(x, g, wg, wu, wd, name):
    t, d = x.shape
    ns, f, _ = wg.shape
    tm = _tile(t, FFN_ROWS)

    def body(x_ref, g_ref, wg_ref, wu_ref, wd_ref, xo_ref, h_ref, gate_ref, up_ref, h_sc, acc_ref):
        s = pl.program_id(1)

        @pl.when(s == 0)
        def _():
            xv = x_ref[...]
            r = lax.rsqrt(jnp.mean(xv * xv, axis=-1, keepdims=True) + EPS)
            hb = (xv * r * g_ref[...]).astype(BF16)
            h_sc[...] = hb
            h_ref[...] = hb

        for r0 in range(0, tm, tm // FFN_SPLIT):
            rows = slice(r0, r0 + tm // FFN_SPLIT)
            hb = h_sc[rows, :]
            gate = lax.dot_general(hb, wg_ref[0], NT_DIMS, preferred_element_type=F32)
            up = lax.dot_general(hb, wu_ref[0], NT_DIMS, preferred_element_type=F32)
            gate_ref[0, rows, :] = gate.astype(BF16)
            up_ref[0, rows, :] = up.astype(BF16)
            act = (gate * _sigmoid_tanh(gate) * up).astype(BF16)
            prod = jnp.dot(act, wd_ref[0], preferred_element_type=F32)
            acc_ref[rows, :] = prod + jnp.where(s > 0, acc_ref[rows, :], 0.0)

        @pl.when(s == ns - 1)
        def _():
            xo_ref[...] = x_ref[...] + 0.5 * acc_ref[...]

    row = pl.BlockSpec((tm, d), lambda i, s: (i, 0))
    vec = pl.BlockSpec((1, d), lambda i, s: (0, 0))
    wrow = pl.BlockSpec((1, f, d), lambda i, s: (s, 0, 0))
    hid = pl.BlockSpec((1, tm, f), lambda i, s: (s, i, 0))
    hid_sh = jax.ShapeDtypeStruct((ns, t, f), BF16)
    return pl.pallas_call(
        body, out_shape=(jax.ShapeDtypeStruct((t, d), F32), jax.ShapeDtypeStruct((t, d), BF16), hid_sh, hid_sh),
        grid=(t // tm, ns), in_specs=[row, vec, wrow, wrow, wrow], out_specs=(row, row, hid, hid),
        scratch_shapes=[pltpu.VMEM((tm, d), BF16), pltpu.VMEM((tm, d), F32)],
        compiler_params=_cparams(("parallel", "arbitrary")), name=name)(x, g, wg, wu, wd)


def _ffn_bwd_x_call(dxo, dxo_b, x, g, gate, up, wg, wu, wd, name):
    t, d = x.shape
    ns, f, _ = wg.shape
    tm = _tile(t, FFN_ROWS)

    def body(dxo_ref, dxb_ref, x_ref, g_ref, gate_ref, up_ref, wg_ref, wu_ref, wd_ref,
             dx_ref, dxob_ref, dgn_ref, dgate_ref, dup_ref, act_ref, dh_ref):
        i, s = pl.program_id(0), pl.program_id(1)
        for r0 in range(0, tm, tm // FFN_SPLIT):
            rows = slice(r0, r0 + tm // FFN_SPLIT)
            dact = lax.dot_general(dxb_ref[rows, :], wd_ref[0], NT_DIMS, preferred_element_type=F32) * 0.5
            gv = gate_ref[0, rows, :].astype(F32)
            uv = up_ref[0, rows, :].astype(F32)
            sg = _sigmoid_tanh(gv)
            silu = gv * sg
            act_ref[0, rows, :] = (silu * uv).astype(BF16)
            dub = (dact * silu).astype(BF16)
            dgb = (dact * uv * sg * (1.0 + gv * (1.0 - sg))).astype(BF16)
            dup_ref[0, rows, :] = dub
            dgate_ref[0, rows, :] = dgb
            prod = (jnp.dot(dgb, wg_ref[0], preferred_element_type=F32)
                    + jnp.dot(dub, wu_ref[0], preferred_element_type=F32))

            dh_ref[rows, :] = prod + jnp.where(s > 0, dh_ref[rows, :], 0.0)

        @pl.when(jnp.logical_and(i == 0, s == 0))
        def _():
            dgn_ref[...] = jnp.zeros_like(dgn_ref)

        @pl.when(s == ns - 1)
        def _():
            dx, dgs = _rms_bwd_rows(x_ref[...], g_ref[...], dh_ref[...])
            dx = dx + dxo_ref[...]
            dx_ref[...] = dx
            dxob_ref[...] = dx.astype(BF16)
            dgn_ref[...] += jnp.sum(dgs, axis=0, keepdims=True)

    row = pl.BlockSpec((tm, d), lambda i, s: (i, 0))
    vec = pl.BlockSpec((1, d), lambda i, s: (0, 0))
    wrow = pl.BlockSpec((1, f, d), lambda i, s: (s, 0, 0))
    hid = pl.BlockSpec((1, tm, f), lambda i, s: (s, i, 0))
    hid_sh = jax.ShapeDtypeStruct((ns, t, f), BF16)
    return pl.pallas_call(
        body,
        out_shape=(jax.ShapeDtypeStruct((t, d), F32), jax.ShapeDtypeStruct((t, d), BF16),
                   jax.ShapeDtypeStruct((1, d), F32), hid_sh, hid_sh, hid_sh),
        grid=(t // tm, ns), in_specs=[row, row, row, vec, hid, hid, wrow, wrow, wrow],
        out_specs=(row, row, vec, hid, hid, hid), scratch_shapes=[pltpu.VMEM((tm, d), F32)],
        compiler_params=_cparams(("arbitrary", "arbitrary")), name=name)(dxo, dxo_b, x, g, gate, up, wg, wu, wd)


def _ffn_bwd_w_call(h, dxo_b, dgate, dup, act, name):
    t, d = h.shape
    ns, _, f = dgate.shape
    tm = _tile(t, FFN_W_ROWS)
    nm = t // tm

    def body(h_ref, dxb_ref, dgate_ref, dup_ref, act_ref, dwg_ref, dwu_ref, dwd_ref, ag_ref, au_ref, ad_ref):
        i = pl.program_id(1)
        hv = h_ref[...]
        pg = lax.dot_general(dgate_ref[0], hv, TN_DIMS, preferred_element_type=F32)
        pu = lax.dot_general(dup_ref[0], hv, TN_DIMS, preferred_element_type=F32)
        pd = lax.dot_general(act_ref[0], dxb_ref[...], TN_DIMS, preferred_element_type=F32)

        ag_ref[...] = pg + jnp.where(i > 0, ag_ref[...], 0.0)
        au_ref[...] = pu + jnp.where(i > 0, au_ref[...], 0.0)
        ad_ref[...] = pd + jnp.where(i > 0, ad_ref[...], 0.0)

        @pl.when(i == nm - 1)
        def _():
            dwg_ref[0] = ag_ref[...].astype(BF16)
            dwu_ref[0] = au_ref[...].astype(BF16)
            dwd_ref[0] = (0.5 * ad_ref[...]).astype(BF16)

    row = pl.BlockSpec((tm, d), lambda s, i: (i, 0))
    hid = pl.BlockSpec((1, tm, f), lambda s, i: (s, i, 0))
    wrow = pl.BlockSpec((1, f, d), lambda s, i: (s, 0, 0))
    wsh = jax.ShapeDtypeStruct((ns, f, d), BF16)
    return pl.pallas_call(
        body, out_shape=(wsh, wsh, wsh),
        grid=(ns, nm), in_specs=[row, row, hid, hid, hid], out_specs=(wrow, wrow, wrow),
        scratch_shapes=[pltpu.VMEM((f, d), F32), pltpu.VMEM((f, d), F32), pltpu.VMEM((f, d), F32)],
        compiler_params=_cparams(("parallel", "arbitrary")), name=name)(h, dxo_b, dgate, dup, act)


def _loss_head(x, g, tgt, name):
    t, w = x.shape
    tr = _row_tile(t)

    def body(x_ref, g_ref, t_ref, loss_ref, dx_ref, dxb_ref, dg_ref):
        xv = x_ref[...]
        gv = g_ref[...]
        r = lax.rsqrt(jnp.mean(xv * xv, axis=-1, keepdims=True) + EPS)
        nrm = xv * r
        err = nrm * gv - t_ref[...]
        dout = err * (1.0 / w)
        dn = dout * gv
        dx = r * (dn - nrm * jnp.mean(dn * nrm, axis=-1, keepdims=True))
        dx_ref[...] = dx
        dxb_ref[...] = dx.astype(BF16)

        @pl.when(pl.program_id(0) == 0)
        def _():
            dg_ref[...] = jnp.zeros_like(dg_ref)
            loss_ref[...] = jnp.zeros_like(loss_ref)

        dg_ref[...] += jnp.sum(dout * nrm, axis=0, keepdims=True)
        part = jnp.sum(jnp.sum(err * err, axis=-1, keepdims=True) * (0.5 / w), axis=0, keepdims=True)
        loss_ref[...] += jnp.broadcast_to(part, loss_ref.shape)

    row = pl.BlockSpec((tr, w), lambda i: (i, 0))
    vec = pl.BlockSpec((1, w), lambda i: (0, 0))
    return pl.pallas_call(
        body, out_shape=(jax.ShapeDtypeStruct((1, LANES), F32), jax.ShapeDtypeStruct((t, w), F32),
                         jax.ShapeDtypeStruct((t, w), BF16), jax.ShapeDtypeStruct((1, w), F32)),
        grid=(t // tr,), in_specs=[row, vec, row],
        out_specs=(pl.BlockSpec((1, LANES), lambda i: (0, 0)), row, row, vec),
        compiler_params=_cparams(("arbitrary",)), name=name)(x, g, tgt)


def _attn_bias():
    slopes = np.asarray(2.0 ** (-8.0 * (np.arange(ATTN_HEADS) + 1) / ATTN_HEADS), np.float32)
    qi = np.arange(QBLOCK)[:, None]
    kj = np.arange(3 * QBLOCK)[None, :]
    rel = np.abs(kj - QBLOCK - qi).astype(np.float32)
    tile = np.where(rel <= WINDOW, -slopes[:, None, None] * rel[None], np.float32(NEG_INF)).astype(np.float32)
    return jnp.asarray(tile.reshape(KV_HEADS, GQ * QBLOCK, 3 * QBLOCK))


def _attn_scores(q, k3, n, nb, bias):
    s = lax.dot_general(q, k3, NT_DIMS, preferred_element_type=F32) * (HEAD_DIM ** -0.5)
    col = lax.broadcasted_iota(jnp.int32, (1, 3 * QBLOCK), 1)
    inside = (col >= jnp.where(n == 0, QBLOCK, 0)) & (col < jnp.where(n == nb - 1, 2 * QBLOCK, 3 * QBLOCK))
    return jnp.where(inside, s + bias, NEG_INF)


Q_COL, K_COL, V_COL, U_COL = 0, ATTN_WIDTH // LANES, ATTN_WIDTH // LANES + 1, ATTN_WIDTH // LANES + 2


def _key_rows(ref, n, nb):
    prev, nxt = jnp.maximum(n - 1, 0), jnp.minimum(n + 1, nb - 1)
    blk = lambda b: ref[pl.ds(pl.multiple_of(b * QBLOCK, QBLOCK), QBLOCK), :]
    return jnp.concatenate([blk(prev), blk(n), blk(nxt)], axis=0)


def _head_tiles(x, kh, low):
    tiles = []
    for g in range(GQ):
        h = GQ * kh + g
        t128 = x[:, LANES * (h // 2):LANES * (h // 2 + 1)]
        t128 = jnp.where(low if h % 2 == 0 else jnp.logical_not(low), t128, 0.0)
        if h % 2 != kh:
            t128 = pltpu.roll(t128, HEAD_DIM, 1)
        tiles.append(t128)
    return jnp.concatenate(tiles, axis=0)


def _head_merge(per_kh, low):
    out = []
    for j in range(ATTN_HEADS // 2):
        pair = []
        for h in (2 * j, 2 * j + 1):
            kh, g = h // GQ, h % GQ
            t128 = per_kh[kh][g * QBLOCK:(g + 1) * QBLOCK, :]
            if h % 2 != kh:
                t128 = pltpu.roll(t128, HEAD_DIM, 1)
            pair.append(t128)
        out.append(jnp.where(low, pair[0], pair[1]))
    return jnp.concatenate(out, axis=1)


def _attn_fwd_proj(proj, sink_rows, bias, name):
    t = proj.shape[0]
    nb = t // QBLOCK
    rows = GQ * QBLOCK

    def body(q_ref, k_ref, v_ref, sink_ref, bias_ref, o_ref, lse_ref):
        n = pl.program_id(0)
        low = lax.broadcasted_iota(jnp.int32, (QBLOCK, LANES), 1) < HEAD_DIM
        k3 = _key_rows(k_ref, n, nb).astype(BF16)
        v3 = _key_rows(v_ref, n, nb).astype(BF16)
        q = q_ref[...]
        outs = []
        for kh in range(KV_HEADS):
            qs = _head_tiles(q, kh, low).astype(BF16)
            s = _attn_scores(qs, k3, n, nb, bias_ref[kh])
            sink = sink_ref[kh]
            mx = jnp.maximum(jnp.max(s, axis=-1, keepdims=True), sink)
            p = jnp.exp(s - mx)
            den = jnp.sum(p, axis=-1, keepdims=True) + jnp.exp(sink - mx)
            outs.append(jnp.dot(p.astype(BF16), v3, preferred_element_type=F32) / den)
            lse_ref[0, kh] = mx + jnp.log(den)
        o_ref[...] = _head_merge(outs, low)

    strip = lambda col: pl.BlockSpec((t, LANES), lambda n, col=col: (0, col))
    rowspec = pl.BlockSpec((KV_HEADS, rows, 1), lambda n: (0, 0, 0))
    biasspec = pl.BlockSpec((KV_HEADS, rows, 3 * QBLOCK), lambda n: (0, 0, 0))
    return pl.pallas_call(
        body, out_shape=(jax.ShapeDtypeStruct((t, ATTN_WIDTH), F32), jax.ShapeDtypeStruct((nb, KV_HEADS, rows, 1), F32)),
        grid=(nb,), in_specs=[pl.BlockSpec((QBLOCK, ATTN_WIDTH), lambda n: (n, 0)), strip(K_COL), strip(V_COL),
                              rowspec, biasspec],
        out_specs=(pl.BlockSpec((QBLOCK, ATTN_WIDTH), lambda n: (n, 0)),
                   pl.BlockSpec((1, KV_HEADS, rows, 1), lambda n: (n, 0, 0, 0))),
        compiler_params=_cparams(("parallel",)), name=name)(proj, proj, proj, sink_rows, bias)


def _attn_bwd_proj(proj, sink_rows, bias, o, lse, do, name):
    t = proj.shape[0]
    nb = t // QBLOCK
    rows = GQ * QBLOCK
    scale = HEAD_DIM ** -0.5

    def body(q_ref, k_ref, v_ref, sink_ref, bias_ref, o_ref, lse_ref, do_ref, dq_ref, dk_ref, dv_ref, ds_ref):
        n = pl.program_id(0)

        @pl.when(n == 0)
        def _():
            dk_ref[...] = jnp.zeros_like(dk_ref)
            dv_ref[...] = jnp.zeros_like(dv_ref)
            ds_ref[...] = jnp.zeros_like(ds_ref)

        low = lax.broadcasted_iota(jnp.int32, (QBLOCK, LANES), 1) < HEAD_DIM
        k3 = _key_rows(k_ref, n, nb).astype(BF16)
        v3 = _key_rows(v_ref, n, nb).astype(BF16)
        q, dov = q_ref[...], do_ref[...]
        dod = dov * o_ref[...]
        dqs = []
        dk3 = jnp.zeros((3 * QBLOCK, LANES), F32)
        dv3 = jnp.zeros((3 * QBLOCK, LANES), F32)
        for kh in range(KV_HEADS):
            qs = _head_tiles(q, kh, low).astype(BF16)
            dos = _head_tiles(dov, kh, low).astype(BF16)
            delta = jnp.sum(_head_tiles(dod, kh, low), axis=-1, keepdims=True)
            lse_kh = lse_ref[0, kh]
            s = _attn_scores(qs, k3, n, nb, bias_ref[kh])
            p = jnp.exp(s - lse_kh)
            dp = lax.dot_general(dos, v3, NT_DIMS, preferred_element_type=F32)
            dsb = (p * (dp - delta)).astype(BF16)
            dqs.append(jnp.dot(dsb, k3, preferred_element_type=F32) * scale)
            dk3 = dk3 + lax.dot_general(dsb, qs, TN_DIMS, preferred_element_type=F32) * scale
            dv3 = dv3 + lax.dot_general(p.astype(BF16), dos, TN_DIMS, preferred_element_type=F32)
            dsink_rows = -jnp.exp(sink_ref[kh] - lse_kh) * delta
            ds_ref[kh] += jnp.sum(dsink_rows.reshape(GQ, QBLOCK, 1), axis=1)
        dq_ref[...] = _head_merge(dqs, low)
        prev, nxt = jnp.maximum(n - 1, 0), jnp.minimum(n + 1, nb - 1)
        for j, b in enumerate((prev, n, nxt)):
            blk = pl.ds(pl.multiple_of(b * QBLOCK, QBLOCK), QBLOCK)
            dk_ref[blk, :] += dk3[j * QBLOCK:(j + 1) * QBLOCK, :]
            dv_ref[blk, :] += dv3[j * QBLOCK:(j + 1) * QBLOCK, :]

    strip = lambda col: pl.BlockSpec((t, LANES), lambda n, col=col: (0, col))
    rowspec = pl.BlockSpec((KV_HEADS, rows, 1), lambda n: (0, 0, 0))
    qspec = pl.BlockSpec((QBLOCK, ATTN_WIDTH), lambda n: (n, 0))
    kv_out = pl.BlockSpec((t, LANES), lambda n: (0, 0))
    biasspec = pl.BlockSpec((KV_HEADS, rows, 3 * QBLOCK), lambda n: (0, 0, 0))
    return pl.pallas_call(
        body,
        out_shape=(jax.ShapeDtypeStruct((t, ATTN_WIDTH), F32), jax.ShapeDtypeStruct((t, LANES), F32),
                   jax.ShapeDtypeStruct((t, LANES), F32), jax.ShapeDtypeStruct((KV_HEADS, GQ, 1), F32)),
        grid=(nb,),
        in_specs=[qspec, strip(K_COL), strip(V_COL), rowspec, biasspec, qspec,
                  pl.BlockSpec((1, KV_HEADS, rows, 1), lambda n: (n, 0, 0, 0)), qspec],
        out_specs=(qspec, kv_out, kv_out, pl.BlockSpec((KV_HEADS, GQ, 1), lambda n: (0, 0, 0))),
        compiler_params=_cparams(("arbitrary",)), name=name)(proj, proj, proj, sink_rows, bias, o, lse, do)


def _scan_tables(a_re, a_im, reverse):
    pw = [(a_re, a_im)]
    for _ in range(SUBLANES - 1):
        pr, pi = pw[-1]
        pw.append((pr * a_re - pi * a_im, pr * a_im + pi * a_re))
    rows = np.arange(SUBLANES)
    tabs = []
    for d in (1, 2, 4):
        mask = (rows <= SUBLANES - 1 - d) if reverse else (rows >= d)
        m = jnp.asarray(mask, F32)[:, None]
        tabs += [m * pw[d - 1][0][None, :], m * pw[d - 1][1][None, :]]
    order = (SUBLANES - 1 - rows) if reverse else rows
    tabs += [jnp.stack([pw[j][0] for j in order]), jnp.stack([pw[j][1] for j in order])]
    tab = jnp.stack(tabs)
    return tab.reshape(8, SUBLANES, N_STRIPS, STRIP_ST).transpose(2, 0, 1, 3)


def _scan_pair_chunk(dirs):
    nblk = dirs[0]['xr'].shape[0] // SUBLANES

    @pl.when(pl.program_id(1) == 0)
    def _():
        for d in dirs:
            d['carry'][...] = jnp.zeros_like(d['carry'])

    for d in dirs:
        vb = d['v'][...].astype(BF16)
        d['xr'][...] = jnp.dot(vb, d['mir'][0], preferred_element_type=F32)
        d['xi'][...] = jnp.dot(vb, d['mii'][0], preferred_element_type=F32)
    carries = [(d['carry'][0], d['carry'][1]) for d in dirs]
    for i in range(nblk):
        for k, d in enumerate(dirs):
            rev = d['reverse']
            rows = pl.ds(((nblk - 1 - i) if rev else i) * SUBLANES, SUBLANES)
            cr, ci = carries[k]
            xr, xi = d['xr'][rows, :], d['xi'][rows, :]
            for j, s in enumerate((1, 2, 4)):
                tr_, ti_ = d['tab'][0, 2 * j], d['tab'][0, 2 * j + 1]
                sh = (SUBLANES - s) if rev else s
                sr, si = pltpu.roll(xr, sh, 0), pltpu.roll(xi, sh, 0)
                xr, xi = xr + tr_ * sr - ti_ * si, xi + tr_ * si + ti_ * sr
            pr, pi = d['tab'][0, 6], d['tab'][0, 7]
            xr, xi = xr + pr * cr - pi * ci, xi + pr * ci + pi * cr
            d['xr'][rows, :] = xr
            d['xi'][rows, :] = xi
            edge = 0 if rev else SUBLANES - 1
            carries[k] = (jnp.broadcast_to(xr[edge:edge + 1, :], xr.shape),
                          jnp.broadcast_to(xi[edge:edge + 1, :], xi.shape))
    for k, d in enumerate(dirs):
        d['carry'][0], d['carry'][1] = carries[k]
        d['y'][...] = (jnp.dot(d['xr'][...].astype(BF16), d['mor'][0], preferred_element_type=F32)
                       + jnp.dot(d['xi'][...].astype(BF16), d['moi'][0], preferred_element_type=F32))


def _scan_pair(v, ops_f, ops_b, name):
    t = v.shape[0]
    tc = _tile(t, SCAN_ROWS)
    nc = t // tc

    def body(vf_ref, vb_ref, *refs):
        ops = refs[:10]
        outs = refs[10:16]
        carries = refs[16:18]
        dirs = []
        for k, (v_ref, rev) in enumerate(((vf_ref, False), (vb_ref, True))):
            mir, mii, tab, mor, moi = ops[5 * k:5 * k + 5]
            y, xr, xi = outs[3 * k:3 * k + 3]
            dirs.append(dict(v=v_ref, mir=mir, mii=mii, tab=tab, mor=mor, moi=moi, y=y, xr=xr, xi=xi,
                             carry=carries[k], reverse=rev))
        _scan_pair_chunk(dirs)

    col0 = v.shape[1] // STRIP_IN - N_STRIPS
    fmap = lambda s, c: (c, s)
    bmap = lambda s, c: (nc - 1 - c, s)
    smap3 = lambda s, c: (s, 0, 0)
    m_in = pl.BlockSpec((1, STRIP_IN, STRIP_ST), smap3)
    m_out = pl.BlockSpec((1, STRIP_ST, STRIP_IN), smap3)
    tabspec = pl.BlockSpec((1, 8, SUBLANES, STRIP_ST), lambda s, c: (s, 0, 0, 0))
    opspecs = [m_in, m_in, tabspec, m_out, m_out]
    y_sh = jax.ShapeDtypeStruct((t, SSM_WIDTH), F32)
    x_sh = jax.ShapeDtypeStruct((t, N_STRIPS * STRIP_ST), F32)
    outspecs = lambda m: [pl.BlockSpec((tc, STRIP_IN), m), pl.BlockSpec((tc, STRIP_ST), m),
                          pl.BlockSpec((tc, STRIP_ST), m)]
    res = pl.pallas_call(
        body, out_shape=[y_sh, x_sh, x_sh] * 2, grid=(N_STRIPS, nc),
        in_specs=[pl.BlockSpec((tc, STRIP_IN), lambda s, c: (c, s + col0)),
                  pl.BlockSpec((tc, STRIP_IN), lambda s, c: (nc - 1 - c, s + col0))] + opspecs * 2,
        out_specs=outspecs(fmap) + outspecs(bmap),
        scratch_shapes=[pltpu.VMEM((2, SUBLANES, STRIP_ST), F32)] * 2,
        compiler_params=_cparams(("parallel", "arbitrary")), name=name)(v, v, *ops_f, *ops_b)
    return tuple(res[:3]), tuple(res[3:])


def _scan_adjoint_pair(dy, u, states, adj_ops, name):
    t = dy.shape[0]
    tc = _tile(t, SCAN_ROWS)
    nc = t // tc
    hb = tc // SUBLANES
    n_out = 6

    def body(*refs):
        c = pl.program_id(1)
        dirs = []
        for k in range(2):
            dy_ref, mir, mii, tab, mor, moi, u_ref, xr_ref, xi_ref, hr_ref, hi_ref = refs[11 * k:11 * k + 11]
            outs = refs[22 + n_out * k:22 + n_out * (k + 1)]
            lr_ref, li_ref, carry = refs[22 + 2 * n_out + 3 * k:22 + 2 * n_out + 3 * k + 3]
            dirs.append(dict(v=dy_ref, mir=mir, mii=mii, tab=tab, mor=mor, moi=moi, y=outs[0], xr=lr_ref, xi=li_ref,
                             carry=carry, reverse=(k == 0), u=u_ref, fx=(xr_ref, xi_ref), halo=(hr_ref, hi_ref),
                             acc=outs[1:]))

        @pl.when(c == 0)
        def _():
            for d in dirs:
                for r in d['acc']:
                    r[...] = jnp.zeros_like(r)

        _scan_pair_chunk(dirs)
        for d in dirs:
            fwd_reverse = not d['reverse']
            rc = (nc - 1 - c) if d['reverse'] else c
            dmir_ref, dmii_ref, dmor_ref, dmoi_ref, da_ref = d['acc']
            xrv, xiv, lrv, liv = d['fx'][0][...], d['fx'][1][...], d['xr'][...], d['xi'][...]
            hr_ref, hi_ref = d['halo']
            row = lax.broadcasted_iota(jnp.int32, xrv.shape, 0)
            if fwd_reverse:
                live = (rc < nc - 1).astype(F32)
                edge_r, edge_i = hr_ref[0:1, :] * live, hi_ref[0:1, :] * live
                xpr = jnp.where(row == tc - 1, edge_r, pltpu.roll(xrv, tc - 1, 0))
                xpi = jnp.where(row == tc - 1, edge_i, pltpu.roll(xiv, tc - 1, 0))
            else:
                live = (rc > 0).astype(F32)
                edge_r, edge_i = hr_ref[SUBLANES - 1:SUBLANES, :] * live, hi_ref[SUBLANES - 1:SUBLANES, :] * live
                xpr = jnp.where(row == 0, edge_r, pltpu.roll(xrv, 1, 0))
                xpi = jnp.where(row == 0, edge_i, pltpu.roll(xiv, 1, 0))
            da_ref[0, 0:1, :] += jnp.sum(xpr * lrv + xpi * liv, axis=0, keepdims=True)
            da_ref[0, 1:2, :] += jnp.sum(xpr * liv - xpi * lrv, axis=0, keepdims=True)
            ub, dyb = d['u'][...].astype(BF16), d['v'][...].astype(BF16)
            dmir_ref[0] += lax.dot_general(ub, lrv.astype(BF16), TN_DIMS, preferred_element_type=F32)
            dmii_ref[0] += lax.dot_general(ub, liv.astype(BF16), TN_DIMS, preferred_element_type=F32)
            dmor_ref[0] += lax.dot_general(xrv.astype(BF16), dyb, TN_DIMS, preferred_element_type=F32)
            dmoi_ref[0] += lax.dot_general(xiv.astype(BF16), dyb, TN_DIMS, preferred_element_type=F32)

    col0 = u.shape[1] // STRIP_IN - N_STRIPS
    smap3 = lambda s, c: (s, 0, 0)
    m_in = pl.BlockSpec((1, STRIP_IN, STRIP_ST), smap3)
    m_out = pl.BlockSpec((1, STRIP_ST, STRIP_IN), smap3)
    tabspec = pl.BlockSpec((1, 8, SUBLANES, STRIP_ST), lambda s, c: (s, 0, 0, 0))
    in_specs, out_specs, args = [], [], []
    for k in range(2):
        reverse = k == 0
        rowblk = (lambda c: nc - 1 - c) if reverse else (lambda c: c)
        tmap = lambda s, c, rowblk=rowblk: (rowblk(c), s)
        umap = lambda s, c, rowblk=rowblk: (rowblk(c), s + col0)
        if not reverse:
            hmap = lambda s, c, rowblk=rowblk: (jnp.minimum((rowblk(c) + 1) * hb, t // SUBLANES - 1), s)
        else:
            hmap = lambda s, c, rowblk=rowblk: (jnp.maximum(rowblk(c) * hb - 1, 0), s)
        narrow = pl.BlockSpec((tc, STRIP_IN), tmap)
        wide = pl.BlockSpec((tc, STRIP_ST), tmap)
        halo = pl.BlockSpec((SUBLANES, STRIP_ST), hmap)
        in_specs += [narrow, m_in, m_in, tabspec, m_out, m_out, pl.BlockSpec((tc, STRIP_IN), umap), wide, wide,
                     halo, halo]
        out_specs += [narrow, m_in, m_in, m_out, m_out, pl.BlockSpec((1, SUBLANES, STRIP_ST), smap3)]
        xr, xi = states[k]
        args += [dy, *adj_ops[k], u, xr, xi, xr, xi]
    out_shape = [jax.ShapeDtypeStruct((t, SSM_WIDTH), F32),
                 jax.ShapeDtypeStruct((N_STRIPS, STRIP_IN, STRIP_ST), F32),
                 jax.ShapeDtypeStruct((N_STRIPS, STRIP_IN, STRIP_ST), F32),
                 jax.ShapeDtypeStruct((N_STRIPS, STRIP_ST, STRIP_IN), F32),
                 jax.ShapeDtypeStruct((N_STRIPS, STRIP_ST, STRIP_IN), F32),
                 jax.ShapeDtypeStruct((N_STRIPS, SUBLANES, STRIP_ST), F32)] * 2
    res = pl.pallas_call(
        body, out_shape=out_shape, grid=(N_STRIPS, nc), in_specs=in_specs, out_specs=out_specs,
        scratch_shapes=[pltpu.VMEM((tc, STRIP_ST), F32), pltpu.VMEM((tc, STRIP_ST), F32),
                        pltpu.VMEM((2, SUBLANES, STRIP_ST), F32)] * 2,
        compiler_params=_cparams(("parallel", "arbitrary")), name=name)(*args)
    return tuple(res[:n_out]), tuple(res[n_out:])


def _ssm_prep(lam_re, lam_im, log_dt, bt_re, bt_im, c_re, c_im):
    lr = jnp.minimum(lam_re, LAMBDA_RE_MAX)
    li = lam_im
    dt = jnp.exp(log_dt)[:, None]
    mag = jnp.exp(lr * dt)
    a_re = mag * jnp.cos(li * dt)
    a_im = mag * jnp.sin(li * dt)
    den = lr * lr + li * li
    coef_re = ((a_re - 1.0) * lr + a_im * li) / den
    coef_im = (a_im * lr - (a_re - 1.0) * li) / den
    bb_re = coef_re[:, None, :] * bt_re - coef_im[:, None, :] * bt_im
    bb_im = coef_re[:, None, :] * bt_im + coef_im[:, None, :] * bt_re
    eye = jnp.eye(SSM_GROUPS // N_STRIPS, dtype=F32)

    def strips(m):
        g, a, b = m.shape
        m4 = m.reshape(N_STRIPS, g // N_STRIPS, a, b)
        return jnp.einsum('sgab,gk->sgakb', m4, eye).reshape(N_STRIPS, g // N_STRIPS * a, g // N_STRIPS * b)

    mi_re = strips(bb_re)
    mi_im = strips(bb_im)
    mo_re = strips(jnp.swapaxes(c_re, 1, 2))
    mo_im = strips(-jnp.swapaxes(c_im, 1, 2))
    return a_re.reshape(-1), a_im.reshape(-1), mi_re, mi_im, mo_re, mo_im


def _gelu(x):
    c = math.sqrt(2.0 / math.pi)
    return 0.5 * x * (1.0 + jnp.tanh(c * (x + 0.044715 * x * x * x)))


def _gelu_grad(x):
    c = math.sqrt(2.0 / math.pi)
    th = jnp.tanh(c * (x + 0.044715 * x * x * x))
    return 0.5 * (1.0 + th) + 0.5 * x * (1.0 - th * th) * c * (1.0 + 3.0 * 0.044715 * x * x)


def _last_cols_specs(u, w, tr):
    half = w // 2
    first = (u.shape[1] - w) // half
    assert first * half == u.shape[1] - w
    return [pl.BlockSpec((tr, half), lambda i, k=k: (i, first + k)) for k in range(2)]


def _ssm_post_fwd(u, yf, yb, d, wglu, bglu, name):
    t, w = yf.shape
    tr = _row_tile(t)

    def body(ua_ref, ub_ref, yf_ref, yb_ref, d_ref, w_ref, b_ref, s_ref, y0_ref, z_ref):
        uv = jnp.concatenate([ua_ref[...], ub_ref[...]], axis=1)
        y0 = d_ref[...] * uv + yf_ref[...] + yb_ref[...]
        yg = _gelu(y0)
        z = jnp.dot(yg.astype(BF16), w_ref[...], preferred_element_type=F32) + b_ref[...]
        s_ref[...] = yg * _sigmoid(z)
        y0_ref[...] = y0
        z_ref[...] = z

    row = pl.BlockSpec((tr, w), lambda i: (i, 0))
    vec = pl.BlockSpec((1, w), lambda i: (0, 0))
    mat = pl.BlockSpec((w, w), lambda i: (0, 0))
    sh = jax.ShapeDtypeStruct((t, w), F32)
    return pl.pallas_call(body, out_shape=(sh, sh, sh), grid=(t // tr,),
                          in_specs=[*_last_cols_specs(u, w, tr), row, row, vec, mat, vec], out_specs=(row, row, row),
                          compiler_params=_cparams(("parallel",)), name=name)(u, u, yf, yb, d, wglu, bglu)


def _ssm_post_bwd(ds, y0, z, u, d, wglu, name):
    t, w = ds.shape
    tr = _row_tile(t)

    def body(ds_ref, y0_ref, z_ref, ua_ref, ub_ref, d_ref, w_ref, dy0_ref, dw_ref, db_ref, dd_ref):
        @pl.when(pl.program_id(0) == 0)
        def _():
            dw_ref[...] = jnp.zeros_like(dw_ref)
            db_ref[...] = jnp.zeros_like(db_ref)
            dd_ref[...] = jnp.zeros_like(dd_ref)

        y0 = y0_ref[...]
        yg = _gelu(y0)
        sg = _sigmoid(z_ref[...])
        dsv = ds_ref[...]
        dz = dsv * yg * sg * (1.0 - sg)
        dzb = dz.astype(BF16)
        dyg = dsv * sg + lax.dot_general(dzb, w_ref[...], (((1,), (1,)), ((), ())), preferred_element_type=F32)
        dy0 = dyg * _gelu_grad(y0)
        dy0_ref[...] = dy0
        dw_ref[...] += lax.dot_general(yg.astype(BF16), dzb, (((0,), (0,)), ((), ())), preferred_element_type=F32)
        db_ref[...] += jnp.sum(dz, axis=0, keepdims=True)
        uv = jnp.concatenate([ua_ref[...], ub_ref[...]], axis=1)
        dd_ref[...] += jnp.sum(dy0 * uv, axis=0, keepdims=True)

    row = pl.BlockSpec((tr, w), lambda i: (i, 0))
    vec = pl.BlockSpec((1, w), lambda i: (0, 0))
    mat = pl.BlockSpec((w, w), lambda i: (0, 0))
    return pl.pallas_call(
        body, out_shape=(jax.ShapeDtypeStruct((t, w), F32), jax.ShapeDtypeStruct((w, w), F32),
                         jax.ShapeDtypeStruct((1, w), F32), jax.ShapeDtypeStruct((1, w), F32)),
        grid=(t // tr,), in_specs=[row, row, row, *_last_cols_specs(u, w, tr), vec, mat],
        out_specs=(row, mat, vec, vec),
        compiler_params=_cparams(("arbitrary",)), name=name)(ds, y0, z, u, u, d, wglu)


def _du_combine(dy0, d, du_f, du_b, name):
    t, w = dy0.shape
    tr = _row_tile(t)

    def body(dy_ref, d_ref, a_ref, b_ref, o_ref):
        o_ref[...] = d_ref[...] * dy_ref[...] + a_ref[...] + b_ref[...]

    row = pl.BlockSpec((tr, w), lambda i: (i, 0))
    vec = pl.BlockSpec((1, w), lambda i: (0, 0))
    return pl.pallas_call(body, out_shape=jax.ShapeDtypeStruct((t, w), F32), grid=(t // tr,),
                          in_specs=[row, vec, row, row], out_specs=row, compiler_params=_cparams(("parallel",)),
                          name=name)(dy0, d, du_f, du_b)


def _ffn_fwd(x, g, wg, wu, wd, tag):
    xo, h, gate, up = _ffn_fwd_call(x, g, wg, wu, wd, f"{tag}_fwd")
    return xo, (h, gate, up)


def _ffn_bwd(dxo, dxo_b, x, g, wg, wu, wd, saved, tag):
    h, gate, up = saved
    dx, dx_b, dg, dgate, dup, act = _ffn_bwd_x_call(dxo, dxo_b, x, g, gate, up, wg, wu, wd, f"{tag}_bwd_x")
    dwg, dwu, dwd = _ffn_bwd_w_call(h, dxo_b, dgate, dup, act, f"{tag}_bwd_w")
    return dx, dx_b, dg, dwg, dwu, dwd


def _local_step(x, tgt, w, get_weights, put_grads, reduce_wide):
    t = x.shape[0]
    row = lambda a: a.reshape(1, -1)
    grads = {}

    w = dict(w)

    ssm_names = ['ssm_lambda_re', 'ssm_lambda_im', 'ssm_log_dt', 'ssm_b_re', 'ssm_b_im', 'ssm_c_re', 'ssm_c_im']
    tr3 = lambda m: jnp.swapaxes(m, 1, 2)
    fwd_ops, adj_ops, vjps = [], [], []
    for direction in range(2):
        rev = direction == 1
        prep, vjp = jax.vjp(_ssm_prep, *[w[n][direction] for n in ssm_names])
        a_re, a_im = prep[0], prep[1]
        mi_re, mi_im, mo_re, mo_im = (m.astype(BF16) for m in prep[2:])
        fwd_ops.append((mi_re, mi_im, _scan_tables(a_re, a_im, rev), mo_re, mo_im))
        adj_ops.append((tr3(mo_re), tr3(mo_im), _scan_tables(a_re, -a_im, not rev), tr3(mi_re), tr3(mi_im)))
        vjps.append(vjp)
    sink_rows = jnp.repeat(w['attn_sinks'].reshape(KV_HEADS, GQ), QBLOCK, axis=1)[..., None]
    bias = _attn_bias()
    prepared = sum(jnp.sum(op[:1, :1].astype(F32)) for ops in fwd_ops + adj_ops for op in ops) + sink_rows[0, 0, 0]

    w.update(get_weights('ffn1', prepared.reshape(1, 1)))
    x1, ffn1_saved = _ffn_fwd(x, w['norm_ffn1'], w['ffn1_w_gate'], w['ffn1_w_up'], w['ffn1_w_down'], "ffn1")
    w.update(get_weights('mix', x1))

    h2 = _rms_fwd(x1, w['norm_mix'], "mix_norm")
    proj = _mm(h2, w['w_in'], tb=True, name="in_proj")[0]
    u = proj

    attn, lse = _attn_fwd_proj(proj, sink_rows, bias, "attn_fwd")

    (y_f, *states_f), (y_b, *states_b) = _scan_pair(u, fwd_ops[0], fwd_ops[1], "s5_fwd")
    ys, states = [y_f, y_b], [states_f, states_b]
    d_row = row(w['ssm_d'])
    s, y0, z = _ssm_post_fwd(u, ys[0], ys[1], d_row, w['ssm_glu_w'], row(w['ssm_glu_b']), "ssm_post")

    ma = _rms_fwd(attn, row(w['attn_out_norm']), "attn_out_norm")
    ms = _rms_fwd(s, row(w['ssm_out_norm']), "ssm_out_norm")
    mixed = jnp.concatenate([ma, ms], axis=-1)
    x2 = _mm(mixed, w['w_out'], res=x1, reduce_s=True, name="out_proj")

    w.update(get_weights('ffn2', x2))
    x3, ffn2_saved = _ffn_fwd(x2, w['norm_ffn2'], w['ffn2_w_gate'], w['ffn2_w_up'], w['ffn2_w_down'], "ffn2")

    loss, dx3, dx3_b, dgf = _loss_head(x3, row(w['final_norm']), tgt, "loss_head")
    grads['final_norm'] = dgf.reshape(w['final_norm'].shape)

    dx2, dx2_b, dg, dwg, dwu, dwd = _ffn_bwd(dx3, dx3_b, x2, w['norm_ffn2'], w['ffn2_w_gate'], w['ffn2_w_up'],
                                             w['ffn2_w_down'], ffn2_saved, "ffn2")
    grads['norm_ffn2'] = dg
    sent = put_grads('ffn2', dict(ffn2_w_gate=dwg, ffn2_w_up=dwu, ffn2_w_down=dwd))

    dmixed = _mm(dx2_b, w['w_out'], tb=True, reduce_s=True, after=sent, name="out_proj_dx")
    dw_out = _mm(mixed, dx2_b, ta=True, out_dtype=BF16, name="out_proj_dw")[0]
    dattn, _, dga = _rms_bwd(attn, row(w['attn_out_norm']), dmixed[:, :ATTN_WIDTH], None, "attn_out_dnorm")
    ds, _, dgs = _rms_bwd(s, row(w['ssm_out_norm']), dmixed[:, ATTN_WIDTH:], None, "ssm_out_dnorm")
    grads.update(attn_out_norm=dga, ssm_out_norm=dgs)

    dy0, dwglu, dbglu, dd = _ssm_post_bwd(ds, y0, z, u, d_row, w['ssm_glu_w'], "ssm_post_bwd")
    grads['ssm_glu_b'] = dbglu
    grads['ssm_d'] = dd.reshape(w['ssm_d'].shape)
    dparams, du_dirs = [], []
    for direction, res in enumerate(_scan_adjoint_pair(dy0, u, states, adj_ops, "s5_adj")):
        du_dir, dmir, dmii, dmor, dmoi, da = res
        du_dirs.append(du_dir)
        da_re = da[:, 0, :].reshape(-1)
        da_im = da[:, 1, :].reshape(-1)
        dparams.append(vjps[direction]((da_re, da_im, dmir, dmii, dmor, dmoi)))
    du = _du_combine(dy0, d_row, du_dirs[0], du_dirs[1], "ssm_du")
    for i, n in enumerate(ssm_names):
        grads[n] = jnp.stack([dparams[0][i], dparams[1][i]])
    wide_sum = reduce_wide(grads)

    dq, dk, dv, dsink = _attn_bwd_proj(proj, sink_rows, bias, attn, lse, dattn, "attn_bwd")
    grads['attn_sinks'] = dsink.reshape(w['attn_sinks'].shape)
    dproj = jnp.concatenate([dq, dk, dv, du], axis=-1).astype(BF16)

    dw_in = _mm(dproj, h2, ta=True, out_dtype=BF16, after=wide_sum, name="in_proj_dw")[0]
    sent = put_grads('mix', dict(w_in=dw_in, ssm_glu_w=dwglu, w_out=dw_out))
    dh2 = _mm(dproj, w['w_in'], reduce_s=True, after=sent, name="in_proj_dx")
    dx1, dx1_b, dgm = _rms_bwd(x1, w['norm_mix'], dh2, dx2, "mix_dnorm")
    grads['norm_mix'] = dgm

    dx0, _, dg, dwg, dwu, dwd = _ffn_bwd(dx1, dx1_b, x, w['norm_ffn1'], w['ffn1_w_gate'], w['ffn1_w_up'],
                                         w['ffn1_w_down'], ffn1_saved, "ffn1")
    grads['norm_ffn1'] = dg
    put_grads('ffn1', dict(ffn1_w_gate=dwg, ffn1_w_up=dwu, ffn1_w_down=dwd))
    return loss, dx0, grads, wide_sum


HBM_SPEC = pl.BlockSpec(memory_space=pl.ANY)


def _chip_peers(x, y):
    return [(1 - x, y), (x, 1 - y), (1 - x, 1 - y)]


HBM_ONLY = pl.BlockSpec(memory_space=pltpu.HBM)
SEM_SPEC = pl.BlockSpec(memory_space=pltpu.SEMAPHORE)
EFFECT = pltpu.SideEffectType.DATAFLOW_SIDE_EFFECTING


def _place_own(src, slot, name):
    r, c = src.shape
    tr = r // 2

    def body(slot_ref, s_ref, o_ref):
        o_ref[0] = s_ref[...]

    return pl.pallas_call(
        body, out_shape=jax.ShapeDtypeStruct((N_CHIPS, r, c), src.dtype),
        grid_spec=pltpu.PrefetchScalarGridSpec(
            num_scalar_prefetch=1, grid=(2,), in_specs=[pl.BlockSpec((tr, c), lambda i, s: (i, 0))],
            out_specs=pl.BlockSpec((1, tr, c), lambda i, s: (s[0], i, 0))),
        compiler_params=_cparams(("parallel",)), name=name)(slot, src)


def _chip_copies(srcs, lands, send_sems, recv_sems, scatter, landed):
    x, y, c = lax.axis_index("x"), lax.axis_index("y"), lax.axis_index("c")
    me = 2 * x + y
    out = []
    for i in range(len(srcs)):
        for j, (px, py) in enumerate(_chip_peers(x, y)):
            p = 2 * px + py
            slot = p if landed else me
            if scatter:
                src, dst = srcs[i].at[p], lands[i].at[slot]
            else:
                rows = _core_half(srcs[i].shape[0], c)
                src, dst = srcs[i].at[rows], lands[i].at[slot, rows]
            out.append(pltpu.make_async_remote_copy(src, dst, send_sems.at[3 * i + j], recv_sems.at[3 * i + j],
                                                    device_id=(px, py, c), device_id_type=MESH))
    return out


def _core_half(nrows, c):
    half = nrows // 2
    return pl.ds(pl.multiple_of(c * half, 16), half)


def _sibling_forward(lands, name):
    n = len(lands)

    def body(*refs):
        bufs = refs[n:2 * n]
        send_sems, recv_sems = refs[2 * n:]
        x, y, c = lax.axis_index("x"), lax.axis_index("y"), lax.axis_index("c")
        mine = [_core_half(b.shape[1], c) for b in bufs]
        theirs = [_core_half(b.shape[1], 1 - c) for b in bufs]
        chips = [2 * px + py for px, py in _chip_peers(x, y)]
        cps = [pltpu.make_async_remote_copy(bufs[i].at[p, mine[i]], bufs[i].at[p, mine[i]], send_sems.at[3 * i + j],
                                            recv_sems.at[3 * i + j], device_id=(x, y, 1 - c), device_id_type=MESH)
               for i in range(n) for j, p in enumerate(chips)]
        for cp in cps:
            cp.start()
        for i in range(n):
            for j, p in enumerate(chips):
                pltpu.make_async_remote_copy(bufs[i].at[p, mine[i]], bufs[i].at[p, theirs[i]], send_sems.at[3 * i + j],
                                             recv_sems.at[3 * i + j], device_id=(x, y, 1 - c),
                                             device_id_type=MESH).wait()

    return pl.pallas_call(
        body, out_shape=[jax.ShapeDtypeStruct(a.shape, a.dtype) for a in lands],
        in_specs=[HBM_SPEC] * n, out_specs=[HBM_SPEC] * n, input_output_aliases={k: k for k in range(n)},
        scratch_shapes=[pltpu.SemaphoreType.DMA((3 * n,)), pltpu.SemaphoreType.DMA((3 * n,))],
        name=name)(*lands)


def _exchange_start(groups, scatter, name, after=None):
    sizes = [len(srcs) for srcs, _ in groups]
    flat_src = [a for srcs, _ in groups for a in srcs]
    flat_land = [a for _, lands in groups for a in lands]
    n = len(flat_src)
    ng = len(groups)

    def body(*refs):
        src_refs, land_refs = refs[:n], refs[n:2 * n]
        n_in = 2 * n + (after is not None)
        sems = refs[n_in:n_in + 2 * ng]
        token_ref = refs[-1]
        off = 0
        for gi, sz in enumerate(sizes):
            for cp in _chip_copies(src_refs[off:off + sz], land_refs[off:off + sz], sems[2 * gi], sems[2 * gi + 1],
                                   scatter, landed=False):
                cp.start()
            off += sz
        token_ref[...] = jnp.zeros_like(token_ref)

    sem_shapes = []
    for sz in sizes:
        sem_shapes += [pltpu.SemaphoreType.DMA((3 * sz,)), pltpu.SemaphoreType.DMA((3 * sz,))]
    hbm = lambda a: pltpu.HBM(a.shape, a.dtype)
    res = pl.pallas_call(
        body, name=name,
        out_shape=(tuple(sem_shapes) + tuple(hbm(a) for a in flat_src) + tuple(hbm(a) for a in flat_land)
                   + (jax.ShapeDtypeStruct((SUBLANES, LANES), F32),)),
        in_specs=[HBM_ONLY] * (2 * n) + [HBM_SPEC] * (after is not None),
        out_specs=tuple([SEM_SPEC] * (2 * ng) + [HBM_ONLY] * (2 * n) + [pl.BlockSpec(memory_space=pltpu.VMEM)]),
        input_output_aliases={k: 2 * ng + k for k in range(2 * n)},
        compiler_params=pltpu.CompilerParams(has_side_effects=EFFECT),
    )(*[pltpu.with_memory_space_constraint(a, pltpu.HBM) for a in flat_src + flat_land],
      *([after] if after is not None else []))
    sems, thru_src, thru_land = res[:2 * ng], res[2 * ng:2 * ng + n], res[2 * ng + n:2 * ng + 2 * n]
    out, off = [], 0
    for gi, sz in enumerate(sizes):
        out.append((sems[2 * gi], sems[2 * gi + 1], list(thru_src[off:off + sz]), list(thru_land[off:off + sz])))
        off += sz
    return out, res[-1]


def _exchange_wait(started, after, scatter, name):
    send_sems, recv_sems, srcs, lands = started
    n = len(srcs)

    def body(*refs):
        src_refs, land_refs = refs[:n], refs[n:2 * n]
        send_ref, recv_ref = refs[2 * n], refs[2 * n + 1]
        for cp in _chip_copies(src_refs, land_refs, send_ref, recv_ref, scatter, landed=True):
            cp.wait_send()
            cp.wait_recv()

    hbm = lambda a: pltpu.HBM(a.shape, a.dtype)
    res = pl.pallas_call(
        body, name=name, out_shape=tuple(hbm(a) for a in srcs) + tuple(hbm(a) for a in lands),
        in_specs=[HBM_ONLY] * (2 * n) + [SEM_SPEC, SEM_SPEC, HBM_SPEC], out_specs=tuple([HBM_ONLY] * (2 * n)),
        input_output_aliases={k: k for k in range(2 * n)},
        compiler_params=pltpu.CompilerParams(has_side_effects=EFFECT),
    )(*srcs, *lands, send_sems, recv_sems, after)
    return list(res[:n]), list(res[n:])


def _half_swap(parts, name):
    n = len(parts)

    def body(*refs):
        ins, outs = refs[:n], refs[n:2 * n]
        send_sems, recv_sems = refs[2 * n:]
        x, y, c = lax.axis_index("x"), lax.axis_index("y"), lax.axis_index("c")
        cps = [pltpu.make_async_remote_copy(ins[i].at[k, _core_half(ins[i].shape[1], 1 - c)], outs[i].at[k],
                                            send_sems.at[N_CHIPS * i + k], recv_sems.at[N_CHIPS * i + k],
                                            device_id=(x, y, 1 - c), device_id_type=MESH)
               for i in range(n) for k in range(N_CHIPS)]
        for cp in cps:
            cp.start()
        for cp in cps:
            cp.wait()

    return pl.pallas_call(
        body, out_shape=[jax.ShapeDtypeStruct((N_CHIPS, p.shape[1] // 2, p.shape[2]), p.dtype) for p in parts],
        in_specs=[HBM_SPEC] * n, out_specs=[HBM_SPEC] * n,
        scratch_shapes=[pltpu.SemaphoreType.DMA((N_CHIPS * n,)), pltpu.SemaphoreType.DMA((N_CHIPS * n,))],
        name=name)(*parts)


def _half_add(parts, sib, slots, name):
    na = len(parts)
    _, r, c = parts[0].shape
    hr = r // 2
    tr = _row_tile(hr, 512)
    nt = hr // tr

    def body(slot_ref, *refs):
        for a in range(na):
            refs[2 * na + a][...] = (refs[2 * a][...].astype(F32) + refs[2 * a + 1][...].astype(F32)).astype(BF16)

    mine = pl.BlockSpec((1, tr, c), lambda k, i, s: (k, i + s[4] * nt, 0))
    half = pl.BlockSpec((1, tr, c), lambda k, i, s: (k, i, 0))
    args = [a for p, sb in zip(parts, sib) for a in (p, sb)]
    return pl.pallas_call(
        body, out_shape=[jax.ShapeDtypeStruct((N_CHIPS, hr, c), BF16)] * na,
        grid_spec=pltpu.PrefetchScalarGridSpec(
            num_scalar_prefetch=1, grid=(N_CHIPS, nt), in_specs=[mine, half] * na, out_specs=[half] * na),
        compiler_params=_cparams(("parallel", "parallel")), name=name)(slots, *args)


def _half_forward(arrs, name):
    n = len(arrs)

    def body(*refs):
        bufs = refs[n:2 * n]
        send_sems, recv_sems = refs[2 * n:]
        x, y, c = lax.axis_index("x"), lax.axis_index("y"), lax.axis_index("c")
        cps = [pltpu.make_async_remote_copy(b.at[_core_half(b.shape[0], c)], b.at[_core_half(b.shape[0], c)],
                                            send_sems.at[i], recv_sems.at[i], device_id=(x, y, 1 - c),
                                            device_id_type=MESH) for i, b in enumerate(bufs)]
        for cp in cps:
            cp.start()
        for i, b in enumerate(bufs):
            pltpu.make_async_remote_copy(b.at[_core_half(b.shape[0], c)], b.at[_core_half(b.shape[0], 1 - c)],
                                         send_sems.at[i], recv_sems.at[i], device_id=(x, y, 1 - c),
                                         device_id_type=MESH).wait()

    return pl.pallas_call(
        body, out_shape=[jax.ShapeDtypeStruct(a.shape, a.dtype) for a in arrs],
        in_specs=[HBM_SPEC] * n, out_specs=[HBM_SPEC] * n, input_output_aliases={k: k for k in range(n)},
        scratch_shapes=[pltpu.SemaphoreType.DMA((n,)), pltpu.SemaphoreType.DMA((n,))],
        name=name)(*arrs)


def _small_exchange(smalls, name):
    nsm = len(smalls)
    rels = [(fx, fy, fc) for fx in (0, 1) for fy in (0, 1) for fc in (0, 1)][1:]

    def body(*refs):
        sins, souts = refs[:nsm], refs[nsm:2 * nsm]
        ssend, srecv, slocal = refs[2 * nsm:]
        x, y, c = lax.axis_index("x"), lax.axis_index("y"), lax.axis_index("c")
        lin = 4 * x + 2 * y + c
        local = [pltpu.make_async_copy(sins[i], souts[i].at[lin], slocal.at[i]) for i in range(nsm)]
        for cp in local:
            cp.start()
        for i in range(nsm):
            for j, (fx, fy, fc) in enumerate(rels):
                pltpu.make_async_remote_copy(sins[i], souts[i].at[lin], ssend.at[i, j], srecv.at[i, j],
                                             device_id=(x ^ fx, y ^ fy, c ^ fc), device_id_type=MESH).start()
        for i in range(nsm):
            for j, (fx, fy, fc) in enumerate(rels):
                src = 4 * (x ^ fx) + 2 * (y ^ fy) + (c ^ fc)
                pltpu.make_async_remote_copy(sins[i], souts[i].at[src], ssend.at[i, j], srecv.at[i, j],
                                             device_id=(x ^ fx, y ^ fy, c ^ fc), device_id_type=MESH).wait()
        for cp in local:
            cp.wait()

    return pl.pallas_call(
        body, out_shape=[jax.ShapeDtypeStruct((N_DEV,) + s.shape, s.dtype) for s in smalls],
        in_specs=[HBM_SPEC] * nsm, out_specs=[HBM_SPEC] * nsm,
        scratch_shapes=[pltpu.SemaphoreType.DMA((nsm, 7)), pltpu.SemaphoreType.DMA((nsm, 7)),
                        pltpu.SemaphoreType.DMA((nsm,))],
        name=name)(*smalls)


def _sum_parts(parts, recv, slots, name):
    na = len(parts)
    _, r, c = parts[0].shape
    tr = _row_tile(r, 192)

    def body(slot_ref, *refs):
        for a in range(na):
            own_ref, r0_ref, r1_ref, r2_ref = refs[4 * a:4 * a + 4]
            refs[4 * na + a][...] = ((own_ref[0].astype(F32) + r0_ref[0].astype(F32))
                                     + (r1_ref[0].astype(F32) + r2_ref[0].astype(F32)))

    blk = lambda k: pl.BlockSpec((1, tr, c), lambda i, s, k=k: (s[k], i, 0))
    out_blk = pl.BlockSpec((tr, c), lambda i, s: (i + s[4] * (r // tr), 0))
    args = [a for p, rv in zip(parts, recv) for a in (p, rv, rv, rv)]
    return pl.pallas_call(
        body, out_shape=[jax.ShapeDtypeStruct((2 * r, c), F32)] * na,
        grid_spec=pltpu.PrefetchScalarGridSpec(
            num_scalar_prefetch=1, grid=(r // tr,), in_specs=[blk(0), blk(1), blk(2), blk(3)] * na,
            out_specs=[out_blk] * na),
        compiler_params=_cparams(("parallel",)), name=name)(slots, *args)


def _small_allreduce(packed, name):
    rows = packed.shape[0]
    pr = rows // N_DEV
    rels = [(fx, fy, fc) for fx in (0, 1) for fy in (0, 1) for fc in (0, 1)][1:]

    def body(in_ref, out_ref, recv_ref, send1, recv1, send2, recv2):
        x, y, c = lax.axis_index("x"), lax.axis_index("y"), lax.axis_index("c")
        lin = 4 * x + 2 * y + c
        piece = lambda ref, k: ref.at[pl.ds(pl.multiple_of(k * pr, pr), pr), :]
        peers = [((x ^ fx, y ^ fy, c ^ fc), 4 * (x ^ fx) + 2 * (y ^ fy) + (c ^ fc)) for fx, fy, fc in rels]
        for j, (dev, plin) in enumerate(peers):
            pltpu.make_async_remote_copy(piece(in_ref, plin), recv_ref.at[lin], send1.at[j], recv1.at[j],
                                         device_id=dev, device_id_type=MESH).start()
        recv_ref[lin] = piece(in_ref, lin)[...]
        for j, (dev, plin) in enumerate(peers):
            pltpu.make_async_remote_copy(piece(in_ref, plin), recv_ref.at[plin], send1.at[j], recv1.at[j],
                                         device_id=dev, device_id_type=MESH).wait()
        acc = recv_ref[0]
        for k in range(1, N_DEV):
            acc = acc + recv_ref[k]
        piece(out_ref, lin)[...] = acc
        for j, (dev, plin) in enumerate(peers):
            pltpu.make_async_remote_copy(piece(out_ref, lin), piece(out_ref, lin), send2.at[j], recv2.at[j],
                                         device_id=dev, device_id_type=MESH).start()
        for j, (dev, plin) in enumerate(peers):
            pltpu.make_async_remote_copy(piece(out_ref, lin), piece(out_ref, plin), send2.at[j], recv2.at[j],
                                         device_id=dev, device_id_type=MESH).wait()

    vm = pl.BlockSpec(memory_space=pltpu.VMEM)
    return pl.pallas_call(
        body, out_shape=jax.ShapeDtypeStruct(packed.shape, F32), in_specs=[vm], out_specs=vm,
        scratch_shapes=[pltpu.VMEM((N_DEV, pr, LANES), F32)] + [pltpu.SemaphoreType.DMA((7,))] * 4,
        compiler_params=pltpu.CompilerParams(vmem_limit_bytes=VMEM_LIMIT), name=name)(packed)


def _adamw_math(w, m, v, g):
    nm = ADAM_B1 * m + (1.0 - ADAM_B1) * g
    nv = ADAM_B2 * v + (1.0 - ADAM_B2) * (g * g)
    m_hat = nm * (1.0 / (1.0 - ADAM_B1 ** ADAM_STEP))
    v_hat = nv * (1.0 / (1.0 - ADAM_B2 ** ADAM_STEP))
    return -ADAM_LR * (m_hat / (jnp.sqrt(v_hat) + ADAM_EPS) + ADAM_WD * w), nm, nv


def _adamw(ws, ms, vs, gs, name):
    na = len(ws)
    r, c = ws[0].shape
    tr = _row_tile(r)

    def body(*refs):
        for a in range(na):
            w_ref, m_ref, v_ref, g_ref = refs[4 * a:4 * a + 4]
            d_ref, nm_ref, nv_ref = refs[4 * na + 3 * a:4 * na + 3 * a + 3]
            d_ref[...], nm_ref[...], nv_ref[...] = _adamw_math(w_ref[...], m_ref[...], v_ref[...], g_ref[...])

    blk = pl.BlockSpec((tr, c), lambda i: (i, 0))
    sh = jax.ShapeDtypeStruct((r, c), F32)
    args = [a for group in zip(ws, ms, vs, gs) for a in group]
    res = pl.pallas_call(body, out_shape=[sh] * (3 * na), grid=(r // tr,), in_specs=[blk] * (4 * na),
                         out_specs=[blk] * (3 * na), compiler_params=_cparams(("parallel",)), name=name)(*args)
    return [tuple(res[3 * a:3 * a + 3]) for a in range(na)]


def _adamw_small(ws, ms, vs, alls, split, name):
    n = len(ws)
    lead = split if split is not None else ()
    nl = len(lead)
    nslots = alls[0].shape[0]

    def blocks(shape):
        if split is None:
            return tuple(shape), (lambda *g: (0,) * len(shape))
        blk = (shape[0], shape[1] // lead[0], shape[2] // lead[1]) + tuple(shape[3:])
        return blk, (lambda *g: (0, g[0], g[1]) + (0,) * (len(shape) - 3))

    def body(*refs):
        w_refs, m_refs, v_refs, a_refs = (refs[k * n:(k + 1) * n] for k in range(4))
        g_refs, d_refs, nm_refs, nv_refs = (refs[(4 + k) * n:(5 + k) * n] for k in range(4))
        k = pl.program_id(nl)
        for i in range(n):
            @pl.when(k == 0)
            def _(i=i):
                g_refs[i][...] = a_refs[i][0]

            @pl.when(k > 0)
            def _(i=i):
                g_refs[i][...] += a_refs[i][0]

            @pl.when(k == nslots - 1)
            def _(i=i):
                d_refs[i][...], nm_refs[i][...], nv_refs[i][...] = _adamw_math(
                    w_refs[i][...], m_refs[i][...], v_refs[i][...], g_refs[i][...])

    specs, aspecs, shapes = [], [], []
    for wa in ws:
        blk, imap = blocks(wa.shape)
        specs.append(pl.BlockSpec(blk, imap))
        aspecs.append(pl.BlockSpec((1,) + blk, (lambda *g, imap=imap: (g[nl],) + imap(*g))))
        shapes.append(jax.ShapeDtypeStruct(wa.shape, F32))
    res = pl.pallas_call(
        body, out_shape=shapes * 4, grid=tuple(lead) + (nslots,), in_specs=specs * 3 + aspecs,
        out_specs=specs * 4, compiler_params=_cparams(("parallel",) * nl + ("arbitrary",)),
        name=name)(*ws, *ms, *vs, *alls)
    return res[:n], res[n:2 * n], res[2 * n:3 * n], res[3 * n:]


def kernel(x, norm_ffn1, ffn1_w_gate, ffn1_w_up, ffn1_w_down, norm_mix, w_in, attn_sinks, ssm_lambda_re, ssm_lambda_im, ssm_log_dt, ssm_b_re, ssm_b_im, ssm_c_re, ssm_c_im, ssm_d, ssm_glu_w, ssm_glu_b, attn_out_norm, ssm_out_norm, w_out, norm_ffn2, ffn2_w_gate, ffn2_w_up, ffn2_w_down, final_norm, loss_target, m_norm_ffn1, m_ffn1_w_gate, m_ffn1_w_up, m_ffn1_w_down, m_norm_mix, m_w_in, m_attn_sinks, m_ssm_lambda_re, m_ssm_lambda_im, m_ssm_log_dt, m_ssm_b_re, m_ssm_b_im, m_ssm_c_re, m_ssm_c_im, m_ssm_d, m_ssm_glu_w, m_ssm_glu_b, m_attn_out_norm, m_ssm_out_norm, m_w_out, m_norm_ffn2, m_ffn2_w_gate, m_ffn2_w_up, m_ffn2_w_down, m_final_norm, v_norm_ffn1, v_ffn1_w_gate, v_ffn1_w_up, v_ffn1_w_down, v_norm_mix, v_w_in, v_attn_sinks, v_ssm_lambda_re, v_ssm_lambda_im, v_ssm_log_dt, v_ssm_b_re, v_ssm_b_im, v_ssm_c_re, v_ssm_c_im, v_ssm_d, v_ssm_glu_w, v_ssm_glu_b, v_attn_out_norm, v_ssm_out_norm, v_w_out, v_norm_ffn2, v_ffn2_w_gate, v_ffn2_w_up, v_ffn2_w_down, v_final_norm):
    given = dict(locals())
    wts = {n: given[n] for n in WEIGHTS}

    order = [g for g in GROUPS]
    cx, cy = lax.axis_index("x"), lax.axis_index("y")
    slots = jnp.stack([2 * cx + cy, 2 * (1 - cx) + cy, 2 * cx + 1 - cy, 2 * (1 - cx) + 1 - cy,
                       lax.axis_index("c")]).astype(jnp.int32)
    def view(a, n):
        if n in TRANSPOSED:
            return jnp.swapaxes(a[0], 0, 1)
        if n in BIG:
            return a[0]
        if n in ('ssm_b_re', 'ssm_b_im'):
            return jnp.swapaxes(a, -1, -2)
        return a.reshape(1, -1) if a.ndim == 1 else a

    def unview(a, n):
        if n in TRANSPOSED:
            return jnp.swapaxes(a, 0, 1)[None]
        if n in ('ssm_b_re', 'ssm_b_im'):
            return jnp.swapaxes(a, -1, -2)
        return a.reshape(wts[n].shape)

    started, gather_token = {}, None
    for g in order:
        shards = [view(wts[n], n).astype(BF16) for n in GROUPS[g]]
        placed = [_place_own(s, slots, f"weights_place_{n}") for n, s in zip(GROUPS[g], shards)]
        st, gather_token = _exchange_start([(shards, placed)], False, f"weights_start_{g}", after=gather_token)
        started[g] = st[0]

    def get_weights(group, after):
        if group == order[0]:
            after = after + gather_token[:1, :1]
        _, lands = _exchange_wait(started[group], after, False, f"weights_wait_{group}")
        lands = _sibling_forward(lands, f"weights_forward_{group}")
        out = dict(zip(GROUPS[group], lands))
        for n in ('w_in', 'ssm_glu_w', 'w_out'):
            if n in out:
                out[n] = out[n].reshape(-1, out[n].shape[-1])
        return out

    sent, tokens = {}, {}

    def put_grads(group, gd):
        parts = []
        for n in GROUPS[group]:
            g = gd[n]
            if g.ndim == 2:
                g = g.reshape(N_CHIPS, g.shape[0] // N_CHIPS, g.shape[1])
            parts.append(g.astype(BF16))
        sib = _half_swap(parts, f"grads_half_swap_{group}")
        same = len({p.shape for p in parts}) == 1
        batches = [list(range(len(parts)))] if same else [[i] for i in range(len(parts))]
        halves = [None] * len(parts)
        for b in batches:
            res = _half_add([parts[i] for i in b], [sib[i] for i in b], slots, f"grads_half_add_{GROUPS[group][b[0]]}")
            for i, h in zip(b, res):
                halves[i] = h
        parts = halves
        lands = [lax.empty(p.shape, p.dtype) for p in parts]
        started_g, tokens[group] = _exchange_start([(parts, lands)], True, f"grads_start_{group}")
        sent[group] = started_g[0]
        return tokens[group]

    w = {n: (wts[n][0] if wts[n].ndim > 1 else wts[n]) for n in SMALL}
    w['norm_ffn1'], w['norm_mix'], w['norm_ffn2'] = wts['norm_ffn1'], wts['norm_mix'], wts['norm_ffn2']
    w['ssm_b_re'], w['ssm_b_im'] = view(wts['ssm_b_re'], 'ssm_b_re')[0], view(wts['ssm_b_im'], 'ssm_b_im')[0]
    w['ssm_log_dt'] = w['ssm_log_dt'] + gather_token[0, 0]
    wide =['ssm_b_re', 'ssm_b_im', 'ssm_c_re', 'ssm_c_im']

    def reduce_wide(gd):
        packed = jnp.concatenate([gd[n].reshape(-1, LANES) for n in wide])
        return _small_allreduce(packed, "small_grads_allreduce")

    loss_row, dx, grads, wide_sum = _local_step(x[0], loss_target[0], w, get_weights, put_grads, reduce_wide)

    out_g, out_d, out_m, out_v = {}, {}, {}, {}

    def finish(group, after):
        names = GROUPS[group]
        parts, recv = _exchange_wait(sent[group], after, True, f"grads_wait_{group}")
        same = len({p.shape for p in parts}) == 1
        batches = [list(range(len(names)))] if same else [[i] for i in range(len(names))]
        sums = [None] * len(names)
        for b in batches:
            res = _sum_parts([parts[i] for i in b], [recv[i] for i in b], slots, f"grad_sum_{names[b[0]]}")
            for i, sm in zip(b, res):
                sums[i] = sm
        full = _half_forward(sums, f"grad_half_forward_{group}")
        for b in batches:
            res = _adamw([view(wts[names[i]], names[i]) for i in b], [view(given['m_' + names[i]], names[i]) for i in b],
                         [view(given['v_' + names[i]], names[i]) for i in b], [full[i] for i in b],
                         f"adamw_{names[b[0]]}")
            for i, (d, nm, nv) in zip(b, res):
                n = names[i]
                out_g[n], out_d[n], out_m[n], out_v[n] = (unview(a, n) for a in (full[i], d, nm, nv))
        return nv

    done = finish('ffn2', tokens['ffn1'])
    done = finish('mix', done)

    nat = {n: view(wts[n], n).shape for n in SMALL}
    narrow = [n for n in SMALL if n not in wide]
    alls = list(_small_exchange([grads[n].reshape(nat[n]) for n in narrow] + [loss_row], "small_grads_allgather"))
    loss = jnp.sum(alls.pop()[:, 0, 0])
    rows = wide_sum.shape[0] // len(wide)
    wide_g = [wide_sum[i * rows:(i + 1) * rows].reshape((1,) + nat[n]) for i, n in enumerate(wide)]
    for group, gs, split, tag in ((narrow, alls, None, "adamw_small"), (wide, wide_g, (2, 4), "adamw_ssm_bc")):
        res = _adamw_small([view(wts[n], n) for n in group], [view(given['m_' + n], n) for n in group],
                           [view(given['v_' + n], n) for n in group], gs, split, tag)
        for dst, vals in zip((out_g, out_d, out_m, out_v), res):
            for n, a in zip(group, vals):
                dst[n] = unview(a, n)

    finish('ffn1', out_v['norm_ffn1'][:, :1] + out_v['ssm_c_re'].reshape(1, -1)[:, :1] + done[:1, :1] + loss)

    return (loss, dx[None], *[out_g[n] for n in WEIGHTS], *[out_d[n] for n in WEIGHTS],
            *[out_m[n] for n in WEIGHTS], *[out_v[n] for n in WEIGHTS])
```

```python
import functools
import math

import numpy as np
import jax
import jax.numpy as jnp
from jax import lax
from jax.experimental import pallas as pl
from jax.experimental.pallas import tpu as pltpu

F32 = jnp.float32
BF16 = jnp.bfloat16
MESH = pl.DeviceIdType.MESH

EPS = 1e-6
NEG_INF = -1e30
LAMBDA_RE_MAX = -1e-4
ATTN_HEADS = 8
KV_HEADS = 2
GQ = ATTN_HEADS // KV_HEADS
HEAD_DIM = 64
ATTN_WIDTH = 512
KV_WIDTH = 128
WINDOW = 128
QBLOCK = 128
SSM_WIDTH = 512
SSM_GROUPS = 32
SSM_CH = 16
SSM_STATE = 64
N_STRIPS = 4
STRIP_IN = SSM_WIDTH // N_STRIPS
STRIP_ST = SSM_GROUPS * SSM_STATE // N_STRIPS
SUBLANES = 8
LANES = 128
N_CHIPS = 4
N_DEV = 8

ADAM_LR = 0.001
ADAM_B1 = 0.9
ADAM_B2 = 0.999
ADAM_EPS = 1e-08
ADAM_WD = 0.01
ADAM_STEP = 10

VMEM_LIMIT = 48 * 1024 * 1024

WEIGHTS = ['norm_ffn1', 'ffn1_w_gate', 'ffn1_w_up', 'ffn1_w_down', 'norm_mix', 'w_in', 'attn_sinks',
           'ssm_lambda_re', 'ssm_lambda_im', 'ssm_log_dt', 'ssm_b_re', 'ssm_b_im', 'ssm_c_re', 'ssm_c_im',
           'ssm_d', 'ssm_glu_w', 'ssm_glu_b', 'attn_out_norm', 'ssm_out_norm', 'w_out', 'norm_ffn2',
           'ffn2_w_gate', 'ffn2_w_up', 'ffn2_w_down', 'final_norm']
BIG = ['ffn1_w_gate', 'ffn1_w_up', 'ffn1_w_down', 'w_in', 'ssm_glu_w', 'w_out',
       'ffn2_w_gate', 'ffn2_w_up', 'ffn2_w_down']
SMALL = [n for n in WEIGHTS if n not in BIG]
TRANSPOSED = ['ffn1_w_gate', 'ffn1_w_up', 'w_in', 'ffn2_w_gate', 'ffn2_w_up']
GROUPS = {'ffn1': ['ffn1_w_gate', 'ffn1_w_up', 'ffn1_w_down'],
          'mix': ['w_in', 'ssm_glu_w', 'w_out'],
          'ffn2': ['ffn2_w_gate', 'ffn2_w_up', 'ffn2_w_down']}


def _cparams(sem=None):
    return pltpu.CompilerParams(dimension_semantics=sem, vmem_limit_bytes=VMEM_LIMIT)


def _tile(n, pref):
    if n <= pref:
        return n
    for t in (pref, pref // 2, pref // 4):
        if t % LANES == 0 and n % t == 0:
            return t
    return n


def _sigmoid(x):
    return 1.0 / (1.0 + jnp.exp(-x))


def _sigmoid_tanh(x):
    return 0.5 * jnp.tanh(0.5 * x) + 0.5


def _mm(a, b, *, ta=False, tb=False, reduce_s=False, res=None, scale=1.0, out_dtype=F32, after=None, name):
    a3 = a if a.ndim == 3 else a[None]
    b3 = b if b.ndim == 3 else b[None]
    sa, sb = a3.shape[0], b3.shape[0]
    ns = max(sa, sb)
    (kk, m) = a3.shape[1:] if ta else a3.shape[1:][::-1]
    (n, kb) = b3.shape[1:] if tb else b3.shape[1:][::-1]
    assert kk == kb, (a3.shape, b3.shape)
    tm, tn, tk = _tile(m, 1024), _tile(n, 1024), _tile(kk, 2048)
    nm, nn, nk = m // tm, n // tn, kk // tk
    has_res = res is not None
    single = nk == 1 and not (reduce_s and ns > 1)

    if reduce_s:
        grid = (nm, nn, ns, nk)
        ids = lambda i, j, s, k: (s, i, j, k)
        sem = ("parallel", "parallel", "arbitrary", "arbitrary")
    else:
        grid = (ns, nm, nn, nk)
        ids = lambda s, i, j, k: (s, i, j, k)
        sem = ("parallel", "parallel", "parallel", "arbitrary")

    def a_map(*g):
        s, i, j, k = ids(*g)
        s = s if sa > 1 else 0
        return (s, k, i) if ta else (s, i, k)

    def b_map(*g):
        s, i, j, k = ids(*g)
        s = s if sb > 1 else 0
        return (s, j, k) if tb else (s, k, j)

    def o_map(*g):
        s, i, j, k = ids(*g)
        return (i, j) if reduce_s else (s, i, j)

    a_blk = (1, tk, tm) if ta else (1, tm, tk)
    b_blk = (1, tn, tk) if tb else (1, tk, tn)
    dims = (((0 if ta else 1,), (1 if tb else 0,)), ((), ()))

    def body(*refs):
        a_ref, b_ref = refs[0], refs[1]
        r_ref = refs[2] if has_res else None
        o_ref = refs[2 + has_res + (after is not None)]
        acc_ref = None if single else refs[-1]
        s, _, _, k = ids(*[pl.program_id(d) for d in range(4)])
        prod = lax.dot_general(a_ref[0].astype(BF16), b_ref[0].astype(BF16), dims, preferred_element_type=F32)

        def finish(out):
            if scale != 1.0:
                out = out * scale
            if has_res:
                out = r_ref[...].reshape(out.shape) + out
            o_ref[...] = out.astype(out_dtype).reshape(o_ref.shape)

        if single:
            finish(prod)
            return
        if reduce_s:
            first = jnp.logical_and(s == 0, k == 0)
            last = jnp.logical_and(s == ns - 1, k == nk - 1)
        else:
            first, last = k == 0, k == nk - 1

        acc_ref[...] = prod + jnp.where(first, 0.0, acc_ref[...])

        @pl.when(last)
        def _():
            finish(acc_ref[...])

    in_specs = [pl.BlockSpec(a_blk, a_map), pl.BlockSpec(b_blk, b_map)]
    args = [a3, b3]
    if reduce_s:
        out_shape = jax.ShapeDtypeStruct((m, n), out_dtype)
        o_spec = pl.BlockSpec((tm, tn), o_map)
    else:
        out_shape = jax.ShapeDtypeStruct((ns, m, n), out_dtype)
        o_spec = pl.BlockSpec((1, tm, tn), o_map)
    if has_res:
        assert res.shape == out_shape.shape
        in_specs.append(o_spec)
        args.append(res)
    if after is not None:
        in_specs.append(HBM_SPEC)
        args.append(after)
    return pl.pallas_call(body, out_shape=out_shape, grid=grid, in_specs=in_specs, out_specs=o_spec,
                          scratch_shapes=[] if single else [pltpu.VMEM((tm, tn), F32)],
                          compiler_params=_cparams(sem), name=name)(*args)


def _row_tile(t, cap=256):
    for step in (16, SUBLANES):
        for tr in range(min(cap, t) // step * step, 0, -step):
            if t % tr == 0:
                return tr
    return t


def _rms_fwd(x, g, name):
    t, w = x.shape
    tr = _row_tile(t)

    def body(x_ref, g_ref, o_ref):
        xv = x_ref[...]
        r = lax.rsqrt(jnp.mean(xv * xv, axis=-1, keepdims=True) + EPS)
        o_ref[...] = (xv * r * g_ref[...]).astype(BF16)

    return pl.pallas_call(
        body, out_shape=jax.ShapeDtypeStruct((t, w), BF16), grid=(t // tr,),
        in_specs=[pl.BlockSpec((tr, w), lambda i: (i, 0)), pl.BlockSpec((1, w), lambda i: (0, 0))],
        out_specs=pl.BlockSpec((tr, w), lambda i: (i, 0)), compiler_params=_cparams(("parallel",)),
        name=name)(x, g)


def _rms_bwd_rows(xv, gv, dhv):
    r = lax.rsqrt(jnp.mean(xv * xv, axis=-1, keepdims=True) + EPS)
    nrm = xv * r
    dn = dhv * gv
    return r * (dn - nrm * jnp.mean(dn * nrm, axis=-1, keepdims=True)), dhv * nrm


def _rms_bwd(x, g, dh, dres, name):
    t, w = x.shape
    tr = _row_tile(t)
    has_res = dres is not None

    def body(*refs):
        if has_res:
            x_ref, g_ref, dh_ref, dr_ref, dx_ref, dxb_ref, dg_ref = refs
        else:
            x_ref, g_ref, dh_ref, dx_ref, dxb_ref, dg_ref = refs
        dx, dgs = _rms_bwd_rows(x_ref[...], g_ref[...], dh_ref[...])
        if has_res:
            dx = dx + dr_ref[...]
        dx_ref[...] = dx
        dxb_ref[...] = dx.astype(BF16)

        @pl.when(pl.program_id(0) == 0)
        def _():
            dg_ref[...] = jnp.zeros_like(dg_ref)

        dg_ref[...] += jnp.sum(dgs, axis=0, keepdims=True)

    row = pl.BlockSpec((tr, w), lambda i: (i, 0))
    vec = pl.BlockSpec((1, w), lambda i: (0, 0))
    ins = [x, g, dh] + ([dres] if has_res else [])
    return pl.pallas_call(
        body, out_shape=(jax.ShapeDtypeStruct((t, w), F32), jax.ShapeDtypeStruct((t, w), BF16),
                         jax.ShapeDtypeStruct((1, w), F32)),
        grid=(t // tr,), in_specs=[row, vec, row] + ([row] if has_res else []),
        out_specs=(row, row, vec), compiler_params=_cparams(("arbitrary",)), name=name)(*ins)


FFN_ROWS = 512
FFN_SPLIT = 2
FFN_W_ROWS = 1024
SCAN_ROWS = 256


NT_DIMS = (((1,), (1,)), ((), ()))
TN_DIMS = (((0,), (0,)), ((), ()))


def _ffn_fwd_call(x, g, wg, wu, wd, name):
    t, d = x.shape
    ns, f, _ = wg.shape
    tm = _tile(t, FFN_ROWS)

    def body(x_ref, g_ref, wg_ref, wu_ref, wd_ref, xo_ref, h_ref, gate_ref, up_ref, h_sc, acc_ref):
        s = pl.program_id(1)

        @pl.when(s == 0)
        def _():
            xv = x_ref[...]
            r = lax.rsqrt(jnp.mean(xv * xv, axis=-1, keepdims=True) + EPS)
            hb = (xv * r * g_ref[...]).astype(BF16)
            h_sc[...] = hb
            h_ref[...] = hb

        for r0 in range(0, tm, tm // FFN_SPLIT):
            rows = slice(r0, r0 + tm // FFN_SPLIT)
            hb = h_sc[rows, :]
            gate = lax.dot_general(hb, wg_ref[0], NT_DIMS, preferred_element_type=F32)
            up = lax.dot_general(hb, wu_ref[0], NT_DIMS, preferred_element_type=F32)
            gate_ref[0, rows, :] = gate.astype(BF16)
            up_ref[0, rows, :] = up.astype(BF16)
            act = (gate * _sigmoid_tanh(gate) * up).astype(BF16)
            prod = jnp.dot(act, wd_ref[0], preferred_element_type=F32)
            acc_ref[rows, :] = prod + jnp.where(s > 0, acc_ref[rows, :], 0.0)

        @pl.when(s == ns - 1)
        def _():
            xo_ref[...] = x_ref[...] + 0.5 * acc_ref[...]

    row = pl.BlockSpec((tm, d), lambda i, s: (i, 0))
    vec = pl.BlockSpec((1, d), lambda i, s: (0, 0))
    wrow = pl.BlockSpec((1, f, d), lambda i, s: (s, 0, 0))
    hid = pl.BlockSpec((1, tm, f), lambda i, s: (s, i, 0))
    hid_sh = jax.ShapeDtypeStruct((ns, t, f), BF16)
    return pl.pallas_call(
        body, out_shape=(jax.ShapeDtypeStruct((t, d), F32), jax.ShapeDtypeStruct((t, d), BF16), hid_sh, hid_sh),
        grid=(t // tm, ns), in_specs=[row, vec, wrow, wrow, wrow], out_specs=(row, row, hid, hid),
        scratch_shapes=[pltpu.VMEM((tm, d), BF16), pltpu.VMEM((tm, d), F32)],
        compiler_params=_cparams(("parallel", "arbitrary")), name=name)(x, g, wg, wu, wd)


def _ffn_bwd_x_call(dxo, dxo_b, x, g, gate, up, wg, wu, wd, name):
    t, d = x.shape
    ns, f, _ = wg.shape
    tm = _tile(t, FFN_ROWS)

    def body(dxo_ref, dxb_ref, x_ref, g_ref, gate_ref, up_ref, wg_ref, wu_ref, wd_ref,
             dx_ref, dxob_ref, dgn_ref, dgate_ref, dup_ref, act_ref, dh_ref):
        i, s = pl.program_id(0), pl.program_id(1)
        for r0 in range(0, tm, tm // FFN_SPLIT):
            rows = slice(r0, r0 + tm // FFN_SPLIT)
            dact = lax.dot_general(dxb_ref[rows, :], wd_ref[0], NT_DIMS, preferred_element_type=F32) * 0.5
            gv = gate_ref[0, rows, :].astype(F32)
            uv = up_ref[0, rows, :].astype(F32)
            sg = _sigmoid_tanh(gv)
            silu = gv * sg
            act_ref[0, rows, :] = (silu * uv).astype(BF16)
            dub = (dact * silu).astype(BF16)
            dgb = (dact * uv * sg * (1.0 + gv * (1.0 - sg))).astype(BF16)
            dup_ref[0, rows, :] = dub
            dgate_ref[0, rows, :] = dgb
            prod = (jnp.dot(dgb, wg_ref[0], preferred_element_type=F32)
                    + jnp.dot(dub, wu_ref[0], preferred_element_type=F32))

            dh_ref[rows, :] = prod + jnp.where(s > 0, dh_ref[rows, :], 0.0)

        @pl.when(jnp.logical_and(i == 0, s == 0))
        def _():
            dgn_ref[...] = jnp.zeros_like(dgn_ref)

        @pl.when(s == ns - 1)
        def _():
            dx, dgs = _rms_bwd_rows(x_ref[...], g_ref[...], dh_ref[...])
            dx = dx + dxo_ref[...]
            dx_ref[...] = dx
            dxob_ref[...] = dx.astype(BF16)
            dgn_ref[...] += jnp.sum(dgs, axis=0, keepdims=True)

    row = pl.BlockSpec((tm, d), lambda i, s: (i, 0))
    vec = pl.BlockSpec((1, d), lambda i, s: (0, 0))
    wrow = pl.BlockSpec((1, f, d), lambda i, s: (s, 0, 0))
    hid = pl.BlockSpec((1, tm, f), lambda i, s: (s, i, 0))
    hid_sh = jax.ShapeDtypeStruct((ns, t, f), BF16)
    return pl.pallas_call(
        body,
        out_shape=(jax.ShapeDtypeStruct((t, d), F32), jax.ShapeDtypeStruct((t, d), BF16),
                   jax.ShapeDtypeStruct((1, d), F32), hid_sh, hid_sh, hid_sh),
        grid=(t // tm, ns), in_specs=[row, row, row, vec, hid, hid, wrow, wrow, wrow],
        out_specs=(row, row, vec, hid, hid, hid), scratch_shapes=[pltpu.VMEM((tm, d), F32)],
        compiler_params=_cparams(("arbitrary", "arbitrary")), name=name)(dxo, dxo_b, x, g, gate, up, wg, wu, wd)


def _ffn_bwd_w_call(h, dxo_b, dgate, dup, act, name):
    t, d = h.shape
    ns, _, f = dgate.shape
    tm = _tile(t, FFN_W_ROWS)
    nm = t // tm

    def body(h_ref, dxb_ref, dgate_ref, dup_ref, act_ref, dwg_ref, dwu_ref, dwd_ref, ag_ref, au_ref, ad_ref):
        i = pl.program_id(1)
        hv = h_ref[...]
        pg = lax.dot_general(dgate_ref[0], hv, TN_DIMS, preferred_element_type=F32)
        pu = lax.dot_general(dup_ref[0], hv, TN_DIMS, preferred_element_type=F32)
        pd = lax.dot_general(act_ref[0], dxb_ref[...], TN_DIMS, preferred_element_type=F32)

        ag_ref[...] = pg + jnp.where(i > 0, ag_ref[...], 0.0)
        au_ref[...] = pu + jnp.where(i > 0, au_ref[...], 0.0)
        ad_ref[...] = pd + jnp.where(i > 0, ad_ref[...], 0.0)

        @pl.when(i == nm - 1)
        def _():
            dwg_ref[0] = ag_ref[...].astype(BF16)
            dwu_ref[0] = au_ref[...].astype(BF16)
            dwd_ref[0] = (0.5 * ad_ref[...]).astype(BF16)

    row = pl.BlockSpec((tm, d), lambda s, i: (i, 0))
    hid = pl.BlockSpec((1, tm, f), lambda s, i: (s, i, 0))
    wrow = pl.BlockSpec((1, f, d), lambda s, i: (s, 0, 0))
    wsh = jax.ShapeDtypeStruct((ns, f, d), BF16)
    return pl.pallas_call(
        body, out_shape=(wsh, wsh, wsh),
        grid=(ns, nm), in_specs=[row, row, hid, hid, hid], out_specs=(wrow, wrow, wrow),
        scratch_shapes=[pltpu.VMEM((f, d), F32), pltpu.VMEM((f, d), F32), pltpu.VMEM((f, d), F32)],
        compiler_params=_cparams(("parallel", "arbitrary")), name=name)(h, dxo_b, dgate, dup, act)


def _loss_head(x, g, tgt, name):
    t, w = x.shape
    tr = _row_tile(t)

    def body(x_ref, g_ref, t_ref, loss_ref, dx_ref, dxb_ref, dg_ref):
        xv = x_ref[...]
        gv = g_ref[...]
        r = lax.rsqrt(jnp.mean(xv * xv, axis=-1, keepdims=True) + EPS)
        nrm = xv * r
        err = nrm * gv - t_ref[...]
        dout = err * (1.0 / w)
        dn = dout * gv
        dx = r * (dn - nrm * jnp.mean(dn * nrm, axis=-1, keepdims=True))
        dx_ref[...] = dx
        dxb_ref[...] = dx.astype(BF16)

        @pl.when(pl.program_id(0) == 0)
        def _():
            dg_ref[...] = jnp.zeros_like(dg_ref)
            loss_ref[...] = jnp.zeros_like(loss_ref)

        dg_ref[...] += jnp.sum(dout * nrm, axis=0, keepdims=True)
        part = jnp.sum(jnp.sum(err * err, axis=-1, keepdims=True) * (0.5 / w), axis=0, keepdims=True)
        loss_ref[...] += jnp.broadcast_to(part, loss_ref.shape)

    row = pl.BlockSpec((tr, w), lambda i: (i, 0))
    vec = pl.BlockSpec((1, w), lambda i: (0, 0))
    return pl.pallas_call(
        body, out_shape=(jax.ShapeDtypeStruct((1, LANES), F32), jax.ShapeDtypeStruct((t, w), F32),
                         jax.ShapeDtypeStruct((t, w), BF16), jax.ShapeDtypeStruct((1, w), F32)),
        grid=(t // tr,), in_specs=[row, vec, row],
        out_specs=(pl.BlockSpec((1, LANES), lambda i: (0, 0)), row, row, vec),
        compiler_params=_cparams(("arbitrary",)), name=name)(x, g, tgt)


def _attn_bias():
    slopes = np.asarray(2.0 ** (-8.0 * (np.arange(ATTN_HEADS) + 1) / ATTN_HEADS), np.float32)
    qi = np.arange(QBLOCK)[:, None]
    kj = np.arange(3 * QBLOCK)[None, :]
    rel = np.abs(kj - QBLOCK - qi).astype(np.float32)
    tile = np.where(rel <= WINDOW, -slopes[:, None, None] * rel[None], np.float32(NEG_INF)).astype(np.float32)
    tile = tile.reshape(KV_HEADS, GQ * QBLOCK, 3 * QBLOCK)
    return jnp.asarray(np.swapaxes(tile, 1, 2))


def _attn_scores(k3, q, n, nb, bias):
    s = lax.dot_general(k3, q, NT_DIMS, preferred_element_type=F32) * (HEAD_DIM ** -0.5)
    key = lax.broadcasted_iota(jnp.int32, (3 * QBLOCK, 1), 0)
    inside = (key >= jnp.where(n == 0, QBLOCK, 0)) & (key < jnp.where(n == nb - 1, 2 * QBLOCK, 3 * QBLOCK))
    return jnp.where(inside, s + bias, NEG_INF)


Q_COL, K_COL, V_COL, U_COL = 0, ATTN_WIDTH // LANES, ATTN_WIDTH // LANES + 1, ATTN_WIDTH // LANES + 2


def _key_rows(ref, n, nb):
    prev, nxt = jnp.maximum(n - 1, 0), jnp.minimum(n + 1, nb - 1)
    blk = lambda b: ref[pl.ds(pl.multiple_of(b * QBLOCK, QBLOCK), QBLOCK), :]
    return jnp.concatenate([blk(prev), blk(n), blk(nxt)], axis=0)


def _head_tiles(x, kh, low):
    tiles = []
    for g in range(GQ):
        h = GQ * kh + g
        t128 = x[:, LANES * (h // 2):LANES * (h // 2 + 1)]
        t128 = jnp.where(low if h % 2 == 0 else jnp.logical_not(low), t128, 0.0)
        if h % 2 != kh:
            t128 = pltpu.roll(t128, HEAD_DIM, 1)
        tiles.append(t128)
    return jnp.concatenate(tiles, axis=0)


def _head_merge(per_kh, low):
    out = []
    for j in range(ATTN_HEADS // 2):
        pair = []
        for h in (2 * j, 2 * j + 1):
            kh, g = h // GQ, h % GQ
            t128 = per_kh[kh][g * QBLOCK:(g + 1) * QBLOCK, :]
            if h % 2 != kh:
                t128 = pltpu.roll(t128, HEAD_DIM, 1)
            pair.append(t128)
        out.append(jnp.where(low, pair[0], pair[1]))
    return jnp.concatenate(out, axis=1)


def _attn_fwd_proj(proj, sink_rows, bias, name):
    t = proj.shape[0]
    nb = t // QBLOCK
    rows = GQ * QBLOCK

    def body(q_ref, k_ref, v_ref, sink_ref, bias_ref, o_ref, lse_ref):
        n = pl.program_id(0)
        low = lax.broadcasted_iota(jnp.int32, (QBLOCK, LANES), 1) < HEAD_DIM
        k3 = _key_rows(k_ref, n, nb).astype(BF16)
        v3 = _key_rows(v_ref, n, nb).astype(BF16)
        q = q_ref[...]
        outs = []
        for kh in range(KV_HEADS):
            qs = _head_tiles(q, kh, low).astype(BF16)
            s = _attn_scores(k3, qs, n, nb, bias_ref[kh])
            sink = sink_ref[kh]
            mx = jnp.maximum(jnp.max(s, axis=0, keepdims=True), sink)
            p = jnp.exp(s - mx)
            den = jnp.sum(p, axis=0, keepdims=True) + jnp.exp(sink - mx)
            pn = (p * (1.0 / den)).astype(BF16)
            outs.append(lax.dot_general(pn, v3, TN_DIMS, preferred_element_type=F32))
            lse_ref[0, kh] = mx + jnp.log(den)
        o_ref[...] = _head_merge(outs, low)

    strip = lambda col: pl.BlockSpec((t, LANES), lambda n, col=col: (0, col))
    rowspec = pl.BlockSpec((KV_HEADS, 1, rows), lambda n: (0, 0, 0))
    biasspec = pl.BlockSpec((KV_HEADS, 3 * QBLOCK, rows), lambda n: (0, 0, 0))
    return pl.pallas_call(
        body, out_shape=(jax.ShapeDtypeStruct((t, ATTN_WIDTH), F32), jax.ShapeDtypeStruct((nb, KV_HEADS, 1, rows), F32)),
        grid=(nb,), in_specs=[pl.BlockSpec((QBLOCK, ATTN_WIDTH), lambda n: (n, 0)), strip(K_COL), strip(V_COL),
                              rowspec, biasspec],
        out_specs=(pl.BlockSpec((QBLOCK, ATTN_WIDTH), lambda n: (n, 0)),
                   pl.BlockSpec((1, KV_HEADS, 1, rows), lambda n: (n, 0, 0, 0))),
        compiler_params=_cparams(("parallel",)), name=name)(proj, proj, proj, sink_rows, bias)


def _attn_bwd_proj(proj, sink_rows, bias, o, lse, do, name):
    t = proj.shape[0]
    nb = t // QBLOCK
    rows = GQ * QBLOCK
    scale = HEAD_DIM ** -0.5

    def body(q_ref, k_ref, v_ref, sink_ref, bias_ref, o_ref, lse_ref, do_ref, dq_ref, dk_ref, dv_ref, ds_ref):
        n = pl.program_id(0)

        @pl.when(n == 0)
        def _():
            dk_ref[...] = jnp.zeros_like(dk_ref)
            dv_ref[...] = jnp.zeros_like(dv_ref)
            ds_ref[...] = jnp.zeros_like(ds_ref)

        low = lax.broadcasted_iota(jnp.int32, (QBLOCK, LANES), 1) < HEAD_DIM
        k3 = _key_rows(k_ref, n, nb).astype(BF16)
        v3 = _key_rows(v_ref, n, nb).astype(BF16)
        q, dov = q_ref[...], do_ref[...]
        dod = dov * o_ref[...]
        dqs = []
        dk3 = jnp.zeros((3 * QBLOCK, LANES), F32)
        dv3 = jnp.zeros((3 * QBLOCK, LANES), F32)
        ones = jnp.ones((SUBLANES, LANES), F32)
        for kh in range(KV_HEADS):
            qs = _head_tiles(q, kh, low).astype(BF16)
            dos = _head_tiles(dov, kh, low).astype(BF16)
            delta = lax.dot_general(ones, _head_tiles(dod, kh, low), NT_DIMS, preferred_element_type=F32,
                                    precision=lax.Precision.HIGHEST)[0:1, :]
            lse_kh = lse_ref[0, kh]
            s = _attn_scores(k3, qs, n, nb, bias_ref[kh])
            p = jnp.exp(s - lse_kh)
            dp = lax.dot_general(v3, dos, NT_DIMS, preferred_element_type=F32)
            dsb = (p * (dp - delta)).astype(BF16)
            dqs.append(lax.dot_general(dsb, k3, TN_DIMS, preferred_element_type=F32) * scale)
            dk3 = dk3 + jnp.dot(dsb, qs, preferred_element_type=F32) * scale
            dv3 = dv3 + jnp.dot(p.astype(BF16), dos, preferred_element_type=F32)
            ds_ref[kh] += -jnp.exp(sink_ref[kh] - lse_kh) * delta
        dq_ref[...] = _head_merge(dqs, low)
        prev, nxt = jnp.maximum(n - 1, 0), jnp.minimum(n + 1, nb - 1)
        for j, b in enumerate((prev, n, nxt)):
            blk = pl.ds(pl.multiple_of(b * QBLOCK, QBLOCK), QBLOCK)
            dk_ref[blk, :] += dk3[j * QBLOCK:(j + 1) * QBLOCK, :]
            dv_ref[blk, :] += dv3[j * QBLOCK:(j + 1) * QBLOCK, :]

    strip = lambda col: pl.BlockSpec((t, LANES), lambda n, col=col: (0, col))
    rowspec = pl.BlockSpec((KV_HEADS, 1, rows), lambda n: (0, 0, 0))
    qspec = pl.BlockSpec((QBLOCK, ATTN_WIDTH), lambda n: (n, 0))
    kv_out = pl.BlockSpec((t, LANES), lambda n: (0, 0))
    biasspec = pl.BlockSpec((KV_HEADS, 3 * QBLOCK, rows), lambda n: (0, 0, 0))
    return pl.pallas_call(
        body,
        out_shape=(jax.ShapeDtypeStruct((t, ATTN_WIDTH), F32), jax.ShapeDtypeStruct((t, LANES), F32),
                   jax.ShapeDtypeStruct((t, LANES), F32), jax.ShapeDtypeStruct((KV_HEADS, 1, rows), F32)),
        grid=(nb,),
        in_specs=[qspec, strip(K_COL), strip(V_COL), rowspec, biasspec, qspec,
                  pl.BlockSpec((1, KV_HEADS, 1, rows), lambda n: (n, 0, 0, 0)), qspec],
        out_specs=(qspec, kv_out, kv_out, rowspec),
        compiler_params=_cparams(("arbitrary",)), name=name)(proj, proj, proj, sink_rows, bias, o, lse, do)


def _scan_tables(a_re, a_im, reverse):
    pw = [(a_re, a_im)]
    for _ in range(SUBLANES - 1):
        pr, pi = pw[-1]
        pw.append((pr * a_re - pi * a_im, pr * a_im + pi * a_re))
    rows = np.arange(SUBLANES)
    tabs = []
    for d in (1, 2, 4):
        mask = (rows <= SUBLANES - 1 - d) if reverse else (rows >= d)
        m = jnp.asarray(mask, F32)[:, None]
        tabs += [m * pw[d - 1][0][None, :], m * pw[d - 1][1][None, :]]
    order = (SUBLANES - 1 - rows) if reverse else rows
    tabs += [jnp.stack([pw[j][0] for j in order]), jnp.stack([pw[j][1] for j in order])]
    tab = jnp.stack(tabs)
    return tab.reshape(8, SUBLANES, N_STRIPS, STRIP_ST).transpose(2, 0, 1, 3)


def _scan_pair_chunk(dirs):
    nblk = dirs[0]['xr'].shape[0] // SUBLANES

    @pl.when(pl.program_id(1) == 0)
    def _():
        for d in dirs:
            d['carry'][...] = jnp.zeros_like(d['carry'])

    for d in dirs:
        vb = d['v'][...].astype(BF16)
        d['xr'][...] = jnp.dot(vb, d['mir'][0], preferred_element_type=F32)
        d['xi'][...] = jnp.dot(vb, d['mii'][0], preferred_element_type=F32)
    carries = [(d['carry'][0], d['carry'][1]) for d in dirs]
    for i in range(nblk):
        for k, d in enumerate(dirs):
            rev = d['reverse']
            rows = pl.ds(((nblk - 1 - i) if rev else i) * SUBLANES, SUBLANES)
            cr, ci = carries[k]
            xr, xi = d['xr'][rows, :], d['xi'][rows, :]
            for j, s in enumerate((1, 2, 4)):
                tr_, ti_ = d['tab'][0, 2 * j], d['tab'][0, 2 * j + 1]
                sh = (SUBLANES - s) if rev else s
                sr, si = pltpu.roll(xr, sh, 0), pltpu.roll(xi, sh, 0)
                xr, xi = xr + tr_ * sr - ti_ * si, xi + tr_ * si + ti_ * sr
            pr, pi = d['tab'][0, 6], d['tab'][0, 7]
            xr, xi = xr + pr * cr - pi * ci, xi + pr * ci + pi * cr
            d['xr'][rows, :] = xr
            d['xi'][rows, :] = xi
            edge = 0 if rev else SUBLANES - 1
            carries[k] = (jnp.broadcast_to(xr[edge:edge + 1, :], xr.shape),
                          jnp.broadcast_to(xi[edge:edge + 1, :], xi.shape))
    for k, d in enumerate(dirs):
        d['carry'][0], d['carry'][1] = carries[k]
        d['y'][...] = (jnp.dot(d['xr'][...].astype(BF16), d['mor'][0], preferred_element_type=F32)
                       + jnp.dot(d['xi'][...].astype(BF16), d['moi'][0], preferred_element_type=F32))


def _scan_pair(v, ops_f, ops_b, name):
    t = v.shape[0]
    tc = _tile(t, SCAN_ROWS)
    nc = t // tc

    def body(vf_ref, vb_ref, *refs):
        ops = refs[:10]
        outs = refs[10:16]
        carries = refs[16:18]
        dirs = []
        for k, (v_ref, rev) in enumerate(((vf_ref, False), (vb_ref, True))):
            mir, mii, tab, mor, moi = ops[5 * k:5 * k + 5]
            y, xr, xi = outs[3 * k:3 * k + 3]
            dirs.append(dict(v=v_ref, mir=mir, mii=mii, tab=tab, mor=mor, moi=moi, y=y, xr=xr, xi=xi,
                             carry=carries[k], reverse=rev))
        _scan_pair_chunk(dirs)

    col0 = v.shape[1] // STRIP_IN - N_STRIPS
    fmap = lambda s, c: (c, s)
    bmap = lambda s, c: (nc - 1 - c, s)
    smap3 = lambda s, c: (s, 0, 0)
    m_in = pl.BlockSpec((1, STRIP_IN, STRIP_ST), smap3)
    m_out = pl.BlockSpec((1, STRIP_ST, STRIP_IN), smap3)
    tabspec = pl.BlockSpec((1, 8, SUBLANES, STRIP_ST), lambda s, c: (s, 0, 0, 0))
    opspecs = [m_in, m_in, tabspec, m_out, m_out]
    y_sh = jax.ShapeDtypeStruct((t, SSM_WIDTH), F32)
    x_sh = jax.ShapeDtypeStruct((t, N_STRIPS * STRIP_ST), F32)
    outspecs = lambda m: [pl.BlockSpec((tc, STRIP_IN), m), pl.BlockSpec((tc, STRIP_ST), m),
                          pl.BlockSpec((tc, STRIP_ST), m)]
    res = pl.pallas_call(
        body, out_shape=[y_sh, x_sh, x_sh] * 2, grid=(N_STRIPS, nc),
        in_specs=[pl.BlockSpec((tc, STRIP_IN), lambda s, c: (c, s + col0)),
                  pl.BlockSpec((tc, STRIP_IN), lambda s, c: (nc - 1 - c, s + col0))] + opspecs * 2,
        out_specs=outspecs(fmap) + outspecs(bmap),
        scratch_shapes=[pltpu.VMEM((2, SUBLANES, STRIP_ST), F32)] * 2,
        compiler_params=_cparams(("parallel", "arbitrary")), name=name)(v, v, *ops_f, *ops_b)
    return tuple(res[:3]), tuple(res[3:])


def _scan_adjoint_pair(dy, u, states, adj_ops, name):
    t = dy.shape[0]
    tc = _tile(t, SCAN_ROWS)
    nc = t // tc
    hb = tc // SUBLANES
    n_out = 6

    def body(*refs):
        c = pl.program_id(1)
        dirs = []
        for k in range(2):
            dy_ref, mir, mii, tab, mor, moi, u_ref, xr_ref, xi_ref, hr_ref, hi_ref = refs[11 * k:11 * k + 11]
            outs = refs[22 + n_out * k:22 + n_out * (k + 1)]
            lr_ref, li_ref, carry = refs[22 + 2 * n_out + 3 * k:22 + 2 * n_out + 3 * k + 3]
            dirs.append(dict(v=dy_ref, mir=mir, mii=mii, tab=tab, mor=mor, moi=moi, y=outs[0], xr=lr_ref, xi=li_ref,
                             carry=carry, reverse=(k == 0), u=u_ref, fx=(xr_ref, xi_ref), halo=(hr_ref, hi_ref),
                             acc=outs[1:]))

        @pl.when(c == 0)
        def _():
            for d in dirs:
                for r in d['acc']:
                    r[...] = jnp.zeros_like(r)

        _scan_pair_chunk(dirs)
        for d in dirs:
            fwd_reverse = not d['reverse']
            rc = (nc - 1 - c) if d['reverse'] else c
            dmir_ref, dmii_ref, dmor_ref, dmoi_ref, da_ref = d['acc']
            xrv, xiv, lrv, liv = d['fx'][0][...], d['fx'][1][...], d['xr'][...], d['xi'][...]
            hr_ref, hi_ref = d['halo']
            row = lax.broadcasted_iota(jnp.int32, xrv.shape, 0)
            if fwd_reverse:
                live = (rc < nc - 1).astype(F32)
                edge_r, edge_i = hr_ref[0:1, :] * live, hi_ref[0:1, :] * live
                xpr = jnp.where(row == tc - 1, edge_r, pltpu.roll(xrv, tc - 1, 0))
                xpi = jnp.where(row == tc - 1, edge_i, pltpu.roll(xiv, tc - 1, 0))
            else:
                live = (rc > 0).astype(F32)
                edge_r, edge_i = hr_ref[SUBLANES - 1:SUBLANES, :] * live, hi_ref[SUBLANES - 1:SUBLANES, :] * live
                xpr = jnp.where(row == 0, edge_r, pltpu.roll(xrv, 1, 0))
                xpi = jnp.where(row == 0, edge_i, pltpu.roll(xiv, 1, 0))
            da_ref[0, 0:1, :] += jnp.sum(xpr * lrv + xpi * liv, axis=0, keepdims=True)
            da_ref[0, 1:2, :] += jnp.sum(xpr * liv - xpi * lrv, axis=0, keepdims=True)
            ub, dyb = d['u'][...].astype(BF16), d['v'][...].astype(BF16)
            dmir_ref[0] += lax.dot_general(ub, lrv.astype(BF16), TN_DIMS, preferred_element_type=F32)
            dmii_ref[0] += lax.dot_general(ub, liv.astype(BF16), TN_DIMS, preferred_element_type=F32)
            dmor_ref[0] += lax.dot_general(xrv.astype(BF16), dyb, TN_DIMS, preferred_element_type=F32)
            dmoi_ref[0] += lax.dot_general(xiv.astype(BF16), dyb, TN_DIMS, preferred_element_type=F32)

    col0 = u.shape[1] // STRIP_IN - N_STRIPS
    smap3 = lambda s, c: (s, 0, 0)
    m_in = pl.BlockSpec((1, STRIP_IN, STRIP_ST), smap3)
    m_out = pl.BlockSpec((1, STRIP_ST, STRIP_IN), smap3)
    tabspec = pl.BlockSpec((1, 8, SUBLANES, STRIP_ST), lambda s, c: (s, 0, 0, 0))
    in_specs, out_specs, args = [], [], []
    for k in range(2):
        reverse = k == 0
        rowblk = (lambda c: nc - 1 - c) if reverse else (lambda c: c)
        tmap = lambda s, c, rowblk=rowblk: (rowblk(c), s)
        umap = lambda s, c, rowblk=rowblk: (rowblk(c), s + col0)
        if not reverse:
            hmap = lambda s, c, rowblk=rowblk: (jnp.minimum((rowblk(c) + 1) * hb, t // SUBLANES - 1), s)
        else:
            hmap = lambda s, c, rowblk=rowblk: (jnp.maximum(rowblk(c) * hb - 1, 0), s)
        narrow = pl.BlockSpec((tc, STRIP_IN), tmap)
        wide = pl.BlockSpec((tc, STRIP_ST), tmap)
        halo = pl.BlockSpec((SUBLANES, STRIP_ST), hmap)
        in_specs += [narrow, m_in, m_in, tabspec, m_out, m_out, pl.BlockSpec((tc, STRIP_IN), umap), wide, wide,
                     halo, halo]
        out_specs += [narrow, m_in, m_in, m_out, m_out, pl.BlockSpec((1, SUBLANES, STRIP_ST), smap3)]
        xr, xi = states[k]
        args += [dy, *adj_ops[k], u, xr, xi, xr, xi]
    out_shape = [jax.ShapeDtypeStruct((t, SSM_WIDTH), F32),
                 jax.ShapeDtypeStruct((N_STRIPS, STRIP_IN, STRIP_ST), F32),
                 jax.ShapeDtypeStruct((N_STRIPS, STRIP_IN, STRIP_ST), F32),
                 jax.ShapeDtypeStruct((N_STRIPS, STRIP_ST, STRIP_IN), F32),
                 jax.ShapeDtypeStruct((N_STRIPS, STRIP_ST, STRIP_IN), F32),
                 jax.ShapeDtypeStruct((N_STRIPS, SUBLANES, STRIP_ST), F32)] * 2
    res = pl.pallas_call(
        body, out_shape=out_shape, grid=(N_STRIPS, nc), in_specs=in_specs, out_specs=out_specs,
        scratch_shapes=[pltpu.VMEM((tc, STRIP_ST), F32), pltpu.VMEM((tc, STRIP_ST), F32),
                        pltpu.VMEM((2, SUBLANES, STRIP_ST), F32)] * 2,
        compiler_params=_cparams(("parallel", "arbitrary")), name=name)(*args)
    return tuple(res[:n_out]), tuple(res[n_out:])


def _ssm_prep(lam_re, lam_im, log_dt, bt_re, bt_im, c_re, c_im):
    lr = jnp.minimum(lam_re, LAMBDA_RE_MAX)
    li = lam_im
    dt = jnp.exp(log_dt)[:, None]
    mag = jnp.exp(lr * dt)
    a_re = mag * jnp.cos(li * dt)
    a_im = mag * jnp.sin(li * dt)
    den = lr * lr + li * li
    coef_re = ((a_re - 1.0) * lr + a_im * li) / den
    coef_im = (a_im * lr - (a_re - 1.0) * li) / den
    bb_re = coef_re[:, None, :] * bt_re - coef_im[:, None, :] * bt_im
    bb_im = coef_re[:, None, :] * bt_im + coef_im[:, None, :] * bt_re
    eye = jnp.eye(SSM_GROUPS // N_STRIPS, dtype=F32)

    def strips(m):
        g, a, b = m.shape
        m4 = m.reshape(N_STRIPS, g // N_STRIPS, a, b)
        return jnp.einsum('sgab,gk->sgakb', m4, eye).reshape(N_STRIPS, g // N_STRIPS * a, g // N_STRIPS * b)

    mi_re = strips(bb_re)
    mi_im = strips(bb_im)
    mo_re = strips(jnp.swapaxes(c_re, 1, 2))
    mo_im = strips(-jnp.swapaxes(c_im, 1, 2))
    return a_re.reshape(-1), a_im.reshape(-1), mi_re, mi_im, mo_re, mo_im


def _gelu(x):
    c = math.sqrt(2.0 / math.pi)
    return 0.5 * x * (1.0 + jnp.tanh(c * (x + 0.044715 * x * x * x)))


def _gelu_grad(x):
    c = math.sqrt(2.0 / math.pi)
    th = jnp.tanh(c * (x + 0.044715 * x * x * x))
    return 0.5 * (1.0 + th) + 0.5 * x * (1.0 - th * th) * c * (1.0 + 3.0 * 0.044715 * x * x)


def _last_cols_specs(u, w, tr):
    half = w // 2
    first = (u.shape[1] - w) // half
    assert first * half == u.shape[1] - w
    return [pl.BlockSpec((tr, half), lambda i, k=k: (i, first + k)) for k in range(2)]


def _ssm_post_fwd(u, yf, yb, d, wglu, bglu, name):
    t, w = yf.shape
    tr = _row_tile(t)

    def body(ua_ref, ub_ref, yf_ref, yb_ref, d_ref, w_ref, b_ref, s_ref, y0_ref, z_ref):
        uv = jnp.concatenate([ua_ref[...], ub_ref[...]], axis=1)
        y0 = d_ref[...] * uv + yf_ref[...] + yb_ref[...]
        yg = _gelu(y0)
        z = jnp.dot(yg.astype(BF16), w_ref[...], preferred_element_type=F32) + b_ref[...]
        s_ref[...] = yg * _sigmoid(z)
        y0_ref[...] = y0
        z_ref[...] = z

    row = pl.BlockSpec((tr, w), lambda i: (i, 0))
    vec = pl.BlockSpec((1, w), lambda i: (0, 0))
    mat = pl.BlockSpec((w, w), lambda i: (0, 0))
    sh = jax.ShapeDtypeStruct((t, w), F32)
    return pl.pallas_call(body, out_shape=(sh, sh, sh), grid=(t // tr,),
                          in_specs=[*_last_cols_specs(u, w, tr), row, row, vec, mat, vec], out_specs=(row, row, row),
                          compiler_params=_cparams(("parallel",)), name=name)(u, u, yf, yb, d, wglu, bglu)


def _ssm_post_bwd(ds, y0, z, u, d, wglu, name):
    t, w = ds.shape
    tr = _row_tile(t)

    def body(ds_ref, y0_ref, z_ref, ua_ref, ub_ref, d_ref, w_ref, dy0_ref, dw_ref, db_ref, dd_ref):
        @pl.when(pl.program_id(0) == 0)
        def _():
            dw_ref[...] = jnp.zeros_like(dw_ref)
            db_ref[...] = jnp.zeros_like(db_ref)
            dd_ref[...] = jnp.zeros_like(dd_ref)

        y0 = y0_ref[...]
        yg = _gelu(y0)
        sg = _sigmoid(z_ref[...])
        dsv = ds_ref[...]
        dz = dsv * yg * sg * (1.0 - sg)
        dzb = dz.astype(BF16)
        dyg = dsv * sg + lax.dot_general(dzb, w_ref[...], (((1,), (1,)), ((), ())), preferred_element_type=F32)
        dy0 = dyg * _gelu_grad(y0)
        dy0_ref[...] = dy0
        dw_ref[...] += lax.dot_general(yg.astype(BF16), dzb, (((0,), (0,)), ((), ())), preferred_element_type=F32)
        db_ref[...] += jnp.sum(dz, axis=0, keepdims=True)
        uv = jnp.concatenate([ua_ref[...], ub_ref[...]], axis=1)
        dd_ref[...] += jnp.sum(dy0 * uv, axis=0, keepdims=True)

    row = pl.BlockSpec((tr, w), lambda i: (i, 0))
    vec = pl.BlockSpec((1, w), lambda i: (0, 0))
    mat = pl.BlockSpec((w, w), lambda i: (0, 0))
    return pl.pallas_call(
        body, out_shape=(jax.ShapeDtypeStruct((t, w), F32), jax.ShapeDtypeStruct((w, w), F32),
                         jax.ShapeDtypeStruct((1, w), F32), jax.ShapeDtypeStruct((1, w), F32)),
        grid=(t // tr,), in_specs=[row, row, row, *_last_cols_specs(u, w, tr), vec, mat],
        out_specs=(row, mat, vec, vec),
        compiler_params=_cparams(("arbitrary",)), name=name)(ds, y0, z, u, u, d, wglu)


def _du_combine(dy0, d, du_f, du_b, name):
    t, w = dy0.shape
    tr = _row_tile(t)

    def body(dy_ref, d_ref, a_ref, b_ref, o_ref):
        o_ref[...] = d_ref[...] * dy_ref[...] + a_ref[...] + b_ref[...]

    row = pl.BlockSpec((tr, w), lambda i: (i, 0))
    vec = pl.BlockSpec((1, w), lambda i: (0, 0))
    return pl.pallas_call(body, out_shape=jax.ShapeDtypeStruct((t, w), F32), grid=(t // tr,),
                          in_specs=[row, vec, row, row], out_specs=row, compiler_params=_cparams(("parallel",)),
                          name=name)(dy0, d, du_f, du_b)


def _ffn_fwd(x, g, wg, wu, wd, tag):
    xo, h, gate, up = _ffn_fwd_call(x, g, wg, wu, wd, f"{tag}_fwd")
    return xo, (h, gate, up)


def _ffn_bwd(dxo, dxo_b, x, g, wg, wu, wd, saved, tag):
    h, gate, up = saved
    dx, dx_b, dg, dgate, dup, act = _ffn_bwd_x_call(dxo, dxo_b, x, g, gate, up, wg, wu, wd, f"{tag}_bwd_x")
    dwg, dwu, dwd = _ffn_bwd_w_call(h, dxo_b, dgate, dup, act, f"{tag}_bwd_w")
    return dx, dx_b, dg, dwg, dwu, dwd


def _local_step(x, tgt, w, get_weights, put_grads, reduce_wide):
    t = x.shape[0]
    row = lambda a: a.reshape(1, -1)
    grads = {}

    w = dict(w)

    ssm_names = ['ssm_lambda_re', 'ssm_lambda_im', 'ssm_log_dt', 'ssm_b_re', 'ssm_b_im', 'ssm_c_re', 'ssm_c_im']
    tr3 = lambda m: jnp.swapaxes(m, 1, 2)
    fwd_ops, adj_ops, vjps = [], [], []
    for direction in range(2):
        rev = direction == 1
        prep, vjp = jax.vjp(_ssm_prep, *[w[n][direction] for n in ssm_names])
        a_re, a_im = prep[0], prep[1]
        mi_re, mi_im, mo_re, mo_im = (m.astype(BF16) for m in prep[2:])
        fwd_ops.append((mi_re, mi_im, _scan_tables(a_re, a_im, rev), mo_re, mo_im))
        adj_ops.append((tr3(mo_re), tr3(mo_im), _scan_tables(a_re, -a_im, not rev), tr3(mi_re), tr3(mi_im)))
        vjps.append(vjp)
    sink_rows = jnp.repeat(w['attn_sinks'].reshape(KV_HEADS, GQ), QBLOCK, axis=1)[:, None, :]
    bias = _attn_bias()
    prepared = sum(jnp.sum(op[:1, :1].astype(F32)) for ops in fwd_ops + adj_ops for op in ops) + sink_rows[0, 0, 0]

    w.update(get_weights('ffn1', prepared.reshape(1, 1)))
    x1, ffn1_saved = _ffn_fwd(x, w['norm_ffn1'], w['ffn1_w_gate'], w['ffn1_w_up'], w['ffn1_w_down'], "ffn1")
    w.update(get_weights('mix', x1))

    h2 = _rms_fwd(x1, w['norm_mix'], "mix_norm")
    proj = _mm(h2, w['w_in'], tb=True, name="in_proj")[0]
    u = proj

    attn, lse = _attn_fwd_proj(proj, sink_rows, bias, "attn_fwd")

    (y_f, *states_f), (y_b, *states_b) = _scan_pair(u, fwd_ops[0], fwd_ops[1], "s5_fwd")
    ys, states = [y_f, y_b], [states_f, states_b]
    d_row = row(w['ssm_d'])
    s, y0, z = _ssm_post_fwd(u, ys[0], ys[1], d_row, w['ssm_glu_w'], row(w['ssm_glu_b']), "ssm_post")

    ma = _rms_fwd(attn, row(w['attn_out_norm']), "attn_out_norm")
    ms = _rms_fwd(s, row(w['ssm_out_norm']), "ssm_out_norm")
    mixed = jnp.concatenate([ma, ms], axis=-1)
    x2 = _mm(mixed, w['w_out'], res=x1, reduce_s=True, name="out_proj")

    w.update(get_weights('ffn2', x2))
    x3, ffn2_saved = _ffn_fwd(x2, w['norm_ffn2'], w['ffn2_w_gate'], w['ffn2_w_up'], w['ffn2_w_down'], "ffn2")

    loss, dx3, dx3_b, dgf = _loss_head(x3, row(w['final_norm']), tgt, "loss_head")
    grads['final_norm'] = dgf.reshape(w['final_norm'].shape)

    dx2, dx2_b, dg, dwg, dwu, dwd = _ffn_bwd(dx3, dx3_b, x2, w['norm_ffn2'], w['ffn2_w_gate'], w['ffn2_w_up'],
                                             w['ffn2_w_down'], ffn2_saved, "ffn2")
    grads['norm_ffn2'] = dg
    sent = put_grads('ffn2', dict(ffn2_w_gate=dwg, ffn2_w_up=dwu, ffn2_w_down=dwd))

    dmixed = _mm(dx2_b, w['w_out'], tb=True, reduce_s=True, after=sent, name="out_proj_dx")
    dw_out = _mm(mixed, dx2_b, ta=True, out_dtype=BF16, name="out_proj_dw")[0]
    dattn, _, dga = _rms_bwd(attn, row(w['attn_out_norm']), dmixed[:, :ATTN_WIDTH], None, "attn_out_dnorm")
    ds, _, dgs = _rms_bwd(s, row(w['ssm_out_norm']), dmixed[:, ATTN_WIDTH:], None, "ssm_out_dnorm")
    grads.update(attn_out_norm=dga, ssm_out_norm=dgs)

    dy0, dwglu, dbglu, dd = _ssm_post_bwd(ds, y0, z, u, d_row, w['ssm_glu_w'], "ssm_post_bwd")
    grads['ssm_glu_b'] = dbglu
    grads['ssm_d'] = dd.reshape(w['ssm_d'].shape)
    dparams, du_dirs = [], []
    for direction, res in enumerate(_scan_adjoint_pair(dy0, u, states, adj_ops, "s5_adj")):
        du_dir, dmir, dmii, dmor, dmoi, da = res
        du_dirs.append(du_dir)
        da_re = da[:, 0, :].reshape(-1)
        da_im = da[:, 1, :].reshape(-1)
        dparams.append(vjps[direction]((da_re, da_im, dmir, dmii, dmor, dmoi)))
    du = _du_combine(dy0, d_row, du_dirs[0], du_dirs[1], "ssm_du")
    for i, n in enumerate(ssm_names):
        grads[n] = jnp.stack([dparams[0][i], dparams[1][i]])
    wide_sum = reduce_wide(grads)

    dq, dk, dv, dsink = _attn_bwd_proj(proj, sink_rows, bias, attn, lse, dattn, "attn_bwd")
    grads['attn_sinks'] = jnp.sum(dsink.reshape(ATTN_HEADS, QBLOCK), axis=-1).reshape(w['attn_sinks'].shape)
    dproj = jnp.concatenate([dq, dk, dv, du], axis=-1).astype(BF16)

    dw_in = _mm(dproj, h2, ta=True, out_dtype=BF16, after=wide_sum, name="in_proj_dw")[0]
    sent = put_grads('mix', dict(w_in=dw_in, ssm_glu_w=dwglu, w_out=dw_out))
    dh2 = _mm(dproj, w['w_in'], reduce_s=True, after=sent, name="in_proj_dx")
    dx1, dx1_b, dgm = _rms_bwd(x1, w['norm_mix'], dh2, dx2, "mix_dnorm")
    grads['norm_mix'] = dgm

    dx0, _, dg, dwg, dwu, dwd = _ffn_bwd(dx1, dx1_b, x, w['norm_ffn1'], w['ffn1_w_gate'], w['ffn1_w_up'],
                                         w['ffn1_w_down'], ffn1_saved, "ffn1")
    grads['norm_ffn1'] = dg
    put_grads('ffn1', dict(ffn1_w_gate=dwg, ffn1_w_up=dwu, ffn1_w_down=dwd))
    return loss, dx0, grads, wide_sum


HBM_SPEC = pl.BlockSpec(memory_space=pl.ANY)


def _chip_peers(x, y):
    return [(1 - x, y), (x, 1 - y), (1 - x, 1 - y)]


HBM_ONLY = pl.BlockSpec(memory_space=pltpu.HBM)
SEM_SPEC = pl.BlockSpec(memory_space=pltpu.SEMAPHORE)
EFFECT = pltpu.SideEffectType.DATAFLOW_SIDE_EFFECTING


def _place_own(src, slot, name):
    r, c = src.shape
    tr = r // 2

    def body(slot_ref, s_ref, o_ref):
        o_ref[0] = s_ref[...]

    return pl.pallas_call(
        body, out_shape=jax.ShapeDtypeStruct((N_CHIPS, r, c), src.dtype),
        grid_spec=pltpu.PrefetchScalarGridSpec(
            num_scalar_prefetch=1, grid=(2,), in_specs=[pl.BlockSpec((tr, c), lambda i, s: (i, 0))],
            out_specs=pl.BlockSpec((1, tr, c), lambda i, s: (s[0], i, 0))),
        compiler_params=_cparams(("parallel",)), name=name)(slot, src)


def _chip_copies(srcs, lands, send_sems, recv_sems, scatter, landed):
    x, y, c = lax.axis_index("x"), lax.axis_index("y"), lax.axis_index("c")
    me = 2 * x + y
    out = []
    for i in range(len(srcs)):
        for j, (px, py) in enumerate(_chip_peers(x, y)):
            p = 2 * px + py
            slot = p if landed else me
            if scatter:
                src, dst = srcs[i].at[p], lands[i].at[slot]
            else:
                rows = _core_half(srcs[i].shape[0], c)
                src, dst = srcs[i].at[rows], lands[i].at[slot, rows]
            out.append(pltpu.make_async_remote_copy(src, dst, send_sems.at[3 * i + j], recv_sems.at[3 * i + j],
                                                    device_id=(px, py, c), device_id_type=MESH))
    return out


def _core_half(nrows, c):
    half = nrows // 2
    return pl.ds(pl.multiple_of(c * half, 16), half)


def _sibling_forward(lands, name):
    n = len(lands)

    def body(*refs):
        bufs = refs[n:2 * n]
        send_sems, recv_sems = refs[2 * n:]
        x, y, c = lax.axis_index("x"), lax.axis_index("y"), lax.axis_index("c")
        mine = [_core_half(b.shape[1], c) for b in bufs]
        theirs = [_core_half(b.shape[1], 1 - c) for b in bufs]
        chips = [2 * px + py for px, py in _chip_peers(x, y)]
        cps = [pltpu.make_async_remote_copy(bufs[i].at[p, mine[i]], bufs[i].at[p, mine[i]], send_sems.at[3 * i + j],
                                            recv_sems.at[3 * i + j], device_id=(x, y, 1 - c), device_id_type=MESH)
               for i in range(n) for j, p in enumerate(chips)]
        for cp in cps:
            cp.start()
        for i in range(n):
            for j, p in enumerate(chips):
                pltpu.make_async_remote_copy(bufs[i].at[p, mine[i]], bufs[i].at[p, theirs[i]], send_sems.at[3 * i + j],
                                             recv_sems.at[3 * i + j], device_id=(x, y, 1 - c),
                                             device_id_type=MESH).wait()

    return pl.pallas_call(
        body, out_shape=[jax.ShapeDtypeStruct(a.shape, a.dtype) for a in lands],
        in_specs=[HBM_SPEC] * n, out_specs=[HBM_SPEC] * n, input_output_aliases={k: k for k in range(n)},
        scratch_shapes=[pltpu.SemaphoreType.DMA((3 * n,)), pltpu.SemaphoreType.DMA((3 * n,))],
        name=name)(*lands)


def _exchange_start(groups, scatter, name, after=None):
    sizes = [len(srcs) for srcs, _ in groups]
    flat_src = [a for srcs, _ in groups for a in srcs]
    flat_land = [a for _, lands in groups for a in lands]
    n = len(flat_src)
    ng = len(groups)

    def body(*refs):
        src_refs, land_refs = refs[:n], refs[n:2 * n]
        n_in = 2 * n + (after is not None)
        sems = refs[n_in:n_in + 2 * ng]
        token_ref = refs[-1]
        off = 0
        for gi, sz in enumerate(sizes):
            for cp in _chip_copies(src_refs[off:off + sz], land_refs[off:off + sz], sems[2 * gi], sems[2 * gi + 1],
                                   scatter, landed=False):
                cp.start()
            off += sz
        token_ref[...] = jnp.zeros_like(token_ref)

    sem_shapes = []
    for sz in sizes:
        sem_shapes += [pltpu.SemaphoreType.DMA((3 * sz,)), pltpu.SemaphoreType.DMA((3 * sz,))]
    hbm = lambda a: pltpu.HBM(a.shape, a.dtype)
    res = pl.pallas_call(
        body, name=name,
        out_shape=(tuple(sem_shapes) + tuple(hbm(a) for a in flat_src) + tuple(hbm(a) for a in flat_land)
                   + (jax.ShapeDtypeStruct((SUBLANES, LANES), F32),)),
        in_specs=[HBM_ONLY] * (2 * n) + [HBM_SPEC] * (after is not None),
        out_specs=tuple([SEM_SPEC] * (2 * ng) + [HBM_ONLY] * (2 * n) + [pl.BlockSpec(memory_space=pltpu.VMEM)]),
        input_output_aliases={k: 2 * ng + k for k in range(2 * n)},
        compiler_params=pltpu.CompilerParams(has_side_effects=EFFECT),
    )(*[pltpu.with_memory_space_constraint(a, pltpu.HBM) for a in flat_src + flat_land],
      *([after] if after is not None else []))
    sems, thru_src, thru_land = res[:2 * ng], res[2 * ng:2 * ng + n], res[2 * ng + n:2 * ng + 2 * n]
    out, off = [], 0
    for gi, sz in enumerate(sizes):
        out.append((sems[2 * gi], sems[2 * gi + 1], list(thru_src[off:off + sz]), list(thru_land[off:off + sz])))
        off += sz
    return out, res[-1]


def _exchange_wait(started, after, scatter, name):
    send_sems, recv_sems, srcs, lands = started
    n = len(srcs)

    def body(*refs):
        src_refs, land_refs = refs[:n], refs[n:2 * n]
        send_ref, recv_ref = refs[2 * n], refs[2 * n + 1]
        for cp in _chip_copies(src_refs, land_refs, send_ref, recv_ref, scatter, landed=True):
            cp.wait_send()
            cp.wait_recv()

    hbm = lambda a: pltpu.HBM(a.shape, a.dtype)
    res = pl.pallas_call(
        body, name=name, out_shape=tuple(hbm(a) for a in srcs) + tuple(hbm(a) for a in lands),
        in_specs=[HBM_ONLY] * (2 * n) + [SEM_SPEC, SEM_SPEC, HBM_SPEC], out_specs=tuple([HBM_ONLY] * (2 * n)),
        input_output_aliases={k: k for k in range(2 * n)},
        compiler_params=pltpu.CompilerParams(has_side_effects=EFFECT),
    )(*srcs, *lands, send_sems, recv_sems, after)
    return list(res[:n]), list(res[n:])


def _half_swap(parts, name):
    n = len(parts)

    def body(*refs):
        ins, outs = refs[:n], refs[n:2 * n]
        send_sems, recv_sems = refs[2 * n:]
        x, y, c = lax.axis_index("x"), lax.axis_index("y"), lax.axis_index("c")
        cps = [pltpu.make_async_remote_copy(ins[i].at[k, _core_half(ins[i].shape[1], 1 - c)], outs[i].at[k],
                                            send_sems.at[N_CHIPS * i + k], recv_sems.at[N_CHIPS * i + k],
                                            device_id=(x, y, 1 - c), device_id_type=MESH)
               for i in range(n) for k in range(N_CHIPS)]
        for cp in cps:
            cp.start()
        for cp in cps:
            cp.wait()

    return pl.pallas_call(
        body, out_shape=[jax.ShapeDtypeStruct((N_CHIPS, p.shape[1] // 2, p.shape[2]), p.dtype) for p in parts],
        in_specs=[HBM_SPEC] * n, out_specs=[HBM_SPEC] * n,
        scratch_shapes=[pltpu.SemaphoreType.DMA((N_CHIPS * n,)), pltpu.SemaphoreType.DMA((N_CHIPS * n,))],
        name=name)(*parts)


def _half_add(parts, sib, slots, name):
    na = len(parts)
    _, r, c = parts[0].shape
    hr = r // 2
    tr = _row_tile(hr, 512)
    nt = hr // tr

    def body(slot_ref, *refs):
        for a in range(na):
            refs[2 * na + a][...] = (refs[2 * a][...].astype(F32) + refs[2 * a + 1][...].astype(F32)).astype(BF16)

    mine = pl.BlockSpec((1, tr, c), lambda k, i, s: (k, i + s[4] * nt, 0))
    half = pl.BlockSpec((1, tr, c), lambda k, i, s: (k, i, 0))
    args = [a for p, sb in zip(parts, sib) for a in (p, sb)]
    return pl.pallas_call(
        body, out_shape=[jax.ShapeDtypeStruct((N_CHIPS, hr, c), BF16)] * na,
        grid_spec=pltpu.PrefetchScalarGridSpec(
            num_scalar_prefetch=1, grid=(N_CHIPS, nt), in_specs=[mine, half] * na, out_specs=[half] * na),
        compiler_params=_cparams(("parallel", "parallel")), name=name)(slots, *args)


def _half_forward(arrs, name):
    n = len(arrs)

    def body(*refs):
        bufs = refs[n:2 * n]
        send_sems, recv_sems = refs[2 * n:]
        x, y, c = lax.axis_index("x"), lax.axis_index("y"), lax.axis_index("c")
        cps = [pltpu.make_async_remote_copy(b.at[_core_half(b.shape[0], c)], b.at[_core_half(b.shape[0], c)],
                                            send_sems.at[i], recv_sems.at[i], device_id=(x, y, 1 - c),
                                            device_id_type=MESH) for i, b in enumerate(bufs)]
        for cp in cps:
            cp.start()
        for i, b in enumerate(bufs):
            pltpu.make_async_remote_copy(b.at[_core_half(b.shape[0], c)], b.at[_core_half(b.shape[0], 1 - c)],
                                         send_sems.at[i], recv_sems.at[i], device_id=(x, y, 1 - c),
                                         device_id_type=MESH).wait()

    return pl.pallas_call(
        body, out_shape=[jax.ShapeDtypeStruct(a.shape, a.dtype) for a in arrs],
        in_specs=[HBM_SPEC] * n, out_specs=[HBM_SPEC] * n, input_output_aliases={k: k for k in range(n)},
        scratch_shapes=[pltpu.SemaphoreType.DMA((n,)), pltpu.SemaphoreType.DMA((n,))],
        name=name)(*arrs)


def _small_exchange(smalls, name):
    nsm = len(smalls)
    rels = [(fx, fy, fc) for fx in (0, 1) for fy in (0, 1) for fc in (0, 1)][1:]

    def body(*refs):
        sins, souts = refs[:nsm], refs[nsm:2 * nsm]
        ssend, srecv, slocal = refs[2 * nsm:]
        x, y, c = lax.axis_index("x"), lax.axis_index("y"), lax.axis_index("c")
        lin = 4 * x + 2 * y + c
        local = [pltpu.make_async_copy(sins[i], souts[i].at[lin], slocal.at[i]) for i in range(nsm)]
        for cp in local:
            cp.start()
        for i in range(nsm):
            for j, (fx, fy, fc) in enumerate(rels):
                pltpu.make_async_remote_copy(sins[i], souts[i].at[lin], ssend.at[i, j], srecv.at[i, j],
                                             device_id=(x ^ fx, y ^ fy, c ^ fc), device_id_type=MESH).start()
        for i in range(nsm):
            for j, (fx, fy, fc) in enumerate(rels):
                src = 4 * (x ^ fx) + 2 * (y ^ fy) + (c ^ fc)
                pltpu.make_async_remote_copy(sins[i], souts[i].at[src], ssend.at[i, j], srecv.at[i, j],
                                             device_id=(x ^ fx, y ^ fy, c ^ fc), device_id_type=MESH).wait()
        for cp in local:
            cp.wait()

    return pl.pallas_call(
        body, out_shape=[jax.ShapeDtypeStruct((N_DEV,) + s.shape, s.dtype) for s in smalls],
        in_specs=[HBM_SPEC] * nsm, out_specs=[HBM_SPEC] * nsm,
        scratch_shapes=[pltpu.SemaphoreType.DMA((nsm, 7)), pltpu.SemaphoreType.DMA((nsm, 7)),
                        pltpu.SemaphoreType.DMA((nsm,))],
        name=name)(*smalls)


def _sum_parts(parts, recv, slots, name):
    na = len(parts)
    _, r, c = parts[0].shape
    tr = _row_tile(r, 192)

    def body(slot_ref, *refs):
        for a in range(na):
            own_ref, r0_ref, r1_ref, r2_ref = refs[4 * a:4 * a + 4]
            refs[4 * na + a][...] = ((own_ref[0].astype(F32) + r0_ref[0].astype(F32))
                                     + (r1_ref[0].astype(F32) + r2_ref[0].astype(F32)))

    blk = lambda k: pl.BlockSpec((1, tr, c), lambda i, s, k=k: (s[k], i, 0))
    out_blk = pl.BlockSpec((tr, c), lambda i, s: (i + s[4] * (r // tr), 0))
    args = [a for p, rv in zip(parts, recv) for a in (p, rv, rv, rv)]
    return pl.pallas_call(
        body, out_shape=[jax.ShapeDtypeStruct((2 * r, c), F32)] * na,
        grid_spec=pltpu.PrefetchScalarGridSpec(
            num_scalar_prefetch=1, grid=(r // tr,), in_specs=[blk(0), blk(1), blk(2), blk(3)] * na,
            out_specs=[out_blk] * na),
        compiler_params=_cparams(("parallel",)), name=name)(slots, *args)


def _small_allreduce(packed, name):
    rows = packed.shape[0]
    pr = rows // N_DEV
    rels = [(fx, fy, fc) for fx in (0, 1) for fy in (0, 1) for fc in (0, 1)][1:]

    def body(in_ref, out_ref, recv_ref, send1, recv1, send2, recv2):
        x, y, c = lax.axis_index("x"), lax.axis_index("y"), lax.axis_index("c")
        lin = 4 * x + 2 * y + c
        piece = lambda ref, k: ref.at[pl.ds(pl.multiple_of(k * pr, pr), pr), :]
        peers = [((x ^ fx, y ^ fy, c ^ fc), 4 * (x ^ fx) + 2 * (y ^ fy) + (c ^ fc)) for fx, fy, fc in rels]
        for j, (dev, plin) in enumerate(peers):
            pltpu.make_async_remote_copy(piece(in_ref, plin), recv_ref.at[lin], send1.at[j], recv1.at[j],
                                         device_id=dev, device_id_type=MESH).start()
        recv_ref[lin] = piece(in_ref, lin)[...]
        for j, (dev, plin) in enumerate(peers):
            pltpu.make_async_remote_copy(piece(in_ref, plin), recv_ref.at[plin], send1.at[j], recv1.at[j],
                                         device_id=dev, device_id_type=MESH).wait()
        acc = recv_ref[0]
        for k in range(1, N_DEV):
            acc = acc + recv_ref[k]
        piece(out_ref, lin)[...] = acc
        for j, (dev, plin) in enumerate(peers):
            pltpu.make_async_remote_copy(piece(out_ref, lin), piece(out_ref, lin), send2.at[j], recv2.at[j],
                                         device_id=dev, device_id_type=MESH).start()
        for j, (dev, plin) in enumerate(peers):
            pltpu.make_async_remote_copy(piece(out_ref, lin), piece(out_ref, plin), send2.at[j], recv2.at[j],
                                         device_id=dev, device_id_type=MESH).wait()

    vm = pl.BlockSpec(memory_space=pltpu.VMEM)
    return pl.pallas_call(
        body, out_shape=jax.ShapeDtypeStruct(packed.shape, F32), in_specs=[vm], out_specs=vm,
        scratch_shapes=[pltpu.VMEM((N_DEV, pr, LANES), F32)] + [pltpu.SemaphoreType.DMA((7,))] * 4,
        compiler_params=pltpu.CompilerParams(vmem_limit_bytes=VMEM_LIMIT), name=name)(packed)


def _adamw_math(w, m, v, g):
    nm = ADAM_B1 * m + (1.0 - ADAM_B1) * g
    nv = ADAM_B2 * v + (1.0 - ADAM_B2) * (g * g)
    m_hat = nm * (1.0 / (1.0 - ADAM_B1 ** ADAM_STEP))
    v_hat = nv * (1.0 / (1.0 - ADAM_B2 ** ADAM_STEP))
    return -ADAM_LR * (m_hat / (jnp.sqrt(v_hat) + ADAM_EPS) + ADAM_WD * w), nm, nv


def _adamw(ws, ms, vs, gs, name):
    na = len(ws)
    r, c = ws[0].shape
    tr = _row_tile(r)

    def body(*refs):
        for a in range(na):
            w_ref, m_ref, v_ref, g_ref = refs[4 * a:4 * a + 4]
            d_ref, nm_ref, nv_ref = refs[4 * na + 3 * a:4 * na + 3 * a + 3]
            d_ref[...], nm_ref[...], nv_ref[...] = _adamw_math(w_ref[...], m_ref[...], v_ref[...], g_ref[...])

    blk = pl.BlockSpec((tr, c), lambda i: (i, 0))
    sh = jax.ShapeDtypeStruct((r, c), F32)
    args = [a for group in zip(ws, ms, vs, gs) for a in group]
    res = pl.pallas_call(body, out_shape=[sh] * (3 * na), grid=(r // tr,), in_specs=[blk] * (4 * na),
                         out_specs=[blk] * (3 * na), compiler_params=_cparams(("parallel",)), name=name)(*args)
    return [tuple(res[3 * a:3 * a + 3]) for a in range(na)]


def _adamw_small(ws, ms, vs, alls, split, name):
    n = len(ws)
    lead = split if split is not None else ()
    nl = len(lead)
    nslots = alls[0].shape[0]

    def blocks(shape):
        if split is None:
            return tuple(shape), (lambda *g: (0,) * len(shape))
        blk = (shape[0], shape[1] // lead[0], shape[2] // lead[1]) + tuple(shape[3:])
        return blk, (lambda *g: (0, g[0], g[1]) + (0,) * (len(shape) - 3))

    def body(*refs):
        w_refs, m_refs, v_refs, a_refs = (refs[k * n:(k + 1) * n] for k in range(4))
        g_refs, d_refs, nm_refs, nv_refs = (refs[(4 + k) * n:(5 + k) * n] for k in range(4))
        k = pl.program_id(nl)
        for i in range(n):
            @pl.when(k == 0)
            def _(i=i):
                g_refs[i][...] = a_refs[i][0]

            @pl.when(k > 0)
            def _(i=i):
                g_refs[i][...] += a_refs[i][0]

            @pl.when(k == nslots - 1)
            def _(i=i):
                d_refs[i][...], nm_refs[i][...], nv_refs[i][...] = _adamw_math(
                    w_refs[i][...], m_refs[i][...], v_refs[i][...], g_refs[i][...])

    specs, aspecs, shapes = [], [], []
    for wa in ws:
        blk, imap = blocks(wa.shape)
        specs.append(pl.BlockSpec(blk, imap))
        aspecs.append(pl.BlockSpec((1,) + blk, (lambda *g, imap=imap: (g[nl],) + imap(*g))))
        shapes.append(jax.ShapeDtypeStruct(wa.shape, F32))
    res = pl.pallas_call(
        body, out_shape=shapes * 4, grid=tuple(lead) + (nslots,), in_specs=specs * 3 + aspecs,
        out_specs=specs * 4, compiler_params=_cparams(("parallel",) * nl + ("arbitrary",)),
        name=name)(*ws, *ms, *vs, *alls)
    return res[:n], res[n:2 * n], res[2 * n:3 * n], res[3 * n:]


def kernel(x, norm_ffn1, ffn1_w_gate, ffn1_w_up, ffn1_w_down, norm_mix, w_in, attn_sinks, ssm_lambda_re, ssm_lambda_im, ssm_log_dt, ssm_b_re, ssm_b_im, ssm_c_re, ssm_c_im, ssm_d, ssm_glu_w, ssm_glu_b, attn_out_norm, ssm_out_norm, w_out, norm_ffn2, ffn2_w_gate, ffn2_w_up, ffn2_w_down, final_norm, loss_target, m_norm_ffn1, m_ffn1_w_gate, m_ffn1_w_up, m_ffn1_w_down, m_norm_mix, m_w_in, m_attn_sinks, m_ssm_lambda_re, m_ssm_lambda_im, m_ssm_log_dt, m_ssm_b_re, m_ssm_b_im, m_ssm_c_re, m_ssm_c_im, m_ssm_d, m_ssm_glu_w, m_ssm_glu_b, m_attn_out_norm, m_ssm_out_norm, m_w_out, m_norm_ffn2, m_ffn2_w_gate, m_ffn2_w_up, m_ffn2_w_down, m_final_norm, v_norm_ffn1, v_ffn1_w_gate, v_ffn1_w_up, v_ffn1_w_down, v_norm_mix, v_w_in, v_attn_sinks, v_ssm_lambda_re, v_ssm_lambda_im, v_ssm_log_dt, v_ssm_b_re, v_ssm_b_im, v_ssm_c_re, v_ssm_c_im, v_ssm_d, v_ssm_glu_w, v_ssm_glu_b, v_attn_out_norm, v_ssm_out_norm, v_w_out, v_norm_ffn2, v_ffn2_w_gate, v_ffn2_w_up, v_ffn2_w_down, v_final_norm):
    given = dict(locals())
    wts = {n: given[n] for n in WEIGHTS}

    order = [g for g in GROUPS]
    cx, cy = lax.axis_index("x"), lax.axis_index("y")
    slots = jnp.stack([2 * cx + cy, 2 * (1 - cx) + cy, 2 * cx + 1 - cy, 2 * (1 - cx) + 1 - cy,
                       lax.axis_index("c")]).astype(jnp.int32)
    def view(a, n):
        if n in TRANSPOSED:
            return jnp.swapaxes(a[0], 0, 1)
        if n in BIG:
            return a[0]
        if n in ('ssm_b_re', 'ssm_b_im'):
            return jnp.swapaxes(a, -1, -2)
        return a.reshape(1, -1) if a.ndim == 1 else a

    def unview(a, n):
        if n in TRANSPOSED:
            return jnp.swapaxes(a, 0, 1)[None]
        if n in ('ssm_b_re', 'ssm_b_im'):
            return jnp.swapaxes(a, -1, -2)
        return a.reshape(wts[n].shape)

    started, gather_token = {}, None
    for g in order:
        shards = [view(wts[n], n).astype(BF16) for n in GROUPS[g]]
        placed = [_place_own(s, slots, f"weights_place_{n}") for n, s in zip(GROUPS[g], shards)]
        st, gather_token = _exchange_start([(shards, placed)], False, f"weights_start_{g}", after=gather_token)
        started[g] = st[0]

    def get_weights(group, after):
        if group == order[0]:
            after = after + gather_token[:1, :1]
        _, lands = _exchange_wait(started[group], after, False, f"weights_wait_{group}")
        lands = _sibling_forward(lands, f"weights_forward_{group}")
        out = dict(zip(GROUPS[group], lands))
        for n in ('w_in', 'ssm_glu_w', 'w_out'):
            if n in out:
                out[n] = out[n].reshape(-1, out[n].shape[-1])
        return out

    sent, tokens = {}, {}

    def put_grads(group, gd):
        parts = []
        for n in GROUPS[group]:
            g = gd[n]
            if g.ndim == 2:
                g = g.reshape(N_CHIPS, g.shape[0] // N_CHIPS, g.shape[1])
            parts.append(g.astype(BF16))
        sib = _half_swap(parts, f"grads_half_swap_{group}")
        same = len({p.shape for p in parts}) == 1
        batches = [list(range(len(parts)))] if same else [[i] for i in range(len(parts))]
        halves = [None] * len(parts)
        for b in batches:
            res = _half_add([parts[i] for i in b], [sib[i] for i in b], slots, f"grads_half_add_{GROUPS[group][b[0]]}")
            for i, h in zip(b, res):
                halves[i] = h
        parts = halves
        lands = [lax.empty(p.shape, p.dtype) for p in parts]
        started_g, tokens[group] = _exchange_start([(parts, lands)], True, f"grads_start_{group}")
        sent[group] = started_g[0]
        return tokens[group]

    w = {n: (wts[n][0] if wts[n].ndim > 1 else wts[n]) for n in SMALL}
    w['norm_ffn1'], w['norm_mix'], w['norm_ffn2'] = wts['norm_ffn1'], wts['norm_mix'], wts['norm_ffn2']
    w['ssm_b_re'], w['ssm_b_im'] = view(wts['ssm_b_re'], 'ssm_b_re')[0], view(wts['ssm_b_im'], 'ssm_b_im')[0]
    w['ssm_log_dt'] = w['ssm_log_dt'] + gather_token[0, 0]
    wide =['ssm_b_re', 'ssm_b_im', 'ssm_c_re', 'ssm_c_im']

    def reduce_wide(gd):
        packed = jnp.concatenate([gd[n].reshape(-1, LANES) for n in wide])
        return _small_allreduce(packed, "small_grads_allreduce")

    loss_row, dx, grads, wide_sum = _local_step(x[0], loss_target[0], w, get_weights, put_grads, reduce_wide)

    out_g, out_d, out_m, out_v = {}, {}, {}, {}

    def finish(group, after):
        names = GROUPS[group]
        parts, recv = _exchange_wait(sent[group], after, True, f"grads_wait_{group}")
        same = len({p.shape for p in parts}) == 1
        batches = [list(range(len(names)))] if same else [[i] for i in range(len(names))]
        sums = [None] * len(names)
        for b in batches:
            res = _sum_parts([parts[i] for i in b], [recv[i] for i in b], slots, f"grad_sum_{names[b[0]]}")
            for i, sm in zip(b, res):
                sums[i] = sm
        full = _half_forward(sums, f"grad_half_forward_{group}")
        for b in batches:
            res = _adamw([view(wts[names[i]], names[i]) for i in b], [view(given['m_' + names[i]], names[i]) for i in b],
                         [view(given['v_' + names[i]], names[i]) for i in b], [full[i] for i in b],
                         f"adamw_{names[b[0]]}")
            for i, (d, nm, nv) in zip(b, res):
                n = names[i]
                out_g[n], out_d[n], out_m[n], out_v[n] = (unview(a, n) for a in (full[i], d, nm, nv))
        return nv

    done = finish('ffn2', tokens['ffn1'])
    done = finish('mix', done)

    nat = {n: view(wts[n], n).shape for n in SMALL}
    narrow = [n for n in SMALL if n not in wide]
    alls = list(_small_exchange([grads[n].reshape(nat[n]) for n in narrow] + [loss_row], "small_grads_allgather"))
    loss = jnp.sum(alls.pop()[:, 0, 0])
    rows = wide_sum.shape[0] // len(wide)
    wide_g = [wide_sum[i * rows:(i + 1) * rows].reshape((1,) + nat[n]) for i, n in enumerate(wide)]
    for group, gs, split, tag in ((narrow, alls, None, "adamw_small"), (wide, wide_g, (2, 4), "adamw_ssm_bc")):
        res = _adamw_small([view(wts[n], n) for n in group], [view(given['m_' + n], n) for n in group],
                           [view(given['v_' + n], n) for n in group], gs, split, tag)
        for dst, vals in zip((out_g, out_d, out_m, out_v), res):
            for n, a in zip(group, vals):
                dst[n] = unview(a, n)

    finish('ffn1', out_v['norm_ffn1'][:, :1] + out_v['ssm_c_re'].reshape(1, -1)[:, :1] + done[:1, :1] + loss)

    return (loss, dx[None], *[out_g[n] for n in WEIGHTS], *[out_d[n] for n in WEIGHTS],
            *[out_m[n] for n in WEIGHTS], *[out_v[n] for n in WEIGHTS])
```

```python
import functools
import math

import numpy as np
import jax
import jax.numpy as jnp
from jax import lax
from jax.experimental import pallas as pl
from jax.experimental.pallas import tpu as pltpu

F32 = jnp.float32
BF16 = jnp.bfloat16
MESH = pl.DeviceIdType.MESH

EPS = 1e-6
NEG_INF = -1e30
LAMBDA_RE_MAX = -1e-4
ATTN_HEADS = 8
KV_HEADS = 2
GQ = ATTN_HEADS // KV_HEADS
HEAD_DIM = 64
ATTN_WIDTH = 512
KV_WIDTH = 128
WINDOW = 128
QBLOCK = 128
SSM_WIDTH = 512
SSM_GROUPS = 32
SSM_CH = 16
SSM_STATE = 64
N_STRIPS = 4
STRIP_IN = SSM_WIDTH // N_STRIPS
STRIP_ST = SSM_GROUPS * SSM_STATE // N_STRIPS
SUBLANES = 8
LANES = 128
N_CHIPS = 4
N_DEV = 8

ADAM_LR = 0.001
ADAM_B1 = 0.9
ADAM_B2 = 0.999
ADAM_EPS = 1e-08
ADAM_WD = 0.01
ADAM_STEP = 10

VMEM_LIMIT = 48 * 1024 * 1024

WEIGHTS = ['norm_ffn1', 'ffn1_w_gate', 'ffn1_w_up', 'ffn1_w_down', 'norm_mix', 'w_in', 'attn_sinks',
           'ssm_lambda_re', 'ssm_lambda_im', 'ssm_log_dt', 'ssm_b_re', 'ssm_b_im', 'ssm_c_re', 'ssm_c_im',
           'ssm_d', 'ssm_glu_w', 'ssm_glu_b', 'attn_out_norm', 'ssm_out_norm', 'w_out', 'norm_ffn2',
           'ffn2_w_gate', 'ffn2_w_up', 'ffn2_w_down', 'final_norm']
BIG = ['ffn1_w_gate', 'ffn1_w_up', 'ffn1_w_down', 'w_in', 'ssm_glu_w', 'w_out',
       'ffn2_w_gate', 'ffn2_w_up', 'ffn2_w_down']
SMALL = [n for n in WEIGHTS if n not in BIG]
TRANSPOSED = ['ffn1_w_gate', 'ffn1_w_up', 'w_in', 'ffn2_w_gate', 'ffn2_w_up']
GROUPS = {'ffn1': ['ffn1_w_gate', 'ffn1_w_up', 'ffn1_w_down'],
          'mix': ['w_in', 'ssm_glu_w', 'w_out'],
          'ffn2': ['ffn2_w_gate', 'ffn2_w_up', 'ffn2_w_down']}


def _cparams(sem=None):
    return pltpu.CompilerParams(dimension_semantics=sem, vmem_limit_bytes=VMEM_LIMIT)


def _tile(n, pref):
    if n <= pref:
        return n
    for t in (pref, pref // 2, pref // 4):
        if t % LANES == 0 and n % t == 0:
            return t
    return n


def _sigmoid(x):
    return 1.0 / (1.0 + jnp.exp(-x))


def _sigmoid_tanh(x):
    return 0.5 * jnp.tanh(0.5 * x) + 0.5


def _mm(a, b, *, ta=False, tb=False, reduce_s=False, res=None, scale=1.0, out_dtype=F32, after=None, name):
    a3 = a if a.ndim == 3 else a[None]
    b3 = b if b.ndim == 3 else b[None]
    sa, sb = a3.shape[0], b3.shape[0]
    ns = max(sa, sb)
    (kk, m) = a3.shape[1:] if ta else a3.shape[1:][::-1]
    (n, kb) = b3.shape[1:] if tb else b3.shape[1:][::-1]
    assert kk == kb, (a3.shape, b3.shape)
    tm, tn, tk = _tile(m, 1024), _tile(n, 1024), _tile(kk, 2048)
    nm, nn, nk = m // tm, n // tn, kk // tk
    has_res = res is not None
    single = nk == 1 and not (reduce_s and ns > 1)

    if reduce_s:
        grid = (nm, nn, ns, nk)
        ids = lambda i, j, s, k: (s, i, j, k)
        sem = ("parallel", "parallel", "arbitrary", "arbitrary")
    else:
        grid = (ns, nm, nn, nk)
        ids = lambda s, i, j, k: (s, i, j, k)
        sem = ("parallel", "parallel", "parallel", "arbitrary")

    def a_map(*g):
        s, i, j, k = ids(*g)
        s = s if sa > 1 else 0
        return (s, k, i) if ta else (s, i, k)

    def b_map(*g):
        s, i, j, k = ids(*g)
        s = s if sb > 1 else 0
        return (s, j, k) if tb else (s, k, j)

    def o_map(*g):
        s, i, j, k = ids(*g)
        return (i, j) if reduce_s else (s, i, j)

    a_blk = (1, tk, tm) if ta else (1, tm, tk)
    b_blk = (1, tn, tk) if tb else (1, tk, tn)
    dims = (((0 if ta else 1,), (1 if tb else 0,)), ((), ()))

    def body(*refs):
        a_ref, b_ref = refs[0], refs[1]
        r_ref = refs[2] if has_res else None
        o_ref = refs[2 + has_res + (after is not None)]
        acc_ref = None if single else refs[-1]
        s, _, _, k = ids(*[pl.program_id(d) for d in range(4)])
        prod = lax.dot_general(a_ref[0].astype(BF16), b_ref[0].astype(BF16), dims, preferred_element_type=F32)

        def finish(out):
            if scale != 1.0:
                out = out * scale
            if has_res:
                out = r_ref[...].reshape(out.shape) + out
            o_ref[...] = out.astype(out_dtype).reshape(o_ref.shape)

        if single:
            finish(prod)
            return
        if reduce_s:
            first = jnp.logical_and(s == 0, k == 0)
            last = jnp.logical_and(s == ns - 1, k == nk - 1)
        else:
            first, last = k == 0, k == nk - 1

        acc_ref[...] = prod + jnp.where(first, 0.0, acc_ref[...])

        @pl.when(last)
        def _():
            finish(acc_ref[...])

    in_specs = [pl.BlockSpec(a_blk, a_map), pl.BlockSpec(b_blk, b_map)]
    args = [a3, b3]
    if reduce_s:
        out_shape = jax.ShapeDtypeStruct((m, n), out_dtype)
        o_spec = pl.BlockSpec((tm, tn), o_map)
    else:
        out_shape = jax.ShapeDtypeStruct((ns, m, n), out_dtype)
        o_spec = pl.BlockSpec((1, tm, tn), o_map)
    if has_res:
        assert res.shape == out_shape.shape
        in_specs.append(o_spec)
        args.append(res)
    if after is not None:
        in_specs.append(HBM_SPEC)
        args.append(after)
    return pl.pallas_call(body, out_shape=out_shape, grid=grid, in_specs=in_specs, out_specs=o_spec,
                          scratch_shapes=[] if single else [pltpu.VMEM((tm, tn), F32)],
                          compiler_params=_cparams(sem), name=name)(*args)


def _row_tile(t, cap=256):
    for step in (16, SUBLANES):
        for tr in range(min(cap, t) // step * step, 0, -step):
            if t % tr == 0:
                return tr
    return t


def _norm_mm(xs, gs, w, *, tb, res, name):
    nx = len(xs)
    t = xs[0].shape[0]
    widths = [x.shape[1] for x in xs]
    k = sum(widths)
    n = w.shape[0] if tb else w.shape[1]
    tm, tn = _tile(t, 512), _tile(n, 512)
    has_res = res is not None

    def body(*refs):
        x_refs, g_refs, w_ref = refs[:nx], refs[nx:2 * nx], refs[2 * nx]
        r_ref = refs[2 * nx + 1] if has_res else None
        o_ref, h_ref, h_sc = refs[2 * nx + 1 + has_res:]

        @pl.when(pl.program_id(1) == 0)
        def _():
            off = 0
            for x_ref, g_ref, wd in zip(x_refs, g_refs, widths):
                xv = x_ref[...]
                r = lax.rsqrt(jnp.mean(xv * xv, axis=-1, keepdims=True) + EPS)
                h_sc[:, off:off + wd] = (xv * r * g_ref[...]).astype(BF16)
                off += wd
            h_ref[...] = h_sc[...]

        prod = lax.dot_general(h_sc[...], w_ref[...], NT_DIMS if tb else (((1,), (0,)), ((), ())),
                               preferred_element_type=F32)
        o_ref[...] = r_ref[...] + prod if has_res else prod

    in_specs = [pl.BlockSpec((tm, wd), lambda i, j: (i, 0)) for wd in widths]
    in_specs += [pl.BlockSpec((1, wd), lambda i, j: (0, 0)) for wd in widths]
    in_specs.append(pl.BlockSpec((tn, k), lambda i, j: (j, 0)) if tb else pl.BlockSpec((k, tn), lambda i, j: (0, j)))
    tile = pl.BlockSpec((tm, tn), lambda i, j: (i, j))
    if has_res:
        in_specs.append(tile)
    return pl.pallas_call(
        body, out_shape=(jax.ShapeDtypeStruct((t, n), F32), jax.ShapeDtypeStruct((t, k), BF16)),
        grid=(t // tm, n // tn), in_specs=in_specs,
        out_specs=(tile, pl.BlockSpec((tm, k), lambda i, j: (i, 0))),
        scratch_shapes=[pltpu.VMEM((tm, k), BF16)], compiler_params=_cparams(("parallel", "arbitrary")),
        name=name)(*xs, *gs, w, *([res] if has_res else []))


def _rms_bwd_rows(xv, gv, dhv):
    r = lax.rsqrt(jnp.mean(xv * xv, axis=-1, keepdims=True) + EPS)
    nrm = xv * r
    dn = dhv * gv
    return r * (dn - nrm * jnp.mean(dn * nrm, axis=-1, keepdims=True)), dhv * nrm


def _rms_bwd(x, g, dh, dres, name, dh_col=0):
    t, w = x.shape
    tr = _row_tile(t)
    has_res = dres is not None

    def body(*refs):
        if has_res:
            x_ref, g_ref, dh_ref, dr_ref, dx_ref, dxb_ref, dg_ref = refs
        else:
            x_ref, g_ref, dh_ref, dx_ref, dxb_ref, dg_ref = refs
        dx, dgs = _rms_bwd_rows(x_ref[...], g_ref[...], dh_ref[...])
        if has_res:
            dx = dx + dr_ref[...]
        dx_ref[...] = dx
        dxb_ref[...] = dx.astype(BF16)

        @pl.when(pl.program_id(0) == 0)
        def _():
            dg_ref[...] = jnp.zeros_like(dg_ref)

        dg_ref[...] += jnp.sum(dgs, axis=0, keepdims=True)

    row = pl.BlockSpec((tr, w), lambda i: (i, 0))
    dh_row = pl.BlockSpec((tr, w), lambda i: (i, dh_col))
    vec = pl.BlockSpec((1, w), lambda i: (0, 0))
    ins = [x, g, dh] + ([dres] if has_res else [])
    return pl.pallas_call(
        body, out_shape=(jax.ShapeDtypeStruct((t, w), F32), jax.ShapeDtypeStruct((t, w), BF16),
                         jax.ShapeDtypeStruct((1, w), F32)),
        grid=(t // tr,), in_specs=[row, vec, dh_row] + ([row] if has_res else []),
        out_specs=(row, row, vec), compiler_params=_cparams(("arbitrary",)), name=name)(*ins)


FFN_ROWS = 512
FFN_SPLIT = 2
FFN_W_ROWS = 1024
SCAN_ROWS = 256


NT_DIMS = (((1,), (1,)), ((), ()))
TN_DIMS = (((0,), (0,)), ((), ()))


def _ffn_fwd_call(x, g, wg, wu, wd, name):
    t, d = x.shape
    ns, f, _ = wg.shape
    tm = _tile(t, FFN_ROWS)

    def body(x_ref, g_ref, wg_ref, wu_ref, wd_ref, xo_ref, h_ref, gate_ref, up_ref, h_sc, acc_ref):
        s = pl.program_id(1)

        @pl.when(s == 0)
        def _():
            xv = x_ref[...]
            r = lax.rsqrt(jnp.mean(xv * xv, axis=-1, keepdims=True) + EPS)
            hb = (xv * r * g_ref[...]).astype(BF16)
            h_sc[...] = hb
            h_ref[...] = hb

        for r0 in range(0, tm, tm // FFN_SPLIT):
            rows = slice(r0, r0 + tm // FFN_SPLIT)
            hb = h_sc[rows, :]
            gate = lax.dot_general(hb, wg_ref[0], NT_DIMS, preferred_element_type=F32)
            up = lax.dot_general(hb, wu_ref[0], NT_DIMS, preferred_element_type=F32)
            gate_ref[0, rows, :] = gate.astype(BF16)
            up_ref[0, rows, :] = up.astype(BF16)
            act = (gate * _sigmoid_tanh(gate) * up).astype(BF16)
            prod = jnp.dot(act, wd_ref[0], preferred_element_type=F32)
            acc_ref[rows, :] = prod + jnp.where(s > 0, acc_ref[rows, :], 0.0)

        @pl.when(s == ns - 1)
        def _():
            xo_ref[...] = x_ref[...] + 0.5 * acc_ref[...]

    row = pl.BlockSpec((tm, d), lambda i, s: (i, 0))
    vec = pl.BlockSpec((1, d), lambda i, s: (0, 0))
    wrow = pl.BlockSpec((1, f, d), lambda i, s: (s, 0, 0))
    hid = pl.BlockSpec((1, tm, f), lambda i, s: (s, i, 0))
    hid_sh = jax.ShapeDtypeStruct((ns, t, f), BF16)
    return pl.pallas_call(
        body, out_shape=(jax.ShapeDtypeStruct((t, d), F32), jax.ShapeDtypeStruct((t, d), BF16), hid_sh, hid_sh),
        grid=(t // tm, ns), in_specs=[row, vec, wrow, wrow, wrow], out_specs=(row, row, hid, hid),
        scratch_shapes=[pltpu.VMEM((tm, d), BF16), pltpu.VMEM((tm, d), F32)],
        compiler_params=_cparams(("parallel", "arbitrary")), name=name)(x, g, wg, wu, wd)


def _ffn_bwd_x_call(dxo, dxo_b, x, g, gate, up, wg, wu, wd, name):
    t, d = x.shape
    ns, f, _ = wg.shape
    tm = _tile(t, FFN_ROWS)

    def body(dxo_ref, dxb_ref, x_ref, g_ref, gate_ref, up_ref, wg_ref, wu_ref, wd_ref,
             dx_ref, dxob_ref, dgn_ref, dgate_ref, dup_ref, act_ref, dh_ref):
        i, s = pl.program_id(0), pl.program_id(1)
        for r0 in range(0, tm, tm // FFN_SPLIT):
            rows = slice(r0, r0 + tm // FFN_SPLIT)
            dact = lax.dot_general(dxb_ref[rows, :], wd_ref[0], NT_DIMS, preferred_element_type=F32) * 0.5
            gv = gate_ref[0, rows, :].astype(F32)
            uv = up_ref[0, rows, :].astype(F32)
            sg = _sigmoid_tanh(gv)
            silu = gv * sg
            act_ref[0, rows, :] = (silu * uv).astype(BF16)
            dub = (dact * silu).astype(BF16)
            dgb = (dact * uv * sg * (1.0 + gv * (1.0 - sg))).astype(BF16)
            dup_ref[0, rows, :] = dub
            dgate_ref[0, rows, :] = dgb
            prod = (jnp.dot(dgb, wg_ref[0], preferred_element_type=F32)
                    + jnp.dot(dub, wu_ref[0], preferred_element_type=F32))

            dh_ref[rows, :] = prod + jnp.where(s > 0, dh_ref[rows, :], 0.0)

        @pl.when(jnp.logical_and(i == 0, s == 0))
        def _():
            dgn_ref[...] = jnp.zeros_like(dgn_ref)

        @pl.when(s == ns - 1)
        def _():
            dx, dgs = _rms_bwd_rows(x_ref[...], g_ref[...], dh_ref[...])
            dx = dx + dxo_ref[...]
            dx_ref[...] = dx
            dxob_ref[...] = dx.astype(BF16)
            dgn_ref[...] += jnp.sum(dgs, axis=0, keepdims=True)

    row = pl.BlockSpec((tm, d), lambda i, s: (i, 0))
    vec = pl.BlockSpec((1, d), lambda i, s: (0, 0))
    wrow = pl.BlockSpec((1, f, d), lambda i, s: (s, 0, 0))
    hid = pl.BlockSpec((1, tm, f), lambda i, s: (s, i, 0))
    hid_sh = jax.ShapeDtypeStruct((ns, t, f), BF16)
    return pl.pallas_call(
        body,
        out_shape=(jax.ShapeDtypeStruct((t, d), F32), jax.ShapeDtypeStruct((t, d), BF16),
                   jax.ShapeDtypeStruct((1, d), F32), hid_sh, hid_sh, hid_sh),
        grid=(t // tm, ns), in_specs=[row, row, row, vec, hid, hid, wrow, wrow, wrow],
        out_specs=(row, row, vec, hid, hid, hid), scratch_shapes=[pltpu.VMEM((tm, d), F32)],
        compiler_params=_cparams(("arbitrary", "arbitrary")), name=name)(dxo, dxo_b, x, g, gate, up, wg, wu, wd)


def _ffn_bwd_w_call(h, dxo_b, dgate, dup, act, name):
    t, d = h.shape
    ns, _, f = dgate.shape
    tm = _tile(t, FFN_W_ROWS)
    nm = t // tm

    def body(h_ref, dxb_ref, dgate_ref, dup_ref, act_ref, dwg_ref, dwu_ref, dwd_ref, ag_ref, au_ref, ad_ref):
        i = pl.program_id(1)
        hv = h_ref[...]
        pg = lax.dot_general(dgate_ref[0], hv, TN_DIMS, preferred_element_type=F32)
        pu = lax.dot_general(dup_ref[0], hv, TN_DIMS, preferred_element_type=F32)
        pd = lax.dot_general(act_ref[0], dxb_ref[...], TN_DIMS, preferred_element_type=F32)

        ag_ref[...] = pg + jnp.where(i > 0, ag_ref[...], 0.0)
        au_ref[...] = pu + jnp.where(i > 0, au_ref[...], 0.0)
        ad_ref[...] = pd + jnp.where(i > 0, ad_ref[...], 0.0)

        @pl.when(i == nm - 1)
        def _():
            dwg_ref[0] = ag_ref[...].astype(BF16)
            dwu_ref[0] = au_ref[...].astype(BF16)
            dwd_ref[0] = (0.5 * ad_ref[...]).astype(BF16)

    row = pl.BlockSpec((tm, d), lambda s, i: (i, 0))
    hid = pl.BlockSpec((1, tm, f), lambda s, i: (s, i, 0))
    wrow = pl.BlockSpec((1, f, d), lambda s, i: (s, 0, 0))
    wsh = jax.ShapeDtypeStruct((ns, f, d), BF16)
    return pl.pallas_call(
        body, out_shape=(wsh, wsh, wsh),
        grid=(ns, nm), in_specs=[row, row, hid, hid, hid], out_specs=(wrow, wrow, wrow),
        scratch_shapes=[pltpu.VMEM((f, d), F32), pltpu.VMEM((f, d), F32), pltpu.VMEM((f, d), F32)],
        compiler_params=_cparams(("parallel", "arbitrary")), name=name)(h, dxo_b, dgate, dup, act)


def _loss_head(x, g, tgt, name):
    t, w = x.shape
    tr = _row_tile(t)

    def body(x_ref, g_ref, t_ref, loss_ref, dx_ref, dxb_ref, dg_ref):
        xv = x_ref[...]
        gv = g_ref[...]
        r = lax.rsqrt(jnp.mean(xv * xv, axis=-1, keepdims=True) + EPS)
        nrm = xv * r
        err = nrm * gv - t_ref[...]
        dout = err * (1.0 / w)
        dn = dout * gv
        dx = r * (dn - nrm * jnp.mean(dn * nrm, axis=-1, keepdims=True))
        dx_ref[...] = dx
        dxb_ref[...] = dx.astype(BF16)

        @pl.when(pl.program_id(0) == 0)
        def _():
            dg_ref[...] = jnp.zeros_like(dg_ref)
            loss_ref[...] = jnp.zeros_like(loss_ref)

        dg_ref[...] += jnp.sum(dout * nrm, axis=0, keepdims=True)
        part = jnp.sum(jnp.sum(err * err, axis=-1, keepdims=True) * (0.5 / w), axis=0, keepdims=True)
        loss_ref[...] += jnp.broadcast_to(part, loss_ref.shape)

    row = pl.BlockSpec((tr, w), lambda i: (i, 0))
    vec = pl.BlockSpec((1, w), lambda i: (0, 0))
    return pl.pallas_call(
        body, out_shape=(jax.ShapeDtypeStruct((1, LANES), F32), jax.ShapeDtypeStruct((t, w), F32),
                         jax.ShapeDtypeStruct((t, w), BF16), jax.ShapeDtypeStruct((1, w), F32)),
        grid=(t // tr,), in_specs=[row, vec, row],
        out_specs=(pl.BlockSpec((1, LANES), lambda i: (0, 0)), row, row, vec),
        compiler_params=_cparams(("arbitrary",)), name=name)(x, g, tgt)


def _attn_bias():
    slopes = np.asarray(2.0 ** (-8.0 * (np.arange(ATTN_HEADS) + 1) / ATTN_HEADS), np.float32)
    qi = np.arange(QBLOCK)[:, None]
    kj = np.arange(3 * QBLOCK)[None, :]
    rel = np.abs(kj - QBLOCK - qi).astype(np.float32)
    tile = np.where(rel <= WINDOW, -slopes[:, None, None] * rel[None], np.float32(NEG_INF)).astype(np.float32)
    tile = tile.reshape(KV_HEADS, GQ * QBLOCK, 3 * QBLOCK)
    return jnp.asarray(np.swapaxes(tile, 1, 2))


def _attn_scores(k3, q, n, nb, bias):
    s = lax.dot_general(k3, q, NT_DIMS, preferred_element_type=F32) * (HEAD_DIM ** -0.5)
    key = lax.broadcasted_iota(jnp.int32, (3 * QBLOCK, 1), 0)
    inside = (key >= jnp.where(n == 0, QBLOCK, 0)) & (key < jnp.where(n == nb - 1, 2 * QBLOCK, 3 * QBLOCK))
    return jnp.where(inside, s + bias, NEG_INF)


Q_COL, K_COL, V_COL, U_COL = 0, ATTN_WIDTH // LANES, ATTN_WIDTH // LANES + 1, ATTN_WIDTH // LANES + 2


def _key_rows(ref, n, nb):
    prev, nxt = jnp.maximum(n - 1, 0), jnp.minimum(n + 1, nb - 1)
    blk = lambda b: ref[pl.ds(pl.multiple_of(b * QBLOCK, QBLOCK), QBLOCK), :]
    return jnp.concatenate([blk(prev), blk(n), blk(nxt)], axis=0)


def _head_tiles(x, kh, low):
    tiles = []
    for g in range(GQ):
        h = GQ * kh + g
        t128 = x[:, LANES * (h // 2):LANES * (h // 2 + 1)]
        t128 = jnp.where(low if h % 2 == 0 else jnp.logical_not(low), t128, 0.0)
        if h % 2 != kh:
            t128 = pltpu.roll(t128, HEAD_DIM, 1)
        tiles.append(t128)
    return jnp.concatenate(tiles, axis=0)


def _head_merge(per_kh, low):
    out = []
    for j in range(ATTN_HEADS // 2):
        pair = []
        for h in (2 * j, 2 * j + 1):
            kh, g = h // GQ, h % GQ
            t128 = per_kh[kh][g * QBLOCK:(g + 1) * QBLOCK, :]
            if h % 2 != kh:
                t128 = pltpu.roll(t128, HEAD_DIM, 1)
            pair.append(t128)
        out.append(jnp.where(low, pair[0], pair[1]))
    return jnp.concatenate(out, axis=1)


def _attn_fwd_proj(proj, sink_rows, bias, name):
    t = proj.shape[0]
    nb = t // QBLOCK
    rows = GQ * QBLOCK

    def body(q_ref, k_ref, v_ref, sink_ref, bias_ref, o_ref, lse_ref):
        n = pl.program_id(0)
        low = lax.broadcasted_iota(jnp.int32, (QBLOCK, LANES), 1) < HEAD_DIM
        k3 = _key_rows(k_ref, n, nb).astype(BF16)
        v3 = _key_rows(v_ref, n, nb).astype(BF16)
        q = q_ref[...]
        outs = []
        for kh in range(KV_HEADS):
            qs = _head_tiles(q, kh, low).astype(BF16)
            s = _attn_scores(k3, qs, n, nb, bias_ref[kh])
            sink = sink_ref[kh]
            mx = jnp.maximum(jnp.max(s, axis=0, keepdims=True), sink)
            p = jnp.exp(s - mx)
            den = jnp.sum(p, axis=0, keepdims=True) + jnp.exp(sink - mx)
            pn = (p * (1.0 / den)).astype(BF16)
            outs.append(lax.dot_general(pn, v3, TN_DIMS, preferred_element_type=F32))
            lse_ref[0, kh] = mx + jnp.log(den)
        o_ref[...] = _head_merge(outs, low)

    strip = lambda col: pl.BlockSpec((t, LANES), lambda n, col=col: (0, col))
    rowspec = pl.BlockSpec((KV_HEADS, 1, rows), lambda n: (0, 0, 0))
    biasspec = pl.BlockSpec((KV_HEADS, 3 * QBLOCK, rows), lambda n: (0, 0, 0))
    return pl.pallas_call(
        body, out_shape=(jax.ShapeDtypeStruct((t, ATTN_WIDTH), F32), jax.ShapeDtypeStruct((nb, KV_HEADS, 1, rows), F32)),
        grid=(nb,), in_specs=[pl.BlockSpec((QBLOCK, ATTN_WIDTH), lambda n: (n, 0)), strip(K_COL), strip(V_COL),
                              rowspec, biasspec],
        out_specs=(pl.BlockSpec((QBLOCK, ATTN_WIDTH), lambda n: (n, 0)),
                   pl.BlockSpec((1, KV_HEADS, 1, rows), lambda n: (n, 0, 0, 0))),
        compiler_params=_cparams(("parallel",)), name=name)(proj, proj, proj, sink_rows, bias)


def _attn_bwd_proj(proj, sink_rows, bias, o, lse, do, name):
    t = proj.shape[0]
    nb = t // QBLOCK
    rows = GQ * QBLOCK
    scale = HEAD_DIM ** -0.5

    def body(q_ref, k_ref, v_ref, sink_ref, bias_ref, o_ref, lse_ref, do_ref, dq_ref, dk_ref, dv_ref, ds_ref):
        n = pl.program_id(0)

        @pl.when(n == 0)
        def _():
            dk_ref[...] = jnp.zeros_like(dk_ref)
            dv_ref[...] = jnp.zeros_like(dv_ref)
            ds_ref[...] = jnp.zeros_like(ds_ref)

        low = lax.broadcasted_iota(jnp.int32, (QBLOCK, LANES), 1) < HEAD_DIM
        k3 = _key_rows(k_ref, n, nb).astype(BF16)
        v3 = _key_rows(v_ref, n, nb).astype(BF16)
        q, dov = q_ref[...], do_ref[...]
        dod = dov * o_ref[...]
        dqs = []
        dk3 = jnp.zeros((3 * QBLOCK, LANES), F32)
        dv3 = jnp.zeros((3 * QBLOCK, LANES), F32)
        ones = jnp.ones((SUBLANES, LANES), F32)
        for kh in range(KV_HEADS):
            qs = _head_tiles(q, kh, low).astype(BF16)
            dos = _head_tiles(dov, kh, low).astype(BF16)
            delta = lax.dot_general(ones, _head_tiles(dod, kh, low), NT_DIMS, preferred_element_type=F32,
                                    precision=lax.Precision.HIGHEST)[0:1, :]
            lse_kh = lse_ref[0, kh]
            s = _attn_scores(k3, qs, n, nb, bias_ref[kh])
            p = jnp.exp(s - lse_kh)
            dp = lax.dot_general(v3, dos, NT_DIMS, preferred_element_type=F32)
            dsb = (p * (dp - delta)).astype(BF16)
            dqs.append(lax.dot_general(dsb, k3, TN_DIMS, preferred_element_type=F32) * scale)
            dk3 = dk3 + jnp.dot(dsb, qs, preferred_element_type=F32) * scale
            dv3 = dv3 + jnp.dot(p.astype(BF16), dos, preferred_element_type=F32)
            ds_ref[kh] += -jnp.exp(sink_ref[kh] - lse_kh) * delta
        dq_ref[...] = _head_merge(dqs, low)
        prev, nxt = jnp.maximum(n - 1, 0), jnp.minimum(n + 1, nb - 1)
        for j, b in enumerate((prev, n, nxt)):
            blk = pl.ds(pl.multiple_of(b * QBLOCK, QBLOCK), QBLOCK)
            dk_ref[blk, :] += dk3[j * QBLOCK:(j + 1) * QBLOCK, :]
            dv_ref[blk, :] += dv3[j * QBLOCK:(j + 1) * QBLOCK, :]

    strip = lambda col: pl.BlockSpec((t, LANES), lambda n, col=col: (0, col))
    rowspec = pl.BlockSpec((KV_HEADS, 1, rows), lambda n: (0, 0, 0))
    qspec = pl.BlockSpec((QBLOCK, ATTN_WIDTH), lambda n: (n, 0))
    kv_out = pl.BlockSpec((t, LANES), lambda n: (0, 0))
    biasspec = pl.BlockSpec((KV_HEADS, 3 * QBLOCK, rows), lambda n: (0, 0, 0))
    return pl.pallas_call(
        body,
        out_shape=(jax.ShapeDtypeStruct((t, ATTN_WIDTH), F32), jax.ShapeDtypeStruct((t, LANES), F32),
                   jax.ShapeDtypeStruct((t, LANES), F32), jax.ShapeDtypeStruct((KV_HEADS, 1, rows), F32)),
        grid=(nb,),
        in_specs=[qspec, strip(K_COL), strip(V_COL), rowspec, biasspec, qspec,
                  pl.BlockSpec((1, KV_HEADS, 1, rows), lambda n: (n, 0, 0, 0)), qspec],
        out_specs=(qspec, kv_out, kv_out, rowspec),
        compiler_params=_cparams(("arbitrary",)), name=name)(proj, proj, proj, sink_rows, bias, o, lse, do)


def _scan_tables(a_re, a_im, reverse):
    pw = [(a_re, a_im)]
    for _ in range(SUBLANES - 1):
        pr, pi = pw[-1]
        pw.append((pr * a_re - pi * a_im, pr * a_im + pi * a_re))
    rows = np.arange(SUBLANES)
    tabs = []
    for d in (1, 2, 4):
        mask = (rows <= SUBLANES - 1 - d) if reverse else (rows >= d)
        m = jnp.asarray(mask, F32)[:, None]
        tabs += [m * pw[d - 1][0][None, :], m * pw[d - 1][1][None, :]]
    order = (SUBLANES - 1 - rows) if reverse else rows
    tabs += [jnp.stack([pw[j][0] for j in order]), jnp.stack([pw[j][1] for j in order])]
    tab = jnp.stack(tabs)
    return tab.reshape(8, SUBLANES, N_STRIPS, STRIP_ST).transpose(2, 0, 1, 3)


def _scan_pair_chunk(dirs):
    nblk = dirs[0]['xr'].shape[0] // SUBLANES

    @pl.when(pl.program_id(1) == 0)
    def _():
        for d in dirs:
            d['carry'][...] = jnp.zeros_like(d['carry'])

    for d in dirs:
        vb = d['v'][...].astype(BF16)
        d['xr'][...] = jnp.dot(vb, d['mir'][0], preferred_element_type=F32)
        d['xi'][...] = jnp.dot(vb, d['mii'][0], preferred_element_type=F32)
    carries = [(d['carry'][0], d['carry'][1]) for d in dirs]
    for i in range(nblk):
        for k, d in enumerate(dirs):
            rev = d['reverse']
            rows = pl.ds(((nblk - 1 - i) if rev else i) * SUBLANES, SUBLANES)
            cr, ci = carries[k]
            xr, xi = d['xr'][rows, :], d['xi'][rows, :]
            for j, s in enumerate((1, 2, 4)):
                tr_, ti_ = d['tab'][0, 2 * j], d['tab'][0, 2 * j + 1]
                sh = (SUBLANES - s) if rev else s
                sr, si = pltpu.roll(xr, sh, 0), pltpu.roll(xi, sh, 0)
                xr, xi = xr + tr_ * sr - ti_ * si, xi + tr_ * si + ti_ * sr
            pr, pi = d['tab'][0, 6], d['tab'][0, 7]
            xr, xi = xr + pr * cr - pi * ci, xi + pr * ci + pi * cr
            d['xr'][rows, :] = xr
            d['xi'][rows, :] = xi
            edge = 0 if rev else SUBLANES - 1
            carries[k] = (jnp.broadcast_to(xr[edge:edge + 1, :], xr.shape),
                          jnp.broadcast_to(xi[edge:edge + 1, :], xi.shape))
    for k, d in enumerate(dirs):
        d['carry'][0], d['carry'][1] = carries[k]
        d['y'][...] = (jnp.dot(d['xr'][...].astype(BF16), d['mor'][0], preferred_element_type=F32)
                       + jnp.dot(d['xi'][...].astype(BF16), d['moi'][0], preferred_element_type=F32))


def _scan_pair(v, ops_f, ops_b, name):
    t = v.shape[0]
    tc = _tile(t, SCAN_ROWS)
    nc = t // tc

    def body(vf_ref, vb_ref, *refs):
        ops = refs[:10]
        outs = refs[10:16]
        carries = refs[16:18]
        dirs = []
        for k, (v_ref, rev) in enumerate(((vf_ref, False), (vb_ref, True))):
            mir, mii, tab, mor, moi = ops[5 * k:5 * k + 5]
            y, xr, xi = outs[3 * k:3 * k + 3]
            dirs.append(dict(v=v_ref, mir=mir, mii=mii, tab=tab, mor=mor, moi=moi, y=y, xr=xr, xi=xi,
                             carry=carries[k], reverse=rev))
        _scan_pair_chunk(dirs)

    col0 = v.shape[1] // STRIP_IN - N_STRIPS
    fmap = lambda s, c: (c, s)
    bmap = lambda s, c: (nc - 1 - c, s)
    smap3 = lambda s, c: (s, 0, 0)
    m_in = pl.BlockSpec((1, STRIP_IN, STRIP_ST), smap3)
    m_out = pl.BlockSpec((1, STRIP_ST, STRIP_IN), smap3)
    tabspec = pl.BlockSpec((1, 8, SUBLANES, STRIP_ST), lambda s, c: (s, 0, 0, 0))
    opspecs = [m_in, m_in, tabspec, m_out, m_out]
    y_sh = jax.ShapeDtypeStruct((t, SSM_WIDTH), F32)
    x_sh = jax.ShapeDtypeStruct((t, N_STRIPS * STRIP_ST), F32)
    outspecs = lambda m: [pl.BlockSpec((tc, STRIP_IN), m), pl.BlockSpec((tc, STRIP_ST), m),
                          pl.BlockSpec((tc, STRIP_ST), m)]
    res = pl.pallas_call(
        body, out_shape=[y_sh, x_sh, x_sh] * 2, grid=(N_STRIPS, nc),
        in_specs=[pl.BlockSpec((tc, STRIP_IN), lambda s, c: (c, s + col0)),
                  pl.BlockSpec((tc, STRIP_IN), lambda s, c: (nc - 1 - c, s + col0))] + opspecs * 2,
        out_specs=outspecs(fmap) + outspecs(bmap),
        scratch_shapes=[pltpu.VMEM((2, SUBLANES, STRIP_ST), F32)] * 2,
        compiler_params=_cparams(("parallel", "arbitrary")), name=name)(v, v, *ops_f, *ops_b)
    return tuple(res[:3]), tuple(res[3:])


def _scan_adjoint_pair(dy, u, states, adj_ops, name):
    t = dy.shape[0]
    tc = _tile(t, SCAN_ROWS)
    nc = t // tc
    hb = tc // SUBLANES
    n_out = 6

    def body(*refs):
        c = pl.program_id(1)
        dirs = []
        for k in range(2):
            dy_ref, mir, mii, tab, mor, moi, u_ref, xr_ref, xi_ref, hr_ref, hi_ref = refs[11 * k:11 * k + 11]
            outs = refs[22 + n_out * k:22 + n_out * (k + 1)]
            lr_ref, li_ref, carry = refs[22 + 2 * n_out + 3 * k:22 + 2 * n_out + 3 * k + 3]
            dirs.append(dict(v=dy_ref, mir=mir, mii=mii, tab=tab, mor=mor, moi=moi, y=outs[0], xr=lr_ref, xi=li_ref,
                             carry=carry, reverse=(k == 0), u=u_ref, fx=(xr_ref, xi_ref), halo=(hr_ref, hi_ref),
                             acc=outs[1:]))

        @pl.when(c == 0)
        def _():
            for d in dirs:
                for r in d['acc']:
                    r[...] = jnp.zeros_like(r)

        _scan_pair_chunk(dirs)
        for d in dirs:
            fwd_reverse = not d['reverse']
            rc = (nc - 1 - c) if d['reverse'] else c
            dmir_ref, dmii_ref, dmor_ref, dmoi_ref, da_ref = d['acc']
            xrv, xiv, lrv, liv = d['fx'][0][...], d['fx'][1][...], d['xr'][...], d['xi'][...]
            hr_ref, hi_ref = d['halo']
            row = lax.broadcasted_iota(jnp.int32, xrv.shape, 0)
            if fwd_reverse:
                live = (rc < nc - 1).astype(F32)
                edge_r, edge_i = hr_ref[0:1, :] * live, hi_ref[0:1, :] * live
                xpr = jnp.where(row == tc - 1, edge_r, pltpu.roll(xrv, tc - 1, 0))
                xpi = jnp.where(row == tc - 1, edge_i, pltpu.roll(xiv, tc - 1, 0))
            else:
                live = (rc > 0).astype(F32)
                edge_r, edge_i = hr_ref[SUBLANES - 1:SUBLANES, :] * live, hi_ref[SUBLANES - 1:SUBLANES, :] * live
                xpr = jnp.where(row == 0, edge_r, pltpu.roll(xrv, 1, 0))
                xpi = jnp.where(row == 0, edge_i, pltpu.roll(xiv, 1, 0))
            da_ref[0, 0:1, :] += jnp.sum(xpr * lrv + xpi * liv, axis=0, keepdims=True)
            da_ref[0, 1:2, :] += jnp.sum(xpr * liv - xpi * lrv, axis=0, keepdims=True)
            ub, dyb = d['u'][...].astype(BF16), d['v'][...].astype(BF16)
            dmir_ref[0] += lax.dot_general(ub, lrv.astype(BF16), TN_DIMS, preferred_element_type=F32)
            dmii_ref[0] += lax.dot_general(ub, liv.astype(BF16), TN_DIMS, preferred_element_type=F32)
            dmor_ref[0] += lax.dot_general(xrv.astype(BF16), dyb, TN_DIMS, preferred_element_type=F32)
            dmoi_ref[0] += lax.dot_general(xiv.astype(BF16), dyb, TN_DIMS, preferred_element_type=F32)

    col0 = u.shape[1] // STRIP_IN - N_STRIPS
    smap3 = lambda s, c: (s, 0, 0)
    m_in = pl.BlockSpec((1, STRIP_IN, STRIP_ST), smap3)
    m_out = pl.BlockSpec((1, STRIP_ST, STRIP_IN), smap3)
    tabspec = pl.BlockSpec((1, 8, SUBLANES, STRIP_ST), lambda s, c: (s, 0, 0, 0))
    in_specs, out_specs, args = [], [], []
    for k in range(2):
        reverse = k == 0
        rowblk = (lambda c: nc - 1 - c) if reverse else (lambda c: c)
        tmap = lambda s, c, rowblk=rowblk: (rowblk(c), s)
        umap = lambda s, c, rowblk=rowblk: (rowblk(c), s + col0)
        if not reverse:
            hmap = lambda s, c, rowblk=rowblk: (jnp.minimum((rowblk(c) + 1) * hb, t // SUBLANES - 1), s)
        else:
            hmap = lambda s, c, rowblk=rowblk: (jnp.maximum(rowblk(c) * hb - 1, 0), s)
        narrow = pl.BlockSpec((tc, STRIP_IN), tmap)
        wide = pl.BlockSpec((tc, STRIP_ST), tmap)
        halo = pl.BlockSpec((SUBLANES, STRIP_ST), hmap)
        in_specs += [narrow, m_in, m_in, tabspec, m_out, m_out, pl.BlockSpec((tc, STRIP_IN), umap), wide, wide,
                     halo, halo]
        out_specs += [narrow, m_in, m_in, m_out, m_out, pl.BlockSpec((1, SUBLANES, STRIP_ST), smap3)]
        xr, xi = states[k]
        args += [dy, *adj_ops[k], u, xr, xi, xr, xi]
    out_shape = [jax.ShapeDtypeStruct((t, SSM_WIDTH), F32),
                 jax.ShapeDtypeStruct((N_STRIPS, STRIP_IN, STRIP_ST), F32),
                 jax.ShapeDtypeStruct((N_STRIPS, STRIP_IN, STRIP_ST), F32),
                 jax.ShapeDtypeStruct((N_STRIPS, STRIP_ST, STRIP_IN), F32),
                 jax.ShapeDtypeStruct((N_STRIPS, STRIP_ST, STRIP_IN), F32),
                 jax.ShapeDtypeStruct((N_STRIPS, SUBLANES, STRIP_ST), F32)] * 2
    res = pl.pallas_call(
        body, out_shape=out_shape, grid=(N_STRIPS, nc), in_specs=in_specs, out_specs=out_specs,
        scratch_shapes=[pltpu.VMEM((tc, STRIP_ST), F32), pltpu.VMEM((tc, STRIP_ST), F32),
                        pltpu.VMEM((2, SUBLANES, STRIP_ST), F32)] * 2,
        compiler_params=_cparams(("parallel", "arbitrary")), name=name)(*args)
    return tuple(res[:n_out]), tuple(res[n_out:])


def _ssm_prep(lam_re, lam_im, log_dt, bt_re, bt_im, c_re, c_im):
    lr = jnp.minimum(lam_re, LAMBDA_RE_MAX)
    li = lam_im
    dt = jnp.exp(log_dt)[:, None]
    mag = jnp.exp(lr * dt)
    a_re = mag * jnp.cos(li * dt)
    a_im = mag * jnp.sin(li * dt)
    den = lr * lr + li * li
    coef_re = ((a_re - 1.0) * lr + a_im * li) / den
    coef_im = (a_im * lr - (a_re - 1.0) * li) / den
    bb_re = coef_re[:, None, :] * bt_re - coef_im[:, None, :] * bt_im
    bb_im = coef_re[:, None, :] * bt_im + coef_im[:, None, :] * bt_re
    eye = jnp.eye(SSM_GROUPS // N_STRIPS, dtype=F32)

    def strips(m):
        g, a, b = m.shape
        m4 = m.reshape(N_STRIPS, g // N_STRIPS, a, b)
        return jnp.einsum('sgab,gk->sgakb', m4, eye).reshape(N_STRIPS, g // N_STRIPS * a, g // N_STRIPS * b)

    mi_re = strips(bb_re)
    mi_im = strips(bb_im)
    mo_re = strips(jnp.swapaxes(c_re, 1, 2))
    mo_im = strips(-jnp.swapaxes(c_im, 1, 2))
    return a_re.reshape(-1), a_im.reshape(-1), mi_re, mi_im, mo_re, mo_im


def _gelu(x):
    c = math.sqrt(2.0 / math.pi)
    return 0.5 * x * (1.0 + jnp.tanh(c * (x + 0.044715 * x * x * x)))


def _gelu_grad(x):
    c = math.sqrt(2.0 / math.pi)
    th = jnp.tanh(c * (x + 0.044715 * x * x * x))
    return 0.5 * (1.0 + th) + 0.5 * x * (1.0 - th * th) * c * (1.0 + 3.0 * 0.044715 * x * x)


def _last_cols_specs(u, w, tr):
    half = w // 2
    first = (u.shape[1] - w) // half
    assert first * half == u.shape[1] - w
    return [pl.BlockSpec((tr, half), lambda i, k=k: (i, first + k)) for k in range(2)]


def _ssm_post_fwd(u, yf, yb, d, wglu, bglu, name):
    t, w = yf.shape
    tr = _row_tile(t)

    def body(ua_ref, ub_ref, yf_ref, yb_ref, d_ref, w_ref, b_ref, s_ref, y0_ref, z_ref):
        uv = jnp.concatenate([ua_ref[...], ub_ref[...]], axis=1)
        y0 = d_ref[...] * uv + yf_ref[...] + yb_ref[...]
        yg = _gelu(y0)
        z = jnp.dot(yg.astype(BF16), w_ref[...], preferred_element_type=F32) + b_ref[...]
        s_ref[...] = yg * _sigmoid(z)
        y0_ref[...] = y0
        z_ref[...] = z

    row = pl.BlockSpec((tr, w), lambda i: (i, 0))
    vec = pl.BlockSpec((1, w), lambda i: (0, 0))
    mat = pl.BlockSpec((w, w), lambda i: (0, 0))
    sh = jax.ShapeDtypeStruct((t, w), F32)
    return pl.pallas_call(body, out_shape=(sh, sh, sh), grid=(t // tr,),
                          in_specs=[*_last_cols_specs(u, w, tr), row, row, vec, mat, vec], out_specs=(row, row, row),
                          compiler_params=_cparams(("parallel",)), name=name)(u, u, yf, yb, d, wglu, bglu)


def _ssm_post_bwd(ds, y0, z, u, d, wglu, name):
    t, w = ds.shape
    tr = _row_tile(t)

    def body(ds_ref, y0_ref, z_ref, ua_ref, ub_ref, d_ref, w_ref, dy0_ref, dw_ref, db_ref, dd_ref):
        @pl.when(pl.program_id(0) == 0)
        def _():
            dw_ref[...] = jnp.zeros_like(dw_ref)
            db_ref[...] = jnp.zeros_like(db_ref)
            dd_ref[...] = jnp.zeros_like(dd_ref)

        y0 = y0_ref[...]
        yg = _gelu(y0)
        sg = _sigmoid(z_ref[...])
        dsv = ds_ref[...]
        dz = dsv * yg * sg * (1.0 - sg)
        dzb = dz.astype(BF16)
        dyg = dsv * sg + lax.dot_general(dzb, w_ref[...], (((1,), (1,)), ((), ())), preferred_element_type=F32)
        dy0 = dyg * _gelu_grad(y0)
        dy0_ref[...] = dy0
        dw_ref[...] += lax.dot_general(yg.astype(BF16), dzb, (((0,), (0,)), ((), ())), preferred_element_type=F32)
        db_ref[...] += jnp.sum(dz, axis=0, keepdims=True)
        uv = jnp.concatenate([ua_ref[...], ub_ref[...]], axis=1)
        dd_ref[...] += jnp.sum(dy0 * uv, axis=0, keepdims=True)

    row = pl.BlockSpec((tr, w), lambda i: (i, 0))
    vec = pl.BlockSpec((1, w), lambda i: (0, 0))
    mat = pl.BlockSpec((w, w), lambda i: (0, 0))
    return pl.pallas_call(
        body, out_shape=(jax.ShapeDtypeStruct((t, w), F32), jax.ShapeDtypeStruct((w, w), F32),
                         jax.ShapeDtypeStruct((1, w), F32), jax.ShapeDtypeStruct((1, w), F32)),
        grid=(t // tr,), in_specs=[row, row, row, *_last_cols_specs(u, w, tr), vec, mat],
        out_specs=(row, mat, vec, vec),
        compiler_params=_cparams(("arbitrary",)), name=name)(ds, y0, z, u, u, d, wglu)


def _du_combine(dy0, d, du_f, du_b, name):
    t, w = dy0.shape
    tr = _row_tile(t)

    def body(dy_ref, d_ref, a_ref, b_ref, o_ref):
        o_ref[...] = d_ref[...] * dy_ref[...] + a_ref[...] + b_ref[...]

    row = pl.BlockSpec((tr, w), lambda i: (i, 0))
    vec = pl.BlockSpec((1, w), lambda i: (0, 0))
    return pl.pallas_call(body, out_shape=jax.ShapeDtypeStruct((t, w), F32), grid=(t // tr,),
                          in_specs=[row, vec, row, row], out_specs=row, compiler_params=_cparams(("parallel",)),
                          name=name)(dy0, d, du_f, du_b)


def _ffn_fwd(x, g, wg, wu, wd, tag):
    xo, h, gate, up = _ffn_fwd_call(x, g, wg, wu, wd, f"{tag}_fwd")
    return xo, (h, gate, up)


def _ffn_bwd(dxo, dxo_b, x, g, wg, wu, wd, saved, tag):
    h, gate, up = saved
    dx, dx_b, dg, dgate, dup, act = _ffn_bwd_x_call(dxo, dxo_b, x, g, gate, up, wg, wu, wd, f"{tag}_bwd_x")
    dwg, dwu, dwd = _ffn_bwd_w_call(h, dxo_b, dgate, dup, act, f"{tag}_bwd_w")
    return dx, dx_b, dg, dwg, dwu, dwd


def _local_step(x, tgt, w, get_weights, put_grads, reduce_wide):
    t = x.shape[0]
    row = lambda a: a.reshape(1, -1)
    grads = {}

    w = dict(w)

    ssm_names = ['ssm_lambda_re', 'ssm_lambda_im', 'ssm_log_dt', 'ssm_b_re', 'ssm_b_im', 'ssm_c_re', 'ssm_c_im']
    tr3 = lambda m: jnp.swapaxes(m, 1, 2)
    fwd_ops, adj_ops, vjps = [], [], []
    for direction in range(2):
        rev = direction == 1
        prep, vjp = jax.vjp(_ssm_prep, *[w[n][direction] for n in ssm_names])
        a_re, a_im = prep[0], prep[1]
        mi_re, mi_im, mo_re, mo_im = (m.astype(BF16) for m in prep[2:])
        fwd_ops.append((mi_re, mi_im, _scan_tables(a_re, a_im, rev), mo_re, mo_im))
        adj_ops.append((tr3(mo_re), tr3(mo_im), _scan_tables(a_re, -a_im, not rev), tr3(mi_re), tr3(mi_im)))
        vjps.append(vjp)
    sink_rows = jnp.repeat(w['attn_sinks'].reshape(KV_HEADS, GQ), QBLOCK, axis=1)[:, None, :]
    bias = _attn_bias()
    prepared = sum(jnp.sum(op[:1, :1].astype(F32)) for ops in fwd_ops + adj_ops for op in ops) + sink_rows[0, 0, 0]

    w.update(get_weights('ffn1', prepared.reshape(1, 1)))
    x1, ffn1_saved = _ffn_fwd(x, w['norm_ffn1'], w['ffn1_w_gate'], w['ffn1_w_up'], w['ffn1_w_down'], "ffn1")
    w.update(get_weights('mix', x1))

    proj, h2 = _norm_mm([x1], [w['norm_mix']], w['w_in'], tb=True, res=None, name="in_proj")
    u = proj

    attn, lse = _attn_fwd_proj(proj, sink_rows, bias, "attn_fwd")

    (y_f, *states_f), (y_b, *states_b) = _scan_pair(u, fwd_ops[0], fwd_ops[1], "s5_fwd")
    ys, states = [y_f, y_b], [states_f, states_b]
    d_row = row(w['ssm_d'])
    s, y0, z = _ssm_post_fwd(u, ys[0], ys[1], d_row, w['ssm_glu_w'], row(w['ssm_glu_b']), "ssm_post")

    x2, mixed = _norm_mm([attn, s], [row(w['attn_out_norm']), row(w['ssm_out_norm'])], w['w_out'], tb=False,
                         res=x1, name="out_proj")

    w.update(get_weights('ffn2', x2))
    x3, ffn2_saved = _ffn_fwd(x2, w['norm_ffn2'], w['ffn2_w_gate'], w['ffn2_w_up'], w['ffn2_w_down'], "ffn2")

    loss, dx3, dx3_b, dgf = _loss_head(x3, row(w['final_norm']), tgt, "loss_head")
    grads['final_norm'] = dgf.reshape(w['final_norm'].shape)

    dx2, dx2_b, dg, dwg, dwu, dwd = _ffn_bwd(dx3, dx3_b, x2, w['norm_ffn2'], w['ffn2_w_gate'], w['ffn2_w_up'],
                                             w['ffn2_w_down'], ffn2_saved, "ffn2")
    grads['norm_ffn2'] = dg
    sent = put_grads('ffn2', dict(ffn2_w_gate=dwg, ffn2_w_up=dwu, ffn2_w_down=dwd))

    dmixed = _mm(dx2_b, w['w_out'], tb=True, reduce_s=True, after=sent, name="out_proj_dx")
    dw_out = _mm(mixed, dx2_b, ta=True, out_dtype=BF16, name="out_proj_dw")[0]
    dattn, _, dga = _rms_bwd(attn, row(w['attn_out_norm']), dmixed, None, "attn_out_dnorm", dh_col=0)
    ds, _, dgs = _rms_bwd(s, row(w['ssm_out_norm']), dmixed, None, "ssm_out_dnorm", dh_col=1)
    grads.update(attn_out_norm=dga, ssm_out_norm=dgs)

    dy0, dwglu, dbglu, dd = _ssm_post_bwd(ds, y0, z, u, d_row, w['ssm_glu_w'], "ssm_post_bwd")
    grads['ssm_glu_b'] = dbglu
    grads['ssm_d'] = dd.reshape(w['ssm_d'].shape)
    dparams, du_dirs = [], []
    for direction, res in enumerate(_scan_adjoint_pair(dy0, u, states, adj_ops, "s5_adj")):
        du_dir, dmir, dmii, dmor, dmoi, da = res
        du_dirs.append(du_dir)
        da_re = da[:, 0, :].reshape(-1)
        da_im = da[:, 1, :].reshape(-1)
        dparams.append(vjps[direction]((da_re, da_im, dmir, dmii, dmor, dmoi)))
    du = _du_combine(dy0, d_row, du_dirs[0], du_dirs[1], "ssm_du")
    for i, n in enumerate(ssm_names):
        grads[n] = jnp.stack([dparams[0][i], dparams[1][i]])
    wide_sum = reduce_wide(grads)

    dq, dk, dv, dsink = _attn_bwd_proj(proj, sink_rows, bias, attn, lse, dattn, "attn_bwd")
    grads['attn_sinks'] = jnp.sum(dsink.reshape(ATTN_HEADS, QBLOCK), axis=-1).reshape(w['attn_sinks'].shape)
    dproj = jnp.concatenate([dq, dk, dv, du], axis=-1).astype(BF16)

    dw_in = _mm(dproj, h2, ta=True, out_dtype=BF16, after=wide_sum, name="in_proj_dw")[0]
    sent = put_grads('mix', dict(w_in=dw_in, ssm_glu_w=dwglu, w_out=dw_out))
    dh2 = _mm(dproj, w['w_in'], reduce_s=True, after=sent, name="in_proj_dx")
    dx1, dx1_b, dgm = _rms_bwd(x1, w['norm_mix'], dh2, dx2, "mix_dnorm")
    grads['norm_mix'] = dgm

    dx0, _, dg, dwg, dwu, dwd = _ffn_bwd(dx1, dx1_b, x, w['norm_ffn1'], w['ffn1_w_gate'], w['ffn1_w_up'],
                                         w['ffn1_w_down'], ffn1_saved, "ffn1")
    grads['norm_ffn1'] = dg
    put_grads('ffn1', dict(ffn1_w_gate=dwg, ffn1_w_up=dwu, ffn1_w_down=dwd))
    return loss, dx0, grads, wide_sum


HBM_SPEC = pl.BlockSpec(memory_space=pl.ANY)


def _chip_peers(x, y):
    return [(1 - x, y), (x, 1 - y), (1 - x, 1 - y)]


HBM_ONLY = pl.BlockSpec(memory_space=pltpu.HBM)
SEM_SPEC = pl.BlockSpec(memory_space=pltpu.SEMAPHORE)
EFFECT = pltpu.SideEffectType.DATAFLOW_SIDE_EFFECTING


def _place_own(srcs, slot, name):
    na = len(srcs)
    r, c = srcs[0].shape
    tr = r // 2

    def body(slot_ref, *refs):
        for a in range(na):
            refs[na + a][0] = refs[a][...]

    return pl.pallas_call(
        body, out_shape=[jax.ShapeDtypeStruct((N_CHIPS, r, c), s.dtype) for s in srcs],
        grid_spec=pltpu.PrefetchScalarGridSpec(
            num_scalar_prefetch=1, grid=(2,), in_specs=[pl.BlockSpec((tr, c), lambda i, s: (i, 0))] * na,
            out_specs=[pl.BlockSpec((1, tr, c), lambda i, s: (s[0], i, 0))] * na),
        compiler_params=_cparams(("parallel",)), name=name)(slot, *srcs)


def _chip_copies(srcs, lands, send_sems, recv_sems, scatter, landed):
    x, y, c = lax.axis_index("x"), lax.axis_index("y"), lax.axis_index("c")
    me = 2 * x + y
    out = []
    for i in range(len(srcs)):
        for j, (px, py) in enumerate(_chip_peers(x, y)):
            p = 2 * px + py
            slot = p if landed else me
            if scatter:
                src, dst = srcs[i].at[p], lands[i].at[slot]
            else:
                rows = _core_half(srcs[i].shape[0], c)
                src, dst = srcs[i].at[rows], lands[i].at[slot, rows]
            out.append(pltpu.make_async_remote_copy(src, dst, send_sems.at[3 * i + j], recv_sems.at[3 * i + j],
                                                    device_id=(px, py, c), device_id_type=MESH))
    return out


def _core_half(nrows, c):
    half = nrows // 2
    return pl.ds(pl.multiple_of(c * half, 16), half)


def _sibling_forward(lands, name):
    n = len(lands)

    def body(*refs):
        bufs = refs[n:2 * n]
        send_sems, recv_sems = refs[2 * n:]
        x, y, c = lax.axis_index("x"), lax.axis_index("y"), lax.axis_index("c")
        mine = [_core_half(b.shape[1], c) for b in bufs]
        theirs = [_core_half(b.shape[1], 1 - c) for b in bufs]
        chips = [2 * px + py for px, py in _chip_peers(x, y)]
        cps = [pltpu.make_async_remote_copy(bufs[i].at[p, mine[i]], bufs[i].at[p, mine[i]], send_sems.at[3 * i + j],
                                            recv_sems.at[3 * i + j], device_id=(x, y, 1 - c), device_id_type=MESH)
               for i in range(n) for j, p in enumerate(chips)]
        for cp in cps:
            cp.start()
        for i in range(n):
            for j, p in enumerate(chips):
                pltpu.make_async_remote_copy(bufs[i].at[p, mine[i]], bufs[i].at[p, theirs[i]], send_sems.at[3 * i + j],
                                             recv_sems.at[3 * i + j], device_id=(x, y, 1 - c),
                                             device_id_type=MESH).wait()

    return pl.pallas_call(
        body, out_shape=[jax.ShapeDtypeStruct(a.shape, a.dtype) for a in lands],
        in_specs=[HBM_SPEC] * n, out_specs=[HBM_SPEC] * n, input_output_aliases={k: k for k in range(n)},
        scratch_shapes=[pltpu.SemaphoreType.DMA((3 * n,)), pltpu.SemaphoreType.DMA((3 * n,))],
        name=name)(*lands)


def _exchange_start(groups, scatter, name, after=None):
    sizes = [len(srcs) for srcs, _ in groups]
    flat_src = [a for srcs, _ in groups for a in srcs]
    flat_land = [a for _, lands in groups for a in lands]
    n = len(flat_src)
    ng = len(groups)

    def body(*refs):
        src_refs, land_refs = refs[:n], refs[n:2 * n]
        n_in = 2 * n + (after is not None)
        sems = refs[n_in:n_in + 2 * ng]
        token_ref = refs[-1]
        off = 0
        for gi, sz in enumerate(sizes):
            for cp in _chip_copies(src_refs[off:off + sz], land_refs[off:off + sz], sems[2 * gi], sems[2 * gi + 1],
                                   scatter, landed=False):
                cp.start()
            off += sz
        token_ref[...] = jnp.zeros_like(token_ref)

    sem_shapes = []
    for sz in sizes:
        sem_shapes += [pltpu.SemaphoreType.DMA((3 * sz,)), pltpu.SemaphoreType.DMA((3 * sz,))]
    hbm = lambda a: pltpu.HBM(a.shape, a.dtype)
    res = pl.pallas_call(
        body, name=name,
        out_shape=(tuple(sem_shapes) + tuple(hbm(a) for a in flat_src) + tuple(hbm(a) for a in flat_land)
                   + (jax.ShapeDtypeStruct((SUBLANES, LANES), F32),)),
        in_specs=[HBM_ONLY] * (2 * n) + [HBM_SPEC] * (after is not None),
        out_specs=tuple([SEM_SPEC] * (2 * ng) + [HBM_ONLY] * (2 * n) + [pl.BlockSpec(memory_space=pltpu.VMEM)]),
        input_output_aliases={k: 2 * ng + k for k in range(2 * n)},
        compiler_params=pltpu.CompilerParams(has_side_effects=EFFECT),
    )(*[pltpu.with_memory_space_constraint(a, pltpu.HBM) for a in flat_src + flat_land],
      *([after] if after is not None else []))
    sems, thru_src, thru_land = res[:2 * ng], res[2 * ng:2 * ng + n], res[2 * ng + n:2 * ng + 2 * n]
    out, off = [], 0
    for gi, sz in enumerate(sizes):
        out.append((sems[2 * gi], sems[2 * gi + 1], list(thru_src[off:off + sz]), list(thru_land[off:off + sz])))
        off += sz
    return out, res[-1]


def _exchange_wait(started, after, scatter, name):
    send_sems, recv_sems, srcs, lands = started
    n = len(srcs)

    def body(*refs):
        src_refs, land_refs = refs[:n], refs[n:2 * n]
        send_ref, recv_ref = refs[2 * n], refs[2 * n + 1]
        for cp in _chip_copies(src_refs, land_refs, send_ref, recv_ref, scatter, landed=True):
            cp.wait_send()
            cp.wait_recv()

    hbm = lambda a: pltpu.HBM(a.shape, a.dtype)
    res = pl.pallas_call(
        body, name=name, out_shape=tuple(hbm(a) for a in srcs) + tuple(hbm(a) for a in lands),
        in_specs=[HBM_ONLY] * (2 * n) + [SEM_SPEC, SEM_SPEC, HBM_SPEC], out_specs=tuple([HBM_ONLY] * (2 * n)),
        input_output_aliases={k: k for k in range(2 * n)},
        compiler_params=pltpu.CompilerParams(has_side_effects=EFFECT),
    )(*srcs, *lands, send_sems, recv_sems, after)
    return list(res[:n]), list(res[n:])


def _half_swap(parts, name):
    n = len(parts)

    def body(*refs):
        ins, outs = refs[:n], refs[n:2 * n]
        send_sems, recv_sems = refs[2 * n:]
        x, y, c = lax.axis_index("x"), lax.axis_index("y"), lax.axis_index("c")
        cps = [pltpu.make_async_remote_copy(ins[i].at[k, _core_half(ins[i].shape[1], 1 - c)], outs[i].at[k],
                                            send_sems.at[N_CHIPS * i + k], recv_sems.at[N_CHIPS * i + k],
                                            device_id=(x, y, 1 - c), device_id_type=MESH)
               for i in range(n) for k in range(N_CHIPS)]
        for cp in cps:
            cp.start()
        for cp in cps:
            cp.wait()

    return pl.pallas_call(
        body, out_shape=[jax.ShapeDtypeStruct((N_CHIPS, p.shape[1] // 2, p.shape[2]), p.dtype) for p in parts],
        in_specs=[HBM_SPEC] * n, out_specs=[HBM_SPEC] * n,
        scratch_shapes=[pltpu.SemaphoreType.DMA((N_CHIPS * n,)), pltpu.SemaphoreType.DMA((N_CHIPS * n,))],
        name=name)(*parts)


def _half_add(parts, sib, slots, name):
    na = len(parts)
    _, r, c = parts[0].shape
    hr = r // 2
    tr = _row_tile(hr, 512)
    nt = hr // tr

    def body(slot_ref, *refs):
        for a in range(na):
            refs[2 * na + a][...] = (refs[2 * a][...].astype(F32) + refs[2 * a + 1][...].astype(F32)).astype(BF16)

    mine = pl.BlockSpec((1, tr, c), lambda k, i, s: (k, i + s[4] * nt, 0))
    half = pl.BlockSpec((1, tr, c), lambda k, i, s: (k, i, 0))
    args = [a for p, sb in zip(parts, sib) for a in (p, sb)]
    return pl.pallas_call(
        body, out_shape=[jax.ShapeDtypeStruct((N_CHIPS, hr, c), BF16)] * na,
        grid_spec=pltpu.PrefetchScalarGridSpec(
            num_scalar_prefetch=1, grid=(N_CHIPS, nt), in_specs=[mine, half] * na, out_specs=[half] * na),
        compiler_params=_cparams(("parallel", "parallel")), name=name)(slots, *args)


def _half_forward(arrs, name):
    n = len(arrs)

    def body(*refs):
        bufs = refs[n:2 * n]
        send_sems, recv_sems = refs[2 * n:]
        x, y, c = lax.axis_index("x"), lax.axis_index("y"), lax.axis_index("c")
        cps = [pltpu.make_async_remote_copy(b.at[_core_half(b.shape[0], c)], b.at[_core_half(b.shape[0], c)],
                                            send_sems.at[i], recv_sems.at[i], device_id=(x, y, 1 - c),
                                            device_id_type=MESH) for i, b in enumerate(bufs)]
        for cp in cps:
            cp.start()
        for i, b in enumerate(bufs):
            pltpu.make_async_remote_copy(b.at[_core_half(b.shape[0], c)], b.at[_core_half(b.shape[0], 1 - c)],
                                         send_sems.at[i], recv_sems.at[i], device_id=(x, y, 1 - c),
                                         device_id_type=MESH).wait()

    return pl.pallas_call(
        body, out_shape=[jax.ShapeDtypeStruct(a.shape, a.dtype) for a in arrs],
        in_specs=[HBM_SPEC] * n, out_specs=[HBM_SPEC] * n, input_output_aliases={k: k for k in range(n)},
        scratch_shapes=[pltpu.SemaphoreType.DMA((n,)), pltpu.SemaphoreType.DMA((n,))],
        name=name)(*arrs)


def _small_exchange(smalls, name):
    nsm = len(smalls)
    rels = [(fx, fy, fc) for fx in (0, 1) for fy in (0, 1) for fc in (0, 1)][1:]

    def body(*refs):
        sins, souts = refs[:nsm], refs[nsm:2 * nsm]
        ssend, srecv, slocal = refs[2 * nsm:]
        x, y, c = lax.axis_index("x"), lax.axis_index("y"), lax.axis_index("c")
        lin = 4 * x + 2 * y + c
        local = [pltpu.make_async_copy(sins[i], souts[i].at[lin], slocal.at[i]) for i in range(nsm)]
        for cp in local:
            cp.start()
        for i in range(nsm):
            for j, (fx, fy, fc) in enumerate(rels):
                pltpu.make_async_remote_copy(sins[i], souts[i].at[lin], ssend.at[i, j], srecv.at[i, j],
                                             device_id=(x ^ fx, y ^ fy, c ^ fc), device_id_type=MESH).start()
        for i in range(nsm):
            for j, (fx, fy, fc) in enumerate(rels):
                src = 4 * (x ^ fx) + 2 * (y ^ fy) + (c ^ fc)
                pltpu.make_async_remote_copy(sins[i], souts[i].at[src], ssend.at[i, j], srecv.at[i, j],
                                             device_id=(x ^ fx, y ^ fy, c ^ fc), device_id_type=MESH).wait()
        for cp in local:
            cp.wait()

    return pl.pallas_call(
        body, out_shape=[jax.ShapeDtypeStruct((N_DEV,) + s.shape, s.dtype) for s in smalls],
        in_specs=[HBM_SPEC] * nsm, out_specs=[HBM_SPEC] * nsm,
        scratch_shapes=[pltpu.SemaphoreType.DMA((nsm, 7)), pltpu.SemaphoreType.DMA((nsm, 7)),
                        pltpu.SemaphoreType.DMA((nsm,))],
        name=name)(*smalls)


def _sum_parts(parts, recv, slots, name):
    na = len(parts)
    _, r, c = parts[0].shape
    tr = _row_tile(r, 192)

    def body(slot_ref, *refs):
        for a in range(na):
            own_ref, r0_ref, r1_ref, r2_ref = refs[4 * a:4 * a + 4]
            refs[4 * na + a][...] = ((own_ref[0].astype(F32) + r0_ref[0].astype(F32))
                                     + (r1_ref[0].astype(F32) + r2_ref[0].astype(F32)))

    blk = lambda k: pl.BlockSpec((1, tr, c), lambda i, s, k=k: (s[k], i, 0))
    out_blk = pl.BlockSpec((tr, c), lambda i, s: (i + s[4] * (r // tr), 0))
    args = [a for p, rv in zip(parts, recv) for a in (p, rv, rv, rv)]
    return pl.pallas_call(
        body, out_shape=[jax.ShapeDtypeStruct((2 * r, c), F32)] * na,
        grid_spec=pltpu.PrefetchScalarGridSpec(
            num_scalar_prefetch=1, grid=(r // tr,), in_specs=[blk(0), blk(1), blk(2), blk(3)] * na,
            out_specs=[out_blk] * na),
        compiler_params=_cparams(("parallel",)), name=name)(slots, *args)


def _small_allreduce(packed, name):
    rows = packed.shape[0]
    pr = rows // N_DEV
    rels = [(fx, fy, fc) for fx in (0, 1) for fy in (0, 1) for fc in (0, 1)][1:]

    def body(in_ref, out_ref, recv_ref, send1, recv1, send2, recv2):
        x, y, c = lax.axis_index("x"), lax.axis_index("y"), lax.axis_index("c")
        lin = 4 * x + 2 * y + c
        piece = lambda ref, k: ref.at[pl.ds(pl.multiple_of(k * pr, pr), pr), :]
        peers = [((x ^ fx, y ^ fy, c ^ fc), 4 * (x ^ fx) + 2 * (y ^ fy) + (c ^ fc)) for fx, fy, fc in rels]
        for j, (dev, plin) in enumerate(peers):
            pltpu.make_async_remote_copy(piece(in_ref, plin), recv_ref.at[lin], send1.at[j], recv1.at[j],
                                         device_id=dev, device_id_type=MESH).start()
        recv_ref[lin] = piece(in_ref, lin)[...]
        for j, (dev, plin) in enumerate(peers):
            pltpu.make_async_remote_copy(piece(in_ref, plin), recv_ref.at[plin], send1.at[j], recv1.at[j],
                                         device_id=dev, device_id_type=MESH).wait()
        acc = recv_ref[0]
        for k in range(1, N_DEV):
            acc = acc + recv_ref[k]
        piece(out_ref, lin)[...] = acc
        for j, (dev, plin) in enumerate(peers):
            pltpu.make_async_remote_copy(piece(out_ref, lin), piece(out_ref, lin), send2.at[j], recv2.at[j],
                                         device_id=dev, device_id_type=MESH).start()
        for j, (dev, plin) in enumerate(peers):
            pltpu.make_async_remote_copy(piece(out_ref, lin), piece(out_ref, plin), send2.at[j], recv2.at[j],
                                         device_id=dev, device_id_type=MESH).wait()

    vm = pl.BlockSpec(memory_space=pltpu.VMEM)
    return pl.pallas_call(
        body, out_shape=jax.ShapeDtypeStruct(packed.shape, F32), in_specs=[vm], out_specs=vm,
        scratch_shapes=[pltpu.VMEM((N_DEV, pr, LANES), F32)] + [pltpu.SemaphoreType.DMA((7,))] * 4,
        compiler_params=pltpu.CompilerParams(vmem_limit_bytes=VMEM_LIMIT), name=name)(packed)


def _adamw_math(w, m, v, g):
    nm = ADAM_B1 * m + (1.0 - ADAM_B1) * g
    nv = ADAM_B2 * v + (1.0 - ADAM_B2) * (g * g)
    m_hat = nm * (1.0 / (1.0 - ADAM_B1 ** ADAM_STEP))
    v_hat = nv * (1.0 / (1.0 - ADAM_B2 ** ADAM_STEP))
    return -ADAM_LR * (m_hat / (jnp.sqrt(v_hat) + ADAM_EPS) + ADAM_WD * w), nm, nv


def _adamw(ws, ms, vs, gs, name):
    na = len(ws)
    r, c = ws[0].shape
    tr = _row_tile(r)

    def body(*refs):
        for a in range(na):
            w_ref, m_ref, v_ref, g_ref = refs[4 * a:4 * a + 4]
            d_ref, nm_ref, nv_ref = refs[4 * na + 3 * a:4 * na + 3 * a + 3]
            d_ref[...], nm_ref[...], nv_ref[...] = _adamw_math(w_ref[...], m_ref[...], v_ref[...], g_ref[...])

    blk = pl.BlockSpec((tr, c), lambda i: (i, 0))
    sh = jax.ShapeDtypeStruct((r, c), F32)
    args = [a for group in zip(ws, ms, vs, gs) for a in group]
    res = pl.pallas_call(body, out_shape=[sh] * (3 * na), grid=(r // tr,), in_specs=[blk] * (4 * na),
                         out_specs=[blk] * (3 * na), compiler_params=_cparams(("parallel",)), name=name)(*args)
    return [tuple(res[3 * a:3 * a + 3]) for a in range(na)]


def _adamw_small(ws, ms, vs, alls, split, name):
    n = len(ws)
    lead = split if split is not None else ()
    nl = len(lead)
    nslots = alls[0].shape[0]

    def blocks(shape):
        if split is None:
            return tuple(shape), (lambda *g: (0,) * len(shape))
        blk = (shape[0], shape[1] // lead[0], shape[2] // lead[1]) + tuple(shape[3:])
        return blk, (lambda *g: (0, g[0], g[1]) + (0,) * (len(shape) - 3))

    def body(*refs):
        w_refs, m_refs, v_refs, a_refs = (refs[k * n:(k + 1) * n] for k in range(4))
        g_refs, d_refs, nm_refs, nv_refs = (refs[(4 + k) * n:(5 + k) * n] for k in range(4))
        k = pl.program_id(nl)
        for i in range(n):
            @pl.when(k == 0)
            def _(i=i):
                g_refs[i][...] = a_refs[i][0]

            @pl.when(k > 0)
            def _(i=i):
                g_refs[i][...] += a_refs[i][0]

            @pl.when(k == nslots - 1)
            def _(i=i):
                d_refs[i][...], nm_refs[i][...], nv_refs[i][...] = _adamw_math(
                    w_refs[i][...], m_refs[i][...], v_refs[i][...], g_refs[i][...])

    specs, aspecs, shapes = [], [], []
    for wa in ws:
        blk, imap = blocks(wa.shape)
        specs.append(pl.BlockSpec(blk, imap))
        aspecs.append(pl.BlockSpec((1,) + blk, (lambda *g, imap=imap: (g[nl],) + imap(*g))))
        shapes.append(jax.ShapeDtypeStruct(wa.shape, F32))
    res = pl.pallas_call(
        body, out_shape=shapes * 4, grid=tuple(lead) + (nslots,), in_specs=specs * 3 + aspecs,
        out_specs=specs * 4, compiler_params=_cparams(("parallel",) * nl + ("arbitrary",)),
        name=name)(*ws, *ms, *vs, *alls)
    return res[:n], res[n:2 * n], res[2 * n:3 * n], res[3 * n:]


def kernel(x, norm_ffn1, ffn1_w_gate, ffn1_w_up, ffn1_w_down, norm_mix, w_in, attn_sinks, ssm_lambda_re, ssm_lambda_im, ssm_log_dt, ssm_b_re, ssm_b_im, ssm_c_re, ssm_c_im, ssm_d, ssm_glu_w, ssm_glu_b, attn_out_norm, ssm_out_norm, w_out, norm_ffn2, ffn2_w_gate, ffn2_w_up, ffn2_w_down, final_norm, loss_target, m_norm_ffn1, m_ffn1_w_gate, m_ffn1_w_up, m_ffn1_w_down, m_norm_mix, m_w_in, m_attn_sinks, m_ssm_lambda_re, m_ssm_lambda_im, m_ssm_log_dt, m_ssm_b_re, m_ssm_b_im, m_ssm_c_re, m_ssm_c_im, m_ssm_d, m_ssm_glu_w, m_ssm_glu_b, m_attn_out_norm, m_ssm_out_norm, m_w_out, m_norm_ffn2, m_ffn2_w_gate, m_ffn2_w_up, m_ffn2_w_down, m_final_norm, v_norm_ffn1, v_ffn1_w_gate, v_ffn1_w_up, v_ffn1_w_down, v_norm_mix, v_w_in, v_attn_sinks, v_ssm_lambda_re, v_ssm_lambda_im, v_ssm_log_dt, v_ssm_b_re, v_ssm_b_im, v_ssm_c_re, v_ssm_c_im, v_ssm_d, v_ssm_glu_w, v_ssm_glu_b, v_attn_out_norm, v_ssm_out_norm, v_w_out, v_norm_ffn2, v_ffn2_w_gate, v_ffn2_w_up, v_ffn2_w_down, v_final_norm):
    given = dict(locals())
    wts = {n: given[n] for n in WEIGHTS}

    order = [g for g in GROUPS]
    cx, cy = lax.axis_index("x"), lax.axis_index("y")
    slots = jnp.stack([2 * cx + cy, 2 * (1 - cx) + cy, 2 * cx + 1 - cy, 2 * (1 - cx) + 1 - cy,
                       lax.axis_index("c")]).astype(jnp.int32)
    def view(a, n):
        if n in TRANSPOSED:
            return jnp.swapaxes(a[0], 0, 1)
        if n in BIG:
            return a[0]
        if n in ('ssm_b_re', 'ssm_b_im'):
            return jnp.swapaxes(a, -1, -2)
        return a.reshape(1, -1) if a.ndim == 1 else a

    def unview(a, n):
        if n in TRANSPOSED:
            return jnp.swapaxes(a, 0, 1)[None]
        if n in ('ssm_b_re', 'ssm_b_im'):
            return jnp.swapaxes(a, -1, -2)
        return a.reshape(wts[n].shape)

    started, gather_token = {}, None
    for g in order:
        shards = [view(wts[n], n).astype(BF16) for n in GROUPS[g]]
        if len({s.shape for s in shards}) == 1:
            placed = _place_own(shards, slots, f"weights_place_{g}")
        else:
            placed = [_place_own([s], slots, f"weights_place_{n}")[0] for n, s in zip(GROUPS[g], shards)]
        st, gather_token = _exchange_start([(shards, placed)], False, f"weights_start_{g}", after=gather_token)
        started[g] = st[0]

    def get_weights(group, after):
        if group == order[0]:
            after = after + gather_token[:1, :1]
        _, lands = _exchange_wait(started[group], after, False, f"weights_wait_{group}")
        lands = _sibling_forward(lands, f"weights_forward_{group}")
        out = dict(zip(GROUPS[group], lands))
        for n in ('w_in', 'ssm_glu_w', 'w_out'):
            if n in out:
                out[n] = out[n].reshape(-1, out[n].shape[-1])
        return out

    sent, tokens = {}, {}

    def put_grads(group, gd):
        parts = []
        for n in GROUPS[group]:
            g = gd[n]
            if g.ndim == 2:
                g = g.reshape(N_CHIPS, g.shape[0] // N_CHIPS, g.shape[1])
            parts.append(g.astype(BF16))
        sib = _half_swap(parts, f"grads_half_swap_{group}")
        same = len({p.shape for p in parts}) == 1
        batches = [list(range(len(parts)))] if same else [[i] for i in range(len(parts))]
        halves = [None] * len(parts)
        for b in batches:
            res = _half_add([parts[i] for i in b], [sib[i] for i in b], slots, f"grads_half_add_{GROUPS[group][b[0]]}")
            for i, h in zip(b, res):
                halves[i] = h
        parts = halves
        lands = [lax.empty(p.shape, p.dtype) for p in parts]
        started_g, tokens[group] = _exchange_start([(parts, lands)], True, f"grads_start_{group}")
        sent[group] = started_g[0]
        return tokens[group]

    w = {n: (wts[n][0] if wts[n].ndim > 1 else wts[n]) for n in SMALL}
    w['norm_ffn1'], w['norm_mix'], w['norm_ffn2'] = wts['norm_ffn1'], wts['norm_mix'], wts['norm_ffn2']
    w['ssm_b_re'], w['ssm_b_im'] = view(wts['ssm_b_re'], 'ssm_b_re')[0], view(wts['ssm_b_im'], 'ssm_b_im')[0]
    w['ssm_log_dt'] = w['ssm_log_dt'] + gather_token[0, 0]
    wide =['ssm_b_re', 'ssm_b_im', 'ssm_c_re', 'ssm_c_im']

    def reduce_wide(gd):
        packed = jnp.concatenate([gd[n].reshape(-1, LANES) for n in wide])
        return _small_allreduce(packed, "small_grads_allreduce")

    loss_row, dx, grads, wide_sum = _local_step(x[0], loss_target[0], w, get_weights, put_grads, reduce_wide)

    out_g, out_d, out_m, out_v = {}, {}, {}, {}

    def finish(group, after):
        names = GROUPS[group]
        parts, recv = _exchange_wait(sent[group], after, True, f"grads_wait_{group}")
        same = len({p.shape for p in parts}) == 1
        batches = [list(range(len(names)))] if same else [[i] for i in range(len(names))]
        sums = [None] * len(names)
        for b in batches:
            res = _sum_parts([parts[i] for i in b], [recv[i] for i in b], slots, f"grad_sum_{names[b[0]]}")
            for i, sm in zip(b, res):
                sums[i] = sm
        full = _half_forward(sums, f"grad_half_forward_{group}")
        for b in batches:
            res = _adamw([view(wts[names[i]], names[i]) for i in b], [view(given['m_' + names[i]], names[i]) for i in b],
                         [view(given['v_' + names[i]], names[i]) for i in b], [full[i] for i in b],
                         f"adamw_{names[b[0]]}")
            for i, (d, nm, nv) in zip(b, res):
                n = names[i]
                out_g[n], out_d[n], out_m[n], out_v[n] = (unview(a, n) for a in (full[i], d, nm, nv))
        return nv

    done = finish('ffn2', tokens['ffn1'])
    done = finish('mix', done)

    nat = {n: view(wts[n], n).shape for n in SMALL}
    narrow = [n for n in SMALL if n not in wide]
    alls = list(_small_exchange([grads[n].reshape(nat[n]) for n in narrow] + [loss_row], "small_grads_allgather"))
    loss = jnp.sum(alls.pop()[:, 0, 0])
    rows = wide_sum.shape[0] // len(wide)
    wide_g = [wide_sum[i * rows:(i + 1) * rows].reshape((1,) + nat[n]) for i, n in enumerate(wide)]
    for group, gs, split, tag in ((narrow, alls, None, "adamw_small"), (wide, wide_g, (2, 4), "adamw_ssm_bc")):
        res = _adamw_small([view(wts[n], n) for n in group], [view(given['m_' + n], n) for n in group],
                           [view(given['v_' + n], n) for n in group], gs, split, tag)
        for dst, vals in zip((out_g, out_d, out_m, out_v), res):
            for n, a in zip(group, vals):
                dst[n] = unview(a, n)

    finish('ffn1', out_v['norm_ffn1'][:, :1] + out_v['ssm_c_re'].reshape(1, -1)[:, :1] + done[:1, :1] + loss)

    return (loss, dx[None], *[out_g[n] for n in WEIGHTS], *[out_d[n] for n in WEIGHTS],
            *[out_m[n] for n in WEIGHTS], *[out_v[n] for n in WEIGHTS])
```

```python
import functools
import math

import numpy as np
import jax
import jax.numpy as jnp
from jax import lax
from jax.experimental import pallas as pl
from jax.experimental.pallas import tpu as pltpu

F32 = jnp.float32
BF16 = jnp.bfloat16
MESH = pl.DeviceIdType.MESH

EPS = 1e-6
NEG_INF = -1e30
LAMBDA_RE_MAX = -1e-4
ATTN_HEADS = 8
KV_HEADS = 2
GQ = ATTN_HEADS // KV_HEADS
HEAD_DIM = 64
ATTN_WIDTH = 512
KV_WIDTH = 128
WINDOW = 128
QBLOCK = 128
SSM_WIDTH = 512
SSM_GROUPS = 32
SSM_CH = 16
SSM_STATE = 64
N_STRIPS = 4
STRIP_IN = SSM_WIDTH // N_STRIPS
STRIP_ST = SSM_GROUPS * SSM_STATE // N_STRIPS
SUBLANES = 8
LANES = 128
N_CHIPS = 4
N_DEV = 8

ADAM_LR = 0.001
ADAM_B1 = 0.9
ADAM_B2 = 0.999
ADAM_EPS = 1e-08
ADAM_WD = 0.01
ADAM_STEP = 10

VMEM_LIMIT = 48 * 1024 * 1024

WEIGHTS = ['norm_ffn1', 'ffn1_w_gate', 'ffn1_w_up', 'ffn1_w_down', 'norm_mix', 'w_in', 'attn_sinks',
           'ssm_lambda_re', 'ssm_lambda_im', 'ssm_log_dt', 'ssm_b_re', 'ssm_b_im', 'ssm_c_re', 'ssm_c_im',
           'ssm_d', 'ssm_glu_w', 'ssm_glu_b', 'attn_out_norm', 'ssm_out_norm', 'w_out', 'norm_ffn2',
           'ffn2_w_gate', 'ffn2_w_up', 'ffn2_w_down', 'final_norm']
BIG = ['ffn1_w_gate', 'ffn1_w_up', 'ffn1_w_down', 'w_in', 'ssm_glu_w', 'w_out',
       'ffn2_w_gate', 'ffn2_w_up', 'ffn2_w_down']
SMALL = [n for n in WEIGHTS if n not in BIG]
TRANSPOSED = ['ffn1_w_gate', 'ffn1_w_up', 'w_in', 'ffn2_w_gate', 'ffn2_w_up']
GROUPS = {'ffn1': ['ffn1_w_gate', 'ffn1_w_up', 'ffn1_w_down'],
          'mix': ['w_in', 'ssm_glu_w', 'w_out'],
          'ffn2': ['ffn2_w_gate', 'ffn2_w_up', 'ffn2_w_down']}


def _cparams(sem=None):
    return pltpu.CompilerParams(dimension_semantics=sem, vmem_limit_bytes=VMEM_LIMIT)


def _tile(n, pref):
    if n <= pref:
        return n
    for t in (pref, pref // 2, pref // 4):
        if t % LANES == 0 and n % t == 0:
            return t
    return n


def _sigmoid(x):
    return 1.0 / (1.0 + jnp.exp(-x))


def _sigmoid_tanh(x):
    return 0.5 * jnp.tanh(0.5 * x) + 0.5


def _mm(a, b, *, ta=False, tb=False, reduce_s=False, res=None, scale=1.0, out_dtype=F32, after=None, name):
    a3 = a if a.ndim == 3 else a[None]
    b3 = b if b.ndim == 3 else b[None]
    sa, sb = a3.shape[0], b3.shape[0]
    ns = max(sa, sb)
    (kk, m) = a3.shape[1:] if ta else a3.shape[1:][::-1]
    (n, kb) = b3.shape[1:] if tb else b3.shape[1:][::-1]
    assert kk == kb, (a3.shape, b3.shape)
    tm, tn, tk = _tile(m, 1024), _tile(n, 1024), _tile(kk, 2048)
    nm, nn, nk = m // tm, n // tn, kk // tk
    has_res = res is not None
    single = nk == 1 and not (reduce_s and ns > 1)

    if reduce_s:
        grid = (nm, nn, ns, nk)
        ids = lambda i, j, s, k: (s, i, j, k)
        sem = ("parallel", "parallel", "arbitrary", "arbitrary")
    else:
        grid = (ns, nm, nn, nk)
        ids = lambda s, i, j, k: (s, i, j, k)
        sem = ("parallel", "parallel", "parallel", "arbitrary")

    def a_map(*g):
        s, i, j, k = ids(*g)
        s = s if sa > 1 else 0
        return (s, k, i) if ta else (s, i, k)

    def b_map(*g):
        s, i, j, k = ids(*g)
        s = s if sb > 1 else 0
        return (s, j, k) if tb else (s, k, j)

    def o_map(*g):
        s, i, j, k = ids(*g)
        return (i, j) if reduce_s else (s, i, j)

    a_blk = (1, tk, tm) if ta else (1, tm, tk)
    b_blk = (1, tn, tk) if tb else (1, tk, tn)
    dims = (((0 if ta else 1,), (1 if tb else 0,)), ((), ()))

    def body(*refs):
        a_ref, b_ref = refs[0], refs[1]
        r_ref = refs[2] if has_res else None
        o_ref = refs[2 + has_res + (after is not None)]
        acc_ref = None if single else refs[-1]
        s, _, _, k = ids(*[pl.program_id(d) for d in range(4)])
        prod = lax.dot_general(a_ref[0].astype(BF16), b_ref[0].astype(BF16), dims, preferred_element_type=F32)

        def finish(out):
            if scale != 1.0:
                out = out * scale
            if has_res:
                out = r_ref[...].reshape(out.shape) + out
            o_ref[...] = out.astype(out_dtype).reshape(o_ref.shape)

        if single:
            finish(prod)
            return
        if reduce_s:
            first = jnp.logical_and(s == 0, k == 0)
            last = jnp.logical_and(s == ns - 1, k == nk - 1)
        else:
            first, last = k == 0, k == nk - 1

        acc_ref[...] = prod + jnp.where(first, 0.0, acc_ref[...])

        @pl.when(last)
        def _():
            finish(acc_ref[...])

    in_specs = [pl.BlockSpec(a_blk, a_map), pl.BlockSpec(b_blk, b_map)]
    args = [a3, b3]
    if reduce_s:
        out_shape = jax.ShapeDtypeStruct((m, n), out_dtype)
        o_spec = pl.BlockSpec((tm, tn), o_map)
    else:
        out_shape = jax.ShapeDtypeStruct((ns, m, n), out_dtype)
        o_spec = pl.BlockSpec((1, tm, tn), o_map)
    if has_res:
        assert res.shape == out_shape.shape
        in_specs.append(o_spec)
        args.append(res)
    if after is not None:
        in_specs.append(HBM_SPEC)
        args.append(after)
    return pl.pallas_call(body, out_shape=out_shape, grid=grid, in_specs=in_specs, out_specs=o_spec,
                          scratch_shapes=[] if single else [pltpu.VMEM((tm, tn), F32)],
                          compiler_params=_cparams(sem), name=name)(*args)


def _row_tile(t, cap=256):
    for step in (16, SUBLANES):
        for tr in range(min(cap, t) // step * step, 0, -step):
            if t % tr == 0:
                return tr
    return t


def _norm_mm(xs, gs, w, *, tb, res, name):
    nx = len(xs)
    t = xs[0].shape[0]
    widths = [x.shape[1] for x in xs]
    k = sum(widths)
    n = w.shape[0] if tb else w.shape[1]
    tm, tn = _tile(t, 512), _tile(n, 512)
    has_res = res is not None

    def body(*refs):
        x_refs, g_refs, w_ref = refs[:nx], refs[nx:2 * nx], refs[2 * nx]
        r_ref = refs[2 * nx + 1] if has_res else None
        o_ref, h_ref, h_sc = refs[2 * nx + 1 + has_res:]

        @pl.when(pl.program_id(1) == 0)
        def _():
            off = 0
            for x_ref, g_ref, wd in zip(x_refs, g_refs, widths):
                xv = x_ref[...]
                r = lax.rsqrt(jnp.mean(xv * xv, axis=-1, keepdims=True) + EPS)
                h_sc[:, off:off + wd] = (xv * r * g_ref[...]).astype(BF16)
                off += wd
            h_ref[...] = h_sc[...]

        prod = lax.dot_general(h_sc[...], w_ref[...], NT_DIMS if tb else (((1,), (0,)), ((), ())),
                               preferred_element_type=F32)
        o_ref[...] = r_ref[...] + prod if has_res else prod

    in_specs = [pl.BlockSpec((tm, wd), lambda i, j: (i, 0)) for wd in widths]
    in_specs += [pl.BlockSpec((1, wd), lambda i, j: (0, 0)) for wd in widths]
    in_specs.append(pl.BlockSpec((tn, k), lambda i, j: (j, 0)) if tb else pl.BlockSpec((k, tn), lambda i, j: (0, j)))
    tile = pl.BlockSpec((tm, tn), lambda i, j: (i, j))
    if has_res:
        in_specs.append(tile)
    return pl.pallas_call(
        body, out_shape=(jax.ShapeDtypeStruct((t, n), F32), jax.ShapeDtypeStruct((t, k), BF16)),
        grid=(t // tm, n // tn), in_specs=in_specs,
        out_specs=(tile, pl.BlockSpec((tm, k), lambda i, j: (i, 0))),
        scratch_shapes=[pltpu.VMEM((tm, k), BF16)], compiler_params=_cparams(("parallel", "arbitrary")),
        name=name)(*xs, *gs, w, *([res] if has_res else []))


def _rms_bwd_rows(xv, gv, dhv):
    r = lax.rsqrt(jnp.mean(xv * xv, axis=-1, keepdims=True) + EPS)
    nrm = xv * r
    dn = dhv * gv
    return r * (dn - nrm * jnp.mean(dn * nrm, axis=-1, keepdims=True)), dhv * nrm


def _mm_rms_bwd(a, b, *, tb, xs, gs, dres, after, name):
    nx = len(xs)
    t, k = a.shape
    widths = [x.shape[1] for x in xs]
    n = sum(widths)
    assert n == (b.shape[0] if tb else b.shape[1])
    tm = _tile(t, 512)
    has_res, has_after = dres is not None, after is not None

    def body(*refs):
        a_ref, b_ref = refs[0], refs[1]
        x_refs, g_refs = refs[2:2 + nx], refs[2 + nx:2 + 2 * nx]
        r_ref = refs[2 + 2 * nx] if has_res else None
        outs = refs[2 + 2 * nx + has_res + has_after:]
        dh = lax.dot_general(a_ref[...], b_ref[...], NT_DIMS if tb else (((1,), (0,)), ((), ())),
                             preferred_element_type=F32)
        off = 0
        for i, wd in enumerate(widths):
            dx_ref, dxb_ref, dg_ref = outs[3 * i:3 * i + 3]
            dx, dgs = _rms_bwd_rows(x_refs[i][...], g_refs[i][...], dh[:, off:off + wd])
            if has_res:
                dx = dx + r_ref[...]
            dx_ref[...] = dx
            dxb_ref[...] = dx.astype(BF16)
            part = jnp.sum(dgs, axis=0, keepdims=True)
            dg_ref[...] = part + jnp.where(pl.program_id(0) > 0, dg_ref[...], 0.0)
            off += wd

    in_specs = [pl.BlockSpec((tm, k), lambda i: (i, 0)), pl.BlockSpec(b.shape, lambda i: (0, 0))]
    in_specs += [pl.BlockSpec((tm, wd), lambda i: (i, 0)) for wd in widths]
    in_specs += [pl.BlockSpec((1, wd), lambda i: (0, 0)) for wd in widths]
    args = [a, b, *xs, *gs]
    if has_res:
        in_specs.append(pl.BlockSpec((tm, widths[0]), lambda i: (i, 0)))
        args.append(dres)
    if has_after:
        in_specs.append(HBM_SPEC)
        args.append(after)
    out_shape, out_specs = [], []
    for wd in widths:
        out_shape += [jax.ShapeDtypeStruct((t, wd), F32), jax.ShapeDtypeStruct((t, wd), BF16),
                      jax.ShapeDtypeStruct((1, wd), F32)]
        out_specs += [pl.BlockSpec((tm, wd), lambda i: (i, 0)), pl.BlockSpec((tm, wd), lambda i: (i, 0)),
                      pl.BlockSpec((1, wd), lambda i: (0, 0))]
    res = pl.pallas_call(body, out_shape=out_shape, grid=(t // tm,), in_specs=in_specs, out_specs=out_specs,
                         compiler_params=_cparams(("arbitrary",)), name=name)(*args)
    return [tuple(res[3 * i:3 * i + 3]) for i in range(nx)]


FFN_ROWS = 512
FFN_SPLIT = 2
FFN_W_ROWS = 1024
SCAN_ROWS = 256


NT_DIMS = (((1,), (1,)), ((), ()))
TN_DIMS = (((0,), (0,)), ((), ()))


def _ffn_fwd_call(x, g, wg, wu, wd, name):
    t, d = x.shape
    ns, f, _ = wg.shape
    tm = _tile(t, FFN_ROWS)

    def body(x_ref, g_ref, wg_ref, wu_ref, wd_ref, xo_ref, h_ref, gate_ref, up_ref, h_sc, acc_ref):
        s = pl.program_id(1)

        @pl.when(s == 0)
        def _():
            xv = x_ref[...]
            r = lax.rsqrt(jnp.mean(xv * xv, axis=-1, keepdims=True) + EPS)
            hb = (xv * r * g_ref[...]).astype(BF16)
            h_sc[...] = hb
            h_ref[...] = hb

        for r0 in range(0, tm, tm // FFN_SPLIT):
            rows = slice(r0, r0 + tm // FFN_SPLIT)
            hb = h_sc[rows, :]
            gate = lax.dot_general(hb, wg_ref[0], NT_DIMS, preferred_element_type=F32)
            up = lax.dot_general(hb, wu_ref[0], NT_DIMS, preferred_element_type=F32)
            gate_ref[0, rows, :] = gate.astype(BF16)
            up_ref[0, rows, :] = up.astype(BF16)
            act = (gate * _sigmoid_tanh(gate) * up).astype(BF16)
            prod = jnp.dot(act, wd_ref[0], preferred_element_type=F32)
            acc_ref[rows, :] = prod + jnp.where(s > 0, acc_ref[rows, :], 0.0)

        @pl.when(s == ns - 1)
        def _():
            xo_ref[...] = x_ref[...] + 0.5 * acc_ref[...]

    row = pl.BlockSpec((tm, d), lambda i, s: (i, 0))
    vec = pl.BlockSpec((1, d), lambda i, s: (0, 0))
    wrow = pl.BlockSpec((1, f, d), lambda i, s: (s, 0, 0))
    hid = pl.BlockSpec((1, tm, f), lambda i, s: (s, i, 0))
    hid_sh = jax.ShapeDtypeStruct((ns, t, f), BF16)
    return pl.pallas_call(
        body, out_shape=(jax.ShapeDtypeStruct((t, d), F32), jax.ShapeDtypeStruct((t, d), BF16), hid_sh, hid_sh),
        grid=(t // tm, ns), in_specs=[row, vec, wrow, wrow, wrow], out_specs=(row, row, hid, hid),
        scratch_shapes=[pltpu.VMEM((tm, d), BF16), pltpu.VMEM((tm, d), F32)],
        compiler_params=_cparams(("parallel", "arbitrary")), name=name)(x, g, wg, wu, wd)


def _ffn_bwd_x_call(dxo, dxo_b, x, g, gate, up, wg, wu, wd, name):
    t, d = x.shape
    ns, f, _ = wg.shape
    tm = _tile(t, FFN_ROWS)

    def body(dxo_ref, dxb_ref, x_ref, g_ref, gate_ref, up_ref, wg_ref, wu_ref, wd_ref,
             dx_ref, dxob_ref, dgn_ref, dgate_ref, dup_ref, act_ref, dh_ref):
        i, s = pl.program_id(0), pl.program_id(1)
        for r0 in range(0, tm, tm // FFN_SPLIT):
            rows = slice(r0, r0 + tm // FFN_SPLIT)
            dact = lax.dot_general(dxb_ref[rows, :], wd_ref[0], NT_DIMS, preferred_element_type=F32) * 0.5
            gv = gate_ref[0, rows, :].astype(F32)
            uv = up_ref[0, rows, :].astype(F32)
            sg = _sigmoid_tanh(gv)
            silu = gv * sg
            act_ref[0, rows, :] = (silu * uv).astype(BF16)
            dub = (dact * silu).astype(BF16)
            dgb = (dact * uv * sg * (1.0 + gv * (1.0 - sg))).astype(BF16)
            dup_ref[0, rows, :] = dub
            dgate_ref[0, rows, :] = dgb
            prod = (jnp.dot(dgb, wg_ref[0], preferred_element_type=F32)
                    + jnp.dot(dub, wu_ref[0], preferred_element_type=F32))

            dh_ref[rows, :] = prod + jnp.where(s > 0, dh_ref[rows, :], 0.0)

        @pl.when(jnp.logical_and(i == 0, s == 0))
        def _():
            dgn_ref[...] = jnp.zeros_like(dgn_ref)

        @pl.when(s == ns - 1)
        def _():
            dx, dgs = _rms_bwd_rows(x_ref[...], g_ref[...], dh_ref[...])
            dx = dx + dxo_ref[...]
            dx_ref[...] = dx
            dxob_ref[...] = dx.astype(BF16)
            dgn_ref[...] += jnp.sum(dgs, axis=0, keepdims=True)

    row = pl.BlockSpec((tm, d), lambda i, s: (i, 0))
    vec = pl.BlockSpec((1, d), lambda i, s: (0, 0))
    wrow = pl.BlockSpec((1, f, d), lambda i, s: (s, 0, 0))
    hid = pl.BlockSpec((1, tm, f), lambda i, s: (s, i, 0))
    hid_sh = jax.ShapeDtypeStruct((ns, t, f), BF16)
    return pl.pallas_call(
        body,
        out_shape=(jax.ShapeDtypeStruct((t, d), F32), jax.ShapeDtypeStruct((t, d), BF16),
                   jax.ShapeDtypeStruct((1, d), F32), hid_sh, hid_sh, hid_sh),
        grid=(t // tm, ns), in_specs=[row, row, row, vec, hid, hid, wrow, wrow, wrow],
        out_specs=(row, row, vec, hid, hid, hid), scratch_shapes=[pltpu.VMEM((tm, d), F32)],
        compiler_params=_cparams(("arbitrary", "arbitrary")), name=name)(dxo, dxo_b, x, g, gate, up, wg, wu, wd)


def _ffn_bwd_w_call(h, dxo_b, dgate, dup, act, name):
    t, d = h.shape
    ns, _, f = dgate.shape
    tm = _tile(t, FFN_W_ROWS)
    nm = t // tm

    def body(h_ref, dxb_ref, dgate_ref, dup_ref, act_ref, dwg_ref, dwu_ref, dwd_ref, ag_ref, au_ref, ad_ref):
        i = pl.program_id(1)
        hv = h_ref[...]
        pg = lax.dot_general(dgate_ref[0], hv, TN_DIMS, preferred_element_type=F32)
        pu = lax.dot_general(dup_ref[0], hv, TN_DIMS, preferred_element_type=F32)
        pd = lax.dot_general(act_ref[0], dxb_ref[...], TN_DIMS, preferred_element_type=F32)

        ag_ref[...] = pg + jnp.where(i > 0, ag_ref[...], 0.0)
        au_ref[...] = pu + jnp.where(i > 0, au_ref[...], 0.0)
        ad_ref[...] = pd + jnp.where(i > 0, ad_ref[...], 0.0)

        @pl.when(i == nm - 1)
        def _():
            dwg_ref[0] = ag_ref[...].astype(BF16)
            dwu_ref[0] = au_ref[...].astype(BF16)
            dwd_ref[0] = (0.5 * ad_ref[...]).astype(BF16)

    row = pl.BlockSpec((tm, d), lambda s, i: (i, 0))
    hid = pl.BlockSpec((1, tm, f), lambda s, i: (s, i, 0))
    wrow = pl.BlockSpec((1, f, d), lambda s, i: (s, 0, 0))
    wsh = jax.ShapeDtypeStruct((ns, f, d), BF16)
    return pl.pallas_call(
        body, out_shape=(wsh, wsh, wsh),
        grid=(ns, nm), in_specs=[row, row, hid, hid, hid], out_specs=(wrow, wrow, wrow),
        scratch_shapes=[pltpu.VMEM((f, d), F32), pltpu.VMEM((f, d), F32), pltpu.VMEM((f, d), F32)],
        compiler_params=_cparams(("parallel", "arbitrary")), name=name)(h, dxo_b, dgate, dup, act)


def _loss_head(x, g, tgt, name):
    t, w = x.shape
    tr = _row_tile(t)

    def body(x_ref, g_ref, t_ref, loss_ref, dx_ref, dxb_ref, dg_ref):
        xv = x_ref[...]
        gv = g_ref[...]
        r = lax.rsqrt(jnp.mean(xv * xv, axis=-1, keepdims=True) + EPS)
        nrm = xv * r
        err = nrm * gv - t_ref[...]
        dout = err * (1.0 / w)
        dn = dout * gv
        dx = r * (dn - nrm * jnp.mean(dn * nrm, axis=-1, keepdims=True))
        dx_ref[...] = dx
        dxb_ref[...] = dx.astype(BF16)

        @pl.when(pl.program_id(0) == 0)
        def _():
            dg_ref[...] = jnp.zeros_like(dg_ref)
            loss_ref[...] = jnp.zeros_like(loss_ref)

        dg_ref[...] += jnp.sum(dout * nrm, axis=0, keepdims=True)
        part = jnp.sum(jnp.sum(err * err, axis=-1, keepdims=True) * (0.5 / w), axis=0, keepdims=True)
        loss_ref[...] += jnp.broadcast_to(part, loss_ref.shape)

    row = pl.BlockSpec((tr, w), lambda i: (i, 0))
    vec = pl.BlockSpec((1, w), lambda i: (0, 0))
    return pl.pallas_call(
        body, out_shape=(jax.ShapeDtypeStruct((1, LANES), F32), jax.ShapeDtypeStruct((t, w), F32),
                         jax.ShapeDtypeStruct((t, w), BF16), jax.ShapeDtypeStruct((1, w), F32)),
        grid=(t // tr,), in_specs=[row, vec, row],
        out_specs=(pl.BlockSpec((1, LANES), lambda i: (0, 0)), row, row, vec),
        compiler_params=_cparams(("arbitrary",)), name=name)(x, g, tgt)


def _attn_bias():
    slopes = np.asarray(2.0 ** (-8.0 * (np.arange(ATTN_HEADS) + 1) / ATTN_HEADS), np.float32)
    qi = np.arange(QBLOCK)[:, None]
    kj = np.arange(3 * QBLOCK)[None, :]
    rel = np.abs(kj - QBLOCK - qi).astype(np.float32)
    tile = np.where(rel <= WINDOW, -slopes[:, None, None] * rel[None], np.float32(NEG_INF)).astype(np.float32)
    tile = tile.reshape(KV_HEADS, GQ * QBLOCK, 3 * QBLOCK)
    return jnp.asarray(np.swapaxes(tile, 1, 2))


def _attn_scores(k3, q, n, nb, bias):
    s = lax.dot_general(k3, q, NT_DIMS, preferred_element_type=F32) * (HEAD_DIM ** -0.5)
    key = lax.broadcasted_iota(jnp.int32, (3 * QBLOCK, 1), 0)
    inside = (key >= jnp.where(n == 0, QBLOCK, 0)) & (key < jnp.where(n == nb - 1, 2 * QBLOCK, 3 * QBLOCK))
    return jnp.where(inside, s + bias, NEG_INF)


Q_COL, K_COL, V_COL, U_COL = 0, ATTN_WIDTH // LANES, ATTN_WIDTH // LANES + 1, ATTN_WIDTH // LANES + 2


def _key_rows(ref, n, nb):
    prev, nxt = jnp.maximum(n - 1, 0), jnp.minimum(n + 1, nb - 1)
    blk = lambda b: ref[pl.ds(pl.multiple_of(b * QBLOCK, QBLOCK), QBLOCK), :]
    return jnp.concatenate([blk(prev), blk(n), blk(nxt)], axis=0)


def _head_tiles(x, kh, low):
    tiles = []
    for g in range(GQ):
        h = GQ * kh + g
        t128 = x[:, LANES * (h // 2):LANES * (h // 2 + 1)]
        t128 = jnp.where(low if h % 2 == 0 else jnp.logical_not(low), t128, 0.0)
        if h % 2 != kh:
            t128 = pltpu.roll(t128, HEAD_DIM, 1)
        tiles.append(t128)
    return jnp.concatenate(tiles, axis=0)


def _head_merge(per_kh, low):
    out = []
    for j in range(ATTN_HEADS // 2):
        pair = []
        for h in (2 * j, 2 * j + 1):
            kh, g = h // GQ, h % GQ
            t128 = per_kh[kh][g * QBLOCK:(g + 1) * QBLOCK, :]
            if h % 2 != kh:
                t128 = pltpu.roll(t128, HEAD_DIM, 1)
            pair.append(t128)
        out.append(jnp.where(low, pair[0], pair[1]))
    return jnp.concatenate(out, axis=1)


def _attn_fwd_proj(proj, sink_rows, bias, name):
    t = proj.shape[0]
    nb = t // QBLOCK
    rows = GQ * QBLOCK

    def body(q_ref, k_ref, v_ref, sink_ref, bias_ref, o_ref, lse_ref):
        n = pl.program_id(0)
        low = lax.broadcasted_iota(jnp.int32, (QBLOCK, LANES), 1) < HEAD_DIM
        k3 = _key_rows(k_ref, n, nb).astype(BF16)
        v3 = _key_rows(v_ref, n, nb).astype(BF16)
        q = q_ref[...]
        outs = []
        for kh in range(KV_HEADS):
            qs = _head_tiles(q, kh, low).astype(BF16)
            s = _attn_scores(k3, qs, n, nb, bias_ref[kh])
            sink = sink_ref[kh]
            mx = jnp.maximum(jnp.max(s, axis=0, keepdims=True), sink)
            p = jnp.exp(s - mx)
            den = jnp.sum(p, axis=0, keepdims=True) + jnp.exp(sink - mx)
            pn = (p * (1.0 / den)).astype(BF16)
            outs.append(lax.dot_general(pn, v3, TN_DIMS, preferred_element_type=F32))
            lse_ref[0, kh] = mx + jnp.log(den)
        o_ref[...] = _head_merge(outs, low)

    strip = lambda col: pl.BlockSpec((t, LANES), lambda n, col=col: (0, col))
    rowspec = pl.BlockSpec((KV_HEADS, 1, rows), lambda n: (0, 0, 0))
    biasspec = pl.BlockSpec((KV_HEADS, 3 * QBLOCK, rows), lambda n: (0, 0, 0))
    return pl.pallas_call(
        body, out_shape=(jax.ShapeDtypeStruct((t, ATTN_WIDTH), F32), jax.ShapeDtypeStruct((nb, KV_HEADS, 1, rows), F32)),
        grid=(nb,), in_specs=[pl.BlockSpec((QBLOCK, ATTN_WIDTH), lambda n: (n, 0)), strip(K_COL), strip(V_COL),
                              rowspec, biasspec],
        out_specs=(pl.BlockSpec((QBLOCK, ATTN_WIDTH), lambda n: (n, 0)),
                   pl.BlockSpec((1, KV_HEADS, 1, rows), lambda n: (n, 0, 0, 0))),
        compiler_params=_cparams(("parallel",)), name=name)(proj, proj, proj, sink_rows, bias)


def _attn_bwd_proj(proj, sink_rows, bias, o, lse, do, name):
    t = proj.shape[0]
    nb = t // QBLOCK
    rows = GQ * QBLOCK
    scale = HEAD_DIM ** -0.5

    def body(q_ref, k_ref, v_ref, sink_ref, bias_ref, o_ref, lse_ref, do_ref, dq_ref, dk_ref, dv_ref, ds_ref):
        n = pl.program_id(0)

        @pl.when(n == 0)
        def _():
            dk_ref[...] = jnp.zeros_like(dk_ref)
            dv_ref[...] = jnp.zeros_like(dv_ref)
            ds_ref[...] = jnp.zeros_like(ds_ref)

        low = lax.broadcasted_iota(jnp.int32, (QBLOCK, LANES), 1) < HEAD_DIM
        k3 = _key_rows(k_ref, n, nb).astype(BF16)
        v3 = _key_rows(v_ref, n, nb).astype(BF16)
        q, dov = q_ref[...], do_ref[...]
        dod = dov * o_ref[...]
        dqs = []
        dk3 = jnp.zeros((3 * QBLOCK, LANES), F32)
        dv3 = jnp.zeros((3 * QBLOCK, LANES), F32)
        ones = jnp.ones((SUBLANES, LANES), F32)
        for kh in range(KV_HEADS):
            qs = _head_tiles(q, kh, low).astype(BF16)
            dos = _head_tiles(dov, kh, low).astype(BF16)
            delta = lax.dot_general(ones, _head_tiles(dod, kh, low), NT_DIMS, preferred_element_type=F32,
                                    precision=lax.Precision.HIGHEST)[0:1, :]
            lse_kh = lse_ref[0, kh]
            s = _attn_scores(k3, qs, n, nb, bias_ref[kh])
            p = jnp.exp(s - lse_kh)
            dp = lax.dot_general(v3, dos, NT_DIMS, preferred_element_type=F32)
            dsb = (p * (dp - delta)).astype(BF16)
            dqs.append(lax.dot_general(dsb, k3, TN_DIMS, preferred_element_type=F32) * scale)
            dk3 = dk3 + jnp.dot(dsb, qs, preferred_element_type=F32) * scale
            dv3 = dv3 + jnp.dot(p.astype(BF16), dos, preferred_element_type=F32)
            ds_ref[kh] += -jnp.exp(sink_ref[kh] - lse_kh) * delta
        dq_ref[...] = _head_merge(dqs, low)
        prev, nxt = jnp.maximum(n - 1, 0), jnp.minimum(n + 1, nb - 1)
        for j, b in enumerate((prev, n, nxt)):
            blk = pl.ds(pl.multiple_of(b * QBLOCK, QBLOCK), QBLOCK)
            dk_ref[blk, :] += dk3[j * QBLOCK:(j + 1) * QBLOCK, :]
            dv_ref[blk, :] += dv3[j * QBLOCK:(j + 1) * QBLOCK, :]

    strip = lambda col: pl.BlockSpec((t, LANES), lambda n, col=col: (0, col))
    rowspec = pl.BlockSpec((KV_HEADS, 1, rows), lambda n: (0, 0, 0))
    qspec = pl.BlockSpec((QBLOCK, ATTN_WIDTH), lambda n: (n, 0))
    kv_out = pl.BlockSpec((t, LANES), lambda n: (0, 0))
    biasspec = pl.BlockSpec((KV_HEADS, 3 * QBLOCK, rows), lambda n: (0, 0, 0))
    return pl.pallas_call(
        body,
        out_shape=(jax.ShapeDtypeStruct((t, ATTN_WIDTH), F32), jax.ShapeDtypeStruct((t, LANES), F32),
                   jax.ShapeDtypeStruct((t, LANES), F32), jax.ShapeDtypeStruct((KV_HEADS, 1, rows), F32)),
        grid=(nb,),
        in_specs=[qspec, strip(K_COL), strip(V_COL), rowspec, biasspec, qspec,
                  pl.BlockSpec((1, KV_HEADS, 1, rows), lambda n: (n, 0, 0, 0)), qspec],
        out_specs=(qspec, kv_out, kv_out, rowspec),
        compiler_params=_cparams(("arbitrary",)), name=name)(proj, proj, proj, sink_rows, bias, o, lse, do)


def _scan_tables(a_re, a_im, reverse):
    pw = [(a_re, a_im)]
    for _ in range(SUBLANES - 1):
        pr, pi = pw[-1]
        pw.append((pr * a_re - pi * a_im, pr * a_im + pi * a_re))
    rows = np.arange(SUBLANES)
    tabs = []
    for d in (1, 2, 4):
        mask = (rows <= SUBLANES - 1 - d) if reverse else (rows >= d)
        m = jnp.asarray(mask, F32)[:, None]
        tabs += [m * pw[d - 1][0][None, :], m * pw[d - 1][1][None, :]]
    order = (SUBLANES - 1 - rows) if reverse else rows
    tabs += [jnp.stack([pw[j][0] for j in order]), jnp.stack([pw[j][1] for j in order])]
    tab = jnp.stack(tabs)
    return tab.reshape(8, SUBLANES, N_STRIPS, STRIP_ST).transpose(2, 0, 1, 3)


def _scan_pair_chunk(dirs):
    nblk = dirs[0]['xr'].shape[0] // SUBLANES

    @pl.when(pl.program_id(1) == 0)
    def _():
        for d in dirs:
            d['carry'][...] = jnp.zeros_like(d['carry'])

    for d in dirs:
        vb = d['v'][...].astype(BF16)
        d['xr'][...] = jnp.dot(vb, d['mir'][0], preferred_element_type=F32)
        d['xi'][...] = jnp.dot(vb, d['mii'][0], preferred_element_type=F32)
    carries = [(d['carry'][0], d['carry'][1]) for d in dirs]
    for i in range(nblk):
        for k, d in enumerate(dirs):
            rev = d['reverse']
            rows = pl.ds(((nblk - 1 - i) if rev else i) * SUBLANES, SUBLANES)
            cr, ci = carries[k]
            xr, xi = d['xr'][rows, :], d['xi'][rows, :]
            for j, s in enumerate((1, 2, 4)):
                tr_, ti_ = d['tab'][0, 2 * j], d['tab'][0, 2 * j + 1]
                sh = (SUBLANES - s) if rev else s
                sr, si = pltpu.roll(xr, sh, 0), pltpu.roll(xi, sh, 0)
                xr, xi = xr + tr_ * sr - ti_ * si, xi + tr_ * si + ti_ * sr
            pr, pi = d['tab'][0, 6], d['tab'][0, 7]
            xr, xi = xr + pr * cr - pi * ci, xi + pr * ci + pi * cr
            d['xr'][rows, :] = xr
            d['xi'][rows, :] = xi
            edge = 0 if rev else SUBLANES - 1
            carries[k] = (jnp.broadcast_to(xr[edge:edge + 1, :], xr.shape),
                          jnp.broadcast_to(xi[edge:edge + 1, :], xi.shape))
    for k, d in enumerate(dirs):
        d['carry'][0], d['carry'][1] = carries[k]
        d['y'][...] = (jnp.dot(d['xr'][...].astype(BF16), d['mor'][0], preferred_element_type=F32)
                       + jnp.dot(d['xi'][...].astype(BF16), d['moi'][0], preferred_element_type=F32))


def _scan_pair(v, ops_f, ops_b, name):
    t = v.shape[0]
    tc = _tile(t, SCAN_ROWS)
    nc = t // tc

    def body(vf_ref, vb_ref, *refs):
        ops = refs[:10]
        outs = refs[10:16]
        carries = refs[16:18]
        dirs = []
        for k, (v_ref, rev) in enumerate(((vf_ref, False), (vb_ref, True))):
            mir, mii, tab, mor, moi = ops[5 * k:5 * k + 5]
            y, xr, xi = outs[3 * k:3 * k + 3]
            dirs.append(dict(v=v_ref, mir=mir, mii=mii, tab=tab, mor=mor, moi=moi, y=y, xr=xr, xi=xi,
                             carry=carries[k], reverse=rev))
        _scan_pair_chunk(dirs)

    col0 = v.shape[1] // STRIP_IN - N_STRIPS
    fmap = lambda s, c: (c, s)
    bmap = lambda s, c: (nc - 1 - c, s)
    smap3 = lambda s, c: (s, 0, 0)
    m_in = pl.BlockSpec((1, STRIP_IN, STRIP_ST), smap3)
    m_out = pl.BlockSpec((1, STRIP_ST, STRIP_IN), smap3)
    tabspec = pl.BlockSpec((1, 8, SUBLANES, STRIP_ST), lambda s, c: (s, 0, 0, 0))
    opspecs = [m_in, m_in, tabspec, m_out, m_out]
    y_sh = jax.ShapeDtypeStruct((t, SSM_WIDTH), F32)
    x_sh = jax.ShapeDtypeStruct((t, N_STRIPS * STRIP_ST), F32)
    outspecs = lambda m: [pl.BlockSpec((tc, STRIP_IN), m), pl.BlockSpec((tc, STRIP_ST), m),
                          pl.BlockSpec((tc, STRIP_ST), m)]
    res = pl.pallas_call(
        body, out_shape=[y_sh, x_sh, x_sh] * 2, grid=(N_STRIPS, nc),
        in_specs=[pl.BlockSpec((tc, STRIP_IN), lambda s, c: (c, s + col0)),
                  pl.BlockSpec((tc, STRIP_IN), lambda s, c: (nc - 1 - c, s + col0))] + opspecs * 2,
        out_specs=outspecs(fmap) + outspecs(bmap),
        scratch_shapes=[pltpu.VMEM((2, SUBLANES, STRIP_ST), F32)] * 2,
        compiler_params=_cparams(("parallel", "arbitrary")), name=name)(v, v, *ops_f, *ops_b)
    return tuple(res[:3]), tuple(res[3:])


def _scan_adjoint_pair(dy, u, states, adj_ops, name):
    t = dy.shape[0]
    tc = _tile(t, SCAN_ROWS)
    nc = t // tc
    hb = tc // SUBLANES
    n_out = 6

    def body(*refs):
        c = pl.program_id(1)
        dirs = []
        for k in range(2):
            dy_ref, mir, mii, tab, mor, moi, u_ref, xr_ref, xi_ref, hr_ref, hi_ref = refs[11 * k:11 * k + 11]
            outs = refs[22 + n_out * k:22 + n_out * (k + 1)]
            lr_ref, li_ref, carry = refs[22 + 2 * n_out + 3 * k:22 + 2 * n_out + 3 * k + 3]
            dirs.append(dict(v=dy_ref, mir=mir, mii=mii, tab=tab, mor=mor, moi=moi, y=outs[0], xr=lr_ref, xi=li_ref,
                             carry=carry, reverse=(k == 0), u=u_ref, fx=(xr_ref, xi_ref), halo=(hr_ref, hi_ref),
                             acc=outs[1:]))

        @pl.when(c == 0)
        def _():
            for d in dirs:
                for r in d['acc']:
                    r[...] = jnp.zeros_like(r)

        _scan_pair_chunk(dirs)
        for d in dirs:
            fwd_reverse = not d['reverse']
            rc = (nc - 1 - c) if d['reverse'] else c
            dmir_ref, dmii_ref, dmor_ref, dmoi_ref, da_ref = d['acc']
            xrv, xiv, lrv, liv = d['fx'][0][...], d['fx'][1][...], d['xr'][...], d['xi'][...]
            hr_ref, hi_ref = d['halo']
            row = lax.broadcasted_iota(jnp.int32, xrv.shape, 0)
            if fwd_reverse:
                live = (rc < nc - 1).astype(F32)
                edge_r, edge_i = hr_ref[0:1, :] * live, hi_ref[0:1, :] * live
                xpr = jnp.where(row == tc - 1, edge_r, pltpu.roll(xrv, tc - 1, 0))
                xpi = jnp.where(row == tc - 1, edge_i, pltpu.roll(xiv, tc - 1, 0))
            else:
                live = (rc > 0).astype(F32)
                edge_r, edge_i = hr_ref[SUBLANES - 1:SUBLANES, :] * live, hi_ref[SUBLANES - 1:SUBLANES, :] * live
                xpr = jnp.where(row == 0, edge_r, pltpu.roll(xrv, 1, 0))
                xpi = jnp.where(row == 0, edge_i, pltpu.roll(xiv, 1, 0))
            da_ref[0, 0:1, :] += jnp.sum(xpr * lrv + xpi * liv, axis=0, keepdims=True)
            da_ref[0, 1:2, :] += jnp.sum(xpr * liv - xpi * lrv, axis=0, keepdims=True)
            ub, dyb = d['u'][...].astype(BF16), d['v'][...].astype(BF16)
            dmir_ref[0] += lax.dot_general(ub, lrv.astype(BF16), TN_DIMS, preferred_element_type=F32)
            dmii_ref[0] += lax.dot_general(ub, liv.astype(BF16), TN_DIMS, preferred_element_type=F32)
            dmor_ref[0] += lax.dot_general(xrv.astype(BF16), dyb, TN_DIMS, preferred_element_type=F32)
            dmoi_ref[0] += lax.dot_general(xiv.astype(BF16), dyb, TN_DIMS, preferred_element_type=F32)

    col0 = u.shape[1] // STRIP_IN - N_STRIPS
    smap3 = lambda s, c: (s, 0, 0)
    m_in = pl.BlockSpec((1, STRIP_IN, STRIP_ST), smap3)
    m_out = pl.BlockSpec((1, STRIP_ST, STRIP_IN), smap3)
    tabspec = pl.BlockSpec((1, 8, SUBLANES, STRIP_ST), lambda s, c: (s, 0, 0, 0))
    in_specs, out_specs, args = [], [], []
    for k in range(2):
        reverse = k == 0
        rowblk = (lambda c: nc - 1 - c) if reverse else (lambda c: c)
        tmap = lambda s, c, rowblk=rowblk: (rowblk(c), s)
        umap = lambda s, c, rowblk=rowblk: (rowblk(c), s + col0)
        if not reverse:
            hmap = lambda s, c, rowblk=rowblk: (jnp.minimum((rowblk(c) + 1) * hb, t // SUBLANES - 1), s)
        else:
            hmap = lambda s, c, rowblk=rowblk: (jnp.maximum(rowblk(c) * hb - 1, 0), s)
        narrow = pl.BlockSpec((tc, STRIP_IN), tmap)
        wide = pl.BlockSpec((tc, STRIP_ST), tmap)
        halo = pl.BlockSpec((SUBLANES, STRIP_ST), hmap)
        in_specs += [narrow, m_in, m_in, tabspec, m_out, m_out, pl.BlockSpec((tc, STRIP_IN), umap), wide, wide,
                     halo, halo]
        out_specs += [narrow, m_in, m_in, m_out, m_out, pl.BlockSpec((1, SUBLANES, STRIP_ST), smap3)]
        xr, xi = states[k]
        args += [dy, *adj_ops[k], u, xr, xi, xr, xi]
    out_shape = [jax.ShapeDtypeStruct((t, SSM_WIDTH), F32),
                 jax.ShapeDtypeStruct((N_STRIPS, STRIP_IN, STRIP_ST), F32),
                 jax.ShapeDtypeStruct((N_STRIPS, STRIP_IN, STRIP_ST), F32),
                 jax.ShapeDtypeStruct((N_STRIPS, STRIP_ST, STRIP_IN), F32),
                 jax.ShapeDtypeStruct((N_STRIPS, STRIP_ST, STRIP_IN), F32),
                 jax.ShapeDtypeStruct((N_STRIPS, SUBLANES, STRIP_ST), F32)] * 2
    res = pl.pallas_call(
        body, out_shape=out_shape, grid=(N_STRIPS, nc), in_specs=in_specs, out_specs=out_specs,
        scratch_shapes=[pltpu.VMEM((tc, STRIP_ST), F32), pltpu.VMEM((tc, STRIP_ST), F32),
                        pltpu.VMEM((2, SUBLANES, STRIP_ST), F32)] * 2,
        compiler_params=_cparams(("parallel", "arbitrary")), name=name)(*args)
    return tuple(res[:n_out]), tuple(res[n_out:])


def _ssm_prep(lam_re, lam_im, log_dt, bt_re, bt_im, c_re, c_im):
    lr = jnp.minimum(lam_re, LAMBDA_RE_MAX)
    li = lam_im
    dt = jnp.exp(log_dt)[:, None]
    mag = jnp.exp(lr * dt)
    a_re = mag * jnp.cos(li * dt)
    a_im = mag * jnp.sin(li * dt)
    den = lr * lr + li * li
    coef_re = ((a_re - 1.0) * lr + a_im * li) / den
    coef_im = (a_im * lr - (a_re - 1.0) * li) / den
    bb_re = coef_re[:, None, :] * bt_re - coef_im[:, None, :] * bt_im
    bb_im = coef_re[:, None, :] * bt_im + coef_im[:, None, :] * bt_re
    eye = jnp.eye(SSM_GROUPS // N_STRIPS, dtype=F32)

    def strips(m):
        g, a, b = m.shape
        m4 = m.reshape(N_STRIPS, g // N_STRIPS, a, b)
        return jnp.einsum('sgab,gk->sgakb', m4, eye).reshape(N_STRIPS, g // N_STRIPS * a, g // N_STRIPS * b)

    mi_re = strips(bb_re)
    mi_im = strips(bb_im)
    mo_re = strips(jnp.swapaxes(c_re, 1, 2))
    mo_im = strips(-jnp.swapaxes(c_im, 1, 2))
    return a_re.reshape(-1), a_im.reshape(-1), mi_re, mi_im, mo_re, mo_im


def _gelu(x):
    c = math.sqrt(2.0 / math.pi)
    return 0.5 * x * (1.0 + jnp.tanh(c * (x + 0.044715 * x * x * x)))


def _gelu_grad(x):
    c = math.sqrt(2.0 / math.pi)
    th = jnp.tanh(c * (x + 0.044715 * x * x * x))
    return 0.5 * (1.0 + th) + 0.5 * x * (1.0 - th * th) * c * (1.0 + 3.0 * 0.044715 * x * x)


def _last_cols_specs(u, w, tr):
    half = w // 2
    first = (u.shape[1] - w) // half
    assert first * half == u.shape[1] - w
    return [pl.BlockSpec((tr, half), lambda i, k=k: (i, first + k)) for k in range(2)]


def _ssm_post_fwd(u, yf, yb, d, wglu, bglu, name):
    t, w = yf.shape
    tr = _row_tile(t)

    def body(ua_ref, ub_ref, yf_ref, yb_ref, d_ref, w_ref, b_ref, s_ref, y0_ref, z_ref):
        uv = jnp.concatenate([ua_ref[...], ub_ref[...]], axis=1)
        y0 = d_ref[...] * uv + yf_ref[...] + yb_ref[...]
        yg = _gelu(y0)
        z = jnp.dot(yg.astype(BF16), w_ref[...], preferred_element_type=F32) + b_ref[...]
        s_ref[...] = yg * _sigmoid(z)
        y0_ref[...] = y0
        z_ref[...] = z

    row = pl.BlockSpec((tr, w), lambda i: (i, 0))
    vec = pl.BlockSpec((1, w), lambda i: (0, 0))
    mat = pl.BlockSpec((w, w), lambda i: (0, 0))
    sh = jax.ShapeDtypeStruct((t, w), F32)
    return pl.pallas_call(body, out_shape=(sh, sh, sh), grid=(t // tr,),
                          in_specs=[*_last_cols_specs(u, w, tr), row, row, vec, mat, vec], out_specs=(row, row, row),
                          compiler_params=_cparams(("parallel",)), name=name)(u, u, yf, yb, d, wglu, bglu)


def _ssm_post_bwd(ds, y0, z, u, d, wglu, name):
    t, w = ds.shape
    tr = _row_tile(t)

    def body(ds_ref, y0_ref, z_ref, ua_ref, ub_ref, d_ref, w_ref, dy0_ref, dw_ref, db_ref, dd_ref):
        @pl.when(pl.program_id(0) == 0)
        def _():
            dw_ref[...] = jnp.zeros_like(dw_ref)
            db_ref[...] = jnp.zeros_like(db_ref)
            dd_ref[...] = jnp.zeros_like(dd_ref)

        y0 = y0_ref[...]
        yg = _gelu(y0)
        sg = _sigmoid(z_ref[...])
        dsv = ds_ref[...]
        dz = dsv * yg * sg * (1.0 - sg)
        dzb = dz.astype(BF16)
        dyg = dsv * sg + lax.dot_general(dzb, w_ref[...], (((1,), (1,)), ((), ())), preferred_element_type=F32)
        dy0 = dyg * _gelu_grad(y0)
        dy0_ref[...] = dy0
        dw_ref[...] += lax.dot_general(yg.astype(BF16), dzb, (((0,), (0,)), ((), ())), preferred_element_type=F32)
        db_ref[...] += jnp.sum(dz, axis=0, keepdims=True)
        uv = jnp.concatenate([ua_ref[...], ub_ref[...]], axis=1)
        dd_ref[...] += jnp.sum(dy0 * uv, axis=0, keepdims=True)

    row = pl.BlockSpec((tr, w), lambda i: (i, 0))
    vec = pl.BlockSpec((1, w), lambda i: (0, 0))
    mat = pl.BlockSpec((w, w), lambda i: (0, 0))
    return pl.pallas_call(
        body, out_shape=(jax.ShapeDtypeStruct((t, w), F32), jax.ShapeDtypeStruct((w, w), F32),
                         jax.ShapeDtypeStruct((1, w), F32), jax.ShapeDtypeStruct((1, w), F32)),
        grid=(t // tr,), in_specs=[row, row, row, *_last_cols_specs(u, w, tr), vec, mat],
        out_specs=(row, mat, vec, vec),
        compiler_params=_cparams(("arbitrary",)), name=name)(ds, y0, z, u, u, d, wglu)


def _du_combine(dy0, d, du_f, du_b, name):
    t, w = dy0.shape
    tr = _row_tile(t)

    def body(dy_ref, d_ref, a_ref, b_ref, o_ref):
        o_ref[...] = d_ref[...] * dy_ref[...] + a_ref[...] + b_ref[...]

    row = pl.BlockSpec((tr, w), lambda i: (i, 0))
    vec = pl.BlockSpec((1, w), lambda i: (0, 0))
    return pl.pallas_call(body, out_shape=jax.ShapeDtypeStruct((t, w), F32), grid=(t // tr,),
                          in_specs=[row, vec, row, row], out_specs=row, compiler_params=_cparams(("parallel",)),
                          name=name)(dy0, d, du_f, du_b)


def _ffn_fwd(x, g, wg, wu, wd, tag):
    xo, h, gate, up = _ffn_fwd_call(x, g, wg, wu, wd, f"{tag}_fwd")
    return xo, (h, gate, up)


def _ffn_bwd(dxo, dxo_b, x, g, wg, wu, wd, saved, tag):
    h, gate, up = saved
    dx, dx_b, dg, dgate, dup, act = _ffn_bwd_x_call(dxo, dxo_b, x, g, gate, up, wg, wu, wd, f"{tag}_bwd_x")
    dwg, dwu, dwd = _ffn_bwd_w_call(h, dxo_b, dgate, dup, act, f"{tag}_bwd_w")
    return dx, dx_b, dg, dwg, dwu, dwd


def _local_step(x, tgt, w, get_weights, put_grads, reduce_wide):
    t = x.shape[0]
    row = lambda a: a.reshape(1, -1)
    grads = {}

    w = dict(w)

    ssm_names = ['ssm_lambda_re', 'ssm_lambda_im', 'ssm_log_dt', 'ssm_b_re', 'ssm_b_im', 'ssm_c_re', 'ssm_c_im']
    tr3 = lambda m: jnp.swapaxes(m, 1, 2)
    fwd_ops, adj_ops, vjps = [], [], []
    for direction in range(2):
        rev = direction == 1
        prep, vjp = jax.vjp(_ssm_prep, *[w[n][direction] for n in ssm_names])
        a_re, a_im = prep[0], prep[1]
        mi_re, mi_im, mo_re, mo_im = (m.astype(BF16) for m in prep[2:])
        fwd_ops.append((mi_re, mi_im, _scan_tables(a_re, a_im, rev), mo_re, mo_im))
        adj_ops.append((tr3(mo_re), tr3(mo_im), _scan_tables(a_re, -a_im, not rev), tr3(mi_re), tr3(mi_im)))
        vjps.append(vjp)
    sink_rows = jnp.repeat(w['attn_sinks'].reshape(KV_HEADS, GQ), QBLOCK, axis=1)[:, None, :]
    bias = _attn_bias()
    prepared = sum(jnp.sum(op[:1, :1].astype(F32)) for ops in fwd_ops + adj_ops for op in ops) + sink_rows[0, 0, 0]

    w.update(get_weights('ffn1', prepared.reshape(1, 1)))
    x1, ffn1_saved = _ffn_fwd(x, w['norm_ffn1'], w['ffn1_w_gate'], w['ffn1_w_up'], w['ffn1_w_down'], "ffn1")
    w.update(get_weights('mix', x1))

    proj, h2 = _norm_mm([x1], [w['norm_mix']], w['w_in'], tb=True, res=None, name="in_proj")
    u = proj

    attn, lse = _attn_fwd_proj(proj, sink_rows, bias, "attn_fwd")

    (y_f, *states_f), (y_b, *states_b) = _scan_pair(u, fwd_ops[0], fwd_ops[1], "s5_fwd")
    ys, states = [y_f, y_b], [states_f, states_b]
    d_row = row(w['ssm_d'])
    s, y0, z = _ssm_post_fwd(u, ys[0], ys[1], d_row, w['ssm_glu_w'], row(w['ssm_glu_b']), "ssm_post")

    x2, mixed = _norm_mm([attn, s], [row(w['attn_out_norm']), row(w['ssm_out_norm'])], w['w_out'], tb=False,
                         res=x1, name="out_proj")

    w.update(get_weights('ffn2', x2))
    x3, ffn2_saved = _ffn_fwd(x2, w['norm_ffn2'], w['ffn2_w_gate'], w['ffn2_w_up'], w['ffn2_w_down'], "ffn2")

    loss, dx3, dx3_b, dgf = _loss_head(x3, row(w['final_norm']), tgt, "loss_head")
    grads['final_norm'] = dgf.reshape(w['final_norm'].shape)

    dx2, dx2_b, dg, dwg, dwu, dwd = _ffn_bwd(dx3, dx3_b, x2, w['norm_ffn2'], w['ffn2_w_gate'], w['ffn2_w_up'],
                                             w['ffn2_w_down'], ffn2_saved, "ffn2")
    grads['norm_ffn2'] = dg
    sent = put_grads('ffn2', dict(ffn2_w_gate=dwg, ffn2_w_up=dwu, ffn2_w_down=dwd))

    (dattn, _, dga), (ds, _, dgs) = _mm_rms_bwd(
        dx2_b, w['w_out'], tb=True, xs=[attn, s], gs=[row(w['attn_out_norm']), row(w['ssm_out_norm'])],
        dres=None, after=sent, name="out_proj_dx")
    dw_out = _mm(mixed, dx2_b, ta=True, out_dtype=BF16, name="out_proj_dw")[0]
    grads.update(attn_out_norm=dga, ssm_out_norm=dgs)

    dy0, dwglu, dbglu, dd = _ssm_post_bwd(ds, y0, z, u, d_row, w['ssm_glu_w'], "ssm_post_bwd")
    grads['ssm_glu_b'] = dbglu
    grads['ssm_d'] = dd.reshape(w['ssm_d'].shape)
    dparams, du_dirs = [], []
    for direction, res in enumerate(_scan_adjoint_pair(dy0, u, states, adj_ops, "s5_adj")):
        du_dir, dmir, dmii, dmor, dmoi, da = res
        du_dirs.append(du_dir)
        da_re = da[:, 0, :].reshape(-1)
        da_im = da[:, 1, :].reshape(-1)
        dparams.append(vjps[direction]((da_re, da_im, dmir, dmii, dmor, dmoi)))
    du = _du_combine(dy0, d_row, du_dirs[0], du_dirs[1], "ssm_du")
    for i, n in enumerate(ssm_names):
        grads[n] = jnp.stack([dparams[0][i], dparams[1][i]])
    wide_sum = reduce_wide(grads)

    dq, dk, dv, dsink = _attn_bwd_proj(proj, sink_rows, bias, attn, lse, dattn, "attn_bwd")
    grads['attn_sinks'] = jnp.sum(dsink.reshape(ATTN_HEADS, QBLOCK), axis=-1).reshape(w['attn_sinks'].shape)
    dproj = jnp.concatenate([dq, dk, dv, du], axis=-1).astype(BF16)

    dw_in = _mm(dproj, h2, ta=True, out_dtype=BF16, after=wide_sum, name="in_proj_dw")[0]
    sent = put_grads('mix', dict(w_in=dw_in, ssm_glu_w=dwglu, w_out=dw_out))
    ((dx1, dx1_b, dgm),) = _mm_rms_bwd(dproj, w['w_in'], tb=False, xs=[x1], gs=[w['norm_mix']], dres=dx2,
                                       after=sent, name="in_proj_dx")
    grads['norm_mix'] = dgm

    dx0, _, dg, dwg, dwu, dwd = _ffn_bwd(dx1, dx1_b, x, w['norm_ffn1'], w['ffn1_w_gate'], w['ffn1_w_up'],
                                         w['ffn1_w_down'], ffn1_saved, "ffn1")
    grads['norm_ffn1'] = dg
    put_grads('ffn1', dict(ffn1_w_gate=dwg, ffn1_w_up=dwu, ffn1_w_down=dwd))
    return loss, dx0, grads, wide_sum


HBM_SPEC = pl.BlockSpec(memory_space=pl.ANY)


def _chip_peers(x, y):
    return [(1 - x, y), (x, 1 - y), (1 - x, 1 - y)]


HBM_ONLY = pl.BlockSpec(memory_space=pltpu.HBM)
SEM_SPEC = pl.BlockSpec(memory_space=pltpu.SEMAPHORE)
EFFECT = pltpu.SideEffectType.DATAFLOW_SIDE_EFFECTING


def _place_own(srcs, slot, name):
    na = len(srcs)
    r, c = srcs[0].shape
    tr = r // 2

    def body(slot_ref, *refs):
        for a in range(na):
            refs[na + a][0] = refs[a][...]

    return pl.pallas_call(
        body, out_shape=[jax.ShapeDtypeStruct((N_CHIPS, r, c), s.dtype) for s in srcs],
        grid_spec=pltpu.PrefetchScalarGridSpec(
            num_scalar_prefetch=1, grid=(2,), in_specs=[pl.BlockSpec((tr, c), lambda i, s: (i, 0))] * na,
            out_specs=[pl.BlockSpec((1, tr, c), lambda i, s: (s[0], i, 0))] * na),
        compiler_params=_cparams(("parallel",)), name=name)(slot, *srcs)


def _chip_copies(srcs, lands, send_sems, recv_sems, scatter, landed):
    x, y, c = lax.axis_index("x"), lax.axis_index("y"), lax.axis_index("c")
    me = 2 * x + y
    out = []
    for i in range(len(srcs)):
        for j, (px, py) in enumerate(_chip_peers(x, y)):
            p = 2 * px + py
            slot = p if landed else me
            if scatter:
                src, dst = srcs[i].at[p], lands[i].at[slot]
            else:
                rows = _core_half(srcs[i].shape[0], c)
                src, dst = srcs[i].at[rows], lands[i].at[slot, rows]
            out.append(pltpu.make_async_remote_copy(src, dst, send_sems.at[3 * i + j], recv_sems.at[3 * i + j],
                                                    device_id=(px, py, c), device_id_type=MESH))
    return out


def _core_half(nrows, c):
    half = nrows // 2
    return pl.ds(pl.multiple_of(c * half, 16), half)


def _sibling_forward(lands, name):
    n = len(lands)

    def body(*refs):
        bufs = refs[n:2 * n]
        send_sems, recv_sems = refs[2 * n:]
        x, y, c = lax.axis_index("x"), lax.axis_index("y"), lax.axis_index("c")
        mine = [_core_half(b.shape[1], c) for b in bufs]
        theirs = [_core_half(b.shape[1], 1 - c) for b in bufs]
        chips = [2 * px + py for px, py in _chip_peers(x, y)]
        cps = [pltpu.make_async_remote_copy(bufs[i].at[p, mine[i]], bufs[i].at[p, mine[i]], send_sems.at[3 * i + j],
                                            recv_sems.at[3 * i + j], device_id=(x, y, 1 - c), device_id_type=MESH)
               for i in range(n) for j, p in enumerate(chips)]
        for cp in cps:
            cp.start()
        for i in range(n):
            for j, p in enumerate(chips):
                pltpu.make_async_remote_copy(bufs[i].at[p, mine[i]], bufs[i].at[p, theirs[i]], send_sems.at[3 * i + j],
                                             recv_sems.at[3 * i + j], device_id=(x, y, 1 - c),
                                             device_id_type=MESH).wait()

    return pl.pallas_call(
        body, out_shape=[jax.ShapeDtypeStruct(a.shape, a.dtype) for a in lands],
        in_specs=[HBM_SPEC] * n, out_specs=[HBM_SPEC] * n, input_output_aliases={k: k for k in range(n)},
        scratch_shapes=[pltpu.SemaphoreType.DMA((3 * n,)), pltpu.SemaphoreType.DMA((3 * n,))],
        name=name)(*lands)


def _exchange_start(groups, scatter, name, after=None):
    sizes = [len(srcs) for srcs, _ in groups]
    flat_src = [a for srcs, _ in groups for a in srcs]
    flat_land = [a for _, lands in groups for a in lands]
    n = len(flat_src)
    ng = len(groups)

    def body(*refs):
        src_refs, land_refs = refs[:n], refs[n:2 * n]
        n_in = 2 * n + (after is not None)
        sems = refs[n_in:n_in + 2 * ng]
        token_ref = refs[-1]
        off = 0
        for gi, sz in enumerate(sizes):
            for cp in _chip_copies(src_refs[off:off + sz], land_refs[off:off + sz], sems[2 * gi], sems[2 * gi + 1],
                                   scatter, landed=False):
                cp.start()
            off += sz
        token_ref[...] = jnp.zeros_like(token_ref)

    sem_shapes = []
    for sz in sizes:
        sem_shapes += [pltpu.SemaphoreType.DMA((3 * sz,)), pltpu.SemaphoreType.DMA((3 * sz,))]
    hbm = lambda a: pltpu.HBM(a.shape, a.dtype)
    res = pl.pallas_call(
        body, name=name,
        out_shape=(tuple(sem_shapes) + tuple(hbm(a) for a in flat_src) + tuple(hbm(a) for a in flat_land)
                   + (jax.ShapeDtypeStruct((SUBLANES, LANES), F32),)),
        in_specs=[HBM_ONLY] * (2 * n) + [HBM_SPEC] * (after is not None),
        out_specs=tuple([SEM_SPEC] * (2 * ng) + [HBM_ONLY] * (2 * n) + [pl.BlockSpec(memory_space=pltpu.VMEM)]),
        input_output_aliases={k: 2 * ng + k for k in range(2 * n)},
        compiler_params=pltpu.CompilerParams(has_side_effects=EFFECT),
    )(*[pltpu.with_memory_space_constraint(a, pltpu.HBM) for a in flat_src + flat_land],
      *([after] if after is not None else []))
    sems, thru_src, thru_land = res[:2 * ng], res[2 * ng:2 * ng + n], res[2 * ng + n:2 * ng + 2 * n]
    out, off = [], 0
    for gi, sz in enumerate(sizes):
        out.append((sems[2 * gi], sems[2 * gi + 1], list(thru_src[off:off + sz]), list(thru_land[off:off + sz])))
        off += sz
    return out, res[-1]


def _exchange_wait(started, after, scatter, name):
    send_sems, recv_sems, srcs, lands = started
    n = len(srcs)

    def body(*refs):
        src_refs, land_refs = refs[:n], refs[n:2 * n]
        send_ref, recv_ref = refs[2 * n], refs[2 * n + 1]
        for cp in _chip_copies(src_refs, land_refs, send_ref, recv_ref, scatter, landed=True):
            cp.wait_send()
            cp.wait_recv()

    hbm = lambda a: pltpu.HBM(a.shape, a.dtype)
    res = pl.pallas_call(
        body, name=name, out_shape=tuple(hbm(a) for a in srcs) + tuple(hbm(a) for a in lands),
        in_specs=[HBM_ONLY] * (2 * n) + [SEM_SPEC, SEM_SPEC, HBM_SPEC], out_specs=tuple([HBM_ONLY] * (2 * n)),
        input_output_aliases={k: k for k in range(2 * n)},
        compiler_params=pltpu.CompilerParams(has_side_effects=EFFECT),
    )(*srcs, *lands, send_sems, recv_sems, after)
    return list(res[:n]), list(res[n:])


def _half_swap(parts, name):
    n = len(parts)

    def body(*refs):
        ins, outs = refs[:n], refs[n:2 * n]
        send_sems, recv_sems = refs[2 * n:]
        x, y, c = lax.axis_index("x"), lax.axis_index("y"), lax.axis_index("c")
        cps = [pltpu.make_async_remote_copy(ins[i].at[k, _core_half(ins[i].shape[1], 1 - c)], outs[i].at[k],
                                            send_sems.at[N_CHIPS * i + k], recv_sems.at[N_CHIPS * i + k],
                                            device_id=(x, y, 1 - c), device_id_type=MESH)
               for i in range(n) for k in range(N_CHIPS)]
        for cp in cps:
            cp.start()
        for cp in cps:
            cp.wait()

    return pl.pallas_call(
        body, out_shape=[jax.ShapeDtypeStruct((N_CHIPS, p.shape[1] // 2, p.shape[2]), p.dtype) for p in parts],
        in_specs=[HBM_SPEC] * n, out_specs=[HBM_SPEC] * n,
        scratch_shapes=[pltpu.SemaphoreType.DMA((N_CHIPS * n,)), pltpu.SemaphoreType.DMA((N_CHIPS * n,))],
        name=name)(*parts)


def _half_add(parts, sib, slots, name):
    na = len(parts)
    _, r, c = parts[0].shape
    hr = r // 2
    tr = _row_tile(hr, 512)
    nt = hr // tr

    def body(slot_ref, *refs):
        for a in range(na):
            refs[2 * na + a][...] = (refs[2 * a][...].astype(F32) + refs[2 * a + 1][...].astype(F32)).astype(BF16)

    mine = pl.BlockSpec((1, tr, c), lambda k, i, s: (k, i + s[4] * nt, 0))
    half = pl.BlockSpec((1, tr, c), lambda k, i, s: (k, i, 0))
    args = [a for p, sb in zip(parts, sib) for a in (p, sb)]
    return pl.pallas_call(
        body, out_shape=[jax.ShapeDtypeStruct((N_CHIPS, hr, c), BF16)] * na,
        grid_spec=pltpu.PrefetchScalarGridSpec(
            num_scalar_prefetch=1, grid=(N_CHIPS, nt), in_specs=[mine, half] * na, out_specs=[half] * na),
        compiler_params=_cparams(("parallel", "parallel")), name=name)(slots, *args)


def _half_forward(arrs, name):
    n = len(arrs)

    def body(*refs):
        bufs = refs[n:2 * n]
        send_sems, recv_sems = refs[2 * n:]
        x, y, c = lax.axis_index("x"), lax.axis_index("y"), lax.axis_index("c")
        cps = [pltpu.make_async_remote_copy(b.at[_core_half(b.shape[0], c)], b.at[_core_half(b.shape[0], c)],
                                            send_sems.at[i], recv_sems.at[i], device_id=(x, y, 1 - c),
                                            device_id_type=MESH) for i, b in enumerate(bufs)]
        for cp in cps:
            cp.start()
        for i, b in enumerate(bufs):
            pltpu.make_async_remote_copy(b.at[_core_half(b.shape[0], c)], b.at[_core_half(b.shape[0], 1 - c)],
                                         send_sems.at[i], recv_sems.at[i], device_id=(x, y, 1 - c),
                                         device_id_type=MESH).wait()

    return pl.pallas_call(
        body, out_shape=[jax.ShapeDtypeStruct(a.shape, a.dtype) for a in arrs],
        in_specs=[HBM_SPEC] * n, out_specs=[HBM_SPEC] * n, input_output_aliases={k: k for k in range(n)},
        scratch_shapes=[pltpu.SemaphoreType.DMA((n,)), pltpu.SemaphoreType.DMA((n,))],
        name=name)(*arrs)


def _small_exchange(smalls, name):
    nsm = len(smalls)
    rels = [(fx, fy, fc) for fx in (0, 1) for fy in (0, 1) for fc in (0, 1)][1:]

    def body(*refs):
        sins, souts = refs[:nsm], refs[nsm:2 * nsm]
        ssend, srecv, slocal = refs[2 * nsm:]
        x, y, c = lax.axis_index("x"), lax.axis_index("y"), lax.axis_index("c")
        lin = 4 * x + 2 * y + c
        local = [pltpu.make_async_copy(sins[i], souts[i].at[lin], slocal.at[i]) for i in range(nsm)]
        for cp in local:
            cp.start()
        for i in range(nsm):
            for j, (fx, fy, fc) in enumerate(rels):
                pltpu.make_async_remote_copy(sins[i], souts[i].at[lin], ssend.at[i, j], srecv.at[i, j],
                                             device_id=(x ^ fx, y ^ fy, c ^ fc), device_id_type=MESH).start()
        for i in range(nsm):
            for j, (fx, fy, fc) in enumerate(rels):
                src = 4 * (x ^ fx) + 2 * (y ^ fy) + (c ^ fc)
                pltpu.make_async_remote_copy(sins[i], souts[i].at[src], ssend.at[i, j], srecv.at[i, j],
                                             device_id=(x ^ fx, y ^ fy, c ^ fc), device_id_type=MESH).wait()
        for cp in local:
            cp.wait()

    return pl.pallas_call(
        body, out_shape=[jax.ShapeDtypeStruct((N_DEV,) + s.shape, s.dtype) for s in smalls],
        in_specs=[HBM_SPEC] * nsm, out_specs=[HBM_SPEC] * nsm,
        scratch_shapes=[pltpu.SemaphoreType.DMA((nsm, 7)), pltpu.SemaphoreType.DMA((nsm, 7)),
                        pltpu.SemaphoreType.DMA((nsm,))],
        name=name)(*smalls)


def _sum_parts(parts, recv, slots, name):
    na = len(parts)
    _, r, c = parts[0].shape
    tr = _row_tile(r, 192)

    def body(slot_ref, *refs):
        for a in range(na):
            own_ref, r0_ref, r1_ref, r2_ref = refs[4 * a:4 * a + 4]
            refs[4 * na + a][...] = ((own_ref[0].astype(F32) + r0_ref[0].astype(F32))
                                     + (r1_ref[0].astype(F32) + r2_ref[0].astype(F32)))

    blk = lambda k: pl.BlockSpec((1, tr, c), lambda i, s, k=k: (s[k], i, 0))
    out_blk = pl.BlockSpec((tr, c), lambda i, s: (i + s[4] * (r // tr), 0))
    args = [a for p, rv in zip(parts, recv) for a in (p, rv, rv, rv)]
    return pl.pallas_call(
        body, out_shape=[jax.ShapeDtypeStruct((2 * r, c), F32)] * na,
        grid_spec=pltpu.PrefetchScalarGridSpec(
            num_scalar_prefetch=1, grid=(r // tr,), in_specs=[blk(0), blk(1), blk(2), blk(3)] * na,
            out_specs=[out_blk] * na),
        compiler_params=_cparams(("parallel",)), name=name)(slots, *args)


def _small_allreduce(packed, name):
    rows = packed.shape[0]
    pr = rows // N_DEV
    rels = [(fx, fy, fc) for fx in (0, 1) for fy in (0, 1) for fc in (0, 1)][1:]

    def body(in_ref, out_ref, recv_ref, send1, recv1, send2, recv2):
        x, y, c = lax.axis_index("x"), lax.axis_index("y"), lax.axis_index("c")
        lin = 4 * x + 2 * y + c
        piece = lambda ref, k: ref.at[pl.ds(pl.multiple_of(k * pr, pr), pr), :]
        peers = [((x ^ fx, y ^ fy, c ^ fc), 4 * (x ^ fx) + 2 * (y ^ fy) + (c ^ fc)) for fx, fy, fc in rels]
        for j, (dev, plin) in enumerate(peers):
            pltpu.make_async_remote_copy(piece(in_ref, plin), recv_ref.at[lin], send1.at[j], recv1.at[j],
                                         device_id=dev, device_id_type=MESH).start()
        recv_ref[lin] = piece(in_ref, lin)[...]
        for j, (dev, plin) in enumerate(peers):
            pltpu.make_async_remote_copy(piece(in_ref, plin), recv_ref.at[plin], send1.at[j], recv1.at[j],
                                         device_id=dev, device_id_type=MESH).wait()
        acc = recv_ref[0]
        for k in range(1, N_DEV):
            acc = acc + recv_ref[k]
        piece(out_ref, lin)[...] = acc
        for j, (dev, plin) in enumerate(peers):
            pltpu.make_async_remote_copy(piece(out_ref, lin), piece(out_ref, lin), send2.at[j], recv2.at[j],
                                         device_id=dev, device_id_type=MESH).start()
        for j, (dev, plin) in enumerate(peers):
            pltpu.make_async_remote_copy(piece(out_ref, lin), piece(out_ref, plin), send2.at[j], recv2.at[j],
                                         device_id=dev, device_id_type=MESH).wait()

    vm = pl.BlockSpec(memory_space=pltpu.VMEM)
    return pl.pallas_call(
        body, out_shape=jax.ShapeDtypeStruct(packed.shape, F32), in_specs=[vm], out_specs=vm,
        scratch_shapes=[pltpu.VMEM((N_DEV, pr, LANES), F32)] + [pltpu.SemaphoreType.DMA((7,))] * 4,
        compiler_params=pltpu.CompilerParams(vmem_limit_bytes=VMEM_LIMIT), name=name)(packed)


def _adamw_math(w, m, v, g):
    nm = ADAM_B1 * m + (1.0 - ADAM_B1) * g
    nv = ADAM_B2 * v + (1.0 - ADAM_B2) * (g * g)
    m_hat = nm * (1.0 / (1.0 - ADAM_B1 ** ADAM_STEP))
    v_hat = nv * (1.0 / (1.0 - ADAM_B2 ** ADAM_STEP))
    return -ADAM_LR * (m_hat / (jnp.sqrt(v_hat) + ADAM_EPS) + ADAM_WD * w), nm, nv


def _adamw(ws, ms, vs, gs, name):
    na = len(ws)
    r, c = ws[0].shape
    tr = _row_tile(r)

    def body(*refs):
        for a in range(na):
            w_ref, m_ref, v_ref, g_ref = refs[4 * a:4 * a + 4]
            d_ref, nm_ref, nv_ref = refs[4 * na + 3 * a:4 * na + 3 * a + 3]
            d_ref[...], nm_ref[...], nv_ref[...] = _adamw_math(w_ref[...], m_ref[...], v_ref[...], g_ref[...])

    blk = pl.BlockSpec((tr, c), lambda i: (i, 0))
    sh = jax.ShapeDtypeStruct((r, c), F32)
    args = [a for group in zip(ws, ms, vs, gs) for a in group]
    res = pl.pallas_call(body, out_shape=[sh] * (3 * na), grid=(r // tr,), in_specs=[blk] * (4 * na),
                         out_specs=[blk] * (3 * na), compiler_params=_cparams(("parallel",)), name=name)(*args)
    return [tuple(res[3 * a:3 * a + 3]) for a in range(na)]


def _adamw_small(ws, ms, vs, alls, split, name):
    n = len(ws)
    lead = split if split is not None else ()
    nl = len(lead)
    nslots = alls[0].shape[0]

    def blocks(shape):
        if split is None:
            return tuple(shape), (lambda *g: (0,) * len(shape))
        blk = (shape[0], shape[1] // lead[0], shape[2] // lead[1]) + tuple(shape[3:])
        return blk, (lambda *g: (0, g[0], g[1]) + (0,) * (len(shape) - 3))

    def body(*refs):
        w_refs, m_refs, v_refs, a_refs = (refs[k * n:(k + 1) * n] for k in range(4))
        g_refs, d_refs, nm_refs, nv_refs = (refs[(4 + k) * n:(5 + k) * n] for k in range(4))
        k = pl.program_id(nl)
        for i in range(n):
            @pl.when(k == 0)
            def _(i=i):
                g_refs[i][...] = a_refs[i][0]

            @pl.when(k > 0)
            def _(i=i):
                g_refs[i][...] += a_refs[i][0]

            @pl.when(k == nslots - 1)
            def _(i=i):
                d_refs[i][...], nm_refs[i][...], nv_refs[i][...] = _adamw_math(
                    w_refs[i][...], m_refs[i][...], v_refs[i][...], g_refs[i][...])

    specs, aspecs, shapes = [], [], []
    for wa in ws:
        blk, imap = blocks(wa.shape)
        specs.append(pl.BlockSpec(blk, imap))
        aspecs.append(pl.BlockSpec((1,) + blk, (lambda *g, imap=imap: (g[nl],) + imap(*g))))
        shapes.append(jax.ShapeDtypeStruct(wa.shape, F32))
    res = pl.pallas_call(
        body, out_shape=shapes * 4, grid=tuple(lead) + (nslots,), in_specs=specs * 3 + aspecs,
        out_specs=specs * 4, compiler_params=_cparams(("parallel",) * nl + ("arbitrary",)),
        name=name)(*ws, *ms, *vs, *alls)
    return res[:n], res[n:2 * n], res[2 * n:3 * n], res[3 * n:]


def kernel(x, norm_ffn1, ffn1_w_gate, ffn1_w_up, ffn1_w_down, norm_mix, w_in, attn_sinks, ssm_lambda_re, ssm_lambda_im, ssm_log_dt, ssm_b_re, ssm_b_im, ssm_c_re, ssm_c_im, ssm_d, ssm_glu_w, ssm_glu_b, attn_out_norm, ssm_out_norm, w_out, norm_ffn2, ffn2_w_gate, ffn2_w_up, ffn2_w_down, final_norm, loss_target, m_norm_ffn1, m_ffn1_w_gate, m_ffn1_w_up, m_ffn1_w_down, m_norm_mix, m_w_in, m_attn_sinks, m_ssm_lambda_re, m_ssm_lambda_im, m_ssm_log_dt, m_ssm_b_re, m_ssm_b_im, m_ssm_c_re, m_ssm_c_im, m_ssm_d, m_ssm_glu_w, m_ssm_glu_b, m_attn_out_norm, m_ssm_out_norm, m_w_out, m_norm_ffn2, m_ffn2_w_gate, m_ffn2_w_up, m_ffn2_w_down, m_final_norm, v_norm_ffn1, v_ffn1_w_gate, v_ffn1_w_up, v_ffn1_w_down, v_norm_mix, v_w_in, v_attn_sinks, v_ssm_lambda_re, v_ssm_lambda_im, v_ssm_log_dt, v_ssm_b_re, v_ssm_b_im, v_ssm_c_re, v_ssm_c_im, v_ssm_d, v_ssm_glu_w, v_ssm_glu_b, v_attn_out_norm, v_ssm_out_norm, v_w_out, v_norm_ffn2, v_ffn2_w_gate, v_ffn2_w_up, v_ffn2_w_down, v_final_norm):
    given = dict(locals())
    wts = {n: given[n] for n in WEIGHTS}

    order = [g for g in GROUPS]
    cx, cy = lax.axis_index("x"), lax.axis_index("y")
    slots = jnp.stack([2 * cx + cy, 2 * (1 - cx) + cy, 2 * cx + 1 - cy, 2 * (1 - cx) + 1 - cy,
                       lax.axis_index("c")]).astype(jnp.int32)
    def view(a, n):
        if n in TRANSPOSED:
            return jnp.swapaxes(a[0], 0, 1)
        if n in BIG:
            return a[0]
        if n in ('ssm_b_re', 'ssm_b_im'):
            return jnp.swapaxes(a, -1, -2)
        return a.reshape(1, -1) if a.ndim == 1 else a

    def unview(a, n):
        if n in TRANSPOSED:
            return jnp.swapaxes(a, 0, 1)[None]
        if n in ('ssm_b_re', 'ssm_b_im'):
            return jnp.swapaxes(a, -1, -2)
        return a.reshape(wts[n].shape)

    started, gather_token = {}, None
    for g in order:
        shards = [view(wts[n], n).astype(BF16) for n in GROUPS[g]]
        if len({s.shape for s in shards}) == 1:
            placed = _place_own(shards, slots, f"weights_place_{g}")
        else:
            placed = [_place_own([s], slots, f"weights_place_{n}")[0] for n, s in zip(GROUPS[g], shards)]
        st, gather_token = _exchange_start([(shards, placed)], False, f"weights_start_{g}", after=gather_token)
        started[g] = st[0]

    def get_weights(group, after):
        if group == order[0]:
            after = after + gather_token[:1, :1]
        _, lands = _exchange_wait(started[group], after, False, f"weights_wait_{group}")
        lands = _sibling_forward(lands, f"weights_forward_{group}")
        out = dict(zip(GROUPS[group], lands))
        for n in ('w_in', 'ssm_glu_w', 'w_out'):
            if n in out:
                out[n] = out[n].reshape(-1, out[n].shape[-1])
        return out

    sent, tokens = {}, {}

    def put_grads(group, gd):
        parts = []
        for n in GROUPS[group]:
            g = gd[n]
            if g.ndim == 2:
                g = g.reshape(N_CHIPS, g.shape[0] // N_CHIPS, g.shape[1])
            parts.append(g.astype(BF16))
        sib = _half_swap(parts, f"grads_half_swap_{group}")
        same = len({p.shape for p in parts}) == 1
        batches = [list(range(len(parts)))] if same else [[i] for i in range(len(parts))]
        halves = [None] * len(parts)
        for b in batches:
            res = _half_add([parts[i] for i in b], [sib[i] for i in b], slots, f"grads_half_add_{GROUPS[group][b[0]]}")
            for i, h in zip(b, res):
                halves[i] = h
        parts = halves
        lands = [lax.empty(p.shape, p.dtype) for p in parts]
        started_g, tokens[group] = _exchange_start([(parts, lands)], True, f"grads_start_{group}")
        sent[group] = started_g[0]
        return tokens[group]

    w = {n: (wts[n][0] if wts[n].ndim > 1 else wts[n]) for n in SMALL}
    w['norm_ffn1'], w['norm_mix'], w['norm_ffn2'] = wts['norm_ffn1'], wts['norm_mix'], wts['norm_ffn2']
    w['ssm_b_re'], w['ssm_b_im'] = view(wts['ssm_b_re'], 'ssm_b_re')[0], view(wts['ssm_b_im'], 'ssm_b_im')[0]
    w['ssm_log_dt'] = w['ssm_log_dt'] + gather_token[0, 0]
    wide =['ssm_b_re', 'ssm_b_im', 'ssm_c_re', 'ssm_c_im']

    def reduce_wide(gd):
        packed = jnp.concatenate([gd[n].reshape(-1, LANES) for n in wide])
        return _small_allreduce(packed, "small_grads_allreduce")

    loss_row, dx, grads, wide_sum = _local_step(x[0], loss_target[0], w, get_weights, put_grads, reduce_wide)

    out_g, out_d, out_m, out_v = {}, {}, {}, {}

    def finish(group, after):
        names = GROUPS[group]
        parts, recv = _exchange_wait(sent[group], after, True, f"grads_wait_{group}")
        same = len({p.shape for p in parts}) == 1
        batches = [list(range(len(names)))] if same else [[i] for i in range(len(names))]
        sums = [None] * len(names)
        for b in batches:
            res = _sum_parts([parts[i] for i in b], [recv[i] for i in b], slots, f"grad_sum_{names[b[0]]}")
            for i, sm in zip(b, res):
                sums[i] = sm
        full = _half_forward(sums, f"grad_half_forward_{group}")
        for b in batches:
            res = _adamw([view(wts[names[i]], names[i]) for i in b], [view(given['m_' + names[i]], names[i]) for i in b],
                         [view(given['v_' + names[i]], names[i]) for i in b], [full[i] for i in b],
                         f"adamw_{names[b[0]]}")
            for i, (d, nm, nv) in zip(b, res):
                n = names[i]
                out_g[n], out_d[n], out_m[n], out_v[n] = (unview(a, n) for a in (full[i], d, nm, nv))
        return nv

    done = finish('ffn2', tokens['ffn1'])
    done = finish('mix', done)

    nat = {n: view(wts[n], n).shape for n in SMALL}
    narrow = [n for n in SMALL if n not in wide]
    alls = list(_small_exchange([grads[n].reshape(nat[n]) for n in narrow] + [loss_row], "small_grads_allgather"))
    loss = jnp.sum(alls.pop()[:, 0, 0])
    rows = wide_sum.shape[0] // len(wide)
    wide_g = [wide_sum[i * rows:(i + 1) * rows].reshape((1,) + nat[n]) for i, n in enumerate(wide)]
    for group, gs, split, tag in ((narrow, alls, None, "adamw_small"), (wide, wide_g, (2, 4), "adamw_ssm_bc")):
        res = _adamw_small([view(wts[n], n) for n in group], [view(given['m_' + n], n) for n in group],
                           [view(given['v_' + n], n) for n in group], gs, split, tag)
        for dst, vals in zip((out_g, out_d, out_m, out_v), res):
            for n, a in zip(group, vals):
                dst[n] = unview(a, n)

    finish('ffn1', out_v['norm_ffn1'][:, :1] + out_v['ssm_c_re'].reshape(1, -1)[:, :1] + done[:1, :1] + loss)

    return (loss, dx[None], *[out_g[n] for n in WEIGHTS], *[out_d[n] for n in WEIGHTS],
            *[out_m[n] for n in WEIGHTS], *[out_v[n] for n in WEIGHTS])
```

```python
import functools
import math

import numpy as np
import jax
import jax.numpy as jnp
from jax import lax
from jax.experimental import pallas as pl
from jax.experimental.pallas import tpu as pltpu

F32 = jnp.float32
BF16 = jnp.bfloat16
MESH = pl.DeviceIdType.MESH

EPS = 1e-6
NEG_INF = -1e30
LAMBDA_RE_MAX = -1e-4
ATTN_HEADS = 8
KV_HEADS = 2
GQ = ATTN_HEADS // KV_HEADS
HEAD_DIM = 64
ATTN_WIDTH = 512
KV_WIDTH = 128
WINDOW = 128
QBLOCK = 128
SSM_WIDTH = 512
SSM_GROUPS = 32
SSM_CH = 16
SSM_STATE = 64
N_STRIPS = 4
STRIP_IN = SSM_WIDTH // N_STRIPS
STRIP_ST = SSM_GROUPS * SSM_STATE // N_STRIPS
SUBLANES = 8
LANES = 128
N_CHIPS = 4
N_DEV = 8

ADAM_LR = 0.001
ADAM_B1 = 0.9
ADAM_B2 = 0.999
ADAM_EPS = 1e-08
ADAM_WD = 0.01
ADAM_STEP = 10

VMEM_LIMIT = 48 * 1024 * 1024

WEIGHTS = ['norm_ffn1', 'ffn1_w_gate', 'ffn1_w_up', 'ffn1_w_down', 'norm_mix', 'w_in', 'attn_sinks',
           'ssm_lambda_re', 'ssm_lambda_im', 'ssm_log_dt', 'ssm_b_re', 'ssm_b_im', 'ssm_c_re', 'ssm_c_im',
           'ssm_d', 'ssm_glu_w', 'ssm_glu_b', 'attn_out_norm', 'ssm_out_norm', 'w_out', 'norm_ffn2',
           'ffn2_w_gate', 'ffn2_w_up', 'ffn2_w_down', 'final_norm']
BIG = ['ffn1_w_gate', 'ffn1_w_up', 'ffn1_w_down', 'w_in', 'ssm_glu_w', 'w_out',
       'ffn2_w_gate', 'ffn2_w_up', 'ffn2_w_down']
SMALL = [n for n in WEIGHTS if n not in BIG]
TRANSPOSED = ['ffn1_w_gate', 'ffn1_w_up', 'w_in', 'ffn2_w_gate', 'ffn2_w_up']
GROUPS = {'ffn1': ['ffn1_w_gate', 'ffn1_w_up', 'ffn1_w_down'],
          'mix': ['w_in', 'ssm_glu_w', 'w_out'],
          'ffn2': ['ffn2_w_gate', 'ffn2_w_up', 'ffn2_w_down']}


def _cparams(sem=None):
    return pltpu.CompilerParams(dimension_semantics=sem, vmem_limit_bytes=VMEM_LIMIT)


def _tile(n, pref):
    if n <= pref:
        return n
    for t in (pref, pref // 2, pref // 4):
        if t % LANES == 0 and n % t == 0:
            return t
    return n


def _sigmoid(x):
    return 1.0 / (1.0 + jnp.exp(-x))


def _sigmoid_tanh(x):
    return 0.5 * jnp.tanh(0.5 * x) + 0.5


def _mm(a, b, *, ta=False, tb=False, reduce_s=False, res=None, scale=1.0, out_dtype=F32, after=None, name):
    a3 = a if a.ndim == 3 else a[None]
    b3 = b if b.ndim == 3 else b[None]
    sa, sb = a3.shape[0], b3.shape[0]
    ns = max(sa, sb)
    (kk, m) = a3.shape[1:] if ta else a3.shape[1:][::-1]
    (n, kb) = b3.shape[1:] if tb else b3.shape[1:][::-1]
    assert kk == kb, (a3.shape, b3.shape)
    tm, tn, tk = _tile(m, 1024), _tile(n, 1024), _tile(kk, 2048)
    nm, nn, nk = m // tm, n // tn, kk // tk
    has_res = res is not None
    single = nk == 1 and not (reduce_s and ns > 1)

    if reduce_s:
        grid = (nm, nn, ns, nk)
        ids = lambda i, j, s, k: (s, i, j, k)
        sem = ("parallel", "parallel", "arbitrary", "arbitrary")
    else:
        grid = (ns, nm, nn, nk)
        ids = lambda s, i, j, k: (s, i, j, k)
        sem = ("parallel", "parallel", "parallel", "arbitrary")

    def a_map(*g):
        s, i, j, k = ids(*g)
        s = s if sa > 1 else 0
        return (s, k, i) if ta else (s, i, k)

    def b_map(*g):
        s, i, j, k = ids(*g)
        s = s if sb > 1 else 0
        return (s, j, k) if tb else (s, k, j)

    def o_map(*g):
        s, i, j, k = ids(*g)
        return (i, j) if reduce_s else (s, i, j)

    a_blk = (1, tk, tm) if ta else (1, tm, tk)
    b_blk = (1, tn, tk) if tb else (1, tk, tn)
    dims = (((0 if ta else 1,), (1 if tb else 0,)), ((), ()))

    def body(*refs):
        a_ref, b_ref = refs[0], refs[1]
        r_ref = refs[2] if has_res else None
        o_ref = refs[2 + has_res + (after is not None)]
        acc_ref = None if single else refs[-1]
        s, _, _, k = ids(*[pl.program_id(d) for d in range(4)])
        prod = lax.dot_general(a_ref[0].astype(BF16), b_ref[0].astype(BF16), dims, preferred_element_type=F32)

        def finish(out):
            if scale != 1.0:
                out = out * scale
            if has_res:
                out = r_ref[...].reshape(out.shape) + out
            o_ref[...] = out.astype(out_dtype).reshape(o_ref.shape)

        if single:
            finish(prod)
            return
        if reduce_s:
            first = jnp.logical_and(s == 0, k == 0)
            last = jnp.logical_and(s == ns - 1, k == nk - 1)
        else:
            first, last = k == 0, k == nk - 1

        acc_ref[...] = prod + jnp.where(first, 0.0, acc_ref[...])

        @pl.when(last)
        def _():
            finish(acc_ref[...])

    in_specs = [pl.BlockSpec(a_blk, a_map), pl.BlockSpec(b_blk, b_map)]
    args = [a3, b3]
    if reduce_s:
        out_shape = jax.ShapeDtypeStruct((m, n), out_dtype)
        o_spec = pl.BlockSpec((tm, tn), o_map)
    else:
        out_shape = jax.ShapeDtypeStruct((ns, m, n), out_dtype)
        o_spec = pl.BlockSpec((1, tm, tn), o_map)
    if has_res:
        assert res.shape == out_shape.shape
        in_specs.append(o_spec)
        args.append(res)
    if after is not None:
        in_specs.append(HBM_SPEC)
        args.append(after)
    return pl.pallas_call(body, out_shape=out_shape, grid=grid, in_specs=in_specs, out_specs=o_spec,
                          scratch_shapes=[] if single else [pltpu.VMEM((tm, tn), F32)],
                          compiler_params=_cparams(sem), name=name)(*args)


def _row_tile(t, cap=256):
    for step in (16, SUBLANES):
        for tr in range(min(cap, t) // step * step, 0, -step):
            if t % tr == 0:
                return tr
    return t


def _norm_mm(xs, gs, w, *, tb, res, name):
    nx = len(xs)
    t = xs[0].shape[0]
    widths = [x.shape[1] for x in xs]
    k = sum(widths)
    n = w.shape[0] if tb else w.shape[1]
    tm, tn = _tile(t, 512), _tile(n, 512)
    has_res = res is not None

    def body(*refs):
        x_refs, g_refs, w_ref = refs[:nx], refs[nx:2 * nx], refs[2 * nx]
        r_ref = refs[2 * nx + 1] if has_res else None
        o_ref, h_ref, h_sc = refs[2 * nx + 1 + has_res:]

        @pl.when(pl.program_id(1) == 0)
        def _():
            off = 0
            for x_ref, g_ref, wd in zip(x_refs, g_refs, widths):
                xv = x_ref[...]
                r = lax.rsqrt(jnp.mean(xv * xv, axis=-1, keepdims=True) + EPS)
                h_sc[:, off:off + wd] = (xv * r * g_ref[...]).astype(BF16)
                off += wd
            h_ref[...] = h_sc[...]

        prod = lax.dot_general(h_sc[...], w_ref[...], NT_DIMS if tb else (((1,), (0,)), ((), ())),
                               preferred_element_type=F32)
        o_ref[...] = r_ref[...] + prod if has_res else prod

    in_specs = [pl.BlockSpec((tm, wd), lambda i, j: (i, 0)) for wd in widths]
    in_specs += [pl.BlockSpec((1, wd), lambda i, j: (0, 0)) for wd in widths]
    in_specs.append(pl.BlockSpec((tn, k), lambda i, j: (j, 0)) if tb else pl.BlockSpec((k, tn), lambda i, j: (0, j)))
    tile = pl.BlockSpec((tm, tn), lambda i, j: (i, j))
    if has_res:
        in_specs.append(tile)
    return pl.pallas_call(
        body, out_shape=(jax.ShapeDtypeStruct((t, n), F32), jax.ShapeDtypeStruct((t, k), BF16)),
        grid=(t // tm, n // tn), in_specs=in_specs,
        out_specs=(tile, pl.BlockSpec((tm, k), lambda i, j: (i, 0))),
        scratch_shapes=[pltpu.VMEM((tm, k), BF16)], compiler_params=_cparams(("parallel", "arbitrary")),
        name=name)(*xs, *gs, w, *([res] if has_res else []))


def _rms_bwd_rows(xv, gv, dhv):
    r = lax.rsqrt(jnp.mean(xv * xv, axis=-1, keepdims=True) + EPS)
    nrm = xv * r
    dn = dhv * gv
    return r * (dn - nrm * jnp.mean(dn * nrm, axis=-1, keepdims=True)), dhv * nrm


def _mm_rms_bwd(a, b, *, tb, xs, gs, dres, after, name):
    nx = len(xs)
    t, k = a.shape
    widths = [x.shape[1] for x in xs]
    n = sum(widths)
    assert n == (b.shape[0] if tb else b.shape[1])
    tm = _tile(t, 512)
    has_res, has_after = dres is not None, after is not None

    def body(*refs):
        a_ref, b_ref = refs[0], refs[1]
        x_refs, g_refs = refs[2:2 + nx], refs[2 + nx:2 + 2 * nx]
        r_ref = refs[2 + 2 * nx] if has_res else None
        outs = refs[2 + 2 * nx + has_res + has_after:]
        dh = lax.dot_general(a_ref[...], b_ref[...], NT_DIMS if tb else (((1,), (0,)), ((), ())),
                             preferred_element_type=F32)
        off = 0
        for i, wd in enumerate(widths):
            dx_ref, dxb_ref, dg_ref = outs[3 * i:3 * i + 3]
            dx, dgs = _rms_bwd_rows(x_refs[i][...], g_refs[i][...], dh[:, off:off + wd])
            if has_res:
                dx = dx + r_ref[...]
            dx_ref[...] = dx
            dxb_ref[...] = dx.astype(BF16)
            part = jnp.sum(dgs, axis=0, keepdims=True)
            dg_ref[...] = part + jnp.where(pl.program_id(0) > 0, dg_ref[...], 0.0)
            off += wd

    in_specs = [pl.BlockSpec((tm, k), lambda i: (i, 0)), pl.BlockSpec(b.shape, lambda i: (0, 0))]
    in_specs += [pl.BlockSpec((tm, wd), lambda i: (i, 0)) for wd in widths]
    in_specs += [pl.BlockSpec((1, wd), lambda i: (0, 0)) for wd in widths]
    args = [a, b, *xs, *gs]
    if has_res:
        in_specs.append(pl.BlockSpec((tm, widths[0]), lambda i: (i, 0)))
        args.append(dres)
    if has_after:
        in_specs.append(HBM_SPEC)
        args.append(after)
    out_shape, out_specs = [], []
    for wd in widths:
        out_shape += [jax.ShapeDtypeStruct((t, wd), F32), jax.ShapeDtypeStruct((t, wd), BF16),
                      jax.ShapeDtypeStruct((1, wd), F32)]
        out_specs += [pl.BlockSpec((tm, wd), lambda i: (i, 0)), pl.BlockSpec((tm, wd), lambda i: (i, 0)),
                      pl.BlockSpec((1, wd), lambda i: (0, 0))]
    res = pl.pallas_call(body, out_shape=out_shape, grid=(t // tm,), in_specs=in_specs, out_specs=out_specs,
                         compiler_params=_cparams(("arbitrary",)), name=name)(*args)
    return [tuple(res[3 * i:3 * i + 3]) for i in range(nx)]


FFN_ROWS = 512
FFN_FWD_ROWS = 1024
FFN_SPLIT = 2
FFN_W_ROWS = 1024
SCAN_ROWS = 256


NT_DIMS = (((1,), (1,)), ((), ()))
TN_DIMS = (((0,), (0,)), ((), ()))


def _ffn_fwd_call(x, g, wg, wu, wd, name):
    t, d = x.shape
    ns, f, _ = wg.shape
    tm = _tile(t, FFN_FWD_ROWS)

    def body(x_ref, g_ref, wg_ref, wu_ref, wd_ref, xo_ref, h_ref, gate_ref, up_ref, h_sc, acc_ref):
        s = pl.program_id(1)

        @pl.when(s == 0)
        def _():
            xv = x_ref[...]
            r = lax.rsqrt(jnp.mean(xv * xv, axis=-1, keepdims=True) + EPS)
            hb = (xv * r * g_ref[...]).astype(BF16)
            h_sc[...] = hb
            h_ref[...] = hb

        for r0 in range(0, tm, tm // FFN_SPLIT):
            rows = slice(r0, r0 + tm // FFN_SPLIT)
            hb = h_sc[rows, :]
            gate = lax.dot_general(hb, wg_ref[0], NT_DIMS, preferred_element_type=F32)
            up = lax.dot_general(hb, wu_ref[0], NT_DIMS, preferred_element_type=F32)
            gate_ref[0, rows, :] = gate.astype(BF16)
            up_ref[0, rows, :] = up.astype(BF16)
            act = (gate * _sigmoid_tanh(gate) * up).astype(BF16)
            prod = jnp.dot(act, wd_ref[0], preferred_element_type=F32)
            acc_ref[rows, :] = prod + jnp.where(s > 0, acc_ref[rows, :], 0.0)

        @pl.when(s == ns - 1)
        def _():
            xo_ref[...] = x_ref[...] + 0.5 * acc_ref[...]

    row = pl.BlockSpec((tm, d), lambda i, s: (i, 0))
    vec = pl.BlockSpec((1, d), lambda i, s: (0, 0))
    wrow = pl.BlockSpec((1, f, d), lambda i, s: (s, 0, 0))
    hid = pl.BlockSpec((1, tm, f), lambda i, s: (s, i, 0))
    hid_sh = jax.ShapeDtypeStruct((ns, t, f), BF16)
    return pl.pallas_call(
        body, out_shape=(jax.ShapeDtypeStruct((t, d), F32), jax.ShapeDtypeStruct((t, d), BF16), hid_sh, hid_sh),
        grid=(t // tm, ns), in_specs=[row, vec, wrow, wrow, wrow], out_specs=(row, row, hid, hid),
        scratch_shapes=[pltpu.VMEM((tm, d), BF16), pltpu.VMEM((tm, d), F32)],
        compiler_params=_cparams(("parallel", "arbitrary")), name=name)(x, g, wg, wu, wd)


def _ffn_bwd_x_call(dxo, dxo_b, x, g, gate, up, wg, wu, wd, name):
    t, d = x.shape
    ns, f, _ = wg.shape
    tm = _tile(t, FFN_ROWS)

    def body(dxo_ref, dxb_ref, x_ref, g_ref, gate_ref, up_ref, wg_ref, wu_ref, wd_ref,
             dx_ref, dxob_ref, dgn_ref, dgate_ref, dup_ref, act_ref, dh_ref):
        s, i = pl.program_id(0), pl.program_id(1)
        base = pl.multiple_of(i * tm, tm)
        for r0 in range(0, tm, tm // FFN_SPLIT):
            rows = slice(r0, r0 + tm // FFN_SPLIT)
            acc_rows = pl.ds(base + r0, tm // FFN_SPLIT)
            dact = lax.dot_general(dxb_ref[rows, :], wd_ref[0], NT_DIMS, preferred_element_type=F32) * 0.5
            gv = gate_ref[0, rows, :].astype(F32)
            uv = up_ref[0, rows, :].astype(F32)
            sg = _sigmoid_tanh(gv)
            silu = gv * sg
            act_ref[0, rows, :] = (silu * uv).astype(BF16)
            dub = (dact * silu).astype(BF16)
            dgb = (dact * uv * sg * (1.0 + gv * (1.0 - sg))).astype(BF16)
            dup_ref[0, rows, :] = dub
            dgate_ref[0, rows, :] = dgb
            prod = (jnp.dot(dgb, wg_ref[0], preferred_element_type=F32)
                    + jnp.dot(dub, wu_ref[0], preferred_element_type=F32))

            dh_ref[acc_rows, :] = prod + jnp.where(s > 0, dh_ref[acc_rows, :], 0.0)

        @pl.when(jnp.logical_and(i == 0, s == 0))
        def _():
            dgn_ref[...] = jnp.zeros_like(dgn_ref)

        @pl.when(s == ns - 1)
        def _():
            dx, dgs = _rms_bwd_rows(x_ref[...], g_ref[...], dh_ref[pl.ds(base, tm), :])
            dx = dx + dxo_ref[...]
            dx_ref[...] = dx
            dxob_ref[...] = dx.astype(BF16)
            dgn_ref[...] += jnp.sum(dgs, axis=0, keepdims=True)

    last_only = lambda s, i: (jnp.where(s == ns - 1, i, 0), 0)
    row_last = pl.BlockSpec((tm, d), last_only)
    row = pl.BlockSpec((tm, d), lambda s, i: (i, 0))
    vec = pl.BlockSpec((1, d), lambda s, i: (0, 0))
    wrow = pl.BlockSpec((1, f, d), lambda s, i: (s, 0, 0))
    hid = pl.BlockSpec((1, tm, f), lambda s, i: (s, i, 0))
    hid_sh = jax.ShapeDtypeStruct((ns, t, f), BF16)
    return pl.pallas_call(
        body,
        out_shape=(jax.ShapeDtypeStruct((t, d), F32), jax.ShapeDtypeStruct((t, d), BF16),
                   jax.ShapeDtypeStruct((1, d), F32), hid_sh, hid_sh, hid_sh),
        grid=(ns, t // tm), in_specs=[row_last, row, row_last, vec, hid, hid, wrow, wrow, wrow],
        out_specs=(row_last, row_last, vec, hid, hid, hid), scratch_shapes=[pltpu.VMEM((t, d), F32)],
        compiler_params=_cparams(("arbitrary", "arbitrary")), name=name)(dxo, dxo_b, x, g, gate, up, wg, wu, wd)


def _ffn_bwd_w_call(h, dxo_b, dgate, dup, act, name):
    t, d = h.shape
    ns, _, f = dgate.shape
    tm = _tile(t, FFN_W_ROWS)
    nm = t // tm

    def body(h_ref, dxb_ref, dgate_ref, dup_ref, act_ref, dwg_ref, dwu_ref, dwd_ref, ag_ref, au_ref, ad_ref):
        i = pl.program_id(1)
        hv = h_ref[...]
        pg = lax.dot_general(dgate_ref[0], hv, TN_DIMS, preferred_element_type=F32)
        pu = lax.dot_general(dup_ref[0], hv, TN_DIMS, preferred_element_type=F32)
        pd = lax.dot_general(act_ref[0], dxb_ref[...], TN_DIMS, preferred_element_type=F32)

        ag_ref[...] = pg + jnp.where(i > 0, ag_ref[...], 0.0)
        au_ref[...] = pu + jnp.where(i > 0, au_ref[...], 0.0)
        ad_ref[...] = pd + jnp.where(i > 0, ad_ref[...], 0.0)

        @pl.when(i == nm - 1)
        def _():
            dwg_ref[0] = ag_ref[...].astype(BF16)
            dwu_ref[0] = au_ref[...].astype(BF16)
            dwd_ref[0] = (0.5 * ad_ref[...]).astype(BF16)

    row = pl.BlockSpec((tm, d), lambda s, i: (i, 0))
    hid = pl.BlockSpec((1, tm, f), lambda s, i: (s, i, 0))
    wrow = pl.BlockSpec((1, f, d), lambda s, i: (s, 0, 0))
    wsh = jax.ShapeDtypeStruct((ns, f, d), BF16)
    return pl.pallas_call(
        body, out_shape=(wsh, wsh, wsh),
        grid=(ns, nm), in_specs=[row, row, hid, hid, hid], out_specs=(wrow, wrow, wrow),
        scratch_shapes=[pltpu.VMEM((f, d), F32), pltpu.VMEM((f, d), F32), pltpu.VMEM((f, d), F32)],
        compiler_params=_cparams(("parallel", "arbitrary")), name=name)(h, dxo_b, dgate, dup, act)


def _loss_head(x, g, tgt, name):
    t, w = x.shape
    tr = _row_tile(t)

    def body(x_ref, g_ref, t_ref, loss_ref, dx_ref, dxb_ref, dg_ref):
        xv = x_ref[...]
        gv = g_ref[...]
        r = lax.rsqrt(jnp.mean(xv * xv, axis=-1, keepdims=True) + EPS)
        nrm = xv * r
        err = nrm * gv - t_ref[...]
        dout = err * (1.0 / w)
        dn = dout * gv
        dx = r * (dn - nrm * jnp.mean(dn * nrm, axis=-1, keepdims=True))
        dx_ref[...] = dx
        dxb_ref[...] = dx.astype(BF16)

        @pl.when(pl.program_id(0) == 0)
        def _():
            dg_ref[...] = jnp.zeros_like(dg_ref)
            loss_ref[...] = jnp.zeros_like(loss_ref)

        dg_ref[...] += jnp.sum(dout * nrm, axis=0, keepdims=True)
        part = jnp.sum(jnp.sum(err * err, axis=-1, keepdims=True) * (0.5 / w), axis=0, keepdims=True)
        loss_ref[...] += jnp.broadcast_to(part, loss_ref.shape)

    row = pl.BlockSpec((tr, w), lambda i: (i, 0))
    vec = pl.BlockSpec((1, w), lambda i: (0, 0))
    return pl.pallas_call(
        body, out_shape=(jax.ShapeDtypeStruct((1, LANES), F32), jax.ShapeDtypeStruct((t, w), F32),
                         jax.ShapeDtypeStruct((t, w), BF16), jax.ShapeDtypeStruct((1, w), F32)),
        grid=(t // tr,), in_specs=[row, vec, row],
        out_specs=(pl.BlockSpec((1, LANES), lambda i: (0, 0)), row, row, vec),
        compiler_params=_cparams(("arbitrary",)), name=name)(x, g, tgt)


def _attn_bias():
    slopes = np.asarray(2.0 ** (-8.0 * (np.arange(ATTN_HEADS) + 1) / ATTN_HEADS), np.float32)
    qi = np.arange(QBLOCK)[:, None]
    kj = np.arange(3 * QBLOCK)[None, :]
    rel = np.abs(kj - QBLOCK - qi).astype(np.float32)
    tile = np.where(rel <= WINDOW, -slopes[:, None, None] * rel[None], np.float32(NEG_INF)).astype(np.float32)
    tile = tile.reshape(KV_HEADS, GQ * QBLOCK, 3 * QBLOCK)
    return jnp.asarray(np.swapaxes(tile, 1, 2))


def _attn_scores(k3, q, n, nb, bias):
    s = lax.dot_general(k3, q, NT_DIMS, preferred_element_type=F32) * (HEAD_DIM ** -0.5)
    key = lax.broadcasted_iota(jnp.int32, (3 * QBLOCK, 1), 0)
    inside = (key >= jnp.where(n == 0, QBLOCK, 0)) & (key < jnp.where(n == nb - 1, 2 * QBLOCK, 3 * QBLOCK))
    return jnp.where(inside, s + bias, NEG_INF)


Q_COL, K_COL, V_COL, U_COL = 0, ATTN_WIDTH // LANES, ATTN_WIDTH // LANES + 1, ATTN_WIDTH // LANES + 2


def _key_rows(ref, n, nb):
    prev, nxt = jnp.maximum(n - 1, 0), jnp.minimum(n + 1, nb - 1)
    blk = lambda b: ref[pl.ds(pl.multiple_of(b * QBLOCK, QBLOCK), QBLOCK), :]
    return jnp.concatenate([blk(prev), blk(n), blk(nxt)], axis=0)


def _head_tiles(x, kh, low):
    tiles = []
    for g in range(GQ):
        h = GQ * kh + g
        t128 = x[:, LANES * (h // 2):LANES * (h // 2 + 1)]
        t128 = jnp.where(low if h % 2 == 0 else jnp.logical_not(low), t128, 0.0)
        if h % 2 != kh:
            t128 = pltpu.roll(t128, HEAD_DIM, 1)
        tiles.append(t128)
    return jnp.concatenate(tiles, axis=0)


def _head_merge(per_kh, low):
    out = []
    for j in range(ATTN_HEADS // 2):
        pair = []
        for h in (2 * j, 2 * j + 1):
            kh, g = h // GQ, h % GQ
            t128 = per_kh[kh][g * QBLOCK:(g + 1) * QBLOCK, :]
            if h % 2 != kh:
                t128 = pltpu.roll(t128, HEAD_DIM, 1)
            pair.append(t128)
        out.append(jnp.where(low, pair[0], pair[1]))
    return jnp.concatenate(out, axis=1)


def _attn_fwd_proj(proj, sink_rows, bias, name):
    t = proj.shape[0]
    nb = t // QBLOCK
    rows = GQ * QBLOCK

    def body(q_ref, k_ref, v_ref, sink_ref, bias_ref, o_ref, lse_ref):
        n = pl.program_id(0)
        low = lax.broadcasted_iota(jnp.int32, (QBLOCK, LANES), 1) < HEAD_DIM
        k3 = _key_rows(k_ref, n, nb).astype(BF16)
        v3 = _key_rows(v_ref, n, nb).astype(BF16)
        q = q_ref[...]
        outs = []
        for kh in range(KV_HEADS):
            qs = _head_tiles(q, kh, low).astype(BF16)
            s = _attn_scores(k3, qs, n, nb, bias_ref[kh])
            sink = sink_ref[kh]
            mx = jnp.maximum(jnp.max(s, axis=0, keepdims=True), sink)
            p = jnp.exp(s - mx)
            den = jnp.sum(p, axis=0, keepdims=True) + jnp.exp(sink - mx)
            pn = (p * (1.0 / den)).astype(BF16)
            outs.append(lax.dot_general(pn, v3, TN_DIMS, preferred_element_type=F32))
            lse_ref[0, kh] = mx + jnp.log(den)
        o_ref[...] = _head_merge(outs, low)

    strip = lambda col: pl.BlockSpec((t, LANES), lambda n, col=col: (0, col))
    rowspec = pl.BlockSpec((KV_HEADS, 1, rows), lambda n: (0, 0, 0))
    biasspec = pl.BlockSpec((KV_HEADS, 3 * QBLOCK, rows), lambda n: (0, 0, 0))
    return pl.pallas_call(
        body, out_shape=(jax.ShapeDtypeStruct((t, ATTN_WIDTH), F32), jax.ShapeDtypeStruct((nb, KV_HEADS, 1, rows), F32)),
        grid=(nb,), in_specs=[pl.BlockSpec((QBLOCK, ATTN_WIDTH), lambda n: (n, 0)), strip(K_COL), strip(V_COL),
                              rowspec, biasspec],
        out_specs=(pl.BlockSpec((QBLOCK, ATTN_WIDTH), lambda n: (n, 0)),
                   pl.BlockSpec((1, KV_HEADS, 1, rows), lambda n: (n, 0, 0, 0))),
        compiler_params=_cparams(("parallel",)), name=name)(proj, proj, proj, sink_rows, bias)


def _attn_bwd_proj(proj, sink_rows, bias, o, lse, do, name):
    t = proj.shape[0]
    nb = t // QBLOCK
    rows = GQ * QBLOCK
    scale = HEAD_DIM ** -0.5

    def body(q_ref, k_ref, v_ref, sink_ref, bias_ref, o_ref, lse_ref, do_ref, dq_ref, dk_ref, dv_ref, ds_ref):
        n = pl.program_id(0)

        @pl.when(n == 0)
        def _():
            dk_ref[...] = jnp.zeros_like(dk_ref)
            dv_ref[...] = jnp.zeros_like(dv_ref)
            ds_ref[...] = jnp.zeros_like(ds_ref)

        low = lax.broadcasted_iota(jnp.int32, (QBLOCK, LANES), 1) < HEAD_DIM
        k3 = _key_rows(k_ref, n, nb).astype(BF16)
        v3 = _key_rows(v_ref, n, nb).astype(BF16)
        q, dov = q_ref[...], do_ref[...]
        dod = dov * o_ref[...]
        dqs = []
        dk3 = jnp.zeros((3 * QBLOCK, LANES), F32)
        dv3 = jnp.zeros((3 * QBLOCK, LANES), F32)
        ones = jnp.ones((SUBLANES, LANES), F32)
        for kh in range(KV_HEADS):
            qs = _head_tiles(q, kh, low).astype(BF16)
            dos = _head_tiles(dov, kh, low).astype(BF16)
            delta = lax.dot_general(ones, _head_tiles(dod, kh, low), NT_DIMS, preferred_element_type=F32,
                                    precision=lax.Precision.HIGHEST)[0:1, :]
            lse_kh = lse_ref[0, kh]
            s = _attn_scores(k3, qs, n, nb, bias_ref[kh])
            p = jnp.exp(s - lse_kh)
            dp = lax.dot_general(v3, dos, NT_DIMS, preferred_element_type=F32)
            dsb = (p * (dp - delta)).astype(BF16)
            dqs.append(lax.dot_general(dsb, k3, TN_DIMS, preferred_element_type=F32) * scale)
            dk3 = dk3 + jnp.dot(dsb, qs, preferred_element_type=F32) * scale
            dv3 = dv3 + jnp.dot(p.astype(BF16), dos, preferred_element_type=F32)
            ds_ref[kh] += -jnp.exp(sink_ref[kh] - lse_kh) * delta
        dq_ref[...] = _head_merge(dqs, low)
        prev, nxt = jnp.maximum(n - 1, 0), jnp.minimum(n + 1, nb - 1)
        for j, b in enumerate((prev, n, nxt)):
            blk = pl.ds(pl.multiple_of(b * QBLOCK, QBLOCK), QBLOCK)
            dk_ref[blk, :] += dk3[j * QBLOCK:(j + 1) * QBLOCK, :]
            dv_ref[blk, :] += dv3[j * QBLOCK:(j + 1) * QBLOCK, :]

    strip = lambda col: pl.BlockSpec((t, LANES), lambda n, col=col: (0, col))
    rowspec = pl.BlockSpec((KV_HEADS, 1, rows), lambda n: (0, 0, 0))
    qspec = pl.BlockSpec((QBLOCK, ATTN_WIDTH), lambda n: (n, 0))
    kv_out = pl.BlockSpec((t, LANES), lambda n: (0, 0))
    biasspec = pl.BlockSpec((KV_HEADS, 3 * QBLOCK, rows), lambda n: (0, 0, 0))
    return pl.pallas_call(
        body,
        out_shape=(jax.ShapeDtypeStruct((t, ATTN_WIDTH), F32), jax.ShapeDtypeStruct((t, LANES), F32),
                   jax.ShapeDtypeStruct((t, LANES), F32), jax.ShapeDtypeStruct((KV_HEADS, 1, rows), F32)),
        grid=(nb,),
        in_specs=[qspec, strip(K_COL), strip(V_COL), rowspec, biasspec, qspec,
                  pl.BlockSpec((1, KV_HEADS, 1, rows), lambda n: (n, 0, 0, 0)), qspec],
        out_specs=(qspec, kv_out, kv_out, rowspec),
        compiler_params=_cparams(("arbitrary",)), name=name)(proj, proj, proj, sink_rows, bias, o, lse, do)


def _scan_tables(a_re, a_im, reverse):
    pw = [(a_re, a_im)]
    for _ in range(SUBLANES - 1):
        pr, pi = pw[-1]
        pw.append((pr * a_re - pi * a_im, pr * a_im + pi * a_re))
    rows = np.arange(SUBLANES)
    tabs = []
    for d in (1, 2, 4):
        mask = (rows <= SUBLANES - 1 - d) if reverse else (rows >= d)
        m = jnp.asarray(mask, F32)[:, None]
        tabs += [m * pw[d - 1][0][None, :], m * pw[d - 1][1][None, :]]
    order = (SUBLANES - 1 - rows) if reverse else rows
    tabs += [jnp.stack([pw[j][0] for j in order]), jnp.stack([pw[j][1] for j in order])]
    tab = jnp.stack(tabs)
    return tab.reshape(8, SUBLANES, N_STRIPS, STRIP_ST).transpose(2, 0, 1, 3)


def _scan_pair_chunk(dirs):
    nblk = dirs[0]['xr'].shape[0] // SUBLANES

    @pl.when(pl.program_id(1) == 0)
    def _():
        for d in dirs:
            d['carry'][...] = jnp.zeros_like(d['carry'])

    for d in dirs:
        vb = d['v'][...].astype(BF16)
        d['xr'][...] = jnp.dot(vb, d['mir'][0], preferred_element_type=F32)
        d['xi'][...] = jnp.dot(vb, d['mii'][0], preferred_element_type=F32)
    carries = [(d['carry'][0], d['carry'][1]) for d in dirs]
    for i in range(nblk):
        for k, d in enumerate(dirs):
            rev = d['reverse']
            rows = pl.ds(((nblk - 1 - i) if rev else i) * SUBLANES, SUBLANES)
            cr, ci = carries[k]
            xr, xi = d['xr'][rows, :], d['xi'][rows, :]
            for j, s in enumerate((1, 2, 4)):
                tr_, ti_ = d['tab'][0, 2 * j], d['tab'][0, 2 * j + 1]
                sh = (SUBLANES - s) if rev else s
                sr, si = pltpu.roll(xr, sh, 0), pltpu.roll(xi, sh, 0)
                xr, xi = xr + tr_ * sr - ti_ * si, xi + tr_ * si + ti_ * sr
            pr, pi = d['tab'][0, 6], d['tab'][0, 7]
            xr, xi = xr + pr * cr - pi * ci, xi + pr * ci + pi * cr
            d['xr'][rows, :] = xr
            d['xi'][rows, :] = xi
            edge = 0 if rev else SUBLANES - 1
            carries[k] = (jnp.broadcast_to(xr[edge:edge + 1, :], xr.shape),
                          jnp.broadcast_to(xi[edge:edge + 1, :], xi.shape))
    for k, d in enumerate(dirs):
        d['carry'][0], d['carry'][1] = carries[k]
        d['y'][...] = (jnp.dot(d['xr'][...].astype(BF16), d['mor'][0], preferred_element_type=F32)
                       + jnp.dot(d['xi'][...].astype(BF16), d['moi'][0], preferred_element_type=F32))


def _scan_pair(v, ops_f, ops_b, name):
    t = v.shape[0]
    tc = _tile(t, SCAN_ROWS)
    nc = t // tc

    def body(vf_ref, vb_ref, *refs):
        ops = refs[:10]
        outs = refs[10:16]
        carries = refs[16:18]
        dirs = []
        for k, (v_ref, rev) in enumerate(((vf_ref, False), (vb_ref, True))):
            mir, mii, tab, mor, moi = ops[5 * k:5 * k + 5]
            y, xr, xi = outs[3 * k:3 * k + 3]
            dirs.append(dict(v=v_ref, mir=mir, mii=mii, tab=tab, mor=mor, moi=moi, y=y, xr=xr, xi=xi,
                             carry=carries[k], reverse=rev))
        _scan_pair_chunk(dirs)

    col0 = v.shape[1] // STRIP_IN - N_STRIPS
    fmap = lambda s, c: (c, s)
    bmap = lambda s, c: (nc - 1 - c, s)
    smap3 = lambda s, c: (s, 0, 0)
    m_in = pl.BlockSpec((1, STRIP_IN, STRIP_ST), smap3)
    m_out = pl.BlockSpec((1, STRIP_ST, STRIP_IN), smap3)
    tabspec = pl.BlockSpec((1, 8, SUBLANES, STRIP_ST), lambda s, c: (s, 0, 0, 0))
    opspecs = [m_in, m_in, tabspec, m_out, m_out]
    y_sh = jax.ShapeDtypeStruct((t, SSM_WIDTH), F32)
    x_sh = jax.ShapeDtypeStruct((t, N_STRIPS * STRIP_ST), F32)
    outspecs = lambda m: [pl.BlockSpec((tc, STRIP_IN), m), pl.BlockSpec((tc, STRIP_ST), m),
                          pl.BlockSpec((tc, STRIP_ST), m)]
    res = pl.pallas_call(
        body, out_shape=[y_sh, x_sh, x_sh] * 2, grid=(N_STRIPS, nc),
        in_specs=[pl.BlockSpec((tc, STRIP_IN), lambda s, c: (c, s + col0)),
                  pl.BlockSpec((tc, STRIP_IN), lambda s, c: (nc - 1 - c, s + col0))] + opspecs * 2,
        out_specs=outspecs(fmap) + outspecs(bmap),
        scratch_shapes=[pltpu.VMEM((2, SUBLANES, STRIP_ST), F32)] * 2,
        compiler_params=_cparams(("parallel", "arbitrary")), name=name)(v, v, *ops_f, *ops_b)
    return tuple(res[:3]), tuple(res[3:])


def _scan_adjoint_pair(dy, u, states, adj_ops, name):
    t = dy.shape[0]
    tc = _tile(t, SCAN_ROWS)
    nc = t // tc
    hb = tc // SUBLANES
    n_out = 6

    def body(*refs):
        c = pl.program_id(1)
        dirs = []
        for k in range(2):
            dy_ref, mir, mii, tab, mor, moi, u_ref, xr_ref, xi_ref, hr_ref, hi_ref = refs[11 * k:11 * k + 11]
            outs = refs[22 + n_out * k:22 + n_out * (k + 1)]
            lr_ref, li_ref, carry = refs[22 + 2 * n_out + 3 * k:22 + 2 * n_out + 3 * k + 3]
            dirs.append(dict(v=dy_ref, mir=mir, mii=mii, tab=tab, mor=mor, moi=moi, y=outs[0], xr=lr_ref, xi=li_ref,
                             carry=carry, reverse=(k == 0), u=u_ref, fx=(xr_ref, xi_ref), halo=(hr_ref, hi_ref),
                             acc=outs[1:]))

        @pl.when(c == 0)
        def _():
            for d in dirs:
                for r in d['acc']:
                    r[...] = jnp.zeros_like(r)

        _scan_pair_chunk(dirs)
        for d in dirs:
            fwd_reverse = not d['reverse']
            rc = (nc - 1 - c) if d['reverse'] else c
            dmir_ref, dmii_ref, dmor_ref, dmoi_ref, da_ref = d['acc']
            xrv, xiv, lrv, liv = d['fx'][0][...], d['fx'][1][...], d['xr'][...], d['xi'][...]
            hr_ref, hi_ref = d['halo']
            row = lax.broadcasted_iota(jnp.int32, xrv.shape, 0)
            if fwd_reverse:
                live = (rc < nc - 1).astype(F32)
                edge_r, edge_i = hr_ref[0:1, :] * live, hi_ref[0:1, :] * live
                xpr = jnp.where(row == tc - 1, edge_r, pltpu.roll(xrv, tc - 1, 0))
                xpi = jnp.where(row == tc - 1, edge_i, pltpu.roll(xiv, tc - 1, 0))
            else:
                live = (rc > 0).astype(F32)
                edge_r, edge_i = hr_ref[SUBLANES - 1:SUBLANES, :] * live, hi_ref[SUBLANES - 1:SUBLANES, :] * live
                xpr = jnp.where(row == 0, edge_r, pltpu.roll(xrv, 1, 0))
                xpi = jnp.where(row == 0, edge_i, pltpu.roll(xiv, 1, 0))
            da_ref[0, 0:1, :] += jnp.sum(xpr * lrv + xpi * liv, axis=0, keepdims=True)
            da_ref[0, 1:2, :] += jnp.sum(xpr * liv - xpi * lrv, axis=0, keepdims=True)
            ub, dyb = d['u'][...].astype(BF16), d['v'][...].astype(BF16)
            dmir_ref[0] += lax.dot_general(ub, lrv.astype(BF16), TN_DIMS, preferred_element_type=F32)
            dmii_ref[0] += lax.dot_general(ub, liv.astype(BF16), TN_DIMS, preferred_element_type=F32)
            dmor_ref[0] += lax.dot_general(xrv.astype(BF16), dyb, TN_DIMS, preferred_element_type=F32)
            dmoi_ref[0] += lax.dot_general(xiv.astype(BF16), dyb, TN_DIMS, preferred_element_type=F32)

    col0 = u.shape[1] // STRIP_IN - N_STRIPS
    smap3 = lambda s, c: (s, 0, 0)
    m_in = pl.BlockSpec((1, STRIP_IN, STRIP_ST), smap3)
    m_out = pl.BlockSpec((1, STRIP_ST, STRIP_IN), smap3)
    tabspec = pl.BlockSpec((1, 8, SUBLANES, STRIP_ST), lambda s, c: (s, 0, 0, 0))
    in_specs, out_specs, args = [], [], []
    for k in range(2):
        reverse = k == 0
        rowblk = (lambda c: nc - 1 - c) if reverse else (lambda c: c)
        tmap = lambda s, c, rowblk=rowblk: (rowblk(c), s)
        umap = lambda s, c, rowblk=rowblk: (rowblk(c), s + col0)
        if not reverse:
            hmap = lambda s, c, rowblk=rowblk: (jnp.minimum((rowblk(c) + 1) * hb, t // SUBLANES - 1), s)
        else:
            hmap = lambda s, c, rowblk=rowblk: (jnp.maximum(rowblk(c) * hb - 1, 0), s)
        narrow = pl.BlockSpec((tc, STRIP_IN), tmap)
        wide = pl.BlockSpec((tc, STRIP_ST), tmap)
        halo = pl.BlockSpec((SUBLANES, STRIP_ST), hmap)
        in_specs += [narrow, m_in, m_in, tabspec, m_out, m_out, pl.BlockSpec((tc, STRIP_IN), umap), wide, wide,
                     halo, halo]
        out_specs += [narrow, m_in, m_in, m_out, m_out, pl.BlockSpec((1, SUBLANES, STRIP_ST), smap3)]
        xr, xi = states[k]
        args += [dy, *adj_ops[k], u, xr, xi, xr, xi]
    out_shape = [jax.ShapeDtypeStruct((t, SSM_WIDTH), F32),
                 jax.ShapeDtypeStruct((N_STRIPS, STRIP_IN, STRIP_ST), F32),
                 jax.ShapeDtypeStruct((N_STRIPS, STRIP_IN, STRIP_ST), F32),
                 jax.ShapeDtypeStruct((N_STRIPS, STRIP_ST, STRIP_IN), F32),
                 jax.ShapeDtypeStruct((N_STRIPS, STRIP_ST, STRIP_IN), F32),
                 jax.ShapeDtypeStruct((N_STRIPS, SUBLANES, STRIP_ST), F32)] * 2
    res = pl.pallas_call(
        body, out_shape=out_shape, grid=(N_STRIPS, nc), in_specs=in_specs, out_specs=out_specs,
        scratch_shapes=[pltpu.VMEM((tc, STRIP_ST), F32), pltpu.VMEM((tc, STRIP_ST), F32),
                        pltpu.VMEM((2, SUBLANES, STRIP_ST), F32)] * 2,
        compiler_params=_cparams(("parallel", "arbitrary")), name=name)(*args)
    return tuple(res[:n_out]), tuple(res[n_out:])


def _ssm_prep(lam_re, lam_im, log_dt, bt_re, bt_im, c_re, c_im):
    lr = jnp.minimum(lam_re, LAMBDA_RE_MAX)
    li = lam_im
    dt = jnp.exp(log_dt)[:, None]
    mag = jnp.exp(lr * dt)
    a_re = mag * jnp.cos(li * dt)
    a_im = mag * jnp.sin(li * dt)
    den = lr * lr + li * li
    coef_re = ((a_re - 1.0) * lr + a_im * li) / den
    coef_im = (a_im * lr - (a_re - 1.0) * li) / den
    bb_re = coef_re[:, None, :] * bt_re - coef_im[:, None, :] * bt_im
    bb_im = coef_re[:, None, :] * bt_im + coef_im[:, None, :] * bt_re
    eye = jnp.eye(SSM_GROUPS // N_STRIPS, dtype=F32)

    def strips(m):
        g, a, b = m.shape
        m4 = m.reshape(N_STRIPS, g // N_STRIPS, a, b)
        return jnp.einsum('sgab,gk->sgakb', m4, eye).reshape(N_STRIPS, g // N_STRIPS * a, g // N_STRIPS * b)

    mi_re = strips(bb_re)
    mi_im = strips(bb_im)
    mo_re = strips(jnp.swapaxes(c_re, 1, 2))
    mo_im = strips(-jnp.swapaxes(c_im, 1, 2))
    return a_re.reshape(-1), a_im.reshape(-1), mi_re, mi_im, mo_re, mo_im


def _gelu(x):
    c = math.sqrt(2.0 / math.pi)
    return 0.5 * x * (1.0 + jnp.tanh(c * (x + 0.044715 * x * x * x)))


def _gelu_grad(x):
    c = math.sqrt(2.0 / math.pi)
    th = jnp.tanh(c * (x + 0.044715 * x * x * x))
    return 0.5 * (1.0 + th) + 0.5 * x * (1.0 - th * th) * c * (1.0 + 3.0 * 0.044715 * x * x)


def _last_cols_specs(u, w, tr):
    half = w // 2
    first = (u.shape[1] - w) // half
    assert first * half == u.shape[1] - w
    return [pl.BlockSpec((tr, half), lambda i, k=k: (i, first + k)) for k in range(2)]


def _ssm_post_fwd(u, yf, yb, d, wglu, bglu, name):
    t, w = yf.shape
    tr = _row_tile(t)

    def body(ua_ref, ub_ref, yf_ref, yb_ref, d_ref, w_ref, b_ref, s_ref, y0_ref, z_ref):
        uv = jnp.concatenate([ua_ref[...], ub_ref[...]], axis=1)
        y0 = d_ref[...] * uv + yf_ref[...] + yb_ref[...]
        yg = _gelu(y0)
        z = jnp.dot(yg.astype(BF16), w_ref[...], preferred_element_type=F32) + b_ref[...]
        s_ref[...] = yg * _sigmoid(z)
        y0_ref[...] = y0
        z_ref[...] = z

    row = pl.BlockSpec((tr, w), lambda i: (i, 0))
    vec = pl.BlockSpec((1, w), lambda i: (0, 0))
    mat = pl.BlockSpec((w, w), lambda i: (0, 0))
    sh = jax.ShapeDtypeStruct((t, w), F32)
    return pl.pallas_call(body, out_shape=(sh, sh, sh), grid=(t // tr,),
                          in_specs=[*_last_cols_specs(u, w, tr), row, row, vec, mat, vec], out_specs=(row, row, row),
                          compiler_params=_cparams(("parallel",)), name=name)(u, u, yf, yb, d, wglu, bglu)


def _ssm_post_bwd(ds, y0, z, u, d, wglu, name):
    t, w = ds.shape
    tr = _row_tile(t)

    def body(ds_ref, y0_ref, z_ref, ua_ref, ub_ref, d_ref, w_ref, dy0_ref, dw_ref, db_ref, dd_ref):
        @pl.when(pl.program_id(0) == 0)
        def _():
            dw_ref[...] = jnp.zeros_like(dw_ref)
            db_ref[...] = jnp.zeros_like(db_ref)
            dd_ref[...] = jnp.zeros_like(dd_ref)

        y0 = y0_ref[...]
        yg = _gelu(y0)
        sg = _sigmoid(z_ref[...])
        dsv = ds_ref[...]
        dz = dsv * yg * sg * (1.0 - sg)
        dzb = dz.astype(BF16)
        dyg = dsv * sg + lax.dot_general(dzb, w_ref[...], (((1,), (1,)), ((), ())), preferred_element_type=F32)
        dy0 = dyg * _gelu_grad(y0)
        dy0_ref[...] = dy0
        dw_ref[...] += lax.dot_general(yg.astype(BF16), dzb, (((0,), (0,)), ((), ())), preferred_element_type=F32)
        db_ref[...] += jnp.sum(dz, axis=0, keepdims=True)
        uv = jnp.concatenate([ua_ref[...], ub_ref[...]], axis=1)
        dd_ref[...] += jnp.sum(dy0 * uv, axis=0, keepdims=True)

    row = pl.BlockSpec((tr, w), lambda i: (i, 0))
    vec = pl.BlockSpec((1, w), lambda i: (0, 0))
    mat = pl.BlockSpec((w, w), lambda i: (0, 0))
    return pl.pallas_call(
        body, out_shape=(jax.ShapeDtypeStruct((t, w), F32), jax.ShapeDtypeStruct((w, w), F32),
                         jax.ShapeDtypeStruct((1, w), F32), jax.ShapeDtypeStruct((1, w), F32)),
        grid=(t // tr,), in_specs=[row, row, row, *_last_cols_specs(u, w, tr), vec, mat],
        out_specs=(row, mat, vec, vec),
        compiler_params=_cparams(("arbitrary",)), name=name)(ds, y0, z, u, u, d, wglu)


def _du_combine(dy0, d, du_f, du_b, name):
    t, w = dy0.shape
    tr = _row_tile(t)

    def body(dy_ref, d_ref, a_ref, b_ref, o_ref):
        o_ref[...] = d_ref[...] * dy_ref[...] + a_ref[...] + b_ref[...]

    row = pl.BlockSpec((tr, w), lambda i: (i, 0))
    vec = pl.BlockSpec((1, w), lambda i: (0, 0))
    return pl.pallas_call(body, out_shape=jax.ShapeDtypeStruct((t, w), F32), grid=(t // tr,),
                          in_specs=[row, vec, row, row], out_specs=row, compiler_params=_cparams(("parallel",)),
                          name=name)(dy0, d, du_f, du_b)


def _ffn_fwd(x, g, wg, wu, wd, tag):
    xo, h, gate, up = _ffn_fwd_call(x, g, wg, wu, wd, f"{tag}_fwd")
    return xo, (h, gate, up)


def _ffn_bwd(dxo, dxo_b, x, g, wg, wu, wd, saved, tag):
    h, gate, up = saved
    dx, dx_b, dg, dgate, dup, act = _ffn_bwd_x_call(dxo, dxo_b, x, g, gate, up, wg, wu, wd, f"{tag}_bwd_x")
    dwg, dwu, dwd = _ffn_bwd_w_call(h, dxo_b, dgate, dup, act, f"{tag}_bwd_w")
    return dx, dx_b, dg, dwg, dwu, dwd


def _local_step(x, tgt, w, get_weights, put_grads, reduce_wide):
    t = x.shape[0]
    row = lambda a: a.reshape(1, -1)
    grads = {}

    w = dict(w)

    ssm_names = ['ssm_lambda_re', 'ssm_lambda_im', 'ssm_log_dt', 'ssm_b_re', 'ssm_b_im', 'ssm_c_re', 'ssm_c_im']
    tr3 = lambda m: jnp.swapaxes(m, 1, 2)
    fwd_ops, adj_ops, vjps = [], [], []
    for direction in range(2):
        rev = direction == 1
        prep, vjp = jax.vjp(_ssm_prep, *[w[n][direction] for n in ssm_names])
        a_re, a_im = prep[0], prep[1]
        mi_re, mi_im, mo_re, mo_im = (m.astype(BF16) for m in prep[2:])
        fwd_ops.append((mi_re, mi_im, _scan_tables(a_re, a_im, rev), mo_re, mo_im))
        adj_ops.append((tr3(mo_re), tr3(mo_im), _scan_tables(a_re, -a_im, not rev), tr3(mi_re), tr3(mi_im)))
        vjps.append(vjp)
    sink_rows = jnp.repeat(w['attn_sinks'].reshape(KV_HEADS, GQ), QBLOCK, axis=1)[:, None, :]
    bias = _attn_bias()
    prepared = sum(jnp.sum(op[:1, :1].astype(F32)) for ops in fwd_ops + adj_ops for op in ops) + sink_rows[0, 0, 0]

    w.update(get_weights('ffn1', prepared.reshape(1, 1)))
    x1, ffn1_saved = _ffn_fwd(x, w['norm_ffn1'], w['ffn1_w_gate'], w['ffn1_w_up'], w['ffn1_w_down'], "ffn1")
    w.update(get_weights('mix', x1))

    proj, h2 = _norm_mm([x1], [w['norm_mix']], w['w_in'], tb=True, res=None, name="in_proj")
    u = proj

    attn, lse = _attn_fwd_proj(proj, sink_rows, bias, "attn_fwd")

    (y_f, *states_f), (y_b, *states_b) = _scan_pair(u, fwd_ops[0], fwd_ops[1], "s5_fwd")
    ys, states = [y_f, y_b], [states_f, states_b]
    d_row = row(w['ssm_d'])
    s, y0, z = _ssm_post_fwd(u, ys[0], ys[1], d_row, w['ssm_glu_w'], row(w['ssm_glu_b']), "ssm_post")

    x2, mixed = _norm_mm([attn, s], [row(w['attn_out_norm']), row(w['ssm_out_norm'])], w['w_out'], tb=False,
                         res=x1, name="out_proj")

    w.update(get_weights('ffn2', x2))
    x3, ffn2_saved = _ffn_fwd(x2, w['norm_ffn2'], w['ffn2_w_gate'], w['ffn2_w_up'], w['ffn2_w_down'], "ffn2")

    loss, dx3, dx3_b, dgf = _loss_head(x3, row(w['final_norm']), tgt, "loss_head")
    grads['final_norm'] = dgf.reshape(w['final_norm'].shape)

    dx2, dx2_b, dg, dwg, dwu, dwd = _ffn_bwd(dx3, dx3_b, x2, w['norm_ffn2'], w['ffn2_w_gate'], w['ffn2_w_up'],
                                             w['ffn2_w_down'], ffn2_saved, "ffn2")
    grads['norm_ffn2'] = dg
    sent = put_grads('ffn2', dict(ffn2_w_gate=dwg, ffn2_w_up=dwu, ffn2_w_down=dwd))

    (dattn, _, dga), (ds, _, dgs) = _mm_rms_bwd(
        dx2_b, w['w_out'], tb=True, xs=[attn, s], gs=[row(w['attn_out_norm']), row(w['ssm_out_norm'])],
        dres=None, after=sent, name="out_proj_dx")
    dw_out = _mm(mixed, dx2_b, ta=True, out_dtype=BF16, name="out_proj_dw")[0]
    grads.update(attn_out_norm=dga, ssm_out_norm=dgs)

    dy0, dwglu, dbglu, dd = _ssm_post_bwd(ds, y0, z, u, d_row, w['ssm_glu_w'], "ssm_post_bwd")
    grads['ssm_glu_b'] = dbglu
    grads['ssm_d'] = dd.reshape(w['ssm_d'].shape)
    dparams, du_dirs = [], []
    for direction, res in enumerate(_scan_adjoint_pair(dy0, u, states, adj_ops, "s5_adj")):
        du_dir, dmir, dmii, dmor, dmoi, da = res
        du_dirs.append(du_dir)
        da_re = da[:, 0, :].reshape(-1)
        da_im = da[:, 1, :].reshape(-1)
        dparams.append(vjps[direction]((da_re, da_im, dmir, dmii, dmor, dmoi)))
    du = _du_combine(dy0, d_row, du_dirs[0], du_dirs[1], "ssm_du")
    for i, n in enumerate(ssm_names):
        grads[n] = jnp.stack([dparams[0][i], dparams[1][i]])
    wide_sum = reduce_wide(grads)

    dq, dk, dv, dsink = _attn_bwd_proj(proj, sink_rows, bias, attn, lse, dattn, "attn_bwd")
    grads['attn_sinks'] = jnp.sum(dsink.reshape(ATTN_HEADS, QBLOCK), axis=-1).reshape(w['attn_sinks'].shape)
    dproj = jnp.concatenate([dq, dk, dv, du], axis=-1).astype(BF16)

    dw_in = _mm(dproj, h2, ta=True, out_dtype=BF16, after=wide_sum, name="in_proj_dw")[0]
    sent = put_grads('mix', dict(w_in=dw_in, ssm_glu_w=dwglu, w_out=dw_out))
    ((dx1, dx1_b, dgm),) = _mm_rms_bwd(dproj, w['w_in'], tb=False, xs=[x1], gs=[w['norm_mix']], dres=dx2,
                                       after=sent, name="in_proj_dx")
    grads['norm_mix'] = dgm

    dx0, _, dg, dwg, dwu, dwd = _ffn_bwd(dx1, dx1_b, x, w['norm_ffn1'], w['ffn1_w_gate'], w['ffn1_w_up'],
                                         w['ffn1_w_down'], ffn1_saved, "ffn1")
    grads['norm_ffn1'] = dg
    put_grads('ffn1', dict(ffn1_w_gate=dwg, ffn1_w_up=dwu, ffn1_w_down=dwd))
    return loss, dx0, grads, wide_sum


HBM_SPEC = pl.BlockSpec(memory_space=pl.ANY)


def _chip_peers(x, y):
    return [(1 - x, y), (x, 1 - y), (1 - x, 1 - y)]


HBM_ONLY = pl.BlockSpec(memory_space=pltpu.HBM)
SEM_SPEC = pl.BlockSpec(memory_space=pltpu.SEMAPHORE)
EFFECT = pltpu.SideEffectType.DATAFLOW_SIDE_EFFECTING


def _place_own(srcs, slot, name):
    na = len(srcs)
    r, c = srcs[0].shape
    tr = r // 2

    def body(slot_ref, *refs):
        for a in range(na):
            refs[na + a][0] = refs[a][...]

    return pl.pallas_call(
        body, out_shape=[jax.ShapeDtypeStruct((N_CHIPS, r, c), s.dtype) for s in srcs],
        grid_spec=pltpu.PrefetchScalarGridSpec(
            num_scalar_prefetch=1, grid=(2,), in_specs=[pl.BlockSpec((tr, c), lambda i, s: (i, 0))] * na,
            out_specs=[pl.BlockSpec((1, tr, c), lambda i, s: (s[0], i, 0))] * na),
        compiler_params=_cparams(("parallel",)), name=name)(slot, *srcs)


def _chip_copies(srcs, lands, send_sems, recv_sems, scatter, landed):
    x, y, c = lax.axis_index("x"), lax.axis_index("y"), lax.axis_index("c")
    me = 2 * x + y
    out = []
    for i in range(len(srcs)):
        for j, (px, py) in enumerate(_chip_peers(x, y)):
            p = 2 * px + py
            slot = p if landed else me
            if scatter:
                src, dst = srcs[i].at[p], lands[i].at[slot]
            else:
                rows = _core_half(srcs[i].shape[0], c)
                src, dst = srcs[i].at[rows], lands[i].at[slot, rows]
            out.append(pltpu.make_async_remote_copy(src, dst, send_sems.at[3 * i + j], recv_sems.at[3 * i + j],
                                                    device_id=(px, py, c), device_id_type=MESH))
    return out


def _core_half(nrows, c):
    half = nrows // 2
    return pl.ds(pl.multiple_of(c * half, 16), half)


def _sibling_forward(lands, name):
    n = len(lands)

    def body(*refs):
        bufs = refs[n:2 * n]
        send_sems, recv_sems = refs[2 * n:]
        x, y, c = lax.axis_index("x"), lax.axis_index("y"), lax.axis_index("c")
        mine = [_core_half(b.shape[1], c) for b in bufs]
        theirs = [_core_half(b.shape[1], 1 - c) for b in bufs]
        chips = [2 * px + py for px, py in _chip_peers(x, y)]
        cps = [pltpu.make_async_remote_copy(bufs[i].at[p, mine[i]], bufs[i].at[p, mine[i]], send_sems.at[3 * i + j],
                                            recv_sems.at[3 * i + j], device_id=(x, y, 1 - c), device_id_type=MESH)
               for i in range(n) for j, p in enumerate(chips)]
        for cp in cps:
            cp.start()
        for i in range(n):
            for j, p in enumerate(chips):
                pltpu.make_async_remote_copy(bufs[i].at[p, mine[i]], bufs[i].at[p, theirs[i]], send_sems.at[3 * i + j],
                                             recv_sems.at[3 * i + j], device_id=(x, y, 1 - c),
                                             device_id_type=MESH).wait()

    return pl.pallas_call(
        body, out_shape=[jax.ShapeDtypeStruct(a.shape, a.dtype) for a in lands],
        in_specs=[HBM_SPEC] * n, out_specs=[HBM_SPEC] * n, input_output_aliases={k: k for k in range(n)},
        scratch_shapes=[pltpu.SemaphoreType.DMA((3 * n,)), pltpu.SemaphoreType.DMA((3 * n,))],
        name=name)(*lands)


def _exchange_start(groups, scatter, name, after=None):
    sizes = [len(srcs) for srcs, _ in groups]
    flat_src = [a for srcs, _ in groups for a in srcs]
    flat_land = [a for _, lands in groups for a in lands]
    n = len(flat_src)
    ng = len(groups)

    def body(*refs):
        src_refs, land_refs = refs[:n], refs[n:2 * n]
        n_in = 2 * n + (after is not None)
        sems = refs[n_in:n_in + 2 * ng]
        token_ref = refs[-1]
        off = 0
        for gi, sz in enumerate(sizes):
            for cp in _chip_copies(src_refs[off:off + sz], land_refs[off:off + sz], sems[2 * gi], sems[2 * gi + 1],
                                   scatter, landed=False):
                cp.start()
            off += sz
        token_ref[...] = jnp.zeros_like(token_ref)

    sem_shapes = []
    for sz in sizes:
        sem_shapes += [pltpu.SemaphoreType.DMA((3 * sz,)), pltpu.SemaphoreType.DMA((3 * sz,))]
    hbm = lambda a: pltpu.HBM(a.shape, a.dtype)
    res = pl.pallas_call(
        body, name=name,
        out_shape=(tuple(sem_shapes) + tuple(hbm(a) for a in flat_src) + tuple(hbm(a) for a in flat_land)
                   + (jax.ShapeDtypeStruct((SUBLANES, LANES), F32),)),
        in_specs=[HBM_ONLY] * (2 * n) + [HBM_SPEC] * (after is not None),
        out_specs=tuple([SEM_SPEC] * (2 * ng) + [HBM_ONLY] * (2 * n) + [pl.BlockSpec(memory_space=pltpu.VMEM)]),
        input_output_aliases={k: 2 * ng + k for k in range(2 * n)},
        compiler_params=pltpu.CompilerParams(has_side_effects=EFFECT),
    )(*[pltpu.with_memory_space_constraint(a, pltpu.HBM) for a in flat_src + flat_land],
      *([after] if after is not None else []))
    sems, thru_src, thru_land = res[:2 * ng], res[2 * ng:2 * ng + n], res[2 * ng + n:2 * ng + 2 * n]
    out, off = [], 0
    for gi, sz in enumerate(sizes):
        out.append((sems[2 * gi], sems[2 * gi + 1], list(thru_src[off:off + sz]), list(thru_land[off:off + sz])))
        off += sz
    return out, res[-1]


def _exchange_wait(started, after, scatter, name):
    send_sems, recv_sems, srcs, lands = started
    n = len(srcs)

    def body(*refs):
        src_refs, land_refs = refs[:n], refs[n:2 * n]
        send_ref, recv_ref = refs[2 * n], refs[2 * n + 1]
        for cp in _chip_copies(src_refs, land_refs, send_ref, recv_ref, scatter, landed=True):
            cp.wait_send()
            cp.wait_recv()

    hbm = lambda a: pltpu.HBM(a.shape, a.dtype)
    res = pl.pallas_call(
        body, name=name, out_shape=tuple(hbm(a) for a in srcs) + tuple(hbm(a) for a in lands),
        in_specs=[HBM_ONLY] * (2 * n) + [SEM_SPEC, SEM_SPEC, HBM_SPEC], out_specs=tuple([HBM_ONLY] * (2 * n)),
        input_output_aliases={k: k for k in range(2 * n)},
        compiler_params=pltpu.CompilerParams(has_side_effects=EFFECT),
    )(*srcs, *lands, send_sems, recv_sems, after)
    return list(res[:n]), list(res[n:])


def _half_swap(parts, name):
    n = len(parts)

    def body(*refs):
        ins, outs = refs[:n], refs[n:2 * n]
        send_sems, recv_sems = refs[2 * n:]
        x, y, c = lax.axis_index("x"), lax.axis_index("y"), lax.axis_index("c")
        cps = [pltpu.make_async_remote_copy(ins[i].at[k, _core_half(ins[i].shape[1], 1 - c)], outs[i].at[k],
                                            send_sems.at[N_CHIPS * i + k], recv_sems.at[N_CHIPS * i + k],
                                            device_id=(x, y, 1 - c), device_id_type=MESH)
               for i in range(n) for k in range(N_CHIPS)]
        for cp in cps:
            cp.start()
        for cp in cps:
            cp.wait()

    return pl.pallas_call(
        body, out_shape=[jax.ShapeDtypeStruct((N_CHIPS, p.shape[1] // 2, p.shape[2]), p.dtype) for p in parts],
        in_specs=[HBM_SPEC] * n, out_specs=[HBM_SPEC] * n,
        scratch_shapes=[pltpu.SemaphoreType.DMA((N_CHIPS * n,)), pltpu.SemaphoreType.DMA((N_CHIPS * n,))],
        name=name)(*parts)


def _half_add(parts, sib, slots, name):
    na = len(parts)
    _, r, c = parts[0].shape
    hr = r // 2
    tr = _row_tile(hr, 512)
    nt = hr // tr

    def body(slot_ref, *refs):
        for a in range(na):
            refs[2 * na + a][...] = (refs[2 * a][...].astype(F32) + refs[2 * a + 1][...].astype(F32)).astype(BF16)

    mine = pl.BlockSpec((1, tr, c), lambda k, i, s: (k, i + s[4] * nt, 0))
    half = pl.BlockSpec((1, tr, c), lambda k, i, s: (k, i, 0))
    args = [a for p, sb in zip(parts, sib) for a in (p, sb)]
    return pl.pallas_call(
        body, out_shape=[jax.ShapeDtypeStruct((N_CHIPS, hr, c), BF16)] * na,
        grid_spec=pltpu.PrefetchScalarGridSpec(
            num_scalar_prefetch=1, grid=(N_CHIPS, nt), in_specs=[mine, half] * na, out_specs=[half] * na),
        compiler_params=_cparams(("parallel", "parallel")), name=name)(slots, *args)


def _half_forward(arrs, name):
    n = len(arrs)

    def body(*refs):
        bufs = refs[n:2 * n]
        send_sems, recv_sems = refs[2 * n:]
        x, y, c = lax.axis_index("x"), lax.axis_index("y"), lax.axis_index("c")
        cps = [pltpu.make_async_remote_copy(b.at[_core_half(b.shape[0], c)], b.at[_core_half(b.shape[0], c)],
                                            send_sems.at[i], recv_sems.at[i], device_id=(x, y, 1 - c),
                                            device_id_type=MESH) for i, b in enumerate(bufs)]
        for cp in cps:
            cp.start()
        for i, b in enumerate(bufs):
            pltpu.make_async_remote_copy(b.at[_core_half(b.shape[0], c)], b.at[_core_half(b.shape[0], 1 - c)],
                                         send_sems.at[i], recv_sems.at[i], device_id=(x, y, 1 - c),
                                         device_id_type=MESH).wait()

    return pl.pallas_call(
        body, out_shape=[jax.ShapeDtypeStruct(a.shape, a.dtype) for a in arrs],
        in_specs=[HBM_SPEC] * n, out_specs=[HBM_SPEC] * n, input_output_aliases={k: k for k in range(n)},
        scratch_shapes=[pltpu.SemaphoreType.DMA((n,)), pltpu.SemaphoreType.DMA((n,))],
        name=name)(*arrs)


def _small_exchange(smalls, name):
    nsm = len(smalls)
    rels = [(fx, fy, fc) for fx in (0, 1) for fy in (0, 1) for fc in (0, 1)][1:]

    def body(*refs):
        sins, souts = refs[:nsm], refs[nsm:2 * nsm]
        ssend, srecv, slocal = refs[2 * nsm:]
        x, y, c = lax.axis_index("x"), lax.axis_index("y"), lax.axis_index("c")
        lin = 4 * x + 2 * y + c
        local = [pltpu.make_async_copy(sins[i], souts[i].at[lin], slocal.at[i]) for i in range(nsm)]
        for cp in local:
            cp.start()
        for i in range(nsm):
            for j, (fx, fy, fc) in enumerate(rels):
                pltpu.make_async_remote_copy(sins[i], souts[i].at[lin], ssend.at[i, j], srecv.at[i, j],
                                             device_id=(x ^ fx, y ^ fy, c ^ fc), device_id_type=MESH).start()
        for i in range(nsm):
            for j, (fx, fy, fc) in enumerate(rels):
                src = 4 * (x ^ fx) + 2 * (y ^ fy) + (c ^ fc)
                pltpu.make_async_remote_copy(sins[i], souts[i].at[src], ssend.at[i, j], srecv.at[i, j],
                                             device_id=(x ^ fx, y ^ fy, c ^ fc), device_id_type=MESH).wait()
        for cp in local:
            cp.wait()

    return pl.pallas_call(
        body, out_shape=[jax.ShapeDtypeStruct((N_DEV,) + s.shape, s.dtype) for s in smalls],
        in_specs=[HBM_SPEC] * nsm, out_specs=[HBM_SPEC] * nsm,
        scratch_shapes=[pltpu.SemaphoreType.DMA((nsm, 7)), pltpu.SemaphoreType.DMA((nsm, 7)),
                        pltpu.SemaphoreType.DMA((nsm,))],
        name=name)(*smalls)


def _sum_parts(parts, recv, slots, name):
    na = len(parts)
    _, r, c = parts[0].shape
    tr = _row_tile(r, 192)

    def body(slot_ref, *refs):
        for a in range(na):
            own_ref, r0_ref, r1_ref, r2_ref = refs[4 * a:4 * a + 4]
            refs[4 * na + a][...] = ((own_ref[0].astype(F32) + r0_ref[0].astype(F32))
                                     + (r1_ref[0].astype(F32) + r2_ref[0].astype(F32)))

    blk = lambda k: pl.BlockSpec((1, tr, c), lambda i, s, k=k: (s[k], i, 0))
    out_blk = pl.BlockSpec((tr, c), lambda i, s: (i + s[4] * (r // tr), 0))
    args = [a for p, rv in zip(parts, recv) for a in (p, rv, rv, rv)]
    return pl.pallas_call(
        body, out_shape=[jax.ShapeDtypeStruct((2 * r, c), F32)] * na,
        grid_spec=pltpu.PrefetchScalarGridSpec(
            num_scalar_prefetch=1, grid=(r // tr,), in_specs=[blk(0), blk(1), blk(2), blk(3)] * na,
            out_specs=[out_blk] * na),
        compiler_params=_cparams(("parallel",)), name=name)(slots, *args)


def _small_allreduce(packed, name):
    rows = packed.shape[0]
    pr = rows // N_DEV
    rels = [(fx, fy, fc) for fx in (0, 1) for fy in (0, 1) for fc in (0, 1)][1:]

    def body(in_ref, out_ref, recv_ref, send1, recv1, send2, recv2):
        x, y, c = lax.axis_index("x"), lax.axis_index("y"), lax.axis_index("c")
        lin = 4 * x + 2 * y + c
        piece = lambda ref, k: ref.at[pl.ds(pl.multiple_of(k * pr, pr), pr), :]
        peers = [((x ^ fx, y ^ fy, c ^ fc), 4 * (x ^ fx) + 2 * (y ^ fy) + (c ^ fc)) for fx, fy, fc in rels]
        for j, (dev, plin) in enumerate(peers):
            pltpu.make_async_remote_copy(piece(in_ref, plin), recv_ref.at[lin], send1.at[j], recv1.at[j],
                                         device_id=dev, device_id_type=MESH).start()
        recv_ref[lin] = piece(in_ref, lin)[...]
        for j, (dev, plin) in enumerate(peers):
            pltpu.make_async_remote_copy(piece(in_ref, plin), recv_ref.at[plin], send1.at[j], recv1.at[j],
                                         device_id=dev, device_id_type=MESH).wait()
        acc = recv_ref[0]
        for k in range(1, N_DEV):
            acc = acc + recv_ref[k]
        piece(out_ref, lin)[...] = acc
        for j, (dev, plin) in enumerate(peers):
            pltpu.make_async_remote_copy(piece(out_ref, lin), piece(out_ref, lin), send2.at[j], recv2.at[j],
                                         device_id=dev, device_id_type=MESH).start()
        for j, (dev, plin) in enumerate(peers):
            pltpu.make_async_remote_copy(piece(out_ref, lin), piece(out_ref, plin), send2.at[j], recv2.at[j],
                                         device_id=dev, device_id_type=MESH).wait()

    vm = pl.BlockSpec(memory_space=pltpu.VMEM)
    return pl.pallas_call(
        body, out_shape=jax.ShapeDtypeStruct(packed.shape, F32), in_specs=[vm], out_specs=vm,
        scratch_shapes=[pltpu.VMEM((N_DEV, pr, LANES), F32)] + [pltpu.SemaphoreType.DMA((7,))] * 4,
        compiler_params=pltpu.CompilerParams(vmem_limit_bytes=VMEM_LIMIT), name=name)(packed)


def _adamw_math(w, m, v, g):
    nm = ADAM_B1 * m + (1.0 - ADAM_B1) * g
    nv = ADAM_B2 * v + (1.0 - ADAM_B2) * (g * g)
    m_hat = nm * (1.0 / (1.0 - ADAM_B1 ** ADAM_STEP))
    v_hat = nv * (1.0 / (1.0 - ADAM_B2 ** ADAM_STEP))
    return -ADAM_LR * (m_hat / (jnp.sqrt(v_hat) + ADAM_EPS) + ADAM_WD * w), nm, nv


def _adamw(ws, ms, vs, gs, name):
    na = len(ws)
    r, c = ws[0].shape
    tr = _row_tile(r)

    def body(*refs):
        for a in range(na):
            w_ref, m_ref, v_ref, g_ref = refs[4 * a:4 * a + 4]
            d_ref, nm_ref, nv_ref = refs[4 * na + 3 * a:4 * na + 3 * a + 3]
            d_ref[...], nm_ref[...], nv_ref[...] = _adamw_math(w_ref[...], m_ref[...], v_ref[...], g_ref[...])

    blk = pl.BlockSpec((tr, c), lambda i: (i, 0))
    sh = jax.ShapeDtypeStruct((r, c), F32)
    args = [a for group in zip(ws, ms, vs, gs) for a in group]
    res = pl.pallas_call(body, out_shape=[sh] * (3 * na), grid=(r // tr,), in_specs=[blk] * (4 * na),
                         out_specs=[blk] * (3 * na), compiler_params=_cparams(("parallel",)), name=name)(*args)
    return [tuple(res[3 * a:3 * a + 3]) for a in range(na)]


def _adamw_small(ws, ms, vs, alls, split, name):
    n = len(ws)
    lead = split if split is not None else ()
    nl = len(lead)
    nslots = alls[0].shape[0]

    def blocks(shape):
        if split is None:
            return tuple(shape), (lambda *g: (0,) * len(shape))
        blk = (shape[0], shape[1] // lead[0], shape[2] // lead[1]) + tuple(shape[3:])
        return blk, (lambda *g: (0, g[0], g[1]) + (0,) * (len(shape) - 3))

    def body(*refs):
        w_refs, m_refs, v_refs, a_refs = (refs[k * n:(k + 1) * n] for k in range(4))
        g_refs, d_refs, nm_refs, nv_refs = (refs[(4 + k) * n:(5 + k) * n] for k in range(4))
        k = pl.program_id(nl)
        for i in range(n):
            @pl.when(k == 0)
            def _(i=i):
                g_refs[i][...] = a_refs[i][0]

            @pl.when(k > 0)
            def _(i=i):
                g_refs[i][...] += a_refs[i][0]

            @pl.when(k == nslots - 1)
            def _(i=i):
                d_refs[i][...], nm_refs[i][...], nv_refs[i][...] = _adamw_math(
                    w_refs[i][...], m_refs[i][...], v_refs[i][...], g_refs[i][...])

    specs, aspecs, shapes = [], [], []
    for wa in ws:
        blk, imap = blocks(wa.shape)
        specs.append(pl.BlockSpec(blk, imap))
        aspecs.append(pl.BlockSpec((1,) + blk, (lambda *g, imap=imap: (g[nl],) + imap(*g))))
        shapes.append(jax.ShapeDtypeStruct(wa.shape, F32))
    res = pl.pallas_call(
        body, out_shape=shapes * 4, grid=tuple(lead) + (nslots,), in_specs=specs * 3 + aspecs,
        out_specs=specs * 4, compiler_params=_cparams(("parallel",) * nl + ("arbitrary",)),
        name=name)(*ws, *ms, *vs, *alls)
    return res[:n], res[n:2 * n], res[2 * n:3 * n], res[3 * n:]


def kernel(x, norm_ffn1, ffn1_w_gate, ffn1_w_up, ffn1_w_down, norm_mix, w_in, attn_sinks, ssm_lambda_re, ssm_lambda_im, ssm_log_dt, ssm_b_re, ssm_b_im, ssm_c_re, ssm_c_im, ssm_d, ssm_glu_w, ssm_glu_b, attn_out_norm, ssm_out_norm, w_out, norm_ffn2, ffn2_w_gate, ffn2_w_up, ffn2_w_down, final_norm, loss_target, m_norm_ffn1, m_ffn1_w_gate, m_ffn1_w_up, m_ffn1_w_down, m_norm_mix, m_w_in, m_attn_sinks, m_ssm_lambda_re, m_ssm_lambda_im, m_ssm_log_dt, m_ssm_b_re, m_ssm_b_im, m_ssm_c_re, m_ssm_c_im, m_ssm_d, m_ssm_glu_w, m_ssm_glu_b, m_attn_out_norm, m_ssm_out_norm, m_w_out, m_norm_ffn2, m_ffn2_w_gate, m_ffn2_w_up, m_ffn2_w_down, m_final_norm, v_norm_ffn1, v_ffn1_w_gate, v_ffn1_w_up, v_ffn1_w_down, v_norm_mix, v_w_in, v_attn_sinks, v_ssm_lambda_re, v_ssm_lambda_im, v_ssm_log_dt, v_ssm_b_re, v_ssm_b_im, v_ssm_c_re, v_ssm_c_im, v_ssm_d, v_ssm_glu_w, v_ssm_glu_b, v_attn_out_norm, v_ssm_out_norm, v_w_out, v_norm_ffn2, v_ffn2_w_gate, v_ffn2_w_up, v_ffn2_w_down, v_final_norm):
    given = dict(locals())
    wts = {n: given[n] for n in WEIGHTS}

    order = [g for g in GROUPS]
    cx, cy = lax.axis_index("x"), lax.axis_index("y")
    slots = jnp.stack([2 * cx + cy, 2 * (1 - cx) + cy, 2 * cx + 1 - cy, 2 * (1 - cx) + 1 - cy,
                       lax.axis_index("c")]).astype(jnp.int32)
    def view(a, n):
        if n in TRANSPOSED:
            return jnp.swapaxes(a[0], 0, 1)
        if n in BIG:
            return a[0]
        if n in ('ssm_b_re', 'ssm_b_im'):
            return jnp.swapaxes(a, -1, -2)
        return a.reshape(1, -1) if a.ndim == 1 else a

    def unview(a, n):
        if n in TRANSPOSED:
            return jnp.swapaxes(a, 0, 1)[None]
        if n in ('ssm_b_re', 'ssm_b_im'):
            return jnp.swapaxes(a, -1, -2)
        return a.reshape(wts[n].shape)

    started, gather_token = {}, None
    for g in order:
        shards = [view(wts[n], n).astype(BF16) for n in GROUPS[g]]
        if len({s.shape for s in shards}) == 1:
            placed = _place_own(shards, slots, f"weights_place_{g}")
        else:
            placed = [_place_own([s], slots, f"weights_place_{n}")[0] for n, s in zip(GROUPS[g], shards)]
        st, gather_token = _exchange_start([(shards, placed)], False, f"weights_start_{g}", after=gather_token)
        started[g] = st[0]

    def get_weights(group, after):
        if group == order[0]:
            after = after + gather_token[:1, :1]
        _, lands = _exchange_wait(started[group], after, False, f"weights_wait_{group}")
        lands = _sibling_forward(lands, f"weights_forward_{group}")
        out = dict(zip(GROUPS[group], lands))
        for n in ('w_in', 'ssm_glu_w', 'w_out'):
            if n in out:
                out[n] = out[n].reshape(-1, out[n].shape[-1])
        return out

    sent, tokens = {}, {}

    def put_grads(group, gd):
        parts = []
        for n in GROUPS[group]:
            g = gd[n]
            if g.ndim == 2:
                g = g.reshape(N_CHIPS, g.shape[0] // N_CHIPS, g.shape[1])
            parts.append(g.astype(BF16))
        sib = _half_swap(parts, f"grads_half_swap_{group}")
        same = len({p.shape for p in parts}) == 1
        batches = [list(range(len(parts)))] if same else [[i] for i in range(len(parts))]
        halves = [None] * len(parts)
        for b in batches:
            res = _half_add([parts[i] for i in b], [sib[i] for i in b], slots, f"grads_half_add_{GROUPS[group][b[0]]}")
            for i, h in zip(b, res):
                halves[i] = h
        parts = halves
        lands = [lax.empty(p.shape, p.dtype) for p in parts]
        started_g, tokens[group] = _exchange_start([(parts, lands)], True, f"grads_start_{group}")
        sent[group] = started_g[0]
        return tokens[group]

    w = {n: (wts[n][0] if wts[n].ndim > 1 else wts[n]) for n in SMALL}
    w['norm_ffn1'], w['norm_mix'], w['norm_ffn2'] = wts['norm_ffn1'], wts['norm_mix'], wts['norm_ffn2']
    w['ssm_b_re'], w['ssm_b_im'] = view(wts['ssm_b_re'], 'ssm_b_re')[0], view(wts['ssm_b_im'], 'ssm_b_im')[0]
    w['ssm_log_dt'] = w['ssm_log_dt'] + gather_token[0, 0]
    wide =['ssm_b_re', 'ssm_b_im', 'ssm_c_re', 'ssm_c_im']

    def reduce_wide(gd):
        packed = jnp.concatenate([gd[n].reshape(-1, LANES) for n in wide])
        return _small_allreduce(packed, "small_grads_allreduce")

    loss_row, dx, grads, wide_sum = _local_step(x[0], loss_target[0], w, get_weights, put_grads, reduce_wide)

    out_g, out_d, out_m, out_v = {}, {}, {}, {}

    def finish(group, after):
        names = GROUPS[group]
        parts, recv = _exchange_wait(sent[group], after, True, f"grads_wait_{group}")
        same = len({p.shape for p in parts}) == 1
        batches = [list(range(len(names)))] if same else [[i] for i in range(len(names))]
        sums = [None] * len(names)
        for b in batches:
            res = _sum_parts([parts[i] for i in b], [recv[i] for i in b], slots, f"grad_sum_{names[b[0]]}")
            for i, sm in zip(b, res):
                sums[i] = sm
        full = _half_forward(sums, f"grad_half_forward_{group}")
        for b in batches:
            res = _adamw([view(wts[names[i]], names[i]) for i in b], [view(given['m_' + names[i]], names[i]) for i in b],
                         [view(given['v_' + names[i]], names[i]) for i in b], [full[i] for i in b],
                         f"adamw_{names[b[0]]}")
            for i, (d, nm, nv) in zip(b, res):
                n = names[i]
                out_g[n], out_d[n], out_m[n], out_v[n] = (unview(a, n) for a in (full[i], d, nm, nv))
        return nv

    done = finish('ffn2', tokens['ffn1'])
    done = finish('mix', done)

    nat = {n: view(wts[n], n).shape for n in SMALL}
    narrow = [n for n in SMALL if n not in wide]
    alls = list(_small_exchange([grads[n].reshape(nat[n]) for n in narrow] + [loss_row], "small_grads_allgather"))
    loss = jnp.sum(alls.pop()[:, 0, 0])
    rows = wide_sum.shape[0] // len(wide)
    wide_g = [wide_sum[i * rows:(i + 1) * rows].reshape((1,) + nat[n]) for i, n in enumerate(wide)]
    for group, gs, split, tag in ((narrow, alls, None, "adamw_small"), (wide, wide_g, (2, 4), "adamw_ssm_bc")):
        res = _adamw_small([view(wts[n], n) for n in group], [view(given['m_' + n], n) for n in group],
                           [view(given['v_' + n], n) for n in group], gs, split, tag)
        for dst, vals in zip((out_g, out_d, out_m, out_v), res):
            for n, a in zip(group, vals):
                dst[n] = unview(a, n)

    finish('ffn1', out_v['norm_ffn1'][:, :1] + out_v['ssm_c_re'].reshape(1, -1)[:, :1] + done[:1, :1] + loss)

    return (loss, dx[None], *[out_g[n] for n in WEIGHTS], *[out_d[n] for n in WEIGHTS],
            *[out_m[n] for n in WEIGHTS], *[out_v[n] for n in WEIGHTS])
```

```python
import functools
import math

import numpy as np
import jax
import jax.numpy as jnp
from jax import lax
from jax.experimental import pallas as pl
from jax.experimental.pallas import tpu as pltpu

F32 = jnp.float32
BF16 = jnp.bfloat16
MESH = pl.DeviceIdType.MESH

EPS = 1e-6
NEG_INF = -1e30
LAMBDA_RE_MAX = -1e-4
ATTN_HEADS = 8
KV_HEADS = 2
GQ = ATTN_HEADS // KV_HEADS
HEAD_DIM = 64
ATTN_WIDTH = 512
KV_WIDTH = 128
WINDOW = 128
QBLOCK = 128
SSM_WIDTH = 512
SSM_GROUPS = 32
SSM_CH = 16
SSM_STATE = 64
N_STRIPS = 4
STRIP_IN = SSM_WIDTH // N_STRIPS
STRIP_ST = SSM_GROUPS * SSM_STATE // N_STRIPS
SUBLANES = 8
LANES = 128
N_CHIPS = 4
N_DEV = 8

ADAM_LR = 0.001
ADAM_B1 = 0.9
ADAM_B2 = 0.999
ADAM_EPS = 1e-08
ADAM_WD = 0.01
ADAM_STEP = 10

VMEM_LIMIT = 48 * 1024 * 1024

WEIGHTS = ['norm_ffn1', 'ffn1_w_gate', 'ffn1_w_up', 'ffn1_w_down', 'norm_mix', 'w_in', 'attn_sinks',
           'ssm_lambda_re', 'ssm_lambda_im', 'ssm_log_dt', 'ssm_b_re', 'ssm_b_im', 'ssm_c_re', 'ssm_c_im',
           'ssm_d', 'ssm_glu_w', 'ssm_glu_b', 'attn_out_norm', 'ssm_out_norm', 'w_out', 'norm_ffn2',
           'ffn2_w_gate', 'ffn2_w_up', 'ffn2_w_down', 'final_norm']
BIG = ['ffn1_w_gate', 'ffn1_w_up', 'ffn1_w_down', 'w_in', 'ssm_glu_w', 'w_out',
       'ffn2_w_gate', 'ffn2_w_up', 'ffn2_w_down']
SMALL = [n for n in WEIGHTS if n not in BIG]
TRANSPOSED = ['ffn1_w_gate', 'ffn1_w_up', 'w_in', 'ffn2_w_gate', 'ffn2_w_up']
GROUPS = {'ffn1': ['ffn1_w_gate', 'ffn1_w_up', 'ffn1_w_down'],
          'mix': ['w_in', 'ssm_glu_w', 'w_out'],
          'ffn2': ['ffn2_w_gate', 'ffn2_w_up', 'ffn2_w_down']}


def _cparams(sem=None):
    return pltpu.CompilerParams(dimension_semantics=sem, vmem_limit_bytes=VMEM_LIMIT)


def _tile(n, pref):
    if n <= pref:
        return n
    for t in (pref, pref // 2, pref // 4):
        if t % LANES == 0 and n % t == 0:
            return t
    return n


def _sigmoid(x):
    return 1.0 / (1.0 + jnp.exp(-x))


def _sigmoid_tanh(x):
    return 0.5 * jnp.tanh(0.5 * x) + 0.5


def _mm(a, b, *, ta=False, tb=False, reduce_s=False, res=None, scale=1.0, out_dtype=F32, after=None, name):
    a3 = a if a.ndim == 3 else a[None]
    b3 = b if b.ndim == 3 else b[None]
    sa, sb = a3.shape[0], b3.shape[0]
    ns = max(sa, sb)
    (kk, m) = a3.shape[1:] if ta else a3.shape[1:][::-1]
    (n, kb) = b3.shape[1:] if tb else b3.shape[1:][::-1]
    assert kk == kb, (a3.shape, b3.shape)
    tm, tn, tk = _tile(m, 1024), _tile(n, 1024), _tile(kk, 2048)
    nm, nn, nk = m // tm, n // tn, kk // tk
    has_res = res is not None
    single = nk == 1 and not (reduce_s and ns > 1)

    if reduce_s:
        grid = (nm, nn, ns, nk)
        ids = lambda i, j, s, k: (s, i, j, k)
        sem = ("parallel", "parallel", "arbitrary", "arbitrary")
    else:
        grid = (ns, nm, nn, nk)
        ids = lambda s, i, j, k: (s, i, j, k)
        sem = ("parallel", "parallel", "parallel", "arbitrary")

    def a_map(*g):
        s, i, j, k = ids(*g)
        s = s if sa > 1 else 0
        return (s, k, i) if ta else (s, i, k)

    def b_map(*g):
        s, i, j, k = ids(*g)
        s = s if sb > 1 else 0
        return (s, j, k) if tb else (s, k, j)

    def o_map(*g):
        s, i, j, k = ids(*g)
        return (i, j) if reduce_s else (s, i, j)

    a_blk = (1, tk, tm) if ta else (1, tm, tk)
    b_blk = (1, tn, tk) if tb else (1, tk, tn)
    dims = (((0 if ta else 1,), (1 if tb else 0,)), ((), ()))

    def body(*refs):
        a_ref, b_ref = refs[0], refs[1]
        r_ref = refs[2] if has_res else None
        o_ref = refs[2 + has_res + (after is not None)]
        acc_ref = None if single else refs[-1]
        s, _, _, k = ids(*[pl.program_id(d) for d in range(4)])
        prod = lax.dot_general(a_ref[0].astype(BF16), b_ref[0].astype(BF16), dims, preferred_element_type=F32)

        def finish(out):
            if scale != 1.0:
                out = out * scale
            if has_res:
                out = r_ref[...].reshape(out.shape) + out
            o_ref[...] = out.astype(out_dtype).reshape(o_ref.shape)

        if single:
            finish(prod)
            return
        if reduce_s:
            first = jnp.logical_and(s == 0, k == 0)
            last = jnp.logical_and(s == ns - 1, k == nk - 1)
        else:
            first, last = k == 0, k == nk - 1

        acc_ref[...] = prod + jnp.where(first, 0.0, acc_ref[...])

        @pl.when(last)
        def _():
            finish(acc_ref[...])

    in_specs = [pl.BlockSpec(a_blk, a_map), pl.BlockSpec(b_blk, b_map)]
    args = [a3, b3]
    if reduce_s:
        out_shape = jax.ShapeDtypeStruct((m, n), out_dtype)
        o_spec = pl.BlockSpec((tm, tn), o_map)
    else:
        out_shape = jax.ShapeDtypeStruct((ns, m, n), out_dtype)
        o_spec = pl.BlockSpec((1, tm, tn), o_map)
    if has_res:
        assert res.shape == out_shape.shape
        in_specs.append(o_spec)
        args.append(res)
    if after is not None:
        in_specs.append(HBM_SPEC)
        args.append(after)
    return pl.pallas_call(body, out_shape=out_shape, grid=grid, in_specs=in_specs, out_specs=o_spec,
                          scratch_shapes=[] if single else [pltpu.VMEM((tm, tn), F32)],
                          compiler_params=_cparams(sem), name=name)(*args)


def _row_tile(t, cap=256):
    for step in (16, SUBLANES):
        for tr in range(min(cap, t) // step * step, 0, -step):
            if t % tr == 0:
                return tr
    return t


def _norm_mm(xs, gs, w, *, tb, res, name):
    nx = len(xs)
    t = xs[0].shape[0]
    widths = [x.shape[1] for x in xs]
    k = sum(widths)
    n = w.shape[0] if tb else w.shape[1]
    tm, tn = _tile(t, 512), _tile(n, 512)
    has_res = res is not None

    def body(*refs):
        x_refs, g_refs, w_ref = refs[:nx], refs[nx:2 * nx], refs[2 * nx]
        r_ref = refs[2 * nx + 1] if has_res else None
        o_ref, h_ref, h_sc = refs[2 * nx + 1 + has_res:]

        @pl.when(pl.program_id(1) == 0)
        def _():
            off = 0
            for x_ref, g_ref, wd in zip(x_refs, g_refs, widths):
                xv = x_ref[...]
                r = lax.rsqrt(jnp.mean(xv * xv, axis=-1, keepdims=True) + EPS)
                h_sc[:, off:off + wd] = (xv * r * g_ref[...]).astype(BF16)
                off += wd
            h_ref[...] = h_sc[...]

        prod = lax.dot_general(h_sc[...], w_ref[...], NT_DIMS if tb else (((1,), (0,)), ((), ())),
                               preferred_element_type=F32)
        o_ref[...] = r_ref[...] + prod if has_res else prod

    in_specs = [pl.BlockSpec((tm, wd), lambda i, j: (i, 0)) for wd in widths]
    in_specs += [pl.BlockSpec((1, wd), lambda i, j: (0, 0)) for wd in widths]
    in_specs.append(pl.BlockSpec((tn, k), lambda i, j: (j, 0)) if tb else pl.BlockSpec((k, tn), lambda i, j: (0, j)))
    tile = pl.BlockSpec((tm, tn), lambda i, j: (i, j))
    if has_res:
        in_specs.append(tile)
    return pl.pallas_call(
        body, out_shape=(jax.ShapeDtypeStruct((t, n), F32), jax.ShapeDtypeStruct((t, k), BF16)),
        grid=(t // tm, n // tn), in_specs=in_specs,
        out_specs=(tile, pl.BlockSpec((tm, k), lambda i, j: (i, 0))),
        scratch_shapes=[pltpu.VMEM((tm, k), BF16)], compiler_params=_cparams(("parallel", "arbitrary")),
        name=name)(*xs, *gs, w, *([res] if has_res else []))


def _rms_bwd_rows(xv, gv, dhv):
    r = lax.rsqrt(jnp.mean(xv * xv, axis=-1, keepdims=True) + EPS)
    nrm = xv * r
    dn = dhv * gv
    return r * (dn - nrm * jnp.mean(dn * nrm, axis=-1, keepdims=True)), dhv * nrm


def _mm_rms_bwd(a, b, *, tb, xs, gs, dres, after, name):
    nx = len(xs)
    t, k = a.shape
    widths = [x.shape[1] for x in xs]
    n = sum(widths)
    assert n == (b.shape[0] if tb else b.shape[1])
    tm = _tile(t, 512)
    has_res, has_after = dres is not None, after is not None

    def body(*refs):
        a_ref, b_ref = refs[0], refs[1]
        x_refs, g_refs = refs[2:2 + nx], refs[2 + nx:2 + 2 * nx]
        r_ref = refs[2 + 2 * nx] if has_res else None
        outs = refs[2 + 2 * nx + has_res + has_after:]
        dh = lax.dot_general(a_ref[...], b_ref[...], NT_DIMS if tb else (((1,), (0,)), ((), ())),
                             preferred_element_type=F32)
        off = 0
        for i, wd in enumerate(widths):
            dx_ref, dxb_ref, dg_ref = outs[3 * i:3 * i + 3]
            dx, dgs = _rms_bwd_rows(x_refs[i][...], g_refs[i][...], dh[:, off:off + wd])
            if has_res:
                dx = dx + r_ref[...]
            dx_ref[...] = dx
            dxb_ref[...] = dx.astype(BF16)
            part = jnp.sum(dgs, axis=0, keepdims=True)
            dg_ref[...] = part + jnp.where(pl.program_id(0) > 0, dg_ref[...], 0.0)
            off += wd

    in_specs = [pl.BlockSpec((tm, k), lambda i: (i, 0)), pl.BlockSpec(b.shape, lambda i: (0, 0))]
    in_specs += [pl.BlockSpec((tm, wd), lambda i: (i, 0)) for wd in widths]
    in_specs += [pl.BlockSpec((1, wd), lambda i: (0, 0)) for wd in widths]
    args = [a, b, *xs, *gs]
    if has_res:
        in_specs.append(pl.BlockSpec((tm, widths[0]), lambda i: (i, 0)))
        args.append(dres)
    if has_after:
        in_specs.append(HBM_SPEC)
        args.append(after)
    out_shape, out_specs = [], []
    for wd in widths:
        out_shape += [jax.ShapeDtypeStruct((t, wd), F32), jax.ShapeDtypeStruct((t, wd), BF16),
                      jax.ShapeDtypeStruct((1, wd), F32)]
        out_specs += [pl.BlockSpec((tm, wd), lambda i: (i, 0)), pl.BlockSpec((tm, wd), lambda i: (i, 0)),
                      pl.BlockSpec((1, wd), lambda i: (0, 0))]
    res = pl.pallas_call(body, out_shape=out_shape, grid=(t // tm,), in_specs=in_specs, out_specs=out_specs,
                         compiler_params=_cparams(("arbitrary",)), name=name)(*args)
    return [tuple(res[3 * i:3 * i + 3]) for i in range(nx)]


FFN_ROWS = 512
FFN_FWD_ROWS = 1024
FFN_SPLIT = 2
FFN_W_ROWS = 1024
SCAN_ROWS = 256


NT_DIMS = (((1,), (1,)), ((), ()))
TN_DIMS = (((0,), (0,)), ((), ()))


def _ffn_fwd_call(x, g, wg, wu, wd, name):
    t, d = x.shape
    ns, f, _ = wg.shape
    tm = _tile(t, FFN_FWD_ROWS)

    def body(x_ref, g_ref, wg_ref, wu_ref, wd_ref, xo_ref, h_ref, gate_ref, up_ref, h_sc, acc_ref):
        s = pl.program_id(1)

        @pl.when(s == 0)
        def _():
            xv = x_ref[...]
            r = lax.rsqrt(jnp.mean(xv * xv, axis=-1, keepdims=True) + EPS)
            hb = (xv * r * g_ref[...]).astype(BF16)
            h_sc[...] = hb
            h_ref[...] = hb

        for r0 in range(0, tm, tm // FFN_SPLIT):
            rows = slice(r0, r0 + tm // FFN_SPLIT)
            hb = h_sc[rows, :]
            gate = lax.dot_general(hb, wg_ref[0], NT_DIMS, preferred_element_type=F32)
            up = lax.dot_general(hb, wu_ref[0], NT_DIMS, preferred_element_type=F32)
            gate_ref[0, rows, :] = gate.astype(BF16)
            up_ref[0, rows, :] = up.astype(BF16)
            act = (gate * _sigmoid_tanh(gate) * up).astype(BF16)
            prod = jnp.dot(act, wd_ref[0], preferred_element_type=F32)
            acc_ref[rows, :] = prod + jnp.where(s > 0, acc_ref[rows, :], 0.0)

        @pl.when(s == ns - 1)
        def _():
            xo_ref[...] = x_ref[...] + 0.5 * acc_ref[...]

    row = pl.BlockSpec((tm, d), lambda i, s: (i, 0))
    vec = pl.BlockSpec((1, d), lambda i, s: (0, 0))
    wrow = pl.BlockSpec((1, f, d), lambda i, s: (s, 0, 0))
    hid = pl.BlockSpec((1, tm, f), lambda i, s: (s, i, 0))
    hid_sh = jax.ShapeDtypeStruct((ns, t, f), BF16)
    return pl.pallas_call(
        body, out_shape=(jax.ShapeDtypeStruct((t, d), F32), jax.ShapeDtypeStruct((t, d), BF16), hid_sh, hid_sh),
        grid=(t // tm, ns), in_specs=[row, vec, wrow, wrow, wrow], out_specs=(row, row, hid, hid),
        scratch_shapes=[pltpu.VMEM((tm, d), BF16), pltpu.VMEM((tm, d), F32)],
        compiler_params=_cparams(("parallel", "arbitrary")), name=name)(x, g, wg, wu, wd)


def _ffn_bwd_x_call(dxo, dxo_b, x, g, gate, up, wg, wu, wd, name):
    t, d = x.shape
    ns, f, _ = wg.shape
    tm = _tile(t, FFN_ROWS)

    def body(dxo_ref, dxb_ref, x_ref, g_ref, gate_ref, up_ref, wg_ref, wu_ref, wd_ref,
             dx_ref, dxob_ref, dgn_ref, dgate_ref, dup_ref, act_ref, dh_ref):
        s, i = pl.program_id(0), pl.program_id(1)
        base = pl.multiple_of(i * tm, tm)
        for r0 in range(0, tm, tm // FFN_SPLIT):
            rows = slice(r0, r0 + tm // FFN_SPLIT)
            acc_rows = pl.ds(base + r0, tm // FFN_SPLIT)
            dact = lax.dot_general(dxb_ref[rows, :], wd_ref[0], NT_DIMS, preferred_element_type=F32) * 0.5
            gv = gate_ref[0, rows, :].astype(F32)
            uv = up_ref[0, rows, :].astype(F32)
            sg = _sigmoid_tanh(gv)
            silu = gv * sg
            act_ref[0, rows, :] = (silu * uv).astype(BF16)
            dub = (dact * silu).astype(BF16)
            dgb = (dact * uv * sg * (1.0 + gv * (1.0 - sg))).astype(BF16)
            dup_ref[0, rows, :] = dub
            dgate_ref[0, rows, :] = dgb
            prod = (jnp.dot(dgb, wg_ref[0], preferred_element_type=F32)
                    + jnp.dot(dub, wu_ref[0], preferred_element_type=F32))

            dh_ref[acc_rows, :] = prod + jnp.where(s > 0, dh_ref[acc_rows, :], 0.0)

        @pl.when(jnp.logical_and(i == 0, s == 0))
        def _():
            dgn_ref[...] = jnp.zeros_like(dgn_ref)

        @pl.when(s == ns - 1)
        def _():
            dx, dgs = _rms_bwd_rows(x_ref[...], g_ref[...], dh_ref[pl.ds(base, tm), :])
            dx = dx + dxo_ref[...]
            dx_ref[...] = dx
            dxob_ref[...] = dx.astype(BF16)
            dgn_ref[...] += jnp.sum(dgs, axis=0, keepdims=True)

    last_only = lambda s, i: (jnp.where(s == ns - 1, i, 0), 0)
    row_last = pl.BlockSpec((tm, d), last_only)
    row = pl.BlockSpec((tm, d), lambda s, i: (i, 0))
    vec = pl.BlockSpec((1, d), lambda s, i: (0, 0))
    wrow = pl.BlockSpec((1, f, d), lambda s, i: (s, 0, 0))
    hid = pl.BlockSpec((1, tm, f), lambda s, i: (s, i, 0))
    hid_sh = jax.ShapeDtypeStruct((ns, t, f), BF16)
    return pl.pallas_call(
        body,
        out_shape=(jax.ShapeDtypeStruct((t, d), F32), jax.ShapeDtypeStruct((t, d), BF16),
                   jax.ShapeDtypeStruct((1, d), F32), hid_sh, hid_sh, hid_sh),
        grid=(ns, t // tm), in_specs=[row_last, row, row_last, vec, hid, hid, wrow, wrow, wrow],
        out_specs=(row_last, row_last, vec, hid, hid, hid), scratch_shapes=[pltpu.VMEM((t, d), F32)],
        compiler_params=_cparams(("arbitrary", "arbitrary")), name=name)(dxo, dxo_b, x, g, gate, up, wg, wu, wd)


def _ffn_bwd_w_call(h, dxo_b, dgate, dup, act, name):
    t, d = h.shape
    ns, _, f = dgate.shape
    tm = _tile(t, FFN_W_ROWS)
    nm = t // tm

    def body(h_ref, dxb_ref, dgate_ref, dup_ref, act_ref, dwg_ref, dwu_ref, dwd_ref, ag_ref, au_ref, ad_ref):
        i = pl.program_id(1)
        hv = h_ref[...]
        pg = lax.dot_general(dgate_ref[0], hv, TN_DIMS, preferred_element_type=F32)
        pu = lax.dot_general(dup_ref[0], hv, TN_DIMS, preferred_element_type=F32)
        pd = lax.dot_general(act_ref[0], dxb_ref[...], TN_DIMS, preferred_element_type=F32)

        ag_ref[...] = pg + jnp.where(i > 0, ag_ref[...], 0.0)
        au_ref[...] = pu + jnp.where(i > 0, au_ref[...], 0.0)
        ad_ref[...] = pd + jnp.where(i > 0, ad_ref[...], 0.0)

        @pl.when(i == nm - 1)
        def _():
            dwg_ref[0] = ag_ref[...].astype(BF16)
            dwu_ref[0] = au_ref[...].astype(BF16)
            dwd_ref[0] = (0.5 * ad_ref[...]).astype(BF16)

    row = pl.BlockSpec((tm, d), lambda s, i: (i, 0))
    hid = pl.BlockSpec((1, tm, f), lambda s, i: (s, i, 0))
    wrow = pl.BlockSpec((1, f, d), lambda s, i: (s, 0, 0))
    wsh = jax.ShapeDtypeStruct((ns, f, d), BF16)
    return pl.pallas_call(
        body, out_shape=(wsh, wsh, wsh),
        grid=(ns, nm), in_specs=[row, row, hid, hid, hid], out_specs=(wrow, wrow, wrow),
        scratch_shapes=[pltpu.VMEM((f, d), F32), pltpu.VMEM((f, d), F32), pltpu.VMEM((f, d), F32)],
        compiler_params=_cparams(("parallel", "arbitrary")), name=name)(h, dxo_b, dgate, dup, act)


def _loss_head(x, g, tgt, name):
    t, w = x.shape
    tr = _row_tile(t)

    def body(x_ref, g_ref, t_ref, loss_ref, dx_ref, dxb_ref, dg_ref):
        xv = x_ref[...]
        gv = g_ref[...]
        r = lax.rsqrt(jnp.mean(xv * xv, axis=-1, keepdims=True) + EPS)
        nrm = xv * r
        err = nrm * gv - t_ref[...]
        dout = err * (1.0 / w)
        dn = dout * gv
        dx = r * (dn - nrm * jnp.mean(dn * nrm, axis=-1, keepdims=True))
        dx_ref[...] = dx
        dxb_ref[...] = dx.astype(BF16)

        @pl.when(pl.program_id(0) == 0)
        def _():
            dg_ref[...] = jnp.zeros_like(dg_ref)
            loss_ref[...] = jnp.zeros_like(loss_ref)

        dg_ref[...] += jnp.sum(dout * nrm, axis=0, keepdims=True)
        part = jnp.sum(jnp.sum(err * err, axis=-1, keepdims=True) * (0.5 / w), axis=0, keepdims=True)
        loss_ref[...] += jnp.broadcast_to(part, loss_ref.shape)

    row = pl.BlockSpec((tr, w), lambda i: (i, 0))
    vec = pl.BlockSpec((1, w), lambda i: (0, 0))
    return pl.pallas_call(
        body, out_shape=(jax.ShapeDtypeStruct((1, LANES), F32), jax.ShapeDtypeStruct((t, w), F32),
                         jax.ShapeDtypeStruct((t, w), BF16), jax.ShapeDtypeStruct((1, w), F32)),
        grid=(t // tr,), in_specs=[row, vec, row],
        out_specs=(pl.BlockSpec((1, LANES), lambda i: (0, 0)), row, row, vec),
        compiler_params=_cparams(("arbitrary",)), name=name)(x, g, tgt)


def _attn_bias():
    slopes = np.asarray(2.0 ** (-8.0 * (np.arange(ATTN_HEADS) + 1) / ATTN_HEADS), np.float32)
    qi = np.arange(QBLOCK)[:, None]
    kj = np.arange(3 * QBLOCK)[None, :]
    rel = np.abs(kj - QBLOCK - qi).astype(np.float32)
    tile = np.where(rel <= WINDOW, -slopes[:, None, None] * rel[None], np.float32(NEG_INF)).astype(np.float32)
    tile = tile.reshape(KV_HEADS, GQ * QBLOCK, 3 * QBLOCK)
    return jnp.asarray(np.swapaxes(tile, 1, 2))


def _attn_scores(k3, q, n, nb, bias):
    s = lax.dot_general(k3, q, NT_DIMS, preferred_element_type=F32) * (HEAD_DIM ** -0.5)
    key = lax.broadcasted_iota(jnp.int32, (3 * QBLOCK, 1), 0)
    inside = (key >= jnp.where(n == 0, QBLOCK, 0)) & (key < jnp.where(n == nb - 1, 2 * QBLOCK, 3 * QBLOCK))
    return jnp.where(inside, s + bias, NEG_INF)


Q_COL, K_COL, V_COL, U_COL = 0, ATTN_WIDTH // LANES, ATTN_WIDTH // LANES + 1, ATTN_WIDTH // LANES + 2


def _key_rows(ref, n, nb):
    prev, nxt = jnp.maximum(n - 1, 0), jnp.minimum(n + 1, nb - 1)
    blk = lambda b: ref[pl.ds(pl.multiple_of(b * QBLOCK, QBLOCK), QBLOCK), :]
    return jnp.concatenate([blk(prev), blk(n), blk(nxt)], axis=0)


def _head_tiles(x, kh, low):
    tiles = []
    for g in range(GQ):
        h = GQ * kh + g
        t128 = x[:, LANES * (h // 2):LANES * (h // 2 + 1)]
        t128 = jnp.where(low if h % 2 == 0 else jnp.logical_not(low), t128, 0.0)
        if h % 2 != kh:
            t128 = pltpu.roll(t128, HEAD_DIM, 1)
        tiles.append(t128)
    return jnp.concatenate(tiles, axis=0)


def _head_merge(per_kh, low):
    out = []
    for j in range(ATTN_HEADS // 2):
        pair = []
        for h in (2 * j, 2 * j + 1):
            kh, g = h // GQ, h % GQ
            t128 = per_kh[kh][g * QBLOCK:(g + 1) * QBLOCK, :]
            if h % 2 != kh:
                t128 = pltpu.roll(t128, HEAD_DIM, 1)
            pair.append(t128)
        out.append(jnp.where(low, pair[0], pair[1]))
    return jnp.concatenate(out, axis=1)


def _attn_fwd_proj(proj, sink_rows, bias, name):
    t = proj.shape[0]
    nb = t // QBLOCK
    rows = GQ * QBLOCK

    def body(q_ref, k_ref, v_ref, sink_ref, bias_ref, o_ref, lse_ref):
        n = pl.program_id(0)
        low = lax.broadcasted_iota(jnp.int32, (QBLOCK, LANES), 1) < HEAD_DIM
        k3 = _key_rows(k_ref, n, nb).astype(BF16)
        v3 = _key_rows(v_ref, n, nb).astype(BF16)
        q = q_ref[...]
        outs = []
        for kh in range(KV_HEADS):
            qs = _head_tiles(q, kh, low).astype(BF16)
            s = _attn_scores(k3, qs, n, nb, bias_ref[kh])
            sink = sink_ref[kh]
            mx = jnp.maximum(jnp.max(s, axis=0, keepdims=True), sink)
            p = jnp.exp(s - mx)
            den = jnp.sum(p, axis=0, keepdims=True) + jnp.exp(sink - mx)
            pn = (p * (1.0 / den)).astype(BF16)
            outs.append(lax.dot_general(pn, v3, TN_DIMS, preferred_element_type=F32))
            lse_ref[0, kh] = mx + jnp.log(den)
        o_ref[...] = _head_merge(outs, low)

    strip = lambda col: pl.BlockSpec((t, LANES), lambda n, col=col: (0, col))
    rowspec = pl.BlockSpec((KV_HEADS, 1, rows), lambda n: (0, 0, 0))
    biasspec = pl.BlockSpec((KV_HEADS, 3 * QBLOCK, rows), lambda n: (0, 0, 0))
    return pl.pallas_call(
        body, out_shape=(jax.ShapeDtypeStruct((t, ATTN_WIDTH), F32), jax.ShapeDtypeStruct((nb, KV_HEADS, 1, rows), F32)),
        grid=(nb,), in_specs=[pl.BlockSpec((QBLOCK, ATTN_WIDTH), lambda n: (n, 0)), strip(K_COL), strip(V_COL),
                              rowspec, biasspec],
        out_specs=(pl.BlockSpec((QBLOCK, ATTN_WIDTH), lambda n: (n, 0)),
                   pl.BlockSpec((1, KV_HEADS, 1, rows), lambda n: (n, 0, 0, 0))),
        compiler_params=_cparams(("parallel",)), name=name)(proj, proj, proj, sink_rows, bias)


def _attn_bwd_proj(proj, sink_rows, bias, o, lse, do, name):
    t = proj.shape[0]
    nb = t // QBLOCK
    rows = GQ * QBLOCK
    scale = HEAD_DIM ** -0.5

    def body(q_ref, k_ref, v_ref, sink_ref, bias_ref, o_ref, lse_ref, do_ref, dq_ref, dk_ref, dv_ref, ds_ref):
        n = pl.program_id(0)

        @pl.when(n == 0)
        def _():
            dk_ref[...] = jnp.zeros_like(dk_ref)
            dv_ref[...] = jnp.zeros_like(dv_ref)
            ds_ref[...] = jnp.zeros_like(ds_ref)

        low = lax.broadcasted_iota(jnp.int32, (QBLOCK, LANES), 1) < HEAD_DIM
        k3 = _key_rows(k_ref, n, nb).astype(BF16)
        v3 = _key_rows(v_ref, n, nb).astype(BF16)
        q, dov = q_ref[...], do_ref[...]
        dod = dov * o_ref[...]
        dqs = []
        dk3 = jnp.zeros((3 * QBLOCK, LANES), F32)
        dv3 = jnp.zeros((3 * QBLOCK, LANES), F32)
        ones = jnp.ones((SUBLANES, LANES), F32)
        for kh in range(KV_HEADS):
            qs = _head_tiles(q, kh, low).astype(BF16)
            dos = _head_tiles(dov, kh, low).astype(BF16)
            delta = lax.dot_general(ones, _head_tiles(dod, kh, low), NT_DIMS, preferred_element_type=F32,
                                    precision=lax.Precision.HIGHEST)[0:1, :]
            lse_kh = lse_ref[0, kh]
            s = _attn_scores(k3, qs, n, nb, bias_ref[kh])
            p = jnp.exp(s - lse_kh)
            dp = lax.dot_general(v3, dos, NT_DIMS, preferred_element_type=F32)
            dsb = (p * (dp - delta)).astype(BF16)
            dqs.append(lax.dot_general(dsb, k3, TN_DIMS, preferred_element_type=F32) * scale)
            dk3 = dk3 + jnp.dot(dsb, qs, preferred_element_type=F32) * scale
            dv3 = dv3 + jnp.dot(p.astype(BF16), dos, preferred_element_type=F32)
            ds_ref[kh] += -jnp.exp(sink_ref[kh] - lse_kh) * delta
        dq_ref[...] = _head_merge(dqs, low)
        prev, nxt = jnp.maximum(n - 1, 0), jnp.minimum(n + 1, nb - 1)
        for j, b in enumerate((prev, n, nxt)):
            blk = pl.ds(pl.multiple_of(b * QBLOCK, QBLOCK), QBLOCK)
            dk_ref[blk, :] += dk3[j * QBLOCK:(j + 1) * QBLOCK, :]
            dv_ref[blk, :] += dv3[j * QBLOCK:(j + 1) * QBLOCK, :]

    strip = lambda col: pl.BlockSpec((t, LANES), lambda n, col=col: (0, col))
    rowspec = pl.BlockSpec((KV_HEADS, 1, rows), lambda n: (0, 0, 0))
    qspec = pl.BlockSpec((QBLOCK, ATTN_WIDTH), lambda n: (n, 0))
    kv_out = pl.BlockSpec((t, LANES), lambda n: (0, 0))
    biasspec = pl.BlockSpec((KV_HEADS, 3 * QBLOCK, rows), lambda n: (0, 0, 0))
    return pl.pallas_call(
        body,
        out_shape=(jax.ShapeDtypeStruct((t, ATTN_WIDTH), F32), jax.ShapeDtypeStruct((t, LANES), F32),
                   jax.ShapeDtypeStruct((t, LANES), F32), jax.ShapeDtypeStruct((KV_HEADS, 1, rows), F32)),
        grid=(nb,),
        in_specs=[qspec, strip(K_COL), strip(V_COL), rowspec, biasspec, qspec,
                  pl.BlockSpec((1, KV_HEADS, 1, rows), lambda n: (n, 0, 0, 0)), qspec],
        out_specs=(qspec, kv_out, kv_out, rowspec),
        compiler_params=_cparams(("arbitrary",)), name=name)(proj, proj, proj, sink_rows, bias, o, lse, do)


def _scan_tables(a_re, a_im, reverse):
    pw = [(a_re, a_im)]
    for _ in range(SUBLANES - 1):
        pr, pi = pw[-1]
        pw.append((pr * a_re - pi * a_im, pr * a_im + pi * a_re))
    rows = np.arange(SUBLANES)
    tabs = []
    for d in (1, 2, 4):
        mask = (rows <= SUBLANES - 1 - d) if reverse else (rows >= d)
        m = jnp.asarray(mask, F32)[:, None]
        tabs += [m * pw[d - 1][0][None, :], m * pw[d - 1][1][None, :]]
    order = (SUBLANES - 1 - rows) if reverse else rows
    tabs += [jnp.stack([pw[j][0] for j in order]), jnp.stack([pw[j][1] for j in order])]
    tab = jnp.stack(tabs)
    return tab.reshape(8, SUBLANES, N_STRIPS, STRIP_ST).transpose(2, 0, 1, 3)


def _scan_pair_chunk(dirs):
    nblk = dirs[0]['xr'].shape[0] // SUBLANES

    @pl.when(pl.program_id(1) == 0)
    def _():
        for d in dirs:
            d['carry'][...] = jnp.zeros_like(d['carry'])

    for d in dirs:
        vb = d['v'][...].astype(BF16)
        d['xr'][...] = jnp.dot(vb, d['mir'][0], preferred_element_type=F32)
        d['xi'][...] = jnp.dot(vb, d['mii'][0], preferred_element_type=F32)
    carries = [(d['carry'][0], d['carry'][1]) for d in dirs]
    for i in range(nblk):
        for k, d in enumerate(dirs):
            rev = d['reverse']
            rows = pl.ds(((nblk - 1 - i) if rev else i) * SUBLANES, SUBLANES)
            cr, ci = carries[k]
            xr, xi = d['xr'][rows, :], d['xi'][rows, :]
            for j, s in enumerate((1, 2, 4)):
                tr_, ti_ = d['tab'][0, 2 * j], d['tab'][0, 2 * j + 1]
                sh = (SUBLANES - s) if rev else s
                sr, si = pltpu.roll(xr, sh, 0), pltpu.roll(xi, sh, 0)
                xr, xi = xr + tr_ * sr - ti_ * si, xi + tr_ * si + ti_ * sr
            pr, pi = d['tab'][0, 6], d['tab'][0, 7]
            xr, xi = xr + pr * cr - pi * ci, xi + pr * ci + pi * cr
            d['xr'][rows, :] = xr
            d['xi'][rows, :] = xi
            edge = 0 if rev else SUBLANES - 1
            carries[k] = (jnp.broadcast_to(xr[edge:edge + 1, :], xr.shape),
                          jnp.broadcast_to(xi[edge:edge + 1, :], xi.shape))
    for k, d in enumerate(dirs):
        d['carry'][0], d['carry'][1] = carries[k]
        d['y'][...] = (jnp.dot(d['xr'][...].astype(BF16), d['mor'][0], preferred_element_type=F32)
                       + jnp.dot(d['xi'][...].astype(BF16), d['moi'][0], preferred_element_type=F32))


def _scan_pair(v, ops_f, ops_b, name):
    t = v.shape[0]
    tc = _tile(t, SCAN_ROWS)
    nc = t // tc

    def body(vf_ref, vb_ref, *refs):
        ops = refs[:10]
        outs = refs[10:16]
        carries = refs[16:18]
        dirs = []
        for k, (v_ref, rev) in enumerate(((vf_ref, False), (vb_ref, True))):
            mir, mii, tab, mor, moi = ops[5 * k:5 * k + 5]
            y, xr, xi = outs[3 * k:3 * k + 3]
            dirs.append(dict(v=v_ref, mir=mir, mii=mii, tab=tab, mor=mor, moi=moi, y=y, xr=xr, xi=xi,
                             carry=carries[k], reverse=rev))
        _scan_pair_chunk(dirs)

    col0 = v.shape[1] // STRIP_IN - N_STRIPS
    fmap = lambda s, c: (c, s)
    bmap = lambda s, c: (nc - 1 - c, s)
    smap3 = lambda s, c: (s, 0, 0)
    m_in = pl.BlockSpec((1, STRIP_IN, STRIP_ST), smap3)
    m_out = pl.BlockSpec((1, STRIP_ST, STRIP_IN), smap3)
    tabspec = pl.BlockSpec((1, 8, SUBLANES, STRIP_ST), lambda s, c: (s, 0, 0, 0))
    opspecs = [m_in, m_in, tabspec, m_out, m_out]
    y_sh = jax.ShapeDtypeStruct((t, SSM_WIDTH), F32)
    x_sh = jax.ShapeDtypeStruct((t, N_STRIPS * STRIP_ST), F32)
    outspecs = lambda m: [pl.BlockSpec((tc, STRIP_IN), m), pl.BlockSpec((tc, STRIP_ST), m),
                          pl.BlockSpec((tc, STRIP_ST), m)]
    res = pl.pallas_call(
        body, out_shape=[y_sh, x_sh, x_sh] * 2, grid=(N_STRIPS, nc),
        in_specs=[pl.BlockSpec((tc, STRIP_IN), lambda s, c: (c, s + col0)),
                  pl.BlockSpec((tc, STRIP_IN), lambda s, c: (nc - 1 - c, s + col0))] + opspecs * 2,
        out_specs=outspecs(fmap) + outspecs(bmap),
        scratch_shapes=[pltpu.VMEM((2, SUBLANES, STRIP_ST), F32)] * 2,
        compiler_params=_cparams(("parallel", "arbitrary")), name=name)(v, v, *ops_f, *ops_b)
    return tuple(res[:3]), tuple(res[3:])


def _scan_adjoint_pair(dy, u, states, adj_ops, name):
    t = dy.shape[0]
    tc = _tile(t, SCAN_ROWS)
    nc = t // tc
    hb = tc // SUBLANES
    n_out = 6

    def body(*refs):
        c = pl.program_id(1)
        dirs = []
        for k in range(2):
            dy_ref, mir, mii, tab, mor, moi, u_ref, xr_ref, xi_ref, hr_ref, hi_ref = refs[11 * k:11 * k + 11]
            outs = refs[22 + n_out * k:22 + n_out * (k + 1)]
            lr_ref, li_ref, carry = refs[22 + 2 * n_out + 3 * k:22 + 2 * n_out + 3 * k + 3]
            dirs.append(dict(v=dy_ref, mir=mir, mii=mii, tab=tab, mor=mor, moi=moi, y=outs[0], xr=lr_ref, xi=li_ref,
                             carry=carry, reverse=(k == 0), u=u_ref, fx=(xr_ref, xi_ref), halo=(hr_ref, hi_ref),
                             acc=outs[1:]))

        @pl.when(c == 0)
        def _():
            for d in dirs:
                for r in d['acc']:
                    r[...] = jnp.zeros_like(r)

        _scan_pair_chunk(dirs)
        for d in dirs:
            fwd_reverse = not d['reverse']
            rc = (nc - 1 - c) if d['reverse'] else c
            dmir_ref, dmii_ref, dmor_ref, dmoi_ref, da_ref = d['acc']
            xrv, xiv, lrv, liv = d['fx'][0][...], d['fx'][1][...], d['xr'][...], d['xi'][...]
            hr_ref, hi_ref = d['halo']
            row = lax.broadcasted_iota(jnp.int32, xrv.shape, 0)
            if fwd_reverse:
                live = (rc < nc - 1).astype(F32)
                edge_r, edge_i = hr_ref[0:1, :] * live, hi_ref[0:1, :] * live
                xpr = jnp.where(row == tc - 1, edge_r, pltpu.roll(xrv, tc - 1, 0))
                xpi = jnp.where(row == tc - 1, edge_i, pltpu.roll(xiv, tc - 1, 0))
            else:
                live = (rc > 0).astype(F32)
                edge_r, edge_i = hr_ref[SUBLANES - 1:SUBLANES, :] * live, hi_ref[SUBLANES - 1:SUBLANES, :] * live
                xpr = jnp.where(row == 0, edge_r, pltpu.roll(xrv, 1, 0))
                xpi = jnp.where(row == 0, edge_i, pltpu.roll(xiv, 1, 0))
            da_ref[0, 0:1, :] += jnp.sum(xpr * lrv + xpi * liv, axis=0, keepdims=True)
            da_ref[0, 1:2, :] += jnp.sum(xpr * liv - xpi * lrv, axis=0, keepdims=True)
            ub, dyb = d['u'][...].astype(BF16), d['v'][...].astype(BF16)
            dmir_ref[0] += lax.dot_general(ub, lrv.astype(BF16), TN_DIMS, preferred_element_type=F32)
            dmii_ref[0] += lax.dot_general(ub, liv.astype(BF16), TN_DIMS, preferred_element_type=F32)
            dmor_ref[0] += lax.dot_general(xrv.astype(BF16), dyb, TN_DIMS, preferred_element_type=F32)
            dmoi_ref[0] += lax.dot_general(xiv.astype(BF16), dyb, TN_DIMS, preferred_element_type=F32)

    col0 = u.shape[1] // STRIP_IN - N_STRIPS
    smap3 = lambda s, c: (s, 0, 0)
    m_in = pl.BlockSpec((1, STRIP_IN, STRIP_ST), smap3)
    m_out = pl.BlockSpec((1, STRIP_ST, STRIP_IN), smap3)
    tabspec = pl.BlockSpec((1, 8, SUBLANES, STRIP_ST), lambda s, c: (s, 0, 0, 0))
    in_specs, out_specs, args = [], [], []
    for k in range(2):
        reverse = k == 0
        rowblk = (lambda c: nc - 1 - c) if reverse else (lambda c: c)
        tmap = lambda s, c, rowblk=rowblk: (rowblk(c), s)
        umap = lambda s, c, rowblk=rowblk: (rowblk(c), s + col0)
        if not reverse:
            hmap = lambda s, c, rowblk=rowblk: (jnp.minimum((rowblk(c) + 1) * hb, t // SUBLANES - 1), s)
        else:
            hmap = lambda s, c, rowblk=rowblk: (jnp.maximum(rowblk(c) * hb - 1, 0), s)
        narrow = pl.BlockSpec((tc, STRIP_IN), tmap)
        wide = pl.BlockSpec((tc, STRIP_ST), tmap)
        halo = pl.BlockSpec((SUBLANES, STRIP_ST), hmap)
        in_specs += [narrow, m_in, m_in, tabspec, m_out, m_out, pl.BlockSpec((tc, STRIP_IN), umap), wide, wide,
                     halo, halo]
        out_specs += [narrow, m_in, m_in, m_out, m_out, pl.BlockSpec((1, SUBLANES, STRIP_ST), smap3)]
        xr, xi = states[k]
        args += [dy, *adj_ops[k], u, xr, xi, xr, xi]
    out_shape = [jax.ShapeDtypeStruct((t, SSM_WIDTH), F32),
                 jax.ShapeDtypeStruct((N_STRIPS, STRIP_IN, STRIP_ST), F32),
                 jax.ShapeDtypeStruct((N_STRIPS, STRIP_IN, STRIP_ST), F32),
                 jax.ShapeDtypeStruct((N_STRIPS, STRIP_ST, STRIP_IN), F32),
                 jax.ShapeDtypeStruct((N_STRIPS, STRIP_ST, STRIP_IN), F32),
                 jax.ShapeDtypeStruct((N_STRIPS, SUBLANES, STRIP_ST), F32)] * 2
    res = pl.pallas_call(
        body, out_shape=out_shape, grid=(N_STRIPS, nc), in_specs=in_specs, out_specs=out_specs,
        scratch_shapes=[pltpu.VMEM((tc, STRIP_ST), F32), pltpu.VMEM((tc, STRIP_ST), F32),
                        pltpu.VMEM((2, SUBLANES, STRIP_ST), F32)] * 2,
        compiler_params=_cparams(("parallel", "arbitrary")), name=name)(*args)
    return tuple(res[:n_out]), tuple(res[n_out:])


def _ssm_prep(lam_re, lam_im, log_dt, bt_re, bt_im, c_re, c_im):
    lr = jnp.minimum(lam_re, LAMBDA_RE_MAX)
    li = lam_im
    dt = jnp.exp(log_dt)[:, None]
    mag = jnp.exp(lr * dt)
    a_re = mag * jnp.cos(li * dt)
    a_im = mag * jnp.sin(li * dt)
    den = lr * lr + li * li
    coef_re = ((a_re - 1.0) * lr + a_im * li) / den
    coef_im = (a_im * lr - (a_re - 1.0) * li) / den
    bb_re = coef_re[:, None, :] * bt_re - coef_im[:, None, :] * bt_im
    bb_im = coef_re[:, None, :] * bt_im + coef_im[:, None, :] * bt_re
    eye = jnp.eye(SSM_GROUPS // N_STRIPS, dtype=F32)

    def strips(m):
        g, a, b = m.shape
        m4 = m.reshape(N_STRIPS, g // N_STRIPS, a, b)
        return jnp.einsum('sgab,gk->sgakb', m4, eye).reshape(N_STRIPS, g // N_STRIPS * a, g // N_STRIPS * b)

    mi_re = strips(bb_re)
    mi_im = strips(bb_im)
    mo_re = strips(jnp.swapaxes(c_re, 1, 2))
    mo_im = strips(-jnp.swapaxes(c_im, 1, 2))
    return a_re.reshape(-1), a_im.reshape(-1), mi_re, mi_im, mo_re, mo_im


def _gelu(x):
    c = math.sqrt(2.0 / math.pi)
    return 0.5 * x * (1.0 + jnp.tanh(c * (x + 0.044715 * x * x * x)))


def _gelu_grad(x):
    c = math.sqrt(2.0 / math.pi)
    th = jnp.tanh(c * (x + 0.044715 * x * x * x))
    return 0.5 * (1.0 + th) + 0.5 * x * (1.0 - th * th) * c * (1.0 + 3.0 * 0.044715 * x * x)


def _last_cols_specs(u, w, tr):
    half = w // 2
    first = (u.shape[1] - w) // half
    assert first * half == u.shape[1] - w
    return [pl.BlockSpec((tr, half), lambda i, k=k: (i, first + k)) for k in range(2)]


def _ssm_post_fwd(u, yf, yb, d, wglu, bglu, name):
    t, w = yf.shape
    tr = _row_tile(t)

    def body(ua_ref, ub_ref, yf_ref, yb_ref, d_ref, w_ref, b_ref, s_ref, y0_ref, z_ref):
        uv = jnp.concatenate([ua_ref[...], ub_ref[...]], axis=1)
        y0 = d_ref[...] * uv + yf_ref[...] + yb_ref[...]
        yg = _gelu(y0)
        z = jnp.dot(yg.astype(BF16), w_ref[...], preferred_element_type=F32) + b_ref[...]
        s_ref[...] = yg * _sigmoid(z)
        y0_ref[...] = y0
        z_ref[...] = z

    row = pl.BlockSpec((tr, w), lambda i: (i, 0))
    vec = pl.BlockSpec((1, w), lambda i: (0, 0))
    mat = pl.BlockSpec((w, w), lambda i: (0, 0))
    sh = jax.ShapeDtypeStruct((t, w), F32)
    return pl.pallas_call(body, out_shape=(sh, sh, sh), grid=(t // tr,),
                          in_specs=[*_last_cols_specs(u, w, tr), row, row, vec, mat, vec], out_specs=(row, row, row),
                          compiler_params=_cparams(("parallel",)), name=name)(u, u, yf, yb, d, wglu, bglu)


def _ssm_post_bwd(ds, y0, z, u, d, wglu, name):
    t, w = ds.shape
    tr = _row_tile(t)

    def body(ds_ref, y0_ref, z_ref, ua_ref, ub_ref, d_ref, w_ref, dy0_ref, dw_ref, db_ref, dd_ref):
        @pl.when(pl.program_id(0) == 0)
        def _():
            dw_ref[...] = jnp.zeros_like(dw_ref)
            db_ref[...] = jnp.zeros_like(db_ref)
            dd_ref[...] = jnp.zeros_like(dd_ref)

        y0 = y0_ref[...]
        yg = _gelu(y0)
        sg = _sigmoid(z_ref[...])
        dsv = ds_ref[...]
        dz = dsv * yg * sg * (1.0 - sg)
        dzb = dz.astype(BF16)
        dyg = dsv * sg + lax.dot_general(dzb, w_ref[...], (((1,), (1,)), ((), ())), preferred_element_type=F32)
        dy0 = dyg * _gelu_grad(y0)
        dy0_ref[...] = dy0
        dw_ref[...] += lax.dot_general(yg.astype(BF16), dzb, (((0,), (0,)), ((), ())), preferred_element_type=F32)
        db_ref[...] += jnp.sum(dz, axis=0, keepdims=True)
        uv = jnp.concatenate([ua_ref[...], ub_ref[...]], axis=1)
        dd_ref[...] += jnp.sum(dy0 * uv, axis=0, keepdims=True)

    row = pl.BlockSpec((tr, w), lambda i: (i, 0))
    vec = pl.BlockSpec((1, w), lambda i: (0, 0))
    mat = pl.BlockSpec((w, w), lambda i: (0, 0))
    return pl.pallas_call(
        body, out_shape=(jax.ShapeDtypeStruct((t, w), F32), jax.ShapeDtypeStruct((w, w), F32),
                         jax.ShapeDtypeStruct((1, w), F32), jax.ShapeDtypeStruct((1, w), F32)),
        grid=(t // tr,), in_specs=[row, row, row, *_last_cols_specs(u, w, tr), vec, mat],
        out_specs=(row, mat, vec, vec),
        compiler_params=_cparams(("arbitrary",)), name=name)(ds, y0, z, u, u, d, wglu)


def _du_combine(dy0, d, du_f, du_b, name):
    t, w = dy0.shape
    tr = _row_tile(t)

    def body(dy_ref, d_ref, a_ref, b_ref, o_ref):
        o_ref[...] = d_ref[...] * dy_ref[...] + a_ref[...] + b_ref[...]

    row = pl.BlockSpec((tr, w), lambda i: (i, 0))
    vec = pl.BlockSpec((1, w), lambda i: (0, 0))
    return pl.pallas_call(body, out_shape=jax.ShapeDtypeStruct((t, w), F32), grid=(t // tr,),
                          in_specs=[row, vec, row, row], out_specs=row, compiler_params=_cparams(("parallel",)),
                          name=name)(dy0, d, du_f, du_b)


def _ffn_fwd(x, g, wg, wu, wd, tag):
    xo, h, gate, up = _ffn_fwd_call(x, g, wg, wu, wd, f"{tag}_fwd")
    return xo, (h, gate, up)


def _ffn_bwd(dxo, dxo_b, x, g, wg, wu, wd, saved, tag):
    h, gate, up = saved
    dx, dx_b, dg, dgate, dup, act = _ffn_bwd_x_call(dxo, dxo_b, x, g, gate, up, wg, wu, wd, f"{tag}_bwd_x")
    dwg, dwu, dwd = _ffn_bwd_w_call(h, dxo_b, dgate, dup, act, f"{tag}_bwd_w")
    return dx, dx_b, dg, dwg, dwu, dwd


def _local_step(x, tgt, w, get_weights, put_grads, reduce_wide):
    t = x.shape[0]
    row = lambda a: a.reshape(1, -1)
    grads = {}

    w = dict(w)

    ssm_names = ['ssm_lambda_re', 'ssm_lambda_im', 'ssm_log_dt', 'ssm_b_re', 'ssm_b_im', 'ssm_c_re', 'ssm_c_im']
    tr3 = lambda m: jnp.swapaxes(m, 1, 2)
    fwd_ops, adj_ops, vjps = [], [], []
    for direction in range(2):
        rev = direction == 1
        prep, vjp = jax.vjp(_ssm_prep, *[w[n][direction] for n in ssm_names])
        a_re, a_im = prep[0], prep[1]
        mi_re, mi_im, mo_re, mo_im = (m.astype(BF16) for m in prep[2:])
        fwd_ops.append((mi_re, mi_im, _scan_tables(a_re, a_im, rev), mo_re, mo_im))
        adj_ops.append((tr3(mo_re), tr3(mo_im), _scan_tables(a_re, -a_im, not rev), tr3(mi_re), tr3(mi_im)))
        vjps.append(vjp)
    sink_rows = jnp.repeat(w['attn_sinks'].reshape(KV_HEADS, GQ), QBLOCK, axis=1)[:, None, :]
    bias = _attn_bias()
    prepared = sum(jnp.sum(op[:1, :1].astype(F32)) for ops in fwd_ops + adj_ops for op in ops) + sink_rows[0, 0, 0]

    w.update(get_weights('ffn1', prepared.reshape(1, 1)))
    x1, ffn1_saved = _ffn_fwd(x, w['norm_ffn1'], w['ffn1_w_gate'], w['ffn1_w_up'], w['ffn1_w_down'], "ffn1")
    w.update(get_weights('mix', x1))

    proj, h2 = _norm_mm([x1], [w['norm_mix']], w['w_in'], tb=True, res=None, name="in_proj")
    u = proj

    attn, lse = _attn_fwd_proj(proj, sink_rows, bias, "attn_fwd")

    (y_f, *states_f), (y_b, *states_b) = _scan_pair(u, fwd_ops[0], fwd_ops[1], "s5_fwd")
    ys, states = [y_f, y_b], [states_f, states_b]
    d_row = row(w['ssm_d'])
    s, y0, z = _ssm_post_fwd(u, ys[0], ys[1], d_row, w['ssm_glu_w'], row(w['ssm_glu_b']), "ssm_post")

    x2, mixed = _norm_mm([attn, s], [row(w['attn_out_norm']), row(w['ssm_out_norm'])], w['w_out'], tb=False,
                         res=x1, name="out_proj")

    w.update(get_weights('ffn2', x2))
    x3, ffn2_saved = _ffn_fwd(x2, w['norm_ffn2'], w['ffn2_w_gate'], w['ffn2_w_up'], w['ffn2_w_down'], "ffn2")

    loss, dx3, dx3_b, dgf = _loss_head(x3, row(w['final_norm']), tgt, "loss_head")
    grads['final_norm'] = dgf.reshape(w['final_norm'].shape)

    dx2, dx2_b, dg, dwg, dwu, dwd = _ffn_bwd(dx3, dx3_b, x2, w['norm_ffn2'], w['ffn2_w_gate'], w['ffn2_w_up'],
                                             w['ffn2_w_down'], ffn2_saved, "ffn2")
    grads['norm_ffn2'] = dg
    sent = put_grads('ffn2', dict(ffn2_w_gate=dwg, ffn2_w_up=dwu, ffn2_w_down=dwd))

    (dattn, _, dga), (ds, _, dgs) = _mm_rms_bwd(
        dx2_b, w['w_out'], tb=True, xs=[attn, s], gs=[row(w['attn_out_norm']), row(w['ssm_out_norm'])],
        dres=None, after=sent, name="out_proj_dx")
    dw_out = _mm(mixed, dx2_b, ta=True, out_dtype=BF16, name="out_proj_dw")[0]
    grads.update(attn_out_norm=dga, ssm_out_norm=dgs)

    dy0, dwglu, dbglu, dd = _ssm_post_bwd(ds, y0, z, u, d_row, w['ssm_glu_w'], "ssm_post_bwd")
    grads['ssm_glu_b'] = dbglu
    grads['ssm_d'] = dd.reshape(w['ssm_d'].shape)
    dparams, du_dirs = [], []
    for direction, res in enumerate(_scan_adjoint_pair(dy0, u, states, adj_ops, "s5_adj")):
        du_dir, dmir, dmii, dmor, dmoi, da = res
        du_dirs.append(du_dir)
        da_re = da[:, 0, :].reshape(-1)
        da_im = da[:, 1, :].reshape(-1)
        dparams.append(vjps[direction]((da_re, da_im, dmir, dmii, dmor, dmoi)))
    du = _du_combine(dy0, d_row, du_dirs[0], du_dirs[1], "ssm_du")
    for i, n in enumerate(ssm_names):
        grads[n] = jnp.stack([dparams[0][i], dparams[1][i]])
    wide_sum = reduce_wide(grads)

    dq, dk, dv, dsink = _attn_bwd_proj(proj, sink_rows, bias, attn, lse, dattn, "attn_bwd")
    grads['attn_sinks'] = jnp.sum(dsink.reshape(ATTN_HEADS, QBLOCK), axis=-1).reshape(w['attn_sinks'].shape)
    dproj = jnp.concatenate([dq, dk, dv, du], axis=-1).astype(BF16)

    dw_in = _mm(dproj, h2, ta=True, out_dtype=BF16, after=wide_sum, name="in_proj_dw")[0]
    sent = put_grads('mix', dict(w_in=dw_in, ssm_glu_w=dwglu, w_out=dw_out))
    ((dx1, dx1_b, dgm),) = _mm_rms_bwd(dproj, w['w_in'], tb=False, xs=[x1], gs=[w['norm_mix']], dres=dx2,
                                       after=sent, name="in_proj_dx")
    grads['norm_mix'] = dgm

    dx0, _, dg, dwg, dwu, dwd = _ffn_bwd(dx1, dx1_b, x, w['norm_ffn1'], w['ffn1_w_gate'], w['ffn1_w_up'],
                                         w['ffn1_w_down'], ffn1_saved, "ffn1")
    grads['norm_ffn1'] = dg
    put_grads('ffn1', dict(ffn1_w_gate=dwg, ffn1_w_up=dwu, ffn1_w_down=dwd))
    return loss, dx0, grads, wide_sum


HBM_SPEC = pl.BlockSpec(memory_space=pl.ANY)


def _chip_peers(x, y):
    return [(1 - x, y), (x, 1 - y), (1 - x, 1 - y)]


HBM_ONLY = pl.BlockSpec(memory_space=pltpu.HBM)
SEM_SPEC = pl.BlockSpec(memory_space=pltpu.SEMAPHORE)
EFFECT = pltpu.SideEffectType.DATAFLOW_SIDE_EFFECTING


def _place_own(srcs, slot, name):
    na = len(srcs)
    r, c = srcs[0].shape
    tr = r // 2

    def body(slot_ref, *refs):
        for a in range(na):
            refs[na + a][0] = refs[a][...]

    return pl.pallas_call(
        body, out_shape=[jax.ShapeDtypeStruct((N_CHIPS, r, c), s.dtype) for s in srcs],
        grid_spec=pltpu.PrefetchScalarGridSpec(
            num_scalar_prefetch=1, grid=(2,), in_specs=[pl.BlockSpec((tr, c), lambda i, s: (i, 0))] * na,
            out_specs=[pl.BlockSpec((1, tr, c), lambda i, s: (s[0], i, 0))] * na),
        compiler_params=_cparams(("parallel",)), name=name)(slot, *srcs)


def _chip_copies(srcs, lands, send_sems, recv_sems, scatter, landed):
    x, y, c = lax.axis_index("x"), lax.axis_index("y"), lax.axis_index("c")
    me = 2 * x + y
    out = []
    for i in range(len(srcs)):
        for j, (px, py) in enumerate(_chip_peers(x, y)):
            p = 2 * px + py
            slot = p if landed else me
            if scatter:
                src, dst = srcs[i].at[p], lands[i].at[slot]
            else:
                rows = _core_half(srcs[i].shape[0], c)
                src, dst = srcs[i].at[rows], lands[i].at[slot, rows]
            out.append(pltpu.make_async_remote_copy(src, dst, send_sems.at[3 * i + j], recv_sems.at[3 * i + j],
                                                    device_id=(px, py, c), device_id_type=MESH))
    return out


def _core_half(nrows, c):
    half = nrows // 2
    return pl.ds(pl.multiple_of(c * half, 16), half)


def _sibling_forward(lands, name):
    n = len(lands)

    def body(*refs):
        bufs = refs[n:2 * n]
        send_sems, recv_sems = refs[2 * n:]
        x, y, c = lax.axis_index("x"), lax.axis_index("y"), lax.axis_index("c")
        mine = [_core_half(b.shape[1], c) for b in bufs]
        theirs = [_core_half(b.shape[1], 1 - c) for b in bufs]
        chips = [2 * px + py for px, py in _chip_peers(x, y)]
        cps = [pltpu.make_async_remote_copy(bufs[i].at[p, mine[i]], bufs[i].at[p, mine[i]], send_sems.at[3 * i + j],
                                            recv_sems.at[3 * i + j], device_id=(x, y, 1 - c), device_id_type=MESH)
               for i in range(n) for j, p in enumerate(chips)]
        for cp in cps:
            cp.start()
        for i in range(n):
            for j, p in enumerate(chips):
                pltpu.make_async_remote_copy(bufs[i].at[p, mine[i]], bufs[i].at[p, theirs[i]], send_sems.at[3 * i + j],
                                             recv_sems.at[3 * i + j], device_id=(x, y, 1 - c),
                                             device_id_type=MESH).wait()

    return pl.pallas_call(
        body, out_shape=[jax.ShapeDtypeStruct(a.shape, a.dtype) for a in lands],
        in_specs=[HBM_SPEC] * n, out_specs=[HBM_SPEC] * n, input_output_aliases={k: k for k in range(n)},
        scratch_shapes=[pltpu.SemaphoreType.DMA((3 * n,)), pltpu.SemaphoreType.DMA((3 * n,))],
        name=name)(*lands)


def _exchange_start(groups, scatter, name, after=None):
    sizes = [len(srcs) for srcs, _ in groups]
    flat_src = [a for srcs, _ in groups for a in srcs]
    flat_land = [a for _, lands in groups for a in lands]
    n = len(flat_src)
    ng = len(groups)

    def body(*refs):
        src_refs, land_refs = refs[:n], refs[n:2 * n]
        n_in = 2 * n + (after is not None)
        sems = refs[n_in:n_in + 2 * ng]
        token_ref = refs[-1]
        off = 0
        for gi, sz in enumerate(sizes):
            for cp in _chip_copies(src_refs[off:off + sz], land_refs[off:off + sz], sems[2 * gi], sems[2 * gi + 1],
                                   scatter, landed=False):
                cp.start()
            off += sz
        token_ref[...] = jnp.zeros_like(token_ref)

    sem_shapes = []
    for sz in sizes:
        sem_shapes += [pltpu.SemaphoreType.DMA((3 * sz,)), pltpu.SemaphoreType.DMA((3 * sz,))]
    hbm = lambda a: pltpu.HBM(a.shape, a.dtype)
    res = pl.pallas_call(
        body, name=name,
        out_shape=(tuple(sem_shapes) + tuple(hbm(a) for a in flat_src) + tuple(hbm(a) for a in flat_land)
                   + (jax.ShapeDtypeStruct((SUBLANES, LANES), F32),)),
        in_specs=[HBM_ONLY] * (2 * n) + [HBM_SPEC] * (after is not None),
        out_specs=tuple([SEM_SPEC] * (2 * ng) + [HBM_ONLY] * (2 * n) + [pl.BlockSpec(memory_space=pltpu.VMEM)]),
        input_output_aliases={k: 2 * ng + k for k in range(2 * n)},
        compiler_params=pltpu.CompilerParams(has_side_effects=EFFECT),
    )(*[pltpu.with_memory_space_constraint(a, pltpu.HBM) for a in flat_src + flat_land],
      *([after] if after is not None else []))
    sems, thru_src, thru_land = res[:2 * ng], res[2 * ng:2 * ng + n], res[2 * ng + n:2 * ng + 2 * n]
    out, off = [], 0
    for gi, sz in enumerate(sizes):
        out.append((sems[2 * gi], sems[2 * gi + 1], list(thru_src[off:off + sz]), list(thru_land[off:off + sz])))
        off += sz
    return out, res[-1]


def _exchange_wait(started, after, scatter, name):
    send_sems, recv_sems, srcs, lands = started
    n = len(srcs)

    def body(*refs):
        src_refs, land_refs = refs[:n], refs[n:2 * n]
        send_ref, recv_ref = refs[2 * n], refs[2 * n + 1]
        for cp in _chip_copies(src_refs, land_refs, send_ref, recv_ref, scatter, landed=True):
            cp.wait_send()
            cp.wait_recv()

    hbm = lambda a: pltpu.HBM(a.shape, a.dtype)
    res = pl.pallas_call(
        body, name=name, out_shape=tuple(hbm(a) for a in srcs) + tuple(hbm(a) for a in lands),
        in_specs=[HBM_ONLY] * (2 * n) + [SEM_SPEC, SEM_SPEC, HBM_SPEC], out_specs=tuple([HBM_ONLY] * (2 * n)),
        input_output_aliases={k: k for k in range(2 * n)},
        compiler_params=pltpu.CompilerParams(has_side_effects=EFFECT),
    )(*srcs, *lands, send_sems, recv_sems, after)
    return list(res[:n]), list(res[n:])


def _half_swap(parts, name):
    n = len(parts)

    def body(*refs):
        ins, outs = refs[:n], refs[n:2 * n]
        send_sems, recv_sems = refs[2 * n:]
        x, y, c = lax.axis_index("x"), lax.axis_index("y"), lax.axis_index("c")
        cps = [pltpu.make_async_remote_copy(ins[i].at[k, _core_half(ins[i].shape[1], 1 - c)], outs[i].at[k],
                                            send_sems.at[N_CHIPS * i + k], recv_sems.at[N_CHIPS * i + k],
                                            device_id=(x, y, 1 - c), device_id_type=MESH)
               for i in range(n) for k in range(N_CHIPS)]
        for cp in cps:
            cp.start()
        for cp in cps:
            cp.wait()

    return pl.pallas_call(
        body, out_shape=[jax.ShapeDtypeStruct((N_CHIPS, p.shape[1] // 2, p.shape[2]), p.dtype) for p in parts],
        in_specs=[HBM_SPEC] * n, out_specs=[HBM_SPEC] * n,
        scratch_shapes=[pltpu.SemaphoreType.DMA((N_CHIPS * n,)), pltpu.SemaphoreType.DMA((N_CHIPS * n,))],
        name=name)(*parts)


def _half_add(parts, sib, slots, name):
    na = len(parts)
    _, r, c = parts[0].shape
    hr = r // 2
    tr = _row_tile(hr, 512)
    nt = hr // tr

    def body(slot_ref, *refs):
        for a in range(na):
            refs[2 * na + a][...] = (refs[2 * a][...].astype(F32) + refs[2 * a + 1][...].astype(F32)).astype(BF16)

    mine = pl.BlockSpec((1, tr, c), lambda k, i, s: (k, i + s[4] * nt, 0))
    half = pl.BlockSpec((1, tr, c), lambda k, i, s: (k, i, 0))
    args = [a for p, sb in zip(parts, sib) for a in (p, sb)]
    return pl.pallas_call(
        body, out_shape=[jax.ShapeDtypeStruct((N_CHIPS, hr, c), BF16)] * na,
        grid_spec=pltpu.PrefetchScalarGridSpec(
            num_scalar_prefetch=1, grid=(N_CHIPS, nt), in_specs=[mine, half] * na, out_specs=[half] * na),
        compiler_params=_cparams(("parallel", "parallel")), name=name)(slots, *args)


def _half_forward(arrs, name):
    n = len(arrs)

    def body(*refs):
        bufs = refs[n:2 * n]
        send_sems, recv_sems = refs[2 * n:]
        x, y, c = lax.axis_index("x"), lax.axis_index("y"), lax.axis_index("c")
        cps = [pltpu.make_async_remote_copy(b.at[_core_half(b.shape[0], c)], b.at[_core_half(b.shape[0], c)],
                                            send_sems.at[i], recv_sems.at[i], device_id=(x, y, 1 - c),
                                            device_id_type=MESH) for i, b in enumerate(bufs)]
        for cp in cps:
            cp.start()
        for i, b in enumerate(bufs):
            pltpu.make_async_remote_copy(b.at[_core_half(b.shape[0], c)], b.at[_core_half(b.shape[0], 1 - c)],
                                         send_sems.at[i], recv_sems.at[i], device_id=(x, y, 1 - c),
                                         device_id_type=MESH).wait()

    return pl.pallas_call(
        body, out_shape=[jax.ShapeDtypeStruct(a.shape, a.dtype) for a in arrs],
        in_specs=[HBM_SPEC] * n, out_specs=[HBM_SPEC] * n, input_output_aliases={k: k for k in range(n)},
        scratch_shapes=[pltpu.SemaphoreType.DMA((n,)), pltpu.SemaphoreType.DMA((n,))],
        name=name)(*arrs)


def _small_exchange(smalls, name):
    nsm = len(smalls)
    rels = [(fx, fy, fc) for fx in (0, 1) for fy in (0, 1) for fc in (0, 1)][1:]

    def body(*refs):
        sins, souts = refs[:nsm], refs[nsm:2 * nsm]
        ssend, srecv, slocal = refs[2 * nsm:]
        x, y, c = lax.axis_index("x"), lax.axis_index("y"), lax.axis_index("c")
        lin = 4 * x + 2 * y + c
        local = [pltpu.make_async_copy(sins[i], souts[i].at[lin], slocal.at[i]) for i in range(nsm)]
        for cp in local:
            cp.start()
        for i in range(nsm):
            for j, (fx, fy, fc) in enumerate(rels):
                pltpu.make_async_remote_copy(sins[i], souts[i].at[lin], ssend.at[i, j], srecv.at[i, j],
                                             device_id=(x ^ fx, y ^ fy, c ^ fc), device_id_type=MESH).start()
        for i in range(nsm):
            for j, (fx, fy, fc) in enumerate(rels):
                src = 4 * (x ^ fx) + 2 * (y ^ fy) + (c ^ fc)
                pltpu.make_async_remote_copy(sins[i], souts[i].at[src], ssend.at[i, j], srecv.at[i, j],
                                             device_id=(x ^ fx, y ^ fy, c ^ fc), device_id_type=MESH).wait()
        for cp in local:
            cp.wait()

    return pl.pallas_call(
        body, out_shape=[jax.ShapeDtypeStruct((N_DEV,) + s.shape, s.dtype) for s in smalls],
        in_specs=[HBM_SPEC] * nsm, out_specs=[HBM_SPEC] * nsm,
        scratch_shapes=[pltpu.SemaphoreType.DMA((nsm, 7)), pltpu.SemaphoreType.DMA((nsm, 7)),
                        pltpu.SemaphoreType.DMA((nsm,))],
        name=name)(*smalls)


def _sum_parts(parts, recv, slots, name):
    na = len(parts)
    _, r, c = parts[0].shape
    tr = _row_tile(r, 192)

    def body(slot_ref, *refs):
        for a in range(na):
            own_ref, r0_ref, r1_ref, r2_ref = refs[4 * a:4 * a + 4]
            refs[4 * na + a][...] = ((own_ref[0].astype(F32) + r0_ref[0].astype(F32))
                                     + (r1_ref[0].astype(F32) + r2_ref[0].astype(F32)))

    blk = lambda k: pl.BlockSpec((1, tr, c), lambda i, s, k=k: (s[k], i, 0))
    out_blk = pl.BlockSpec((tr, c), lambda i, s: (i + s[4] * (r // tr), 0))
    args = [a for p, rv in zip(parts, recv) for a in (p, rv, rv, rv)]
    return pl.pallas_call(
        body, out_shape=[jax.ShapeDtypeStruct((2 * r, c), F32)] * na,
        grid_spec=pltpu.PrefetchScalarGridSpec(
            num_scalar_prefetch=1, grid=(r // tr,), in_specs=[blk(0), blk(1), blk(2), blk(3)] * na,
            out_specs=[out_blk] * na),
        compiler_params=_cparams(("parallel",)), name=name)(slots, *args)


ALL_PEERS = [(fx, fy, fc) for fx in (0, 1) for fy in (0, 1) for fc in (0, 1)][1:]


def _all8_copies(src, land, send_sems, recv_sems, landed):
    x, y, c = lax.axis_index("x"), lax.axis_index("y"), lax.axis_index("c")
    lin = 4 * x + 2 * y + c
    out = []
    for j, (fx, fy, fc) in enumerate(ALL_PEERS):
        px, py, pc = x ^ fx, y ^ fy, c ^ fc
        slot = (4 * px + 2 * py + pc) if landed else lin
        out.append(pltpu.make_async_remote_copy(src, land.at[slot], send_sems.at[j], recv_sems.at[j],
                                                device_id=(px, py, pc), device_id_type=MESH))
    return out


def _all8_start(src, name):
    land = lax.empty((N_DEV,) + src.shape, src.dtype)

    def body(src_ref, land_ref, send_sems, recv_sems, src_thru, land_thru, token_ref):
        for cp in _all8_copies(src_ref, land_ref, send_sems, recv_sems, landed=False):
            cp.start()
        token_ref[...] = jnp.zeros_like(token_ref)

    hbm = lambda a: pltpu.HBM(a.shape, a.dtype)
    res = pl.pallas_call(
        body, name=name,
        out_shape=(pltpu.SemaphoreType.DMA((7,)), pltpu.SemaphoreType.DMA((7,)), hbm(src), hbm(land),
                   jax.ShapeDtypeStruct((SUBLANES, LANES), F32)),
        in_specs=[HBM_ONLY, HBM_ONLY],
        out_specs=(SEM_SPEC, SEM_SPEC, HBM_ONLY, HBM_ONLY, pl.BlockSpec(memory_space=pltpu.VMEM)),
        input_output_aliases={0: 2, 1: 3}, compiler_params=pltpu.CompilerParams(has_side_effects=EFFECT),
    )(pltpu.with_memory_space_constraint(src, pltpu.HBM), pltpu.with_memory_space_constraint(land, pltpu.HBM))
    return res[:4], res[4]


def _all8_wait(started, after, name):
    send_sems, recv_sems, src, land = started

    def body(src_ref, land_ref, send_ref, recv_ref, after_ref, src_out, land_out):
        for cp in _all8_copies(src_ref, land_ref, send_ref, recv_ref, landed=True):
            cp.wait_send()
            cp.wait_recv()

    hbm = lambda a: pltpu.HBM(a.shape, a.dtype)
    res = pl.pallas_call(
        body, name=name, out_shape=(hbm(src), hbm(land)),
        in_specs=[HBM_ONLY, HBM_ONLY, SEM_SPEC, SEM_SPEC, HBM_SPEC], out_specs=(HBM_ONLY, HBM_ONLY),
        input_output_aliases={0: 0, 1: 1}, compiler_params=pltpu.CompilerParams(has_side_effects=EFFECT),
    )(src, land, send_sems, recv_sems, after)
    return res[0], res[1]


def _sum8(own, land, lin, name):
    r, c = own.shape
    tr = _row_tile(r)

    def body(lin_ref, own_ref, land_ref, o_ref):
        me = lin_ref[0]
        acc = None
        for k in range(N_DEV):
            term = jnp.where(me == k, own_ref[...], land_ref[k])
            acc = term if acc is None else acc + term
        o_ref[...] = acc

    return pl.pallas_call(
        body, out_shape=jax.ShapeDtypeStruct((r, c), F32),
        grid_spec=pltpu.PrefetchScalarGridSpec(
            num_scalar_prefetch=1, grid=(r // tr,),
            in_specs=[pl.BlockSpec((tr, c), lambda i, s: (i, 0)), pl.BlockSpec((N_DEV, tr, c), lambda i, s: (0, i, 0))],
            out_specs=pl.BlockSpec((tr, c), lambda i, s: (i, 0))),
        compiler_params=_cparams(("parallel",)), name=name)(lin, own, land)


def _adamw_math(w, m, v, g):
    nm = ADAM_B1 * m + (1.0 - ADAM_B1) * g
    nv = ADAM_B2 * v + (1.0 - ADAM_B2) * (g * g)
    m_hat = nm * (1.0 / (1.0 - ADAM_B1 ** ADAM_STEP))
    v_hat = nv * (1.0 / (1.0 - ADAM_B2 ** ADAM_STEP))
    return -ADAM_LR * (m_hat / (jnp.sqrt(v_hat) + ADAM_EPS) + ADAM_WD * w), nm, nv


def _adamw(ws, ms, vs, gs, name):
    na = len(ws)
    r, c = ws[0].shape
    tr = _row_tile(r)

    def body(*refs):
        for a in range(na):
            w_ref, m_ref, v_ref, g_ref = refs[4 * a:4 * a + 4]
            d_ref, nm_ref, nv_ref = refs[4 * na + 3 * a:4 * na + 3 * a + 3]
            d_ref[...], nm_ref[...], nv_ref[...] = _adamw_math(w_ref[...], m_ref[...], v_ref[...], g_ref[...])

    blk = pl.BlockSpec((tr, c), lambda i: (i, 0))
    sh = jax.ShapeDtypeStruct((r, c), F32)
    args = [a for group in zip(ws, ms, vs, gs) for a in group]
    res = pl.pallas_call(body, out_shape=[sh] * (3 * na), grid=(r // tr,), in_specs=[blk] * (4 * na),
                         out_specs=[blk] * (3 * na), compiler_params=_cparams(("parallel",)), name=name)(*args)
    return [tuple(res[3 * a:3 * a + 3]) for a in range(na)]


def _adamw_small(ws, ms, vs, alls, split, name):
    n = len(ws)
    lead = split if split is not None else ()
    nl = len(lead)
    nslots = alls[0].shape[0]

    def blocks(shape):
        if split is None:
            return tuple(shape), (lambda *g: (0,) * len(shape))
        blk = (shape[0], shape[1] // lead[0], shape[2] // lead[1]) + tuple(shape[3:])
        return blk, (lambda *g: (0, g[0], g[1]) + (0,) * (len(shape) - 3))

    def body(*refs):
        w_refs, m_refs, v_refs, a_refs = (refs[k * n:(k + 1) * n] for k in range(4))
        g_refs, d_refs, nm_refs, nv_refs = (refs[(4 + k) * n:(5 + k) * n] for k in range(4))
        k = pl.program_id(nl)
        for i in range(n):
            @pl.when(k == 0)
            def _(i=i):
                g_refs[i][...] = a_refs[i][0]

            @pl.when(k > 0)
            def _(i=i):
                g_refs[i][...] += a_refs[i][0]

            @pl.when(k == nslots - 1)
            def _(i=i):
                d_refs[i][...], nm_refs[i][...], nv_refs[i][...] = _adamw_math(
                    w_refs[i][...], m_refs[i][...], v_refs[i][...], g_refs[i][...])

    specs, aspecs, shapes = [], [], []
    for wa in ws:
        blk, imap = blocks(wa.shape)
        specs.append(pl.BlockSpec(blk, imap))
        aspecs.append(pl.BlockSpec((1,) + blk, (lambda *g, imap=imap: (g[nl],) + imap(*g))))
        shapes.append(jax.ShapeDtypeStruct(wa.shape, F32))
    res = pl.pallas_call(
        body, out_shape=shapes * 4, grid=tuple(lead) + (nslots,), in_specs=specs * 3 + aspecs,
        out_specs=specs * 4, compiler_params=_cparams(("parallel",) * nl + ("arbitrary",)),
        name=name)(*ws, *ms, *vs, *alls)
    return res[:n], res[n:2 * n], res[2 * n:3 * n], res[3 * n:]


def kernel(x, norm_ffn1, ffn1_w_gate, ffn1_w_up, ffn1_w_down, norm_mix, w_in, attn_sinks, ssm_lambda_re, ssm_lambda_im, ssm_log_dt, ssm_b_re, ssm_b_im, ssm_c_re, ssm_c_im, ssm_d, ssm_glu_w, ssm_glu_b, attn_out_norm, ssm_out_norm, w_out, norm_ffn2, ffn2_w_gate, ffn2_w_up, ffn2_w_down, final_norm, loss_target, m_norm_ffn1, m_ffn1_w_gate, m_ffn1_w_up, m_ffn1_w_down, m_norm_mix, m_w_in, m_attn_sinks, m_ssm_lambda_re, m_ssm_lambda_im, m_ssm_log_dt, m_ssm_b_re, m_ssm_b_im, m_ssm_c_re, m_ssm_c_im, m_ssm_d, m_ssm_glu_w, m_ssm_glu_b, m_attn_out_norm, m_ssm_out_norm, m_w_out, m_norm_ffn2, m_ffn2_w_gate, m_ffn2_w_up, m_ffn2_w_down, m_final_norm, v_norm_ffn1, v_ffn1_w_gate, v_ffn1_w_up, v_ffn1_w_down, v_norm_mix, v_w_in, v_attn_sinks, v_ssm_lambda_re, v_ssm_lambda_im, v_ssm_log_dt, v_ssm_b_re, v_ssm_b_im, v_ssm_c_re, v_ssm_c_im, v_ssm_d, v_ssm_glu_w, v_ssm_glu_b, v_attn_out_norm, v_ssm_out_norm, v_w_out, v_norm_ffn2, v_ffn2_w_gate, v_ffn2_w_up, v_ffn2_w_down, v_final_norm):
    given = dict(locals())
    wts = {n: given[n] for n in WEIGHTS}

    order = [g for g in GROUPS]
    cx, cy = lax.axis_index("x"), lax.axis_index("y")
    slots = jnp.stack([2 * cx + cy, 2 * (1 - cx) + cy, 2 * cx + 1 - cy, 2 * (1 - cx) + 1 - cy,
                       lax.axis_index("c")]).astype(jnp.int32)
    def view(a, n):
        if n in TRANSPOSED:
            return jnp.swapaxes(a[0], 0, 1)
        if n in BIG:
            return a[0]
        if n in ('ssm_b_re', 'ssm_b_im'):
            return jnp.swapaxes(a, -1, -2)
        return a.reshape(1, -1) if a.ndim == 1 else a

    def unview(a, n):
        if n in TRANSPOSED:
            return jnp.swapaxes(a, 0, 1)[None]
        if n in ('ssm_b_re', 'ssm_b_im'):
            return jnp.swapaxes(a, -1, -2)
        return a.reshape(wts[n].shape)

    started, gather_token = {}, None
    for g in order:
        shards = [view(wts[n], n).astype(BF16) for n in GROUPS[g]]
        if len({s.shape for s in shards}) == 1:
            placed = _place_own(shards, slots, f"weights_place_{g}")
        else:
            placed = [_place_own([s], slots, f"weights_place_{n}")[0] for n, s in zip(GROUPS[g], shards)]
        st, gather_token = _exchange_start([(shards, placed)], False, f"weights_start_{g}", after=gather_token)
        started[g] = st[0]

    def get_weights(group, after):
        if group == order[0]:
            after = after + gather_token[:1, :1]
        _, lands = _exchange_wait(started[group], after, False, f"weights_wait_{group}")
        lands = _sibling_forward(lands, f"weights_forward_{group}")
        out = dict(zip(GROUPS[group], lands))
        for n in ('w_in', 'ssm_glu_w', 'w_out'):
            if n in out:
                out[n] = out[n].reshape(-1, out[n].shape[-1])
        return out

    sent, tokens = {}, {}

    def put_grads(group, gd):
        parts = []
        for n in GROUPS[group]:
            g = gd[n]
            if g.ndim == 2:
                g = g.reshape(N_CHIPS, g.shape[0] // N_CHIPS, g.shape[1])
            parts.append(g.astype(BF16))
        sib = _half_swap(parts, f"grads_half_swap_{group}")
        same = len({p.shape for p in parts}) == 1
        batches = [list(range(len(parts)))] if same else [[i] for i in range(len(parts))]
        halves = [None] * len(parts)
        for b in batches:
            res = _half_add([parts[i] for i in b], [sib[i] for i in b], slots, f"grads_half_add_{GROUPS[group][b[0]]}")
            for i, h in zip(b, res):
                halves[i] = h
        parts = halves
        lands = [lax.empty(p.shape, p.dtype) for p in parts]
        started_g, tokens[group] = _exchange_start([(parts, lands)], True, f"grads_start_{group}")
        sent[group] = started_g[0]
        return tokens[group]

    w = {n: (wts[n][0] if wts[n].ndim > 1 else wts[n]) for n in SMALL}
    w['norm_ffn1'], w['norm_mix'], w['norm_ffn2'] = wts['norm_ffn1'], wts['norm_mix'], wts['norm_ffn2']
    w['ssm_b_re'], w['ssm_b_im'] = view(wts['ssm_b_re'], 'ssm_b_re')[0], view(wts['ssm_b_im'], 'ssm_b_im')[0]
    w['ssm_log_dt'] = w['ssm_log_dt'] + gather_token[0, 0]
    wide =['ssm_b_re', 'ssm_b_im', 'ssm_c_re', 'ssm_c_im']

    wide_started = []

    def reduce_wide(gd):
        packed = jnp.concatenate([gd[n].reshape(-1, LANES) for n in wide])
        started_w, token = _all8_start(packed, "small_grads_start")
        wide_started.append(started_w)
        return token

    loss_row, dx, grads, _ = _local_step(x[0], loss_target[0], w, get_weights, put_grads, reduce_wide)

    out_g, out_d, out_m, out_v = {}, {}, {}, {}

    def finish(group, after):
        names = GROUPS[group]
        parts, recv = _exchange_wait(sent[group], after, True, f"grads_wait_{group}")
        same = len({p.shape for p in parts}) == 1
        batches = [list(range(len(names)))] if same else [[i] for i in range(len(names))]
        sums = [None] * len(names)
        for b in batches:
            res = _sum_parts([parts[i] for i in b], [recv[i] for i in b], slots, f"grad_sum_{names[b[0]]}")
            for i, sm in zip(b, res):
                sums[i] = sm
        full = _half_forward(sums, f"grad_half_forward_{group}")
        for b in batches:
            res = _adamw([view(wts[names[i]], names[i]) for i in b], [view(given['m_' + names[i]], names[i]) for i in b],
                         [view(given['v_' + names[i]], names[i]) for i in b], [full[i] for i in b],
                         f"adamw_{names[b[0]]}")
            for i, (d, nm, nv) in zip(b, res):
                n = names[i]
                out_g[n], out_d[n], out_m[n], out_v[n] = (unview(a, n) for a in (full[i], d, nm, nv))
        return nv

    done = finish('ffn2', tokens['ffn1'])
    done = finish('mix', done)

    nat = {n: view(wts[n], n).shape for n in SMALL}
    narrow = [n for n in SMALL if n not in wide]
    alls = list(_small_exchange([grads[n].reshape(nat[n]) for n in narrow] + [loss_row], "small_grads_allgather"))
    loss = jnp.sum(alls.pop()[:, 0, 0])
    own_w, land_w = _all8_wait(wide_started[0], done, "small_grads_wait")
    lin = (4 * cx + 2 * cy + lax.axis_index("c")).astype(jnp.int32).reshape(1)
    wide_sum = _sum8(own_w, land_w, lin, "small_grads_sum")
    rows = wide_sum.shape[0] // len(wide)
    wide_g = [wide_sum[i * rows:(i + 1) * rows].reshape((1,) + nat[n]) for i, n in enumerate(wide)]
    for group, gs, split, tag in ((narrow, alls, None, "adamw_small"), (wide, wide_g, (2, 4), "adamw_ssm_bc")):
        res = _adamw_small([view(wts[n], n) for n in group], [view(given['m_' + n], n) for n in group],
                           [view(given['v_' + n], n) for n in group], gs, split, tag)
        for dst, vals in zip((out_g, out_d, out_m, out_v), res):
            for n, a in zip(group, vals):
                dst[n] = unview(a, n)

    finish('ffn1', out_v['norm_ffn1'][:, :1] + out_v['ssm_c_re'].reshape(1, -1)[:, :1] + done[:1, :1] + loss)

    return (loss, dx[None], *[out_g[n] for n in WEIGHTS], *[out_d[n] for n in WEIGHTS],
            *[out_m[n] for n in WEIGHTS], *[out_v[n] for n in WEIGHTS])
```

```python
import functools
import math

import numpy as np
import jax
import jax.numpy as jnp
from jax import lax
from jax.experimental import pallas as pl
from jax.experimental.pallas import tpu as pltpu

F32 = jnp.float32
BF16 = jnp.bfloat16
MESH = pl.DeviceIdType.MESH

EPS = 1e-6
NEG_INF = -1e30
LAMBDA_RE_MAX = -1e-4
ATTN_HEADS = 8
KV_HEADS = 2
GQ = ATTN_HEADS // KV_HEADS
HEAD_DIM = 64
ATTN_WIDTH = 512
KV_WIDTH = 128
WINDOW = 128
QBLOCK = 128
SSM_WIDTH = 512
SSM_GROUPS = 32
SSM_CH = 16
SSM_STATE = 64
N_STRIPS = 4
STRIP_IN = SSM_WIDTH // N_STRIPS
STRIP_ST = SSM_GROUPS * SSM_STATE // N_STRIPS
SUBLANES = 8
LANES = 128
N_CHIPS = 4
N_DEV = 8

ADAM_LR = 0.001
ADAM_B1 = 0.9
ADAM_B2 = 0.999
ADAM_EPS = 1e-08
ADAM_WD = 0.01
ADAM_STEP = 10

VMEM_LIMIT = 48 * 1024 * 1024

WEIGHTS = ['norm_ffn1', 'ffn1_w_gate', 'ffn1_w_up', 'ffn1_w_down', 'norm_mix', 'w_in', 'attn_sinks',
           'ssm_lambda_re', 'ssm_lambda_im', 'ssm_log_dt', 'ssm_b_re', 'ssm_b_im', 'ssm_c_re', 'ssm_c_im',
           'ssm_d', 'ssm_glu_w', 'ssm_glu_b', 'attn_out_norm', 'ssm_out_norm', 'w_out', 'norm_ffn2',
           'ffn2_w_gate', 'ffn2_w_up', 'ffn2_w_down', 'final_norm']
BIG = ['ffn1_w_gate', 'ffn1_w_up', 'ffn1_w_down', 'w_in', 'ssm_glu_w', 'w_out',
       'ffn2_w_gate', 'ffn2_w_up', 'ffn2_w_down']
SMALL = [n for n in WEIGHTS if n not in BIG]
TRANSPOSED = ['ffn1_w_gate', 'ffn1_w_up', 'w_in', 'ffn2_w_gate', 'ffn2_w_up']
GROUPS = {'ffn1': ['ffn1_w_gate', 'ffn1_w_up', 'ffn1_w_down'],
          'mix': ['w_in', 'ssm_glu_w', 'w_out'],
          'ffn2': ['ffn2_w_gate', 'ffn2_w_up', 'ffn2_w_down']}


def _cparams(sem=None):
    return pltpu.CompilerParams(dimension_semantics=sem, vmem_limit_bytes=VMEM_LIMIT)


def _tile(n, pref):
    if n <= pref:
        return n
    for t in (pref, pref // 2, pref // 4):
        if t % LANES == 0 and n % t == 0:
            return t
    return n


def _sigmoid(x):
    return 1.0 / (1.0 + jnp.exp(-x))


def _sigmoid_tanh(x):
    return 0.5 * jnp.tanh(0.5 * x) + 0.5


def _mm(a, b, *, ta=False, tb=False, reduce_s=False, res=None, scale=1.0, out_dtype=F32, after=None, name):
    a3 = a if a.ndim == 3 else a[None]
    b3 = b if b.ndim == 3 else b[None]
    sa, sb = a3.shape[0], b3.shape[0]
    ns = max(sa, sb)
    (kk, m) = a3.shape[1:] if ta else a3.shape[1:][::-1]
    (n, kb) = b3.shape[1:] if tb else b3.shape[1:][::-1]
    assert kk == kb, (a3.shape, b3.shape)
    tm, tn, tk = _tile(m, 1024), _tile(n, 1024), _tile(kk, 2048)
    nm, nn, nk = m // tm, n // tn, kk // tk
    has_res = res is not None
    single = nk == 1 and not (reduce_s and ns > 1)

    if reduce_s:
        grid = (nm, nn, ns, nk)
        ids = lambda i, j, s, k: (s, i, j, k)
        sem = ("parallel", "parallel", "arbitrary", "arbitrary")
    else:
        grid = (ns, nm, nn, nk)
        ids = lambda s, i, j, k: (s, i, j, k)
        sem = ("parallel", "parallel", "parallel", "arbitrary")

    def a_map(*g):
        s, i, j, k = ids(*g)
        s = s if sa > 1 else 0
        return (s, k, i) if ta else (s, i, k)

    def b_map(*g):
        s, i, j, k = ids(*g)
        s = s if sb > 1 else 0
        return (s, j, k) if tb else (s, k, j)

    def o_map(*g):
        s, i, j, k = ids(*g)
        return (i, j) if reduce_s else (s, i, j)

    a_blk = (1, tk, tm) if ta else (1, tm, tk)
    b_blk = (1, tn, tk) if tb else (1, tk, tn)
    dims = (((0 if ta else 1,), (1 if tb else 0,)), ((), ()))

    def body(*refs):
        a_ref, b_ref = refs[0], refs[1]
        r_ref = refs[2] if has_res else None
        o_ref = refs[2 + has_res + (after is not None)]
        acc_ref = None if single else refs[-1]
        s, _, _, k = ids(*[pl.program_id(d) for d in range(4)])
        prod = lax.dot_general(a_ref[0].astype(BF16), b_ref[0].astype(BF16), dims, preferred_element_type=F32)

        def finish(out):
            if scale != 1.0:
                out = out * scale
            if has_res:
                out = r_ref[...].reshape(out.shape) + out
            o_ref[...] = out.astype(out_dtype).reshape(o_ref.shape)

        if single:
            finish(prod)
            return
        if reduce_s:
            first = jnp.logical_and(s == 0, k == 0)
            last = jnp.logical_and(s == ns - 1, k == nk - 1)
        else:
            first, last = k == 0, k == nk - 1

        acc_ref[...] = prod + jnp.where(first, 0.0, acc_ref[...])

        @pl.when(last)
        def _():
            finish(acc_ref[...])

    in_specs = [pl.BlockSpec(a_blk, a_map), pl.BlockSpec(b_blk, b_map)]
    args = [a3, b3]
    if reduce_s:
        out_shape = jax.ShapeDtypeStruct((m, n), out_dtype)
        o_spec = pl.BlockSpec((tm, tn), o_map)
    else:
        out_shape = jax.ShapeDtypeStruct((ns, m, n), out_dtype)
        o_spec = pl.BlockSpec((1, tm, tn), o_map)
    if has_res:
        assert res.shape == out_shape.shape
        in_specs.append(o_spec)
        args.append(res)
    if after is not None:
        in_specs.append(HBM_SPEC)
        args.append(after)
    return pl.pallas_call(body, out_shape=out_shape, grid=grid, in_specs=in_specs, out_specs=o_spec,
                          scratch_shapes=[] if single else [pltpu.VMEM((tm, tn), F32)],
                          compiler_params=_cparams(sem), name=name)(*args)


def _row_tile(t, cap=256):
    for step in (16, SUBLANES):
        for tr in range(min(cap, t) // step * step, 0, -step):
            if t % tr == 0:
                return tr
    return t


def _norm_mm(xs, gs, w, *, tb, res, name):
    nx = len(xs)
    t = xs[0].shape[0]
    widths = [x.shape[1] for x in xs]
    k = sum(widths)
    n = w.shape[0] if tb else w.shape[1]
    tm, tn = _tile(t, 512), _tile(n, 512)
    has_res = res is not None

    def body(*refs):
        x_refs, g_refs, w_ref = refs[:nx], refs[nx:2 * nx], refs[2 * nx]
        r_ref = refs[2 * nx + 1] if has_res else None
        o_ref, h_ref, h_sc = refs[2 * nx + 1 + has_res:]

        @pl.when(pl.program_id(1) == 0)
        def _():
            off = 0
            for x_ref, g_ref, wd in zip(x_refs, g_refs, widths):
                xv = x_ref[...]
                r = lax.rsqrt(jnp.mean(xv * xv, axis=-1, keepdims=True) + EPS)
                h_sc[:, off:off + wd] = (xv * r * g_ref[...]).astype(BF16)
                off += wd
            h_ref[...] = h_sc[...]

        prod = lax.dot_general(h_sc[...], w_ref[...], NT_DIMS if tb else (((1,), (0,)), ((), ())),
                               preferred_element_type=F32)
        o_ref[...] = r_ref[...] + prod if has_res else prod

    in_specs = [pl.BlockSpec((tm, wd), lambda i, j: (i, 0)) for wd in widths]
    in_specs += [pl.BlockSpec((1, wd), lambda i, j: (0, 0)) for wd in widths]
    in_specs.append(pl.BlockSpec((tn, k), lambda i, j: (j, 0)) if tb else pl.BlockSpec((k, tn), lambda i, j: (0, j)))
    tile = pl.BlockSpec((tm, tn), lambda i, j: (i, j))
    if has_res:
        in_specs.append(tile)
    return pl.pallas_call(
        body, out_shape=(jax.ShapeDtypeStruct((t, n), F32), jax.ShapeDtypeStruct((t, k), BF16)),
        grid=(t // tm, n // tn), in_specs=in_specs,
        out_specs=(tile, pl.BlockSpec((tm, k), lambda i, j: (i, 0))),
        scratch_shapes=[pltpu.VMEM((tm, k), BF16)], compiler_params=_cparams(("parallel", "arbitrary")),
        name=name)(*xs, *gs, w, *([res] if has_res else []))


def _rms_bwd_rows(xv, gv, dhv):
    r = lax.rsqrt(jnp.mean(xv * xv, axis=-1, keepdims=True) + EPS)
    nrm = xv * r
    dn = dhv * gv
    return r * (dn - nrm * jnp.mean(dn * nrm, axis=-1, keepdims=True)), dhv * nrm


def _mm_rms_bwd(a, b, *, tb, xs, gs, dres, after, name):
    nx = len(xs)
    t, k = a.shape
    widths = [x.shape[1] for x in xs]
    n = sum(widths)
    assert n == (b.shape[0] if tb else b.shape[1])
    tm = _tile(t, 512)
    has_res, has_after = dres is not None, after is not None

    def body(*refs):
        a_ref, b_ref = refs[0], refs[1]
        x_refs, g_refs = refs[2:2 + nx], refs[2 + nx:2 + 2 * nx]
        r_ref = refs[2 + 2 * nx] if has_res else None
        outs = refs[2 + 2 * nx + has_res + has_after:]
        dh = lax.dot_general(a_ref[...], b_ref[...], NT_DIMS if tb else (((1,), (0,)), ((), ())),
                             preferred_element_type=F32)
        off = 0
        for i, wd in enumerate(widths):
            dx_ref, dxb_ref, dg_ref = outs[3 * i:3 * i + 3]
            dx, dgs = _rms_bwd_rows(x_refs[i][...], g_refs[i][...], dh[:, off:off + wd])
            if has_res:
                dx = dx + r_ref[...]
            dx_ref[...] = dx
            dxb_ref[...] = dx.astype(BF16)
            part = jnp.sum(dgs, axis=0, keepdims=True)
            dg_ref[...] = part + jnp.where(pl.program_id(0) > 0, dg_ref[...], 0.0)
            off += wd

    in_specs = [pl.BlockSpec((tm, k), lambda i: (i, 0)), pl.BlockSpec(b.shape, lambda i: (0, 0))]
    in_specs += [pl.BlockSpec((tm, wd), lambda i: (i, 0)) for wd in widths]
    in_specs += [pl.BlockSpec((1, wd), lambda i: (0, 0)) for wd in widths]
    args = [a, b, *xs, *gs]
    if has_res:
        in_specs.append(pl.BlockSpec((tm, widths[0]), lambda i: (i, 0)))
        args.append(dres)
    if has_after:
        in_specs.append(HBM_SPEC)
        args.append(after)
    out_shape, out_specs = [], []
    for wd in widths:
        out_shape += [jax.ShapeDtypeStruct((t, wd), F32), jax.ShapeDtypeStruct((t, wd), BF16),
                      jax.ShapeDtypeStruct((1, wd), F32)]
        out_specs += [pl.BlockSpec((tm, wd), lambda i: (i, 0)), pl.BlockSpec((tm, wd), lambda i: (i, 0)),
                      pl.BlockSpec((1, wd), lambda i: (0, 0))]
    res = pl.pallas_call(body, out_shape=out_shape, grid=(t // tm,), in_specs=in_specs, out_specs=out_specs,
                         compiler_params=_cparams(("arbitrary",)), name=name)(*args)
    return [tuple(res[3 * i:3 * i + 3]) for i in range(nx)]


FFN_ROWS = 512
FFN_FWD_ROWS = 1024
FFN_SPLIT = 2
FFN_W_ROWS = 1024
SCAN_ROWS = 256


NT_DIMS = (((1,), (1,)), ((), ()))
TN_DIMS = (((0,), (0,)), ((), ()))


def _ffn_fwd_call(x, g, wg, wu, wd, name):
    t, d = x.shape
    ns, f, _ = wg.shape
    tm = _tile(t, FFN_FWD_ROWS)

    def body(x_ref, g_ref, wg_ref, wu_ref, wd_ref, xo_ref, h_ref, gate_ref, up_ref, h_sc, acc_ref):
        s = pl.program_id(1)

        @pl.when(s == 0)
        def _():
            xv = x_ref[...]
            r = lax.rsqrt(jnp.mean(xv * xv, axis=-1, keepdims=True) + EPS)
            hb = (xv * r * g_ref[...]).astype(BF16)
            h_sc[...] = hb
            h_ref[...] = hb

        for r0 in range(0, tm, tm // FFN_SPLIT):
            rows = slice(r0, r0 + tm // FFN_SPLIT)
            hb = h_sc[rows, :]
            gate = lax.dot_general(hb, wg_ref[0], NT_DIMS, preferred_element_type=F32)
            up = lax.dot_general(hb, wu_ref[0], NT_DIMS, preferred_element_type=F32)
            gate_ref[0, rows, :] = gate.astype(BF16)
            up_ref[0, rows, :] = up.astype(BF16)
            act = (gate * _sigmoid_tanh(gate) * up).astype(BF16)
            prod = jnp.dot(act, wd_ref[0], preferred_element_type=F32)
            acc_ref[rows, :] = prod + jnp.where(s > 0, acc_ref[rows, :], 0.0)

        @pl.when(s == ns - 1)
        def _():
            xo_ref[...] = x_ref[...] + 0.5 * acc_ref[...]

    row = pl.BlockSpec((tm, d), lambda i, s: (i, 0))
    vec = pl.BlockSpec((1, d), lambda i, s: (0, 0))
    wrow = pl.BlockSpec((1, f, d), lambda i, s: (s, 0, 0))
    hid = pl.BlockSpec((1, tm, f), lambda i, s: (s, i, 0))
    hid_sh = jax.ShapeDtypeStruct((ns, t, f), BF16)
    return pl.pallas_call(
        body, out_shape=(jax.ShapeDtypeStruct((t, d), F32), jax.ShapeDtypeStruct((t, d), BF16), hid_sh, hid_sh),
        grid=(t // tm, ns), in_specs=[row, vec, wrow, wrow, wrow], out_specs=(row, row, hid, hid),
        scratch_shapes=[pltpu.VMEM((tm, d), BF16), pltpu.VMEM((tm, d), F32)],
        compiler_params=_cparams(("parallel", "arbitrary")), name=name)(x, g, wg, wu, wd)


def _ffn_bwd_x_call(dxo, dxo_b, x, g, gate, up, wg, wu, wd, name):
    t, d = x.shape
    ns, f, _ = wg.shape
    tm = _tile(t, FFN_ROWS)

    def body(dxo_ref, dxb_ref, x_ref, g_ref, gate_ref, up_ref, wg_ref, wu_ref, wd_ref,
             dx_ref, dxob_ref, dgn_ref, dgate_ref, dup_ref, act_ref, dh_ref):
        s, i = pl.program_id(0), pl.program_id(1)
        base = pl.multiple_of(i * tm, tm)
        for r0 in range(0, tm, tm // FFN_SPLIT):
            rows = slice(r0, r0 + tm // FFN_SPLIT)
            acc_rows = pl.ds(base + r0, tm // FFN_SPLIT)
            dact = lax.dot_general(dxb_ref[rows, :], wd_ref[0], NT_DIMS, preferred_element_type=F32) * 0.5
            gv = gate_ref[0, rows, :].astype(F32)
            uv = up_ref[0, rows, :].astype(F32)
            sg = _sigmoid_tanh(gv)
            silu = gv * sg
            act_ref[0, rows, :] = (silu * uv).astype(BF16)
            dub = (dact * silu).astype(BF16)
            dgb = (dact * uv * sg * (1.0 + gv * (1.0 - sg))).astype(BF16)
            dup_ref[0, rows, :] = dub
            dgate_ref[0, rows, :] = dgb
            prod = (jnp.dot(dgb, wg_ref[0], preferred_element_type=F32)
                    + jnp.dot(dub, wu_ref[0], preferred_element_type=F32))

            dh_ref[acc_rows, :] = prod + jnp.where(s > 0, dh_ref[acc_rows, :], 0.0)

        @pl.when(jnp.logical_and(i == 0, s == 0))
        def _():
            dgn_ref[...] = jnp.zeros_like(dgn_ref)

        @pl.when(s == ns - 1)
        def _():
            dx, dgs = _rms_bwd_rows(x_ref[...], g_ref[...], dh_ref[pl.ds(base, tm), :])
            dx = dx + dxo_ref[...]
            dx_ref[...] = dx
            dxob_ref[...] = dx.astype(BF16)
            dgn_ref[...] += jnp.sum(dgs, axis=0, keepdims=True)

    last_only = lambda s, i: (jnp.where(s == ns - 1, i, 0), 0)
    row_last = pl.BlockSpec((tm, d), last_only)
    row = pl.BlockSpec((tm, d), lambda s, i: (i, 0))
    vec = pl.BlockSpec((1, d), lambda s, i: (0, 0))
    wrow = pl.BlockSpec((1, f, d), lambda s, i: (s, 0, 0))
    hid = pl.BlockSpec((1, tm, f), lambda s, i: (s, i, 0))
    hid_sh = jax.ShapeDtypeStruct((ns, t, f), BF16)
    return pl.pallas_call(
        body,
        out_shape=(jax.ShapeDtypeStruct((t, d), F32), jax.ShapeDtypeStruct((t, d), BF16),
                   jax.ShapeDtypeStruct((1, d), F32), hid_sh, hid_sh, hid_sh),
        grid=(ns, t // tm), in_specs=[row_last, row, row_last, vec, hid, hid, wrow, wrow, wrow],
        out_specs=(row_last, row_last, vec, hid, hid, hid), scratch_shapes=[pltpu.VMEM((t, d), F32)],
        compiler_params=_cparams(("arbitrary", "arbitrary")), name=name)(dxo, dxo_b, x, g, gate, up, wg, wu, wd)


def _ffn_bwd_w_call(h, dxo_b, dgate, dup, act, name, after=None):
    t, d = h.shape
    ns, _, f = dgate.shape
    tm = _tile(t, FFN_W_ROWS)
    nm = t // tm

    def body(h_ref, dxb_ref, dgate_ref, dup_ref, act_ref, *rest):
        dwg_ref, dwu_ref, dwd_ref, ag_ref, au_ref, ad_ref = rest[-6:]
        i = pl.program_id(1)
        hv = h_ref[...]
        pg = lax.dot_general(dgate_ref[0], hv, TN_DIMS, preferred_element_type=F32)
        pu = lax.dot_general(dup_ref[0], hv, TN_DIMS, preferred_element_type=F32)
        pd = lax.dot_general(act_ref[0], dxb_ref[...], TN_DIMS, preferred_element_type=F32)

        ag_ref[...] = pg + jnp.where(i > 0, ag_ref[...], 0.0)
        au_ref[...] = pu + jnp.where(i > 0, au_ref[...], 0.0)
        ad_ref[...] = pd + jnp.where(i > 0, ad_ref[...], 0.0)

        @pl.when(i == nm - 1)
        def _():
            dwg_ref[0] = ag_ref[...].astype(BF16)
            dwu_ref[0] = au_ref[...].astype(BF16)
            dwd_ref[0] = (0.5 * ad_ref[...]).astype(BF16)

    row = pl.BlockSpec((tm, d), lambda s, i: (i, 0))
    hid = pl.BlockSpec((1, tm, f), lambda s, i: (s, i, 0))
    wrow = pl.BlockSpec((1, f, d), lambda s, i: (s, 0, 0))
    wsh = jax.ShapeDtypeStruct((ns, f, d), BF16)
    return pl.pallas_call(
        body, out_shape=(wsh, wsh, wsh),
        grid=(ns, nm), in_specs=[row, row, hid, hid, hid] + [HBM_SPEC] * (after is not None),
        out_specs=(wrow, wrow, wrow),
        scratch_shapes=[pltpu.VMEM((f, d), F32), pltpu.VMEM((f, d), F32), pltpu.VMEM((f, d), F32)],
        compiler_params=_cparams(("parallel", "arbitrary")), name=name)(
            h, dxo_b, dgate, dup, act, *([after] if after is not None else []))


def _loss_head(x, g, tgt, name):
    t, w = x.shape
    tr = _row_tile(t)

    def body(x_ref, g_ref, t_ref, loss_ref, dx_ref, dxb_ref, dg_ref):
        xv = x_ref[...]
        gv = g_ref[...]
        r = lax.rsqrt(jnp.mean(xv * xv, axis=-1, keepdims=True) + EPS)
        nrm = xv * r
        err = nrm * gv - t_ref[...]
        dout = err * (1.0 / w)
        dn = dout * gv
        dx = r * (dn - nrm * jnp.mean(dn * nrm, axis=-1, keepdims=True))
        dx_ref[...] = dx
        dxb_ref[...] = dx.astype(BF16)

        @pl.when(pl.program_id(0) == 0)
        def _():
            dg_ref[...] = jnp.zeros_like(dg_ref)
            loss_ref[...] = jnp.zeros_like(loss_ref)

        dg_ref[...] += jnp.sum(dout * nrm, axis=0, keepdims=True)
        part = jnp.sum(jnp.sum(err * err, axis=-1, keepdims=True) * (0.5 / w), axis=0, keepdims=True)
        loss_ref[...] += jnp.broadcast_to(part, loss_ref.shape)

    row = pl.BlockSpec((tr, w), lambda i: (i, 0))
    vec = pl.BlockSpec((1, w), lambda i: (0, 0))
    return pl.pallas_call(
        body, out_shape=(jax.ShapeDtypeStruct((1, LANES), F32), jax.ShapeDtypeStruct((t, w), F32),
                         jax.ShapeDtypeStruct((t, w), BF16), jax.ShapeDtypeStruct((1, w), F32)),
        grid=(t // tr,), in_specs=[row, vec, row],
        out_specs=(pl.BlockSpec((1, LANES), lambda i: (0, 0)), row, row, vec),
        compiler_params=_cparams(("arbitrary",)), name=name)(x, g, tgt)


def _attn_bias():
    slopes = np.asarray(2.0 ** (-8.0 * (np.arange(ATTN_HEADS) + 1) / ATTN_HEADS), np.float32)
    qi = np.arange(QBLOCK)[:, None]
    kj = np.arange(3 * QBLOCK)[None, :]
    rel = np.abs(kj - QBLOCK - qi).astype(np.float32)
    tile = np.where(rel <= WINDOW, -slopes[:, None, None] * rel[None], np.float32(NEG_INF)).astype(np.float32)
    tile = tile.reshape(KV_HEADS, GQ * QBLOCK, 3 * QBLOCK)
    return jnp.asarray(np.swapaxes(tile, 1, 2))


def _attn_scores(k3, q, n, nb, bias):
    s = lax.dot_general(k3, q, NT_DIMS, preferred_element_type=F32) * (HEAD_DIM ** -0.5)
    key = lax.broadcasted_iota(jnp.int32, (3 * QBLOCK, 1), 0)
    inside = (key >= jnp.where(n == 0, QBLOCK, 0)) & (key < jnp.where(n == nb - 1, 2 * QBLOCK, 3 * QBLOCK))
    return jnp.where(inside, s + bias, NEG_INF)


Q_COL, K_COL, V_COL, U_COL = 0, ATTN_WIDTH // LANES, ATTN_WIDTH // LANES + 1, ATTN_WIDTH // LANES + 2


def _key_rows(ref, n, nb):
    prev, nxt = jnp.maximum(n - 1, 0), jnp.minimum(n + 1, nb - 1)
    blk = lambda b: ref[pl.ds(pl.multiple_of(b * QBLOCK, QBLOCK), QBLOCK), :]
    return jnp.concatenate([blk(prev), blk(n), blk(nxt)], axis=0)


def _head_tiles(x, kh, low):
    tiles = []
    for g in range(GQ):
        h = GQ * kh + g
        t128 = x[:, LANES * (h // 2):LANES * (h // 2 + 1)]
        t128 = jnp.where(low if h % 2 == 0 else jnp.logical_not(low), t128, 0.0)
        if h % 2 != kh:
            t128 = pltpu.roll(t128, HEAD_DIM, 1)
        tiles.append(t128)
    return jnp.concatenate(tiles, axis=0)


def _head_merge(per_kh, low):
    out = []
    for j in range(ATTN_HEADS // 2):
        pair = []
        for h in (2 * j, 2 * j + 1):
            kh, g = h // GQ, h % GQ
            t128 = per_kh[kh][g * QBLOCK:(g + 1) * QBLOCK, :]
            if h % 2 != kh:
                t128 = pltpu.roll(t128, HEAD_DIM, 1)
            pair.append(t128)
        out.append(jnp.where(low, pair[0], pair[1]))
    return jnp.concatenate(out, axis=1)


def _attn_fwd_proj(proj, sink_rows, bias, name):
    t = proj.shape[0]
    nb = t // QBLOCK
    rows = GQ * QBLOCK

    def body(q_ref, k_ref, v_ref, sink_ref, bias_ref, o_ref, lse_ref):
        n = pl.program_id(0)
        low = lax.broadcasted_iota(jnp.int32, (QBLOCK, LANES), 1) < HEAD_DIM
        k3 = _key_rows(k_ref, n, nb).astype(BF16)
        v3 = _key_rows(v_ref, n, nb).astype(BF16)
        q = q_ref[...]
        outs = []
        for kh in range(KV_HEADS):
            qs = _head_tiles(q, kh, low).astype(BF16)
            s = _attn_scores(k3, qs, n, nb, bias_ref[kh])
            sink = sink_ref[kh]
            mx = jnp.maximum(jnp.max(s, axis=0, keepdims=True), sink)
            p = jnp.exp(s - mx)
            den = jnp.sum(p, axis=0, keepdims=True) + jnp.exp(sink - mx)
            pn = (p * (1.0 / den)).astype(BF16)
            outs.append(lax.dot_general(pn, v3, TN_DIMS, preferred_element_type=F32))
            lse_ref[0, kh] = mx + jnp.log(den)
        o_ref[...] = _head_merge(outs, low)

    strip = lambda col: pl.BlockSpec((t, LANES), lambda n, col=col: (0, col))
    rowspec = pl.BlockSpec((KV_HEADS, 1, rows), lambda n: (0, 0, 0))
    biasspec = pl.BlockSpec((KV_HEADS, 3 * QBLOCK, rows), lambda n: (0, 0, 0))
    return pl.pallas_call(
        body, out_shape=(jax.ShapeDtypeStruct((t, ATTN_WIDTH), F32), jax.ShapeDtypeStruct((nb, KV_HEADS, 1, rows), F32)),
        grid=(nb,), in_specs=[pl.BlockSpec((QBLOCK, ATTN_WIDTH), lambda n: (n, 0)), strip(K_COL), strip(V_COL),
                              rowspec, biasspec],
        out_specs=(pl.BlockSpec((QBLOCK, ATTN_WIDTH), lambda n: (n, 0)),
                   pl.BlockSpec((1, KV_HEADS, 1, rows), lambda n: (n, 0, 0, 0))),
        compiler_params=_cparams(("parallel",)), name=name)(proj, proj, proj, sink_rows, bias)


def _attn_bwd_proj(proj, sink_rows, bias, o, lse, do, name):
    t = proj.shape[0]
    nb = t // QBLOCK
    rows = GQ * QBLOCK
    scale = HEAD_DIM ** -0.5

    def body(q_ref, k_ref, v_ref, sink_ref, bias_ref, o_ref, lse_ref, do_ref, dq_ref, dk_ref, dv_ref, ds_ref):
        n = pl.program_id(0)

        @pl.when(n == 0)
        def _():
            dk_ref[...] = jnp.zeros_like(dk_ref)
            dv_ref[...] = jnp.zeros_like(dv_ref)
            ds_ref[...] = jnp.zeros_like(ds_ref)

        low = lax.broadcasted_iota(jnp.int32, (QBLOCK, LANES), 1) < HEAD_DIM
        k3 = _key_rows(k_ref, n, nb).astype(BF16)
        v3 = _key_rows(v_ref, n, nb).astype(BF16)
        q, dov = q_ref[...], do_ref[...]
        dod = dov * o_ref[...]
        dqs = []
        dk3 = jnp.zeros((3 * QBLOCK, LANES), F32)
        dv3 = jnp.zeros((3 * QBLOCK, LANES), F32)
        ones = jnp.ones((SUBLANES, LANES), F32)
        for kh in range(KV_HEADS):
            qs = _head_tiles(q, kh, low).astype(BF16)
            dos = _head_tiles(dov, kh, low).astype(BF16)
            delta = lax.dot_general(ones, _head_tiles(dod, kh, low), NT_DIMS, preferred_element_type=F32,
                                    precision=lax.Precision.HIGHEST)[0:1, :]
            lse_kh = lse_ref[0, kh]
            s = _attn_scores(k3, qs, n, nb, bias_ref[kh])
            p = jnp.exp(s - lse_kh)
            dp = lax.dot_general(v3, dos, NT_DIMS, preferred_element_type=F32)
            dsb = (p * (dp - delta)).astype(BF16)
            dqs.append(lax.dot_general(dsb, k3, TN_DIMS, preferred_element_type=F32) * scale)
            dk3 = dk3 + jnp.dot(dsb, qs, preferred_element_type=F32) * scale
            dv3 = dv3 + jnp.dot(p.astype(BF16), dos, preferred_element_type=F32)
            ds_ref[kh] += -jnp.exp(sink_ref[kh] - lse_kh) * delta
        dq_ref[...] = _head_merge(dqs, low)
        prev, nxt = jnp.maximum(n - 1, 0), jnp.minimum(n + 1, nb - 1)
        for j, b in enumerate((prev, n, nxt)):
            blk = pl.ds(pl.multiple_of(b * QBLOCK, QBLOCK), QBLOCK)
            dk_ref[blk, :] += dk3[j * QBLOCK:(j + 1) * QBLOCK, :]
            dv_ref[blk, :] += dv3[j * QBLOCK:(j + 1) * QBLOCK, :]

    strip = lambda col: pl.BlockSpec((t, LANES), lambda n, col=col: (0, col))
    rowspec = pl.BlockSpec((KV_HEADS, 1, rows), lambda n: (0, 0, 0))
    qspec = pl.BlockSpec((QBLOCK, ATTN_WIDTH), lambda n: (n, 0))
    kv_out = pl.BlockSpec((t, LANES), lambda n: (0, 0))
    biasspec = pl.BlockSpec((KV_HEADS, 3 * QBLOCK, rows), lambda n: (0, 0, 0))
    return pl.pallas_call(
        body,
        out_shape=(jax.ShapeDtypeStruct((t, ATTN_WIDTH), F32), jax.ShapeDtypeStruct((t, LANES), F32),
                   jax.ShapeDtypeStruct((t, LANES), F32), jax.ShapeDtypeStruct((KV_HEADS, 1, rows), F32)),
        grid=(nb,),
        in_specs=[qspec, strip(K_COL), strip(V_COL), rowspec, biasspec, qspec,
                  pl.BlockSpec((1, KV_HEADS, 1, rows), lambda n: (n, 0, 0, 0)), qspec],
        out_specs=(qspec, kv_out, kv_out, rowspec),
        compiler_params=_cparams(("arbitrary",)), name=name)(proj, proj, proj, sink_rows, bias, o, lse, do)


def _scan_tables(a_re, a_im, reverse):
    pw = [(a_re, a_im)]
    for _ in range(SUBLANES - 1):
        pr, pi = pw[-1]
        pw.append((pr * a_re - pi * a_im, pr * a_im + pi * a_re))
    rows = np.arange(SUBLANES)
    tabs = []
    for d in (1, 2, 4):
        mask = (rows <= SUBLANES - 1 - d) if reverse else (rows >= d)
        m = jnp.asarray(mask, F32)[:, None]
        tabs += [m * pw[d - 1][0][None, :], m * pw[d - 1][1][None, :]]
    order = (SUBLANES - 1 - rows) if reverse else rows
    tabs += [jnp.stack([pw[j][0] for j in order]), jnp.stack([pw[j][1] for j in order])]
    tab = jnp.stack(tabs)
    return tab.reshape(8, SUBLANES, N_STRIPS, STRIP_ST).transpose(2, 0, 1, 3)


def _scan_pair_chunk(dirs):
    nblk = dirs[0]['xr'].shape[0] // SUBLANES

    @pl.when(pl.program_id(1) == 0)
    def _():
        for d in dirs:
            d['carry'][...] = jnp.zeros_like(d['carry'])

    for d in dirs:
        vb = d['v'][...].astype(BF16)
        d['xr'][...] = jnp.dot(vb, d['mir'][0], preferred_element_type=F32)
        d['xi'][...] = jnp.dot(vb, d['mii'][0], preferred_element_type=F32)
    carries = [(d['carry'][0], d['carry'][1]) for d in dirs]
    for i in range(nblk):
        for k, d in enumerate(dirs):
            rev = d['reverse']
            rows = pl.ds(((nblk - 1 - i) if rev else i) * SUBLANES, SUBLANES)
            cr, ci = carries[k]
            xr, xi = d['xr'][rows, :], d['xi'][rows, :]
            for j, s in enumerate((1, 2, 4)):
                tr_, ti_ = d['tab'][0, 2 * j], d['tab'][0, 2 * j + 1]
                sh = (SUBLANES - s) if rev else s
                sr, si = pltpu.roll(xr, sh, 0), pltpu.roll(xi, sh, 0)
                xr, xi = xr + tr_ * sr - ti_ * si, xi + tr_ * si + ti_ * sr
            pr, pi = d['tab'][0, 6], d['tab'][0, 7]
            xr, xi = xr + pr * cr - pi * ci, xi + pr * ci + pi * cr
            d['xr'][rows, :] = xr
            d['xi'][rows, :] = xi
            edge = 0 if rev else SUBLANES - 1
            carries[k] = (jnp.broadcast_to(xr[edge:edge + 1, :], xr.shape),
                          jnp.broadcast_to(xi[edge:edge + 1, :], xi.shape))
    for k, d in enumerate(dirs):
        d['carry'][0], d['carry'][1] = carries[k]
        d['y'][...] = (jnp.dot(d['xr'][...].astype(BF16), d['mor'][0], preferred_element_type=F32)
                       + jnp.dot(d['xi'][...].astype(BF16), d['moi'][0], preferred_element_type=F32))


def _scan_pair(v, ops_f, ops_b, name):
    t = v.shape[0]
    tc = _tile(t, SCAN_ROWS)
    nc = t // tc

    def body(vf_ref, vb_ref, *refs):
        ops = refs[:10]
        outs = refs[10:16]
        carries = refs[16:18]
        dirs = []
        for k, (v_ref, rev) in enumerate(((vf_ref, False), (vb_ref, True))):
            mir, mii, tab, mor, moi = ops[5 * k:5 * k + 5]
            y, xr, xi = outs[3 * k:3 * k + 3]
            dirs.append(dict(v=v_ref, mir=mir, mii=mii, tab=tab, mor=mor, moi=moi, y=y, xr=xr, xi=xi,
                             carry=carries[k], reverse=rev))
        _scan_pair_chunk(dirs)

    col0 = v.shape[1] // STRIP_IN - N_STRIPS
    fmap = lambda s, c: (c, s)
    bmap = lambda s, c: (nc - 1 - c, s)
    smap3 = lambda s, c: (s, 0, 0)
    m_in = pl.BlockSpec((1, STRIP_IN, STRIP_ST), smap3)
    m_out = pl.BlockSpec((1, STRIP_ST, STRIP_IN), smap3)
    tabspec = pl.BlockSpec((1, 8, SUBLANES, STRIP_ST), lambda s, c: (s, 0, 0, 0))
    opspecs = [m_in, m_in, tabspec, m_out, m_out]
    y_sh = jax.ShapeDtypeStruct((t, SSM_WIDTH), F32)
    x_sh = jax.ShapeDtypeStruct((t, N_STRIPS * STRIP_ST), F32)
    outspecs = lambda m: [pl.BlockSpec((tc, STRIP_IN), m), pl.BlockSpec((tc, STRIP_ST), m),
                          pl.BlockSpec((tc, STRIP_ST), m)]
    res = pl.pallas_call(
        body, out_shape=[y_sh, x_sh, x_sh] * 2, grid=(N_STRIPS, nc),
        in_specs=[pl.BlockSpec((tc, STRIP_IN), lambda s, c: (c, s + col0)),
                  pl.BlockSpec((tc, STRIP_IN), lambda s, c: (nc - 1 - c, s + col0))] + opspecs * 2,
        out_specs=outspecs(fmap) + outspecs(bmap),
        scratch_shapes=[pltpu.VMEM((2, SUBLANES, STRIP_ST), F32)] * 2,
        compiler_params=_cparams(("parallel", "arbitrary")), name=name)(v, v, *ops_f, *ops_b)
    return tuple(res[:3]), tuple(res[3:])


def _scan_adjoint_pair(dy, u, states, adj_ops, name):
    t = dy.shape[0]
    tc = _tile(t, SCAN_ROWS)
    nc = t // tc
    hb = tc // SUBLANES
    n_out = 6

    def body(*refs):
        c = pl.program_id(1)
        dirs = []
        for k in range(2):
            dy_ref, mir, mii, tab, mor, moi, u_ref, xr_ref, xi_ref, hr_ref, hi_ref = refs[11 * k:11 * k + 11]
            outs = refs[22 + n_out * k:22 + n_out * (k + 1)]
            lr_ref, li_ref, carry = refs[22 + 2 * n_out + 3 * k:22 + 2 * n_out + 3 * k + 3]
            dirs.append(dict(v=dy_ref, mir=mir, mii=mii, tab=tab, mor=mor, moi=moi, y=outs[0], xr=lr_ref, xi=li_ref,
                             carry=carry, reverse=(k == 0), u=u_ref, fx=(xr_ref, xi_ref), halo=(hr_ref, hi_ref),
                             acc=outs[1:]))

        @pl.when(c == 0)
        def _():
            for d in dirs:
                for r in d['acc']:
                    r[...] = jnp.zeros_like(r)

        _scan_pair_chunk(dirs)
        for d in dirs:
            fwd_reverse = not d['reverse']
            rc = (nc - 1 - c) if d['reverse'] else c
            dmir_ref, dmii_ref, dmor_ref, dmoi_ref, da_ref = d['acc']
            xrv, xiv, lrv, liv = d['fx'][0][...], d['fx'][1][...], d['xr'][...], d['xi'][...]
            hr_ref, hi_ref = d['halo']
            row = lax.broadcasted_iota(jnp.int32, xrv.shape, 0)
            if fwd_reverse:
                live = (rc < nc - 1).astype(F32)
                edge_r, edge_i = hr_ref[0:1, :] * live, hi_ref[0:1, :] * live
                xpr = jnp.where(row == tc - 1, edge_r, pltpu.roll(xrv, tc - 1, 0))
                xpi = jnp.where(row == tc - 1, edge_i, pltpu.roll(xiv, tc - 1, 0))
            else:
                live = (rc > 0).astype(F32)
                edge_r, edge_i = hr_ref[SUBLANES - 1:SUBLANES, :] * live, hi_ref[SUBLANES - 1:SUBLANES, :] * live
                xpr = jnp.where(row == 0, edge_r, pltpu.roll(xrv, 1, 0))
                xpi = jnp.where(row == 0, edge_i, pltpu.roll(xiv, 1, 0))
            da_ref[0, 0:1, :] += jnp.sum(xpr * lrv + xpi * liv, axis=0, keepdims=True)
            da_ref[0, 1:2, :] += jnp.sum(xpr * liv - xpi * lrv, axis=0, keepdims=True)
            ub, dyb = d['u'][...].astype(BF16), d['v'][...].astype(BF16)
            dmir_ref[0] += lax.dot_general(ub, lrv.astype(BF16), TN_DIMS, preferred_element_type=F32)
            dmii_ref[0] += lax.dot_general(ub, liv.astype(BF16), TN_DIMS, preferred_element_type=F32)
            dmor_ref[0] += lax.dot_general(xrv.astype(BF16), dyb, TN_DIMS, preferred_element_type=F32)
            dmoi_ref[0] += lax.dot_general(xiv.astype(BF16), dyb, TN_DIMS, preferred_element_type=F32)

    col0 = u.shape[1] // STRIP_IN - N_STRIPS
    smap3 = lambda s, c: (s, 0, 0)
    m_in = pl.BlockSpec((1, STRIP_IN, STRIP_ST), smap3)
    m_out = pl.BlockSpec((1, STRIP_ST, STRIP_IN), smap3)
    tabspec = pl.BlockSpec((1, 8, SUBLANES, STRIP_ST), lambda s, c: (s, 0, 0, 0))
    in_specs, out_specs, args = [], [], []
    for k in range(2):
        reverse = k == 0
        rowblk = (lambda c: nc - 1 - c) if reverse else (lambda c: c)
        tmap = lambda s, c, rowblk=rowblk: (rowblk(c), s)
        umap = lambda s, c, rowblk=rowblk: (rowblk(c), s + col0)
        if not reverse:
            hmap = lambda s, c, rowblk=rowblk: (jnp.minimum((rowblk(c) + 1) * hb, t // SUBLANES - 1), s)
        else:
            hmap = lambda s, c, rowblk=rowblk: (jnp.maximum(rowblk(c) * hb - 1, 0), s)
        narrow = pl.BlockSpec((tc, STRIP_IN), tmap)
        wide = pl.BlockSpec((tc, STRIP_ST), tmap)
        halo = pl.BlockSpec((SUBLANES, STRIP_ST), hmap)
        in_specs += [narrow, m_in, m_in, tabspec, m_out, m_out, pl.BlockSpec((tc, STRIP_IN), umap), wide, wide,
                     halo, halo]
        out_specs += [narrow, m_in, m_in, m_out, m_out, pl.BlockSpec((1, SUBLANES, STRIP_ST), smap3)]
        xr, xi = states[k]
        args += [dy, *adj_ops[k], u, xr, xi, xr, xi]
    out_shape = [jax.ShapeDtypeStruct((t, SSM_WIDTH), F32),
                 jax.ShapeDtypeStruct((N_STRIPS, STRIP_IN, STRIP_ST), F32),
                 jax.ShapeDtypeStruct((N_STRIPS, STRIP_IN, STRIP_ST), F32),
                 jax.ShapeDtypeStruct((N_STRIPS, STRIP_ST, STRIP_IN), F32),
                 jax.ShapeDtypeStruct((N_STRIPS, STRIP_ST, STRIP_IN), F32),
                 jax.ShapeDtypeStruct((N_STRIPS, SUBLANES, STRIP_ST), F32)] * 2
    res = pl.pallas_call(
        body, out_shape=out_shape, grid=(N_STRIPS, nc), in_specs=in_specs, out_specs=out_specs,
        scratch_shapes=[pltpu.VMEM((tc, STRIP_ST), F32), pltpu.VMEM((tc, STRIP_ST), F32),
                        pltpu.VMEM((2, SUBLANES, STRIP_ST), F32)] * 2,
        compiler_params=_cparams(("parallel", "arbitrary")), name=name)(*args)
    return tuple(res[:n_out]), tuple(res[n_out:])


def _ssm_prep(lam_re, lam_im, log_dt, bt_re, bt_im, c_re, c_im):
    lr = jnp.minimum(lam_re, LAMBDA_RE_MAX)
    li = lam_im
    dt = jnp.exp(log_dt)[:, None]
    mag = jnp.exp(lr * dt)
    a_re = mag * jnp.cos(li * dt)
    a_im = mag * jnp.sin(li * dt)
    den = lr * lr + li * li
    coef_re = ((a_re - 1.0) * lr + a_im * li) / den
    coef_im = (a_im * lr - (a_re - 1.0) * li) / den
    bb_re = coef_re[:, None, :] * bt_re - coef_im[:, None, :] * bt_im
    bb_im = coef_re[:, None, :] * bt_im + coef_im[:, None, :] * bt_re
    eye = jnp.eye(SSM_GROUPS // N_STRIPS, dtype=F32)

    def strips(m):
        g, a, b = m.shape
        m4 = m.reshape(N_STRIPS, g // N_STRIPS, a, b)
        return jnp.einsum('sgab,gk->sgakb', m4, eye).reshape(N_STRIPS, g // N_STRIPS * a, g // N_STRIPS * b)

    mi_re = strips(bb_re)
    mi_im = strips(bb_im)
    mo_re = strips(jnp.swapaxes(c_re, 1, 2))
    mo_im = strips(-jnp.swapaxes(c_im, 1, 2))
    return a_re.reshape(-1), a_im.reshape(-1), mi_re, mi_im, mo_re, mo_im


def _gelu(x):
    c = math.sqrt(2.0 / math.pi)
    return 0.5 * x * (1.0 + jnp.tanh(c * (x + 0.044715 * x * x * x)))


def _gelu_grad(x):
    c = math.sqrt(2.0 / math.pi)
    th = jnp.tanh(c * (x + 0.044715 * x * x * x))
    return 0.5 * (1.0 + th) + 0.5 * x * (1.0 - th * th) * c * (1.0 + 3.0 * 0.044715 * x * x)


def _last_cols_specs(u, w, tr):
    half = w // 2
    first = (u.shape[1] - w) // half
    assert first * half == u.shape[1] - w
    return [pl.BlockSpec((tr, half), lambda i, k=k: (i, first + k)) for k in range(2)]


def _ssm_post_fwd(u, yf, yb, d, wglu, bglu, name):
    t, w = yf.shape
    tr = _row_tile(t)

    def body(ua_ref, ub_ref, yf_ref, yb_ref, d_ref, w_ref, b_ref, s_ref, y0_ref, z_ref):
        uv = jnp.concatenate([ua_ref[...], ub_ref[...]], axis=1)
        y0 = d_ref[...] * uv + yf_ref[...] + yb_ref[...]
        yg = _gelu(y0)
        z = jnp.dot(yg.astype(BF16), w_ref[...], preferred_element_type=F32) + b_ref[...]
        s_ref[...] = yg * _sigmoid(z)
        y0_ref[...] = y0
        z_ref[...] = z

    row = pl.BlockSpec((tr, w), lambda i: (i, 0))
    vec = pl.BlockSpec((1, w), lambda i: (0, 0))
    mat = pl.BlockSpec((w, w), lambda i: (0, 0))
    sh = jax.ShapeDtypeStruct((t, w), F32)
    return pl.pallas_call(body, out_shape=(sh, sh, sh), grid=(t // tr,),
                          in_specs=[*_last_cols_specs(u, w, tr), row, row, vec, mat, vec], out_specs=(row, row, row),
                          compiler_params=_cparams(("parallel",)), name=name)(u, u, yf, yb, d, wglu, bglu)


def _ssm_post_bwd(ds, y0, z, u, d, wglu, name):
    t, w = ds.shape
    tr = _row_tile(t)

    def body(ds_ref, y0_ref, z_ref, ua_ref, ub_ref, d_ref, w_ref, dy0_ref, dw_ref, db_ref, dd_ref):
        @pl.when(pl.program_id(0) == 0)
        def _():
            dw_ref[...] = jnp.zeros_like(dw_ref)
            db_ref[...] = jnp.zeros_like(db_ref)
            dd_ref[...] = jnp.zeros_like(dd_ref)

        y0 = y0_ref[...]
        yg = _gelu(y0)
        sg = _sigmoid(z_ref[...])
        dsv = ds_ref[...]
        dz = dsv * yg * sg * (1.0 - sg)
        dzb = dz.astype(BF16)
        dyg = dsv * sg + lax.dot_general(dzb, w_ref[...], (((1,), (1,)), ((), ())), preferred_element_type=F32)
        dy0 = dyg * _gelu_grad(y0)
        dy0_ref[...] = dy0
        dw_ref[...] += lax.dot_general(yg.astype(BF16), dzb, (((0,), (0,)), ((), ())), preferred_element_type=F32)
        db_ref[...] += jnp.sum(dz, axis=0, keepdims=True)
        uv = jnp.concatenate([ua_ref[...], ub_ref[...]], axis=1)
        dd_ref[...] += jnp.sum(dy0 * uv, axis=0, keepdims=True)

    row = pl.BlockSpec((tr, w), lambda i: (i, 0))
    vec = pl.BlockSpec((1, w), lambda i: (0, 0))
    mat = pl.BlockSpec((w, w), lambda i: (0, 0))
    return pl.pallas_call(
        body, out_shape=(jax.ShapeDtypeStruct((t, w), F32), jax.ShapeDtypeStruct((w, w), F32),
                         jax.ShapeDtypeStruct((1, w), F32), jax.ShapeDtypeStruct((1, w), F32)),
        grid=(t // tr,), in_specs=[row, row, row, *_last_cols_specs(u, w, tr), vec, mat],
        out_specs=(row, mat, vec, vec),
        compiler_params=_cparams(("arbitrary",)), name=name)(ds, y0, z, u, u, d, wglu)


def _du_combine(dy0, d, du_f, du_b, name):
    t, w = dy0.shape
    tr = _row_tile(t)

    def body(dy_ref, d_ref, a_ref, b_ref, o_ref):
        o_ref[...] = d_ref[...] * dy_ref[...] + a_ref[...] + b_ref[...]

    row = pl.BlockSpec((tr, w), lambda i: (i, 0))
    vec = pl.BlockSpec((1, w), lambda i: (0, 0))
    return pl.pallas_call(body, out_shape=jax.ShapeDtypeStruct((t, w), F32), grid=(t // tr,),
                          in_specs=[row, vec, row, row], out_specs=row, compiler_params=_cparams(("parallel",)),
                          name=name)(dy0, d, du_f, du_b)


def _ffn_fwd(x, g, wg, wu, wd, tag):
    xo, h, gate, up = _ffn_fwd_call(x, g, wg, wu, wd, f"{tag}_fwd")
    return xo, (h, gate, up)


def _ffn_bwd(dxo, dxo_b, x, g, wg, wu, wd, saved, tag, between=None):
    h, gate, up = saved
    dx, dx_b, dg, dgate, dup, act = _ffn_bwd_x_call(dxo, dxo_b, x, g, gate, up, wg, wu, wd, f"{tag}_bwd_x")
    after = between(dg) if between is not None else None
    dwg, dwu, dwd = _ffn_bwd_w_call(h, dxo_b, dgate, dup, act, f"{tag}_bwd_w", after=after)
    return dx, dx_b, dg, dwg, dwu, dwd


def _local_step(x, tgt, w, get_weights, put_grads, reduce_wide, put_narrow):
    t = x.shape[0]
    row = lambda a: a.reshape(1, -1)
    grads = {}

    w = dict(w)

    ssm_names = ['ssm_lambda_re', 'ssm_lambda_im', 'ssm_log_dt', 'ssm_b_re', 'ssm_b_im', 'ssm_c_re', 'ssm_c_im']
    tr3 = lambda m: jnp.swapaxes(m, 1, 2)
    fwd_ops, adj_ops, vjps = [], [], []
    for direction in range(2):
        rev = direction == 1
        prep, vjp = jax.vjp(_ssm_prep, *[w[n][direction] for n in ssm_names])
        a_re, a_im = prep[0], prep[1]
        mi_re, mi_im, mo_re, mo_im = (m.astype(BF16) for m in prep[2:])
        fwd_ops.append((mi_re, mi_im, _scan_tables(a_re, a_im, rev), mo_re, mo_im))
        adj_ops.append((tr3(mo_re), tr3(mo_im), _scan_tables(a_re, -a_im, not rev), tr3(mi_re), tr3(mi_im)))
        vjps.append(vjp)
    sink_rows = jnp.repeat(w['attn_sinks'].reshape(KV_HEADS, GQ), QBLOCK, axis=1)[:, None, :]
    bias = _attn_bias()
    prepared = sum(jnp.sum(op[:1, :1].astype(F32)) for ops in fwd_ops + adj_ops for op in ops) + sink_rows[0, 0, 0]

    w.update(get_weights('ffn1', prepared.reshape(1, 1)))
    x1, ffn1_saved = _ffn_fwd(x, w['norm_ffn1'], w['ffn1_w_gate'], w['ffn1_w_up'], w['ffn1_w_down'], "ffn1")
    w.update(get_weights('mix', x1))

    proj, h2 = _norm_mm([x1], [w['norm_mix']], w['w_in'], tb=True, res=None, name="in_proj")
    u = proj

    attn, lse = _attn_fwd_proj(proj, sink_rows, bias, "attn_fwd")

    (y_f, *states_f), (y_b, *states_b) = _scan_pair(u, fwd_ops[0], fwd_ops[1], "s5_fwd")
    ys, states = [y_f, y_b], [states_f, states_b]
    d_row = row(w['ssm_d'])
    s, y0, z = _ssm_post_fwd(u, ys[0], ys[1], d_row, w['ssm_glu_w'], row(w['ssm_glu_b']), "ssm_post")

    x2, mixed = _norm_mm([attn, s], [row(w['attn_out_norm']), row(w['ssm_out_norm'])], w['w_out'], tb=False,
                         res=x1, name="out_proj")

    w.update(get_weights('ffn2', x2))
    x3, ffn2_saved = _ffn_fwd(x2, w['norm_ffn2'], w['ffn2_w_gate'], w['ffn2_w_up'], w['ffn2_w_down'], "ffn2")

    loss, dx3, dx3_b, dgf = _loss_head(x3, row(w['final_norm']), tgt, "loss_head")
    grads['final_norm'] = dgf.reshape(w['final_norm'].shape)

    dx2, dx2_b, dg, dwg, dwu, dwd = _ffn_bwd(dx3, dx3_b, x2, w['norm_ffn2'], w['ffn2_w_gate'], w['ffn2_w_up'],
                                             w['ffn2_w_down'], ffn2_saved, "ffn2")
    grads['norm_ffn2'] = dg
    sent = put_grads('ffn2', dict(ffn2_w_gate=dwg, ffn2_w_up=dwu, ffn2_w_down=dwd))

    (dattn, _, dga), (ds, _, dgs) = _mm_rms_bwd(
        dx2_b, w['w_out'], tb=True, xs=[attn, s], gs=[row(w['attn_out_norm']), row(w['ssm_out_norm'])],
        dres=None, after=sent, name="out_proj_dx")
    dw_out = _mm(mixed, dx2_b, ta=True, out_dtype=BF16, name="out_proj_dw")[0]
    grads.update(attn_out_norm=dga, ssm_out_norm=dgs)

    dy0, dwglu, dbglu, dd = _ssm_post_bwd(ds, y0, z, u, d_row, w['ssm_glu_w'], "ssm_post_bwd")
    grads['ssm_glu_b'] = dbglu
    grads['ssm_d'] = dd.reshape(w['ssm_d'].shape)
    dparams, du_dirs = [], []
    for direction, res in enumerate(_scan_adjoint_pair(dy0, u, states, adj_ops, "s5_adj")):
        du_dir, dmir, dmii, dmor, dmoi, da = res
        du_dirs.append(du_dir)
        da_re = da[:, 0, :].reshape(-1)
        da_im = da[:, 1, :].reshape(-1)
        dparams.append(vjps[direction]((da_re, da_im, dmir, dmii, dmor, dmoi)))
    du = _du_combine(dy0, d_row, du_dirs[0], du_dirs[1], "ssm_du")
    for i, n in enumerate(ssm_names):
        grads[n] = jnp.stack([dparams[0][i], dparams[1][i]])
    wide_sum = reduce_wide(grads)

    dq, dk, dv, dsink = _attn_bwd_proj(proj, sink_rows, bias, attn, lse, dattn, "attn_bwd")
    grads['attn_sinks'] = jnp.sum(dsink.reshape(ATTN_HEADS, QBLOCK), axis=-1).reshape(w['attn_sinks'].shape)
    dproj = jnp.concatenate([dq, dk, dv, du], axis=-1).astype(BF16)

    dw_in = _mm(dproj, h2, ta=True, out_dtype=BF16, after=wide_sum, name="in_proj_dw")[0]
    sent = put_grads('mix', dict(w_in=dw_in, ssm_glu_w=dwglu, w_out=dw_out))
    ((dx1, dx1_b, dgm),) = _mm_rms_bwd(dproj, w['w_in'], tb=False, xs=[x1], gs=[w['norm_mix']], dres=dx2,
                                       after=sent, name="in_proj_dx")
    grads['norm_mix'] = dgm

    def all_small_known(dg):
        grads['norm_ffn1'] = dg
        return put_narrow(grads, loss)

    dx0, _, dg, dwg, dwu, dwd = _ffn_bwd(dx1, dx1_b, x, w['norm_ffn1'], w['ffn1_w_gate'], w['ffn1_w_up'],
                                         w['ffn1_w_down'], ffn1_saved, "ffn1", between=all_small_known)
    put_grads('ffn1', dict(ffn1_w_gate=dwg, ffn1_w_up=dwu, ffn1_w_down=dwd))
    return dx0, grads


HBM_SPEC = pl.BlockSpec(memory_space=pl.ANY)


def _chip_peers(x, y):
    return [(1 - x, y), (x, 1 - y), (1 - x, 1 - y)]


HBM_ONLY = pl.BlockSpec(memory_space=pltpu.HBM)
SEM_SPEC = pl.BlockSpec(memory_space=pltpu.SEMAPHORE)
EFFECT = pltpu.SideEffectType.DATAFLOW_SIDE_EFFECTING


def _place_own(srcs, slot, name):
    na = len(srcs)
    r, c = srcs[0].shape
    tr = r // 2

    def body(slot_ref, *refs):
        for a in range(na):
            refs[na + a][0] = refs[a][...]

    return pl.pallas_call(
        body, out_shape=[jax.ShapeDtypeStruct((N_CHIPS, r, c), s.dtype) for s in srcs],
        grid_spec=pltpu.PrefetchScalarGridSpec(
            num_scalar_prefetch=1, grid=(2,), in_specs=[pl.BlockSpec((tr, c), lambda i, s: (i, 0))] * na,
            out_specs=[pl.BlockSpec((1, tr, c), lambda i, s: (s[0], i, 0))] * na),
        compiler_params=_cparams(("parallel",)), name=name)(slot, *srcs)


def _chip_copies(srcs, lands, send_sems, recv_sems, scatter, landed):
    x, y, c = lax.axis_index("x"), lax.axis_index("y"), lax.axis_index("c")
    me = 2 * x + y
    out = []
    for i in range(len(srcs)):
        for j, (px, py) in enumerate(_chip_peers(x, y)):
            p = 2 * px + py
            slot = p if landed else me
            if scatter:
                src, dst = srcs[i].at[p], lands[i].at[slot]
            else:
                rows = _core_half(srcs[i].shape[0], c)
                src, dst = srcs[i].at[rows], lands[i].at[slot, rows]
            out.append(pltpu.make_async_remote_copy(src, dst, send_sems.at[3 * i + j], recv_sems.at[3 * i + j],
                                                    device_id=(px, py, c), device_id_type=MESH))
    return out


def _core_half(nrows, c):
    half = nrows // 2
    return pl.ds(pl.multiple_of(c * half, 16), half)


def _sibling_forward(lands, name):
    n = len(lands)

    def body(*refs):
        bufs = refs[n:2 * n]
        send_sems, recv_sems = refs[2 * n:]
        x, y, c = lax.axis_index("x"), lax.axis_index("y"), lax.axis_index("c")
        mine = [_core_half(b.shape[1], c) for b in bufs]
        theirs = [_core_half(b.shape[1], 1 - c) for b in bufs]
        chips = [2 * px + py for px, py in _chip_peers(x, y)]
        cps = [pltpu.make_async_remote_copy(bufs[i].at[p, mine[i]], bufs[i].at[p, mine[i]], send_sems.at[3 * i + j],
                                            recv_sems.at[3 * i + j], device_id=(x, y, 1 - c), device_id_type=MESH)
               for i in range(n) for j, p in enumerate(chips)]
        for cp in cps:
            cp.start()
        for i in range(n):
            for j, p in enumerate(chips):
                pltpu.make_async_remote_copy(bufs[i].at[p, mine[i]], bufs[i].at[p, theirs[i]], send_sems.at[3 * i + j],
                                             recv_sems.at[3 * i + j], device_id=(x, y, 1 - c),
                                             device_id_type=MESH).wait()

    return pl.pallas_call(
        body, out_shape=[jax.ShapeDtypeStruct(a.shape, a.dtype) for a in lands],
        in_specs=[HBM_SPEC] * n, out_specs=[HBM_SPEC] * n, input_output_aliases={k: k for k in range(n)},
        scratch_shapes=[pltpu.SemaphoreType.DMA((3 * n,)), pltpu.SemaphoreType.DMA((3 * n,))],
        name=name)(*lands)


def _exchange_start(groups, scatter, name, after=None):
    sizes = [len(srcs) for srcs, _ in groups]
    flat_src = [a for srcs, _ in groups for a in srcs]
    flat_land = [a for _, lands in groups for a in lands]
    n = len(flat_src)
    ng = len(groups)

    def body(*refs):
        src_refs, land_refs = refs[:n], refs[n:2 * n]
        n_in = 2 * n + (after is not None)
        sems = refs[n_in:n_in + 2 * ng]
        token_ref = refs[-1]
        off = 0
        for gi, sz in enumerate(sizes):
            for cp in _chip_copies(src_refs[off:off + sz], land_refs[off:off + sz], sems[2 * gi], sems[2 * gi + 1],
                                   scatter, landed=False):
                cp.start()
            off += sz
        token_ref[...] = jnp.zeros_like(token_ref)

    sem_shapes = []
    for sz in sizes:
        sem_shapes += [pltpu.SemaphoreType.DMA((3 * sz,)), pltpu.SemaphoreType.DMA((3 * sz,))]
    hbm = lambda a: pltpu.HBM(a.shape, a.dtype)
    res = pl.pallas_call(
        body, name=name,
        out_shape=(tuple(sem_shapes) + tuple(hbm(a) for a in flat_src) + tuple(hbm(a) for a in flat_land)
                   + (jax.ShapeDtypeStruct((SUBLANES, LANES), F32),)),
        in_specs=[HBM_ONLY] * (2 * n) + [HBM_SPEC] * (after is not None),
        out_specs=tuple([SEM_SPEC] * (2 * ng) + [HBM_ONLY] * (2 * n) + [pl.BlockSpec(memory_space=pltpu.VMEM)]),
        input_output_aliases={k: 2 * ng + k for k in range(2 * n)},
        compiler_params=pltpu.CompilerParams(has_side_effects=EFFECT),
    )(*[pltpu.with_memory_space_constraint(a, pltpu.HBM) for a in flat_src + flat_land],
      *([after] if after is not None else []))
    sems, thru_src, thru_land = res[:2 * ng], res[2 * ng:2 * ng + n], res[2 * ng + n:2 * ng + 2 * n]
    out, off = [], 0
    for gi, sz in enumerate(sizes):
        out.append((sems[2 * gi], sems[2 * gi + 1], list(thru_src[off:off + sz]), list(thru_land[off:off + sz])))
        off += sz
    return out, res[-1]


def _exchange_wait(started, after, scatter, name):
    send_sems, recv_sems, srcs, lands = started
    n = len(srcs)

    def body(*refs):
        src_refs, land_refs = refs[:n], refs[n:2 * n]
        send_ref, recv_ref = refs[2 * n], refs[2 * n + 1]
        for cp in _chip_copies(src_refs, land_refs, send_ref, recv_ref, scatter, landed=True):
            cp.wait_send()
            cp.wait_recv()

    hbm = lambda a: pltpu.HBM(a.shape, a.dtype)
    res = pl.pallas_call(
        body, name=name, out_shape=tuple(hbm(a) for a in srcs) + tuple(hbm(a) for a in lands),
        in_specs=[HBM_ONLY] * (2 * n) + [SEM_SPEC, SEM_SPEC, HBM_SPEC], out_specs=tuple([HBM_ONLY] * (2 * n)),
        input_output_aliases={k: k for k in range(2 * n)},
        compiler_params=pltpu.CompilerParams(has_side_effects=EFFECT),
    )(*srcs, *lands, send_sems, recv_sems, after)
    return list(res[:n]), list(res[n:])


def _half_swap(parts, name):
    n = len(parts)

    def body(*refs):
        ins, outs = refs[:n], refs[n:2 * n]
        send_sems, recv_sems = refs[2 * n:]
        x, y, c = lax.axis_index("x"), lax.axis_index("y"), lax.axis_index("c")
        cps = [pltpu.make_async_remote_copy(ins[i].at[k, _core_half(ins[i].shape[1], 1 - c)], outs[i].at[k],
                                            send_sems.at[N_CHIPS * i + k], recv_sems.at[N_CHIPS * i + k],
                                            device_id=(x, y, 1 - c), device_id_type=MESH)
               for i in range(n) for k in range(N_CHIPS)]
        for cp in cps:
            cp.start()
        for cp in cps:
            cp.wait()

    return pl.pallas_call(
        body, out_shape=[jax.ShapeDtypeStruct((N_CHIPS, p.shape[1] // 2, p.shape[2]), p.dtype) for p in parts],
        in_specs=[HBM_SPEC] * n, out_specs=[HBM_SPEC] * n,
        scratch_shapes=[pltpu.SemaphoreType.DMA((N_CHIPS * n,)), pltpu.SemaphoreType.DMA((N_CHIPS * n,))],
        name=name)(*parts)


def _half_add(parts, sib, slots, name):
    na = len(parts)
    _, r, c = parts[0].shape
    hr = r // 2
    tr = _row_tile(hr, 512)
    nt = hr // tr

    def body(slot_ref, *refs):
        for a in range(na):
            refs[2 * na + a][...] = (refs[2 * a][...].astype(F32) + refs[2 * a + 1][...].astype(F32)).astype(BF16)

    mine = pl.BlockSpec((1, tr, c), lambda k, i, s: (k, i + s[4] * nt, 0))
    half = pl.BlockSpec((1, tr, c), lambda k, i, s: (k, i, 0))
    args = [a for p, sb in zip(parts, sib) for a in (p, sb)]
    return pl.pallas_call(
        body, out_shape=[jax.ShapeDtypeStruct((N_CHIPS, hr, c), BF16)] * na,
        grid_spec=pltpu.PrefetchScalarGridSpec(
            num_scalar_prefetch=1, grid=(N_CHIPS, nt), in_specs=[mine, half] * na, out_specs=[half] * na),
        compiler_params=_cparams(("parallel", "parallel")), name=name)(slots, *args)


def _half_forward(arrs, name):
    n = len(arrs)

    def body(*refs):
        bufs = refs[n:2 * n]
        send_sems, recv_sems = refs[2 * n:]
        x, y, c = lax.axis_index("x"), lax.axis_index("y"), lax.axis_index("c")
        cps = [pltpu.make_async_remote_copy(b.at[_core_half(b.shape[0], c)], b.at[_core_half(b.shape[0], c)],
                                            send_sems.at[i], recv_sems.at[i], device_id=(x, y, 1 - c),
                                            device_id_type=MESH) for i, b in enumerate(bufs)]
        for cp in cps:
            cp.start()
        for i, b in enumerate(bufs):
            pltpu.make_async_remote_copy(b.at[_core_half(b.shape[0], c)], b.at[_core_half(b.shape[0], 1 - c)],
                                         send_sems.at[i], recv_sems.at[i], device_id=(x, y, 1 - c),
                                         device_id_type=MESH).wait()

    return pl.pallas_call(
        body, out_shape=[jax.ShapeDtypeStruct(a.shape, a.dtype) for a in arrs],
        in_specs=[HBM_SPEC] * n, out_specs=[HBM_SPEC] * n, input_output_aliases={k: k for k in range(n)},
        scratch_shapes=[pltpu.SemaphoreType.DMA((n,)), pltpu.SemaphoreType.DMA((n,))],
        name=name)(*arrs)


def _sum_parts(parts, recv, slots, name):
    na = len(parts)
    _, r, c = parts[0].shape
    tr = _row_tile(r, 192)

    def body(slot_ref, *refs):
        for a in range(na):
            own_ref, r0_ref, r1_ref, r2_ref = refs[4 * a:4 * a + 4]
            refs[4 * na + a][...] = ((own_ref[0].astype(F32) + r0_ref[0].astype(F32))
                                     + (r1_ref[0].astype(F32) + r2_ref[0].astype(F32)))

    blk = lambda k: pl.BlockSpec((1, tr, c), lambda i, s, k=k: (s[k], i, 0))
    out_blk = pl.BlockSpec((tr, c), lambda i, s: (i + s[4] * (r // tr), 0))
    args = [a for p, rv in zip(parts, recv) for a in (p, rv, rv, rv)]
    return pl.pallas_call(
        body, out_shape=[jax.ShapeDtypeStruct((2 * r, c), F32)] * na,
        grid_spec=pltpu.PrefetchScalarGridSpec(
            num_scalar_prefetch=1, grid=(r // tr,), in_specs=[blk(0), blk(1), blk(2), blk(3)] * na,
            out_specs=[out_blk] * na),
        compiler_params=_cparams(("parallel",)), name=name)(slots, *args)


ALL_PEERS = [(fx, fy, fc) for fx in (0, 1) for fy in (0, 1) for fc in (0, 1)][1:]


def _all8_copies(srcs, lands, send_sems, recv_sems, landed):
    x, y, c = lax.axis_index("x"), lax.axis_index("y"), lax.axis_index("c")
    lin = 4 * x + 2 * y + c
    out = []
    for i, (src, land) in enumerate(zip(srcs, lands)):
        for j, (fx, fy, fc) in enumerate(ALL_PEERS):
            px, py, pc = x ^ fx, y ^ fy, c ^ fc
            slot = (4 * px + 2 * py + pc) if landed else lin
            out.append(pltpu.make_async_remote_copy(src, land.at[slot], send_sems.at[7 * i + j], recv_sems.at[7 * i + j],
                                                    device_id=(px, py, pc), device_id_type=MESH))
    return out


def _all8_start(srcs, name):
    n = len(srcs)
    lands = [lax.empty((N_DEV,) + s.shape, s.dtype) for s in srcs]

    def body(*refs):
        for cp in _all8_copies(refs[:n], refs[n:2 * n], refs[2 * n], refs[2 * n + 1], landed=False):
            cp.start()
        refs[-1][...] = jnp.zeros_like(refs[-1])

    hbm = lambda a: pltpu.HBM(a.shape, a.dtype)
    res = pl.pallas_call(
        body, name=name,
        out_shape=(pltpu.SemaphoreType.DMA((7 * n,)), pltpu.SemaphoreType.DMA((7 * n,)), *[hbm(a) for a in srcs],
                   *[hbm(a) for a in lands], jax.ShapeDtypeStruct((SUBLANES, LANES), F32)),
        in_specs=[HBM_ONLY] * (2 * n),
        out_specs=(SEM_SPEC, SEM_SPEC, *[HBM_ONLY] * (2 * n), pl.BlockSpec(memory_space=pltpu.VMEM)),
        input_output_aliases={k: 2 + k for k in range(2 * n)},
        compiler_params=pltpu.CompilerParams(has_side_effects=EFFECT),
    )(*[pltpu.with_memory_space_constraint(a, pltpu.HBM) for a in list(srcs) + lands])
    return (res[0], res[1], list(res[2:2 + n]), list(res[2 + n:2 + 2 * n])), res[-1]


def _all8_wait(started, after, name):
    send_sems, recv_sems, srcs, lands = started
    n = len(srcs)

    def body(*refs):
        for cp in _all8_copies(refs[:n], refs[n:2 * n], refs[2 * n], refs[2 * n + 1], landed=True):
            cp.wait_send()
            cp.wait_recv()

    hbm = lambda a: pltpu.HBM(a.shape, a.dtype)
    res = pl.pallas_call(
        body, name=name, out_shape=tuple(hbm(a) for a in srcs) + tuple(hbm(a) for a in lands),
        in_specs=[HBM_ONLY] * (2 * n) + [SEM_SPEC, SEM_SPEC, HBM_SPEC], out_specs=tuple([HBM_ONLY] * (2 * n)),
        input_output_aliases={k: k for k in range(2 * n)},
        compiler_params=pltpu.CompilerParams(has_side_effects=EFFECT),
    )(*srcs, *lands, send_sems, recv_sems, after)
    return list(res[:n]), list(res[n:])


def _sum8(own, land, lin, name):
    r, c = own.shape
    tr = _row_tile(r)

    def body(lin_ref, own_ref, land_ref, o_ref):
        me = lin_ref[0]
        acc = None
        for k in range(N_DEV):
            term = jnp.where(me == k, own_ref[...], land_ref[k])
            acc = term if acc is None else acc + term
        o_ref[...] = acc

    return pl.pallas_call(
        body, out_shape=jax.ShapeDtypeStruct((r, c), F32),
        grid_spec=pltpu.PrefetchScalarGridSpec(
            num_scalar_prefetch=1, grid=(r // tr,),
            in_specs=[pl.BlockSpec((tr, c), lambda i, s: (i, 0)), pl.BlockSpec((N_DEV, tr, c), lambda i, s: (0, i, 0))],
            out_specs=pl.BlockSpec((tr, c), lambda i, s: (i, 0))),
        compiler_params=_cparams(("parallel",)), name=name)(lin, own, land)


def _adamw_math(w, m, v, g):
    nm = ADAM_B1 * m + (1.0 - ADAM_B1) * g
    nv = ADAM_B2 * v + (1.0 - ADAM_B2) * (g * g)
    m_hat = nm * (1.0 / (1.0 - ADAM_B1 ** ADAM_STEP))
    v_hat = nv * (1.0 / (1.0 - ADAM_B2 ** ADAM_STEP))
    return -ADAM_LR * (m_hat / (jnp.sqrt(v_hat) + ADAM_EPS) + ADAM_WD * w), nm, nv


def _adamw(ws, ms, vs, gs, name):
    na = len(ws)
    r, c = ws[0].shape
    tr = _row_tile(r)

    def body(*refs):
        for a in range(na):
            w_ref, m_ref, v_ref, g_ref = refs[4 * a:4 * a + 4]
            go_ref, d_ref, nm_ref, nv_ref = refs[4 * na + 4 * a:4 * na + 4 * a + 4]
            g = g_ref[...]
            go_ref[...] = g
            d_ref[...], nm_ref[...], nv_ref[...] = _adamw_math(w_ref[...], m_ref[...], v_ref[...], g)

    blk = pl.BlockSpec((tr, c), lambda i: (i, 0))
    sh = jax.ShapeDtypeStruct((r, c), F32)
    args = [a for group in zip(ws, ms, vs, gs) for a in group]
    res = pl.pallas_call(body, out_shape=[sh] * (4 * na), grid=(r // tr,), in_specs=[blk] * (4 * na),
                         out_specs=[blk] * (4 * na), compiler_params=_cparams(("parallel",)), name=name)(*args)
    return [tuple(res[4 * a:4 * a + 4]) for a in range(na)]


def _adamw_small(ws, ms, vs, alls, split, name, owns=None, lin=None):
    n = len(ws)
    lead = split if split is not None else ()
    nl = len(lead)
    nslots = alls[0].shape[0]
    has_own = owns is not None
    nin = 5 if has_own else 4

    def blocks(shape):
        if split is None:
            return tuple(shape), (lambda *g: (0,) * len(shape))
        blk = (shape[0], shape[1] // lead[0], shape[2] // lead[1]) + tuple(shape[3:])
        return blk, (lambda *g: (0, g[0], g[1]) + (0,) * (len(shape) - 3))

    def body(*refs):
        if has_own:
            lin_ref, refs = refs[0], refs[1:]
        w_refs, m_refs, v_refs, a_refs = (refs[k * n:(k + 1) * n] for k in range(4))
        own_refs = refs[4 * n:5 * n] if has_own else None
        g_refs, d_refs, nm_refs, nv_refs = (refs[(nin + k) * n:(nin + 1 + k) * n] for k in range(4))
        k = pl.program_id(nl)
        for i in range(n):
            term = a_refs[i][0]
            if has_own:
                term = jnp.where(k == lin_ref[0], own_refs[i][...], term)

            @pl.when(k == 0)
            def _(i=i, term=term):
                g_refs[i][...] = term

            @pl.when(k > 0)
            def _(i=i, term=term):
                g_refs[i][...] += term

            @pl.when(k == nslots - 1)
            def _(i=i):
                d_refs[i][...], nm_refs[i][...], nv_refs[i][...] = _adamw_math(
                    w_refs[i][...], m_refs[i][...], v_refs[i][...], g_refs[i][...])

    specs, aspecs, shapes = [], [], []
    for wa in ws:
        blk, imap = blocks(wa.shape)
        specs.append(pl.BlockSpec(blk, imap))
        aspecs.append(pl.BlockSpec((1,) + blk, (lambda *g, imap=imap: (g[nl],) + imap(*g))))
        shapes.append(jax.ShapeDtypeStruct(wa.shape, F32))
    grid = tuple(lead) + (nslots,)
    sem = _cparams(("parallel",) * nl + ("arbitrary",))
    if has_own:
        res = pl.pallas_call(
            body, out_shape=shapes * 4,
            grid_spec=pltpu.PrefetchScalarGridSpec(num_scalar_prefetch=1, grid=grid,
                                                   in_specs=specs * 3 + aspecs + specs, out_specs=specs * 4),
            compiler_params=sem, name=name)(lin, *ws, *ms, *vs, *alls, *owns)
    else:
        res = pl.pallas_call(body, out_shape=shapes * 4, grid=grid, in_specs=specs * 3 + aspecs,
                             out_specs=specs * 4, compiler_params=sem, name=name)(*ws, *ms, *vs, *alls)
    return res[:n], res[n:2 * n], res[2 * n:3 * n], res[3 * n:]


def kernel(x, norm_ffn1, ffn1_w_gate, ffn1_w_up, ffn1_w_down, norm_mix, w_in, attn_sinks, ssm_lambda_re, ssm_lambda_im, ssm_log_dt, ssm_b_re, ssm_b_im, ssm_c_re, ssm_c_im, ssm_d, ssm_glu_w, ssm_glu_b, attn_out_norm, ssm_out_norm, w_out, norm_ffn2, ffn2_w_gate, ffn2_w_up, ffn2_w_down, final_norm, loss_target, m_norm_ffn1, m_ffn1_w_gate, m_ffn1_w_up, m_ffn1_w_down, m_norm_mix, m_w_in, m_attn_sinks, m_ssm_lambda_re, m_ssm_lambda_im, m_ssm_log_dt, m_ssm_b_re, m_ssm_b_im, m_ssm_c_re, m_ssm_c_im, m_ssm_d, m_ssm_glu_w, m_ssm_glu_b, m_attn_out_norm, m_ssm_out_norm, m_w_out, m_norm_ffn2, m_ffn2_w_gate, m_ffn2_w_up, m_ffn2_w_down, m_final_norm, v_norm_ffn1, v_ffn1_w_gate, v_ffn1_w_up, v_ffn1_w_down, v_norm_mix, v_w_in, v_attn_sinks, v_ssm_lambda_re, v_ssm_lambda_im, v_ssm_log_dt, v_ssm_b_re, v_ssm_b_im, v_ssm_c_re, v_ssm_c_im, v_ssm_d, v_ssm_glu_w, v_ssm_glu_b, v_attn_out_norm, v_ssm_out_norm, v_w_out, v_norm_ffn2, v_ffn2_w_gate, v_ffn2_w_up, v_ffn2_w_down, v_final_norm):
    given = dict(locals())
    wts = {n: given[n] for n in WEIGHTS}

    order = [g for g in GROUPS]
    cx, cy = lax.axis_index("x"), lax.axis_index("y")
    slots = jnp.stack([2 * cx + cy, 2 * (1 - cx) + cy, 2 * cx + 1 - cy, 2 * (1 - cx) + 1 - cy,
                       lax.axis_index("c")]).astype(jnp.int32)
    def view(a, n):
        if n in TRANSPOSED:
            return jnp.swapaxes(a[0], 0, 1)
        if n in BIG:
            return a[0]
        if n in ('ssm_b_re', 'ssm_b_im'):
            return jnp.swapaxes(a, -1, -2)
        return a.reshape(1, -1) if a.ndim == 1 else a

    def unview(a, n):
        if n in TRANSPOSED:
            return jnp.swapaxes(a, 0, 1)[None]
        if n in ('ssm_b_re', 'ssm_b_im'):
            return jnp.swapaxes(a, -1, -2)
        return a.reshape(wts[n].shape)

    started, gather_token = {}, None
    for g in order:
        shards = [view(wts[n], n).astype(BF16) for n in GROUPS[g]]
        if len({s.shape for s in shards}) == 1:
            placed = _place_own(shards, slots, f"weights_place_{g}")
        else:
            placed = [_place_own([s], slots, f"weights_place_{n}")[0] for n, s in zip(GROUPS[g], shards)]
        st, gather_token = _exchange_start([(shards, placed)], False, f"weights_start_{g}", after=gather_token)
        started[g] = st[0]

    def get_weights(group, after):
        if group == order[0]:
            after = after + gather_token[:1, :1]
        _, lands = _exchange_wait(started[group], after, False, f"weights_wait_{group}")
        lands = _sibling_forward(lands, f"weights_forward_{group}")
        out = dict(zip(GROUPS[group], lands))
        for n in ('w_in', 'ssm_glu_w', 'w_out'):
            if n in out:
                out[n] = out[n].reshape(-1, out[n].shape[-1])
        return out

    sent, tokens = {}, {}

    def put_grads(group, gd):
        parts = []
        for n in GROUPS[group]:
            g = gd[n]
            if g.ndim == 2:
                g = g.reshape(N_CHIPS, g.shape[0] // N_CHIPS, g.shape[1])
            parts.append(g.astype(BF16))
        sib = _half_swap(parts, f"grads_half_swap_{group}")
        same = len({p.shape for p in parts}) == 1
        batches = [list(range(len(parts)))] if same else [[i] for i in range(len(parts))]
        halves = [None] * len(parts)
        for b in batches:
            res = _half_add([parts[i] for i in b], [sib[i] for i in b], slots, f"grads_half_add_{GROUPS[group][b[0]]}")
            for i, h in zip(b, res):
                halves[i] = h
        parts = halves
        lands = [lax.empty(p.shape, p.dtype) for p in parts]
        started_g, tokens[group] = _exchange_start([(parts, lands)], True, f"grads_start_{group}")
        sent[group] = started_g[0]
        return tokens[group]

    w = {n: (wts[n][0] if wts[n].ndim > 1 else wts[n]) for n in SMALL}
    w['norm_ffn1'], w['norm_mix'], w['norm_ffn2'] = wts['norm_ffn1'], wts['norm_mix'], wts['norm_ffn2']
    w['ssm_b_re'], w['ssm_b_im'] = view(wts['ssm_b_re'], 'ssm_b_re')[0], view(wts['ssm_b_im'], 'ssm_b_im')[0]
    w['ssm_log_dt'] = w['ssm_log_dt'] + gather_token[0, 0]
    wide =['ssm_b_re', 'ssm_b_im', 'ssm_c_re', 'ssm_c_im']

    nat = {n: view(wts[n], n).shape for n in SMALL}
    narrow = [n for n in SMALL if n not in wide]
    wide_started, narrow_started = [], []

    def reduce_wide(gd):
        packed = jnp.concatenate([gd[n].reshape(-1, LANES) for n in wide])
        started_w, token = _all8_start([packed], "small_grads_start")
        wide_started.append(started_w)
        return token

    def put_narrow(gd, loss_row):
        started_n, token = _all8_start([gd[n].reshape(nat[n]) for n in narrow] + [loss_row], "narrow_grads_start")
        narrow_started.append(started_n)
        return token

    dx, grads = _local_step(x[0], loss_target[0], w, get_weights, put_grads, reduce_wide, put_narrow)

    out_g, out_d, out_m, out_v = {}, {}, {}, {}

    def finish(group, after):
        names = GROUPS[group]
        parts, recv = _exchange_wait(sent[group], after, True, f"grads_wait_{group}")
        same = len({p.shape for p in parts}) == 1
        batches = [list(range(len(names)))] if same else [[i] for i in range(len(names))]
        sums = [None] * len(names)
        for b in batches:
            res = _sum_parts([parts[i] for i in b], [recv[i] for i in b], slots, f"grad_sum_{names[b[0]]}")
            for i, sm in zip(b, res):
                sums[i] = sm
        full = _half_forward(sums, f"grad_half_forward_{group}")
        for b in batches:
            res = _adamw([view(wts[names[i]], names[i]) for i in b], [view(given['m_' + names[i]], names[i]) for i in b],
                         [view(given['v_' + names[i]], names[i]) for i in b], [full[i] for i in b],
                         f"adamw_{names[b[0]]}")
            for i, (g, d, nm, nv) in zip(b, res):
                n = names[i]
                out_g[n], out_d[n], out_m[n], out_v[n] = (unview(a, n) for a in (g, d, nm, nv))
        return nv

    done = finish('ffn2', tokens['ffn1'])
    done = finish('mix', done)

    lin = (4 * cx + 2 * cy + lax.axis_index("c")).astype(jnp.int32).reshape(1)
    (own_w,), (land_w,) = _all8_wait(wide_started[0], done, "small_grads_wait")
    wide_sum = _sum8(own_w, land_w, lin, "small_grads_sum")
    rows = wide_sum.shape[0] // len(wide)
    wide_g = [wide_sum[i * rows:(i + 1) * rows].reshape((1,) + nat[n]) for i, n in enumerate(wide)]
    owns_n, lands_n = _all8_wait(narrow_started[0], done, "narrow_grads_wait")
    loss_shares = jnp.where(jnp.arange(N_DEV) == lin[0], owns_n[-1][0, 0], lands_n[-1][:, 0, 0])
    loss = jnp.sum(loss_shares)
    for group, gs, owns, split, tag in ((narrow, lands_n[:-1], owns_n[:-1], None, "adamw_small"),
                                        (wide, wide_g, None, (2, 4), "adamw_ssm_bc")):
        res = _adamw_small([view(wts[n], n) for n in group], [view(given['m_' + n], n) for n in group],
                           [view(given['v_' + n], n) for n in group], gs, split, tag, owns=owns,
                           lin=lin if owns is not None else None)
        for dst, vals in zip((out_g, out_d, out_m, out_v), res):
            for n, a in zip(group, vals):
                dst[n] = unview(a, n)

    finish('ffn1', out_v['norm_ffn1'][:, :1] + out_v['ssm_c_re'].reshape(1, -1)[:, :1] + done[:1, :1] + loss)

    return (loss, dx[None], *[out_g[n] for n in WEIGHTS], *[out_d[n] for n in WEIGHTS],
            *[out_m[n] for n in WEIGHTS], *[out_v[n] for n in WEIGHTS])
```

```python
import functools
import math

import numpy as np
import jax
import jax.numpy as jnp
from jax import lax
from jax.experimental import pallas as pl
from jax.experimental.pallas import tpu as pltpu

F32 = jnp.float32
BF16 = jnp.bfloat16
MESH = pl.DeviceIdType.MESH

EPS = 1e-6
NEG_INF = -1e30
LAMBDA_RE_MAX = -1e-4
ATTN_HEADS = 8
KV_HEADS = 2
GQ = ATTN_HEADS // KV_HEADS
HEAD_DIM = 64
ATTN_WIDTH = 512
KV_WIDTH = 128
WINDOW = 128
QBLOCK = 128
SSM_WIDTH = 512
SSM_GROUPS = 32
SSM_CH = 16
SSM_STATE = 64
N_STRIPS = 4
STRIP_IN = SSM_WIDTH // N_STRIPS
STRIP_ST = SSM_GROUPS * SSM_STATE // N_STRIPS
SUBLANES = 8
LANES = 128
N_CHIPS = 4
N_DEV = 8

ADAM_LR = 0.001
ADAM_B1 = 0.9
ADAM_B2 = 0.999
ADAM_EPS = 1e-08
ADAM_WD = 0.01
ADAM_STEP = 10

VMEM_LIMIT = 48 * 1024 * 1024

WEIGHTS = ['norm_ffn1', 'ffn1_w_gate', 'ffn1_w_up', 'ffn1_w_down', 'norm_mix', 'w_in', 'attn_sinks',
           'ssm_lambda_re', 'ssm_lambda_im', 'ssm_log_dt', 'ssm_b_re', 'ssm_b_im', 'ssm_c_re', 'ssm_c_im',
           'ssm_d', 'ssm_glu_w', 'ssm_glu_b', 'attn_out_norm', 'ssm_out_norm', 'w_out', 'norm_ffn2',
           'ffn2_w_gate', 'ffn2_w_up', 'ffn2_w_down', 'final_norm']
BIG = ['ffn1_w_gate', 'ffn1_w_up', 'ffn1_w_down', 'w_in', 'ssm_glu_w', 'w_out',
       'ffn2_w_gate', 'ffn2_w_up', 'ffn2_w_down']
SMALL = [n for n in WEIGHTS if n not in BIG]
TRANSPOSED = ['ffn1_w_gate', 'ffn1_w_up', 'w_in', 'ffn2_w_gate', 'ffn2_w_up']
GROUPS = {'ffn1': ['ffn1_w_gate', 'ffn1_w_up', 'ffn1_w_down'],
          'mix': ['w_in', 'ssm_glu_w', 'w_out'],
          'ffn2': ['ffn2_w_gate', 'ffn2_w_up', 'ffn2_w_down']}


def _cparams(sem=None):
    return pltpu.CompilerParams(dimension_semantics=sem, vmem_limit_bytes=VMEM_LIMIT)


def _tile(n, pref):
    if n <= pref:
        return n
    for t in (pref, pref // 2, pref // 4):
        if t % LANES == 0 and n % t == 0:
            return t
    return n


def _sigmoid(x):
    return 1.0 / (1.0 + jnp.exp(-x))


def _sigmoid_tanh(x):
    return 0.5 * jnp.tanh(0.5 * x) + 0.5


def _mm(a, b, *, ta=False, tb=False, reduce_s=False, res=None, scale=1.0, out_dtype=F32, after=None, name):
    a3 = a if a.ndim == 3 else a[None]
    b3 = b if b.ndim == 3 else b[None]
    sa, sb = a3.shape[0], b3.shape[0]
    ns = max(sa, sb)
    (kk, m) = a3.shape[1:] if ta else a3.shape[1:][::-1]
    (n, kb) = b3.shape[1:] if tb else b3.shape[1:][::-1]
    assert kk == kb, (a3.shape, b3.shape)
    tm, tn, tk = _tile(m, 1024), _tile(n, 1024), _tile(kk, 2048)
    nm, nn, nk = m // tm, n // tn, kk // tk
    has_res = res is not None
    single = nk == 1 and not (reduce_s and ns > 1)

    if reduce_s:
        grid = (nm, nn, ns, nk)
        ids = lambda i, j, s, k: (s, i, j, k)
        sem = ("parallel", "parallel", "arbitrary", "arbitrary")
    else:
        grid = (ns, nm, nn, nk)
        ids = lambda s, i, j, k: (s, i, j, k)
        sem = ("parallel", "parallel", "parallel", "arbitrary")

    def a_map(*g):
        s, i, j, k = ids(*g)
        s = s if sa > 1 else 0
        return (s, k, i) if ta else (s, i, k)

    def b_map(*g):
        s, i, j, k = ids(*g)
        s = s if sb > 1 else 0
        return (s, j, k) if tb else (s, k, j)

    def o_map(*g):
        s, i, j, k = ids(*g)
        return (i, j) if reduce_s else (s, i, j)

    a_blk = (1, tk, tm) if ta else (1, tm, tk)
    b_blk = (1, tn, tk) if tb else (1, tk, tn)
    dims = (((0 if ta else 1,), (1 if tb else 0,)), ((), ()))

    def body(*refs):
        a_ref, b_ref = refs[0], refs[1]
        r_ref = refs[2] if has_res else None
        o_ref = refs[2 + has_res + (after is not None)]
        acc_ref = None if single else refs[-1]
        s, _, _, k = ids(*[pl.program_id(d) for d in range(4)])
        prod = lax.dot_general(a_ref[0].astype(BF16), b_ref[0].astype(BF16), dims, preferred_element_type=F32)

        def finish(out):
            if scale != 1.0:
                out = out * scale
            if has_res:
                out = r_ref[...].reshape(out.shape) + out
            o_ref[...] = out.astype(out_dtype).reshape(o_ref.shape)

        if single:
            finish(prod)
            return
        if reduce_s:
            first = jnp.logical_and(s == 0, k == 0)
            last = jnp.logical_and(s == ns - 1, k == nk - 1)
        else:
            first, last = k == 0, k == nk - 1

        acc_ref[...] = prod + jnp.where(first, 0.0, acc_ref[...])

        @pl.when(last)
        def _():
            finish(acc_ref[...])

    in_specs = [pl.BlockSpec(a_blk, a_map), pl.BlockSpec(b_blk, b_map)]
    args = [a3, b3]
    if reduce_s:
        out_shape = jax.ShapeDtypeStruct((m, n), out_dtype)
        o_spec = pl.BlockSpec((tm, tn), o_map)
    else:
        out_shape = jax.ShapeDtypeStruct((ns, m, n), out_dtype)
        o_spec = pl.BlockSpec((1, tm, tn), o_map)
    if has_res:
        assert res.shape == out_shape.shape
        in_specs.append(o_spec)
        args.append(res)
    if after is not None:
        in_specs.append(HBM_SPEC)
        args.append(after)
    return pl.pallas_call(body, out_shape=out_shape, grid=grid, in_specs=in_specs, out_specs=o_spec,
                          scratch_shapes=[] if single else [pltpu.VMEM((tm, tn), F32)],
                          compiler_params=_cparams(sem), name=name)(*args)


def _row_tile(t, cap=256):
    for step in (16, SUBLANES):
        for tr in range(min(cap, t) // step * step, 0, -step):
            if t % tr == 0:
                return tr
    return t


def _norm_mm(xs, gs, w, *, tb, res, name):
    nx = len(xs)
    t = xs[0].shape[0]
    widths = [x.shape[1] for x in xs]
    k = sum(widths)
    n = w.shape[0] if tb else w.shape[1]
    tm = _tile(t, 512)
    tn = n if n <= 1536 else _tile(n, 512)
    has_res = res is not None

    def body(*refs):
        x_refs, g_refs, w_ref = refs[:nx], refs[nx:2 * nx], refs[2 * nx]
        r_ref = refs[2 * nx + 1] if has_res else None
        o_ref, h_ref, h_sc = refs[2 * nx + 1 + has_res:]

        @pl.when(pl.program_id(1) == 0)
        def _():
            off = 0
            for x_ref, g_ref, wd in zip(x_refs, g_refs, widths):
                xv = x_ref[...]
                r = lax.rsqrt(jnp.mean(xv * xv, axis=-1, keepdims=True) + EPS)
                h_sc[:, off:off + wd] = (xv * r * g_ref[...]).astype(BF16)
                off += wd
            h_ref[...] = h_sc[...]

        prod = lax.dot_general(h_sc[...], w_ref[...], NT_DIMS if tb else (((1,), (0,)), ((), ())),
                               preferred_element_type=F32)
        o_ref[...] = r_ref[...] + prod if has_res else prod

    in_specs = [pl.BlockSpec((tm, wd), lambda i, j: (i, 0)) for wd in widths]
    in_specs += [pl.BlockSpec((1, wd), lambda i, j: (0, 0)) for wd in widths]
    in_specs.append(pl.BlockSpec((tn, k), lambda i, j: (j, 0)) if tb else pl.BlockSpec((k, tn), lambda i, j: (0, j)))
    tile = pl.BlockSpec((tm, tn), lambda i, j: (i, j))
    if has_res:
        in_specs.append(tile)
    return pl.pallas_call(
        body, out_shape=(jax.ShapeDtypeStruct((t, n), F32), jax.ShapeDtypeStruct((t, k), BF16)),
        grid=(t // tm, n // tn), in_specs=in_specs,
        out_specs=(tile, pl.BlockSpec((tm, k), lambda i, j: (i, 0))),
        scratch_shapes=[pltpu.VMEM((tm, k), BF16)], compiler_params=_cparams(("parallel", "arbitrary")),
        name=name)(*xs, *gs, w, *([res] if has_res else []))


def _rms_bwd_rows(xv, gv, dhv):
    r = lax.rsqrt(jnp.mean(xv * xv, axis=-1, keepdims=True) + EPS)
    nrm = xv * r
    dn = dhv * gv
    return r * (dn - nrm * jnp.mean(dn * nrm, axis=-1, keepdims=True)), dhv * nrm


def _mm_rms_bwd(a, b, *, tb, xs, gs, dres, after, name):
    nx = len(xs)
    t, k = a.shape
    widths = [x.shape[1] for x in xs]
    n = sum(widths)
    assert n == (b.shape[0] if tb else b.shape[1])
    tm = _tile(t, 512)
    has_res, has_after = dres is not None, after is not None

    def body(*refs):
        a_ref, b_ref = refs[0], refs[1]
        x_refs, g_refs = refs[2:2 + nx], refs[2 + nx:2 + 2 * nx]
        r_ref = refs[2 + 2 * nx] if has_res else None
        outs = refs[2 + 2 * nx + has_res + has_after:]
        dh = lax.dot_general(a_ref[...], b_ref[...], NT_DIMS if tb else (((1,), (0,)), ((), ())),
                             preferred_element_type=F32)
        off = 0
        for i, wd in enumerate(widths):
            dx_ref, dxb_ref, dg_ref = outs[3 * i:3 * i + 3]
            dx, dgs = _rms_bwd_rows(x_refs[i][...], g_refs[i][...], dh[:, off:off + wd])
            if has_res:
                dx = dx + r_ref[...]
            dx_ref[...] = dx
            dxb_ref[...] = dx.astype(BF16)
            part = jnp.sum(dgs, axis=0, keepdims=True)
            dg_ref[...] = part + jnp.where(pl.program_id(0) > 0, dg_ref[...], 0.0)
            off += wd

    in_specs = [pl.BlockSpec((tm, k), lambda i: (i, 0)), pl.BlockSpec(b.shape, lambda i: (0, 0))]
    in_specs += [pl.BlockSpec((tm, wd), lambda i: (i, 0)) for wd in widths]
    in_specs += [pl.BlockSpec((1, wd), lambda i: (0, 0)) for wd in widths]
    args = [a, b, *xs, *gs]
    if has_res:
        in_specs.append(pl.BlockSpec((tm, widths[0]), lambda i: (i, 0)))
        args.append(dres)
    if has_after:
        in_specs.append(HBM_SPEC)
        args.append(after)
    out_shape, out_specs = [], []
    for wd in widths:
        out_shape += [jax.ShapeDtypeStruct((t, wd), F32), jax.ShapeDtypeStruct((t, wd), BF16),
                      jax.ShapeDtypeStruct((1, wd), F32)]
        out_specs += [pl.BlockSpec((tm, wd), lambda i: (i, 0)), pl.BlockSpec((tm, wd), lambda i: (i, 0)),
                      pl.BlockSpec((1, wd), lambda i: (0, 0))]
    res = pl.pallas_call(body, out_shape=out_shape, grid=(t // tm,), in_specs=in_specs, out_specs=out_specs,
                         compiler_params=_cparams(("arbitrary",)), name=name)(*args)
    return [tuple(res[3 * i:3 * i + 3]) for i in range(nx)]


FFN_ROWS = 512
FFN_FWD_ROWS = 1024
FFN_SPLIT = 2
FFN_W_ROWS = 1024
SCAN_ROWS = 256


NT_DIMS = (((1,), (1,)), ((), ()))
TN_DIMS = (((0,), (0,)), ((), ()))


def _ffn_fwd_call(x, g, wg, wu, wd, name):
    t, d = x.shape
    ns, f, _ = wg.shape
    tm = _tile(t, FFN_FWD_ROWS)

    def body(x_ref, g_ref, wg_ref, wu_ref, wd_ref, xo_ref, h_ref, gate_ref, up_ref, h_sc, acc_ref):
        s = pl.program_id(1)

        @pl.when(s == 0)
        def _():
            xv = x_ref[...]
            r = lax.rsqrt(jnp.mean(xv * xv, axis=-1, keepdims=True) + EPS)
            hb = (xv * r * g_ref[...]).astype(BF16)
            h_sc[...] = hb
            h_ref[...] = hb

        for r0 in range(0, tm, tm // FFN_SPLIT):
            rows = slice(r0, r0 + tm // FFN_SPLIT)
            hb = h_sc[rows, :]
            gate = lax.dot_general(hb, wg_ref[0], NT_DIMS, preferred_element_type=F32)
            up = lax.dot_general(hb, wu_ref[0], NT_DIMS, preferred_element_type=F32)
            gate_ref[0, rows, :] = gate.astype(BF16)
            up_ref[0, rows, :] = up.astype(BF16)
            act = (gate * _sigmoid_tanh(gate) * up).astype(BF16)
            prod = jnp.dot(act, wd_ref[0], preferred_element_type=F32)
            acc_ref[rows, :] = prod + jnp.where(s > 0, acc_ref[rows, :], 0.0)

        @pl.when(s == ns - 1)
        def _():
            xo_ref[...] = x_ref[...] + 0.5 * acc_ref[...]

    row = pl.BlockSpec((tm, d), lambda i, s: (i, 0))
    vec = pl.BlockSpec((1, d), lambda i, s: (0, 0))
    wrow = pl.BlockSpec((1, f, d), lambda i, s: (s, 0, 0))
    hid = pl.BlockSpec((1, tm, f), lambda i, s: (s, i, 0))
    hid_sh = jax.ShapeDtypeStruct((ns, t, f), BF16)
    return pl.pallas_call(
        body, out_shape=(jax.ShapeDtypeStruct((t, d), F32), jax.ShapeDtypeStruct((t, d), BF16), hid_sh, hid_sh),
        grid=(t // tm, ns), in_specs=[row, vec, wrow, wrow, wrow], out_specs=(row, row, hid, hid),
        scratch_shapes=[pltpu.VMEM((tm, d), BF16), pltpu.VMEM((tm, d), F32)],
        compiler_params=_cparams(("parallel", "arbitrary")), name=name)(x, g, wg, wu, wd)


def _ffn_bwd_x_call(dxo, dxo_b, x, g, gate, up, wg, wu, wd, name):
    t, d = x.shape
    ns, f, _ = wg.shape
    tm = _tile(t, FFN_ROWS)

    def body(dxo_ref, dxb_ref, x_ref, g_ref, gate_ref, up_ref, wg_ref, wu_ref, wd_ref,
             dx_ref, dxob_ref, dgn_ref, dgate_ref, dup_ref, act_ref, dh_ref):
        s, i = pl.program_id(0), pl.program_id(1)
        base = pl.multiple_of(i * tm, tm)
        for r0 in range(0, tm, tm // FFN_SPLIT):
            rows = slice(r0, r0 + tm // FFN_SPLIT)
            acc_rows = pl.ds(base + r0, tm // FFN_SPLIT)
            dact = lax.dot_general(dxb_ref[rows, :], wd_ref[0], NT_DIMS, preferred_element_type=F32) * 0.5
            gv = gate_ref[0, rows, :].astype(F32)
            uv = up_ref[0, rows, :].astype(F32)
            sg = _sigmoid_tanh(gv)
            silu = gv * sg
            act_ref[0, rows, :] = (silu * uv).astype(BF16)
            dub = (dact * silu).astype(BF16)
            dgb = (dact * uv * sg * (1.0 + gv * (1.0 - sg))).astype(BF16)
            dup_ref[0, rows, :] = dub
            dgate_ref[0, rows, :] = dgb
            prod = (jnp.dot(dgb, wg_ref[0], preferred_element_type=F32)
                    + jnp.dot(dub, wu_ref[0], preferred_element_type=F32))

            dh_ref[acc_rows, :] = prod + jnp.where(s > 0, dh_ref[acc_rows, :], 0.0)

        @pl.when(jnp.logical_and(i == 0, s == 0))
        def _():
            dgn_ref[...] = jnp.zeros_like(dgn_ref)

        @pl.when(s == ns - 1)
        def _():
            dx, dgs = _rms_bwd_rows(x_ref[...], g_ref[...], dh_ref[pl.ds(base, tm), :])
            dx = dx + dxo_ref[...]
            dx_ref[...] = dx
            dxob_ref[...] = dx.astype(BF16)
            dgn_ref[...] += jnp.sum(dgs, axis=0, keepdims=True)

    last_only = lambda s, i: (jnp.where(s == ns - 1, i, 0), 0)
    row_last = pl.BlockSpec((tm, d), last_only)
    row = pl.BlockSpec((tm, d), lambda s, i: (i, 0))
    vec = pl.BlockSpec((1, d), lambda s, i: (0, 0))
    wrow = pl.BlockSpec((1, f, d), lambda s, i: (s, 0, 0))
    hid = pl.BlockSpec((1, tm, f), lambda s, i: (s, i, 0))
    hid_sh = jax.ShapeDtypeStruct((ns, t, f), BF16)
    return pl.pallas_call(
        body,
        out_shape=(jax.ShapeDtypeStruct((t, d), F32), jax.ShapeDtypeStruct((t, d), BF16),
                   jax.ShapeDtypeStruct((1, d), F32), hid_sh, hid_sh, hid_sh),
        grid=(ns, t // tm), in_specs=[row_last, row, row_last, vec, hid, hid, wrow, wrow, wrow],
        out_specs=(row_last, row_last, vec, hid, hid, hid), scratch_shapes=[pltpu.VMEM((t, d), F32)],
        compiler_params=_cparams(("arbitrary", "arbitrary")), name=name)(dxo, dxo_b, x, g, gate, up, wg, wu, wd)


def _ffn_bwd_w_call(h, dxo_b, dgate, dup, act, name, after=None):
    t, d = h.shape
    ns, _, f = dgate.shape
    tm = _tile(t, FFN_W_ROWS)
    nm = t // tm

    def body(h_ref, dxb_ref, dgate_ref, dup_ref, act_ref, *rest):
        dwg_ref, dwu_ref, dwd_ref, ag_ref, au_ref, ad_ref = rest[-6:]
        i = pl.program_id(1)
        hv = h_ref[...]
        pg = lax.dot_general(dgate_ref[0], hv, TN_DIMS, preferred_element_type=F32)
        pu = lax.dot_general(dup_ref[0], hv, TN_DIMS, preferred_element_type=F32)
        pd = lax.dot_general(act_ref[0], dxb_ref[...], TN_DIMS, preferred_element_type=F32)

        ag_ref[...] = pg + jnp.where(i > 0, ag_ref[...], 0.0)
        au_ref[...] = pu + jnp.where(i > 0, au_ref[...], 0.0)
        ad_ref[...] = pd + jnp.where(i > 0, ad_ref[...], 0.0)

        @pl.when(i == nm - 1)
        def _():
            dwg_ref[0] = ag_ref[...].astype(BF16)
            dwu_ref[0] = au_ref[...].astype(BF16)
            dwd_ref[0] = (0.5 * ad_ref[...]).astype(BF16)

    row = pl.BlockSpec((tm, d), lambda s, i: (i, 0))
    hid = pl.BlockSpec((1, tm, f), lambda s, i: (s, i, 0))
    wrow = pl.BlockSpec((1, f, d), lambda s, i: (s, 0, 0))
    wsh = jax.ShapeDtypeStruct((ns, f, d), BF16)
    return pl.pallas_call(
        body, out_shape=(wsh, wsh, wsh),
        grid=(ns, nm), in_specs=[row, row, hid, hid, hid] + [HBM_SPEC] * (after is not None),
        out_specs=(wrow, wrow, wrow),
        scratch_shapes=[pltpu.VMEM((f, d), F32), pltpu.VMEM((f, d), F32), pltpu.VMEM((f, d), F32)],
        compiler_params=_cparams(("parallel", "arbitrary")), name=name)(
            h, dxo_b, dgate, dup, act, *([after] if after is not None else []))


def _loss_head(x, g, tgt, name):
    t, w = x.shape
    tr = _row_tile(t)

    def body(x_ref, g_ref, t_ref, loss_ref, dx_ref, dxb_ref, dg_ref):
        xv = x_ref[...]
        gv = g_ref[...]
        r = lax.rsqrt(jnp.mean(xv * xv, axis=-1, keepdims=True) + EPS)
        nrm = xv * r
        err = nrm * gv - t_ref[...]
        dout = err * (1.0 / w)
        dn = dout * gv
        dx = r * (dn - nrm * jnp.mean(dn * nrm, axis=-1, keepdims=True))
        dx_ref[...] = dx
        dxb_ref[...] = dx.astype(BF16)

        @pl.when(pl.program_id(0) == 0)
        def _():
            dg_ref[...] = jnp.zeros_like(dg_ref)
            loss_ref[...] = jnp.zeros_like(loss_ref)

        dg_ref[...] += jnp.sum(dout * nrm, axis=0, keepdims=True)
        part = jnp.sum(jnp.sum(err * err, axis=-1, keepdims=True) * (0.5 / w), axis=0, keepdims=True)
        loss_ref[...] += jnp.broadcast_to(part, loss_ref.shape)

    row = pl.BlockSpec((tr, w), lambda i: (i, 0))
    vec = pl.BlockSpec((1, w), lambda i: (0, 0))
    return pl.pallas_call(
        body, out_shape=(jax.ShapeDtypeStruct((1, LANES), F32), jax.ShapeDtypeStruct((t, w), F32),
                         jax.ShapeDtypeStruct((t, w), BF16), jax.ShapeDtypeStruct((1, w), F32)),
        grid=(t // tr,), in_specs=[row, vec, row],
        out_specs=(pl.BlockSpec((1, LANES), lambda i: (0, 0)), row, row, vec),
        compiler_params=_cparams(("arbitrary",)), name=name)(x, g, tgt)


def _attn_bias():
    slopes = np.asarray(2.0 ** (-8.0 * (np.arange(ATTN_HEADS) + 1) / ATTN_HEADS), np.float32)
    qi = np.arange(QBLOCK)[:, None]
    kj = np.arange(3 * QBLOCK)[None, :]
    rel = np.abs(kj - QBLOCK - qi).astype(np.float32)
    tile = np.where(rel <= WINDOW, -slopes[:, None, None] * rel[None], np.float32(NEG_INF)).astype(np.float32)
    tile = tile.reshape(KV_HEADS, GQ * QBLOCK, 3 * QBLOCK)
    return jnp.asarray(np.swapaxes(tile, 1, 2))


def _attn_scores(k3, q, n, nb, bias):
    s = lax.dot_general(k3, q, NT_DIMS, preferred_element_type=F32) * (HEAD_DIM ** -0.5)
    key = lax.broadcasted_iota(jnp.int32, (3 * QBLOCK, 1), 0)
    inside = (key >= jnp.where(n == 0, QBLOCK, 0)) & (key < jnp.where(n == nb - 1, 2 * QBLOCK, 3 * QBLOCK))
    return jnp.where(inside, s + bias, NEG_INF)


Q_COL, K_COL, V_COL, U_COL = 0, ATTN_WIDTH // LANES, ATTN_WIDTH // LANES + 1, ATTN_WIDTH // LANES + 2


def _key_rows(ref, n, nb):
    prev, nxt = jnp.maximum(n - 1, 0), jnp.minimum(n + 1, nb - 1)
    blk = lambda b: ref[pl.ds(pl.multiple_of(b * QBLOCK, QBLOCK), QBLOCK), :]
    return jnp.concatenate([blk(prev), blk(n), blk(nxt)], axis=0)


def _head_tiles(x, kh, low):
    tiles = []
    for g in range(GQ):
        h = GQ * kh + g
        t128 = x[:, LANES * (h // 2):LANES * (h // 2 + 1)]
        t128 = jnp.where(low if h % 2 == 0 else jnp.logical_not(low), t128, 0.0)
        if h % 2 != kh:
            t128 = pltpu.roll(t128, HEAD_DIM, 1)
        tiles.append(t128)
    return jnp.concatenate(tiles, axis=0)


def _head_merge(per_kh, low):
    out = []
    for j in range(ATTN_HEADS // 2):
        pair = []
        for h in (2 * j, 2 * j + 1):
            kh, g = h // GQ, h % GQ
            t128 = per_kh[kh][g * QBLOCK:(g + 1) * QBLOCK, :]
            if h % 2 != kh:
                t128 = pltpu.roll(t128, HEAD_DIM, 1)
            pair.append(t128)
        out.append(jnp.where(low, pair[0], pair[1]))
    return jnp.concatenate(out, axis=1)


def _attn_fwd_proj(proj, sink_rows, bias, name):
    t = proj.shape[0]
    nb = t // QBLOCK
    rows = GQ * QBLOCK

    def body(q_ref, k_ref, v_ref, sink_ref, bias_ref, o_ref, lse_ref):
        n = pl.program_id(0)
        low = lax.broadcasted_iota(jnp.int32, (QBLOCK, LANES), 1) < HEAD_DIM
        k3 = _key_rows(k_ref, n, nb).astype(BF16)
        v3 = _key_rows(v_ref, n, nb).astype(BF16)
        q = q_ref[...]
        outs = []
        for kh in range(KV_HEADS):
            qs = _head_tiles(q, kh, low).astype(BF16)
            s = _attn_scores(k3, qs, n, nb, bias_ref[kh])
            sink = sink_ref[kh]
            mx = jnp.maximum(jnp.max(s, axis=0, keepdims=True), sink)
            p = jnp.exp(s - mx)
            den = jnp.sum(p, axis=0, keepdims=True) + jnp.exp(sink - mx)
            pn = (p * (1.0 / den)).astype(BF16)
            outs.append(lax.dot_general(pn, v3, TN_DIMS, preferred_element_type=F32))
            lse_ref[0, kh] = mx + jnp.log(den)
        o_ref[...] = _head_merge(outs, low)

    strip = lambda col: pl.BlockSpec((t, LANES), lambda n, col=col: (0, col))
    rowspec = pl.BlockSpec((KV_HEADS, 1, rows), lambda n: (0, 0, 0))
    biasspec = pl.BlockSpec((KV_HEADS, 3 * QBLOCK, rows), lambda n: (0, 0, 0))
    return pl.pallas_call(
        body, out_shape=(jax.ShapeDtypeStruct((t, ATTN_WIDTH), F32), jax.ShapeDtypeStruct((nb, KV_HEADS, 1, rows), F32)),
        grid=(nb,), in_specs=[pl.BlockSpec((QBLOCK, ATTN_WIDTH), lambda n: (n, 0)), strip(K_COL), strip(V_COL),
                              rowspec, biasspec],
        out_specs=(pl.BlockSpec((QBLOCK, ATTN_WIDTH), lambda n: (n, 0)),
                   pl.BlockSpec((1, KV_HEADS, 1, rows), lambda n: (n, 0, 0, 0))),
        compiler_params=_cparams(("parallel",)), name=name)(proj, proj, proj, sink_rows, bias)


def _attn_bwd_proj(proj, sink_rows, bias, o, lse, do, name):
    t = proj.shape[0]
    nb = t // QBLOCK
    rows = GQ * QBLOCK
    scale = HEAD_DIM ** -0.5

    def body(q_ref, k_ref, v_ref, sink_ref, bias_ref, o_ref, lse_ref, do_ref, dq_ref, dk_ref, dv_ref, ds_ref):
        n = pl.program_id(0)

        @pl.when(n == 0)
        def _():
            dk_ref[...] = jnp.zeros_like(dk_ref)
            dv_ref[...] = jnp.zeros_like(dv_ref)
            ds_ref[...] = jnp.zeros_like(ds_ref)

        low = lax.broadcasted_iota(jnp.int32, (QBLOCK, LANES), 1) < HEAD_DIM
        k3 = _key_rows(k_ref, n, nb).astype(BF16)
        v3 = _key_rows(v_ref, n, nb).astype(BF16)
        q, dov = q_ref[...], do_ref[...]
        dod = dov * o_ref[...]
        dqs = []
        dk3 = jnp.zeros((3 * QBLOCK, LANES), F32)
        dv3 = jnp.zeros((3 * QBLOCK, LANES), F32)
        ones = jnp.ones((SUBLANES, LANES), F32)
        for kh in range(KV_HEADS):
            qs = _head_tiles(q, kh, low).astype(BF16)
            dos = _head_tiles(dov, kh, low).astype(BF16)
            delta = lax.dot_general(ones, _head_tiles(dod, kh, low), NT_DIMS, preferred_element_type=F32,
                                    precision=lax.Precision.HIGHEST)[0:1, :]
            lse_kh = lse_ref[0, kh]
            s = _attn_scores(k3, qs, n, nb, bias_ref[kh])
            p = jnp.exp(s - lse_kh)
            dp = lax.dot_general(v3, dos, NT_DIMS, preferred_element_type=F32)
            dsb = (p * (dp - delta)).astype(BF16)
            dqs.append(lax.dot_general(dsb, k3, TN_DIMS, preferred_element_type=F32) * scale)
            dk3 = dk3 + jnp.dot(dsb, qs, preferred_element_type=F32) * scale
            dv3 = dv3 + jnp.dot(p.astype(BF16), dos, preferred_element_type=F32)
            ds_ref[kh] += -jnp.exp(sink_ref[kh] - lse_kh) * delta
        dq_ref[...] = _head_merge(dqs, low)
        prev, nxt = jnp.maximum(n - 1, 0), jnp.minimum(n + 1, nb - 1)
        for j, b in enumerate((prev, n, nxt)):
            blk = pl.ds(pl.multiple_of(b * QBLOCK, QBLOCK), QBLOCK)
            dk_ref[blk, :] += dk3[j * QBLOCK:(j + 1) * QBLOCK, :]
            dv_ref[blk, :] += dv3[j * QBLOCK:(j + 1) * QBLOCK, :]

    strip = lambda col: pl.BlockSpec((t, LANES), lambda n, col=col: (0, col))
    rowspec = pl.BlockSpec((KV_HEADS, 1, rows), lambda n: (0, 0, 0))
    qspec = pl.BlockSpec((QBLOCK, ATTN_WIDTH), lambda n: (n, 0))
    kv_out = pl.BlockSpec((t, LANES), lambda n: (0, 0))
    biasspec = pl.BlockSpec((KV_HEADS, 3 * QBLOCK, rows), lambda n: (0, 0, 0))
    return pl.pallas_call(
        body,
        out_shape=(jax.ShapeDtypeStruct((t, ATTN_WIDTH), F32), jax.ShapeDtypeStruct((t, LANES), F32),
                   jax.ShapeDtypeStruct((t, LANES), F32), jax.ShapeDtypeStruct((KV_HEADS, 1, rows), F32)),
        grid=(nb,),
        in_specs=[qspec, strip(K_COL), strip(V_COL), rowspec, biasspec, qspec,
                  pl.BlockSpec((1, KV_HEADS, 1, rows), lambda n: (n, 0, 0, 0)), qspec],
        out_specs=(qspec, kv_out, kv_out, rowspec),
        compiler_params=_cparams(("arbitrary",)), name=name)(proj, proj, proj, sink_rows, bias, o, lse, do)


def _scan_tables(a_re, a_im, reverse):
    pw = [(a_re, a_im)]
    for _ in range(SUBLANES - 1):
        pr, pi = pw[-1]
        pw.append((pr * a_re - pi * a_im, pr * a_im + pi * a_re))
    rows = np.arange(SUBLANES)
    tabs = []
    for d in (1, 2, 4):
        mask = (rows <= SUBLANES - 1 - d) if reverse else (rows >= d)
        m = jnp.asarray(mask, F32)[:, None]
        tabs += [m * pw[d - 1][0][None, :], m * pw[d - 1][1][None, :]]
    order = (SUBLANES - 1 - rows) if reverse else rows
    tabs += [jnp.stack([pw[j][0] for j in order]), jnp.stack([pw[j][1] for j in order])]
    tab = jnp.stack(tabs)
    return tab.reshape(8, SUBLANES, N_STRIPS, STRIP_ST).transpose(2, 0, 1, 3)


def _scan_pair_chunk(dirs):
    nblk = dirs[0]['xr'].shape[0] // SUBLANES

    @pl.when(pl.program_id(1) == 0)
    def _():
        for d in dirs:
            d['carry'][...] = jnp.zeros_like(d['carry'])

    for d in dirs:
        vb = d['v'][...].astype(BF16)
        d['xr'][...] = jnp.dot(vb, d['mir'][0], preferred_element_type=F32)
        d['xi'][...] = jnp.dot(vb, d['mii'][0], preferred_element_type=F32)
    carries = [(d['carry'][0], d['carry'][1]) for d in dirs]
    for i in range(nblk):
        for k, d in enumerate(dirs):
            rev = d['reverse']
            rows = pl.ds(((nblk - 1 - i) if rev else i) * SUBLANES, SUBLANES)
            cr, ci = carries[k]
            xr, xi = d['xr'][rows, :], d['xi'][rows, :]
            for j, s in enumerate((1, 2, 4)):
                tr_, ti_ = d['tab'][0, 2 * j], d['tab'][0, 2 * j + 1]
                sh = (SUBLANES - s) if rev else s
                sr, si = pltpu.roll(xr, sh, 0), pltpu.roll(xi, sh, 0)
                xr, xi = xr + tr_ * sr - ti_ * si, xi + tr_ * si + ti_ * sr
            pr, pi = d['tab'][0, 6], d['tab'][0, 7]
            xr, xi = xr + pr * cr - pi * ci, xi + pr * ci + pi * cr
            d['xr'][rows, :] = xr
            d['xi'][rows, :] = xi
            edge = 0 if rev else SUBLANES - 1
            carries[k] = (jnp.broadcast_to(xr[edge:edge + 1, :], xr.shape),
                          jnp.broadcast_to(xi[edge:edge + 1, :], xi.shape))
    for k, d in enumerate(dirs):
        d['carry'][0], d['carry'][1] = carries[k]
        d['y'][...] = (jnp.dot(d['xr'][...].astype(BF16), d['mor'][0], preferred_element_type=F32)
                       + jnp.dot(d['xi'][...].astype(BF16), d['moi'][0], preferred_element_type=F32))


def _scan_pair(v, ops_f, ops_b, name):
    t = v.shape[0]
    tc = _tile(t, SCAN_ROWS)
    nc = t // tc

    def body(vf_ref, vb_ref, *refs):
        ops = refs[:10]
        outs = refs[10:16]
        carries = refs[16:18]
        dirs = []
        for k, (v_ref, rev) in enumerate(((vf_ref, False), (vb_ref, True))):
            mir, mii, tab, mor, moi = ops[5 * k:5 * k + 5]
            y, xr, xi = outs[3 * k:3 * k + 3]
            dirs.append(dict(v=v_ref, mir=mir, mii=mii, tab=tab, mor=mor, moi=moi, y=y, xr=xr, xi=xi,
                             carry=carries[k], reverse=rev))
        _scan_pair_chunk(dirs)

    col0 = v.shape[1] // STRIP_IN - N_STRIPS
    fmap = lambda s, c: (c, s)
    bmap = lambda s, c: (nc - 1 - c, s)
    smap3 = lambda s, c: (s, 0, 0)
    m_in = pl.BlockSpec((1, STRIP_IN, STRIP_ST), smap3)
    m_out = pl.BlockSpec((1, STRIP_ST, STRIP_IN), smap3)
    tabspec = pl.BlockSpec((1, 8, SUBLANES, STRIP_ST), lambda s, c: (s, 0, 0, 0))
    opspecs = [m_in, m_in, tabspec, m_out, m_out]
    y_sh = jax.ShapeDtypeStruct((t, SSM_WIDTH), F32)
    x_sh = jax.ShapeDtypeStruct((t, N_STRIPS * STRIP_ST), F32)
    outspecs = lambda m: [pl.BlockSpec((tc, STRIP_IN), m), pl.BlockSpec((tc, STRIP_ST), m),
                          pl.BlockSpec((tc, STRIP_ST), m)]
    res = pl.pallas_call(
        body, out_shape=[y_sh, x_sh, x_sh] * 2, grid=(N_STRIPS, nc),
        in_specs=[pl.BlockSpec((tc, STRIP_IN), lambda s, c: (c, s + col0)),
                  pl.BlockSpec((tc, STRIP_IN), lambda s, c: (nc - 1 - c, s + col0))] + opspecs * 2,
        out_specs=outspecs(fmap) + outspecs(bmap),
        scratch_shapes=[pltpu.VMEM((2, SUBLANES, STRIP_ST), F32)] * 2,
        compiler_params=_cparams(("parallel", "arbitrary")), name=name)(v, v, *ops_f, *ops_b)
    return tuple(res[:3]), tuple(res[3:])


def _scan_adjoint_pair(dy, u, states, adj_ops, name):
    t = dy.shape[0]
    tc = _tile(t, SCAN_ROWS)
    nc = t // tc
    hb = tc // SUBLANES
    n_out = 6

    def body(*refs):
        c = pl.program_id(1)
        dirs = []
        for k in range(2):
            dy_ref, mir, mii, tab, mor, moi, u_ref, xr_ref, xi_ref, hr_ref, hi_ref = refs[11 * k:11 * k + 11]
            outs = refs[22 + n_out * k:22 + n_out * (k + 1)]
            lr_ref, li_ref, carry = refs[22 + 2 * n_out + 3 * k:22 + 2 * n_out + 3 * k + 3]
            dirs.append(dict(v=dy_ref, mir=mir, mii=mii, tab=tab, mor=mor, moi=moi, y=outs[0], xr=lr_ref, xi=li_ref,
                             carry=carry, reverse=(k == 0), u=u_ref, fx=(xr_ref, xi_ref), halo=(hr_ref, hi_ref),
                             acc=outs[1:]))

        @pl.when(c == 0)
        def _():
            for d in dirs:
                for r in d['acc']:
                    r[...] = jnp.zeros_like(r)

        _scan_pair_chunk(dirs)
        for d in dirs:
            fwd_reverse = not d['reverse']
            rc = (nc - 1 - c) if d['reverse'] else c
            dmir_ref, dmii_ref, dmor_ref, dmoi_ref, da_ref = d['acc']
            xrv, xiv, lrv, liv = d['fx'][0][...], d['fx'][1][...], d['xr'][...], d['xi'][...]
            hr_ref, hi_ref = d['halo']
            row = lax.broadcasted_iota(jnp.int32, xrv.shape, 0)
            if fwd_reverse:
                live = (rc < nc - 1).astype(F32)
                edge_r, edge_i = hr_ref[0:1, :] * live, hi_ref[0:1, :] * live
                xpr = jnp.where(row == tc - 1, edge_r, pltpu.roll(xrv, tc - 1, 0))
                xpi = jnp.where(row == tc - 1, edge_i, pltpu.roll(xiv, tc - 1, 0))
            else:
                live = (rc > 0).astype(F32)
                edge_r, edge_i = hr_ref[SUBLANES - 1:SUBLANES, :] * live, hi_ref[SUBLANES - 1:SUBLANES, :] * live
                xpr = jnp.where(row == 0, edge_r, pltpu.roll(xrv, 1, 0))
                xpi = jnp.where(row == 0, edge_i, pltpu.roll(xiv, 1, 0))
            da_ref[0, 0:1, :] += jnp.sum(xpr * lrv + xpi * liv, axis=0, keepdims=True)
            da_ref[0, 1:2, :] += jnp.sum(xpr * liv - xpi * lrv, axis=0, keepdims=True)
            ub, dyb = d['u'][...].astype(BF16), d['v'][...].astype(BF16)
            dmir_ref[0] += lax.dot_general(ub, lrv.astype(BF16), TN_DIMS, preferred_element_type=F32)
            dmii_ref[0] += lax.dot_general(ub, liv.astype(BF16), TN_DIMS, preferred_element_type=F32)
            dmor_ref[0] += lax.dot_general(xrv.astype(BF16), dyb, TN_DIMS, preferred_element_type=F32)
            dmoi_ref[0] += lax.dot_general(xiv.astype(BF16), dyb, TN_DIMS, preferred_element_type=F32)

    col0 = u.shape[1] // STRIP_IN - N_STRIPS
    smap3 = lambda s, c: (s, 0, 0)
    m_in = pl.BlockSpec((1, STRIP_IN, STRIP_ST), smap3)
    m_out = pl.BlockSpec((1, STRIP_ST, STRIP_IN), smap3)
    tabspec = pl.BlockSpec((1, 8, SUBLANES, STRIP_ST), lambda s, c: (s, 0, 0, 0))
    in_specs, out_specs, args = [], [], []
    for k in range(2):
        reverse = k == 0
        rowblk = (lambda c: nc - 1 - c) if reverse else (lambda c: c)
        tmap = lambda s, c, rowblk=rowblk: (rowblk(c), s)
        umap = lambda s, c, rowblk=rowblk: (rowblk(c), s + col0)
        if not reverse:
            hmap = lambda s, c, rowblk=rowblk: (jnp.minimum((rowblk(c) + 1) * hb, t // SUBLANES - 1), s)
        else:
            hmap = lambda s, c, rowblk=rowblk: (jnp.maximum(rowblk(c) * hb - 1, 0), s)
        narrow = pl.BlockSpec((tc, STRIP_IN), tmap)
        wide = pl.BlockSpec((tc, STRIP_ST), tmap)
        halo = pl.BlockSpec((SUBLANES, STRIP_ST), hmap)
        in_specs += [narrow, m_in, m_in, tabspec, m_out, m_out, pl.BlockSpec((tc, STRIP_IN), umap), wide, wide,
                     halo, halo]
        out_specs += [narrow, m_in, m_in, m_out, m_out, pl.BlockSpec((1, SUBLANES, STRIP_ST), smap3)]
        xr, xi = states[k]
        args += [dy, *adj_ops[k], u, xr, xi, xr, xi]
    out_shape = [jax.ShapeDtypeStruct((t, SSM_WIDTH), F32),
                 jax.ShapeDtypeStruct((N_STRIPS, STRIP_IN, STRIP_ST), F32),
                 jax.ShapeDtypeStruct((N_STRIPS, STRIP_IN, STRIP_ST), F32),
                 jax.ShapeDtypeStruct((N_STRIPS, STRIP_ST, STRIP_IN), F32),
                 jax.ShapeDtypeStruct((N_STRIPS, STRIP_ST, STRIP_IN), F32),
                 jax.ShapeDtypeStruct((N_STRIPS, SUBLANES, STRIP_ST), F32)] * 2
    res = pl.pallas_call(
        body, out_shape=out_shape, grid=(N_STRIPS, nc), in_specs=in_specs, out_specs=out_specs,
        scratch_shapes=[pltpu.VMEM((tc, STRIP_ST), F32), pltpu.VMEM((tc, STRIP_ST), F32),
                        pltpu.VMEM((2, SUBLANES, STRIP_ST), F32)] * 2,
        compiler_params=_cparams(("parallel", "arbitrary")), name=name)(*args)
    return tuple(res[:n_out]), tuple(res[n_out:])


def _ssm_prep(lam_re, lam_im, log_dt, bt_re, bt_im, c_re, c_im):
    lr = jnp.minimum(lam_re, LAMBDA_RE_MAX)
    li = lam_im
    dt = jnp.exp(log_dt)[:, None]
    mag = jnp.exp(lr * dt)
    a_re = mag * jnp.cos(li * dt)
    a_im = mag * jnp.sin(li * dt)
    den = lr * lr + li * li
    coef_re = ((a_re - 1.0) * lr + a_im * li) / den
    coef_im = (a_im * lr - (a_re - 1.0) * li) / den
    bb_re = coef_re[:, None, :] * bt_re - coef_im[:, None, :] * bt_im
    bb_im = coef_re[:, None, :] * bt_im + coef_im[:, None, :] * bt_re
    eye = jnp.eye(SSM_GROUPS // N_STRIPS, dtype=F32)

    def strips(m):
        g, a, b = m.shape
        m4 = m.reshape(N_STRIPS, g // N_STRIPS, a, b)
        return jnp.einsum('sgab,gk->sgakb', m4, eye).reshape(N_STRIPS, g // N_STRIPS * a, g // N_STRIPS * b)

    mi_re = strips(bb_re)
    mi_im = strips(bb_im)
    mo_re = strips(jnp.swapaxes(c_re, 1, 2))
    mo_im = strips(-jnp.swapaxes(c_im, 1, 2))
    return a_re.reshape(-1), a_im.reshape(-1), mi_re, mi_im, mo_re, mo_im


def _gelu(x):
    c = math.sqrt(2.0 / math.pi)
    return 0.5 * x * (1.0 + jnp.tanh(c * (x + 0.044715 * x * x * x)))


def _gelu_grad(x):
    c = math.sqrt(2.0 / math.pi)
    th = jnp.tanh(c * (x + 0.044715 * x * x * x))
    return 0.5 * (1.0 + th) + 0.5 * x * (1.0 - th * th) * c * (1.0 + 3.0 * 0.044715 * x * x)


def _last_cols_specs(u, w, tr):
    half = w // 2
    first = (u.shape[1] - w) // half
    assert first * half == u.shape[1] - w
    return [pl.BlockSpec((tr, half), lambda i, k=k: (i, first + k)) for k in range(2)]


def _ssm_post_fwd(u, yf, yb, d, wglu, bglu, name):
    t, w = yf.shape
    tr = _row_tile(t)

    def body(ua_ref, ub_ref, yf_ref, yb_ref, d_ref, w_ref, b_ref, s_ref, y0_ref, z_ref):
        uv = jnp.concatenate([ua_ref[...], ub_ref[...]], axis=1)
        y0 = d_ref[...] * uv + yf_ref[...] + yb_ref[...]
        yg = _gelu(y0)
        z = jnp.dot(yg.astype(BF16), w_ref[...], preferred_element_type=F32) + b_ref[...]
        s_ref[...] = yg * _sigmoid(z)
        y0_ref[...] = y0
        z_ref[...] = z

    row = pl.BlockSpec((tr, w), lambda i: (i, 0))
    vec = pl.BlockSpec((1, w), lambda i: (0, 0))
    mat = pl.BlockSpec((w, w), lambda i: (0, 0))
    sh = jax.ShapeDtypeStruct((t, w), F32)
    return pl.pallas_call(body, out_shape=(sh, sh, sh), grid=(t // tr,),
                          in_specs=[*_last_cols_specs(u, w, tr), row, row, vec, mat, vec], out_specs=(row, row, row),
                          compiler_params=_cparams(("parallel",)), name=name)(u, u, yf, yb, d, wglu, bglu)


def _ssm_post_bwd(ds, y0, z, u, d, wglu, name):
    t, w = ds.shape
    tr = _row_tile(t)

    def body(ds_ref, y0_ref, z_ref, ua_ref, ub_ref, d_ref, w_ref, dy0_ref, dw_ref, db_ref, dd_ref):
        @pl.when(pl.program_id(0) == 0)
        def _():
            dw_ref[...] = jnp.zeros_like(dw_ref)
            db_ref[...] = jnp.zeros_like(db_ref)
            dd_ref[...] = jnp.zeros_like(dd_ref)

        y0 = y0_ref[...]
        yg = _gelu(y0)
        sg = _sigmoid(z_ref[...])
        dsv = ds_ref[...]
        dz = dsv * yg * sg * (1.0 - sg)
        dzb = dz.astype(BF16)
        dyg = dsv * sg + lax.dot_general(dzb, w_ref[...], (((1,), (1,)), ((), ())), preferred_element_type=F32)
        dy0 = dyg * _gelu_grad(y0)
        dy0_ref[...] = dy0
        dw_ref[...] += lax.dot_general(yg.astype(BF16), dzb, (((0,), (0,)), ((), ())), preferred_element_type=F32)
        db_ref[...] += jnp.sum(dz, axis=0, keepdims=True)
        uv = jnp.concatenate([ua_ref[...], ub_ref[...]], axis=1)
        dd_ref[...] += jnp.sum(dy0 * uv, axis=0, keepdims=True)

    row = pl.BlockSpec((tr, w), lambda i: (i, 0))
    vec = pl.BlockSpec((1, w), lambda i: (0, 0))
    mat = pl.BlockSpec((w, w), lambda i: (0, 0))
    return pl.pallas_call(
        body, out_shape=(jax.ShapeDtypeStruct((t, w), F32), jax.ShapeDtypeStruct((w, w), F32),
                         jax.ShapeDtypeStruct((1, w), F32), jax.ShapeDtypeStruct((1, w), F32)),
        grid=(t // tr,), in_specs=[row, row, row, *_last_cols_specs(u, w, tr), vec, mat],
        out_specs=(row, mat, vec, vec),
        compiler_params=_cparams(("arbitrary",)), name=name)(ds, y0, z, u, u, d, wglu)


def _du_combine(dy0, d, du_f, du_b, name):
    t, w = dy0.shape
    tr = _row_tile(t)

    def body(dy_ref, d_ref, a_ref, b_ref, o_ref):
        o_ref[...] = d_ref[...] * dy_ref[...] + a_ref[...] + b_ref[...]

    row = pl.BlockSpec((tr, w), lambda i: (i, 0))
    vec = pl.BlockSpec((1, w), lambda i: (0, 0))
    return pl.pallas_call(body, out_shape=jax.ShapeDtypeStruct((t, w), F32), grid=(t // tr,),
                          in_specs=[row, vec, row, row], out_specs=row, compiler_params=_cparams(("parallel",)),
                          name=name)(dy0, d, du_f, du_b)


def _ffn_fwd(x, g, wg, wu, wd, tag):
    xo, h, gate, up = _ffn_fwd_call(x, g, wg, wu, wd, f"{tag}_fwd")
    return xo, (h, gate, up)


def _ffn_bwd(dxo, dxo_b, x, g, wg, wu, wd, saved, tag, between=None):
    h, gate, up = saved
    dx, dx_b, dg, dgate, dup, act = _ffn_bwd_x_call(dxo, dxo_b, x, g, gate, up, wg, wu, wd, f"{tag}_bwd_x")
    after = between(dg) if between is not None else None
    dwg, dwu, dwd = _ffn_bwd_w_call(h, dxo_b, dgate, dup, act, f"{tag}_bwd_w", after=after)
    return dx, dx_b, dg, dwg, dwu, dwd


def _local_step(x, tgt, w, get_weights, put_grads, reduce_wide, put_narrow):
    t = x.shape[0]
    row = lambda a: a.reshape(1, -1)
    grads = {}

    w = dict(w)

    ssm_names = ['ssm_lambda_re', 'ssm_lambda_im', 'ssm_log_dt', 'ssm_b_re', 'ssm_b_im', 'ssm_c_re', 'ssm_c_im']
    tr3 = lambda m: jnp.swapaxes(m, 1, 2)
    fwd_ops, adj_ops, vjps = [], [], []
    for direction in range(2):
        rev = direction == 1
        prep, vjp = jax.vjp(_ssm_prep, *[w[n][direction] for n in ssm_names])
        a_re, a_im = prep[0], prep[1]
        mi_re, mi_im, mo_re, mo_im = (m.astype(BF16) for m in prep[2:])
        fwd_ops.append((mi_re, mi_im, _scan_tables(a_re, a_im, rev), mo_re, mo_im))
        adj_ops.append((tr3(mo_re), tr3(mo_im), _scan_tables(a_re, -a_im, not rev), tr3(mi_re), tr3(mi_im)))
        vjps.append(vjp)
    sink_rows = jnp.repeat(w['attn_sinks'].reshape(KV_HEADS, GQ), QBLOCK, axis=1)[:, None, :]
    bias = _attn_bias()
    prepared = sum(jnp.sum(op[:1, :1].astype(F32)) for ops in fwd_ops + adj_ops for op in ops) + sink_rows[0, 0, 0]

    w.update(get_weights('ffn1', prepared.reshape(1, 1)))
    x1, ffn1_saved = _ffn_fwd(x, w['norm_ffn1'], w['ffn1_w_gate'], w['ffn1_w_up'], w['ffn1_w_down'], "ffn1")
    w.update(get_weights('mix', x1))

    proj, h2 = _norm_mm([x1], [w['norm_mix']], w['w_in'], tb=True, res=None, name="in_proj")
    u = proj

    attn, lse = _attn_fwd_proj(proj, sink_rows, bias, "attn_fwd")

    (y_f, *states_f), (y_b, *states_b) = _scan_pair(u, fwd_ops[0], fwd_ops[1], "s5_fwd")
    ys, states = [y_f, y_b], [states_f, states_b]
    d_row = row(w['ssm_d'])
    s, y0, z = _ssm_post_fwd(u, ys[0], ys[1], d_row, w['ssm_glu_w'], row(w['ssm_glu_b']), "ssm_post")

    x2, mixed = _norm_mm([attn, s], [row(w['attn_out_norm']), row(w['ssm_out_norm'])], w['w_out'], tb=False,
                         res=x1, name="out_proj")

    w.update(get_weights('ffn2', x2))
    x3, ffn2_saved = _ffn_fwd(x2, w['norm_ffn2'], w['ffn2_w_gate'], w['ffn2_w_up'], w['ffn2_w_down'], "ffn2")

    loss, dx3, dx3_b, dgf = _loss_head(x3, row(w['final_norm']), tgt, "loss_head")
    grads['final_norm'] = dgf.reshape(w['final_norm'].shape)

    dx2, dx2_b, dg, dwg, dwu, dwd = _ffn_bwd(dx3, dx3_b, x2, w['norm_ffn2'], w['ffn2_w_gate'], w['ffn2_w_up'],
                                             w['ffn2_w_down'], ffn2_saved, "ffn2")
    grads['norm_ffn2'] = dg
    sent = put_grads('ffn2', dict(ffn2_w_gate=dwg, ffn2_w_up=dwu, ffn2_w_down=dwd))

    (dattn, _, dga), (ds, _, dgs) = _mm_rms_bwd(
        dx2_b, w['w_out'], tb=True, xs=[attn, s], gs=[row(w['attn_out_norm']), row(w['ssm_out_norm'])],
        dres=None, after=sent, name="out_proj_dx")
    dw_out = _mm(mixed, dx2_b, ta=True, out_dtype=BF16, name="out_proj_dw")[0]
    grads.update(attn_out_norm=dga, ssm_out_norm=dgs)

    dy0, dwglu, dbglu, dd = _ssm_post_bwd(ds, y0, z, u, d_row, w['ssm_glu_w'], "ssm_post_bwd")
    grads['ssm_glu_b'] = dbglu
    grads['ssm_d'] = dd.reshape(w['ssm_d'].shape)
    dparams, du_dirs = [], []
    for direction, res in enumerate(_scan_adjoint_pair(dy0, u, states, adj_ops, "s5_adj")):
        du_dir, dmir, dmii, dmor, dmoi, da = res
        du_dirs.append(du_dir)
        da_re = da[:, 0, :].reshape(-1)
        da_im = da[:, 1, :].reshape(-1)
        dparams.append(vjps[direction]((da_re, da_im, dmir, dmii, dmor, dmoi)))
    du = _du_combine(dy0, d_row, du_dirs[0], du_dirs[1], "ssm_du")
    for i, n in enumerate(ssm_names):
        grads[n] = jnp.stack([dparams[0][i], dparams[1][i]])
    wide_sum = reduce_wide(grads)

    dq, dk, dv, dsink = _attn_bwd_proj(proj, sink_rows, bias, attn, lse, dattn, "attn_bwd")
    grads['attn_sinks'] = jnp.sum(dsink.reshape(ATTN_HEADS, QBLOCK), axis=-1).reshape(w['attn_sinks'].shape)
    dproj = jnp.concatenate([dq, dk, dv, du], axis=-1).astype(BF16)

    dw_in = _mm(dproj, h2, ta=True, out_dtype=BF16, after=wide_sum, name="in_proj_dw")[0]
    sent = put_grads('mix', dict(w_in=dw_in, ssm_glu_w=dwglu, w_out=dw_out))
    ((dx1, dx1_b, dgm),) = _mm_rms_bwd(dproj, w['w_in'], tb=False, xs=[x1], gs=[w['norm_mix']], dres=dx2,
                                       after=sent, name="in_proj_dx")
    grads['norm_mix'] = dgm

    def all_small_known(dg):
        grads['norm_ffn1'] = dg
        return put_narrow(grads, loss)

    dx0, _, dg, dwg, dwu, dwd = _ffn_bwd(dx1, dx1_b, x, w['norm_ffn1'], w['ffn1_w_gate'], w['ffn1_w_up'],
                                         w['ffn1_w_down'], ffn1_saved, "ffn1", between=all_small_known)
    put_grads('ffn1', dict(ffn1_w_gate=dwg, ffn1_w_up=dwu, ffn1_w_down=dwd))
    return dx0, grads


HBM_SPEC = pl.BlockSpec(memory_space=pl.ANY)


def _chip_peers(x, y):
    return [(1 - x, y), (x, 1 - y), (1 - x, 1 - y)]


HBM_ONLY = pl.BlockSpec(memory_space=pltpu.HBM)
SEM_SPEC = pl.BlockSpec(memory_space=pltpu.SEMAPHORE)
EFFECT = pltpu.SideEffectType.DATAFLOW_SIDE_EFFECTING


def _place_own(srcs, slot, name):
    na = len(srcs)
    r, c = srcs[0].shape
    tr = r // 2

    def body(slot_ref, *refs):
        for a in range(na):
            refs[na + a][0] = refs[a][...]

    return pl.pallas_call(
        body, out_shape=[jax.ShapeDtypeStruct((N_CHIPS, r, c), s.dtype) for s in srcs],
        grid_spec=pltpu.PrefetchScalarGridSpec(
            num_scalar_prefetch=1, grid=(2,), in_specs=[pl.BlockSpec((tr, c), lambda i, s: (i, 0))] * na,
            out_specs=[pl.BlockSpec((1, tr, c), lambda i, s: (s[0], i, 0))] * na),
        compiler_params=_cparams(("parallel",)), name=name)(slot, *srcs)


def _chip_copies(srcs, lands, send_sems, recv_sems, scatter, landed):
    x, y, c = lax.axis_index("x"), lax.axis_index("y"), lax.axis_index("c")
    me = 2 * x + y
    out = []
    for i in range(len(srcs)):
        for j, (px, py) in enumerate(_chip_peers(x, y)):
            p = 2 * px + py
            slot = p if landed else me
            if scatter:
                src, dst = srcs[i].at[p], lands[i].at[slot]
            else:
                rows = _core_half(srcs[i].shape[0], c)
                src, dst = srcs[i].at[rows], lands[i].at[slot, rows]
            out.append(pltpu.make_async_remote_copy(src, dst, send_sems.at[3 * i + j], recv_sems.at[3 * i + j],
                                                    device_id=(px, py, c), device_id_type=MESH))
    return out


def _core_half(nrows, c):
    half = nrows // 2
    return pl.ds(pl.multiple_of(c * half, 16), half)


def _sibling_forward(lands, name):
    n = len(lands)

    def body(*refs):
        bufs = refs[n:2 * n]
        send_sems, recv_sems = refs[2 * n:]
        x, y, c = lax.axis_index("x"), lax.axis_index("y"), lax.axis_index("c")
        mine = [_core_half(b.shape[1], c) for b in bufs]
        theirs = [_core_half(b.shape[1], 1 - c) for b in bufs]
        chips = [2 * px + py for px, py in _chip_peers(x, y)]
        cps = [pltpu.make_async_remote_copy(bufs[i].at[p, mine[i]], bufs[i].at[p, mine[i]], send_sems.at[3 * i + j],
                                            recv_sems.at[3 * i + j], device_id=(x, y, 1 - c), device_id_type=MESH)
               for i in range(n) for j, p in enumerate(chips)]
        for cp in cps:
            cp.start()
        for i in range(n):
            for j, p in enumerate(chips):
                pltpu.make_async_remote_copy(bufs[i].at[p, mine[i]], bufs[i].at[p, theirs[i]], send_sems.at[3 * i + j],
                                             recv_sems.at[3 * i + j], device_id=(x, y, 1 - c),
                                             device_id_type=MESH).wait()

    return pl.pallas_call(
        body, out_shape=[jax.ShapeDtypeStruct(a.shape, a.dtype) for a in lands],
        in_specs=[HBM_SPEC] * n, out_specs=[HBM_SPEC] * n, input_output_aliases={k: k for k in range(n)},
        scratch_shapes=[pltpu.SemaphoreType.DMA((3 * n,)), pltpu.SemaphoreType.DMA((3 * n,))],
        name=name)(*lands)


def _exchange_start(groups, scatter, name, after=None):
    sizes = [len(srcs) for srcs, _ in groups]
    flat_src = [a for srcs, _ in groups for a in srcs]
    flat_land = [a for _, lands in groups for a in lands]
    n = len(flat_src)
    ng = len(groups)

    def body(*refs):
        src_refs, land_refs = refs[:n], refs[n:2 * n]
        n_in = 2 * n + (after is not None)
        sems = refs[n_in:n_in + 2 * ng]
        token_ref = refs[-1]
        off = 0
        for gi, sz in enumerate(sizes):
            for cp in _chip_copies(src_refs[off:off + sz], land_refs[off:off + sz], sems[2 * gi], sems[2 * gi + 1],
                                   scatter, landed=False):
                cp.start()
            off += sz
        token_ref[...] = jnp.zeros_like(token_ref)

    sem_shapes = []
    for sz in sizes:
        sem_shapes += [pltpu.SemaphoreType.DMA((3 * sz,)), pltpu.SemaphoreType.DMA((3 * sz,))]
    hbm = lambda a: pltpu.HBM(a.shape, a.dtype)
    res = pl.pallas_call(
        body, name=name,
        out_shape=(tuple(sem_shapes) + tuple(hbm(a) for a in flat_src) + tuple(hbm(a) for a in flat_land)
                   + (jax.ShapeDtypeStruct((SUBLANES, LANES), F32),)),
        in_specs=[HBM_ONLY] * (2 * n) + [HBM_SPEC] * (after is not None),
        out_specs=tuple([SEM_SPEC] * (2 * ng) + [HBM_ONLY] * (2 * n) + [pl.BlockSpec(memory_space=pltpu.VMEM)]),
        input_output_aliases={k: 2 * ng + k for k in range(2 * n)},
        compiler_params=pltpu.CompilerParams(has_side_effects=EFFECT),
    )(*[pltpu.with_memory_space_constraint(a, pltpu.HBM) for a in flat_src + flat_land],
      *([after] if after is not None else []))
    sems, thru_src, thru_land = res[:2 * ng], res[2 * ng:2 * ng + n], res[2 * ng + n:2 * ng + 2 * n]
    out, off = [], 0
    for gi, sz in enumerate(sizes):
        out.append((sems[2 * gi], sems[2 * gi + 1], list(thru_src[off:off + sz]), list(thru_land[off:off + sz])))
        off += sz
    return out, res[-1]


def _exchange_wait(started, after, scatter, name):
    send_sems, recv_sems, srcs, lands = started
    n = len(srcs)

    def body(*refs):
        src_refs, land_refs = refs[:n], refs[n:2 * n]
        send_ref, recv_ref = refs[2 * n], refs[2 * n + 1]
        for cp in _chip_copies(src_refs, land_refs, send_ref, recv_ref, scatter, landed=True):
            cp.wait_send()
            cp.wait_recv()

    hbm = lambda a: pltpu.HBM(a.shape, a.dtype)
    res = pl.pallas_call(
        body, name=name, out_shape=tuple(hbm(a) for a in srcs) + tuple(hbm(a) for a in lands),
        in_specs=[HBM_ONLY] * (2 * n) + [SEM_SPEC, SEM_SPEC, HBM_SPEC], out_specs=tuple([HBM_ONLY] * (2 * n)),
        input_output_aliases={k: k for k in range(2 * n)},
        compiler_params=pltpu.CompilerParams(has_side_effects=EFFECT),
    )(*srcs, *lands, send_sems, recv_sems, after)
    return list(res[:n]), list(res[n:])


def _half_swap(parts, name):
    n = len(parts)

    def body(*refs):
        ins, outs = refs[:n], refs[n:2 * n]
        send_sems, recv_sems = refs[2 * n:]
        x, y, c = lax.axis_index("x"), lax.axis_index("y"), lax.axis_index("c")
        cps = [pltpu.make_async_remote_copy(ins[i].at[k, _core_half(ins[i].shape[1], 1 - c)], outs[i].at[k],
                                            send_sems.at[N_CHIPS * i + k], recv_sems.at[N_CHIPS * i + k],
                                            device_id=(x, y, 1 - c), device_id_type=MESH)
               for i in range(n) for k in range(N_CHIPS)]
        for cp in cps:
            cp.start()
        for cp in cps:
            cp.wait()

    return pl.pallas_call(
        body, out_shape=[jax.ShapeDtypeStruct((N_CHIPS, p.shape[1] // 2, p.shape[2]), p.dtype) for p in parts],
        in_specs=[HBM_SPEC] * n, out_specs=[HBM_SPEC] * n,
        scratch_shapes=[pltpu.SemaphoreType.DMA((N_CHIPS * n,)), pltpu.SemaphoreType.DMA((N_CHIPS * n,))],
        name=name)(*parts)


def _half_add(parts, sib, slots, name):
    na = len(parts)
    _, r, c = parts[0].shape
    hr = r // 2
    tr = _row_tile(hr, 512)
    nt = hr // tr

    def body(slot_ref, *refs):
        for a in range(na):
            refs[2 * na + a][...] = (refs[2 * a][...].astype(F32) + refs[2 * a + 1][...].astype(F32)).astype(BF16)

    mine = pl.BlockSpec((1, tr, c), lambda k, i, s: (k, i + s[4] * nt, 0))
    half = pl.BlockSpec((1, tr, c), lambda k, i, s: (k, i, 0))
    args = [a for p, sb in zip(parts, sib) for a in (p, sb)]
    return pl.pallas_call(
        body, out_shape=[jax.ShapeDtypeStruct((N_CHIPS, hr, c), BF16)] * na,
        grid_spec=pltpu.PrefetchScalarGridSpec(
            num_scalar_prefetch=1, grid=(N_CHIPS, nt), in_specs=[mine, half] * na, out_specs=[half] * na),
        compiler_params=_cparams(("parallel", "parallel")), name=name)(slots, *args)


def _half_forward(arrs, name):
    n = len(arrs)

    def body(*refs):
        bufs = refs[n:2 * n]
        send_sems, recv_sems = refs[2 * n:]
        x, y, c = lax.axis_index("x"), lax.axis_index("y"), lax.axis_index("c")
        cps = [pltpu.make_async_remote_copy(b.at[_core_half(b.shape[0], c)], b.at[_core_half(b.shape[0], c)],
                                            send_sems.at[i], recv_sems.at[i], device_id=(x, y, 1 - c),
                                            device_id_type=MESH) for i, b in enumerate(bufs)]
        for cp in cps:
            cp.start()
        for i, b in enumerate(bufs):
            pltpu.make_async_remote_copy(b.at[_core_half(b.shape[0], c)], b.at[_core_half(b.shape[0], 1 - c)],
                                         send_sems.at[i], recv_sems.at[i], device_id=(x, y, 1 - c),
                                         device_id_type=MESH).wait()

    return pl.pallas_call(
        body, out_shape=[jax.ShapeDtypeStruct(a.shape, a.dtype) for a in arrs],
        in_specs=[HBM_SPEC] * n, out_specs=[HBM_SPEC] * n, input_output_aliases={k: k for k in range(n)},
        scratch_shapes=[pltpu.SemaphoreType.DMA((n,)), pltpu.SemaphoreType.DMA((n,))],
        name=name)(*arrs)


def _sum_parts(parts, recv, slots, name):
    na = len(parts)
    _, r, c = parts[0].shape
    tr = _row_tile(r, 192)

    def body(slot_ref, *refs):
        for a in range(na):
            own_ref, r0_ref, r1_ref, r2_ref = refs[4 * a:4 * a + 4]
            refs[4 * na + a][...] = ((own_ref[0].astype(F32) + r0_ref[0].astype(F32))
                                     + (r1_ref[0].astype(F32) + r2_ref[0].astype(F32)))

    blk = lambda k: pl.BlockSpec((1, tr, c), lambda i, s, k=k: (s[k], i, 0))
    out_blk = pl.BlockSpec((tr, c), lambda i, s: (i + s[4] * (r // tr), 0))
    args = [a for p, rv in zip(parts, recv) for a in (p, rv, rv, rv)]
    return pl.pallas_call(
        body, out_shape=[jax.ShapeDtypeStruct((2 * r, c), F32)] * na,
        grid_spec=pltpu.PrefetchScalarGridSpec(
            num_scalar_prefetch=1, grid=(r // tr,), in_specs=[blk(0), blk(1), blk(2), blk(3)] * na,
            out_specs=[out_blk] * na),
        compiler_params=_cparams(("parallel",)), name=name)(slots, *args)


ALL_PEERS = [(fx, fy, fc) for fx in (0, 1) for fy in (0, 1) for fc in (0, 1)][1:]


def _all8_copies(srcs, lands, send_sems, recv_sems, landed):
    x, y, c = lax.axis_index("x"), lax.axis_index("y"), lax.axis_index("c")
    lin = 4 * x + 2 * y + c
    out = []
    for i, (src, land) in enumerate(zip(srcs, lands)):
        for j, (fx, fy, fc) in enumerate(ALL_PEERS):
            px, py, pc = x ^ fx, y ^ fy, c ^ fc
            slot = (4 * px + 2 * py + pc) if landed else lin
            out.append(pltpu.make_async_remote_copy(src, land.at[slot], send_sems.at[7 * i + j], recv_sems.at[7 * i + j],
                                                    device_id=(px, py, pc), device_id_type=MESH))
    return out


def _all8_start(srcs, name):
    n = len(srcs)
    lands = [lax.empty((N_DEV,) + s.shape, s.dtype) for s in srcs]

    def body(*refs):
        for cp in _all8_copies(refs[:n], refs[n:2 * n], refs[2 * n], refs[2 * n + 1], landed=False):
            cp.start()
        refs[-1][...] = jnp.zeros_like(refs[-1])

    hbm = lambda a: pltpu.HBM(a.shape, a.dtype)
    res = pl.pallas_call(
        body, name=name,
        out_shape=(pltpu.SemaphoreType.DMA((7 * n,)), pltpu.SemaphoreType.DMA((7 * n,)), *[hbm(a) for a in srcs],
                   *[hbm(a) for a in lands], jax.ShapeDtypeStruct((SUBLANES, LANES), F32)),
        in_specs=[HBM_ONLY] * (2 * n),
        out_specs=(SEM_SPEC, SEM_SPEC, *[HBM_ONLY] * (2 * n), pl.BlockSpec(memory_space=pltpu.VMEM)),
        input_output_aliases={k: 2 + k for k in range(2 * n)},
        compiler_params=pltpu.CompilerParams(has_side_effects=EFFECT),
    )(*[pltpu.with_memory_space_constraint(a, pltpu.HBM) for a in list(srcs) + lands])
    return (res[0], res[1], list(res[2:2 + n]), list(res[2 + n:2 + 2 * n])), res[-1]


def _all8_wait(started, after, name):
    send_sems, recv_sems, srcs, lands = started
    n = len(srcs)

    def body(*refs):
        for cp in _all8_copies(refs[:n], refs[n:2 * n], refs[2 * n], refs[2 * n + 1], landed=True):
            cp.wait_send()
            cp.wait_recv()

    hbm = lambda a: pltpu.HBM(a.shape, a.dtype)
    res = pl.pallas_call(
        body, name=name, out_shape=tuple(hbm(a) for a in srcs) + tuple(hbm(a) for a in lands),
        in_specs=[HBM_ONLY] * (2 * n) + [SEM_SPEC, SEM_SPEC, HBM_SPEC], out_specs=tuple([HBM_ONLY] * (2 * n)),
        input_output_aliases={k: k for k in range(2 * n)},
        compiler_params=pltpu.CompilerParams(has_side_effects=EFFECT),
    )(*srcs, *lands, send_sems, recv_sems, after)
    return list(res[:n]), list(res[n:])


def _sum8(own, land, lin, name):
    r, c = own.shape
    tr = _row_tile(r)

    def body(lin_ref, own_ref, land_ref, o_ref):
        me = lin_ref[0]
        acc = None
        for k in range(N_DEV):
            term = jnp.where(me == k, own_ref[...], land_ref[k])
            acc = term if acc is None else acc + term
        o_ref[...] = acc

    return pl.pallas_call(
        body, out_shape=jax.ShapeDtypeStruct((r, c), F32),
        grid_spec=pltpu.PrefetchScalarGridSpec(
            num_scalar_prefetch=1, grid=(r // tr,),
            in_specs=[pl.BlockSpec((tr, c), lambda i, s: (i, 0)), pl.BlockSpec((N_DEV, tr, c), lambda i, s: (0, i, 0))],
            out_specs=pl.BlockSpec((tr, c), lambda i, s: (i, 0))),
        compiler_params=_cparams(("parallel",)), name=name)(lin, own, land)


def _adamw_math(w, m, v, g):
    nm = ADAM_B1 * m + (1.0 - ADAM_B1) * g
    nv = ADAM_B2 * v + (1.0 - ADAM_B2) * (g * g)
    m_hat = nm * (1.0 / (1.0 - ADAM_B1 ** ADAM_STEP))
    v_hat = nv * (1.0 / (1.0 - ADAM_B2 ** ADAM_STEP))
    return -ADAM_LR * (m_hat / (jnp.sqrt(v_hat) + ADAM_EPS) + ADAM_WD * w), nm, nv


def _adamw(ws, ms, vs, gs, name):
    na = len(ws)
    r, c = ws[0].shape
    tr = _row_tile(r)

    def body(*refs):
        for a in range(na):
            w_ref, m_ref, v_ref, g_ref = refs[4 * a:4 * a + 4]
            go_ref, d_ref, nm_ref, nv_ref = refs[4 * na + 4 * a:4 * na + 4 * a + 4]
            g = g_ref[...]
            go_ref[...] = g
            d_ref[...], nm_ref[...], nv_ref[...] = _adamw_math(w_ref[...], m_ref[...], v_ref[...], g)

    blk = pl.BlockSpec((tr, c), lambda i: (i, 0))
    sh = jax.ShapeDtypeStruct((r, c), F32)
    args = [a for group in zip(ws, ms, vs, gs) for a in group]
    res = pl.pallas_call(body, out_shape=[sh] * (4 * na), grid=(r // tr,), in_specs=[blk] * (4 * na),
                         out_specs=[blk] * (4 * na), compiler_params=_cparams(("parallel",)), name=name)(*args)
    return [tuple(res[4 * a:4 * a + 4]) for a in range(na)]


def _adamw_small(ws, ms, vs, alls, split, name, owns=None, lin=None):
    n = len(ws)
    lead = split if split is not None else ()
    nl = len(lead)
    nslots = alls[0].shape[0]
    has_own = owns is not None
    nin = 5 if has_own else 4

    def blocks(shape):
        if split is None:
            return tuple(shape), (lambda *g: (0,) * len(shape))
        blk = (shape[0], shape[1] // lead[0], shape[2] // lead[1]) + tuple(shape[3:])
        return blk, (lambda *g: (0, g[0], g[1]) + (0,) * (len(shape) - 3))

    def body(*refs):
        if has_own:
            lin_ref, refs = refs[0], refs[1:]
        w_refs, m_refs, v_refs, a_refs = (refs[k * n:(k + 1) * n] for k in range(4))
        own_refs = refs[4 * n:5 * n] if has_own else None
        g_refs, d_refs, nm_refs, nv_refs = (refs[(nin + k) * n:(nin + 1 + k) * n] for k in range(4))
        k = pl.program_id(nl)
        for i in range(n):
            term = a_refs[i][0]
            if has_own:
                term = jnp.where(k == lin_ref[0], own_refs[i][...], term)

            @pl.when(k == 0)
            def _(i=i, term=term):
                g_refs[i][...] = term

            @pl.when(k > 0)
            def _(i=i, term=term):
                g_refs[i][...] += term

            @pl.when(k == nslots - 1)
            def _(i=i):
                d_refs[i][...], nm_refs[i][...], nv_refs[i][...] = _adamw_math(
                    w_refs[i][...], m_refs[i][...], v_refs[i][...], g_refs[i][...])

    specs, aspecs, shapes = [], [], []
    for wa in ws:
        blk, imap = blocks(wa.shape)
        specs.append(pl.BlockSpec(blk, imap))
        aspecs.append(pl.BlockSpec((1,) + blk, (lambda *g, imap=imap: (g[nl],) + imap(*g))))
        shapes.append(jax.ShapeDtypeStruct(wa.shape, F32))
    grid = tuple(lead) + (nslots,)
    sem = _cparams(("parallel",) * nl + ("arbitrary",))
    if has_own:
        res = pl.pallas_call(
            body, out_shape=shapes * 4,
            grid_spec=pltpu.PrefetchScalarGridSpec(num_scalar_prefetch=1, grid=grid,
                                                   in_specs=specs * 3 + aspecs + specs, out_specs=specs * 4),
            compiler_params=sem, name=name)(lin, *ws, *ms, *vs, *alls, *owns)
    else:
        res = pl.pallas_call(body, out_shape=shapes * 4, grid=grid, in_specs=specs * 3 + aspecs,
                             out_specs=specs * 4, compiler_params=sem, name=name)(*ws, *ms, *vs, *alls)
    return res[:n], res[n:2 * n], res[2 * n:3 * n], res[3 * n:]


def kernel(x, norm_ffn1, ffn1_w_gate, ffn1_w_up, ffn1_w_down, norm_mix, w_in, attn_sinks, ssm_lambda_re, ssm_lambda_im, ssm_log_dt, ssm_b_re, ssm_b_im, ssm_c_re, ssm_c_im, ssm_d, ssm_glu_w, ssm_glu_b, attn_out_norm, ssm_out_norm, w_out, norm_ffn2, ffn2_w_gate, ffn2_w_up, ffn2_w_down, final_norm, loss_target, m_norm_ffn1, m_ffn1_w_gate, m_ffn1_w_up, m_ffn1_w_down, m_norm_mix, m_w_in, m_attn_sinks, m_ssm_lambda_re, m_ssm_lambda_im, m_ssm_log_dt, m_ssm_b_re, m_ssm_b_im, m_ssm_c_re, m_ssm_c_im, m_ssm_d, m_ssm_glu_w, m_ssm_glu_b, m_attn_out_norm, m_ssm_out_norm, m_w_out, m_norm_ffn2, m_ffn2_w_gate, m_ffn2_w_up, m_ffn2_w_down, m_final_norm, v_norm_ffn1, v_ffn1_w_gate, v_ffn1_w_up, v_ffn1_w_down, v_norm_mix, v_w_in, v_attn_sinks, v_ssm_lambda_re, v_ssm_lambda_im, v_ssm_log_dt, v_ssm_b_re, v_ssm_b_im, v_ssm_c_re, v_ssm_c_im, v_ssm_d, v_ssm_glu_w, v_ssm_glu_b, v_attn_out_norm, v_ssm_out_norm, v_w_out, v_norm_ffn2, v_ffn2_w_gate, v_ffn2_w_up, v_ffn2_w_down, v_final_norm):
    given = dict(locals())
    wts = {n: given[n] for n in WEIGHTS}

    order = [g for g in GROUPS]
    cx, cy = lax.axis_index("x"), lax.axis_index("y")
    slots = jnp.stack([2 * cx + cy, 2 * (1 - cx) + cy, 2 * cx + 1 - cy, 2 * (1 - cx) + 1 - cy,
                       lax.axis_index("c")]).astype(jnp.int32)
    def view(a, n):
        if n in TRANSPOSED:
            return jnp.swapaxes(a[0], 0, 1)
        if n in BIG:
            return a[0]
        if n in ('ssm_b_re', 'ssm_b_im'):
            return jnp.swapaxes(a, -1, -2)
        return a.reshape(1, -1) if a.ndim == 1 else a

    def unview(a, n):
        if n in TRANSPOSED:
            return jnp.swapaxes(a, 0, 1)[None]
        if n in ('ssm_b_re', 'ssm_b_im'):
            return jnp.swapaxes(a, -1, -2)
        return a.reshape(wts[n].shape)

    started, gather_token = {}, None
    for g in order:
        shards = [view(wts[n], n).astype(BF16) for n in GROUPS[g]]
        if len({s.shape for s in shards}) == 1:
            placed = _place_own(shards, slots, f"weights_place_{g}")
        else:
            placed = [_place_own([s], slots, f"weights_place_{n}")[0] for n, s in zip(GROUPS[g], shards)]
        st, gather_token = _exchange_start([(shards, placed)], False, f"weights_start_{g}", after=gather_token)
        started[g] = st[0]

    def get_weights(group, after):
        if group == order[0]:
            after = after + gather_token[:1, :1]
        _, lands = _exchange_wait(started[group], after, False, f"weights_wait_{group}")
        lands = _sibling_forward(lands, f"weights_forward_{group}")
        out = dict(zip(GROUPS[group], lands))
        for n in ('w_in', 'ssm_glu_w', 'w_out'):
            if n in out:
                out[n] = out[n].reshape(-1, out[n].shape[-1])
        return out

    sent, tokens = {}, {}

    def put_grads(group, gd):
        parts = []
        for n in GROUPS[group]:
            g = gd[n]
            if g.ndim == 2:
                g = g.reshape(N_CHIPS, g.shape[0] // N_CHIPS, g.shape[1])
            parts.append(g.astype(BF16))
        sib = _half_swap(parts, f"grads_half_swap_{group}")
        same = len({p.shape for p in parts}) == 1
        batches = [list(range(len(parts)))] if same else [[i] for i in range(len(parts))]
        halves = [None] * len(parts)
        for b in batches:
            res = _half_add([parts[i] for i in b], [sib[i] for i in b], slots, f"grads_half_add_{GROUPS[group][b[0]]}")
            for i, h in zip(b, res):
                halves[i] = h
        parts = halves
        lands = [lax.empty(p.shape, p.dtype) for p in parts]
        started_g, tokens[group] = _exchange_start([(parts, lands)], True, f"grads_start_{group}")
        sent[group] = started_g[0]
        return tokens[group]

    w = {n: (wts[n][0] if wts[n].ndim > 1 else wts[n]) for n in SMALL}
    w['norm_ffn1'], w['norm_mix'], w['norm_ffn2'] = wts['norm_ffn1'], wts['norm_mix'], wts['norm_ffn2']
    w['ssm_b_re'], w['ssm_b_im'] = view(wts['ssm_b_re'], 'ssm_b_re')[0], view(wts['ssm_b_im'], 'ssm_b_im')[0]
    w['ssm_log_dt'] = w['ssm_log_dt'] + gather_token[0, 0]
    wide =['ssm_b_re', 'ssm_b_im', 'ssm_c_re', 'ssm_c_im']

    nat = {n: view(wts[n], n).shape for n in SMALL}
    narrow = [n for n in SMALL if n not in wide]
    wide_started, narrow_started = [], []

    def reduce_wide(gd):
        packed = jnp.concatenate([gd[n].reshape(-1, LANES) for n in wide])
        started_w, token = _all8_start([packed], "small_grads_start")
        wide_started.append(started_w)
        return token

    def put_narrow(gd, loss_row):
        started_n, token = _all8_start([gd[n].reshape(nat[n]) for n in narrow] + [loss_row], "narrow_grads_start")
        narrow_started.append(started_n)
        return token

    dx, grads = _local_step(x[0], loss_target[0], w, get_weights, put_grads, reduce_wide, put_narrow)

    out_g, out_d, out_m, out_v = {}, {}, {}, {}

    def finish(group, after):
        names = GROUPS[group]
        parts, recv = _exchange_wait(sent[group], after, True, f"grads_wait_{group}")
        same = len({p.shape for p in parts}) == 1
        batches = [list(range(len(names)))] if same else [[i] for i in range(len(names))]
        sums = [None] * len(names)
        for b in batches:
            res = _sum_parts([parts[i] for i in b], [recv[i] for i in b], slots, f"grad_sum_{names[b[0]]}")
            for i, sm in zip(b, res):
                sums[i] = sm
        full = _half_forward(sums, f"grad_half_forward_{group}")
        for b in batches:
            res = _adamw([view(wts[names[i]], names[i]) for i in b], [view(given['m_' + names[i]], names[i]) for i in b],
                         [view(given['v_' + names[i]], names[i]) for i in b], [full[i] for i in b],
                         f"adamw_{names[b[0]]}")
            for i, (g, d, nm, nv) in zip(b, res):
                n = names[i]
                out_g[n], out_d[n], out_m[n], out_v[n] = (unview(a, n) for a in (g, d, nm, nv))
        return nv

    done = finish('ffn2', tokens['ffn1'])
    done = finish('mix', done)

    lin = (4 * cx + 2 * cy + lax.axis_index("c")).astype(jnp.int32).reshape(1)
    (own_w,), (land_w,) = _all8_wait(wide_started[0], done, "small_grads_wait")
    wide_sum = _sum8(own_w, land_w, lin, "small_grads_sum")
    rows = wide_sum.shape[0] // len(wide)
    wide_g = [wide_sum[i * rows:(i + 1) * rows].reshape((1,) + nat[n]) for i, n in enumerate(wide)]
    owns_n, lands_n = _all8_wait(narrow_started[0], done, "narrow_grads_wait")
    loss_shares = jnp.where(jnp.arange(N_DEV) == lin[0], owns_n[-1][0, 0], lands_n[-1][:, 0, 0])
    loss = jnp.sum(loss_shares)
    for group, gs, owns, split, tag in ((narrow, lands_n[:-1], owns_n[:-1], None, "adamw_small"),
                                        (wide, wide_g, None, (2, 4), "adamw_ssm_bc")):
        res = _adamw_small([view(wts[n], n) for n in group], [view(given['m_' + n], n) for n in group],
                           [view(given['v_' + n], n) for n in group], gs, split, tag, owns=owns,
                           lin=lin if owns is not None else None)
        for dst, vals in zip((out_g, out_d, out_m, out_v), res):
            for n, a in zip(group, vals):
                dst[n] = unview(a, n)

    finish('ffn1', out_v['norm_ffn1'][:, :1] + out_v['ssm_c_re'].reshape(1, -1)[:, :1] + done[:1, :1] + loss)

    return (loss, dx[None], *[out_g[n] for n in WEIGHTS], *[out_d[n] for n in WEIGHTS],
            *[out_m[n] for n in WEIGHTS], *[out_v[n] for n in WEIGHTS])
```

```python
import functools
import math

import numpy as np
import jax
import jax.numpy as jnp
from jax import lax
from jax.experimental import pallas as pl
from jax.experimental.pallas import tpu as pltpu

F32 = jnp.float32
BF16 = jnp.bfloat16
MESH = pl.DeviceIdType.MESH

EPS = 1e-6
NEG_INF = -1e30
LAMBDA_RE_MAX = -1e-4
ATTN_HEADS = 8
KV_HEADS = 2
GQ = ATTN_HEADS // KV_HEADS
HEAD_DIM = 64
ATTN_WIDTH = 512
KV_WIDTH = 128
WINDOW = 128
QBLOCK = 128
SSM_WIDTH = 512
SSM_GROUPS = 32
SSM_CH = 16
SSM_STATE = 64
N_STRIPS = 4
STRIP_IN = SSM_WIDTH // N_STRIPS
STRIP_ST = SSM_GROUPS * SSM_STATE // N_STRIPS
SUBLANES = 8
LANES = 128
N_CHIPS = 4
N_DEV = 8

ADAM_LR = 0.001
ADAM_B1 = 0.9
ADAM_B2 = 0.999
ADAM_EPS = 1e-08
ADAM_WD = 0.01
ADAM_STEP = 10

VMEM_LIMIT = 48 * 1024 * 1024

WEIGHTS = ['norm_ffn1', 'ffn1_w_gate', 'ffn1_w_up', 'ffn1_w_down', 'norm_mix', 'w_in', 'attn_sinks',
           'ssm_lambda_re', 'ssm_lambda_im', 'ssm_log_dt', 'ssm_b_re', 'ssm_b_im', 'ssm_c_re', 'ssm_c_im',
           'ssm_d', 'ssm_glu_w', 'ssm_glu_b', 'attn_out_norm', 'ssm_out_norm', 'w_out', 'norm_ffn2',
           'ffn2_w_gate', 'ffn2_w_up', 'ffn2_w_down', 'final_norm']
BIG = ['ffn1_w_gate', 'ffn1_w_up', 'ffn1_w_down', 'w_in', 'ssm_glu_w', 'w_out',
       'ffn2_w_gate', 'ffn2_w_up', 'ffn2_w_down']
SMALL = [n for n in WEIGHTS if n not in BIG]
TRANSPOSED = ['ffn1_w_gate', 'ffn1_w_up', 'w_in', 'ffn2_w_gate', 'ffn2_w_up']
GROUPS = {'ffn1': ['ffn1_w_gate', 'ffn1_w_up', 'ffn1_w_down'],
          'mix': ['w_in', 'ssm_glu_w', 'w_out'],
          'ffn2': ['ffn2_w_gate', 'ffn2_w_up', 'ffn2_w_down']}


def _cparams(sem=None):
    return pltpu.CompilerParams(dimension_semantics=sem, vmem_limit_bytes=VMEM_LIMIT)


def _tile(n, pref):
    if n <= pref:
        return n
    for t in (pref, pref // 2, pref // 4):
        if t % LANES == 0 and n % t == 0:
            return t
    return n


def _sigmoid(x):
    return 1.0 / (1.0 + jnp.exp(-x))


def _sigmoid_tanh(x):
    return 0.5 * jnp.tanh(0.5 * x) + 0.5


def _mm(a, b, *, ta=False, tb=False, reduce_s=False, res=None, scale=1.0, out_dtype=F32, after=None, name):
    a3 = a if a.ndim == 3 else a[None]
    b3 = b if b.ndim == 3 else b[None]
    sa, sb = a3.shape[0], b3.shape[0]
    ns = max(sa, sb)
    (kk, m) = a3.shape[1:] if ta else a3.shape[1:][::-1]
    (n, kb) = b3.shape[1:] if tb else b3.shape[1:][::-1]
    assert kk == kb, (a3.shape, b3.shape)
    tm, tn, tk = _tile(m, 1024), _tile(n, 1024), _tile(kk, 2048)
    nm, nn, nk = m // tm, n // tn, kk // tk
    has_res = res is not None
    single = nk == 1 and not (reduce_s and ns > 1)

    if reduce_s:
        grid = (nm, nn, ns, nk)
        ids = lambda i, j, s, k: (s, i, j, k)
        sem = ("parallel", "parallel", "arbitrary", "arbitrary")
    else:
        grid = (ns, nm, nn, nk)
        ids = lambda s, i, j, k: (s, i, j, k)
        sem = ("parallel", "parallel", "parallel", "arbitrary")

    def a_map(*g):
        s, i, j, k = ids(*g)
        s = s if sa > 1 else 0
        return (s, k, i) if ta else (s, i, k)

    def b_map(*g):
        s, i, j, k = ids(*g)
        s = s if sb > 1 else 0
        return (s, j, k) if tb else (s, k, j)

    def o_map(*g):
        s, i, j, k = ids(*g)
        return (i, j) if reduce_s else (s, i, j)

    a_blk = (1, tk, tm) if ta else (1, tm, tk)
    b_blk = (1, tn, tk) if tb else (1, tk, tn)
    dims = (((0 if ta else 1,), (1 if tb else 0,)), ((), ()))

    def body(*refs):
        a_ref, b_ref = refs[0], refs[1]
        r_ref = refs[2] if has_res else None
        o_ref = refs[2 + has_res + (after is not None)]
        acc_ref = None if single else refs[-1]
        s, _, _, k = ids(*[pl.program_id(d) for d in range(4)])
        prod = lax.dot_general(a_ref[0].astype(BF16), b_ref[0].astype(BF16), dims, preferred_element_type=F32)

        def finish(out):
            if scale != 1.0:
                out = out * scale
            if has_res:
                out = r_ref[...].reshape(out.shape) + out
            o_ref[...] = out.astype(out_dtype).reshape(o_ref.shape)

        if single:
            finish(prod)
            return
        if reduce_s:
            first = jnp.logical_and(s == 0, k == 0)
            last = jnp.logical_and(s == ns - 1, k == nk - 1)
        else:
            first, last = k == 0, k == nk - 1

        acc_ref[...] = prod + jnp.where(first, 0.0, acc_ref[...])

        @pl.when(last)
        def _():
            finish(acc_ref[...])

    in_specs = [pl.BlockSpec(a_blk, a_map), pl.BlockSpec(b_blk, b_map)]
    args = [a3, b3]
    if reduce_s:
        out_shape = jax.ShapeDtypeStruct((m, n), out_dtype)
        o_spec = pl.BlockSpec((tm, tn), o_map)
    else:
        out_shape = jax.ShapeDtypeStruct((ns, m, n), out_dtype)
        o_spec = pl.BlockSpec((1, tm, tn), o_map)
    if has_res:
        assert res.shape == out_shape.shape
        in_specs.append(o_spec)
        args.append(res)
    if after is not None:
        in_specs.append(HBM_SPEC)
        args.append(after)
    return pl.pallas_call(body, out_shape=out_shape, grid=grid, in_specs=in_specs, out_specs=o_spec,
                          scratch_shapes=[] if single else [pltpu.VMEM((tm, tn), F32)],
                          compiler_params=_cparams(sem), name=name)(*args)


def _row_tile(t, cap=256):
    for step in (16, SUBLANES):
        for tr in range(min(cap, t) // step * step, 0, -step):
            if t % tr == 0:
                return tr
    return t


def _norm_mm(xs, gs, w, *, tb, res, name):
    nx = len(xs)
    t = xs[0].shape[0]
    widths = [x.shape[1] for x in xs]
    k = sum(widths)
    n = w.shape[0] if tb else w.shape[1]
    tm = _tile(t, 512)
    tn = n if n <= 1536 else _tile(n, 512)
    has_res = res is not None

    def body(*refs):
        x_refs, g_refs, w_ref = refs[:nx], refs[nx:2 * nx], refs[2 * nx]
        r_ref = refs[2 * nx + 1] if has_res else None
        o_ref, h_ref, h_sc = refs[2 * nx + 1 + has_res:]

        @pl.when(pl.program_id(1) == 0)
        def _():
            off = 0
            for x_ref, g_ref, wd in zip(x_refs, g_refs, widths):
                xv = x_ref[...]
                r = lax.rsqrt(jnp.mean(xv * xv, axis=-1, keepdims=True) + EPS)
                h_sc[:, off:off + wd] = (xv * r * g_ref[...]).astype(BF16)
                off += wd
            h_ref[...] = h_sc[...]

        prod = lax.dot_general(h_sc[...], w_ref[...], NT_DIMS if tb else (((1,), (0,)), ((), ())),
                               preferred_element_type=F32)
        o_ref[...] = r_ref[...] + prod if has_res else prod

    in_specs = [pl.BlockSpec((tm, wd), lambda i, j: (i, 0)) for wd in widths]
    in_specs += [pl.BlockSpec((1, wd), lambda i, j: (0, 0)) for wd in widths]
    in_specs.append(pl.BlockSpec((tn, k), lambda i, j: (j, 0)) if tb else pl.BlockSpec((k, tn), lambda i, j: (0, j)))
    tile = pl.BlockSpec((tm, tn), lambda i, j: (i, j))
    if has_res:
        in_specs.append(tile)
    return pl.pallas_call(
        body, out_shape=(jax.ShapeDtypeStruct((t, n), F32), jax.ShapeDtypeStruct((t, k), BF16)),
        grid=(t // tm, n // tn), in_specs=in_specs,
        out_specs=(tile, pl.BlockSpec((tm, k), lambda i, j: (i, 0))),
        scratch_shapes=[pltpu.VMEM((tm, k), BF16)], compiler_params=_cparams(("parallel", "arbitrary")),
        name=name)(*xs, *gs, w, *([res] if has_res else []))


def _rms_bwd_rows(xv, gv, dhv):
    r = lax.rsqrt(jnp.mean(xv * xv, axis=-1, keepdims=True) + EPS)
    nrm = xv * r
    dn = dhv * gv
    return r * (dn - nrm * jnp.mean(dn * nrm, axis=-1, keepdims=True)), dhv * nrm


def _mm_rms_bwd(a, b, *, tb, xs, gs, dres, after, name):
    nx = len(xs)
    t, k = a.shape
    widths = [x.shape[1] for x in xs]
    n = sum(widths)
    assert n == (b.shape[0] if tb else b.shape[1])
    tm = _tile(t, 512)
    has_res, has_after = dres is not None, after is not None

    def body(*refs):
        a_ref, b_ref = refs[0], refs[1]
        x_refs, g_refs = refs[2:2 + nx], refs[2 + nx:2 + 2 * nx]
        r_ref = refs[2 + 2 * nx] if has_res else None
        outs = refs[2 + 2 * nx + has_res + has_after:]
        dh = lax.dot_general(a_ref[...], b_ref[...], NT_DIMS if tb else (((1,), (0,)), ((), ())),
                             preferred_element_type=F32)
        off = 0
        for i, wd in enumerate(widths):
            dx_ref, dxb_ref, dg_ref = outs[3 * i:3 * i + 3]
            dx, dgs = _rms_bwd_rows(x_refs[i][...], g_refs[i][...], dh[:, off:off + wd])
            if has_res:
                dx = dx + r_ref[...]
            dx_ref[...] = dx
            dxb_ref[...] = dx.astype(BF16)
            part = jnp.sum(dgs, axis=0, keepdims=True)
            dg_ref[...] = part + jnp.where(pl.program_id(0) > 0, dg_ref[...], 0.0)
            off += wd

    in_specs = [pl.BlockSpec((tm, k), lambda i: (i, 0)), pl.BlockSpec(b.shape, lambda i: (0, 0))]
    in_specs += [pl.BlockSpec((tm, wd), lambda i: (i, 0)) for wd in widths]
    in_specs += [pl.BlockSpec((1, wd), lambda i: (0, 0)) for wd in widths]
    args = [a, b, *xs, *gs]
    if has_res:
        in_specs.append(pl.BlockSpec((tm, widths[0]), lambda i: (i, 0)))
        args.append(dres)
    if has_after:
        in_specs.append(HBM_SPEC)
        args.append(after)
    out_shape, out_specs = [], []
    for wd in widths:
        out_shape += [jax.ShapeDtypeStruct((t, wd), F32), jax.ShapeDtypeStruct((t, wd), BF16),
                      jax.ShapeDtypeStruct((1, wd), F32)]
        out_specs += [pl.BlockSpec((tm, wd), lambda i: (i, 0)), pl.BlockSpec((tm, wd), lambda i: (i, 0)),
                      pl.BlockSpec((1, wd), lambda i: (0, 0))]
    res = pl.pallas_call(body, out_shape=out_shape, grid=(t // tm,), in_specs=in_specs, out_specs=out_specs,
                         compiler_params=_cparams(("arbitrary",)), name=name)(*args)
    return [tuple(res[3 * i:3 * i + 3]) for i in range(nx)]


FFN_ROWS = 512
FFN_FWD_ROWS = 1024
FFN_SPLIT = 2
FFN_W_ROWS = 1024
SCAN_ROWS = 256


NT_DIMS = (((1,), (1,)), ((), ()))
TN_DIMS = (((0,), (0,)), ((), ()))


def _ffn_fwd_call(x, g, wg, wu, wd, name):
    t, d = x.shape
    ns, f, _ = wg.shape
    tm = _tile(t, FFN_FWD_ROWS)

    def body(x_ref, g_ref, wg_ref, wu_ref, wd_ref, xo_ref, h_ref, gate_ref, up_ref, h_sc, acc_ref):
        s = pl.program_id(1)

        @pl.when(s == 0)
        def _():
            xv = x_ref[...]
            r = lax.rsqrt(jnp.mean(xv * xv, axis=-1, keepdims=True) + EPS)
            hb = (xv * r * g_ref[...]).astype(BF16)
            h_sc[...] = hb
            h_ref[...] = hb

        for r0 in range(0, tm, tm // FFN_SPLIT):
            rows = slice(r0, r0 + tm // FFN_SPLIT)
            hb = h_sc[rows, :]
            gate = lax.dot_general(hb, wg_ref[0], NT_DIMS, preferred_element_type=F32)
            up = lax.dot_general(hb, wu_ref[0], NT_DIMS, preferred_element_type=F32)
            gate_ref[0, rows, :] = gate.astype(BF16)
            up_ref[0, rows, :] = up.astype(BF16)
            act = (gate * _sigmoid_tanh(gate) * up).astype(BF16)
            prod = jnp.dot(act, wd_ref[0], preferred_element_type=F32)
            acc_ref[rows, :] = prod + jnp.where(s > 0, acc_ref[rows, :], 0.0)

        @pl.when(s == ns - 1)
        def _():
            xo_ref[...] = x_ref[...] + 0.5 * acc_ref[...]

    row = pl.BlockSpec((tm, d), lambda i, s: (i, 0))
    vec = pl.BlockSpec((1, d), lambda i, s: (0, 0))
    wrow = pl.BlockSpec((1, f, d), lambda i, s: (s, 0, 0))
    hid = pl.BlockSpec((1, tm, f), lambda i, s: (s, i, 0))
    hid_sh = jax.ShapeDtypeStruct((ns, t, f), BF16)
    return pl.pallas_call(
        body, out_shape=(jax.ShapeDtypeStruct((t, d), F32), jax.ShapeDtypeStruct((t, d), BF16), hid_sh, hid_sh),
        grid=(t // tm, ns), in_specs=[row, vec, wrow, wrow, wrow], out_specs=(row, row, hid, hid),
        scratch_shapes=[pltpu.VMEM((tm, d), BF16), pltpu.VMEM((tm, d), F32)],
        compiler_params=_cparams(("parallel", "arbitrary")), name=name)(x, g, wg, wu, wd)


def _ffn_bwd_x_call(dxo, dxo_b, x, g, gate, up, wg, wu, wd, name):
    t, d = x.shape
    ns, f, _ = wg.shape
    tm = _tile(t, FFN_ROWS)

    def body(dxo_ref, dxb_ref, x_ref, g_ref, gate_ref, up_ref, wg_ref, wu_ref, wd_ref,
             dx_ref, dxob_ref, dgn_ref, dgate_ref, dup_ref, act_ref, dh_ref):
        s, i = pl.program_id(0), pl.program_id(1)
        base = pl.multiple_of(i * tm, tm)
        for r0 in range(0, tm, tm // FFN_SPLIT):
            rows = slice(r0, r0 + tm // FFN_SPLIT)
            acc_rows = pl.ds(base + r0, tm // FFN_SPLIT)
            dact = lax.dot_general(dxb_ref[rows, :], wd_ref[0], NT_DIMS, preferred_element_type=F32) * 0.5
            gv = gate_ref[0, rows, :].astype(F32)
            uv = up_ref[0, rows, :].astype(F32)
            sg = _sigmoid_tanh(gv)
            silu = gv * sg
            act_ref[0, rows, :] = (silu * uv).astype(BF16)
            dub = (dact * silu).astype(BF16)
            dgb = (dact * uv * sg * (1.0 + gv * (1.0 - sg))).astype(BF16)
            dup_ref[0, rows, :] = dub
            dgate_ref[0, rows, :] = dgb
            prod = (jnp.dot(dgb, wg_ref[0], preferred_element_type=F32)
                    + jnp.dot(dub, wu_ref[0], preferred_element_type=F32))

            dh_ref[acc_rows, :] = prod + jnp.where(s > 0, dh_ref[acc_rows, :], 0.0)

        @pl.when(jnp.logical_and(i == 0, s == 0))
        def _():
            dgn_ref[...] = jnp.zeros_like(dgn_ref)

        @pl.when(s == ns - 1)
        def _():
            dx, dgs = _rms_bwd_rows(x_ref[...], g_ref[...], dh_ref[pl.ds(base, tm), :])
            dx = dx + dxo_ref[...]
            dx_ref[...] = dx
            dxob_ref[...] = dx.astype(BF16)
            dgn_ref[...] += jnp.sum(dgs, axis=0, keepdims=True)

    last_only = lambda s, i: (jnp.where(s == ns - 1, i, 0), 0)
    row_last = pl.BlockSpec((tm, d), last_only)
    row = pl.BlockSpec((tm, d), lambda s, i: (i, 0))
    vec = pl.BlockSpec((1, d), lambda s, i: (0, 0))
    wrow = pl.BlockSpec((1, f, d), lambda s, i: (s, 0, 0))
    hid = pl.BlockSpec((1, tm, f), lambda s, i: (s, i, 0))
    hid_sh = jax.ShapeDtypeStruct((ns, t, f), BF16)
    return pl.pallas_call(
        body,
        out_shape=(jax.ShapeDtypeStruct((t, d), F32), jax.ShapeDtypeStruct((t, d), BF16),
                   jax.ShapeDtypeStruct((1, d), F32), hid_sh, hid_sh, hid_sh),
        grid=(ns, t // tm), in_specs=[row_last, row, row_last, vec, hid, hid, wrow, wrow, wrow],
        out_specs=(row_last, row_last, vec, hid, hid, hid), scratch_shapes=[pltpu.VMEM((t, d), F32)],
        compiler_params=_cparams(("arbitrary", "arbitrary")), name=name)(dxo, dxo_b, x, g, gate, up, wg, wu, wd)


def _ffn_bwd_w_call(h, dxo_b, dgate, dup, act, name, after=None):
    t, d = h.shape
    ns, _, f = dgate.shape
    tm = _tile(t, FFN_W_ROWS)
    nm = t // tm

    def body(h_ref, dxb_ref, dgate_ref, dup_ref, act_ref, *rest):
        dwg_ref, dwu_ref, dwd_ref, ag_ref, au_ref, ad_ref = rest[-6:]
        i = pl.program_id(1)
        hv = h_ref[...]
        pg = lax.dot_general(dgate_ref[0], hv, TN_DIMS, preferred_element_type=F32)
        pu = lax.dot_general(dup_ref[0], hv, TN_DIMS, preferred_element_type=F32)
        pd = lax.dot_general(act_ref[0], dxb_ref[...], TN_DIMS, preferred_element_type=F32)

        ag_ref[...] = pg + jnp.where(i > 0, ag_ref[...], 0.0)
        au_ref[...] = pu + jnp.where(i > 0, au_ref[...], 0.0)
        ad_ref[...] = pd + jnp.where(i > 0, ad_ref[...], 0.0)

        @pl.when(i == nm - 1)
        def _():
            dwg_ref[0] = ag_ref[...].astype(BF16)
            dwu_ref[0] = au_ref[...].astype(BF16)
            dwd_ref[0] = (0.5 * ad_ref[...]).astype(BF16)

    row = pl.BlockSpec((tm, d), lambda s, i: (i, 0))
    hid = pl.BlockSpec((1, tm, f), lambda s, i: (s, i, 0))
    wrow = pl.BlockSpec((1, f, d), lambda s, i: (s, 0, 0))
    wsh = jax.ShapeDtypeStruct((ns, f, d), BF16)
    return pl.pallas_call(
        body, out_shape=(wsh, wsh, wsh),
        grid=(ns, nm), in_specs=[row, row, hid, hid, hid] + [HBM_SPEC] * (after is not None),
        out_specs=(wrow, wrow, wrow),
        scratch_shapes=[pltpu.VMEM((f, d), F32), pltpu.VMEM((f, d), F32), pltpu.VMEM((f, d), F32)],
        compiler_params=_cparams(("parallel", "arbitrary")), name=name)(
            h, dxo_b, dgate, dup, act, *([after] if after is not None else []))


def _loss_head(x, g, tgt, name):
    t, w = x.shape
    tr = _row_tile(t)

    def body(x_ref, g_ref, t_ref, loss_ref, dx_ref, dxb_ref, dg_ref):
        xv = x_ref[...]
        gv = g_ref[...]
        r = lax.rsqrt(jnp.mean(xv * xv, axis=-1, keepdims=True) + EPS)
        nrm = xv * r
        err = nrm * gv - t_ref[...]
        dout = err * (1.0 / w)
        dn = dout * gv
        dx = r * (dn - nrm * jnp.mean(dn * nrm, axis=-1, keepdims=True))
        dx_ref[...] = dx
        dxb_ref[...] = dx.astype(BF16)

        @pl.when(pl.program_id(0) == 0)
        def _():
            dg_ref[...] = jnp.zeros_like(dg_ref)
            loss_ref[...] = jnp.zeros_like(loss_ref)

        dg_ref[...] += jnp.sum(dout * nrm, axis=0, keepdims=True)
        part = jnp.sum(jnp.sum(err * err, axis=-1, keepdims=True) * (0.5 / w), axis=0, keepdims=True)
        loss_ref[...] += jnp.broadcast_to(part, loss_ref.shape)

    row = pl.BlockSpec((tr, w), lambda i: (i, 0))
    vec = pl.BlockSpec((1, w), lambda i: (0, 0))
    return pl.pallas_call(
        body, out_shape=(jax.ShapeDtypeStruct((1, LANES), F32), jax.ShapeDtypeStruct((t, w), F32),
                         jax.ShapeDtypeStruct((t, w), BF16), jax.ShapeDtypeStruct((1, w), F32)),
        grid=(t // tr,), in_specs=[row, vec, row],
        out_specs=(pl.BlockSpec((1, LANES), lambda i: (0, 0)), row, row, vec),
        compiler_params=_cparams(("arbitrary",)), name=name)(x, g, tgt)


def _attn_bias():
    slopes = np.asarray(2.0 ** (-8.0 * (np.arange(ATTN_HEADS) + 1) / ATTN_HEADS), np.float32)
    qi = np.arange(QBLOCK)[:, None]
    kj = np.arange(3 * QBLOCK)[None, :]
    rel = np.abs(kj - QBLOCK - qi).astype(np.float32)
    tile = np.where(rel <= WINDOW, -slopes[:, None, None] * rel[None], np.float32(NEG_INF)).astype(np.float32)
    tile = tile.reshape(KV_HEADS, GQ * QBLOCK, 3 * QBLOCK)
    return jnp.asarray(np.swapaxes(tile, 1, 2))


def _attn_scores(k3, q, n, nb, bias):
    s = lax.dot_general(k3, q, NT_DIMS, preferred_element_type=F32) * (HEAD_DIM ** -0.5)
    key = lax.broadcasted_iota(jnp.int32, (3 * QBLOCK, 1), 0)
    inside = (key >= jnp.where(n == 0, QBLOCK, 0)) & (key < jnp.where(n == nb - 1, 2 * QBLOCK, 3 * QBLOCK))
    return jnp.where(inside, s + bias, NEG_INF)


Q_COL, K_COL, V_COL, U_COL = 0, ATTN_WIDTH // LANES, ATTN_WIDTH // LANES + 1, ATTN_WIDTH // LANES + 2


def _key_rows(ref, n, nb):
    prev, nxt = jnp.maximum(n - 1, 0), jnp.minimum(n + 1, nb - 1)
    blk = lambda b: ref[pl.ds(pl.multiple_of(b * QBLOCK, QBLOCK), QBLOCK), :]
    return jnp.concatenate([blk(prev), blk(n), blk(nxt)], axis=0)


def _head_tiles(x, kh, low):
    tiles = []
    for g in range(GQ):
        h = GQ * kh + g
        t128 = x[:, LANES * (h // 2):LANES * (h // 2 + 1)]
        t128 = jnp.where(low if h % 2 == 0 else jnp.logical_not(low), t128, 0.0)
        if h % 2 != kh:
            t128 = pltpu.roll(t128, HEAD_DIM, 1)
        tiles.append(t128)
    return jnp.concatenate(tiles, axis=0)


def _head_merge(per_kh, low):
    out = []
    for j in range(ATTN_HEADS // 2):
        pair = []
        for h in (2 * j, 2 * j + 1):
            kh, g = h // GQ, h % GQ
            t128 = per_kh[kh][g * QBLOCK:(g + 1) * QBLOCK, :]
            if h % 2 != kh:
                t128 = pltpu.roll(t128, HEAD_DIM, 1)
            pair.append(t128)
        out.append(jnp.where(low, pair[0], pair[1]))
    return jnp.concatenate(out, axis=1)


def _attn_fwd_proj(proj, sink_rows, bias, name):
    t = proj.shape[0]
    nb = t // QBLOCK
    rows = GQ * QBLOCK

    def body(q_ref, k_ref, v_ref, sink_ref, bias_ref, o_ref, lse_ref):
        n = pl.program_id(0)
        low = lax.broadcasted_iota(jnp.int32, (QBLOCK, LANES), 1) < HEAD_DIM
        k3 = _key_rows(k_ref, n, nb).astype(BF16)
        v3 = _key_rows(v_ref, n, nb).astype(BF16)
        q = q_ref[...]
        outs = []
        for kh in range(KV_HEADS):
            qs = _head_tiles(q, kh, low).astype(BF16)
            s = _attn_scores(k3, qs, n, nb, bias_ref[kh])
            sink = sink_ref[kh]
            mx = jnp.maximum(jnp.max(s, axis=0, keepdims=True), sink)
            p = jnp.exp(s - mx)
            den = jnp.sum(p, axis=0, keepdims=True) + jnp.exp(sink - mx)
            pn = (p * (1.0 / den)).astype(BF16)
            outs.append(lax.dot_general(pn, v3, TN_DIMS, preferred_element_type=F32))
            lse_ref[0, kh] = mx + jnp.log(den)
        o_ref[...] = _head_merge(outs, low)

    strip = lambda col: pl.BlockSpec((t, LANES), lambda n, col=col: (0, col))
    rowspec = pl.BlockSpec((KV_HEADS, 1, rows), lambda n: (0, 0, 0))
    biasspec = pl.BlockSpec((KV_HEADS, 3 * QBLOCK, rows), lambda n: (0, 0, 0))
    return pl.pallas_call(
        body, out_shape=(jax.ShapeDtypeStruct((t, ATTN_WIDTH), F32), jax.ShapeDtypeStruct((nb, KV_HEADS, 1, rows), F32)),
        grid=(nb,), in_specs=[pl.BlockSpec((QBLOCK, ATTN_WIDTH), lambda n: (n, 0)), strip(K_COL), strip(V_COL),
                              rowspec, biasspec],
        out_specs=(pl.BlockSpec((QBLOCK, ATTN_WIDTH), lambda n: (n, 0)),
                   pl.BlockSpec((1, KV_HEADS, 1, rows), lambda n: (n, 0, 0, 0))),
        compiler_params=_cparams(("parallel",)), name=name)(proj, proj, proj, sink_rows, bias)


def _attn_bwd_proj(proj, sink_rows, bias, o, lse, do, name):
    t = proj.shape[0]
    nb = t // QBLOCK
    rows = GQ * QBLOCK
    scale = HEAD_DIM ** -0.5

    def body(q_ref, k_ref, v_ref, sink_ref, bias_ref, o_ref, lse_ref, do_ref, dq_ref, dk_ref, dv_ref, ds_ref):
        n = pl.program_id(0)

        @pl.when(n == 0)
        def _():
            dk_ref[...] = jnp.zeros_like(dk_ref)
            dv_ref[...] = jnp.zeros_like(dv_ref)
            ds_ref[...] = jnp.zeros_like(ds_ref)

        low = lax.broadcasted_iota(jnp.int32, (QBLOCK, LANES), 1) < HEAD_DIM
        k3 = _key_rows(k_ref, n, nb).astype(BF16)
        v3 = _key_rows(v_ref, n, nb).astype(BF16)
        q, dov = q_ref[...], do_ref[...]
        dod = dov * o_ref[...]
        dqs = []
        dk3 = jnp.zeros((3 * QBLOCK, LANES), F32)
        dv3 = jnp.zeros((3 * QBLOCK, LANES), F32)
        ones = jnp.ones((SUBLANES, LANES), F32)
        for kh in range(KV_HEADS):
            qs = _head_tiles(q, kh, low).astype(BF16)
            dos = _head_tiles(dov, kh, low).astype(BF16)
            delta = lax.dot_general(ones, _head_tiles(dod, kh, low), NT_DIMS, preferred_element_type=F32,
                                    precision=lax.Precision.HIGHEST)[0:1, :]
            lse_kh = lse_ref[0, kh]
            s = _attn_scores(k3, qs, n, nb, bias_ref[kh])
            p = jnp.exp(s - lse_kh)
            dp = lax.dot_general(v3, dos, NT_DIMS, preferred_element_type=F32)
            dsb = (p * (dp - delta)).astype(BF16)
            dqs.append(lax.dot_general(dsb, k3, TN_DIMS, preferred_element_type=F32) * scale)
            dk3 = dk3 + jnp.dot(dsb, qs, preferred_element_type=F32) * scale
            dv3 = dv3 + jnp.dot(p.astype(BF16), dos, preferred_element_type=F32)
            ds_ref[kh] += -jnp.exp(sink_ref[kh] - lse_kh) * delta
        dq_ref[...] = _head_merge(dqs, low)
        prev, nxt = jnp.maximum(n - 1, 0), jnp.minimum(n + 1, nb - 1)
        for j, b in enumerate((prev, n, nxt)):
            blk = pl.ds(pl.multiple_of(b * QBLOCK, QBLOCK), QBLOCK)
            dk_ref[blk, :] += dk3[j * QBLOCK:(j + 1) * QBLOCK, :]
            dv_ref[blk, :] += dv3[j * QBLOCK:(j + 1) * QBLOCK, :]

    strip = lambda col: pl.BlockSpec((t, LANES), lambda n, col=col: (0, col))
    rowspec = pl.BlockSpec((KV_HEADS, 1, rows), lambda n: (0, 0, 0))
    qspec = pl.BlockSpec((QBLOCK, ATTN_WIDTH), lambda n: (n, 0))
    kv_out = pl.BlockSpec((t, LANES), lambda n: (0, 0))
    biasspec = pl.BlockSpec((KV_HEADS, 3 * QBLOCK, rows), lambda n: (0, 0, 0))
    return pl.pallas_call(
        body,
        out_shape=(jax.ShapeDtypeStruct((t, ATTN_WIDTH), F32), jax.ShapeDtypeStruct((t, LANES), F32),
                   jax.ShapeDtypeStruct((t, LANES), F32), jax.ShapeDtypeStruct((KV_HEADS, 1, rows), F32)),
        grid=(nb,),
        in_specs=[qspec, strip(K_COL), strip(V_COL), rowspec, biasspec, qspec,
                  pl.BlockSpec((1, KV_HEADS, 1, rows), lambda n: (n, 0, 0, 0)), qspec],
        out_specs=(qspec, kv_out, kv_out, rowspec),
        compiler_params=_cparams(("arbitrary",)), name=name)(proj, proj, proj, sink_rows, bias, o, lse, do)


def _scan_tables(a_re, a_im, reverse):
    pw = [(a_re, a_im)]
    for _ in range(SUBLANES - 1):
        pr, pi = pw[-1]
        pw.append((pr * a_re - pi * a_im, pr * a_im + pi * a_re))
    rows = np.arange(SUBLANES)
    tabs = []
    for d in (1, 2, 4):
        mask = (rows <= SUBLANES - 1 - d) if reverse else (rows >= d)
        m = jnp.asarray(mask, F32)[:, None]
        tabs += [m * pw[d - 1][0][None, :], m * pw[d - 1][1][None, :]]
    order = (SUBLANES - 1 - rows) if reverse else rows
    tabs += [jnp.stack([pw[j][0] for j in order]), jnp.stack([pw[j][1] for j in order])]
    tab = jnp.stack(tabs)
    return tab.reshape(8, SUBLANES, N_STRIPS, STRIP_ST).transpose(2, 0, 1, 3)


def _scan_pair_chunk(dirs):
    nblk = dirs[0]['xr'].shape[0] // SUBLANES

    @pl.when(pl.program_id(1) == 0)
    def _():
        for d in dirs:
            d['carry'][...] = jnp.zeros_like(d['carry'])

    for d in dirs:
        vb = d['v'][...].astype(BF16)
        d['xr'][...] = jnp.dot(vb, d['mir'][0], preferred_element_type=F32)
        d['xi'][...] = jnp.dot(vb, d['mii'][0], preferred_element_type=F32)
    carries = [(d['carry'][0], d['carry'][1]) for d in dirs]
    for i in range(nblk):
        for k, d in enumerate(dirs):
            rev = d['reverse']
            rows = pl.ds(((nblk - 1 - i) if rev else i) * SUBLANES, SUBLANES)
            cr, ci = carries[k]
            xr, xi = d['xr'][rows, :], d['xi'][rows, :]
            for j, s in enumerate((1, 2, 4)):
                tr_, ti_ = d['tab'][0, 2 * j], d['tab'][0, 2 * j + 1]
                sh = (SUBLANES - s) if rev else s
                sr, si = pltpu.roll(xr, sh, 0), pltpu.roll(xi, sh, 0)
                xr, xi = xr + tr_ * sr - ti_ * si, xi + tr_ * si + ti_ * sr
            pr, pi = d['tab'][0, 6], d['tab'][0, 7]
            xr, xi = xr + pr * cr - pi * ci, xi + pr * ci + pi * cr
            d['xr'][rows, :] = xr
            d['xi'][rows, :] = xi
            edge = 0 if rev else SUBLANES - 1
            carries[k] = (jnp.broadcast_to(xr[edge:edge + 1, :], xr.shape),
                          jnp.broadcast_to(xi[edge:edge + 1, :], xi.shape))
    for k, d in enumerate(dirs):
        d['carry'][0], d['carry'][1] = carries[k]
        d['y'][...] = (jnp.dot(d['xr'][...].astype(BF16), d['mor'][0], preferred_element_type=F32)
                       + jnp.dot(d['xi'][...].astype(BF16), d['moi'][0], preferred_element_type=F32))


def _scan_pair(v, ops_f, ops_b, name):
    t = v.shape[0]
    tc = _tile(t, SCAN_ROWS)
    nc = t // tc

    def body(vf_ref, vb_ref, *refs):
        ops = refs[:10]
        outs = refs[10:16]
        carries = refs[16:18]
        dirs = []
        for k, (v_ref, rev) in enumerate(((vf_ref, False), (vb_ref, True))):
            mir, mii, tab, mor, moi = ops[5 * k:5 * k + 5]
            y, xr, xi = outs[3 * k:3 * k + 3]
            dirs.append(dict(v=v_ref, mir=mir, mii=mii, tab=tab, mor=mor, moi=moi, y=y, xr=xr, xi=xi,
                             carry=carries[k], reverse=rev))
        _scan_pair_chunk(dirs)

    col0 = v.shape[1] // STRIP_IN - N_STRIPS
    fmap = lambda s, c: (c, s)
    bmap = lambda s, c: (nc - 1 - c, s)
    smap3 = lambda s, c: (s, 0, 0)
    m_in = pl.BlockSpec((1, STRIP_IN, STRIP_ST), smap3)
    m_out = pl.BlockSpec((1, STRIP_ST, STRIP_IN), smap3)
    tabspec = pl.BlockSpec((1, 8, SUBLANES, STRIP_ST), lambda s, c: (s, 0, 0, 0))
    opspecs = [m_in, m_in, tabspec, m_out, m_out]
    y_sh = jax.ShapeDtypeStruct((t, SSM_WIDTH), F32)
    x_sh = jax.ShapeDtypeStruct((t, N_STRIPS * STRIP_ST), F32)
    outspecs = lambda m: [pl.BlockSpec((tc, STRIP_IN), m), pl.BlockSpec((tc, STRIP_ST), m),
                          pl.BlockSpec((tc, STRIP_ST), m)]
    res = pl.pallas_call(
        body, out_shape=[y_sh, x_sh, x_sh] * 2, grid=(N_STRIPS, nc),
        in_specs=[pl.BlockSpec((tc, STRIP_IN), lambda s, c: (c, s + col0)),
                  pl.BlockSpec((tc, STRIP_IN), lambda s, c: (nc - 1 - c, s + col0))] + opspecs * 2,
        out_specs=outspecs(fmap) + outspecs(bmap),
        scratch_shapes=[pltpu.VMEM((2, SUBLANES, STRIP_ST), F32)] * 2,
        compiler_params=_cparams(("parallel", "arbitrary")), name=name)(v, v, *ops_f, *ops_b)
    return tuple(res[:3]), tuple(res[3:])


def _scan_adjoint_pair(dy, u, states, adj_ops, name):
    t = dy.shape[0]
    tc = _tile(t, SCAN_ROWS)
    nc = t // tc
    hb = tc // SUBLANES
    n_out = 6

    def body(*refs):
        c = pl.program_id(1)
        dirs = []
        for k in range(2):
            dy_ref, mir, mii, tab, mor, moi, u_ref, xr_ref, xi_ref, hr_ref, hi_ref = refs[11 * k:11 * k + 11]
            outs = refs[22 + n_out * k:22 + n_out * (k + 1)]
            lr_ref, li_ref, carry = refs[22 + 2 * n_out + 3 * k:22 + 2 * n_out + 3 * k + 3]
            dirs.append(dict(v=dy_ref, mir=mir, mii=mii, tab=tab, mor=mor, moi=moi, y=outs[0], xr=lr_ref, xi=li_ref,
                             carry=carry, reverse=(k == 0), u=u_ref, fx=(xr_ref, xi_ref), halo=(hr_ref, hi_ref),
                             acc=outs[1:]))

        @pl.when(c == 0)
        def _():
            for d in dirs:
                for r in d['acc']:
                    r[...] = jnp.zeros_like(r)

        _scan_pair_chunk(dirs)
        for d in dirs:
            fwd_reverse = not d['reverse']
            rc = (nc - 1 - c) if d['reverse'] else c
            dmir_ref, dmii_ref, dmor_ref, dmoi_ref, da_ref = d['acc']
            xrv, xiv, lrv, liv = d['fx'][0][...], d['fx'][1][...], d['xr'][...], d['xi'][...]
            hr_ref, hi_ref = d['halo']
            row = lax.broadcasted_iota(jnp.int32, xrv.shape, 0)
            if fwd_reverse:
                live = (rc < nc - 1).astype(F32)
                edge_r, edge_i = hr_ref[0:1, :] * live, hi_ref[0:1, :] * live
                xpr = jnp.where(row == tc - 1, edge_r, pltpu.roll(xrv, tc - 1, 0))
                xpi = jnp.where(row == tc - 1, edge_i, pltpu.roll(xiv, tc - 1, 0))
            else:
                live = (rc > 0).astype(F32)
                edge_r, edge_i = hr_ref[SUBLANES - 1:SUBLANES, :] * live, hi_ref[SUBLANES - 1:SUBLANES, :] * live
                xpr = jnp.where(row == 0, edge_r, pltpu.roll(xrv, 1, 0))
                xpi = jnp.where(row == 0, edge_i, pltpu.roll(xiv, 1, 0))
            da_ref[0, 0:1, :] += jnp.sum(xpr * lrv + xpi * liv, axis=0, keepdims=True)
            da_ref[0, 1:2, :] += jnp.sum(xpr * liv - xpi * lrv, axis=0, keepdims=True)
            ub, dyb = d['u'][...].astype(BF16), d['v'][...].astype(BF16)
            dmir_ref[0] += lax.dot_general(ub, lrv.astype(BF16), TN_DIMS, preferred_element_type=F32)
            dmii_ref[0] += lax.dot_general(ub, liv.astype(BF16), TN_DIMS, preferred_element_type=F32)
            dmor_ref[0] += lax.dot_general(xrv.astype(BF16), dyb, TN_DIMS, preferred_element_type=F32)
            dmoi_ref[0] += lax.dot_general(xiv.astype(BF16), dyb, TN_DIMS, preferred_element_type=F32)

    col0 = u.shape[1] // STRIP_IN - N_STRIPS
    smap3 = lambda s, c: (s, 0, 0)
    m_in = pl.BlockSpec((1, STRIP_IN, STRIP_ST), smap3)
    m_out = pl.BlockSpec((1, STRIP_ST, STRIP_IN), smap3)
    tabspec = pl.BlockSpec((1, 8, SUBLANES, STRIP_ST), lambda s, c: (s, 0, 0, 0))
    in_specs, out_specs, args = [], [], []
    for k in range(2):
        reverse = k == 0
        rowblk = (lambda c: nc - 1 - c) if reverse else (lambda c: c)
        tmap = lambda s, c, rowblk=rowblk: (rowblk(c), s)
        umap = lambda s, c, rowblk=rowblk: (rowblk(c), s + col0)
        if not reverse:
            hmap = lambda s, c, rowblk=rowblk: (jnp.minimum((rowblk(c) + 1) * hb, t // SUBLANES - 1), s)
        else:
            hmap = lambda s, c, rowblk=rowblk: (jnp.maximum(rowblk(c) * hb - 1, 0), s)
        narrow = pl.BlockSpec((tc, STRIP_IN), tmap)
        wide = pl.BlockSpec((tc, STRIP_ST), tmap)
        halo = pl.BlockSpec((SUBLANES, STRIP_ST), hmap)
        in_specs += [narrow, m_in, m_in, tabspec, m_out, m_out, pl.BlockSpec((tc, STRIP_IN), umap), wide, wide,
                     halo, halo]
        out_specs += [narrow, m_in, m_in, m_out, m_out, pl.BlockSpec((1, SUBLANES, STRIP_ST), smap3)]
        xr, xi = states[k]
        args += [dy, *adj_ops[k], u, xr, xi, xr, xi]
    out_shape = [jax.ShapeDtypeStruct((t, SSM_WIDTH), F32),
                 jax.ShapeDtypeStruct((N_STRIPS, STRIP_IN, STRIP_ST), F32),
                 jax.ShapeDtypeStruct((N_STRIPS, STRIP_IN, STRIP_ST), F32),
                 jax.ShapeDtypeStruct((N_STRIPS, STRIP_ST, STRIP_IN), F32),
                 jax.ShapeDtypeStruct((N_STRIPS, STRIP_ST, STRIP_IN), F32),
                 jax.ShapeDtypeStruct((N_STRIPS, SUBLANES, STRIP_ST), F32)] * 2
    res = pl.pallas_call(
        body, out_shape=out_shape, grid=(N_STRIPS, nc), in_specs=in_specs, out_specs=out_specs,
        scratch_shapes=[pltpu.VMEM((tc, STRIP_ST), F32), pltpu.VMEM((tc, STRIP_ST), F32),
                        pltpu.VMEM((2, SUBLANES, STRIP_ST), F32)] * 2,
        compiler_params=_cparams(("parallel", "arbitrary")), name=name)(*args)
    return tuple(res[:n_out]), tuple(res[n_out:])


def _ssm_prep(lam_re, lam_im, log_dt, bt_re, bt_im, c_re, c_im):
    lr = jnp.minimum(lam_re, LAMBDA_RE_MAX)
    li = lam_im
    dt = jnp.exp(log_dt)[:, None]
    mag = jnp.exp(lr * dt)
    a_re = mag * jnp.cos(li * dt)
    a_im = mag * jnp.sin(li * dt)
    den = lr * lr + li * li
    coef_re = ((a_re - 1.0) * lr + a_im * li) / den
    coef_im = (a_im * lr - (a_re - 1.0) * li) / den
    bb_re = coef_re[:, None, :] * bt_re - coef_im[:, None, :] * bt_im
    bb_im = coef_re[:, None, :] * bt_im + coef_im[:, None, :] * bt_re
    eye = jnp.eye(SSM_GROUPS // N_STRIPS, dtype=F32)

    def strips(m):
        g, a, b = m.shape
        m4 = m.reshape(N_STRIPS, g // N_STRIPS, a, b)
        return jnp.einsum('sgab,gk->sgakb', m4, eye).reshape(N_STRIPS, g // N_STRIPS * a, g // N_STRIPS * b)

    mi_re = strips(bb_re)
    mi_im = strips(bb_im)
    mo_re = strips(jnp.swapaxes(c_re, 1, 2))
    mo_im = strips(-jnp.swapaxes(c_im, 1, 2))
    return a_re.reshape(-1), a_im.reshape(-1), mi_re, mi_im, mo_re, mo_im


def _gelu(x):
    c = math.sqrt(2.0 / math.pi)
    return 0.5 * x * (1.0 + jnp.tanh(c * (x + 0.044715 * x * x * x)))


def _gelu_grad(x):
    c = math.sqrt(2.0 / math.pi)
    th = jnp.tanh(c * (x + 0.044715 * x * x * x))
    return 0.5 * (1.0 + th) + 0.5 * x * (1.0 - th * th) * c * (1.0 + 3.0 * 0.044715 * x * x)


def _last_cols_specs(u, w, tr):
    half = w // 2
    first = (u.shape[1] - w) // half
    assert first * half == u.shape[1] - w
    return [pl.BlockSpec((tr, half), lambda i, k=k: (i, first + k)) for k in range(2)]


def _ssm_post_fwd(u, yf, yb, d, wglu, bglu, name):
    t, w = yf.shape
    tr = _row_tile(t)

    def body(ua_ref, ub_ref, yf_ref, yb_ref, d_ref, w_ref, b_ref, s_ref, y0_ref, z_ref):
        uv = jnp.concatenate([ua_ref[...], ub_ref[...]], axis=1)
        y0 = d_ref[...] * uv + yf_ref[...] + yb_ref[...]
        yg = _gelu(y0)
        z = jnp.dot(yg.astype(BF16), w_ref[...], preferred_element_type=F32) + b_ref[...]
        s_ref[...] = yg * _sigmoid(z)
        y0_ref[...] = y0
        z_ref[...] = z

    row = pl.BlockSpec((tr, w), lambda i: (i, 0))
    vec = pl.BlockSpec((1, w), lambda i: (0, 0))
    mat = pl.BlockSpec((w, w), lambda i: (0, 0))
    sh = jax.ShapeDtypeStruct((t, w), F32)
    return pl.pallas_call(body, out_shape=(sh, sh, sh), grid=(t // tr,),
                          in_specs=[*_last_cols_specs(u, w, tr), row, row, vec, mat, vec], out_specs=(row, row, row),
                          compiler_params=_cparams(("parallel",)), name=name)(u, u, yf, yb, d, wglu, bglu)


def _ssm_post_bwd(ds, y0, z, u, d, wglu, name):
    t, w = ds.shape
    tr = _row_tile(t)

    def body(ds_ref, y0_ref, z_ref, ua_ref, ub_ref, d_ref, w_ref, dy0_ref, dw_ref, db_ref, dd_ref):
        @pl.when(pl.program_id(0) == 0)
        def _():
            dw_ref[...] = jnp.zeros_like(dw_ref)
            db_ref[...] = jnp.zeros_like(db_ref)
            dd_ref[...] = jnp.zeros_like(dd_ref)

        y0 = y0_ref[...]
        yg = _gelu(y0)
        sg = _sigmoid(z_ref[...])
        dsv = ds_ref[...]
        dz = dsv * yg * sg * (1.0 - sg)
        dzb = dz.astype(BF16)
        dyg = dsv * sg + lax.dot_general(dzb, w_ref[...], (((1,), (1,)), ((), ())), preferred_element_type=F32)
        dy0 = dyg * _gelu_grad(y0)
        dy0_ref[...] = dy0
        dw_ref[...] += lax.dot_general(yg.astype(BF16), dzb, (((0,), (0,)), ((), ())), preferred_element_type=F32)
        db_ref[...] += jnp.sum(dz, axis=0, keepdims=True)
        uv = jnp.concatenate([ua_ref[...], ub_ref[...]], axis=1)
        dd_ref[...] += jnp.sum(dy0 * uv, axis=0, keepdims=True)

    row = pl.BlockSpec((tr, w), lambda i: (i, 0))
    vec = pl.BlockSpec((1, w), lambda i: (0, 0))
    mat = pl.BlockSpec((w, w), lambda i: (0, 0))
    return pl.pallas_call(
        body, out_shape=(jax.ShapeDtypeStruct((t, w), F32), jax.ShapeDtypeStruct((w, w), F32),
                         jax.ShapeDtypeStruct((1, w), F32), jax.ShapeDtypeStruct((1, w), F32)),
        grid=(t // tr,), in_specs=[row, row, row, *_last_cols_specs(u, w, tr), vec, mat],
        out_specs=(row, mat, vec, vec),
        compiler_params=_cparams(("arbitrary",)), name=name)(ds, y0, z, u, u, d, wglu)


def _du_combine(dy0, d, du_f, du_b, name):
    t, w = dy0.shape
    tr = _row_tile(t)

    def body(dy_ref, d_ref, a_ref, b_ref, o_ref):
        o_ref[...] = d_ref[...] * dy_ref[...] + a_ref[...] + b_ref[...]

    row = pl.BlockSpec((tr, w), lambda i: (i, 0))
    vec = pl.BlockSpec((1, w), lambda i: (0, 0))
    return pl.pallas_call(body, out_shape=jax.ShapeDtypeStruct((t, w), F32), grid=(t // tr,),
                          in_specs=[row, vec, row, row], out_specs=row, compiler_params=_cparams(("parallel",)),
                          name=name)(dy0, d, du_f, du_b)


def _ffn_fwd(x, g, wg, wu, wd, tag):
    xo, h, gate, up = _ffn_fwd_call(x, g, wg, wu, wd, f"{tag}_fwd")
    return xo, (h, gate, up)


def _ffn_bwd(dxo, dxo_b, x, g, wg, wu, wd, saved, tag, between=None):
    h, gate, up = saved
    dx, dx_b, dg, dgate, dup, act = _ffn_bwd_x_call(dxo, dxo_b, x, g, gate, up, wg, wu, wd, f"{tag}_bwd_x")
    after = between(dg) if between is not None else None
    dwg, dwu, dwd = _ffn_bwd_w_call(h, dxo_b, dgate, dup, act, f"{tag}_bwd_w", after=after)
    return dx, dx_b, dg, dwg, dwu, dwd


def _local_step(x, tgt, w, get_weights, put_grads, reduce_wide, put_narrow):
    t = x.shape[0]
    row = lambda a: a.reshape(1, -1)
    grads = {}

    w = dict(w)

    ssm_names = ['ssm_lambda_re', 'ssm_lambda_im', 'ssm_log_dt', 'ssm_b_re', 'ssm_b_im', 'ssm_c_re', 'ssm_c_im']
    tr3 = lambda m: jnp.swapaxes(m, 1, 2)
    fwd_ops, adj_ops, vjps = [], [], []
    for direction in range(2):
        rev = direction == 1
        prep, vjp = jax.vjp(_ssm_prep, *[w[n][direction] for n in ssm_names])
        a_re, a_im = prep[0], prep[1]
        mi_re, mi_im, mo_re, mo_im = (m.astype(BF16) for m in prep[2:])
        fwd_ops.append((mi_re, mi_im, _scan_tables(a_re, a_im, rev), mo_re, mo_im))
        adj_ops.append((tr3(mo_re), tr3(mo_im), _scan_tables(a_re, -a_im, not rev), tr3(mi_re), tr3(mi_im)))
        vjps.append(vjp)
    sink_rows = jnp.repeat(w['attn_sinks'].reshape(KV_HEADS, GQ), QBLOCK, axis=1)[:, None, :]
    bias = _attn_bias()
    prepared = sum(jnp.sum(op[:1, :1].astype(F32)) for ops in fwd_ops + adj_ops for op in ops) + sink_rows[0, 0, 0]

    w.update(get_weights('ffn1', prepared.reshape(1, 1)))
    x1, ffn1_saved = _ffn_fwd(x, w['norm_ffn1'], w['ffn1_w_gate'], w['ffn1_w_up'], w['ffn1_w_down'], "ffn1")
    w.update(get_weights('mix', x1))

    proj, h2 = _norm_mm([x1], [w['norm_mix']], w['w_in'], tb=True, res=None, name="in_proj")
    u = proj

    attn, lse = _attn_fwd_proj(proj, sink_rows, bias, "attn_fwd")

    (y_f, *states_f), (y_b, *states_b) = _scan_pair(u, fwd_ops[0], fwd_ops[1], "s5_fwd")
    ys, states = [y_f, y_b], [states_f, states_b]
    d_row = row(w['ssm_d'])
    s, y0, z = _ssm_post_fwd(u, ys[0], ys[1], d_row, w['ssm_glu_w'], row(w['ssm_glu_b']), "ssm_post")

    x2, mixed = _norm_mm([attn, s], [row(w['attn_out_norm']), row(w['ssm_out_norm'])], w['w_out'], tb=False,
                         res=x1, name="out_proj")

    w.update(get_weights('ffn2', x2))
    x3, ffn2_saved = _ffn_fwd(x2, w['norm_ffn2'], w['ffn2_w_gate'], w['ffn2_w_up'], w['ffn2_w_down'], "ffn2")

    loss, dx3, dx3_b, dgf = _loss_head(x3, row(w['final_norm']), tgt, "loss_head")
    grads['final_norm'] = dgf.reshape(w['final_norm'].shape)

    dx2, dx2_b, dg, dwg, dwu, dwd = _ffn_bwd(dx3, dx3_b, x2, w['norm_ffn2'], w['ffn2_w_gate'], w['ffn2_w_up'],
                                             w['ffn2_w_down'], ffn2_saved, "ffn2")
    grads['norm_ffn2'] = dg
    sent = put_grads('ffn2', dict(ffn2_w_gate=dwg, ffn2_w_up=dwu, ffn2_w_down=dwd))

    (dattn, _, dga), (ds, _, dgs) = _mm_rms_bwd(
        dx2_b, w['w_out'], tb=True, xs=[attn, s], gs=[row(w['attn_out_norm']), row(w['ssm_out_norm'])],
        dres=None, after=sent, name="out_proj_dx")
    dw_out = _mm(mixed, dx2_b, ta=True, out_dtype=BF16, name="out_proj_dw")[0]
    grads.update(attn_out_norm=dga, ssm_out_norm=dgs)

    dy0, dwglu, dbglu, dd = _ssm_post_bwd(ds, y0, z, u, d_row, w['ssm_glu_w'], "ssm_post_bwd")
    grads['ssm_glu_b'] = dbglu
    grads['ssm_d'] = dd.reshape(w['ssm_d'].shape)
    dparams, du_dirs = [], []
    for direction, res in enumerate(_scan_adjoint_pair(dy0, u, states, adj_ops, "s5_adj")):
        du_dir, dmir, dmii, dmor, dmoi, da = res
        du_dirs.append(du_dir)
        da_re = da[:, 0, :].reshape(-1)
        da_im = da[:, 1, :].reshape(-1)
        dparams.append(vjps[direction]((da_re, da_im, dmir, dmii, dmor, dmoi)))
    du = _du_combine(dy0, d_row, du_dirs[0], du_dirs[1], "ssm_du")
    for i, n in enumerate(ssm_names):
        grads[n] = jnp.stack([dparams[0][i], dparams[1][i]])
    wide_sum = reduce_wide(grads)

    dq, dk, dv, dsink = _attn_bwd_proj(proj, sink_rows, bias, attn, lse, dattn, "attn_bwd")
    grads['attn_sinks'] = jnp.sum(dsink.reshape(ATTN_HEADS, QBLOCK), axis=-1).reshape(w['attn_sinks'].shape)
    dproj = jnp.concatenate([dq, dk, dv, du], axis=-1).astype(BF16)

    dw_in = _mm(dproj, h2, ta=True, out_dtype=BF16, after=wide_sum, name="in_proj_dw")[0]
    sent = put_grads('mix', dict(w_in=dw_in, ssm_glu_w=dwglu, w_out=dw_out))
    ((dx1, dx1_b, dgm),) = _mm_rms_bwd(dproj, w['w_in'], tb=False, xs=[x1], gs=[w['norm_mix']], dres=dx2,
                                       after=sent, name="in_proj_dx")
    grads['norm_mix'] = dgm

    def all_small_known(dg):
        grads['norm_ffn1'] = dg
        return put_narrow(grads, loss)

    dx0, _, dg, dwg, dwu, dwd = _ffn_bwd(dx1, dx1_b, x, w['norm_ffn1'], w['ffn1_w_gate'], w['ffn1_w_up'],
                                         w['ffn1_w_down'], ffn1_saved, "ffn1", between=all_small_known)
    put_grads('ffn1', dict(ffn1_w_gate=dwg, ffn1_w_up=dwu, ffn1_w_down=dwd))
    return dx0, grads


HBM_SPEC = pl.BlockSpec(memory_space=pl.ANY)


def _chip_peers(x, y):
    return [(1 - x, y), (x, 1 - y), (1 - x, 1 - y)]


HBM_ONLY = pl.BlockSpec(memory_space=pltpu.HBM)
SEM_SPEC = pl.BlockSpec(memory_space=pltpu.SEMAPHORE)
EFFECT = pltpu.SideEffectType.DATAFLOW_SIDE_EFFECTING


def _place_own(srcs, slot, name):
    na = len(srcs)
    r, c = srcs[0].shape
    tr = r // 2

    def body(slot_ref, *refs):
        for a in range(na):
            refs[na + a][0] = refs[a][...]

    return pl.pallas_call(
        body, out_shape=[jax.ShapeDtypeStruct((N_CHIPS, r, c), s.dtype) for s in srcs],
        grid_spec=pltpu.PrefetchScalarGridSpec(
            num_scalar_prefetch=1, grid=(2,), in_specs=[pl.BlockSpec((tr, c), lambda i, s: (i, 0))] * na,
            out_specs=[pl.BlockSpec((1, tr, c), lambda i, s: (s[0], i, 0))] * na),
        compiler_params=_cparams(("parallel",)), name=name)(slot, *srcs)


def _chip_copies(srcs, lands, send_sems, recv_sems, scatter, landed):
    x, y, c = lax.axis_index("x"), lax.axis_index("y"), lax.axis_index("c")
    me = 2 * x + y
    out = []
    for i in range(len(srcs)):
        for j, (px, py) in enumerate(_chip_peers(x, y)):
            p = 2 * px + py
            slot = p if landed else me
            if scatter:
                src, dst = srcs[i].at[p], lands[i].at[slot]
            else:
                rows = _core_half(srcs[i].shape[0], c)
                src, dst = srcs[i].at[rows], lands[i].at[slot, rows]
            out.append(pltpu.make_async_remote_copy(src, dst, send_sems.at[3 * i + j], recv_sems.at[3 * i + j],
                                                    device_id=(px, py, c), device_id_type=MESH))
    return out


def _core_half(nrows, c):
    half = nrows // 2
    return pl.ds(pl.multiple_of(c * half, 16), half)


FORWARD_BARRIER_ID, HALF_SWAP_BARRIER_ID, HALF_FORWARD_BARRIER_ID = 0, 1, 2


def _sibling_handshake():
    x, y, c = lax.axis_index("x"), lax.axis_index("y"), lax.axis_index("c")
    barrier = pltpu.get_barrier_semaphore()
    pl.semaphore_signal(barrier, inc=1, device_id=(x, y, 1 - c), device_id_type=MESH)
    pl.semaphore_wait(barrier, 1)


def _sibling_forward(lands, name):
    n = len(lands)

    def body(*refs):
        bufs = refs[n:2 * n]
        send_sems, recv_sems = refs[2 * n:]
        _sibling_handshake()
        x, y, c = lax.axis_index("x"), lax.axis_index("y"), lax.axis_index("c")
        mine = [_core_half(b.shape[1], c) for b in bufs]
        theirs = [_core_half(b.shape[1], 1 - c) for b in bufs]
        chips = [2 * px + py for px, py in _chip_peers(x, y)]
        cps = [pltpu.make_async_remote_copy(bufs[i].at[p, mine[i]], bufs[i].at[p, mine[i]], send_sems.at[3 * i + j],
                                            recv_sems.at[3 * i + j], device_id=(x, y, 1 - c), device_id_type=MESH)
               for i in range(n) for j, p in enumerate(chips)]
        for cp in cps:
            cp.start()
        for i in range(n):
            for j, p in enumerate(chips):
                pltpu.make_async_remote_copy(bufs[i].at[p, mine[i]], bufs[i].at[p, theirs[i]], send_sems.at[3 * i + j],
                                             recv_sems.at[3 * i + j], device_id=(x, y, 1 - c),
                                             device_id_type=MESH).wait()

    return pl.pallas_call(
        body, out_shape=[jax.ShapeDtypeStruct(a.shape, a.dtype) for a in lands],
        in_specs=[HBM_SPEC] * n, out_specs=[HBM_SPEC] * n, input_output_aliases={k: k for k in range(n)},
        scratch_shapes=[pltpu.SemaphoreType.DMA((3 * n,)), pltpu.SemaphoreType.DMA((3 * n,))],
        compiler_params=pltpu.CompilerParams(collective_id=FORWARD_BARRIER_ID), name=name)(*lands)


def _exchange_start(groups, scatter, name, after=None):
    sizes = [len(srcs) for srcs, _ in groups]
    flat_src = [a for srcs, _ in groups for a in srcs]
    flat_land = [a for _, lands in groups for a in lands]
    n = len(flat_src)
    ng = len(groups)

    def body(*refs):
        src_refs, land_refs = refs[:n], refs[n:2 * n]
        n_in = 2 * n + (after is not None)
        sems = refs[n_in:n_in + 2 * ng]
        token_ref = refs[-1]
        off = 0
        for gi, sz in enumerate(sizes):
            for cp in _chip_copies(src_refs[off:off + sz], land_refs[off:off + sz], sems[2 * gi], sems[2 * gi + 1],
                                   scatter, landed=False):
                cp.start()
            off += sz
        token_ref[...] = jnp.zeros_like(token_ref)

    sem_shapes = []
    for sz in sizes:
        sem_shapes += [pltpu.SemaphoreType.DMA((3 * sz,)), pltpu.SemaphoreType.DMA((3 * sz,))]
    hbm = lambda a: pltpu.HBM(a.shape, a.dtype)
    res = pl.pallas_call(
        body, name=name,
        out_shape=(tuple(sem_shapes) + tuple(hbm(a) for a in flat_src) + tuple(hbm(a) for a in flat_land)
                   + (jax.ShapeDtypeStruct((SUBLANES, LANES), F32),)),
        in_specs=[HBM_ONLY] * (2 * n) + [HBM_SPEC] * (after is not None),
        out_specs=tuple([SEM_SPEC] * (2 * ng) + [HBM_ONLY] * (2 * n) + [pl.BlockSpec(memory_space=pltpu.VMEM)]),
        input_output_aliases={k: 2 * ng + k for k in range(2 * n)},
        compiler_params=pltpu.CompilerParams(has_side_effects=EFFECT),
    )(*[pltpu.with_memory_space_constraint(a, pltpu.HBM) for a in flat_src + flat_land],
      *([after] if after is not None else []))
    sems, thru_src, thru_land = res[:2 * ng], res[2 * ng:2 * ng + n], res[2 * ng + n:2 * ng + 2 * n]
    out, off = [], 0
    for gi, sz in enumerate(sizes):
        out.append((sems[2 * gi], sems[2 * gi + 1], list(thru_src[off:off + sz]), list(thru_land[off:off + sz])))
        off += sz
    return out, res[-1]


def _exchange_wait(started, after, scatter, name):
    send_sems, recv_sems, srcs, lands = started
    n = len(srcs)

    def body(*refs):
        src_refs, land_refs = refs[:n], refs[n:2 * n]
        send_ref, recv_ref = refs[2 * n], refs[2 * n + 1]
        for cp in _chip_copies(src_refs, land_refs, send_ref, recv_ref, scatter, landed=True):
            cp.wait_send()
            cp.wait_recv()

    hbm = lambda a: pltpu.HBM(a.shape, a.dtype)
    res = pl.pallas_call(
        body, name=name, out_shape=tuple(hbm(a) for a in srcs) + tuple(hbm(a) for a in lands),
        in_specs=[HBM_ONLY] * (2 * n) + [SEM_SPEC, SEM_SPEC, HBM_SPEC], out_specs=tuple([HBM_ONLY] * (2 * n)),
        input_output_aliases={k: k for k in range(2 * n)},
        compiler_params=pltpu.CompilerParams(has_side_effects=EFFECT),
    )(*srcs, *lands, send_sems, recv_sems, after)
    return list(res[:n]), list(res[n:])


def _half_swap(parts, name):
    n = len(parts)

    def body(*refs):
        ins, outs = refs[:n], refs[n:2 * n]
        send_sems, recv_sems = refs[2 * n:]
        _sibling_handshake()
        x, y, c = lax.axis_index("x"), lax.axis_index("y"), lax.axis_index("c")
        cps = [pltpu.make_async_remote_copy(ins[i].at[k, _core_half(ins[i].shape[1], 1 - c)], outs[i].at[k],
                                            send_sems.at[N_CHIPS * i + k], recv_sems.at[N_CHIPS * i + k],
                                            device_id=(x, y, 1 - c), device_id_type=MESH)
               for i in range(n) for k in range(N_CHIPS)]
        for cp in cps:
            cp.start()
        for cp in cps:
            cp.wait()

    return pl.pallas_call(
        body, out_shape=[jax.ShapeDtypeStruct((N_CHIPS, p.shape[1] // 2, p.shape[2]), p.dtype) for p in parts],
        in_specs=[HBM_SPEC] * n, out_specs=[HBM_SPEC] * n,
        scratch_shapes=[pltpu.SemaphoreType.DMA((N_CHIPS * n,)), pltpu.SemaphoreType.DMA((N_CHIPS * n,))],
        compiler_params=pltpu.CompilerParams(collective_id=HALF_SWAP_BARRIER_ID), name=name)(*parts)


def _half_add(parts, sib, slots, name):
    na = len(parts)
    _, r, c = parts[0].shape
    hr = r // 2
    tr = _row_tile(hr, 512)
    nt = hr // tr

    def body(slot_ref, *refs):
        for a in range(na):
            refs[2 * na + a][...] = (refs[2 * a][...].astype(F32) + refs[2 * a + 1][...].astype(F32)).astype(BF16)

    mine = pl.BlockSpec((1, tr, c), lambda k, i, s: (k, i + s[4] * nt, 0))
    half = pl.BlockSpec((1, tr, c), lambda k, i, s: (k, i, 0))
    args = [a for p, sb in zip(parts, sib) for a in (p, sb)]
    return pl.pallas_call(
        body, out_shape=[jax.ShapeDtypeStruct((N_CHIPS, hr, c), BF16)] * na,
        grid_spec=pltpu.PrefetchScalarGridSpec(
            num_scalar_prefetch=1, grid=(N_CHIPS, nt), in_specs=[mine, half] * na, out_specs=[half] * na),
        compiler_params=_cparams(("parallel", "parallel")), name=name)(slots, *args)


def _half_forward(arrs, name):
    n = len(arrs)

    def body(*refs):
        bufs = refs[n:2 * n]
        send_sems, recv_sems = refs[2 * n:]
        _sibling_handshake()
        x, y, c = lax.axis_index("x"), lax.axis_index("y"), lax.axis_index("c")
        cps = [pltpu.make_async_remote_copy(b.at[_core_half(b.shape[0], c)], b.at[_core_half(b.shape[0], c)],
                                            send_sems.at[i], recv_sems.at[i], device_id=(x, y, 1 - c),
                                            device_id_type=MESH) for i, b in enumerate(bufs)]
        for cp in cps:
            cp.start()
        for i, b in enumerate(bufs):
            pltpu.make_async_remote_copy(b.at[_core_half(b.shape[0], c)], b.at[_core_half(b.shape[0], 1 - c)],
                                         send_sems.at[i], recv_sems.at[i], device_id=(x, y, 1 - c),
                                         device_id_type=MESH).wait()

    return pl.pallas_call(
        body, out_shape=[jax.ShapeDtypeStruct(a.shape, a.dtype) for a in arrs],
        in_specs=[HBM_SPEC] * n, out_specs=[HBM_SPEC] * n, input_output_aliases={k: k for k in range(n)},
        scratch_shapes=[pltpu.SemaphoreType.DMA((n,)), pltpu.SemaphoreType.DMA((n,))],
        compiler_params=pltpu.CompilerParams(collective_id=HALF_FORWARD_BARRIER_ID), name=name)(*arrs)


def _sum_parts(parts, recv, slots, name):
    na = len(parts)
    _, r, c = parts[0].shape
    tr = _row_tile(r, 192)

    def body(slot_ref, *refs):
        for a in range(na):
            own_ref, r0_ref, r1_ref, r2_ref = refs[4 * a:4 * a + 4]
            refs[4 * na + a][...] = ((own_ref[0].astype(F32) + r0_ref[0].astype(F32))
                                     + (r1_ref[0].astype(F32) + r2_ref[0].astype(F32)))

    blk = lambda k: pl.BlockSpec((1, tr, c), lambda i, s, k=k: (s[k], i, 0))
    out_blk = pl.BlockSpec((tr, c), lambda i, s: (i + s[4] * (r // tr), 0))
    args = [a for p, rv in zip(parts, recv) for a in (p, rv, rv, rv)]
    return pl.pallas_call(
        body, out_shape=[jax.ShapeDtypeStruct((2 * r, c), F32)] * na,
        grid_spec=pltpu.PrefetchScalarGridSpec(
            num_scalar_prefetch=1, grid=(r // tr,), in_specs=[blk(0), blk(1), blk(2), blk(3)] * na,
            out_specs=[out_blk] * na),
        compiler_params=_cparams(("parallel",)), name=name)(slots, *args)


ALL_PEERS = [(fx, fy, fc) for fx in (0, 1) for fy in (0, 1) for fc in (0, 1)][1:]


def _all8_copies(srcs, lands, send_sems, recv_sems, landed):
    x, y, c = lax.axis_index("x"), lax.axis_index("y"), lax.axis_index("c")
    lin = 4 * x + 2 * y + c
    out = []
    for i, (src, land) in enumerate(zip(srcs, lands)):
        for j, (fx, fy, fc) in enumerate(ALL_PEERS):
            px, py, pc = x ^ fx, y ^ fy, c ^ fc
            slot = (4 * px + 2 * py + pc) if landed else lin
            out.append(pltpu.make_async_remote_copy(src, land.at[slot], send_sems.at[7 * i + j], recv_sems.at[7 * i + j],
                                                    device_id=(px, py, pc), device_id_type=MESH))
    return out


def _all8_start(srcs, name):
    n = len(srcs)
    lands = [lax.empty((N_DEV,) + s.shape, s.dtype) for s in srcs]

    def body(*refs):
        for cp in _all8_copies(refs[:n], refs[n:2 * n], refs[2 * n], refs[2 * n + 1], landed=False):
            cp.start()
        refs[-1][...] = jnp.zeros_like(refs[-1])

    hbm = lambda a: pltpu.HBM(a.shape, a.dtype)
    res = pl.pallas_call(
        body, name=name,
        out_shape=(pltpu.SemaphoreType.DMA((7 * n,)), pltpu.SemaphoreType.DMA((7 * n,)), *[hbm(a) for a in srcs],
                   *[hbm(a) for a in lands], jax.ShapeDtypeStruct((SUBLANES, LANES), F32)),
        in_specs=[HBM_ONLY] * (2 * n),
        out_specs=(SEM_SPEC, SEM_SPEC, *[HBM_ONLY] * (2 * n), pl.BlockSpec(memory_space=pltpu.VMEM)),
        input_output_aliases={k: 2 + k for k in range(2 * n)},
        compiler_params=pltpu.CompilerParams(has_side_effects=EFFECT),
    )(*[pltpu.with_memory_space_constraint(a, pltpu.HBM) for a in list(srcs) + lands])
    return (res[0], res[1], list(res[2:2 + n]), list(res[2 + n:2 + 2 * n])), res[-1]


def _all8_wait(started, after, name):
    send_sems, recv_sems, srcs, lands = started
    n = len(srcs)

    def body(*refs):
        for cp in _all8_copies(refs[:n], refs[n:2 * n], refs[2 * n], refs[2 * n + 1], landed=True):
            cp.wait_send()
            cp.wait_recv()

    hbm = lambda a: pltpu.HBM(a.shape, a.dtype)
    res = pl.pallas_call(
        body, name=name, out_shape=tuple(hbm(a) for a in srcs) + tuple(hbm(a) for a in lands),
        in_specs=[HBM_ONLY] * (2 * n) + [SEM_SPEC, SEM_SPEC, HBM_SPEC], out_specs=tuple([HBM_ONLY] * (2 * n)),
        input_output_aliases={k: k for k in range(2 * n)},
        compiler_params=pltpu.CompilerParams(has_side_effects=EFFECT),
    )(*srcs, *lands, send_sems, recv_sems, after)
    return list(res[:n]), list(res[n:])


def _sum8(own, land, lin, name):
    r, c = own.shape
    tr = _row_tile(r)

    def body(lin_ref, own_ref, land_ref, o_ref):
        me = lin_ref[0]
        acc = None
        for k in range(N_DEV):
            term = jnp.where(me == k, own_ref[...], land_ref[k])
            acc = term if acc is None else acc + term
        o_ref[...] = acc

    return pl.pallas_call(
        body, out_shape=jax.ShapeDtypeStruct((r, c), F32),
        grid_spec=pltpu.PrefetchScalarGridSpec(
            num_scalar_prefetch=1, grid=(r // tr,),
            in_specs=[pl.BlockSpec((tr, c), lambda i, s: (i, 0)), pl.BlockSpec((N_DEV, tr, c), lambda i, s: (0, i, 0))],
            out_specs=pl.BlockSpec((tr, c), lambda i, s: (i, 0))),
        compiler_params=_cparams(("parallel",)), name=name)(lin, own, land)


def _adamw_math(w, m, v, g):
    nm = ADAM_B1 * m + (1.0 - ADAM_B1) * g
    nv = ADAM_B2 * v + (1.0 - ADAM_B2) * (g * g)
    m_hat = nm * (1.0 / (1.0 - ADAM_B1 ** ADAM_STEP))
    v_hat = nv * (1.0 / (1.0 - ADAM_B2 ** ADAM_STEP))
    return -ADAM_LR * (m_hat / (jnp.sqrt(v_hat) + ADAM_EPS) + ADAM_WD * w), nm, nv


def _adamw(ws, ms, vs, gs, name):
    na = len(ws)
    r, c = ws[0].shape
    tr = _row_tile(r)

    def body(*refs):
        for a in range(na):
            w_ref, m_ref, v_ref, g_ref = refs[4 * a:4 * a + 4]
            go_ref, d_ref, nm_ref, nv_ref = refs[4 * na + 4 * a:4 * na + 4 * a + 4]
            g = g_ref[...]
            go_ref[...] = g
            d_ref[...], nm_ref[...], nv_ref[...] = _adamw_math(w_ref[...], m_ref[...], v_ref[...], g)

    blk = pl.BlockSpec((tr, c), lambda i: (i, 0))
    sh = jax.ShapeDtypeStruct((r, c), F32)
    args = [a for group in zip(ws, ms, vs, gs) for a in group]
    res = pl.pallas_call(body, out_shape=[sh] * (4 * na), grid=(r // tr,), in_specs=[blk] * (4 * na),
                         out_specs=[blk] * (4 * na), compiler_params=_cparams(("parallel",)), name=name)(*args)
    return [tuple(res[4 * a:4 * a + 4]) for a in range(na)]


def _adamw_small(ws, ms, vs, alls, split, name, owns=None, lin=None):
    n = len(ws)
    lead = split if split is not None else ()
    nl = len(lead)
    nslots = alls[0].shape[0]
    has_own = owns is not None
    nin = 5 if has_own else 4

    def blocks(shape):
        if split is None:
            return tuple(shape), (lambda *g: (0,) * len(shape))
        blk = (shape[0], shape[1] // lead[0], shape[2] // lead[1]) + tuple(shape[3:])
        return blk, (lambda *g: (0, g[0], g[1]) + (0,) * (len(shape) - 3))

    def body(*refs):
        if has_own:
            lin_ref, refs = refs[0], refs[1:]
        w_refs, m_refs, v_refs, a_refs = (refs[k * n:(k + 1) * n] for k in range(4))
        own_refs = refs[4 * n:5 * n] if has_own else None
        g_refs, d_refs, nm_refs, nv_refs = (refs[(nin + k) * n:(nin + 1 + k) * n] for k in range(4))
        k = pl.program_id(nl)
        for i in range(n):
            term = a_refs[i][0]
            if has_own:
                term = jnp.where(k == lin_ref[0], own_refs[i][...], term)

            @pl.when(k == 0)
            def _(i=i, term=term):
                g_refs[i][...] = term

            @pl.when(k > 0)
            def _(i=i, term=term):
                g_refs[i][...] += term

            @pl.when(k == nslots - 1)
            def _(i=i):
                d_refs[i][...], nm_refs[i][...], nv_refs[i][...] = _adamw_math(
                    w_refs[i][...], m_refs[i][...], v_refs[i][...], g_refs[i][...])

    specs, aspecs, shapes = [], [], []
    for wa in ws:
        blk, imap = blocks(wa.shape)
        specs.append(pl.BlockSpec(blk, imap))
        aspecs.append(pl.BlockSpec((1,) + blk, (lambda *g, imap=imap: (g[nl],) + imap(*g))))
        shapes.append(jax.ShapeDtypeStruct(wa.shape, F32))
    grid = tuple(lead) + (nslots,)
    sem = _cparams(("parallel",) * nl + ("arbitrary",))
    if has_own:
        res = pl.pallas_call(
            body, out_shape=shapes * 4,
            grid_spec=pltpu.PrefetchScalarGridSpec(num_scalar_prefetch=1, grid=grid,
                                                   in_specs=specs * 3 + aspecs + specs, out_specs=specs * 4),
            compiler_params=sem, name=name)(lin, *ws, *ms, *vs, *alls, *owns)
    else:
        res = pl.pallas_call(body, out_shape=shapes * 4, grid=grid, in_specs=specs * 3 + aspecs,
                             out_specs=specs * 4, compiler_params=sem, name=name)(*ws, *ms, *vs, *alls)
    return res[:n], res[n:2 * n], res[2 * n:3 * n], res[3 * n:]


def kernel(x, norm_ffn1, ffn1_w_gate, ffn1_w_up, ffn1_w_down, norm_mix, w_in, attn_sinks, ssm_lambda_re, ssm_lambda_im, ssm_log_dt, ssm_b_re, ssm_b_im, ssm_c_re, ssm_c_im, ssm_d, ssm_glu_w, ssm_glu_b, attn_out_norm, ssm_out_norm, w_out, norm_ffn2, ffn2_w_gate, ffn2_w_up, ffn2_w_down, final_norm, loss_target, m_norm_ffn1, m_ffn1_w_gate, m_ffn1_w_up, m_ffn1_w_down, m_norm_mix, m_w_in, m_attn_sinks, m_ssm_lambda_re, m_ssm_lambda_im, m_ssm_log_dt, m_ssm_b_re, m_ssm_b_im, m_ssm_c_re, m_ssm_c_im, m_ssm_d, m_ssm_glu_w, m_ssm_glu_b, m_attn_out_norm, m_ssm_out_norm, m_w_out, m_norm_ffn2, m_ffn2_w_gate, m_ffn2_w_up, m_ffn2_w_down, m_final_norm, v_norm_ffn1, v_ffn1_w_gate, v_ffn1_w_up, v_ffn1_w_down, v_norm_mix, v_w_in, v_attn_sinks, v_ssm_lambda_re, v_ssm_lambda_im, v_ssm_log_dt, v_ssm_b_re, v_ssm_b_im, v_ssm_c_re, v_ssm_c_im, v_ssm_d, v_ssm_glu_w, v_ssm_glu_b, v_attn_out_norm, v_ssm_out_norm, v_w_out, v_norm_ffn2, v_ffn2_w_gate, v_ffn2_w_up, v_ffn2_w_down, v_final_norm):
    given = dict(locals())
    wts = {n: given[n] for n in WEIGHTS}

    order = [g for g in GROUPS]
    cx, cy = lax.axis_index("x"), lax.axis_index("y")
    slots = jnp.stack([2 * cx + cy, 2 * (1 - cx) + cy, 2 * cx + 1 - cy, 2 * (1 - cx) + 1 - cy,
                       lax.axis_index("c")]).astype(jnp.int32)
    def view(a, n):
        if n in TRANSPOSED:
            return jnp.swapaxes(a[0], 0, 1)
        if n in BIG:
            return a[0]
        if n in ('ssm_b_re', 'ssm_b_im'):
            return jnp.swapaxes(a, -1, -2)
        return a.reshape(1, -1) if a.ndim == 1 else a

    def unview(a, n):
        if n in TRANSPOSED:
            return jnp.swapaxes(a, 0, 1)[None]
        if n in ('ssm_b_re', 'ssm_b_im'):
            return jnp.swapaxes(a, -1, -2)
        return a.reshape(wts[n].shape)

    started, gather_token = {}, None
    for g in order:
        shards = [view(wts[n], n).astype(BF16) for n in GROUPS[g]]
        if len({s.shape for s in shards}) == 1:
            placed = _place_own(shards, slots, f"weights_place_{g}")
        else:
            placed = [_place_own([s], slots, f"weights_place_{n}")[0] for n, s in zip(GROUPS[g], shards)]
        st, gather_token = _exchange_start([(shards, placed)], False, f"weights_start_{g}", after=gather_token)
        started[g] = st[0]

    def get_weights(group, after):
        if group == order[0]:
            after = after + gather_token[:1, :1]
        _, lands = _exchange_wait(started[group], after, False, f"weights_wait_{group}")
        lands = _sibling_forward(lands, f"weights_forward_{group}")
        out = dict(zip(GROUPS[group], lands))
        for n in ('w_in', 'ssm_glu_w', 'w_out'):
            if n in out:
                out[n] = out[n].reshape(-1, out[n].shape[-1])
        return out

    sent, tokens = {}, {}

    def put_grads(group, gd):
        parts = []
        for n in GROUPS[group]:
            g = gd[n]
            if g.ndim == 2:
                g = g.reshape(N_CHIPS, g.shape[0] // N_CHIPS, g.shape[1])
            parts.append(g.astype(BF16))
        sib = _half_swap(parts, f"grads_half_swap_{group}")
        same = len({p.shape for p in parts}) == 1
        batches = [list(range(len(parts)))] if same else [[i] for i in range(len(parts))]
        halves = [None] * len(parts)
        for b in batches:
            res = _half_add([parts[i] for i in b], [sib[i] for i in b], slots, f"grads_half_add_{GROUPS[group][b[0]]}")
            for i, h in zip(b, res):
                halves[i] = h
        parts = halves
        lands = [lax.empty(p.shape, p.dtype) for p in parts]
        started_g, tokens[group] = _exchange_start([(parts, lands)], True, f"grads_start_{group}")
        sent[group] = started_g[0]
        return tokens[group]

    w = {n: (wts[n][0] if wts[n].ndim > 1 else wts[n]) for n in SMALL}
    w['norm_ffn1'], w['norm_mix'], w['norm_ffn2'] = wts['norm_ffn1'], wts['norm_mix'], wts['norm_ffn2']
    w['ssm_b_re'], w['ssm_b_im'] = view(wts['ssm_b_re'], 'ssm_b_re')[0], view(wts['ssm_b_im'], 'ssm_b_im')[0]
    w['ssm_log_dt'] = w['ssm_log_dt'] + gather_token[0, 0]
    wide =['ssm_b_re', 'ssm_b_im', 'ssm_c_re', 'ssm_c_im']

    nat = {n: view(wts[n], n).shape for n in SMALL}
    narrow = [n for n in SMALL if n not in wide]
    wide_started, narrow_started = [], []

    def reduce_wide(gd):
        packed = jnp.concatenate([gd[n].reshape(-1, LANES) for n in wide])
        started_w, token = _all8_start([packed], "small_grads_start")
        wide_started.append(started_w)
        return token

    def put_narrow(gd, loss_row):
        started_n, token = _all8_start([gd[n].reshape(nat[n]) for n in narrow] + [loss_row], "narrow_grads_start")
        narrow_started.append(started_n)
        return token

    dx, grads = _local_step(x[0], loss_target[0], w, get_weights, put_grads, reduce_wide, put_narrow)

    out_g, out_d, out_m, out_v = {}, {}, {}, {}

    def finish(group, after):
        names = GROUPS[group]
        parts, recv = _exchange_wait(sent[group], after, True, f"grads_wait_{group}")
        same = len({p.shape for p in parts}) == 1
        batches = [list(range(len(names)))] if same else [[i] for i in range(len(names))]
        sums = [None] * len(names)
        for b in batches:
            res = _sum_parts([parts[i] for i in b], [recv[i] for i in b], slots, f"grad_sum_{names[b[0]]}")
            for i, sm in zip(b, res):
                sums[i] = sm
        full = _half_forward(sums, f"grad_half_forward_{group}")
        for b in batches:
            res = _adamw([view(wts[names[i]], names[i]) for i in b], [view(given['m_' + names[i]], names[i]) for i in b],
                         [view(given['v_' + names[i]], names[i]) for i in b], [full[i] for i in b],
                         f"adamw_{names[b[0]]}")
            for i, (g, d, nm, nv) in zip(b, res):
                n = names[i]
                out_g[n], out_d[n], out_m[n], out_v[n] = (unview(a, n) for a in (g, d, nm, nv))
        return nv

    done = finish('ffn2', tokens['ffn1'])
    done = finish('mix', done)

    lin = (4 * cx + 2 * cy + lax.axis_index("c")).astype(jnp.int32).reshape(1)
    (own_w,), (land_w,) = _all8_wait(wide_started[0], done, "small_grads_wait")
    wide_sum = _sum8(own_w, land_w, lin, "small_grads_sum")
    rows = wide_sum.shape[0] // len(wide)
    wide_g = [wide_sum[i * rows:(i + 1) * rows].reshape((1,) + nat[n]) for i, n in enumerate(wide)]
    owns_n, lands_n = _all8_wait(narrow_started[0], done, "narrow_grads_wait")
    loss_shares = jnp.where(jnp.arange(N_DEV) == lin[0], owns_n[-1][0, 0], lands_n[-1][:, 0, 0])
    loss = jnp.sum(loss_shares)
    for group, gs, owns, split, tag in ((narrow, lands_n[:-1], owns_n[:-1], None, "adamw_small"),
                                        (wide, wide_g, None, (2, 4), "adamw_ssm_bc")):
        res = _adamw_small([view(wts[n], n) for n in group], [view(given['m_' + n], n) for n in group],
                           [view(given['v_' + n], n) for n in group], gs, split, tag, owns=owns,
                           lin=lin if owns is not None else None)
        for dst, vals in zip((out_g, out_d, out_m, out_v), res):
            for n, a in zip(group, vals):
                dst[n] = unview(a, n)

    finish('ffn1', out_v['norm_ffn1'][:, :1] + out_v['ssm_c_re'].reshape(1, -1)[:, :1] + done[:1, :1] + loss)

    return (loss, dx[None], *[out_g[n] for n in WEIGHTS], *[out_d[n] for n in WEIGHTS],
            *[out_m[n] for n in WEIGHTS], *[out_v[n] for n in WEIGHTS])
```

```python
import functools
import math

import numpy as np
import jax
import jax.numpy as jnp
from jax import lax
from jax.experimental import pallas as pl
from jax.experimental.pallas import tpu as pltpu

F32 = jnp.float32
BF16 = jnp.bfloat16
MESH = pl.DeviceIdType.MESH

EPS = 1e-6
NEG_INF = -1e30
LAMBDA_RE_MAX = -1e-4
ATTN_HEADS = 8
KV_HEADS = 2
GQ = ATTN_HEADS // KV_HEADS
HEAD_DIM = 64
ATTN_WIDTH = 512
KV_WIDTH = 128
WINDOW = 128
QBLOCK = 128
SSM_WIDTH = 512
SSM_GROUPS = 32
SSM_CH = 16
SSM_STATE = 64
N_STRIPS = 4
STRIP_IN = SSM_WIDTH // N_STRIPS
STRIP_ST = SSM_GROUPS * SSM_STATE // N_STRIPS
SUBLANES = 8
LANES = 128
N_CHIPS = 4
N_DEV = 8

ADAM_LR = 0.001
ADAM_B1 = 0.9
ADAM_B2 = 0.999
ADAM_EPS = 1e-08
ADAM_WD = 0.01
ADAM_STEP = 10

VMEM_LIMIT = 48 * 1024 * 1024

WEIGHTS = ['norm_ffn1', 'ffn1_w_gate', 'ffn1_w_up', 'ffn1_w_down', 'norm_mix', 'w_in', 'attn_sinks',
           'ssm_lambda_re', 'ssm_lambda_im', 'ssm_log_dt', 'ssm_b_re', 'ssm_b_im', 'ssm_c_re', 'ssm_c_im',
           'ssm_d', 'ssm_glu_w', 'ssm_glu_b', 'attn_out_norm', 'ssm_out_norm', 'w_out', 'norm_ffn2',
           'ffn2_w_gate', 'ffn2_w_up', 'ffn2_w_down', 'final_norm']
BIG = ['ffn1_w_gate', 'ffn1_w_up', 'ffn1_w_down', 'w_in', 'ssm_glu_w', 'w_out',
       'ffn2_w_gate', 'ffn2_w_up', 'ffn2_w_down']
SMALL = [n for n in WEIGHTS if n not in BIG]
TRANSPOSED = ['ffn1_w_gate', 'ffn1_w_up', 'w_in', 'ffn2_w_gate', 'ffn2_w_up']
GROUPS = {'ffn1': ['ffn1_w_gate', 'ffn1_w_up', 'ffn1_w_down'],
          'mix': ['w_in', 'ssm_glu_w', 'w_out'],
          'ffn2': ['ffn2_w_gate', 'ffn2_w_up', 'ffn2_w_down']}


def _cparams(sem=None):
    return pltpu.CompilerParams(dimension_semantics=sem, vmem_limit_bytes=VMEM_LIMIT)


def _tile(n, pref):
    if n <= pref:
        return n
    for t in (pref, pref // 2, pref // 4):
        if t % LANES == 0 and n % t == 0:
            return t
    return n


def _sigmoid(x):
    return 1.0 / (1.0 + jnp.exp(-x))


def _sigmoid_tanh(x):
    return 0.5 * jnp.tanh(0.5 * x) + 0.5


def _mm(a, b, *, ta=False, tb=False, reduce_s=False, res=None, scale=1.0, out_dtype=F32, after=None, name):
    a3 = a if a.ndim == 3 else a[None]
    b3 = b if b.ndim == 3 else b[None]
    sa, sb = a3.shape[0], b3.shape[0]
    ns = max(sa, sb)
    (kk, m) = a3.shape[1:] if ta else a3.shape[1:][::-1]
    (n, kb) = b3.shape[1:] if tb else b3.shape[1:][::-1]
    assert kk == kb, (a3.shape, b3.shape)
    tm, tn, tk = _tile(m, 1024), _tile(n, 1024), _tile(kk, 2048)
    nm, nn, nk = m // tm, n // tn, kk // tk
    has_res = res is not None
    single = nk == 1 and not (reduce_s and ns > 1)

    if reduce_s:
        grid = (nm, nn, ns, nk)
        ids = lambda i, j, s, k: (s, i, j, k)
        sem = ("parallel", "parallel", "arbitrary", "arbitrary")
    else:
        grid = (ns, nm, nn, nk)
        ids = lambda s, i, j, k: (s, i, j, k)
        sem = ("parallel", "parallel", "parallel", "arbitrary")

    def a_map(*g):
        s, i, j, k = ids(*g)
        s = s if sa > 1 else 0
        return (s, k, i) if ta else (s, i, k)

    def b_map(*g):
        s, i, j, k = ids(*g)
        s = s if sb > 1 else 0
        return (s, j, k) if tb else (s, k, j)

    def o_map(*g):
        s, i, j, k = ids(*g)
        return (i, j) if reduce_s else (s, i, j)

    a_blk = (1, tk, tm) if ta else (1, tm, tk)
    b_blk = (1, tn, tk) if tb else (1, tk, tn)
    dims = (((0 if ta else 1,), (1 if tb else 0,)), ((), ()))

    def body(*refs):
        a_ref, b_ref = refs[0], refs[1]
        r_ref = refs[2] if has_res else None
        o_ref = refs[2 + has_res + (after is not None)]
        acc_ref = None if single else refs[-1]
        s, _, _, k = ids(*[pl.program_id(d) for d in range(4)])
        prod = lax.dot_general(a_ref[0].astype(BF16), b_ref[0].astype(BF16), dims, preferred_element_type=F32)

        def finish(out):
            if scale != 1.0:
                out = out * scale
            if has_res:
                out = r_ref[...].reshape(out.shape) + out
            o_ref[...] = out.astype(out_dtype).reshape(o_ref.shape)

        if single:
            finish(prod)
            return
        if reduce_s:
            first = jnp.logical_and(s == 0, k == 0)
            last = jnp.logical_and(s == ns - 1, k == nk - 1)
        else:
            first, last = k == 0, k == nk - 1

        acc_ref[...] = prod + jnp.where(first, 0.0, acc_ref[...])

        @pl.when(last)
        def _():
            finish(acc_ref[...])

    in_specs = [pl.BlockSpec(a_blk, a_map), pl.BlockSpec(b_blk, b_map)]
    args = [a3, b3]
    if reduce_s:
        out_shape = jax.ShapeDtypeStruct((m, n), out_dtype)
        o_spec = pl.BlockSpec((tm, tn), o_map)
    else:
        out_shape = jax.ShapeDtypeStruct((ns, m, n), out_dtype)
        o_spec = pl.BlockSpec((1, tm, tn), o_map)
    if has_res:
        assert res.shape == out_shape.shape
        in_specs.append(o_spec)
        args.append(res)
    if after is not None:
        in_specs.append(HBM_SPEC)
        args.append(after)
    return pl.pallas_call(body, out_shape=out_shape, grid=grid, in_specs=in_specs, out_specs=o_spec,
                          scratch_shapes=[] if single else [pltpu.VMEM((tm, tn), F32)],
                          compiler_params=_cparams(sem), name=name)(*args)


def _row_tile(t, cap=256):
    for step in (16, SUBLANES):
        for tr in range(min(cap, t) // step * step, 0, -step):
            if t % tr == 0:
                return tr
    return t


def _norm_mm(xs, gs, w, *, tb, res, name):
    nx = len(xs)
    t = xs[0].shape[0]
    widths = [x.shape[1] for x in xs]
    k = sum(widths)
    n = w.shape[0] if tb else w.shape[1]
    tm = _tile(t, 512)
    tn = n if n <= 1536 else _tile(n, 512)
    has_res = res is not None

    def body(*refs):
        x_refs, g_refs, w_ref = refs[:nx], refs[nx:2 * nx], refs[2 * nx]
        r_ref = refs[2 * nx + 1] if has_res else None
        o_ref, h_ref, h_sc = refs[2 * nx + 1 + has_res:]

        @pl.when(pl.program_id(1) == 0)
        def _():
            off = 0
            for x_ref, g_ref, wd in zip(x_refs, g_refs, widths):
                xv = x_ref[...]
                r = lax.rsqrt(jnp.mean(xv * xv, axis=-1, keepdims=True) + EPS)
                h_sc[:, off:off + wd] = (xv * r * g_ref[...]).astype(BF16)
                off += wd
            h_ref[...] = h_sc[...]

        prod = lax.dot_general(h_sc[...], w_ref[...], NT_DIMS if tb else (((1,), (0,)), ((), ())),
                               preferred_element_type=F32)
        o_ref[...] = r_ref[...] + prod if has_res else prod

    in_specs = [pl.BlockSpec((tm, wd), lambda i, j: (i, 0)) for wd in widths]
    in_specs += [pl.BlockSpec((1, wd), lambda i, j: (0, 0)) for wd in widths]
    in_specs.append(pl.BlockSpec((tn, k), lambda i, j: (j, 0)) if tb else pl.BlockSpec((k, tn), lambda i, j: (0, j)))
    tile = pl.BlockSpec((tm, tn), lambda i, j: (i, j))
    if has_res:
        in_specs.append(tile)
    return pl.pallas_call(
        body, out_shape=(jax.ShapeDtypeStruct((t, n), F32), jax.ShapeDtypeStruct((t, k), BF16)),
        grid=(t // tm, n // tn), in_specs=in_specs,
        out_specs=(tile, pl.BlockSpec((tm, k), lambda i, j: (i, 0))),
        scratch_shapes=[pltpu.VMEM((tm, k), BF16)], compiler_params=_cparams(("parallel", "arbitrary")),
        name=name)(*xs, *gs, w, *([res] if has_res else []))


def _rms_bwd_rows(xv, gv, dhv):
    r = lax.rsqrt(jnp.mean(xv * xv, axis=-1, keepdims=True) + EPS)
    nrm = xv * r
    dn = dhv * gv
    return r * (dn - nrm * jnp.mean(dn * nrm, axis=-1, keepdims=True)), dhv * nrm


def _mm_rms_bwd(a, b, *, tb, xs, gs, dres, after, name):
    nx = len(xs)
    t, k = a.shape
    widths = [x.shape[1] for x in xs]
    n = sum(widths)
    assert n == (b.shape[0] if tb else b.shape[1])
    tm = _tile(t, 512)
    has_res, has_after = dres is not None, after is not None

    def body(*refs):
        a_ref, b_ref = refs[0], refs[1]
        x_refs, g_refs = refs[2:2 + nx], refs[2 + nx:2 + 2 * nx]
        r_ref = refs[2 + 2 * nx] if has_res else None
        outs = refs[2 + 2 * nx + has_res + has_after:]
        dh = lax.dot_general(a_ref[...], b_ref[...], NT_DIMS if tb else (((1,), (0,)), ((), ())),
                             preferred_element_type=F32)
        off = 0
        for i, wd in enumerate(widths):
            dx_ref, dxb_ref, dg_ref = outs[3 * i:3 * i + 3]
            dx, dgs = _rms_bwd_rows(x_refs[i][...], g_refs[i][...], dh[:, off:off + wd])
            if has_res:
                dx = dx + r_ref[...]
            dx_ref[...] = dx
            dxb_ref[...] = dx.astype(BF16)
            part = jnp.sum(dgs, axis=0, keepdims=True)
            dg_ref[...] = part + jnp.where(pl.program_id(0) > 0, dg_ref[...], 0.0)
            off += wd

    in_specs = [pl.BlockSpec((tm, k), lambda i: (i, 0)), pl.BlockSpec(b.shape, lambda i: (0, 0))]
    in_specs += [pl.BlockSpec((tm, wd), lambda i: (i, 0)) for wd in widths]
    in_specs += [pl.BlockSpec((1, wd), lambda i: (0, 0)) for wd in widths]
    args = [a, b, *xs, *gs]
    if has_res:
        in_specs.append(pl.BlockSpec((tm, widths[0]), lambda i: (i, 0)))
        args.append(dres)
    if has_after:
        in_specs.append(HBM_SPEC)
        args.append(after)
    out_shape, out_specs = [], []
    for wd in widths:
        out_shape += [jax.ShapeDtypeStruct((t, wd), F32), jax.ShapeDtypeStruct((t, wd), BF16),
                      jax.ShapeDtypeStruct((1, wd), F32)]
        out_specs += [pl.BlockSpec((tm, wd), lambda i: (i, 0)), pl.BlockSpec((tm, wd), lambda i: (i, 0)),
                      pl.BlockSpec((1, wd), lambda i: (0, 0))]
    res = pl.pallas_call(body, out_shape=out_shape, grid=(t // tm,), in_specs=in_specs, out_specs=out_specs,
                         compiler_params=_cparams(("arbitrary",)), name=name)(*args)
    return [tuple(res[3 * i:3 * i + 3]) for i in range(nx)]


FFN_ROWS = 512
FFN_FWD_ROWS = 1024
FFN_SPLIT = 2
FFN_W_ROWS = 1024
SCAN_ROWS = 256


NT_DIMS = (((1,), (1,)), ((), ()))
TN_DIMS = (((0,), (0,)), ((), ()))


def _ffn_fwd_call(x, g, wg, wu, wd, name):
    t, d = x.shape
    ns, f, _ = wg.shape
    tm = _tile(t, FFN_FWD_ROWS)

    def body(x_ref, g_ref, wg_ref, wu_ref, wd_ref, xo_ref, h_ref, gate_ref, up_ref, h_sc, acc_ref):
        s = pl.program_id(1)

        @pl.when(s == 0)
        def _():
            xv = x_ref[...]
            r = lax.rsqrt(jnp.mean(xv * xv, axis=-1, keepdims=True) + EPS)
            hb = (xv * r * g_ref[...]).astype(BF16)
            h_sc[...] = hb
            h_ref[...] = hb

        for r0 in range(0, tm, tm // FFN_SPLIT):
            rows = slice(r0, r0 + tm // FFN_SPLIT)
            hb = h_sc[rows, :]
            gate = lax.dot_general(hb, wg_ref[0], NT_DIMS, preferred_element_type=F32)
            up = lax.dot_general(hb, wu_ref[0], NT_DIMS, preferred_element_type=F32)
            gate_ref[0, rows, :] = gate.astype(BF16)
            up_ref[0, rows, :] = up.astype(BF16)
            act = (gate * _sigmoid_tanh(gate) * up).astype(BF16)
            prod = jnp.dot(act, wd_ref[0], preferred_element_type=F32)
            acc_ref[rows, :] = prod + jnp.where(s > 0, acc_ref[rows, :], 0.0)

        @pl.when(s == ns - 1)
        def _():
            xo_ref[...] = x_ref[...] + 0.5 * acc_ref[...]

    row = pl.BlockSpec((tm, d), lambda i, s: (i, 0))
    vec = pl.BlockSpec((1, d), lambda i, s: (0, 0))
    wrow = pl.BlockSpec((1, f, d), lambda i, s: (s, 0, 0))
    hid = pl.BlockSpec((1, tm, f), lambda i, s: (s, i, 0))
    hid_sh = jax.ShapeDtypeStruct((ns, t, f), BF16)
    return pl.pallas_call(
        body, out_shape=(jax.ShapeDtypeStruct((t, d), F32), jax.ShapeDtypeStruct((t, d), BF16), hid_sh, hid_sh),
        grid=(t // tm, ns), in_specs=[row, vec, wrow, wrow, wrow], out_specs=(row, row, hid, hid),
        scratch_shapes=[pltpu.VMEM((tm, d), BF16), pltpu.VMEM((tm, d), F32)],
        compiler_params=_cparams(("parallel", "arbitrary")), name=name)(x, g, wg, wu, wd)


def _ffn_bwd_x_call(dxo, dxo_b, x, g, gate, up, wg, wu, wd, name):
    t, d = x.shape
    ns, f, _ = wg.shape
    tm = _tile(t, FFN_ROWS)

    def body(dxo_ref, dxb_ref, x_ref, g_ref, gate_ref, up_ref, wg_ref, wu_ref, wd_ref,
             dx_ref, dxob_ref, dgn_ref, dgate_ref, dup_ref, act_ref, dh_ref):
        s, i = pl.program_id(0), pl.program_id(1)
        base = pl.multiple_of(i * tm, tm)
        for r0 in range(0, tm, tm // FFN_SPLIT):
            rows = slice(r0, r0 + tm // FFN_SPLIT)
            acc_rows = pl.ds(base + r0, tm // FFN_SPLIT)
            dact = lax.dot_general(dxb_ref[rows, :], wd_ref[0], NT_DIMS, preferred_element_type=F32) * 0.5
            gv = gate_ref[0, rows, :].astype(F32)
            uv = up_ref[0, rows, :].astype(F32)
            sg = _sigmoid_tanh(gv)
            silu = gv * sg
            act_ref[0, rows, :] = (silu * uv).astype(BF16)
            dub = (dact * silu).astype(BF16)
            dgb = (dact * uv * sg * (1.0 + gv * (1.0 - sg))).astype(BF16)
            dup_ref[0, rows, :] = dub
            dgate_ref[0, rows, :] = dgb
            prod = (jnp.dot(dgb, wg_ref[0], preferred_element_type=F32)
                    + jnp.dot(dub, wu_ref[0], preferred_element_type=F32))

            dh_ref[acc_rows, :] = prod + jnp.where(s > 0, dh_ref[acc_rows, :], 0.0)

        @pl.when(jnp.logical_and(i == 0, s == 0))
        def _():
            dgn_ref[...] = jnp.zeros_like(dgn_ref)

        @pl.when(s == ns - 1)
        def _():
            dx, dgs = _rms_bwd_rows(x_ref[...], g_ref[...], dh_ref[pl.ds(base, tm), :])
            dx = dx + dxo_ref[...]
            dx_ref[...] = dx
            dxob_ref[...] = dx.astype(BF16)
            dgn_ref[...] += jnp.sum(dgs, axis=0, keepdims=True)

    last_only = lambda s, i: (jnp.where(s == ns - 1, i, 0), 0)
    row_last = pl.BlockSpec((tm, d), last_only)
    row = pl.BlockSpec((tm, d), lambda s, i: (i, 0))
    vec = pl.BlockSpec((1, d), lambda s, i: (0, 0))
    wrow = pl.BlockSpec((1, f, d), lambda s, i: (s, 0, 0))
    hid = pl.BlockSpec((1, tm, f), lambda s, i: (s, i, 0))
    hid_sh = jax.ShapeDtypeStruct((ns, t, f), BF16)
    return pl.pallas_call(
        body,
        out_shape=(jax.ShapeDtypeStruct((t, d), F32), jax.ShapeDtypeStruct((t, d), BF16),
                   jax.ShapeDtypeStruct((1, d), F32), hid_sh, hid_sh, hid_sh),
        grid=(ns, t // tm), in_specs=[row_last, row, row_last, vec, hid, hid, wrow, wrow, wrow],
        out_specs=(row_last, row_last, vec, hid, hid, hid), scratch_shapes=[pltpu.VMEM((t, d), F32)],
        compiler_params=_cparams(("arbitrary", "arbitrary")), name=name)(dxo, dxo_b, x, g, gate, up, wg, wu, wd)


def _ffn_bwd_w_call(h, dxo_b, dgate, dup, act, name, after=None):
    t, d = h.shape
    ns, _, f = dgate.shape
    tm = _tile(t, FFN_W_ROWS)
    nm = t // tm

    def body(h_ref, dxb_ref, dgate_ref, dup_ref, act_ref, *rest):
        dwg_ref, dwu_ref, dwd_ref, ag_ref, au_ref, ad_ref = rest[-6:]
        i = pl.program_id(1)
        hv = h_ref[...]
        pg = lax.dot_general(dgate_ref[0], hv, TN_DIMS, preferred_element_type=F32)
        pu = lax.dot_general(dup_ref[0], hv, TN_DIMS, preferred_element_type=F32)
        pd = lax.dot_general(act_ref[0], dxb_ref[...], TN_DIMS, preferred_element_type=F32)

        ag_ref[...] = pg + jnp.where(i > 0, ag_ref[...], 0.0)
        au_ref[...] = pu + jnp.where(i > 0, au_ref[...], 0.0)
        ad_ref[...] = pd + jnp.where(i > 0, ad_ref[...], 0.0)

        @pl.when(i == nm - 1)
        def _():
            dwg_ref[0] = ag_ref[...].astype(BF16)
            dwu_ref[0] = au_ref[...].astype(BF16)
            dwd_ref[0] = (0.5 * ad_ref[...]).astype(BF16)

    row = pl.BlockSpec((tm, d), lambda s, i: (i, 0))
    hid = pl.BlockSpec((1, tm, f), lambda s, i: (s, i, 0))
    wrow = pl.BlockSpec((1, f, d), lambda s, i: (s, 0, 0))
    wsh = jax.ShapeDtypeStruct((ns, f, d), BF16)
    return pl.pallas_call(
        body, out_shape=(wsh, wsh, wsh),
        grid=(ns, nm), in_specs=[row, row, hid, hid, hid] + [HBM_SPEC] * (after is not None),
        out_specs=(wrow, wrow, wrow),
        scratch_shapes=[pltpu.VMEM((f, d), F32), pltpu.VMEM((f, d), F32), pltpu.VMEM((f, d), F32)],
        compiler_params=_cparams(("parallel", "arbitrary")), name=name)(
            h, dxo_b, dgate, dup, act, *([after] if after is not None else []))


def _loss_head(x, g, tgt, name):
    t, w = x.shape
    tr = _row_tile(t)

    def body(x_ref, g_ref, t_ref, loss_ref, dx_ref, dxb_ref, dg_ref):
        xv = x_ref[...]
        gv = g_ref[...]
        r = lax.rsqrt(jnp.mean(xv * xv, axis=-1, keepdims=True) + EPS)
        nrm = xv * r
        err = nrm * gv - t_ref[...]
        dout = err * (1.0 / w)
        dn = dout * gv
        dx = r * (dn - nrm * jnp.mean(dn * nrm, axis=-1, keepdims=True))
        dx_ref[...] = dx
        dxb_ref[...] = dx.astype(BF16)

        @pl.when(pl.program_id(0) == 0)
        def _():
            dg_ref[...] = jnp.zeros_like(dg_ref)
            loss_ref[...] = jnp.zeros_like(loss_ref)

        dg_ref[...] += jnp.sum(dout * nrm, axis=0, keepdims=True)
        part = jnp.sum(jnp.sum(err * err, axis=-1, keepdims=True) * (0.5 / w), axis=0, keepdims=True)
        loss_ref[...] += jnp.broadcast_to(part, loss_ref.shape)

    row = pl.BlockSpec((tr, w), lambda i: (i, 0))
    vec = pl.BlockSpec((1, w), lambda i: (0, 0))
    return pl.pallas_call(
        body, out_shape=(jax.ShapeDtypeStruct((1, LANES), F32), jax.ShapeDtypeStruct((t, w), F32),
                         jax.ShapeDtypeStruct((t, w), BF16), jax.ShapeDtypeStruct((1, w), F32)),
        grid=(t // tr,), in_specs=[row, vec, row],
        out_specs=(pl.BlockSpec((1, LANES), lambda i: (0, 0)), row, row, vec),
        compiler_params=_cparams(("arbitrary",)), name=name)(x, g, tgt)


def _attn_bias():
    slopes = np.asarray(2.0 ** (-8.0 * (np.arange(ATTN_HEADS) + 1) / ATTN_HEADS), np.float32)
    qi = np.arange(QBLOCK)[:, None]
    kj = np.arange(3 * QBLOCK)[None, :]
    rel = np.abs(kj - QBLOCK - qi).astype(np.float32)
    tile = np.where(rel <= WINDOW, -slopes[:, None, None] * rel[None], np.float32(NEG_INF)).astype(np.float32)
    tile = tile.reshape(KV_HEADS, GQ * QBLOCK, 3 * QBLOCK)
    return jnp.asarray(np.swapaxes(tile, 1, 2))


def _attn_scores(k3, q, n, nb, bias):
    s = lax.dot_general(k3, q, NT_DIMS, preferred_element_type=F32) * (HEAD_DIM ** -0.5)
    key = lax.broadcasted_iota(jnp.int32, (3 * QBLOCK, 1), 0)
    inside = (key >= jnp.where(n == 0, QBLOCK, 0)) & (key < jnp.where(n == nb - 1, 2 * QBLOCK, 3 * QBLOCK))
    return jnp.where(inside, s + bias, NEG_INF)


Q_COL, K_COL, V_COL, U_COL = 0, ATTN_WIDTH // LANES, ATTN_WIDTH // LANES + 1, ATTN_WIDTH // LANES + 2


def _key_rows(ref, n, nb):
    prev, nxt = jnp.maximum(n - 1, 0), jnp.minimum(n + 1, nb - 1)
    blk = lambda b: ref[pl.ds(pl.multiple_of(b * QBLOCK, QBLOCK), QBLOCK), :]
    return jnp.concatenate([blk(prev), blk(n), blk(nxt)], axis=0)


def _head_tiles(x, kh, low):
    tiles = []
    for g in range(GQ):
        h = GQ * kh + g
        t128 = x[:, LANES * (h // 2):LANES * (h // 2 + 1)]
        t128 = jnp.where(low if h % 2 == 0 else jnp.logical_not(low), t128, 0.0)
        if h % 2 != kh:
            t128 = pltpu.roll(t128, HEAD_DIM, 1)
        tiles.append(t128)
    return jnp.concatenate(tiles, axis=0)


def _head_merge(per_kh, low):
    out = []
    for j in range(ATTN_HEADS // 2):
        pair = []
        for h in (2 * j, 2 * j + 1):
            kh, g = h // GQ, h % GQ
            t128 = per_kh[kh][g * QBLOCK:(g + 1) * QBLOCK, :]
            if h % 2 != kh:
                t128 = pltpu.roll(t128, HEAD_DIM, 1)
            pair.append(t128)
        out.append(jnp.where(low, pair[0], pair[1]))
    return jnp.concatenate(out, axis=1)


def _attn_fwd_proj(proj, sink_rows, bias, name):
    t = proj.shape[0]
    nb = t // QBLOCK
    rows = GQ * QBLOCK

    def body(q_ref, k_ref, v_ref, sink_ref, bias_ref, o_ref, lse_ref):
        n = pl.program_id(0)
        low = lax.broadcasted_iota(jnp.int32, (QBLOCK, LANES), 1) < HEAD_DIM
        k3 = _key_rows(k_ref, n, nb).astype(BF16)
        v3 = _key_rows(v_ref, n, nb).astype(BF16)
        q = q_ref[...]
        outs = []
        for kh in range(KV_HEADS):
            qs = _head_tiles(q, kh, low).astype(BF16)
            s = _attn_scores(k3, qs, n, nb, bias_ref[kh])
            sink = sink_ref[kh]
            mx = jnp.maximum(jnp.max(s, axis=0, keepdims=True), sink)
            p = jnp.exp(s - mx)
            den = jnp.sum(p, axis=0, keepdims=True) + jnp.exp(sink - mx)
            pn = (p * (1.0 / den)).astype(BF16)
            outs.append(lax.dot_general(pn, v3, TN_DIMS, preferred_element_type=F32))
            lse_ref[0, kh] = mx + jnp.log(den)
        o_ref[...] = _head_merge(outs, low)

    strip = lambda col: pl.BlockSpec((t, LANES), lambda n, col=col: (0, col))
    rowspec = pl.BlockSpec((KV_HEADS, 1, rows), lambda n: (0, 0, 0))
    biasspec = pl.BlockSpec((KV_HEADS, 3 * QBLOCK, rows), lambda n: (0, 0, 0))
    return pl.pallas_call(
        body, out_shape=(jax.ShapeDtypeStruct((t, ATTN_WIDTH), F32), jax.ShapeDtypeStruct((nb, KV_HEADS, 1, rows), F32)),
        grid=(nb,), in_specs=[pl.BlockSpec((QBLOCK, ATTN_WIDTH), lambda n: (n, 0)), strip(K_COL), strip(V_COL),
                              rowspec, biasspec],
        out_specs=(pl.BlockSpec((QBLOCK, ATTN_WIDTH), lambda n: (n, 0)),
                   pl.BlockSpec((1, KV_HEADS, 1, rows), lambda n: (n, 0, 0, 0))),
        compiler_params=_cparams(("parallel",)), name=name)(proj, proj, proj, sink_rows, bias)


def _attn_bwd_proj(proj, sink_rows, bias, o, lse, do, name):
    t = proj.shape[0]
    nb = t // QBLOCK
    rows = GQ * QBLOCK
    scale = HEAD_DIM ** -0.5

    def body(q_ref, k_ref, v_ref, sink_ref, bias_ref, o_ref, lse_ref, do_ref, dq_ref, dk_ref, dv_ref, ds_ref):
        n = pl.program_id(0)

        @pl.when(n == 0)
        def _():
            dk_ref[...] = jnp.zeros_like(dk_ref)
            dv_ref[...] = jnp.zeros_like(dv_ref)
            ds_ref[...] = jnp.zeros_like(ds_ref)

        low = lax.broadcasted_iota(jnp.int32, (QBLOCK, LANES), 1) < HEAD_DIM
        k3 = _key_rows(k_ref, n, nb).astype(BF16)
        v3 = _key_rows(v_ref, n, nb).astype(BF16)
        q, dov = q_ref[...], do_ref[...]
        dod = dov * o_ref[...]
        dqs = []
        dk3 = jnp.zeros((3 * QBLOCK, LANES), F32)
        dv3 = jnp.zeros((3 * QBLOCK, LANES), F32)
        ones = jnp.ones((SUBLANES, LANES), F32)
        for kh in range(KV_HEADS):
            qs = _head_tiles(q, kh, low).astype(BF16)
            dos = _head_tiles(dov, kh, low).astype(BF16)
            delta = lax.dot_general(ones, _head_tiles(dod, kh, low), NT_DIMS, preferred_element_type=F32,
                                    precision=lax.Precision.HIGHEST)[0:1, :]
            lse_kh = lse_ref[0, kh]
            s = _attn_scores(k3, qs, n, nb, bias_ref[kh])
            p = jnp.exp(s - lse_kh)
            dp = lax.dot_general(v3, dos, NT_DIMS, preferred_element_type=F32)
            dsb = (p * (dp - delta)).astype(BF16)
            dqs.append(lax.dot_general(dsb, k3, TN_DIMS, preferred_element_type=F32) * scale)
            dk3 = dk3 + jnp.dot(dsb, qs, preferred_element_type=F32) * scale
            dv3 = dv3 + jnp.dot(p.astype(BF16), dos, preferred_element_type=F32)
            ds_ref[kh] += -jnp.exp(sink_ref[kh] - lse_kh) * delta
        dq_ref[...] = _head_merge(dqs, low)
        prev, nxt = jnp.maximum(n - 1, 0), jnp.minimum(n + 1, nb - 1)
        for j, b in enumerate((prev, n, nxt)):
            blk = pl.ds(pl.multiple_of(b * QBLOCK, QBLOCK), QBLOCK)
            dk_ref[blk, :] += dk3[j * QBLOCK:(j + 1) * QBLOCK, :]
            dv_ref[blk, :] += dv3[j * QBLOCK:(j + 1) * QBLOCK, :]

    strip = lambda col: pl.BlockSpec((t, LANES), lambda n, col=col: (0, col))
    rowspec = pl.BlockSpec((KV_HEADS, 1, rows), lambda n: (0, 0, 0))
    qspec = pl.BlockSpec((QBLOCK, ATTN_WIDTH), lambda n: (n, 0))
    kv_out = pl.BlockSpec((t, LANES), lambda n: (0, 0))
    biasspec = pl.BlockSpec((KV_HEADS, 3 * QBLOCK, rows), lambda n: (0, 0, 0))
    return pl.pallas_call(
        body,
        out_shape=(jax.ShapeDtypeStruct((t, ATTN_WIDTH), F32), jax.ShapeDtypeStruct((t, LANES), F32),
                   jax.ShapeDtypeStruct((t, LANES), F32), jax.ShapeDtypeStruct((KV_HEADS, 1, rows), F32)),
        grid=(nb,),
        in_specs=[qspec, strip(K_COL), strip(V_COL), rowspec, biasspec, qspec,
                  pl.BlockSpec((1, KV_HEADS, 1, rows), lambda n: (n, 0, 0, 0)), qspec],
        out_specs=(qspec, kv_out, kv_out, rowspec),
        compiler_params=_cparams(("arbitrary",)), name=name)(proj, proj, proj, sink_rows, bias, o, lse, do)


def _scan_tables(a_re, a_im, reverse):
    pw = [(a_re, a_im)]
    for _ in range(SUBLANES - 1):
        pr, pi = pw[-1]
        pw.append((pr * a_re - pi * a_im, pr * a_im + pi * a_re))
    rows = np.arange(SUBLANES)
    tabs = []
    for d in (1, 2, 4):
        mask = (rows <= SUBLANES - 1 - d) if reverse else (rows >= d)
        m = jnp.asarray(mask, F32)[:, None]
        tabs += [m * pw[d - 1][0][None, :], m * pw[d - 1][1][None, :]]
    order = (SUBLANES - 1 - rows) if reverse else rows
    tabs += [jnp.stack([pw[j][0] for j in order]), jnp.stack([pw[j][1] for j in order])]
    tab = jnp.stack(tabs)
    return tab.reshape(8, SUBLANES, N_STRIPS, STRIP_ST).transpose(2, 0, 1, 3)


def _scan_pair_chunk(dirs):
    nblk = dirs[0]['xr'].shape[0] // SUBLANES

    @pl.when(pl.program_id(1) == 0)
    def _():
        for d in dirs:
            d['carry'][...] = jnp.zeros_like(d['carry'])

    for d in dirs:
        vb = d['v'][...].astype(BF16)
        d['xr'][...] = jnp.dot(vb, d['mir'][0], preferred_element_type=F32)
        d['xi'][...] = jnp.dot(vb, d['mii'][0], preferred_element_type=F32)
    carries = [(d['carry'][0], d['carry'][1]) for d in dirs]
    for i in range(nblk):
        for k, d in enumerate(dirs):
            rev = d['reverse']
            rows = pl.ds(((nblk - 1 - i) if rev else i) * SUBLANES, SUBLANES)
            cr, ci = carries[k]
            xr, xi = d['xr'][rows, :], d['xi'][rows, :]
            for j, s in enumerate((1, 2, 4)):
                tr_, ti_ = d['tab'][0, 2 * j], d['tab'][0, 2 * j + 1]
                sh = (SUBLANES - s) if rev else s
                sr, si = pltpu.roll(xr, sh, 0), pltpu.roll(xi, sh, 0)
                xr, xi = xr + tr_ * sr - ti_ * si, xi + tr_ * si + ti_ * sr
            pr, pi = d['tab'][0, 6], d['tab'][0, 7]
            xr, xi = xr + pr * cr - pi * ci, xi + pr * ci + pi * cr
            d['xr'][rows, :] = xr
            d['xi'][rows, :] = xi
            edge = 0 if rev else SUBLANES - 1
            carries[k] = (jnp.broadcast_to(xr[edge:edge + 1, :], xr.shape),
                          jnp.broadcast_to(xi[edge:edge + 1, :], xi.shape))
    for k, d in enumerate(dirs):
        d['carry'][0], d['carry'][1] = carries[k]
        d['y'][...] = (jnp.dot(d['xr'][...].astype(BF16), d['mor'][0], preferred_element_type=F32)
                       + jnp.dot(d['xi'][...].astype(BF16), d['moi'][0], preferred_element_type=F32))


def _scan_pair(v, ops_f, ops_b, name):
    t = v.shape[0]
    tc = _tile(t, SCAN_ROWS)
    nc = t // tc

    def body(vf_ref, vb_ref, *refs):
        ops = refs[:10]
        outs = refs[10:16]
        carries = refs[16:18]
        dirs = []
        for k, (v_ref, rev) in enumerate(((vf_ref, False), (vb_ref, True))):
            mir, mii, tab, mor, moi = ops[5 * k:5 * k + 5]
            y, xr, xi = outs[3 * k:3 * k + 3]
            dirs.append(dict(v=v_ref, mir=mir, mii=mii, tab=tab, mor=mor, moi=moi, y=y, xr=xr, xi=xi,
                             carry=carries[k], reverse=rev))
        _scan_pair_chunk(dirs)

    col0 = v.shape[1] // STRIP_IN - N_STRIPS
    fmap = lambda s, c: (c, s)
    bmap = lambda s, c: (nc - 1 - c, s)
    smap3 = lambda s, c: (s, 0, 0)
    m_in = pl.BlockSpec((1, STRIP_IN, STRIP_ST), smap3)
    m_out = pl.BlockSpec((1, STRIP_ST, STRIP_IN), smap3)
    tabspec = pl.BlockSpec((1, 8, SUBLANES, STRIP_ST), lambda s, c: (s, 0, 0, 0))
    opspecs = [m_in, m_in, tabspec, m_out, m_out]
    y_sh = jax.ShapeDtypeStruct((t, SSM_WIDTH), F32)
    x_sh = jax.ShapeDtypeStruct((t, N_STRIPS * STRIP_ST), F32)
    outspecs = lambda m: [pl.BlockSpec((tc, STRIP_IN), m), pl.BlockSpec((tc, STRIP_ST), m),
                          pl.BlockSpec((tc, STRIP_ST), m)]
    res = pl.pallas_call(
        body, out_shape=[y_sh, x_sh, x_sh] * 2, grid=(N_STRIPS, nc),
        in_specs=[pl.BlockSpec((tc, STRIP_IN), lambda s, c: (c, s + col0)),
                  pl.BlockSpec((tc, STRIP_IN), lambda s, c: (nc - 1 - c, s + col0))] + opspecs * 2,
        out_specs=outspecs(fmap) + outspecs(bmap),
        scratch_shapes=[pltpu.VMEM((2, SUBLANES, STRIP_ST), F32)] * 2,
        compiler_params=_cparams(("parallel", "arbitrary")), name=name)(v, v, *ops_f, *ops_b)
    return tuple(res[:3]), tuple(res[3:])


def _scan_adjoint_pair(dy, u, states, adj_ops, name):
    t = dy.shape[0]
    tc = _tile(t, SCAN_ROWS)
    nc = t // tc
    hb = tc // SUBLANES
    n_out = 6

    def body(*refs):
        c = pl.program_id(1)
        dirs = []
        for k in range(2):
            dy_ref, mir, mii, tab, mor, moi, u_ref, xr_ref, xi_ref, hr_ref, hi_ref = refs[11 * k:11 * k + 11]
            outs = refs[22 + n_out * k:22 + n_out * (k + 1)]
            lr_ref, li_ref, carry = refs[22 + 2 * n_out + 3 * k:22 + 2 * n_out + 3 * k + 3]
            dirs.append(dict(v=dy_ref, mir=mir, mii=mii, tab=tab, mor=mor, moi=moi, y=outs[0], xr=lr_ref, xi=li_ref,
                             carry=carry, reverse=(k == 0), u=u_ref, fx=(xr_ref, xi_ref), halo=(hr_ref, hi_ref),
                             acc=outs[1:]))

        @pl.when(c == 0)
        def _():
            for d in dirs:
                for r in d['acc']:
                    r[...] = jnp.zeros_like(r)

        _scan_pair_chunk(dirs)
        for d in dirs:
            fwd_reverse = not d['reverse']
            rc = (nc - 1 - c) if d['reverse'] else c
            dmir_ref, dmii_ref, dmor_ref, dmoi_ref, da_ref = d['acc']
            xrv, xiv, lrv, liv = d['fx'][0][...], d['fx'][1][...], d['xr'][...], d['xi'][...]
            hr_ref, hi_ref = d['halo']
            row = lax.broadcasted_iota(jnp.int32, xrv.shape, 0)
            if fwd_reverse:
                live = (rc < nc - 1).astype(F32)
                edge_r, edge_i = hr_ref[0:1, :] * live, hi_ref[0:1, :] * live
                xpr = jnp.where(row == tc - 1, edge_r, pltpu.roll(xrv, tc - 1, 0))
                xpi = jnp.where(row == tc - 1, edge_i, pltpu.roll(xiv, tc - 1, 0))
            else:
                live = (rc > 0).astype(F32)
                edge_r, edge_i = hr_ref[SUBLANES - 1:SUBLANES, :] * live, hi_ref[SUBLANES - 1:SUBLANES, :] * live
                xpr = jnp.where(row == 0, edge_r, pltpu.roll(xrv, 1, 0))
                xpi = jnp.where(row == 0, edge_i, pltpu.roll(xiv, 1, 0))
            da_ref[0, 0:1, :] += jnp.sum(xpr * lrv + xpi * liv, axis=0, keepdims=True)
            da_ref[0, 1:2, :] += jnp.sum(xpr * liv - xpi * lrv, axis=0, keepdims=True)
            ub, dyb = d['u'][...].astype(BF16), d['v'][...].astype(BF16)
            dmir_ref[0] += lax.dot_general(ub, lrv.astype(BF16), TN_DIMS, preferred_element_type=F32)
            dmii_ref[0] += lax.dot_general(ub, liv.astype(BF16), TN_DIMS, preferred_element_type=F32)
            dmor_ref[0] += lax.dot_general(xrv.astype(BF16), dyb, TN_DIMS, preferred_element_type=F32)
            dmoi_ref[0] += lax.dot_general(xiv.astype(BF16), dyb, TN_DIMS, preferred_element_type=F32)

    col0 = u.shape[1] // STRIP_IN - N_STRIPS
    smap3 = lambda s, c: (s, 0, 0)
    m_in = pl.BlockSpec((1, STRIP_IN, STRIP_ST), smap3)
    m_out = pl.BlockSpec((1, STRIP_ST, STRIP_IN), smap3)
    tabspec = pl.BlockSpec((1, 8, SUBLANES, STRIP_ST), lambda s, c: (s, 0, 0, 0))
    in_specs, out_specs, args = [], [], []
    for k in range(2):
        reverse = k == 0
        rowblk = (lambda c: nc - 1 - c) if reverse else (lambda c: c)
        tmap = lambda s, c, rowblk=rowblk: (rowblk(c), s)
        umap = lambda s, c, rowblk=rowblk: (rowblk(c), s + col0)
        if not reverse:
            hmap = lambda s, c, rowblk=rowblk: (jnp.minimum((rowblk(c) + 1) * hb, t // SUBLANES - 1), s)
        else:
            hmap = lambda s, c, rowblk=rowblk: (jnp.maximum(rowblk(c) * hb - 1, 0), s)
        narrow = pl.BlockSpec((tc, STRIP_IN), tmap)
        wide = pl.BlockSpec((tc, STRIP_ST), tmap)
        halo = pl.BlockSpec((SUBLANES, STRIP_ST), hmap)
        in_specs += [narrow, m_in, m_in, tabspec, m_out, m_out, pl.BlockSpec((tc, STRIP_IN), umap), wide, wide,
                     halo, halo]
        out_specs += [narrow, m_in, m_in, m_out, m_out, pl.BlockSpec((1, SUBLANES, STRIP_ST), smap3)]
        xr, xi = states[k]
        args += [dy, *adj_ops[k], u, xr, xi, xr, xi]
    out_shape = [jax.ShapeDtypeStruct((t, SSM_WIDTH), F32),
                 jax.ShapeDtypeStruct((N_STRIPS, STRIP_IN, STRIP_ST), F32),
                 jax.ShapeDtypeStruct((N_STRIPS, STRIP_IN, STRIP_ST), F32),
                 jax.ShapeDtypeStruct((N_STRIPS, STRIP_ST, STRIP_IN), F32),
                 jax.ShapeDtypeStruct((N_STRIPS, STRIP_ST, STRIP_IN), F32),
                 jax.ShapeDtypeStruct((N_STRIPS, SUBLANES, STRIP_ST), F32)] * 2
    res = pl.pallas_call(
        body, out_shape=out_shape, grid=(N_STRIPS, nc), in_specs=in_specs, out_specs=out_specs,
        scratch_shapes=[pltpu.VMEM((tc, STRIP_ST), F32), pltpu.VMEM((tc, STRIP_ST), F32),
                        pltpu.VMEM((2, SUBLANES, STRIP_ST), F32)] * 2,
        compiler_params=_cparams(("parallel", "arbitrary")), name=name)(*args)
    return tuple(res[:n_out]), tuple(res[n_out:])


def _ssm_prep(lam_re, lam_im, log_dt, bt_re, bt_im, c_re, c_im):
    lr = jnp.minimum(lam_re, LAMBDA_RE_MAX)
    li = lam_im
    dt = jnp.exp(log_dt)[:, None]
    mag = jnp.exp(lr * dt)
    a_re = mag * jnp.cos(li * dt)
    a_im = mag * jnp.sin(li * dt)
    den = lr * lr + li * li
    coef_re = ((a_re - 1.0) * lr + a_im * li) / den
    coef_im = (a_im * lr - (a_re - 1.0) * li) / den
    bb_re = coef_re[:, None, :] * bt_re - coef_im[:, None, :] * bt_im
    bb_im = coef_re[:, None, :] * bt_im + coef_im[:, None, :] * bt_re
    eye = jnp.eye(SSM_GROUPS // N_STRIPS, dtype=F32)

    def strips(m):
        g, a, b = m.shape
        m4 = m.reshape(N_STRIPS, g // N_STRIPS, a, b)
        return jnp.einsum('sgab,gk->sgakb', m4, eye).reshape(N_STRIPS, g // N_STRIPS * a, g // N_STRIPS * b)

    mi_re = strips(bb_re)
    mi_im = strips(bb_im)
    mo_re = strips(jnp.swapaxes(c_re, 1, 2))
    mo_im = strips(-jnp.swapaxes(c_im, 1, 2))
    return a_re.reshape(-1), a_im.reshape(-1), mi_re, mi_im, mo_re, mo_im


def _gelu(x):
    c = math.sqrt(2.0 / math.pi)
    return 0.5 * x * (1.0 + jnp.tanh(c * (x + 0.044715 * x * x * x)))


def _gelu_grad(x):
    c = math.sqrt(2.0 / math.pi)
    th = jnp.tanh(c * (x + 0.044715 * x * x * x))
    return 0.5 * (1.0 + th) + 0.5 * x * (1.0 - th * th) * c * (1.0 + 3.0 * 0.044715 * x * x)


def _last_cols_specs(u, w, tr):
    half = w // 2
    first = (u.shape[1] - w) // half
    assert first * half == u.shape[1] - w
    return [pl.BlockSpec((tr, half), lambda i, k=k: (i, first + k)) for k in range(2)]


def _ssm_post_fwd(u, yf, yb, d, wglu, bglu, name):
    t, w = yf.shape
    tr = _row_tile(t)

    def body(ua_ref, ub_ref, yf_ref, yb_ref, d_ref, w_ref, b_ref, s_ref, y0_ref, z_ref):
        uv = jnp.concatenate([ua_ref[...], ub_ref[...]], axis=1)
        y0 = d_ref[...] * uv + yf_ref[...] + yb_ref[...]
        yg = _gelu(y0)
        z = jnp.dot(yg.astype(BF16), w_ref[...], preferred_element_type=F32) + b_ref[...]
        s_ref[...] = yg * _sigmoid(z)
        y0_ref[...] = y0
        z_ref[...] = z

    row = pl.BlockSpec((tr, w), lambda i: (i, 0))
    vec = pl.BlockSpec((1, w), lambda i: (0, 0))
    mat = pl.BlockSpec((w, w), lambda i: (0, 0))
    sh = jax.ShapeDtypeStruct((t, w), F32)
    return pl.pallas_call(body, out_shape=(sh, sh, sh), grid=(t // tr,),
                          in_specs=[*_last_cols_specs(u, w, tr), row, row, vec, mat, vec], out_specs=(row, row, row),
                          compiler_params=_cparams(("parallel",)), name=name)(u, u, yf, yb, d, wglu, bglu)


def _ssm_post_bwd(ds, y0, z, u, d, wglu, name):
    t, w = ds.shape
    tr = _row_tile(t)

    def body(ds_ref, y0_ref, z_ref, ua_ref, ub_ref, d_ref, w_ref, dy0_ref, dw_ref, db_ref, dd_ref):
        @pl.when(pl.program_id(0) == 0)
        def _():
            dw_ref[...] = jnp.zeros_like(dw_ref)
            db_ref[...] = jnp.zeros_like(db_ref)
            dd_ref[...] = jnp.zeros_like(dd_ref)

        y0 = y0_ref[...]
        yg = _gelu(y0)
        sg = _sigmoid(z_ref[...])
        dsv = ds_ref[...]
        dz = dsv * yg * sg * (1.0 - sg)
        dzb = dz.astype(BF16)
        dyg = dsv * sg + lax.dot_general(dzb, w_ref[...], (((1,), (1,)), ((), ())), preferred_element_type=F32)
        dy0 = dyg * _gelu_grad(y0)
        dy0_ref[...] = dy0
        dw_ref[...] += lax.dot_general(yg.astype(BF16), dzb, (((0,), (0,)), ((), ())), preferred_element_type=F32)
        db_ref[...] += jnp.sum(dz, axis=0, keepdims=True)
        uv = jnp.concatenate([ua_ref[...], ub_ref[...]], axis=1)
        dd_ref[...] += jnp.sum(dy0 * uv, axis=0, keepdims=True)

    row = pl.BlockSpec((tr, w), lambda i: (i, 0))
    vec = pl.BlockSpec((1, w), lambda i: (0, 0))
    mat = pl.BlockSpec((w, w), lambda i: (0, 0))
    return pl.pallas_call(
        body, out_shape=(jax.ShapeDtypeStruct((t, w), F32), jax.ShapeDtypeStruct((w, w), F32),
                         jax.ShapeDtypeStruct((1, w), F32), jax.ShapeDtypeStruct((1, w), F32)),
        grid=(t // tr,), in_specs=[row, row, row, *_last_cols_specs(u, w, tr), vec, mat],
        out_specs=(row, mat, vec, vec),
        compiler_params=_cparams(("arbitrary",)), name=name)(ds, y0, z, u, u, d, wglu)


def _du_combine(dy0, d, du_f, du_b, name):
    t, w = dy0.shape
    tr = _row_tile(t)

    def body(dy_ref, d_ref, a_ref, b_ref, o_ref):
        o_ref[...] = d_ref[...] * dy_ref[...] + a_ref[...] + b_ref[...]

    row = pl.BlockSpec((tr, w), lambda i: (i, 0))
    vec = pl.BlockSpec((1, w), lambda i: (0, 0))
    return pl.pallas_call(body, out_shape=jax.ShapeDtypeStruct((t, w), F32), grid=(t // tr,),
                          in_specs=[row, vec, row, row], out_specs=row, compiler_params=_cparams(("parallel",)),
                          name=name)(dy0, d, du_f, du_b)


def _ffn_fwd(x, g, wg, wu, wd, tag):
    xo, h, gate, up = _ffn_fwd_call(x, g, wg, wu, wd, f"{tag}_fwd")
    return xo, (h, gate, up)


def _ffn_bwd(dxo, dxo_b, x, g, wg, wu, wd, saved, tag, between=None):
    h, gate, up = saved
    dx, dx_b, dg, dgate, dup, act = _ffn_bwd_x_call(dxo, dxo_b, x, g, gate, up, wg, wu, wd, f"{tag}_bwd_x")
    after = between(dg) if between is not None else None
    dwg, dwu, dwd = _ffn_bwd_w_call(h, dxo_b, dgate, dup, act, f"{tag}_bwd_w", after=after)
    return dx, dx_b, dg, dwg, dwu, dwd


def _local_step(x, tgt, w, get_weights, put_grads, reduce_wide, put_narrow):
    t = x.shape[0]
    row = lambda a: a.reshape(1, -1)
    grads = {}

    w = dict(w)

    ssm_names = ['ssm_lambda_re', 'ssm_lambda_im', 'ssm_log_dt', 'ssm_b_re', 'ssm_b_im', 'ssm_c_re', 'ssm_c_im']
    tr3 = lambda m: jnp.swapaxes(m, 1, 2)
    fwd_ops, adj_ops, vjps = [], [], []
    for direction in range(2):
        rev = direction == 1
        prep, vjp = jax.vjp(_ssm_prep, *[w[n][direction] for n in ssm_names])
        a_re, a_im = prep[0], prep[1]
        mi_re, mi_im, mo_re, mo_im = (m.astype(BF16) for m in prep[2:])
        fwd_ops.append((mi_re, mi_im, _scan_tables(a_re, a_im, rev), mo_re, mo_im))
        adj_ops.append((tr3(mo_re), tr3(mo_im), _scan_tables(a_re, -a_im, not rev), tr3(mi_re), tr3(mi_im)))
        vjps.append(vjp)
    sink_rows = jnp.repeat(w['attn_sinks'].reshape(KV_HEADS, GQ), QBLOCK, axis=1)[:, None, :]
    bias = _attn_bias()
    prepared = sum(jnp.sum(op[:1, :1].astype(F32)) for ops in fwd_ops + adj_ops for op in ops) + sink_rows[0, 0, 0]

    w.update(get_weights('ffn1', prepared.reshape(1, 1)))
    x1, ffn1_saved = _ffn_fwd(x, w['norm_ffn1'], w['ffn1_w_gate'], w['ffn1_w_up'], w['ffn1_w_down'], "ffn1")
    w.update(get_weights('mix', x1))

    proj, h2 = _norm_mm([x1], [w['norm_mix']], w['w_in'], tb=True, res=None, name="in_proj")
    u = proj

    attn, lse = _attn_fwd_proj(proj, sink_rows, bias, "attn_fwd")

    (y_f, *states_f), (y_b, *states_b) = _scan_pair(u, fwd_ops[0], fwd_ops[1], "s5_fwd")
    ys, states = [y_f, y_b], [states_f, states_b]
    d_row = row(w['ssm_d'])
    s, y0, z = _ssm_post_fwd(u, ys[0], ys[1], d_row, w['ssm_glu_w'], row(w['ssm_glu_b']), "ssm_post")

    x2, mixed = _norm_mm([attn, s], [row(w['attn_out_norm']), row(w['ssm_out_norm'])], w['w_out'], tb=False,
                         res=x1, name="out_proj")

    w.update(get_weights('ffn2', x2))
    x3, ffn2_saved = _ffn_fwd(x2, w['norm_ffn2'], w['ffn2_w_gate'], w['ffn2_w_up'], w['ffn2_w_down'], "ffn2")

    loss, dx3, dx3_b, dgf = _loss_head(x3, row(w['final_norm']), tgt, "loss_head")
    grads['final_norm'] = dgf.reshape(w['final_norm'].shape)

    dx2, dx2_b, dg, dwg, dwu, dwd = _ffn_bwd(dx3, dx3_b, x2, w['norm_ffn2'], w['ffn2_w_gate'], w['ffn2_w_up'],
                                             w['ffn2_w_down'], ffn2_saved, "ffn2")
    grads['norm_ffn2'] = dg
    sent = put_grads('ffn2', dict(ffn2_w_gate=dwg, ffn2_w_up=dwu, ffn2_w_down=dwd))

    (dattn, _, dga), (ds, _, dgs) = _mm_rms_bwd(
        dx2_b, w['w_out'], tb=True, xs=[attn, s], gs=[row(w['attn_out_norm']), row(w['ssm_out_norm'])],
        dres=None, after=sent, name="out_proj_dx")
    dw_out = _mm(mixed, dx2_b, ta=True, out_dtype=BF16, name="out_proj_dw")[0]
    grads.update(attn_out_norm=dga, ssm_out_norm=dgs)

    dy0, dwglu, dbglu, dd = _ssm_post_bwd(ds, y0, z, u, d_row, w['ssm_glu_w'], "ssm_post_bwd")
    grads['ssm_glu_b'] = dbglu
    grads['ssm_d'] = dd.reshape(w['ssm_d'].shape)
    dparams, du_dirs = [], []
    for direction, res in enumerate(_scan_adjoint_pair(dy0, u, states, adj_ops, "s5_adj")):
        du_dir, dmir, dmii, dmor, dmoi, da = res
        du_dirs.append(du_dir)
        da_re = da[:, 0, :].reshape(-1)
        da_im = da[:, 1, :].reshape(-1)
        dparams.append(vjps[direction]((da_re, da_im, dmir, dmii, dmor, dmoi)))
    du = _du_combine(dy0, d_row, du_dirs[0], du_dirs[1], "ssm_du")
    for i, n in enumerate(ssm_names):
        grads[n] = jnp.stack([dparams[0][i], dparams[1][i]])
    wide_sum = reduce_wide(grads)

    dq, dk, dv, dsink = _attn_bwd_proj(proj, sink_rows, bias, attn, lse, dattn, "attn_bwd")
    grads['attn_sinks'] = jnp.sum(dsink.reshape(ATTN_HEADS, QBLOCK), axis=-1).reshape(w['attn_sinks'].shape)
    dproj = jnp.concatenate([dq, dk, dv, du], axis=-1).astype(BF16)

    dw_in = _mm(dproj, h2, ta=True, out_dtype=BF16, after=wide_sum, name="in_proj_dw")[0]
    sent = put_grads('mix', dict(w_in=dw_in, ssm_glu_w=dwglu, w_out=dw_out))
    ((dx1, dx1_b, dgm),) = _mm_rms_bwd(dproj, w['w_in'], tb=False, xs=[x1], gs=[w['norm_mix']], dres=dx2,
                                       after=sent, name="in_proj_dx")
    grads['norm_mix'] = dgm

    def all_small_known(dg):
        grads['norm_ffn1'] = dg
        return put_narrow(grads, loss)

    dx0, _, dg, dwg, dwu, dwd = _ffn_bwd(dx1, dx1_b, x, w['norm_ffn1'], w['ffn1_w_gate'], w['ffn1_w_up'],
                                         w['ffn1_w_down'], ffn1_saved, "ffn1", between=all_small_known)
    put_grads('ffn1', dict(ffn1_w_gate=dwg, ffn1_w_up=dwu, ffn1_w_down=dwd))
    return dx0, grads


HBM_SPEC = pl.BlockSpec(memory_space=pl.ANY)


def _chip_peers(x, y):
    return [(1 - x, y), (x, 1 - y), (1 - x, 1 - y)]


HBM_ONLY = pl.BlockSpec(memory_space=pltpu.HBM)
SEM_SPEC = pl.BlockSpec(memory_space=pltpu.SEMAPHORE)
EFFECT = pltpu.SideEffectType.DATAFLOW_SIDE_EFFECTING


def _place_own(srcs, slot, name):
    na = len(srcs)
    r, c = srcs[0].shape
    tr = r // 2

    def body(slot_ref, *refs):
        for a in range(na):
            refs[na + a][0] = refs[a][...]

    return pl.pallas_call(
        body, out_shape=[jax.ShapeDtypeStruct((N_CHIPS, r, c), s.dtype) for s in srcs],
        grid_spec=pltpu.PrefetchScalarGridSpec(
            num_scalar_prefetch=1, grid=(2,), in_specs=[pl.BlockSpec((tr, c), lambda i, s: (i, 0))] * na,
            out_specs=[pl.BlockSpec((1, tr, c), lambda i, s: (s[0], i, 0))] * na),
        compiler_params=_cparams(("parallel",)), name=name)(slot, *srcs)


def _chip_copies(srcs, lands, send_sems, recv_sems, scatter, landed):
    x, y, c = lax.axis_index("x"), lax.axis_index("y"), lax.axis_index("c")
    me = 2 * x + y
    out = []
    for i in range(len(srcs)):
        for j, (px, py) in enumerate(_chip_peers(x, y)):
            p = 2 * px + py
            slot = p if landed else me
            if scatter:
                src, dst = srcs[i].at[p], lands[i].at[slot]
            else:
                rows = _core_half(srcs[i].shape[0], c)
                src, dst = srcs[i].at[rows], lands[i].at[slot, rows]
            out.append(pltpu.make_async_remote_copy(src, dst, send_sems.at[3 * i + j], recv_sems.at[3 * i + j],
                                                    device_id=(px, py, c), device_id_type=MESH))
    return out


def _core_half(nrows, c):
    half = nrows // 2
    return pl.ds(pl.multiple_of(c * half, 16), half)


FORWARD_BARRIER_ID, HALF_SWAP_BARRIER_ID, HALF_FORWARD_BARRIER_ID = 0, 1, 2
GATHER_BARRIER_ID, SCATTER_BARRIER_ID = 3, 6


def _sibling_handshake():
    x, y, c = lax.axis_index("x"), lax.axis_index("y"), lax.axis_index("c")
    barrier = pltpu.get_barrier_semaphore()
    pl.semaphore_signal(barrier, inc=1, device_id=(x, y, 1 - c), device_id_type=MESH)
    pl.semaphore_wait(barrier, 1)


def _sibling_forward(lands, name):
    n = len(lands)

    def body(*refs):
        bufs = refs[n:2 * n]
        send_sems, recv_sems = refs[2 * n:]
        _sibling_handshake()
        x, y, c = lax.axis_index("x"), lax.axis_index("y"), lax.axis_index("c")
        mine = [_core_half(b.shape[1], c) for b in bufs]
        theirs = [_core_half(b.shape[1], 1 - c) for b in bufs]
        chips = [2 * px + py for px, py in _chip_peers(x, y)]
        cps = [pltpu.make_async_remote_copy(bufs[i].at[p, mine[i]], bufs[i].at[p, mine[i]], send_sems.at[3 * i + j],
                                            recv_sems.at[3 * i + j], device_id=(x, y, 1 - c), device_id_type=MESH)
               for i in range(n) for j, p in enumerate(chips)]
        for cp in cps:
            cp.start()
        for i in range(n):
            for j, p in enumerate(chips):
                pltpu.make_async_remote_copy(bufs[i].at[p, mine[i]], bufs[i].at[p, theirs[i]], send_sems.at[3 * i + j],
                                             recv_sems.at[3 * i + j], device_id=(x, y, 1 - c),
                                             device_id_type=MESH).wait()

    return pl.pallas_call(
        body, out_shape=[jax.ShapeDtypeStruct(a.shape, a.dtype) for a in lands],
        in_specs=[HBM_SPEC] * n, out_specs=[HBM_SPEC] * n, input_output_aliases={k: k for k in range(n)},
        scratch_shapes=[pltpu.SemaphoreType.DMA((3 * n,)), pltpu.SemaphoreType.DMA((3 * n,))],
        compiler_params=pltpu.CompilerParams(collective_id=FORWARD_BARRIER_ID), name=name)(*lands)


def _chip_handshake():
    x, y, c = lax.axis_index("x"), lax.axis_index("y"), lax.axis_index("c")
    barrier = pltpu.get_barrier_semaphore()
    for px, py in _chip_peers(x, y):
        pl.semaphore_signal(barrier, inc=1, device_id=(px, py, c), device_id_type=MESH)
    pl.semaphore_wait(barrier, 3)


def _exchange_start(groups, scatter, name, barrier_id, after=None):
    sizes = [len(srcs) for srcs, _ in groups]
    flat_src = [a for srcs, _ in groups for a in srcs]
    flat_land = [a for _, lands in groups for a in lands]
    n = len(flat_src)
    ng = len(groups)

    def body(*refs):
        src_refs, land_refs = refs[:n], refs[n:2 * n]
        n_in = 2 * n + (after is not None)
        sems = refs[n_in:n_in + 2 * ng]
        token_ref = refs[-1]
        _chip_handshake()
        off = 0
        for gi, sz in enumerate(sizes):
            for cp in _chip_copies(src_refs[off:off + sz], land_refs[off:off + sz], sems[2 * gi], sems[2 * gi + 1],
                                   scatter, landed=False):
                cp.start()
            off += sz
        token_ref[...] = jnp.zeros_like(token_ref)

    sem_shapes = []
    for sz in sizes:
        sem_shapes += [pltpu.SemaphoreType.DMA((3 * sz,)), pltpu.SemaphoreType.DMA((3 * sz,))]
    hbm = lambda a: pltpu.HBM(a.shape, a.dtype)
    res = pl.pallas_call(
        body, name=name,
        out_shape=(tuple(sem_shapes) + tuple(hbm(a) for a in flat_src) + tuple(hbm(a) for a in flat_land)
                   + (jax.ShapeDtypeStruct((SUBLANES, LANES), F32),)),
        in_specs=[HBM_ONLY] * (2 * n) + [HBM_SPEC] * (after is not None),
        out_specs=tuple([SEM_SPEC] * (2 * ng) + [HBM_ONLY] * (2 * n) + [pl.BlockSpec(memory_space=pltpu.VMEM)]),
        input_output_aliases={k: 2 * ng + k for k in range(2 * n)},
        compiler_params=pltpu.CompilerParams(has_side_effects=EFFECT, collective_id=barrier_id),
    )(*[pltpu.with_memory_space_constraint(a, pltpu.HBM) for a in flat_src + flat_land],
      *([after] if after is not None else []))
    sems, thru_src, thru_land = res[:2 * ng], res[2 * ng:2 * ng + n], res[2 * ng + n:2 * ng + 2 * n]
    out, off = [], 0
    for gi, sz in enumerate(sizes):
        out.append((sems[2 * gi], sems[2 * gi + 1], list(thru_src[off:off + sz]), list(thru_land[off:off + sz])))
        off += sz
    return out, res[-1]


def _exchange_wait(started, after, scatter, name):
    send_sems, recv_sems, srcs, lands = started
    n = len(srcs)

    def body(*refs):
        src_refs, land_refs = refs[:n], refs[n:2 * n]
        send_ref, recv_ref = refs[2 * n], refs[2 * n + 1]
        for cp in _chip_copies(src_refs, land_refs, send_ref, recv_ref, scatter, landed=True):
            cp.wait_send()
            cp.wait_recv()

    hbm = lambda a: pltpu.HBM(a.shape, a.dtype)
    res = pl.pallas_call(
        body, name=name, out_shape=tuple(hbm(a) for a in srcs) + tuple(hbm(a) for a in lands),
        in_specs=[HBM_ONLY] * (2 * n) + [SEM_SPEC, SEM_SPEC, HBM_SPEC], out_specs=tuple([HBM_ONLY] * (2 * n)),
        input_output_aliases={k: k for k in range(2 * n)},
        compiler_params=pltpu.CompilerParams(has_side_effects=EFFECT),
    )(*srcs, *lands, send_sems, recv_sems, after)
    return list(res[:n]), list(res[n:])


def _half_swap(parts, name):
    n = len(parts)

    def body(*refs):
        ins, outs = refs[:n], refs[n:2 * n]
        send_sems, recv_sems = refs[2 * n:]
        _sibling_handshake()
        x, y, c = lax.axis_index("x"), lax.axis_index("y"), lax.axis_index("c")
        cps = [pltpu.make_async_remote_copy(ins[i].at[k, _core_half(ins[i].shape[1], 1 - c)], outs[i].at[k],
                                            send_sems.at[N_CHIPS * i + k], recv_sems.at[N_CHIPS * i + k],
                                            device_id=(x, y, 1 - c), device_id_type=MESH)
               for i in range(n) for k in range(N_CHIPS)]
        for cp in cps:
            cp.start()
        for cp in cps:
            cp.wait()

    return pl.pallas_call(
        body, out_shape=[jax.ShapeDtypeStruct((N_CHIPS, p.shape[1] // 2, p.shape[2]), p.dtype) for p in parts],
        in_specs=[HBM_SPEC] * n, out_specs=[HBM_SPEC] * n,
        scratch_shapes=[pltpu.SemaphoreType.DMA((N_CHIPS * n,)), pltpu.SemaphoreType.DMA((N_CHIPS * n,))],
        compiler_params=pltpu.CompilerParams(collective_id=HALF_SWAP_BARRIER_ID), name=name)(*parts)


def _half_add(parts, sib, slots, name):
    na = len(parts)
    _, r, c = parts[0].shape
    hr = r // 2
    tr = _row_tile(hr, 512)
    nt = hr // tr

    def body(slot_ref, *refs):
        for a in range(na):
            refs[2 * na + a][...] = (refs[2 * a][...].astype(F32) + refs[2 * a + 1][...].astype(F32)).astype(BF16)

    mine = pl.BlockSpec((1, tr, c), lambda k, i, s: (k, i + s[4] * nt, 0))
    half = pl.BlockSpec((1, tr, c), lambda k, i, s: (k, i, 0))
    args = [a for p, sb in zip(parts, sib) for a in (p, sb)]
    return pl.pallas_call(
        body, out_shape=[jax.ShapeDtypeStruct((N_CHIPS, hr, c), BF16)] * na,
        grid_spec=pltpu.PrefetchScalarGridSpec(
            num_scalar_prefetch=1, grid=(N_CHIPS, nt), in_specs=[mine, half] * na, out_specs=[half] * na),
        compiler_params=_cparams(("parallel", "parallel")), name=name)(slots, *args)


def _half_forward(arrs, name):
    n = len(arrs)

    def body(*refs):
        bufs = refs[n:2 * n]
        send_sems, recv_sems = refs[2 * n:]
        _sibling_handshake()
        x, y, c = lax.axis_index("x"), lax.axis_index("y"), lax.axis_index("c")
        cps = [pltpu.make_async_remote_copy(b.at[_core_half(b.shape[0], c)], b.at[_core_half(b.shape[0], c)],
                                            send_sems.at[i], recv_sems.at[i], device_id=(x, y, 1 - c),
                                            device_id_type=MESH) for i, b in enumerate(bufs)]
        for cp in cps:
            cp.start()
        for i, b in enumerate(bufs):
            pltpu.make_async_remote_copy(b.at[_core_half(b.shape[0], c)], b.at[_core_half(b.shape[0], 1 - c)],
                                         send_sems.at[i], recv_sems.at[i], device_id=(x, y, 1 - c),
                                         device_id_type=MESH).wait()

    return pl.pallas_call(
        body, out_shape=[jax.ShapeDtypeStruct(a.shape, a.dtype) for a in arrs],
        in_specs=[HBM_SPEC] * n, out_specs=[HBM_SPEC] * n, input_output_aliases={k: k for k in range(n)},
        scratch_shapes=[pltpu.SemaphoreType.DMA((n,)), pltpu.SemaphoreType.DMA((n,))],
        compiler_params=pltpu.CompilerParams(collective_id=HALF_FORWARD_BARRIER_ID), name=name)(*arrs)


def _sum_parts(parts, recv, slots, name):
    na = len(parts)
    _, r, c = parts[0].shape
    tr = _row_tile(r, 192)

    def body(slot_ref, *refs):
        for a in range(na):
            own_ref, r0_ref, r1_ref, r2_ref = refs[4 * a:4 * a + 4]
            refs[4 * na + a][...] = ((own_ref[0].astype(F32) + r0_ref[0].astype(F32))
                                     + (r1_ref[0].astype(F32) + r2_ref[0].astype(F32)))

    blk = lambda k: pl.BlockSpec((1, tr, c), lambda i, s, k=k: (s[k], i, 0))
    out_blk = pl.BlockSpec((tr, c), lambda i, s: (i + s[4] * (r // tr), 0))
    args = [a for p, rv in zip(parts, recv) for a in (p, rv, rv, rv)]
    return pl.pallas_call(
        body, out_shape=[jax.ShapeDtypeStruct((2 * r, c), F32)] * na,
        grid_spec=pltpu.PrefetchScalarGridSpec(
            num_scalar_prefetch=1, grid=(r // tr,), in_specs=[blk(0), blk(1), blk(2), blk(3)] * na,
            out_specs=[out_blk] * na),
        compiler_params=_cparams(("parallel",)), name=name)(slots, *args)


ALL_PEERS = [(fx, fy, fc) for fx in (0, 1) for fy in (0, 1) for fc in (0, 1)][1:]


def _all8_copies(srcs, lands, send_sems, recv_sems, landed):
    x, y, c = lax.axis_index("x"), lax.axis_index("y"), lax.axis_index("c")
    lin = 4 * x + 2 * y + c
    out = []
    for i, (src, land) in enumerate(zip(srcs, lands)):
        for j, (fx, fy, fc) in enumerate(ALL_PEERS):
            px, py, pc = x ^ fx, y ^ fy, c ^ fc
            slot = (4 * px + 2 * py + pc) if landed else lin
            out.append(pltpu.make_async_remote_copy(src, land.at[slot], send_sems.at[7 * i + j], recv_sems.at[7 * i + j],
                                                    device_id=(px, py, pc), device_id_type=MESH))
    return out


def _all8_start(srcs, name):
    n = len(srcs)
    lands = [lax.empty((N_DEV,) + s.shape, s.dtype) for s in srcs]

    def body(*refs):
        for cp in _all8_copies(refs[:n], refs[n:2 * n], refs[2 * n], refs[2 * n + 1], landed=False):
            cp.start()
        refs[-1][...] = jnp.zeros_like(refs[-1])

    hbm = lambda a: pltpu.HBM(a.shape, a.dtype)
    res = pl.pallas_call(
        body, name=name,
        out_shape=(pltpu.SemaphoreType.DMA((7 * n,)), pltpu.SemaphoreType.DMA((7 * n,)), *[hbm(a) for a in srcs],
                   *[hbm(a) for a in lands], jax.ShapeDtypeStruct((SUBLANES, LANES), F32)),
        in_specs=[HBM_ONLY] * (2 * n),
        out_specs=(SEM_SPEC, SEM_SPEC, *[HBM_ONLY] * (2 * n), pl.BlockSpec(memory_space=pltpu.VMEM)),
        input_output_aliases={k: 2 + k for k in range(2 * n)},
        compiler_params=pltpu.CompilerParams(has_side_effects=EFFECT),
    )(*[pltpu.with_memory_space_constraint(a, pltpu.HBM) for a in list(srcs) + lands])
    return (res[0], res[1], list(res[2:2 + n]), list(res[2 + n:2 + 2 * n])), res[-1]


def _all8_wait(started, after, name):
    send_sems, recv_sems, srcs, lands = started
    n = len(srcs)

    def body(*refs):
        for cp in _all8_copies(refs[:n], refs[n:2 * n], refs[2 * n], refs[2 * n + 1], landed=True):
            cp.wait_send()
            cp.wait_recv()

    hbm = lambda a: pltpu.HBM(a.shape, a.dtype)
    res = pl.pallas_call(
        body, name=name, out_shape=tuple(hbm(a) for a in srcs) + tuple(hbm(a) for a in lands),
        in_specs=[HBM_ONLY] * (2 * n) + [SEM_SPEC, SEM_SPEC, HBM_SPEC], out_specs=tuple([HBM_ONLY] * (2 * n)),
        input_output_aliases={k: k for k in range(2 * n)},
        compiler_params=pltpu.CompilerParams(has_side_effects=EFFECT),
    )(*srcs, *lands, send_sems, recv_sems, after)
    return list(res[:n]), list(res[n:])


def _sum8(own, land, lin, name):
    r, c = own.shape
    tr = _row_tile(r)

    def body(lin_ref, own_ref, land_ref, o_ref):
        me = lin_ref[0]
        acc = None
        for k in range(N_DEV):
            term = jnp.where(me == k, own_ref[...], land_ref[k])
            acc = term if acc is None else acc + term
        o_ref[...] = acc

    return pl.pallas_call(
        body, out_shape=jax.ShapeDtypeStruct((r, c), F32),
        grid_spec=pltpu.PrefetchScalarGridSpec(
            num_scalar_prefetch=1, grid=(r // tr,),
            in_specs=[pl.BlockSpec((tr, c), lambda i, s: (i, 0)), pl.BlockSpec((N_DEV, tr, c), lambda i, s: (0, i, 0))],
            out_specs=pl.BlockSpec((tr, c), lambda i, s: (i, 0))),
        compiler_params=_cparams(("parallel",)), name=name)(lin, own, land)


def _adamw_math(w, m, v, g):
    nm = ADAM_B1 * m + (1.0 - ADAM_B1) * g
    nv = ADAM_B2 * v + (1.0 - ADAM_B2) * (g * g)
    m_hat = nm * (1.0 / (1.0 - ADAM_B1 ** ADAM_STEP))
    v_hat = nv * (1.0 / (1.0 - ADAM_B2 ** ADAM_STEP))
    return -ADAM_LR * (m_hat / (jnp.sqrt(v_hat) + ADAM_EPS) + ADAM_WD * w), nm, nv


def _adamw(ws, ms, vs, gs, name):
    na = len(ws)
    r, c = ws[0].shape
    tr = _row_tile(r)

    def body(*refs):
        for a in range(na):
            w_ref, m_ref, v_ref, g_ref = refs[4 * a:4 * a + 4]
            go_ref, d_ref, nm_ref, nv_ref = refs[4 * na + 4 * a:4 * na + 4 * a + 4]
            g = g_ref[...]
            go_ref[...] = g
            d_ref[...], nm_ref[...], nv_ref[...] = _adamw_math(w_ref[...], m_ref[...], v_ref[...], g)

    blk = pl.BlockSpec((tr, c), lambda i: (i, 0))
    sh = jax.ShapeDtypeStruct((r, c), F32)
    args = [a for group in zip(ws, ms, vs, gs) for a in group]
    res = pl.pallas_call(body, out_shape=[sh] * (4 * na), grid=(r // tr,), in_specs=[blk] * (4 * na),
                         out_specs=[blk] * (4 * na), compiler_params=_cparams(("parallel",)), name=name)(*args)
    return [tuple(res[4 * a:4 * a + 4]) for a in range(na)]


def _adamw_small(ws, ms, vs, alls, split, name, owns=None, lin=None):
    n = len(ws)
    lead = split if split is not None else ()
    nl = len(lead)
    nslots = alls[0].shape[0]
    has_own = owns is not None
    nin = 5 if has_own else 4

    def blocks(shape):
        if split is None:
            return tuple(shape), (lambda *g: (0,) * len(shape))
        blk = (shape[0], shape[1] // lead[0], shape[2] // lead[1]) + tuple(shape[3:])
        return blk, (lambda *g: (0, g[0], g[1]) + (0,) * (len(shape) - 3))

    def body(*refs):
        if has_own:
            lin_ref, refs = refs[0], refs[1:]
        w_refs, m_refs, v_refs, a_refs = (refs[k * n:(k + 1) * n] for k in range(4))
        own_refs = refs[4 * n:5 * n] if has_own else None
        g_refs, d_refs, nm_refs, nv_refs = (refs[(nin + k) * n:(nin + 1 + k) * n] for k in range(4))
        k = pl.program_id(nl)
        for i in range(n):
            term = a_refs[i][0]
            if has_own:
                term = jnp.where(k == lin_ref[0], own_refs[i][...], term)

            @pl.when(k == 0)
            def _(i=i, term=term):
                g_refs[i][...] = term

            @pl.when(k > 0)
            def _(i=i, term=term):
                g_refs[i][...] += term

            @pl.when(k == nslots - 1)
            def _(i=i):
                d_refs[i][...], nm_refs[i][...], nv_refs[i][...] = _adamw_math(
                    w_refs[i][...], m_refs[i][...], v_refs[i][...], g_refs[i][...])

    specs, aspecs, shapes = [], [], []
    for wa in ws:
        blk, imap = blocks(wa.shape)
        specs.append(pl.BlockSpec(blk, imap))
        aspecs.append(pl.BlockSpec((1,) + blk, (lambda *g, imap=imap: (g[nl],) + imap(*g))))
        shapes.append(jax.ShapeDtypeStruct(wa.shape, F32))
    grid = tuple(lead) + (nslots,)
    sem = _cparams(("parallel",) * nl + ("arbitrary",))
    if has_own:
        res = pl.pallas_call(
            body, out_shape=shapes * 4,
            grid_spec=pltpu.PrefetchScalarGridSpec(num_scalar_prefetch=1, grid=grid,
                                                   in_specs=specs * 3 + aspecs + specs, out_specs=specs * 4),
            compiler_params=sem, name=name)(lin, *ws, *ms, *vs, *alls, *owns)
    else:
        res = pl.pallas_call(body, out_shape=shapes * 4, grid=grid, in_specs=specs * 3 + aspecs,
                             out_specs=specs * 4, compiler_params=sem, name=name)(*ws, *ms, *vs, *alls)
    return res[:n], res[n:2 * n], res[2 * n:3 * n], res[3 * n:]


def kernel(x, norm_ffn1, ffn1_w_gate, ffn1_w_up, ffn1_w_down, norm_mix, w_in, attn_sinks, ssm_lambda_re, ssm_lambda_im, ssm_log_dt, ssm_b_re, ssm_b_im, ssm_c_re, ssm_c_im, ssm_d, ssm_glu_w, ssm_glu_b, attn_out_norm, ssm_out_norm, w_out, norm_ffn2, ffn2_w_gate, ffn2_w_up, ffn2_w_down, final_norm, loss_target, m_norm_ffn1, m_ffn1_w_gate, m_ffn1_w_up, m_ffn1_w_down, m_norm_mix, m_w_in, m_attn_sinks, m_ssm_lambda_re, m_ssm_lambda_im, m_ssm_log_dt, m_ssm_b_re, m_ssm_b_im, m_ssm_c_re, m_ssm_c_im, m_ssm_d, m_ssm_glu_w, m_ssm_glu_b, m_attn_out_norm, m_ssm_out_norm, m_w_out, m_norm_ffn2, m_ffn2_w_gate, m_ffn2_w_up, m_ffn2_w_down, m_final_norm, v_norm_ffn1, v_ffn1_w_gate, v_ffn1_w_up, v_ffn1_w_down, v_norm_mix, v_w_in, v_attn_sinks, v_ssm_lambda_re, v_ssm_lambda_im, v_ssm_log_dt, v_ssm_b_re, v_ssm_b_im, v_ssm_c_re, v_ssm_c_im, v_ssm_d, v_ssm_glu_w, v_ssm_glu_b, v_attn_out_norm, v_ssm_out_norm, v_w_out, v_norm_ffn2, v_ffn2_w_gate, v_ffn2_w_up, v_ffn2_w_down, v_final_norm):
    given = dict(locals())
    wts = {n: given[n] for n in WEIGHTS}

    order = [g for g in GROUPS]
    cx, cy = lax.axis_index("x"), lax.axis_index("y")
    slots = jnp.stack([2 * cx + cy, 2 * (1 - cx) + cy, 2 * cx + 1 - cy, 2 * (1 - cx) + 1 - cy,
                       lax.axis_index("c")]).astype(jnp.int32)
    def view(a, n):
        if n in TRANSPOSED:
            return jnp.swapaxes(a[0], 0, 1)
        if n in BIG:
            return a[0]
        if n in ('ssm_b_re', 'ssm_b_im'):
            return jnp.swapaxes(a, -1, -2)
        return a.reshape(1, -1) if a.ndim == 1 else a

    def unview(a, n):
        if n in TRANSPOSED:
            return jnp.swapaxes(a, 0, 1)[None]
        if n in ('ssm_b_re', 'ssm_b_im'):
            return jnp.swapaxes(a, -1, -2)
        return a.reshape(wts[n].shape)

    started, gather_token = {}, None
    for g in order:
        shards = [view(wts[n], n).astype(BF16) for n in GROUPS[g]]
        if len({s.shape for s in shards}) == 1:
            placed = _place_own(shards, slots, f"weights_place_{g}")
        else:
            placed = [_place_own([s], slots, f"weights_place_{n}")[0] for n, s in zip(GROUPS[g], shards)]
        st, gather_token = _exchange_start([(shards, placed)], False, f"weights_start_{g}",
                                           GATHER_BARRIER_ID + list(GROUPS).index(g), after=gather_token)
        started[g] = st[0]

    def get_weights(group, after):
        if group == order[0]:
            after = after + gather_token[:1, :1]
        _, lands = _exchange_wait(started[group], after, False, f"weights_wait_{group}")
        lands = _sibling_forward(lands, f"weights_forward_{group}")
        out = dict(zip(GROUPS[group], lands))
        for n in ('w_in', 'ssm_glu_w', 'w_out'):
            if n in out:
                out[n] = out[n].reshape(-1, out[n].shape[-1])
        return out

    sent, tokens = {}, {}

    def put_grads(group, gd):
        parts = []
        for n in GROUPS[group]:
            g = gd[n]
            if g.ndim == 2:
                g = g.reshape(N_CHIPS, g.shape[0] // N_CHIPS, g.shape[1])
            parts.append(g.astype(BF16))
        sib = _half_swap(parts, f"grads_half_swap_{group}")
        same = len({p.shape for p in parts}) == 1
        batches = [list(range(len(parts)))] if same else [[i] for i in range(len(parts))]
        halves = [None] * len(parts)
        for b in batches:
            res = _half_add([parts[i] for i in b], [sib[i] for i in b], slots, f"grads_half_add_{GROUPS[group][b[0]]}")
            for i, h in zip(b, res):
                halves[i] = h
        parts = halves
        lands = [lax.empty(p.shape, p.dtype) for p in parts]
        started_g, tokens[group] = _exchange_start([(parts, lands)], True, f"grads_start_{group}",
                                                   SCATTER_BARRIER_ID + list(GROUPS).index(group))
        sent[group] = started_g[0]
        return tokens[group]

    w = {n: (wts[n][0] if wts[n].ndim > 1 else wts[n]) for n in SMALL}
    w['norm_ffn1'], w['norm_mix'], w['norm_ffn2'] = wts['norm_ffn1'], wts['norm_mix'], wts['norm_ffn2']
    w['ssm_b_re'], w['ssm_b_im'] = view(wts['ssm_b_re'], 'ssm_b_re')[0], view(wts['ssm_b_im'], 'ssm_b_im')[0]
    w['ssm_log_dt'] = w['ssm_log_dt'] + gather_token[0, 0]
    wide =['ssm_b_re', 'ssm_b_im', 'ssm_c_re', 'ssm_c_im']

    nat = {n: view(wts[n], n).shape for n in SMALL}
    narrow = [n for n in SMALL if n not in wide]
    wide_started, narrow_started = [], []

    def reduce_wide(gd):
        packed = jnp.concatenate([gd[n].reshape(-1, LANES) for n in wide])
        started_w, token = _all8_start([packed], "small_grads_start")
        wide_started.append(started_w)
        return token

    def put_narrow(gd, loss_row):
        started_n, token = _all8_start([gd[n].reshape(nat[n]) for n in narrow] + [loss_row], "narrow_grads_start")
        narrow_started.append(started_n)
        return token

    dx, grads = _local_step(x[0], loss_target[0], w, get_weights, put_grads, reduce_wide, put_narrow)

    out_g, out_d, out_m, out_v = {}, {}, {}, {}

    def finish(group, after):
        names = GROUPS[group]
        parts, recv = _exchange_wait(sent[group], after, True, f"grads_wait_{group}")
        same = len({p.shape for p in parts}) == 1
        batches = [list(range(len(names)))] if same else [[i] for i in range(len(names))]
        sums = [None] * len(names)
        for b in batches:
            res = _sum_parts([parts[i] for i in b], [recv[i] for i in b], slots, f"grad_sum_{names[b[0]]}")
            for i, sm in zip(b, res):
                sums[i] = sm
        full = _half_forward(sums, f"grad_half_forward_{group}")
        for b in batches:
            res = _adamw([view(wts[names[i]], names[i]) for i in b], [view(given['m_' + names[i]], names[i]) for i in b],
                         [view(given['v_' + names[i]], names[i]) for i in b], [full[i] for i in b],
                         f"adamw_{names[b[0]]}")
            for i, (g, d, nm, nv) in zip(b, res):
                n = names[i]
                out_g[n], out_d[n], out_m[n], out_v[n] = (unview(a, n) for a in (g, d, nm, nv))
        return nv

    done = finish('ffn2', tokens['ffn1'])
    done = finish('mix', done)

    lin = (4 * cx + 2 * cy + lax.axis_index("c")).astype(jnp.int32).reshape(1)
    (own_w,), (land_w,) = _all8_wait(wide_started[0], done, "small_grads_wait")
    wide_sum = _sum8(own_w, land_w, lin, "small_grads_sum")
    rows = wide_sum.shape[0] // len(wide)
    wide_g = [wide_sum[i * rows:(i + 1) * rows].reshape((1,) + nat[n]) for i, n in enumerate(wide)]
    owns_n, lands_n = _all8_wait(narrow_started[0], done, "narrow_grads_wait")
    loss_shares = jnp.where(jnp.arange(N_DEV) == lin[0], owns_n[-1][0, 0], lands_n[-1][:, 0, 0])
    loss = jnp.sum(loss_shares)
    for group, gs, owns, split, tag in ((narrow, lands_n[:-1], owns_n[:-1], None, "adamw_small"),
                                        (wide, wide_g, None, (2, 4), "adamw_ssm_bc")):
        res = _adamw_small([view(wts[n], n) for n in group], [view(given['m_' + n], n) for n in group],
                           [view(given['v_' + n], n) for n in group], gs, split, tag, owns=owns,
                           lin=lin if owns is not None else None)
        for dst, vals in zip((out_g, out_d, out_m, out_v), res):
            for n, a in zip(group, vals):
                dst[n] = unview(a, n)

    finish('ffn1', out_v['norm_ffn1'][:, :1] + out_v['ssm_c_re'].reshape(1, -1)[:, :1] + done[:1, :1] + loss)

    return (loss, dx[None], *[out_g[n] for n in WEIGHTS], *[out_d[n] for n in WEIGHTS],
            *[out_m[n] for n in WEIGHTS], *[out_v[n] for n in WEIGHTS])
```
